```python
import math
import jax
import jax.numpy as jnp
from jax import lax
import numpy as np

D_MODEL = 2048
BATCH = 8
SEQ = 2048
DEPTH = 2

CHUNK = 64
N_MEM = 256
N_A_LAYERS = DEPTH // 2
N_B_LAYERS = DEPTH - N_A_LAYERS
MIX_WIDTH = D_MODEL
MAIN_WIDTH = 3 * MIX_WIDTH // 4
MEM_WIDTH = MIX_WIDTH - MAIN_WIDTH
HEAD_DIM = 128
SSM_GROUP = 16
SSM_GROUPS = MAIN_WIDTH // SSM_GROUP
SSM_STATE = 64
FOX_HEADS = MAIN_WIDTH // HEAD_DIM
MEM_HEADS = 4
MEM_HEAD_DIM = MEM_WIDTH // MEM_HEADS
Q_BLOCK = 128
IN_WIDTH = 2 * MAIN_WIDTH + 2 * MEM_WIDTH
EPS = 1e-6
DT_MIN = 1e-3
DT_MAX = 1e-1

kernel_name = "yoco_s5_fox_memory_hybrid"


def rmsnorm(x, g):
    xf = x.astype(jnp.float32)
    y = xf * lax.rsqrt(jnp.mean(xf * xf, axis=-1, keepdims=True) + EPS) * g.astype(jnp.float32)
    return y.astype(x.dtype)


def _scan_binop(e1, e2):
    a1r, a1i, b1r, b1i = e1
    a2r, a2i, b2r, b2i = e2
    ar = a2r * a1r - a2i * a1i
    ai = a2r * a1i + a2i * a1r
    br = a2r * b1r - a2i * b1i + b2r
    bi = a2r * b1i + a2i * b1r + b2i
    return (ar, ai, br, bi)


def s5_ssm(u, lam_re, lam_im, log_step, b_re, b_im, c_re, c_im, d_skip):
    bsz, seqlen, _ = u.shape
    uf = u.astype(jnp.float32)
    ug = uf.reshape(bsz, seqlen, SSM_GROUPS, SSM_GROUP)
    lr = lam_re.astype(jnp.float32)
    li = lam_im.astype(jnp.float32)
    dt = jnp.exp(log_step.astype(jnp.float32))[:, None]
    mag = jnp.exp(lr * dt)
    ar = mag * jnp.cos(li * dt)
    ai = mag * jnp.sin(li * dt)
    den = lr * lr + li * li
    cr = ((ar - 1.0) * lr + ai * li) / den
    ci = (ai * lr - (ar - 1.0) * li) / den
    br = b_re.astype(jnp.float32)
    bi = b_im.astype(jnp.float32)
    bbar_re = cr[..., None] * br - ci[..., None] * bi
    bbar_im = cr[..., None] * bi + ci[..., None] * br
    bu_re = jnp.einsum('blgh,gph->blgp', ug, bbar_re)
    bu_im = jnp.einsum('blgh,gph->blgp', ug, bbar_im)
    a_re = jnp.broadcast_to(ar[None, None], (1, seqlen, SSM_GROUPS, SSM_STATE))
    a_im = jnp.broadcast_to(ai[None, None], (1, seqlen, SSM_GROUPS, SSM_STATE))
    _, _, x_re, x_im = lax.associative_scan(_scan_binop, (a_re, a_im, bu_re, bu_im), axis=1)
    y = (jnp.einsum('blgp,ghp->blgh', x_re, c_re.astype(jnp.float32))
         - jnp.einsum('blgp,ghp->blgh', x_im, c_im.astype(jnp.float32)))
    y = y.reshape(bsz, seqlen, MAIN_WIDTH) + d_skip.astype(jnp.float32) * uf
    return y.astype(u.dtype)


def memory_attention(qm, mem, mem_g, w_mem_kv):
    bsz, seqlen, _ = qm.shape
    memn = rmsnorm(mem, mem_g)
    kv = memn @ w_mem_kv
    km, vm = jnp.split(kv, 2, axis=-1)
    km = km.reshape(bsz, -1, MEM_HEADS, MEM_HEAD_DIM).astype(jnp.float32)
    vm = vm.reshape(bsz, -1, MEM_HEADS, MEM_HEAD_DIM).astype(jnp.float32)
    q = qm.reshape(bsz, seqlen, MEM_HEADS, MEM_HEAD_DIM).astype(jnp.float32) * (MEM_HEAD_DIM ** -0.5)
    s = jnp.einsum('blhd,bmhd->bhlm', q, km)
    p = jax.nn.softmax(s, axis=-1)
    o = jnp.einsum('bhlm,bmhd->blhd', p, vm)
    return o.reshape(bsz, seqlen, MEM_WIDTH).astype(qm.dtype)


def forgetting_attention(q, k, v, fcum):
    _, seqlen, _, dh = q.shape
    qf = q.astype(jnp.float32) * (dh ** -0.5)
    kf = k.astype(jnp.float32)
    vf = v.astype(jnp.float32)
    outs = []
    for blk in range(seqlen // Q_BLOCK):
        q0 = blk * Q_BLOCK
        q1 = q0 + Q_BLOCK
        s = jnp.einsum('bqhd,bkhd->bhqk', qf[:, q0:q1], kf[:, :q1])
        s = s + fcum[:, :, q0:q1, None] - fcum[:, :, None, :q1]
        causal = jnp.arange(q0, q1)[:, None] >= jnp.arange(q1)[None, :]
        s = jnp.where(causal, s, -jnp.inf)
        p = jax.nn.softmax(s, axis=-1)
        outs.append(jnp.einsum('bhqk,bkhd->bqhd', p, vf[:, :q1]))
    return jnp.concatenate(outs, axis=1).astype(q.dtype)


def _fwd_setup_inputs(seed: int = 0) -> dict:
    key = jax.random.key(seed)
    ks = jax.random.split(key, 32)
    f32 = jnp.float32
    D = D_MODEL
    nrm = lambda k, shape, scale: jax.random.normal(k, shape, f32) * scale
    x = jax.random.normal(ks[0], (BATCH, SEQ, D), f32)
    mem = jax.random.normal(ks[1], (BATCH, N_MEM, D), f32)
    pre_norm_g = 1.0 + nrm(ks[2], (DEPTH, D), 0.02)
    post_norm_g = 1.0 + nrm(ks[3], (DEPTH, D), 0.02)
    w_in_a = nrm(ks[4], (N_A_LAYERS, D, IN_WIDTH), D ** -0.5)
    lam_re = -0.5 + nrm(ks[5], (N_A_LAYERS, SSM_GROUPS, SSM_STATE), 0.01)
    lam_im = (math.pi * jnp.arange(SSM_STATE, dtype=f32))[None, None, :] + nrm(ks[6], (N_A_LAYERS, SSM_GROUPS, SSM_STATE), 0.01)
    log_step = jax.random.uniform(ks[7], (N_A_LAYERS, SSM_GROUPS), f32, math.log(DT_MIN), math.log(DT_MAX))
    b_re = nrm(ks[8], (N_A_LAYERS, SSM_GROUPS, SSM_STATE, SSM_GROUP), (2.0 * SSM_GROUP) ** -0.5)
    b_im = nrm(ks[9], (N_A_LAYERS, SSM_GROUPS, SSM_STATE, SSM_GROUP), (2.0 * SSM_GROUP) ** -0.5)
    c_re = nrm(ks[10], (N_A_LAYERS, SSM_GROUPS, SSM_GROUP, SSM_STATE), (2.0 * SSM_STATE) ** -0.5)
    c_im = nrm(ks[11], (N_A_LAYERS, SSM_GROUPS, SSM_GROUP, SSM_STATE), (2.0 * SSM_STATE) ** -0.5)
    d_skip = nrm(ks[12], (N_A_LAYERS, MAIN_WIDTH), 1.0)
    w_glu = nrm(ks[13], (N_A_LAYERS, MAIN_WIDTH, MAIN_WIDTH), MAIN_WIDTH ** -0.5)
    b_glu = nrm(ks[14], (N_A_LAYERS, MAIN_WIDTH), 0.01)
    kv_norm_g = 1.0 + nrm(ks[15], (D,), 0.02)
    w_kv = nrm(ks[16], (D, 2 * MAIN_WIDTH), D ** -0.5)
    w_fgate = nrm(ks[17], (D, FOX_HEADS), D ** -0.5)
    b_fgate = nrm(ks[18], (FOX_HEADS,), 0.1)
    w_in_b = nrm(ks[19], (N_B_LAYERS, D, IN_WIDTH), D ** -0.5)
    mem_norm_g = 1.0 + nrm(ks[20], (DEPTH, D), 0.02)
    w_mem_kv = nrm(ks[21], (DEPTH, D, 2 * MEM_WIDTH), D ** -0.5)
    w_out = nrm(ks[22], (DEPTH, MIX_WIDTH, D), MIX_WIDTH ** -0.5)
    return {"x": x, "mem": mem, "pre_norm_g": pre_norm_g, "post_norm_g": post_norm_g,
            "w_in_a": w_in_a, "lam_re": lam_re, "lam_im": lam_im, "log_step": log_step,
            "b_re": b_re, "b_im": b_im, "c_re": c_re, "c_im": c_im, "d_skip": d_skip,
            "w_glu": w_glu, "b_glu": b_glu, "kv_norm_g": kv_norm_g, "w_kv": w_kv,
            "w_fgate": w_fgate, "b_fgate": b_fgate, "w_in_b": w_in_b,
            "mem_norm_g": mem_norm_g, "w_mem_kv": w_mem_kv, "w_out": w_out}


def _fwd_reference(x, mem, pre_norm_g, post_norm_g, w_in_a, lam_re, lam_im, log_step,
              b_re, b_im, c_re, c_im, d_skip, w_glu, b_glu, kv_norm_g, w_kv,
              w_fgate, b_fgate, w_in_b, mem_norm_g, w_mem_kv, w_out):
    bsz, seqlen, _ = x.shape
    h = x
    k_sh = v_sh = fcum = None
    split_pts = [MAIN_WIDTH, 2 * MAIN_WIDTH, 2 * MAIN_WIDTH + MEM_WIDTH]
    for i in range(DEPTH):
        hn = rmsnorm(h, pre_norm_g[i])
        if i < N_A_LAYERS:
            la = i
            proj = hn @ w_in_a[la]
            u, z, qm, zm = jnp.split(proj, split_pts, axis=-1)
            y = s5_ssm(u, lam_re[la], lam_im[la], log_step[la], b_re[la], b_im[la],
                       c_re[la], c_im[la], d_skip[la])
            yg = jax.nn.gelu(y)
            y = yg * jax.nn.sigmoid(yg @ w_glu[la] + b_glu[la])
            main = y * jax.nn.silu(z)
        else:
            lb = i - N_A_LAYERS
            proj = hn @ w_in_b[lb]
            q, z, qm, zm = jnp.split(proj, split_pts, axis=-1)
            q = q.reshape(bsz, seqlen, FOX_HEADS, HEAD_DIM)
            att = forgetting_attention(q, k_sh, v_sh, fcum).reshape(bsz, seqlen, MAIN_WIDTH)
            main = att * jax.nn.silu(z)
        memo = memory_attention(qm, mem, mem_norm_g[i], w_mem_kv[i]) * jax.nn.silu(zm)
        o = jnp.concatenate([main, memo], axis=-1) @ w_out[i]
        h = h + rmsnorm(o, post_norm_g[i])
        if i == N_A_LAYERS - 1:
            kv_in = rmsnorm(h, kv_norm_g)
            kv = kv_in @ w_kv
            k_sh, v_sh = jnp.split(kv, 2, axis=-1)
            k_sh = k_sh.reshape(bsz, seqlen, FOX_HEADS, HEAD_DIM)
            v_sh = v_sh.reshape(bsz, seqlen, FOX_HEADS, HEAD_DIM)
            logf = jax.nn.log_sigmoid((kv_in @ w_fgate).astype(jnp.float32) + b_fgate.astype(jnp.float32))
            fcum = jnp.transpose(jnp.cumsum(logf, axis=1), (0, 2, 1))
    return h


import jax as _jax
import jax.numpy as _jnp

TWIN_FORMAT = 'train_step'
FWD_PARAMS = ['x', 'mem', 'pre_norm_g', 'post_norm_g', 'w_in_a', 'lam_re', 'lam_im', 'log_step', 'b_re', 'b_im', 'c_re', 'c_im', 'd_skip', 'w_glu', 'b_glu', 'kv_norm_g', 'w_kv', 'w_fgate', 'b_fgate', 'w_in_b', 'mem_norm_g', 'w_mem_kv', 'w_out']
TWIN_WEIGHTS = ['pre_norm_g', 'post_norm_g', 'w_in_a', 'lam_re', 'lam_im', 'log_step', 'b_re', 'b_im', 'c_re', 'c_im', 'd_skip', 'w_glu', 'b_glu', 'kv_norm_g', 'w_kv', 'w_fgate', 'b_fgate', 'w_in_b', 'mem_norm_g', 'w_mem_kv', 'w_out']
TWIN_DIFF_INPUT = 'x'
TWIN_INPUTS = ['x', 'mem', 'pre_norm_g', 'post_norm_g', 'w_in_a', 'lam_re', 'lam_im', 'log_step', 'b_re', 'b_im', 'c_re', 'c_im', 'd_skip', 'w_glu', 'b_glu', 'kv_norm_g', 'w_kv', 'w_fgate', 'b_fgate', 'w_in_b', 'mem_norm_g', 'w_mem_kv', 'w_out', 'loss_target', 'm_pre_norm_g', 'm_post_norm_g', 'm_w_in_a', 'm_lam_re', 'm_lam_im', 'm_log_step', 'm_b_re', 'm_b_im', 'm_c_re', 'm_c_im', 'm_d_skip', 'm_w_glu', 'm_b_glu', 'm_kv_norm_g', 'm_w_kv', 'm_w_fgate', 'm_b_fgate', 'm_w_in_b', 'm_mem_norm_g', 'm_w_mem_kv', 'm_w_out', 'v_pre_norm_g', 'v_post_norm_g', 'v_w_in_a', 'v_lam_re', 'v_lam_im', 'v_log_step', 'v_b_re', 'v_b_im', 'v_c_re', 'v_c_im', 'v_d_skip', 'v_w_glu', 'v_b_glu', 'v_kv_norm_g', 'v_w_kv', 'v_w_fgate', 'v_b_fgate', 'v_w_in_b', 'v_mem_norm_g', 'v_w_mem_kv', 'v_w_out']
TWIN_OUTPUTS = ['loss', 'grad_x', 'grad_pre_norm_g', 'grad_post_norm_g', 'grad_w_in_a', 'grad_lam_re', 'grad_lam_im', 'grad_log_step', 'grad_b_re', 'grad_b_im', 'grad_c_re', 'grad_c_im', 'grad_d_skip', 'grad_w_glu', 'grad_b_glu', 'grad_kv_norm_g', 'grad_w_kv', 'grad_w_fgate', 'grad_b_fgate', 'grad_w_in_b', 'grad_mem_norm_g', 'grad_w_mem_kv', 'grad_w_out', 'delta_pre_norm_g', 'delta_post_norm_g', 'delta_w_in_a', 'delta_lam_re', 'delta_lam_im', 'delta_log_step', 'delta_b_re', 'delta_b_im', 'delta_c_re', 'delta_c_im', 'delta_d_skip', 'delta_w_glu', 'delta_b_glu', 'delta_kv_norm_g', 'delta_w_kv', 'delta_w_fgate', 'delta_b_fgate', 'delta_w_in_b', 'delta_mem_norm_g', 'delta_w_mem_kv', 'delta_w_out', 'new_m_pre_norm_g', 'new_m_post_norm_g', 'new_m_w_in_a', 'new_m_lam_re', 'new_m_lam_im', 'new_m_log_step', 'new_m_b_re', 'new_m_b_im', 'new_m_c_re', 'new_m_c_im', 'new_m_d_skip', 'new_m_w_glu', 'new_m_b_glu', 'new_m_kv_norm_g', 'new_m_w_kv', 'new_m_w_fgate', 'new_m_b_fgate', 'new_m_w_in_b', 'new_m_mem_norm_g', 'new_m_w_mem_kv', 'new_m_w_out', 'new_v_pre_norm_g', 'new_v_post_norm_g', 'new_v_w_in_a', 'new_v_lam_re', 'new_v_lam_im', 'new_v_log_step', 'new_v_b_re', 'new_v_b_im', 'new_v_c_re', 'new_v_c_im', 'new_v_d_skip', 'new_v_w_glu', 'new_v_b_glu', 'new_v_kv_norm_g', 'new_v_w_kv', 'new_v_w_fgate', 'new_v_b_fgate', 'new_v_w_in_b', 'new_v_mem_norm_g', 'new_v_w_mem_kv', 'new_v_w_out']
TWIN_LEAF_KINDS = {'loss': 'loss', 'grad_x': 'grad_x', 'grad_pre_norm_g': 'grad_w', 'grad_post_norm_g': 'grad_w', 'grad_w_in_a': 'grad_w', 'grad_lam_re': 'grad_w', 'grad_lam_im': 'grad_w', 'grad_log_step': 'grad_w', 'grad_b_re': 'grad_w', 'grad_b_im': 'grad_w', 'grad_c_re': 'grad_w', 'grad_c_im': 'grad_w', 'grad_d_skip': 'grad_w', 'grad_w_glu': 'grad_w', 'grad_b_glu': 'grad_w', 'grad_kv_norm_g': 'grad_w', 'grad_w_kv': 'grad_w', 'grad_w_fgate': 'grad_w', 'grad_b_fgate': 'grad_w', 'grad_w_in_b': 'grad_w', 'grad_mem_norm_g': 'grad_w', 'grad_w_mem_kv': 'grad_w', 'grad_w_out': 'grad_w', 'delta_pre_norm_g': 'delta_w', 'delta_post_norm_g': 'delta_w', 'delta_w_in_a': 'delta_w', 'delta_lam_re': 'delta_w', 'delta_lam_im': 'delta_w', 'delta_log_step': 'delta_w', 'delta_b_re': 'delta_w', 'delta_b_im': 'delta_w', 'delta_c_re': 'delta_w', 'delta_c_im': 'delta_w', 'delta_d_skip': 'delta_w', 'delta_w_glu': 'delta_w', 'delta_b_glu': 'delta_w', 'delta_kv_norm_g': 'delta_w', 'delta_w_kv': 'delta_w', 'delta_w_fgate': 'delta_w', 'delta_b_fgate': 'delta_w', 'delta_w_in_b': 'delta_w', 'delta_mem_norm_g': 'delta_w', 'delta_w_mem_kv': 'delta_w', 'delta_w_out': 'delta_w', 'new_m_pre_norm_g': 'new_m', 'new_m_post_norm_g': 'new_m', 'new_m_w_in_a': 'new_m', 'new_m_lam_re': 'new_m', 'new_m_lam_im': 'new_m', 'new_m_log_step': 'new_m', 'new_m_b_re': 'new_m', 'new_m_b_im': 'new_m', 'new_m_c_re': 'new_m', 'new_m_c_im': 'new_m', 'new_m_d_skip': 'new_m', 'new_m_w_glu': 'new_m', 'new_m_b_glu': 'new_m', 'new_m_kv_norm_g': 'new_m', 'new_m_w_kv': 'new_m', 'new_m_w_fgate': 'new_m', 'new_m_b_fgate': 'new_m', 'new_m_w_in_b': 'new_m', 'new_m_mem_norm_g': 'new_m', 'new_m_w_mem_kv': 'new_m', 'new_m_w_out': 'new_m', 'new_v_pre_norm_g': 'new_v', 'new_v_post_norm_g': 'new_v', 'new_v_w_in_a': 'new_v', 'new_v_lam_re': 'new_v', 'new_v_lam_im': 'new_v', 'new_v_log_step': 'new_v', 'new_v_b_re': 'new_v', 'new_v_b_im': 'new_v', 'new_v_c_re': 'new_v', 'new_v_c_im': 'new_v', 'new_v_d_skip': 'new_v', 'new_v_w_glu': 'new_v', 'new_v_b_glu': 'new_v', 'new_v_kv_norm_g': 'new_v', 'new_v_w_kv': 'new_v', 'new_v_w_fgate': 'new_v', 'new_v_b_fgate': 'new_v', 'new_v_w_in_b': 'new_v', 'new_v_mem_norm_g': 'new_v', 'new_v_w_mem_kv': 'new_v', 'new_v_w_out': 'new_v'}


def _forward(args):
    return _fwd_reference(*[args[k] for k in FWD_PARAMS])


def _output_shape():
    out = _jax.eval_shape(lambda: _forward(_fwd_setup_inputs(0)))
    return out.shape, out.dtype

N_MICROBATCH = 1
ADAM_LR = 0.001
ADAM_B1 = 0.9
ADAM_B2 = 0.999
ADAM_EPS = 1e-08
ADAM_WD = 0.01
ADAM_STEP = 10
PER_EXAMPLE_BATCH_AXIS = {'x': 0, 'mem': 0, 'loss_target': 0}
SHARED_INPUTS = []
_WEIGHT_DTYPES = {'pre_norm_g': _jnp.float32, 'post_norm_g': _jnp.float32, 'w_in_a': _jnp.float32, 'lam_re': _jnp.float32, 'lam_im': _jnp.float32, 'log_step': _jnp.float32, 'b_re': _jnp.float32, 'b_im': _jnp.float32, 'c_re': _jnp.float32, 'c_im': _jnp.float32, 'd_skip': _jnp.float32, 'w_glu': _jnp.float32, 'b_glu': _jnp.float32, 'kv_norm_g': _jnp.float32, 'w_kv': _jnp.float32, 'w_fgate': _jnp.float32, 'b_fgate': _jnp.float32, 'w_in_b': _jnp.float32, 'mem_norm_g': _jnp.float32, 'w_mem_kv': _jnp.float32, 'w_out': _jnp.float32}
MOMENT_SCALE = {'pre_norm_g': 2.023468e-01, 'post_norm_g': 8.002734e+00, 'w_in_a': 1.818031e-01, 'lam_re': 9.923560e-03, 'lam_im': 1.010065e-02, 'log_step': 6.980315e+00, 'b_re': 6.666529e-03, 'b_im': 6.659851e-03, 'c_re': 1.343359e-02, 'c_im': 1.343243e-02, 'd_skip': 2.511086e-01, 'w_glu': 5.754270e-02, 'b_glu': 1.036080e-01, 'kv_norm_g': 1.596912e-01, 'w_kv': 1.250222e-01, 'w_fgate': 3.631864e-01, 'b_fgate': 5.027634e-01, 'w_in_b': 9.530191e-02, 'mem_norm_g': 3.255484e-02, 'w_mem_kv': 4.649893e-02, 'w_out': 1.787929e-01}


def _to_microbatches(a, axis):
    t = _jnp.moveaxis(a, axis, 0)
    t = t.reshape((N_MICROBATCH, t.shape[0] // N_MICROBATCH) + t.shape[1:])
    return _jnp.moveaxis(t, 1, axis + 1)


def setup_inputs(seed: int = 0) -> dict:
    inp = _fwd_setup_inputs(seed)
    key = _jax.random.fold_in(_jax.random.key(seed), 7919)
    shape, _ = _output_shape()
    out = dict(inp)
    out["loss_target"] = _jax.random.normal(_jax.random.fold_in(key, 0), shape, _jnp.float32)
    for i, name in enumerate(TWIN_WEIGHTS):
        w = inp[name].astype(_jnp.float32)
        if MOMENT_SCALE is None:
            s = _jnp.sqrt(_jnp.mean(_jnp.square(w)) + 1e-30)
        else:
            s = MOMENT_SCALE[name]
        km, kv = _jax.random.split(_jax.random.fold_in(key, i + 1))
        out[name] = w
        out["m_" + name] = s * _jax.random.normal(km, w.shape, _jnp.float32)
        out["v_" + name] = (s * s) * _jax.random.uniform(kv, w.shape, _jnp.float32, 0.5, 1.5)
    if N_MICROBATCH > 1:
        for name, axis in PER_EXAMPLE_BATCH_AXIS.items():
            out[name] = _to_microbatches(out[name], axis)
    return {'x': out['x'], 'mem': out['mem'], 'pre_norm_g': out['pre_norm_g'], 'post_norm_g': out['post_norm_g'], 'w_in_a': out['w_in_a'], 'lam_re': out['lam_re'], 'lam_im': out['lam_im'], 'log_step': out['log_step'], 'b_re': out['b_re'], 'b_im': out['b_im'], 'c_re': out['c_re'], 'c_im': out['c_im'], 'd_skip': out['d_skip'], 'w_glu': out['w_glu'], 'b_glu': out['b_glu'], 'kv_norm_g': out['kv_norm_g'], 'w_kv': out['w_kv'], 'w_fgate': out['w_fgate'], 'b_fgate': out['b_fgate'], 'w_in_b': out['w_in_b'], 'mem_norm_g': out['mem_norm_g'], 'w_mem_kv': out['w_mem_kv'], 'w_out': out['w_out'], 'loss_target': out['loss_target'], 'm_pre_norm_g': out['m_pre_norm_g'], 'm_post_norm_g': out['m_post_norm_g'], 'm_w_in_a': out['m_w_in_a'], 'm_lam_re': out['m_lam_re'], 'm_lam_im': out['m_lam_im'], 'm_log_step': out['m_log_step'], 'm_b_re': out['m_b_re'], 'm_b_im': out['m_b_im'], 'm_c_re': out['m_c_re'], 'm_c_im': out['m_c_im'], 'm_d_skip': out['m_d_skip'], 'm_w_glu': out['m_w_glu'], 'm_b_glu': out['m_b_glu'], 'm_kv_norm_g': out['m_kv_norm_g'], 'm_w_kv': out['m_w_kv'], 'm_w_fgate': out['m_w_fgate'], 'm_b_fgate': out['m_b_fgate'], 'm_w_in_b': out['m_w_in_b'], 'm_mem_norm_g': out['m_mem_norm_g'], 'm_w_mem_kv': out['m_w_mem_kv'], 'm_w_out': out['m_w_out'], 'v_pre_norm_g': out['v_pre_norm_g'], 'v_post_norm_g': out['v_post_norm_g'], 'v_w_in_a': out['v_w_in_a'], 'v_lam_re': out['v_lam_re'], 'v_lam_im': out['v_lam_im'], 'v_log_step': out['v_log_step'], 'v_b_re': out['v_b_re'], 'v_b_im': out['v_b_im'], 'v_c_re': out['v_c_re'], 'v_c_im': out['v_c_im'], 'v_d_skip': out['v_d_skip'], 'v_w_glu': out['v_w_glu'], 'v_b_glu': out['v_b_glu'], 'v_kv_norm_g': out['v_kv_norm_g'], 'v_w_kv': out['v_w_kv'], 'v_w_fgate': out['v_w_fgate'], 'v_b_fgate': out['v_b_fgate'], 'v_w_in_b': out['v_w_in_b'], 'v_mem_norm_g': out['v_mem_norm_g'], 'v_w_mem_kv': out['v_w_mem_kv'], 'v_w_out': out['v_w_out']}


def _loss(weights, diff, rest, loss_target):
    with _jax.named_scope("forward"):
        args = {**rest, TWIN_DIFF_INPUT: diff, **{k: w.astype(_WEIGHT_DTYPES[k]) for k, w in weights.items()}}
        y = _forward(args)
    with _jax.named_scope("loss_head"):
        err = _jnp.square(y.astype(_jnp.float32) - loss_target)
        return 0.5 * _jnp.sum(_jnp.mean(err, axis=-1)) if err.ndim else 0.5 * err


def _adamw(w, g, m, v):
    m = ADAM_B1 * m + (1.0 - ADAM_B1) * g
    v = ADAM_B2 * v + (1.0 - ADAM_B2) * _jnp.square(g)
    m_hat = m / (1.0 - ADAM_B1 ** ADAM_STEP)
    v_hat = v / (1.0 - ADAM_B2 ** ADAM_STEP)
    delta = -ADAM_LR * (m_hat / (_jnp.sqrt(v_hat) + ADAM_EPS) + ADAM_WD * w)
    return delta, m, v


def reference(x, mem, pre_norm_g, post_norm_g, w_in_a, lam_re, lam_im, log_step, b_re, b_im, c_re, c_im, d_skip, w_glu, b_glu, kv_norm_g, w_kv, w_fgate, b_fgate, w_in_b, mem_norm_g, w_mem_kv, w_out, loss_target, m_pre_norm_g, m_post_norm_g, m_w_in_a, m_lam_re, m_lam_im, m_log_step, m_b_re, m_b_im, m_c_re, m_c_im, m_d_skip, m_w_glu, m_b_glu, m_kv_norm_g, m_w_kv, m_w_fgate, m_b_fgate, m_w_in_b, m_mem_norm_g, m_w_mem_kv, m_w_out, v_pre_norm_g, v_post_norm_g, v_w_in_a, v_lam_re, v_lam_im, v_log_step, v_b_re, v_b_im, v_c_re, v_c_im, v_d_skip, v_w_glu, v_b_glu, v_kv_norm_g, v_w_kv, v_w_fgate, v_b_fgate, v_w_in_b, v_mem_norm_g, v_w_mem_kv, v_w_out):
    given = dict(x=x, mem=mem, pre_norm_g=pre_norm_g, post_norm_g=post_norm_g, w_in_a=w_in_a, lam_re=lam_re, lam_im=lam_im, log_step=log_step, b_re=b_re, b_im=b_im, c_re=c_re, c_im=c_im, d_skip=d_skip, w_glu=w_glu, b_glu=b_glu, kv_norm_g=kv_norm_g, w_kv=w_kv, w_fgate=w_fgate, b_fgate=b_fgate, w_in_b=w_in_b, mem_norm_g=mem_norm_g, w_mem_kv=w_mem_kv, w_out=w_out, loss_target=loss_target, m_pre_norm_g=m_pre_norm_g, m_post_norm_g=m_post_norm_g, m_w_in_a=m_w_in_a, m_lam_re=m_lam_re, m_lam_im=m_lam_im, m_log_step=m_log_step, m_b_re=m_b_re, m_b_im=m_b_im, m_c_re=m_c_re, m_c_im=m_c_im, m_d_skip=m_d_skip, m_w_glu=m_w_glu, m_b_glu=m_b_glu, m_kv_norm_g=m_kv_norm_g, m_w_kv=m_w_kv, m_w_fgate=m_w_fgate, m_b_fgate=m_b_fgate, m_w_in_b=m_w_in_b, m_mem_norm_g=m_mem_norm_g, m_w_mem_kv=m_w_mem_kv, m_w_out=m_w_out, v_pre_norm_g=v_pre_norm_g, v_post_norm_g=v_post_norm_g, v_w_in_a=v_w_in_a, v_lam_re=v_lam_re, v_lam_im=v_lam_im, v_log_step=v_log_step, v_b_re=v_b_re, v_b_im=v_b_im, v_c_re=v_c_re, v_c_im=v_c_im, v_d_skip=v_d_skip, v_w_glu=v_w_glu, v_b_glu=v_b_glu, v_kv_norm_g=v_kv_norm_g, v_w_kv=v_w_kv, v_w_fgate=v_w_fgate, v_b_fgate=v_b_fgate, v_w_in_b=v_w_in_b, v_mem_norm_g=v_mem_norm_g, v_w_mem_kv=v_w_mem_kv, v_w_out=v_w_out)
    weights = {n: given[n] for n in TWIN_WEIGHTS}
    shared = {n: given[n] for n in SHARED_INPUTS}
    per_example = {n: given[n] for n in ['x', 'mem']}
    grad_fn = _jax.value_and_grad(_loss, argnums=(0, 1))

    def one_microbatch(ex, loss_target):
        ex = dict(ex)
        diff = ex.pop(TWIN_DIFF_INPUT)
        return grad_fn(weights, diff, {**shared, **ex}, loss_target)

    if N_MICROBATCH == 1:
        loss, (grad_w, grad_x) = one_microbatch(per_example, given["loss_target"])
    else:
        def body(carry, xs):
            loss_sum, grad_sum = carry
            l_k, (gw_k, gx_k) = one_microbatch(xs[0], xs[1])
            with _jax.named_scope("update"):
                return (loss_sum + l_k, _jax.tree.map(_jnp.add, grad_sum, gw_k)), gx_k

        init = (_jnp.zeros((), _jnp.float32), _jax.tree.map(_jnp.zeros_like, weights))
        (loss, grad_w), grad_x = _jax.lax.scan(body, init, (per_example, given["loss_target"]))
    with _jax.named_scope("update"):
        delta_w, new_m, new_v = {}, {}, {}
        for n in TWIN_WEIGHTS:
            delta_w[n], new_m[n], new_v[n] = _adamw(weights[n], grad_w[n], given["m_" + n], given["v_" + n])
    return (loss, grad_x, *[grad_w[n] for n in TWIN_WEIGHTS], *[delta_w[n] for n in TWIN_WEIGHTS],
            *[new_m[n] for n in TWIN_WEIGHTS], *[new_v[n] for n in TWIN_WEIGHTS])
```

```python
import functools
import math

import jax
import jax.numpy as jnp
from jax import lax
from jax.experimental import pallas as pl
from jax.experimental.pallas import tpu as pltpu

F32 = jnp.float32
BF16 = jnp.bfloat16

D_MODEL = 2048
N_MEM = 256
MAIN_WIDTH = 1536
MEM_WIDTH = 512
IN_WIDTH = 2 * MAIN_WIDTH + 2 * MEM_WIDTH
HEAD_DIM = 128
FOX_HEADS = MAIN_WIDTH // HEAD_DIM
MEM_HEADS = MEM_WIDTH // HEAD_DIM
SSM_GROUP = 16
SSM_GROUPS = MAIN_WIDTH // SSM_GROUP
SSM_STATE = 64
GROUPS_PER_BLOCK = 8
SSM_BLOCKS = SSM_GROUPS // GROUPS_PER_BLOCK
STATE_COLS = GROUPS_PER_BLOCK * SSM_STATE
EPS = 1e-6
ADAM_LR = 0.001
ADAM_B1 = 0.9
ADAM_B2 = 0.999
ADAM_EPS = 1e-08
ADAM_WD = 0.01
ADAM_STEP = 10
N_CHIPS = 4
LANES = 128
SUBLANES = 8
VMEM_LIMIT_BYTES = 56 * 1024 * 1024
NEG_BIG = -1e30
MESH_AXES = ("x", "y", "c")


def _params(*sem):
    return pltpu.CompilerParams(dimension_semantics=sem if sem else None,
                                vmem_limit_bytes=VMEM_LIMIT_BYTES)


def _sigmoid(x):
    return 1.0 / (1.0 + jnp.exp(-x))


def _gelu(x):
    c = math.sqrt(2.0 / math.pi)
    return 0.5 * x * (1.0 + jnp.tanh(c * (x + 0.044715 * (x * x * x))))


def _gelu_grad(x):
    c = math.sqrt(2.0 / math.pi)
    t = jnp.tanh(c * (x + 0.044715 * (x * x * x)))
    return 0.5 * (1.0 + t) + 0.5 * x * (1.0 - t * t) * (c * (1.0 + 3.0 * 0.044715 * (x * x)))


def _silu_and_grad(z):
    s = _sigmoid(z)
    return z * s, s * (1.0 + z * (1.0 - s))


def _mm(a, b, *, name, ta=False, tb=False, out_dtype=F32, shards=1, tm=512, tn=512, tk=512):
    if ta:
        K, M = a.shape
    else:
        M, K = a.shape
    if tb:
        N, kb = b.shape
    else:
        kb, N = b.shape
    assert K == kb, (a.shape, b.shape)
    ns = N // shards
    tm, tk = min(tm, M), min(tk, K)
    tn = next(c for c in (tn, 384, 256, LANES) if c <= tn and ns % c == 0)
    assert M % tm == 0 and ns % tn == 0 and K % tk == 0 and N % shards == 0
    nk = K // tk
    dn = (((0 if ta else 1,), (1 if tb else 0,)), ((), ()))

    def body(a_ref, b_ref, o_ref, acc_ref):
        k = pl.program_id(2)

        @pl.when(k == 0)
        def _():
            acc_ref[...] = jnp.zeros_like(acc_ref)

        acc_ref[...] += lax.dot_general(a_ref[...].astype(BF16), b_ref[...].astype(BF16), dn,
                                        preferred_element_type=F32)

        @pl.when(k == nk - 1)
        def _():
            o_ref[...] = acc_ref[...].astype(o_ref.dtype)

    a_spec = (pl.BlockSpec((tk, tm), lambda i, j, k: (k, i)) if ta
              else pl.BlockSpec((tm, tk), lambda i, j, k: (i, k)))
    b_spec = (pl.BlockSpec((tn, tk), lambda i, j, k: (j, k)) if tb
              else pl.BlockSpec((tk, tn), lambda i, j, k: (k, j)))
    if shards == 1:
        out_shape = jax.ShapeDtypeStruct((M, N), out_dtype)
        o_spec = pl.BlockSpec((tm, tn), lambda i, j, k: (i, j))
    else:
        nb = ns // tn
        out_shape = jax.ShapeDtypeStruct((shards, M, ns), out_dtype)
        o_spec = pl.BlockSpec((None, tm, tn), lambda i, j, k: (j // nb, i, j % nb))
    return pl.pallas_call(
        body, name=name, out_shape=out_shape,
        grid=(M // tm, N // tn, nk),
        in_specs=[a_spec, b_spec], out_specs=o_spec,
        scratch_shapes=[pltpu.VMEM((tm, tn), F32)],
        compiler_params=_params("parallel", "parallel", "arbitrary"),
    )(a, b)


def _rmsnorm_fwd(x, g, *, name, res=None, out_dtype=F32, tr=256):
    L, D = x.shape
    tr = min(tr, L)
    has_res = res is not None

    def body(*refs):
        if has_res:
            x_ref, g_ref, r_ref, o_ref = refs
        else:
            x_ref, g_ref, o_ref = refs
        xf = x_ref[...]
        r = lax.rsqrt(jnp.mean(xf * xf, axis=-1, keepdims=True) + EPS)
        y = xf * r * g_ref[...]
        if has_res:
            y = r_ref[...] + y
        o_ref[...] = y.astype(o_ref.dtype)

    row = pl.BlockSpec((tr, D), lambda i: (i, 0))
    vec = pl.BlockSpec((1, D), lambda i: (0, 0))
    ins = [x, g.reshape(1, D)] + ([res] if has_res else [])
    return pl.pallas_call(
        body, name=name, out_shape=jax.ShapeDtypeStruct((L, D), out_dtype),
        grid=(L // tr,), in_specs=[row, vec] + ([row] if has_res else []), out_specs=row,
        compiler_params=_params("parallel"),
    )(*ins)


def _rmsnorm_bwd(x, g, dy, *, name, adds=(), dx_dtype=F32, tr=256):
    L, D = x.shape
    tr = min(tr, L)
    dys = dy if isinstance(dy, tuple) else (dy,)
    n_dy, n_add = len(dys), len(adds)

    def body(*refs):
        x_ref, g_ref = refs[:2]
        dy_refs = refs[2:2 + n_dy]
        add_refs = refs[2 + n_dy:2 + n_dy + n_add]
        dx_ref, dg_ref = refs[2 + n_dy + n_add:]
        xf = x_ref[...]
        dyf = dy_refs[0][...].astype(F32)
        for d_ref in dy_refs[1:]:
            dyf = dyf + d_ref[...].astype(F32)
        r = lax.rsqrt(jnp.mean(xf * xf, axis=-1, keepdims=True) + EPS)
        gy = dyf * g_ref[...]
        c = jnp.mean(xf * gy, axis=-1, keepdims=True) * (r * r * r)
        dx = gy * r - xf * c
        for a_ref in add_refs:
            dx = dx + a_ref[...].astype(F32)
        dx_ref[...] = dx.astype(dx_ref.dtype)

        @pl.when(pl.program_id(0) == 0)
        def _():
            dg_ref[...] = jnp.zeros_like(dg_ref)

        dg_ref[...] += jnp.sum(dyf * xf * r, axis=0, keepdims=True)

    row = pl.BlockSpec((tr, D), lambda i: (i, 0))
    vec = pl.BlockSpec((1, D), lambda i: (0, 0))
    dx, dg = pl.pallas_call(
        body, name=name,
        out_shape=(jax.ShapeDtypeStruct((L, D), dx_dtype), jax.ShapeDtypeStruct((1, D), F32)),
        grid=(L // tr,), in_specs=[row, vec] + [row] * (n_dy + n_add), out_specs=(row, vec),
        compiler_params=_params("arbitrary"),
    )(x, g.reshape(1, D), *dys, *adds)
    return dx, dg.reshape(D)


def _loss_head(h, target, *, tr=256):
    L, D = h.shape
    tr = min(tr, L)

    def body(h_ref, t_ref, dh_ref, loss_ref):
        e = h_ref[...] - t_ref[...]
        dh_ref[...] = e * (1.0 / D)

        @pl.when(pl.program_id(0) == 0)
        def _():
            loss_ref[...] = jnp.zeros_like(loss_ref)

        loss_ref[...] += jnp.sum(e * e, axis=0, keepdims=True) * (0.5 / D)

    row = pl.BlockSpec((tr, D), lambda i: (i, 0))
    vec = pl.BlockSpec((1, D), lambda i: (0, 0))
    dh, lp = pl.pallas_call(
        body, name="loss_head",
        out_shape=(jax.ShapeDtypeStruct((L, D), F32), jax.ShapeDtypeStruct((1, D), F32)),
        grid=(L // tr,), in_specs=[row, row], out_specs=(row, vec),
        compiler_params=_params("arbitrary"),
    )(h, target)
    return dh, lp


def _s5_coeffs(lr, li, ls):
    dt = jnp.exp(ls)
    mag = jnp.exp(lr * dt)
    ar = mag * jnp.cos(li * dt)
    ai = mag * jnp.sin(li * dt)
    den = lr * lr + li * li
    cr = ((ar - 1.0) * lr + ai * li) / den
    ci = (ai * lr - (ar - 1.0) * li) / den
    return dt, ar, ai, den, cr, ci


def _s5_prep(lam_re, lam_im, log_step, b_re_t, b_im_t):
    G, P = lam_re.shape
    H = b_re_t.shape[1]

    def body(lr_ref, li_ref, ls_ref, br_ref, bi_ref, ar_ref, ai_ref, bbr_ref, bbi_ref):
        _, ar, ai, _, cr, ci = _s5_coeffs(lr_ref[...], li_ref[...], ls_ref[...])
        ar_ref[...] = ar
        ai_ref[...] = ai
        br, bi = br_ref[...], bi_ref[...]
        crb, cib = cr[:, None, :], ci[:, None, :]
        bbr_ref[...] = crb * br - cib * bi
        bbi_ref[...] = crb * bi + cib * br

    return pl.pallas_call(
        body, name="s5_prep",
        out_shape=(jax.ShapeDtypeStruct((G, P), F32), jax.ShapeDtypeStruct((G, P), F32),
                   jax.ShapeDtypeStruct((G, H, P), F32), jax.ShapeDtypeStruct((G, H, P), F32)),
        compiler_params=_params(),
    )(lam_re, lam_im, log_step.reshape(G, 1), b_re_t, b_im_t)


def _s5_prep_bwd(lam_re, lam_im, log_step, b_re_t, b_im_t, d_ar, d_ai, d_bbr, d_bbi):
    G, P = lam_re.shape
    H = b_re_t.shape[1]

    def body(lr_ref, li_ref, ls_ref, br_ref, bi_ref, dar_ref, dai_ref, dbbr_ref, dbbi_ref,
             dlr_ref, dli_ref, dls_ref, dbr_ref, dbi_ref):
        lr, li = lr_ref[...], li_ref[...]
        dt, ar, ai, den, cr, ci = _s5_coeffs(lr, li, ls_ref[...])
        br, bi = br_ref[...], bi_ref[...]
        gbr, gbi = dbbr_ref[...], dbbi_ref[...]
        crb, cib = cr[:, None, :], ci[:, None, :]
        dbr_ref[...] = crb * gbr + cib * gbi
        dbi_ref[...] = crb * gbi - cib * gbr
        gcr = jnp.sum(br * gbr + bi * gbi, axis=1)
        gci = jnp.sum(br * gbi - bi * gbr, axis=1)
        ilr, ili = lr / den, -li / den
        gar = dar_ref[...] + (ilr * gcr + ili * gci)
        gai = dai_ref[...] + (ilr * gci - ili * gcr)
        qr, qi = cr * ilr - ci * ili, cr * ili + ci * ilr
        glr = -(qr * gcr + qi * gci)
        gli = -(qr * gci - qi * gcr)
        glr = glr + dt * (ar * gar + ai * gai)
        gli = gli + dt * (ar * gai - ai * gar)
        wr, wi = lr * ar - li * ai, lr * ai + li * ar
        gdt = jnp.sum(wr * gar + wi * gai, axis=1, keepdims=True)
        dlr_ref[...] = glr
        dli_ref[...] = gli
        dls_ref[...] = gdt * dt

    return pl.pallas_call(
        body, name="s5_prep_bwd",
        out_shape=(jax.ShapeDtypeStruct((G, P), F32), jax.ShapeDtypeStruct((G, P), F32),
                   jax.ShapeDtypeStruct((G, 1), F32),
                   jax.ShapeDtypeStruct((G, H, P), F32), jax.ShapeDtypeStruct((G, H, P), F32)),
        compiler_params=_params(),
    )(lam_re, lam_im, log_step.reshape(G, 1), b_re_t, b_im_t, d_ar, d_ai, d_bbr, d_bbi)


def _s5_block_mats(bbr_t, bbi_t, c_re, c_im):
    eye = jnp.eye(GROUPS_PER_BLOCK, dtype=F32)
    bb = jnp.stack([bbr_t, bbi_t], axis=2)
    bb = bb.reshape(SSM_BLOCKS, GROUPS_PER_BLOCK, SSM_GROUP, 2, 1, SSM_STATE)
    bmat = bb * eye[None, :, None, None, :, None]
    bmat = bmat.reshape(SSM_BLOCKS, LANES, 2 * STATE_COLS)
    cc = jnp.stack([c_re, -c_im], axis=0)
    cc = cc.reshape(2, SSM_BLOCKS, GROUPS_PER_BLOCK, SSM_GROUP, SSM_STATE)
    cc = jnp.transpose(cc, (1, 0, 2, 4, 3))
    cmat = cc[:, :, :, :, None, :] * eye[None, None, :, None, :, None]
    cmat = cmat.reshape(SSM_BLOCKS, 2 * STATE_COLS, LANES)
    return bmat.astype(BF16), cmat.astype(BF16)


def _s5_block_diag(dmat):
    eye = jnp.eye(GROUPS_PER_BLOCK, dtype=F32)
    d = dmat.reshape(SSM_BLOCKS, GROUPS_PER_BLOCK, SSM_GROUP, 2, GROUPS_PER_BLOCK, SSM_STATE)
    d = jnp.sum(d * eye[None, :, None, None, :, None], axis=4)
    d = jnp.transpose(d, (3, 0, 1, 2, 4))
    return d.reshape(2, SSM_GROUPS, SSM_GROUP, SSM_STATE)


def _s5_a_rows(ar, ai):
    a = jnp.concatenate([ar.reshape(SSM_BLOCKS, STATE_COLS), ai.reshape(SSM_BLOCKS, STATE_COLS)], axis=1)
    return jnp.broadcast_to(a[:, None, :], (SSM_BLOCKS, SUBLANES, 2 * STATE_COLS))


def _s5_fwd(proj, bmat, cmat, a_rows, d_skip, *, tc=512):
    L = proj.shape[0]
    tc = min(tc, L)
    nt = L // tc
    n8 = tc // SUBLANES
    S = STATE_COLS

    def body(u_ref, b_ref, c_ref, a_ref, d_ref, y_ref, yg_ref, xp_ref, bu_s, xp_s, carry_s):
        @pl.when(pl.program_id(1) == 0)
        def _():
            carry_s[...] = jnp.zeros_like(carry_s)

        u = u_ref[...]
        bu = jnp.dot(u.astype(BF16), b_ref[...], preferred_element_type=F32)
        bu_s[...] = bu.reshape(n8, SUBLANES, 2 * S)
        ar, ai = a_ref[0:1, :S], a_ref[0:1, S:]

        def step(i, carry):
            cr, ci = carry
            for j in range(SUBLANES):
                xp_s[i, j:j + 1, :S] = cr
                xp_s[i, j:j + 1, S:] = ci
                br = bu_s[i, j:j + 1, :S]
                bi = bu_s[i, j:j + 1, S:]
                cr, ci = ar * cr - ai * ci + br, ar * ci + ai * cr + bi
            return cr, ci

        cr, ci = lax.fori_loop(0, n8, step, (carry_s[0:1, :S], carry_s[0:1, S:]))
        carry_s[0:1, :S] = cr
        carry_s[0:1, S:] = ci
        xp = xp_s[...].reshape(tc, 2 * S)
        xp_ref[...] = xp
        x_re = ar * xp[:, :S] - ai * xp[:, S:] + bu[:, :S]
        x_im = ar * xp[:, S:] + ai * xp[:, :S] + bu[:, S:]
        xs = jnp.concatenate([x_re, x_im], axis=1).astype(BF16)
        y = jnp.dot(xs, c_ref[...], preferred_element_type=F32) + d_ref[...] * u
        y_ref[...] = y
        yg_ref[...] = _gelu(y).astype(BF16)

    return pl.pallas_call(
        body, name="s5_fwd",
        out_shape=(jax.ShapeDtypeStruct((L, MAIN_WIDTH), F32),
                   jax.ShapeDtypeStruct((L, MAIN_WIDTH), BF16),
                   jax.ShapeDtypeStruct((L, SSM_BLOCKS * 2 * S), F32)),
        grid=(SSM_BLOCKS, nt),
        in_specs=[pl.BlockSpec((tc, LANES), lambda b, t: (t, b)),
                  pl.BlockSpec((None, LANES, 2 * S), lambda b, t: (b, 0, 0)),
                  pl.BlockSpec((None, 2 * S, LANES), lambda b, t: (b, 0, 0)),
                  pl.BlockSpec((None, SUBLANES, 2 * S), lambda b, t: (b, 0, 0)),
                  pl.BlockSpec((1, LANES), lambda b, t: (0, b))],
        out_specs=(pl.BlockSpec((tc, LANES), lambda b, t: (t, b)),
                   pl.BlockSpec((tc, LANES), lambda b, t: (t, b)),
                   pl.BlockSpec((tc, 2 * S), lambda b, t: (t, b))),
        scratch_shapes=[pltpu.VMEM((n8, SUBLANES, 2 * S), F32),
                        pltpu.VMEM((n8, SUBLANES, 2 * S), F32),
                        pltpu.VMEM((SUBLANES, 2 * S), F32)],
        compiler_params=_params("parallel", "arbitrary"),
    )(proj, bmat, cmat, a_rows, d_skip.reshape(1, MAIN_WIDTH))


def _s5_bwd(proj, dy, xp, bmat, cmat, a_rows, d_skip, *, tc=512):
    L = proj.shape[0]
    tc = min(tc, L)
    nt = L // tc
    n8 = tc // SUBLANES
    S = STATE_COLS
    nn = (((1,), (1,)), ((), ()))
    tn = (((0,), (0,)), ((), ()))

    def body(u_ref, dy_ref, xp_ref, b_ref, c_ref, a_ref, d_ref,
             du_ref, db_ref, dc_ref, da_ref, dd_ref, dl_s, carry_s):
        @pl.when(pl.program_id(1) == 0)
        def _():
            carry_s[...] = jnp.zeros_like(carry_s)
            db_ref[...] = jnp.zeros_like(db_ref)
            dc_ref[...] = jnp.zeros_like(dc_ref)
            da_ref[...] = jnp.zeros_like(da_ref)
            dd_ref[...] = jnp.zeros_like(dd_ref)

        u = u_ref[...]
        dy = dy_ref[...]
        xp = xp_ref[...]
        ub = u.astype(BF16)
        dyb = dy.astype(BF16)
        ar, ai = a_ref[0:1, :S], a_ref[0:1, S:]
        bu = jnp.dot(ub, b_ref[...], preferred_element_type=F32)
        x_re = ar * xp[:, :S] - ai * xp[:, S:] + bu[:, :S]
        x_im = ar * xp[:, S:] + ai * xp[:, :S] + bu[:, S:]
        xs = jnp.concatenate([x_re, x_im], axis=1).astype(BF16)
        dc_ref[...] += lax.dot_general(dyb, xs, tn, preferred_element_type=F32)
        dx = lax.dot_general(dyb, c_ref[...], nn, preferred_element_type=F32)
        dl_s[...] = dx.reshape(n8, SUBLANES, 2 * S)

        def step(k, carry):
            cr, ci = carry
            i = n8 - 1 - k
            for j in range(SUBLANES - 1, -1, -1):
                lr = dl_s[i, j:j + 1, :S] + (ar * cr + ai * ci)
                li = dl_s[i, j:j + 1, S:] + (ar * ci - ai * cr)
                dl_s[i, j:j + 1, :S] = lr
                dl_s[i, j:j + 1, S:] = li
                cr, ci = lr, li
            return cr, ci

        cr, ci = lax.fori_loop(0, n8, step, (carry_s[0:1, :S], carry_s[0:1, S:]))
        carry_s[0:1, :S] = cr
        carry_s[0:1, S:] = ci
        lam = dl_s[...].reshape(tc, 2 * S)
        l_re, l_im = lam[:, :S], lam[:, S:]
        da_ref[0:1, :S] += jnp.sum(l_re * xp[:, :S] + l_im * xp[:, S:], axis=0, keepdims=True)
        da_ref[0:1, S:] += jnp.sum(l_im * xp[:, :S] - l_re * xp[:, S:], axis=0, keepdims=True)
        lamb = lam.astype(BF16)
        du_ref[...] = lax.dot_general(lamb, b_ref[...], nn, preferred_element_type=F32) + d_ref[...] * dy
        db_ref[...] += lax.dot_general(ub, lamb, tn, preferred_element_type=F32)
        dd_ref[0:1, :] += jnp.sum(dy * u, axis=0, keepdims=True)

    rev = lambda b, t: (nt - 1 - t, b)
    return pl.pallas_call(
        body, name="s5_bwd",
        out_shape=(jax.ShapeDtypeStruct((L, MAIN_WIDTH), F32),
                   jax.ShapeDtypeStruct((SSM_BLOCKS, LANES, 2 * S), F32),
                   jax.ShapeDtypeStruct((SSM_BLOCKS, LANES, 2 * S), F32),
                   jax.ShapeDtypeStruct((SSM_BLOCKS, SUBLANES, 2 * S), F32),
                   jax.ShapeDtypeStruct((SUBLANES, MAIN_WIDTH), F32)),
        grid=(SSM_BLOCKS, nt),
        in_specs=[pl.BlockSpec((tc, LANES), rev),
                  pl.BlockSpec((tc, LANES), rev),
                  pl.BlockSpec((tc, 2 * S), rev),
                  pl.BlockSpec((None, LANES, 2 * S), lambda b, t: (b, 0, 0)),
                  pl.BlockSpec((None, 2 * S, LANES), lambda b, t: (b, 0, 0)),
                  pl.BlockSpec((None, SUBLANES, 2 * S), lambda b, t: (b, 0, 0)),
                  pl.BlockSpec((1, LANES), lambda b, t: (0, b))],
        out_specs=(pl.BlockSpec((tc, LANES), rev),
                   pl.BlockSpec((None, LANES, 2 * S), lambda b, t: (b, 0, 0)),
                   pl.BlockSpec((None, LANES, 2 * S), lambda b, t: (b, 0, 0)),
                   pl.BlockSpec((None, SUBLANES, 2 * S), lambda b, t: (b, 0, 0)),
                   pl.BlockSpec((SUBLANES, LANES), lambda b, t: (0, b))),
        scratch_shapes=[pltpu.VMEM((n8, SUBLANES, 2 * S), F32),
                        pltpu.VMEM((SUBLANES, 2 * S), F32)],
        compiler_params=_params("parallel", "arbitrary"),
    )(proj, dy, xp, bmat, cmat, a_rows, d_skip.reshape(1, MAIN_WIDTH))


def _row_specs(tr):
    main = pl.BlockSpec((tr, MAIN_WIDTH), lambda i: (i, 0))
    z = pl.BlockSpec((tr, MAIN_WIDTH), lambda i: (i, 1))
    zm = pl.BlockSpec((tr, MEM_WIDTH), lambda i: (i, IN_WIDTH // MEM_WIDTH - 1))
    mem = pl.BlockSpec((tr, MEM_WIDTH), lambda i: (i, 0))
    cat = pl.BlockSpec((tr, D_MODEL), lambda i: (i, 0))
    vec = pl.BlockSpec((1, MAIN_WIDTH), lambda i: (0, 0))
    return main, z, zm, mem, cat, vec


def _gate_a_fwd(y, t, b_glu, proj, o_mem, *, tr=256):
    L = y.shape[0]
    tr = min(tr, L)

    def body(y_ref, t_ref, b_ref, z_ref, zm_ref, om_ref, o_ref):
        yg = _gelu(y_ref[...])
        sz, _ = _silu_and_grad(z_ref[...])
        o_ref[:, :MAIN_WIDTH] = (yg * _sigmoid(t_ref[...] + b_ref[...]) * sz).astype(BF16)
        szm, _ = _silu_and_grad(zm_ref[...])
        o_ref[:, MAIN_WIDTH:] = (om_ref[...] * szm).astype(BF16)

    main, z, zm, mem, cat, vec = _row_specs(tr)
    return pl.pallas_call(
        body, name="gate_a_fwd", out_shape=jax.ShapeDtypeStruct((L, D_MODEL), BF16),
        grid=(L // tr,), in_specs=[main, main, vec, z, zm, mem], out_specs=cat,
        compiler_params=_params("parallel"),
    )(y, t, b_glu.reshape(1, MAIN_WIDTH), proj, proj, o_mem)


def _gate_a_bwd(dcat, y, t, b_glu, proj, o_mem, *, tr=256):
    L = y.shape[0]
    tr = min(tr, L)

    def body(dc_ref, y_ref, t_ref, b_ref, z_ref, zm_ref, om_ref,
             dz_ref, dzm_ref, dt_ref, dyg_ref, dom_ref, db_ref):
        dmain = dc_ref[:, :MAIN_WIDTH]
        dmemo = dc_ref[:, MAIN_WIDTH:]
        yg = _gelu(y_ref[...])
        sg = _sigmoid(t_ref[...] + b_ref[...])
        sz, gz = _silu_and_grad(z_ref[...])
        dz_ref[...] = (dmain * (yg * sg) * gz).astype(BF16)
        dy2 = dmain * sz
        dyg_ref[...] = dy2 * sg
        dt = dy2 * yg * (sg * (1.0 - sg))
        dt_ref[...] = dt.astype(BF16)

        @pl.when(pl.program_id(0) == 0)
        def _():
            db_ref[...] = jnp.zeros_like(db_ref)

        db_ref[...] += jnp.sum(dt, axis=0, keepdims=True)
        szm, gzm = _silu_and_grad(zm_ref[...])
        dom_ref[...] = dmemo * szm
        dzm_ref[...] = (dmemo * om_ref[...] * gzm).astype(BF16)

    main, z, zm, mem, cat, vec = _row_specs(tr)
    outs = pl.pallas_call(
        body, name="gate_a_bwd",
        out_shape=(jax.ShapeDtypeStruct((L, MAIN_WIDTH), BF16), jax.ShapeDtypeStruct((L, MEM_WIDTH), BF16),
                   jax.ShapeDtypeStruct((L, MAIN_WIDTH), BF16), jax.ShapeDtypeStruct((L, MAIN_WIDTH), F32),
                   jax.ShapeDtypeStruct((L, MEM_WIDTH), F32), jax.ShapeDtypeStruct((1, MAIN_WIDTH), F32)),
        grid=(L // tr,), in_specs=[cat, main, main, vec, z, zm, mem],
        out_specs=(main, mem, main, main, mem, vec),
        compiler_params=_params("arbitrary"),
    )(dcat, y, t, b_glu.reshape(1, MAIN_WIDTH), proj, proj, o_mem)
    return outs


def _gelu_bwd(dyg_a, dyg_b, y, *, tr=256):
    L = y.shape[0]
    tr = min(tr, L)

    def body(a_ref, b_ref, y_ref, o_ref):
        o_ref[...] = (a_ref[...] + b_ref[...]) * _gelu_grad(y_ref[...])

    main = pl.BlockSpec((tr, MAIN_WIDTH), lambda i: (i, 0))
    return pl.pallas_call(
        body, name="gelu_bwd", out_shape=jax.ShapeDtypeStruct((L, MAIN_WIDTH), F32),
        grid=(L // tr,), in_specs=[main, main, main], out_specs=main,
        compiler_params=_params("parallel"),
    )(dyg_a, dyg_b, y)


def _gate_b_fwd(att, proj, o_mem, *, tr=256):
    L = att.shape[0]
    tr = min(tr, L)

    def body(a_ref, z_ref, zm_ref, om_ref, o_ref):
        sz, _ = _silu_and_grad(z_ref[...])
        o_ref[:, :MAIN_WIDTH] = (a_ref[...] * sz).astype(BF16)
        szm, _ = _silu_and_grad(zm_ref[...])
        o_ref[:, MAIN_WIDTH:] = (om_ref[...] * szm).astype(BF16)

    main, z, zm, mem, cat, _ = _row_specs(tr)
    return pl.pallas_call(
        body, name="gate_b_fwd", out_shape=jax.ShapeDtypeStruct((L, D_MODEL), BF16),
        grid=(L // tr,), in_specs=[main, z, zm, mem], out_specs=cat,
        compiler_params=_params("parallel"),
    )(att, proj, proj, o_mem)


def _gate_b_bwd(dcat, att, proj, o_mem, *, tr=256):
    L = att.shape[0]
    tr = min(tr, L)

    def body(dc_ref, a_ref, z_ref, zm_ref, om_ref, da_ref, dz_ref, dom_ref, dzm_ref, dl_ref):
        dmain = dc_ref[:, :MAIN_WIDTH]
        dmemo = dc_ref[:, MAIN_WIDTH:]
        att = a_ref[...]
        sz, gz = _silu_and_grad(z_ref[...])
        datt = dmain * sz
        da_ref[...] = datt
        dz_ref[...] = (dmain * att * gz).astype(BF16)
        szm, gzm = _silu_and_grad(zm_ref[...])
        dom_ref[...] = dmemo * szm
        dzm_ref[...] = (dmemo * om_ref[...] * gzm).astype(BF16)
        prod = datt * att
        for h in range(FOX_HEADS):
            dl_ref[h] = jnp.sum(prod[:, h * HEAD_DIM:(h + 1) * HEAD_DIM], axis=1, keepdims=True)

    main, z, zm, mem, cat, _ = _row_specs(tr)
    delta = pl.BlockSpec((FOX_HEADS, tr, 1), lambda i: (0, i, 0))
    return pl.pallas_call(
        body, name="gate_b_bwd",
        out_shape=(jax.ShapeDtypeStruct((L, MAIN_WIDTH), F32), jax.ShapeDtypeStruct((L, MAIN_WIDTH), BF16),
                   jax.ShapeDtypeStruct((L, MEM_WIDTH), F32), jax.ShapeDtypeStruct((L, MEM_WIDTH), BF16),
                   jax.ShapeDtypeStruct((FOX_HEADS, L, 1), F32)),
        grid=(L // tr,), in_specs=[cat, main, z, zm, mem], out_specs=(main, main, mem, mem, delta),
        compiler_params=_params("parallel"),
    )(dcat, att, proj, proj, o_mem)


_MEM_Q_COL = (2 * MAIN_WIDTH) // HEAD_DIM
_NT = (((1,), (1,)), ((), ()))
_TN = (((0,), (0,)), ((), ()))


def _mem_probs(q_ref, k_ref):
    qs = (q_ref[...] * (HEAD_DIM ** -0.5)).astype(BF16)
    s = lax.dot_general(qs, k_ref[...].astype(BF16), _NT, preferred_element_type=F32)
    e = jnp.exp(s - jnp.max(s, axis=-1, keepdims=True))
    return qs, e / jnp.sum(e, axis=-1, keepdims=True)


def _mem_attn_fwd(proj, kvm, *, tq=512):
    L = proj.shape[0]
    tq = min(tq, L)

    def body(q_ref, k_ref, v_ref, o_ref):
        _, p = _mem_probs(q_ref, k_ref)
        o_ref[...] = jnp.dot(p.astype(BF16), v_ref[...].astype(BF16), preferred_element_type=F32)

    return pl.pallas_call(
        body, name="mem_attn_fwd", out_shape=jax.ShapeDtypeStruct((L, MEM_WIDTH), F32),
        grid=(MEM_HEADS, L // tq),
        in_specs=[pl.BlockSpec((tq, HEAD_DIM), lambda h, i: (i, _MEM_Q_COL + h)),
                  pl.BlockSpec((N_MEM, HEAD_DIM), lambda h, i: (0, h)),
                  pl.BlockSpec((N_MEM, HEAD_DIM), lambda h, i: (0, MEM_HEADS + h))],
        out_specs=pl.BlockSpec((tq, HEAD_DIM), lambda h, i: (i, h)),
        compiler_params=_params("parallel", "parallel"),
    )(proj, kvm, kvm)


def _mem_attn_bwd(proj, kvm, do, *, tq=512):
    L = proj.shape[0]
    tq = min(tq, L)

    def body(q_ref, k_ref, v_ref, do_ref, dq_ref, dk_ref, dv_ref):
        @pl.when(pl.program_id(1) == 0)
        def _():
            dk_ref[...] = jnp.zeros_like(dk_ref)
            dv_ref[...] = jnp.zeros_like(dv_ref)

        qs, p = _mem_probs(q_ref, k_ref)
        dob = do_ref[...].astype(BF16)
        dp = lax.dot_general(dob, v_ref[...].astype(BF16), _NT, preferred_element_type=F32)
        ds = p * (dp - jnp.sum(p * dp, axis=-1, keepdims=True))
        dsb = ds.astype(BF16)
        dq = jnp.dot(dsb, k_ref[...].astype(BF16), preferred_element_type=F32) * (HEAD_DIM ** -0.5)
        dq_ref[...] = dq.astype(BF16)
        dk_ref[...] += lax.dot_general(dsb, qs, _TN, preferred_element_type=F32)
        dv_ref[...] += lax.dot_general(p.astype(BF16), dob, _TN, preferred_element_type=F32)

    dq, dk, dv = pl.pallas_call(
        body, name="mem_attn_bwd",
        out_shape=(jax.ShapeDtypeStruct((L, MEM_WIDTH), BF16),
                   jax.ShapeDtypeStruct((N_MEM, MEM_WIDTH), F32),
                   jax.ShapeDtypeStruct((N_MEM, MEM_WIDTH), F32)),
        grid=(MEM_HEADS, L // tq),
        in_specs=[pl.BlockSpec((tq, HEAD_DIM), lambda h, i: (i, _MEM_Q_COL + h)),
                  pl.BlockSpec((N_MEM, HEAD_DIM), lambda h, i: (0, h)),
                  pl.BlockSpec((N_MEM, HEAD_DIM), lambda h, i: (0, MEM_HEADS + h)),
                  pl.BlockSpec((tq, HEAD_DIM), lambda h, i: (i, h))],
        out_specs=(pl.BlockSpec((tq, HEAD_DIM), lambda h, i: (i, h)),
                   pl.BlockSpec((N_MEM, HEAD_DIM), lambda h, i: (0, h)),
                   pl.BlockSpec((N_MEM, HEAD_DIM), lambda h, i: (0, h))),
        compiler_params=_params("parallel", "arbitrary"),
    )(proj, kvm, kvm, do)
    return dq, jnp.concatenate([dk, dv], axis=1)


def _tile_cumsum(x, row, reverse):
    for sh in (1, 2, 4):
        if reverse:
            x = x + jnp.where(row < SUBLANES - sh, pltpu.roll(x, SUBLANES - sh, 0), 0.0)
        else:
            x = x + jnp.where(row >= sh, pltpu.roll(x, sh, 0), 0.0)
    return x


def _fgate_fwd(pre, b_pad):
    L = pre.shape[0]
    n8 = L // SUBLANES

    def body(p_ref, b_ref, o_ref):
        row = lax.broadcasted_iota(jnp.int32, (SUBLANES, LANES), 0)
        b = b_ref[...]

        def step(i, carry):
            x = p_ref[i] + b
            logf = jnp.minimum(x, 0.0) - jnp.log(1.0 + jnp.exp(-jnp.abs(x)))
            t = _tile_cumsum(logf, row, False) + carry
            o_ref[i] = t
            return t[SUBLANES - 1:SUBLANES, :]

        lax.fori_loop(0, n8, step, jnp.zeros((1, LANES), F32))

    out = pl.pallas_call(
        body, name="fgate_fwd", out_shape=jax.ShapeDtypeStruct((n8, SUBLANES, LANES), F32),
        compiler_params=_params(),
    )(pre.reshape(n8, SUBLANES, LANES), b_pad.reshape(1, LANES))
    return out.reshape(L, LANES)


def _fgate_bwd(dfcum, pre, b_pad):
    L = pre.shape[0]
    n8 = L // SUBLANES

    def body(d_ref, p_ref, b_ref, o_ref, s_ref):
        row = lax.broadcasted_iota(jnp.int32, (SUBLANES, LANES), 0)
        b = b_ref[...]

        def step(k, carry):
            c, acc = carry
            i = n8 - 1 - k
            t = _tile_cumsum(d_ref[i], row, True) + c
            dpre = t * _sigmoid(-(p_ref[i] + b))
            o_ref[i] = dpre
            return t[0:1, :], acc + dpre

        _, acc = lax.fori_loop(0, n8, step, (jnp.zeros((1, LANES), F32), jnp.zeros((SUBLANES, LANES), F32)))
        s_ref[...] = jnp.sum(acc, axis=0, keepdims=True)

    dpre, db = pl.pallas_call(
        body, name="fgate_bwd",
        out_shape=(jax.ShapeDtypeStruct((n8, SUBLANES, LANES), F32), jax.ShapeDtypeStruct((1, LANES), F32)),
        compiler_params=_params(),
    )(dfcum.reshape(n8, SUBLANES, LANES), pre.reshape(n8, SUBLANES, LANES), b_pad.reshape(1, LANES))
    return dpre.reshape(L, LANES), db


def _fox_scores(q_ref, k_ref, fq_ref, fk_ref, qi, ki, tq, tk):
    qs = (q_ref[...] * (HEAD_DIM ** -0.5)).astype(BF16)
    s = lax.dot_general(qs, k_ref[...].astype(BF16), _NT, preferred_element_type=F32)
    s = s + fq_ref[...] - fk_ref[...]
    row = qi * tq + lax.broadcasted_iota(jnp.int32, (tq, tk), 0)
    col = ki * tk + lax.broadcasted_iota(jnp.int32, (tq, tk), 1)
    return qs, jnp.where(row >= col, s, NEG_BIG)


def _fox_fwd(proj, kv, fq, fk, *, tq=256):
    L = proj.shape[0]
    tq = min(tq, L)
    nq = L // tq

    def body(q_ref, k_ref, v_ref, fq_ref, fk_ref, o_ref, lse_ref, m_s, l_s, acc_s):
        qi, ki = pl.program_id(1), pl.program_id(2)

        @pl.when(ki == 0)
        def _():
            m_s[...] = jnp.full_like(m_s, NEG_BIG)
            l_s[...] = jnp.zeros_like(l_s)
            acc_s[...] = jnp.zeros_like(acc_s)

        @pl.when(ki <= qi)
        def _():
            _, s = _fox_scores(q_ref, k_ref, fq_ref, fk_ref, qi, ki, tq, tq)
            m_new = jnp.maximum(m_s[...], jnp.max(s, axis=-1, keepdims=True))
            alpha = jnp.exp(m_s[...] - m_new)
            p = jnp.exp(s - m_new)
            l_s[...] = alpha * l_s[...] + jnp.sum(p, axis=-1, keepdims=True)
            acc_s[...] = alpha * acc_s[...] + jnp.dot(p.astype(BF16), v_ref[...].astype(BF16),
                                                      preferred_element_type=F32)
            m_s[...] = m_new

        @pl.when(ki == qi)
        def _():
            o_ref[...] = acc_s[...] / l_s[...]
            lse_ref[...] = m_s[...] + jnp.log(l_s[...])

    kmap = lambda h, i, j: (jnp.minimum(j, i), h)
    vmap = lambda h, i, j: (jnp.minimum(j, i), FOX_HEADS + h)
    return pl.pallas_call(
        body, name="fox_fwd",
        out_shape=(jax.ShapeDtypeStruct((L, MAIN_WIDTH), F32), jax.ShapeDtypeStruct((FOX_HEADS, L, 1), F32)),
        grid=(FOX_HEADS, nq, nq),
        in_specs=[pl.BlockSpec((tq, HEAD_DIM), lambda h, i, j: (i, h)),
                  pl.BlockSpec((tq, HEAD_DIM), kmap),
                  pl.BlockSpec((tq, HEAD_DIM), vmap),
                  pl.BlockSpec((None, tq, 1), lambda h, i, j: (h, i, 0)),
                  pl.BlockSpec((None, 1, tq), lambda h, i, j: (h, 0, jnp.minimum(j, i)))],
        out_specs=(pl.BlockSpec((tq, HEAD_DIM), lambda h, i, j: (i, h)),
                   pl.BlockSpec((None, tq, 1), lambda h, i, j: (h, i, 0))),
        scratch_shapes=[pltpu.VMEM((tq, 1), F32), pltpu.VMEM((tq, 1), F32), pltpu.VMEM((tq, HEAD_DIM), F32)],
        compiler_params=_params("parallel", "parallel", "arbitrary"),
    )(proj, kv, kv, fq, fk)


def _fox_bwd_dq(proj, kv, fq, fk, lse, delta, datt, *, tq=256):
    L = proj.shape[0]
    tq = min(tq, L)
    nq = L // tq

    def body(q_ref, k_ref, v_ref, fq_ref, fk_ref, lse_ref, dl_ref, do_ref, dq_ref, df_ref, acc_s, df_s):
        qi, ki = pl.program_id(1), pl.program_id(2)

        @pl.when(ki == 0)
        def _():
            acc_s[...] = jnp.zeros_like(acc_s)
            df_s[...] = jnp.zeros_like(df_s)

        @pl.when(ki <= qi)
        def _():
            _, s = _fox_scores(q_ref, k_ref, fq_ref, fk_ref, qi, ki, tq, tq)
            p = jnp.exp(s - lse_ref[...])
            dp = lax.dot_general(do_ref[...].astype(BF16), v_ref[...].astype(BF16), _NT,
                                 preferred_element_type=F32)
            ds = p * (dp - dl_ref[...])
            acc_s[...] += jnp.dot(ds.astype(BF16), k_ref[...].astype(BF16), preferred_element_type=F32)
            df_s[...] += jnp.sum(ds, axis=1, keepdims=True)

        @pl.when(ki == qi)
        def _():
            dq_ref[...] = (acc_s[...] * (HEAD_DIM ** -0.5)).astype(BF16)
            df_ref[...] = df_s[...]

    kmap = lambda h, i, j: (jnp.minimum(j, i), h)
    vmap = lambda h, i, j: (jnp.minimum(j, i), FOX_HEADS + h)
    qcol = pl.BlockSpec((None, tq, 1), lambda h, i, j: (h, i, 0))
    return pl.pallas_call(
        body, name="fox_bwd_dq",
        out_shape=(jax.ShapeDtypeStruct((L, MAIN_WIDTH), BF16), jax.ShapeDtypeStruct((FOX_HEADS, L, 1), F32)),
        grid=(FOX_HEADS, nq, nq),
        in_specs=[pl.BlockSpec((tq, HEAD_DIM), lambda h, i, j: (i, h)),
                  pl.BlockSpec((tq, HEAD_DIM), kmap),
                  pl.BlockSpec((tq, HEAD_DIM), vmap),
                  qcol,
                  pl.BlockSpec((None, 1, tq), lambda h, i, j: (h, 0, jnp.minimum(j, i))),
                  qcol, qcol,
                  pl.BlockSpec((tq, HEAD_DIM), lambda h, i, j: (i, h))],
        out_specs=(pl.BlockSpec((tq, HEAD_DIM), lambda h, i, j: (i, h)), qcol),
        scratch_shapes=[pltpu.VMEM((tq, HEAD_DIM), F32), pltpu.VMEM((tq, 1), F32)],
        compiler_params=_params("parallel", "parallel", "arbitrary"),
    )(proj, kv, kv, fq, fk, lse, delta, datt)


def _fox_bwd_dkv(proj, kv, fq, fk, lse, delta, datt, *, tq=256):
    L = proj.shape[0]
    tq = min(tq, L)
    nq = L // tq

    def body(q_ref, k_ref, v_ref, fq_ref, fk_ref, lse_ref, dl_ref, do_ref,
             dk_ref, dv_ref, df_ref, dk_s, dv_s, df_s):
        ki, qi = pl.program_id(1), pl.program_id(2)

        @pl.when(qi == 0)
        def _():
            dk_s[...] = jnp.zeros_like(dk_s)
            dv_s[...] = jnp.zeros_like(dv_s)
            df_s[...] = jnp.zeros_like(df_s)

        @pl.when(qi >= ki)
        def _():
            qs, s = _fox_scores(q_ref, k_ref, fq_ref, fk_ref, qi, ki, tq, tq)
            p = jnp.exp(s - lse_ref[...])
            dob = do_ref[...].astype(BF16)
            dp = lax.dot_general(dob, v_ref[...].astype(BF16), _NT, preferred_element_type=F32)
            ds = p * (dp - dl_ref[...])
            dv_s[...] += lax.dot_general(p.astype(BF16), dob, _TN, preferred_element_type=F32)
            dk_s[...] += lax.dot_general(ds.astype(BF16), qs, _TN, preferred_element_type=F32)
            df_s[...] -= jnp.sum(ds, axis=0, keepdims=True)

        @pl.when(qi == nq - 1)
        def _():
            dk_ref[...] = dk_s[...].astype(BF16)
            dv_ref[...] = dv_s[...].astype(BF16)
            df_ref[...] = df_s[...]

    qmap = lambda h, j, i: (jnp.maximum(i, j), h)
    qcol = pl.BlockSpec((None, tq, 1), lambda h, j, i: (h, jnp.maximum(i, j), 0))
    return pl.pallas_call(
        body, name="fox_bwd_dkv",
        out_shape=(jax.ShapeDtypeStruct((L, MAIN_WIDTH), BF16),
                   jax.ShapeDtypeStruct((L, MAIN_WIDTH), BF16),
                   jax.ShapeDtypeStruct((FOX_HEADS, 1, L), F32)),
        grid=(FOX_HEADS, nq, nq),
        in_specs=[pl.BlockSpec((tq, HEAD_DIM), qmap),
                  pl.BlockSpec((tq, HEAD_DIM), lambda h, j, i: (j, h)),
                  pl.BlockSpec((tq, HEAD_DIM), lambda h, j, i: (j, FOX_HEADS + h)),
                  qcol,
                  pl.BlockSpec((None, 1, tq), lambda h, j, i: (h, 0, j)),
                  qcol, qcol,
                  pl.BlockSpec((tq, HEAD_DIM), qmap)],
        out_specs=(pl.BlockSpec((tq, HEAD_DIM), lambda h, j, i: (j, h)),
                   pl.BlockSpec((tq, HEAD_DIM), lambda h, j, i: (j, h)),
                   pl.BlockSpec((None, 1, tq), lambda h, j, i: (h, 0, j))),
        scratch_shapes=[pltpu.VMEM((tq, HEAD_DIM), F32), pltpu.VMEM((tq, HEAD_DIM), F32),
                        pltpu.VMEM((1, tq), F32)],
        compiler_params=_params("parallel", "parallel", "arbitrary"),
    )(proj, kv, kv, fq, fk, lse, delta, datt)


def _pad_lanes(a):
    return jnp.pad(a, ((0, 0), (0, LANES - a.shape[1])))


def _mem_branch_fwd(mem, g, w_mk, proj, tag):
    memn = _rmsnorm_fwd(mem, g, name="mem_norm_" + tag, out_dtype=BF16)
    kvm = _mm(memn, w_mk, name="mem_kv_" + tag)
    return memn, kvm, _mem_attn_fwd(proj, kvm)


def _mem_branch_bwd(mem, g, w_mk, proj, memn, kvm, do_mem, tag):
    dqm, dkvm = _mem_attn_bwd(proj, kvm, do_mem)
    dkvm = dkvm.astype(BF16)
    dw_mk = _mm(memn, dkvm, ta=True, name="dw_mem_kv_" + tag, out_dtype=BF16)
    dmemn = _mm(dkvm, w_mk, tb=True, name="dmemn_" + tag)
    _, dg = _rmsnorm_bwd(mem, g, dmemn, name="mem_norm_bwd_" + tag, dx_dtype=BF16)
    return dqm, dw_mk, dg


def _local_step(x, mem, target, w):
    L = x.shape[0]
    g = {}

    b_re_t = jnp.transpose(w["b_re"], (0, 2, 1))
    b_im_t = jnp.transpose(w["b_im"], (0, 2, 1))
    ar, ai, bbr_t, bbi_t = _s5_prep(w["lam_re"], w["lam_im"], w["log_step"], b_re_t, b_im_t)
    bmat, cmat = _s5_block_mats(bbr_t, bbi_t, w["c_re"], w["c_im"])
    a_rows = _s5_a_rows(ar, ai)

    hn0 = _rmsnorm_fwd(x, w["pre_norm_g"][0], name="pre_norm_0", out_dtype=BF16)
    proj_a = _mm(hn0, w["w_in_a"], name="in_proj_a")
    y, yg, xp = _s5_fwd(proj_a, bmat, cmat, a_rows, w["d_skip"])
    t = _mm(yg, w["w_glu"], name="glu_proj")
    memn0, kvm0, om0 = _mem_branch_fwd(mem, w["mem_norm_g"][0], w["w_mem_kv"][0], proj_a, "0")
    cat0 = _gate_a_fwd(y, t, w["b_glu"], proj_a, om0)
    o0 = _mm(cat0, w["w_out"][0], name="out_proj_0")
    h1 = _rmsnorm_fwd(o0, w["post_norm_g"][0], res=x, name="post_norm_0")

    kv_in = _rmsnorm_fwd(h1, w["kv_norm_g"], name="kv_norm", out_dtype=BF16)
    kv = _mm(kv_in, w["w_kv"], name="kv_proj")
    pre_f = _mm(kv_in, w["w_fgate"], name="fgate_proj")
    b_f = jnp.pad(w["b_fgate"], (0, LANES - FOX_HEADS))
    fcum = _fgate_fwd(pre_f, b_f)
    fc = jnp.transpose(fcum[:, :FOX_HEADS])
    fq, fk = fc.reshape(FOX_HEADS, L, 1), fc.reshape(FOX_HEADS, 1, L)

    hn1 = _rmsnorm_fwd(h1, w["pre_norm_g"][1], name="pre_norm_1", out_dtype=BF16)
    proj_b = _mm(hn1, w["w_in_b"], name="in_proj_b")
    att, lse = _fox_fwd(proj_b, kv, fq, fk)
    memn1, kvm1, om1 = _mem_branch_fwd(mem, w["mem_norm_g"][1], w["w_mem_kv"][1], proj_b, "1")
    cat1 = _gate_b_fwd(att, proj_b, om1)
    o1 = _mm(cat1, w["w_out"][1], name="out_proj_1")
    h2 = _rmsnorm_fwd(o1, w["post_norm_g"][1], res=h1, name="post_norm_1")

    dh2, loss_row = _loss_head(h2, target)

    do1, dpost1 = _rmsnorm_bwd(o1, w["post_norm_g"][1], dh2, name="post_norm_bwd_1", dx_dtype=BF16)
    dcat1 = _mm(do1, w["w_out"][1], tb=True, name="dcat_1")
    g["w_out_1"] = _mm(cat1, do1, ta=True, name="dw_out_1", out_dtype=BF16)
    datt, dz1, dom1, dzm1, delta = _gate_b_bwd(dcat1, att, proj_b, om1)
    dqm1, g["w_mem_kv_1"], dmemg1 = _mem_branch_bwd(mem, w["mem_norm_g"][1], w["w_mem_kv"][1], proj_b,
                                                   memn1, kvm1, dom1, "1")
    dq, dfq = _fox_bwd_dq(proj_b, kv, fq, fk, lse, delta, datt)
    dk, dv, dfk = _fox_bwd_dkv(proj_b, kv, fq, fk, lse, delta, datt)
    dproj_b = jnp.concatenate([dq, dz1, dqm1, dzm1], axis=1)
    g["w_in_b"] = _mm(hn1, dproj_b, ta=True, name="dw_in_b", out_dtype=BF16, shards=N_CHIPS)
    dhn1 = _mm(dproj_b, w["w_in_b"], tb=True, name="dhn_1")

    dkv = jnp.concatenate([dk, dv], axis=1)
    g["w_kv"] = _mm(kv_in, dkv, ta=True, name="dw_kv", out_dtype=BF16, shards=N_CHIPS)
    dkv_in_a = _mm(dkv, w["w_kv"], tb=True, name="dkv_in_kv")
    dfcum = _pad_lanes(jnp.transpose(dfq.reshape(FOX_HEADS, L) + dfk.reshape(FOX_HEADS, L)))
    dpre_f, db_f = _fgate_bwd(dfcum, pre_f, b_f)
    g["b_fgate"] = db_f[0, :FOX_HEADS]
    g["w_fgate"] = _mm(kv_in, dpre_f, ta=True, name="dw_fgate")[:, :FOX_HEADS]
    dkv_in_b = _mm(dpre_f, w["w_fgate"], tb=True, name="dkv_in_fgate")
    dh1_kv, g["kv_norm_g"] = _rmsnorm_bwd(h1, w["kv_norm_g"], (dkv_in_a, dkv_in_b), name="kv_norm_bwd")
    dh1, dpre1 = _rmsnorm_bwd(h1, w["pre_norm_g"][1], dhn1, adds=(dh2, dh1_kv), name="pre_norm_bwd_1")

    do0, dpost0 = _rmsnorm_bwd(o0, w["post_norm_g"][0], dh1, name="post_norm_bwd_0", dx_dtype=BF16)
    dcat0 = _mm(do0, w["w_out"][0], tb=True, name="dcat_0")
    g["w_out_0"] = _mm(cat0, do0, ta=True, name="dw_out_0", out_dtype=BF16)
    dz0, dzm0, dt, dyg_a, dom0, db_glu = _gate_a_bwd(dcat0, y, t, w["b_glu"], proj_a, om0)
    g["b_glu"] = db_glu[0]
    g["w_glu"] = _mm(yg, dt, ta=True, name="dw_glu", out_dtype=BF16)
    dyg_b = _mm(dt, w["w_glu"], tb=True, name="dyg")
    dy = _gelu_bwd(dyg_a, dyg_b, y)
    du, db_blk, dc_blk, da_rows, dd_skip = _s5_bwd(proj_a, dy, xp, bmat, cmat, a_rows, w["d_skip"])
    g["d_skip"] = dd_skip[0]
    dqm0, g["w_mem_kv_0"], dmemg0 = _mem_branch_bwd(mem, w["mem_norm_g"][0], w["w_mem_kv"][0], proj_a,
                                                   memn0, kvm0, dom0, "0")
    dproj_a = jnp.concatenate([du.astype(BF16), dz0, dqm0, dzm0], axis=1)
    g["w_in_a"] = _mm(hn0, dproj_a, ta=True, name="dw_in_a", out_dtype=BF16, shards=N_CHIPS)
    dhn0 = _mm(dproj_a, w["w_in_a"], tb=True, name="dhn_0")
    grad_x, dpre0 = _rmsnorm_bwd(x, w["pre_norm_g"][0], dhn0, adds=(dh1,), name="pre_norm_bwd_0")

    dbb = _s5_block_diag(db_blk)
    dcc = _s5_block_diag(dc_blk)
    g["c_re"], g["c_im"] = dcc[0], -dcc[1]
    d_ar = da_rows[:, 0, :STATE_COLS].reshape(SSM_GROUPS, SSM_STATE)
    d_ai = da_rows[:, 0, STATE_COLS:].reshape(SSM_GROUPS, SSM_STATE)
    dlr, dli, dls, dbr_t, dbi_t = _s5_prep_bwd(w["lam_re"], w["lam_im"], w["log_step"], b_re_t, b_im_t,
                                               d_ar, d_ai, dbb[0], dbb[1])
    g["lam_re"], g["lam_im"], g["log_step"] = dlr, dli, dls[:, 0]
    g["b_re"] = jnp.transpose(dbr_t, (0, 2, 1))
    g["b_im"] = jnp.transpose(dbi_t, (0, 2, 1))
    g["pre_norm_g"] = jnp.stack([dpre0, dpre1])
    g["post_norm_g"] = jnp.stack([dpost0, dpost1])
    g["mem_norm_g"] = jnp.stack([dmemg0, dmemg1])
    return loss_row, grad_x, g


_MESH = pl.DeviceIdType.MESH
_ANY = pl.BlockSpec(memory_space=pl.ANY)


def _place():
    x, y, c = lax.axis_index("x"), lax.axis_index("y"), lax.axis_index("c")
    chips = [(1 - x, y), (x, 1 - y), (1 - x, 1 - y)]
    return x, y, c, chips


def _all_gather_chips(parts):
    n = len(parts)

    def body(*refs):
        ins, outs = refs[:n], refs[n:2 * n]
        ici_send, ici_recv, d2d_send, d2d_recv, local_sem = refs[2 * n:]
        x, y, c, chips = _place()
        me = 2 * x + y
        sib = (x, y, 1 - c)

        def ici(i, k, src_chip_j, dst):
            return pltpu.make_async_remote_copy(
                src_ref=ins[i].at[c] if src_chip_j is None else outs[i].at[src_chip_j, c],
                dst_ref=outs[i].at[me if src_chip_j is None else src_chip_j, c],
                send_sem=ici_send.at[i * 3 + k], recv_sem=ici_recv.at[i * 3 + k],
                device_id=dst, device_id_type=_MESH)

        def d2d(i, k, chip_j, half):
            return pltpu.make_async_remote_copy(
                src_ref=outs[i].at[chip_j, half], dst_ref=outs[i].at[chip_j, half],
                send_sem=d2d_send.at[i * 3 + k], recv_sem=d2d_recv.at[i * 3 + k],
                device_id=sib, device_id_type=_MESH)

        local = [pltpu.make_async_copy(ins[i], outs[i].at[me], local_sem.at[i]) for i in range(n)]
        for cp in local:
            cp.start()
        sends = [ici(i, k, None, (*chips[k], c)) for i in range(n) for k in range(3)]
        for cp in sends:
            cp.start()
        passed = []
        for k, (cx, cy) in enumerate(chips):
            for i in range(n):
                ici(i, k, 2 * cx + cy, (x, y, c)).wait_recv()
                fwd = d2d(i, k, 2 * cx + cy, c)
                fwd.start()
                passed.append(fwd)
        for k, (cx, cy) in enumerate(chips):
            for i in range(n):
                d2d(i, k, 2 * cx + cy, 1 - c).wait_recv()
        for cp in sends + passed:
            cp.wait_send()
        for cp in local:
            cp.wait()

    return pl.pallas_call(
        body, name="all_gather_weights",
        out_shape=[jax.ShapeDtypeStruct((N_CHIPS,) + p.shape, p.dtype) for p in parts],
        in_specs=[_ANY] * n, out_specs=[_ANY] * n,
        scratch_shapes=[pltpu.SemaphoreType.DMA((3 * n,)), pltpu.SemaphoreType.DMA((3 * n,)),
                        pltpu.SemaphoreType.DMA((3 * n,)), pltpu.SemaphoreType.DMA((3 * n,)),
                        pltpu.SemaphoreType.DMA((n,))],
    )(*parts)


def _swap_halves(grads):
    n = len(grads)

    def body(*refs):
        ins, outs = refs[:n], refs[n:2 * n]
        send_sem, recv_sem = refs[2 * n:]
        x, y, c, _ = _place()
        copies = [pltpu.make_async_remote_copy(
            src_ref=ins[i].at[:, 1 - c], dst_ref=outs[i],
            send_sem=send_sem.at[i], recv_sem=recv_sem.at[i],
            device_id=(x, y, 1 - c), device_id_type=_MESH) for i in range(n)]
        for cp in copies:
            cp.start()
        for cp in copies:
            cp.wait()

    return pl.pallas_call(
        body, name="grad_swap_halves",
        out_shape=[jax.ShapeDtypeStruct((N_CHIPS,) + g.shape[2:], g.dtype) for g in grads],
        in_specs=[_ANY] * n, out_specs=[_ANY] * n,
        scratch_shapes=[pltpu.SemaphoreType.DMA((n,)), pltpu.SemaphoreType.DMA((n,))],
    )(*grads)


def _pair_sum(g, r, c_idx, *, name):
    _, _, h, C = g.shape
    tr = min(h, 256)

    def body(c_ref, g_ref, r_ref, o_ref):
        o_ref[...] = (g_ref[...].astype(F32) + r_ref[...].astype(F32)).astype(o_ref.dtype)

    return pl.pallas_call(
        body, name=name, out_shape=jax.ShapeDtypeStruct((N_CHIPS, h, C), g.dtype),
        grid_spec=pltpu.PrefetchScalarGridSpec(
            num_scalar_prefetch=1, grid=(N_CHIPS, h // tr),
            in_specs=[pl.BlockSpec((None, None, tr, C), lambda j, i, s: (j, s[0], i, 0)),
                      pl.BlockSpec((None, tr, C), lambda j, i, s: (j, i, 0))],
            out_specs=pl.BlockSpec((None, tr, C), lambda j, i, s: (j, i, 0))),
        compiler_params=_params("parallel", "parallel"),
    )(c_idx, g, r)


def _send_to_owners(sums):
    n = len(sums)

    def body(*refs):
        ins, outs = refs[:n], refs[n:2 * n]
        send_sem, recv_sem = refs[2 * n:]
        x, y, c, chips = _place()
        copies = [pltpu.make_async_remote_copy(
            src_ref=ins[i].at[2 * cx + cy], dst_ref=outs[i].at[k],
            send_sem=send_sem.at[i * 3 + k], recv_sem=recv_sem.at[i * 3 + k],
            device_id=(cx, cy, c), device_id_type=_MESH)
            for i in range(n) for k, (cx, cy) in enumerate(chips)]
        for cp in copies:
            cp.start()
        for cp in copies:
            cp.wait()

    return pl.pallas_call(
        body, name="grad_send_to_owners",
        out_shape=[jax.ShapeDtypeStruct((3,) + s.shape[1:], s.dtype) for s in sums],
        in_specs=[_ANY] * n, out_specs=[_ANY] * n,
        scratch_shapes=[pltpu.SemaphoreType.DMA((3 * n,)), pltpu.SemaphoreType.DMA((3 * n,))],
    )(*sums)


def _owner_sum(s, r, j_idx, *, name):
    _, h, C = s.shape
    tr = min(h, 256)

    def body(j_ref, s_ref, r_ref, o_ref):
        acc = s_ref[...].astype(F32)
        for k in range(3):
            acc = acc + r_ref[k].astype(F32)
        o_ref[...] = acc

    return pl.pallas_call(
        body, name=name, out_shape=jax.ShapeDtypeStruct((h, C), F32),
        grid_spec=pltpu.PrefetchScalarGridSpec(
            num_scalar_prefetch=1, grid=(h // tr,),
            in_specs=[pl.BlockSpec((None, tr, C), lambda i, s: (s[0], i, 0)),
                      pl.BlockSpec((3, tr, C), lambda i, s: (0, i, 0))],
            out_specs=pl.BlockSpec((tr, C), lambda i, s: (i, 0))),
        compiler_params=_params("parallel"),
    )(j_idx, s, r)


def _share_with_sibling(halves):
    n = len(halves)

    def body(*refs):
        ins, outs = refs[:n], refs[n:2 * n]
        send_sem, recv_sem, local_sem = refs[2 * n:]
        x, y, c, _ = _place()
        local = [pltpu.make_async_copy(ins[i], outs[i].at[c], local_sem.at[i]) for i in range(n)]
        for cp in local:
            cp.start()
        copies = [pltpu.make_async_remote_copy(
            src_ref=ins[i], dst_ref=outs[i].at[c], send_sem=send_sem.at[i], recv_sem=recv_sem.at[i],
            device_id=(x, y, 1 - c), device_id_type=_MESH) for i in range(n)]
        for cp in copies:
            cp.start()
        for i in range(n):
            pltpu.make_async_remote_copy(
                src_ref=ins[i], dst_ref=outs[i].at[1 - c], send_sem=send_sem.at[i], recv_sem=recv_sem.at[i],
                device_id=(x, y, 1 - c), device_id_type=_MESH).wait_recv()
        for cp in copies:
            cp.wait_send()
        for cp in local:
            cp.wait()

    return pl.pallas_call(
        body, name="grad_share_with_sibling",
        out_shape=[jax.ShapeDtypeStruct((2,) + q.shape, q.dtype) for q in halves],
        in_specs=[_ANY] * n, out_specs=[_ANY] * n,
        scratch_shapes=[pltpu.SemaphoreType.DMA((n,)), pltpu.SemaphoreType.DMA((n,)),
                        pltpu.SemaphoreType.DMA((n,))],
    )(*halves)


def _reduce_scatter_chips(grads, c_idx, j_idx):
    views = [g.reshape(N_CHIPS, 2, g.shape[1] // 2, g.shape[2]) for g in grads]
    arrived = _swap_halves(views)
    sums = [_pair_sum(v, r, c_idx, name=f"grad_pair_sum_{i}") for i, (v, r) in enumerate(zip(views, arrived))]
    arrived = _send_to_owners(sums)
    halves = [_owner_sum(s, r, j_idx, name=f"grad_owner_sum_{i}") for i, (s, r) in enumerate(zip(sums, arrived))]
    full = _share_with_sibling(halves)
    return [f.reshape(g.shape[1], g.shape[2]) for f, g in zip(full, grads)]


def _all_reduce_small(buf):
    R = buf.shape[0]
    n_dev = 2 * N_CHIPS

    def body(x_ref, o_ref, all_ref, send_sems, recv_sems, local_sem):
        x, y, c, chips = _place()
        me, sib = (x, y, c), (x, y, 1 - c)

        def rows(px, py, pc):
            return all_ref.at[4 * px + 2 * py + pc]

        def copy(k, block, to, src=None):
            return pltpu.make_async_remote_copy(
                src_ref=rows(*block) if src is None else src, dst_ref=rows(*block),
                send_sem=send_sems.at[k], recv_sem=recv_sems.at[k], device_id=to, device_id_type=_MESH)

        mine = pltpu.make_async_copy(x_ref, rows(*me), local_sem)
        mine.start()
        first = [copy(0, me, sib, src=x_ref)]
        first += [copy(1 + j, me, (*chip, c), src=x_ref) for j, chip in enumerate(chips)]
        for cp in first:
            cp.start()
        passed = [copy(4 + j, (*chip, c), sib) for j, chip in enumerate(chips)]
        for j, chip in enumerate(chips):
            copy(1 + j, (*chip, c), me).wait_recv()
            passed[j].start()
        copy(0, sib, me).wait_recv()
        for j, chip in enumerate(chips):
            copy(4 + j, (*chip, 1 - c), me).wait_recv()
        for cp in first + passed:
            cp.wait_send()
        mine.wait()
        acc = all_ref[0]
        for d in range(1, n_dev):
            acc = acc + all_ref[d]
        o_ref[...] = acc

    vmem = pl.BlockSpec(memory_space=pltpu.VMEM)
    return pl.pallas_call(
        body, name="all_reduce_small", out_shape=jax.ShapeDtypeStruct((R, LANES), F32),
        in_specs=[vmem], out_specs=vmem,
        scratch_shapes=[pltpu.VMEM((n_dev, R, LANES), F32),
                        pltpu.SemaphoreType.DMA((7,)), pltpu.SemaphoreType.DMA((7,)), pltpu.SemaphoreType.DMA],
        compiler_params=_params(),
    )(buf)


def _adamw(w, g, m, v, *, name):
    R, C = w.shape
    tr = next(c for c in (256, 192, 128, 64, 32, 16, 8, R) if R % c == 0)

    def body(w_ref, g_ref, m_ref, v_ref, d_ref, nm_ref, nv_ref):
        g = g_ref[...]
        m = ADAM_B1 * m_ref[...] + (1.0 - ADAM_B1) * g
        v = ADAM_B2 * v_ref[...] + (1.0 - ADAM_B2) * (g * g)
        nm_ref[...] = m
        nv_ref[...] = v
        m_hat = m / (1.0 - ADAM_B1 ** ADAM_STEP)
        v_hat = v / (1.0 - ADAM_B2 ** ADAM_STEP)
        d_ref[...] = -ADAM_LR * (m_hat / (jnp.sqrt(v_hat) + ADAM_EPS) + ADAM_WD * w_ref[...])

    blk = pl.BlockSpec((tr, C), lambda i: (i, 0))
    sds = jax.ShapeDtypeStruct((R, C), F32)
    return pl.pallas_call(
        body, name=name, out_shape=(sds, sds, sds), grid=(R // tr,),
        in_specs=[blk] * 4, out_specs=(blk, blk, blk),
        compiler_params=_params("parallel"),
    )(w, g, m, v)


_TILE = SUBLANES * LANES


def _pack(arrays):
    rows = []
    for a in arrays:
        flat = a.reshape(-1)
        flat = jnp.pad(flat, (0, (-flat.shape[0]) % _TILE))
        rows.append(flat.reshape(-1, LANES))
    return jnp.concatenate(rows, axis=0)


def _unpack(buf, shapes):
    out, r = [], 0
    for s in shapes:
        size = math.prod(s)
        nr = -(-size // _TILE) * SUBLANES
        out.append(buf[r:r + nr].reshape(-1)[:size].reshape(s))
        r += nr
    return out


_BIG = ("w_in_a", "w_glu", "w_kv", "w_in_b", "w_mem_kv", "w_out")
_REPLICATED = ("pre_norm_g", "post_norm_g", "lam_re", "lam_im", "log_step", "b_re", "b_im", "c_re", "c_im",
               "kv_norm_g", "b_fgate", "mem_norm_g")
_SHARDED_SMALL = ("d_skip", "b_glu", "w_fgate")
_WEIGHTS = ("pre_norm_g", "post_norm_g", "w_in_a", "lam_re", "lam_im", "log_step", "b_re", "b_im", "c_re",
            "c_im", "d_skip", "w_glu", "b_glu", "kv_norm_g", "w_kv", "w_fgate", "b_fgate", "w_in_b",
            "mem_norm_g", "w_mem_kv", "w_out")


def _halves(a):
    return a.reshape(2, a.shape[0] // 2, a.shape[1])


def _unhalve(a):
    return a.reshape(N_CHIPS, 2 * a.shape[2], a.shape[3])


def _columns(a):
    return jnp.transpose(a, (1, 0, 2)).reshape(a.shape[1], N_CHIPS * a.shape[2])


def kernel(x, mem, pre_norm_g, post_norm_g, w_in_a, lam_re, lam_im, log_step, b_re, b_im, c_re, c_im, d_skip, w_glu, b_glu, kv_norm_g, w_kv, w_fgate, b_fgate, w_in_b, mem_norm_g, w_mem_kv, w_out, loss_target, m_pre_norm_g, m_post_norm_g, m_w_in_a, m_lam_re, m_lam_im, m_log_step, m_b_re, m_b_im, m_c_re, m_c_im, m_d_skip, m_w_glu, m_b_glu, m_kv_norm_g, m_w_kv, m_w_fgate, m_b_fgate, m_w_in_b, m_mem_norm_g, m_w_mem_kv, m_w_out, v_pre_norm_g, v_post_norm_g, v_w_in_a, v_lam_re, v_lam_im, v_log_step, v_b_re, v_b_im, v_c_re, v_c_im, v_d_skip, v_w_glu, v_b_glu, v_kv_norm_g, v_w_kv, v_w_fgate, v_b_fgate, v_w_in_b, v_mem_norm_g, v_w_mem_kv, v_w_out):
    a = dict(locals())
    xi, yi, ci = lax.axis_index("x"), lax.axis_index("y"), lax.axis_index("c")
    chip = 2 * xi + yi
    c_idx = jnp.reshape(ci, (1,)).astype(jnp.int32)
    j_idx = jnp.reshape(chip, (1,)).astype(jnp.int32)

    vec = jnp.zeros((2 * SUBLANES, MAIN_WIDTH // N_CHIPS), F32)
    vec = vec.at[0].set(a["d_skip"][0]).at[1].set(a["b_glu"][0])
    parts = [a["w_in_a"][0].astype(BF16), a["w_glu"][0].astype(BF16), a["w_kv"].astype(BF16),
             _pad_lanes(a["w_fgate"]).astype(BF16), a["w_in_b"][0].astype(BF16),
             a["w_mem_kv"].reshape(-1, a["w_mem_kv"].shape[2]).astype(BF16),
             a["w_out"].reshape(-1, D_MODEL).astype(BF16), vec]
    gat = _all_gather_chips([_halves(p) for p in parts])
    w_in_a, w_glu, w_kv, w_fg, w_in_b, w_mk, w_out, vecs = gat
    w = dict(
        w_in_a=_columns(_unhalve(w_in_a)), w_glu=w_glu.reshape(MAIN_WIDTH, MAIN_WIDTH),
        w_kv=_columns(_unhalve(w_kv)), w_fgate=w_fg.reshape(D_MODEL, LANES), w_in_b=_columns(_unhalve(w_in_b)),
        w_mem_kv=jnp.transpose(w_mk, (1, 0, 2, 3)).reshape(2, D_MODEL, 2 * MEM_WIDTH),
        w_out=jnp.transpose(w_out, (1, 0, 2, 3)).reshape(2, D_MODEL, D_MODEL),
        d_skip=vecs[:, 0, 0, :].reshape(MAIN_WIDTH), b_glu=vecs[:, 0, 1, :].reshape(MAIN_WIDTH),
        pre_norm_g=a["pre_norm_g"], post_norm_g=a["post_norm_g"], mem_norm_g=a["mem_norm_g"],
        kv_norm_g=a["kv_norm_g"], b_fgate=a["b_fgate"],
        lam_re=a["lam_re"][0], lam_im=a["lam_im"][0], log_step=a["log_step"][0],
        b_re=a["b_re"][0], b_im=a["b_im"][0], c_re=a["c_re"][0], c_im=a["c_im"][0])

    loss_row, grad_x, g = _local_step(a["x"][0], a["mem"][0], a["loss_target"][0], w)
    loss = lax.psum(jnp.sum(loss_row), MESH_AXES)

    big = [g["w_in_a"], g["w_glu"].reshape(N_CHIPS, -1, MAIN_WIDTH), g["w_kv"], g["w_in_b"],
           g["w_mem_kv_0"].reshape(N_CHIPS, -1, 2 * MEM_WIDTH), g["w_mem_kv_1"].reshape(N_CHIPS, -1, 2 * MEM_WIDTH),
           g["w_out_0"].reshape(N_CHIPS, -1, D_MODEL), g["w_out_1"].reshape(N_CHIPS, -1, D_MODEL)]
    r_in_a, r_glu, r_kv, r_in_b, r_mk0, r_mk1, r_out0, r_out1 = _reduce_scatter_chips(big, c_idx, j_idx)
    grads = {"w_in_a": r_in_a[None], "w_glu": r_glu[None], "w_kv": r_kv, "w_in_b": r_in_b[None],
             "w_mem_kv": jnp.stack([r_mk0, r_mk1]), "w_out": jnp.stack([r_out0, r_out1])}

    small_names = _REPLICATED + _SHARDED_SMALL
    small = _all_reduce_small(_pack([g[n] for n in small_names]))
    small = dict(zip(small_names, _unpack(small, [g[n].shape for n in small_names])))
    for n in _REPLICATED:
        grads[n] = small[n].reshape(a[n].shape)
    nd = MAIN_WIDTH // N_CHIPS
    grads["d_skip"] = lax.dynamic_slice(small["d_skip"], (chip * nd,), (nd,))[None]
    grads["b_glu"] = lax.dynamic_slice(small["b_glu"], (chip * nd,), (nd,))[None]
    nf = D_MODEL // N_CHIPS
    grads["w_fgate"] = lax.dynamic_slice(small["w_fgate"], (chip * nf, 0), (nf, FOX_HEADS))

    delta, new_m, new_v = {}, {}, {}
    for n in _BIG:
        shape = a[n].shape
        d2 = (-1, shape[-1])
        d, m, v = _adamw(a[n].reshape(d2), grads[n].reshape(d2), a["m_" + n].reshape(d2),
                         a["v_" + n].reshape(d2), name="adamw_" + n)
        delta[n], new_m[n], new_v[n] = d.reshape(shape), m.reshape(shape), v.reshape(shape)
    shapes = [a[n].shape for n in small_names]
    d, m, v = _adamw(_pack([a[n] for n in small_names]), _pack([grads[n] for n in small_names]),
                     _pack([a["m_" + n] for n in small_names]), _pack([a["v_" + n] for n in small_names]),
                     name="adamw_small")
    for n, dd, mm, vv in zip(small_names, _unpack(d, shapes), _unpack(m, shapes), _unpack(v, shapes)):
        delta[n], new_m[n], new_v[n] = dd, mm, vv

    return (loss, grad_x[None], *[grads[n] for n in _WEIGHTS], *[delta[n] for n in _WEIGHTS],
            *[new_m[n] for n in _WEIGHTS], *[new_v[n] for n in _WEIGHTS])
```

```python
import functools
import math

import jax
import jax.numpy as jnp
from jax import lax
from jax.experimental import pallas as pl
from jax.experimental.pallas import tpu as pltpu

F32 = jnp.float32
BF16 = jnp.bfloat16

D_MODEL = 2048
N_MEM = 256
MAIN_WIDTH = 1536
MEM_WIDTH = 512
IN_WIDTH = 2 * MAIN_WIDTH + 2 * MEM_WIDTH
HEAD_DIM = 128
FOX_HEADS = MAIN_WIDTH // HEAD_DIM
MEM_HEADS = MEM_WIDTH // HEAD_DIM
SSM_GROUP = 16
SSM_GROUPS = MAIN_WIDTH // SSM_GROUP
SSM_STATE = 64
GROUPS_PER_BLOCK = 8
SSM_BLOCKS = SSM_GROUPS // GROUPS_PER_BLOCK
STATE_COLS = GROUPS_PER_BLOCK * SSM_STATE
EPS = 1e-6
ADAM_LR = 0.001
ADAM_B1 = 0.9
ADAM_B2 = 0.999
ADAM_EPS = 1e-08
ADAM_WD = 0.01
ADAM_STEP = 10
N_CHIPS = 4
LANES = 128
SUBLANES = 8
VMEM_LIMIT_BYTES = 56 * 1024 * 1024
NEG_BIG = -1e30
MESH_AXES = ("x", "y", "c")


def _params(*sem):
    return pltpu.CompilerParams(dimension_semantics=sem if sem else None,
                                vmem_limit_bytes=VMEM_LIMIT_BYTES)


def _sigmoid(x):
    return 1.0 / (1.0 + jnp.exp(-x))


def _gelu(x):
    c = math.sqrt(2.0 / math.pi)
    return 0.5 * x * (1.0 + jnp.tanh(c * (x + 0.044715 * (x * x * x))))


def _gelu_grad(x):
    c = math.sqrt(2.0 / math.pi)
    t = jnp.tanh(c * (x + 0.044715 * (x * x * x)))
    return 0.5 * (1.0 + t) + 0.5 * x * (1.0 - t * t) * (c * (1.0 + 3.0 * 0.044715 * (x * x)))


def _silu_and_grad(z):
    s = _sigmoid(z)
    return z * s, s * (1.0 + z * (1.0 - s))


_TILE_CHOICES = (2048, 1024, 768, 512, 384, 256, LANES)


def _tile(n, cap):
    return next(c for c in _TILE_CHOICES if c <= cap and n % c == 0)


def _mm(a, b, *, name, ta=False, tb=False, out_dtype=F32, shards=1, tm=1024, tn=1024, tk=2048):
    if ta:
        K, M = a.shape
    else:
        M, K = a.shape
    if tb:
        N, kb = b.shape
    else:
        kb, N = b.shape
    assert K == kb, (a.shape, b.shape)
    ns = N // shards
    tm, tn, tk = _tile(M, tm), _tile(ns, tn), _tile(K, tk)
    assert M % tm == 0 and ns % tn == 0 and K % tk == 0 and N % shards == 0
    nk = K // tk
    dn = (((0 if ta else 1,), (1 if tb else 0,)), ((), ()))

    def body(a_ref, b_ref, o_ref, acc_ref):
        k = pl.program_id(2)

        @pl.when(k == 0)
        def _():
            acc_ref[...] = jnp.zeros_like(acc_ref)

        acc_ref[...] += lax.dot_general(a_ref[...].astype(BF16), b_ref[...].astype(BF16), dn,
                                        preferred_element_type=F32)

        @pl.when(k == nk - 1)
        def _():
            o_ref[...] = acc_ref[...].astype(o_ref.dtype)

    a_spec = (pl.BlockSpec((tk, tm), lambda i, j, k: (k, i)) if ta
              else pl.BlockSpec((tm, tk), lambda i, j, k: (i, k)))
    b_spec = (pl.BlockSpec((tn, tk), lambda i, j, k: (j, k)) if tb
              else pl.BlockSpec((tk, tn), lambda i, j, k: (k, j)))
    if shards == 1:
        out_shape = jax.ShapeDtypeStruct((M, N), out_dtype)
        o_spec = pl.BlockSpec((tm, tn), lambda i, j, k: (i, j))
    else:
        nb = ns // tn
        out_shape = jax.ShapeDtypeStruct((shards, M, ns), out_dtype)
        o_spec = pl.BlockSpec((None, tm, tn), lambda i, j, k: (j // nb, i, j % nb))
    return pl.pallas_call(
        body, name=name, out_shape=out_shape,
        grid=(M // tm, N // tn, nk),
        in_specs=[a_spec, b_spec], out_specs=o_spec,
        scratch_shapes=[pltpu.VMEM((tm, tn), F32)],
        compiler_params=_params("parallel", "parallel", "arbitrary"),
    )(a, b)


def _rmsnorm_fwd(x, g, *, name, res=None, out_dtype=F32, tr=256):
    L, D = x.shape
    tr = min(tr, L)
    has_res = res is not None

    def body(*refs):
        if has_res:
            x_ref, g_ref, r_ref, o_ref = refs
        else:
            x_ref, g_ref, o_ref = refs
        xf = x_ref[...]
        r = lax.rsqrt(jnp.mean(xf * xf, axis=-1, keepdims=True) + EPS)
        y = xf * r * g_ref[...]
        if has_res:
            y = r_ref[...] + y
        o_ref[...] = y.astype(o_ref.dtype)

    row = pl.BlockSpec((tr, D), lambda i: (i, 0))
    vec = pl.BlockSpec((1, D), lambda i: (0, 0))
    ins = [x, g.reshape(1, D)] + ([res] if has_res else [])
    return pl.pallas_call(
        body, name=name, out_shape=jax.ShapeDtypeStruct((L, D), out_dtype),
        grid=(L // tr,), in_specs=[row, vec] + ([row] if has_res else []), out_specs=row,
        compiler_params=_params("parallel"),
    )(*ins)


def _rmsnorm_bwd(x, g, dy, *, name, adds=(), dx_dtype=F32, tr=256):
    L, D = x.shape
    tr = min(tr, L)
    dys = dy if isinstance(dy, tuple) else (dy,)
    n_dy, n_add = len(dys), len(adds)

    def body(*refs):
        x_ref, g_ref = refs[:2]
        dy_refs = refs[2:2 + n_dy]
        add_refs = refs[2 + n_dy:2 + n_dy + n_add]
        dx_ref, dg_ref = refs[2 + n_dy + n_add:]
        xf = x_ref[...]
        dyf = dy_refs[0][...].astype(F32)
        for d_ref in dy_refs[1:]:
            dyf = dyf + d_ref[...].astype(F32)
        r = lax.rsqrt(jnp.mean(xf * xf, axis=-1, keepdims=True) + EPS)
        gy = dyf * g_ref[...]
        c = jnp.mean(xf * gy, axis=-1, keepdims=True) * (r * r * r)
        dx = gy * r - xf * c
        for a_ref in add_refs:
            dx = dx + a_ref[...].astype(F32)
        dx_ref[...] = dx.astype(dx_ref.dtype)

        @pl.when(pl.program_id(0) == 0)
        def _():
            dg_ref[...] = jnp.zeros_like(dg_ref)

        dg_ref[...] += jnp.sum(dyf * xf * r, axis=0, keepdims=True)

    row = pl.BlockSpec((tr, D), lambda i: (i, 0))
    vec = pl.BlockSpec((1, D), lambda i: (0, 0))
    dx, dg = pl.pallas_call(
        body, name=name,
        out_shape=(jax.ShapeDtypeStruct((L, D), dx_dtype), jax.ShapeDtypeStruct((1, D), F32)),
        grid=(L // tr,), in_specs=[row, vec] + [row] * (n_dy + n_add), out_specs=(row, vec),
        compiler_params=_params("arbitrary"),
    )(x, g.reshape(1, D), *dys, *adds)
    return dx, dg.reshape(D)


def _loss_head(h, target, *, tr=256):
    L, D = h.shape
    tr = min(tr, L)

    def body(h_ref, t_ref, dh_ref, loss_ref):
        e = h_ref[...] - t_ref[...]
        dh_ref[...] = e * (1.0 / D)

        @pl.when(pl.program_id(0) == 0)
        def _():
            loss_ref[...] = jnp.zeros_like(loss_ref)

        loss_ref[...] += jnp.sum(e * e, axis=0, keepdims=True) * (0.5 / D)

    row = pl.BlockSpec((tr, D), lambda i: (i, 0))
    vec = pl.BlockSpec((1, D), lambda i: (0, 0))
    dh, lp = pl.pallas_call(
        body, name="loss_head",
        out_shape=(jax.ShapeDtypeStruct((L, D), F32), jax.ShapeDtypeStruct((1, D), F32)),
        grid=(L // tr,), in_specs=[row, row], out_specs=(row, vec),
        compiler_params=_params("arbitrary"),
    )(h, target)
    return dh, lp


def _s5_coeffs(lr, li, ls):
    dt = jnp.exp(ls)
    mag = jnp.exp(lr * dt)
    ar = mag * jnp.cos(li * dt)
    ai = mag * jnp.sin(li * dt)
    den = lr * lr + li * li
    cr = ((ar - 1.0) * lr + ai * li) / den
    ci = (ai * lr - (ar - 1.0) * li) / den
    return dt, ar, ai, den, cr, ci


def _s5_prep(lam_re, lam_im, log_step, b_re_t, b_im_t):
    G, P = lam_re.shape
    H = b_re_t.shape[1]

    def body(lr_ref, li_ref, ls_ref, br_ref, bi_ref, ar_ref, ai_ref, bbr_ref, bbi_ref):
        _, ar, ai, _, cr, ci = _s5_coeffs(lr_ref[...], li_ref[...], ls_ref[...])
        ar_ref[...] = ar
        ai_ref[...] = ai
        br, bi = br_ref[...], bi_ref[...]
        crb, cib = cr[:, None, :], ci[:, None, :]
        bbr_ref[...] = crb * br - cib * bi
        bbi_ref[...] = crb * bi + cib * br

    return pl.pallas_call(
        body, name="s5_prep",
        out_shape=(jax.ShapeDtypeStruct((G, P), F32), jax.ShapeDtypeStruct((G, P), F32),
                   jax.ShapeDtypeStruct((G, H, P), F32), jax.ShapeDtypeStruct((G, H, P), F32)),
        compiler_params=_params(),
    )(lam_re, lam_im, log_step.reshape(G, 1), b_re_t, b_im_t)


def _s5_prep_bwd(lam_re, lam_im, log_step, b_re_t, b_im_t, d_ar, d_ai, d_bbr, d_bbi):
    G, P = lam_re.shape
    H = b_re_t.shape[1]

    def body(lr_ref, li_ref, ls_ref, br_ref, bi_ref, dar_ref, dai_ref, dbbr_ref, dbbi_ref,
             dlr_ref, dli_ref, dls_ref, dbr_ref, dbi_ref):
        lr, li = lr_ref[...], li_ref[...]
        dt, ar, ai, den, cr, ci = _s5_coeffs(lr, li, ls_ref[...])
        br, bi = br_ref[...], bi_ref[...]
        gbr, gbi = dbbr_ref[...], dbbi_ref[...]
        crb, cib = cr[:, None, :], ci[:, None, :]
        dbr_ref[...] = crb * gbr + cib * gbi
        dbi_ref[...] = crb * gbi - cib * gbr
        gcr = jnp.sum(br * gbr + bi * gbi, axis=1)
        gci = jnp.sum(br * gbi - bi * gbr, axis=1)
        ilr, ili = lr / den, -li / den
        gar = dar_ref[...] + (ilr * gcr + ili * gci)
        gai = dai_ref[...] + (ilr * gci - ili * gcr)
        qr, qi = cr * ilr - ci * ili, cr * ili + ci * ilr
        glr = -(qr * gcr + qi * gci)
        gli = -(qr * gci - qi * gcr)
        glr = glr + dt * (ar * gar + ai * gai)
        gli = gli + dt * (ar * gai - ai * gar)
        wr, wi = lr * ar - li * ai, lr * ai + li * ar
        gdt = jnp.sum(wr * gar + wi * gai, axis=1, keepdims=True)
        dlr_ref[...] = glr
        dli_ref[...] = gli
        dls_ref[...] = gdt * dt

    return pl.pallas_call(
        body, name="s5_prep_bwd",
        out_shape=(jax.ShapeDtypeStruct((G, P), F32), jax.ShapeDtypeStruct((G, P), F32),
                   jax.ShapeDtypeStruct((G, 1), F32),
                   jax.ShapeDtypeStruct((G, H, P), F32), jax.ShapeDtypeStruct((G, H, P), F32)),
        compiler_params=_params(),
    )(lam_re, lam_im, log_step.reshape(G, 1), b_re_t, b_im_t, d_ar, d_ai, d_bbr, d_bbi)


def _s5_block_mats(bbr_t, bbi_t, c_re, c_im):
    eye = jnp.eye(GROUPS_PER_BLOCK, dtype=F32)
    bb = jnp.stack([bbr_t, bbi_t], axis=2)
    bb = bb.reshape(SSM_BLOCKS, GROUPS_PER_BLOCK, SSM_GROUP, 2, 1, SSM_STATE)
    bmat = bb * eye[None, :, None, None, :, None]
    bmat = bmat.reshape(SSM_BLOCKS, LANES, 2 * STATE_COLS)
    cc = jnp.stack([c_re, -c_im], axis=0)
    cc = cc.reshape(2, SSM_BLOCKS, GROUPS_PER_BLOCK, SSM_GROUP, SSM_STATE)
    cc = jnp.transpose(cc, (1, 0, 2, 4, 3))
    cmat = cc[:, :, :, :, None, :] * eye[None, None, :, None, :, None]
    cmat = cmat.reshape(SSM_BLOCKS, 2 * STATE_COLS, LANES)
    return bmat.astype(BF16), cmat.astype(BF16)


def _s5_block_diag(dmat):
    eye = jnp.eye(GROUPS_PER_BLOCK, dtype=F32)
    d = dmat.reshape(SSM_BLOCKS, GROUPS_PER_BLOCK, SSM_GROUP, 2, GROUPS_PER_BLOCK, SSM_STATE)
    d = jnp.sum(d * eye[None, :, None, None, :, None], axis=4)
    d = jnp.transpose(d, (3, 0, 1, 2, 4))
    return d.reshape(2, SSM_GROUPS, SSM_GROUP, SSM_STATE)


def _s5_a_rows(ar, ai):
    a = jnp.concatenate([ar.reshape(SSM_BLOCKS, STATE_COLS), ai.reshape(SSM_BLOCKS, STATE_COLS)], axis=1)
    return jnp.broadcast_to(a[:, None, :], (SSM_BLOCKS, SUBLANES, 2 * STATE_COLS))


def _s5_fwd(proj, bmat, cmat, a_rows, d_skip, *, tc=512):
    L = proj.shape[0]
    tc = min(tc, L)
    nt = L // tc
    n8 = tc // SUBLANES
    S = STATE_COLS

    def body(u_ref, b_ref, c_ref, a_ref, d_ref, y_ref, yg_ref, xp_ref, bu_s, xp_s, carry_s):
        @pl.when(pl.program_id(1) == 0)
        def _():
            carry_s[...] = jnp.zeros_like(carry_s)

        u = u_ref[...]
        bu = jnp.dot(u.astype(BF16), b_ref[...], preferred_element_type=F32)
        bu_s[...] = bu.reshape(n8, SUBLANES, 2 * S)
        ar, ai = a_ref[0:1, :S], a_ref[0:1, S:]

        def step(i, carry):
            cr, ci = carry
            for j in range(SUBLANES):
                xp_s[i, j:j + 1, :S] = cr
                xp_s[i, j:j + 1, S:] = ci
                br = bu_s[i, j:j + 1, :S]
                bi = bu_s[i, j:j + 1, S:]
                cr, ci = ar * cr - ai * ci + br, ar * ci + ai * cr + bi
            return cr, ci

        cr, ci = lax.fori_loop(0, n8, step, (carry_s[0:1, :S], carry_s[0:1, S:]))
        carry_s[0:1, :S] = cr
        carry_s[0:1, S:] = ci
        xp = xp_s[...].reshape(tc, 2 * S)
        xp_ref[...] = xp
        x_re = ar * xp[:, :S] - ai * xp[:, S:] + bu[:, :S]
        x_im = ar * xp[:, S:] + ai * xp[:, :S] + bu[:, S:]
        xs = jnp.concatenate([x_re, x_im], axis=1).astype(BF16)
        y = jnp.dot(xs, c_ref[...], preferred_element_type=F32) + d_ref[...] * u
        y_ref[...] = y
        yg_ref[...] = _gelu(y).astype(BF16)

    return pl.pallas_call(
        body, name="s5_fwd",
        out_shape=(jax.ShapeDtypeStruct((L, MAIN_WIDTH), F32),
                   jax.ShapeDtypeStruct((L, MAIN_WIDTH), BF16),
                   jax.ShapeDtypeStruct((L, SSM_BLOCKS * 2 * S), F32)),
        grid=(SSM_BLOCKS, nt),
        in_specs=[pl.BlockSpec((tc, LANES), lambda b, t: (t, b)),
                  pl.BlockSpec((None, LANES, 2 * S), lambda b, t: (b, 0, 0)),
                  pl.BlockSpec((None, 2 * S, LANES), lambda b, t: (b, 0, 0)),
                  pl.BlockSpec((None, SUBLANES, 2 * S), lambda b, t: (b, 0, 0)),
                  pl.BlockSpec((1, LANES), lambda b, t: (0, b))],
        out_specs=(pl.BlockSpec((tc, LANES), lambda b, t: (t, b)),
                   pl.BlockSpec((tc, LANES), lambda b, t: (t, b)),
                   pl.BlockSpec((tc, 2 * S), lambda b, t: (t, b))),
        scratch_shapes=[pltpu.VMEM((n8, SUBLANES, 2 * S), F32),
                        pltpu.VMEM((n8, SUBLANES, 2 * S), F32),
                        pltpu.VMEM((SUBLANES, 2 * S), F32)],
        compiler_params=_params("parallel", "arbitrary"),
    )(proj, bmat, cmat, a_rows, d_skip.reshape(1, MAIN_WIDTH))


def _s5_bwd(proj, dy, xp, bmat, cmat, a_rows, d_skip, *, tc=512):
    L = proj.shape[0]
    tc = min(tc, L)
    nt = L // tc
    n8 = tc // SUBLANES
    S = STATE_COLS
    nn = (((1,), (1,)), ((), ()))
    tn = (((0,), (0,)), ((), ()))

    def body(u_ref, dy_ref, xp_ref, b_ref, c_ref, a_ref, d_ref,
             du_ref, db_ref, dc_ref, da_ref, dd_ref, dl_s, carry_s):
        @pl.when(pl.program_id(1) == 0)
        def _():
            carry_s[...] = jnp.zeros_like(carry_s)
            db_ref[...] = jnp.zeros_like(db_ref)
            dc_ref[...] = jnp.zeros_like(dc_ref)
            da_ref[...] = jnp.zeros_like(da_ref)
            dd_ref[...] = jnp.zeros_like(dd_ref)

        u = u_ref[...]
        dy = dy_ref[...]
        xp = xp_ref[...]
        ub = u.astype(BF16)
        dyb = dy.astype(BF16)
        ar, ai = a_ref[0:1, :S], a_ref[0:1, S:]
        bu = jnp.dot(ub, b_ref[...], preferred_element_type=F32)
        x_re = ar * xp[:, :S] - ai * xp[:, S:] + bu[:, :S]
        x_im = ar * xp[:, S:] + ai * xp[:, :S] + bu[:, S:]
        xs = jnp.concatenate([x_re, x_im], axis=1).astype(BF16)
        dc_ref[...] += lax.dot_general(dyb, xs, tn, preferred_element_type=F32)
        dx = lax.dot_general(dyb, c_ref[...], nn, preferred_element_type=F32)
        dl_s[...] = dx.reshape(n8, SUBLANES, 2 * S)

        def step(k, carry):
            cr, ci = carry
            i = n8 - 1 - k
            for j in range(SUBLANES - 1, -1, -1):
                lr = dl_s[i, j:j + 1, :S] + (ar * cr + ai * ci)
                li = dl_s[i, j:j + 1, S:] + (ar * ci - ai * cr)
                dl_s[i, j:j + 1, :S] = lr
                dl_s[i, j:j + 1, S:] = li
                cr, ci = lr, li
            return cr, ci

        cr, ci = lax.fori_loop(0, n8, step, (carry_s[0:1, :S], carry_s[0:1, S:]))
        carry_s[0:1, :S] = cr
        carry_s[0:1, S:] = ci
        lam = dl_s[...].reshape(tc, 2 * S)
        l_re, l_im = lam[:, :S], lam[:, S:]
        da_ref[0:1, :S] += jnp.sum(l_re * xp[:, :S] + l_im * xp[:, S:], axis=0, keepdims=True)
        da_ref[0:1, S:] += jnp.sum(l_im * xp[:, :S] - l_re * xp[:, S:], axis=0, keepdims=True)
        lamb = lam.astype(BF16)
        du_ref[...] = lax.dot_general(lamb, b_ref[...], nn, preferred_element_type=F32) + d_ref[...] * dy
        db_ref[...] += lax.dot_general(ub, lamb, tn, preferred_element_type=F32)
        dd_ref[0:1, :] += jnp.sum(dy * u, axis=0, keepdims=True)

    rev = lambda b, t: (nt - 1 - t, b)
    return pl.pallas_call(
        body, name="s5_bwd",
        out_shape=(jax.ShapeDtypeStruct((L, MAIN_WIDTH), F32),
                   jax.ShapeDtypeStruct((SSM_BLOCKS, LANES, 2 * S), F32),
                   jax.ShapeDtypeStruct((SSM_BLOCKS, LANES, 2 * S), F32),
                   jax.ShapeDtypeStruct((SSM_BLOCKS, SUBLANES, 2 * S), F32),
                   jax.ShapeDtypeStruct((SUBLANES, MAIN_WIDTH), F32)),
        grid=(SSM_BLOCKS, nt),
        in_specs=[pl.BlockSpec((tc, LANES), rev),
                  pl.BlockSpec((tc, LANES), rev),
                  pl.BlockSpec((tc, 2 * S), rev),
                  pl.BlockSpec((None, LANES, 2 * S), lambda b, t: (b, 0, 0)),
                  pl.BlockSpec((None, 2 * S, LANES), lambda b, t: (b, 0, 0)),
                  pl.BlockSpec((None, SUBLANES, 2 * S), lambda b, t: (b, 0, 0)),
                  pl.BlockSpec((1, LANES), lambda b, t: (0, b))],
        out_specs=(pl.BlockSpec((tc, LANES), rev),
                   pl.BlockSpec((None, LANES, 2 * S), lambda b, t: (b, 0, 0)),
                   pl.BlockSpec((None, LANES, 2 * S), lambda b, t: (b, 0, 0)),
                   pl.BlockSpec((None, SUBLANES, 2 * S), lambda b, t: (b, 0, 0)),
                   pl.BlockSpec((SUBLANES, LANES), lambda b, t: (0, b))),
        scratch_shapes=[pltpu.VMEM((n8, SUBLANES, 2 * S), F32),
                        pltpu.VMEM((SUBLANES, 2 * S), F32)],
        compiler_params=_params("parallel", "arbitrary"),
    )(proj, dy, xp, bmat, cmat, a_rows, d_skip.reshape(1, MAIN_WIDTH))


def _row_specs(tr):
    main = pl.BlockSpec((tr, MAIN_WIDTH), lambda i: (i, 0))
    z = pl.BlockSpec((tr, MAIN_WIDTH), lambda i: (i, 1))
    zm = pl.BlockSpec((tr, MEM_WIDTH), lambda i: (i, IN_WIDTH // MEM_WIDTH - 1))
    mem = pl.BlockSpec((tr, MEM_WIDTH), lambda i: (i, 0))
    cat = pl.BlockSpec((tr, D_MODEL), lambda i: (i, 0))
    vec = pl.BlockSpec((1, MAIN_WIDTH), lambda i: (0, 0))
    return main, z, zm, mem, cat, vec


def _gate_a_fwd(y, t, b_glu, proj, o_mem, *, tr=256):
    L = y.shape[0]
    tr = min(tr, L)

    def body(y_ref, t_ref, b_ref, z_ref, zm_ref, om_ref, o_ref):
        yg = _gelu(y_ref[...])
        sz, _ = _silu_and_grad(z_ref[...])
        o_ref[:, :MAIN_WIDTH] = (yg * _sigmoid(t_ref[...] + b_ref[...]) * sz).astype(BF16)
        szm, _ = _silu_and_grad(zm_ref[...])
        o_ref[:, MAIN_WIDTH:] = (om_ref[...] * szm).astype(BF16)

    main, z, zm, mem, cat, vec = _row_specs(tr)
    return pl.pallas_call(
        body, name="gate_a_fwd", out_shape=jax.ShapeDtypeStruct((L, D_MODEL), BF16),
        grid=(L // tr,), in_specs=[main, main, vec, z, zm, mem], out_specs=cat,
        compiler_params=_params("parallel"),
    )(y, t, b_glu.reshape(1, MAIN_WIDTH), proj, proj, o_mem)


def _gate_a_bwd(dcat, y, t, b_glu, proj, o_mem, *, tr=256):
    L = y.shape[0]
    tr = min(tr, L)

    def body(dc_ref, y_ref, t_ref, b_ref, z_ref, zm_ref, om_ref,
             dz_ref, dzm_ref, dt_ref, dyg_ref, dom_ref, db_ref):
        dmain = dc_ref[:, :MAIN_WIDTH]
        dmemo = dc_ref[:, MAIN_WIDTH:]
        yg = _gelu(y_ref[...])
        sg = _sigmoid(t_ref[...] + b_ref[...])
        sz, gz = _silu_and_grad(z_ref[...])
        dz_ref[...] = (dmain * (yg * sg) * gz).astype(BF16)
        dy2 = dmain * sz
        dyg_ref[...] = dy2 * sg
        dt = dy2 * yg * (sg * (1.0 - sg))
        dt_ref[...] = dt.astype(BF16)

        @pl.when(pl.program_id(0) == 0)
        def _():
            db_ref[...] = jnp.zeros_like(db_ref)

        db_ref[...] += jnp.sum(dt, axis=0, keepdims=True)
        szm, gzm = _silu_and_grad(zm_ref[...])
        dom_ref[...] = dmemo * szm
        dzm_ref[...] = (dmemo * om_ref[...] * gzm).astype(BF16)

    main, z, zm, mem, cat, vec = _row_specs(tr)
    outs = pl.pallas_call(
        body, name="gate_a_bwd",
        out_shape=(jax.ShapeDtypeStruct((L, MAIN_WIDTH), BF16), jax.ShapeDtypeStruct((L, MEM_WIDTH), BF16),
                   jax.ShapeDtypeStruct((L, MAIN_WIDTH), BF16), jax.ShapeDtypeStruct((L, MAIN_WIDTH), F32),
                   jax.ShapeDtypeStruct((L, MEM_WIDTH), F32), jax.ShapeDtypeStruct((1, MAIN_WIDTH), F32)),
        grid=(L // tr,), in_specs=[cat, main, main, vec, z, zm, mem],
        out_specs=(main, mem, main, main, mem, vec),
        compiler_params=_params("arbitrary"),
    )(dcat, y, t, b_glu.reshape(1, MAIN_WIDTH), proj, proj, o_mem)
    return outs


def _gelu_bwd(dyg_a, dyg_b, y, *, tr=256):
    L = y.shape[0]
    tr = min(tr, L)

    def body(a_ref, b_ref, y_ref, o_ref):
        o_ref[...] = (a_ref[...] + b_ref[...]) * _gelu_grad(y_ref[...])

    main = pl.BlockSpec((tr, MAIN_WIDTH), lambda i: (i, 0))
    return pl.pallas_call(
        body, name="gelu_bwd", out_shape=jax.ShapeDtypeStruct((L, MAIN_WIDTH), F32),
        grid=(L // tr,), in_specs=[main, main, main], out_specs=main,
        compiler_params=_params("parallel"),
    )(dyg_a, dyg_b, y)


def _gate_b_fwd(att, proj, o_mem, *, tr=256):
    L = att.shape[0]
    tr = min(tr, L)

    def body(a_ref, z_ref, zm_ref, om_ref, o_ref):
        sz, _ = _silu_and_grad(z_ref[...])
        o_ref[:, :MAIN_WIDTH] = (a_ref[...] * sz).astype(BF16)
        szm, _ = _silu_and_grad(zm_ref[...])
        o_ref[:, MAIN_WIDTH:] = (om_ref[...] * szm).astype(BF16)

    main, z, zm, mem, cat, _ = _row_specs(tr)
    return pl.pallas_call(
        body, name="gate_b_fwd", out_shape=jax.ShapeDtypeStruct((L, D_MODEL), BF16),
        grid=(L // tr,), in_specs=[main, z, zm, mem], out_specs=cat,
        compiler_params=_params("parallel"),
    )(att, proj, proj, o_mem)


def _gate_b_bwd(dcat, att, proj, o_mem, *, tr=256):
    L = att.shape[0]
    tr = min(tr, L)

    def body(dc_ref, a_ref, z_ref, zm_ref, om_ref, da_ref, dz_ref, dom_ref, dzm_ref, dl_ref):
        dmain = dc_ref[:, :MAIN_WIDTH]
        dmemo = dc_ref[:, MAIN_WIDTH:]
        att = a_ref[...]
        sz, gz = _silu_and_grad(z_ref[...])
        datt = dmain * sz
        da_ref[...] = datt
        dz_ref[...] = (dmain * att * gz).astype(BF16)
        szm, gzm = _silu_and_grad(zm_ref[...])
        dom_ref[...] = dmemo * szm
        dzm_ref[...] = (dmemo * om_ref[...] * gzm).astype(BF16)
        prod = datt * att
        for h in range(FOX_HEADS):
            dl_ref[h] = jnp.sum(prod[:, h * HEAD_DIM:(h + 1) * HEAD_DIM], axis=1, keepdims=True)

    main, z, zm, mem, cat, _ = _row_specs(tr)
    delta = pl.BlockSpec((FOX_HEADS, tr, 1), lambda i: (0, i, 0))
    return pl.pallas_call(
        body, name="gate_b_bwd",
        out_shape=(jax.ShapeDtypeStruct((L, MAIN_WIDTH), F32), jax.ShapeDtypeStruct((L, MAIN_WIDTH), BF16),
                   jax.ShapeDtypeStruct((L, MEM_WIDTH), F32), jax.ShapeDtypeStruct((L, MEM_WIDTH), BF16),
                   jax.ShapeDtypeStruct((FOX_HEADS, L, 1), F32)),
        grid=(L // tr,), in_specs=[cat, main, z, zm, mem], out_specs=(main, main, mem, mem, delta),
        compiler_params=_params("parallel"),
    )(dcat, att, proj, proj, o_mem)


_MEM_Q_COL = (2 * MAIN_WIDTH) // HEAD_DIM
_NT = (((1,), (1,)), ((), ()))
_TN = (((0,), (0,)), ((), ()))


def _mem_probs(q_ref, k_ref):
    qs = (q_ref[...] * (HEAD_DIM ** -0.5)).astype(BF16)
    s = lax.dot_general(qs, k_ref[...].astype(BF16), _NT, preferred_element_type=F32)
    e = jnp.exp(s - jnp.max(s, axis=-1, keepdims=True))
    return qs, e / jnp.sum(e, axis=-1, keepdims=True)


def _mem_attn_fwd(proj, kvm, *, tq=512):
    L = proj.shape[0]
    tq = min(tq, L)

    def body(q_ref, k_ref, v_ref, o_ref):
        _, p = _mem_probs(q_ref, k_ref)
        o_ref[...] = jnp.dot(p.astype(BF16), v_ref[...].astype(BF16), preferred_element_type=F32)

    return pl.pallas_call(
        body, name="mem_attn_fwd", out_shape=jax.ShapeDtypeStruct((L, MEM_WIDTH), F32),
        grid=(MEM_HEADS, L // tq),
        in_specs=[pl.BlockSpec((tq, HEAD_DIM), lambda h, i: (i, _MEM_Q_COL + h)),
                  pl.BlockSpec((N_MEM, HEAD_DIM), lambda h, i: (0, h)),
                  pl.BlockSpec((N_MEM, HEAD_DIM), lambda h, i: (0, MEM_HEADS + h))],
        out_specs=pl.BlockSpec((tq, HEAD_DIM), lambda h, i: (i, h)),
        compiler_params=_params("parallel", "parallel"),
    )(proj, kvm, kvm)


def _mem_attn_bwd(proj, kvm, do, *, tq=512):
    L = proj.shape[0]
    tq = min(tq, L)

    def body(q_ref, k_ref, v_ref, do_ref, dq_ref, dk_ref, dv_ref):
        @pl.when(pl.program_id(1) == 0)
        def _():
            dk_ref[...] = jnp.zeros_like(dk_ref)
            dv_ref[...] = jnp.zeros_like(dv_ref)

        qs, p = _mem_probs(q_ref, k_ref)
        dob = do_ref[...].astype(BF16)
        dp = lax.dot_general(dob, v_ref[...].astype(BF16), _NT, preferred_element_type=F32)
        ds = p * (dp - jnp.sum(p * dp, axis=-1, keepdims=True))
        dsb = ds.astype(BF16)
        dq = jnp.dot(dsb, k_ref[...].astype(BF16), preferred_element_type=F32) * (HEAD_DIM ** -0.5)
        dq_ref[...] = dq.astype(BF16)
        dk_ref[...] += lax.dot_general(dsb, qs, _TN, preferred_element_type=F32)
        dv_ref[...] += lax.dot_general(p.astype(BF16), dob, _TN, preferred_element_type=F32)

    dq, dk, dv = pl.pallas_call(
        body, name="mem_attn_bwd",
        out_shape=(jax.ShapeDtypeStruct((L, MEM_WIDTH), BF16),
                   jax.ShapeDtypeStruct((N_MEM, MEM_WIDTH), F32),
                   jax.ShapeDtypeStruct((N_MEM, MEM_WIDTH), F32)),
        grid=(MEM_HEADS, L // tq),
        in_specs=[pl.BlockSpec((tq, HEAD_DIM), lambda h, i: (i, _MEM_Q_COL + h)),
                  pl.BlockSpec((N_MEM, HEAD_DIM), lambda h, i: (0, h)),
                  pl.BlockSpec((N_MEM, HEAD_DIM), lambda h, i: (0, MEM_HEADS + h)),
                  pl.BlockSpec((tq, HEAD_DIM), lambda h, i: (i, h))],
        out_specs=(pl.BlockSpec((tq, HEAD_DIM), lambda h, i: (i, h)),
                   pl.BlockSpec((N_MEM, HEAD_DIM), lambda h, i: (0, h)),
                   pl.BlockSpec((N_MEM, HEAD_DIM), lambda h, i: (0, h))),
        compiler_params=_params("parallel", "arbitrary"),
    )(proj, kvm, kvm, do)
    return dq, jnp.concatenate([dk, dv], axis=1)


def _tile_cumsum(x, row, reverse):
    for sh in (1, 2, 4):
        if reverse:
            x = x + jnp.where(row < SUBLANES - sh, pltpu.roll(x, SUBLANES - sh, 0), 0.0)
        else:
            x = x + jnp.where(row >= sh, pltpu.roll(x, sh, 0), 0.0)
    return x


def _fgate_fwd(pre, b_pad):
    L = pre.shape[0]
    n8 = L // SUBLANES

    def body(p_ref, b_ref, o_ref):
        row = lax.broadcasted_iota(jnp.int32, (SUBLANES, LANES), 0)
        b = b_ref[...]

        def step(i, carry):
            x = p_ref[i] + b
            logf = jnp.minimum(x, 0.0) - jnp.log(1.0 + jnp.exp(-jnp.abs(x)))
            t = _tile_cumsum(logf, row, False) + carry
            o_ref[i] = t
            return t[SUBLANES - 1:SUBLANES, :]

        lax.fori_loop(0, n8, step, jnp.zeros((1, LANES), F32))

    out = pl.pallas_call(
        body, name="fgate_fwd", out_shape=jax.ShapeDtypeStruct((n8, SUBLANES, LANES), F32),
        compiler_params=_params(),
    )(pre.reshape(n8, SUBLANES, LANES), b_pad.reshape(1, LANES))
    return out.reshape(L, LANES)


def _fgate_bwd(dfcum, pre, b_pad):
    L = pre.shape[0]
    n8 = L // SUBLANES

    def body(d_ref, p_ref, b_ref, o_ref, s_ref):
        row = lax.broadcasted_iota(jnp.int32, (SUBLANES, LANES), 0)
        b = b_ref[...]

        def step(k, carry):
            c, acc = carry
            i = n8 - 1 - k
            t = _tile_cumsum(d_ref[i], row, True) + c
            dpre = t * _sigmoid(-(p_ref[i] + b))
            o_ref[i] = dpre
            return t[0:1, :], acc + dpre

        _, acc = lax.fori_loop(0, n8, step, (jnp.zeros((1, LANES), F32), jnp.zeros((SUBLANES, LANES), F32)))
        s_ref[...] = jnp.sum(acc, axis=0, keepdims=True)

    dpre, db = pl.pallas_call(
        body, name="fgate_bwd",
        out_shape=(jax.ShapeDtypeStruct((n8, SUBLANES, LANES), F32), jax.ShapeDtypeStruct((1, LANES), F32)),
        compiler_params=_params(),
    )(dfcum.reshape(n8, SUBLANES, LANES), pre.reshape(n8, SUBLANES, LANES), b_pad.reshape(1, LANES))
    return dpre.reshape(L, LANES), db


FOX_BLOCK = 256


def _fox_scores(qs, k, fq, fk, diagonal):
    s = lax.dot_general(qs, k, _NT, preferred_element_type=F32) + fq - fk
    if diagonal:
        row = lax.broadcasted_iota(jnp.int32, s.shape, 0)
        col = lax.broadcasted_iota(jnp.int32, s.shape, 1)
        s = jnp.where(row >= col, s, NEG_BIG)
    return s


def _fox_specs(tq, L):
    nq = L // tq
    return dict(
        rows=lambda off: pl.BlockSpec((tq, HEAD_DIM), lambda h, i: (i, off + h)),
        seq=lambda off: pl.BlockSpec((L, HEAD_DIM), lambda h, i: (0, off + h)),
        col=pl.BlockSpec((None, None, tq, 1), lambda h, i: (h, i, 0, 0)),
        col_all=pl.BlockSpec((None, nq, tq, 1), lambda h, i: (h, 0, 0, 0)),
        row=pl.BlockSpec((None, None, 1, tq), lambda h, i: (h, i, 0, 0)),
        row_all=pl.BlockSpec((None, nq, 1, tq), lambda h, i: (h, 0, 0, 0)))


def _fox_fwd(proj, kv, fq, fk):
    L = proj.shape[0]
    tq = min(FOX_BLOCK, L)
    nq = L // tq
    sp = _fox_specs(tq, L)

    def body(q_ref, k_ref, v_ref, fq_ref, fk_ref, o_ref, lse_ref, m_s, l_s, acc_s):
        qi = pl.program_id(1)
        qs = (q_ref[...] * (HEAD_DIM ** -0.5)).astype(BF16)
        fq = fq_ref[...]
        m_s[...] = jnp.full_like(m_s, NEG_BIG)
        l_s[...] = jnp.zeros_like(l_s)
        acc_s[...] = jnp.zeros_like(acc_s)

        def block(j, diagonal):
            r0 = pl.multiple_of(j * tq, tq)
            s = _fox_scores(qs, k_ref[pl.ds(r0, tq), :], fq, fk_ref[j], diagonal)
            m_new = jnp.maximum(m_s[...], jnp.max(s, axis=-1, keepdims=True))
            alpha = jnp.exp(m_s[...] - m_new)
            p = jnp.exp(s - m_new)
            l_s[...] = alpha * l_s[...] + jnp.sum(p, axis=-1, keepdims=True)
            acc_s[...] = alpha * acc_s[...] + jnp.dot(p.astype(BF16), v_ref[pl.ds(r0, tq), :],
                                                      preferred_element_type=F32)
            m_s[...] = m_new

        def below(j, carry):
            block(j, False)
            return carry

        lax.fori_loop(0, qi, below, 0)
        block(qi, True)
        o_ref[...] = acc_s[...] / l_s[...]
        lse_ref[...] = m_s[...] + jnp.log(l_s[...])

    return pl.pallas_call(
        body, name="fox_fwd",
        out_shape=(jax.ShapeDtypeStruct((L, MAIN_WIDTH), F32),
                   jax.ShapeDtypeStruct((FOX_HEADS, nq, tq, 1), F32)),
        grid=(FOX_HEADS, nq),
        in_specs=[sp["rows"](0), sp["seq"](0), sp["seq"](FOX_HEADS), sp["col"], sp["row_all"]],
        out_specs=(sp["rows"](0), sp["col"]),
        scratch_shapes=[pltpu.VMEM((tq, 1), F32), pltpu.VMEM((tq, 1), F32), pltpu.VMEM((tq, HEAD_DIM), F32)],
        compiler_params=_params("parallel", "parallel"),
    )(proj, kv, kv, fq, fk)


def _fox_bwd_dq(proj, kv, fq, fk, lse, delta, datt):
    L = proj.shape[0]
    tq = min(FOX_BLOCK, L)
    nq = L // tq
    sp = _fox_specs(tq, L)

    def body(q_ref, k_ref, v_ref, fq_ref, fk_ref, lse_ref, dl_ref, do_ref, dq_ref, df_ref, acc_s, df_s):
        qi = pl.program_id(1)
        qs = (q_ref[...] * (HEAD_DIM ** -0.5)).astype(BF16)
        dob = do_ref[...].astype(BF16)
        fq, lse, dl = fq_ref[...], lse_ref[...], dl_ref[...]
        acc_s[...] = jnp.zeros_like(acc_s)
        df_s[...] = jnp.zeros_like(df_s)

        def block(j, diagonal):
            r0 = pl.multiple_of(j * tq, tq)
            k = k_ref[pl.ds(r0, tq), :]
            p = jnp.exp(_fox_scores(qs, k, fq, fk_ref[j], diagonal) - lse)
            dp = lax.dot_general(dob, v_ref[pl.ds(r0, tq), :], _NT, preferred_element_type=F32)
            ds = p * (dp - dl)
            acc_s[...] += jnp.dot(ds.astype(BF16), k, preferred_element_type=F32)
            df_s[...] += jnp.sum(ds, axis=1, keepdims=True)

        def below(j, carry):
            block(j, False)
            return carry

        lax.fori_loop(0, qi, below, 0)
        block(qi, True)
        dq_ref[...] = (acc_s[...] * (HEAD_DIM ** -0.5)).astype(BF16)
        df_ref[...] = df_s[...]

    return pl.pallas_call(
        body, name="fox_bwd_dq",
        out_shape=(jax.ShapeDtypeStruct((L, MAIN_WIDTH), BF16),
                   jax.ShapeDtypeStruct((FOX_HEADS, nq, tq, 1), F32)),
        grid=(FOX_HEADS, nq),
        in_specs=[sp["rows"](0), sp["seq"](0), sp["seq"](FOX_HEADS), sp["col"], sp["row_all"],
                  sp["col"], sp["col"], sp["rows"](0)],
        out_specs=(sp["rows"](0), sp["col"]),
        scratch_shapes=[pltpu.VMEM((tq, HEAD_DIM), F32), pltpu.VMEM((tq, 1), F32)],
        compiler_params=_params("parallel", "parallel"),
    )(proj, kv, kv, fq, fk, lse, delta, datt)


def _fox_bwd_dkv(proj, kv, fq, fk, lse, delta, datt):
    L = proj.shape[0]
    tq = min(FOX_BLOCK, L)
    nq = L // tq
    sp = _fox_specs(tq, L)

    def body(q_ref, k_ref, v_ref, fq_ref, fk_ref, lse_ref, dl_ref, do_ref,
             dk_ref, dv_ref, df_ref, dk_s, dv_s, df_s):
        ki = pl.program_id(1)
        k, v, fk = k_ref[...], v_ref[...], fk_ref[...]
        dk_s[...] = jnp.zeros_like(dk_s)
        dv_s[...] = jnp.zeros_like(dv_s)
        df_s[...] = jnp.zeros_like(df_s)

        def block(i, diagonal):
            r0 = pl.multiple_of(i * tq, tq)
            qs = (q_ref[pl.ds(r0, tq), :] * (HEAD_DIM ** -0.5)).astype(BF16)
            dob = do_ref[pl.ds(r0, tq), :].astype(BF16)
            p = jnp.exp(_fox_scores(qs, k, fq_ref[i], fk, diagonal) - lse_ref[i])
            dp = lax.dot_general(dob, v, _NT, preferred_element_type=F32)
            ds = p * (dp - dl_ref[i])
            dv_s[...] += lax.dot_general(p.astype(BF16), dob, _TN, preferred_element_type=F32)
            dk_s[...] += lax.dot_general(ds.astype(BF16), qs, _TN, preferred_element_type=F32)
            df_s[...] -= jnp.sum(ds, axis=0, keepdims=True)

        def above(i, carry):
            block(i, False)
            return carry

        block(ki, True)
        lax.fori_loop(ki + 1, nq, above, 0)
        dk_ref[...] = dk_s[...].astype(BF16)
        dv_ref[...] = dv_s[...].astype(BF16)
        df_ref[...] = df_s[...]

    return pl.pallas_call(
        body, name="fox_bwd_dkv",
        out_shape=(jax.ShapeDtypeStruct((L, MAIN_WIDTH), BF16),
                   jax.ShapeDtypeStruct((L, MAIN_WIDTH), BF16),
                   jax.ShapeDtypeStruct((FOX_HEADS, nq, 1, tq), F32)),
        grid=(FOX_HEADS, nq),
        in_specs=[sp["seq"](0), sp["rows"](0), sp["rows"](FOX_HEADS), sp["col_all"], sp["row"],
                  sp["col_all"], sp["col_all"], sp["seq"](0)],
        out_specs=(sp["rows"](0), sp["rows"](0), sp["row"]),
        scratch_shapes=[pltpu.VMEM((tq, HEAD_DIM), F32), pltpu.VMEM((tq, HEAD_DIM), F32),
                        pltpu.VMEM((1, tq), F32)],
        compiler_params=_params("parallel", "parallel"),
    )(proj, kv, kv, fq, fk, lse, delta, datt)


def _pad_lanes(a):
    return jnp.pad(a, ((0, 0), (0, LANES - a.shape[1])))


def _mem_branch_fwd(mem, g, w_mk, proj, tag):
    memn = _rmsnorm_fwd(mem, g, name="mem_norm_" + tag, out_dtype=BF16)
    kvm = _mm(memn, w_mk, name="mem_kv_" + tag)
    return memn, kvm, _mem_attn_fwd(proj, kvm)


def _mem_branch_bwd(mem, g, w_mk, proj, memn, kvm, do_mem, tag):
    dqm, dkvm = _mem_attn_bwd(proj, kvm, do_mem)
    dkvm = dkvm.astype(BF16)
    dw_mk = _mm(memn, dkvm, ta=True, name="dw_mem_kv_" + tag, out_dtype=BF16)
    dmemn = _mm(dkvm, w_mk, tb=True, name="dmemn_" + tag)
    _, dg = _rmsnorm_bwd(mem, g, dmemn, name="mem_norm_bwd_" + tag, dx_dtype=BF16)
    return dqm, dw_mk, dg


def _local_step(x, mem, target, w):
    L = x.shape[0]
    g = {}

    b_re_t = jnp.transpose(w["b_re"], (0, 2, 1))
    b_im_t = jnp.transpose(w["b_im"], (0, 2, 1))
    ar, ai, bbr_t, bbi_t = _s5_prep(w["lam_re"], w["lam_im"], w["log_step"], b_re_t, b_im_t)
    bmat, cmat = _s5_block_mats(bbr_t, bbi_t, w["c_re"], w["c_im"])
    a_rows = _s5_a_rows(ar, ai)

    hn0 = _rmsnorm_fwd(x, w["pre_norm_g"][0], name="pre_norm_0", out_dtype=BF16)
    proj_a = _mm(hn0, w["w_in_a"], name="in_proj_a")
    y, yg, xp = _s5_fwd(proj_a, bmat, cmat, a_rows, w["d_skip"])
    t = _mm(yg, w["w_glu"], name="glu_proj")
    memn0, kvm0, om0 = _mem_branch_fwd(mem, w["mem_norm_g"][0], w["w_mem_kv"][0], proj_a, "0")
    cat0 = _gate_a_fwd(y, t, w["b_glu"], proj_a, om0)
    o0 = _mm(cat0, w["w_out"][0], name="out_proj_0")
    h1 = _rmsnorm_fwd(o0, w["post_norm_g"][0], res=x, name="post_norm_0")

    kv_in = _rmsnorm_fwd(h1, w["kv_norm_g"], name="kv_norm", out_dtype=BF16)
    kv = _mm(kv_in, w["w_kv"], name="kv_proj", out_dtype=BF16)
    pre_f = _mm(kv_in, w["w_fgate"], name="fgate_proj")
    b_f = jnp.pad(w["b_fgate"], (0, LANES - FOX_HEADS))
    fcum = _fgate_fwd(pre_f, b_f)
    fc = jnp.transpose(fcum[:, :FOX_HEADS])
    tq = min(FOX_BLOCK, L)
    fq, fk = fc.reshape(FOX_HEADS, L // tq, tq, 1), fc.reshape(FOX_HEADS, L // tq, 1, tq)

    hn1 = _rmsnorm_fwd(h1, w["pre_norm_g"][1], name="pre_norm_1", out_dtype=BF16)
    proj_b = _mm(hn1, w["w_in_b"], name="in_proj_b")
    att, lse = _fox_fwd(proj_b, kv, fq, fk)
    memn1, kvm1, om1 = _mem_branch_fwd(mem, w["mem_norm_g"][1], w["w_mem_kv"][1], proj_b, "1")
    cat1 = _gate_b_fwd(att, proj_b, om1)
    o1 = _mm(cat1, w["w_out"][1], name="out_proj_1")
    h2 = _rmsnorm_fwd(o1, w["post_norm_g"][1], res=h1, name="post_norm_1")

    dh2, loss_row = _loss_head(h2, target)

    do1, dpost1 = _rmsnorm_bwd(o1, w["post_norm_g"][1], dh2, name="post_norm_bwd_1", dx_dtype=BF16)
    dcat1 = _mm(do1, w["w_out"][1], tb=True, name="dcat_1")
    g["w_out_1"] = _mm(cat1, do1, ta=True, name="dw_out_1", out_dtype=BF16)
    datt, dz1, dom1, dzm1, delta = _gate_b_bwd(dcat1, att, proj_b, om1)
    dqm1, g["w_mem_kv_1"], dmemg1 = _mem_branch_bwd(mem, w["mem_norm_g"][1], w["w_mem_kv"][1], proj_b,
                                                   memn1, kvm1, dom1, "1")
    delta = delta.reshape(fq.shape)
    dq, dfq = _fox_bwd_dq(proj_b, kv, fq, fk, lse, delta, datt)
    dk, dv, dfk = _fox_bwd_dkv(proj_b, kv, fq, fk, lse, delta, datt)
    dproj_b = jnp.concatenate([dq, dz1, dqm1, dzm1], axis=1)
    g["w_in_b"] = _mm(hn1, dproj_b, ta=True, name="dw_in_b", out_dtype=BF16, shards=N_CHIPS)
    dhn1 = _mm(dproj_b, w["w_in_b"], tb=True, name="dhn_1")

    dkv = jnp.concatenate([dk, dv], axis=1)
    g["w_kv"] = _mm(kv_in, dkv, ta=True, name="dw_kv", out_dtype=BF16, shards=N_CHIPS)
    dkv_in_a = _mm(dkv, w["w_kv"], tb=True, name="dkv_in_kv")
    dfcum = _pad_lanes(jnp.transpose(dfq.reshape(FOX_HEADS, L) + dfk.reshape(FOX_HEADS, L)))
    dpre_f, db_f = _fgate_bwd(dfcum, pre_f, b_f)
    g["b_fgate"] = db_f[0, :FOX_HEADS]
    g["w_fgate"] = _mm(kv_in, dpre_f, ta=True, name="dw_fgate")[:, :FOX_HEADS]
    dkv_in_b = _mm(dpre_f, w["w_fgate"], tb=True, name="dkv_in_fgate")
    dh1_kv, g["kv_norm_g"] = _rmsnorm_bwd(h1, w["kv_norm_g"], (dkv_in_a, dkv_in_b), name="kv_norm_bwd")
    dh1, dpre1 = _rmsnorm_bwd(h1, w["pre_norm_g"][1], dhn1, adds=(dh2, dh1_kv), name="pre_norm_bwd_1")

    do0, dpost0 = _rmsnorm_bwd(o0, w["post_norm_g"][0], dh1, name="post_norm_bwd_0", dx_dtype=BF16)
    dcat0 = _mm(do0, w["w_out"][0], tb=True, name="dcat_0")
    g["w_out_0"] = _mm(cat0, do0, ta=True, name="dw_out_0", out_dtype=BF16)
    dz0, dzm0, dt, dyg_a, dom0, db_glu = _gate_a_bwd(dcat0, y, t, w["b_glu"], proj_a, om0)
    g["b_glu"] = db_glu[0]
    g["w_glu"] = _mm(yg, dt, ta=True, name="dw_glu", out_dtype=BF16)
    dyg_b = _mm(dt, w["w_glu"], tb=True, name="dyg")
    dy = _gelu_bwd(dyg_a, dyg_b, y)
    du, db_blk, dc_blk, da_rows, dd_skip = _s5_bwd(proj_a, dy, xp, bmat, cmat, a_rows, w["d_skip"])
    g["d_skip"] = dd_skip[0]
    dqm0, g["w_mem_kv_0"], dmemg0 = _mem_branch_bwd(mem, w["mem_norm_g"][0], w["w_mem_kv"][0], proj_a,
                                                   memn0, kvm0, dom0, "0")
    dproj_a = jnp.concatenate([du.astype(BF16), dz0, dqm0, dzm0], axis=1)
    g["w_in_a"] = _mm(hn0, dproj_a, ta=True, name="dw_in_a", out_dtype=BF16, shards=N_CHIPS)
    dhn0 = _mm(dproj_a, w["w_in_a"], tb=True, name="dhn_0")
    grad_x, dpre0 = _rmsnorm_bwd(x, w["pre_norm_g"][0], dhn0, adds=(dh1,), name="pre_norm_bwd_0")

    dbb = _s5_block_diag(db_blk)
    dcc = _s5_block_diag(dc_blk)
    g["c_re"], g["c_im"] = dcc[0], -dcc[1]
    d_ar = da_rows[:, 0, :STATE_COLS].reshape(SSM_GROUPS, SSM_STATE)
    d_ai = da_rows[:, 0, STATE_COLS:].reshape(SSM_GROUPS, SSM_STATE)
    dlr, dli, dls, dbr_t, dbi_t = _s5_prep_bwd(w["lam_re"], w["lam_im"], w["log_step"], b_re_t, b_im_t,
                                               d_ar, d_ai, dbb[0], dbb[1])
    g["lam_re"], g["lam_im"], g["log_step"] = dlr, dli, dls[:, 0]
    g["b_re"] = jnp.transpose(dbr_t, (0, 2, 1))
    g["b_im"] = jnp.transpose(dbi_t, (0, 2, 1))
    g["pre_norm_g"] = jnp.stack([dpre0, dpre1])
    g["post_norm_g"] = jnp.stack([dpost0, dpost1])
    g["mem_norm_g"] = jnp.stack([dmemg0, dmemg1])
    return loss_row, grad_x, g


_MESH = pl.DeviceIdType.MESH
_ANY = pl.BlockSpec(memory_space=pl.ANY)


def _place():
    x, y, c = lax.axis_index("x"), lax.axis_index("y"), lax.axis_index("c")
    chips = [(1 - x, y), (x, 1 - y), (1 - x, 1 - y)]
    return x, y, c, chips


def _all_gather_chips(parts):
    n = len(parts)

    def body(*refs):
        ins, outs = refs[:n], refs[n:2 * n]
        ici_send, ici_recv, d2d_send, d2d_recv = refs[2 * n:]
        x, y, c, chips = _place()
        me = 2 * x + y
        sib = (x, y, 1 - c)

        def ici(i, k, src_chip_j, dst):
            return pltpu.make_async_remote_copy(
                src_ref=ins[i].at[c] if src_chip_j is None else outs[i].at[src_chip_j, c],
                dst_ref=outs[i].at[me if src_chip_j is None else src_chip_j, c],
                send_sem=ici_send.at[i * 3 + k], recv_sem=ici_recv.at[i * 3 + k],
                device_id=dst, device_id_type=_MESH)

        def d2d(i, k, chip_j, half):
            return pltpu.make_async_remote_copy(
                src_ref=outs[i].at[chip_j, half], dst_ref=outs[i].at[chip_j, half],
                send_sem=d2d_send.at[i * 3 + k], recv_sem=d2d_recv.at[i * 3 + k],
                device_id=sib, device_id_type=_MESH)

        sends = [ici(i, k, None, (*chips[k], c)) for i in range(n) for k in range(3)]
        for cp in sends:
            cp.start()
        passed = []
        for k, (cx, cy) in enumerate(chips):
            for i in range(n):
                ici(i, k, 2 * cx + cy, (x, y, c)).wait_recv()
                fwd = d2d(i, k, 2 * cx + cy, c)
                fwd.start()
                passed.append(fwd)
        for k, (cx, cy) in enumerate(chips):
            for i in range(n):
                d2d(i, k, 2 * cx + cy, 1 - c).wait_recv()
        for cp in sends + passed:
            cp.wait_send()

    return pl.pallas_call(
        body, name="all_gather_weights",
        out_shape=[jax.ShapeDtypeStruct((N_CHIPS,) + p.shape, p.dtype) for p in parts],
        in_specs=[_ANY] * n, out_specs=[_ANY] * n,
        scratch_shapes=[pltpu.SemaphoreType.DMA((3 * n,)), pltpu.SemaphoreType.DMA((3 * n,)),
                        pltpu.SemaphoreType.DMA((3 * n,)), pltpu.SemaphoreType.DMA((3 * n,))],
    )(*parts)


def _swap_halves(grads):
    n = len(grads)

    def body(*refs):
        ins, outs = refs[:n], refs[n:2 * n]
        send_sem, recv_sem = refs[2 * n:]
        x, y, c, _ = _place()
        copies = [pltpu.make_async_remote_copy(
            src_ref=ins[i].at[:, 1 - c], dst_ref=outs[i],
            send_sem=send_sem.at[i], recv_sem=recv_sem.at[i],
            device_id=(x, y, 1 - c), device_id_type=_MESH) for i in range(n)]
        for cp in copies:
            cp.start()
        for cp in copies:
            cp.wait()

    return pl.pallas_call(
        body, name="grad_swap_halves",
        out_shape=[jax.ShapeDtypeStruct((N_CHIPS,) + g.shape[2:], g.dtype) for g in grads],
        in_specs=[_ANY] * n, out_specs=[_ANY] * n,
        scratch_shapes=[pltpu.SemaphoreType.DMA((n,)), pltpu.SemaphoreType.DMA((n,))],
    )(*grads)


def _pair_sum(g, r, c_idx, *, name):
    _, _, h, C = g.shape
    tr = min(h, 256)

    def body(c_ref, g_ref, r_ref, o_ref):
        o_ref[...] = (g_ref[...].astype(F32) + r_ref[...].astype(F32)).astype(o_ref.dtype)

    return pl.pallas_call(
        body, name=name, out_shape=jax.ShapeDtypeStruct((N_CHIPS, h, C), g.dtype),
        grid_spec=pltpu.PrefetchScalarGridSpec(
            num_scalar_prefetch=1, grid=(N_CHIPS, h // tr),
            in_specs=[pl.BlockSpec((None, None, tr, C), lambda j, i, s: (j, s[0], i, 0)),
                      pl.BlockSpec((None, tr, C), lambda j, i, s: (j, i, 0))],
            out_specs=pl.BlockSpec((None, tr, C), lambda j, i, s: (j, i, 0))),
        compiler_params=_params("parallel", "parallel"),
    )(c_idx, g, r)


def _send_to_owners(sums):
    n = len(sums)

    def body(*refs):
        ins, outs = refs[:n], refs[n:2 * n]
        send_sem, recv_sem = refs[2 * n:]
        x, y, c, chips = _place()
        copies = [pltpu.make_async_remote_copy(
            src_ref=ins[i].at[2 * cx + cy], dst_ref=outs[i].at[k],
            send_sem=send_sem.at[i * 3 + k], recv_sem=recv_sem.at[i * 3 + k],
            device_id=(cx, cy, c), device_id_type=_MESH)
            for i in range(n) for k, (cx, cy) in enumerate(chips)]
        for cp in copies:
            cp.start()
        for cp in copies:
            cp.wait()

    return pl.pallas_call(
        body, name="grad_send_to_owners",
        out_shape=[jax.ShapeDtypeStruct((3,) + s.shape[1:], s.dtype) for s in sums],
        in_specs=[_ANY] * n, out_specs=[_ANY] * n,
        scratch_shapes=[pltpu.SemaphoreType.DMA((3 * n,)), pltpu.SemaphoreType.DMA((3 * n,))],
    )(*sums)


def _owner_sum(s, r, jc_idx, *, name):
    _, h, C = s.shape
    tr = min(h, 256)

    def body(jc_ref, s_ref, r_ref, o_ref):
        acc = s_ref[...].astype(F32)
        for k in range(3):
            acc = acc + r_ref[k].astype(F32)
        o_ref[...] = acc

    return pl.pallas_call(
        body, name=name, out_shape=jax.ShapeDtypeStruct((2, h, C), F32),
        grid_spec=pltpu.PrefetchScalarGridSpec(
            num_scalar_prefetch=1, grid=(h // tr,),
            in_specs=[pl.BlockSpec((None, tr, C), lambda i, s: (s[0], i, 0)),
                      pl.BlockSpec((3, tr, C), lambda i, s: (0, i, 0))],
            out_specs=pl.BlockSpec((None, tr, C), lambda i, s: (s[1], i, 0))),
        compiler_params=_params("parallel"),
    )(jc_idx, s, r)


def _share_with_sibling(bufs):
    n = len(bufs)

    def body(*refs):
        ins, outs = refs[:n], refs[n:2 * n]
        send_sem, recv_sem = refs[2 * n:]
        x, y, c, _ = _place()

        def copy(i, half):
            return pltpu.make_async_remote_copy(
                src_ref=ins[i].at[half], dst_ref=outs[i].at[half],
                send_sem=send_sem.at[i], recv_sem=recv_sem.at[i],
                device_id=(x, y, 1 - c), device_id_type=_MESH)

        copies = [copy(i, c) for i in range(n)]
        for cp in copies:
            cp.start()
        for i in range(n):
            copy(i, 1 - c).wait_recv()
        for cp in copies:
            cp.wait_send()

    return pl.pallas_call(
        body, name="grad_share_with_sibling",
        out_shape=[jax.ShapeDtypeStruct(b.shape, b.dtype) for b in bufs],
        in_specs=[_ANY] * n, out_specs=[_ANY] * n,
        input_output_aliases={i: i for i in range(n)},
        scratch_shapes=[pltpu.SemaphoreType.DMA((n,)), pltpu.SemaphoreType.DMA((n,))],
    )(*bufs)


def _reduce_scatter_chips(grads, c_idx, jc_idx):
    views = [g.reshape(N_CHIPS, 2, g.shape[1] // 2, g.shape[2]) for g in grads]
    arrived = _swap_halves(views)
    sums = [_pair_sum(v, r, c_idx, name=f"grad_pair_sum_{i}") for i, (v, r) in enumerate(zip(views, arrived))]
    arrived = _send_to_owners(sums)
    halves = [_owner_sum(s, r, jc_idx, name=f"grad_owner_sum_{i}") for i, (s, r) in enumerate(zip(sums, arrived))]
    full = _share_with_sibling(halves)
    return [f.reshape(g.shape[1], g.shape[2]) for f, g in zip(full, grads)]


def _all_reduce_small(buf):
    R = buf.shape[0]
    n_dev = 2 * N_CHIPS

    def body(x_ref, o_ref, all_ref, send_sems, recv_sems, local_sem):
        x, y, c, chips = _place()
        me, sib = (x, y, c), (x, y, 1 - c)

        def rows(px, py, pc):
            return all_ref.at[4 * px + 2 * py + pc]

        def copy(k, block, to, src=None):
            return pltpu.make_async_remote_copy(
                src_ref=rows(*block) if src is None else src, dst_ref=rows(*block),
                send_sem=send_sems.at[k], recv_sem=recv_sems.at[k], device_id=to, device_id_type=_MESH)

        mine = pltpu.make_async_copy(x_ref, rows(*me), local_sem)
        mine.start()
        first = [copy(0, me, sib, src=x_ref)]
        first += [copy(1 + j, me, (*chip, c), src=x_ref) for j, chip in enumerate(chips)]
        for cp in first:
            cp.start()
        passed = [copy(4 + j, (*chip, c), sib) for j, chip in enumerate(chips)]
        for j, chip in enumerate(chips):
            copy(1 + j, (*chip, c), me).wait_recv()
            passed[j].start()
        copy(0, sib, me).wait_recv()
        for j, chip in enumerate(chips):
            copy(4 + j, (*chip, 1 - c), me).wait_recv()
        for cp in first + passed:
            cp.wait_send()
        mine.wait()
        acc = all_ref[0]
        for d in range(1, n_dev):
            acc = acc + all_ref[d]
        o_ref[...] = acc

    vmem = pl.BlockSpec(memory_space=pltpu.VMEM)
    return pl.pallas_call(
        body, name="all_reduce_small", out_shape=jax.ShapeDtypeStruct((R, LANES), F32),
        in_specs=[vmem], out_specs=vmem,
        scratch_shapes=[pltpu.VMEM((n_dev, R, LANES), F32),
                        pltpu.SemaphoreType.DMA((7,)), pltpu.SemaphoreType.DMA((7,)), pltpu.SemaphoreType.DMA],
        compiler_params=_params(),
    )(buf)


def _adamw(w, g, m, v, *, name):
    R, C = w.shape
    tr = next(c for c in (256, 192, 128, 64, 32, 16, 8, R) if R % c == 0)

    def body(w_ref, g_ref, m_ref, v_ref, d_ref, nm_ref, nv_ref):
        g = g_ref[...]
        m = ADAM_B1 * m_ref[...] + (1.0 - ADAM_B1) * g
        v = ADAM_B2 * v_ref[...] + (1.0 - ADAM_B2) * (g * g)
        nm_ref[...] = m
        nv_ref[...] = v
        m_hat = m / (1.0 - ADAM_B1 ** ADAM_STEP)
        v_hat = v / (1.0 - ADAM_B2 ** ADAM_STEP)
        d_ref[...] = -ADAM_LR * (m_hat / (jnp.sqrt(v_hat) + ADAM_EPS) + ADAM_WD * w_ref[...])

    blk = pl.BlockSpec((tr, C), lambda i: (i, 0))
    sds = jax.ShapeDtypeStruct((R, C), F32)
    return pl.pallas_call(
        body, name=name, out_shape=(sds, sds, sds), grid=(R // tr,),
        in_specs=[blk] * 4, out_specs=(blk, blk, blk),
        compiler_params=_params("parallel"),
    )(w, g, m, v)


_TILE = SUBLANES * LANES


def _pack(arrays):
    rows = []
    for a in arrays:
        flat = a.reshape(-1)
        flat = jnp.pad(flat, (0, (-flat.shape[0]) % _TILE))
        rows.append(flat.reshape(-1, LANES))
    return jnp.concatenate(rows, axis=0)


def _unpack(buf, shapes):
    out, r = [], 0
    for s in shapes:
        size = math.prod(s)
        nr = -(-size // _TILE) * SUBLANES
        out.append(buf[r:r + nr].reshape(-1)[:size].reshape(s))
        r += nr
    return out


_BIG = ("w_in_a", "w_glu", "w_kv", "w_in_b", "w_mem_kv", "w_out")
_REPLICATED = ("pre_norm_g", "post_norm_g", "lam_re", "lam_im", "log_step", "b_re", "b_im", "c_re", "c_im",
               "kv_norm_g", "b_fgate", "mem_norm_g")
_SHARDED_SMALL = ("d_skip", "b_glu", "w_fgate")
_WEIGHTS = ("pre_norm_g", "post_norm_g", "w_in_a", "lam_re", "lam_im", "log_step", "b_re", "b_im", "c_re",
            "c_im", "d_skip", "w_glu", "b_glu", "kv_norm_g", "w_kv", "w_fgate", "b_fgate", "w_in_b",
            "mem_norm_g", "w_mem_kv", "w_out")


def _halves(a):
    return a.reshape(2, a.shape[0] // 2, a.shape[1])


def _unhalve(a):
    return a.reshape(N_CHIPS, 2 * a.shape[2], a.shape[3])


def _columns(a):
    return jnp.transpose(a, (1, 0, 2)).reshape(a.shape[1], N_CHIPS * a.shape[2])


def kernel(x, mem, pre_norm_g, post_norm_g, w_in_a, lam_re, lam_im, log_step, b_re, b_im, c_re, c_im, d_skip, w_glu, b_glu, kv_norm_g, w_kv, w_fgate, b_fgate, w_in_b, mem_norm_g, w_mem_kv, w_out, loss_target, m_pre_norm_g, m_post_norm_g, m_w_in_a, m_lam_re, m_lam_im, m_log_step, m_b_re, m_b_im, m_c_re, m_c_im, m_d_skip, m_w_glu, m_b_glu, m_kv_norm_g, m_w_kv, m_w_fgate, m_b_fgate, m_w_in_b, m_mem_norm_g, m_w_mem_kv, m_w_out, v_pre_norm_g, v_post_norm_g, v_w_in_a, v_lam_re, v_lam_im, v_log_step, v_b_re, v_b_im, v_c_re, v_c_im, v_d_skip, v_w_glu, v_b_glu, v_kv_norm_g, v_w_kv, v_w_fgate, v_b_fgate, v_w_in_b, v_mem_norm_g, v_w_mem_kv, v_w_out):
    a = dict(locals())
    xi, yi, ci = lax.axis_index("x"), lax.axis_index("y"), lax.axis_index("c")
    chip = 2 * xi + yi
    c_idx = jnp.reshape(ci, (1,)).astype(jnp.int32)
    jc_idx = jnp.stack([chip, ci]).astype(jnp.int32)

    vec = jnp.zeros((2 * SUBLANES, MAIN_WIDTH // N_CHIPS), F32)
    vec = vec.at[0].set(a["d_skip"][0]).at[1].set(a["b_glu"][0])
    parts = [a["w_in_a"][0].astype(BF16), a["w_glu"][0].astype(BF16), a["w_kv"].astype(BF16),
             _pad_lanes(a["w_fgate"]).astype(BF16), a["w_in_b"][0].astype(BF16),
             a["w_mem_kv"].reshape(-1, a["w_mem_kv"].shape[2]).astype(BF16),
             a["w_out"].reshape(-1, D_MODEL).astype(BF16), vec]
    parts = [_halves(p) for p in parts]
    gat = [lax.dynamic_update_index_in_dim(g, p, chip, 0) for g, p in zip(_all_gather_chips(parts), parts)]
    w_in_a, w_glu, w_kv, w_fg, w_in_b, w_mk, w_out, vecs = gat
    w = dict(
        w_in_a=_columns(_unhalve(w_in_a)), w_glu=w_glu.reshape(MAIN_WIDTH, MAIN_WIDTH),
        w_kv=_columns(_unhalve(w_kv)), w_fgate=w_fg.reshape(D_MODEL, LANES), w_in_b=_columns(_unhalve(w_in_b)),
        w_mem_kv=jnp.transpose(w_mk, (1, 0, 2, 3)).reshape(2, D_MODEL, 2 * MEM_WIDTH),
        w_out=jnp.transpose(w_out, (1, 0, 2, 3)).reshape(2, D_MODEL, D_MODEL),
        d_skip=vecs[:, 0, 0, :].reshape(MAIN_WIDTH), b_glu=vecs[:, 0, 1, :].reshape(MAIN_WIDTH),
        pre_norm_g=a["pre_norm_g"], post_norm_g=a["post_norm_g"], mem_norm_g=a["mem_norm_g"],
        kv_norm_g=a["kv_norm_g"], b_fgate=a["b_fgate"],
        lam_re=a["lam_re"][0], lam_im=a["lam_im"][0], log_step=a["log_step"][0],
        b_re=a["b_re"][0], b_im=a["b_im"][0], c_re=a["c_re"][0], c_im=a["c_im"][0])

    loss_row, grad_x, g = _local_step(a["x"][0], a["mem"][0], a["loss_target"][0], w)
    loss = lax.psum(jnp.sum(loss_row), MESH_AXES)

    big = [g["w_in_a"], g["w_glu"].reshape(N_CHIPS, -1, MAIN_WIDTH), g["w_kv"], g["w_in_b"],
           g["w_mem_kv_0"].reshape(N_CHIPS, -1, 2 * MEM_WIDTH), g["w_mem_kv_1"].reshape(N_CHIPS, -1, 2 * MEM_WIDTH),
           g["w_out_0"].reshape(N_CHIPS, -1, D_MODEL), g["w_out_1"].reshape(N_CHIPS, -1, D_MODEL)]
    r_in_a, r_glu, r_kv, r_in_b, r_mk0, r_mk1, r_out0, r_out1 = _reduce_scatter_chips(big, c_idx, jc_idx)
    grads = {"w_in_a": r_in_a[None], "w_glu": r_glu[None], "w_kv": r_kv, "w_in_b": r_in_b[None],
             "w_mem_kv": jnp.stack([r_mk0, r_mk1]), "w_out": jnp.stack([r_out0, r_out1])}

    small_names = _REPLICATED + _SHARDED_SMALL
    small = _all_reduce_small(_pack([g[n] for n in small_names]))
    small = dict(zip(small_names, _unpack(small, [g[n].shape for n in small_names])))
    for n in _REPLICATED:
        grads[n] = small[n].reshape(a[n].shape)
    nd = MAIN_WIDTH // N_CHIPS
    grads["d_skip"] = lax.dynamic_slice(small["d_skip"], (chip * nd,), (nd,))[None]
    grads["b_glu"] = lax.dynamic_slice(small["b_glu"], (chip * nd,), (nd,))[None]
    nf = D_MODEL // N_CHIPS
    grads["w_fgate"] = lax.dynamic_slice(small["w_fgate"], (chip * nf, 0), (nf, FOX_HEADS))

    delta, new_m, new_v = {}, {}, {}
    for n in _BIG:
        shape = a[n].shape
        d2 = (-1, shape[-1])
        d, m, v = _adamw(a[n].reshape(d2), grads[n].reshape(d2), a["m_" + n].reshape(d2),
                         a["v_" + n].reshape(d2), name="adamw_" + n)
        delta[n], new_m[n], new_v[n] = d.reshape(shape), m.reshape(shape), v.reshape(shape)
    shapes = [a[n].shape for n in small_names]
    d, m, v = _adamw(_pack([a[n] for n in small_names]), _pack([grads[n] for n in small_names]),
                     _pack([a["m_" + n] for n in small_names]), _pack([a["v_" + n] for n in small_names]),
                     name="adamw_small")
    for n, dd, mm, vv in zip(small_names, _unpack(d, shapes), _unpack(m, shapes), _unpack(v, shapes)):
        delta[n], new_m[n], new_v[n] = dd, mm, vv

    return (loss, grad_x[None], *[grads[n] for n in _WEIGHTS], *[delta[n] for n in _WEIGHTS],
            *[new_m[n] for n in _WEIGHTS], *[new_v[n] for n in _WEIGHTS])
```

```python
import functools
import math

import jax
import jax.numpy as jnp
from jax import lax
from jax.experimental import pallas as pl
from jax.experimental.pallas import tpu as pltpu

F32 = jnp.float32
BF16 = jnp.bfloat16

D_MODEL = 2048
N_MEM = 256
MAIN_WIDTH = 1536
MEM_WIDTH = 512
IN_WIDTH = 2 * MAIN_WIDTH + 2 * MEM_WIDTH
HEAD_DIM = 128
FOX_HEADS = MAIN_WIDTH // HEAD_DIM
MEM_HEADS = MEM_WIDTH // HEAD_DIM
SSM_GROUP = 16
SSM_GROUPS = MAIN_WIDTH // SSM_GROUP
SSM_STATE = 64
GROUPS_PER_BLOCK = 8
SSM_BLOCKS = SSM_GROUPS // GROUPS_PER_BLOCK
STATE_COLS = GROUPS_PER_BLOCK * SSM_STATE
EPS = 1e-6
ADAM_LR = 0.001
ADAM_B1 = 0.9
ADAM_B2 = 0.999
ADAM_EPS = 1e-08
ADAM_WD = 0.01
ADAM_STEP = 10
N_CHIPS = 4
LANES = 128
SUBLANES = 8
VMEM_LIMIT_BYTES = 56 * 1024 * 1024
NEG_BIG = -1e30
MESH_AXES = ("x", "y", "c")


def _params(*sem):
    return pltpu.CompilerParams(dimension_semantics=sem if sem else None,
                                vmem_limit_bytes=VMEM_LIMIT_BYTES)


def _sigmoid(x):
    return 1.0 / (1.0 + jnp.exp(-x))


def _gelu(x):
    c = math.sqrt(2.0 / math.pi)
    return 0.5 * x * (1.0 + jnp.tanh(c * (x + 0.044715 * (x * x * x))))


def _gelu_grad(x):
    c = math.sqrt(2.0 / math.pi)
    t = jnp.tanh(c * (x + 0.044715 * (x * x * x)))
    return 0.5 * (1.0 + t) + 0.5 * x * (1.0 - t * t) * (c * (1.0 + 3.0 * 0.044715 * (x * x)))


def _silu_and_grad(z):
    s = _sigmoid(z)
    return z * s, s * (1.0 + z * (1.0 - s))


_TILE_CHOICES = (2048, 1024, 768, 512, 384, 256, LANES)


def _tile(n, cap):
    return next(c for c in _TILE_CHOICES if c <= cap and n % c == 0)


def _mm(a, b, *, name, ta=False, tb=False, out_dtype=F32, shards=1, tm=1024, tn=1024, tk=2048):
    if ta:
        K, M = a.shape
    else:
        M, K = a.shape
    if tb:
        N, kb = b.shape
    else:
        kb, N = b.shape
    assert K == kb, (a.shape, b.shape)
    ns = N // shards
    tm, tn, tk = _tile(M, tm), _tile(ns, tn), _tile(K, tk)
    assert M % tm == 0 and ns % tn == 0 and K % tk == 0 and N % shards == 0
    nk = K // tk
    dn = (((0 if ta else 1,), (1 if tb else 0,)), ((), ()))

    def body(a_ref, b_ref, o_ref, acc_ref):
        k = pl.program_id(2)

        @pl.when(k == 0)
        def _():
            acc_ref[...] = jnp.zeros_like(acc_ref)

        acc_ref[...] += lax.dot_general(a_ref[...].astype(BF16), b_ref[...].astype(BF16), dn,
                                        preferred_element_type=F32)

        @pl.when(k == nk - 1)
        def _():
            o_ref[...] = acc_ref[...].astype(o_ref.dtype)

    a_spec = (pl.BlockSpec((tk, tm), lambda i, j, k: (k, i)) if ta
              else pl.BlockSpec((tm, tk), lambda i, j, k: (i, k)))
    b_spec = (pl.BlockSpec((tn, tk), lambda i, j, k: (j, k)) if tb
              else pl.BlockSpec((tk, tn), lambda i, j, k: (k, j)))
    if shards == 1:
        out_shape = jax.ShapeDtypeStruct((M, N), out_dtype)
        o_spec = pl.BlockSpec((tm, tn), lambda i, j, k: (i, j))
    else:
        nb = ns // tn
        out_shape = jax.ShapeDtypeStruct((shards, M, ns), out_dtype)
        o_spec = pl.BlockSpec((None, tm, tn), lambda i, j, k: (j // nb, i, j % nb))
    return pl.pallas_call(
        body, name=name, out_shape=out_shape,
        grid=(M // tm, N // tn, nk),
        in_specs=[a_spec, b_spec], out_specs=o_spec,
        scratch_shapes=[pltpu.VMEM((tm, tn), F32)],
        compiler_params=_params("parallel", "parallel", "arbitrary"),
    )(a, b)


def _rmsnorm_fwd(x, g, *, name, res=None, out_dtype=F32, tr=256):
    L, D = x.shape
    tr = min(tr, L)
    has_res = res is not None

    def body(*refs):
        if has_res:
            x_ref, g_ref, r_ref, o_ref = refs
        else:
            x_ref, g_ref, o_ref = refs
        xf = x_ref[...]
        r = lax.rsqrt(jnp.mean(xf * xf, axis=-1, keepdims=True) + EPS)
        y = xf * r * g_ref[...]
        if has_res:
            y = r_ref[...] + y
        o_ref[...] = y.astype(o_ref.dtype)

    row = pl.BlockSpec((tr, D), lambda i: (i, 0))
    vec = pl.BlockSpec((1, D), lambda i: (0, 0))
    ins = [x, g.reshape(1, D)] + ([res] if has_res else [])
    return pl.pallas_call(
        body, name=name, out_shape=jax.ShapeDtypeStruct((L, D), out_dtype),
        grid=(L // tr,), in_specs=[row, vec] + ([row] if has_res else []), out_specs=row,
        compiler_params=_params("parallel"),
    )(*ins)


def _rmsnorm_bwd(x, g, dy, *, name, adds=(), dx_dtype=F32, tr=256):
    L, D = x.shape
    tr = min(tr, L)
    dys = dy if isinstance(dy, tuple) else (dy,)
    n_dy, n_add = len(dys), len(adds)

    def body(*refs):
        x_ref, g_ref = refs[:2]
        dy_refs = refs[2:2 + n_dy]
        add_refs = refs[2 + n_dy:2 + n_dy + n_add]
        dx_ref, dg_ref = refs[2 + n_dy + n_add:]
        xf = x_ref[...]
        dyf = dy_refs[0][...].astype(F32)
        for d_ref in dy_refs[1:]:
            dyf = dyf + d_ref[...].astype(F32)
        r = lax.rsqrt(jnp.mean(xf * xf, axis=-1, keepdims=True) + EPS)
        gy = dyf * g_ref[...]
        c = jnp.mean(xf * gy, axis=-1, keepdims=True) * (r * r * r)
        dx = gy * r - xf * c
        for a_ref in add_refs:
            dx = dx + a_ref[...].astype(F32)
        dx_ref[...] = dx.astype(dx_ref.dtype)

        @pl.when(pl.program_id(0) == 0)
        def _():
            dg_ref[...] = jnp.zeros_like(dg_ref)

        dg_ref[...] += jnp.sum(dyf * xf * r, axis=0, keepdims=True)

    row = pl.BlockSpec((tr, D), lambda i: (i, 0))
    vec = pl.BlockSpec((1, D), lambda i: (0, 0))
    dx, dg = pl.pallas_call(
        body, name=name,
        out_shape=(jax.ShapeDtypeStruct((L, D), dx_dtype), jax.ShapeDtypeStruct((1, D), F32)),
        grid=(L // tr,), in_specs=[row, vec] + [row] * (n_dy + n_add), out_specs=(row, vec),
        compiler_params=_params("arbitrary"),
    )(x, g.reshape(1, D), *dys, *adds)
    return dx, dg.reshape(D)


def _loss_head(h, target, *, tr=256):
    L, D = h.shape
    tr = min(tr, L)

    def body(h_ref, t_ref, dh_ref, loss_ref):
        e = h_ref[...] - t_ref[...]
        dh_ref[...] = e * (1.0 / D)

        @pl.when(pl.program_id(0) == 0)
        def _():
            loss_ref[...] = jnp.zeros_like(loss_ref)

        loss_ref[...] += jnp.sum(e * e, axis=0, keepdims=True) * (0.5 / D)

    row = pl.BlockSpec((tr, D), lambda i: (i, 0))
    vec = pl.BlockSpec((1, D), lambda i: (0, 0))
    dh, lp = pl.pallas_call(
        body, name="loss_head",
        out_shape=(jax.ShapeDtypeStruct((L, D), F32), jax.ShapeDtypeStruct((1, D), F32)),
        grid=(L // tr,), in_specs=[row, row], out_specs=(row, vec),
        compiler_params=_params("arbitrary"),
    )(h, target)
    return dh, lp


def _s5_coeffs(lr, li, ls):
    dt = jnp.exp(ls)
    mag = jnp.exp(lr * dt)
    ar = mag * jnp.cos(li * dt)
    ai = mag * jnp.sin(li * dt)
    den = lr * lr + li * li
    cr = ((ar - 1.0) * lr + ai * li) / den
    ci = (ai * lr - (ar - 1.0) * li) / den
    return dt, ar, ai, den, cr, ci


def _s5_prep(lam_re, lam_im, log_step, b_re_t, b_im_t):
    G, P = lam_re.shape
    H = b_re_t.shape[1]

    def body(lr_ref, li_ref, ls_ref, br_ref, bi_ref, ar_ref, ai_ref, bbr_ref, bbi_ref):
        _, ar, ai, _, cr, ci = _s5_coeffs(lr_ref[...], li_ref[...], ls_ref[...])
        ar_ref[...] = ar
        ai_ref[...] = ai
        br, bi = br_ref[...], bi_ref[...]
        crb, cib = cr[:, None, :], ci[:, None, :]
        bbr_ref[...] = crb * br - cib * bi
        bbi_ref[...] = crb * bi + cib * br

    return pl.pallas_call(
        body, name="s5_prep",
        out_shape=(jax.ShapeDtypeStruct((G, P), F32), jax.ShapeDtypeStruct((G, P), F32),
                   jax.ShapeDtypeStruct((G, H, P), F32), jax.ShapeDtypeStruct((G, H, P), F32)),
        compiler_params=_params(),
    )(lam_re, lam_im, log_step.reshape(G, 1), b_re_t, b_im_t)


def _s5_prep_bwd(lam_re, lam_im, log_step, b_re_t, b_im_t, d_ar, d_ai, d_bbr, d_bbi):
    G, P = lam_re.shape
    H = b_re_t.shape[1]

    def body(lr_ref, li_ref, ls_ref, br_ref, bi_ref, dar_ref, dai_ref, dbbr_ref, dbbi_ref,
             dlr_ref, dli_ref, dls_ref, dbr_ref, dbi_ref):
        lr, li = lr_ref[...], li_ref[...]
        dt, ar, ai, den, cr, ci = _s5_coeffs(lr, li, ls_ref[...])
        br, bi = br_ref[...], bi_ref[...]
        gbr, gbi = dbbr_ref[...], dbbi_ref[...]
        crb, cib = cr[:, None, :], ci[:, None, :]
        dbr_ref[...] = crb * gbr + cib * gbi
        dbi_ref[...] = crb * gbi - cib * gbr
        gcr = jnp.sum(br * gbr + bi * gbi, axis=1)
        gci = jnp.sum(br * gbi - bi * gbr, axis=1)
        ilr, ili = lr / den, -li / den
        gar = dar_ref[...] + (ilr * gcr + ili * gci)
        gai = dai_ref[...] + (ilr * gci - ili * gcr)
        qr, qi = cr * ilr - ci * ili, cr * ili + ci * ilr
        glr = -(qr * gcr + qi * gci)
        gli = -(qr * gci - qi * gcr)
        glr = glr + dt * (ar * gar + ai * gai)
        gli = gli + dt * (ar * gai - ai * gar)
        wr, wi = lr * ar - li * ai, lr * ai + li * ar
        gdt = jnp.sum(wr * gar + wi * gai, axis=1, keepdims=True)
        dlr_ref[...] = glr
        dli_ref[...] = gli
        dls_ref[...] = gdt * dt

    return pl.pallas_call(
        body, name="s5_prep_bwd",
        out_shape=(jax.ShapeDtypeStruct((G, P), F32), jax.ShapeDtypeStruct((G, P), F32),
                   jax.ShapeDtypeStruct((G, 1), F32),
                   jax.ShapeDtypeStruct((G, H, P), F32), jax.ShapeDtypeStruct((G, H, P), F32)),
        compiler_params=_params(),
    )(lam_re, lam_im, log_step.reshape(G, 1), b_re_t, b_im_t, d_ar, d_ai, d_bbr, d_bbi)


def _s5_block_mats(bbr_t, bbi_t, c_re, c_im):
    bmat = _s5_expand(bbr_t, bbi_t)
    cmat = jnp.transpose(_s5_expand(c_re, -c_im), (0, 2, 1))
    return bmat.astype(BF16), cmat.astype(BF16)


def _s5_diag_mask():
    r = lax.broadcasted_iota(jnp.int32, (LANES, 2 * STATE_COLS), 0) // SSM_GROUP
    c = (lax.broadcasted_iota(jnp.int32, (LANES, 2 * STATE_COLS), 1) % STATE_COLS) // SSM_STATE
    return (r == c).astype(F32)


def _s5_expand(re, im):
    re = jnp.tile(re.reshape(SSM_BLOCKS, LANES, SSM_STATE), (1, 1, GROUPS_PER_BLOCK))
    im = jnp.tile(im.reshape(SSM_BLOCKS, LANES, SSM_STATE), (1, 1, GROUPS_PER_BLOCK))
    return jnp.concatenate([re, im], axis=-1) * _s5_diag_mask()[None]


def _s5_block_diag(dmat):
    d = dmat * _s5_diag_mask()[None]
    parts = []
    for ri in range(2):
        acc = 0.0
        for g in range(GROUPS_PER_BLOCK):
            c0 = ri * STATE_COLS + g * SSM_STATE
            acc = acc + d[:, :, c0:c0 + SSM_STATE]
        parts.append(acc.reshape(SSM_GROUPS, SSM_GROUP, SSM_STATE))
    return jnp.stack(parts)


def _s5_a_rows(ar, ai):
    a = jnp.concatenate([ar.reshape(SSM_BLOCKS, STATE_COLS), ai.reshape(SSM_BLOCKS, STATE_COLS)], axis=1)
    return jnp.broadcast_to(a[:, None, :], (SSM_BLOCKS, SUBLANES, 2 * STATE_COLS))


def _s5_fwd(proj, bmat, cmat, a_rows, d_skip, *, tc=512):
    L = proj.shape[0]
    tc = min(tc, L)
    nt = L // tc
    n8 = tc // SUBLANES
    S = STATE_COLS

    def body(u_ref, b_ref, c_ref, a_ref, d_ref, y_ref, yg_ref, xp_ref, bu_s, xp_s, carry_s):
        @pl.when(pl.program_id(1) == 0)
        def _():
            carry_s[...] = jnp.zeros_like(carry_s)

        u = u_ref[...]
        bu = jnp.dot(u.astype(BF16), b_ref[...], preferred_element_type=F32)
        bu_s[...] = bu.reshape(n8, SUBLANES, 2 * S)
        ar, ai = a_ref[0:1, :S], a_ref[0:1, S:]

        def step(i, carry):
            cr, ci = carry
            for j in range(SUBLANES):
                xp_s[i, j:j + 1, :S] = cr
                xp_s[i, j:j + 1, S:] = ci
                br = bu_s[i, j:j + 1, :S]
                bi = bu_s[i, j:j + 1, S:]
                cr, ci = ar * cr - ai * ci + br, ar * ci + ai * cr + bi
            return cr, ci

        cr, ci = lax.fori_loop(0, n8, step, (carry_s[0:1, :S], carry_s[0:1, S:]))
        carry_s[0:1, :S] = cr
        carry_s[0:1, S:] = ci
        xp = xp_s[...].reshape(tc, 2 * S)
        xp_ref[...] = xp
        x_re = ar * xp[:, :S] - ai * xp[:, S:] + bu[:, :S]
        x_im = ar * xp[:, S:] + ai * xp[:, :S] + bu[:, S:]
        xs = jnp.concatenate([x_re, x_im], axis=1).astype(BF16)
        y = jnp.dot(xs, c_ref[...], preferred_element_type=F32) + d_ref[...] * u
        y_ref[...] = y
        yg_ref[...] = _gelu(y).astype(BF16)

    return pl.pallas_call(
        body, name="s5_fwd",
        out_shape=(jax.ShapeDtypeStruct((L, MAIN_WIDTH), F32),
                   jax.ShapeDtypeStruct((L, MAIN_WIDTH), BF16),
                   jax.ShapeDtypeStruct((L, SSM_BLOCKS * 2 * S), F32)),
        grid=(SSM_BLOCKS, nt),
        in_specs=[pl.BlockSpec((tc, LANES), lambda b, t: (t, b)),
                  pl.BlockSpec((None, LANES, 2 * S), lambda b, t: (b, 0, 0)),
                  pl.BlockSpec((None, 2 * S, LANES), lambda b, t: (b, 0, 0)),
                  pl.BlockSpec((None, SUBLANES, 2 * S), lambda b, t: (b, 0, 0)),
                  pl.BlockSpec((1, LANES), lambda b, t: (0, b))],
        out_specs=(pl.BlockSpec((tc, LANES), lambda b, t: (t, b)),
                   pl.BlockSpec((tc, LANES), lambda b, t: (t, b)),
                   pl.BlockSpec((tc, 2 * S), lambda b, t: (t, b))),
        scratch_shapes=[pltpu.VMEM((n8, SUBLANES, 2 * S), F32),
                        pltpu.VMEM((n8, SUBLANES, 2 * S), F32),
                        pltpu.VMEM((SUBLANES, 2 * S), F32)],
        compiler_params=_params("parallel", "arbitrary"),
    )(proj, bmat, cmat, a_rows, d_skip.reshape(1, MAIN_WIDTH))


def _s5_bwd(proj, dy, xp, bmat, cmat, a_rows, d_skip, *, tc=512):
    L = proj.shape[0]
    tc = min(tc, L)
    nt = L // tc
    n8 = tc // SUBLANES
    S = STATE_COLS
    nn = (((1,), (1,)), ((), ()))
    tn = (((0,), (0,)), ((), ()))

    def body(u_ref, dy_ref, xp_ref, b_ref, c_ref, a_ref, d_ref,
             du_ref, db_ref, dc_ref, da_ref, dd_ref, dl_s, carry_s):
        @pl.when(pl.program_id(1) == 0)
        def _():
            carry_s[...] = jnp.zeros_like(carry_s)
            db_ref[...] = jnp.zeros_like(db_ref)
            dc_ref[...] = jnp.zeros_like(dc_ref)
            da_ref[...] = jnp.zeros_like(da_ref)
            dd_ref[...] = jnp.zeros_like(dd_ref)

        u = u_ref[...]
        dy = dy_ref[...]
        xp = xp_ref[...]
        ub = u.astype(BF16)
        dyb = dy.astype(BF16)
        ar, ai = a_ref[0:1, :S], a_ref[0:1, S:]
        bu = jnp.dot(ub, b_ref[...], preferred_element_type=F32)
        x_re = ar * xp[:, :S] - ai * xp[:, S:] + bu[:, :S]
        x_im = ar * xp[:, S:] + ai * xp[:, :S] + bu[:, S:]
        xs = jnp.concatenate([x_re, x_im], axis=1).astype(BF16)
        dc_ref[...] += lax.dot_general(dyb, xs, tn, preferred_element_type=F32)
        dx = lax.dot_general(dyb, c_ref[...], nn, preferred_element_type=F32)
        dl_s[...] = dx.reshape(n8, SUBLANES, 2 * S)

        def step(k, carry):
            cr, ci = carry
            i = n8 - 1 - k
            for j in range(SUBLANES - 1, -1, -1):
                lr = dl_s[i, j:j + 1, :S] + (ar * cr + ai * ci)
                li = dl_s[i, j:j + 1, S:] + (ar * ci - ai * cr)
                dl_s[i, j:j + 1, :S] = lr
                dl_s[i, j:j + 1, S:] = li
                cr, ci = lr, li
            return cr, ci

        cr, ci = lax.fori_loop(0, n8, step, (carry_s[0:1, :S], carry_s[0:1, S:]))
        carry_s[0:1, :S] = cr
        carry_s[0:1, S:] = ci
        lam = dl_s[...].reshape(tc, 2 * S)
        l_re, l_im = lam[:, :S], lam[:, S:]
        da_ref[0:1, :S] += jnp.sum(l_re * xp[:, :S] + l_im * xp[:, S:], axis=0, keepdims=True)
        da_ref[0:1, S:] += jnp.sum(l_im * xp[:, :S] - l_re * xp[:, S:], axis=0, keepdims=True)
        lamb = lam.astype(BF16)
        du_ref[...] = lax.dot_general(lamb, b_ref[...], nn, preferred_element_type=F32) + d_ref[...] * dy
        db_ref[...] += lax.dot_general(ub, lamb, tn, preferred_element_type=F32)
        dd_ref[0:1, :] += jnp.sum(dy * u, axis=0, keepdims=True)

    rev = lambda b, t: (nt - 1 - t, b)
    return pl.pallas_call(
        body, name="s5_bwd",
        out_shape=(jax.ShapeDtypeStruct((L, MAIN_WIDTH), F32),
                   jax.ShapeDtypeStruct((SSM_BLOCKS, LANES, 2 * S), F32),
                   jax.ShapeDtypeStruct((SSM_BLOCKS, LANES, 2 * S), F32),
                   jax.ShapeDtypeStruct((SSM_BLOCKS, SUBLANES, 2 * S), F32),
                   jax.ShapeDtypeStruct((SUBLANES, MAIN_WIDTH), F32)),
        grid=(SSM_BLOCKS, nt),
        in_specs=[pl.BlockSpec((tc, LANES), rev),
                  pl.BlockSpec((tc, LANES), rev),
                  pl.BlockSpec((tc, 2 * S), rev),
                  pl.BlockSpec((None, LANES, 2 * S), lambda b, t: (b, 0, 0)),
                  pl.BlockSpec((None, 2 * S, LANES), lambda b, t: (b, 0, 0)),
                  pl.BlockSpec((None, SUBLANES, 2 * S), lambda b, t: (b, 0, 0)),
                  pl.BlockSpec((1, LANES), lambda b, t: (0, b))],
        out_specs=(pl.BlockSpec((tc, LANES), rev),
                   pl.BlockSpec((None, LANES, 2 * S), lambda b, t: (b, 0, 0)),
                   pl.BlockSpec((None, LANES, 2 * S), lambda b, t: (b, 0, 0)),
                   pl.BlockSpec((None, SUBLANES, 2 * S), lambda b, t: (b, 0, 0)),
                   pl.BlockSpec((SUBLANES, LANES), lambda b, t: (0, b))),
        scratch_shapes=[pltpu.VMEM((n8, SUBLANES, 2 * S), F32),
                        pltpu.VMEM((SUBLANES, 2 * S), F32)],
        compiler_params=_params("parallel", "arbitrary"),
    )(proj, dy, xp, bmat, cmat, a_rows, d_skip.reshape(1, MAIN_WIDTH))


def _row_specs(tr):
    main = pl.BlockSpec((tr, MAIN_WIDTH), lambda i: (i, 0))
    z = pl.BlockSpec((tr, MAIN_WIDTH), lambda i: (i, 1))
    zm = pl.BlockSpec((tr, MEM_WIDTH), lambda i: (i, IN_WIDTH // MEM_WIDTH - 1))
    mem = pl.BlockSpec((tr, MEM_WIDTH), lambda i: (i, 0))
    cat = pl.BlockSpec((tr, D_MODEL), lambda i: (i, 0))
    vec = pl.BlockSpec((1, MAIN_WIDTH), lambda i: (0, 0))
    return main, z, zm, mem, cat, vec


def _gate_a_fwd(y, t, b_glu, proj, o_mem, *, tr=256):
    L = y.shape[0]
    tr = min(tr, L)

    def body(y_ref, t_ref, b_ref, z_ref, zm_ref, om_ref, o_ref):
        yg = _gelu(y_ref[...])
        sz, _ = _silu_and_grad(z_ref[...])
        o_ref[:, :MAIN_WIDTH] = (yg * _sigmoid(t_ref[...] + b_ref[...]) * sz).astype(BF16)
        szm, _ = _silu_and_grad(zm_ref[...])
        o_ref[:, MAIN_WIDTH:] = (om_ref[...] * szm).astype(BF16)

    main, z, zm, mem, cat, vec = _row_specs(tr)
    return pl.pallas_call(
        body, name="gate_a_fwd", out_shape=jax.ShapeDtypeStruct((L, D_MODEL), BF16),
        grid=(L // tr,), in_specs=[main, main, vec, z, zm, mem], out_specs=cat,
        compiler_params=_params("parallel"),
    )(y, t, b_glu.reshape(1, MAIN_WIDTH), proj, proj, o_mem)


def _gate_a_bwd(dcat, y, t, b_glu, proj, o_mem, *, tr=256):
    L = y.shape[0]
    tr = min(tr, L)

    def body(dc_ref, y_ref, t_ref, b_ref, z_ref, zm_ref, om_ref,
             dz_ref, dzm_ref, dt_ref, dyg_ref, dom_ref, db_ref):
        dmain = dc_ref[:, :MAIN_WIDTH]
        dmemo = dc_ref[:, MAIN_WIDTH:]
        yg = _gelu(y_ref[...])
        sg = _sigmoid(t_ref[...] + b_ref[...])
        sz, gz = _silu_and_grad(z_ref[...])
        dz_ref[...] = (dmain * (yg * sg) * gz).astype(BF16)
        dy2 = dmain * sz
        dyg_ref[...] = dy2 * sg
        dt = dy2 * yg * (sg * (1.0 - sg))
        dt_ref[...] = dt.astype(BF16)

        @pl.when(pl.program_id(0) == 0)
        def _():
            db_ref[...] = jnp.zeros_like(db_ref)

        db_ref[...] += jnp.sum(dt, axis=0, keepdims=True)
        szm, gzm = _silu_and_grad(zm_ref[...])
        dom_ref[...] = dmemo * szm
        dzm_ref[...] = (dmemo * om_ref[...] * gzm).astype(BF16)

    main, z, zm, mem, cat, vec = _row_specs(tr)
    outs = pl.pallas_call(
        body, name="gate_a_bwd",
        out_shape=(jax.ShapeDtypeStruct((L, MAIN_WIDTH), BF16), jax.ShapeDtypeStruct((L, MEM_WIDTH), BF16),
                   jax.ShapeDtypeStruct((L, MAIN_WIDTH), BF16), jax.ShapeDtypeStruct((L, MAIN_WIDTH), F32),
                   jax.ShapeDtypeStruct((L, MEM_WIDTH), F32), jax.ShapeDtypeStruct((1, MAIN_WIDTH), F32)),
        grid=(L // tr,), in_specs=[cat, main, main, vec, z, zm, mem],
        out_specs=(main, mem, main, main, mem, vec),
        compiler_params=_params("arbitrary"),
    )(dcat, y, t, b_glu.reshape(1, MAIN_WIDTH), proj, proj, o_mem)
    return outs


def _gelu_bwd(dyg_a, dyg_b, y, *, tr=256):
    L = y.shape[0]
    tr = min(tr, L)

    def body(a_ref, b_ref, y_ref, o_ref):
        o_ref[...] = (a_ref[...] + b_ref[...]) * _gelu_grad(y_ref[...])

    main = pl.BlockSpec((tr, MAIN_WIDTH), lambda i: (i, 0))
    return pl.pallas_call(
        body, name="gelu_bwd", out_shape=jax.ShapeDtypeStruct((L, MAIN_WIDTH), F32),
        grid=(L // tr,), in_specs=[main, main, main], out_specs=main,
        compiler_params=_params("parallel"),
    )(dyg_a, dyg_b, y)


def _gate_b_fwd(att, proj, o_mem, *, tr=256):
    L = att.shape[0]
    tr = min(tr, L)

    def body(a_ref, z_ref, zm_ref, om_ref, o_ref):
        sz, _ = _silu_and_grad(z_ref[...])
        o_ref[:, :MAIN_WIDTH] = (a_ref[...] * sz).astype(BF16)
        szm, _ = _silu_and_grad(zm_ref[...])
        o_ref[:, MAIN_WIDTH:] = (om_ref[...] * szm).astype(BF16)

    main, z, zm, mem, cat, _ = _row_specs(tr)
    return pl.pallas_call(
        body, name="gate_b_fwd", out_shape=jax.ShapeDtypeStruct((L, D_MODEL), BF16),
        grid=(L // tr,), in_specs=[main, z, zm, mem], out_specs=cat,
        compiler_params=_params("parallel"),
    )(att, proj, proj, o_mem)


def _gate_b_bwd(dcat, att, proj, o_mem, *, tr=256):
    L = att.shape[0]
    tr = min(tr, L)

    def body(dc_ref, a_ref, z_ref, zm_ref, om_ref, da_ref, dz_ref, dom_ref, dzm_ref, dl_ref):
        dmain = dc_ref[:, :MAIN_WIDTH]
        dmemo = dc_ref[:, MAIN_WIDTH:]
        att = a_ref[...]
        sz, gz = _silu_and_grad(z_ref[...])
        datt = dmain * sz
        da_ref[...] = datt
        dz_ref[...] = (dmain * att * gz).astype(BF16)
        szm, gzm = _silu_and_grad(zm_ref[...])
        dom_ref[...] = dmemo * szm
        dzm_ref[...] = (dmemo * om_ref[...] * gzm).astype(BF16)
        prod = datt * att
        for h in range(FOX_HEADS):
            dl_ref[h] = jnp.sum(prod[:, h * HEAD_DIM:(h + 1) * HEAD_DIM], axis=1, keepdims=True)

    main, z, zm, mem, cat, _ = _row_specs(tr)
    delta = pl.BlockSpec((FOX_HEADS, tr, 1), lambda i: (0, i, 0))
    return pl.pallas_call(
        body, name="gate_b_bwd",
        out_shape=(jax.ShapeDtypeStruct((L, MAIN_WIDTH), F32), jax.ShapeDtypeStruct((L, MAIN_WIDTH), BF16),
                   jax.ShapeDtypeStruct((L, MEM_WIDTH), F32), jax.ShapeDtypeStruct((L, MEM_WIDTH), BF16),
                   jax.ShapeDtypeStruct((FOX_HEADS, L, 1), F32)),
        grid=(L // tr,), in_specs=[cat, main, z, zm, mem], out_specs=(main, main, mem, mem, delta),
        compiler_params=_params("parallel"),
    )(dcat, att, proj, proj, o_mem)


_MEM_Q_COL = (2 * MAIN_WIDTH) // HEAD_DIM
_NT = (((1,), (1,)), ((), ()))
_TN = (((0,), (0,)), ((), ()))


def _mem_probs(q_ref, k_ref):
    qs = (q_ref[...] * (HEAD_DIM ** -0.5)).astype(BF16)
    s = lax.dot_general(qs, k_ref[...].astype(BF16), _NT, preferred_element_type=F32)
    e = jnp.exp(s - jnp.max(s, axis=-1, keepdims=True))
    return qs, e / jnp.sum(e, axis=-1, keepdims=True)


def _mem_attn_fwd(proj, kvm, *, tq=512):
    L = proj.shape[0]
    tq = min(tq, L)

    def body(q_ref, k_ref, v_ref, o_ref):
        _, p = _mem_probs(q_ref, k_ref)
        o_ref[...] = jnp.dot(p.astype(BF16), v_ref[...].astype(BF16), preferred_element_type=F32)

    return pl.pallas_call(
        body, name="mem_attn_fwd", out_shape=jax.ShapeDtypeStruct((L, MEM_WIDTH), F32),
        grid=(MEM_HEADS, L // tq),
        in_specs=[pl.BlockSpec((tq, HEAD_DIM), lambda h, i: (i, _MEM_Q_COL + h)),
                  pl.BlockSpec((N_MEM, HEAD_DIM), lambda h, i: (0, h)),
                  pl.BlockSpec((N_MEM, HEAD_DIM), lambda h, i: (0, MEM_HEADS + h))],
        out_specs=pl.BlockSpec((tq, HEAD_DIM), lambda h, i: (i, h)),
        compiler_params=_params("parallel", "parallel"),
    )(proj, kvm, kvm)


def _mem_attn_bwd(proj, kvm, do, *, tq=512):
    L = proj.shape[0]
    tq = min(tq, L)

    def body(q_ref, k_ref, v_ref, do_ref, dq_ref, dk_ref, dv_ref):
        @pl.when(pl.program_id(1) == 0)
        def _():
            dk_ref[...] = jnp.zeros_like(dk_ref)
            dv_ref[...] = jnp.zeros_like(dv_ref)

        qs, p = _mem_probs(q_ref, k_ref)
        dob = do_ref[...].astype(BF16)
        dp = lax.dot_general(dob, v_ref[...].astype(BF16), _NT, preferred_element_type=F32)
        ds = p * (dp - jnp.sum(p * dp, axis=-1, keepdims=True))
        dsb = ds.astype(BF16)
        dq = jnp.dot(dsb, k_ref[...].astype(BF16), preferred_element_type=F32) * (HEAD_DIM ** -0.5)
        dq_ref[...] = dq.astype(BF16)
        dk_ref[...] += lax.dot_general(dsb, qs, _TN, preferred_element_type=F32)
        dv_ref[...] += lax.dot_general(p.astype(BF16), dob, _TN, preferred_element_type=F32)

    dq, dk, dv = pl.pallas_call(
        body, name="mem_attn_bwd",
        out_shape=(jax.ShapeDtypeStruct((L, MEM_WIDTH), BF16),
                   jax.ShapeDtypeStruct((N_MEM, MEM_WIDTH), F32),
                   jax.ShapeDtypeStruct((N_MEM, MEM_WIDTH), F32)),
        grid=(MEM_HEADS, L // tq),
        in_specs=[pl.BlockSpec((tq, HEAD_DIM), lambda h, i: (i, _MEM_Q_COL + h)),
                  pl.BlockSpec((N_MEM, HEAD_DIM), lambda h, i: (0, h)),
                  pl.BlockSpec((N_MEM, HEAD_DIM), lambda h, i: (0, MEM_HEADS + h)),
                  pl.BlockSpec((tq, HEAD_DIM), lambda h, i: (i, h))],
        out_specs=(pl.BlockSpec((tq, HEAD_DIM), lambda h, i: (i, h)),
                   pl.BlockSpec((N_MEM, HEAD_DIM), lambda h, i: (0, h)),
                   pl.BlockSpec((N_MEM, HEAD_DIM), lambda h, i: (0, h))),
        compiler_params=_params("parallel", "arbitrary"),
    )(proj, kvm, kvm, do)
    return dq, jnp.concatenate([dk, dv], axis=1)


def _tile_cumsum(x, row, reverse):
    for sh in (1, 2, 4):
        if reverse:
            x = x + jnp.where(row < SUBLANES - sh, pltpu.roll(x, SUBLANES - sh, 0), 0.0)
        else:
            x = x + jnp.where(row >= sh, pltpu.roll(x, sh, 0), 0.0)
    return x


def _fgate_fwd(pre, b_pad):
    L = pre.shape[0]
    n8 = L // SUBLANES

    def body(p_ref, b_ref, o_ref):
        row = lax.broadcasted_iota(jnp.int32, (SUBLANES, LANES), 0)
        b = b_ref[...]

        def step(i, carry):
            x = p_ref[i] + b
            logf = jnp.minimum(x, 0.0) - jnp.log(1.0 + jnp.exp(-jnp.abs(x)))
            t = _tile_cumsum(logf, row, False) + carry
            o_ref[i] = t
            return t[SUBLANES - 1:SUBLANES, :]

        lax.fori_loop(0, n8, step, jnp.zeros((1, LANES), F32))

    out = pl.pallas_call(
        body, name="fgate_fwd", out_shape=jax.ShapeDtypeStruct((n8, SUBLANES, LANES), F32),
        compiler_params=_params(),
    )(pre.reshape(n8, SUBLANES, LANES), b_pad.reshape(1, LANES))
    return out.reshape(L, LANES)


def _fgate_bwd(dfcum, pre, b_pad):
    L = pre.shape[0]
    n8 = L // SUBLANES

    def body(d_ref, p_ref, b_ref, o_ref, s_ref):
        row = lax.broadcasted_iota(jnp.int32, (SUBLANES, LANES), 0)
        b = b_ref[...]

        def step(k, carry):
            c, acc = carry
            i = n8 - 1 - k
            t = _tile_cumsum(d_ref[i], row, True) + c
            dpre = t * _sigmoid(-(p_ref[i] + b))
            o_ref[i] = dpre
            return t[0:1, :], acc + dpre

        _, acc = lax.fori_loop(0, n8, step, (jnp.zeros((1, LANES), F32), jnp.zeros((SUBLANES, LANES), F32)))
        s_ref[...] = jnp.sum(acc, axis=0, keepdims=True)

    dpre, db = pl.pallas_call(
        body, name="fgate_bwd",
        out_shape=(jax.ShapeDtypeStruct((n8, SUBLANES, LANES), F32), jax.ShapeDtypeStruct((1, LANES), F32)),
        compiler_params=_params(),
    )(dfcum.reshape(n8, SUBLANES, LANES), pre.reshape(n8, SUBLANES, LANES), b_pad.reshape(1, LANES))
    return dpre.reshape(L, LANES), db


FOX_BLOCK = 512


def _fox_scores(qs, k, fq, fk, diagonal):
    s = lax.dot_general(qs, k, _NT, preferred_element_type=F32) + fq - fk
    if diagonal:
        row = lax.broadcasted_iota(jnp.int32, s.shape, 0)
        col = lax.broadcasted_iota(jnp.int32, s.shape, 1)
        s = jnp.where(row >= col, s, NEG_BIG)
    return s


def _fox_specs(tq, L):
    nq = L // tq
    return dict(
        rows=lambda off: pl.BlockSpec((tq, HEAD_DIM), lambda h, i: (i, off + h)),
        seq=lambda off: pl.BlockSpec((L, HEAD_DIM), lambda h, i: (0, off + h)),
        col=pl.BlockSpec((None, None, tq, 1), lambda h, i: (h, i, 0, 0)),
        col_all=pl.BlockSpec((None, nq, tq, 1), lambda h, i: (h, 0, 0, 0)),
        row=pl.BlockSpec((None, None, 1, tq), lambda h, i: (h, i, 0, 0)),
        row_all=pl.BlockSpec((None, nq, 1, tq), lambda h, i: (h, 0, 0, 0)))


def _fox_fwd(proj, kv, fq, fk):
    L = proj.shape[0]
    tq = min(FOX_BLOCK, L)
    nq = L // tq
    sp = _fox_specs(tq, L)

    def body(q_ref, k_ref, v_ref, fq_ref, fk_ref, o_ref, lse_ref, m_s, l_s, acc_s):
        qi = pl.program_id(1)
        qs = (q_ref[...] * (HEAD_DIM ** -0.5)).astype(BF16)
        fq = fq_ref[...]
        m_s[...] = jnp.full_like(m_s, NEG_BIG)
        l_s[...] = jnp.zeros_like(l_s)
        acc_s[...] = jnp.zeros_like(acc_s)

        def block(j, diagonal):
            r0 = pl.multiple_of(j * tq, tq)
            s = _fox_scores(qs, k_ref[pl.ds(r0, tq), :], fq, fk_ref[j], diagonal)
            m_new = jnp.maximum(m_s[...], jnp.max(s, axis=-1, keepdims=True))
            alpha = jnp.exp(m_s[...] - m_new)
            p = jnp.exp(s - m_new)
            l_s[...] = alpha * l_s[...] + jnp.sum(p, axis=-1, keepdims=True)
            acc_s[...] = alpha * acc_s[...] + jnp.dot(p.astype(BF16), v_ref[pl.ds(r0, tq), :],
                                                      preferred_element_type=F32)
            m_s[...] = m_new

        def below(j, carry):
            block(j, False)
            return carry

        lax.fori_loop(0, qi, below, 0)
        block(qi, True)
        o_ref[...] = acc_s[...] / l_s[...]
        lse_ref[...] = m_s[...] + jnp.log(l_s[...])

    return pl.pallas_call(
        body, name="fox_fwd",
        out_shape=(jax.ShapeDtypeStruct((L, MAIN_WIDTH), F32),
                   jax.ShapeDtypeStruct((FOX_HEADS, nq, tq, 1), F32)),
        grid=(FOX_HEADS, nq),
        in_specs=[sp["rows"](0), sp["seq"](0), sp["seq"](FOX_HEADS), sp["col"], sp["row_all"]],
        out_specs=(sp["rows"](0), sp["col"]),
        scratch_shapes=[pltpu.VMEM((tq, 1), F32), pltpu.VMEM((tq, 1), F32), pltpu.VMEM((tq, HEAD_DIM), F32)],
        compiler_params=_params("parallel", "parallel"),
    )(proj, kv, kv, fq, fk)


def _fox_bwd_dq(proj, kv, fq, fk, lse, delta, datt):
    L = proj.shape[0]
    tq = min(FOX_BLOCK, L)
    nq = L // tq
    sp = _fox_specs(tq, L)

    def body(q_ref, k_ref, v_ref, fq_ref, fk_ref, lse_ref, dl_ref, do_ref, dq_ref, df_ref, acc_s, df_s):
        qi = pl.program_id(1)
        qs = (q_ref[...] * (HEAD_DIM ** -0.5)).astype(BF16)
        dob = do_ref[...].astype(BF16)
        fq, lse, dl = fq_ref[...], lse_ref[...], dl_ref[...]
        acc_s[...] = jnp.zeros_like(acc_s)
        df_s[...] = jnp.zeros_like(df_s)

        def block(j, diagonal):
            r0 = pl.multiple_of(j * tq, tq)
            k = k_ref[pl.ds(r0, tq), :]
            p = jnp.exp(_fox_scores(qs, k, fq, fk_ref[j], diagonal) - lse)
            dp = lax.dot_general(dob, v_ref[pl.ds(r0, tq), :], _NT, preferred_element_type=F32)
            ds = p * (dp - dl)
            acc_s[...] += jnp.dot(ds.astype(BF16), k, preferred_element_type=F32)
            df_s[...] += jnp.sum(ds, axis=1, keepdims=True)

        def below(j, carry):
            block(j, False)
            return carry

        lax.fori_loop(0, qi, below, 0)
        block(qi, True)
        dq_ref[...] = (acc_s[...] * (HEAD_DIM ** -0.5)).astype(BF16)
        df_ref[...] = df_s[...]

    return pl.pallas_call(
        body, name="fox_bwd_dq",
        out_shape=(jax.ShapeDtypeStruct((L, MAIN_WIDTH), BF16),
                   jax.ShapeDtypeStruct((FOX_HEADS, nq, tq, 1), F32)),
        grid=(FOX_HEADS, nq),
        in_specs=[sp["rows"](0), sp["seq"](0), sp["seq"](FOX_HEADS), sp["col"], sp["row_all"],
                  sp["col"], sp["col"], sp["rows"](0)],
        out_specs=(sp["rows"](0), sp["col"]),
        scratch_shapes=[pltpu.VMEM((tq, HEAD_DIM), F32), pltpu.VMEM((tq, 1), F32)],
        compiler_params=_params("parallel", "parallel"),
    )(proj, kv, kv, fq, fk, lse, delta, datt)


def _fox_bwd_dkv(proj, kv, fq, fk, lse, delta, datt):
    L = proj.shape[0]
    tq = min(FOX_BLOCK, L)
    nq = L // tq
    sp = _fox_specs(tq, L)

    def body(q_ref, k_ref, v_ref, fq_ref, fk_ref, lse_ref, dl_ref, do_ref,
             dk_ref, dv_ref, df_ref, dk_s, dv_s, df_s):
        ki = pl.program_id(1)
        k, v, fk = k_ref[...], v_ref[...], fk_ref[...]
        dk_s[...] = jnp.zeros_like(dk_s)
        dv_s[...] = jnp.zeros_like(dv_s)
        df_s[...] = jnp.zeros_like(df_s)

        def block(i, diagonal):
            r0 = pl.multiple_of(i * tq, tq)
            qs = (q_ref[pl.ds(r0, tq), :] * (HEAD_DIM ** -0.5)).astype(BF16)
            dob = do_ref[pl.ds(r0, tq), :].astype(BF16)
            p = jnp.exp(_fox_scores(qs, k, fq_ref[i], fk, diagonal) - lse_ref[i])
            dp = lax.dot_general(dob, v, _NT, preferred_element_type=F32)
            ds = p * (dp - dl_ref[i])
            dv_s[...] += lax.dot_general(p.astype(BF16), dob, _TN, preferred_element_type=F32)
            dk_s[...] += lax.dot_general(ds.astype(BF16), qs, _TN, preferred_element_type=F32)
            df_s[...] -= jnp.sum(ds, axis=0, keepdims=True)

        def above(i, carry):
            block(i, False)
            return carry

        block(ki, True)
        lax.fori_loop(ki + 1, nq, above, 0)
        dk_ref[...] = dk_s[...].astype(BF16)
        dv_ref[...] = dv_s[...].astype(BF16)
        df_ref[...] = df_s[...]

    return pl.pallas_call(
        body, name="fox_bwd_dkv",
        out_shape=(jax.ShapeDtypeStruct((L, MAIN_WIDTH), BF16),
                   jax.ShapeDtypeStruct((L, MAIN_WIDTH), BF16),
                   jax.ShapeDtypeStruct((FOX_HEADS, nq, 1, tq), F32)),
        grid=(FOX_HEADS, nq),
        in_specs=[sp["seq"](0), sp["rows"](0), sp["rows"](FOX_HEADS), sp["col_all"], sp["row"],
                  sp["col_all"], sp["col_all"], sp["seq"](0)],
        out_specs=(sp["rows"](0), sp["rows"](0), sp["row"]),
        scratch_shapes=[pltpu.VMEM((tq, HEAD_DIM), F32), pltpu.VMEM((tq, HEAD_DIM), F32),
                        pltpu.VMEM((1, tq), F32)],
        compiler_params=_params("parallel", "parallel"),
    )(proj, kv, kv, fq, fk, lse, delta, datt)


def _pad_lanes(a):
    return jnp.pad(a, ((0, 0), (0, LANES - a.shape[1])))


def _mem_branch_fwd(mem, g, w_mk, proj, tag):
    memn = _rmsnorm_fwd(mem, g, name="mem_norm_" + tag, out_dtype=BF16)
    kvm = _mm(memn, w_mk, name="mem_kv_" + tag)
    return memn, kvm, _mem_attn_fwd(proj, kvm)


def _mem_branch_bwd(mem, g, w_mk, proj, memn, kvm, do_mem, tag):
    dqm, dkvm = _mem_attn_bwd(proj, kvm, do_mem)
    dkvm = dkvm.astype(BF16)
    dw_mk = _mm(memn, dkvm, ta=True, name="dw_mem_kv_" + tag, out_dtype=BF16)
    dmemn = _mm(dkvm, w_mk, tb=True, name="dmemn_" + tag)
    _, dg = _rmsnorm_bwd(mem, g, dmemn, name="mem_norm_bwd_" + tag, dx_dtype=BF16)
    return dqm, dw_mk, dg


def _local_step(x, mem, target, w):
    L = x.shape[0]
    g = {}

    b_re_t = jnp.transpose(w["b_re"], (0, 2, 1))
    b_im_t = jnp.transpose(w["b_im"], (0, 2, 1))
    ar, ai, bbr_t, bbi_t = _s5_prep(w["lam_re"], w["lam_im"], w["log_step"], b_re_t, b_im_t)
    bmat, cmat = _s5_block_mats(bbr_t, bbi_t, w["c_re"], w["c_im"])
    a_rows = _s5_a_rows(ar, ai)

    hn0 = _rmsnorm_fwd(x, w["pre_norm_g"][0], name="pre_norm_0", out_dtype=BF16)
    proj_a = _mm(hn0, w["w_in_a"], name="in_proj_a")
    y, yg, xp = _s5_fwd(proj_a, bmat, cmat, a_rows, w["d_skip"])
    t = _mm(yg, w["w_glu"], name="glu_proj")
    memn0, kvm0, om0 = _mem_branch_fwd(mem, w["mem_norm_g"][0], w["w_mem_kv"][0], proj_a, "0")
    cat0 = _gate_a_fwd(y, t, w["b_glu"], proj_a, om0)
    o0 = _mm(cat0, w["w_out"][0], name="out_proj_0")
    h1 = _rmsnorm_fwd(o0, w["post_norm_g"][0], res=x, name="post_norm_0")

    kv_in = _rmsnorm_fwd(h1, w["kv_norm_g"], name="kv_norm", out_dtype=BF16)
    kv = _mm(kv_in, w["w_kv"], name="kv_proj", out_dtype=BF16)
    pre_f = _mm(kv_in, w["w_fgate"], name="fgate_proj")
    b_f = jnp.pad(w["b_fgate"], (0, LANES - FOX_HEADS))
    fcum = _fgate_fwd(pre_f, b_f)
    fc = jnp.transpose(fcum[:, :FOX_HEADS])
    tq = min(FOX_BLOCK, L)
    fq, fk = fc.reshape(FOX_HEADS, L // tq, tq, 1), fc.reshape(FOX_HEADS, L // tq, 1, tq)

    hn1 = _rmsnorm_fwd(h1, w["pre_norm_g"][1], name="pre_norm_1", out_dtype=BF16)
    proj_b = _mm(hn1, w["w_in_b"], name="in_proj_b")
    att, lse = _fox_fwd(proj_b, kv, fq, fk)
    memn1, kvm1, om1 = _mem_branch_fwd(mem, w["mem_norm_g"][1], w["w_mem_kv"][1], proj_b, "1")
    cat1 = _gate_b_fwd(att, proj_b, om1)
    o1 = _mm(cat1, w["w_out"][1], name="out_proj_1")
    h2 = _rmsnorm_fwd(o1, w["post_norm_g"][1], res=h1, name="post_norm_1")

    dh2, loss_row = _loss_head(h2, target)

    do1, dpost1 = _rmsnorm_bwd(o1, w["post_norm_g"][1], dh2, name="post_norm_bwd_1", dx_dtype=BF16)
    dcat1 = _mm(do1, w["w_out"][1], tb=True, name="dcat_1")
    g["w_out_1"] = _mm(cat1, do1, ta=True, name="dw_out_1", out_dtype=BF16)
    datt, dz1, dom1, dzm1, delta = _gate_b_bwd(dcat1, att, proj_b, om1)
    dqm1, g["w_mem_kv_1"], dmemg1 = _mem_branch_bwd(mem, w["mem_norm_g"][1], w["w_mem_kv"][1], proj_b,
                                                   memn1, kvm1, dom1, "1")
    delta = delta.reshape(fq.shape)
    dq, dfq = _fox_bwd_dq(proj_b, kv, fq, fk, lse, delta, datt)
    dk, dv, dfk = _fox_bwd_dkv(proj_b, kv, fq, fk, lse, delta, datt)
    dproj_b = jnp.concatenate([dq, dz1, dqm1, dzm1], axis=1)
    g["w_in_b"] = _mm(hn1, dproj_b, ta=True, name="dw_in_b", out_dtype=BF16, shards=N_CHIPS)
    dhn1 = _mm(dproj_b, w["w_in_b"], tb=True, name="dhn_1")

    dkv = jnp.concatenate([dk, dv], axis=1)
    g["w_kv"] = _mm(kv_in, dkv, ta=True, name="dw_kv", out_dtype=BF16, shards=N_CHIPS)
    dkv_in_a = _mm(dkv, w["w_kv"], tb=True, name="dkv_in_kv")
    dfcum = _pad_lanes(jnp.transpose(dfq.reshape(FOX_HEADS, L) + dfk.reshape(FOX_HEADS, L)))
    dpre_f, db_f = _fgate_bwd(dfcum, pre_f, b_f)
    g["b_fgate"] = db_f[0, :FOX_HEADS]
    g["w_fgate"] = _mm(kv_in, dpre_f, ta=True, name="dw_fgate")[:, :FOX_HEADS]
    dkv_in_b = _mm(dpre_f, w["w_fgate"], tb=True, name="dkv_in_fgate")
    dh1_kv, g["kv_norm_g"] = _rmsnorm_bwd(h1, w["kv_norm_g"], (dkv_in_a, dkv_in_b), name="kv_norm_bwd")
    dh1, dpre1 = _rmsnorm_bwd(h1, w["pre_norm_g"][1], dhn1, adds=(dh2, dh1_kv), name="pre_norm_bwd_1")

    do0, dpost0 = _rmsnorm_bwd(o0, w["post_norm_g"][0], dh1, name="post_norm_bwd_0", dx_dtype=BF16)
    dcat0 = _mm(do0, w["w_out"][0], tb=True, name="dcat_0")
    g["w_out_0"] = _mm(cat0, do0, ta=True, name="dw_out_0", out_dtype=BF16)
    dz0, dzm0, dt, dyg_a, dom0, db_glu = _gate_a_bwd(dcat0, y, t, w["b_glu"], proj_a, om0)
    g["b_glu"] = db_glu[0]
    g["w_glu"] = _mm(yg, dt, ta=True, name="dw_glu", out_dtype=BF16)
    dyg_b = _mm(dt, w["w_glu"], tb=True, name="dyg")
    dy = _gelu_bwd(dyg_a, dyg_b, y)
    du, db_blk, dc_blk, da_rows, dd_skip = _s5_bwd(proj_a, dy, xp, bmat, cmat, a_rows, w["d_skip"])
    g["d_skip"] = dd_skip[0]
    dqm0, g["w_mem_kv_0"], dmemg0 = _mem_branch_bwd(mem, w["mem_norm_g"][0], w["w_mem_kv"][0], proj_a,
                                                   memn0, kvm0, dom0, "0")
    dproj_a = jnp.concatenate([du.astype(BF16), dz0, dqm0, dzm0], axis=1)
    g["w_in_a"] = _mm(hn0, dproj_a, ta=True, name="dw_in_a", out_dtype=BF16, shards=N_CHIPS)
    dhn0 = _mm(dproj_a, w["w_in_a"], tb=True, name="dhn_0")
    grad_x, dpre0 = _rmsnorm_bwd(x, w["pre_norm_g"][0], dhn0, adds=(dh1,), name="pre_norm_bwd_0")

    dbb = _s5_block_diag(db_blk)
    dcc = _s5_block_diag(dc_blk)
    g["c_re"], g["c_im"] = dcc[0], -dcc[1]
    d_ar = da_rows[:, 0, :STATE_COLS].reshape(SSM_GROUPS, SSM_STATE)
    d_ai = da_rows[:, 0, STATE_COLS:].reshape(SSM_GROUPS, SSM_STATE)
    dlr, dli, dls, dbr_t, dbi_t = _s5_prep_bwd(w["lam_re"], w["lam_im"], w["log_step"], b_re_t, b_im_t,
                                               d_ar, d_ai, dbb[0], dbb[1])
    g["lam_re"], g["lam_im"], g["log_step"] = dlr, dli, dls[:, 0]
    g["b_re"] = jnp.transpose(dbr_t, (0, 2, 1))
    g["b_im"] = jnp.transpose(dbi_t, (0, 2, 1))
    g["pre_norm_g"] = jnp.stack([dpre0, dpre1])
    g["post_norm_g"] = jnp.stack([dpost0, dpost1])
    g["mem_norm_g"] = jnp.stack([dmemg0, dmemg1])
    return loss_row, grad_x, g


_MESH = pl.DeviceIdType.MESH
_ANY = pl.BlockSpec(memory_space=pl.ANY)


def _place():
    x, y, c = lax.axis_index("x"), lax.axis_index("y"), lax.axis_index("c")
    chips = [(1 - x, y), (x, 1 - y), (1 - x, 1 - y)]
    return x, y, c, chips


def _all_gather_chips(parts):
    n = len(parts)

    def body(*refs):
        ins, outs = refs[:n], refs[n:2 * n]
        ici_send, ici_recv, d2d_send, d2d_recv = refs[2 * n:]
        x, y, c, chips = _place()
        me = 2 * x + y
        sib = (x, y, 1 - c)

        def ici(i, k, src_chip_j, dst):
            return pltpu.make_async_remote_copy(
                src_ref=ins[i].at[c] if src_chip_j is None else outs[i].at[src_chip_j, c],
                dst_ref=outs[i].at[me if src_chip_j is None else src_chip_j, c],
                send_sem=ici_send.at[i * 3 + k], recv_sem=ici_recv.at[i * 3 + k],
                device_id=dst, device_id_type=_MESH)

        def d2d(i, k, chip_j, half):
            return pltpu.make_async_remote_copy(
                src_ref=outs[i].at[chip_j, half], dst_ref=outs[i].at[chip_j, half],
                send_sem=d2d_send.at[i * 3 + k], recv_sem=d2d_recv.at[i * 3 + k],
                device_id=sib, device_id_type=_MESH)

        sends = [ici(i, k, None, (*chips[k], c)) for i in range(n) for k in range(3)]
        for cp in sends:
            cp.start()
        passed = []
        for k, (cx, cy) in enumerate(chips):
            for i in range(n):
                ici(i, k, 2 * cx + cy, (x, y, c)).wait_recv()
                fwd = d2d(i, k, 2 * cx + cy, c)
                fwd.start()
                passed.append(fwd)
        for k, (cx, cy) in enumerate(chips):
            for i in range(n):
                d2d(i, k, 2 * cx + cy, 1 - c).wait_recv()
        for cp in sends + passed:
            cp.wait_send()

    return pl.pallas_call(
        body, name="all_gather_weights",
        out_shape=[jax.ShapeDtypeStruct((N_CHIPS,) + p.shape, p.dtype) for p in parts],
        in_specs=[_ANY] * n, out_specs=[_ANY] * n,
        scratch_shapes=[pltpu.SemaphoreType.DMA((3 * n,)), pltpu.SemaphoreType.DMA((3 * n,)),
                        pltpu.SemaphoreType.DMA((3 * n,)), pltpu.SemaphoreType.DMA((3 * n,))],
    )(*parts)


def _swap_halves(grads):
    n = len(grads)

    def body(*refs):
        ins, outs = refs[:n], refs[n:2 * n]
        send_sem, recv_sem = refs[2 * n:]
        x, y, c, _ = _place()
        copies = [pltpu.make_async_remote_copy(
            src_ref=ins[i].at[:, 1 - c], dst_ref=outs[i],
            send_sem=send_sem.at[i], recv_sem=recv_sem.at[i],
            device_id=(x, y, 1 - c), device_id_type=_MESH) for i in range(n)]
        for cp in copies:
            cp.start()
        for cp in copies:
            cp.wait()

    return pl.pallas_call(
        body, name="grad_swap_halves",
        out_shape=[jax.ShapeDtypeStruct((N_CHIPS,) + g.shape[2:], g.dtype) for g in grads],
        in_specs=[_ANY] * n, out_specs=[_ANY] * n,
        scratch_shapes=[pltpu.SemaphoreType.DMA((n,)), pltpu.SemaphoreType.DMA((n,))],
    )(*grads)


def _pair_sum(g, r, c_idx, *, name):
    _, _, h, C = g.shape
    tr = min(h, 256)

    def body(c_ref, g_ref, r_ref, o_ref):
        o_ref[...] = (g_ref[...].astype(F32) + r_ref[...].astype(F32)).astype(o_ref.dtype)

    return pl.pallas_call(
        body, name=name, out_shape=jax.ShapeDtypeStruct((N_CHIPS, h, C), g.dtype),
        grid_spec=pltpu.PrefetchScalarGridSpec(
            num_scalar_prefetch=1, grid=(N_CHIPS, h // tr),
            in_specs=[pl.BlockSpec((None, None, tr, C), lambda j, i, s: (j, s[0], i, 0)),
                      pl.BlockSpec((None, tr, C), lambda j, i, s: (j, i, 0))],
            out_specs=pl.BlockSpec((None, tr, C), lambda j, i, s: (j, i, 0))),
        compiler_params=_params("parallel", "parallel"),
    )(c_idx, g, r)


def _send_to_owners(sums):
    n = len(sums)

    def body(*refs):
        ins, outs = refs[:n], refs[n:2 * n]
        send_sem, recv_sem = refs[2 * n:]
        x, y, c, chips = _place()
        copies = [pltpu.make_async_remote_copy(
            src_ref=ins[i].at[2 * cx + cy], dst_ref=outs[i].at[k],
            send_sem=send_sem.at[i * 3 + k], recv_sem=recv_sem.at[i * 3 + k],
            device_id=(cx, cy, c), device_id_type=_MESH)
            for i in range(n) for k, (cx, cy) in enumerate(chips)]
        for cp in copies:
            cp.start()
        for cp in copies:
            cp.wait()

    return pl.pallas_call(
        body, name="grad_send_to_owners",
        out_shape=[jax.ShapeDtypeStruct((3,) + s.shape[1:], s.dtype) for s in sums],
        in_specs=[_ANY] * n, out_specs=[_ANY] * n,
        scratch_shapes=[pltpu.SemaphoreType.DMA((3 * n,)), pltpu.SemaphoreType.DMA((3 * n,))],
    )(*sums)


def _owner_sum(s, r, jc_idx, *, name):
    _, h, C = s.shape
    tr = min(h, 256)

    def body(jc_ref, s_ref, r_ref, o_ref):
        acc = s_ref[...].astype(F32)
        for k in range(3):
            acc = acc + r_ref[k].astype(F32)
        o_ref[...] = acc

    return pl.pallas_call(
        body, name=name, out_shape=jax.ShapeDtypeStruct((2, h, C), F32),
        grid_spec=pltpu.PrefetchScalarGridSpec(
            num_scalar_prefetch=1, grid=(h // tr,),
            in_specs=[pl.BlockSpec((None, tr, C), lambda i, s: (s[0], i, 0)),
                      pl.BlockSpec((3, tr, C), lambda i, s: (0, i, 0))],
            out_specs=pl.BlockSpec((None, tr, C), lambda i, s: (s[1], i, 0))),
        compiler_params=_params("parallel"),
    )(jc_idx, s, r)


def _share_with_sibling(bufs):
    n = len(bufs)

    def body(*refs):
        ins, outs = refs[:n], refs[n:2 * n]
        send_sem, recv_sem = refs[2 * n:]
        x, y, c, _ = _place()

        def copy(i, half):
            return pltpu.make_async_remote_copy(
                src_ref=ins[i].at[half], dst_ref=outs[i].at[half],
                send_sem=send_sem.at[i], recv_sem=recv_sem.at[i],
                device_id=(x, y, 1 - c), device_id_type=_MESH)

        copies = [copy(i, c) for i in range(n)]
        for cp in copies:
            cp.start()
        for i in range(n):
            copy(i, 1 - c).wait_recv()
        for cp in copies:
            cp.wait_send()

    return pl.pallas_call(
        body, name="grad_share_with_sibling",
        out_shape=[jax.ShapeDtypeStruct(b.shape, b.dtype) for b in bufs],
        in_specs=[_ANY] * n, out_specs=[_ANY] * n,
        input_output_aliases={i: i for i in range(n)},
        scratch_shapes=[pltpu.SemaphoreType.DMA((n,)), pltpu.SemaphoreType.DMA((n,))],
    )(*bufs)


def _reduce_scatter_chips(grads, c_idx, jc_idx):
    views = [g.reshape(N_CHIPS, 2, g.shape[1] // 2, g.shape[2]) for g in grads]
    arrived = _swap_halves(views)
    sums = [_pair_sum(v, r, c_idx, name=f"grad_pair_sum_{i}") for i, (v, r) in enumerate(zip(views, arrived))]
    arrived = _send_to_owners(sums)
    halves = [_owner_sum(s, r, jc_idx, name=f"grad_owner_sum_{i}") for i, (s, r) in enumerate(zip(sums, arrived))]
    full = _share_with_sibling(halves)
    return [f.reshape(g.shape[1], g.shape[2]) for f, g in zip(full, grads)]


def _all_reduce_small(buf):
    R = buf.shape[0]
    n_dev = 2 * N_CHIPS

    def body(x_ref, o_ref, all_ref, send_sems, recv_sems, local_sem):
        x, y, c, chips = _place()
        me, sib = (x, y, c), (x, y, 1 - c)

        def rows(px, py, pc):
            return all_ref.at[4 * px + 2 * py + pc]

        def copy(k, block, to, src=None):
            return pltpu.make_async_remote_copy(
                src_ref=rows(*block) if src is None else src, dst_ref=rows(*block),
                send_sem=send_sems.at[k], recv_sem=recv_sems.at[k], device_id=to, device_id_type=_MESH)

        mine = pltpu.make_async_copy(x_ref, rows(*me), local_sem)
        mine.start()
        first = [copy(0, me, sib, src=x_ref)]
        first += [copy(1 + j, me, (*chip, c), src=x_ref) for j, chip in enumerate(chips)]
        for cp in first:
            cp.start()
        passed = [copy(4 + j, (*chip, c), sib) for j, chip in enumerate(chips)]
        for j, chip in enumerate(chips):
            copy(1 + j, (*chip, c), me).wait_recv()
            passed[j].start()
        copy(0, sib, me).wait_recv()
        for j, chip in enumerate(chips):
            copy(4 + j, (*chip, 1 - c), me).wait_recv()
        for cp in first + passed:
            cp.wait_send()
        mine.wait()
        acc = all_ref[0]
        for d in range(1, n_dev):
            acc = acc + all_ref[d]
        o_ref[...] = acc

    vmem = pl.BlockSpec(memory_space=pltpu.VMEM)
    return pl.pallas_call(
        body, name="all_reduce_small", out_shape=jax.ShapeDtypeStruct((R, LANES), F32),
        in_specs=[vmem], out_specs=vmem,
        scratch_shapes=[pltpu.VMEM((n_dev, R, LANES), F32),
                        pltpu.SemaphoreType.DMA((7,)), pltpu.SemaphoreType.DMA((7,)), pltpu.SemaphoreType.DMA],
        compiler_params=_params(),
    )(buf)


def _adamw(w, g, m, v, *, name):
    R, C = w.shape
    whole_fits = 7 * 2 * R * C * 4 <= VMEM_LIMIT_BYTES // 2
    tr = R if whole_fits else next(c for c in (256, 192, 128, 64, 32, 16, 8) if R % c == 0)

    def body(w_ref, g_ref, m_ref, v_ref, d_ref, nm_ref, nv_ref):
        g = g_ref[...]
        m = ADAM_B1 * m_ref[...] + (1.0 - ADAM_B1) * g
        v = ADAM_B2 * v_ref[...] + (1.0 - ADAM_B2) * (g * g)
        nm_ref[...] = m
        nv_ref[...] = v
        m_hat = m / (1.0 - ADAM_B1 ** ADAM_STEP)
        v_hat = v / (1.0 - ADAM_B2 ** ADAM_STEP)
        d_ref[...] = -ADAM_LR * (m_hat / (jnp.sqrt(v_hat) + ADAM_EPS) + ADAM_WD * w_ref[...])

    blk = pl.BlockSpec((tr, C), lambda i: (i, 0))
    sds = jax.ShapeDtypeStruct((R, C), F32)
    return pl.pallas_call(
        body, name=name, out_shape=(sds, sds, sds), grid=(R // tr,),
        in_specs=[blk] * 4, out_specs=(blk, blk, blk),
        compiler_params=_params("parallel"),
    )(w, g, m, v)


_TILE = SUBLANES * LANES


def _pack(arrays):
    rows = []
    for a in arrays:
        flat = a.reshape(-1)
        flat = jnp.pad(flat, (0, (-flat.shape[0]) % _TILE))
        rows.append(flat.reshape(-1, LANES))
    return jnp.concatenate(rows, axis=0)


def _unpack(buf, shapes):
    out, r = [], 0
    for s in shapes:
        size = math.prod(s)
        nr = -(-size // _TILE) * SUBLANES
        out.append(buf[r:r + nr].reshape(-1)[:size].reshape(s))
        r += nr
    return out


_BIG = ("w_in_a", "w_glu", "w_kv", "w_in_b", "w_mem_kv", "w_out")
_REPLICATED = ("pre_norm_g", "post_norm_g", "lam_re", "lam_im", "log_step", "b_re", "b_im", "c_re", "c_im",
               "kv_norm_g", "b_fgate", "mem_norm_g")
_SHARDED_SMALL = ("d_skip", "b_glu", "w_fgate")
_WEIGHTS = ("pre_norm_g", "post_norm_g", "w_in_a", "lam_re", "lam_im", "log_step", "b_re", "b_im", "c_re",
            "c_im", "d_skip", "w_glu", "b_glu", "kv_norm_g", "w_kv", "w_fgate", "b_fgate", "w_in_b",
            "mem_norm_g", "w_mem_kv", "w_out")


def _halves(a):
    return a.reshape(2, a.shape[0] // 2, a.shape[1])


def _unhalve(a):
    return a.reshape(N_CHIPS, 2 * a.shape[2], a.shape[3])


def _columns(a):
    return jnp.transpose(a, (1, 0, 2)).reshape(a.shape[1], N_CHIPS * a.shape[2])


def kernel(x, mem, pre_norm_g, post_norm_g, w_in_a, lam_re, lam_im, log_step, b_re, b_im, c_re, c_im, d_skip, w_glu, b_glu, kv_norm_g, w_kv, w_fgate, b_fgate, w_in_b, mem_norm_g, w_mem_kv, w_out, loss_target, m_pre_norm_g, m_post_norm_g, m_w_in_a, m_lam_re, m_lam_im, m_log_step, m_b_re, m_b_im, m_c_re, m_c_im, m_d_skip, m_w_glu, m_b_glu, m_kv_norm_g, m_w_kv, m_w_fgate, m_b_fgate, m_w_in_b, m_mem_norm_g, m_w_mem_kv, m_w_out, v_pre_norm_g, v_post_norm_g, v_w_in_a, v_lam_re, v_lam_im, v_log_step, v_b_re, v_b_im, v_c_re, v_c_im, v_d_skip, v_w_glu, v_b_glu, v_kv_norm_g, v_w_kv, v_w_fgate, v_b_fgate, v_w_in_b, v_mem_norm_g, v_w_mem_kv, v_w_out):
    a = dict(locals())
    xi, yi, ci = lax.axis_index("x"), lax.axis_index("y"), lax.axis_index("c")
    chip = 2 * xi + yi
    c_idx = jnp.reshape(ci, (1,)).astype(jnp.int32)
    jc_idx = jnp.stack([chip, ci]).astype(jnp.int32)

    vec = jnp.zeros((2 * SUBLANES, MAIN_WIDTH // N_CHIPS), F32)
    vec = vec.at[0].set(a["d_skip"][0]).at[1].set(a["b_glu"][0])
    parts = [a["w_in_a"][0].astype(BF16), a["w_glu"][0].astype(BF16), a["w_kv"].astype(BF16),
             _pad_lanes(a["w_fgate"]).astype(BF16), a["w_in_b"][0].astype(BF16),
             a["w_mem_kv"].reshape(-1, a["w_mem_kv"].shape[2]).astype(BF16),
             a["w_out"].reshape(-1, D_MODEL).astype(BF16), vec]
    parts = [_halves(p) for p in parts]
    gat = [lax.dynamic_update_index_in_dim(g, p, chip, 0) for g, p in zip(_all_gather_chips(parts), parts)]
    w_in_a, w_glu, w_kv, w_fg, w_in_b, w_mk, w_out, vecs = gat
    w = dict(
        w_in_a=_columns(_unhalve(w_in_a)), w_glu=w_glu.reshape(MAIN_WIDTH, MAIN_WIDTH),
        w_kv=_columns(_unhalve(w_kv)), w_fgate=w_fg.reshape(D_MODEL, LANES), w_in_b=_columns(_unhalve(w_in_b)),
        w_mem_kv=jnp.transpose(w_mk, (1, 0, 2, 3)).reshape(2, D_MODEL, 2 * MEM_WIDTH),
        w_out=jnp.transpose(w_out, (1, 0, 2, 3)).reshape(2, D_MODEL, D_MODEL),
        d_skip=vecs[:, 0, 0, :].reshape(MAIN_WIDTH), b_glu=vecs[:, 0, 1, :].reshape(MAIN_WIDTH),
        pre_norm_g=a["pre_norm_g"], post_norm_g=a["post_norm_g"], mem_norm_g=a["mem_norm_g"],
        kv_norm_g=a["kv_norm_g"], b_fgate=a["b_fgate"],
        lam_re=a["lam_re"][0], lam_im=a["lam_im"][0], log_step=a["log_step"][0],
        b_re=a["b_re"][0], b_im=a["b_im"][0], c_re=a["c_re"][0], c_im=a["c_im"][0])

    loss_row, grad_x, g = _local_step(a["x"][0], a["mem"][0], a["loss_target"][0], w)
    loss = lax.psum(jnp.sum(loss_row), MESH_AXES)

    big = [g["w_in_a"], g["w_glu"].reshape(N_CHIPS, -1, MAIN_WIDTH), g["w_kv"], g["w_in_b"],
           g["w_mem_kv_0"].reshape(N_CHIPS, -1, 2 * MEM_WIDTH), g["w_mem_kv_1"].reshape(N_CHIPS, -1, 2 * MEM_WIDTH),
           g["w_out_0"].reshape(N_CHIPS, -1, D_MODEL), g["w_out_1"].reshape(N_CHIPS, -1, D_MODEL)]
    r_in_a, r_glu, r_kv, r_in_b, r_mk0, r_mk1, r_out0, r_out1 = _reduce_scatter_chips(big, c_idx, jc_idx)
    grads = {"w_in_a": r_in_a[None], "w_glu": r_glu[None], "w_kv": r_kv, "w_in_b": r_in_b[None],
             "w_mem_kv": jnp.stack([r_mk0, r_mk1]), "w_out": jnp.stack([r_out0, r_out1])}

    small_names = _REPLICATED + _SHARDED_SMALL
    small = _all_reduce_small(_pack([g[n] for n in small_names]))
    small = dict(zip(small_names, _unpack(small, [g[n].shape for n in small_names])))
    for n in _REPLICATED:
        grads[n] = small[n].reshape(a[n].shape)
    nd = MAIN_WIDTH // N_CHIPS
    grads["d_skip"] = lax.dynamic_slice(small["d_skip"], (chip * nd,), (nd,))[None]
    grads["b_glu"] = lax.dynamic_slice(small["b_glu"], (chip * nd,), (nd,))[None]
    nf = D_MODEL // N_CHIPS
    grads["w_fgate"] = lax.dynamic_slice(small["w_fgate"], (chip * nf, 0), (nf, FOX_HEADS))

    delta, new_m, new_v = {}, {}, {}
    for n in _BIG:
        shape = a[n].shape
        d2 = (-1, shape[-1])
        d, m, v = _adamw(a[n].reshape(d2), grads[n].reshape(d2), a["m_" + n].reshape(d2),
                         a["v_" + n].reshape(d2), name="adamw_" + n)
        delta[n], new_m[n], new_v[n] = d.reshape(shape), m.reshape(shape), v.reshape(shape)
    shapes = [a[n].shape for n in small_names]
    d, m, v = _adamw(_pack([a[n] for n in small_names]), _pack([grads[n] for n in small_names]),
                     _pack([a["m_" + n] for n in small_names]), _pack([a["v_" + n] for n in small_names]),
                     name="adamw_small")
    for n, dd, mm, vv in zip(small_names, _unpack(d, shapes), _unpack(m, shapes), _unpack(v, shapes)):
        delta[n], new_m[n], new_v[n] = dd, mm, vv

    return (loss, grad_x[None], *[grads[n] for n in _WEIGHTS], *[delta[n] for n in _WEIGHTS],
            *[new_m[n] for n in _WEIGHTS], *[new_v[n] for n in _WEIGHTS])
```

```python
import functools
import math

import jax
import jax.numpy as jnp
from jax import lax
from jax.experimental import pallas as pl
from jax.experimental.pallas import tpu as pltpu

F32 = jnp.float32
BF16 = jnp.bfloat16

D_MODEL = 2048
N_MEM = 256
MAIN_WIDTH = 1536
MEM_WIDTH = 512
IN_WIDTH = 2 * MAIN_WIDTH + 2 * MEM_WIDTH
HEAD_DIM = 128
FOX_HEADS = MAIN_WIDTH // HEAD_DIM
MEM_HEADS = MEM_WIDTH // HEAD_DIM
SSM_GROUP = 16
SSM_GROUPS = MAIN_WIDTH // SSM_GROUP
SSM_STATE = 64
GROUPS_PER_BLOCK = 8
SSM_BLOCKS = SSM_GROUPS // GROUPS_PER_BLOCK
STATE_COLS = GROUPS_PER_BLOCK * SSM_STATE
EPS = 1e-6
ADAM_LR = 0.001
ADAM_B1 = 0.9
ADAM_B2 = 0.999
ADAM_EPS = 1e-08
ADAM_WD = 0.01
ADAM_STEP = 10
N_CHIPS = 4
LANES = 128
SUBLANES = 8
VMEM_LIMIT_BYTES = 56 * 1024 * 1024
NEG_BIG = -1e30
MESH_AXES = ("x", "y", "c")


def _params(*sem):
    return pltpu.CompilerParams(dimension_semantics=sem if sem else None,
                                vmem_limit_bytes=VMEM_LIMIT_BYTES)


def _sigmoid(x):
    return 1.0 / (1.0 + jnp.exp(-x))


def _gelu(x):
    c = math.sqrt(2.0 / math.pi)
    return 0.5 * x * (1.0 + jnp.tanh(c * (x + 0.044715 * (x * x * x))))


def _gelu_grad(x):
    c = math.sqrt(2.0 / math.pi)
    t = jnp.tanh(c * (x + 0.044715 * (x * x * x)))
    return 0.5 * (1.0 + t) + 0.5 * x * (1.0 - t * t) * (c * (1.0 + 3.0 * 0.044715 * (x * x)))


def _silu_and_grad(z):
    s = _sigmoid(z)
    return z * s, s * (1.0 + z * (1.0 - s))


_TILE_CHOICES = (2048, 1024, 768, 512, 384, 256, LANES)


def _tile(n, cap):
    return next(c for c in _TILE_CHOICES if c <= cap and n % c == 0)


def _mm(a, b, *, name, ta=False, tb=False, out_dtype=F32, shards=1, tm=1024, tn=1024, tk=2048):
    if ta:
        K, M = a.shape
    else:
        M, K = a.shape
    if tb:
        N, kb = b.shape
    else:
        kb, N = b.shape
    assert K == kb, (a.shape, b.shape)
    ns = N // shards
    tm, tn, tk = _tile(M, tm), _tile(ns, tn), _tile(K, tk)
    assert M % tm == 0 and ns % tn == 0 and K % tk == 0 and N % shards == 0
    nk = K // tk
    dn = (((0 if ta else 1,), (1 if tb else 0,)), ((), ()))

    def body(a_ref, b_ref, o_ref, acc_ref):
        k = pl.program_id(2)

        @pl.when(k == 0)
        def _():
            acc_ref[...] = jnp.zeros_like(acc_ref)

        acc_ref[...] += lax.dot_general(a_ref[...].astype(BF16), b_ref[...].astype(BF16), dn,
                                        preferred_element_type=F32)

        @pl.when(k == nk - 1)
        def _():
            o_ref[...] = acc_ref[...].astype(o_ref.dtype)

    a_spec = (pl.BlockSpec((tk, tm), lambda i, j, k: (k, i)) if ta
              else pl.BlockSpec((tm, tk), lambda i, j, k: (i, k)))
    b_spec = (pl.BlockSpec((tn, tk), lambda i, j, k: (j, k)) if tb
              else pl.BlockSpec((tk, tn), lambda i, j, k: (k, j)))
    if shards == 1:
        out_shape = jax.ShapeDtypeStruct((M, N), out_dtype)
        o_spec = pl.BlockSpec((tm, tn), lambda i, j, k: (i, j))
    else:
        nb = ns // tn
        out_shape = jax.ShapeDtypeStruct((shards, M, ns), out_dtype)
        o_spec = pl.BlockSpec((None, tm, tn), lambda i, j, k: (j // nb, i, j % nb))
    return pl.pallas_call(
        body, name=name, out_shape=out_shape,
        grid=(M // tm, N // tn, nk),
        in_specs=[a_spec, b_spec], out_specs=o_spec,
        scratch_shapes=[pltpu.VMEM((tm, tn), F32)],
        compiler_params=_params("parallel", "parallel", "arbitrary"),
    )(a, b)


def _rmsnorm_fwd(x, g, *, name, res=None, out_dtype=F32, tr=256):
    L, D = x.shape
    tr = min(tr, L)
    has_res = res is not None

    def body(*refs):
        if has_res:
            x_ref, g_ref, r_ref, o_ref = refs
        else:
            x_ref, g_ref, o_ref = refs
        xf = x_ref[...]
        r = lax.rsqrt(jnp.mean(xf * xf, axis=-1, keepdims=True) + EPS)
        y = xf * r * g_ref[...]
        if has_res:
            y = r_ref[...] + y
        o_ref[...] = y.astype(o_ref.dtype)

    row = pl.BlockSpec((tr, D), lambda i: (i, 0))
    vec = pl.BlockSpec((1, D), lambda i: (0, 0))
    ins = [x, g.reshape(1, D)] + ([res] if has_res else [])
    return pl.pallas_call(
        body, name=name, out_shape=jax.ShapeDtypeStruct((L, D), out_dtype),
        grid=(L // tr,), in_specs=[row, vec] + ([row] if has_res else []), out_specs=row,
        compiler_params=_params("parallel"),
    )(*ins)


def _rmsnorm_bwd(x, g, dy, *, name, adds=(), dx_dtype=F32, tr=256):
    L, D = x.shape
    tr = min(tr, L)
    dys = dy if isinstance(dy, tuple) else (dy,)
    n_dy, n_add = len(dys), len(adds)

    def body(*refs):
        x_ref, g_ref = refs[:2]
        dy_refs = refs[2:2 + n_dy]
        add_refs = refs[2 + n_dy:2 + n_dy + n_add]
        dx_ref, dg_ref = refs[2 + n_dy + n_add:]
        xf = x_ref[...]
        dyf = dy_refs[0][...].astype(F32)
        for d_ref in dy_refs[1:]:
            dyf = dyf + d_ref[...].astype(F32)
        r = lax.rsqrt(jnp.mean(xf * xf, axis=-1, keepdims=True) + EPS)
        gy = dyf * g_ref[...]
        c = jnp.mean(xf * gy, axis=-1, keepdims=True) * (r * r * r)
        dx = gy * r - xf * c
        for a_ref in add_refs:
            dx = dx + a_ref[...].astype(F32)
        dx_ref[...] = dx.astype(dx_ref.dtype)

        @pl.when(pl.program_id(0) == 0)
        def _():
            dg_ref[...] = jnp.zeros_like(dg_ref)

        dg_ref[...] += jnp.sum(dyf * xf * r, axis=0, keepdims=True)

    row = pl.BlockSpec((tr, D), lambda i: (i, 0))
    vec = pl.BlockSpec((1, D), lambda i: (0, 0))
    dx, dg = pl.pallas_call(
        body, name=name,
        out_shape=(jax.ShapeDtypeStruct((L, D), dx_dtype), jax.ShapeDtypeStruct((1, D), F32)),
        grid=(L // tr,), in_specs=[row, vec] + [row] * (n_dy + n_add), out_specs=(row, vec),
        compiler_params=_params("arbitrary"),
    )(x, g.reshape(1, D), *dys, *adds)
    return dx, dg.reshape(D)


def _loss_head(h, target, *, tr=256):
    L, D = h.shape
    tr = min(tr, L)

    def body(h_ref, t_ref, dh_ref, loss_ref):
        e = h_ref[...] - t_ref[...]
        dh_ref[...] = e * (1.0 / D)

        @pl.when(pl.program_id(0) == 0)
        def _():
            loss_ref[...] = jnp.zeros_like(loss_ref)

        loss_ref[...] += jnp.sum(e * e, axis=0, keepdims=True) * (0.5 / D)

    row = pl.BlockSpec((tr, D), lambda i: (i, 0))
    vec = pl.BlockSpec((1, D), lambda i: (0, 0))
    dh, lp = pl.pallas_call(
        body, name="loss_head",
        out_shape=(jax.ShapeDtypeStruct((L, D), F32), jax.ShapeDtypeStruct((1, D), F32)),
        grid=(L // tr,), in_specs=[row, row], out_specs=(row, vec),
        compiler_params=_params("arbitrary"),
    )(h, target)
    return dh, lp


def _s5_coeffs(lr, li, ls):
    dt = jnp.exp(ls)
    mag = jnp.exp(lr * dt)
    ar = mag * jnp.cos(li * dt)
    ai = mag * jnp.sin(li * dt)
    den = lr * lr + li * li
    cr = ((ar - 1.0) * lr + ai * li) / den
    ci = (ai * lr - (ar - 1.0) * li) / den
    return dt, ar, ai, den, cr, ci


def _s5_prep(lam_re, lam_im, log_step, b_re_t, b_im_t):
    G, P = lam_re.shape
    H = b_re_t.shape[1]

    def body(lr_ref, li_ref, ls_ref, br_ref, bi_ref, ar_ref, ai_ref, bbr_ref, bbi_ref):
        _, ar, ai, _, cr, ci = _s5_coeffs(lr_ref[...], li_ref[...], ls_ref[...])
        ar_ref[...] = ar
        ai_ref[...] = ai
        br, bi = br_ref[...], bi_ref[...]
        crb, cib = cr[:, None, :], ci[:, None, :]
        bbr_ref[...] = crb * br - cib * bi
        bbi_ref[...] = crb * bi + cib * br

    return pl.pallas_call(
        body, name="s5_prep",
        out_shape=(jax.ShapeDtypeStruct((G, P), F32), jax.ShapeDtypeStruct((G, P), F32),
                   jax.ShapeDtypeStruct((G, H, P), F32), jax.ShapeDtypeStruct((G, H, P), F32)),
        compiler_params=_params(),
    )(lam_re, lam_im, log_step.reshape(G, 1), b_re_t, b_im_t)


def _s5_prep_bwd(lam_re, lam_im, log_step, b_re_t, b_im_t, d_ar, d_ai, d_bbr, d_bbi):
    G, P = lam_re.shape
    H = b_re_t.shape[1]

    def body(lr_ref, li_ref, ls_ref, br_ref, bi_ref, dar_ref, dai_ref, dbbr_ref, dbbi_ref,
             dlr_ref, dli_ref, dls_ref, dbr_ref, dbi_ref):
        lr, li = lr_ref[...], li_ref[...]
        dt, ar, ai, den, cr, ci = _s5_coeffs(lr, li, ls_ref[...])
        br, bi = br_ref[...], bi_ref[...]
        gbr, gbi = dbbr_ref[...], dbbi_ref[...]
        crb, cib = cr[:, None, :], ci[:, None, :]
        dbr_ref[...] = crb * gbr + cib * gbi
        dbi_ref[...] = crb * gbi - cib * gbr
        gcr = jnp.sum(br * gbr + bi * gbi, axis=1)
        gci = jnp.sum(br * gbi - bi * gbr, axis=1)
        ilr, ili = lr / den, -li / den
        gar = dar_ref[...] + (ilr * gcr + ili * gci)
        gai = dai_ref[...] + (ilr * gci - ili * gcr)
        qr, qi = cr * ilr - ci * ili, cr * ili + ci * ilr
        glr = -(qr * gcr + qi * gci)
        gli = -(qr * gci - qi * gcr)
        glr = glr + dt * (ar * gar + ai * gai)
        gli = gli + dt * (ar * gai - ai * gar)
        wr, wi = lr * ar - li * ai, lr * ai + li * ar
        gdt = jnp.sum(wr * gar + wi * gai, axis=1, keepdims=True)
        dlr_ref[...] = glr
        dli_ref[...] = gli
        dls_ref[...] = gdt * dt

    return pl.pallas_call(
        body, name="s5_prep_bwd",
        out_shape=(jax.ShapeDtypeStruct((G, P), F32), jax.ShapeDtypeStruct((G, P), F32),
                   jax.ShapeDtypeStruct((G, 1), F32),
                   jax.ShapeDtypeStruct((G, H, P), F32), jax.ShapeDtypeStruct((G, H, P), F32)),
        compiler_params=_params(),
    )(lam_re, lam_im, log_step.reshape(G, 1), b_re_t, b_im_t, d_ar, d_ai, d_bbr, d_bbi)


def _s5_block_mats(bbr_t, bbi_t, c_re, c_im):
    bmat = _s5_expand(bbr_t, bbi_t)
    cmat = jnp.transpose(_s5_expand(c_re, -c_im), (0, 2, 1))
    return bmat.astype(BF16), cmat.astype(BF16)


def _s5_diag_mask():
    r = lax.broadcasted_iota(jnp.int32, (LANES, 2 * STATE_COLS), 0) // SSM_GROUP
    c = (lax.broadcasted_iota(jnp.int32, (LANES, 2 * STATE_COLS), 1) % STATE_COLS) // SSM_STATE
    return (r == c).astype(F32)


def _s5_expand(re, im):
    re = jnp.tile(re.reshape(SSM_BLOCKS, LANES, SSM_STATE), (1, 1, GROUPS_PER_BLOCK))
    im = jnp.tile(im.reshape(SSM_BLOCKS, LANES, SSM_STATE), (1, 1, GROUPS_PER_BLOCK))
    return jnp.concatenate([re, im], axis=-1) * _s5_diag_mask()[None]


def _s5_block_diag(dmat):
    d = dmat * _s5_diag_mask()[None]
    parts = []
    for ri in range(2):
        acc = 0.0
        for g in range(GROUPS_PER_BLOCK):
            c0 = ri * STATE_COLS + g * SSM_STATE
            acc = acc + d[:, :, c0:c0 + SSM_STATE]
        parts.append(acc.reshape(SSM_GROUPS, SSM_GROUP, SSM_STATE))
    return jnp.stack(parts)


def _s5_a_rows(ar, ai):
    a = jnp.concatenate([ar.reshape(SSM_BLOCKS, STATE_COLS), ai.reshape(SSM_BLOCKS, STATE_COLS)], axis=1)
    return jnp.broadcast_to(a[:, None, :], (SSM_BLOCKS, SUBLANES, 2 * STATE_COLS))


def _s5_fwd(proj, bmat, cmat, a_rows, d_skip, *, tc=512):
    L = proj.shape[0]
    tc = min(tc, L)
    nt = L // tc
    n8 = tc // SUBLANES
    S = STATE_COLS

    def body(u_ref, b_ref, c_ref, a_ref, d_ref, y_ref, yg_ref, xp_ref, bu_s, xp_s, carry_s):
        @pl.when(pl.program_id(1) == 0)
        def _():
            carry_s[...] = jnp.zeros_like(carry_s)

        u = u_ref[...]
        bu = jnp.dot(u.astype(BF16), b_ref[...], preferred_element_type=F32)
        bu_s[...] = bu.reshape(n8, SUBLANES, 2 * S)
        ar, ai = a_ref[0:1, :S], a_ref[0:1, S:]

        def step(i, carry):
            cr, ci = carry
            for j in range(SUBLANES):
                xp_s[i, j:j + 1, :S] = cr
                xp_s[i, j:j + 1, S:] = ci
                br = bu_s[i, j:j + 1, :S]
                bi = bu_s[i, j:j + 1, S:]
                cr, ci = ar * cr - ai * ci + br, ar * ci + ai * cr + bi
            return cr, ci

        cr, ci = lax.fori_loop(0, n8, step, (carry_s[0:1, :S], carry_s[0:1, S:]))
        carry_s[0:1, :S] = cr
        carry_s[0:1, S:] = ci
        xp = xp_s[...].reshape(tc, 2 * S)
        xp_ref[...] = xp
        x_re = ar * xp[:, :S] - ai * xp[:, S:] + bu[:, :S]
        x_im = ar * xp[:, S:] + ai * xp[:, :S] + bu[:, S:]
        xs = jnp.concatenate([x_re, x_im], axis=1).astype(BF16)
        y = jnp.dot(xs, c_ref[...], preferred_element_type=F32) + d_ref[...] * u
        y_ref[...] = y
        yg_ref[...] = _gelu(y).astype(BF16)

    return pl.pallas_call(
        body, name="s5_fwd",
        out_shape=(jax.ShapeDtypeStruct((L, MAIN_WIDTH), F32),
                   jax.ShapeDtypeStruct((L, MAIN_WIDTH), BF16),
                   jax.ShapeDtypeStruct((L, SSM_BLOCKS * 2 * S), F32)),
        grid=(SSM_BLOCKS, nt),
        in_specs=[pl.BlockSpec((tc, LANES), lambda b, t: (t, b)),
                  pl.BlockSpec((None, LANES, 2 * S), lambda b, t: (b, 0, 0)),
                  pl.BlockSpec((None, 2 * S, LANES), lambda b, t: (b, 0, 0)),
                  pl.BlockSpec((None, SUBLANES, 2 * S), lambda b, t: (b, 0, 0)),
                  pl.BlockSpec((1, LANES), lambda b, t: (0, b))],
        out_specs=(pl.BlockSpec((tc, LANES), lambda b, t: (t, b)),
                   pl.BlockSpec((tc, LANES), lambda b, t: (t, b)),
                   pl.BlockSpec((tc, 2 * S), lambda b, t: (t, b))),
        scratch_shapes=[pltpu.VMEM((n8, SUBLANES, 2 * S), F32),
                        pltpu.VMEM((n8, SUBLANES, 2 * S), F32),
                        pltpu.VMEM((SUBLANES, 2 * S), F32)],
        compiler_params=_params("parallel", "arbitrary"),
    )(proj, bmat, cmat, a_rows, d_skip.reshape(1, MAIN_WIDTH))


def _s5_bwd(proj, dy, xp, bmat, cmat, a_rows, d_skip, *, tc=512):
    L = proj.shape[0]
    tc = min(tc, L)
    nt = L // tc
    n8 = tc // SUBLANES
    S = STATE_COLS
    nn = (((1,), (1,)), ((), ()))
    tn = (((0,), (0,)), ((), ()))

    def body(u_ref, dy_ref, xp_ref, b_ref, c_ref, a_ref, d_ref,
             du_ref, db_ref, dc_ref, da_ref, dd_ref, dl_s, carry_s):
        @pl.when(pl.program_id(1) == 0)
        def _():
            carry_s[...] = jnp.zeros_like(carry_s)
            db_ref[...] = jnp.zeros_like(db_ref)
            dc_ref[...] = jnp.zeros_like(dc_ref)
            da_ref[...] = jnp.zeros_like(da_ref)
            dd_ref[...] = jnp.zeros_like(dd_ref)

        u = u_ref[...]
        dy = dy_ref[...]
        xp = xp_ref[...]
        ub = u.astype(BF16)
        dyb = dy.astype(BF16)
        ar, ai = a_ref[0:1, :S], a_ref[0:1, S:]
        bu = jnp.dot(ub, b_ref[...], preferred_element_type=F32)
        x_re = ar * xp[:, :S] - ai * xp[:, S:] + bu[:, :S]
        x_im = ar * xp[:, S:] + ai * xp[:, :S] + bu[:, S:]
        xs = jnp.concatenate([x_re, x_im], axis=1).astype(BF16)
        dc_ref[...] += lax.dot_general(dyb, xs, tn, preferred_element_type=F32)
        dx = lax.dot_general(dyb, c_ref[...], nn, preferred_element_type=F32)
        dl_s[...] = dx.reshape(n8, SUBLANES, 2 * S)

        def step(k, carry):
            cr, ci = carry
            i = n8 - 1 - k
            for j in range(SUBLANES - 1, -1, -1):
                lr = dl_s[i, j:j + 1, :S] + (ar * cr + ai * ci)
                li = dl_s[i, j:j + 1, S:] + (ar * ci - ai * cr)
                dl_s[i, j:j + 1, :S] = lr
                dl_s[i, j:j + 1, S:] = li
                cr, ci = lr, li
            return cr, ci

        cr, ci = lax.fori_loop(0, n8, step, (carry_s[0:1, :S], carry_s[0:1, S:]))
        carry_s[0:1, :S] = cr
        carry_s[0:1, S:] = ci
        lam = dl_s[...].reshape(tc, 2 * S)
        l_re, l_im = lam[:, :S], lam[:, S:]
        da_ref[0:1, :S] += jnp.sum(l_re * xp[:, :S] + l_im * xp[:, S:], axis=0, keepdims=True)
        da_ref[0:1, S:] += jnp.sum(l_im * xp[:, :S] - l_re * xp[:, S:], axis=0, keepdims=True)
        lamb = lam.astype(BF16)
        du_ref[...] = lax.dot_general(lamb, b_ref[...], nn, preferred_element_type=F32) + d_ref[...] * dy
        db_ref[...] += lax.dot_general(ub, lamb, tn, preferred_element_type=F32)
        dd_ref[0:1, :] += jnp.sum(dy * u, axis=0, keepdims=True)

    rev = lambda b, t: (nt - 1 - t, b)
    return pl.pallas_call(
        body, name="s5_bwd",
        out_shape=(jax.ShapeDtypeStruct((L, MAIN_WIDTH), F32),
                   jax.ShapeDtypeStruct((SSM_BLOCKS, LANES, 2 * S), F32),
                   jax.ShapeDtypeStruct((SSM_BLOCKS, LANES, 2 * S), F32),
                   jax.ShapeDtypeStruct((SSM_BLOCKS, SUBLANES, 2 * S), F32),
                   jax.ShapeDtypeStruct((SUBLANES, MAIN_WIDTH), F32)),
        grid=(SSM_BLOCKS, nt),
        in_specs=[pl.BlockSpec((tc, LANES), rev),
                  pl.BlockSpec((tc, LANES), rev),
                  pl.BlockSpec((tc, 2 * S), rev),
                  pl.BlockSpec((None, LANES, 2 * S), lambda b, t: (b, 0, 0)),
                  pl.BlockSpec((None, 2 * S, LANES), lambda b, t: (b, 0, 0)),
                  pl.BlockSpec((None, SUBLANES, 2 * S), lambda b, t: (b, 0, 0)),
                  pl.BlockSpec((1, LANES), lambda b, t: (0, b))],
        out_specs=(pl.BlockSpec((tc, LANES), rev),
                   pl.BlockSpec((None, LANES, 2 * S), lambda b, t: (b, 0, 0)),
                   pl.BlockSpec((None, LANES, 2 * S), lambda b, t: (b, 0, 0)),
                   pl.BlockSpec((None, SUBLANES, 2 * S), lambda b, t: (b, 0, 0)),
                   pl.BlockSpec((SUBLANES, LANES), lambda b, t: (0, b))),
        scratch_shapes=[pltpu.VMEM((n8, SUBLANES, 2 * S), F32),
                        pltpu.VMEM((SUBLANES, 2 * S), F32)],
        compiler_params=_params("parallel", "arbitrary"),
    )(proj, dy, xp, bmat, cmat, a_rows, d_skip.reshape(1, MAIN_WIDTH))


def _row_specs(tr):
    main = pl.BlockSpec((tr, MAIN_WIDTH), lambda i: (i, 0))
    z = pl.BlockSpec((tr, MAIN_WIDTH), lambda i: (i, 1))
    zm = pl.BlockSpec((tr, MEM_WIDTH), lambda i: (i, IN_WIDTH // MEM_WIDTH - 1))
    mem = pl.BlockSpec((tr, MEM_WIDTH), lambda i: (i, 0))
    cat = pl.BlockSpec((tr, D_MODEL), lambda i: (i, 0))
    vec = pl.BlockSpec((1, MAIN_WIDTH), lambda i: (0, 0))
    return main, z, zm, mem, cat, vec


def _gate_a_fwd(y, t, b_glu, proj, o_mem, *, tr=256):
    L = y.shape[0]
    tr = min(tr, L)

    def body(y_ref, t_ref, b_ref, z_ref, zm_ref, om_ref, o_ref):
        yg = _gelu(y_ref[...])
        sz, _ = _silu_and_grad(z_ref[...])
        o_ref[:, :MAIN_WIDTH] = (yg * _sigmoid(t_ref[...] + b_ref[...]) * sz).astype(BF16)
        szm, _ = _silu_and_grad(zm_ref[...])
        o_ref[:, MAIN_WIDTH:] = (om_ref[...] * szm).astype(BF16)

    main, z, zm, mem, cat, vec = _row_specs(tr)
    return pl.pallas_call(
        body, name="gate_a_fwd", out_shape=jax.ShapeDtypeStruct((L, D_MODEL), BF16),
        grid=(L // tr,), in_specs=[main, main, vec, z, zm, mem], out_specs=cat,
        compiler_params=_params("parallel"),
    )(y, t, b_glu.reshape(1, MAIN_WIDTH), proj, proj, o_mem)


def _gate_a_bwd(dcat, y, t, b_glu, proj, o_mem, *, tr=256):
    L = y.shape[0]
    tr = min(tr, L)

    def body(dc_ref, y_ref, t_ref, b_ref, z_ref, zm_ref, om_ref,
             dz_ref, dzm_ref, dt_ref, dyg_ref, dom_ref, db_ref):
        dmain = dc_ref[:, :MAIN_WIDTH]
        dmemo = dc_ref[:, MAIN_WIDTH:]
        yg = _gelu(y_ref[...])
        sg = _sigmoid(t_ref[...] + b_ref[...])
        sz, gz = _silu_and_grad(z_ref[...])
        dz_ref[...] = (dmain * (yg * sg) * gz).astype(BF16)
        dy2 = dmain * sz
        dyg_ref[...] = dy2 * sg
        dt = dy2 * yg * (sg * (1.0 - sg))
        dt_ref[...] = dt.astype(BF16)

        @pl.when(pl.program_id(0) == 0)
        def _():
            db_ref[...] = jnp.zeros_like(db_ref)

        db_ref[...] += jnp.sum(dt, axis=0, keepdims=True)
        szm, gzm = _silu_and_grad(zm_ref[...])
        dom_ref[...] = dmemo * szm
        dzm_ref[...] = (dmemo * om_ref[...] * gzm).astype(BF16)

    main, z, zm, mem, cat, vec = _row_specs(tr)
    outs = pl.pallas_call(
        body, name="gate_a_bwd",
        out_shape=(jax.ShapeDtypeStruct((L, MAIN_WIDTH), BF16), jax.ShapeDtypeStruct((L, MEM_WIDTH), BF16),
                   jax.ShapeDtypeStruct((L, MAIN_WIDTH), BF16), jax.ShapeDtypeStruct((L, MAIN_WIDTH), F32),
                   jax.ShapeDtypeStruct((L, MEM_WIDTH), F32), jax.ShapeDtypeStruct((1, MAIN_WIDTH), F32)),
        grid=(L // tr,), in_specs=[cat, main, main, vec, z, zm, mem],
        out_specs=(main, mem, main, main, mem, vec),
        compiler_params=_params("arbitrary"),
    )(dcat, y, t, b_glu.reshape(1, MAIN_WIDTH), proj, proj, o_mem)
    return outs


def _gelu_bwd(dyg_a, dyg_b, y, *, tr=256):
    L = y.shape[0]
    tr = min(tr, L)

    def body(a_ref, b_ref, y_ref, o_ref):
        o_ref[...] = (a_ref[...] + b_ref[...]) * _gelu_grad(y_ref[...])

    main = pl.BlockSpec((tr, MAIN_WIDTH), lambda i: (i, 0))
    return pl.pallas_call(
        body, name="gelu_bwd", out_shape=jax.ShapeDtypeStruct((L, MAIN_WIDTH), F32),
        grid=(L // tr,), in_specs=[main, main, main], out_specs=main,
        compiler_params=_params("parallel"),
    )(dyg_a, dyg_b, y)


def _gate_b_fwd(att, proj, o_mem, *, tr=256):
    L = att.shape[0]
    tr = min(tr, L)

    def body(a_ref, z_ref, zm_ref, om_ref, o_ref):
        sz, _ = _silu_and_grad(z_ref[...])
        o_ref[:, :MAIN_WIDTH] = (a_ref[...] * sz).astype(BF16)
        szm, _ = _silu_and_grad(zm_ref[...])
        o_ref[:, MAIN_WIDTH:] = (om_ref[...] * szm).astype(BF16)

    main, z, zm, mem, cat, _ = _row_specs(tr)
    return pl.pallas_call(
        body, name="gate_b_fwd", out_shape=jax.ShapeDtypeStruct((L, D_MODEL), BF16),
        grid=(L // tr,), in_specs=[main, z, zm, mem], out_specs=cat,
        compiler_params=_params("parallel"),
    )(att, proj, proj, o_mem)


def _gate_b_bwd(dcat, att, proj, o_mem, *, tr=256):
    L = att.shape[0]
    tr = min(tr, L)

    def body(dc_ref, a_ref, z_ref, zm_ref, om_ref, da_ref, dz_ref, dom_ref, dzm_ref, dl_ref):
        dmain = dc_ref[:, :MAIN_WIDTH]
        dmemo = dc_ref[:, MAIN_WIDTH:]
        att = a_ref[...]
        sz, gz = _silu_and_grad(z_ref[...])
        datt = dmain * sz
        da_ref[...] = datt
        dz_ref[...] = (dmain * att * gz).astype(BF16)
        szm, gzm = _silu_and_grad(zm_ref[...])
        dom_ref[...] = dmemo * szm
        dzm_ref[...] = (dmemo * om_ref[...] * gzm).astype(BF16)
        prod = datt * att
        for h in range(FOX_HEADS):
            dl_ref[h] = jnp.sum(prod[:, h * HEAD_DIM:(h + 1) * HEAD_DIM], axis=1, keepdims=True)

    main, z, zm, mem, cat, _ = _row_specs(tr)
    delta = pl.BlockSpec((FOX_HEADS, tr, 1), lambda i: (0, i, 0))
    return pl.pallas_call(
        body, name="gate_b_bwd",
        out_shape=(jax.ShapeDtypeStruct((L, MAIN_WIDTH), F32), jax.ShapeDtypeStruct((L, MAIN_WIDTH), BF16),
                   jax.ShapeDtypeStruct((L, MEM_WIDTH), F32), jax.ShapeDtypeStruct((L, MEM_WIDTH), BF16),
                   jax.ShapeDtypeStruct((FOX_HEADS, L, 1), F32)),
        grid=(L // tr,), in_specs=[cat, main, z, zm, mem], out_specs=(main, main, mem, mem, delta),
        compiler_params=_params("parallel"),
    )(dcat, att, proj, proj, o_mem)


_MEM_Q_COL = (2 * MAIN_WIDTH) // HEAD_DIM
_NT = (((1,), (1,)), ((), ()))
_TN = (((0,), (0,)), ((), ()))


def _mem_probs(q_ref, k_ref):
    qs = (q_ref[...] * (HEAD_DIM ** -0.5)).astype(BF16)
    s = lax.dot_general(qs, k_ref[...].astype(BF16), _NT, preferred_element_type=F32)
    e = jnp.exp(s - jnp.max(s, axis=-1, keepdims=True))
    return qs, e / jnp.sum(e, axis=-1, keepdims=True)


def _mem_attn_fwd(proj, kvm, *, tq=512):
    L = proj.shape[0]
    tq = min(tq, L)

    def body(q_ref, k_ref, v_ref, o_ref):
        _, p = _mem_probs(q_ref, k_ref)
        o_ref[...] = jnp.dot(p.astype(BF16), v_ref[...].astype(BF16), preferred_element_type=F32)

    return pl.pallas_call(
        body, name="mem_attn_fwd", out_shape=jax.ShapeDtypeStruct((L, MEM_WIDTH), F32),
        grid=(MEM_HEADS, L // tq),
        in_specs=[pl.BlockSpec((tq, HEAD_DIM), lambda h, i: (i, _MEM_Q_COL + h)),
                  pl.BlockSpec((N_MEM, HEAD_DIM), lambda h, i: (0, h)),
                  pl.BlockSpec((N_MEM, HEAD_DIM), lambda h, i: (0, MEM_HEADS + h))],
        out_specs=pl.BlockSpec((tq, HEAD_DIM), lambda h, i: (i, h)),
        compiler_params=_params("parallel", "parallel"),
    )(proj, kvm, kvm)


def _mem_attn_bwd(proj, kvm, do, *, tq=512):
    L = proj.shape[0]
    tq = min(tq, L)

    def body(q_ref, k_ref, v_ref, do_ref, dq_ref, dk_ref, dv_ref):
        @pl.when(pl.program_id(1) == 0)
        def _():
            dk_ref[...] = jnp.zeros_like(dk_ref)
            dv_ref[...] = jnp.zeros_like(dv_ref)

        qs, p = _mem_probs(q_ref, k_ref)
        dob = do_ref[...].astype(BF16)
        dp = lax.dot_general(dob, v_ref[...].astype(BF16), _NT, preferred_element_type=F32)
        ds = p * (dp - jnp.sum(p * dp, axis=-1, keepdims=True))
        dsb = ds.astype(BF16)
        dq = jnp.dot(dsb, k_ref[...].astype(BF16), preferred_element_type=F32) * (HEAD_DIM ** -0.5)
        dq_ref[...] = dq.astype(BF16)
        dk_ref[...] += lax.dot_general(dsb, qs, _TN, preferred_element_type=F32)
        dv_ref[...] += lax.dot_general(p.astype(BF16), dob, _TN, preferred_element_type=F32)

    dq, dk, dv = pl.pallas_call(
        body, name="mem_attn_bwd",
        out_shape=(jax.ShapeDtypeStruct((L, MEM_WIDTH), BF16),
                   jax.ShapeDtypeStruct((N_MEM, MEM_WIDTH), F32),
                   jax.ShapeDtypeStruct((N_MEM, MEM_WIDTH), F32)),
        grid=(MEM_HEADS, L // tq),
        in_specs=[pl.BlockSpec((tq, HEAD_DIM), lambda h, i: (i, _MEM_Q_COL + h)),
                  pl.BlockSpec((N_MEM, HEAD_DIM), lambda h, i: (0, h)),
                  pl.BlockSpec((N_MEM, HEAD_DIM), lambda h, i: (0, MEM_HEADS + h)),
                  pl.BlockSpec((tq, HEAD_DIM), lambda h, i: (i, h))],
        out_specs=(pl.BlockSpec((tq, HEAD_DIM), lambda h, i: (i, h)),
                   pl.BlockSpec((N_MEM, HEAD_DIM), lambda h, i: (0, h)),
                   pl.BlockSpec((N_MEM, HEAD_DIM), lambda h, i: (0, h))),
        compiler_params=_params("parallel", "arbitrary"),
    )(proj, kvm, kvm, do)
    return dq, jnp.concatenate([dk, dv], axis=1)


def _tile_cumsum(x, row, reverse):
    for sh in (1, 2, 4):
        if reverse:
            x = x + jnp.where(row < SUBLANES - sh, pltpu.roll(x, SUBLANES - sh, 0), 0.0)
        else:
            x = x + jnp.where(row >= sh, pltpu.roll(x, sh, 0), 0.0)
    return x


def _fgate_fwd(pre, b_pad):
    L = pre.shape[0]
    n8 = L // SUBLANES

    def body(p_ref, b_ref, o_ref):
        row = lax.broadcasted_iota(jnp.int32, (SUBLANES, LANES), 0)
        b = b_ref[...]

        def step(i, carry):
            x = p_ref[i] + b
            logf = jnp.minimum(x, 0.0) - jnp.log(1.0 + jnp.exp(-jnp.abs(x)))
            t = _tile_cumsum(logf, row, False) + carry
            o_ref[i] = t
            return t[SUBLANES - 1:SUBLANES, :]

        lax.fori_loop(0, n8, step, jnp.zeros((1, LANES), F32))

    out = pl.pallas_call(
        body, name="fgate_fwd", out_shape=jax.ShapeDtypeStruct((n8, SUBLANES, LANES), F32),
        compiler_params=_params(),
    )(pre.reshape(n8, SUBLANES, LANES), b_pad.reshape(1, LANES))
    return out.reshape(L, LANES)


def _fgate_bwd(dfcum, pre, b_pad):
    L = pre.shape[0]
    n8 = L // SUBLANES

    def body(d_ref, p_ref, b_ref, o_ref, s_ref):
        row = lax.broadcasted_iota(jnp.int32, (SUBLANES, LANES), 0)
        b = b_ref[...]

        def step(k, carry):
            c, acc = carry
            i = n8 - 1 - k
            t = _tile_cumsum(d_ref[i], row, True) + c
            dpre = t * _sigmoid(-(p_ref[i] + b))
            o_ref[i] = dpre
            return t[0:1, :], acc + dpre

        _, acc = lax.fori_loop(0, n8, step, (jnp.zeros((1, LANES), F32), jnp.zeros((SUBLANES, LANES), F32)))
        s_ref[...] = jnp.sum(acc, axis=0, keepdims=True)

    dpre, db = pl.pallas_call(
        body, name="fgate_bwd",
        out_shape=(jax.ShapeDtypeStruct((n8, SUBLANES, LANES), F32), jax.ShapeDtypeStruct((1, LANES), F32)),
        compiler_params=_params(),
    )(dfcum.reshape(n8, SUBLANES, LANES), pre.reshape(n8, SUBLANES, LANES), b_pad.reshape(1, LANES))
    return dpre.reshape(L, LANES), db


FOX_BLOCK = 512


def _fox_scores(qs, k, fq, fk, diagonal):
    s = lax.dot_general(qs, k, _NT, preferred_element_type=F32) + fq - fk
    if diagonal:
        row = lax.broadcasted_iota(jnp.int32, s.shape, 0)
        col = lax.broadcasted_iota(jnp.int32, s.shape, 1)
        s = jnp.where(row >= col, s, NEG_BIG)
    return s


def _fox_specs(tq, L):
    nq = L // tq
    return dict(
        rows=lambda off: pl.BlockSpec((tq, HEAD_DIM), lambda h, i: (i, off + h)),
        seq=lambda off: pl.BlockSpec((L, HEAD_DIM), lambda h, i: (0, off + h)),
        col=pl.BlockSpec((None, None, tq, 1), lambda h, i: (h, i, 0, 0)),
        col_all=pl.BlockSpec((None, nq, tq, 1), lambda h, i: (h, 0, 0, 0)),
        row=pl.BlockSpec((None, None, 1, tq), lambda h, i: (h, i, 0, 0)),
        row_all=pl.BlockSpec((None, nq, 1, tq), lambda h, i: (h, 0, 0, 0)))


def _fox_fwd(proj, kv, fq, fk):
    L = proj.shape[0]
    tq = min(FOX_BLOCK, L)
    nq = L // tq
    sp = _fox_specs(tq, L)

    def body(q_ref, k_ref, v_ref, fq_ref, fk_ref, o_ref, lse_ref, m_s, l_s, acc_s):
        qi = pl.program_id(1)
        qs = (q_ref[...] * (HEAD_DIM ** -0.5)).astype(BF16)
        fq = fq_ref[...]
        m_s[...] = jnp.full_like(m_s, NEG_BIG)
        l_s[...] = jnp.zeros_like(l_s)
        acc_s[...] = jnp.zeros_like(acc_s)

        def block(j, diagonal):
            r0 = pl.multiple_of(j * tq, tq)
            s = _fox_scores(qs, k_ref[pl.ds(r0, tq), :], fq, fk_ref[j], diagonal)
            m_new = jnp.maximum(m_s[...], jnp.max(s, axis=-1, keepdims=True))
            alpha = jnp.exp(m_s[...] - m_new)
            p = jnp.exp(s - m_new)
            l_s[...] = alpha * l_s[...] + jnp.sum(p, axis=-1, keepdims=True)
            acc_s[...] = alpha * acc_s[...] + jnp.dot(p.astype(BF16), v_ref[pl.ds(r0, tq), :],
                                                      preferred_element_type=F32)
            m_s[...] = m_new

        def below(j, carry):
            block(j, False)
            return carry

        lax.fori_loop(0, qi, below, 0)
        block(qi, True)
        o_ref[...] = acc_s[...] / l_s[...]
        lse_ref[...] = m_s[...] + jnp.log(l_s[...])

    return pl.pallas_call(
        body, name="fox_fwd",
        out_shape=(jax.ShapeDtypeStruct((L, MAIN_WIDTH), F32),
                   jax.ShapeDtypeStruct((FOX_HEADS, nq, tq, 1), F32)),
        grid=(FOX_HEADS, nq),
        in_specs=[sp["rows"](0), sp["seq"](0), sp["seq"](FOX_HEADS), sp["col"], sp["row_all"]],
        out_specs=(sp["rows"](0), sp["col"]),
        scratch_shapes=[pltpu.VMEM((tq, 1), F32), pltpu.VMEM((tq, 1), F32), pltpu.VMEM((tq, HEAD_DIM), F32)],
        compiler_params=_params("parallel", "parallel"),
    )(proj, kv, kv, fq, fk)


def _fox_bwd_dq(proj, kv, fq, fk, lse, delta, datt):
    L = proj.shape[0]
    tq = min(FOX_BLOCK, L)
    nq = L // tq
    sp = _fox_specs(tq, L)

    def body(q_ref, k_ref, v_ref, fq_ref, fk_ref, lse_ref, dl_ref, do_ref, dq_ref, df_ref, acc_s, df_s):
        qi = pl.program_id(1)
        qs = (q_ref[...] * (HEAD_DIM ** -0.5)).astype(BF16)
        dob = do_ref[...].astype(BF16)
        fq, lse, dl = fq_ref[...], lse_ref[...], dl_ref[...]
        acc_s[...] = jnp.zeros_like(acc_s)
        df_s[...] = jnp.zeros_like(df_s)

        def block(j, diagonal):
            r0 = pl.multiple_of(j * tq, tq)
            k = k_ref[pl.ds(r0, tq), :]
            p = jnp.exp(_fox_scores(qs, k, fq, fk_ref[j], diagonal) - lse)
            dp = lax.dot_general(dob, v_ref[pl.ds(r0, tq), :], _NT, preferred_element_type=F32)
            ds = p * (dp - dl)
            acc_s[...] += jnp.dot(ds.astype(BF16), k, preferred_element_type=F32)
            df_s[...] += jnp.sum(ds, axis=1, keepdims=True)

        def below(j, carry):
            block(j, False)
            return carry

        lax.fori_loop(0, qi, below, 0)
        block(qi, True)
        dq_ref[...] = (acc_s[...] * (HEAD_DIM ** -0.5)).astype(BF16)
        df_ref[...] = df_s[...]

    return pl.pallas_call(
        body, name="fox_bwd_dq",
        out_shape=(jax.ShapeDtypeStruct((L, MAIN_WIDTH), BF16),
                   jax.ShapeDtypeStruct((FOX_HEADS, nq, tq, 1), F32)),
        grid=(FOX_HEADS, nq),
        in_specs=[sp["rows"](0), sp["seq"](0), sp["seq"](FOX_HEADS), sp["col"], sp["row_all"],
                  sp["col"], sp["col"], sp["rows"](0)],
        out_specs=(sp["rows"](0), sp["col"]),
        scratch_shapes=[pltpu.VMEM((tq, HEAD_DIM), F32), pltpu.VMEM((tq, 1), F32)],
        compiler_params=_params("parallel", "parallel"),
    )(proj, kv, kv, fq, fk, lse, delta, datt)


def _fox_bwd_dkv(proj, kv, fq, fk, lse, delta, datt):
    L = proj.shape[0]
    tq = min(FOX_BLOCK, L)
    nq = L // tq
    sp = _fox_specs(tq, L)

    def body(q_ref, k_ref, v_ref, fq_ref, fk_ref, lse_ref, dl_ref, do_ref,
             dk_ref, dv_ref, df_ref, dk_s, dv_s, df_s):
        ki = pl.program_id(1)
        k, v, fk = k_ref[...], v_ref[...], fk_ref[...]
        dk_s[...] = jnp.zeros_like(dk_s)
        dv_s[...] = jnp.zeros_like(dv_s)
        df_s[...] = jnp.zeros_like(df_s)

        def block(i, diagonal):
            r0 = pl.multiple_of(i * tq, tq)
            qs = (q_ref[pl.ds(r0, tq), :] * (HEAD_DIM ** -0.5)).astype(BF16)
            dob = do_ref[pl.ds(r0, tq), :].astype(BF16)
            p = jnp.exp(_fox_scores(qs, k, fq_ref[i], fk, diagonal) - lse_ref[i])
            dp = lax.dot_general(dob, v, _NT, preferred_element_type=F32)
            ds = p * (dp - dl_ref[i])
            dv_s[...] += lax.dot_general(p.astype(BF16), dob, _TN, preferred_element_type=F32)
            dk_s[...] += lax.dot_general(ds.astype(BF16), qs, _TN, preferred_element_type=F32)
            df_s[...] -= jnp.sum(ds, axis=0, keepdims=True)

        def above(i, carry):
            block(i, False)
            return carry

        block(ki, True)
        lax.fori_loop(ki + 1, nq, above, 0)
        dk_ref[...] = dk_s[...].astype(BF16)
        dv_ref[...] = dv_s[...].astype(BF16)
        df_ref[...] = df_s[...]

    return pl.pallas_call(
        body, name="fox_bwd_dkv",
        out_shape=(jax.ShapeDtypeStruct((L, MAIN_WIDTH), BF16),
                   jax.ShapeDtypeStruct((L, MAIN_WIDTH), BF16),
                   jax.ShapeDtypeStruct((FOX_HEADS, nq, 1, tq), F32)),
        grid=(FOX_HEADS, nq),
        in_specs=[sp["seq"](0), sp["rows"](0), sp["rows"](FOX_HEADS), sp["col_all"], sp["row"],
                  sp["col_all"], sp["col_all"], sp["seq"](0)],
        out_specs=(sp["rows"](0), sp["rows"](0), sp["row"]),
        scratch_shapes=[pltpu.VMEM((tq, HEAD_DIM), F32), pltpu.VMEM((tq, HEAD_DIM), F32),
                        pltpu.VMEM((1, tq), F32)],
        compiler_params=_params("parallel", "parallel"),
    )(proj, kv, kv, fq, fk, lse, delta, datt)


def _pad_lanes(a):
    return jnp.pad(a, ((0, 0), (0, LANES - a.shape[1])))


def _mem_branch_fwd(mem, g, w_mk, proj, tag):
    memn = _rmsnorm_fwd(mem, g, name="mem_norm_" + tag, out_dtype=BF16)
    kvm = _mm(memn, w_mk, name="mem_kv_" + tag)
    return memn, kvm, _mem_attn_fwd(proj, kvm)


def _mem_branch_bwd(mem, g, w_mk, proj, memn, kvm, do_mem, tag):
    dqm, dkvm = _mem_attn_bwd(proj, kvm, do_mem)
    dkvm = dkvm.astype(BF16)
    dw_mk = _mm(memn, dkvm, ta=True, name="dw_mem_kv_" + tag, out_dtype=BF16)
    dmemn = _mm(dkvm, w_mk, tb=True, name="dmemn_" + tag)
    _, dg = _rmsnorm_bwd(mem, g, dmemn, name="mem_norm_bwd_" + tag, dx_dtype=BF16)
    return dqm, dw_mk, dg


def _local_step(x, mem, target, w, fetch=None, layer_b_done=None):
    L = x.shape[0]
    g = {}
    w = dict(w)

    b_re_t = jnp.transpose(w["b_re"], (0, 2, 1))
    b_im_t = jnp.transpose(w["b_im"], (0, 2, 1))
    ar, ai, bbr_t, bbi_t = _s5_prep(w["lam_re"], w["lam_im"], w["log_step"], b_re_t, b_im_t)
    bmat, cmat = _s5_block_mats(bbr_t, bbi_t, w["c_re"], w["c_im"])
    a_rows = _s5_a_rows(ar, ai)

    hn0 = _rmsnorm_fwd(x, w["pre_norm_g"][0], name="pre_norm_0", out_dtype=BF16)
    proj_a = _mm(hn0, w["w_in_a"], name="in_proj_a")
    y, yg, xp = _s5_fwd(proj_a, bmat, cmat, a_rows, w["d_skip"])
    if fetch is not None:
        w.update(fetch("b", yg))
    t = _mm(yg, w["w_glu"], name="glu_proj")
    memn0, kvm0, om0 = _mem_branch_fwd(mem, w["mem_norm_g"][0], w["w_mem_kv"][0], proj_a, "0")
    cat0 = _gate_a_fwd(y, t, w["b_glu"], proj_a, om0)
    o0 = _mm(cat0, w["w_out"][0], name="out_proj_0")
    h1 = _rmsnorm_fwd(o0, w["post_norm_g"][0], res=x, name="post_norm_0")

    kv_in = _rmsnorm_fwd(h1, w["kv_norm_g"], name="kv_norm", out_dtype=BF16)
    if fetch is not None:
        w.update(fetch("c", kv_in))
    kv = _mm(kv_in, w["w_kv"], name="kv_proj", out_dtype=BF16)
    pre_f = _mm(kv_in, w["w_fgate"], name="fgate_proj")
    b_f = jnp.pad(w["b_fgate"], (0, LANES - FOX_HEADS))
    fcum = _fgate_fwd(pre_f, b_f)
    fc = jnp.transpose(fcum[:, :FOX_HEADS])
    tq = min(FOX_BLOCK, L)
    fq, fk = fc.reshape(FOX_HEADS, L // tq, tq, 1), fc.reshape(FOX_HEADS, L // tq, 1, tq)

    hn1 = _rmsnorm_fwd(h1, w["pre_norm_g"][1], name="pre_norm_1", out_dtype=BF16)
    proj_b = _mm(hn1, w["w_in_b"], name="in_proj_b")
    att, lse = _fox_fwd(proj_b, kv, fq, fk)
    memn1, kvm1, om1 = _mem_branch_fwd(mem, w["mem_norm_g"][1], w["w_mem_kv"][1], proj_b, "1")
    cat1 = _gate_b_fwd(att, proj_b, om1)
    o1 = _mm(cat1, w["w_out"][1], name="out_proj_1")
    h2 = _rmsnorm_fwd(o1, w["post_norm_g"][1], res=h1, name="post_norm_1")

    dh2, loss_row = _loss_head(h2, target)

    do1, dpost1 = _rmsnorm_bwd(o1, w["post_norm_g"][1], dh2, name="post_norm_bwd_1", dx_dtype=BF16)
    dcat1 = _mm(do1, w["w_out"][1], tb=True, name="dcat_1")
    g["w_out_1"] = _mm(cat1, do1, ta=True, name="dw_out_1", out_dtype=BF16)
    datt, dz1, dom1, dzm1, delta = _gate_b_bwd(dcat1, att, proj_b, om1)
    dqm1, g["w_mem_kv_1"], dmemg1 = _mem_branch_bwd(mem, w["mem_norm_g"][1], w["w_mem_kv"][1], proj_b,
                                                   memn1, kvm1, dom1, "1")
    delta = delta.reshape(fq.shape)
    dq, dfq = _fox_bwd_dq(proj_b, kv, fq, fk, lse, delta, datt)
    dk, dv, dfk = _fox_bwd_dkv(proj_b, kv, fq, fk, lse, delta, datt)
    dproj_b = jnp.concatenate([dq, dz1, dqm1, dzm1], axis=1)
    g["w_in_b"] = _mm(hn1, dproj_b, ta=True, name="dw_in_b", out_dtype=BF16, shards=N_CHIPS)
    dhn1 = _mm(dproj_b, w["w_in_b"], tb=True, name="dhn_1")

    dkv = jnp.concatenate([dk, dv], axis=1)
    g["w_kv"] = _mm(kv_in, dkv, ta=True, name="dw_kv", out_dtype=BF16, shards=N_CHIPS)
    dkv_in_a = _mm(dkv, w["w_kv"], tb=True, name="dkv_in_kv")
    dfcum = _pad_lanes(jnp.transpose(dfq.reshape(FOX_HEADS, L) + dfk.reshape(FOX_HEADS, L)))
    dpre_f, db_f = _fgate_bwd(dfcum, pre_f, b_f)
    g["b_fgate"] = db_f[0, :FOX_HEADS]
    g["w_fgate"] = _mm(kv_in, dpre_f, ta=True, name="dw_fgate")[:, :FOX_HEADS]
    dkv_in_b = _mm(dpre_f, w["w_fgate"], tb=True, name="dkv_in_fgate")
    dh1_kv, g["kv_norm_g"] = _rmsnorm_bwd(h1, w["kv_norm_g"], (dkv_in_a, dkv_in_b), name="kv_norm_bwd")
    dh1, dpre1 = _rmsnorm_bwd(h1, w["pre_norm_g"][1], dhn1, adds=(dh2, dh1_kv), name="pre_norm_bwd_1")
    if layer_b_done is not None:
        dh1 = layer_b_done(g, dh1)

    do0, dpost0 = _rmsnorm_bwd(o0, w["post_norm_g"][0], dh1, name="post_norm_bwd_0", dx_dtype=BF16)
    dcat0 = _mm(do0, w["w_out"][0], tb=True, name="dcat_0")
    g["w_out_0"] = _mm(cat0, do0, ta=True, name="dw_out_0", out_dtype=BF16)
    dz0, dzm0, dt, dyg_a, dom0, db_glu = _gate_a_bwd(dcat0, y, t, w["b_glu"], proj_a, om0)
    g["b_glu"] = db_glu[0]
    g["w_glu"] = _mm(yg, dt, ta=True, name="dw_glu", out_dtype=BF16)
    dyg_b = _mm(dt, w["w_glu"], tb=True, name="dyg")
    dy = _gelu_bwd(dyg_a, dyg_b, y)
    du, db_blk, dc_blk, da_rows, dd_skip = _s5_bwd(proj_a, dy, xp, bmat, cmat, a_rows, w["d_skip"])
    g["d_skip"] = dd_skip[0]
    dqm0, g["w_mem_kv_0"], dmemg0 = _mem_branch_bwd(mem, w["mem_norm_g"][0], w["w_mem_kv"][0], proj_a,
                                                   memn0, kvm0, dom0, "0")
    dproj_a = jnp.concatenate([du.astype(BF16), dz0, dqm0, dzm0], axis=1)
    g["w_in_a"] = _mm(hn0, dproj_a, ta=True, name="dw_in_a", out_dtype=BF16, shards=N_CHIPS)
    dhn0 = _mm(dproj_a, w["w_in_a"], tb=True, name="dhn_0")
    grad_x, dpre0 = _rmsnorm_bwd(x, w["pre_norm_g"][0], dhn0, adds=(dh1,), name="pre_norm_bwd_0")

    dbb = _s5_block_diag(db_blk)
    dcc = _s5_block_diag(dc_blk)
    g["c_re"], g["c_im"] = dcc[0], -dcc[1]
    d_ar = da_rows[:, 0, :STATE_COLS].reshape(SSM_GROUPS, SSM_STATE)
    d_ai = da_rows[:, 0, STATE_COLS:].reshape(SSM_GROUPS, SSM_STATE)
    dlr, dli, dls, dbr_t, dbi_t = _s5_prep_bwd(w["lam_re"], w["lam_im"], w["log_step"], b_re_t, b_im_t,
                                               d_ar, d_ai, dbb[0], dbb[1])
    g["lam_re"], g["lam_im"], g["log_step"] = dlr, dli, dls[:, 0]
    g["b_re"] = jnp.transpose(dbr_t, (0, 2, 1))
    g["b_im"] = jnp.transpose(dbi_t, (0, 2, 1))
    g["pre_norm_g"] = jnp.stack([dpre0, dpre1])
    g["post_norm_g"] = jnp.stack([dpost0, dpost1])
    g["mem_norm_g"] = jnp.stack([dmemg0, dmemg1])
    return loss_row, grad_x, g


_MESH = pl.DeviceIdType.MESH
_ANY = pl.BlockSpec(memory_space=pl.ANY)


def _place():
    x, y, c = lax.axis_index("x"), lax.axis_index("y"), lax.axis_index("c")
    chips = [(1 - x, y), (x, 1 - y), (1 - x, 1 - y)]
    return x, y, c, chips


def _all_gather_chips(parts):
    n = len(parts)

    def body(*refs):
        ins, outs = refs[:n], refs[n:2 * n]
        ici_send, ici_recv, d2d_send, d2d_recv = refs[2 * n:]
        x, y, c, chips = _place()
        me = 2 * x + y
        sib = (x, y, 1 - c)

        def ici(i, k, src_chip_j, dst):
            return pltpu.make_async_remote_copy(
                src_ref=ins[i].at[c] if src_chip_j is None else outs[i].at[src_chip_j, c],
                dst_ref=outs[i].at[me if src_chip_j is None else src_chip_j, c],
                send_sem=ici_send.at[i * 3 + k], recv_sem=ici_recv.at[i * 3 + k],
                device_id=dst, device_id_type=_MESH)

        def d2d(i, k, chip_j, half):
            return pltpu.make_async_remote_copy(
                src_ref=outs[i].at[chip_j, half], dst_ref=outs[i].at[chip_j, half],
                send_sem=d2d_send.at[i * 3 + k], recv_sem=d2d_recv.at[i * 3 + k],
                device_id=sib, device_id_type=_MESH)

        sends = [ici(i, k, None, (*chips[k], c)) for i in range(n) for k in range(3)]
        for cp in sends:
            cp.start()
        passed = []
        for k, (cx, cy) in enumerate(chips):
            for i in range(n):
                ici(i, k, 2 * cx + cy, (x, y, c)).wait_recv()
                fwd = d2d(i, k, 2 * cx + cy, c)
                fwd.start()
                passed.append(fwd)
        for k, (cx, cy) in enumerate(chips):
            for i in range(n):
                d2d(i, k, 2 * cx + cy, 1 - c).wait_recv()
        for cp in sends + passed:
            cp.wait_send()

    return pl.pallas_call(
        body, name="all_gather_weights",
        out_shape=[jax.ShapeDtypeStruct((N_CHIPS,) + p.shape, p.dtype) for p in parts],
        in_specs=[_ANY] * n, out_specs=[_ANY] * n,
        scratch_shapes=[pltpu.SemaphoreType.DMA((3 * n,)), pltpu.SemaphoreType.DMA((3 * n,)),
                        pltpu.SemaphoreType.DMA((3 * n,)), pltpu.SemaphoreType.DMA((3 * n,))],
    )(*parts)


_HBM = pl.BlockSpec(memory_space=pltpu.HBM)
_SEM = pl.BlockSpec(memory_space=pltpu.SEMAPHORE)
_SIDE = pltpu.SideEffectType.DATAFLOW_SIDE_EFFECTING


def _in_hbm(a):
    return pltpu.with_memory_space_constraint(a, pltpu.HBM)


def _hbm_like(a):
    return pltpu.HBM(a.shape, a.dtype)


def _ici_copies(srcs, lands, send_sem, recv_sem, src_at, dst_at, wait_at):
    x, y, c, chips = _place()
    start, wait = [], []
    for i in range(len(srcs)):
        for k, (cx, cy) in enumerate(chips):
            sem = dict(send_sem=send_sem.at[3 * i + k], recv_sem=recv_sem.at[3 * i + k],
                       device_id=(cx, cy, c), device_id_type=_MESH)
            src = src_at(srcs[i], 2 * cx + cy, c)
            start.append(pltpu.make_async_remote_copy(src_ref=src, dst_ref=dst_at(lands[i], 2 * x + y, k, c), **sem))
            wait.append(pltpu.make_async_remote_copy(src_ref=src, dst_ref=wait_at(lands[i], 2 * cx + cy, k, c), **sem))
    return start, wait


def _ici_start(srcs, lands, token, route, *, name):
    n = len(srcs)

    def body(*refs):
        start, _ = _ici_copies(refs[:n], refs[n:2 * n], refs[2 * n + 1], refs[2 * n + 2], *route)
        for cp in start:
            cp.start()

    sems = pltpu.SemaphoreType.DMA((3 * n,))
    outs = pl.pallas_call(
        body, name=name,
        out_shape=(sems, sems, *[_hbm_like(a) for a in srcs], *[_hbm_like(a) for a in lands], _hbm_like(token)),
        in_specs=[_HBM] * (2 * n + 1), out_specs=(_SEM, _SEM, *[_HBM] * (2 * n + 1)),
        input_output_aliases={i: 2 + i for i in range(2 * n + 1)},
        compiler_params=pltpu.CompilerParams(has_side_effects=_SIDE),
    )(*[_in_hbm(a) for a in srcs], *[_in_hbm(a) for a in lands], _in_hbm(token))
    return (outs[0], outs[1], list(outs[2:2 + n]), list(outs[2 + n:2 + 2 * n])), outs[2 + 2 * n]


def _ici_wait(handle, after, route, *, name):
    send_sem, recv_sem, srcs, lands = handle
    n = len(srcs)

    def body(*refs):
        _, wait = _ici_copies(refs[:n], refs[n:2 * n], refs[2 * n], refs[2 * n + 1], *route)
        for cp in wait:
            cp.wait_send()
            cp.wait_recv()

    outs = pl.pallas_call(
        body, name=name,
        out_shape=(*[_hbm_like(a) for a in srcs], *[_hbm_like(a) for a in lands]),
        in_specs=[_HBM] * (2 * n) + [_SEM, _SEM, _ANY], out_specs=tuple([_HBM] * (2 * n)),
        input_output_aliases={i: i for i in range(2 * n)},
        compiler_params=pltpu.CompilerParams(has_side_effects=_SIDE),
    )(*srcs, *lands, send_sem, recv_sem, after)
    return list(outs[:n]), list(outs[n:])


_GATHER_ROUTE = (lambda s, j, c: s.at[c], lambda l, me, k, c: l.at[me, c], lambda l, j, k, c: l.at[j, c])
_SCATTER_ROUTE = (lambda s, j, c: s.at[j], lambda l, me, k, c: l.at[k], lambda l, j, k, c: l.at[k])


def _gather_forward(lands, tag):
    n = len(lands)

    def body(*refs):
        ins, outs = refs[:n], refs[n:2 * n]
        send_sem, recv_sem = refs[2 * n:]
        x, y, c, chips = _place()

        def copy(i, k, half):
            cx, cy = chips[k]
            return pltpu.make_async_remote_copy(
                src_ref=ins[i].at[2 * cx + cy, half], dst_ref=outs[i].at[2 * cx + cy, half],
                send_sem=send_sem.at[3 * i + k], recv_sem=recv_sem.at[3 * i + k],
                device_id=(x, y, 1 - c), device_id_type=_MESH)

        copies = [copy(i, k, c) for i in range(n) for k in range(3)]
        for cp in copies:
            cp.start()
        for i in range(n):
            for k in range(3):
                copy(i, k, 1 - c).wait_recv()
        for cp in copies:
            cp.wait_send()

    return pl.pallas_call(
        body, name="gather_forward_to_sibling_" + tag,
        out_shape=[jax.ShapeDtypeStruct(a.shape, a.dtype) for a in lands],
        in_specs=[_ANY] * n, out_specs=[_ANY] * n,
        input_output_aliases={i: i for i in range(n)},
        scratch_shapes=[pltpu.SemaphoreType.DMA((3 * n,)), pltpu.SemaphoreType.DMA((3 * n,))],
    )(*lands)


def _swap_halves(grads, tag):
    n = len(grads)

    def body(*refs):
        ins, outs = refs[:n], refs[n:2 * n]
        send_sem, recv_sem = refs[2 * n:]
        x, y, c, _ = _place()
        copies = [pltpu.make_async_remote_copy(
            src_ref=ins[i].at[:, 1 - c], dst_ref=outs[i],
            send_sem=send_sem.at[i], recv_sem=recv_sem.at[i],
            device_id=(x, y, 1 - c), device_id_type=_MESH) for i in range(n)]
        for cp in copies:
            cp.start()
        for cp in copies:
            cp.wait()

    return pl.pallas_call(
        body, name="grad_swap_halves_" + tag,
        out_shape=[jax.ShapeDtypeStruct((N_CHIPS,) + g.shape[2:], g.dtype) for g in grads],
        in_specs=[_ANY] * n, out_specs=[_ANY] * n,
        scratch_shapes=[pltpu.SemaphoreType.DMA((n,)), pltpu.SemaphoreType.DMA((n,))],
    )(*grads)


def _pair_sum(g, r, c_idx, *, name):
    _, _, h, C = g.shape
    tr = min(h, 256)

    def body(c_ref, g_ref, r_ref, o_ref):
        o_ref[...] = (g_ref[...].astype(F32) + r_ref[...].astype(F32)).astype(o_ref.dtype)

    return pl.pallas_call(
        body, name=name, out_shape=jax.ShapeDtypeStruct((N_CHIPS, h, C), g.dtype),
        grid_spec=pltpu.PrefetchScalarGridSpec(
            num_scalar_prefetch=1, grid=(N_CHIPS, h // tr),
            in_specs=[pl.BlockSpec((None, None, tr, C), lambda j, i, s: (j, s[0], i, 0)),
                      pl.BlockSpec((None, tr, C), lambda j, i, s: (j, i, 0))],
            out_specs=pl.BlockSpec((None, tr, C), lambda j, i, s: (j, i, 0))),
        compiler_params=_params("parallel", "parallel"),
    )(c_idx, g, r)


def _send_to_owners(sums, tag):
    n = len(sums)

    def body(*refs):
        ins, outs = refs[:n], refs[n:2 * n]
        send_sem, recv_sem = refs[2 * n:]
        x, y, c, chips = _place()
        copies = [pltpu.make_async_remote_copy(
            src_ref=ins[i].at[2 * cx + cy], dst_ref=outs[i].at[k],
            send_sem=send_sem.at[i * 3 + k], recv_sem=recv_sem.at[i * 3 + k],
            device_id=(cx, cy, c), device_id_type=_MESH)
            for i in range(n) for k, (cx, cy) in enumerate(chips)]
        for cp in copies:
            cp.start()
        for cp in copies:
            cp.wait()

    return pl.pallas_call(
        body, name="grad_send_to_owners_" + tag,
        out_shape=[jax.ShapeDtypeStruct((3,) + s.shape[1:], s.dtype) for s in sums],
        in_specs=[_ANY] * n, out_specs=[_ANY] * n,
        scratch_shapes=[pltpu.SemaphoreType.DMA((3 * n,)), pltpu.SemaphoreType.DMA((3 * n,))],
    )(*sums)


def _owner_sum(s, r, jc_idx, *, name):
    _, h, C = s.shape
    tr = min(h, 256)

    def body(jc_ref, s_ref, r_ref, o_ref):
        acc = s_ref[...].astype(F32)
        for k in range(3):
            acc = acc + r_ref[k].astype(F32)
        o_ref[...] = acc

    return pl.pallas_call(
        body, name=name, out_shape=jax.ShapeDtypeStruct((2, h, C), F32),
        grid_spec=pltpu.PrefetchScalarGridSpec(
            num_scalar_prefetch=1, grid=(h // tr,),
            in_specs=[pl.BlockSpec((None, tr, C), lambda i, s: (s[0], i, 0)),
                      pl.BlockSpec((3, tr, C), lambda i, s: (0, i, 0))],
            out_specs=pl.BlockSpec((None, tr, C), lambda i, s: (s[1], i, 0))),
        compiler_params=_params("parallel"),
    )(jc_idx, s, r)


def _share_with_sibling(bufs, tag):
    n = len(bufs)

    def body(*refs):
        ins, outs = refs[:n], refs[n:2 * n]
        send_sem, recv_sem = refs[2 * n:]
        x, y, c, _ = _place()

        def copy(i, half):
            return pltpu.make_async_remote_copy(
                src_ref=ins[i].at[half], dst_ref=outs[i].at[half],
                send_sem=send_sem.at[i], recv_sem=recv_sem.at[i],
                device_id=(x, y, 1 - c), device_id_type=_MESH)

        copies = [copy(i, c) for i in range(n)]
        for cp in copies:
            cp.start()
        for i in range(n):
            copy(i, 1 - c).wait_recv()
        for cp in copies:
            cp.wait_send()

    return pl.pallas_call(
        body, name="grad_share_with_sibling_" + tag,
        out_shape=[jax.ShapeDtypeStruct(b.shape, b.dtype) for b in bufs],
        in_specs=[_ANY] * n, out_specs=[_ANY] * n,
        input_output_aliases={i: i for i in range(n)},
        scratch_shapes=[pltpu.SemaphoreType.DMA((n,)), pltpu.SemaphoreType.DMA((n,))],
    )(*bufs)


def _chip_sums(grads, c_idx, tag):
    views = [g.reshape(N_CHIPS, 2, g.shape[1] // 2, g.shape[2]) for g in grads]
    arrived = _swap_halves(views, tag)
    return [_pair_sum(v, r, c_idx, name=f"grad_pair_sum_{tag}_{i}") for i, (v, r) in enumerate(zip(views, arrived))]


def _owner_totals(sums, arrived, jc_idx, tag):
    halves = [_owner_sum(s, r, jc_idx, name=f"grad_owner_sum_{tag}_{i}") for i, (s, r) in enumerate(zip(sums, arrived))]
    return [f.reshape(-1, f.shape[2]) for f in _share_with_sibling(halves, tag)]


def _all_reduce_small(buf):
    R = buf.shape[0]
    n_dev = 2 * N_CHIPS

    def body(x_ref, o_ref, all_ref, send_sems, recv_sems, local_sem):
        x, y, c, chips = _place()
        me, sib = (x, y, c), (x, y, 1 - c)

        def rows(px, py, pc):
            return all_ref.at[4 * px + 2 * py + pc]

        def copy(k, block, to, src=None):
            return pltpu.make_async_remote_copy(
                src_ref=rows(*block) if src is None else src, dst_ref=rows(*block),
                send_sem=send_sems.at[k], recv_sem=recv_sems.at[k], device_id=to, device_id_type=_MESH)

        mine = pltpu.make_async_copy(x_ref, rows(*me), local_sem)
        mine.start()
        first = [copy(0, me, sib, src=x_ref)]
        first += [copy(1 + j, me, (*chip, c), src=x_ref) for j, chip in enumerate(chips)]
        for cp in first:
            cp.start()
        passed = [copy(4 + j, (*chip, c), sib) for j, chip in enumerate(chips)]
        for j, chip in enumerate(chips):
            copy(1 + j, (*chip, c), me).wait_recv()
            passed[j].start()
        copy(0, sib, me).wait_recv()
        for j, chip in enumerate(chips):
            copy(4 + j, (*chip, 1 - c), me).wait_recv()
        for cp in first + passed:
            cp.wait_send()
        mine.wait()
        acc = all_ref[0]
        for d in range(1, n_dev):
            acc = acc + all_ref[d]
        o_ref[...] = acc

    vmem = pl.BlockSpec(memory_space=pltpu.VMEM)
    return pl.pallas_call(
        body, name="all_reduce_small", out_shape=jax.ShapeDtypeStruct((R, LANES), F32),
        in_specs=[vmem], out_specs=vmem,
        scratch_shapes=[pltpu.VMEM((n_dev, R, LANES), F32),
                        pltpu.SemaphoreType.DMA((7,)), pltpu.SemaphoreType.DMA((7,)), pltpu.SemaphoreType.DMA],
        compiler_params=_params(),
    )(buf)


def _adamw(w, g, m, v, *, name):
    R, C = w.shape
    whole_fits = 7 * 2 * R * C * 4 <= VMEM_LIMIT_BYTES // 2
    tr = R if whole_fits else next(c for c in (256, 192, 128, 64, 32, 16, 8) if R % c == 0)

    def body(w_ref, g_ref, m_ref, v_ref, d_ref, nm_ref, nv_ref):
        g = g_ref[...]
        m = ADAM_B1 * m_ref[...] + (1.0 - ADAM_B1) * g
        v = ADAM_B2 * v_ref[...] + (1.0 - ADAM_B2) * (g * g)
        nm_ref[...] = m
        nv_ref[...] = v
        m_hat = m / (1.0 - ADAM_B1 ** ADAM_STEP)
        v_hat = v / (1.0 - ADAM_B2 ** ADAM_STEP)
        d_ref[...] = -ADAM_LR * (m_hat / (jnp.sqrt(v_hat) + ADAM_EPS) + ADAM_WD * w_ref[...])

    blk = pl.BlockSpec((tr, C), lambda i: (i, 0))
    sds = jax.ShapeDtypeStruct((R, C), F32)
    return pl.pallas_call(
        body, name=name, out_shape=(sds, sds, sds), grid=(R // tr,),
        in_specs=[blk] * 4, out_specs=(blk, blk, blk),
        compiler_params=_params("parallel"),
    )(w, g, m, v)


_TILE = SUBLANES * LANES


def _pack(arrays):
    rows = []
    for a in arrays:
        flat = a.reshape(-1)
        flat = jnp.pad(flat, (0, (-flat.shape[0]) % _TILE))
        rows.append(flat.reshape(-1, LANES))
    return jnp.concatenate(rows, axis=0)


def _unpack(buf, shapes):
    out, r = [], 0
    for s in shapes:
        size = math.prod(s)
        nr = -(-size // _TILE) * SUBLANES
        out.append(buf[r:r + nr].reshape(-1)[:size].reshape(s))
        r += nr
    return out


_BIG = ("w_in_a", "w_glu", "w_kv", "w_in_b", "w_mem_kv", "w_out")
_REPLICATED = ("pre_norm_g", "post_norm_g", "lam_re", "lam_im", "log_step", "b_re", "b_im", "c_re", "c_im",
               "kv_norm_g", "b_fgate", "mem_norm_g")
_SHARDED_SMALL = ("d_skip", "b_glu", "w_fgate")
_WEIGHTS = ("pre_norm_g", "post_norm_g", "w_in_a", "lam_re", "lam_im", "log_step", "b_re", "b_im", "c_re",
            "c_im", "d_skip", "w_glu", "b_glu", "kv_norm_g", "w_kv", "w_fgate", "b_fgate", "w_in_b",
            "mem_norm_g", "w_mem_kv", "w_out")


def _halves(a):
    return a.reshape(2, a.shape[0] // 2, a.shape[1])


def _unhalve(a):
    return a.reshape(N_CHIPS, 2 * a.shape[2], a.shape[3])


def _columns(a):
    return jnp.transpose(a, (1, 0, 2)).reshape(a.shape[1], N_CHIPS * a.shape[2])


def kernel(x, mem, pre_norm_g, post_norm_g, w_in_a, lam_re, lam_im, log_step, b_re, b_im, c_re, c_im, d_skip, w_glu, b_glu, kv_norm_g, w_kv, w_fgate, b_fgate, w_in_b, mem_norm_g, w_mem_kv, w_out, loss_target, m_pre_norm_g, m_post_norm_g, m_w_in_a, m_lam_re, m_lam_im, m_log_step, m_b_re, m_b_im, m_c_re, m_c_im, m_d_skip, m_w_glu, m_b_glu, m_kv_norm_g, m_w_kv, m_w_fgate, m_b_fgate, m_w_in_b, m_mem_norm_g, m_w_mem_kv, m_w_out, v_pre_norm_g, v_post_norm_g, v_w_in_a, v_lam_re, v_lam_im, v_log_step, v_b_re, v_b_im, v_c_re, v_c_im, v_d_skip, v_w_glu, v_b_glu, v_kv_norm_g, v_w_kv, v_w_fgate, v_b_fgate, v_w_in_b, v_mem_norm_g, v_w_mem_kv, v_w_out):
    a = dict(locals())
    xi, yi, ci = lax.axis_index("x"), lax.axis_index("y"), lax.axis_index("c")
    chip = 2 * xi + yi
    c_idx = jnp.reshape(ci, (1,)).astype(jnp.int32)
    jc_idx = jnp.stack([chip, ci]).astype(jnp.int32)

    vec = jnp.zeros((2 * SUBLANES, MAIN_WIDTH // N_CHIPS), F32)
    vec = vec.at[0].set(a["d_skip"][0]).at[1].set(a["b_glu"][0])
    def own_slot(gathered, parts):
        return [lax.dynamic_update_index_in_dim(g, p, chip, 0) for g, p in zip(gathered, parts)]

    parts_a = [_halves(a["w_in_a"][0].astype(BF16)), _halves(vec)]
    parts_b = [_halves(a["w_glu"][0].astype(BF16)), _halves(a["w_mem_kv"].reshape(-1, 2 * MEM_WIDTH).astype(BF16)),
               _halves(a["w_out"].reshape(-1, D_MODEL).astype(BF16))]
    parts_c = [_halves(a["w_kv"].astype(BF16)), _halves(_pad_lanes(a["w_fgate"]).astype(BF16)),
               _halves(a["w_in_b"][0].astype(BF16))]
    w_in_a, vecs = own_slot(_all_gather_chips(parts_a), parts_a)
    travelling = {}
    for tag, parts in (("b", parts_b), ("c", parts_c)):
        lands = [lax.empty((N_CHIPS,) + p.shape, p.dtype) for p in parts]
        travelling[tag], vecs = _ici_start(parts, lands, vecs, _GATHER_ROUTE, name=f"gather_{tag}_start")

    def fetch(tag, after):
        parts, lands = _ici_wait(travelling[tag], after, _GATHER_ROUTE, name=f"gather_{tag}_wait")
        full = own_slot(_gather_forward(lands, tag), parts)
        if tag == "b":
            w_glu, w_mk, w_out = full
            return dict(w_glu=w_glu.reshape(MAIN_WIDTH, MAIN_WIDTH),
                        w_mem_kv=jnp.transpose(w_mk, (1, 0, 2, 3)).reshape(2, D_MODEL, 2 * MEM_WIDTH),
                        w_out=jnp.transpose(w_out, (1, 0, 2, 3)).reshape(2, D_MODEL, D_MODEL))
        w_kv, w_fg, w_in_b = full
        return dict(w_kv=_columns(_unhalve(w_kv)), w_fgate=w_fg.reshape(D_MODEL, LANES),
                    w_in_b=_columns(_unhalve(w_in_b)))

    w = dict(
        w_in_a=_columns(_unhalve(w_in_a)),
        d_skip=vecs[:, 0, 0, :].reshape(MAIN_WIDTH), b_glu=vecs[:, 0, 1, :].reshape(MAIN_WIDTH),
        pre_norm_g=a["pre_norm_g"], post_norm_g=a["post_norm_g"], mem_norm_g=a["mem_norm_g"],
        kv_norm_g=a["kv_norm_g"], b_fgate=a["b_fgate"],
        lam_re=a["lam_re"][0], lam_im=a["lam_im"][0], log_step=a["log_step"][0],
        b_re=a["b_re"][0], b_im=a["b_im"][0], c_re=a["c_re"][0], c_im=a["c_im"][0])

    sent = {}

    def layer_b_done(g, token):
        big = [g["w_kv"], g["w_in_b"], g["w_mem_kv_1"].reshape(N_CHIPS, -1, 2 * MEM_WIDTH),
               g["w_out_1"].reshape(N_CHIPS, -1, D_MODEL)]
        sums = _chip_sums(big, c_idx, "b")
        lands = [lax.empty((3,) + s.shape[1:], s.dtype) for s in sums]
        sent["b"], token = _ici_start(sums, lands, token, _SCATTER_ROUTE, name="grad_send_b_start")
        return token

    loss_row, grad_x, g = _local_step(a["x"][0], a["mem"][0], a["loss_target"][0], w, fetch, layer_b_done)
    loss = lax.psum(jnp.sum(loss_row), MESH_AXES)

    sums_b, arrived_b = _ici_wait(sent["b"], grad_x, _SCATTER_ROUTE, name="grad_send_b_wait")
    r_kv, r_in_b, r_mk1, r_out1 = _owner_totals(sums_b, arrived_b, jc_idx, "b")
    big = [g["w_in_a"], g["w_glu"].reshape(N_CHIPS, -1, MAIN_WIDTH),
           g["w_mem_kv_0"].reshape(N_CHIPS, -1, 2 * MEM_WIDTH), g["w_out_0"].reshape(N_CHIPS, -1, D_MODEL)]
    sums_a = _chip_sums(big, c_idx, "a")
    r_in_a, r_glu, r_mk0, r_out0 = _owner_totals(sums_a, _send_to_owners(sums_a, "a"), jc_idx, "a")
    grads = {"w_in_a": r_in_a[None], "w_glu": r_glu[None], "w_kv": r_kv, "w_in_b": r_in_b[None],
             "w_mem_kv": jnp.stack([r_mk0, r_mk1]), "w_out": jnp.stack([r_out0, r_out1])}

    small_names = _REPLICATED + _SHARDED_SMALL
    small = _all_reduce_small(_pack([g[n] for n in small_names]))
    small = dict(zip(small_names, _unpack(small, [g[n].shape for n in small_names])))
    for n in _REPLICATED:
        grads[n] = small[n].reshape(a[n].shape)
    nd = MAIN_WIDTH // N_CHIPS
    grads["d_skip"] = lax.dynamic_slice(small["d_skip"], (chip * nd,), (nd,))[None]
    grads["b_glu"] = lax.dynamic_slice(small["b_glu"], (chip * nd,), (nd,))[None]
    nf = D_MODEL // N_CHIPS
    grads["w_fgate"] = lax.dynamic_slice(small["w_fgate"], (chip * nf, 0), (nf, FOX_HEADS))

    delta, new_m, new_v = {}, {}, {}
    for n in _BIG:
        shape = a[n].shape
        d2 = (-1, shape[-1])
        d, m, v = _adamw(a[n].reshape(d2), grads[n].reshape(d2), a["m_" + n].reshape(d2),
                         a["v_" + n].reshape(d2), name="adamw_" + n)
        delta[n], new_m[n], new_v[n] = d.reshape(shape), m.reshape(shape), v.reshape(shape)
    shapes = [a[n].shape for n in small_names]
    d, m, v = _adamw(_pack([a[n] for n in small_names]), _pack([grads[n] for n in small_names]),
                     _pack([a["m_" + n] for n in small_names]), _pack([a["v_" + n] for n in small_names]),
                     name="adamw_small")
    for n, dd, mm, vv in zip(small_names, _unpack(d, shapes), _unpack(m, shapes), _unpack(v, shapes)):
        delta[n], new_m[n], new_v[n] = dd, mm, vv

    return (loss, grad_x[None], *[grads[n] for n in _WEIGHTS], *[delta[n] for n in _WEIGHTS],
            *[new_m[n] for n in _WEIGHTS], *[new_v[n] for n in _WEIGHTS])
```

```python
import functools
import math

import jax
import jax.numpy as jnp
from jax import lax
from jax.experimental import pallas as pl
from jax.experimental.pallas import tpu as pltpu

F32 = jnp.float32
BF16 = jnp.bfloat16

D_MODEL = 2048
N_MEM = 256
MAIN_WIDTH = 1536
MEM_WIDTH = 512
IN_WIDTH = 2 * MAIN_WIDTH + 2 * MEM_WIDTH
HEAD_DIM = 128
FOX_HEADS = MAIN_WIDTH // HEAD_DIM
MEM_HEADS = MEM_WIDTH // HEAD_DIM
SSM_GROUP = 16
SSM_GROUPS = MAIN_WIDTH // SSM_GROUP
SSM_STATE = 64
GROUPS_PER_BLOCK = 8
SSM_BLOCKS = SSM_GROUPS // GROUPS_PER_BLOCK
STATE_COLS = GROUPS_PER_BLOCK * SSM_STATE
EPS = 1e-6
ADAM_LR = 0.001
ADAM_B1 = 0.9
ADAM_B2 = 0.999
ADAM_EPS = 1e-08
ADAM_WD = 0.01
ADAM_STEP = 10
N_CHIPS = 4
LANES = 128
SUBLANES = 8
VMEM_LIMIT_BYTES = 56 * 1024 * 1024
NEG_BIG = -1e30
MESH_AXES = ("x", "y", "c")


def _params(*sem):
    return pltpu.CompilerParams(dimension_semantics=sem if sem else None,
                                vmem_limit_bytes=VMEM_LIMIT_BYTES)


def _sigmoid(x):
    return 1.0 / (1.0 + jnp.exp(-x))


def _gelu(x):
    c = math.sqrt(2.0 / math.pi)
    return 0.5 * x * (1.0 + jnp.tanh(c * (x + 0.044715 * (x * x * x))))


def _gelu_grad(x):
    c = math.sqrt(2.0 / math.pi)
    t = jnp.tanh(c * (x + 0.044715 * (x * x * x)))
    return 0.5 * (1.0 + t) + 0.5 * x * (1.0 - t * t) * (c * (1.0 + 3.0 * 0.044715 * (x * x)))


def _silu_and_grad(z):
    s = _sigmoid(z)
    return z * s, s * (1.0 + z * (1.0 - s))


_TILE_CHOICES = (2048, 1024, 768, 512, 384, 256, LANES)


def _tile(n, cap):
    return next(c for c in _TILE_CHOICES if c <= cap and n % c == 0)


def _mm(a, b, *, name, ta=False, tb=False, out_dtype=F32, shards=1, tm=1024, tn=1024, tk=2048):
    if ta:
        K, M = a.shape
    else:
        M, K = a.shape
    if tb:
        N, kb = b.shape
    else:
        kb, N = b.shape
    assert K == kb, (a.shape, b.shape)
    ns = N // shards
    tm, tn, tk = _tile(M, tm), _tile(ns, tn), _tile(K, tk)
    assert M % tm == 0 and ns % tn == 0 and K % tk == 0 and N % shards == 0
    nk = K // tk
    dn = (((0 if ta else 1,), (1 if tb else 0,)), ((), ()))

    def body(a_ref, b_ref, o_ref, acc_ref):
        k = pl.program_id(2)

        @pl.when(k == 0)
        def _():
            acc_ref[...] = jnp.zeros_like(acc_ref)

        acc_ref[...] += lax.dot_general(a_ref[...].astype(BF16), b_ref[...].astype(BF16), dn,
                                        preferred_element_type=F32)

        @pl.when(k == nk - 1)
        def _():
            o_ref[...] = acc_ref[...].astype(o_ref.dtype)

    a_spec = (pl.BlockSpec((tk, tm), lambda i, j, k: (k, i)) if ta
              else pl.BlockSpec((tm, tk), lambda i, j, k: (i, k)))
    b_spec = (pl.BlockSpec((tn, tk), lambda i, j, k: (j, k)) if tb
              else pl.BlockSpec((tk, tn), lambda i, j, k: (k, j)))
    if shards == 1:
        out_shape = jax.ShapeDtypeStruct((M, N), out_dtype)
        o_spec = pl.BlockSpec((tm, tn), lambda i, j, k: (i, j))
    else:
        nb = ns // tn
        out_shape = jax.ShapeDtypeStruct((shards, M, ns), out_dtype)
        o_spec = pl.BlockSpec((None, tm, tn), lambda i, j, k: (j // nb, i, j % nb))
    return pl.pallas_call(
        body, name=name, out_shape=out_shape,
        grid=(M // tm, N // tn, nk),
        in_specs=[a_spec, b_spec], out_specs=o_spec,
        scratch_shapes=[pltpu.VMEM((tm, tn), F32)],
        compiler_params=_params("parallel", "parallel", "arbitrary"),
    )(a, b)


def _rmsnorm_fwd(x, g, *, name, res=None, out_dtype=F32, tr=256):
    L, D = x.shape
    tr = min(tr, L)
    has_res = res is not None

    def body(*refs):
        if has_res:
            x_ref, g_ref, r_ref, o_ref = refs
        else:
            x_ref, g_ref, o_ref = refs
        xf = x_ref[...]
        r = lax.rsqrt(jnp.mean(xf * xf, axis=-1, keepdims=True) + EPS)
        y = xf * r * g_ref[...]
        if has_res:
            y = r_ref[...] + y
        o_ref[...] = y.astype(o_ref.dtype)

    row = pl.BlockSpec((tr, D), lambda i: (i, 0))
    vec = pl.BlockSpec((1, D), lambda i: (0, 0))
    ins = [x, g.reshape(1, D)] + ([res] if has_res else [])
    return pl.pallas_call(
        body, name=name, out_shape=jax.ShapeDtypeStruct((L, D), out_dtype),
        grid=(L // tr,), in_specs=[row, vec] + ([row] if has_res else []), out_specs=row,
        compiler_params=_params("parallel"),
    )(*ins)


def _rmsnorm_bwd(x, g, dy, *, name, adds=(), dx_dtype=F32, tr=256):
    L, D = x.shape
    tr = min(tr, L)
    dys = dy if isinstance(dy, tuple) else (dy,)
    n_dy, n_add = len(dys), len(adds)

    def body(*refs):
        x_ref, g_ref = refs[:2]
        dy_refs = refs[2:2 + n_dy]
        add_refs = refs[2 + n_dy:2 + n_dy + n_add]
        dx_ref, dg_ref = refs[2 + n_dy + n_add:]
        xf = x_ref[...]
        dyf = dy_refs[0][...].astype(F32)
        for d_ref in dy_refs[1:]:
            dyf = dyf + d_ref[...].astype(F32)
        r = lax.rsqrt(jnp.mean(xf * xf, axis=-1, keepdims=True) + EPS)
        gy = dyf * g_ref[...]
        c = jnp.mean(xf * gy, axis=-1, keepdims=True) * (r * r * r)
        dx = gy * r - xf * c
        for a_ref in add_refs:
            dx = dx + a_ref[...].astype(F32)
        dx_ref[...] = dx.astype(dx_ref.dtype)

        @pl.when(pl.program_id(0) == 0)
        def _():
            dg_ref[...] = jnp.zeros_like(dg_ref)

        dg_ref[...] += jnp.sum(dyf * xf * r, axis=0, keepdims=True)

    row = pl.BlockSpec((tr, D), lambda i: (i, 0))
    vec = pl.BlockSpec((1, D), lambda i: (0, 0))
    dx, dg = pl.pallas_call(
        body, name=name,
        out_shape=(jax.ShapeDtypeStruct((L, D), dx_dtype), jax.ShapeDtypeStruct((1, D), F32)),
        grid=(L // tr,), in_specs=[row, vec] + [row] * (n_dy + n_add), out_specs=(row, vec),
        compiler_params=_params("arbitrary"),
    )(x, g.reshape(1, D), *dys, *adds)
    return dx, dg.reshape(D)


def _loss_head(h, target, *, tr=256):
    L, D = h.shape
    tr = min(tr, L)

    def body(h_ref, t_ref, dh_ref, loss_ref):
        e = h_ref[...] - t_ref[...]
        dh_ref[...] = e * (1.0 / D)

        @pl.when(pl.program_id(0) == 0)
        def _():
            loss_ref[...] = jnp.zeros_like(loss_ref)

        loss_ref[...] += jnp.sum(e * e, axis=0, keepdims=True) * (0.5 / D)

    row = pl.BlockSpec((tr, D), lambda i: (i, 0))
    vec = pl.BlockSpec((1, D), lambda i: (0, 0))
    dh, lp = pl.pallas_call(
        body, name="loss_head",
        out_shape=(jax.ShapeDtypeStruct((L, D), F32), jax.ShapeDtypeStruct((1, D), F32)),
        grid=(L // tr,), in_specs=[row, row], out_specs=(row, vec),
        compiler_params=_params("arbitrary"),
    )(h, target)
    return dh, lp


def _s5_coeffs(lr, li, ls):
    dt = jnp.exp(ls)
    mag = jnp.exp(lr * dt)
    ar = mag * jnp.cos(li * dt)
    ai = mag * jnp.sin(li * dt)
    den = lr * lr + li * li
    cr = ((ar - 1.0) * lr + ai * li) / den
    ci = (ai * lr - (ar - 1.0) * li) / den
    return dt, ar, ai, den, cr, ci


def _s5_prep(lam_re, lam_im, log_step, b_re_t, b_im_t):
    G, P = lam_re.shape
    H = b_re_t.shape[1]

    def body(lr_ref, li_ref, ls_ref, br_ref, bi_ref, ar_ref, ai_ref, bbr_ref, bbi_ref):
        _, ar, ai, _, cr, ci = _s5_coeffs(lr_ref[...], li_ref[...], ls_ref[...])
        ar_ref[...] = ar
        ai_ref[...] = ai
        br, bi = br_ref[...], bi_ref[...]
        crb, cib = cr[:, None, :], ci[:, None, :]
        bbr_ref[...] = crb * br - cib * bi
        bbi_ref[...] = crb * bi + cib * br

    return pl.pallas_call(
        body, name="s5_prep",
        out_shape=(jax.ShapeDtypeStruct((G, P), F32), jax.ShapeDtypeStruct((G, P), F32),
                   jax.ShapeDtypeStruct((G, H, P), F32), jax.ShapeDtypeStruct((G, H, P), F32)),
        compiler_params=_params(),
    )(lam_re, lam_im, log_step.reshape(G, 1), b_re_t, b_im_t)


def _s5_prep_bwd(lam_re, lam_im, log_step, b_re_t, b_im_t, d_ar, d_ai, d_bbr, d_bbi):
    G, P = lam_re.shape
    H = b_re_t.shape[1]

    def body(lr_ref, li_ref, ls_ref, br_ref, bi_ref, dar_ref, dai_ref, dbbr_ref, dbbi_ref,
             dlr_ref, dli_ref, dls_ref, dbr_ref, dbi_ref):
        lr, li = lr_ref[...], li_ref[...]
        dt, ar, ai, den, cr, ci = _s5_coeffs(lr, li, ls_ref[...])
        br, bi = br_ref[...], bi_ref[...]
        gbr, gbi = dbbr_ref[...], dbbi_ref[...]
        crb, cib = cr[:, None, :], ci[:, None, :]
        dbr_ref[...] = crb * gbr + cib * gbi
        dbi_ref[...] = crb * gbi - cib * gbr
        gcr = jnp.sum(br * gbr + bi * gbi, axis=1)
        gci = jnp.sum(br * gbi - bi * gbr, axis=1)
        ilr, ili = lr / den, -li / den
        gar = dar_ref[...] + (ilr * gcr + ili * gci)
        gai = dai_ref[...] + (ilr * gci - ili * gcr)
        qr, qi = cr * ilr - ci * ili, cr * ili + ci * ilr
        glr = -(qr * gcr + qi * gci)
        gli = -(qr * gci - qi * gcr)
        glr = glr + dt * (ar * gar + ai * gai)
        gli = gli + dt * (ar * gai - ai * gar)
        wr, wi = lr * ar - li * ai, lr * ai + li * ar
        gdt = jnp.sum(wr * gar + wi * gai, axis=1, keepdims=True)
        dlr_ref[...] = glr
        dli_ref[...] = gli
        dls_ref[...] = gdt * dt

    return pl.pallas_call(
        body, name="s5_prep_bwd",
        out_shape=(jax.ShapeDtypeStruct((G, P), F32), jax.ShapeDtypeStruct((G, P), F32),
                   jax.ShapeDtypeStruct((G, 1), F32),
                   jax.ShapeDtypeStruct((G, H, P), F32), jax.ShapeDtypeStruct((G, H, P), F32)),
        compiler_params=_params(),
    )(lam_re, lam_im, log_step.reshape(G, 1), b_re_t, b_im_t, d_ar, d_ai, d_bbr, d_bbi)


def _s5_block_mats(bbr_t, bbi_t, c_re, c_im):
    bmat = _s5_expand(bbr_t, bbi_t)
    cmat = jnp.transpose(_s5_expand(c_re, -c_im), (0, 2, 1))
    return bmat.astype(BF16), cmat.astype(BF16)


def _s5_diag_mask():
    r = lax.broadcasted_iota(jnp.int32, (LANES, 2 * STATE_COLS), 0) // SSM_GROUP
    c = (lax.broadcasted_iota(jnp.int32, (LANES, 2 * STATE_COLS), 1) % STATE_COLS) // SSM_STATE
    return (r == c).astype(F32)


def _s5_expand(re, im):
    re = jnp.tile(re.reshape(SSM_BLOCKS, LANES, SSM_STATE), (1, 1, GROUPS_PER_BLOCK))
    im = jnp.tile(im.reshape(SSM_BLOCKS, LANES, SSM_STATE), (1, 1, GROUPS_PER_BLOCK))
    return jnp.concatenate([re, im], axis=-1) * _s5_diag_mask()[None]


def _s5_block_diag(dmat):
    d = dmat * _s5_diag_mask()[None]
    parts = []
    for ri in range(2):
        acc = 0.0
        for g in range(GROUPS_PER_BLOCK):
            c0 = ri * STATE_COLS + g * SSM_STATE
            acc = acc + d[:, :, c0:c0 + SSM_STATE]
        parts.append(acc.reshape(SSM_GROUPS, SSM_GROUP, SSM_STATE))
    return jnp.stack(parts)


def _s5_a_rows(ar, ai):
    a = jnp.concatenate([ar.reshape(SSM_BLOCKS, STATE_COLS), ai.reshape(SSM_BLOCKS, STATE_COLS)], axis=1)
    return jnp.broadcast_to(a[:, None, :], (SSM_BLOCKS, SUBLANES, 2 * STATE_COLS))


def _s5_fwd(proj, bmat, cmat, a_rows, d_skip, *, tc=512):
    L = proj.shape[0]
    tc = min(tc, L)
    nt = L // tc
    n8 = tc // SUBLANES
    S = STATE_COLS

    def body(u_ref, b_ref, c_ref, a_ref, d_ref, y_ref, yg_ref, xp_ref, bu_s, xp_s, carry_s):
        @pl.when(pl.program_id(1) == 0)
        def _():
            carry_s[...] = jnp.zeros_like(carry_s)

        u = u_ref[...]
        bu = jnp.dot(u.astype(BF16), b_ref[...], preferred_element_type=F32)
        bu_s[...] = bu.reshape(n8, SUBLANES, 2 * S)
        ar, ai = a_ref[0:1, :S], a_ref[0:1, S:]

        def step(i, carry):
            cr, ci = carry
            for j in range(SUBLANES):
                xp_s[i, j:j + 1, :S] = cr
                xp_s[i, j:j + 1, S:] = ci
                br = bu_s[i, j:j + 1, :S]
                bi = bu_s[i, j:j + 1, S:]
                cr, ci = ar * cr - ai * ci + br, ar * ci + ai * cr + bi
            return cr, ci

        cr, ci = lax.fori_loop(0, n8, step, (carry_s[0:1, :S], carry_s[0:1, S:]))
        carry_s[0:1, :S] = cr
        carry_s[0:1, S:] = ci
        xp = xp_s[...].reshape(tc, 2 * S)
        xp_ref[...] = xp
        x_re = ar * xp[:, :S] - ai * xp[:, S:] + bu[:, :S]
        x_im = ar * xp[:, S:] + ai * xp[:, :S] + bu[:, S:]
        xs = jnp.concatenate([x_re, x_im], axis=1).astype(BF16)
        y = jnp.dot(xs, c_ref[...], preferred_element_type=F32) + d_ref[...] * u
        y_ref[...] = y
        yg_ref[...] = _gelu(y).astype(BF16)

    return pl.pallas_call(
        body, name="s5_fwd",
        out_shape=(jax.ShapeDtypeStruct((L, MAIN_WIDTH), F32),
                   jax.ShapeDtypeStruct((L, MAIN_WIDTH), BF16),
                   jax.ShapeDtypeStruct((L, SSM_BLOCKS * 2 * S), F32)),
        grid=(SSM_BLOCKS, nt),
        in_specs=[pl.BlockSpec((tc, LANES), lambda b, t: (t, b)),
                  pl.BlockSpec((None, LANES, 2 * S), lambda b, t: (b, 0, 0)),
                  pl.BlockSpec((None, 2 * S, LANES), lambda b, t: (b, 0, 0)),
                  pl.BlockSpec((None, SUBLANES, 2 * S), lambda b, t: (b, 0, 0)),
                  pl.BlockSpec((1, LANES), lambda b, t: (0, b))],
        out_specs=(pl.BlockSpec((tc, LANES), lambda b, t: (t, b)),
                   pl.BlockSpec((tc, LANES), lambda b, t: (t, b)),
                   pl.BlockSpec((tc, 2 * S), lambda b, t: (t, b))),
        scratch_shapes=[pltpu.VMEM((n8, SUBLANES, 2 * S), F32),
                        pltpu.VMEM((n8, SUBLANES, 2 * S), F32),
                        pltpu.VMEM((SUBLANES, 2 * S), F32)],
        compiler_params=_params("parallel", "arbitrary"),
    )(proj, bmat, cmat, a_rows, d_skip.reshape(1, MAIN_WIDTH))


def _s5_bwd(proj, dy, xp, bmat, cmat, a_rows, d_skip, *, tc=512):
    L = proj.shape[0]
    tc = min(tc, L)
    nt = L // tc
    n8 = tc // SUBLANES
    S = STATE_COLS
    nn = (((1,), (1,)), ((), ()))
    tn = (((0,), (0,)), ((), ()))

    def body(u_ref, dy_ref, xp_ref, b_ref, c_ref, a_ref, d_ref,
             du_ref, db_ref, dc_ref, da_ref, dd_ref, dl_s, carry_s):
        @pl.when(pl.program_id(1) == 0)
        def _():
            carry_s[...] = jnp.zeros_like(carry_s)
            db_ref[...] = jnp.zeros_like(db_ref)
            dc_ref[...] = jnp.zeros_like(dc_ref)
            da_ref[...] = jnp.zeros_like(da_ref)
            dd_ref[...] = jnp.zeros_like(dd_ref)

        u = u_ref[...]
        dy = dy_ref[...]
        xp = xp_ref[...]
        ub = u.astype(BF16)
        dyb = dy.astype(BF16)
        ar, ai = a_ref[0:1, :S], a_ref[0:1, S:]
        bu = jnp.dot(ub, b_ref[...], preferred_element_type=F32)
        x_re = ar * xp[:, :S] - ai * xp[:, S:] + bu[:, :S]
        x_im = ar * xp[:, S:] + ai * xp[:, :S] + bu[:, S:]
        xs = jnp.concatenate([x_re, x_im], axis=1).astype(BF16)
        dc_ref[...] += lax.dot_general(dyb, xs, tn, preferred_element_type=F32)
        dx = lax.dot_general(dyb, c_ref[...], nn, preferred_element_type=F32)
        dl_s[...] = dx.reshape(n8, SUBLANES, 2 * S)

        def step(k, carry):
            cr, ci = carry
            i = n8 - 1 - k
            for j in range(SUBLANES - 1, -1, -1):
                lr = dl_s[i, j:j + 1, :S] + (ar * cr + ai * ci)
                li = dl_s[i, j:j + 1, S:] + (ar * ci - ai * cr)
                dl_s[i, j:j + 1, :S] = lr
                dl_s[i, j:j + 1, S:] = li
                cr, ci = lr, li
            return cr, ci

        cr, ci = lax.fori_loop(0, n8, step, (carry_s[0:1, :S], carry_s[0:1, S:]))
        carry_s[0:1, :S] = cr
        carry_s[0:1, S:] = ci
        lam = dl_s[...].reshape(tc, 2 * S)
        l_re, l_im = lam[:, :S], lam[:, S:]
        da_ref[0:1, :S] += jnp.sum(l_re * xp[:, :S] + l_im * xp[:, S:], axis=0, keepdims=True)
        da_ref[0:1, S:] += jnp.sum(l_im * xp[:, :S] - l_re * xp[:, S:], axis=0, keepdims=True)
        lamb = lam.astype(BF16)
        du_ref[...] = lax.dot_general(lamb, b_ref[...], nn, preferred_element_type=F32) + d_ref[...] * dy
        db_ref[...] += lax.dot_general(ub, lamb, tn, preferred_element_type=F32)
        dd_ref[0:1, :] += jnp.sum(dy * u, axis=0, keepdims=True)

    rev = lambda b, t: (nt - 1 - t, b)
    return pl.pallas_call(
        body, name="s5_bwd",
        out_shape=(jax.ShapeDtypeStruct((L, MAIN_WIDTH), F32),
                   jax.ShapeDtypeStruct((SSM_BLOCKS, LANES, 2 * S), F32),
                   jax.ShapeDtypeStruct((SSM_BLOCKS, LANES, 2 * S), F32),
                   jax.ShapeDtypeStruct((SSM_BLOCKS, SUBLANES, 2 * S), F32),
                   jax.ShapeDtypeStruct((SUBLANES, MAIN_WIDTH), F32)),
        grid=(SSM_BLOCKS, nt),
        in_specs=[pl.BlockSpec((tc, LANES), rev),
                  pl.BlockSpec((tc, LANES), rev),
                  pl.BlockSpec((tc, 2 * S), rev),
                  pl.BlockSpec((None, LANES, 2 * S), lambda b, t: (b, 0, 0)),
                  pl.BlockSpec((None, 2 * S, LANES), lambda b, t: (b, 0, 0)),
                  pl.BlockSpec((None, SUBLANES, 2 * S), lambda b, t: (b, 0, 0)),
                  pl.BlockSpec((1, LANES), lambda b, t: (0, b))],
        out_specs=(pl.BlockSpec((tc, LANES), rev),
                   pl.BlockSpec((None, LANES, 2 * S), lambda b, t: (b, 0, 0)),
                   pl.BlockSpec((None, LANES, 2 * S), lambda b, t: (b, 0, 0)),
                   pl.BlockSpec((None, SUBLANES, 2 * S), lambda b, t: (b, 0, 0)),
                   pl.BlockSpec((SUBLANES, LANES), lambda b, t: (0, b))),
        scratch_shapes=[pltpu.VMEM((n8, SUBLANES, 2 * S), F32),
                        pltpu.VMEM((SUBLANES, 2 * S), F32)],
        compiler_params=_params("parallel", "arbitrary"),
    )(proj, dy, xp, bmat, cmat, a_rows, d_skip.reshape(1, MAIN_WIDTH))


def _row_specs(tr):
    main = pl.BlockSpec((tr, MAIN_WIDTH), lambda i: (i, 0))
    z = pl.BlockSpec((tr, MAIN_WIDTH), lambda i: (i, 1))
    zm = pl.BlockSpec((tr, MEM_WIDTH), lambda i: (i, IN_WIDTH // MEM_WIDTH - 1))
    mem = pl.BlockSpec((tr, MEM_WIDTH), lambda i: (i, 0))
    cat = pl.BlockSpec((tr, D_MODEL), lambda i: (i, 0))
    vec = pl.BlockSpec((1, MAIN_WIDTH), lambda i: (0, 0))
    return main, z, zm, mem, cat, vec


def _gate_a_fwd(y, t, b_glu, proj, o_mem, *, tr=256):
    L = y.shape[0]
    tr = min(tr, L)

    def body(y_ref, t_ref, b_ref, z_ref, zm_ref, om_ref, o_ref):
        yg = _gelu(y_ref[...])
        sz, _ = _silu_and_grad(z_ref[...])
        o_ref[:, :MAIN_WIDTH] = (yg * _sigmoid(t_ref[...] + b_ref[...]) * sz).astype(BF16)
        szm, _ = _silu_and_grad(zm_ref[...])
        o_ref[:, MAIN_WIDTH:] = (om_ref[...] * szm).astype(BF16)

    main, z, zm, mem, cat, vec = _row_specs(tr)
    return pl.pallas_call(
        body, name="gate_a_fwd", out_shape=jax.ShapeDtypeStruct((L, D_MODEL), BF16),
        grid=(L // tr,), in_specs=[main, main, vec, z, zm, mem], out_specs=cat,
        compiler_params=_params("parallel"),
    )(y, t, b_glu.reshape(1, MAIN_WIDTH), proj, proj, o_mem)


def _gate_a_bwd(dcat, y, t, b_glu, proj, o_mem, *, tr=256):
    L = y.shape[0]
    tr = min(tr, L)

    def body(dc_ref, y_ref, t_ref, b_ref, z_ref, zm_ref, om_ref,
             dz_ref, dzm_ref, dt_ref, dyg_ref, dom_ref, db_ref):
        dmain = dc_ref[:, :MAIN_WIDTH]
        dmemo = dc_ref[:, MAIN_WIDTH:]
        yg = _gelu(y_ref[...])
        sg = _sigmoid(t_ref[...] + b_ref[...])
        sz, gz = _silu_and_grad(z_ref[...])
        dz_ref[...] = (dmain * (yg * sg) * gz).astype(BF16)
        dy2 = dmain * sz
        dyg_ref[...] = dy2 * sg
        dt = dy2 * yg * (sg * (1.0 - sg))
        dt_ref[...] = dt.astype(BF16)

        @pl.when(pl.program_id(0) == 0)
        def _():
            db_ref[...] = jnp.zeros_like(db_ref)

        db_ref[...] += jnp.sum(dt, axis=0, keepdims=True)
        szm, gzm = _silu_and_grad(zm_ref[...])
        dom_ref[...] = dmemo * szm
        dzm_ref[...] = (dmemo * om_ref[...] * gzm).astype(BF16)

    main, z, zm, mem, cat, vec = _row_specs(tr)
    outs = pl.pallas_call(
        body, name="gate_a_bwd",
        out_shape=(jax.ShapeDtypeStruct((L, MAIN_WIDTH), BF16), jax.ShapeDtypeStruct((L, MEM_WIDTH), BF16),
                   jax.ShapeDtypeStruct((L, MAIN_WIDTH), BF16), jax.ShapeDtypeStruct((L, MAIN_WIDTH), F32),
                   jax.ShapeDtypeStruct((L, MEM_WIDTH), F32), jax.ShapeDtypeStruct((1, MAIN_WIDTH), F32)),
        grid=(L // tr,), in_specs=[cat, main, main, vec, z, zm, mem],
        out_specs=(main, mem, main, main, mem, vec),
        compiler_params=_params("arbitrary"),
    )(dcat, y, t, b_glu.reshape(1, MAIN_WIDTH), proj, proj, o_mem)
    return outs


def _gelu_bwd(dyg_a, dyg_b, y, *, tr=256):
    L = y.shape[0]
    tr = min(tr, L)

    def body(a_ref, b_ref, y_ref, o_ref):
        o_ref[...] = (a_ref[...] + b_ref[...]) * _gelu_grad(y_ref[...])

    main = pl.BlockSpec((tr, MAIN_WIDTH), lambda i: (i, 0))
    return pl.pallas_call(
        body, name="gelu_bwd", out_shape=jax.ShapeDtypeStruct((L, MAIN_WIDTH), F32),
        grid=(L // tr,), in_specs=[main, main, main], out_specs=main,
        compiler_params=_params("parallel"),
    )(dyg_a, dyg_b, y)


def _gate_b_fwd(att, proj, o_mem, *, tr=256):
    L = att.shape[0]
    tr = min(tr, L)

    def body(a_ref, z_ref, zm_ref, om_ref, o_ref):
        sz, _ = _silu_and_grad(z_ref[...])
        o_ref[:, :MAIN_WIDTH] = (a_ref[...] * sz).astype(BF16)
        szm, _ = _silu_and_grad(zm_ref[...])
        o_ref[:, MAIN_WIDTH:] = (om_ref[...] * szm).astype(BF16)

    main, z, zm, mem, cat, _ = _row_specs(tr)
    return pl.pallas_call(
        body, name="gate_b_fwd", out_shape=jax.ShapeDtypeStruct((L, D_MODEL), BF16),
        grid=(L // tr,), in_specs=[main, z, zm, mem], out_specs=cat,
        compiler_params=_params("parallel"),
    )(att, proj, proj, o_mem)


def _gate_b_bwd(dcat, att, proj, o_mem, *, tr=256):
    L = att.shape[0]
    tr = min(tr, L)

    def body(dc_ref, a_ref, z_ref, zm_ref, om_ref, da_ref, dz_ref, dom_ref, dzm_ref, dl_ref):
        dmain = dc_ref[:, :MAIN_WIDTH]
        dmemo = dc_ref[:, MAIN_WIDTH:]
        att = a_ref[...]
        sz, gz = _silu_and_grad(z_ref[...])
        datt = dmain * sz
        da_ref[...] = datt
        dz_ref[...] = (dmain * att * gz).astype(BF16)
        szm, gzm = _silu_and_grad(zm_ref[...])
        dom_ref[...] = dmemo * szm
        dzm_ref[...] = (dmemo * om_ref[...] * gzm).astype(BF16)
        prod = datt * att
        for h in range(FOX_HEADS):
            dl_ref[h] = jnp.sum(prod[:, h * HEAD_DIM:(h + 1) * HEAD_DIM], axis=1, keepdims=True)

    main, z, zm, mem, cat, _ = _row_specs(tr)
    delta = pl.BlockSpec((FOX_HEADS, tr, 1), lambda i: (0, i, 0))
    return pl.pallas_call(
        body, name="gate_b_bwd",
        out_shape=(jax.ShapeDtypeStruct((L, MAIN_WIDTH), F32), jax.ShapeDtypeStruct((L, MAIN_WIDTH), BF16),
                   jax.ShapeDtypeStruct((L, MEM_WIDTH), F32), jax.ShapeDtypeStruct((L, MEM_WIDTH), BF16),
                   jax.ShapeDtypeStruct((FOX_HEADS, L, 1), F32)),
        grid=(L // tr,), in_specs=[cat, main, z, zm, mem], out_specs=(main, main, mem, mem, delta),
        compiler_params=_params("parallel"),
    )(dcat, att, proj, proj, o_mem)


_MEM_Q_COL = (2 * MAIN_WIDTH) // HEAD_DIM
_NT = (((1,), (1,)), ((), ()))
_TN = (((0,), (0,)), ((), ()))


def _mem_probs(q_ref, k_ref):
    qs = (q_ref[...] * (HEAD_DIM ** -0.5)).astype(BF16)
    s = lax.dot_general(qs, k_ref[...].astype(BF16), _NT, preferred_element_type=F32)
    e = jnp.exp(s - jnp.max(s, axis=-1, keepdims=True))
    return qs, e / jnp.sum(e, axis=-1, keepdims=True)


def _mem_attn_fwd(proj, kvm, *, tq=512):
    L = proj.shape[0]
    tq = min(tq, L)

    def body(q_ref, k_ref, v_ref, o_ref):
        _, p = _mem_probs(q_ref, k_ref)
        o_ref[...] = jnp.dot(p.astype(BF16), v_ref[...].astype(BF16), preferred_element_type=F32)

    return pl.pallas_call(
        body, name="mem_attn_fwd", out_shape=jax.ShapeDtypeStruct((L, MEM_WIDTH), F32),
        grid=(MEM_HEADS, L // tq),
        in_specs=[pl.BlockSpec((tq, HEAD_DIM), lambda h, i: (i, _MEM_Q_COL + h)),
                  pl.BlockSpec((N_MEM, HEAD_DIM), lambda h, i: (0, h)),
                  pl.BlockSpec((N_MEM, HEAD_DIM), lambda h, i: (0, MEM_HEADS + h))],
        out_specs=pl.BlockSpec((tq, HEAD_DIM), lambda h, i: (i, h)),
        compiler_params=_params("parallel", "parallel"),
    )(proj, kvm, kvm)


def _mem_attn_bwd(proj, kvm, do, *, tq=512):
    L = proj.shape[0]
    tq = min(tq, L)

    def body(q_ref, k_ref, v_ref, do_ref, dq_ref, dk_ref, dv_ref):
        @pl.when(pl.program_id(1) == 0)
        def _():
            dk_ref[...] = jnp.zeros_like(dk_ref)
            dv_ref[...] = jnp.zeros_like(dv_ref)

        qs, p = _mem_probs(q_ref, k_ref)
        dob = do_ref[...].astype(BF16)
        dp = lax.dot_general(dob, v_ref[...].astype(BF16), _NT, preferred_element_type=F32)
        ds = p * (dp - jnp.sum(p * dp, axis=-1, keepdims=True))
        dsb = ds.astype(BF16)
        dq = jnp.dot(dsb, k_ref[...].astype(BF16), preferred_element_type=F32) * (HEAD_DIM ** -0.5)
        dq_ref[...] = dq.astype(BF16)
        dk_ref[...] += lax.dot_general(dsb, qs, _TN, preferred_element_type=F32)
        dv_ref[...] += lax.dot_general(p.astype(BF16), dob, _TN, preferred_element_type=F32)

    dq, dk, dv = pl.pallas_call(
        body, name="mem_attn_bwd",
        out_shape=(jax.ShapeDtypeStruct((L, MEM_WIDTH), BF16),
                   jax.ShapeDtypeStruct((N_MEM, MEM_WIDTH), F32),
                   jax.ShapeDtypeStruct((N_MEM, MEM_WIDTH), F32)),
        grid=(MEM_HEADS, L // tq),
        in_specs=[pl.BlockSpec((tq, HEAD_DIM), lambda h, i: (i, _MEM_Q_COL + h)),
                  pl.BlockSpec((N_MEM, HEAD_DIM), lambda h, i: (0, h)),
                  pl.BlockSpec((N_MEM, HEAD_DIM), lambda h, i: (0, MEM_HEADS + h)),
                  pl.BlockSpec((tq, HEAD_DIM), lambda h, i: (i, h))],
        out_specs=(pl.BlockSpec((tq, HEAD_DIM), lambda h, i: (i, h)),
                   pl.BlockSpec((N_MEM, HEAD_DIM), lambda h, i: (0, h)),
                   pl.BlockSpec((N_MEM, HEAD_DIM), lambda h, i: (0, h))),
        compiler_params=_params("parallel", "arbitrary"),
    )(proj, kvm, kvm, do)
    return dq, jnp.concatenate([dk, dv], axis=1)


def _tile_cumsum(x, row, reverse):
    for sh in (1, 2, 4):
        if reverse:
            x = x + jnp.where(row < SUBLANES - sh, pltpu.roll(x, SUBLANES - sh, 0), 0.0)
        else:
            x = x + jnp.where(row >= sh, pltpu.roll(x, sh, 0), 0.0)
    return x


def _fgate_fwd(pre, b_pad):
    L = pre.shape[0]
    n8 = L // SUBLANES

    def body(p_ref, b_ref, o_ref):
        row = lax.broadcasted_iota(jnp.int32, (SUBLANES, LANES), 0)
        b = b_ref[...]

        def step(i, carry):
            x = p_ref[i] + b
            logf = jnp.minimum(x, 0.0) - jnp.log(1.0 + jnp.exp(-jnp.abs(x)))
            t = _tile_cumsum(logf, row, False) + carry
            o_ref[i] = t
            return t[SUBLANES - 1:SUBLANES, :]

        lax.fori_loop(0, n8, step, jnp.zeros((1, LANES), F32))

    out = pl.pallas_call(
        body, name="fgate_fwd", out_shape=jax.ShapeDtypeStruct((n8, SUBLANES, LANES), F32),
        compiler_params=_params(),
    )(pre.reshape(n8, SUBLANES, LANES), b_pad.reshape(1, LANES))
    return out.reshape(L, LANES)


def _fgate_bwd(dfcum, pre, b_pad):
    L = pre.shape[0]
    n8 = L // SUBLANES

    def body(d_ref, p_ref, b_ref, o_ref, s_ref):
        row = lax.broadcasted_iota(jnp.int32, (SUBLANES, LANES), 0)
        b = b_ref[...]

        def step(k, carry):
            c, acc = carry
            i = n8 - 1 - k
            t = _tile_cumsum(d_ref[i], row, True) + c
            dpre = t * _sigmoid(-(p_ref[i] + b))
            o_ref[i] = dpre
            return t[0:1, :], acc + dpre

        _, acc = lax.fori_loop(0, n8, step, (jnp.zeros((1, LANES), F32), jnp.zeros((SUBLANES, LANES), F32)))
        s_ref[...] = jnp.sum(acc, axis=0, keepdims=True)

    dpre, db = pl.pallas_call(
        body, name="fgate_bwd",
        out_shape=(jax.ShapeDtypeStruct((n8, SUBLANES, LANES), F32), jax.ShapeDtypeStruct((1, LANES), F32)),
        compiler_params=_params(),
    )(dfcum.reshape(n8, SUBLANES, LANES), pre.reshape(n8, SUBLANES, LANES), b_pad.reshape(1, LANES))
    return dpre.reshape(L, LANES), db


FOX_BLOCK = 512


def _fox_scores(qs, k, fq, fk, diagonal):
    s = lax.dot_general(qs, k, _NT, preferred_element_type=F32) + fq - fk
    if diagonal:
        row = lax.broadcasted_iota(jnp.int32, s.shape, 0)
        col = lax.broadcasted_iota(jnp.int32, s.shape, 1)
        s = jnp.where(row >= col, s, NEG_BIG)
    return s


def _fox_specs(tq, L):
    nq = L // tq
    return dict(
        rows=lambda off: pl.BlockSpec((tq, HEAD_DIM), lambda h, i: (i, off + h)),
        seq=lambda off: pl.BlockSpec((L, HEAD_DIM), lambda h, i: (0, off + h)),
        col=pl.BlockSpec((None, None, tq, 1), lambda h, i: (h, i, 0, 0)),
        col_all=pl.BlockSpec((None, nq, tq, 1), lambda h, i: (h, 0, 0, 0)),
        row=pl.BlockSpec((None, None, 1, tq), lambda h, i: (h, i, 0, 0)),
        row_all=pl.BlockSpec((None, nq, 1, tq), lambda h, i: (h, 0, 0, 0)))


def _fox_fwd(proj, kv, fq, fk):
    L = proj.shape[0]
    tq = min(FOX_BLOCK, L)
    nq = L // tq
    sp = _fox_specs(tq, L)

    def body(q_ref, k_ref, v_ref, fq_ref, fk_ref, o_ref, lse_ref, m_s, l_s, acc_s):
        qi = pl.program_id(1)
        qs = (q_ref[...] * (HEAD_DIM ** -0.5)).astype(BF16)
        fq = fq_ref[...]
        m_s[...] = jnp.full_like(m_s, NEG_BIG)
        l_s[...] = jnp.zeros_like(l_s)
        acc_s[...] = jnp.zeros_like(acc_s)

        def block(j, diagonal):
            r0 = pl.multiple_of(j * tq, tq)
            s = _fox_scores(qs, k_ref[pl.ds(r0, tq), :], fq, fk_ref[j], diagonal)
            m_new = jnp.maximum(m_s[...], jnp.max(s, axis=-1, keepdims=True))
            alpha = jnp.exp(m_s[...] - m_new)
            p = jnp.exp(s - m_new)
            l_s[...] = alpha * l_s[...] + jnp.sum(p, axis=-1, keepdims=True)
            acc_s[...] = alpha * acc_s[...] + jnp.dot(p.astype(BF16), v_ref[pl.ds(r0, tq), :],
                                                      preferred_element_type=F32)
            m_s[...] = m_new

        def below(j, carry):
            block(j, False)
            return carry

        lax.fori_loop(0, qi, below, 0)
        block(qi, True)
        o_ref[...] = acc_s[...] / l_s[...]
        lse_ref[...] = m_s[...] + jnp.log(l_s[...])

    return pl.pallas_call(
        body, name="fox_fwd",
        out_shape=(jax.ShapeDtypeStruct((L, MAIN_WIDTH), F32),
                   jax.ShapeDtypeStruct((FOX_HEADS, nq, tq, 1), F32)),
        grid=(FOX_HEADS, nq),
        in_specs=[sp["rows"](0), sp["seq"](0), sp["seq"](FOX_HEADS), sp["col"], sp["row_all"]],
        out_specs=(sp["rows"](0), sp["col"]),
        scratch_shapes=[pltpu.VMEM((tq, 1), F32), pltpu.VMEM((tq, 1), F32), pltpu.VMEM((tq, HEAD_DIM), F32)],
        compiler_params=_params("parallel", "parallel"),
    )(proj, kv, kv, fq, fk)


def _fox_bwd_dq(proj, kv, fq, fk, lse, delta, datt):
    L = proj.shape[0]
    tq = min(FOX_BLOCK, L)
    nq = L // tq
    sp = _fox_specs(tq, L)

    def body(q_ref, k_ref, v_ref, fq_ref, fk_ref, lse_ref, dl_ref, do_ref, dq_ref, df_ref, acc_s, df_s):
        qi = pl.program_id(1)
        qs = (q_ref[...] * (HEAD_DIM ** -0.5)).astype(BF16)
        dob = do_ref[...].astype(BF16)
        fq, lse, dl = fq_ref[...], lse_ref[...], dl_ref[...]
        acc_s[...] = jnp.zeros_like(acc_s)
        df_s[...] = jnp.zeros_like(df_s)

        def block(j, diagonal):
            r0 = pl.multiple_of(j * tq, tq)
            k = k_ref[pl.ds(r0, tq), :]
            p = jnp.exp(_fox_scores(qs, k, fq, fk_ref[j], diagonal) - lse)
            dp = lax.dot_general(dob, v_ref[pl.ds(r0, tq), :], _NT, preferred_element_type=F32)
            ds = p * (dp - dl)
            acc_s[...] += jnp.dot(ds.astype(BF16), k, preferred_element_type=F32)
            df_s[...] += jnp.sum(ds, axis=1, keepdims=True)

        def below(j, carry):
            block(j, False)
            return carry

        lax.fori_loop(0, qi, below, 0)
        block(qi, True)
        dq_ref[...] = (acc_s[...] * (HEAD_DIM ** -0.5)).astype(BF16)
        df_ref[...] = df_s[...]

    return pl.pallas_call(
        body, name="fox_bwd_dq",
        out_shape=(jax.ShapeDtypeStruct((L, MAIN_WIDTH), BF16),
                   jax.ShapeDtypeStruct((FOX_HEADS, nq, tq, 1), F32)),
        grid=(FOX_HEADS, nq),
        in_specs=[sp["rows"](0), sp["seq"](0), sp["seq"](FOX_HEADS), sp["col"], sp["row_all"],
                  sp["col"], sp["col"], sp["rows"](0)],
        out_specs=(sp["rows"](0), sp["col"]),
        scratch_shapes=[pltpu.VMEM((tq, HEAD_DIM), F32), pltpu.VMEM((tq, 1), F32)],
        compiler_params=_params("parallel", "parallel"),
    )(proj, kv, kv, fq, fk, lse, delta, datt)


def _fox_bwd_dkv(proj, kv, fq, fk, lse, delta, datt):
    L = proj.shape[0]
    tq = min(FOX_BLOCK, L)
    nq = L // tq
    sp = _fox_specs(tq, L)

    def body(q_ref, k_ref, v_ref, fq_ref, fk_ref, lse_ref, dl_ref, do_ref,
             dk_ref, dv_ref, df_ref, dk_s, dv_s, df_s):
        ki = pl.program_id(1)
        k, v, fk = k_ref[...], v_ref[...], fk_ref[...]
        dk_s[...] = jnp.zeros_like(dk_s)
        dv_s[...] = jnp.zeros_like(dv_s)
        df_s[...] = jnp.zeros_like(df_s)

        def block(i, diagonal):
            r0 = pl.multiple_of(i * tq, tq)
            qs = (q_ref[pl.ds(r0, tq), :] * (HEAD_DIM ** -0.5)).astype(BF16)
            dob = do_ref[pl.ds(r0, tq), :].astype(BF16)
            p = jnp.exp(_fox_scores(qs, k, fq_ref[i], fk, diagonal) - lse_ref[i])
            dp = lax.dot_general(dob, v, _NT, preferred_element_type=F32)
            ds = p * (dp - dl_ref[i])
            dv_s[...] += lax.dot_general(p.astype(BF16), dob, _TN, preferred_element_type=F32)
            dk_s[...] += lax.dot_general(ds.astype(BF16), qs, _TN, preferred_element_type=F32)
            df_s[...] -= jnp.sum(ds, axis=0, keepdims=True)

        def above(i, carry):
            block(i, False)
            return carry

        block(ki, True)
        lax.fori_loop(ki + 1, nq, above, 0)
        dk_ref[...] = dk_s[...].astype(BF16)
        dv_ref[...] = dv_s[...].astype(BF16)
        df_ref[...] = df_s[...]

    return pl.pallas_call(
        body, name="fox_bwd_dkv",
        out_shape=(jax.ShapeDtypeStruct((L, MAIN_WIDTH), BF16),
                   jax.ShapeDtypeStruct((L, MAIN_WIDTH), BF16),
                   jax.ShapeDtypeStruct((FOX_HEADS, nq, 1, tq), F32)),
        grid=(FOX_HEADS, nq),
        in_specs=[sp["seq"](0), sp["rows"](0), sp["rows"](FOX_HEADS), sp["col_all"], sp["row"],
                  sp["col_all"], sp["col_all"], sp["seq"](0)],
        out_specs=(sp["rows"](0), sp["rows"](0), sp["row"]),
        scratch_shapes=[pltpu.VMEM((tq, HEAD_DIM), F32), pltpu.VMEM((tq, HEAD_DIM), F32),
                        pltpu.VMEM((1, tq), F32)],
        compiler_params=_params("parallel", "parallel"),
    )(proj, kv, kv, fq, fk, lse, delta, datt)


def _pad_lanes(a):
    return jnp.pad(a, ((0, 0), (0, LANES - a.shape[1])))


def _mem_branch_fwd(mem, g, w_mk, proj, tag):
    memn = _rmsnorm_fwd(mem, g, name="mem_norm_" + tag, out_dtype=BF16)
    kvm = _mm(memn, w_mk, name="mem_kv_" + tag)
    return memn, kvm, _mem_attn_fwd(proj, kvm)


def _mem_branch_bwd(mem, g, w_mk, proj, memn, kvm, do_mem, tag):
    dqm, dkvm = _mem_attn_bwd(proj, kvm, do_mem)
    dkvm = dkvm.astype(BF16)
    dw_mk = _mm(memn, dkvm, ta=True, name="dw_mem_kv_" + tag, out_dtype=BF16)
    dmemn = _mm(dkvm, w_mk, tb=True, name="dmemn_" + tag)
    _, dg = _rmsnorm_bwd(mem, g, dmemn, name="mem_norm_bwd_" + tag, dx_dtype=BF16)
    return dqm, dw_mk, dg


def _local_step(x, mem, target, w, fetch=None, grads_ready=None):
    if grads_ready is None:
        grads_ready = lambda group, grads, token: token
    L = x.shape[0]
    g = {}
    w = dict(w)

    b_re_t = jnp.transpose(w["b_re"], (0, 2, 1))
    b_im_t = jnp.transpose(w["b_im"], (0, 2, 1))
    ar, ai, bbr_t, bbi_t = _s5_prep(w["lam_re"], w["lam_im"], w["log_step"], b_re_t, b_im_t)
    bmat, cmat = _s5_block_mats(bbr_t, bbi_t, w["c_re"], w["c_im"])
    a_rows = _s5_a_rows(ar, ai)

    hn0 = _rmsnorm_fwd(x, w["pre_norm_g"][0], name="pre_norm_0", out_dtype=BF16)
    proj_a = _mm(hn0, w["w_in_a"], name="in_proj_a")
    y, yg, xp = _s5_fwd(proj_a, bmat, cmat, a_rows, w["d_skip"])
    if fetch is not None:
        w.update(fetch("b", yg))
    t = _mm(yg, w["w_glu"], name="glu_proj")
    memn0, kvm0, om0 = _mem_branch_fwd(mem, w["mem_norm_g"][0], w["w_mem_kv"][0], proj_a, "0")
    cat0 = _gate_a_fwd(y, t, w["b_glu"], proj_a, om0)
    o0 = _mm(cat0, w["w_out"][0], name="out_proj_0")
    h1 = _rmsnorm_fwd(o0, w["post_norm_g"][0], res=x, name="post_norm_0")

    kv_in = _rmsnorm_fwd(h1, w["kv_norm_g"], name="kv_norm", out_dtype=BF16)
    if fetch is not None:
        w.update(fetch("c", kv_in))
    kv = _mm(kv_in, w["w_kv"], name="kv_proj", out_dtype=BF16)
    pre_f = _mm(kv_in, w["w_fgate"], name="fgate_proj")
    b_f = jnp.pad(w["b_fgate"], (0, LANES - FOX_HEADS))
    fcum = _fgate_fwd(pre_f, b_f)
    fc = jnp.transpose(fcum[:, :FOX_HEADS])
    tq = min(FOX_BLOCK, L)
    fq, fk = fc.reshape(FOX_HEADS, L // tq, tq, 1), fc.reshape(FOX_HEADS, L // tq, 1, tq)

    hn1 = _rmsnorm_fwd(h1, w["pre_norm_g"][1], name="pre_norm_1", out_dtype=BF16)
    proj_b = _mm(hn1, w["w_in_b"], name="in_proj_b")
    att, lse = _fox_fwd(proj_b, kv, fq, fk)
    memn1, kvm1, om1 = _mem_branch_fwd(mem, w["mem_norm_g"][1], w["w_mem_kv"][1], proj_b, "1")
    cat1 = _gate_b_fwd(att, proj_b, om1)
    o1 = _mm(cat1, w["w_out"][1], name="out_proj_1")
    h2 = _rmsnorm_fwd(o1, w["post_norm_g"][1], res=h1, name="post_norm_1")

    dh2, loss_row = _loss_head(h2, target)

    do1, dpost1 = _rmsnorm_bwd(o1, w["post_norm_g"][1], dh2, name="post_norm_bwd_1", dx_dtype=BF16)
    dcat1 = _mm(do1, w["w_out"][1], tb=True, name="dcat_1")
    g["w_out_1"] = _mm(cat1, do1, ta=True, name="dw_out_1", out_dtype=BF16)
    datt, dz1, dom1, dzm1, delta = _gate_b_bwd(dcat1, att, proj_b, om1)
    dqm1, g["w_mem_kv_1"], dmemg1 = _mem_branch_bwd(mem, w["mem_norm_g"][1], w["w_mem_kv"][1], proj_b,
                                                   memn1, kvm1, dom1, "1")
    delta = delta.reshape(fq.shape)
    dq, dfq = _fox_bwd_dq(proj_b, kv, fq, fk, lse, delta, datt)
    dk, dv, dfk = _fox_bwd_dkv(proj_b, kv, fq, fk, lse, delta, datt)
    dproj_b = jnp.concatenate([dq, dz1, dqm1, dzm1], axis=1)
    g["w_in_b"] = _mm(hn1, dproj_b, ta=True, name="dw_in_b", out_dtype=BF16, shards=N_CHIPS)
    dhn1 = _mm(dproj_b, w["w_in_b"], tb=True, name="dhn_1")

    dkv = jnp.concatenate([dk, dv], axis=1)
    g["w_kv"] = _mm(kv_in, dkv, ta=True, name="dw_kv", out_dtype=BF16, shards=N_CHIPS)
    dkv_in_a = _mm(dkv, w["w_kv"], tb=True, name="dkv_in_kv")
    dfcum = _pad_lanes(jnp.transpose(dfq.reshape(FOX_HEADS, L) + dfk.reshape(FOX_HEADS, L)))
    dpre_f, db_f = _fgate_bwd(dfcum, pre_f, b_f)
    g["b_fgate"] = db_f[0, :FOX_HEADS]
    g["w_fgate"] = _mm(kv_in, dpre_f, ta=True, name="dw_fgate")[:, :FOX_HEADS]
    dkv_in_b = _mm(dpre_f, w["w_fgate"], tb=True, name="dkv_in_fgate")
    dh1_kv, g["kv_norm_g"] = _rmsnorm_bwd(h1, w["kv_norm_g"], (dkv_in_a, dkv_in_b), name="kv_norm_bwd")
    dh1, dpre1 = _rmsnorm_bwd(h1, w["pre_norm_g"][1], dhn1, adds=(dh2, dh1_kv), name="pre_norm_bwd_1")
    dh1 = grads_ready("b", g, dh1)

    do0, dpost0 = _rmsnorm_bwd(o0, w["post_norm_g"][0], dh1, name="post_norm_bwd_0", dx_dtype=BF16)
    dcat0 = _mm(do0, w["w_out"][0], tb=True, name="dcat_0")
    g["w_out_0"] = _mm(cat0, do0, ta=True, name="dw_out_0", out_dtype=BF16)
    dz0, dzm0, dt, dyg_a, dom0, db_glu = _gate_a_bwd(dcat0, y, t, w["b_glu"], proj_a, om0)
    g["b_glu"] = db_glu[0]
    g["w_glu"] = _mm(yg, dt, ta=True, name="dw_glu", out_dtype=BF16)
    dyg_b = _mm(dt, w["w_glu"], tb=True, name="dyg")
    dqm0, g["w_mem_kv_0"], dmemg0 = _mem_branch_bwd(mem, w["mem_norm_g"][0], w["w_mem_kv"][0], proj_a,
                                                   memn0, kvm0, dom0, "0")
    dy = grads_ready("a1", g, _gelu_bwd(dyg_a, dyg_b, y))
    du, db_blk, dc_blk, da_rows, dd_skip = _s5_bwd(proj_a, dy, xp, bmat, cmat, a_rows, w["d_skip"])
    g["d_skip"] = dd_skip[0]
    dproj_a = jnp.concatenate([du.astype(BF16), dz0, dqm0, dzm0], axis=1)
    g["w_in_a"] = _mm(hn0, dproj_a, ta=True, name="dw_in_a", out_dtype=BF16, shards=N_CHIPS)
    dproj_a = grads_ready("a2", g, dproj_a)
    dhn0 = _mm(dproj_a, w["w_in_a"], tb=True, name="dhn_0")
    grad_x, dpre0 = _rmsnorm_bwd(x, w["pre_norm_g"][0], dhn0, adds=(dh1,), name="pre_norm_bwd_0")

    dbb = _s5_block_diag(db_blk)
    dcc = _s5_block_diag(dc_blk)
    g["c_re"], g["c_im"] = dcc[0], -dcc[1]
    d_ar = da_rows[:, 0, :STATE_COLS].reshape(SSM_GROUPS, SSM_STATE)
    d_ai = da_rows[:, 0, STATE_COLS:].reshape(SSM_GROUPS, SSM_STATE)
    dlr, dli, dls, dbr_t, dbi_t = _s5_prep_bwd(w["lam_re"], w["lam_im"], w["log_step"], b_re_t, b_im_t,
                                               d_ar, d_ai, dbb[0], dbb[1])
    g["lam_re"], g["lam_im"], g["log_step"] = dlr, dli, dls[:, 0]
    g["b_re"] = jnp.transpose(dbr_t, (0, 2, 1))
    g["b_im"] = jnp.transpose(dbi_t, (0, 2, 1))
    g["pre_norm_g"] = jnp.stack([dpre0, dpre1])
    g["post_norm_g"] = jnp.stack([dpost0, dpost1])
    g["mem_norm_g"] = jnp.stack([dmemg0, dmemg1])
    return loss_row, grad_x, g


_MESH = pl.DeviceIdType.MESH
_ANY = pl.BlockSpec(memory_space=pl.ANY)


def _place():
    x, y, c = lax.axis_index("x"), lax.axis_index("y"), lax.axis_index("c")
    chips = [(1 - x, y), (x, 1 - y), (1 - x, 1 - y)]
    return x, y, c, chips


def _all_gather_chips(parts):
    n = len(parts)

    def body(*refs):
        ins, outs = refs[:n], refs[n:2 * n]
        ici_send, ici_recv, d2d_send, d2d_recv = refs[2 * n:]
        x, y, c, chips = _place()
        me = 2 * x + y
        sib = (x, y, 1 - c)

        def ici(i, k, src_chip_j, dst):
            return pltpu.make_async_remote_copy(
                src_ref=ins[i].at[c] if src_chip_j is None else outs[i].at[src_chip_j, c],
                dst_ref=outs[i].at[me if src_chip_j is None else src_chip_j, c],
                send_sem=ici_send.at[i * 3 + k], recv_sem=ici_recv.at[i * 3 + k],
                device_id=dst, device_id_type=_MESH)

        def d2d(i, k, chip_j, half):
            return pltpu.make_async_remote_copy(
                src_ref=outs[i].at[chip_j, half], dst_ref=outs[i].at[chip_j, half],
                send_sem=d2d_send.at[i * 3 + k], recv_sem=d2d_recv.at[i * 3 + k],
                device_id=sib, device_id_type=_MESH)

        sends = [ici(i, k, None, (*chips[k], c)) for i in range(n) for k in range(3)]
        for cp in sends:
            cp.start()
        passed = []
        for k, (cx, cy) in enumerate(chips):
            for i in range(n):
                ici(i, k, 2 * cx + cy, (x, y, c)).wait_recv()
                fwd = d2d(i, k, 2 * cx + cy, c)
                fwd.start()
                passed.append(fwd)
        for k, (cx, cy) in enumerate(chips):
            for i in range(n):
                d2d(i, k, 2 * cx + cy, 1 - c).wait_recv()
        for cp in sends + passed:
            cp.wait_send()

    return pl.pallas_call(
        body, name="all_gather_weights",
        out_shape=[jax.ShapeDtypeStruct((N_CHIPS,) + p.shape, p.dtype) for p in parts],
        in_specs=[_ANY] * n, out_specs=[_ANY] * n,
        scratch_shapes=[pltpu.SemaphoreType.DMA((3 * n,)), pltpu.SemaphoreType.DMA((3 * n,)),
                        pltpu.SemaphoreType.DMA((3 * n,)), pltpu.SemaphoreType.DMA((3 * n,))],
    )(*parts)


_HBM = pl.BlockSpec(memory_space=pltpu.HBM)
_SEM = pl.BlockSpec(memory_space=pltpu.SEMAPHORE)
_SIDE = pltpu.SideEffectType.DATAFLOW_SIDE_EFFECTING


def _in_hbm(a):
    return pltpu.with_memory_space_constraint(a, pltpu.HBM)


def _hbm_like(a):
    return pltpu.HBM(a.shape, a.dtype)


def _ici_copies(srcs, lands, send_sem, recv_sem, src_at, dst_at, wait_at):
    x, y, c, chips = _place()
    start, wait = [], []
    for i in range(len(srcs)):
        for k, (cx, cy) in enumerate(chips):
            sem = dict(send_sem=send_sem.at[3 * i + k], recv_sem=recv_sem.at[3 * i + k],
                       device_id=(cx, cy, c), device_id_type=_MESH)
            src = src_at(srcs[i], 2 * cx + cy, c)
            start.append(pltpu.make_async_remote_copy(src_ref=src, dst_ref=dst_at(lands[i], 2 * x + y, k, c), **sem))
            wait.append(pltpu.make_async_remote_copy(src_ref=src, dst_ref=wait_at(lands[i], 2 * cx + cy, k, c), **sem))
    return start, wait


def _ici_start(srcs, lands, token, route, *, name):
    n = len(srcs)

    def body(*refs):
        start, _ = _ici_copies(refs[:n], refs[n:2 * n], refs[2 * n + 1], refs[2 * n + 2], *route)
        for cp in start:
            cp.start()

    sems = pltpu.SemaphoreType.DMA((3 * n,))
    outs = pl.pallas_call(
        body, name=name,
        out_shape=(sems, sems, *[_hbm_like(a) for a in srcs], *[_hbm_like(a) for a in lands], _hbm_like(token)),
        in_specs=[_HBM] * (2 * n + 1), out_specs=(_SEM, _SEM, *[_HBM] * (2 * n + 1)),
        input_output_aliases={i: 2 + i for i in range(2 * n + 1)},
        compiler_params=pltpu.CompilerParams(has_side_effects=_SIDE),
    )(*[_in_hbm(a) for a in srcs], *[_in_hbm(a) for a in lands], _in_hbm(token))
    return (outs[0], outs[1], list(outs[2:2 + n]), list(outs[2 + n:2 + 2 * n])), outs[2 + 2 * n]


def _ici_wait(handle, after, route, *, name):
    send_sem, recv_sem, srcs, lands = handle
    n = len(srcs)

    def body(*refs):
        _, wait = _ici_copies(refs[:n], refs[n:2 * n], refs[2 * n], refs[2 * n + 1], *route)
        for cp in wait:
            cp.wait_send()
            cp.wait_recv()

    outs = pl.pallas_call(
        body, name=name,
        out_shape=(*[_hbm_like(a) for a in srcs], *[_hbm_like(a) for a in lands]),
        in_specs=[_HBM] * (2 * n) + [_SEM, _SEM, _ANY], out_specs=tuple([_HBM] * (2 * n)),
        input_output_aliases={i: i for i in range(2 * n)},
        compiler_params=pltpu.CompilerParams(has_side_effects=_SIDE),
    )(*srcs, *lands, send_sem, recv_sem, after)
    return list(outs[:n]), list(outs[n:])


_GATHER_ROUTE = (lambda s, j, c: s.at[c], lambda l, me, k, c: l.at[me, c], lambda l, j, k, c: l.at[j, c])
_SCATTER_ROUTE = (lambda s, j, c: s.at[j], lambda l, me, k, c: l.at[k], lambda l, j, k, c: l.at[k])


def _gather_forward(lands, tag):
    n = len(lands)

    def body(*refs):
        ins, outs = refs[:n], refs[n:2 * n]
        send_sem, recv_sem = refs[2 * n:]
        x, y, c, chips = _place()

        def copy(i, k, half):
            cx, cy = chips[k]
            return pltpu.make_async_remote_copy(
                src_ref=ins[i].at[2 * cx + cy, half], dst_ref=outs[i].at[2 * cx + cy, half],
                send_sem=send_sem.at[3 * i + k], recv_sem=recv_sem.at[3 * i + k],
                device_id=(x, y, 1 - c), device_id_type=_MESH)

        copies = [copy(i, k, c) for i in range(n) for k in range(3)]
        for cp in copies:
            cp.start()
        for i in range(n):
            for k in range(3):
                copy(i, k, 1 - c).wait_recv()
        for cp in copies:
            cp.wait_send()

    return pl.pallas_call(
        body, name="gather_forward_to_sibling_" + tag,
        out_shape=[jax.ShapeDtypeStruct(a.shape, a.dtype) for a in lands],
        in_specs=[_ANY] * n, out_specs=[_ANY] * n,
        input_output_aliases={i: i for i in range(n)},
        scratch_shapes=[pltpu.SemaphoreType.DMA((3 * n,)), pltpu.SemaphoreType.DMA((3 * n,))],
    )(*lands)


def _swap_halves(grads, tag):
    n = len(grads)

    def body(*refs):
        ins, outs = refs[:n], refs[n:2 * n]
        send_sem, recv_sem = refs[2 * n:]
        x, y, c, _ = _place()
        copies = [pltpu.make_async_remote_copy(
            src_ref=ins[i].at[:, 1 - c], dst_ref=outs[i],
            send_sem=send_sem.at[i], recv_sem=recv_sem.at[i],
            device_id=(x, y, 1 - c), device_id_type=_MESH) for i in range(n)]
        for cp in copies:
            cp.start()
        for cp in copies:
            cp.wait()

    return pl.pallas_call(
        body, name="grad_swap_halves_" + tag,
        out_shape=[jax.ShapeDtypeStruct((N_CHIPS,) + g.shape[2:], g.dtype) for g in grads],
        in_specs=[_ANY] * n, out_specs=[_ANY] * n,
        scratch_shapes=[pltpu.SemaphoreType.DMA((n,)), pltpu.SemaphoreType.DMA((n,))],
    )(*grads)


def _pair_sum(g, r, c_idx, *, name):
    _, _, h, C = g.shape
    tr = min(h, 256)

    def body(c_ref, g_ref, r_ref, o_ref):
        o_ref[...] = (g_ref[...].astype(F32) + r_ref[...].astype(F32)).astype(o_ref.dtype)

    return pl.pallas_call(
        body, name=name, out_shape=jax.ShapeDtypeStruct((N_CHIPS, h, C), g.dtype),
        grid_spec=pltpu.PrefetchScalarGridSpec(
            num_scalar_prefetch=1, grid=(N_CHIPS, h // tr),
            in_specs=[pl.BlockSpec((None, None, tr, C), lambda j, i, s: (j, s[0], i, 0)),
                      pl.BlockSpec((None, tr, C), lambda j, i, s: (j, i, 0))],
            out_specs=pl.BlockSpec((None, tr, C), lambda j, i, s: (j, i, 0))),
        compiler_params=_params("parallel", "parallel"),
    )(c_idx, g, r)


def _owner_sum(s, r, jc_idx, *, name):
    _, h, C = s.shape
    tr = min(h, 256)

    def body(jc_ref, s_ref, r_ref, o_ref):
        acc = s_ref[...].astype(F32)
        for k in range(3):
            acc = acc + r_ref[k].astype(F32)
        o_ref[...] = acc

    return pl.pallas_call(
        body, name=name, out_shape=jax.ShapeDtypeStruct((2, h, C), F32),
        grid_spec=pltpu.PrefetchScalarGridSpec(
            num_scalar_prefetch=1, grid=(h // tr,),
            in_specs=[pl.BlockSpec((None, tr, C), lambda i, s: (s[0], i, 0)),
                      pl.BlockSpec((3, tr, C), lambda i, s: (0, i, 0))],
            out_specs=pl.BlockSpec((None, tr, C), lambda i, s: (s[1], i, 0))),
        compiler_params=_params("parallel"),
    )(jc_idx, s, r)


def _share_with_sibling(bufs, tag):
    n = len(bufs)

    def body(*refs):
        ins, outs = refs[:n], refs[n:2 * n]
        send_sem, recv_sem = refs[2 * n:]
        x, y, c, _ = _place()

        def copy(i, half):
            return pltpu.make_async_remote_copy(
                src_ref=ins[i].at[half], dst_ref=outs[i].at[half],
                send_sem=send_sem.at[i], recv_sem=recv_sem.at[i],
                device_id=(x, y, 1 - c), device_id_type=_MESH)

        copies = [copy(i, c) for i in range(n)]
        for cp in copies:
            cp.start()
        for i in range(n):
            copy(i, 1 - c).wait_recv()
        for cp in copies:
            cp.wait_send()

    return pl.pallas_call(
        body, name="grad_share_with_sibling_" + tag,
        out_shape=[jax.ShapeDtypeStruct(b.shape, b.dtype) for b in bufs],
        in_specs=[_ANY] * n, out_specs=[_ANY] * n,
        input_output_aliases={i: i for i in range(n)},
        scratch_shapes=[pltpu.SemaphoreType.DMA((n,)), pltpu.SemaphoreType.DMA((n,))],
    )(*bufs)


def _chip_sums(grads, c_idx, tag):
    views = [g.reshape(N_CHIPS, 2, g.shape[1] // 2, g.shape[2]) for g in grads]
    arrived = _swap_halves(views, tag)
    return [_pair_sum(v, r, c_idx, name=f"grad_pair_sum_{tag}_{i}") for i, (v, r) in enumerate(zip(views, arrived))]


def _owner_totals(sums, arrived, jc_idx, tag):
    halves = [_owner_sum(s, r, jc_idx, name=f"grad_owner_sum_{tag}_{i}") for i, (s, r) in enumerate(zip(sums, arrived))]
    return [f.reshape(-1, f.shape[2]) for f in _share_with_sibling(halves, tag)]


def _all_reduce_small(buf):
    R = buf.shape[0]
    n_dev = 2 * N_CHIPS

    def body(x_ref, o_ref, all_ref, send_sems, recv_sems, local_sem):
        x, y, c, chips = _place()
        me, sib = (x, y, c), (x, y, 1 - c)

        def rows(px, py, pc):
            return all_ref.at[4 * px + 2 * py + pc]

        def copy(k, block, to, src=None):
            return pltpu.make_async_remote_copy(
                src_ref=rows(*block) if src is None else src, dst_ref=rows(*block),
                send_sem=send_sems.at[k], recv_sem=recv_sems.at[k], device_id=to, device_id_type=_MESH)

        mine = pltpu.make_async_copy(x_ref, rows(*me), local_sem)
        mine.start()
        first = [copy(0, me, sib, src=x_ref)]
        first += [copy(1 + j, me, (*chip, c), src=x_ref) for j, chip in enumerate(chips)]
        for cp in first:
            cp.start()
        passed = [copy(4 + j, (*chip, c), sib) for j, chip in enumerate(chips)]
        for j, chip in enumerate(chips):
            copy(1 + j, (*chip, c), me).wait_recv()
            passed[j].start()
        copy(0, sib, me).wait_recv()
        for j, chip in enumerate(chips):
            copy(4 + j, (*chip, 1 - c), me).wait_recv()
        for cp in first + passed:
            cp.wait_send()
        mine.wait()
        acc = all_ref[0]
        for d in range(1, n_dev):
            acc = acc + all_ref[d]
        o_ref[...] = acc

    vmem = pl.BlockSpec(memory_space=pltpu.VMEM)
    return pl.pallas_call(
        body, name="all_reduce_small", out_shape=jax.ShapeDtypeStruct((R, LANES), F32),
        in_specs=[vmem], out_specs=vmem,
        scratch_shapes=[pltpu.VMEM((n_dev, R, LANES), F32),
                        pltpu.SemaphoreType.DMA((7,)), pltpu.SemaphoreType.DMA((7,)), pltpu.SemaphoreType.DMA],
        compiler_params=_params(),
    )(buf)


def _adamw(w, g, m, v, *, name):
    R, C = w.shape
    whole_fits = 7 * 2 * R * C * 4 <= VMEM_LIMIT_BYTES // 2
    tr = R if whole_fits else next(c for c in (256, 192, 128, 64, 32, 16, 8) if R % c == 0)

    def body(w_ref, g_ref, m_ref, v_ref, d_ref, nm_ref, nv_ref):
        g = g_ref[...]
        m = ADAM_B1 * m_ref[...] + (1.0 - ADAM_B1) * g
        v = ADAM_B2 * v_ref[...] + (1.0 - ADAM_B2) * (g * g)
        nm_ref[...] = m
        nv_ref[...] = v
        m_hat = m / (1.0 - ADAM_B1 ** ADAM_STEP)
        v_hat = v / (1.0 - ADAM_B2 ** ADAM_STEP)
        d_ref[...] = -ADAM_LR * (m_hat / (jnp.sqrt(v_hat) + ADAM_EPS) + ADAM_WD * w_ref[...])

    blk = pl.BlockSpec((tr, C), lambda i: (i, 0))
    sds = jax.ShapeDtypeStruct((R, C), F32)
    return pl.pallas_call(
        body, name=name, out_shape=(sds, sds, sds), grid=(R // tr,),
        in_specs=[blk] * 4, out_specs=(blk, blk, blk),
        compiler_params=_params("parallel"),
    )(w, g, m, v)


_TILE = SUBLANES * LANES


def _pack(arrays):
    rows = []
    for a in arrays:
        flat = a.reshape(-1)
        flat = jnp.pad(flat, (0, (-flat.shape[0]) % _TILE))
        rows.append(flat.reshape(-1, LANES))
    return jnp.concatenate(rows, axis=0)


def _unpack(buf, shapes):
    out, r = [], 0
    for s in shapes:
        size = math.prod(s)
        nr = -(-size // _TILE) * SUBLANES
        out.append(buf[r:r + nr].reshape(-1)[:size].reshape(s))
        r += nr
    return out


_BIG = ("w_in_a", "w_glu", "w_kv", "w_in_b", "w_mem_kv", "w_out")
_REPLICATED = ("pre_norm_g", "post_norm_g", "lam_re", "lam_im", "log_step", "b_re", "b_im", "c_re", "c_im",
               "kv_norm_g", "b_fgate", "mem_norm_g")
_SHARDED_SMALL = ("d_skip", "b_glu", "w_fgate")
_WEIGHTS = ("pre_norm_g", "post_norm_g", "w_in_a", "lam_re", "lam_im", "log_step", "b_re", "b_im", "c_re",
            "c_im", "d_skip", "w_glu", "b_glu", "kv_norm_g", "w_kv", "w_fgate", "b_fgate", "w_in_b",
            "mem_norm_g", "w_mem_kv", "w_out")


def _halves(a):
    return a.reshape(2, a.shape[0] // 2, a.shape[1])


def _unhalve(a):
    return a.reshape(N_CHIPS, 2 * a.shape[2], a.shape[3])


def _columns(a):
    return jnp.transpose(a, (1, 0, 2)).reshape(a.shape[1], N_CHIPS * a.shape[2])


def kernel(x, mem, pre_norm_g, post_norm_g, w_in_a, lam_re, lam_im, log_step, b_re, b_im, c_re, c_im, d_skip, w_glu, b_glu, kv_norm_g, w_kv, w_fgate, b_fgate, w_in_b, mem_norm_g, w_mem_kv, w_out, loss_target, m_pre_norm_g, m_post_norm_g, m_w_in_a, m_lam_re, m_lam_im, m_log_step, m_b_re, m_b_im, m_c_re, m_c_im, m_d_skip, m_w_glu, m_b_glu, m_kv_norm_g, m_w_kv, m_w_fgate, m_b_fgate, m_w_in_b, m_mem_norm_g, m_w_mem_kv, m_w_out, v_pre_norm_g, v_post_norm_g, v_w_in_a, v_lam_re, v_lam_im, v_log_step, v_b_re, v_b_im, v_c_re, v_c_im, v_d_skip, v_w_glu, v_b_glu, v_kv_norm_g, v_w_kv, v_w_fgate, v_b_fgate, v_w_in_b, v_mem_norm_g, v_w_mem_kv, v_w_out):
    a = dict(locals())
    xi, yi, ci = lax.axis_index("x"), lax.axis_index("y"), lax.axis_index("c")
    chip = 2 * xi + yi
    c_idx = jnp.reshape(ci, (1,)).astype(jnp.int32)
    jc_idx = jnp.stack([chip, ci]).astype(jnp.int32)

    vec = jnp.zeros((2 * SUBLANES, MAIN_WIDTH // N_CHIPS), F32)
    vec = vec.at[0].set(a["d_skip"][0]).at[1].set(a["b_glu"][0])
    def own_slot(gathered, parts):
        return [lax.dynamic_update_index_in_dim(g, p, chip, 0) for g, p in zip(gathered, parts)]

    parts_a = [_halves(a["w_in_a"][0].astype(BF16)), _halves(vec)]
    parts_b = [_halves(a["w_glu"][0].astype(BF16)), _halves(a["w_mem_kv"].reshape(-1, 2 * MEM_WIDTH).astype(BF16)),
               _halves(a["w_out"].reshape(-1, D_MODEL).astype(BF16))]
    parts_c = [_halves(a["w_kv"].astype(BF16)), _halves(_pad_lanes(a["w_fgate"]).astype(BF16)),
               _halves(a["w_in_b"][0].astype(BF16))]
    w_in_a, vecs = own_slot(_all_gather_chips(parts_a), parts_a)
    travelling = {}
    for tag, parts in (("b", parts_b), ("c", parts_c)):
        lands = [lax.empty((N_CHIPS,) + p.shape, p.dtype) for p in parts]
        travelling[tag], vecs = _ici_start(parts, lands, vecs, _GATHER_ROUTE, name=f"gather_{tag}_start")

    def fetch(tag, after):
        parts, lands = _ici_wait(travelling[tag], after, _GATHER_ROUTE, name=f"gather_{tag}_wait")
        full = own_slot(_gather_forward(lands, tag), parts)
        if tag == "b":
            w_glu, w_mk, w_out = full
            return dict(w_glu=w_glu.reshape(MAIN_WIDTH, MAIN_WIDTH),
                        w_mem_kv=jnp.transpose(w_mk, (1, 0, 2, 3)).reshape(2, D_MODEL, 2 * MEM_WIDTH),
                        w_out=jnp.transpose(w_out, (1, 0, 2, 3)).reshape(2, D_MODEL, D_MODEL))
        w_kv, w_fg, w_in_b = full
        return dict(w_kv=_columns(_unhalve(w_kv)), w_fgate=w_fg.reshape(D_MODEL, LANES),
                    w_in_b=_columns(_unhalve(w_in_b)))

    w = dict(
        w_in_a=_columns(_unhalve(w_in_a)),
        d_skip=vecs[:, 0, 0, :].reshape(MAIN_WIDTH), b_glu=vecs[:, 0, 1, :].reshape(MAIN_WIDTH),
        pre_norm_g=a["pre_norm_g"], post_norm_g=a["post_norm_g"], mem_norm_g=a["mem_norm_g"],
        kv_norm_g=a["kv_norm_g"], b_fgate=a["b_fgate"],
        lam_re=a["lam_re"][0], lam_im=a["lam_im"][0], log_step=a["log_step"][0],
        b_re=a["b_re"][0], b_im=a["b_im"][0], c_re=a["c_re"][0], c_im=a["c_im"][0])

    sent = {}

    def grads_ready(tag, g, token):
        big = {"b": lambda: [g["w_kv"], g["w_in_b"], g["w_mem_kv_1"].reshape(N_CHIPS, -1, 2 * MEM_WIDTH),
                             g["w_out_1"].reshape(N_CHIPS, -1, D_MODEL)],
               "a1": lambda: [g["w_glu"].reshape(N_CHIPS, -1, MAIN_WIDTH),
                              g["w_mem_kv_0"].reshape(N_CHIPS, -1, 2 * MEM_WIDTH),
                              g["w_out_0"].reshape(N_CHIPS, -1, D_MODEL)],
               "a2": lambda: [g["w_in_a"]]}[tag]()
        sums = _chip_sums(big, c_idx, tag)
        lands = [lax.empty((3,) + s.shape[1:], s.dtype) for s in sums]
        sent[tag], token = _ici_start(sums, lands, token, _SCATTER_ROUTE, name=f"grad_send_{tag}_start")
        return token

    loss_row, grad_x, g = _local_step(a["x"][0], a["mem"][0], a["loss_target"][0], w, fetch, grads_ready)
    loss = lax.psum(jnp.sum(loss_row), MESH_AXES)

    def totals(tag, after):
        sums, arrived = _ici_wait(sent[tag], after, _SCATTER_ROUTE, name=f"grad_send_{tag}_wait")
        return _owner_totals(sums, arrived, jc_idx, tag)

    r_kv, r_in_b, r_mk1, r_out1 = totals("b", grad_x)
    r_glu, r_mk0, r_out0 = totals("a1", r_out1)
    (r_in_a,) = totals("a2", r_out0)
    grads = {"w_in_a": r_in_a[None], "w_glu": r_glu[None], "w_kv": r_kv, "w_in_b": r_in_b[None],
             "w_mem_kv": jnp.stack([r_mk0, r_mk1]), "w_out": jnp.stack([r_out0, r_out1])}

    small_names = _REPLICATED + _SHARDED_SMALL
    small = _all_reduce_small(_pack([g[n] for n in small_names]))
    small = dict(zip(small_names, _unpack(small, [g[n].shape for n in small_names])))
    for n in _REPLICATED:
        grads[n] = small[n].reshape(a[n].shape)
    nd = MAIN_WIDTH // N_CHIPS
    grads["d_skip"] = lax.dynamic_slice(small["d_skip"], (chip * nd,), (nd,))[None]
    grads["b_glu"] = lax.dynamic_slice(small["b_glu"], (chip * nd,), (nd,))[None]
    nf = D_MODEL // N_CHIPS
    grads["w_fgate"] = lax.dynamic_slice(small["w_fgate"], (chip * nf, 0), (nf, FOX_HEADS))

    delta, new_m, new_v = {}, {}, {}
    for n in _BIG:
        shape = a[n].shape
        d2 = (-1, shape[-1])
        d, m, v = _adamw(a[n].reshape(d2), grads[n].reshape(d2), a["m_" + n].reshape(d2),
                         a["v_" + n].reshape(d2), name="adamw_" + n)
        delta[n], new_m[n], new_v[n] = d.reshape(shape), m.reshape(shape), v.reshape(shape)
    shapes = [a[n].shape for n in small_names]
    d, m, v = _adamw(_pack([a[n] for n in small_names]), _pack([grads[n] for n in small_names]),
                     _pack([a["m_" + n] for n in small_names]), _pack([a["v_" + n] for n in small_names]),
                     name="adamw_small")
    for n, dd, mm, vv in zip(small_names, _unpack(d, shapes), _unpack(m, shapes), _unpack(v, shapes)):
        delta[n], new_m[n], new_v[n] = dd, mm, vv

    return (loss, grad_x[None], *[grads[n] for n in _WEIGHTS], *[delta[n] for n in _WEIGHTS],
            *[new_m[n] for n in _WEIGHTS], *[new_v[n] for n in _WEIGHTS])
```

```python
import functools
import math

import jax
import jax.numpy as jnp
from jax import lax
from jax.experimental import pallas as pl
from jax.experimental.pallas import tpu as pltpu

F32 = jnp.float32
BF16 = jnp.bfloat16

D_MODEL = 2048
N_MEM = 256
MAIN_WIDTH = 1536
MEM_WIDTH = 512
IN_WIDTH = 2 * MAIN_WIDTH + 2 * MEM_WIDTH
HEAD_DIM = 128
FOX_HEADS = MAIN_WIDTH // HEAD_DIM
MEM_HEADS = MEM_WIDTH // HEAD_DIM
SSM_GROUP = 16
SSM_GROUPS = MAIN_WIDTH // SSM_GROUP
SSM_STATE = 64
GROUPS_PER_BLOCK = 8
SSM_BLOCKS = SSM_GROUPS // GROUPS_PER_BLOCK
STATE_COLS = GROUPS_PER_BLOCK * SSM_STATE
EPS = 1e-6
ADAM_LR = 0.001
ADAM_B1 = 0.9
ADAM_B2 = 0.999
ADAM_EPS = 1e-08
ADAM_WD = 0.01
ADAM_STEP = 10
N_CHIPS = 4
LANES = 128
SUBLANES = 8
VMEM_LIMIT_BYTES = 56 * 1024 * 1024
NEG_BIG = -1e30
MESH_AXES = ("x", "y", "c")


def _params(*sem):
    return pltpu.CompilerParams(dimension_semantics=sem if sem else None,
                                vmem_limit_bytes=VMEM_LIMIT_BYTES)


def _sigmoid(x):
    return 1.0 / (1.0 + jnp.exp(-x))


def _gelu(x):
    c = math.sqrt(2.0 / math.pi)
    return 0.5 * x * (1.0 + jnp.tanh(c * (x + 0.044715 * (x * x * x))))


def _gelu_grad(x):
    c = math.sqrt(2.0 / math.pi)
    t = jnp.tanh(c * (x + 0.044715 * (x * x * x)))
    return 0.5 * (1.0 + t) + 0.5 * x * (1.0 - t * t) * (c * (1.0 + 3.0 * 0.044715 * (x * x)))


def _silu_and_grad(z):
    s = _sigmoid(z)
    return z * s, s * (1.0 + z * (1.0 - s))


_TILE_CHOICES = (2048, 1024, 768, 512, 384, 256, LANES)


def _tile(n, cap):
    return next(c for c in _TILE_CHOICES if c <= cap and n % c == 0)


def _mm(a, b, *, name, ta=False, tb=False, out_dtype=F32, shards=1, tm=1024, tn=1024, tk=2048):
    if ta:
        K, M = a.shape
    else:
        M, K = a.shape
    if tb:
        N, kb = b.shape
    else:
        kb, N = b.shape
    assert K == kb, (a.shape, b.shape)
    ns = N // shards
    tm, tn, tk = _tile(M, tm), _tile(ns, tn), _tile(K, tk)
    assert M % tm == 0 and ns % tn == 0 and K % tk == 0 and N % shards == 0
    nk = K // tk
    dn = (((0 if ta else 1,), (1 if tb else 0,)), ((), ()))

    def body(a_ref, b_ref, o_ref, acc_ref):
        k = pl.program_id(2)

        @pl.when(k == 0)
        def _():
            acc_ref[...] = jnp.zeros_like(acc_ref)

        acc_ref[...] += lax.dot_general(a_ref[...].astype(BF16), b_ref[...].astype(BF16), dn,
                                        preferred_element_type=F32)

        @pl.when(k == nk - 1)
        def _():
            o_ref[...] = acc_ref[...].astype(o_ref.dtype)

    a_spec = (pl.BlockSpec((tk, tm), lambda i, j, k: (k, i)) if ta
              else pl.BlockSpec((tm, tk), lambda i, j, k: (i, k)))
    b_spec = (pl.BlockSpec((tn, tk), lambda i, j, k: (j, k)) if tb
              else pl.BlockSpec((tk, tn), lambda i, j, k: (k, j)))
    if shards == 1:
        out_shape = jax.ShapeDtypeStruct((M, N), out_dtype)
        o_spec = pl.BlockSpec((tm, tn), lambda i, j, k: (i, j))
    else:
        nb = ns // tn
        out_shape = jax.ShapeDtypeStruct((shards, M, ns), out_dtype)
        o_spec = pl.BlockSpec((None, tm, tn), lambda i, j, k: (j // nb, i, j % nb))
    return pl.pallas_call(
        body, name=name, out_shape=out_shape,
        grid=(M // tm, N // tn, nk),
        in_specs=[a_spec, b_spec], out_specs=o_spec,
        scratch_shapes=[pltpu.VMEM((tm, tn), F32)],
        compiler_params=_params("parallel", "parallel", "arbitrary"),
    )(a, b)


def _rmsnorm_fwd(x, g, *, name, res=None, out_dtype=F32, tr=256):
    L, D = x.shape
    tr = min(tr, L)
    has_res = res is not None

    def body(*refs):
        if has_res:
            x_ref, g_ref, r_ref, o_ref = refs
        else:
            x_ref, g_ref, o_ref = refs
        xf = x_ref[...]
        r = lax.rsqrt(jnp.mean(xf * xf, axis=-1, keepdims=True) + EPS)
        y = xf * r * g_ref[...]
        if has_res:
            y = r_ref[...] + y
        o_ref[...] = y.astype(o_ref.dtype)

    row = pl.BlockSpec((tr, D), lambda i: (i, 0))
    vec = pl.BlockSpec((1, D), lambda i: (0, 0))
    ins = [x, g.reshape(1, D)] + ([res] if has_res else [])
    return pl.pallas_call(
        body, name=name, out_shape=jax.ShapeDtypeStruct((L, D), out_dtype),
        grid=(L // tr,), in_specs=[row, vec] + ([row] if has_res else []), out_specs=row,
        compiler_params=_params("parallel"),
    )(*ins)


def _rmsnorm_bwd(x, g, dy, *, name, adds=(), dx_dtype=F32, tr=256):
    L, D = x.shape
    tr = min(tr, L)
    dys = dy if isinstance(dy, tuple) else (dy,)
    n_dy, n_add = len(dys), len(adds)

    def body(*refs):
        x_ref, g_ref = refs[:2]
        dy_refs = refs[2:2 + n_dy]
        add_refs = refs[2 + n_dy:2 + n_dy + n_add]
        dx_ref, dg_ref = refs[2 + n_dy + n_add:]
        xf = x_ref[...]
        dyf = dy_refs[0][...].astype(F32)
        for d_ref in dy_refs[1:]:
            dyf = dyf + d_ref[...].astype(F32)
        r = lax.rsqrt(jnp.mean(xf * xf, axis=-1, keepdims=True) + EPS)
        gy = dyf * g_ref[...]
        c = jnp.mean(xf * gy, axis=-1, keepdims=True) * (r * r * r)
        dx = gy * r - xf * c
        for a_ref in add_refs:
            dx = dx + a_ref[...].astype(F32)
        dx_ref[...] = dx.astype(dx_ref.dtype)

        @pl.when(pl.program_id(0) == 0)
        def _():
            dg_ref[...] = jnp.zeros_like(dg_ref)

        dg_ref[...] += jnp.sum(dyf * xf * r, axis=0, keepdims=True)

    row = pl.BlockSpec((tr, D), lambda i: (i, 0))
    vec = pl.BlockSpec((1, D), lambda i: (0, 0))
    dx, dg = pl.pallas_call(
        body, name=name,
        out_shape=(jax.ShapeDtypeStruct((L, D), dx_dtype), jax.ShapeDtypeStruct((1, D), F32)),
        grid=(L // tr,), in_specs=[row, vec] + [row] * (n_dy + n_add), out_specs=(row, vec),
        compiler_params=_params("arbitrary"),
    )(x, g.reshape(1, D), *dys, *adds)
    return dx, dg.reshape(D)


def _final_norm_loss(o, g, res, target, *, tr=256):
    L, D = o.shape
    tr = min(tr, L)

    def body(o_ref, g_ref, r_ref, t_ref, dh_ref, loss_ref):
        xf = o_ref[...]
        r = lax.rsqrt(jnp.mean(xf * xf, axis=-1, keepdims=True) + EPS)
        e = (r_ref[...] + xf * r * g_ref[...]) - t_ref[...]
        dh_ref[...] = e * (1.0 / D)

        @pl.when(pl.program_id(0) == 0)
        def _():
            loss_ref[...] = jnp.zeros_like(loss_ref)

        loss_ref[...] += jnp.sum(e * e, axis=0, keepdims=True) * (0.5 / D)

    row = pl.BlockSpec((tr, D), lambda i: (i, 0))
    vec = pl.BlockSpec((1, D), lambda i: (0, 0))
    dh, lp = pl.pallas_call(
        body, name="post_norm_1_loss",
        out_shape=(jax.ShapeDtypeStruct((L, D), F32), jax.ShapeDtypeStruct((1, D), F32)),
        grid=(L // tr,), in_specs=[row, vec, row, row], out_specs=(row, vec),
        compiler_params=_params("arbitrary"),
    )(o, g.reshape(1, D), res, target)
    return dh, lp


def _s5_coeffs(lr, li, ls):
    dt = jnp.exp(ls)
    mag = jnp.exp(lr * dt)
    ar = mag * jnp.cos(li * dt)
    ai = mag * jnp.sin(li * dt)
    den = lr * lr + li * li
    cr = ((ar - 1.0) * lr + ai * li) / den
    ci = (ai * lr - (ar - 1.0) * li) / den
    return dt, ar, ai, den, cr, ci


def _s5_prep(lam_re, lam_im, log_step, b_re_t, b_im_t):
    G, P = lam_re.shape
    H = b_re_t.shape[1]

    def body(lr_ref, li_ref, ls_ref, br_ref, bi_ref, ar_ref, ai_ref, bbr_ref, bbi_ref):
        _, ar, ai, _, cr, ci = _s5_coeffs(lr_ref[...], li_ref[...], ls_ref[...])
        ar_ref[...] = ar
        ai_ref[...] = ai
        br, bi = br_ref[...], bi_ref[...]
        crb, cib = cr[:, None, :], ci[:, None, :]
        bbr_ref[...] = crb * br - cib * bi
        bbi_ref[...] = crb * bi + cib * br

    return pl.pallas_call(
        body, name="s5_prep",
        out_shape=(jax.ShapeDtypeStruct((G, P), F32), jax.ShapeDtypeStruct((G, P), F32),
                   jax.ShapeDtypeStruct((G, H, P), F32), jax.ShapeDtypeStruct((G, H, P), F32)),
        compiler_params=_params(),
    )(lam_re, lam_im, log_step.reshape(G, 1), b_re_t, b_im_t)


def _s5_prep_bwd(lam_re, lam_im, log_step, b_re_t, b_im_t, d_ar, d_ai, d_bbr, d_bbi):
    G, P = lam_re.shape
    H = b_re_t.shape[1]

    def body(lr_ref, li_ref, ls_ref, br_ref, bi_ref, dar_ref, dai_ref, dbbr_ref, dbbi_ref,
             dlr_ref, dli_ref, dls_ref, dbr_ref, dbi_ref):
        lr, li = lr_ref[...], li_ref[...]
        dt, ar, ai, den, cr, ci = _s5_coeffs(lr, li, ls_ref[...])
        br, bi = br_ref[...], bi_ref[...]
        gbr, gbi = dbbr_ref[...], dbbi_ref[...]
        crb, cib = cr[:, None, :], ci[:, None, :]
        dbr_ref[...] = crb * gbr + cib * gbi
        dbi_ref[...] = crb * gbi - cib * gbr
        gcr = jnp.sum(br * gbr + bi * gbi, axis=1)
        gci = jnp.sum(br * gbi - bi * gbr, axis=1)
        ilr, ili = lr / den, -li / den
        gar = dar_ref[...] + (ilr * gcr + ili * gci)
        gai = dai_ref[...] + (ilr * gci - ili * gcr)
        qr, qi = cr * ilr - ci * ili, cr * ili + ci * ilr
        glr = -(qr * gcr + qi * gci)
        gli = -(qr * gci - qi * gcr)
        glr = glr + dt * (ar * gar + ai * gai)
        gli = gli + dt * (ar * gai - ai * gar)
        wr, wi = lr * ar - li * ai, lr * ai + li * ar
        gdt = jnp.sum(wr * gar + wi * gai, axis=1, keepdims=True)
        dlr_ref[...] = glr
        dli_ref[...] = gli
        dls_ref[...] = gdt * dt

    return pl.pallas_call(
        body, name="s5_prep_bwd",
        out_shape=(jax.ShapeDtypeStruct((G, P), F32), jax.ShapeDtypeStruct((G, P), F32),
                   jax.ShapeDtypeStruct((G, 1), F32),
                   jax.ShapeDtypeStruct((G, H, P), F32), jax.ShapeDtypeStruct((G, H, P), F32)),
        compiler_params=_params(),
    )(lam_re, lam_im, log_step.reshape(G, 1), b_re_t, b_im_t, d_ar, d_ai, d_bbr, d_bbi)


def _s5_block_mats(bbr_t, bbi_t, c_re, c_im):
    bmat = _s5_expand(bbr_t, bbi_t)
    cmat = jnp.transpose(_s5_expand(c_re, -c_im), (0, 2, 1))
    return bmat.astype(BF16), cmat.astype(BF16)


def _s5_diag_mask():
    r = lax.broadcasted_iota(jnp.int32, (LANES, 2 * STATE_COLS), 0) // SSM_GROUP
    c = (lax.broadcasted_iota(jnp.int32, (LANES, 2 * STATE_COLS), 1) % STATE_COLS) // SSM_STATE
    return (r == c).astype(F32)


def _s5_expand(re, im):
    re = jnp.tile(re.reshape(SSM_BLOCKS, LANES, SSM_STATE), (1, 1, GROUPS_PER_BLOCK))
    im = jnp.tile(im.reshape(SSM_BLOCKS, LANES, SSM_STATE), (1, 1, GROUPS_PER_BLOCK))
    return jnp.concatenate([re, im], axis=-1) * _s5_diag_mask()[None]


def _s5_block_diag(dmat):
    d = dmat * _s5_diag_mask()[None]
    parts = []
    for ri in range(2):
        acc = 0.0
        for g in range(GROUPS_PER_BLOCK):
            c0 = ri * STATE_COLS + g * SSM_STATE
            acc = acc + d[:, :, c0:c0 + SSM_STATE]
        parts.append(acc.reshape(SSM_GROUPS, SSM_GROUP, SSM_STATE))
    return jnp.stack(parts)


def _s5_a_rows(ar, ai):
    a = jnp.concatenate([ar.reshape(SSM_BLOCKS, STATE_COLS), ai.reshape(SSM_BLOCKS, STATE_COLS)], axis=1)
    return jnp.broadcast_to(a[:, None, :], (SSM_BLOCKS, SUBLANES, 2 * STATE_COLS))


def _s5_fwd(proj, bmat, cmat, a_rows, d_skip, *, tc=512):
    L = proj.shape[0]
    tc = min(tc, L)
    nt = L // tc
    n8 = tc // SUBLANES
    S = STATE_COLS

    def body(u_ref, b_ref, c_ref, a_ref, d_ref, y_ref, yg_ref, xp_ref, bu_s, xp_s, carry_s):
        @pl.when(pl.program_id(1) == 0)
        def _():
            carry_s[...] = jnp.zeros_like(carry_s)

        u = u_ref[...]
        bu = jnp.dot(u.astype(BF16), b_ref[...], preferred_element_type=F32)
        bu_s[...] = bu.reshape(n8, SUBLANES, 2 * S)
        ar, ai = a_ref[0:1, :S], a_ref[0:1, S:]

        def step(i, carry):
            cr, ci = carry
            for j in range(SUBLANES):
                xp_s[i, j:j + 1, :S] = cr
                xp_s[i, j:j + 1, S:] = ci
                br = bu_s[i, j:j + 1, :S]
                bi = bu_s[i, j:j + 1, S:]
                cr, ci = ar * cr - ai * ci + br, ar * ci + ai * cr + bi
            return cr, ci

        cr, ci = lax.fori_loop(0, n8, step, (carry_s[0:1, :S], carry_s[0:1, S:]))
        carry_s[0:1, :S] = cr
        carry_s[0:1, S:] = ci
        xp = xp_s[...].reshape(tc, 2 * S)
        xp_ref[...] = xp
        x_re = ar * xp[:, :S] - ai * xp[:, S:] + bu[:, :S]
        x_im = ar * xp[:, S:] + ai * xp[:, :S] + bu[:, S:]
        xs = jnp.concatenate([x_re, x_im], axis=1).astype(BF16)
        y = jnp.dot(xs, c_ref[...], preferred_element_type=F32) + d_ref[...] * u
        y_ref[...] = y
        yg_ref[...] = _gelu(y).astype(BF16)

    return pl.pallas_call(
        body, name="s5_fwd",
        out_shape=(jax.ShapeDtypeStruct((L, MAIN_WIDTH), F32),
                   jax.ShapeDtypeStruct((L, MAIN_WIDTH), BF16),
                   jax.ShapeDtypeStruct((L, SSM_BLOCKS * 2 * S), F32)),
        grid=(SSM_BLOCKS, nt),
        in_specs=[pl.BlockSpec((tc, LANES), lambda b, t: (t, b)),
                  pl.BlockSpec((None, LANES, 2 * S), lambda b, t: (b, 0, 0)),
                  pl.BlockSpec((None, 2 * S, LANES), lambda b, t: (b, 0, 0)),
                  pl.BlockSpec((None, SUBLANES, 2 * S), lambda b, t: (b, 0, 0)),
                  pl.BlockSpec((1, LANES), lambda b, t: (0, b))],
        out_specs=(pl.BlockSpec((tc, LANES), lambda b, t: (t, b)),
                   pl.BlockSpec((tc, LANES), lambda b, t: (t, b)),
                   pl.BlockSpec((tc, 2 * S), lambda b, t: (t, b))),
        scratch_shapes=[pltpu.VMEM((n8, SUBLANES, 2 * S), F32),
                        pltpu.VMEM((n8, SUBLANES, 2 * S), F32),
                        pltpu.VMEM((SUBLANES, 2 * S), F32)],
        compiler_params=_params("parallel", "arbitrary"),
    )(proj, bmat, cmat, a_rows, d_skip.reshape(1, MAIN_WIDTH))


def _s5_bwd(proj, dyg_a, dyg_b, y, xp, bmat, cmat, a_rows, d_skip, *, tc=512):
    L = proj.shape[0]
    tc = min(tc, L)
    nt = L // tc
    n8 = tc // SUBLANES
    S = STATE_COLS
    nn = (((1,), (1,)), ((), ()))
    tn = (((0,), (0,)), ((), ()))

    def body(u_ref, dyga_ref, dygb_ref, y_ref, xp_ref, b_ref, c_ref, a_ref, d_ref,
             du_ref, db_ref, dc_ref, da_ref, dd_ref, dl_s, carry_s):
        @pl.when(pl.program_id(1) == 0)
        def _():
            carry_s[...] = jnp.zeros_like(carry_s)
            db_ref[...] = jnp.zeros_like(db_ref)
            dc_ref[...] = jnp.zeros_like(dc_ref)
            da_ref[...] = jnp.zeros_like(da_ref)
            dd_ref[...] = jnp.zeros_like(dd_ref)

        u = u_ref[...]
        dy = (dyga_ref[...] + dygb_ref[...]) * _gelu_grad(y_ref[...])
        xp = xp_ref[...]
        ub = u.astype(BF16)
        dyb = dy.astype(BF16)
        ar, ai = a_ref[0:1, :S], a_ref[0:1, S:]
        bu = jnp.dot(ub, b_ref[...], preferred_element_type=F32)
        x_re = ar * xp[:, :S] - ai * xp[:, S:] + bu[:, :S]
        x_im = ar * xp[:, S:] + ai * xp[:, :S] + bu[:, S:]
        xs = jnp.concatenate([x_re, x_im], axis=1).astype(BF16)
        dc_ref[...] += lax.dot_general(dyb, xs, tn, preferred_element_type=F32)
        dx = lax.dot_general(dyb, c_ref[...], nn, preferred_element_type=F32)
        dl_s[...] = dx.reshape(n8, SUBLANES, 2 * S)

        def step(k, carry):
            cr, ci = carry
            i = n8 - 1 - k
            for j in range(SUBLANES - 1, -1, -1):
                lr = dl_s[i, j:j + 1, :S] + (ar * cr + ai * ci)
                li = dl_s[i, j:j + 1, S:] + (ar * ci - ai * cr)
                dl_s[i, j:j + 1, :S] = lr
                dl_s[i, j:j + 1, S:] = li
                cr, ci = lr, li
            return cr, ci

        cr, ci = lax.fori_loop(0, n8, step, (carry_s[0:1, :S], carry_s[0:1, S:]))
        carry_s[0:1, :S] = cr
        carry_s[0:1, S:] = ci
        lam = dl_s[...].reshape(tc, 2 * S)
        l_re, l_im = lam[:, :S], lam[:, S:]
        da_ref[0:1, :S] += jnp.sum(l_re * xp[:, :S] + l_im * xp[:, S:], axis=0, keepdims=True)
        da_ref[0:1, S:] += jnp.sum(l_im * xp[:, :S] - l_re * xp[:, S:], axis=0, keepdims=True)
        lamb = lam.astype(BF16)
        du_ref[...] = lax.dot_general(lamb, b_ref[...], nn, preferred_element_type=F32) + d_ref[...] * dy
        db_ref[...] += lax.dot_general(ub, lamb, tn, preferred_element_type=F32)
        dd_ref[0:1, :] += jnp.sum(dy * u, axis=0, keepdims=True)

    rev = lambda b, t: (nt - 1 - t, b)
    return pl.pallas_call(
        body, name="s5_bwd",
        out_shape=(jax.ShapeDtypeStruct((L, MAIN_WIDTH), F32),
                   jax.ShapeDtypeStruct((SSM_BLOCKS, LANES, 2 * S), F32),
                   jax.ShapeDtypeStruct((SSM_BLOCKS, LANES, 2 * S), F32),
                   jax.ShapeDtypeStruct((SSM_BLOCKS, SUBLANES, 2 * S), F32),
                   jax.ShapeDtypeStruct((SUBLANES, MAIN_WIDTH), F32)),
        grid=(SSM_BLOCKS, nt),
        in_specs=[pl.BlockSpec((tc, LANES), rev),
                  pl.BlockSpec((tc, LANES), rev),
                  pl.BlockSpec((tc, LANES), rev),
                  pl.BlockSpec((tc, LANES), rev),
                  pl.BlockSpec((tc, 2 * S), rev),
                  pl.BlockSpec((None, LANES, 2 * S), lambda b, t: (b, 0, 0)),
                  pl.BlockSpec((None, 2 * S, LANES), lambda b, t: (b, 0, 0)),
                  pl.BlockSpec((None, SUBLANES, 2 * S), lambda b, t: (b, 0, 0)),
                  pl.BlockSpec((1, LANES), lambda b, t: (0, b))],
        out_specs=(pl.BlockSpec((tc, LANES), rev),
                   pl.BlockSpec((None, LANES, 2 * S), lambda b, t: (b, 0, 0)),
                   pl.BlockSpec((None, LANES, 2 * S), lambda b, t: (b, 0, 0)),
                   pl.BlockSpec((None, SUBLANES, 2 * S), lambda b, t: (b, 0, 0)),
                   pl.BlockSpec((SUBLANES, LANES), lambda b, t: (0, b))),
        scratch_shapes=[pltpu.VMEM((n8, SUBLANES, 2 * S), F32),
                        pltpu.VMEM((SUBLANES, 2 * S), F32)],
        compiler_params=_params("parallel", "arbitrary"),
    )(proj, dyg_a, dyg_b, y, xp, bmat, cmat, a_rows, d_skip.reshape(1, MAIN_WIDTH))


def _row_specs(tr):
    main = pl.BlockSpec((tr, MAIN_WIDTH), lambda i: (i, 0))
    z = pl.BlockSpec((tr, MAIN_WIDTH), lambda i: (i, 1))
    zm = pl.BlockSpec((tr, MEM_WIDTH), lambda i: (i, IN_WIDTH // MEM_WIDTH - 1))
    mem = pl.BlockSpec((tr, MEM_WIDTH), lambda i: (i, 0))
    cat = pl.BlockSpec((tr, D_MODEL), lambda i: (i, 0))
    vec = pl.BlockSpec((1, MAIN_WIDTH), lambda i: (0, 0))
    return main, z, zm, mem, cat, vec


def _gate_a_fwd(y, t, b_glu, proj, o_mem, *, tr=256):
    L = y.shape[0]
    tr = min(tr, L)

    def body(y_ref, t_ref, b_ref, z_ref, zm_ref, om_ref, o_ref):
        yg = _gelu(y_ref[...])
        sz, _ = _silu_and_grad(z_ref[...])
        o_ref[:, :MAIN_WIDTH] = (yg * _sigmoid(t_ref[...] + b_ref[...]) * sz).astype(BF16)
        szm, _ = _silu_and_grad(zm_ref[...])
        o_ref[:, MAIN_WIDTH:] = (om_ref[...] * szm).astype(BF16)

    main, z, zm, mem, cat, vec = _row_specs(tr)
    return pl.pallas_call(
        body, name="gate_a_fwd", out_shape=jax.ShapeDtypeStruct((L, D_MODEL), BF16),
        grid=(L // tr,), in_specs=[main, main, vec, z, zm, mem], out_specs=cat,
        compiler_params=_params("parallel"),
    )(y, t, b_glu.reshape(1, MAIN_WIDTH), proj, proj, o_mem)


def _gate_a_bwd(dcat, y, t, b_glu, proj, o_mem, *, tr=256):
    L = y.shape[0]
    tr = min(tr, L)

    def body(dc_ref, y_ref, t_ref, b_ref, z_ref, zm_ref, om_ref,
             dz_ref, dzm_ref, dt_ref, dyg_ref, dom_ref, db_ref):
        dmain = dc_ref[:, :MAIN_WIDTH]
        dmemo = dc_ref[:, MAIN_WIDTH:]
        yg = _gelu(y_ref[...])
        sg = _sigmoid(t_ref[...] + b_ref[...])
        sz, gz = _silu_and_grad(z_ref[...])
        dz_ref[...] = (dmain * (yg * sg) * gz).astype(BF16)
        dy2 = dmain * sz
        dyg_ref[...] = dy2 * sg
        dt = dy2 * yg * (sg * (1.0 - sg))
        dt_ref[...] = dt.astype(BF16)

        @pl.when(pl.program_id(0) == 0)
        def _():
            db_ref[...] = jnp.zeros_like(db_ref)

        db_ref[...] += jnp.sum(dt, axis=0, keepdims=True)
        szm, gzm = _silu_and_grad(zm_ref[...])
        dom_ref[...] = dmemo * szm
        dzm_ref[...] = (dmemo * om_ref[...] * gzm).astype(BF16)

    main, z, zm, mem, cat, vec = _row_specs(tr)
    outs = pl.pallas_call(
        body, name="gate_a_bwd",
        out_shape=(jax.ShapeDtypeStruct((L, MAIN_WIDTH), BF16), jax.ShapeDtypeStruct((L, MEM_WIDTH), BF16),
                   jax.ShapeDtypeStruct((L, MAIN_WIDTH), BF16), jax.ShapeDtypeStruct((L, MAIN_WIDTH), F32),
                   jax.ShapeDtypeStruct((L, MEM_WIDTH), F32), jax.ShapeDtypeStruct((1, MAIN_WIDTH), F32)),
        grid=(L // tr,), in_specs=[cat, main, main, vec, z, zm, mem],
        out_specs=(main, mem, main, main, mem, vec),
        compiler_params=_params("arbitrary"),
    )(dcat, y, t, b_glu.reshape(1, MAIN_WIDTH), proj, proj, o_mem)
    return outs


def _gate_b_fwd(att, proj, o_mem, *, tr=256):
    L = att.shape[0]
    tr = min(tr, L)

    def body(a_ref, z_ref, zm_ref, om_ref, o_ref):
        sz, _ = _silu_and_grad(z_ref[...])
        o_ref[:, :MAIN_WIDTH] = (a_ref[...] * sz).astype(BF16)
        szm, _ = _silu_and_grad(zm_ref[...])
        o_ref[:, MAIN_WIDTH:] = (om_ref[...] * szm).astype(BF16)

    main, z, zm, mem, cat, _ = _row_specs(tr)
    return pl.pallas_call(
        body, name="gate_b_fwd", out_shape=jax.ShapeDtypeStruct((L, D_MODEL), BF16),
        grid=(L // tr,), in_specs=[main, z, zm, mem], out_specs=cat,
        compiler_params=_params("parallel"),
    )(att, proj, proj, o_mem)


def _gate_b_bwd(dcat, att, proj, o_mem, *, tr=256):
    L = att.shape[0]
    tr = min(tr, L)

    def body(dc_ref, a_ref, z_ref, zm_ref, om_ref, da_ref, dz_ref, dom_ref, dzm_ref, dl_ref):
        dmain = dc_ref[:, :MAIN_WIDTH]
        dmemo = dc_ref[:, MAIN_WIDTH:]
        att = a_ref[...]
        sz, gz = _silu_and_grad(z_ref[...])
        datt = dmain * sz
        da_ref[...] = datt
        dz_ref[...] = (dmain * att * gz).astype(BF16)
        szm, gzm = _silu_and_grad(zm_ref[...])
        dom_ref[...] = dmemo * szm
        dzm_ref[...] = (dmemo * om_ref[...] * gzm).astype(BF16)
        prod = datt * att
        for h in range(FOX_HEADS):
            dl_ref[h] = jnp.sum(prod[:, h * HEAD_DIM:(h + 1) * HEAD_DIM], axis=1, keepdims=True)

    main, z, zm, mem, cat, _ = _row_specs(tr)
    delta = pl.BlockSpec((FOX_HEADS, tr, 1), lambda i: (0, i, 0))
    return pl.pallas_call(
        body, name="gate_b_bwd",
        out_shape=(jax.ShapeDtypeStruct((L, MAIN_WIDTH), F32), jax.ShapeDtypeStruct((L, MAIN_WIDTH), BF16),
                   jax.ShapeDtypeStruct((L, MEM_WIDTH), F32), jax.ShapeDtypeStruct((L, MEM_WIDTH), BF16),
                   jax.ShapeDtypeStruct((FOX_HEADS, L, 1), F32)),
        grid=(L // tr,), in_specs=[cat, main, z, zm, mem], out_specs=(main, main, mem, mem, delta),
        compiler_params=_params("parallel"),
    )(dcat, att, proj, proj, o_mem)


_MEM_Q_COL = (2 * MAIN_WIDTH) // HEAD_DIM
_NT = (((1,), (1,)), ((), ()))
_TN = (((0,), (0,)), ((), ()))


def _mem_probs(q_ref, k_ref):
    qs = (q_ref[...] * (HEAD_DIM ** -0.5)).astype(BF16)
    s = lax.dot_general(qs, k_ref[...].astype(BF16), _NT, preferred_element_type=F32)
    e = jnp.exp(s - jnp.max(s, axis=-1, keepdims=True))
    return qs, e / jnp.sum(e, axis=-1, keepdims=True)


def _mem_attn_fwd(proj, kvm, *, tq=2048):
    L = proj.shape[0]
    tq = min(tq, L)

    def body(q_ref, k_ref, v_ref, o_ref):
        _, p = _mem_probs(q_ref, k_ref)
        o_ref[...] = jnp.dot(p.astype(BF16), v_ref[...].astype(BF16), preferred_element_type=F32)

    return pl.pallas_call(
        body, name="mem_attn_fwd", out_shape=jax.ShapeDtypeStruct((L, MEM_WIDTH), F32),
        grid=(MEM_HEADS, L // tq),
        in_specs=[pl.BlockSpec((tq, HEAD_DIM), lambda h, i: (i, _MEM_Q_COL + h)),
                  pl.BlockSpec((N_MEM, HEAD_DIM), lambda h, i: (0, h)),
                  pl.BlockSpec((N_MEM, HEAD_DIM), lambda h, i: (0, MEM_HEADS + h))],
        out_specs=pl.BlockSpec((tq, HEAD_DIM), lambda h, i: (i, h)),
        compiler_params=_params("parallel", "parallel"),
    )(proj, kvm, kvm)


def _mem_attn_bwd(proj, kvm, do, *, tq=2048):
    L = proj.shape[0]
    tq = min(tq, L)

    def body(q_ref, k_ref, v_ref, do_ref, dq_ref, dk_ref, dv_ref):
        @pl.when(pl.program_id(1) == 0)
        def _():
            dk_ref[...] = jnp.zeros_like(dk_ref)
            dv_ref[...] = jnp.zeros_like(dv_ref)

        qs, p = _mem_probs(q_ref, k_ref)
        dob = do_ref[...].astype(BF16)
        dp = lax.dot_general(dob, v_ref[...].astype(BF16), _NT, preferred_element_type=F32)
        ds = p * (dp - jnp.sum(p * dp, axis=-1, keepdims=True))
        dsb = ds.astype(BF16)
        dq = jnp.dot(dsb, k_ref[...].astype(BF16), preferred_element_type=F32) * (HEAD_DIM ** -0.5)
        dq_ref[...] = dq.astype(BF16)
        dk_ref[...] += lax.dot_general(dsb, qs, _TN, preferred_element_type=F32)
        dv_ref[...] += lax.dot_general(p.astype(BF16), dob, _TN, preferred_element_type=F32)

    dq, dk, dv = pl.pallas_call(
        body, name="mem_attn_bwd",
        out_shape=(jax.ShapeDtypeStruct((L, MEM_WIDTH), BF16),
                   jax.ShapeDtypeStruct((N_MEM, MEM_WIDTH), F32),
                   jax.ShapeDtypeStruct((N_MEM, MEM_WIDTH), F32)),
        grid=(MEM_HEADS, L // tq),
        in_specs=[pl.BlockSpec((tq, HEAD_DIM), lambda h, i: (i, _MEM_Q_COL + h)),
                  pl.BlockSpec((N_MEM, HEAD_DIM), lambda h, i: (0, h)),
                  pl.BlockSpec((N_MEM, HEAD_DIM), lambda h, i: (0, MEM_HEADS + h)),
                  pl.BlockSpec((tq, HEAD_DIM), lambda h, i: (i, h))],
        out_specs=(pl.BlockSpec((tq, HEAD_DIM), lambda h, i: (i, h)),
                   pl.BlockSpec((N_MEM, HEAD_DIM), lambda h, i: (0, h)),
                   pl.BlockSpec((N_MEM, HEAD_DIM), lambda h, i: (0, h))),
        compiler_params=_params("parallel", "arbitrary"),
    )(proj, kvm, kvm, do)
    return dq, jnp.concatenate([dk, dv], axis=1)


def _tile_cumsum(x, row, reverse):
    for sh in (1, 2, 4):
        if reverse:
            x = x + jnp.where(row < SUBLANES - sh, pltpu.roll(x, SUBLANES - sh, 0), 0.0)
        else:
            x = x + jnp.where(row >= sh, pltpu.roll(x, sh, 0), 0.0)
    return x


def _fgate_fwd(pre, b_pad):
    L = pre.shape[0]
    n8 = L // SUBLANES

    def body(p_ref, b_ref, o_ref):
        row = lax.broadcasted_iota(jnp.int32, (SUBLANES, LANES), 0)
        b = b_ref[...]

        def step(i, carry):
            x = p_ref[i] + b
            logf = jnp.minimum(x, 0.0) - jnp.log(1.0 + jnp.exp(-jnp.abs(x)))
            t = _tile_cumsum(logf, row, False) + carry
            o_ref[i] = t
            return t[SUBLANES - 1:SUBLANES, :]

        lax.fori_loop(0, n8, step, jnp.zeros((1, LANES), F32))

    out = pl.pallas_call(
        body, name="fgate_fwd", out_shape=jax.ShapeDtypeStruct((n8, SUBLANES, LANES), F32),
        compiler_params=_params(),
    )(pre.reshape(n8, SUBLANES, LANES), b_pad.reshape(1, LANES))
    return out.reshape(L, LANES)


def _fgate_bwd(dfcum, pre, b_pad):
    L = pre.shape[0]
    n8 = L // SUBLANES

    def body(d_ref, p_ref, b_ref, o_ref, s_ref):
        row = lax.broadcasted_iota(jnp.int32, (SUBLANES, LANES), 0)
        b = b_ref[...]

        def step(k, carry):
            c, acc = carry
            i = n8 - 1 - k
            t = _tile_cumsum(d_ref[i], row, True) + c
            dpre = t * _sigmoid(-(p_ref[i] + b))
            o_ref[i] = dpre
            return t[0:1, :], acc + dpre

        _, acc = lax.fori_loop(0, n8, step, (jnp.zeros((1, LANES), F32), jnp.zeros((SUBLANES, LANES), F32)))
        s_ref[...] = jnp.sum(acc, axis=0, keepdims=True)

    dpre, db = pl.pallas_call(
        body, name="fgate_bwd",
        out_shape=(jax.ShapeDtypeStruct((n8, SUBLANES, LANES), F32), jax.ShapeDtypeStruct((1, LANES), F32)),
        compiler_params=_params(),
    )(dfcum.reshape(n8, SUBLANES, LANES), pre.reshape(n8, SUBLANES, LANES), b_pad.reshape(1, LANES))
    return dpre.reshape(L, LANES), db


FOX_BLOCK = 512


def _fox_scores(qs, k, fq, fk, diagonal):
    s = lax.dot_general(qs, k, _NT, preferred_element_type=F32) + fq - fk
    if diagonal:
        row = lax.broadcasted_iota(jnp.int32, s.shape, 0)
        col = lax.broadcasted_iota(jnp.int32, s.shape, 1)
        s = jnp.where(row >= col, s, NEG_BIG)
    return s


def _fox_specs(tq, L):
    nq = L // tq
    return dict(
        rows=lambda off: pl.BlockSpec((tq, HEAD_DIM), lambda h, i: (i, off + h)),
        seq=lambda off: pl.BlockSpec((L, HEAD_DIM), lambda h, i: (0, off + h)),
        col=pl.BlockSpec((None, None, tq, 1), lambda h, i: (h, i, 0, 0)),
        col_all=pl.BlockSpec((None, nq, tq, 1), lambda h, i: (h, 0, 0, 0)),
        row=pl.BlockSpec((None, None, 1, tq), lambda h, i: (h, i, 0, 0)),
        row_all=pl.BlockSpec((None, nq, 1, tq), lambda h, i: (h, 0, 0, 0)))


def _fox_fwd(proj, kv, fq, fk):
    L = proj.shape[0]
    tq = min(FOX_BLOCK, L)
    nq = L // tq
    sp = _fox_specs(tq, L)

    def body(q_ref, k_ref, v_ref, fq_ref, fk_ref, o_ref, lse_ref, m_s, l_s, acc_s):
        qi = pl.program_id(1)
        qs = (q_ref[...] * (HEAD_DIM ** -0.5)).astype(BF16)
        fq = fq_ref[...]
        m_s[...] = jnp.full_like(m_s, NEG_BIG)
        l_s[...] = jnp.zeros_like(l_s)
        acc_s[...] = jnp.zeros_like(acc_s)

        def block(j, diagonal):
            r0 = pl.multiple_of(j * tq, tq)
            s = _fox_scores(qs, k_ref[pl.ds(r0, tq), :], fq, fk_ref[j], diagonal)
            m_new = jnp.maximum(m_s[...], jnp.max(s, axis=-1, keepdims=True))
            alpha = jnp.exp(m_s[...] - m_new)
            p = jnp.exp(s - m_new)
            l_s[...] = alpha * l_s[...] + jnp.sum(p, axis=-1, keepdims=True)
            acc_s[...] = alpha * acc_s[...] + jnp.dot(p.astype(BF16), v_ref[pl.ds(r0, tq), :],
                                                      preferred_element_type=F32)
            m_s[...] = m_new

        def below(j, carry):
            block(j, False)
            return carry

        lax.fori_loop(0, qi, below, 0)
        block(qi, True)
        o_ref[...] = acc_s[...] / l_s[...]
        lse_ref[...] = m_s[...] + jnp.log(l_s[...])

    return pl.pallas_call(
        body, name="fox_fwd",
        out_shape=(jax.ShapeDtypeStruct((L, MAIN_WIDTH), F32),
                   jax.ShapeDtypeStruct((FOX_HEADS, nq, tq, 1), F32)),
        grid=(FOX_HEADS, nq),
        in_specs=[sp["rows"](0), sp["seq"](0), sp["seq"](FOX_HEADS), sp["col"], sp["row_all"]],
        out_specs=(sp["rows"](0), sp["col"]),
        scratch_shapes=[pltpu.VMEM((tq, 1), F32), pltpu.VMEM((tq, 1), F32), pltpu.VMEM((tq, HEAD_DIM), F32)],
        compiler_params=_params("parallel", "parallel"),
    )(proj, kv, kv, fq, fk)


def _fox_bwd_dq(proj, kv, fq, fk, lse, delta, datt):
    L = proj.shape[0]
    tq = min(FOX_BLOCK, L)
    nq = L // tq
    sp = _fox_specs(tq, L)

    def body(q_ref, k_ref, v_ref, fq_ref, fk_ref, lse_ref, dl_ref, do_ref, dq_ref, df_ref, acc_s, df_s):
        qi = pl.program_id(1)
        qs = (q_ref[...] * (HEAD_DIM ** -0.5)).astype(BF16)
        dob = do_ref[...].astype(BF16)
        fq, lse, dl = fq_ref[...], lse_ref[...], dl_ref[...]
        acc_s[...] = jnp.zeros_like(acc_s)
        df_s[...] = jnp.zeros_like(df_s)

        def block(j, diagonal):
            r0 = pl.multiple_of(j * tq, tq)
            k = k_ref[pl.ds(r0, tq), :]
            p = jnp.exp(_fox_scores(qs, k, fq, fk_ref[j], diagonal) - lse)
            dp = lax.dot_general(dob, v_ref[pl.ds(r0, tq), :], _NT, preferred_element_type=F32)
            ds = p * (dp - dl)
            acc_s[...] += jnp.dot(ds.astype(BF16), k, preferred_element_type=F32)
            df_s[...] += jnp.sum(ds, axis=1, keepdims=True)

        def below(j, carry):
            block(j, False)
            return carry

        lax.fori_loop(0, qi, below, 0)
        block(qi, True)
        dq_ref[...] = (acc_s[...] * (HEAD_DIM ** -0.5)).astype(BF16)
        df_ref[...] = df_s[...]

    return pl.pallas_call(
        body, name="fox_bwd_dq",
        out_shape=(jax.ShapeDtypeStruct((L, MAIN_WIDTH), BF16),
                   jax.ShapeDtypeStruct((FOX_HEADS, nq, tq, 1), F32)),
        grid=(FOX_HEADS, nq),
        in_specs=[sp["rows"](0), sp["seq"](0), sp["seq"](FOX_HEADS), sp["col"], sp["row_all"],
                  sp["col"], sp["col"], sp["rows"](0)],
        out_specs=(sp["rows"](0), sp["col"]),
        scratch_shapes=[pltpu.VMEM((tq, HEAD_DIM), F32), pltpu.VMEM((tq, 1), F32)],
        compiler_params=_params("parallel", "parallel"),
    )(proj, kv, kv, fq, fk, lse, delta, datt)


def _fox_bwd_dkv(proj, kv, fq, fk, lse, delta, datt):
    L = proj.shape[0]
    tq = min(FOX_BLOCK, L)
    nq = L // tq
    sp = _fox_specs(tq, L)

    def body(q_ref, k_ref, v_ref, fq_ref, fk_ref, lse_ref, dl_ref, do_ref,
             dk_ref, dv_ref, df_ref, dk_s, dv_s, df_s):
        ki = pl.program_id(1)
        k, v, fk = k_ref[...], v_ref[...], fk_ref[...]
        dk_s[...] = jnp.zeros_like(dk_s)
        dv_s[...] = jnp.zeros_like(dv_s)
        df_s[...] = jnp.zeros_like(df_s)

        def block(i, diagonal):
            r0 = pl.multiple_of(i * tq, tq)
            qs = (q_ref[pl.ds(r0, tq), :] * (HEAD_DIM ** -0.5)).astype(BF16)
            dob = do_ref[pl.ds(r0, tq), :].astype(BF16)
            p = jnp.exp(_fox_scores(qs, k, fq_ref[i], fk, diagonal) - lse_ref[i])
            dp = lax.dot_general(dob, v, _NT, preferred_element_type=F32)
            ds = p * (dp - dl_ref[i])
            dv_s[...] += lax.dot_general(p.astype(BF16), dob, _TN, preferred_element_type=F32)
            dk_s[...] += lax.dot_general(ds.astype(BF16), qs, _TN, preferred_element_type=F32)
            df_s[...] -= jnp.sum(ds, axis=0, keepdims=True)

        def above(i, carry):
            block(i, False)
            return carry

        block(ki, True)
        lax.fori_loop(ki + 1, nq, above, 0)
        dk_ref[...] = dk_s[...].astype(BF16)
        dv_ref[...] = dv_s[...].astype(BF16)
        df_ref[...] = df_s[...]

    return pl.pallas_call(
        body, name="fox_bwd_dkv",
        out_shape=(jax.ShapeDtypeStruct((L, MAIN_WIDTH), BF16),
                   jax.ShapeDtypeStruct((L, MAIN_WIDTH), BF16),
                   jax.ShapeDtypeStruct((FOX_HEADS, nq, 1, tq), F32)),
        grid=(FOX_HEADS, nq),
        in_specs=[sp["seq"](0), sp["rows"](0), sp["rows"](FOX_HEADS), sp["col_all"], sp["row"],
                  sp["col_all"], sp["col_all"], sp["seq"](0)],
        out_specs=(sp["rows"](0), sp["rows"](0), sp["row"]),
        scratch_shapes=[pltpu.VMEM((tq, HEAD_DIM), F32), pltpu.VMEM((tq, HEAD_DIM), F32),
                        pltpu.VMEM((1, tq), F32)],
        compiler_params=_params("parallel", "parallel"),
    )(proj, kv, kv, fq, fk, lse, delta, datt)


def _pad_lanes(a):
    return jnp.pad(a, ((0, 0), (0, LANES - a.shape[1])))


def _mem_branch_fwd(mem, g, w_mk, proj, tag):
    memn = _rmsnorm_fwd(mem, g, name="mem_norm_" + tag, out_dtype=BF16)
    kvm = _mm(memn, w_mk, name="mem_kv_" + tag)
    return memn, kvm, _mem_attn_fwd(proj, kvm)


def _mem_branch_bwd(mem, g, w_mk, proj, memn, kvm, do_mem, tag):
    dqm, dkvm = _mem_attn_bwd(proj, kvm, do_mem)
    dkvm = dkvm.astype(BF16)
    dw_mk = _mm(memn, dkvm, ta=True, name="dw_mem_kv_" + tag, out_dtype=BF16)
    dmemn = _mm(dkvm, w_mk, tb=True, name="dmemn_" + tag)
    _, dg = _rmsnorm_bwd(mem, g, dmemn, name="mem_norm_bwd_" + tag, dx_dtype=BF16)
    return dqm, dw_mk, dg


def _local_step(x, mem, target, w, fetch=None, grads_ready=None):
    if grads_ready is None:
        grads_ready = lambda group, grads, token: token
    L = x.shape[0]
    g = {}
    w = dict(w)

    b_re_t = jnp.transpose(w["b_re"], (0, 2, 1))
    b_im_t = jnp.transpose(w["b_im"], (0, 2, 1))
    ar, ai, bbr_t, bbi_t = _s5_prep(w["lam_re"], w["lam_im"], w["log_step"], b_re_t, b_im_t)
    bmat, cmat = _s5_block_mats(bbr_t, bbi_t, w["c_re"], w["c_im"])
    a_rows = _s5_a_rows(ar, ai)

    hn0 = _rmsnorm_fwd(x, w["pre_norm_g"][0], name="pre_norm_0", out_dtype=BF16)
    if fetch is not None:
        w.update(fetch("a", hn0))
    proj_a = _mm(hn0, w["w_in_a"], name="in_proj_a")
    y, yg, xp = _s5_fwd(proj_a, bmat, cmat, a_rows, w["d_skip"])
    if fetch is not None:
        w.update(fetch("b", yg))
    t = _mm(yg, w["w_glu"], name="glu_proj")
    memn0, kvm0, om0 = _mem_branch_fwd(mem, w["mem_norm_g"][0], w["w_mem_kv"][0], proj_a, "0")
    cat0 = _gate_a_fwd(y, t, w["b_glu"], proj_a, om0)
    o0 = _mm(cat0, w["w_out"][0], name="out_proj_0")
    h1 = _rmsnorm_fwd(o0, w["post_norm_g"][0], res=x, name="post_norm_0")

    kv_in = _rmsnorm_fwd(h1, w["kv_norm_g"], name="kv_norm", out_dtype=BF16)
    if fetch is not None:
        w.update(fetch("c", kv_in))
    kv = _mm(kv_in, w["w_kv"], name="kv_proj", out_dtype=BF16)
    pre_f = _mm(kv_in, w["w_fgate"], name="fgate_proj")
    b_f = jnp.pad(w["b_fgate"], (0, LANES - FOX_HEADS))
    fcum = _fgate_fwd(pre_f, b_f)
    fc = jnp.transpose(fcum[:, :FOX_HEADS])
    tq = min(FOX_BLOCK, L)
    fq, fk = fc.reshape(FOX_HEADS, L // tq, tq, 1), fc.reshape(FOX_HEADS, L // tq, 1, tq)

    hn1 = _rmsnorm_fwd(h1, w["pre_norm_g"][1], name="pre_norm_1", out_dtype=BF16)
    proj_b = _mm(hn1, w["w_in_b"], name="in_proj_b")
    att, lse = _fox_fwd(proj_b, kv, fq, fk)
    memn1, kvm1, om1 = _mem_branch_fwd(mem, w["mem_norm_g"][1], w["w_mem_kv"][1], proj_b, "1")
    cat1 = _gate_b_fwd(att, proj_b, om1)
    o1 = _mm(cat1, w["w_out"][1], name="out_proj_1")
    dh2, loss_row = _final_norm_loss(o1, w["post_norm_g"][1], h1, target)

    do1, dpost1 = _rmsnorm_bwd(o1, w["post_norm_g"][1], dh2, name="post_norm_bwd_1", dx_dtype=BF16)
    dcat1 = _mm(do1, w["w_out"][1], tb=True, name="dcat_1")
    g["w_out_1"] = _mm(cat1, do1, ta=True, name="dw_out_1", out_dtype=BF16)
    datt, dz1, dom1, dzm1, delta = _gate_b_bwd(dcat1, att, proj_b, om1)
    dqm1, g["w_mem_kv_1"], dmemg1 = _mem_branch_bwd(mem, w["mem_norm_g"][1], w["w_mem_kv"][1], proj_b,
                                                   memn1, kvm1, dom1, "1")
    delta = delta.reshape(fq.shape)
    dq, dfq = _fox_bwd_dq(proj_b, kv, fq, fk, lse, delta, datt)
    dk, dv, dfk = _fox_bwd_dkv(proj_b, kv, fq, fk, lse, delta, datt)
    dproj_b = jnp.concatenate([dq, dz1, dqm1, dzm1], axis=1)
    g["w_in_b"] = _mm(hn1, dproj_b, ta=True, name="dw_in_b", out_dtype=BF16, shards=N_CHIPS)
    dhn1 = _mm(dproj_b, w["w_in_b"], tb=True, name="dhn_1")

    dkv = jnp.concatenate([dk, dv], axis=1)
    g["w_kv"] = _mm(kv_in, dkv, ta=True, name="dw_kv", out_dtype=BF16, shards=N_CHIPS)
    dkv_in_a = _mm(dkv, w["w_kv"], tb=True, name="dkv_in_kv")
    dfcum = _pad_lanes(jnp.transpose(dfq.reshape(FOX_HEADS, L) + dfk.reshape(FOX_HEADS, L)))
    dpre_f, db_f = _fgate_bwd(dfcum, pre_f, b_f)
    g["b_fgate"] = db_f[0, :FOX_HEADS]
    g["w_fgate"] = _mm(kv_in, dpre_f, ta=True, name="dw_fgate")[:, :FOX_HEADS]
    dkv_in_b = _mm(dpre_f, w["w_fgate"], tb=True, name="dkv_in_fgate")
    dh1_kv, g["kv_norm_g"] = _rmsnorm_bwd(h1, w["kv_norm_g"], (dkv_in_a, dkv_in_b), name="kv_norm_bwd")
    dh1, dpre1 = _rmsnorm_bwd(h1, w["pre_norm_g"][1], dhn1, adds=(dh2, dh1_kv), name="pre_norm_bwd_1")
    dh1 = grads_ready("b", g, dh1)

    do0, dpost0 = _rmsnorm_bwd(o0, w["post_norm_g"][0], dh1, name="post_norm_bwd_0", dx_dtype=BF16)
    dcat0 = _mm(do0, w["w_out"][0], tb=True, name="dcat_0")
    g["w_out_0"] = _mm(cat0, do0, ta=True, name="dw_out_0", out_dtype=BF16)
    dz0, dzm0, dt, dyg_a, dom0, db_glu = _gate_a_bwd(dcat0, y, t, w["b_glu"], proj_a, om0)
    g["b_glu"] = db_glu[0]
    g["w_glu"] = _mm(yg, dt, ta=True, name="dw_glu", out_dtype=BF16)
    dyg_b = _mm(dt, w["w_glu"], tb=True, name="dyg")
    dqm0, g["w_mem_kv_0"], dmemg0 = _mem_branch_bwd(mem, w["mem_norm_g"][0], w["w_mem_kv"][0], proj_a,
                                                   memn0, kvm0, dom0, "0")
    dyg_b = grads_ready("a1", g, dyg_b)
    du, db_blk, dc_blk, da_rows, dd_skip = _s5_bwd(proj_a, dyg_a, dyg_b, y, xp, bmat, cmat, a_rows, w["d_skip"])
    g["d_skip"] = dd_skip[0]
    dproj_a = jnp.concatenate([du.astype(BF16), dz0, dqm0, dzm0], axis=1)
    g["w_in_a"] = _mm(hn0, dproj_a, ta=True, name="dw_in_a", out_dtype=BF16, shards=N_CHIPS)
    dproj_a = grads_ready("a2", g, dproj_a)
    dhn0 = _mm(dproj_a, w["w_in_a"], tb=True, name="dhn_0")
    grad_x, dpre0 = _rmsnorm_bwd(x, w["pre_norm_g"][0], dhn0, adds=(dh1,), name="pre_norm_bwd_0")

    dbb = _s5_block_diag(db_blk)
    dcc = _s5_block_diag(dc_blk)
    g["c_re"], g["c_im"] = dcc[0], -dcc[1]
    d_ar = da_rows[:, 0, :STATE_COLS].reshape(SSM_GROUPS, SSM_STATE)
    d_ai = da_rows[:, 0, STATE_COLS:].reshape(SSM_GROUPS, SSM_STATE)
    dlr, dli, dls, dbr_t, dbi_t = _s5_prep_bwd(w["lam_re"], w["lam_im"], w["log_step"], b_re_t, b_im_t,
                                               d_ar, d_ai, dbb[0], dbb[1])
    g["lam_re"], g["lam_im"], g["log_step"] = dlr, dli, dls[:, 0]
    g["b_re"] = jnp.transpose(dbr_t, (0, 2, 1))
    g["b_im"] = jnp.transpose(dbi_t, (0, 2, 1))
    g["pre_norm_g"] = jnp.stack([dpre0, dpre1])
    g["post_norm_g"] = jnp.stack([dpost0, dpost1])
    g["mem_norm_g"] = jnp.stack([dmemg0, dmemg1])
    return loss_row, grad_x, g


_MESH = pl.DeviceIdType.MESH
_ANY = pl.BlockSpec(memory_space=pl.ANY)


def _place():
    x, y, c = lax.axis_index("x"), lax.axis_index("y"), lax.axis_index("c")
    chips = [(1 - x, y), (x, 1 - y), (1 - x, 1 - y)]
    return x, y, c, chips


_HBM = pl.BlockSpec(memory_space=pltpu.HBM)
_SEM = pl.BlockSpec(memory_space=pltpu.SEMAPHORE)
_SIDE = pltpu.SideEffectType.DATAFLOW_SIDE_EFFECTING


def _in_hbm(a):
    return pltpu.with_memory_space_constraint(a, pltpu.HBM)


def _hbm_like(a):
    return pltpu.HBM(a.shape, a.dtype)


def _ici_copies(srcs, lands, send_sem, recv_sem, src_at, dst_at, wait_at):
    x, y, c, chips = _place()
    start, wait = [], []
    for i in range(len(srcs)):
        for k, (cx, cy) in enumerate(chips):
            sem = dict(send_sem=send_sem.at[3 * i + k], recv_sem=recv_sem.at[3 * i + k],
                       device_id=(cx, cy, c), device_id_type=_MESH)
            src = src_at(srcs[i], 2 * cx + cy, c)
            start.append(pltpu.make_async_remote_copy(src_ref=src, dst_ref=dst_at(lands[i], 2 * x + y, k, c), **sem))
            wait.append(pltpu.make_async_remote_copy(src_ref=src, dst_ref=wait_at(lands[i], 2 * cx + cy, k, c), **sem))
    return start, wait


def _ici_start(srcs, lands, token, route, *, name):
    n = len(srcs)

    def body(*refs):
        start, _ = _ici_copies(refs[:n], refs[n:2 * n], refs[2 * n + 1], refs[2 * n + 2], *route)
        for cp in start:
            cp.start()

    sems = pltpu.SemaphoreType.DMA((3 * n,))
    outs = pl.pallas_call(
        body, name=name,
        out_shape=(sems, sems, *[_hbm_like(a) for a in srcs], *[_hbm_like(a) for a in lands], _hbm_like(token)),
        in_specs=[_HBM] * (2 * n + 1), out_specs=(_SEM, _SEM, *[_HBM] * (2 * n + 1)),
        input_output_aliases={i: 2 + i for i in range(2 * n + 1)},
        compiler_params=pltpu.CompilerParams(has_side_effects=_SIDE),
    )(*[_in_hbm(a) for a in srcs], *[_in_hbm(a) for a in lands], _in_hbm(token))
    return (outs[0], outs[1], list(outs[2:2 + n]), list(outs[2 + n:2 + 2 * n])), outs[2 + 2 * n]


def _ici_wait(handle, after, route, *, name):
    send_sem, recv_sem, srcs, lands = handle
    n = len(srcs)

    def body(*refs):
        _, wait = _ici_copies(refs[:n], refs[n:2 * n], refs[2 * n], refs[2 * n + 1], *route)
        for cp in wait:
            cp.wait_send()
            cp.wait_recv()

    outs = pl.pallas_call(
        body, name=name,
        out_shape=(*[_hbm_like(a) for a in srcs], *[_hbm_like(a) for a in lands]),
        in_specs=[_HBM] * (2 * n) + [_SEM, _SEM, _ANY], out_specs=tuple([_HBM] * (2 * n)),
        input_output_aliases={i: i for i in range(2 * n)},
        compiler_params=pltpu.CompilerParams(has_side_effects=_SIDE),
    )(*srcs, *lands, send_sem, recv_sem, after)
    return list(outs[:n]), list(outs[n:])


_GATHER_ROUTE = (lambda s, j, c: s.at[c], lambda l, me, k, c: l.at[me, c], lambda l, j, k, c: l.at[j, c])
_SCATTER_ROUTE = (lambda s, j, c: s.at[j], lambda l, me, k, c: l.at[k], lambda l, j, k, c: l.at[k])


def _gather_forward(lands, tag):
    n = len(lands)

    def body(*refs):
        ins, outs = refs[:n], refs[n:2 * n]
        send_sem, recv_sem = refs[2 * n:]
        x, y, c, chips = _place()

        def copy(i, k, half):
            cx, cy = chips[k]
            return pltpu.make_async_remote_copy(
                src_ref=ins[i].at[2 * cx + cy, half], dst_ref=outs[i].at[2 * cx + cy, half],
                send_sem=send_sem.at[3 * i + k], recv_sem=recv_sem.at[3 * i + k],
                device_id=(x, y, 1 - c), device_id_type=_MESH)

        copies = [copy(i, k, c) for i in range(n) for k in range(3)]
        for cp in copies:
            cp.start()
        for i in range(n):
            for k in range(3):
                copy(i, k, 1 - c).wait_recv()
        for cp in copies:
            cp.wait_send()

    return pl.pallas_call(
        body, name="gather_forward_to_sibling_" + tag,
        out_shape=[jax.ShapeDtypeStruct(a.shape, a.dtype) for a in lands],
        in_specs=[_ANY] * n, out_specs=[_ANY] * n,
        input_output_aliases={i: i for i in range(n)},
        scratch_shapes=[pltpu.SemaphoreType.DMA((3 * n,)), pltpu.SemaphoreType.DMA((3 * n,))],
    )(*lands)


def _swap_halves(grads, tag):
    n = len(grads)

    def body(*refs):
        ins, outs = refs[:n], refs[n:2 * n]
        send_sem, recv_sem = refs[2 * n:]
        x, y, c, _ = _place()
        copies = [pltpu.make_async_remote_copy(
            src_ref=ins[i].at[:, 1 - c], dst_ref=outs[i],
            send_sem=send_sem.at[i], recv_sem=recv_sem.at[i],
            device_id=(x, y, 1 - c), device_id_type=_MESH) for i in range(n)]
        for cp in copies:
            cp.start()
        for cp in copies:
            cp.wait()

    return pl.pallas_call(
        body, name="grad_swap_halves_" + tag,
        out_shape=[jax.ShapeDtypeStruct((N_CHIPS,) + g.shape[2:], g.dtype) for g in grads],
        in_specs=[_ANY] * n, out_specs=[_ANY] * n,
        scratch_shapes=[pltpu.SemaphoreType.DMA((n,)), pltpu.SemaphoreType.DMA((n,))],
    )(*grads)


def _sum_rows(h, C):
    return next(c for c in (1024, 512, 256, 192, 128, 64, 32, 16, 8) if h % c == 0 and c * C <= 1 << 20)


def _pair_sum(g, r, c_idx, *, name):
    _, _, h, C = g.shape
    tr = _sum_rows(h, C)

    def body(c_ref, g_ref, r_ref, o_ref):
        o_ref[...] = (g_ref[...].astype(F32) + r_ref[...].astype(F32)).astype(o_ref.dtype)

    return pl.pallas_call(
        body, name=name, out_shape=jax.ShapeDtypeStruct((N_CHIPS, h, C), g.dtype),
        grid_spec=pltpu.PrefetchScalarGridSpec(
            num_scalar_prefetch=1, grid=(N_CHIPS, h // tr),
            in_specs=[pl.BlockSpec((None, None, tr, C), lambda j, i, s: (j, s[0], i, 0)),
                      pl.BlockSpec((None, tr, C), lambda j, i, s: (j, i, 0))],
            out_specs=pl.BlockSpec((None, tr, C), lambda j, i, s: (j, i, 0))),
        compiler_params=_params("parallel", "parallel"),
    )(c_idx, g, r)


def _owner_sum(s, r, jc_idx, *, name):
    _, h, C = s.shape
    tr = _sum_rows(h, C)

    def body(jc_ref, s_ref, r_ref, o_ref):
        acc = s_ref[...].astype(F32)
        for k in range(3):
            acc = acc + r_ref[k].astype(F32)
        o_ref[...] = acc

    return pl.pallas_call(
        body, name=name, out_shape=jax.ShapeDtypeStruct((2, h, C), F32),
        grid_spec=pltpu.PrefetchScalarGridSpec(
            num_scalar_prefetch=1, grid=(h // tr,),
            in_specs=[pl.BlockSpec((None, tr, C), lambda i, s: (s[0], i, 0)),
                      pl.BlockSpec((3, tr, C), lambda i, s: (0, i, 0))],
            out_specs=pl.BlockSpec((None, tr, C), lambda i, s: (s[1], i, 0))),
        compiler_params=_params("parallel"),
    )(jc_idx, s, r)


def _share_with_sibling(bufs, tag):
    n = len(bufs)

    def body(*refs):
        ins, outs = refs[:n], refs[n:2 * n]
        send_sem, recv_sem = refs[2 * n:]
        x, y, c, _ = _place()

        def copy(i, half):
            return pltpu.make_async_remote_copy(
                src_ref=ins[i].at[half], dst_ref=outs[i].at[half],
                send_sem=send_sem.at[i], recv_sem=recv_sem.at[i],
                device_id=(x, y, 1 - c), device_id_type=_MESH)

        copies = [copy(i, c) for i in range(n)]
        for cp in copies:
            cp.start()
        for i in range(n):
            copy(i, 1 - c).wait_recv()
        for cp in copies:
            cp.wait_send()

    return pl.pallas_call(
        body, name="grad_share_with_sibling_" + tag,
        out_shape=[jax.ShapeDtypeStruct(b.shape, b.dtype) for b in bufs],
        in_specs=[_ANY] * n, out_specs=[_ANY] * n,
        input_output_aliases={i: i for i in range(n)},
        scratch_shapes=[pltpu.SemaphoreType.DMA((n,)), pltpu.SemaphoreType.DMA((n,))],
    )(*bufs)


def _chip_sums(grads, c_idx, tag):
    views = [g.reshape(N_CHIPS, 2, g.shape[1] // 2, g.shape[2]) for g in grads]
    arrived = _swap_halves(views, tag)
    return [_pair_sum(v, r, c_idx, name=f"grad_pair_sum_{tag}_{i}") for i, (v, r) in enumerate(zip(views, arrived))]


def _owner_totals(sums, arrived, jc_idx, tag):
    halves = [_owner_sum(s, r, jc_idx, name=f"grad_owner_sum_{tag}_{i}") for i, (s, r) in enumerate(zip(sums, arrived))]
    return [f.reshape(-1, f.shape[2]) for f in _share_with_sibling(halves, tag)]


def _all_reduce_small(buf):
    R = buf.shape[0]
    n_dev = 2 * N_CHIPS

    def body(x_ref, o_ref, all_ref, send_sems, recv_sems, local_sem):
        x, y, c, chips = _place()
        me, sib = (x, y, c), (x, y, 1 - c)

        def rows(px, py, pc):
            return all_ref.at[4 * px + 2 * py + pc]

        def copy(k, block, to, src=None):
            return pltpu.make_async_remote_copy(
                src_ref=rows(*block) if src is None else src, dst_ref=rows(*block),
                send_sem=send_sems.at[k], recv_sem=recv_sems.at[k], device_id=to, device_id_type=_MESH)

        mine = pltpu.make_async_copy(x_ref, rows(*me), local_sem)
        mine.start()
        first = [copy(0, me, sib, src=x_ref)]
        first += [copy(1 + j, me, (*chip, c), src=x_ref) for j, chip in enumerate(chips)]
        for cp in first:
            cp.start()
        passed = [copy(4 + j, (*chip, c), sib) for j, chip in enumerate(chips)]
        for j, chip in enumerate(chips):
            copy(1 + j, (*chip, c), me).wait_recv()
            passed[j].start()
        copy(0, sib, me).wait_recv()
        for j, chip in enumerate(chips):
            copy(4 + j, (*chip, 1 - c), me).wait_recv()
        for cp in first + passed:
            cp.wait_send()
        mine.wait()
        acc = all_ref[0]
        for d in range(1, n_dev):
            acc = acc + all_ref[d]
        o_ref[...] = acc

    vmem = pl.BlockSpec(memory_space=pltpu.VMEM)
    return pl.pallas_call(
        body, name="all_reduce_small", out_shape=jax.ShapeDtypeStruct((R, LANES), F32),
        in_specs=[vmem], out_specs=vmem,
        scratch_shapes=[pltpu.VMEM((n_dev, R, LANES), F32),
                        pltpu.SemaphoreType.DMA((7,)), pltpu.SemaphoreType.DMA((7,)), pltpu.SemaphoreType.DMA],
        compiler_params=_params(),
    )(buf)


def _adamw(w, g, m, v, *, name):
    R, C = w.shape
    whole_fits = 7 * 2 * R * C * 4 <= VMEM_LIMIT_BYTES // 2
    tr = R if whole_fits else next(c for c in (256, 192, 128, 64, 32, 16, 8) if R % c == 0)

    def body(w_ref, g_ref, m_ref, v_ref, d_ref, nm_ref, nv_ref):
        g = g_ref[...]
        m = ADAM_B1 * m_ref[...] + (1.0 - ADAM_B1) * g
        v = ADAM_B2 * v_ref[...] + (1.0 - ADAM_B2) * (g * g)
        nm_ref[...] = m
        nv_ref[...] = v
        m_hat = m / (1.0 - ADAM_B1 ** ADAM_STEP)
        v_hat = v / (1.0 - ADAM_B2 ** ADAM_STEP)
        d_ref[...] = -ADAM_LR * (m_hat / (jnp.sqrt(v_hat) + ADAM_EPS) + ADAM_WD * w_ref[...])

    blk = pl.BlockSpec((tr, C), lambda i: (i, 0))
    sds = jax.ShapeDtypeStruct((R, C), F32)
    return pl.pallas_call(
        body, name=name, out_shape=(sds, sds, sds), grid=(R // tr,),
        in_specs=[blk] * 4, out_specs=(blk, blk, blk),
        compiler_params=_params("parallel"),
    )(w, g, m, v)


_TILE = SUBLANES * LANES


def _pack(arrays):
    rows = []
    for a in arrays:
        flat = a.reshape(-1)
        flat = jnp.pad(flat, (0, (-flat.shape[0]) % _TILE))
        rows.append(flat.reshape(-1, LANES))
    return jnp.concatenate(rows, axis=0)


def _unpack(buf, shapes):
    out, r = [], 0
    for s in shapes:
        size = math.prod(s)
        nr = -(-size // _TILE) * SUBLANES
        out.append(buf[r:r + nr].reshape(-1)[:size].reshape(s))
        r += nr
    return out


_BIG = ("w_in_a", "w_glu", "w_kv", "w_in_b", "w_mem_kv", "w_out")
_REPLICATED = ("pre_norm_g", "post_norm_g", "lam_re", "lam_im", "log_step", "b_re", "b_im", "c_re", "c_im",
               "kv_norm_g", "b_fgate", "mem_norm_g")
_SHARDED_SMALL = ("d_skip", "b_glu", "w_fgate")
_WEIGHTS = ("pre_norm_g", "post_norm_g", "w_in_a", "lam_re", "lam_im", "log_step", "b_re", "b_im", "c_re",
            "c_im", "d_skip", "w_glu", "b_glu", "kv_norm_g", "w_kv", "w_fgate", "b_fgate", "w_in_b",
            "mem_norm_g", "w_mem_kv", "w_out")


def _halves(a):
    return a.reshape(2, a.shape[0] // 2, a.shape[1])


def _unhalve(a):
    return a.reshape(N_CHIPS, 2 * a.shape[2], a.shape[3])


def _columns(a):
    return jnp.transpose(a, (1, 0, 2)).reshape(a.shape[1], N_CHIPS * a.shape[2])


def kernel(x, mem, pre_norm_g, post_norm_g, w_in_a, lam_re, lam_im, log_step, b_re, b_im, c_re, c_im, d_skip, w_glu, b_glu, kv_norm_g, w_kv, w_fgate, b_fgate, w_in_b, mem_norm_g, w_mem_kv, w_out, loss_target, m_pre_norm_g, m_post_norm_g, m_w_in_a, m_lam_re, m_lam_im, m_log_step, m_b_re, m_b_im, m_c_re, m_c_im, m_d_skip, m_w_glu, m_b_glu, m_kv_norm_g, m_w_kv, m_w_fgate, m_b_fgate, m_w_in_b, m_mem_norm_g, m_w_mem_kv, m_w_out, v_pre_norm_g, v_post_norm_g, v_w_in_a, v_lam_re, v_lam_im, v_log_step, v_b_re, v_b_im, v_c_re, v_c_im, v_d_skip, v_w_glu, v_b_glu, v_kv_norm_g, v_w_kv, v_w_fgate, v_b_fgate, v_w_in_b, v_mem_norm_g, v_w_mem_kv, v_w_out):
    a = dict(locals())
    xi, yi, ci = lax.axis_index("x"), lax.axis_index("y"), lax.axis_index("c")
    chip = 2 * xi + yi
    c_idx = jnp.reshape(ci, (1,)).astype(jnp.int32)
    jc_idx = jnp.stack([chip, ci]).astype(jnp.int32)

    vec = jnp.zeros((2 * SUBLANES, MAIN_WIDTH // N_CHIPS), F32)
    vec = vec.at[0].set(a["d_skip"][0]).at[1].set(a["b_glu"][0])
    def own_slot(gathered, parts):
        return [lax.dynamic_update_index_in_dim(g, p, chip, 0) for g, p in zip(gathered, parts)]

    parts_a = [_halves(a["w_in_a"][0].astype(BF16)), _halves(vec)]
    parts_b = [_halves(a["w_glu"][0].astype(BF16)), _halves(a["w_mem_kv"].reshape(-1, 2 * MEM_WIDTH).astype(BF16)),
               _halves(a["w_out"].reshape(-1, D_MODEL).astype(BF16))]
    parts_c = [_halves(a["w_kv"].astype(BF16)), _halves(_pad_lanes(a["w_fgate"]).astype(BF16)),
               _halves(a["w_in_b"][0].astype(BF16))]
    travelling, token = {}, a["pre_norm_g"]
    for tag, parts in (("a", parts_a), ("b", parts_b), ("c", parts_c)):
        lands = [lax.empty((N_CHIPS,) + p.shape, p.dtype) for p in parts]
        travelling[tag], token = _ici_start(parts, lands, token, _GATHER_ROUTE, name=f"gather_{tag}_start")

    def fetch(tag, after):
        parts, lands = _ici_wait(travelling[tag], after, _GATHER_ROUTE, name=f"gather_{tag}_wait")
        full = own_slot(_gather_forward(lands, tag), parts)
        if tag == "a":
            w_in_a, vecs = full
            return dict(w_in_a=_columns(_unhalve(w_in_a)), d_skip=vecs[:, 0, 0, :].reshape(MAIN_WIDTH),
                        b_glu=vecs[:, 0, 1, :].reshape(MAIN_WIDTH))
        if tag == "b":
            w_glu, w_mk, w_out = full
            return dict(w_glu=w_glu.reshape(MAIN_WIDTH, MAIN_WIDTH),
                        w_mem_kv=jnp.transpose(w_mk, (1, 0, 2, 3)).reshape(2, D_MODEL, 2 * MEM_WIDTH),
                        w_out=jnp.transpose(w_out, (1, 0, 2, 3)).reshape(2, D_MODEL, D_MODEL))
        w_kv, w_fg, w_in_b = full
        return dict(w_kv=_columns(_unhalve(w_kv)), w_fgate=w_fg.reshape(D_MODEL, LANES),
                    w_in_b=_columns(_unhalve(w_in_b)))

    w = dict(
        pre_norm_g=token, post_norm_g=a["post_norm_g"], mem_norm_g=a["mem_norm_g"],
        kv_norm_g=a["kv_norm_g"], b_fgate=a["b_fgate"],
        lam_re=a["lam_re"][0], lam_im=a["lam_im"][0], log_step=a["log_step"][0],
        b_re=a["b_re"][0], b_im=a["b_im"][0], c_re=a["c_re"][0], c_im=a["c_im"][0])

    sent = {}

    def grads_ready(tag, g, token):
        big = {"b": lambda: [g["w_kv"], g["w_in_b"], g["w_mem_kv_1"].reshape(N_CHIPS, -1, 2 * MEM_WIDTH),
                             g["w_out_1"].reshape(N_CHIPS, -1, D_MODEL)],
               "a1": lambda: [g["w_glu"].reshape(N_CHIPS, -1, MAIN_WIDTH),
                              g["w_mem_kv_0"].reshape(N_CHIPS, -1, 2 * MEM_WIDTH),
                              g["w_out_0"].reshape(N_CHIPS, -1, D_MODEL)],
               "a2": lambda: [g["w_in_a"]]}[tag]()
        sums = _chip_sums(big, c_idx, tag)
        lands = [lax.empty((3,) + s.shape[1:], s.dtype) for s in sums]
        sent[tag], token = _ici_start(sums, lands, token, _SCATTER_ROUTE, name=f"grad_send_{tag}_start")
        return token

    loss_row, grad_x, g = _local_step(a["x"][0], a["mem"][0], a["loss_target"][0], w, fetch, grads_ready)
    loss = lax.psum(jnp.sum(loss_row), MESH_AXES)

    def totals(tag, after):
        sums, arrived = _ici_wait(sent[tag], after, _SCATTER_ROUTE, name=f"grad_send_{tag}_wait")
        return _owner_totals(sums, arrived, jc_idx, tag)

    r_kv, r_in_b, r_mk1, r_out1 = totals("b", grad_x)
    r_glu, r_mk0, r_out0 = totals("a1", r_out1)
    (r_in_a,) = totals("a2", r_out0)
    grads = {"w_in_a": r_in_a[None], "w_glu": r_glu[None], "w_kv": r_kv, "w_in_b": r_in_b[None],
             "w_mem_kv": jnp.stack([r_mk0, r_mk1]), "w_out": jnp.stack([r_out0, r_out1])}

    small_names = _REPLICATED + _SHARDED_SMALL
    small = _all_reduce_small(_pack([g[n] for n in small_names]))
    small = dict(zip(small_names, _unpack(small, [g[n].shape for n in small_names])))
    for n in _REPLICATED:
        grads[n] = small[n].reshape(a[n].shape)
    nd = MAIN_WIDTH // N_CHIPS
    grads["d_skip"] = lax.dynamic_slice(small["d_skip"], (chip * nd,), (nd,))[None]
    grads["b_glu"] = lax.dynamic_slice(small["b_glu"], (chip * nd,), (nd,))[None]
    nf = D_MODEL // N_CHIPS
    grads["w_fgate"] = lax.dynamic_slice(small["w_fgate"], (chip * nf, 0), (nf, FOX_HEADS))

    delta, new_m, new_v = {}, {}, {}
    for n in _BIG:
        shape = a[n].shape
        d2 = (-1, shape[-1])
        d, m, v = _adamw(a[n].reshape(d2), grads[n].reshape(d2), a["m_" + n].reshape(d2),
                         a["v_" + n].reshape(d2), name="adamw_" + n)
        delta[n], new_m[n], new_v[n] = d.reshape(shape), m.reshape(shape), v.reshape(shape)
    shapes = [a[n].shape for n in small_names]
    d, m, v = _adamw(_pack([a[n] for n in small_names]), _pack([grads[n] for n in small_names]),
                     _pack([a["m_" + n] for n in small_names]), _pack([a["v_" + n] for n in small_names]),
                     name="adamw_small")
    for n, dd, mm, vv in zip(small_names, _unpack(d, shapes), _unpack(m, shapes), _unpack(v, shapes)):
        delta[n], new_m[n], new_v[n] = dd, mm, vv

    return (loss, grad_x[None], *[grads[n] for n in _WEIGHTS], *[delta[n] for n in _WEIGHTS],
            *[new_m[n] for n in _WEIGHTS], *[new_v[n] for n in _WEIGHTS])
```

```python
import functools
import math

import jax
import jax.numpy as jnp
from jax import lax
from jax.experimental import pallas as pl
from jax.experimental.pallas import tpu as pltpu

F32 = jnp.float32
BF16 = jnp.bfloat16

D_MODEL = 2048
N_MEM = 256
MAIN_WIDTH = 1536
MEM_WIDTH = 512
IN_WIDTH = 2 * MAIN_WIDTH + 2 * MEM_WIDTH
HEAD_DIM = 128
FOX_HEADS = MAIN_WIDTH // HEAD_DIM
MEM_HEADS = MEM_WIDTH // HEAD_DIM
SSM_GROUP = 16
SSM_GROUPS = MAIN_WIDTH // SSM_GROUP
SSM_STATE = 64
GROUPS_PER_BLOCK = 8
SSM_BLOCKS = SSM_GROUPS // GROUPS_PER_BLOCK
STATE_COLS = GROUPS_PER_BLOCK * SSM_STATE
EPS = 1e-6
ADAM_LR = 0.001
ADAM_B1 = 0.9
ADAM_B2 = 0.999
ADAM_EPS = 1e-08
ADAM_WD = 0.01
ADAM_STEP = 10
N_CHIPS = 4
LANES = 128
SUBLANES = 8
VMEM_LIMIT_BYTES = 56 * 1024 * 1024
NEG_BIG = -1e30
MESH_AXES = ("x", "y", "c")


def _params(*sem):
    return pltpu.CompilerParams(dimension_semantics=sem if sem else None,
                                vmem_limit_bytes=VMEM_LIMIT_BYTES)


def _sigmoid(x):
    return 1.0 / (1.0 + jnp.exp(-x))


def _gelu(x):
    c = math.sqrt(2.0 / math.pi)
    return 0.5 * x * (1.0 + jnp.tanh(c * (x + 0.044715 * (x * x * x))))


def _gelu_grad(x):
    c = math.sqrt(2.0 / math.pi)
    t = jnp.tanh(c * (x + 0.044715 * (x * x * x)))
    return 0.5 * (1.0 + t) + 0.5 * x * (1.0 - t * t) * (c * (1.0 + 3.0 * 0.044715 * (x * x)))


def _silu_and_grad(z):
    s = _sigmoid(z)
    return z * s, s * (1.0 + z * (1.0 - s))


_TILE_CHOICES = (2048, 1024, 768, 512, 384, 256, LANES)


def _tile(n, cap):
    return next(c for c in _TILE_CHOICES if c <= cap and n % c == 0)


def _mm(a, b, *, name, ta=False, tb=False, out_dtype=F32, shards=1, tm=1024, tn=1024, tk=2048):
    if ta:
        K, M = a.shape
    else:
        M, K = a.shape
    if tb:
        N, kb = b.shape
    else:
        kb, N = b.shape
    assert K == kb, (a.shape, b.shape)
    ns = N // shards
    tm, tn, tk = _tile(M, tm), _tile(ns, tn), _tile(K, tk)
    assert M % tm == 0 and ns % tn == 0 and K % tk == 0 and N % shards == 0
    nk = K // tk
    dn = (((0 if ta else 1,), (1 if tb else 0,)), ((), ()))

    def body(a_ref, b_ref, o_ref, acc_ref):
        k = pl.program_id(2)

        @pl.when(k == 0)
        def _():
            acc_ref[...] = jnp.zeros_like(acc_ref)

        acc_ref[...] += lax.dot_general(a_ref[...].astype(BF16), b_ref[...].astype(BF16), dn,
                                        preferred_element_type=F32)

        @pl.when(k == nk - 1)
        def _():
            o_ref[...] = acc_ref[...].astype(o_ref.dtype)

    a_spec = (pl.BlockSpec((tk, tm), lambda i, j, k: (k, i)) if ta
              else pl.BlockSpec((tm, tk), lambda i, j, k: (i, k)))
    b_spec = (pl.BlockSpec((tn, tk), lambda i, j, k: (j, k)) if tb
              else pl.BlockSpec((tk, tn), lambda i, j, k: (k, j)))
    if shards == 1:
        out_shape = jax.ShapeDtypeStruct((M, N), out_dtype)
        o_spec = pl.BlockSpec((tm, tn), lambda i, j, k: (i, j))
    else:
        nb = ns // tn
        out_shape = jax.ShapeDtypeStruct((shards, M, ns), out_dtype)
        o_spec = pl.BlockSpec((None, tm, tn), lambda i, j, k: (j // nb, i, j % nb))
    return pl.pallas_call(
        body, name=name, out_shape=out_shape,
        grid=(M // tm, N // tn, nk),
        in_specs=[a_spec, b_spec], out_specs=o_spec,
        scratch_shapes=[pltpu.VMEM((tm, tn), F32)],
        compiler_params=_params("parallel", "parallel", "arbitrary"),
    )(a, b)


def _rmsnorm_fwd(x, g, *, name, res=None, out_dtype=F32, tr=256):
    L, D = x.shape
    tr = min(tr, L)
    has_res = res is not None

    def body(*refs):
        if has_res:
            x_ref, g_ref, r_ref, o_ref = refs
        else:
            x_ref, g_ref, o_ref = refs
        xf = x_ref[...]
        r = lax.rsqrt(jnp.mean(xf * xf, axis=-1, keepdims=True) + EPS)
        y = xf * r * g_ref[...]
        if has_res:
            y = r_ref[...] + y
        o_ref[...] = y.astype(o_ref.dtype)

    row = pl.BlockSpec((tr, D), lambda i: (i, 0))
    vec = pl.BlockSpec((1, D), lambda i: (0, 0))
    ins = [x, g.reshape(1, D)] + ([res] if has_res else [])
    return pl.pallas_call(
        body, name=name, out_shape=jax.ShapeDtypeStruct((L, D), out_dtype),
        grid=(L // tr,), in_specs=[row, vec] + ([row] if has_res else []), out_specs=row,
        compiler_params=_params("parallel"),
    )(*ins)


def _rmsnorm_bwd(x, g, dy, *, name, adds=(), dx_dtype=F32, tr=256):
    L, D = x.shape
    tr = min(tr, L)
    dys = dy if isinstance(dy, tuple) else (dy,)
    n_dy, n_add = len(dys), len(adds)

    def body(*refs):
        x_ref, g_ref = refs[:2]
        dy_refs = refs[2:2 + n_dy]
        add_refs = refs[2 + n_dy:2 + n_dy + n_add]
        dx_ref, dg_ref = refs[2 + n_dy + n_add:]
        xf = x_ref[...]
        dyf = dy_refs[0][...].astype(F32)
        for d_ref in dy_refs[1:]:
            dyf = dyf + d_ref[...].astype(F32)
        r = lax.rsqrt(jnp.mean(xf * xf, axis=-1, keepdims=True) + EPS)
        gy = dyf * g_ref[...]
        c = jnp.mean(xf * gy, axis=-1, keepdims=True) * (r * r * r)
        dx = gy * r - xf * c
        for a_ref in add_refs:
            dx = dx + a_ref[...].astype(F32)
        dx_ref[...] = dx.astype(dx_ref.dtype)

        @pl.when(pl.program_id(0) == 0)
        def _():
            dg_ref[...] = jnp.zeros_like(dg_ref)

        dg_ref[...] += jnp.sum(dyf * xf * r, axis=0, keepdims=True)

    row = pl.BlockSpec((tr, D), lambda i: (i, 0))
    vec = pl.BlockSpec((1, D), lambda i: (0, 0))
    dx, dg = pl.pallas_call(
        body, name=name,
        out_shape=(jax.ShapeDtypeStruct((L, D), dx_dtype), jax.ShapeDtypeStruct((1, D), F32)),
        grid=(L // tr,), in_specs=[row, vec] + [row] * (n_dy + n_add), out_specs=(row, vec),
        compiler_params=_params("arbitrary"),
    )(x, g.reshape(1, D), *dys, *adds)
    return dx, dg.reshape(D)


def _final_norm_loss(o, g, res, target, *, tr=256):
    L, D = o.shape
    tr = min(tr, L)

    def body(o_ref, g_ref, r_ref, t_ref, dh_ref, loss_ref):
        xf = o_ref[...]
        r = lax.rsqrt(jnp.mean(xf * xf, axis=-1, keepdims=True) + EPS)
        e = (r_ref[...] + xf * r * g_ref[...]) - t_ref[...]
        dh_ref[...] = e * (1.0 / D)

        @pl.when(pl.program_id(0) == 0)
        def _():
            loss_ref[...] = jnp.zeros_like(loss_ref)

        loss_ref[...] += jnp.sum(e * e, axis=0, keepdims=True) * (0.5 / D)

    row = pl.BlockSpec((tr, D), lambda i: (i, 0))
    vec = pl.BlockSpec((1, D), lambda i: (0, 0))
    dh, lp = pl.pallas_call(
        body, name="post_norm_1_loss",
        out_shape=(jax.ShapeDtypeStruct((L, D), F32), jax.ShapeDtypeStruct((1, D), F32)),
        grid=(L // tr,), in_specs=[row, vec, row, row], out_specs=(row, vec),
        compiler_params=_params("arbitrary"),
    )(o, g.reshape(1, D), res, target)
    return dh, lp


def _s5_coeffs(lr, li, ls):
    dt = jnp.exp(ls)
    mag = jnp.exp(lr * dt)
    ar = mag * jnp.cos(li * dt)
    ai = mag * jnp.sin(li * dt)
    den = lr * lr + li * li
    cr = ((ar - 1.0) * lr + ai * li) / den
    ci = (ai * lr - (ar - 1.0) * li) / den
    return dt, ar, ai, den, cr, ci


def _s5_prep(lam_re, lam_im, log_step, b_re_t, b_im_t):
    G, P = lam_re.shape
    H = b_re_t.shape[1]

    def body(lr_ref, li_ref, ls_ref, br_ref, bi_ref, ar_ref, ai_ref, bbr_ref, bbi_ref):
        _, ar, ai, _, cr, ci = _s5_coeffs(lr_ref[...], li_ref[...], ls_ref[...])
        ar_ref[...] = ar
        ai_ref[...] = ai
        br, bi = br_ref[...], bi_ref[...]
        crb, cib = cr[:, None, :], ci[:, None, :]
        bbr_ref[...] = crb * br - cib * bi
        bbi_ref[...] = crb * bi + cib * br

    return pl.pallas_call(
        body, name="s5_prep",
        out_shape=(jax.ShapeDtypeStruct((G, P), F32), jax.ShapeDtypeStruct((G, P), F32),
                   jax.ShapeDtypeStruct((G, H, P), F32), jax.ShapeDtypeStruct((G, H, P), F32)),
        compiler_params=_params(),
    )(lam_re, lam_im, log_step.reshape(G, 1), b_re_t, b_im_t)


def _s5_prep_bwd(lam_re, lam_im, log_step, b_re_t, b_im_t, d_ar, d_ai, d_bbr, d_bbi):
    G, P = lam_re.shape
    H = b_re_t.shape[1]

    def body(lr_ref, li_ref, ls_ref, br_ref, bi_ref, dar_ref, dai_ref, dbbr_ref, dbbi_ref,
             dlr_ref, dli_ref, dls_ref, dbr_ref, dbi_ref):
        lr, li = lr_ref[...], li_ref[...]
        dt, ar, ai, den, cr, ci = _s5_coeffs(lr, li, ls_ref[...])
        br, bi = br_ref[...], bi_ref[...]
        gbr, gbi = dbbr_ref[...], dbbi_ref[...]
        crb, cib = cr[:, None, :], ci[:, None, :]
        dbr_ref[...] = crb * gbr + cib * gbi
        dbi_ref[...] = crb * gbi - cib * gbr
        gcr = jnp.sum(br * gbr + bi * gbi, axis=1)
        gci = jnp.sum(br * gbi - bi * gbr, axis=1)
        ilr, ili = lr / den, -li / den
        gar = dar_ref[...] + (ilr * gcr + ili * gci)
        gai = dai_ref[...] + (ilr * gci - ili * gcr)
        qr, qi = cr * ilr - ci * ili, cr * ili + ci * ilr
        glr = -(qr * gcr + qi * gci)
        gli = -(qr * gci - qi * gcr)
        glr = glr + dt * (ar * gar + ai * gai)
        gli = gli + dt * (ar * gai - ai * gar)
        wr, wi = lr * ar - li * ai, lr * ai + li * ar
        gdt = jnp.sum(wr * gar + wi * gai, axis=1, keepdims=True)
        dlr_ref[...] = glr
        dli_ref[...] = gli
        dls_ref[...] = gdt * dt

    return pl.pallas_call(
        body, name="s5_prep_bwd",
        out_shape=(jax.ShapeDtypeStruct((G, P), F32), jax.ShapeDtypeStruct((G, P), F32),
                   jax.ShapeDtypeStruct((G, 1), F32),
                   jax.ShapeDtypeStruct((G, H, P), F32), jax.ShapeDtypeStruct((G, H, P), F32)),
        compiler_params=_params(),
    )(lam_re, lam_im, log_step.reshape(G, 1), b_re_t, b_im_t, d_ar, d_ai, d_bbr, d_bbi)


def _s5_block_mats(bbr_t, bbi_t, c_re, c_im):
    bmat = _s5_expand(bbr_t, bbi_t)
    cmat = jnp.transpose(_s5_expand(c_re, -c_im), (0, 2, 1))
    return bmat.astype(BF16), cmat.astype(BF16)


def _s5_diag_mask():
    r = lax.broadcasted_iota(jnp.int32, (LANES, 2 * STATE_COLS), 0) // SSM_GROUP
    c = (lax.broadcasted_iota(jnp.int32, (LANES, 2 * STATE_COLS), 1) % STATE_COLS) // SSM_STATE
    return (r == c).astype(F32)


def _s5_expand(re, im):
    re = jnp.tile(re.reshape(SSM_BLOCKS, LANES, SSM_STATE), (1, 1, GROUPS_PER_BLOCK))
    im = jnp.tile(im.reshape(SSM_BLOCKS, LANES, SSM_STATE), (1, 1, GROUPS_PER_BLOCK))
    return jnp.concatenate([re, im], axis=-1) * _s5_diag_mask()[None]


def _s5_block_diag(dmat):
    d = dmat * _s5_diag_mask()[None]
    parts = []
    for ri in range(2):
        acc = 0.0
        for g in range(GROUPS_PER_BLOCK):
            c0 = ri * STATE_COLS + g * SSM_STATE
            acc = acc + d[:, :, c0:c0 + SSM_STATE]
        parts.append(acc.reshape(SSM_GROUPS, SSM_GROUP, SSM_STATE))
    return jnp.stack(parts)


def _s5_a_rows(ar, ai):
    a = jnp.concatenate([ar.reshape(SSM_BLOCKS, STATE_COLS), ai.reshape(SSM_BLOCKS, STATE_COLS)], axis=1)
    return jnp.broadcast_to(a[:, None, :], (SSM_BLOCKS, SUBLANES, 2 * STATE_COLS))


def _s5_fwd(proj, bmat, cmat, a_rows, d_skip, *, tc=512):
    L = proj.shape[0]
    tc = min(tc, L)
    nt = L // tc
    n8 = tc // SUBLANES
    S = STATE_COLS

    def body(u_ref, b_ref, c_ref, a_ref, d_ref, y_ref, yg_ref, xp_ref, bu_s, xp_s, carry_s):
        @pl.when(pl.program_id(1) == 0)
        def _():
            carry_s[...] = jnp.zeros_like(carry_s)

        u = u_ref[...]
        bu = jnp.dot(u.astype(BF16), b_ref[...], preferred_element_type=F32)
        bu_s[...] = bu.reshape(n8, SUBLANES, 2 * S)
        ar, ai = a_ref[0:1, :S], a_ref[0:1, S:]

        def step(i, carry):
            cr, ci = carry
            for j in range(SUBLANES):
                xp_s[i, j:j + 1, :S] = cr
                xp_s[i, j:j + 1, S:] = ci
                br = bu_s[i, j:j + 1, :S]
                bi = bu_s[i, j:j + 1, S:]
                cr, ci = ar * cr - ai * ci + br, ar * ci + ai * cr + bi
            return cr, ci

        cr, ci = lax.fori_loop(0, n8, step, (carry_s[0:1, :S], carry_s[0:1, S:]))
        carry_s[0:1, :S] = cr
        carry_s[0:1, S:] = ci
        xp = xp_s[...].reshape(tc, 2 * S)
        xp_ref[...] = xp
        x_re = ar * xp[:, :S] - ai * xp[:, S:] + bu[:, :S]
        x_im = ar * xp[:, S:] + ai * xp[:, :S] + bu[:, S:]
        xs = jnp.concatenate([x_re, x_im], axis=1).astype(BF16)
        y = jnp.dot(xs, c_ref[...], preferred_element_type=F32) + d_ref[...] * u
        y_ref[...] = y
        yg_ref[...] = _gelu(y).astype(BF16)

    return pl.pallas_call(
        body, name="s5_fwd",
        out_shape=(jax.ShapeDtypeStruct((L, MAIN_WIDTH), F32),
                   jax.ShapeDtypeStruct((L, MAIN_WIDTH), BF16),
                   jax.ShapeDtypeStruct((L, SSM_BLOCKS * 2 * S), F32)),
        grid=(SSM_BLOCKS, nt),
        in_specs=[pl.BlockSpec((tc, LANES), lambda b, t: (t, b)),
                  pl.BlockSpec((None, LANES, 2 * S), lambda b, t: (b, 0, 0)),
                  pl.BlockSpec((None, 2 * S, LANES), lambda b, t: (b, 0, 0)),
                  pl.BlockSpec((None, SUBLANES, 2 * S), lambda b, t: (b, 0, 0)),
                  pl.BlockSpec((1, LANES), lambda b, t: (0, b))],
        out_specs=(pl.BlockSpec((tc, LANES), lambda b, t: (t, b)),
                   pl.BlockSpec((tc, LANES), lambda b, t: (t, b)),
                   pl.BlockSpec((tc, 2 * S), lambda b, t: (t, b))),
        scratch_shapes=[pltpu.VMEM((n8, SUBLANES, 2 * S), F32),
                        pltpu.VMEM((n8, SUBLANES, 2 * S), F32),
                        pltpu.VMEM((SUBLANES, 2 * S), F32)],
        compiler_params=_params("parallel", "arbitrary"),
    )(proj, bmat, cmat, a_rows, d_skip.reshape(1, MAIN_WIDTH))


def _s5_bwd(proj, dyg_a, dyg_b, y, xp, bmat, cmat, a_rows, d_skip, *, tc=512):
    L = proj.shape[0]
    tc = min(tc, L)
    nt = L // tc
    n8 = tc // SUBLANES
    S = STATE_COLS
    nn = (((1,), (1,)), ((), ()))
    tn = (((0,), (0,)), ((), ()))

    def body(u_ref, dyga_ref, dygb_ref, y_ref, xp_ref, b_ref, c_ref, a_ref, d_ref,
             du_ref, db_ref, dc_ref, da_ref, dd_ref, dl_s, carry_s):
        @pl.when(pl.program_id(1) == 0)
        def _():
            carry_s[...] = jnp.zeros_like(carry_s)
            db_ref[...] = jnp.zeros_like(db_ref)
            dc_ref[...] = jnp.zeros_like(dc_ref)
            da_ref[...] = jnp.zeros_like(da_ref)
            dd_ref[...] = jnp.zeros_like(dd_ref)

        u = u_ref[...]
        dy = (dyga_ref[...] + dygb_ref[...]) * _gelu_grad(y_ref[...])
        xp = xp_ref[...]
        ub = u.astype(BF16)
        dyb = dy.astype(BF16)
        ar, ai = a_ref[0:1, :S], a_ref[0:1, S:]
        bu = jnp.dot(ub, b_ref[...], preferred_element_type=F32)
        x_re = ar * xp[:, :S] - ai * xp[:, S:] + bu[:, :S]
        x_im = ar * xp[:, S:] + ai * xp[:, :S] + bu[:, S:]
        xs = jnp.concatenate([x_re, x_im], axis=1).astype(BF16)
        dc_ref[...] += lax.dot_general(dyb, xs, tn, preferred_element_type=F32)
        dx = lax.dot_general(dyb, c_ref[...], nn, preferred_element_type=F32)
        dl_s[...] = dx.reshape(n8, SUBLANES, 2 * S)

        def step(k, carry):
            cr, ci = carry
            i = n8 - 1 - k
            for j in range(SUBLANES - 1, -1, -1):
                lr = dl_s[i, j:j + 1, :S] + (ar * cr + ai * ci)
                li = dl_s[i, j:j + 1, S:] + (ar * ci - ai * cr)
                dl_s[i, j:j + 1, :S] = lr
                dl_s[i, j:j + 1, S:] = li
                cr, ci = lr, li
            return cr, ci

        cr, ci = lax.fori_loop(0, n8, step, (carry_s[0:1, :S], carry_s[0:1, S:]))
        carry_s[0:1, :S] = cr
        carry_s[0:1, S:] = ci
        lam = dl_s[...].reshape(tc, 2 * S)
        l_re, l_im = lam[:, :S], lam[:, S:]
        da_ref[0:1, :S] += jnp.sum(l_re * xp[:, :S] + l_im * xp[:, S:], axis=0, keepdims=True)
        da_ref[0:1, S:] += jnp.sum(l_im * xp[:, :S] - l_re * xp[:, S:], axis=0, keepdims=True)
        lamb = lam.astype(BF16)
        du_ref[...] = lax.dot_general(lamb, b_ref[...], nn, preferred_element_type=F32) + d_ref[...] * dy
        db_ref[...] += lax.dot_general(ub, lamb, tn, preferred_element_type=F32)
        dd_ref[0:1, :] += jnp.sum(dy * u, axis=0, keepdims=True)

    rev = lambda b, t: (nt - 1 - t, b)
    return pl.pallas_call(
        body, name="s5_bwd",
        out_shape=(jax.ShapeDtypeStruct((L, MAIN_WIDTH), F32),
                   jax.ShapeDtypeStruct((SSM_BLOCKS, LANES, 2 * S), F32),
                   jax.ShapeDtypeStruct((SSM_BLOCKS, LANES, 2 * S), F32),
                   jax.ShapeDtypeStruct((SSM_BLOCKS, SUBLANES, 2 * S), F32),
                   jax.ShapeDtypeStruct((SUBLANES, MAIN_WIDTH), F32)),
        grid=(SSM_BLOCKS, nt),
        in_specs=[pl.BlockSpec((tc, LANES), rev),
                  pl.BlockSpec((tc, LANES), rev),
                  pl.BlockSpec((tc, LANES), rev),
                  pl.BlockSpec((tc, LANES), rev),
                  pl.BlockSpec((tc, 2 * S), rev),
                  pl.BlockSpec((None, LANES, 2 * S), lambda b, t: (b, 0, 0)),
                  pl.BlockSpec((None, 2 * S, LANES), lambda b, t: (b, 0, 0)),
                  pl.BlockSpec((None, SUBLANES, 2 * S), lambda b, t: (b, 0, 0)),
                  pl.BlockSpec((1, LANES), lambda b, t: (0, b))],
        out_specs=(pl.BlockSpec((tc, LANES), rev),
                   pl.BlockSpec((None, LANES, 2 * S), lambda b, t: (b, 0, 0)),
                   pl.BlockSpec((None, LANES, 2 * S), lambda b, t: (b, 0, 0)),
                   pl.BlockSpec((None, SUBLANES, 2 * S), lambda b, t: (b, 0, 0)),
                   pl.BlockSpec((SUBLANES, LANES), lambda b, t: (0, b))),
        scratch_shapes=[pltpu.VMEM((n8, SUBLANES, 2 * S), F32),
                        pltpu.VMEM((SUBLANES, 2 * S), F32)],
        compiler_params=_params("parallel", "arbitrary"),
    )(proj, dyg_a, dyg_b, y, xp, bmat, cmat, a_rows, d_skip.reshape(1, MAIN_WIDTH))


def _row_specs(tr):
    main = pl.BlockSpec((tr, MAIN_WIDTH), lambda i: (i, 0))
    z = pl.BlockSpec((tr, MAIN_WIDTH), lambda i: (i, 1))
    zm = pl.BlockSpec((tr, MEM_WIDTH), lambda i: (i, IN_WIDTH // MEM_WIDTH - 1))
    mem = pl.BlockSpec((tr, MEM_WIDTH), lambda i: (i, 0))
    cat = pl.BlockSpec((tr, D_MODEL), lambda i: (i, 0))
    vec = pl.BlockSpec((1, MAIN_WIDTH), lambda i: (0, 0))
    return main, z, zm, mem, cat, vec


def _gate_a_fwd(y, t, b_glu, proj, o_mem, *, tr=256):
    L = y.shape[0]
    tr = min(tr, L)

    def body(y_ref, t_ref, b_ref, z_ref, zm_ref, om_ref, o_ref):
        yg = _gelu(y_ref[...])
        sz, _ = _silu_and_grad(z_ref[...])
        o_ref[:, :MAIN_WIDTH] = (yg * _sigmoid(t_ref[...] + b_ref[...]) * sz).astype(BF16)
        szm, _ = _silu_and_grad(zm_ref[...])
        o_ref[:, MAIN_WIDTH:] = (om_ref[...] * szm).astype(BF16)

    main, z, zm, mem, cat, vec = _row_specs(tr)
    return pl.pallas_call(
        body, name="gate_a_fwd", out_shape=jax.ShapeDtypeStruct((L, D_MODEL), BF16),
        grid=(L // tr,), in_specs=[main, main, vec, z, zm, mem], out_specs=cat,
        compiler_params=_params("parallel"),
    )(y, t, b_glu.reshape(1, MAIN_WIDTH), proj, proj, o_mem)


def _gate_a_bwd(dcat, y, t, b_glu, proj, o_mem, *, tr=256):
    L = y.shape[0]
    tr = min(tr, L)

    def body(dc_ref, y_ref, t_ref, b_ref, z_ref, zm_ref, om_ref,
             dz_ref, dzm_ref, dt_ref, dyg_ref, dom_ref, db_ref):
        dmain = dc_ref[:, :MAIN_WIDTH]
        dmemo = dc_ref[:, MAIN_WIDTH:]
        yg = _gelu(y_ref[...])
        sg = _sigmoid(t_ref[...] + b_ref[...])
        sz, gz = _silu_and_grad(z_ref[...])
        dz_ref[...] = (dmain * (yg * sg) * gz).astype(BF16)
        dy2 = dmain * sz
        dyg_ref[...] = dy2 * sg
        dt = dy2 * yg * (sg * (1.0 - sg))
        dt_ref[...] = dt.astype(BF16)

        @pl.when(pl.program_id(0) == 0)
        def _():
            db_ref[...] = jnp.zeros_like(db_ref)

        db_ref[...] += jnp.sum(dt, axis=0, keepdims=True)
        szm, gzm = _silu_and_grad(zm_ref[...])
        dom_ref[...] = dmemo * szm
        dzm_ref[...] = (dmemo * om_ref[...] * gzm).astype(BF16)

    main, z, zm, mem, cat, vec = _row_specs(tr)
    outs = pl.pallas_call(
        body, name="gate_a_bwd",
        out_shape=(jax.ShapeDtypeStruct((L, MAIN_WIDTH), BF16), jax.ShapeDtypeStruct((L, MEM_WIDTH), BF16),
                   jax.ShapeDtypeStruct((L, MAIN_WIDTH), BF16), jax.ShapeDtypeStruct((L, MAIN_WIDTH), F32),
                   jax.ShapeDtypeStruct((L, MEM_WIDTH), F32), jax.ShapeDtypeStruct((1, MAIN_WIDTH), F32)),
        grid=(L // tr,), in_specs=[cat, main, main, vec, z, zm, mem],
        out_specs=(main, mem, main, main, mem, vec),
        compiler_params=_params("arbitrary"),
    )(dcat, y, t, b_glu.reshape(1, MAIN_WIDTH), proj, proj, o_mem)
    return outs


def _gate_b_fwd(att, proj, o_mem, *, tr=256):
    L = att.shape[0]
    tr = min(tr, L)

    def body(a_ref, z_ref, zm_ref, om_ref, o_ref):
        sz, _ = _silu_and_grad(z_ref[...])
        o_ref[:, :MAIN_WIDTH] = (a_ref[...] * sz).astype(BF16)
        szm, _ = _silu_and_grad(zm_ref[...])
        o_ref[:, MAIN_WIDTH:] = (om_ref[...] * szm).astype(BF16)

    main, z, zm, mem, cat, _ = _row_specs(tr)
    return pl.pallas_call(
        body, name="gate_b_fwd", out_shape=jax.ShapeDtypeStruct((L, D_MODEL), BF16),
        grid=(L // tr,), in_specs=[main, z, zm, mem], out_specs=cat,
        compiler_params=_params("parallel"),
    )(att, proj, proj, o_mem)


def _gate_b_bwd(dcat, att, proj, o_mem, *, tr=256):
    L = att.shape[0]
    tr = min(tr, L)

    def body(dc_ref, a_ref, z_ref, zm_ref, om_ref, da_ref, dz_ref, dom_ref, dzm_ref, dl_ref):
        dmain = dc_ref[:, :MAIN_WIDTH]
        dmemo = dc_ref[:, MAIN_WIDTH:]
        att = a_ref[...]
        sz, gz = _silu_and_grad(z_ref[...])
        datt = dmain * sz
        da_ref[...] = datt
        dz_ref[...] = (dmain * att * gz).astype(BF16)
        szm, gzm = _silu_and_grad(zm_ref[...])
        dom_ref[...] = dmemo * szm
        dzm_ref[...] = (dmemo * om_ref[...] * gzm).astype(BF16)
        prod = datt * att
        for h in range(FOX_HEADS):
            dl_ref[h] = jnp.sum(prod[:, h * HEAD_DIM:(h + 1) * HEAD_DIM], axis=1, keepdims=True)

    main, z, zm, mem, cat, _ = _row_specs(tr)
    delta = pl.BlockSpec((FOX_HEADS, tr, 1), lambda i: (0, i, 0))
    return pl.pallas_call(
        body, name="gate_b_bwd",
        out_shape=(jax.ShapeDtypeStruct((L, MAIN_WIDTH), F32), jax.ShapeDtypeStruct((L, MAIN_WIDTH), BF16),
                   jax.ShapeDtypeStruct((L, MEM_WIDTH), F32), jax.ShapeDtypeStruct((L, MEM_WIDTH), BF16),
                   jax.ShapeDtypeStruct((FOX_HEADS, L, 1), F32)),
        grid=(L // tr,), in_specs=[cat, main, z, zm, mem], out_specs=(main, main, mem, mem, delta),
        compiler_params=_params("parallel"),
    )(dcat, att, proj, proj, o_mem)


_MEM_Q_COL = (2 * MAIN_WIDTH) // HEAD_DIM
_NT = (((1,), (1,)), ((), ()))
_TN = (((0,), (0,)), ((), ()))


def _mem_probs(q_ref, k_ref):
    qs = (q_ref[...] * (HEAD_DIM ** -0.5)).astype(BF16)
    s = lax.dot_general(qs, k_ref[...].astype(BF16), _NT, preferred_element_type=F32)
    e = jnp.exp(s - jnp.max(s, axis=-1, keepdims=True))
    return qs, e / jnp.sum(e, axis=-1, keepdims=True)


def _mem_attn_fwd(proj, kvm, *, tq=2048):
    L = proj.shape[0]
    tq = min(tq, L)

    def body(q_ref, k_ref, v_ref, o_ref):
        _, p = _mem_probs(q_ref, k_ref)
        o_ref[...] = jnp.dot(p.astype(BF16), v_ref[...].astype(BF16), preferred_element_type=F32)

    return pl.pallas_call(
        body, name="mem_attn_fwd", out_shape=jax.ShapeDtypeStruct((L, MEM_WIDTH), F32),
        grid=(MEM_HEADS, L // tq),
        in_specs=[pl.BlockSpec((tq, HEAD_DIM), lambda h, i: (i, _MEM_Q_COL + h)),
                  pl.BlockSpec((N_MEM, HEAD_DIM), lambda h, i: (0, h)),
                  pl.BlockSpec((N_MEM, HEAD_DIM), lambda h, i: (0, MEM_HEADS + h))],
        out_specs=pl.BlockSpec((tq, HEAD_DIM), lambda h, i: (i, h)),
        compiler_params=_params("parallel", "parallel"),
    )(proj, kvm, kvm)


def _mem_attn_bwd(proj, kvm, do, *, tq=2048):
    L = proj.shape[0]
    tq = min(tq, L)

    def body(q_ref, k_ref, v_ref, do_ref, dq_ref, dk_ref, dv_ref):
        @pl.when(pl.program_id(1) == 0)
        def _():
            dk_ref[...] = jnp.zeros_like(dk_ref)
            dv_ref[...] = jnp.zeros_like(dv_ref)

        qs, p = _mem_probs(q_ref, k_ref)
        dob = do_ref[...].astype(BF16)
        dp = lax.dot_general(dob, v_ref[...].astype(BF16), _NT, preferred_element_type=F32)
        ds = p * (dp - jnp.sum(p * dp, axis=-1, keepdims=True))
        dsb = ds.astype(BF16)
        dq = jnp.dot(dsb, k_ref[...].astype(BF16), preferred_element_type=F32) * (HEAD_DIM ** -0.5)
        dq_ref[...] = dq.astype(BF16)
        dk_ref[...] += lax.dot_general(dsb, qs, _TN, preferred_element_type=F32)
        dv_ref[...] += lax.dot_general(p.astype(BF16), dob, _TN, preferred_element_type=F32)

    dq, dk, dv = pl.pallas_call(
        body, name="mem_attn_bwd",
        out_shape=(jax.ShapeDtypeStruct((L, MEM_WIDTH), BF16),
                   jax.ShapeDtypeStruct((N_MEM, MEM_WIDTH), F32),
                   jax.ShapeDtypeStruct((N_MEM, MEM_WIDTH), F32)),
        grid=(MEM_HEADS, L // tq),
        in_specs=[pl.BlockSpec((tq, HEAD_DIM), lambda h, i: (i, _MEM_Q_COL + h)),
                  pl.BlockSpec((N_MEM, HEAD_DIM), lambda h, i: (0, h)),
                  pl.BlockSpec((N_MEM, HEAD_DIM), lambda h, i: (0, MEM_HEADS + h)),
                  pl.BlockSpec((tq, HEAD_DIM), lambda h, i: (i, h))],
        out_specs=(pl.BlockSpec((tq, HEAD_DIM), lambda h, i: (i, h)),
                   pl.BlockSpec((N_MEM, HEAD_DIM), lambda h, i: (0, h)),
                   pl.BlockSpec((N_MEM, HEAD_DIM), lambda h, i: (0, h))),
        compiler_params=_params("parallel", "arbitrary"),
    )(proj, kvm, kvm, do)
    return dq, jnp.concatenate([dk, dv], axis=1)


def _tile_cumsum(x, row, reverse):
    for sh in (1, 2, 4):
        if reverse:
            x = x + jnp.where(row < SUBLANES - sh, pltpu.roll(x, SUBLANES - sh, 0), 0.0)
        else:
            x = x + jnp.where(row >= sh, pltpu.roll(x, sh, 0), 0.0)
    return x


def _fgate_fwd(pre, b_pad):
    L = pre.shape[0]
    n8 = L // SUBLANES

    def body(p_ref, b_ref, o_ref):
        row = lax.broadcasted_iota(jnp.int32, (SUBLANES, LANES), 0)
        b = b_ref[...]

        def step(i, carry):
            x = p_ref[i] + b
            logf = jnp.minimum(x, 0.0) - jnp.log(1.0 + jnp.exp(-jnp.abs(x)))
            t = _tile_cumsum(logf, row, False) + carry
            o_ref[i] = t
            return t[SUBLANES - 1:SUBLANES, :]

        lax.fori_loop(0, n8, step, jnp.zeros((1, LANES), F32))

    out = pl.pallas_call(
        body, name="fgate_fwd", out_shape=jax.ShapeDtypeStruct((n8, SUBLANES, LANES), F32),
        compiler_params=_params(),
    )(pre.reshape(n8, SUBLANES, LANES), b_pad.reshape(1, LANES))
    return out.reshape(L, LANES)


def _fgate_bwd(dfcum, pre, b_pad):
    L = pre.shape[0]
    n8 = L // SUBLANES

    def body(d_ref, p_ref, b_ref, o_ref, s_ref):
        row = lax.broadcasted_iota(jnp.int32, (SUBLANES, LANES), 0)
        b = b_ref[...]

        def step(k, carry):
            c, acc = carry
            i = n8 - 1 - k
            t = _tile_cumsum(d_ref[i], row, True) + c
            dpre = t * _sigmoid(-(p_ref[i] + b))
            o_ref[i] = dpre
            return t[0:1, :], acc + dpre

        _, acc = lax.fori_loop(0, n8, step, (jnp.zeros((1, LANES), F32), jnp.zeros((SUBLANES, LANES), F32)))
        s_ref[...] = jnp.sum(acc, axis=0, keepdims=True)

    dpre, db = pl.pallas_call(
        body, name="fgate_bwd",
        out_shape=(jax.ShapeDtypeStruct((n8, SUBLANES, LANES), F32), jax.ShapeDtypeStruct((1, LANES), F32)),
        compiler_params=_params(),
    )(dfcum.reshape(n8, SUBLANES, LANES), pre.reshape(n8, SUBLANES, LANES), b_pad.reshape(1, LANES))
    return dpre.reshape(L, LANES), db


FOX_BLOCK = 512


def _fox_scores(qs, k, fk, diagonal):
    s = lax.dot_general(qs, k, _NT, preferred_element_type=F32) - fk
    if diagonal:
        row = lax.broadcasted_iota(jnp.int32, s.shape, 0)
        col = lax.broadcasted_iota(jnp.int32, s.shape, 1)
        s = jnp.where(row >= col, s, NEG_BIG)
    return s


def _fox_specs(tq, L):
    nq = L // tq
    return dict(
        rows=lambda off: pl.BlockSpec((tq, HEAD_DIM), lambda h, i: (i, off + h)),
        seq=lambda off: pl.BlockSpec((L, HEAD_DIM), lambda h, i: (0, off + h)),
        col=pl.BlockSpec((None, None, tq, 1), lambda h, i: (h, i, 0, 0)),
        col_all=pl.BlockSpec((None, nq, tq, 1), lambda h, i: (h, 0, 0, 0)),
        row=pl.BlockSpec((None, None, 1, tq), lambda h, i: (h, i, 0, 0)),
        row_all=pl.BlockSpec((None, nq, 1, tq), lambda h, i: (h, 0, 0, 0)))


def _fox_fwd(proj, kv, fk):
    L = proj.shape[0]
    tq = min(FOX_BLOCK, L)
    nq = L // tq
    sp = _fox_specs(tq, L)

    def body(q_ref, k_ref, v_ref, fk_ref, o_ref, lse_ref, m_s, l_s, acc_s):
        qi = pl.program_id(1)
        qs = (q_ref[...] * (HEAD_DIM ** -0.5)).astype(BF16)
        m_s[...] = jnp.full_like(m_s, NEG_BIG)
        l_s[...] = jnp.zeros_like(l_s)
        acc_s[...] = jnp.zeros_like(acc_s)

        def block(j, diagonal):
            r0 = pl.multiple_of(j * tq, tq)
            s = _fox_scores(qs, k_ref[pl.ds(r0, tq), :], fk_ref[j], diagonal)
            m_new = jnp.maximum(m_s[...], jnp.max(s, axis=-1, keepdims=True))
            alpha = jnp.exp(m_s[...] - m_new)
            p = jnp.exp(s - m_new)
            l_s[...] = alpha * l_s[...] + jnp.sum(p, axis=-1, keepdims=True)
            acc_s[...] = alpha * acc_s[...] + jnp.dot(p.astype(BF16), v_ref[pl.ds(r0, tq), :],
                                                      preferred_element_type=F32)
            m_s[...] = m_new

        def below(j, carry):
            block(j, False)
            return carry

        lax.fori_loop(0, qi, below, 0)
        block(qi, True)
        o_ref[...] = acc_s[...] / l_s[...]
        lse_ref[...] = m_s[...] + jnp.log(l_s[...])

    return pl.pallas_call(
        body, name="fox_fwd",
        out_shape=(jax.ShapeDtypeStruct((L, MAIN_WIDTH), F32),
                   jax.ShapeDtypeStruct((FOX_HEADS, nq, tq, 1), F32)),
        grid=(FOX_HEADS, nq),
        in_specs=[sp["rows"](0), sp["seq"](0), sp["seq"](FOX_HEADS), sp["row_all"]],
        out_specs=(sp["rows"](0), sp["col"]),
        scratch_shapes=[pltpu.VMEM((tq, 1), F32), pltpu.VMEM((tq, 1), F32), pltpu.VMEM((tq, HEAD_DIM), F32)],
        compiler_params=_params("parallel", "parallel"),
    )(proj, kv, kv, fk)


def _fox_bwd_dq(proj, kv, fk, lse, delta, datt):
    L = proj.shape[0]
    tq = min(FOX_BLOCK, L)
    nq = L // tq
    sp = _fox_specs(tq, L)

    def body(q_ref, k_ref, v_ref, fk_ref, lse_ref, dl_ref, do_ref, dq_ref, df_ref, acc_s, df_s):
        qi = pl.program_id(1)
        qs = (q_ref[...] * (HEAD_DIM ** -0.5)).astype(BF16)
        dob = do_ref[...].astype(BF16)
        lse, dl = lse_ref[...], dl_ref[...]
        acc_s[...] = jnp.zeros_like(acc_s)
        df_s[...] = jnp.zeros_like(df_s)

        def block(j, diagonal):
            r0 = pl.multiple_of(j * tq, tq)
            k = k_ref[pl.ds(r0, tq), :]
            p = jnp.exp(_fox_scores(qs, k, fk_ref[j], diagonal) - lse)
            dp = lax.dot_general(dob, v_ref[pl.ds(r0, tq), :], _NT, preferred_element_type=F32)
            ds = p * (dp - dl)
            acc_s[...] += jnp.dot(ds.astype(BF16), k, preferred_element_type=F32)
            df_s[...] += jnp.sum(ds, axis=1, keepdims=True)

        def below(j, carry):
            block(j, False)
            return carry

        lax.fori_loop(0, qi, below, 0)
        block(qi, True)
        dq_ref[...] = (acc_s[...] * (HEAD_DIM ** -0.5)).astype(BF16)
        df_ref[...] = df_s[...]

    return pl.pallas_call(
        body, name="fox_bwd_dq",
        out_shape=(jax.ShapeDtypeStruct((L, MAIN_WIDTH), BF16),
                   jax.ShapeDtypeStruct((FOX_HEADS, nq, tq, 1), F32)),
        grid=(FOX_HEADS, nq),
        in_specs=[sp["rows"](0), sp["seq"](0), sp["seq"](FOX_HEADS), sp["row_all"],
                  sp["col"], sp["col"], sp["rows"](0)],
        out_specs=(sp["rows"](0), sp["col"]),
        scratch_shapes=[pltpu.VMEM((tq, HEAD_DIM), F32), pltpu.VMEM((tq, 1), F32)],
        compiler_params=_params("parallel", "parallel"),
    )(proj, kv, kv, fk, lse, delta, datt)


def _fox_bwd_dkv(proj, kv, fk, lse, delta, datt):
    L = proj.shape[0]
    tq = min(FOX_BLOCK, L)
    nq = L // tq
    sp = _fox_specs(tq, L)

    def body(q_ref, k_ref, v_ref, fk_ref, lse_ref, dl_ref, do_ref,
             dk_ref, dv_ref, df_ref, dk_s, dv_s, df_s):
        ki = pl.program_id(1)
        k, v, fk = k_ref[...], v_ref[...], fk_ref[...]
        dk_s[...] = jnp.zeros_like(dk_s)
        dv_s[...] = jnp.zeros_like(dv_s)
        df_s[...] = jnp.zeros_like(df_s)

        def block(i, diagonal):
            r0 = pl.multiple_of(i * tq, tq)
            qs = (q_ref[pl.ds(r0, tq), :] * (HEAD_DIM ** -0.5)).astype(BF16)
            dob = do_ref[pl.ds(r0, tq), :].astype(BF16)
            p = jnp.exp(_fox_scores(qs, k, fk, diagonal) - lse_ref[i])
            dp = lax.dot_general(dob, v, _NT, preferred_element_type=F32)
            ds = p * (dp - dl_ref[i])
            dv_s[...] += lax.dot_general(p.astype(BF16), dob, _TN, preferred_element_type=F32)
            dk_s[...] += lax.dot_general(ds.astype(BF16), qs, _TN, preferred_element_type=F32)
            df_s[...] -= jnp.sum(ds, axis=0, keepdims=True)

        def above(i, carry):
            block(i, False)
            return carry

        block(ki, True)
        lax.fori_loop(ki + 1, nq, above, 0)
        dk_ref[...] = dk_s[...].astype(BF16)
        dv_ref[...] = dv_s[...].astype(BF16)
        df_ref[...] = df_s[...]

    return pl.pallas_call(
        body, name="fox_bwd_dkv",
        out_shape=(jax.ShapeDtypeStruct((L, MAIN_WIDTH), BF16),
                   jax.ShapeDtypeStruct((L, MAIN_WIDTH), BF16),
                   jax.ShapeDtypeStruct((FOX_HEADS, nq, 1, tq), F32)),
        grid=(FOX_HEADS, nq),
        in_specs=[sp["seq"](0), sp["rows"](0), sp["rows"](FOX_HEADS), sp["row"],
                  sp["col_all"], sp["col_all"], sp["seq"](0)],
        out_specs=(sp["rows"](0), sp["rows"](0), sp["row"]),
        scratch_shapes=[pltpu.VMEM((tq, HEAD_DIM), F32), pltpu.VMEM((tq, HEAD_DIM), F32),
                        pltpu.VMEM((1, tq), F32)],
        compiler_params=_params("parallel", "parallel"),
    )(proj, kv, kv, fk, lse, delta, datt)


def _pad_lanes(a):
    return jnp.pad(a, ((0, 0), (0, LANES - a.shape[1])))


def _mem_branch_fwd(mem, g, w_mk, proj, tag):
    memn = _rmsnorm_fwd(mem, g, name="mem_norm_" + tag, out_dtype=BF16)
    kvm = _mm(memn, w_mk, name="mem_kv_" + tag)
    return memn, kvm, _mem_attn_fwd(proj, kvm)


def _mem_branch_bwd(mem, g, w_mk, proj, memn, kvm, do_mem, tag):
    dqm, dkvm = _mem_attn_bwd(proj, kvm, do_mem)
    dkvm = dkvm.astype(BF16)
    dw_mk = _mm(memn, dkvm, ta=True, name="dw_mem_kv_" + tag, out_dtype=BF16)
    dmemn = _mm(dkvm, w_mk, tb=True, name="dmemn_" + tag)
    _, dg = _rmsnorm_bwd(mem, g, dmemn, name="mem_norm_bwd_" + tag, dx_dtype=BF16)
    return dqm, dw_mk, dg


def _local_step(x, mem, target, w, fetch=None, grads_ready=None):
    if grads_ready is None:
        grads_ready = lambda group, grads, token: token
    L = x.shape[0]
    g = {}
    w = dict(w)

    b_re_t = jnp.transpose(w["b_re"], (0, 2, 1))
    b_im_t = jnp.transpose(w["b_im"], (0, 2, 1))
    ar, ai, bbr_t, bbi_t = _s5_prep(w["lam_re"], w["lam_im"], w["log_step"], b_re_t, b_im_t)
    bmat, cmat = _s5_block_mats(bbr_t, bbi_t, w["c_re"], w["c_im"])
    a_rows = _s5_a_rows(ar, ai)

    hn0 = _rmsnorm_fwd(x, w["pre_norm_g"][0], name="pre_norm_0", out_dtype=BF16)
    if fetch is not None:
        w.update(fetch("a", hn0))
    proj_a = _mm(hn0, w["w_in_a"], name="in_proj_a")
    y, yg, xp = _s5_fwd(proj_a, bmat, cmat, a_rows, w["d_skip"])
    if fetch is not None:
        w.update(fetch("b", yg))
    t = _mm(yg, w["w_glu"], name="glu_proj")
    memn0, kvm0, om0 = _mem_branch_fwd(mem, w["mem_norm_g"][0], w["w_mem_kv"][0], proj_a, "0")
    cat0 = _gate_a_fwd(y, t, w["b_glu"], proj_a, om0)
    o0 = _mm(cat0, w["w_out"][0], name="out_proj_0")
    h1 = _rmsnorm_fwd(o0, w["post_norm_g"][0], res=x, name="post_norm_0")

    kv_in = _rmsnorm_fwd(h1, w["kv_norm_g"], name="kv_norm", out_dtype=BF16)
    if fetch is not None:
        w.update(fetch("c", kv_in))
    kv = _mm(kv_in, w["w_kv"], name="kv_proj", out_dtype=BF16)
    pre_f = _mm(kv_in, w["w_fgate"], name="fgate_proj")
    b_f = jnp.pad(w["b_fgate"], (0, LANES - FOX_HEADS))
    fcum = _fgate_fwd(pre_f, b_f)
    fc = jnp.transpose(fcum[:, :FOX_HEADS])
    tq = min(FOX_BLOCK, L)
    fk = fc.reshape(FOX_HEADS, L // tq, 1, tq)

    hn1 = _rmsnorm_fwd(h1, w["pre_norm_g"][1], name="pre_norm_1", out_dtype=BF16)
    proj_b = _mm(hn1, w["w_in_b"], name="in_proj_b")
    att, lse = _fox_fwd(proj_b, kv, fk)
    memn1, kvm1, om1 = _mem_branch_fwd(mem, w["mem_norm_g"][1], w["w_mem_kv"][1], proj_b, "1")
    cat1 = _gate_b_fwd(att, proj_b, om1)
    o1 = _mm(cat1, w["w_out"][1], name="out_proj_1")
    dh2, loss_row = _final_norm_loss(o1, w["post_norm_g"][1], h1, target)

    do1, dpost1 = _rmsnorm_bwd(o1, w["post_norm_g"][1], dh2, name="post_norm_bwd_1", dx_dtype=BF16)
    dcat1 = _mm(do1, w["w_out"][1], tb=True, name="dcat_1")
    g["w_out_1"] = _mm(cat1, do1, ta=True, name="dw_out_1", out_dtype=BF16)
    datt, dz1, dom1, dzm1, delta = _gate_b_bwd(dcat1, att, proj_b, om1)
    dqm1, g["w_mem_kv_1"], dmemg1 = _mem_branch_bwd(mem, w["mem_norm_g"][1], w["w_mem_kv"][1], proj_b,
                                                   memn1, kvm1, dom1, "1")
    delta = delta.reshape(lse.shape)
    dq, dfq = _fox_bwd_dq(proj_b, kv, fk, lse, delta, datt)
    dk, dv, dfk = _fox_bwd_dkv(proj_b, kv, fk, lse, delta, datt)
    dproj_b = jnp.concatenate([dq, dz1, dqm1, dzm1], axis=1)
    g["w_in_b"] = _mm(hn1, dproj_b, ta=True, name="dw_in_b", out_dtype=BF16, shards=N_CHIPS)
    dhn1 = _mm(dproj_b, w["w_in_b"], tb=True, name="dhn_1")

    dkv = jnp.concatenate([dk, dv], axis=1)
    g["w_kv"] = _mm(kv_in, dkv, ta=True, name="dw_kv", out_dtype=BF16, shards=N_CHIPS)
    dkv_in_a = _mm(dkv, w["w_kv"], tb=True, name="dkv_in_kv")
    dfcum = _pad_lanes(jnp.transpose(dfq.reshape(FOX_HEADS, L) + dfk.reshape(FOX_HEADS, L)))
    dpre_f, db_f = _fgate_bwd(dfcum, pre_f, b_f)
    g["b_fgate"] = db_f[0, :FOX_HEADS]
    g["w_fgate"] = _mm(kv_in, dpre_f, ta=True, name="dw_fgate")[:, :FOX_HEADS]
    dkv_in_b = _mm(dpre_f, w["w_fgate"], tb=True, name="dkv_in_fgate")
    dh1_kv, g["kv_norm_g"] = _rmsnorm_bwd(h1, w["kv_norm_g"], (dkv_in_a, dkv_in_b), name="kv_norm_bwd")
    dh1, dpre1 = _rmsnorm_bwd(h1, w["pre_norm_g"][1], dhn1, adds=(dh2, dh1_kv), name="pre_norm_bwd_1")
    dh1 = grads_ready("b", g, dh1)

    do0, dpost0 = _rmsnorm_bwd(o0, w["post_norm_g"][0], dh1, name="post_norm_bwd_0", dx_dtype=BF16)
    dcat0 = _mm(do0, w["w_out"][0], tb=True, name="dcat_0")
    g["w_out_0"] = _mm(cat0, do0, ta=True, name="dw_out_0", out_dtype=BF16)
    dz0, dzm0, dt, dyg_a, dom0, db_glu = _gate_a_bwd(dcat0, y, t, w["b_glu"], proj_a, om0)
    g["b_glu"] = db_glu[0]
    g["w_glu"] = _mm(yg, dt, ta=True, name="dw_glu", out_dtype=BF16)
    dyg_b = _mm(dt, w["w_glu"], tb=True, name="dyg")
    dqm0, g["w_mem_kv_0"], dmemg0 = _mem_branch_bwd(mem, w["mem_norm_g"][0], w["w_mem_kv"][0], proj_a,
                                                   memn0, kvm0, dom0, "0")
    dyg_b = grads_ready("a1", g, dyg_b)
    du, db_blk, dc_blk, da_rows, dd_skip = _s5_bwd(proj_a, dyg_a, dyg_b, y, xp, bmat, cmat, a_rows, w["d_skip"])
    g["d_skip"] = dd_skip[0]
    dproj_a = jnp.concatenate([du.astype(BF16), dz0, dqm0, dzm0], axis=1)
    g["w_in_a"] = _mm(hn0, dproj_a, ta=True, name="dw_in_a", out_dtype=BF16, shards=N_CHIPS)
    dproj_a = grads_ready("a2", g, dproj_a)
    dhn0 = _mm(dproj_a, w["w_in_a"], tb=True, name="dhn_0")
    grad_x, dpre0 = _rmsnorm_bwd(x, w["pre_norm_g"][0], dhn0, adds=(dh1,), name="pre_norm_bwd_0")

    dbb = _s5_block_diag(db_blk)
    dcc = _s5_block_diag(dc_blk)
    g["c_re"], g["c_im"] = dcc[0], -dcc[1]
    d_ar = da_rows[:, 0, :STATE_COLS].reshape(SSM_GROUPS, SSM_STATE)
    d_ai = da_rows[:, 0, STATE_COLS:].reshape(SSM_GROUPS, SSM_STATE)
    dlr, dli, dls, dbr_t, dbi_t = _s5_prep_bwd(w["lam_re"], w["lam_im"], w["log_step"], b_re_t, b_im_t,
                                               d_ar, d_ai, dbb[0], dbb[1])
    g["lam_re"], g["lam_im"], g["log_step"] = dlr, dli, dls[:, 0]
    g["b_re"] = jnp.transpose(dbr_t, (0, 2, 1))
    g["b_im"] = jnp.transpose(dbi_t, (0, 2, 1))
    g["pre_norm_g"] = jnp.stack([dpre0, dpre1])
    g["post_norm_g"] = jnp.stack([dpost0, dpost1])
    g["mem_norm_g"] = jnp.stack([dmemg0, dmemg1])
    return loss_row, grad_x, g


_MESH = pl.DeviceIdType.MESH
_ANY = pl.BlockSpec(memory_space=pl.ANY)


def _place():
    x, y, c = lax.axis_index("x"), lax.axis_index("y"), lax.axis_index("c")
    chips = [(1 - x, y), (x, 1 - y), (1 - x, 1 - y)]
    return x, y, c, chips


_HBM = pl.BlockSpec(memory_space=pltpu.HBM)
_SEM = pl.BlockSpec(memory_space=pltpu.SEMAPHORE)
_SIDE = pltpu.SideEffectType.DATAFLOW_SIDE_EFFECTING


def _in_hbm(a):
    return pltpu.with_memory_space_constraint(a, pltpu.HBM)


def _hbm_like(a):
    return pltpu.HBM(a.shape, a.dtype)


def _ici_copies(srcs, lands, send_sem, recv_sem, src_at, dst_at, wait_at):
    x, y, c, chips = _place()
    start, wait = [], []
    for i in range(len(srcs)):
        for k, (cx, cy) in enumerate(chips):
            sem = dict(send_sem=send_sem.at[3 * i + k], recv_sem=recv_sem.at[3 * i + k],
                       device_id=(cx, cy, c), device_id_type=_MESH)
            src = src_at(srcs[i], 2 * cx + cy, c)
            start.append(pltpu.make_async_remote_copy(src_ref=src, dst_ref=dst_at(lands[i], 2 * x + y, k, c), **sem))
            wait.append(pltpu.make_async_remote_copy(src_ref=src, dst_ref=wait_at(lands[i], 2 * cx + cy, k, c), **sem))
    return start, wait


_BLOCK_ROUTE = (lambda s, j, c: s, lambda l, me, k, c: l.at[me, c], lambda l, j, k, c: l.at[j, c])


def _ici_start(srcs, lands, token, route, *, name):
    n = len(srcs)

    def body(*refs):
        start, _ = _ici_copies(refs[:n], refs[n:2 * n], refs[2 * n + 1], refs[2 * n + 2], *route)
        for cp in start:
            cp.start()

    sems = pltpu.SemaphoreType.DMA((3 * n,))
    outs = pl.pallas_call(
        body, name=name,
        out_shape=(sems, sems, *[_hbm_like(a) for a in srcs], *[_hbm_like(a) for a in lands], _hbm_like(token)),
        in_specs=[_HBM] * (2 * n + 1), out_specs=(_SEM, _SEM, *[_HBM] * (2 * n + 1)),
        input_output_aliases={i: 2 + i for i in range(2 * n + 1)},
        compiler_params=pltpu.CompilerParams(has_side_effects=_SIDE),
    )(*[_in_hbm(a) for a in srcs], *[_in_hbm(a) for a in lands], _in_hbm(token))
    return (outs[0], outs[1], list(outs[2:2 + n]), list(outs[2 + n:2 + 2 * n])), outs[2 + 2 * n]


def _ici_wait(handle, after, route, *, name):
    send_sem, recv_sem, srcs, lands = handle
    n = len(srcs)

    def body(*refs):
        _, wait = _ici_copies(refs[:n], refs[n:2 * n], refs[2 * n], refs[2 * n + 1], *route)
        for cp in wait:
            cp.wait_send()
            cp.wait_recv()

    outs = pl.pallas_call(
        body, name=name,
        out_shape=(*[_hbm_like(a) for a in srcs], *[_hbm_like(a) for a in lands]),
        in_specs=[_HBM] * (2 * n) + [_SEM, _SEM, _ANY], out_specs=tuple([_HBM] * (2 * n)),
        input_output_aliases={i: i for i in range(2 * n)},
        compiler_params=pltpu.CompilerParams(has_side_effects=_SIDE),
    )(*srcs, *lands, send_sem, recv_sem, after)
    return list(outs[:n]), list(outs[n:])


_GATHER_ROUTE = (lambda s, j, c: s.at[c], lambda l, me, k, c: l.at[me, c], lambda l, j, k, c: l.at[j, c])
_SCATTER_ROUTE = (lambda s, j, c: s.at[j], lambda l, me, k, c: l.at[k], lambda l, j, k, c: l.at[k])


def _gather_forward(lands, tag, own=False):
    n = len(lands)
    m = 4 if own else 3

    def body(*refs):
        ins, outs = refs[:n], refs[n:2 * n]
        send_sem, recv_sem = refs[2 * n:]
        x, y, c, chips = _place()
        slots = [2 * cx + cy for cx, cy in chips] + [2 * x + y]

        def copy(i, k, half):
            return pltpu.make_async_remote_copy(
                src_ref=ins[i].at[slots[k], half], dst_ref=outs[i].at[slots[k], half],
                send_sem=send_sem.at[m * i + k], recv_sem=recv_sem.at[m * i + k],
                device_id=(x, y, 1 - c), device_id_type=_MESH)

        copies = [copy(i, k, c) for i in range(n) for k in range(m)]
        for cp in copies:
            cp.start()
        for i in range(n):
            for k in range(m):
                copy(i, k, 1 - c).wait_recv()
        for cp in copies:
            cp.wait_send()

    return pl.pallas_call(
        body, name="gather_forward_to_sibling_" + tag,
        out_shape=[jax.ShapeDtypeStruct(a.shape, a.dtype) for a in lands],
        in_specs=[_ANY] * n, out_specs=[_ANY] * n,
        input_output_aliases={i: i for i in range(n)},
        scratch_shapes=[pltpu.SemaphoreType.DMA((m * n,)), pltpu.SemaphoreType.DMA((m * n,))],
    )(*lands)


def _swap_halves(grads, tag):
    n = len(grads)

    def body(*refs):
        ins, outs = refs[:n], refs[n:2 * n]
        send_sem, recv_sem = refs[2 * n:]
        x, y, c, _ = _place()
        copies = [pltpu.make_async_remote_copy(
            src_ref=ins[i].at[:, 1 - c], dst_ref=outs[i],
            send_sem=send_sem.at[i], recv_sem=recv_sem.at[i],
            device_id=(x, y, 1 - c), device_id_type=_MESH) for i in range(n)]
        for cp in copies:
            cp.start()
        for cp in copies:
            cp.wait()

    return pl.pallas_call(
        body, name="grad_swap_halves_" + tag,
        out_shape=[jax.ShapeDtypeStruct((N_CHIPS,) + g.shape[2:], g.dtype) for g in grads],
        in_specs=[_ANY] * n, out_specs=[_ANY] * n,
        scratch_shapes=[pltpu.SemaphoreType.DMA((n,)), pltpu.SemaphoreType.DMA((n,))],
    )(*grads)


def _sum_rows(h, C):
    return max(d for d in range(SUBLANES, h + 1, SUBLANES) if h % d == 0 and d * C <= 1 << 20)


def _pair_sum(g, r, c_idx, *, name):
    _, _, h, C = g.shape
    tr = _sum_rows(h, C)

    def body(c_ref, g_ref, r_ref, o_ref):
        o_ref[...] = (g_ref[...].astype(F32) + r_ref[...].astype(F32)).astype(o_ref.dtype)

    return pl.pallas_call(
        body, name=name, out_shape=jax.ShapeDtypeStruct((N_CHIPS, h, C), g.dtype),
        grid_spec=pltpu.PrefetchScalarGridSpec(
            num_scalar_prefetch=1, grid=(N_CHIPS, h // tr),
            in_specs=[pl.BlockSpec((None, None, tr, C), lambda j, i, s: (j, s[0], i, 0)),
                      pl.BlockSpec((None, tr, C), lambda j, i, s: (j, i, 0))],
            out_specs=pl.BlockSpec((None, tr, C), lambda j, i, s: (j, i, 0))),
        compiler_params=_params("parallel", "parallel"),
    )(c_idx, g, r)


def _owner_sum(s, r, jc_idx, *, name):
    _, h, C = s.shape
    tr = _sum_rows(h, C)

    def body(jc_ref, s_ref, r_ref, o_ref):
        acc = s_ref[...].astype(F32)
        for k in range(3):
            acc = acc + r_ref[k].astype(F32)
        o_ref[...] = acc

    return pl.pallas_call(
        body, name=name, out_shape=jax.ShapeDtypeStruct((2, h, C), F32),
        grid_spec=pltpu.PrefetchScalarGridSpec(
            num_scalar_prefetch=1, grid=(h // tr,),
            in_specs=[pl.BlockSpec((None, tr, C), lambda i, s: (s[0], i, 0)),
                      pl.BlockSpec((3, tr, C), lambda i, s: (0, i, 0))],
            out_specs=pl.BlockSpec((None, tr, C), lambda i, s: (s[1], i, 0))),
        compiler_params=_params("parallel"),
    )(jc_idx, s, r)


def _share_with_sibling(bufs, tag):
    n = len(bufs)

    def body(*refs):
        ins, outs = refs[:n], refs[n:2 * n]
        send_sem, recv_sem = refs[2 * n:]
        x, y, c, _ = _place()

        def copy(i, half):
            return pltpu.make_async_remote_copy(
                src_ref=ins[i].at[half], dst_ref=outs[i].at[half],
                send_sem=send_sem.at[i], recv_sem=recv_sem.at[i],
                device_id=(x, y, 1 - c), device_id_type=_MESH)

        copies = [copy(i, c) for i in range(n)]
        for cp in copies:
            cp.start()
        for i in range(n):
            copy(i, 1 - c).wait_recv()
        for cp in copies:
            cp.wait_send()

    return pl.pallas_call(
        body, name="grad_share_with_sibling_" + tag,
        out_shape=[jax.ShapeDtypeStruct(b.shape, b.dtype) for b in bufs],
        in_specs=[_ANY] * n, out_specs=[_ANY] * n,
        input_output_aliases={i: i for i in range(n)},
        scratch_shapes=[pltpu.SemaphoreType.DMA((n,)), pltpu.SemaphoreType.DMA((n,))],
    )(*bufs)


def _chip_sums(grads, c_idx, tag):
    views = [g.reshape(N_CHIPS, 2, g.shape[1] // 2, g.shape[2]) for g in grads]
    arrived = _swap_halves(views, tag)
    return [_pair_sum(v, r, c_idx, name=f"grad_pair_sum_{tag}_{i}") for i, (v, r) in enumerate(zip(views, arrived))]


def _owner_totals(sums, arrived, jc_idx, tag):
    halves = [_owner_sum(s, r, jc_idx, name=f"grad_owner_sum_{tag}_{i}") for i, (s, r) in enumerate(zip(sums, arrived))]
    return [f.reshape(-1, f.shape[2]) for f in _share_with_sibling(halves, tag)]


def _sum_devices(blocks):
    R = blocks.shape[2]
    tr = _sum_rows(R, 2 * N_CHIPS * LANES)

    def body(b_ref, o_ref):
        acc = b_ref[0, 0]
        for d in range(1, 2 * N_CHIPS):
            acc = acc + b_ref[d // 2, d % 2]
        o_ref[...] = acc

    return pl.pallas_call(
        body, name="sum_small_over_devices", out_shape=jax.ShapeDtypeStruct((R, LANES), F32),
        grid=(R // tr,),
        in_specs=[pl.BlockSpec((N_CHIPS, 2, tr, LANES), lambda i: (0, 0, i, 0))],
        out_specs=pl.BlockSpec((tr, LANES), lambda i: (i, 0)),
        compiler_params=_params("parallel"),
    )(blocks)


def _adamw(w, g, m, v, *, name):
    R, C = w.shape
    whole_fits = 7 * 2 * R * C * 4 <= VMEM_LIMIT_BYTES // 2
    tr = R if whole_fits else next(c for c in (256, 192, 128, 64, 32, 16, 8) if R % c == 0)

    def body(w_ref, g_ref, m_ref, v_ref, d_ref, nm_ref, nv_ref):
        g = g_ref[...]
        m = ADAM_B1 * m_ref[...] + (1.0 - ADAM_B1) * g
        v = ADAM_B2 * v_ref[...] + (1.0 - ADAM_B2) * (g * g)
        nm_ref[...] = m
        nv_ref[...] = v
        m_hat = m / (1.0 - ADAM_B1 ** ADAM_STEP)
        v_hat = v / (1.0 - ADAM_B2 ** ADAM_STEP)
        d_ref[...] = -ADAM_LR * (m_hat / (jnp.sqrt(v_hat) + ADAM_EPS) + ADAM_WD * w_ref[...])

    blk = pl.BlockSpec((tr, C), lambda i: (i, 0))
    sds = jax.ShapeDtypeStruct((R, C), F32)
    return pl.pallas_call(
        body, name=name, out_shape=(sds, sds, sds), grid=(R // tr,),
        in_specs=[blk] * 4, out_specs=(blk, blk, blk),
        compiler_params=_params("parallel"),
    )(w, g, m, v)


_TILE = SUBLANES * LANES


def _pack(arrays):
    rows = []
    for a in arrays:
        flat = a.reshape(-1)
        flat = jnp.pad(flat, (0, (-flat.shape[0]) % _TILE))
        rows.append(flat.reshape(-1, LANES))
    return jnp.concatenate(rows, axis=0)


def _unpack(buf, shapes):
    out, r = [], 0
    for s in shapes:
        size = math.prod(s)
        nr = -(-size // _TILE) * SUBLANES
        out.append(buf[r:r + nr].reshape(-1)[:size].reshape(s))
        r += nr
    return out


_BIG = ("w_in_a", "w_glu", "w_kv", "w_in_b", "w_mem_kv", "w_out")
_REPLICATED = ("pre_norm_g", "post_norm_g", "lam_re", "lam_im", "log_step", "b_re", "b_im", "c_re", "c_im",
               "kv_norm_g", "b_fgate", "mem_norm_g")
_SHARDED_SMALL = ("d_skip", "b_glu", "w_fgate")
_WEIGHTS = ("pre_norm_g", "post_norm_g", "w_in_a", "lam_re", "lam_im", "log_step", "b_re", "b_im", "c_re",
            "c_im", "d_skip", "w_glu", "b_glu", "kv_norm_g", "w_kv", "w_fgate", "b_fgate", "w_in_b",
            "mem_norm_g", "w_mem_kv", "w_out")


def _halves(a):
    return a.reshape(2, a.shape[0] // 2, a.shape[1])


def _unhalve(a):
    return a.reshape(N_CHIPS, 2 * a.shape[2], a.shape[3])


def _columns(a):
    return jnp.transpose(a, (1, 0, 2)).reshape(a.shape[1], N_CHIPS * a.shape[2])


def kernel(x, mem, pre_norm_g, post_norm_g, w_in_a, lam_re, lam_im, log_step, b_re, b_im, c_re, c_im, d_skip, w_glu, b_glu, kv_norm_g, w_kv, w_fgate, b_fgate, w_in_b, mem_norm_g, w_mem_kv, w_out, loss_target, m_pre_norm_g, m_post_norm_g, m_w_in_a, m_lam_re, m_lam_im, m_log_step, m_b_re, m_b_im, m_c_re, m_c_im, m_d_skip, m_w_glu, m_b_glu, m_kv_norm_g, m_w_kv, m_w_fgate, m_b_fgate, m_w_in_b, m_mem_norm_g, m_w_mem_kv, m_w_out, v_pre_norm_g, v_post_norm_g, v_w_in_a, v_lam_re, v_lam_im, v_log_step, v_b_re, v_b_im, v_c_re, v_c_im, v_d_skip, v_w_glu, v_b_glu, v_kv_norm_g, v_w_kv, v_w_fgate, v_b_fgate, v_w_in_b, v_mem_norm_g, v_w_mem_kv, v_w_out):
    a = dict(locals())
    xi, yi, ci = lax.axis_index("x"), lax.axis_index("y"), lax.axis_index("c")
    chip = 2 * xi + yi
    c_idx = jnp.reshape(ci, (1,)).astype(jnp.int32)
    jc_idx = jnp.stack([chip, ci]).astype(jnp.int32)

    vec = jnp.zeros((2 * SUBLANES, MAIN_WIDTH // N_CHIPS), F32)
    vec = vec.at[0].set(a["d_skip"][0]).at[1].set(a["b_glu"][0])
    def own_slot(gathered, parts):
        return [lax.dynamic_update_index_in_dim(g, p, chip, 0) for g, p in zip(gathered, parts)]

    parts_a = [_halves(a["w_in_a"][0].astype(BF16)), _halves(vec)]
    parts_b = [_halves(a["w_glu"][0].astype(BF16)), _halves(a["w_mem_kv"].reshape(-1, 2 * MEM_WIDTH).astype(BF16)),
               _halves(a["w_out"].reshape(-1, D_MODEL).astype(BF16))]
    parts_c = [_halves(a["w_kv"].astype(BF16)), _halves(_pad_lanes(a["w_fgate"]).astype(BF16)),
               _halves(a["w_in_b"][0].astype(BF16))]
    travelling, token = {}, a["pre_norm_g"]
    for tag, parts in (("a", parts_a), ("b", parts_b), ("c", parts_c)):
        lands = [lax.empty((N_CHIPS,) + p.shape, p.dtype) for p in parts]
        travelling[tag], token = _ici_start(parts, lands, token, _GATHER_ROUTE, name=f"gather_{tag}_start")

    def fetch(tag, after):
        parts, lands = _ici_wait(travelling[tag], after, _GATHER_ROUTE, name=f"gather_{tag}_wait")
        full = own_slot(_gather_forward(lands, tag), parts)
        if tag == "a":
            w_in_a, vecs = full
            return dict(w_in_a=_columns(_unhalve(w_in_a)), d_skip=vecs[:, 0, 0, :].reshape(MAIN_WIDTH),
                        b_glu=vecs[:, 0, 1, :].reshape(MAIN_WIDTH))
        if tag == "b":
            w_glu, w_mk, w_out = full
            return dict(w_glu=w_glu.reshape(MAIN_WIDTH, MAIN_WIDTH),
                        w_mem_kv=jnp.transpose(w_mk, (1, 0, 2, 3)).reshape(2, D_MODEL, 2 * MEM_WIDTH),
                        w_out=jnp.transpose(w_out, (1, 0, 2, 3)).reshape(2, D_MODEL, D_MODEL))
        w_kv, w_fg, w_in_b = full
        return dict(w_kv=_columns(_unhalve(w_kv)), w_fgate=w_fg.reshape(D_MODEL, LANES),
                    w_in_b=_columns(_unhalve(w_in_b)))

    w = dict(
        pre_norm_g=token, post_norm_g=a["post_norm_g"], mem_norm_g=a["mem_norm_g"],
        kv_norm_g=a["kv_norm_g"], b_fgate=a["b_fgate"],
        lam_re=a["lam_re"][0], lam_im=a["lam_im"][0], log_step=a["log_step"][0],
        b_re=a["b_re"][0], b_im=a["b_im"][0], c_re=a["c_re"][0], c_im=a["c_im"][0])

    sent = {}

    def grads_ready(tag, g, token):
        big = {"b": lambda: [g["w_kv"], g["w_in_b"], g["w_mem_kv_1"].reshape(N_CHIPS, -1, 2 * MEM_WIDTH),
                             g["w_out_1"].reshape(N_CHIPS, -1, D_MODEL)],
               "a1": lambda: [g["w_glu"].reshape(N_CHIPS, -1, MAIN_WIDTH),
                              g["w_mem_kv_0"].reshape(N_CHIPS, -1, 2 * MEM_WIDTH),
                              g["w_out_0"].reshape(N_CHIPS, -1, D_MODEL)],
               "a2": lambda: [g["w_in_a"]]}[tag]()
        sums = _chip_sums(big, c_idx, tag)
        lands = [lax.empty((3,) + s.shape[1:], s.dtype) for s in sums]
        sent[tag], token = _ici_start(sums, lands, token, _SCATTER_ROUTE, name=f"grad_send_{tag}_start")
        return token

    loss_row, grad_x, g = _local_step(a["x"][0], a["mem"][0], a["loss_target"][0], w, fetch, grads_ready)

    small_names = _REPLICATED + _SHARDED_SMALL
    pack = _pack([g[n] for n in small_names])
    blocks = lax.empty((N_CHIPS, 2) + pack.shape, F32)
    small_sent, loss_row = _ici_start([pack], [blocks], loss_row, _BLOCK_ROUTE, name="small_sums_start")
    loss = lax.psum(jnp.sum(loss_row), MESH_AXES)

    def totals(tag, after):
        sums, arrived = _ici_wait(sent[tag], after, _SCATTER_ROUTE, name=f"grad_send_{tag}_wait")
        return _owner_totals(sums, arrived, jc_idx, tag)

    r_kv, r_in_b, r_mk1, r_out1 = totals("b", grad_x)
    r_glu, r_mk0, r_out0 = totals("a1", r_out1)
    (r_in_a,) = totals("a2", r_out0)
    grads = {"w_in_a": r_in_a[None], "w_glu": r_glu[None], "w_kv": r_kv, "w_in_b": r_in_b[None],
             "w_mem_kv": jnp.stack([r_mk0, r_mk1]), "w_out": jnp.stack([r_out0, r_out1])}

    delta, new_m, new_v = {}, {}, {}
    for n in _BIG:
        shape = a[n].shape
        d2 = (-1, shape[-1])
        d, m, v = _adamw(a[n].reshape(d2), grads[n].reshape(d2), a["m_" + n].reshape(d2),
                         a["v_" + n].reshape(d2), name="adamw_" + n)
        delta[n], new_m[n], new_v[n] = d.reshape(shape), m.reshape(shape), v.reshape(shape)

    (pack,), (blocks,) = _ici_wait(small_sent, delta["w_out"], _BLOCK_ROUTE, name="small_sums_wait")
    blocks = lax.dynamic_update_slice(blocks, pack[None, None], (chip, ci, 0, 0))
    (blocks,) = _gather_forward([blocks], "small", own=True)
    small = dict(zip(small_names, _unpack(_sum_devices(blocks), [g[n].shape for n in small_names])))
    for n in _REPLICATED:
        grads[n] = small[n].reshape(a[n].shape)
    nd = MAIN_WIDTH // N_CHIPS
    grads["d_skip"] = lax.dynamic_slice(small["d_skip"], (chip * nd,), (nd,))[None]
    grads["b_glu"] = lax.dynamic_slice(small["b_glu"], (chip * nd,), (nd,))[None]
    nf = D_MODEL // N_CHIPS
    grads["w_fgate"] = lax.dynamic_slice(small["w_fgate"], (chip * nf, 0), (nf, FOX_HEADS))

    shapes = [a[n].shape for n in small_names]
    d, m, v = _adamw(_pack([a[n] for n in small_names]), _pack([grads[n] for n in small_names]),
                     _pack([a["m_" + n] for n in small_names]), _pack([a["v_" + n] for n in small_names]),
                     name="adamw_small")
    for n, dd, mm, vv in zip(small_names, _unpack(d, shapes), _unpack(m, shapes), _unpack(v, shapes)):
        delta[n], new_m[n], new_v[n] = dd, mm, vv

    return (loss, grad_x[None], *[grads[n] for n in _WEIGHTS], *[delta[n] for n in _WEIGHTS],
            *[new_m[n] for n in _WEIGHTS], *[new_v[n] for n in _WEIGHTS])
```

```python
import functools
import math

import jax
import jax.numpy as jnp
from jax import lax
from jax.experimental import pallas as pl
from jax.experimental.pallas import tpu as pltpu

F32 = jnp.float32
BF16 = jnp.bfloat16

D_MODEL = 2048
N_MEM = 256
MAIN_WIDTH = 1536
MEM_WIDTH = 512
IN_WIDTH = 2 * MAIN_WIDTH + 2 * MEM_WIDTH
HEAD_DIM = 128
FOX_HEADS = MAIN_WIDTH // HEAD_DIM
MEM_HEADS = MEM_WIDTH // HEAD_DIM
SSM_GROUP = 16
SSM_GROUPS = MAIN_WIDTH // SSM_GROUP
SSM_STATE = 64
GROUPS_PER_BLOCK = 8
SSM_BLOCKS = SSM_GROUPS // GROUPS_PER_BLOCK
STATE_COLS = GROUPS_PER_BLOCK * SSM_STATE
EPS = 1e-6
ADAM_LR = 0.001
ADAM_B1 = 0.9
ADAM_B2 = 0.999
ADAM_EPS = 1e-08
ADAM_WD = 0.01
ADAM_STEP = 10
N_CHIPS = 4
LANES = 128
SUBLANES = 8
VMEM_LIMIT_BYTES = 56 * 1024 * 1024
NEG_BIG = -1e30
MESH_AXES = ("x", "y", "c")


def _params(*sem):
    return pltpu.CompilerParams(dimension_semantics=sem if sem else None,
                                vmem_limit_bytes=VMEM_LIMIT_BYTES)


def _sigmoid(x):
    return 1.0 / (1.0 + jnp.exp(-x))


def _gelu(x):
    c = math.sqrt(2.0 / math.pi)
    return 0.5 * x * (1.0 + jnp.tanh(c * (x + 0.044715 * (x * x * x))))


def _gelu_grad(x):
    c = math.sqrt(2.0 / math.pi)
    t = jnp.tanh(c * (x + 0.044715 * (x * x * x)))
    return 0.5 * (1.0 + t) + 0.5 * x * (1.0 - t * t) * (c * (1.0 + 3.0 * 0.044715 * (x * x)))


def _silu_and_grad(z):
    s = _sigmoid(z)
    return z * s, s * (1.0 + z * (1.0 - s))


_TILE_CHOICES = (2048, 1024, 768, 512, 384, 256, LANES)


def _tile(n, cap):
    return next(c for c in _TILE_CHOICES if c <= cap and n % c == 0)


def _mm(a, b, *, name, ta=False, tb=False, out_dtype=F32, shards=1, tm=1024, tn=1024, tk=2048):
    if ta:
        K, M = a.shape
    else:
        M, K = a.shape
    if tb:
        N, kb = b.shape
    else:
        kb, N = b.shape
    assert K == kb, (a.shape, b.shape)
    ns = N // shards
    tm, tn, tk = _tile(M, tm), _tile(ns, tn), _tile(K, tk)
    assert M % tm == 0 and ns % tn == 0 and K % tk == 0 and N % shards == 0
    nk = K // tk
    dn = (((0 if ta else 1,), (1 if tb else 0,)), ((), ()))

    def body(a_ref, b_ref, o_ref, acc_ref):
        k = pl.program_id(2)

        @pl.when(k == 0)
        def _():
            acc_ref[...] = jnp.zeros_like(acc_ref)

        acc_ref[...] += lax.dot_general(a_ref[...].astype(BF16), b_ref[...].astype(BF16), dn,
                                        preferred_element_type=F32)

        @pl.when(k == nk - 1)
        def _():
            o_ref[...] = acc_ref[...].astype(o_ref.dtype)

    a_spec = (pl.BlockSpec((tk, tm), lambda i, j, k: (k, i)) if ta
              else pl.BlockSpec((tm, tk), lambda i, j, k: (i, k)))
    b_spec = (pl.BlockSpec((tn, tk), lambda i, j, k: (j, k)) if tb
              else pl.BlockSpec((tk, tn), lambda i, j, k: (k, j)))
    if shards == 1:
        out_shape = jax.ShapeDtypeStruct((M, N), out_dtype)
        o_spec = pl.BlockSpec((tm, tn), lambda i, j, k: (i, j))
    else:
        nb = ns // tn
        out_shape = jax.ShapeDtypeStruct((shards, M, ns), out_dtype)
        o_spec = pl.BlockSpec((None, tm, tn), lambda i, j, k: (j // nb, i, j % nb))
    return pl.pallas_call(
        body, name=name, out_shape=out_shape,
        grid=(M // tm, N // tn, nk),
        in_specs=[a_spec, b_spec], out_specs=o_spec,
        scratch_shapes=[pltpu.VMEM((tm, tn), F32)],
        compiler_params=_params("parallel", "parallel", "arbitrary"),
    )(a, b)


def _rmsnorm_fwd(x, g, *, name, res=None, out_dtype=F32, tr=256):
    L, D = x.shape
    tr = min(tr, L)
    has_res = res is not None

    def body(*refs):
        if has_res:
            x_ref, g_ref, r_ref, o_ref = refs
        else:
            x_ref, g_ref, o_ref = refs
        xf = x_ref[...]
        r = lax.rsqrt(jnp.mean(xf * xf, axis=-1, keepdims=True) + EPS)
        y = xf * r * g_ref[...]
        if has_res:
            y = r_ref[...] + y
        o_ref[...] = y.astype(o_ref.dtype)

    row = pl.BlockSpec((tr, D), lambda i: (i, 0))
    vec = pl.BlockSpec((1, D), lambda i: (0, 0))
    ins = [x, g.reshape(1, D)] + ([res] if has_res else [])
    return pl.pallas_call(
        body, name=name, out_shape=jax.ShapeDtypeStruct((L, D), out_dtype),
        grid=(L // tr,), in_specs=[row, vec] + ([row] if has_res else []), out_specs=row,
        compiler_params=_params("parallel"),
    )(*ins)


def _rmsnorm_bwd(x, g, dy, *, name, adds=(), dx_dtype=F32, tr=256):
    L, D = x.shape
    tr = min(tr, L)
    dys = dy if isinstance(dy, tuple) else (dy,)
    n_dy, n_add = len(dys), len(adds)

    def body(*refs):
        x_ref, g_ref = refs[:2]
        dy_refs = refs[2:2 + n_dy]
        add_refs = refs[2 + n_dy:2 + n_dy + n_add]
        dx_ref, dg_ref = refs[2 + n_dy + n_add:]
        xf = x_ref[...]
        dyf = dy_refs[0][...].astype(F32)
        for d_ref in dy_refs[1:]:
            dyf = dyf + d_ref[...].astype(F32)
        r = lax.rsqrt(jnp.mean(xf * xf, axis=-1, keepdims=True) + EPS)
        gy = dyf * g_ref[...]
        c = jnp.mean(xf * gy, axis=-1, keepdims=True) * (r * r * r)
        dx = gy * r - xf * c
        for a_ref in add_refs:
            dx = dx + a_ref[...].astype(F32)
        dx_ref[...] = dx.astype(dx_ref.dtype)

        @pl.when(pl.program_id(0) == 0)
        def _():
            dg_ref[...] = jnp.zeros_like(dg_ref)

        dg_ref[...] += jnp.sum(dyf * xf * r, axis=0, keepdims=True)

    row = pl.BlockSpec((tr, D), lambda i: (i, 0))
    vec = pl.BlockSpec((1, D), lambda i: (0, 0))
    dx, dg = pl.pallas_call(
        body, name=name,
        out_shape=(jax.ShapeDtypeStruct((L, D), dx_dtype), jax.ShapeDtypeStruct((1, D), F32)),
        grid=(L // tr,), in_specs=[row, vec] + [row] * (n_dy + n_add), out_specs=(row, vec),
        compiler_params=_params("arbitrary"),
    )(x, g.reshape(1, D), *dys, *adds)
    return dx, dg.reshape(D)


def _final_norm_loss(o, g, res, target, *, tr=256):
    L, D = o.shape
    tr = min(tr, L)

    def body(o_ref, g_ref, r_ref, t_ref, dh_ref, loss_ref):
        xf = o_ref[...]
        r = lax.rsqrt(jnp.mean(xf * xf, axis=-1, keepdims=True) + EPS)
        e = (r_ref[...] + xf * r * g_ref[...]) - t_ref[...]
        dh_ref[...] = e * (1.0 / D)

        @pl.when(pl.program_id(0) == 0)
        def _():
            loss_ref[...] = jnp.zeros_like(loss_ref)

        loss_ref[...] += jnp.sum(e * e, axis=0, keepdims=True) * (0.5 / D)

    row = pl.BlockSpec((tr, D), lambda i: (i, 0))
    vec = pl.BlockSpec((1, D), lambda i: (0, 0))
    dh, lp = pl.pallas_call(
        body, name="post_norm_1_loss",
        out_shape=(jax.ShapeDtypeStruct((L, D), F32), jax.ShapeDtypeStruct((1, D), F32)),
        grid=(L // tr,), in_specs=[row, vec, row, row], out_specs=(row, vec),
        compiler_params=_params("arbitrary"),
    )(o, g.reshape(1, D), res, target)
    return dh, lp


def _s5_coeffs(lr, li, ls):
    dt = jnp.exp(ls)
    mag = jnp.exp(lr * dt)
    ar = mag * jnp.cos(li * dt)
    ai = mag * jnp.sin(li * dt)
    den = lr * lr + li * li
    cr = ((ar - 1.0) * lr + ai * li) / den
    ci = (ai * lr - (ar - 1.0) * li) / den
    return dt, ar, ai, den, cr, ci


def _s5_prep(lam_re, lam_im, log_step, b_re_t, b_im_t):
    G, P = lam_re.shape
    H = b_re_t.shape[1]

    def body(lr_ref, li_ref, ls_ref, br_ref, bi_ref, ar_ref, ai_ref, bbr_ref, bbi_ref):
        _, ar, ai, _, cr, ci = _s5_coeffs(lr_ref[...], li_ref[...], ls_ref[...])
        ar_ref[...] = ar
        ai_ref[...] = ai
        br, bi = br_ref[...], bi_ref[...]
        crb, cib = cr[:, None, :], ci[:, None, :]
        bbr_ref[...] = crb * br - cib * bi
        bbi_ref[...] = crb * bi + cib * br

    return pl.pallas_call(
        body, name="s5_prep",
        out_shape=(jax.ShapeDtypeStruct((G, P), F32), jax.ShapeDtypeStruct((G, P), F32),
                   jax.ShapeDtypeStruct((G, H, P), F32), jax.ShapeDtypeStruct((G, H, P), F32)),
        compiler_params=_params(),
    )(lam_re, lam_im, log_step.reshape(G, 1), b_re_t, b_im_t)


def _s5_prep_bwd(lam_re, lam_im, log_step, b_re_t, b_im_t, d_ar, d_ai, d_bbr, d_bbi):
    G, P = lam_re.shape
    H = b_re_t.shape[1]

    def body(lr_ref, li_ref, ls_ref, br_ref, bi_ref, dar_ref, dai_ref, dbbr_ref, dbbi_ref,
             dlr_ref, dli_ref, dls_ref, dbr_ref, dbi_ref):
        lr, li = lr_ref[...], li_ref[...]
        dt, ar, ai, den, cr, ci = _s5_coeffs(lr, li, ls_ref[...])
        br, bi = br_ref[...], bi_ref[...]
        gbr, gbi = dbbr_ref[...], dbbi_ref[...]
        crb, cib = cr[:, None, :], ci[:, None, :]
        dbr_ref[...] = crb * gbr + cib * gbi
        dbi_ref[...] = crb * gbi - cib * gbr
        gcr = jnp.sum(br * gbr + bi * gbi, axis=1)
        gci = jnp.sum(br * gbi - bi * gbr, axis=1)
        ilr, ili = lr / den, -li / den
        gar = dar_ref[...] + (ilr * gcr + ili * gci)
        gai = dai_ref[...] + (ilr * gci - ili * gcr)
        qr, qi = cr * ilr - ci * ili, cr * ili + ci * ilr
        glr = -(qr * gcr + qi * gci)
        gli = -(qr * gci - qi * gcr)
        glr = glr + dt * (ar * gar + ai * gai)
        gli = gli + dt * (ar * gai - ai * gar)
        wr, wi = lr * ar - li * ai, lr * ai + li * ar
        gdt = jnp.sum(wr * gar + wi * gai, axis=1, keepdims=True)
        dlr_ref[...] = glr
        dli_ref[...] = gli
        dls_ref[...] = gdt * dt

    return pl.pallas_call(
        body, name="s5_prep_bwd",
        out_shape=(jax.ShapeDtypeStruct((G, P), F32), jax.ShapeDtypeStruct((G, P), F32),
                   jax.ShapeDtypeStruct((G, 1), F32),
                   jax.ShapeDtypeStruct((G, H, P), F32), jax.ShapeDtypeStruct((G, H, P), F32)),
        compiler_params=_params(),
    )(lam_re, lam_im, log_step.reshape(G, 1), b_re_t, b_im_t, d_ar, d_ai, d_bbr, d_bbi)


def _s5_block_mats(bbr_t, bbi_t, c_re, c_im):
    bmat = _s5_expand(bbr_t, bbi_t)
    cmat = jnp.transpose(_s5_expand(c_re, -c_im), (0, 2, 1))
    return bmat.astype(BF16), cmat.astype(BF16)


def _s5_diag_mask():
    r = lax.broadcasted_iota(jnp.int32, (LANES, 2 * STATE_COLS), 0) // SSM_GROUP
    c = (lax.broadcasted_iota(jnp.int32, (LANES, 2 * STATE_COLS), 1) % STATE_COLS) // SSM_STATE
    return (r == c).astype(F32)


def _s5_expand(re, im):
    re = jnp.tile(re.reshape(SSM_BLOCKS, LANES, SSM_STATE), (1, 1, GROUPS_PER_BLOCK))
    im = jnp.tile(im.reshape(SSM_BLOCKS, LANES, SSM_STATE), (1, 1, GROUPS_PER_BLOCK))
    return jnp.concatenate([re, im], axis=-1) * _s5_diag_mask()[None]


def _s5_block_diag(dmat):
    d = dmat * _s5_diag_mask()[None]
    parts = []
    for ri in range(2):
        acc = 0.0
        for g in range(GROUPS_PER_BLOCK):
            c0 = ri * STATE_COLS + g * SSM_STATE
            acc = acc + d[:, :, c0:c0 + SSM_STATE]
        parts.append(acc.reshape(SSM_GROUPS, SSM_GROUP, SSM_STATE))
    return jnp.stack(parts)


def _s5_a_rows(ar, ai):
    a = jnp.concatenate([ar.reshape(SSM_BLOCKS, STATE_COLS), ai.reshape(SSM_BLOCKS, STATE_COLS)], axis=1)
    return jnp.broadcast_to(a[:, None, :], (SSM_BLOCKS, SUBLANES, 2 * STATE_COLS))


def _s5_fwd(proj, bmat, cmat, a_rows, d_skip, *, tc=512):
    L = proj.shape[0]
    tc = min(tc, L)
    nt = L // tc
    n8 = tc // SUBLANES
    S = STATE_COLS

    def body(u_ref, b_ref, c_ref, a_ref, d_ref, y_ref, yg_ref, xp_ref, bu_s, xp_s, carry_s):
        @pl.when(pl.program_id(1) == 0)
        def _():
            carry_s[...] = jnp.zeros_like(carry_s)

        u = u_ref[...]
        bu = jnp.dot(u.astype(BF16), b_ref[...], preferred_element_type=F32)
        bu_s[...] = bu.reshape(n8, SUBLANES, 2 * S)
        ar, ai = a_ref[0:1, :S], a_ref[0:1, S:]

        def step(i, carry):
            cr, ci = carry
            for j in range(SUBLANES):
                xp_s[i, j:j + 1, :S] = cr
                xp_s[i, j:j + 1, S:] = ci
                br = bu_s[i, j:j + 1, :S]
                bi = bu_s[i, j:j + 1, S:]
                cr, ci = ar * cr - ai * ci + br, ar * ci + ai * cr + bi
            return cr, ci

        cr, ci = lax.fori_loop(0, n8, step, (carry_s[0:1, :S], carry_s[0:1, S:]))
        carry_s[0:1, :S] = cr
        carry_s[0:1, S:] = ci
        xp = xp_s[...].reshape(tc, 2 * S)
        xp_ref[...] = xp
        x_re = ar * xp[:, :S] - ai * xp[:, S:] + bu[:, :S]
        x_im = ar * xp[:, S:] + ai * xp[:, :S] + bu[:, S:]
        xs = jnp.concatenate([x_re, x_im], axis=1).astype(BF16)
        y = jnp.dot(xs, c_ref[...], preferred_element_type=F32) + d_ref[...] * u
        y_ref[...] = y
        yg_ref[...] = _gelu(y).astype(BF16)

    return pl.pallas_call(
        body, name="s5_fwd",
        out_shape=(jax.ShapeDtypeStruct((L, MAIN_WIDTH), F32),
                   jax.ShapeDtypeStruct((L, MAIN_WIDTH), BF16),
                   jax.ShapeDtypeStruct((L, SSM_BLOCKS * 2 * S), F32)),
        grid=(SSM_BLOCKS, nt),
        in_specs=[pl.BlockSpec((tc, LANES), lambda b, t: (t, b)),
                  pl.BlockSpec((None, LANES, 2 * S), lambda b, t: (b, 0, 0)),
                  pl.BlockSpec((None, 2 * S, LANES), lambda b, t: (b, 0, 0)),
                  pl.BlockSpec((None, SUBLANES, 2 * S), lambda b, t: (b, 0, 0)),
                  pl.BlockSpec((1, LANES), lambda b, t: (0, b))],
        out_specs=(pl.BlockSpec((tc, LANES), lambda b, t: (t, b)),
                   pl.BlockSpec((tc, LANES), lambda b, t: (t, b)),
                   pl.BlockSpec((tc, 2 * S), lambda b, t: (t, b))),
        scratch_shapes=[pltpu.VMEM((n8, SUBLANES, 2 * S), F32),
                        pltpu.VMEM((n8, SUBLANES, 2 * S), F32),
                        pltpu.VMEM((SUBLANES, 2 * S), F32)],
        compiler_params=_params("parallel", "arbitrary"),
    )(proj, bmat, cmat, a_rows, d_skip.reshape(1, MAIN_WIDTH))


def _s5_bwd(proj, dyg_a, dyg_b, y, xp, bmat, cmat, a_rows, d_skip, dproj, *, tc=512):
    L = proj.shape[0]
    tc = min(tc, L)
    nt = L // tc
    n8 = tc // SUBLANES
    S = STATE_COLS
    nn = (((1,), (1,)), ((), ()))
    tn = (((0,), (0,)), ((), ()))

    def body(u_ref, dyga_ref, dygb_ref, y_ref, xp_ref, b_ref, c_ref, a_ref, d_ref, dp_hbm,
             du_ref, db_ref, dc_ref, da_ref, dd_ref, dl_s, carry_s):
        @pl.when(pl.program_id(1) == 0)
        def _():
            carry_s[...] = jnp.zeros_like(carry_s)
            db_ref[...] = jnp.zeros_like(db_ref)
            dc_ref[...] = jnp.zeros_like(dc_ref)
            da_ref[...] = jnp.zeros_like(da_ref)
            dd_ref[...] = jnp.zeros_like(dd_ref)

        u = u_ref[...]
        dy = (dyga_ref[...] + dygb_ref[...]) * _gelu_grad(y_ref[...])
        xp = xp_ref[...]
        ub = u.astype(BF16)
        dyb = dy.astype(BF16)
        ar, ai = a_ref[0:1, :S], a_ref[0:1, S:]
        bu = jnp.dot(ub, b_ref[...], preferred_element_type=F32)
        x_re = ar * xp[:, :S] - ai * xp[:, S:] + bu[:, :S]
        x_im = ar * xp[:, S:] + ai * xp[:, :S] + bu[:, S:]
        xs = jnp.concatenate([x_re, x_im], axis=1).astype(BF16)
        dc_ref[...] += lax.dot_general(dyb, xs, tn, preferred_element_type=F32)
        dx = lax.dot_general(dyb, c_ref[...], nn, preferred_element_type=F32)
        dl_s[...] = dx.reshape(n8, SUBLANES, 2 * S)

        def step(k, carry):
            cr, ci = carry
            i = n8 - 1 - k
            for j in range(SUBLANES - 1, -1, -1):
                lr = dl_s[i, j:j + 1, :S] + (ar * cr + ai * ci)
                li = dl_s[i, j:j + 1, S:] + (ar * ci - ai * cr)
                dl_s[i, j:j + 1, :S] = lr
                dl_s[i, j:j + 1, S:] = li
                cr, ci = lr, li
            return cr, ci

        cr, ci = lax.fori_loop(0, n8, step, (carry_s[0:1, :S], carry_s[0:1, S:]))
        carry_s[0:1, :S] = cr
        carry_s[0:1, S:] = ci
        lam = dl_s[...].reshape(tc, 2 * S)
        l_re, l_im = lam[:, :S], lam[:, S:]
        da_ref[0:1, :S] += jnp.sum(l_re * xp[:, :S] + l_im * xp[:, S:], axis=0, keepdims=True)
        da_ref[0:1, S:] += jnp.sum(l_im * xp[:, :S] - l_re * xp[:, S:], axis=0, keepdims=True)
        lamb = lam.astype(BF16)
        du = lax.dot_general(lamb, b_ref[...], nn, preferred_element_type=F32) + d_ref[...] * dy
        du_ref[...] = du.astype(du_ref.dtype)
        db_ref[...] += lax.dot_general(ub, lamb, tn, preferred_element_type=F32)
        dd_ref[0:1, :] += jnp.sum(dy * u, axis=0, keepdims=True)

    rev = lambda b, t: (nt - 1 - t, b)
    return pl.pallas_call(
        body, name="s5_bwd",
        out_shape=(jax.ShapeDtypeStruct(dproj.shape, dproj.dtype),
                   jax.ShapeDtypeStruct((SSM_BLOCKS, LANES, 2 * S), F32),
                   jax.ShapeDtypeStruct((SSM_BLOCKS, LANES, 2 * S), F32),
                   jax.ShapeDtypeStruct((SSM_BLOCKS, SUBLANES, 2 * S), F32),
                   jax.ShapeDtypeStruct((SUBLANES, MAIN_WIDTH), F32)),
        input_output_aliases={9: 0},
        grid=(SSM_BLOCKS, nt),
        in_specs=[pl.BlockSpec((tc, LANES), rev),
                  pl.BlockSpec((tc, LANES), rev),
                  pl.BlockSpec((tc, LANES), rev),
                  pl.BlockSpec((tc, LANES), rev),
                  pl.BlockSpec((tc, 2 * S), rev),
                  pl.BlockSpec((None, LANES, 2 * S), lambda b, t: (b, 0, 0)),
                  pl.BlockSpec((None, 2 * S, LANES), lambda b, t: (b, 0, 0)),
                  pl.BlockSpec((None, SUBLANES, 2 * S), lambda b, t: (b, 0, 0)),
                  pl.BlockSpec((1, LANES), lambda b, t: (0, b)),
                  _ANY],
        out_specs=(pl.BlockSpec((tc, LANES), rev),
                   pl.BlockSpec((None, LANES, 2 * S), lambda b, t: (b, 0, 0)),
                   pl.BlockSpec((None, LANES, 2 * S), lambda b, t: (b, 0, 0)),
                   pl.BlockSpec((None, SUBLANES, 2 * S), lambda b, t: (b, 0, 0)),
                   pl.BlockSpec((SUBLANES, LANES), lambda b, t: (0, b))),
        scratch_shapes=[pltpu.VMEM((n8, SUBLANES, 2 * S), F32),
                        pltpu.VMEM((SUBLANES, 2 * S), F32)],
        compiler_params=_params("parallel", "arbitrary"),
    )(proj, dyg_a, dyg_b, y, xp, bmat, cmat, a_rows, d_skip.reshape(1, MAIN_WIDTH), dproj)


_Z_COLS = slice(MAIN_WIDTH, 2 * MAIN_WIDTH)
_ZM_COLS = slice(2 * MAIN_WIDTH + MEM_WIDTH, IN_WIDTH)


def _proj_rows(tr):
    return pl.BlockSpec((tr, IN_WIDTH), lambda i: (i, 0))


def _row_specs(tr):
    main = pl.BlockSpec((tr, MAIN_WIDTH), lambda i: (i, 0))
    z = pl.BlockSpec((tr, MAIN_WIDTH), lambda i: (i, 1))
    zm = pl.BlockSpec((tr, MEM_WIDTH), lambda i: (i, IN_WIDTH // MEM_WIDTH - 1))
    mem = pl.BlockSpec((tr, MEM_WIDTH), lambda i: (i, 0))
    cat = pl.BlockSpec((tr, D_MODEL), lambda i: (i, 0))
    vec = pl.BlockSpec((1, MAIN_WIDTH), lambda i: (0, 0))
    return main, z, zm, mem, cat, vec


def _gate_a_fwd(y, t, b_glu, proj, o_mem, *, tr=256):
    L = y.shape[0]
    tr = min(tr, L)

    def body(y_ref, t_ref, b_ref, z_ref, zm_ref, om_ref, o_ref):
        yg = _gelu(y_ref[...])
        sz, _ = _silu_and_grad(z_ref[...])
        o_ref[:, :MAIN_WIDTH] = (yg * _sigmoid(t_ref[...] + b_ref[...]) * sz).astype(BF16)
        szm, _ = _silu_and_grad(zm_ref[...])
        o_ref[:, MAIN_WIDTH:] = (om_ref[...] * szm).astype(BF16)

    main, z, zm, mem, cat, vec = _row_specs(tr)
    return pl.pallas_call(
        body, name="gate_a_fwd", out_shape=jax.ShapeDtypeStruct((L, D_MODEL), BF16),
        grid=(L // tr,), in_specs=[main, main, vec, z, zm, mem], out_specs=cat,
        compiler_params=_params("parallel"),
    )(y, t, b_glu.reshape(1, MAIN_WIDTH), proj, proj, o_mem)


def _gate_a_bwd(dcat, y, t, b_glu, proj, o_mem, *, tr=256):
    L = y.shape[0]
    tr = min(tr, L)

    def body(dc_ref, y_ref, t_ref, b_ref, z_ref, zm_ref, om_ref,
             dp_ref, dt_ref, dyg_ref, dom_ref, db_ref):
        dmain = dc_ref[:, :MAIN_WIDTH]
        dmemo = dc_ref[:, MAIN_WIDTH:]
        yg = _gelu(y_ref[...])
        sg = _sigmoid(t_ref[...] + b_ref[...])
        sz, gz = _silu_and_grad(z_ref[...])
        dp_ref[:, _Z_COLS] = (dmain * (yg * sg) * gz).astype(BF16)
        dy2 = dmain * sz
        dyg_ref[...] = dy2 * sg
        dt = dy2 * yg * (sg * (1.0 - sg))
        dt_ref[...] = dt.astype(BF16)

        @pl.when(pl.program_id(0) == 0)
        def _():
            db_ref[...] = jnp.zeros_like(db_ref)

        db_ref[...] += jnp.sum(dt, axis=0, keepdims=True)
        szm, gzm = _silu_and_grad(zm_ref[...])
        dom_ref[...] = dmemo * szm
        dp_ref[:, _ZM_COLS] = (dmemo * om_ref[...] * gzm).astype(BF16)

    main, z, zm, mem, cat, vec = _row_specs(tr)
    outs = pl.pallas_call(
        body, name="gate_a_bwd",
        out_shape=(jax.ShapeDtypeStruct((L, IN_WIDTH), BF16),
                   jax.ShapeDtypeStruct((L, MAIN_WIDTH), BF16), jax.ShapeDtypeStruct((L, MAIN_WIDTH), F32),
                   jax.ShapeDtypeStruct((L, MEM_WIDTH), F32), jax.ShapeDtypeStruct((1, MAIN_WIDTH), F32)),
        grid=(L // tr,), in_specs=[cat, main, main, vec, z, zm, mem],
        out_specs=(_proj_rows(tr), main, main, mem, vec),
        compiler_params=_params("arbitrary"),
    )(dcat, y, t, b_glu.reshape(1, MAIN_WIDTH), proj, proj, o_mem)
    return outs


def _gate_b_fwd(att, proj, o_mem, *, tr=256):
    L = att.shape[0]
    tr = min(tr, L)

    def body(a_ref, z_ref, zm_ref, om_ref, o_ref):
        sz, _ = _silu_and_grad(z_ref[...])
        o_ref[:, :MAIN_WIDTH] = (a_ref[...] * sz).astype(BF16)
        szm, _ = _silu_and_grad(zm_ref[...])
        o_ref[:, MAIN_WIDTH:] = (om_ref[...] * szm).astype(BF16)

    main, z, zm, mem, cat, _ = _row_specs(tr)
    return pl.pallas_call(
        body, name="gate_b_fwd", out_shape=jax.ShapeDtypeStruct((L, D_MODEL), BF16),
        grid=(L // tr,), in_specs=[main, z, zm, mem], out_specs=cat,
        compiler_params=_params("parallel"),
    )(att, proj, proj, o_mem)


def _gate_b_bwd(dcat, att, proj, o_mem, *, tr=256):
    L = att.shape[0]
    tr = min(tr, L)

    def body(dc_ref, a_ref, z_ref, zm_ref, om_ref, da_ref, dp_ref, dom_ref, dl_ref):
        dmain = dc_ref[:, :MAIN_WIDTH]
        dmemo = dc_ref[:, MAIN_WIDTH:]
        att = a_ref[...]
        sz, gz = _silu_and_grad(z_ref[...])
        datt = dmain * sz
        da_ref[...] = datt
        dp_ref[:, _Z_COLS] = (dmain * att * gz).astype(BF16)
        szm, gzm = _silu_and_grad(zm_ref[...])
        dom_ref[...] = dmemo * szm
        dp_ref[:, _ZM_COLS] = (dmemo * om_ref[...] * gzm).astype(BF16)
        prod = datt * att
        for h in range(FOX_HEADS):
            dl_ref[h] = jnp.sum(prod[:, h * HEAD_DIM:(h + 1) * HEAD_DIM], axis=1, keepdims=True)

    main, z, zm, mem, cat, _ = _row_specs(tr)
    delta = pl.BlockSpec((FOX_HEADS, tr, 1), lambda i: (0, i, 0))
    return pl.pallas_call(
        body, name="gate_b_bwd",
        out_shape=(jax.ShapeDtypeStruct((L, MAIN_WIDTH), F32), jax.ShapeDtypeStruct((L, IN_WIDTH), BF16),
                   jax.ShapeDtypeStruct((L, MEM_WIDTH), F32), jax.ShapeDtypeStruct((FOX_HEADS, L, 1), F32)),
        grid=(L // tr,), in_specs=[cat, main, z, zm, mem], out_specs=(main, _proj_rows(tr), mem, delta),
        compiler_params=_params("parallel"),
    )(dcat, att, proj, proj, o_mem)


_MEM_Q_COL = (2 * MAIN_WIDTH) // HEAD_DIM
_NT = (((1,), (1,)), ((), ()))
_TN = (((0,), (0,)), ((), ()))


def _mem_probs(q_ref, k_ref):
    qs = (q_ref[...] * (HEAD_DIM ** -0.5)).astype(BF16)
    s = lax.dot_general(qs, k_ref[...].astype(BF16), _NT, preferred_element_type=F32)
    e = jnp.exp(s - jnp.max(s, axis=-1, keepdims=True))
    return qs, e / jnp.sum(e, axis=-1, keepdims=True)


def _mem_attn_fwd(proj, kvm, *, tq=2048):
    L = proj.shape[0]
    tq = min(tq, L)

    def body(q_ref, k_ref, v_ref, o_ref):
        _, p = _mem_probs(q_ref, k_ref)
        o_ref[...] = jnp.dot(p.astype(BF16), v_ref[...].astype(BF16), preferred_element_type=F32)

    return pl.pallas_call(
        body, name="mem_attn_fwd", out_shape=jax.ShapeDtypeStruct((L, MEM_WIDTH), F32),
        grid=(MEM_HEADS, L // tq),
        in_specs=[pl.BlockSpec((tq, HEAD_DIM), lambda h, i: (i, _MEM_Q_COL + h)),
                  pl.BlockSpec((N_MEM, HEAD_DIM), lambda h, i: (0, h)),
                  pl.BlockSpec((N_MEM, HEAD_DIM), lambda h, i: (0, MEM_HEADS + h))],
        out_specs=pl.BlockSpec((tq, HEAD_DIM), lambda h, i: (i, h)),
        compiler_params=_params("parallel", "parallel"),
    )(proj, kvm, kvm)


def _mem_attn_bwd(proj, kvm, do, dproj, *, tq=2048):
    L = proj.shape[0]
    tq = min(tq, L)

    def body(q_ref, k_ref, v_ref, do_ref, dp_hbm, dq_ref, dk_ref, dv_ref):
        @pl.when(pl.program_id(1) == 0)
        def _():
            dk_ref[...] = jnp.zeros_like(dk_ref)
            dv_ref[...] = jnp.zeros_like(dv_ref)

        qs, p = _mem_probs(q_ref, k_ref)
        dob = do_ref[...].astype(BF16)
        dp = lax.dot_general(dob, v_ref[...].astype(BF16), _NT, preferred_element_type=F32)
        ds = p * (dp - jnp.sum(p * dp, axis=-1, keepdims=True))
        dsb = ds.astype(BF16)
        dq = jnp.dot(dsb, k_ref[...].astype(BF16), preferred_element_type=F32) * (HEAD_DIM ** -0.5)
        dq_ref[...] = dq.astype(BF16)
        dk_ref[...] += lax.dot_general(dsb, qs, _TN, preferred_element_type=F32)
        dv_ref[...] += lax.dot_general(p.astype(BF16), dob, _TN, preferred_element_type=F32)

    dproj, dk, dv = pl.pallas_call(
        body, name="mem_attn_bwd",
        out_shape=(jax.ShapeDtypeStruct(dproj.shape, dproj.dtype),
                   jax.ShapeDtypeStruct((N_MEM, MEM_WIDTH), F32),
                   jax.ShapeDtypeStruct((N_MEM, MEM_WIDTH), F32)),
        grid=(MEM_HEADS, L // tq),
        in_specs=[pl.BlockSpec((tq, HEAD_DIM), lambda h, i: (i, _MEM_Q_COL + h)),
                  pl.BlockSpec((N_MEM, HEAD_DIM), lambda h, i: (0, h)),
                  pl.BlockSpec((N_MEM, HEAD_DIM), lambda h, i: (0, MEM_HEADS + h)),
                  pl.BlockSpec((tq, HEAD_DIM), lambda h, i: (i, h)),
                  _ANY],
        out_specs=(pl.BlockSpec((tq, HEAD_DIM), lambda h, i: (i, _MEM_Q_COL + h)),
                   pl.BlockSpec((N_MEM, HEAD_DIM), lambda h, i: (0, h)),
                   pl.BlockSpec((N_MEM, HEAD_DIM), lambda h, i: (0, h))),
        input_output_aliases={4: 0},
        compiler_params=_params("parallel", "arbitrary"),
    )(proj, kvm, kvm, do, dproj)
    return dproj, jnp.concatenate([dk, dv], axis=1)


def _tile_cumsum(x, row, reverse):
    for sh in (1, 2, 4):
        if reverse:
            x = x + jnp.where(row < SUBLANES - sh, pltpu.roll(x, SUBLANES - sh, 0), 0.0)
        else:
            x = x + jnp.where(row >= sh, pltpu.roll(x, sh, 0), 0.0)
    return x


def _fgate_fwd(pre, b_pad):
    L = pre.shape[0]
    n8 = L // SUBLANES

    def body(p_ref, b_ref, o_ref):
        row = lax.broadcasted_iota(jnp.int32, (SUBLANES, LANES), 0)
        b = b_ref[...]

        def step(i, carry):
            x = p_ref[i] + b
            logf = jnp.minimum(x, 0.0) - jnp.log(1.0 + jnp.exp(-jnp.abs(x)))
            t = _tile_cumsum(logf, row, False) + carry
            o_ref[i] = t
            return t[SUBLANES - 1:SUBLANES, :]

        lax.fori_loop(0, n8, step, jnp.zeros((1, LANES), F32))

    out = pl.pallas_call(
        body, name="fgate_fwd", out_shape=jax.ShapeDtypeStruct((n8, SUBLANES, LANES), F32),
        compiler_params=_params(),
    )(pre.reshape(n8, SUBLANES, LANES), b_pad.reshape(1, LANES))
    return out.reshape(L, LANES)


def _fgate_bwd(dfcum, pre, b_pad):
    L = pre.shape[0]
    n8 = L // SUBLANES

    def body(d_ref, p_ref, b_ref, o_ref, s_ref):
        row = lax.broadcasted_iota(jnp.int32, (SUBLANES, LANES), 0)
        b = b_ref[...]

        def step(k, carry):
            c, acc = carry
            i = n8 - 1 - k
            t = _tile_cumsum(d_ref[i], row, True) + c
            dpre = t * _sigmoid(-(p_ref[i] + b))
            o_ref[i] = dpre
            return t[0:1, :], acc + dpre

        _, acc = lax.fori_loop(0, n8, step, (jnp.zeros((1, LANES), F32), jnp.zeros((SUBLANES, LANES), F32)))
        s_ref[...] = jnp.sum(acc, axis=0, keepdims=True)

    dpre, db = pl.pallas_call(
        body, name="fgate_bwd",
        out_shape=(jax.ShapeDtypeStruct((n8, SUBLANES, LANES), F32), jax.ShapeDtypeStruct((1, LANES), F32)),
        compiler_params=_params(),
    )(dfcum.reshape(n8, SUBLANES, LANES), pre.reshape(n8, SUBLANES, LANES), b_pad.reshape(1, LANES))
    return dpre.reshape(L, LANES), db


FOX_BLOCK = 512


def _fox_scores(qs, k, fk, diagonal):
    s = lax.dot_general(qs, k, _NT, preferred_element_type=F32) - fk
    if diagonal:
        row = lax.broadcasted_iota(jnp.int32, s.shape, 0)
        col = lax.broadcasted_iota(jnp.int32, s.shape, 1)
        s = jnp.where(row >= col, s, NEG_BIG)
    return s


def _fox_specs(tq, L):
    nq = L // tq
    return dict(
        rows=lambda off: pl.BlockSpec((tq, HEAD_DIM), lambda h, i: (i, off + h)),
        seq=lambda off: pl.BlockSpec((L, HEAD_DIM), lambda h, i: (0, off + h)),
        col=pl.BlockSpec((None, None, tq, 1), lambda h, i: (h, i, 0, 0)),
        col_all=pl.BlockSpec((None, nq, tq, 1), lambda h, i: (h, 0, 0, 0)),
        row=pl.BlockSpec((None, None, 1, tq), lambda h, i: (h, i, 0, 0)),
        row_all=pl.BlockSpec((None, nq, 1, tq), lambda h, i: (h, 0, 0, 0)))


FOX_FWD_HEADS = 2


def _fox_fwd(proj, kv, fk):
    L = proj.shape[0]
    tq = min(FOX_BLOCK, L)
    nq = L // tq
    nh = FOX_FWD_HEADS
    W = nh * HEAD_DIM

    def body(q_ref, k_ref, v_ref, fk_ref, o_ref, lse_ref, m_s, l_s, acc_s):
        qi = pl.program_id(1)
        cols = [slice(a * HEAD_DIM, (a + 1) * HEAD_DIM) for a in range(nh)]
        qs = [(q_ref[:, cs] * (HEAD_DIM ** -0.5)).astype(BF16) for cs in cols]
        m_s[...] = jnp.full_like(m_s, NEG_BIG)
        l_s[...] = jnp.zeros_like(l_s)
        acc_s[...] = jnp.zeros_like(acc_s)

        def block(j, diagonal):
            r0 = pl.multiple_of(j * tq, tq)
            for a, cs in enumerate(cols):
                s = _fox_scores(qs[a], k_ref[pl.ds(r0, tq), cs], fk_ref[a, j], diagonal)
                m_new = jnp.maximum(m_s[a], jnp.max(s, axis=-1, keepdims=True))
                alpha = jnp.exp(m_s[a] - m_new)
                p = jnp.exp(s - m_new)
                l_s[a] = alpha * l_s[a] + jnp.sum(p, axis=-1, keepdims=True)
                acc_s[a] = alpha * acc_s[a] + jnp.dot(p.astype(BF16), v_ref[pl.ds(r0, tq), cs],
                                                      preferred_element_type=F32)
                m_s[a] = m_new

        def below(j, carry):
            block(j, False)
            return carry

        lax.fori_loop(0, qi, below, 0)
        block(qi, True)
        for a, cs in enumerate(cols):
            o_ref[:, cs] = acc_s[a] / l_s[a]
            lse_ref[a] = m_s[a] + jnp.log(l_s[a])

    return pl.pallas_call(
        body, name="fox_fwd",
        out_shape=(jax.ShapeDtypeStruct((L, MAIN_WIDTH), F32),
                   jax.ShapeDtypeStruct((FOX_HEADS, nq, tq, 1), F32)),
        grid=(FOX_HEADS // nh, nq),
        in_specs=[pl.BlockSpec((tq, W), lambda h, i: (i, h)),
                  pl.BlockSpec((L, W), lambda h, i: (0, h)),
                  pl.BlockSpec((L, W), lambda h, i: (0, FOX_HEADS // nh + h)),
                  pl.BlockSpec((nh, nq, 1, tq), lambda h, i: (h, 0, 0, 0))],
        out_specs=(pl.BlockSpec((tq, W), lambda h, i: (i, h)),
                   pl.BlockSpec((nh, None, tq, 1), lambda h, i: (h, i, 0, 0))),
        scratch_shapes=[pltpu.VMEM((nh, tq, 1), F32), pltpu.VMEM((nh, tq, 1), F32),
                        pltpu.VMEM((nh, tq, HEAD_DIM), F32)],
        compiler_params=_params("parallel", "parallel"),
    )(proj, kv, kv, fk)


def _fox_bwd_dq(proj, kv, fk, lse, delta, datt, dproj):
    L = proj.shape[0]
    tq = min(FOX_BLOCK, L)
    nq = L // tq
    sp = _fox_specs(tq, L)

    def body(q_ref, k_ref, v_ref, fk_ref, lse_ref, dl_ref, do_ref, dp_hbm, dq_ref, df_ref, acc_s, df_s):
        qi = pl.program_id(1)
        qs = (q_ref[...] * (HEAD_DIM ** -0.5)).astype(BF16)
        dob = do_ref[...].astype(BF16)
        lse, dl = lse_ref[...], dl_ref[...]
        acc_s[...] = jnp.zeros_like(acc_s)
        df_s[...] = jnp.zeros_like(df_s)

        def block(j, diagonal):
            r0 = pl.multiple_of(j * tq, tq)
            k = k_ref[pl.ds(r0, tq), :]
            p = jnp.exp(_fox_scores(qs, k, fk_ref[j], diagonal) - lse)
            dp = lax.dot_general(dob, v_ref[pl.ds(r0, tq), :], _NT, preferred_element_type=F32)
            ds = p * (dp - dl)
            acc_s[...] += jnp.dot(ds.astype(BF16), k, preferred_element_type=F32)
            df_s[...] += jnp.sum(ds, axis=1, keepdims=True)

        def below(j, carry):
            block(j, False)
            return carry

        lax.fori_loop(0, qi, below, 0)
        block(qi, True)
        dq_ref[...] = (acc_s[...] * (HEAD_DIM ** -0.5)).astype(BF16)
        df_ref[...] = df_s[...]

    return pl.pallas_call(
        body, name="fox_bwd_dq",
        out_shape=(jax.ShapeDtypeStruct(dproj.shape, dproj.dtype),
                   jax.ShapeDtypeStruct((FOX_HEADS, nq, tq, 1), F32)),
        grid=(FOX_HEADS, nq),
        in_specs=[sp["rows"](0), sp["seq"](0), sp["seq"](FOX_HEADS), sp["row_all"],
                  sp["col"], sp["col"], sp["rows"](0), _ANY],
        out_specs=(sp["rows"](0), sp["col"]),
        input_output_aliases={7: 0},
        scratch_shapes=[pltpu.VMEM((tq, HEAD_DIM), F32), pltpu.VMEM((tq, 1), F32)],
        compiler_params=_params("parallel", "parallel"),
    )(proj, kv, kv, fk, lse, delta, datt, dproj)


def _fox_bwd_dkv(proj, kv, fk, lse, delta, datt):
    L = proj.shape[0]
    tq = min(FOX_BLOCK, L)
    nq = L // tq
    sp = _fox_specs(tq, L)

    def body(q_ref, k_ref, v_ref, fk_ref, lse_ref, dl_ref, do_ref,
             dk_ref, dv_ref, df_ref, dk_s, dv_s, df_s):
        ki = pl.program_id(1)
        k, v, fk = k_ref[...], v_ref[...], fk_ref[...]
        dk_s[...] = jnp.zeros_like(dk_s)
        dv_s[...] = jnp.zeros_like(dv_s)
        df_s[...] = jnp.zeros_like(df_s)

        def block(i, diagonal):
            r0 = pl.multiple_of(i * tq, tq)
            qs = (q_ref[pl.ds(r0, tq), :] * (HEAD_DIM ** -0.5)).astype(BF16)
            dob = do_ref[pl.ds(r0, tq), :].astype(BF16)
            p = jnp.exp(_fox_scores(qs, k, fk, diagonal) - lse_ref[i])
            dp = lax.dot_general(dob, v, _NT, preferred_element_type=F32)
            ds = p * (dp - dl_ref[i])
            dv_s[...] += lax.dot_general(p.astype(BF16), dob, _TN, preferred_element_type=F32)
            dk_s[...] += lax.dot_general(ds.astype(BF16), qs, _TN, preferred_element_type=F32)
            df_s[...] -= jnp.sum(ds, axis=0, keepdims=True)

        def above(i, carry):
            block(i, False)
            return carry

        block(ki, True)
        lax.fori_loop(ki + 1, nq, above, 0)
        dk_ref[...] = dk_s[...].astype(BF16)
        dv_ref[...] = dv_s[...].astype(BF16)
        df_ref[...] = df_s[...]

    return pl.pallas_call(
        body, name="fox_bwd_dkv",
        out_shape=(jax.ShapeDtypeStruct((L, MAIN_WIDTH), BF16),
                   jax.ShapeDtypeStruct((L, MAIN_WIDTH), BF16),
                   jax.ShapeDtypeStruct((FOX_HEADS, nq, 1, tq), F32)),
        grid=(FOX_HEADS, nq),
        in_specs=[sp["seq"](0), sp["rows"](0), sp["rows"](FOX_HEADS), sp["row"],
                  sp["col_all"], sp["col_all"], sp["seq"](0)],
        out_specs=(sp["rows"](0), sp["rows"](0), sp["row"]),
        scratch_shapes=[pltpu.VMEM((tq, HEAD_DIM), F32), pltpu.VMEM((tq, HEAD_DIM), F32),
                        pltpu.VMEM((1, tq), F32)],
        compiler_params=_params("parallel", "parallel"),
    )(proj, kv, kv, fk, lse, delta, datt)


def _pad_lanes(a):
    return jnp.pad(a, ((0, 0), (0, LANES - a.shape[1])))


def _mem_branch_fwd(mem, g, w_mk, proj, tag):
    memn = _rmsnorm_fwd(mem, g, name="mem_norm_" + tag, out_dtype=BF16)
    kvm = _mm(memn, w_mk, name="mem_kv_" + tag)
    return memn, kvm, _mem_attn_fwd(proj, kvm)


def _mem_branch_bwd(mem, g, w_mk, proj, memn, kvm, do_mem, dproj, tag):
    dproj, dkvm = _mem_attn_bwd(proj, kvm, do_mem, dproj)
    dkvm = dkvm.astype(BF16)
    dw_mk = _mm(memn, dkvm, ta=True, name="dw_mem_kv_" + tag, out_dtype=BF16)
    dmemn = _mm(dkvm, w_mk, tb=True, name="dmemn_" + tag)
    _, dg = _rmsnorm_bwd(mem, g, dmemn, name="mem_norm_bwd_" + tag, dx_dtype=BF16)
    return dproj, dw_mk, dg


def _local_step(x, mem, target, w, fetch=None, grads_ready=None):
    if grads_ready is None:
        grads_ready = lambda group, grads, token: token
    L = x.shape[0]
    g = {}
    w = dict(w)

    b_re_t = jnp.transpose(w["b_re"], (0, 2, 1))
    b_im_t = jnp.transpose(w["b_im"], (0, 2, 1))
    ar, ai, bbr_t, bbi_t = _s5_prep(w["lam_re"], w["lam_im"], w["log_step"], b_re_t, b_im_t)
    bmat, cmat = _s5_block_mats(bbr_t, bbi_t, w["c_re"], w["c_im"])
    a_rows = _s5_a_rows(ar, ai)

    hn0 = _rmsnorm_fwd(x, w["pre_norm_g"][0], name="pre_norm_0", out_dtype=BF16)
    if fetch is not None:
        w.update(fetch("a", hn0))
    proj_a = _mm(hn0, w["w_in_a"], name="in_proj_a")
    y, yg, xp = _s5_fwd(proj_a, bmat, cmat, a_rows, w["d_skip"])
    if fetch is not None:
        w.update(fetch("b", yg))
    t = _mm(yg, w["w_glu"], name="glu_proj")
    memn0, kvm0, om0 = _mem_branch_fwd(mem, w["mem_norm_g"][0], w["w_mem_kv"][0], proj_a, "0")
    cat0 = _gate_a_fwd(y, t, w["b_glu"], proj_a, om0)
    o0 = _mm(cat0, w["w_out"][0], name="out_proj_0")
    h1 = _rmsnorm_fwd(o0, w["post_norm_g"][0], res=x, name="post_norm_0")

    kv_in = _rmsnorm_fwd(h1, w["kv_norm_g"], name="kv_norm", out_dtype=BF16)
    if fetch is not None:
        w.update(fetch("c", kv_in))
    kv = _mm(kv_in, w["w_kv"], name="kv_proj", out_dtype=BF16)
    pre_f = _mm(kv_in, w["w_fgate"], name="fgate_proj")
    b_f = jnp.pad(w["b_fgate"], (0, LANES - FOX_HEADS))
    fcum = _fgate_fwd(pre_f, b_f)
    fc = jnp.transpose(fcum[:, :FOX_HEADS])
    tq = min(FOX_BLOCK, L)
    fk = fc.reshape(FOX_HEADS, L // tq, 1, tq)

    hn1 = _rmsnorm_fwd(h1, w["pre_norm_g"][1], name="pre_norm_1", out_dtype=BF16)
    proj_b = _mm(hn1, w["w_in_b"], name="in_proj_b")
    att, lse = _fox_fwd(proj_b, kv, fk)
    memn1, kvm1, om1 = _mem_branch_fwd(mem, w["mem_norm_g"][1], w["w_mem_kv"][1], proj_b, "1")
    cat1 = _gate_b_fwd(att, proj_b, om1)
    o1 = _mm(cat1, w["w_out"][1], name="out_proj_1")
    dh2, loss_row = _final_norm_loss(o1, w["post_norm_g"][1], h1, target)

    do1, dpost1 = _rmsnorm_bwd(o1, w["post_norm_g"][1], dh2, name="post_norm_bwd_1", dx_dtype=BF16)
    dcat1 = _mm(do1, w["w_out"][1], tb=True, name="dcat_1")
    g["w_out_1"] = _mm(cat1, do1, ta=True, name="dw_out_1", out_dtype=BF16)
    datt, dproj_b, dom1, delta = _gate_b_bwd(dcat1, att, proj_b, om1)
    dproj_b, g["w_mem_kv_1"], dmemg1 = _mem_branch_bwd(mem, w["mem_norm_g"][1], w["w_mem_kv"][1], proj_b,
                                                      memn1, kvm1, dom1, dproj_b, "1")
    delta = delta.reshape(lse.shape)
    dproj_b, dfq = _fox_bwd_dq(proj_b, kv, fk, lse, delta, datt, dproj_b)
    dk, dv, dfk = _fox_bwd_dkv(proj_b, kv, fk, lse, delta, datt)
    g["w_in_b"] = _mm(hn1, dproj_b, ta=True, name="dw_in_b", out_dtype=BF16, shards=N_CHIPS)
    dhn1 = _mm(dproj_b, w["w_in_b"], tb=True, name="dhn_1")

    dkv = jnp.concatenate([dk, dv], axis=1)
    g["w_kv"] = _mm(kv_in, dkv, ta=True, name="dw_kv", out_dtype=BF16, shards=N_CHIPS)
    dkv_in_a = _mm(dkv, w["w_kv"], tb=True, name="dkv_in_kv")
    dfcum = _pad_lanes(jnp.transpose(dfq.reshape(FOX_HEADS, L) + dfk.reshape(FOX_HEADS, L)))
    dpre_f, db_f = _fgate_bwd(dfcum, pre_f, b_f)
    g["b_fgate"] = db_f[0, :FOX_HEADS]
    g["w_fgate"] = _mm(kv_in, dpre_f, ta=True, name="dw_fgate")[:, :FOX_HEADS]
    dkv_in_b = _mm(dpre_f, w["w_fgate"], tb=True, name="dkv_in_fgate")
    dh1_kv, g["kv_norm_g"] = _rmsnorm_bwd(h1, w["kv_norm_g"], (dkv_in_a, dkv_in_b), name="kv_norm_bwd")
    dh1, dpre1 = _rmsnorm_bwd(h1, w["pre_norm_g"][1], dhn1, adds=(dh2, dh1_kv), name="pre_norm_bwd_1")
    dh1 = grads_ready("b", g, dh1)

    do0, dpost0 = _rmsnorm_bwd(o0, w["post_norm_g"][0], dh1, name="post_norm_bwd_0", dx_dtype=BF16)
    dcat0 = _mm(do0, w["w_out"][0], tb=True, name="dcat_0")
    g["w_out_0"] = _mm(cat0, do0, ta=True, name="dw_out_0", out_dtype=BF16)
    dproj_a, dt, dyg_a, dom0, db_glu = _gate_a_bwd(dcat0, y, t, w["b_glu"], proj_a, om0)
    g["b_glu"] = db_glu[0]
    g["w_glu"] = _mm(yg, dt, ta=True, name="dw_glu", out_dtype=BF16)
    dyg_b = _mm(dt, w["w_glu"], tb=True, name="dyg")
    dproj_a, g["w_mem_kv_0"], dmemg0 = _mem_branch_bwd(mem, w["mem_norm_g"][0], w["w_mem_kv"][0], proj_a,
                                                      memn0, kvm0, dom0, dproj_a, "0")
    dyg_b = grads_ready("a1", g, dyg_b)
    dproj_a, db_blk, dc_blk, da_rows, dd_skip = _s5_bwd(proj_a, dyg_a, dyg_b, y, xp, bmat, cmat, a_rows,
                                                        w["d_skip"], dproj_a)
    g["d_skip"] = dd_skip[0]
    g["w_in_a"] = _mm(hn0, dproj_a, ta=True, name="dw_in_a", out_dtype=BF16, shards=N_CHIPS)
    dproj_a = grads_ready("a2", g, dproj_a)
    dhn0 = _mm(dproj_a, w["w_in_a"], tb=True, name="dhn_0")
    grad_x, dpre0 = _rmsnorm_bwd(x, w["pre_norm_g"][0], dhn0, adds=(dh1,), name="pre_norm_bwd_0")

    dbb = _s5_block_diag(db_blk)
    dcc = _s5_block_diag(dc_blk)
    g["c_re"], g["c_im"] = dcc[0], -dcc[1]
    d_ar = da_rows[:, 0, :STATE_COLS].reshape(SSM_GROUPS, SSM_STATE)
    d_ai = da_rows[:, 0, STATE_COLS:].reshape(SSM_GROUPS, SSM_STATE)
    dlr, dli, dls, dbr_t, dbi_t = _s5_prep_bwd(w["lam_re"], w["lam_im"], w["log_step"], b_re_t, b_im_t,
                                               d_ar, d_ai, dbb[0], dbb[1])
    g["lam_re"], g["lam_im"], g["log_step"] = dlr, dli, dls[:, 0]
    g["b_re"] = jnp.transpose(dbr_t, (0, 2, 1))
    g["b_im"] = jnp.transpose(dbi_t, (0, 2, 1))
    g["pre_norm_g"] = jnp.stack([dpre0, dpre1])
    g["post_norm_g"] = jnp.stack([dpost0, dpost1])
    g["mem_norm_g"] = jnp.stack([dmemg0, dmemg1])
    return loss_row, grad_x, g


_MESH = pl.DeviceIdType.MESH
_ANY = pl.BlockSpec(memory_space=pl.ANY)


def _place():
    x, y, c = lax.axis_index("x"), lax.axis_index("y"), lax.axis_index("c")
    chips = [(1 - x, y), (x, 1 - y), (1 - x, 1 - y)]
    return x, y, c, chips


_HBM = pl.BlockSpec(memory_space=pltpu.HBM)
_SEM = pl.BlockSpec(memory_space=pltpu.SEMAPHORE)
_SIDE = pltpu.SideEffectType.DATAFLOW_SIDE_EFFECTING


def _in_hbm(a):
    return pltpu.with_memory_space_constraint(a, pltpu.HBM)


def _hbm_like(a):
    return pltpu.HBM(a.shape, a.dtype)


def _ici_copies(srcs, lands, send_sem, recv_sem, src_at, dst_at, wait_at):
    x, y, c, chips = _place()
    start, wait = [], []
    for i in range(len(srcs)):
        for k, (cx, cy) in enumerate(chips):
            sem = dict(send_sem=send_sem.at[3 * i + k], recv_sem=recv_sem.at[3 * i + k],
                       device_id=(cx, cy, c), device_id_type=_MESH)
            src = src_at(srcs[i], 2 * cx + cy, c)
            start.append(pltpu.make_async_remote_copy(src_ref=src, dst_ref=dst_at(lands[i], 2 * x + y, k, c), **sem))
            wait.append(pltpu.make_async_remote_copy(src_ref=src, dst_ref=wait_at(lands[i], 2 * cx + cy, k, c), **sem))
    return start, wait


_BLOCK_ROUTE = (lambda s, j, c: s, lambda l, me, k, c: l.at[me, c], lambda l, j, k, c: l.at[j, c])


def _ici_start(srcs, lands, token, route, *, name):
    n = len(srcs)

    def body(*refs):
        start, _ = _ici_copies(refs[:n], refs[n:2 * n], refs[2 * n + 1], refs[2 * n + 2], *route)
        for cp in start:
            cp.start()

    sems = pltpu.SemaphoreType.DMA((3 * n,))
    outs = pl.pallas_call(
        body, name=name,
        out_shape=(sems, sems, *[_hbm_like(a) for a in srcs], *[_hbm_like(a) for a in lands], _hbm_like(token)),
        in_specs=[_HBM] * (2 * n + 1), out_specs=(_SEM, _SEM, *[_HBM] * (2 * n + 1)),
        input_output_aliases={i: 2 + i for i in range(2 * n + 1)},
        compiler_params=pltpu.CompilerParams(has_side_effects=_SIDE),
    )(*[_in_hbm(a) for a in srcs], *[_in_hbm(a) for a in lands], _in_hbm(token))
    return (outs[0], outs[1], list(outs[2:2 + n]), list(outs[2 + n:2 + 2 * n])), outs[2 + 2 * n]


def _ici_wait(handle, after, route, *, name):
    send_sem, recv_sem, srcs, lands = handle
    n = len(srcs)
    after = list(after) if isinstance(after, (list, tuple)) else [after]

    def body(*refs):
        _, wait = _ici_copies(refs[:n], refs[n:2 * n], refs[2 * n], refs[2 * n + 1], *route)
        for cp in wait:
            cp.wait_send()
            cp.wait_recv()

    outs = pl.pallas_call(
        body, name=name,
        out_shape=(*[_hbm_like(a) for a in srcs], *[_hbm_like(a) for a in lands]),
        in_specs=[_HBM] * (2 * n) + [_SEM, _SEM] + [_ANY] * len(after), out_specs=tuple([_HBM] * (2 * n)),
        input_output_aliases={i: i for i in range(2 * n)},
        compiler_params=pltpu.CompilerParams(has_side_effects=_SIDE),
    )(*srcs, *lands, send_sem, recv_sem, *after)
    return list(outs[:n]), list(outs[n:])


_GATHER_ROUTE = (lambda s, j, c: s.at[c], lambda l, me, k, c: l.at[me, c], lambda l, j, k, c: l.at[j, c])
_SCATTER_ROUTE = (lambda s, j, c: s.at[j], lambda l, me, k, c: l.at[k], lambda l, j, k, c: l.at[k])


def _gather_forward(lands, tag, own=False):
    n = len(lands)
    m = 4 if own else 3

    def body(*refs):
        ins, outs = refs[:n], refs[n:2 * n]
        send_sem, recv_sem = refs[2 * n:]
        x, y, c, chips = _place()
        slots = [2 * cx + cy for cx, cy in chips] + [2 * x + y]

        def copy(i, k, half):
            return pltpu.make_async_remote_copy(
                src_ref=ins[i].at[slots[k], half], dst_ref=outs[i].at[slots[k], half],
                send_sem=send_sem.at[m * i + k], recv_sem=recv_sem.at[m * i + k],
                device_id=(x, y, 1 - c), device_id_type=_MESH)

        copies = [copy(i, k, c) for i in range(n) for k in range(m)]
        for cp in copies:
            cp.start()
        for i in range(n):
            for k in range(m):
                copy(i, k, 1 - c).wait_recv()
        for cp in copies:
            cp.wait_send()

    return pl.pallas_call(
        body, name="gather_forward_to_sibling_" + tag,
        out_shape=[jax.ShapeDtypeStruct(a.shape, a.dtype) for a in lands],
        in_specs=[_ANY] * n, out_specs=[_ANY] * n,
        input_output_aliases={i: i for i in range(n)},
        scratch_shapes=[pltpu.SemaphoreType.DMA((m * n,)), pltpu.SemaphoreType.DMA((m * n,))],
    )(*lands)


def _swap_halves(grads, tag):
    n = len(grads)

    def body(*refs):
        ins, outs = refs[:n], refs[n:2 * n]
        send_sem, recv_sem = refs[2 * n:]
        x, y, c, _ = _place()
        copies = [pltpu.make_async_remote_copy(
            src_ref=ins[i].at[:, 1 - c], dst_ref=outs[i],
            send_sem=send_sem.at[i], recv_sem=recv_sem.at[i],
            device_id=(x, y, 1 - c), device_id_type=_MESH) for i in range(n)]
        for cp in copies:
            cp.start()
        for cp in copies:
            cp.wait()

    return pl.pallas_call(
        body, name="grad_swap_halves_" + tag,
        out_shape=[jax.ShapeDtypeStruct((N_CHIPS,) + g.shape[2:], g.dtype) for g in grads],
        in_specs=[_ANY] * n, out_specs=[_ANY] * n,
        scratch_shapes=[pltpu.SemaphoreType.DMA((n,)), pltpu.SemaphoreType.DMA((n,))],
    )(*grads)


def _sum_rows(h, C):
    return max(d for d in range(SUBLANES, h + 1, SUBLANES) if h % d == 0 and d * C <= 1 << 20)


def _pair_sum(g, r, c_idx, *, name):
    _, _, h, C = g.shape
    tr = _sum_rows(h, C)

    def body(c_ref, g_ref, r_ref, o_ref):
        o_ref[...] = (g_ref[...].astype(F32) + r_ref[...].astype(F32)).astype(o_ref.dtype)

    return pl.pallas_call(
        body, name=name, out_shape=jax.ShapeDtypeStruct((N_CHIPS, h, C), g.dtype),
        grid_spec=pltpu.PrefetchScalarGridSpec(
            num_scalar_prefetch=1, grid=(N_CHIPS, h // tr),
            in_specs=[pl.BlockSpec((None, None, tr, C), lambda j, i, s: (j, s[0], i, 0)),
                      pl.BlockSpec((None, tr, C), lambda j, i, s: (j, i, 0))],
            out_specs=pl.BlockSpec((None, tr, C), lambda j, i, s: (j, i, 0))),
        compiler_params=_params("parallel", "parallel"),
    )(c_idx, g, r)


def _owner_sum(s, r, jc_idx, *, name):
    _, h, C = s.shape
    tr = _sum_rows(h, C)

    def body(jc_ref, s_ref, r_ref, o_ref):
        acc = s_ref[...].astype(F32)
        for k in range(3):
            acc = acc + r_ref[k].astype(F32)
        o_ref[...] = acc

    return pl.pallas_call(
        body, name=name, out_shape=jax.ShapeDtypeStruct((2, h, C), F32),
        grid_spec=pltpu.PrefetchScalarGridSpec(
            num_scalar_prefetch=1, grid=(h // tr,),
            in_specs=[pl.BlockSpec((None, tr, C), lambda i, s: (s[0], i, 0)),
                      pl.BlockSpec((3, tr, C), lambda i, s: (0, i, 0))],
            out_specs=pl.BlockSpec((None, tr, C), lambda i, s: (s[1], i, 0))),
        compiler_params=_params("parallel"),
    )(jc_idx, s, r)


def _share_with_sibling(bufs, tag):
    n = len(bufs)

    def body(*refs):
        ins, outs = refs[:n], refs[n:2 * n]
        send_sem, recv_sem = refs[2 * n:]
        x, y, c, _ = _place()

        def copy(i, half):
            return pltpu.make_async_remote_copy(
                src_ref=ins[i].at[half], dst_ref=outs[i].at[half],
                send_sem=send_sem.at[i], recv_sem=recv_sem.at[i],
                device_id=(x, y, 1 - c), device_id_type=_MESH)

        copies = [copy(i, c) for i in range(n)]
        for cp in copies:
            cp.start()
        for i in range(n):
            copy(i, 1 - c).wait_recv()
        for cp in copies:
            cp.wait_send()

    return pl.pallas_call(
        body, name="grad_share_with_sibling_" + tag,
        out_shape=[jax.ShapeDtypeStruct(b.shape, b.dtype) for b in bufs],
        in_specs=[_ANY] * n, out_specs=[_ANY] * n,
        input_output_aliases={i: i for i in range(n)},
        scratch_shapes=[pltpu.SemaphoreType.DMA((n,)), pltpu.SemaphoreType.DMA((n,))],
    )(*bufs)


def _chip_sums(grads, c_idx, tag):
    views = [g.reshape(N_CHIPS, 2, g.shape[1] // 2, g.shape[2]) for g in grads]
    arrived = _swap_halves(views, tag)
    return [_pair_sum(v, r, c_idx, name=f"grad_pair_sum_{tag}_{i}") for i, (v, r) in enumerate(zip(views, arrived))]


def _owner_totals(sums, arrived, jc_idx, tag):
    halves = [_owner_sum(s, r, jc_idx, name=f"grad_owner_sum_{tag}_{i}") for i, (s, r) in enumerate(zip(sums, arrived))]
    return [f.reshape(-1, f.shape[2]) for f in _share_with_sibling(halves, tag)]


def _sum_devices(blocks):
    R = blocks.shape[2]
    tr = _sum_rows(R, 2 * N_CHIPS * LANES)

    def body(b_ref, o_ref):
        acc = b_ref[0, 0]
        for d in range(1, 2 * N_CHIPS):
            acc = acc + b_ref[d // 2, d % 2]
        o_ref[...] = acc

    return pl.pallas_call(
        body, name="sum_small_over_devices", out_shape=jax.ShapeDtypeStruct((R, LANES), F32),
        grid=(R // tr,),
        in_specs=[pl.BlockSpec((N_CHIPS, 2, tr, LANES), lambda i: (0, 0, i, 0))],
        out_specs=pl.BlockSpec((tr, LANES), lambda i: (i, 0)),
        compiler_params=_params("parallel"),
    )(blocks)


def _adamw(w, g, m, v, *, name):
    R, C = w.shape
    whole_fits = 7 * 2 * R * C * 4 <= VMEM_LIMIT_BYTES // 2
    tr = R if whole_fits else next(c for c in (256, 192, 128, 64, 32, 16, 8) if R % c == 0)

    def body(w_ref, g_ref, m_ref, v_ref, d_ref, nm_ref, nv_ref):
        g = g_ref[...]
        m = ADAM_B1 * m_ref[...] + (1.0 - ADAM_B1) * g
        v = ADAM_B2 * v_ref[...] + (1.0 - ADAM_B2) * (g * g)
        nm_ref[...] = m
        nv_ref[...] = v
        m_hat = m / (1.0 - ADAM_B1 ** ADAM_STEP)
        v_hat = v / (1.0 - ADAM_B2 ** ADAM_STEP)
        d_ref[...] = -ADAM_LR * (m_hat / (jnp.sqrt(v_hat) + ADAM_EPS) + ADAM_WD * w_ref[...])

    blk = pl.BlockSpec((tr, C), lambda i: (i, 0))
    sds = jax.ShapeDtypeStruct((R, C), F32)
    return pl.pallas_call(
        body, name=name, out_shape=(sds, sds, sds), grid=(R // tr,),
        in_specs=[blk] * 4, out_specs=(blk, blk, blk),
        compiler_params=_params("parallel"),
    )(w, g, m, v)


_TILE = SUBLANES * LANES


def _pack(arrays):
    rows = []
    for a in arrays:
        flat = a.reshape(-1)
        flat = jnp.pad(flat, (0, (-flat.shape[0]) % _TILE))
        rows.append(flat.reshape(-1, LANES))
    return jnp.concatenate(rows, axis=0)


def _unpack(buf, shapes):
    out, r = [], 0
    for s in shapes:
        size = math.prod(s)
        nr = -(-size // _TILE) * SUBLANES
        out.append(buf[r:r + nr].reshape(-1)[:size].reshape(s))
        r += nr
    return out


_BIG = ("w_in_a", "w_glu", "w_kv", "w_in_b", "w_mem_kv", "w_out")
_REPLICATED = ("pre_norm_g", "post_norm_g", "lam_re", "lam_im", "log_step", "b_re", "b_im", "c_re", "c_im",
               "kv_norm_g", "b_fgate", "mem_norm_g")
_SHARDED_SMALL = ("d_skip", "b_glu", "w_fgate")
_WEIGHTS = ("pre_norm_g", "post_norm_g", "w_in_a", "lam_re", "lam_im", "log_step", "b_re", "b_im", "c_re",
            "c_im", "d_skip", "w_glu", "b_glu", "kv_norm_g", "w_kv", "w_fgate", "b_fgate", "w_in_b",
            "mem_norm_g", "w_mem_kv", "w_out")


def _halves(a):
    return a.reshape(2, a.shape[0] // 2, a.shape[1])


def _unhalve(a):
    return a.reshape(N_CHIPS, 2 * a.shape[2], a.shape[3])


def _columns(a):
    return jnp.transpose(a, (1, 0, 2)).reshape(a.shape[1], N_CHIPS * a.shape[2])


def kernel(x, mem, pre_norm_g, post_norm_g, w_in_a, lam_re, lam_im, log_step, b_re, b_im, c_re, c_im, d_skip, w_glu, b_glu, kv_norm_g, w_kv, w_fgate, b_fgate, w_in_b, mem_norm_g, w_mem_kv, w_out, loss_target, m_pre_norm_g, m_post_norm_g, m_w_in_a, m_lam_re, m_lam_im, m_log_step, m_b_re, m_b_im, m_c_re, m_c_im, m_d_skip, m_w_glu, m_b_glu, m_kv_norm_g, m_w_kv, m_w_fgate, m_b_fgate, m_w_in_b, m_mem_norm_g, m_w_mem_kv, m_w_out, v_pre_norm_g, v_post_norm_g, v_w_in_a, v_lam_re, v_lam_im, v_log_step, v_b_re, v_b_im, v_c_re, v_c_im, v_d_skip, v_w_glu, v_b_glu, v_kv_norm_g, v_w_kv, v_w_fgate, v_b_fgate, v_w_in_b, v_mem_norm_g, v_w_mem_kv, v_w_out):
    a = dict(locals())
    xi, yi, ci = lax.axis_index("x"), lax.axis_index("y"), lax.axis_index("c")
    chip = 2 * xi + yi
    c_idx = jnp.reshape(ci, (1,)).astype(jnp.int32)
    jc_idx = jnp.stack([chip, ci]).astype(jnp.int32)

    vec = jnp.zeros((2 * SUBLANES, MAIN_WIDTH // N_CHIPS), F32)
    vec = vec.at[0].set(a["d_skip"][0]).at[1].set(a["b_glu"][0])
    def own_slot(gathered, parts):
        return [lax.dynamic_update_index_in_dim(g, p, chip, 0) for g, p in zip(gathered, parts)]

    parts_a = [_halves(a["w_in_a"][0].astype(BF16)), _halves(vec)]
    parts_b = [_halves(a["w_glu"][0].astype(BF16)), _halves(a["w_mem_kv"].reshape(-1, 2 * MEM_WIDTH).astype(BF16)),
               _halves(a["w_out"].reshape(-1, D_MODEL).astype(BF16))]
    parts_c = [_halves(a["w_kv"].astype(BF16)), _halves(_pad_lanes(a["w_fgate"]).astype(BF16)),
               _halves(a["w_in_b"][0].astype(BF16))]
    travelling, token = {}, a["pre_norm_g"]
    for tag, parts in (("a", parts_a), ("b", parts_b), ("c", parts_c)):
        lands = [lax.empty((N_CHIPS,) + p.shape, p.dtype) for p in parts]
        travelling[tag], token = _ici_start(parts, lands, token, _GATHER_ROUTE, name=f"gather_{tag}_start")

    def fetch(tag, after):
        parts, lands = _ici_wait(travelling[tag], after, _GATHER_ROUTE, name=f"gather_{tag}_wait")
        full = own_slot(_gather_forward(lands, tag), parts)
        if tag == "a":
            w_in_a, vecs = full
            return dict(w_in_a=_columns(_unhalve(w_in_a)), d_skip=vecs[:, 0, 0, :].reshape(MAIN_WIDTH),
                        b_glu=vecs[:, 0, 1, :].reshape(MAIN_WIDTH))
        if tag == "b":
            w_glu, w_mk, w_out = full
            return dict(w_glu=w_glu.reshape(MAIN_WIDTH, MAIN_WIDTH),
                        w_mem_kv=jnp.transpose(w_mk, (1, 0, 2, 3)).reshape(2, D_MODEL, 2 * MEM_WIDTH),
                        w_out=jnp.transpose(w_out, (1, 0, 2, 3)).reshape(2, D_MODEL, D_MODEL))
        w_kv, w_fg, w_in_b = full
        return dict(w_kv=_columns(_unhalve(w_kv)), w_fgate=w_fg.reshape(D_MODEL, LANES),
                    w_in_b=_columns(_unhalve(w_in_b)))

    w = dict(
        pre_norm_g=token, post_norm_g=a["post_norm_g"], mem_norm_g=a["mem_norm_g"],
        kv_norm_g=a["kv_norm_g"], b_fgate=a["b_fgate"],
        lam_re=a["lam_re"][0], lam_im=a["lam_im"][0], log_step=a["log_step"][0],
        b_re=a["b_re"][0], b_im=a["b_im"][0], c_re=a["c_re"][0], c_im=a["c_im"][0])

    sent = {}

    def grads_ready(tag, g, token):
        big = {"b": lambda: [g["w_kv"], g["w_in_b"], g["w_mem_kv_1"].reshape(N_CHIPS, -1, 2 * MEM_WIDTH),
                             g["w_out_1"].reshape(N_CHIPS, -1, D_MODEL)],
               "a1": lambda: [g["w_glu"].reshape(N_CHIPS, -1, MAIN_WIDTH),
                              g["w_mem_kv_0"].reshape(N_CHIPS, -1, 2 * MEM_WIDTH),
                              g["w_out_0"].reshape(N_CHIPS, -1, D_MODEL)],
               "a2": lambda: [g["w_in_a"]]}[tag]()
        sums = _chip_sums(big, c_idx, tag)
        lands = [lax.empty((3,) + s.shape[1:], s.dtype) for s in sums]
        sent[tag], token = _ici_start(sums, lands, token, _SCATTER_ROUTE, name=f"grad_send_{tag}_start")
        return token

    loss_row, grad_x, g = _local_step(a["x"][0], a["mem"][0], a["loss_target"][0], w, fetch, grads_ready)

    small_names = _REPLICATED + _SHARDED_SMALL
    pack = _pack([g[n] for n in small_names])
    blocks = lax.empty((N_CHIPS, 2) + pack.shape, F32)
    small_sent, loss_row = _ici_start([pack], [blocks], loss_row, _BLOCK_ROUTE, name="small_sums_start")
    loss = lax.psum(jnp.sum(loss_row), MESH_AXES)

    def totals(tag, after):
        sums, arrived = _ici_wait(sent[tag], after, _SCATTER_ROUTE, name=f"grad_send_{tag}_wait")
        return _owner_totals(sums, arrived, jc_idx, tag)

    r_kv, r_in_b, r_mk1, r_out1 = totals("b", grad_x)
    r_glu, r_mk0, r_out0 = totals("a1", r_out1)
    (r_in_a,) = totals("a2", r_out0)
    grads = {"w_in_a": r_in_a[None], "w_glu": r_glu[None], "w_kv": r_kv, "w_in_b": r_in_b[None],
             "w_mem_kv": jnp.stack([r_mk0, r_mk1]), "w_out": jnp.stack([r_out0, r_out1])}

    delta, new_m, new_v = {}, {}, {}
    for n in _BIG:
        shape = a[n].shape
        d2 = (-1, shape[-1])
        d, m, v = _adamw(a[n].reshape(d2), grads[n].reshape(d2), a["m_" + n].reshape(d2),
                         a["v_" + n].reshape(d2), name="adamw_" + n)
        delta[n], new_m[n], new_v[n] = d.reshape(shape), m.reshape(shape), v.reshape(shape)

    (pack,), (blocks,) = _ici_wait(small_sent, [delta[n] for n in _BIG], _BLOCK_ROUTE, name="small_sums_wait")
    blocks = lax.dynamic_update_slice(blocks, pack[None, None], (chip, ci, 0, 0))
    (blocks,) = _gather_forward([blocks], "small", own=True)
    small = dict(zip(small_names, _unpack(_sum_devices(blocks), [g[n].shape for n in small_names])))
    for n in _REPLICATED:
        grads[n] = small[n].reshape(a[n].shape)
    nd = MAIN_WIDTH // N_CHIPS
    grads["d_skip"] = lax.dynamic_slice(small["d_skip"], (chip * nd,), (nd,))[None]
    grads["b_glu"] = lax.dynamic_slice(small["b_glu"], (chip * nd,), (nd,))[None]
    nf = D_MODEL // N_CHIPS
    grads["w_fgate"] = lax.dynamic_slice(small["w_fgate"], (chip * nf, 0), (nf, FOX_HEADS))

    shapes = [a[n].shape for n in small_names]
    d, m, v = _adamw(_pack([a[n] for n in small_names]), _pack([grads[n] for n in small_names]),
                     _pack([a["m_" + n] for n in small_names]), _pack([a["v_" + n] for n in small_names]),
                     name="adamw_small")
    for n, dd, mm, vv in zip(small_names, _unpack(d, shapes), _unpack(m, shapes), _unpack(v, shapes)):
        delta[n], new_m[n], new_v[n] = dd, mm, vv

    return (loss, grad_x[None], *[grads[n] for n in _WEIGHTS], *[delta[n] for n in _WEIGHTS],
            *[new_m[n] for n in _WEIGHTS], *[new_v[n] for n in _WEIGHTS])
```

```python
import functools
import math

import jax
import jax.numpy as jnp
from jax import lax
from jax.experimental import pallas as pl
from jax.experimental.pallas import tpu as pltpu

F32 = jnp.float32
BF16 = jnp.bfloat16

D_MODEL = 2048
N_MEM = 256
MAIN_WIDTH = 1536
MEM_WIDTH = 512
IN_WIDTH = 2 * MAIN_WIDTH + 2 * MEM_WIDTH
HEAD_DIM = 128
FOX_HEADS = MAIN_WIDTH // HEAD_DIM
MEM_HEADS = MEM_WIDTH // HEAD_DIM
SSM_GROUP = 16
SSM_GROUPS = MAIN_WIDTH // SSM_GROUP
SSM_STATE = 64
GROUPS_PER_BLOCK = 8
SSM_BLOCKS = SSM_GROUPS // GROUPS_PER_BLOCK
STATE_COLS = GROUPS_PER_BLOCK * SSM_STATE
EPS = 1e-6
ADAM_LR = 0.001
ADAM_B1 = 0.9
ADAM_B2 = 0.999
ADAM_EPS = 1e-08
ADAM_WD = 0.01
ADAM_STEP = 10
N_CHIPS = 4
LANES = 128
SUBLANES = 8
VMEM_LIMIT_BYTES = 56 * 1024 * 1024
NEG_BIG = -1e30
MESH_AXES = ("x", "y", "c")


def _params(*sem):
    return pltpu.CompilerParams(dimension_semantics=sem if sem else None,
                                vmem_limit_bytes=VMEM_LIMIT_BYTES)


def _sigmoid(x):
    return 1.0 / (1.0 + jnp.exp(-x))


def _gelu(x):
    c = math.sqrt(2.0 / math.pi)
    return 0.5 * x * (1.0 + jnp.tanh(c * (x + 0.044715 * (x * x * x))))


def _gelu_grad(x):
    c = math.sqrt(2.0 / math.pi)
    t = jnp.tanh(c * (x + 0.044715 * (x * x * x)))
    return 0.5 * (1.0 + t) + 0.5 * x * (1.0 - t * t) * (c * (1.0 + 3.0 * 0.044715 * (x * x)))


def _silu_and_grad(z):
    s = _sigmoid(z)
    return z * s, s * (1.0 + z * (1.0 - s))


_TILE_CHOICES = (2048, 1024, 768, 512, 384, 256, LANES)


def _tile(n, cap):
    return next(c for c in _TILE_CHOICES if c <= cap and n % c == 0)


def _mm(a, b, *, name, ta=False, tb=False, out_dtype=F32, shards=1, tm=1024, tn=1024, tk=2048):
    if ta:
        K, M = a.shape
    else:
        M, K = a.shape
    if tb:
        N, kb = b.shape
    else:
        kb, N = b.shape
    assert K == kb, (a.shape, b.shape)
    ns = N // shards
    tm, tn, tk = _tile(M, tm), _tile(ns, tn), _tile(K, tk)
    assert M % tm == 0 and ns % tn == 0 and K % tk == 0 and N % shards == 0
    nk = K // tk
    dn = (((0 if ta else 1,), (1 if tb else 0,)), ((), ()))

    def body(a_ref, b_ref, o_ref, acc_ref):
        k = pl.program_id(2)

        @pl.when(k == 0)
        def _():
            acc_ref[...] = jnp.zeros_like(acc_ref)

        acc_ref[...] += lax.dot_general(a_ref[...].astype(BF16), b_ref[...].astype(BF16), dn,
                                        preferred_element_type=F32)

        @pl.when(k == nk - 1)
        def _():
            o_ref[...] = acc_ref[...].astype(o_ref.dtype)

    a_spec = (pl.BlockSpec((tk, tm), lambda i, j, k: (k, i)) if ta
              else pl.BlockSpec((tm, tk), lambda i, j, k: (i, k)))
    b_spec = (pl.BlockSpec((tn, tk), lambda i, j, k: (j, k)) if tb
              else pl.BlockSpec((tk, tn), lambda i, j, k: (k, j)))
    if shards == 1:
        out_shape = jax.ShapeDtypeStruct((M, N), out_dtype)
        o_spec = pl.BlockSpec((tm, tn), lambda i, j, k: (i, j))
    else:
        nb = ns // tn
        out_shape = jax.ShapeDtypeStruct((shards, M, ns), out_dtype)
        o_spec = pl.BlockSpec((None, tm, tn), lambda i, j, k: (j // nb, i, j % nb))
    return pl.pallas_call(
        body, name=name, out_shape=out_shape,
        grid=(M // tm, N // tn, nk),
        in_specs=[a_spec, b_spec], out_specs=o_spec,
        scratch_shapes=[pltpu.VMEM((tm, tn), F32)],
        compiler_params=_params("parallel", "parallel", "arbitrary"),
    )(a, b)


def _rmsnorm_fwd(x, g, *, name, res=None, out_dtype=F32, tr=256):
    L, D = x.shape
    tr = min(tr, L)
    has_res = res is not None

    def body(*refs):
        if has_res:
            x_ref, g_ref, r_ref, o_ref = refs
        else:
            x_ref, g_ref, o_ref = refs
        xf = x_ref[...]
        r = lax.rsqrt(jnp.mean(xf * xf, axis=-1, keepdims=True) + EPS)
        y = xf * r * g_ref[...]
        if has_res:
            y = r_ref[...] + y
        o_ref[...] = y.astype(o_ref.dtype)

    row = pl.BlockSpec((tr, D), lambda i: (i, 0))
    vec = pl.BlockSpec((1, D), lambda i: (0, 0))
    ins = [x, g.reshape(1, D)] + ([res] if has_res else [])
    return pl.pallas_call(
        body, name=name, out_shape=jax.ShapeDtypeStruct((L, D), out_dtype),
        grid=(L // tr,), in_specs=[row, vec] + ([row] if has_res else []), out_specs=row,
        compiler_params=_params("parallel"),
    )(*ins)


def _rmsnorm_bwd(x, g, dy, *, name, adds=(), dx_dtype=F32, tr=256):
    L, D = x.shape
    tr = min(tr, L)
    dys = dy if isinstance(dy, tuple) else (dy,)
    n_dy, n_add = len(dys), len(adds)

    def body(*refs):
        x_ref, g_ref = refs[:2]
        dy_refs = refs[2:2 + n_dy]
        add_refs = refs[2 + n_dy:2 + n_dy + n_add]
        dx_ref, dg_ref = refs[2 + n_dy + n_add:]
        xf = x_ref[...]
        dyf = dy_refs[0][...].astype(F32)
        for d_ref in dy_refs[1:]:
            dyf = dyf + d_ref[...].astype(F32)
        r = lax.rsqrt(jnp.mean(xf * xf, axis=-1, keepdims=True) + EPS)
        gy = dyf * g_ref[...]
        c = jnp.mean(xf * gy, axis=-1, keepdims=True) * (r * r * r)
        dx = gy * r - xf * c
        for a_ref in add_refs:
            dx = dx + a_ref[...].astype(F32)
        dx_ref[...] = dx.astype(dx_ref.dtype)

        @pl.when(pl.program_id(0) == 0)
        def _():
            dg_ref[...] = jnp.zeros_like(dg_ref)

        dg_ref[...] += jnp.sum(dyf * xf * r, axis=0, keepdims=True)

    row = pl.BlockSpec((tr, D), lambda i: (i, 0))
    vec = pl.BlockSpec((1, D), lambda i: (0, 0))
    dx, dg = pl.pallas_call(
        body, name=name,
        out_shape=(jax.ShapeDtypeStruct((L, D), dx_dtype), jax.ShapeDtypeStruct((1, D), F32)),
        grid=(L // tr,), in_specs=[row, vec] + [row] * (n_dy + n_add), out_specs=(row, vec),
        compiler_params=_params("arbitrary"),
    )(x, g.reshape(1, D), *dys, *adds)
    return dx, dg.reshape(D)


def _final_norm_loss(o, g, res, target, *, tr=256):
    L, D = o.shape
    tr = min(tr, L)

    def body(o_ref, g_ref, r_ref, t_ref, dh_ref, loss_ref):
        xf = o_ref[...]
        r = lax.rsqrt(jnp.mean(xf * xf, axis=-1, keepdims=True) + EPS)
        e = (r_ref[...] + xf * r * g_ref[...]) - t_ref[...]
        dh_ref[...] = e * (1.0 / D)

        @pl.when(pl.program_id(0) == 0)
        def _():
            loss_ref[...] = jnp.zeros_like(loss_ref)

        loss_ref[...] += jnp.sum(e * e, axis=0, keepdims=True) * (0.5 / D)

    row = pl.BlockSpec((tr, D), lambda i: (i, 0))
    vec = pl.BlockSpec((1, D), lambda i: (0, 0))
    dh, lp = pl.pallas_call(
        body, name="post_norm_1_loss",
        out_shape=(jax.ShapeDtypeStruct((L, D), F32), jax.ShapeDtypeStruct((1, D), F32)),
        grid=(L // tr,), in_specs=[row, vec, row, row], out_specs=(row, vec),
        compiler_params=_params("arbitrary"),
    )(o, g.reshape(1, D), res, target)
    return dh, lp


def _s5_coeffs(lr, li, ls):
    dt = jnp.exp(ls)
    mag = jnp.exp(lr * dt)
    ar = mag * jnp.cos(li * dt)
    ai = mag * jnp.sin(li * dt)
    den = lr * lr + li * li
    cr = ((ar - 1.0) * lr + ai * li) / den
    ci = (ai * lr - (ar - 1.0) * li) / den
    return dt, ar, ai, den, cr, ci


def _s5_prep(lam_re, lam_im, log_step, b_re_t, b_im_t):
    G, P = lam_re.shape
    H = b_re_t.shape[1]

    def body(lr_ref, li_ref, ls_ref, br_ref, bi_ref, ar_ref, ai_ref, bbr_ref, bbi_ref):
        _, ar, ai, _, cr, ci = _s5_coeffs(lr_ref[...], li_ref[...], ls_ref[...])
        ar_ref[...] = ar
        ai_ref[...] = ai
        br, bi = br_ref[...], bi_ref[...]
        crb, cib = cr[:, None, :], ci[:, None, :]
        bbr_ref[...] = crb * br - cib * bi
        bbi_ref[...] = crb * bi + cib * br

    return pl.pallas_call(
        body, name="s5_prep",
        out_shape=(jax.ShapeDtypeStruct((G, P), F32), jax.ShapeDtypeStruct((G, P), F32),
                   jax.ShapeDtypeStruct((G, H, P), F32), jax.ShapeDtypeStruct((G, H, P), F32)),
        compiler_params=_params(),
    )(lam_re, lam_im, log_step.reshape(G, 1), b_re_t, b_im_t)


def _s5_prep_bwd(lam_re, lam_im, log_step, b_re_t, b_im_t, d_ar, d_ai, d_bbr, d_bbi):
    G, P = lam_re.shape
    H = b_re_t.shape[1]

    def body(lr_ref, li_ref, ls_ref, br_ref, bi_ref, dar_ref, dai_ref, dbbr_ref, dbbi_ref,
             dlr_ref, dli_ref, dls_ref, dbr_ref, dbi_ref):
        lr, li = lr_ref[...], li_ref[...]
        dt, ar, ai, den, cr, ci = _s5_coeffs(lr, li, ls_ref[...])
        br, bi = br_ref[...], bi_ref[...]
        gbr, gbi = dbbr_ref[...], dbbi_ref[...]
        crb, cib = cr[:, None, :], ci[:, None, :]
        dbr_ref[...] = crb * gbr + cib * gbi
        dbi_ref[...] = crb * gbi - cib * gbr
        gcr = jnp.sum(br * gbr + bi * gbi, axis=1)
        gci = jnp.sum(br * gbi - bi * gbr, axis=1)
        ilr, ili = lr / den, -li / den
        gar = dar_ref[...] + (ilr * gcr + ili * gci)
        gai = dai_ref[...] + (ilr * gci - ili * gcr)
        qr, qi = cr * ilr - ci * ili, cr * ili + ci * ilr
        glr = -(qr * gcr + qi * gci)
        gli = -(qr * gci - qi * gcr)
        glr = glr + dt * (ar * gar + ai * gai)
        gli = gli + dt * (ar * gai - ai * gar)
        wr, wi = lr * ar - li * ai, lr * ai + li * ar
        gdt = jnp.sum(wr * gar + wi * gai, axis=1, keepdims=True)
        dlr_ref[...] = glr
        dli_ref[...] = gli
        dls_ref[...] = gdt * dt

    return pl.pallas_call(
        body, name="s5_prep_bwd",
        out_shape=(jax.ShapeDtypeStruct((G, P), F32), jax.ShapeDtypeStruct((G, P), F32),
                   jax.ShapeDtypeStruct((G, 1), F32),
                   jax.ShapeDtypeStruct((G, H, P), F32), jax.ShapeDtypeStruct((G, H, P), F32)),
        compiler_params=_params(),
    )(lam_re, lam_im, log_step.reshape(G, 1), b_re_t, b_im_t, d_ar, d_ai, d_bbr, d_bbi)


def _s5_block_mats(bbr_t, bbi_t, c_re, c_im):
    bmat = _s5_expand(bbr_t, bbi_t)
    cmat = jnp.transpose(_s5_expand(c_re, -c_im), (0, 2, 1))
    return bmat.astype(BF16), cmat.astype(BF16)


def _s5_diag_mask():
    r = lax.broadcasted_iota(jnp.int32, (LANES, 2 * STATE_COLS), 0) // SSM_GROUP
    c = (lax.broadcasted_iota(jnp.int32, (LANES, 2 * STATE_COLS), 1) % STATE_COLS) // SSM_STATE
    return (r == c).astype(F32)


def _s5_expand(re, im):
    re = jnp.tile(re.reshape(SSM_BLOCKS, LANES, SSM_STATE), (1, 1, GROUPS_PER_BLOCK))
    im = jnp.tile(im.reshape(SSM_BLOCKS, LANES, SSM_STATE), (1, 1, GROUPS_PER_BLOCK))
    return jnp.concatenate([re, im], axis=-1) * _s5_diag_mask()[None]


def _s5_block_diag(dmat):
    d = dmat * _s5_diag_mask()[None]
    parts = []
    for ri in range(2):
        acc = 0.0
        for g in range(GROUPS_PER_BLOCK):
            c0 = ri * STATE_COLS + g * SSM_STATE
            acc = acc + d[:, :, c0:c0 + SSM_STATE]
        parts.append(acc.reshape(SSM_GROUPS, SSM_GROUP, SSM_STATE))
    return jnp.stack(parts)


def _s5_a_rows(ar, ai):
    a = jnp.concatenate([ar.reshape(SSM_BLOCKS, STATE_COLS), ai.reshape(SSM_BLOCKS, STATE_COLS)], axis=1)
    return jnp.broadcast_to(a[:, None, :], (SSM_BLOCKS, SUBLANES, 2 * STATE_COLS))


def _s5_fwd(proj, bmat, cmat, a_rows, d_skip, *, tc=512):
    L = proj.shape[0]
    tc = min(tc, L)
    nt = L // tc
    n8 = tc // SUBLANES
    S = STATE_COLS

    def body(u_ref, b_ref, c_ref, a_ref, d_ref, y_ref, yg_ref, xp_ref, bu_s, xp_s, carry_s):
        @pl.when(pl.program_id(1) == 0)
        def _():
            carry_s[...] = jnp.zeros_like(carry_s)

        u = u_ref[...]
        bu = jnp.dot(u.astype(BF16), b_ref[...], preferred_element_type=F32)
        bu_s[...] = bu.reshape(n8, SUBLANES, 2 * S)
        ar, ai = a_ref[0:1, :S], a_ref[0:1, S:]

        def step(i, carry):
            cr, ci = carry
            for j in range(SUBLANES):
                xp_s[i, j:j + 1, :S] = cr
                xp_s[i, j:j + 1, S:] = ci
                br = bu_s[i, j:j + 1, :S]
                bi = bu_s[i, j:j + 1, S:]
                cr, ci = ar * cr - ai * ci + br, ar * ci + ai * cr + bi
            return cr, ci

        cr, ci = lax.fori_loop(0, n8, step, (carry_s[0:1, :S], carry_s[0:1, S:]))
        carry_s[0:1, :S] = cr
        carry_s[0:1, S:] = ci
        xp = xp_s[...].reshape(tc, 2 * S)
        xp_ref[...] = xp
        x_re = ar * xp[:, :S] - ai * xp[:, S:] + bu[:, :S]
        x_im = ar * xp[:, S:] + ai * xp[:, :S] + bu[:, S:]
        xs = jnp.concatenate([x_re, x_im], axis=1).astype(BF16)
        y = jnp.dot(xs, c_ref[...], preferred_element_type=F32) + d_ref[...] * u
        y_ref[...] = y
        yg_ref[...] = _gelu(y).astype(BF16)

    return pl.pallas_call(
        body, name="s5_fwd",
        out_shape=(jax.ShapeDtypeStruct((L, MAIN_WIDTH), F32),
                   jax.ShapeDtypeStruct((L, MAIN_WIDTH), BF16),
                   jax.ShapeDtypeStruct((L, SSM_BLOCKS * 2 * S), F32)),
        grid=(SSM_BLOCKS, nt),
        in_specs=[pl.BlockSpec((tc, LANES), lambda b, t: (t, b)),
                  pl.BlockSpec((None, LANES, 2 * S), lambda b, t: (b, 0, 0)),
                  pl.BlockSpec((None, 2 * S, LANES), lambda b, t: (b, 0, 0)),
                  pl.BlockSpec((None, SUBLANES, 2 * S), lambda b, t: (b, 0, 0)),
                  pl.BlockSpec((1, LANES), lambda b, t: (0, b))],
        out_specs=(pl.BlockSpec((tc, LANES), lambda b, t: (t, b)),
                   pl.BlockSpec((tc, LANES), lambda b, t: (t, b)),
                   pl.BlockSpec((tc, 2 * S), lambda b, t: (t, b))),
        scratch_shapes=[pltpu.VMEM((n8, SUBLANES, 2 * S), F32),
                        pltpu.VMEM((n8, SUBLANES, 2 * S), F32),
                        pltpu.VMEM((SUBLANES, 2 * S), F32)],
        compiler_params=_params("parallel", "arbitrary"),
    )(proj, bmat, cmat, a_rows, d_skip.reshape(1, MAIN_WIDTH))


def _s5_bwd(proj, dyg_a, dyg_b, y, xp, bmat, cmat, a_rows, d_skip, dproj, *, tc=512):
    L = proj.shape[0]
    tc = min(tc, L)
    nt = L // tc
    n8 = tc // SUBLANES
    S = STATE_COLS
    nn = (((1,), (1,)), ((), ()))
    tn = (((0,), (0,)), ((), ()))

    def body(u_ref, dyga_ref, dygb_ref, y_ref, xp_ref, b_ref, c_ref, a_ref, d_ref, dp_hbm,
             du_ref, db_ref, dc_ref, da_ref, dd_ref, dl_s, carry_s):
        @pl.when(pl.program_id(1) == 0)
        def _():
            carry_s[...] = jnp.zeros_like(carry_s)
            db_ref[...] = jnp.zeros_like(db_ref)
            dc_ref[...] = jnp.zeros_like(dc_ref)
            da_ref[...] = jnp.zeros_like(da_ref)
            dd_ref[...] = jnp.zeros_like(dd_ref)

        u = u_ref[...]
        dy = (dyga_ref[...] + dygb_ref[...]) * _gelu_grad(y_ref[...])
        xp = xp_ref[...]
        ub = u.astype(BF16)
        dyb = dy.astype(BF16)
        ar, ai = a_ref[0:1, :S], a_ref[0:1, S:]
        bu = jnp.dot(ub, b_ref[...], preferred_element_type=F32)
        x_re = ar * xp[:, :S] - ai * xp[:, S:] + bu[:, :S]
        x_im = ar * xp[:, S:] + ai * xp[:, :S] + bu[:, S:]
        xs = jnp.concatenate([x_re, x_im], axis=1).astype(BF16)
        dc_ref[...] += lax.dot_general(dyb, xs, tn, preferred_element_type=F32)
        dx = lax.dot_general(dyb, c_ref[...], nn, preferred_element_type=F32)
        dl_s[...] = dx.reshape(n8, SUBLANES, 2 * S)

        def step(k, carry):
            cr, ci = carry
            i = n8 - 1 - k
            for j in range(SUBLANES - 1, -1, -1):
                lr = dl_s[i, j:j + 1, :S] + (ar * cr + ai * ci)
                li = dl_s[i, j:j + 1, S:] + (ar * ci - ai * cr)
                dl_s[i, j:j + 1, :S] = lr
                dl_s[i, j:j + 1, S:] = li
                cr, ci = lr, li
            return cr, ci

        cr, ci = lax.fori_loop(0, n8, step, (carry_s[0:1, :S], carry_s[0:1, S:]))
        carry_s[0:1, :S] = cr
        carry_s[0:1, S:] = ci
        lam = dl_s[...].reshape(tc, 2 * S)
        l_re, l_im = lam[:, :S], lam[:, S:]
        da_ref[0:1, :S] += jnp.sum(l_re * xp[:, :S] + l_im * xp[:, S:], axis=0, keepdims=True)
        da_ref[0:1, S:] += jnp.sum(l_im * xp[:, :S] - l_re * xp[:, S:], axis=0, keepdims=True)
        lamb = lam.astype(BF16)
        du = lax.dot_general(lamb, b_ref[...], nn, preferred_element_type=F32) + d_ref[...] * dy
        du_ref[...] = du.astype(du_ref.dtype)
        db_ref[...] += lax.dot_general(ub, lamb, tn, preferred_element_type=F32)
        dd_ref[0:1, :] += jnp.sum(dy * u, axis=0, keepdims=True)

    rev = lambda b, t: (nt - 1 - t, b)
    return pl.pallas_call(
        body, name="s5_bwd",
        out_shape=(jax.ShapeDtypeStruct(dproj.shape, dproj.dtype),
                   jax.ShapeDtypeStruct((SSM_BLOCKS, LANES, 2 * S), F32),
                   jax.ShapeDtypeStruct((SSM_BLOCKS, LANES, 2 * S), F32),
                   jax.ShapeDtypeStruct((SSM_BLOCKS, SUBLANES, 2 * S), F32),
                   jax.ShapeDtypeStruct((SUBLANES, MAIN_WIDTH), F32)),
        input_output_aliases={9: 0},
        grid=(SSM_BLOCKS, nt),
        in_specs=[pl.BlockSpec((tc, LANES), rev),
                  pl.BlockSpec((tc, LANES), rev),
                  pl.BlockSpec((tc, LANES), rev),
                  pl.BlockSpec((tc, LANES), rev),
                  pl.BlockSpec((tc, 2 * S), rev),
                  pl.BlockSpec((None, LANES, 2 * S), lambda b, t: (b, 0, 0)),
                  pl.BlockSpec((None, 2 * S, LANES), lambda b, t: (b, 0, 0)),
                  pl.BlockSpec((None, SUBLANES, 2 * S), lambda b, t: (b, 0, 0)),
                  pl.BlockSpec((1, LANES), lambda b, t: (0, b)),
                  _ANY],
        out_specs=(pl.BlockSpec((tc, LANES), rev),
                   pl.BlockSpec((None, LANES, 2 * S), lambda b, t: (b, 0, 0)),
                   pl.BlockSpec((None, LANES, 2 * S), lambda b, t: (b, 0, 0)),
                   pl.BlockSpec((None, SUBLANES, 2 * S), lambda b, t: (b, 0, 0)),
                   pl.BlockSpec((SUBLANES, LANES), lambda b, t: (0, b))),
        scratch_shapes=[pltpu.VMEM((n8, SUBLANES, 2 * S), F32),
                        pltpu.VMEM((SUBLANES, 2 * S), F32)],
        compiler_params=_params("parallel", "arbitrary"),
    )(proj, dyg_a, dyg_b, y, xp, bmat, cmat, a_rows, d_skip.reshape(1, MAIN_WIDTH), dproj)


_Z_COLS = slice(MAIN_WIDTH, 2 * MAIN_WIDTH)
_ZM_COLS = slice(2 * MAIN_WIDTH + MEM_WIDTH, IN_WIDTH)


def _proj_rows(tr):
    return pl.BlockSpec((tr, IN_WIDTH), lambda i: (i, 0))


def _row_specs(tr):
    main = pl.BlockSpec((tr, MAIN_WIDTH), lambda i: (i, 0))
    z = pl.BlockSpec((tr, MAIN_WIDTH), lambda i: (i, 1))
    zm = pl.BlockSpec((tr, MEM_WIDTH), lambda i: (i, IN_WIDTH // MEM_WIDTH - 1))
    mem = pl.BlockSpec((tr, MEM_WIDTH), lambda i: (i, 0))
    cat = pl.BlockSpec((tr, D_MODEL), lambda i: (i, 0))
    vec = pl.BlockSpec((1, MAIN_WIDTH), lambda i: (0, 0))
    return main, z, zm, mem, cat, vec


def _gate_a_fwd(y, t, b_glu, proj, o_mem, *, tr=256):
    L = y.shape[0]
    tr = min(tr, L)

    def body(y_ref, t_ref, b_ref, z_ref, zm_ref, om_ref, o_ref):
        yg = _gelu(y_ref[...])
        sz, _ = _silu_and_grad(z_ref[...])
        o_ref[:, :MAIN_WIDTH] = (yg * _sigmoid(t_ref[...] + b_ref[...]) * sz).astype(BF16)
        szm, _ = _silu_and_grad(zm_ref[...])
        o_ref[:, MAIN_WIDTH:] = (om_ref[...] * szm).astype(BF16)

    main, z, zm, mem, cat, vec = _row_specs(tr)
    return pl.pallas_call(
        body, name="gate_a_fwd", out_shape=jax.ShapeDtypeStruct((L, D_MODEL), BF16),
        grid=(L // tr,), in_specs=[main, main, vec, z, zm, mem], out_specs=cat,
        compiler_params=_params("parallel"),
    )(y, t, b_glu.reshape(1, MAIN_WIDTH), proj, proj, o_mem)


def _gate_a_bwd(dcat, y, t, b_glu, proj, o_mem, *, tr=256):
    L = y.shape[0]
    tr = min(tr, L)

    def body(dc_ref, y_ref, t_ref, b_ref, z_ref, zm_ref, om_ref,
             dp_ref, dt_ref, dyg_ref, dom_ref, db_ref):
        dmain = dc_ref[:, :MAIN_WIDTH]
        dmemo = dc_ref[:, MAIN_WIDTH:]
        yg = _gelu(y_ref[...])
        sg = _sigmoid(t_ref[...] + b_ref[...])
        sz, gz = _silu_and_grad(z_ref[...])
        dp_ref[:, _Z_COLS] = (dmain * (yg * sg) * gz).astype(BF16)
        dy2 = dmain * sz
        dyg_ref[...] = dy2 * sg
        dt = dy2 * yg * (sg * (1.0 - sg))
        dt_ref[...] = dt.astype(BF16)

        @pl.when(pl.program_id(0) == 0)
        def _():
            db_ref[...] = jnp.zeros_like(db_ref)

        db_ref[...] += jnp.sum(dt, axis=0, keepdims=True)
        szm, gzm = _silu_and_grad(zm_ref[...])
        dom_ref[...] = dmemo * szm
        dp_ref[:, _ZM_COLS] = (dmemo * om_ref[...] * gzm).astype(BF16)

    main, z, zm, mem, cat, vec = _row_specs(tr)
    outs = pl.pallas_call(
        body, name="gate_a_bwd",
        out_shape=(jax.ShapeDtypeStruct((L, IN_WIDTH), BF16),
                   jax.ShapeDtypeStruct((L, MAIN_WIDTH), BF16), jax.ShapeDtypeStruct((L, MAIN_WIDTH), F32),
                   jax.ShapeDtypeStruct((L, MEM_WIDTH), F32), jax.ShapeDtypeStruct((1, MAIN_WIDTH), F32)),
        grid=(L // tr,), in_specs=[cat, main, main, vec, z, zm, mem],
        out_specs=(_proj_rows(tr), main, main, mem, vec),
        compiler_params=_params("arbitrary"),
    )(dcat, y, t, b_glu.reshape(1, MAIN_WIDTH), proj, proj, o_mem)
    return outs


def _gate_b_fwd(att, proj, o_mem, *, tr=256):
    L = att.shape[0]
    tr = min(tr, L)

    def body(a_ref, z_ref, zm_ref, om_ref, o_ref):
        sz, _ = _silu_and_grad(z_ref[...])
        o_ref[:, :MAIN_WIDTH] = (a_ref[...] * sz).astype(BF16)
        szm, _ = _silu_and_grad(zm_ref[...])
        o_ref[:, MAIN_WIDTH:] = (om_ref[...] * szm).astype(BF16)

    main, z, zm, mem, cat, _ = _row_specs(tr)
    return pl.pallas_call(
        body, name="gate_b_fwd", out_shape=jax.ShapeDtypeStruct((L, D_MODEL), BF16),
        grid=(L // tr,), in_specs=[main, z, zm, mem], out_specs=cat,
        compiler_params=_params("parallel"),
    )(att, proj, proj, o_mem)


def _gate_b_bwd(dcat, att, proj, o_mem, *, tr=256):
    L = att.shape[0]
    tr = min(tr, L)

    def body(dc_ref, a_ref, z_ref, zm_ref, om_ref, da_ref, dp_ref, dom_ref, dl_ref):
        dmain = dc_ref[:, :MAIN_WIDTH]
        dmemo = dc_ref[:, MAIN_WIDTH:]
        att = a_ref[...]
        sz, gz = _silu_and_grad(z_ref[...])
        datt = dmain * sz
        da_ref[...] = datt
        dp_ref[:, _Z_COLS] = (dmain * att * gz).astype(BF16)
        szm, gzm = _silu_and_grad(zm_ref[...])
        dom_ref[...] = dmemo * szm
        dp_ref[:, _ZM_COLS] = (dmemo * om_ref[...] * gzm).astype(BF16)
        prod = datt * att
        for h in range(FOX_HEADS):
            dl_ref[h] = jnp.sum(prod[:, h * HEAD_DIM:(h + 1) * HEAD_DIM], axis=1, keepdims=True)

    main, z, zm, mem, cat, _ = _row_specs(tr)
    delta = pl.BlockSpec((FOX_HEADS, tr, 1), lambda i: (0, i, 0))
    return pl.pallas_call(
        body, name="gate_b_bwd",
        out_shape=(jax.ShapeDtypeStruct((L, MAIN_WIDTH), F32), jax.ShapeDtypeStruct((L, IN_WIDTH), BF16),
                   jax.ShapeDtypeStruct((L, MEM_WIDTH), F32), jax.ShapeDtypeStruct((FOX_HEADS, L, 1), F32)),
        grid=(L // tr,), in_specs=[cat, main, z, zm, mem], out_specs=(main, _proj_rows(tr), mem, delta),
        compiler_params=_params("parallel"),
    )(dcat, att, proj, proj, o_mem)


_MEM_Q_COL = (2 * MAIN_WIDTH) // HEAD_DIM
_NT = (((1,), (1,)), ((), ()))
_TN = (((0,), (0,)), ((), ()))


def _mem_probs(q_ref, k_ref):
    qs = (q_ref[...] * (HEAD_DIM ** -0.5)).astype(BF16)
    s = lax.dot_general(qs, k_ref[...].astype(BF16), _NT, preferred_element_type=F32)
    e = jnp.exp(s - jnp.max(s, axis=-1, keepdims=True))
    return qs, e / jnp.sum(e, axis=-1, keepdims=True)


def _mem_attn_fwd(proj, kvm, *, tq=2048):
    L = proj.shape[0]
    tq = min(tq, L)

    def body(q_ref, k_ref, v_ref, o_ref):
        _, p = _mem_probs(q_ref, k_ref)
        o_ref[...] = jnp.dot(p.astype(BF16), v_ref[...].astype(BF16), preferred_element_type=F32)

    return pl.pallas_call(
        body, name="mem_attn_fwd", out_shape=jax.ShapeDtypeStruct((L, MEM_WIDTH), F32),
        grid=(MEM_HEADS, L // tq),
        in_specs=[pl.BlockSpec((tq, HEAD_DIM), lambda h, i: (i, _MEM_Q_COL + h)),
                  pl.BlockSpec((N_MEM, HEAD_DIM), lambda h, i: (0, h)),
                  pl.BlockSpec((N_MEM, HEAD_DIM), lambda h, i: (0, MEM_HEADS + h))],
        out_specs=pl.BlockSpec((tq, HEAD_DIM), lambda h, i: (i, h)),
        compiler_params=_params("parallel", "parallel"),
    )(proj, kvm, kvm)


def _mem_attn_bwd(proj, kvm, do, dproj, *, tq=2048):
    L = proj.shape[0]
    tq = min(tq, L)

    def body(q_ref, k_ref, v_ref, do_ref, dp_hbm, dq_ref, dk_ref, dv_ref):
        @pl.when(pl.program_id(1) == 0)
        def _():
            dk_ref[...] = jnp.zeros_like(dk_ref)
            dv_ref[...] = jnp.zeros_like(dv_ref)

        qs, p = _mem_probs(q_ref, k_ref)
        dob = do_ref[...].astype(BF16)
        dp = lax.dot_general(dob, v_ref[...].astype(BF16), _NT, preferred_element_type=F32)
        ds = p * (dp - jnp.sum(p * dp, axis=-1, keepdims=True))
        dsb = ds.astype(BF16)
        dq = jnp.dot(dsb, k_ref[...].astype(BF16), preferred_element_type=F32) * (HEAD_DIM ** -0.5)
        dq_ref[...] = dq.astype(BF16)
        dk_ref[...] += lax.dot_general(dsb, qs, _TN, preferred_element_type=F32)
        dv_ref[...] += lax.dot_general(p.astype(BF16), dob, _TN, preferred_element_type=F32)

    dproj, dk, dv = pl.pallas_call(
        body, name="mem_attn_bwd",
        out_shape=(jax.ShapeDtypeStruct(dproj.shape, dproj.dtype),
                   jax.ShapeDtypeStruct((N_MEM, MEM_WIDTH), F32),
                   jax.ShapeDtypeStruct((N_MEM, MEM_WIDTH), F32)),
        grid=(MEM_HEADS, L // tq),
        in_specs=[pl.BlockSpec((tq, HEAD_DIM), lambda h, i: (i, _MEM_Q_COL + h)),
                  pl.BlockSpec((N_MEM, HEAD_DIM), lambda h, i: (0, h)),
                  pl.BlockSpec((N_MEM, HEAD_DIM), lambda h, i: (0, MEM_HEADS + h)),
                  pl.BlockSpec((tq, HEAD_DIM), lambda h, i: (i, h)),
                  _ANY],
        out_specs=(pl.BlockSpec((tq, HEAD_DIM), lambda h, i: (i, _MEM_Q_COL + h)),
                   pl.BlockSpec((N_MEM, HEAD_DIM), lambda h, i: (0, h)),
                   pl.BlockSpec((N_MEM, HEAD_DIM), lambda h, i: (0, h))),
        input_output_aliases={4: 0},
        compiler_params=_params("parallel", "arbitrary"),
    )(proj, kvm, kvm, do, dproj)
    return dproj, jnp.concatenate([dk, dv], axis=1)


def _tile_cumsum(x, row, reverse):
    for sh in (1, 2, 4):
        if reverse:
            x = x + jnp.where(row < SUBLANES - sh, pltpu.roll(x, SUBLANES - sh, 0), 0.0)
        else:
            x = x + jnp.where(row >= sh, pltpu.roll(x, sh, 0), 0.0)
    return x


def _fgate_fwd(pre, b_pad):
    L = pre.shape[0]
    n8 = L // SUBLANES

    def body(p_ref, b_ref, o_ref):
        row = lax.broadcasted_iota(jnp.int32, (SUBLANES, LANES), 0)
        b = b_ref[...]

        def step(i, carry):
            x = p_ref[i] + b
            logf = jnp.minimum(x, 0.0) - jnp.log(1.0 + jnp.exp(-jnp.abs(x)))
            t = _tile_cumsum(logf, row, False) + carry
            o_ref[i] = t
            return t[SUBLANES - 1:SUBLANES, :]

        lax.fori_loop(0, n8, step, jnp.zeros((1, LANES), F32))

    out = pl.pallas_call(
        body, name="fgate_fwd", out_shape=jax.ShapeDtypeStruct((n8, SUBLANES, LANES), F32),
        compiler_params=_params(),
    )(pre.reshape(n8, SUBLANES, LANES), b_pad.reshape(1, LANES))
    return out.reshape(L, LANES)


def _fgate_bwd(dfcum, pre, b_pad):
    L = pre.shape[0]
    n8 = L // SUBLANES

    def body(d_ref, p_ref, b_ref, o_ref, s_ref):
        row = lax.broadcasted_iota(jnp.int32, (SUBLANES, LANES), 0)
        b = b_ref[...]

        def step(k, carry):
            c, acc = carry
            i = n8 - 1 - k
            t = _tile_cumsum(d_ref[i], row, True) + c
            dpre = t * _sigmoid(-(p_ref[i] + b))
            o_ref[i] = dpre
            return t[0:1, :], acc + dpre

        _, acc = lax.fori_loop(0, n8, step, (jnp.zeros((1, LANES), F32), jnp.zeros((SUBLANES, LANES), F32)))
        s_ref[...] = jnp.sum(acc, axis=0, keepdims=True)

    dpre, db = pl.pallas_call(
        body, name="fgate_bwd",
        out_shape=(jax.ShapeDtypeStruct((n8, SUBLANES, LANES), F32), jax.ShapeDtypeStruct((1, LANES), F32)),
        compiler_params=_params(),
    )(dfcum.reshape(n8, SUBLANES, LANES), pre.reshape(n8, SUBLANES, LANES), b_pad.reshape(1, LANES))
    return dpre.reshape(L, LANES), db


FOX_BLOCK = 512


def _fox_scores(qs, k, fk, diagonal):
    s = lax.dot_general(qs, k, _NT, preferred_element_type=F32) - fk
    if diagonal:
        row = lax.broadcasted_iota(jnp.int32, s.shape, 0)
        col = lax.broadcasted_iota(jnp.int32, s.shape, 1)
        s = jnp.where(row >= col, s, NEG_BIG)
    return s


def _fox_specs(tq, L):
    nq = L // tq
    return dict(
        rows=lambda off: pl.BlockSpec((tq, HEAD_DIM), lambda h, i: (i, off + h)),
        seq=lambda off: pl.BlockSpec((L, HEAD_DIM), lambda h, i: (0, off + h)),
        col=pl.BlockSpec((None, None, tq, 1), lambda h, i: (h, i, 0, 0)),
        col_all=pl.BlockSpec((None, nq, tq, 1), lambda h, i: (h, 0, 0, 0)),
        row=pl.BlockSpec((None, None, 1, tq), lambda h, i: (h, i, 0, 0)),
        row_all=pl.BlockSpec((None, nq, 1, tq), lambda h, i: (h, 0, 0, 0)))


FOX_FWD_HEADS = 2


def _fox_fwd(proj, kv, fk):
    L = proj.shape[0]
    tq = min(FOX_BLOCK, L)
    nq = L // tq
    nh = FOX_FWD_HEADS
    W = nh * HEAD_DIM

    def body(q_ref, k_ref, v_ref, fk_ref, o_ref, lse_ref, m_s, l_s, acc_s):
        qi = pl.program_id(1)
        cols = [slice(a * HEAD_DIM, (a + 1) * HEAD_DIM) for a in range(nh)]
        qs = [(q_ref[:, cs] * (HEAD_DIM ** -0.5)).astype(BF16) for cs in cols]
        m_s[...] = jnp.full_like(m_s, NEG_BIG)
        l_s[...] = jnp.zeros_like(l_s)
        acc_s[...] = jnp.zeros_like(acc_s)

        def block(j, diagonal):
            r0 = pl.multiple_of(j * tq, tq)
            for a, cs in enumerate(cols):
                s = _fox_scores(qs[a], k_ref[pl.ds(r0, tq), cs], fk_ref[a, j], diagonal)
                m_new = jnp.maximum(m_s[a], jnp.max(s, axis=-1, keepdims=True))
                alpha = jnp.exp(m_s[a] - m_new)
                p = jnp.exp(s - m_new)
                l_s[a] = alpha * l_s[a] + jnp.sum(p, axis=-1, keepdims=True)
                acc_s[a] = alpha * acc_s[a] + jnp.dot(p.astype(BF16), v_ref[pl.ds(r0, tq), cs],
                                                      preferred_element_type=F32)
                m_s[a] = m_new

        def below(j, carry):
            block(j, False)
            return carry

        lax.fori_loop(0, qi, below, 0)
        block(qi, True)
        for a, cs in enumerate(cols):
            o_ref[:, cs] = acc_s[a] / l_s[a]
            lse_ref[a] = m_s[a] + jnp.log(l_s[a])

    return pl.pallas_call(
        body, name="fox_fwd",
        out_shape=(jax.ShapeDtypeStruct((L, MAIN_WIDTH), F32),
                   jax.ShapeDtypeStruct((FOX_HEADS, nq, tq, 1), F32)),
        grid=(FOX_HEADS // nh, nq),
        in_specs=[pl.BlockSpec((tq, W), lambda h, i: (i, h)),
                  pl.BlockSpec((L, W), lambda h, i: (0, h)),
                  pl.BlockSpec((L, W), lambda h, i: (0, FOX_HEADS // nh + h)),
                  pl.BlockSpec((nh, nq, 1, tq), lambda h, i: (h, 0, 0, 0))],
        out_specs=(pl.BlockSpec((tq, W), lambda h, i: (i, h)),
                   pl.BlockSpec((nh, None, tq, 1), lambda h, i: (h, i, 0, 0))),
        scratch_shapes=[pltpu.VMEM((nh, tq, 1), F32), pltpu.VMEM((nh, tq, 1), F32),
                        pltpu.VMEM((nh, tq, HEAD_DIM), F32)],
        compiler_params=_params("parallel", "parallel"),
    )(proj, kv, kv, fk)


def _fox_bwd_dq(proj, kv, fk, lse, delta, datt, dproj):
    L = proj.shape[0]
    tq = min(FOX_BLOCK, L)
    nq = L // tq
    sp = _fox_specs(tq, L)

    def body(q_ref, k_ref, v_ref, fk_ref, lse_ref, dl_ref, do_ref, dp_hbm, dq_ref, df_ref, acc_s, df_s):
        qi = pl.program_id(1)
        qs = (q_ref[...] * (HEAD_DIM ** -0.5)).astype(BF16)
        dob = do_ref[...].astype(BF16)
        lse, dl = lse_ref[...], dl_ref[...]
        acc_s[...] = jnp.zeros_like(acc_s)
        df_s[...] = jnp.zeros_like(df_s)

        def block(j, diagonal):
            r0 = pl.multiple_of(j * tq, tq)
            k = k_ref[pl.ds(r0, tq), :]
            p = jnp.exp(_fox_scores(qs, k, fk_ref[j], diagonal) - lse)
            dp = lax.dot_general(dob, v_ref[pl.ds(r0, tq), :], _NT, preferred_element_type=F32)
            ds = p * (dp - dl)
            acc_s[...] += jnp.dot(ds.astype(BF16), k, preferred_element_type=F32)
            df_s[...] += jnp.sum(ds, axis=1, keepdims=True)

        def below(j, carry):
            block(j, False)
            return carry

        lax.fori_loop(0, qi, below, 0)
        block(qi, True)
        dq_ref[...] = (acc_s[...] * (HEAD_DIM ** -0.5)).astype(BF16)
        df_ref[...] = df_s[...]

    return pl.pallas_call(
        body, name="fox_bwd_dq",
        out_shape=(jax.ShapeDtypeStruct(dproj.shape, dproj.dtype),
                   jax.ShapeDtypeStruct((FOX_HEADS, nq, tq, 1), F32)),
        grid=(FOX_HEADS, nq),
        in_specs=[sp["rows"](0), sp["seq"](0), sp["seq"](FOX_HEADS), sp["row_all"],
                  sp["col"], sp["col"], sp["rows"](0), _ANY],
        out_specs=(sp["rows"](0), sp["col"]),
        input_output_aliases={7: 0},
        scratch_shapes=[pltpu.VMEM((tq, HEAD_DIM), F32), pltpu.VMEM((tq, 1), F32)],
        compiler_params=_params("parallel", "parallel"),
    )(proj, kv, kv, fk, lse, delta, datt, dproj)


def _fox_bwd_dkv(proj, kv, fk, lse, delta, datt):
    L = proj.shape[0]
    tq = min(FOX_BLOCK, L)
    nq = L // tq
    sp = _fox_specs(tq, L)

    def body(q_ref, k_ref, v_ref, fk_ref, lse_ref, dl_ref, do_ref,
             dk_ref, dv_ref, df_ref, dk_s, dv_s, df_s):
        ki = pl.program_id(1)
        k, v, fk = k_ref[...], v_ref[...], fk_ref[...]
        dk_s[...] = jnp.zeros_like(dk_s)
        dv_s[...] = jnp.zeros_like(dv_s)
        df_s[...] = jnp.zeros_like(df_s)

        def block(i, diagonal):
            r0 = pl.multiple_of(i * tq, tq)
            qs = (q_ref[pl.ds(r0, tq), :] * (HEAD_DIM ** -0.5)).astype(BF16)
            dob = do_ref[pl.ds(r0, tq), :].astype(BF16)
            p = jnp.exp(_fox_scores(qs, k, fk, diagonal) - lse_ref[i])
            dp = lax.dot_general(dob, v, _NT, preferred_element_type=F32)
            ds = p * (dp - dl_ref[i])
            dv_s[...] += lax.dot_general(p.astype(BF16), dob, _TN, preferred_element_type=F32)
            dk_s[...] += lax.dot_general(ds.astype(BF16), qs, _TN, preferred_element_type=F32)
            df_s[...] -= jnp.sum(ds, axis=0, keepdims=True)

        def above(i, carry):
            block(i, False)
            return carry

        block(ki, True)
        lax.fori_loop(ki + 1, nq, above, 0)
        dk_ref[...] = dk_s[...].astype(BF16)
        dv_ref[...] = dv_s[...].astype(BF16)
        df_ref[...] = df_s[...]

    return pl.pallas_call(
        body, name="fox_bwd_dkv",
        out_shape=(jax.ShapeDtypeStruct((L, MAIN_WIDTH), BF16),
                   jax.ShapeDtypeStruct((L, MAIN_WIDTH), BF16),
                   jax.ShapeDtypeStruct((FOX_HEADS, nq, 1, tq), F32)),
        grid=(FOX_HEADS, nq),
        in_specs=[sp["seq"](0), sp["rows"](0), sp["rows"](FOX_HEADS), sp["row"],
                  sp["col_all"], sp["col_all"], sp["seq"](0)],
        out_specs=(sp["rows"](0), sp["rows"](0), sp["row"]),
        scratch_shapes=[pltpu.VMEM((tq, HEAD_DIM), F32), pltpu.VMEM((tq, HEAD_DIM), F32),
                        pltpu.VMEM((1, tq), F32)],
        compiler_params=_params("parallel", "parallel"),
    )(proj, kv, kv, fk, lse, delta, datt)


def _pad_lanes(a):
    return jnp.pad(a, ((0, 0), (0, LANES - a.shape[1])))


def _mem_branch_fwd(memn, w_mk, proj, tag):
    kvm = _mm(memn, w_mk, name="mem_kv_" + tag)
    return kvm, _mem_attn_fwd(proj, kvm)


def _mem_branch_bwd(mem, g, w_mk, proj, memn, kvm, do_mem, dproj, tag):
    dproj, dkvm = _mem_attn_bwd(proj, kvm, do_mem, dproj)
    dkvm = dkvm.astype(BF16)
    dw_mk = _mm(memn, dkvm, ta=True, name="dw_mem_kv_" + tag, out_dtype=BF16)
    dmemn = _mm(dkvm, w_mk, tb=True, name="dmemn_" + tag)
    _, dg = _rmsnorm_bwd(mem, g, dmemn, name="mem_norm_bwd_" + tag, dx_dtype=BF16)
    return dproj, dw_mk, dg


def _local_step(x, mem, target, w, fetch=None, grads_ready=None):
    if grads_ready is None:
        grads_ready = lambda group, grads, token: token
    L = x.shape[0]
    g = {}
    w = dict(w)

    b_re_t = jnp.transpose(w["b_re"], (0, 2, 1))
    b_im_t = jnp.transpose(w["b_im"], (0, 2, 1))
    ar, ai, bbr_t, bbi_t = _s5_prep(w["lam_re"], w["lam_im"], w["log_step"], b_re_t, b_im_t)
    bmat, cmat = _s5_block_mats(bbr_t, bbi_t, w["c_re"], w["c_im"])
    a_rows = _s5_a_rows(ar, ai)

    hn0 = _rmsnorm_fwd(x, w["pre_norm_g"][0], name="pre_norm_0", out_dtype=BF16)
    memn0 = _rmsnorm_fwd(mem, w["mem_norm_g"][0], name="mem_norm_0", out_dtype=BF16)
    memn1 = _rmsnorm_fwd(mem, w["mem_norm_g"][1], name="mem_norm_1", out_dtype=BF16)
    if fetch is not None:
        w.update(fetch("a", [hn0, memn0, memn1, bmat, cmat, a_rows]))
    proj_a = _mm(hn0, w["w_in_a"], name="in_proj_a")
    y, yg, xp = _s5_fwd(proj_a, bmat, cmat, a_rows, w["d_skip"])
    if fetch is not None:
        w.update(fetch("b", yg))
    t = _mm(yg, w["w_glu"], name="glu_proj")
    kvm0, om0 = _mem_branch_fwd(memn0, w["w_mem_kv"][0], proj_a, "0")
    cat0 = _gate_a_fwd(y, t, w["b_glu"], proj_a, om0)
    o0 = _mm(cat0, w["w_out"][0], name="out_proj_0")
    h1 = _rmsnorm_fwd(o0, w["post_norm_g"][0], res=x, name="post_norm_0")

    kv_in = _rmsnorm_fwd(h1, w["kv_norm_g"], name="kv_norm", out_dtype=BF16)
    if fetch is not None:
        w.update(fetch("c", kv_in))
    kv = _mm(kv_in, w["w_kv"], name="kv_proj", out_dtype=BF16)
    pre_f = _mm(kv_in, w["w_fgate"], name="fgate_proj")
    b_f = jnp.pad(w["b_fgate"], (0, LANES - FOX_HEADS))
    fcum = _fgate_fwd(pre_f, b_f)
    fc = jnp.transpose(fcum[:, :FOX_HEADS])
    tq = min(FOX_BLOCK, L)
    fk = fc.reshape(FOX_HEADS, L // tq, 1, tq)

    hn1 = _rmsnorm_fwd(h1, w["pre_norm_g"][1], name="pre_norm_1", out_dtype=BF16)
    proj_b = _mm(hn1, w["w_in_b"], name="in_proj_b")
    att, lse = _fox_fwd(proj_b, kv, fk)
    kvm1, om1 = _mem_branch_fwd(memn1, w["w_mem_kv"][1], proj_b, "1")
    cat1 = _gate_b_fwd(att, proj_b, om1)
    o1 = _mm(cat1, w["w_out"][1], name="out_proj_1")
    dh2, loss_row = _final_norm_loss(o1, w["post_norm_g"][1], h1, target)

    do1, dpost1 = _rmsnorm_bwd(o1, w["post_norm_g"][1], dh2, name="post_norm_bwd_1", dx_dtype=BF16)
    dcat1 = _mm(do1, w["w_out"][1], tb=True, name="dcat_1")
    g["w_out_1"] = _mm(cat1, do1, ta=True, name="dw_out_1", out_dtype=BF16)
    datt, dproj_b, dom1, delta = _gate_b_bwd(dcat1, att, proj_b, om1)
    dproj_b, g["w_mem_kv_1"], dmemg1 = _mem_branch_bwd(mem, w["mem_norm_g"][1], w["w_mem_kv"][1], proj_b,
                                                      memn1, kvm1, dom1, dproj_b, "1")
    delta = delta.reshape(lse.shape)
    dproj_b, dfq = _fox_bwd_dq(proj_b, kv, fk, lse, delta, datt, dproj_b)
    dk, dv, dfk = _fox_bwd_dkv(proj_b, kv, fk, lse, delta, datt)
    g["w_in_b"] = _mm(hn1, dproj_b, ta=True, name="dw_in_b", out_dtype=BF16, shards=N_CHIPS)
    dhn1 = _mm(dproj_b, w["w_in_b"], tb=True, name="dhn_1")

    dkv = jnp.concatenate([dk, dv], axis=1)
    g["w_kv"] = _mm(kv_in, dkv, ta=True, name="dw_kv", out_dtype=BF16, shards=N_CHIPS)
    dkv_in_a = _mm(dkv, w["w_kv"], tb=True, name="dkv_in_kv")
    dfcum = _pad_lanes(jnp.transpose(dfq.reshape(FOX_HEADS, L) + dfk.reshape(FOX_HEADS, L)))
    dpre_f, db_f = _fgate_bwd(dfcum, pre_f, b_f)
    g["b_fgate"] = db_f[0, :FOX_HEADS]
    g["w_fgate"] = _mm(kv_in, dpre_f, ta=True, name="dw_fgate")[:, :FOX_HEADS]
    dkv_in_b = _mm(dpre_f, w["w_fgate"], tb=True, name="dkv_in_fgate")
    dh1_kv, g["kv_norm_g"] = _rmsnorm_bwd(h1, w["kv_norm_g"], (dkv_in_a, dkv_in_b), name="kv_norm_bwd")
    dh1, dpre1 = _rmsnorm_bwd(h1, w["pre_norm_g"][1], dhn1, adds=(dh2, dh1_kv), name="pre_norm_bwd_1")
    dh1 = grads_ready("b", g, dh1)

    do0, dpost0 = _rmsnorm_bwd(o0, w["post_norm_g"][0], dh1, name="post_norm_bwd_0", dx_dtype=BF16)
    dcat0 = _mm(do0, w["w_out"][0], tb=True, name="dcat_0")
    g["w_out_0"] = _mm(cat0, do0, ta=True, name="dw_out_0", out_dtype=BF16)
    dcat0 = grads_ready("b_send", g, dcat0)
    dproj_a, dt, dyg_a, dom0, db_glu = _gate_a_bwd(dcat0, y, t, w["b_glu"], proj_a, om0)
    g["b_glu"] = db_glu[0]
    g["w_glu"] = _mm(yg, dt, ta=True, name="dw_glu", out_dtype=BF16)
    dyg_b = _mm(dt, w["w_glu"], tb=True, name="dyg")
    dproj_a, g["w_mem_kv_0"], dmemg0 = _mem_branch_bwd(mem, w["mem_norm_g"][0], w["w_mem_kv"][0], proj_a,
                                                      memn0, kvm0, dom0, dproj_a, "0")
    dyg_b = grads_ready("a1", g, dyg_b)
    dproj_a, db_blk, dc_blk, da_rows, dd_skip = _s5_bwd(proj_a, dyg_a, dyg_b, y, xp, bmat, cmat, a_rows,
                                                        w["d_skip"], dproj_a)
    dproj_a = grads_ready("a1_send", g, dproj_a)
    g["d_skip"] = dd_skip[0]
    g["w_in_a"] = _mm(hn0, dproj_a, ta=True, name="dw_in_a", out_dtype=BF16, shards=N_CHIPS)
    dproj_a = grads_ready("a2", g, dproj_a)
    dhn0 = _mm(dproj_a, w["w_in_a"], tb=True, name="dhn_0")
    grad_x, dpre0 = _rmsnorm_bwd(x, w["pre_norm_g"][0], dhn0, adds=(dh1,), name="pre_norm_bwd_0")

    dbb = _s5_block_diag(db_blk)
    dcc = _s5_block_diag(dc_blk)
    g["c_re"], g["c_im"] = dcc[0], -dcc[1]
    d_ar = da_rows[:, 0, :STATE_COLS].reshape(SSM_GROUPS, SSM_STATE)
    d_ai = da_rows[:, 0, STATE_COLS:].reshape(SSM_GROUPS, SSM_STATE)
    dlr, dli, dls, dbr_t, dbi_t = _s5_prep_bwd(w["lam_re"], w["lam_im"], w["log_step"], b_re_t, b_im_t,
                                               d_ar, d_ai, dbb[0], dbb[1])
    g["lam_re"], g["lam_im"], g["log_step"] = dlr, dli, dls[:, 0]
    g["b_re"] = jnp.transpose(dbr_t, (0, 2, 1))
    g["b_im"] = jnp.transpose(dbi_t, (0, 2, 1))
    g["pre_norm_g"] = jnp.stack([dpre0, dpre1])
    g["post_norm_g"] = jnp.stack([dpost0, dpost1])
    g["mem_norm_g"] = jnp.stack([dmemg0, dmemg1])
    return loss_row, grad_x, g


_MESH = pl.DeviceIdType.MESH
_ANY = pl.BlockSpec(memory_space=pl.ANY)


def _place():
    x, y, c = lax.axis_index("x"), lax.axis_index("y"), lax.axis_index("c")
    chips = [(1 - x, y), (x, 1 - y), (1 - x, 1 - y)]
    return x, y, c, chips


_HBM = pl.BlockSpec(memory_space=pltpu.HBM)
_SEM = pl.BlockSpec(memory_space=pltpu.SEMAPHORE)
_SIDE = pltpu.SideEffectType.DATAFLOW_SIDE_EFFECTING


def _in_hbm(a):
    return pltpu.with_memory_space_constraint(a, pltpu.HBM)


def _hbm_like(a):
    return pltpu.HBM(a.shape, a.dtype)


def _ici_copies(srcs, lands, send_sem, recv_sem, src_at, dst_at, wait_at, to_sibling=False):
    x, y, c, chips = _place()
    peers = [(x, y, 1 - c)] if to_sibling else [(cx, cy, c) for cx, cy in chips]
    m = len(peers)
    start, wait = [], []
    for i in range(len(srcs)):
        for k, (px, py, pc) in enumerate(peers):
            sem = dict(send_sem=send_sem.at[m * i + k], recv_sem=recv_sem.at[m * i + k],
                       device_id=(px, py, pc), device_id_type=_MESH)
            src = src_at(srcs[i], 2 * px + py, c)
            start.append(pltpu.make_async_remote_copy(src_ref=src, dst_ref=dst_at(lands[i], 2 * x + y, k, c), **sem))
            wait.append(pltpu.make_async_remote_copy(src_ref=src, dst_ref=wait_at(lands[i], 2 * px + py, k, c), **sem))
    return start, wait


def _route_peers(route):
    return 1 if len(route) == 4 else 3


_BLOCK_ROUTE = (lambda s, j, c: s, lambda l, me, k, c: l.at[me, c], lambda l, j, k, c: l.at[j, c])


def _ici_start(srcs, lands, token, route, *, name):
    n = len(srcs)

    def body(*refs):
        start, _ = _ici_copies(refs[:n], refs[n:2 * n], refs[2 * n + 1], refs[2 * n + 2], *route)
        for cp in start:
            cp.start()

    sems = pltpu.SemaphoreType.DMA((_route_peers(route) * n,))
    outs = pl.pallas_call(
        body, name=name,
        out_shape=(sems, sems, *[_hbm_like(a) for a in srcs], *[_hbm_like(a) for a in lands], _hbm_like(token)),
        in_specs=[_HBM] * (2 * n + 1), out_specs=(_SEM, _SEM, *[_HBM] * (2 * n + 1)),
        input_output_aliases={i: 2 + i for i in range(2 * n + 1)},
        compiler_params=pltpu.CompilerParams(has_side_effects=_SIDE),
    )(*[_in_hbm(a) for a in srcs], *[_in_hbm(a) for a in lands], _in_hbm(token))
    return (outs[0], outs[1], list(outs[2:2 + n]), list(outs[2 + n:2 + 2 * n])), outs[2 + 2 * n]


def _ici_wait(handle, after, route, *, name):
    send_sem, recv_sem, srcs, lands = handle
    n = len(srcs)
    after = list(after) if isinstance(after, (list, tuple)) else [after]

    def body(*refs):
        _, wait = _ici_copies(refs[:n], refs[n:2 * n], refs[2 * n], refs[2 * n + 1], *route)
        for cp in wait:
            cp.wait_send()
            cp.wait_recv()

    outs = pl.pallas_call(
        body, name=name,
        out_shape=(*[_hbm_like(a) for a in srcs], *[_hbm_like(a) for a in lands]),
        in_specs=[_HBM] * (2 * n) + [_SEM, _SEM] + [_ANY] * len(after), out_specs=tuple([_HBM] * (2 * n)),
        input_output_aliases={i: i for i in range(2 * n)},
        compiler_params=pltpu.CompilerParams(has_side_effects=_SIDE),
    )(*srcs, *lands, send_sem, recv_sem, *after)
    return list(outs[:n]), list(outs[n:])


_GATHER_ROUTE = (lambda s, j, c: s.at[c], lambda l, me, k, c: l.at[me, c], lambda l, j, k, c: l.at[j, c])
_SCATTER_ROUTE = (lambda s, j, c: s.at[j], lambda l, me, k, c: l.at[k], lambda l, j, k, c: l.at[k])
_SWAP_ROUTE = (lambda s, j, c: s.at[:, 1 - c], lambda l, me, k, c: l, lambda l, j, k, c: l, True)


def _gather_forward(lands, tag, own=False):
    n = len(lands)
    m = 4 if own else 3

    def body(*refs):
        ins, outs = refs[:n], refs[n:2 * n]
        send_sem, recv_sem = refs[2 * n:]
        x, y, c, chips = _place()
        slots = [2 * cx + cy for cx, cy in chips] + [2 * x + y]

        def copy(i, k, half):
            return pltpu.make_async_remote_copy(
                src_ref=ins[i].at[slots[k], half], dst_ref=outs[i].at[slots[k], half],
                send_sem=send_sem.at[m * i + k], recv_sem=recv_sem.at[m * i + k],
                device_id=(x, y, 1 - c), device_id_type=_MESH)

        copies = [copy(i, k, c) for i in range(n) for k in range(m)]
        for cp in copies:
            cp.start()
        for i in range(n):
            for k in range(m):
                copy(i, k, 1 - c).wait_recv()
        for cp in copies:
            cp.wait_send()

    return pl.pallas_call(
        body, name="gather_forward_to_sibling_" + tag,
        out_shape=[jax.ShapeDtypeStruct(a.shape, a.dtype) for a in lands],
        in_specs=[_ANY] * n, out_specs=[_ANY] * n,
        input_output_aliases={i: i for i in range(n)},
        scratch_shapes=[pltpu.SemaphoreType.DMA((m * n,)), pltpu.SemaphoreType.DMA((m * n,))],
    )(*lands)


def _swap_halves(grads, tag):
    n = len(grads)

    def body(*refs):
        ins, outs = refs[:n], refs[n:2 * n]
        send_sem, recv_sem = refs[2 * n:]
        x, y, c, _ = _place()
        copies = [pltpu.make_async_remote_copy(
            src_ref=ins[i].at[:, 1 - c], dst_ref=outs[i],
            send_sem=send_sem.at[i], recv_sem=recv_sem.at[i],
            device_id=(x, y, 1 - c), device_id_type=_MESH) for i in range(n)]
        for cp in copies:
            cp.start()
        for cp in copies:
            cp.wait()

    return pl.pallas_call(
        body, name="grad_swap_halves_" + tag,
        out_shape=[jax.ShapeDtypeStruct((N_CHIPS,) + g.shape[2:], g.dtype) for g in grads],
        in_specs=[_ANY] * n, out_specs=[_ANY] * n,
        scratch_shapes=[pltpu.SemaphoreType.DMA((n,)), pltpu.SemaphoreType.DMA((n,))],
    )(*grads)


def _sum_rows(h, C):
    return max(d for d in range(SUBLANES, h + 1, SUBLANES) if h % d == 0 and d * C <= 1 << 20)


def _pair_sum(g, r, c_idx, *, name):
    _, _, h, C = g.shape
    tr = _sum_rows(h, C)

    def body(c_ref, g_ref, r_ref, o_ref):
        o_ref[...] = (g_ref[...].astype(F32) + r_ref[...].astype(F32)).astype(o_ref.dtype)

    return pl.pallas_call(
        body, name=name, out_shape=jax.ShapeDtypeStruct((N_CHIPS, h, C), g.dtype),
        grid_spec=pltpu.PrefetchScalarGridSpec(
            num_scalar_prefetch=1, grid=(N_CHIPS, h // tr),
            in_specs=[pl.BlockSpec((None, None, tr, C), lambda j, i, s: (j, s[0], i, 0)),
                      pl.BlockSpec((None, tr, C), lambda j, i, s: (j, i, 0))],
            out_specs=pl.BlockSpec((None, tr, C), lambda j, i, s: (j, i, 0))),
        compiler_params=_params("parallel", "parallel"),
    )(c_idx, g, r)


def _owner_sum(s, r, jc_idx, *, name):
    _, h, C = s.shape
    tr = _sum_rows(h, C)

    def body(jc_ref, s_ref, r_ref, o_ref):
        acc = s_ref[...].astype(F32)
        for k in range(3):
            acc = acc + r_ref[k].astype(F32)
        o_ref[...] = acc

    return pl.pallas_call(
        body, name=name, out_shape=jax.ShapeDtypeStruct((2, h, C), F32),
        grid_spec=pltpu.PrefetchScalarGridSpec(
            num_scalar_prefetch=1, grid=(h // tr,),
            in_specs=[pl.BlockSpec((None, tr, C), lambda i, s: (s[0], i, 0)),
                      pl.BlockSpec((3, tr, C), lambda i, s: (0, i, 0))],
            out_specs=pl.BlockSpec((None, tr, C), lambda i, s: (s[1], i, 0))),
        compiler_params=_params("parallel"),
    )(jc_idx, s, r)


def _share_with_sibling(bufs, tag):
    n = len(bufs)

    def body(*refs):
        ins, outs = refs[:n], refs[n:2 * n]
        send_sem, recv_sem = refs[2 * n:]
        x, y, c, _ = _place()

        def copy(i, half):
            return pltpu.make_async_remote_copy(
                src_ref=ins[i].at[half], dst_ref=outs[i].at[half],
                send_sem=send_sem.at[i], recv_sem=recv_sem.at[i],
                device_id=(x, y, 1 - c), device_id_type=_MESH)

        copies = [copy(i, c) for i in range(n)]
        for cp in copies:
            cp.start()
        for i in range(n):
            copy(i, 1 - c).wait_recv()
        for cp in copies:
            cp.wait_send()

    return pl.pallas_call(
        body, name="grad_share_with_sibling_" + tag,
        out_shape=[jax.ShapeDtypeStruct(b.shape, b.dtype) for b in bufs],
        in_specs=[_ANY] * n, out_specs=[_ANY] * n,
        input_output_aliases={i: i for i in range(n)},
        scratch_shapes=[pltpu.SemaphoreType.DMA((n,)), pltpu.SemaphoreType.DMA((n,))],
    )(*bufs)


def _chip_sums(grads, c_idx, tag):
    views = [g.reshape(N_CHIPS, 2, g.shape[1] // 2, g.shape[2]) for g in grads]
    arrived = _swap_halves(views, tag)
    return [_pair_sum(v, r, c_idx, name=f"grad_pair_sum_{tag}_{i}") for i, (v, r) in enumerate(zip(views, arrived))]


def _owner_totals(sums, arrived, jc_idx, tag):
    halves = [_owner_sum(s, r, jc_idx, name=f"grad_owner_sum_{tag}_{i}") for i, (s, r) in enumerate(zip(sums, arrived))]
    return [f.reshape(-1, f.shape[2]) for f in _share_with_sibling(halves, tag)]


def _sum_devices(blocks):
    R = blocks.shape[2]
    tr = _sum_rows(R, 2 * N_CHIPS * LANES)

    def body(b_ref, o_ref):
        acc = b_ref[0, 0]
        for d in range(1, 2 * N_CHIPS):
            acc = acc + b_ref[d // 2, d % 2]
        o_ref[...] = acc

    return pl.pallas_call(
        body, name="sum_small_over_devices", out_shape=jax.ShapeDtypeStruct((R, LANES), F32),
        grid=(R // tr,),
        in_specs=[pl.BlockSpec((N_CHIPS, 2, tr, LANES), lambda i: (0, 0, i, 0))],
        out_specs=pl.BlockSpec((tr, LANES), lambda i: (i, 0)),
        compiler_params=_params("parallel"),
    )(blocks)


def _adamw(w, g, m, v, *, name):
    R, C = w.shape
    whole_fits = 7 * 2 * R * C * 4 <= VMEM_LIMIT_BYTES // 2
    tr = R if whole_fits else next(c for c in (256, 192, 128, 64, 32, 16, 8) if R % c == 0)

    def body(w_ref, g_ref, m_ref, v_ref, d_ref, nm_ref, nv_ref):
        g = g_ref[...]
        m = ADAM_B1 * m_ref[...] + (1.0 - ADAM_B1) * g
        v = ADAM_B2 * v_ref[...] + (1.0 - ADAM_B2) * (g * g)
        nm_ref[...] = m
        nv_ref[...] = v
        m_hat = m / (1.0 - ADAM_B1 ** ADAM_STEP)
        v_hat = v / (1.0 - ADAM_B2 ** ADAM_STEP)
        d_ref[...] = -ADAM_LR * (m_hat / (jnp.sqrt(v_hat) + ADAM_EPS) + ADAM_WD * w_ref[...])

    blk = pl.BlockSpec((tr, C), lambda i: (i, 0))
    sds = jax.ShapeDtypeStruct((R, C), F32)
    return pl.pallas_call(
        body, name=name, out_shape=(sds, sds, sds), grid=(R // tr,),
        in_specs=[blk] * 4, out_specs=(blk, blk, blk),
        compiler_params=_params("parallel"),
    )(w, g, m, v)


_TILE = SUBLANES * LANES


def _pack(arrays):
    rows = []
    for a in arrays:
        flat = a.reshape(-1)
        flat = jnp.pad(flat, (0, (-flat.shape[0]) % _TILE))
        rows.append(flat.reshape(-1, LANES))
    return jnp.concatenate(rows, axis=0)


def _unpack(buf, shapes):
    out, r = [], 0
    for s in shapes:
        size = math.prod(s)
        nr = -(-size // _TILE) * SUBLANES
        out.append(buf[r:r + nr].reshape(-1)[:size].reshape(s))
        r += nr
    return out


_BIG = ("w_in_a", "w_glu", "w_kv", "w_in_b", "w_mem_kv", "w_out")
_REPLICATED = ("pre_norm_g", "post_norm_g", "lam_re", "lam_im", "log_step", "b_re", "b_im", "c_re", "c_im",
               "kv_norm_g", "b_fgate", "mem_norm_g")
_SHARDED_SMALL = ("d_skip", "b_glu", "w_fgate")
_WEIGHTS = ("pre_norm_g", "post_norm_g", "w_in_a", "lam_re", "lam_im", "log_step", "b_re", "b_im", "c_re",
            "c_im", "d_skip", "w_glu", "b_glu", "kv_norm_g", "w_kv", "w_fgate", "b_fgate", "w_in_b",
            "mem_norm_g", "w_mem_kv", "w_out")


def _halves(a):
    return a.reshape(2, a.shape[0] // 2, a.shape[1])


def _unhalve(a):
    return a.reshape(N_CHIPS, 2 * a.shape[2], a.shape[3])


def _columns(a):
    return jnp.transpose(a, (1, 0, 2)).reshape(a.shape[1], N_CHIPS * a.shape[2])


def kernel(x, mem, pre_norm_g, post_norm_g, w_in_a, lam_re, lam_im, log_step, b_re, b_im, c_re, c_im, d_skip, w_glu, b_glu, kv_norm_g, w_kv, w_fgate, b_fgate, w_in_b, mem_norm_g, w_mem_kv, w_out, loss_target, m_pre_norm_g, m_post_norm_g, m_w_in_a, m_lam_re, m_lam_im, m_log_step, m_b_re, m_b_im, m_c_re, m_c_im, m_d_skip, m_w_glu, m_b_glu, m_kv_norm_g, m_w_kv, m_w_fgate, m_b_fgate, m_w_in_b, m_mem_norm_g, m_w_mem_kv, m_w_out, v_pre_norm_g, v_post_norm_g, v_w_in_a, v_lam_re, v_lam_im, v_log_step, v_b_re, v_b_im, v_c_re, v_c_im, v_d_skip, v_w_glu, v_b_glu, v_kv_norm_g, v_w_kv, v_w_fgate, v_b_fgate, v_w_in_b, v_mem_norm_g, v_w_mem_kv, v_w_out):
    a = dict(locals())
    xi, yi, ci = lax.axis_index("x"), lax.axis_index("y"), lax.axis_index("c")
    chip = 2 * xi + yi
    c_idx = jnp.reshape(ci, (1,)).astype(jnp.int32)
    jc_idx = jnp.stack([chip, ci]).astype(jnp.int32)

    vec = jnp.zeros((2 * SUBLANES, MAIN_WIDTH // N_CHIPS), F32)
    vec = vec.at[0].set(a["d_skip"][0]).at[1].set(a["b_glu"][0])
    def own_slot(gathered, parts):
        return [lax.dynamic_update_index_in_dim(g, p, chip, 0) for g, p in zip(gathered, parts)]

    parts_a = [_halves(a["w_in_a"][0].astype(BF16)), _halves(vec)]
    parts_b = [_halves(a["w_glu"][0].astype(BF16)), _halves(a["w_mem_kv"].reshape(-1, 2 * MEM_WIDTH).astype(BF16)),
               _halves(a["w_out"].reshape(-1, D_MODEL).astype(BF16))]
    parts_c = [_halves(a["w_kv"].astype(BF16)), _halves(_pad_lanes(a["w_fgate"]).astype(BF16)),
               _halves(a["w_in_b"][0].astype(BF16))]
    travelling, token = {}, a["pre_norm_g"]
    for tag, parts in (("a", parts_a), ("b", parts_b), ("c", parts_c)):
        lands = [lax.empty((N_CHIPS,) + p.shape, p.dtype) for p in parts]
        travelling[tag], token = _ici_start(parts, lands, token, _GATHER_ROUTE, name=f"gather_{tag}_start")

    def fetch(tag, after):
        parts, lands = _ici_wait(travelling[tag], after, _GATHER_ROUTE, name=f"gather_{tag}_wait")
        full = own_slot(_gather_forward(lands, tag), parts)
        if tag == "a":
            w_in_a, vecs = full
            return dict(w_in_a=_columns(_unhalve(w_in_a)), d_skip=vecs[:, 0, 0, :].reshape(MAIN_WIDTH),
                        b_glu=vecs[:, 0, 1, :].reshape(MAIN_WIDTH))
        if tag == "b":
            w_glu, w_mk, w_out = full
            return dict(w_glu=w_glu.reshape(MAIN_WIDTH, MAIN_WIDTH),
                        w_mem_kv=[w_mk[:, i].reshape(D_MODEL, 2 * MEM_WIDTH) for i in range(2)],
                        w_out=[w_out[:, i].reshape(D_MODEL, D_MODEL) for i in range(2)])
        w_kv, w_fg, w_in_b = full
        return dict(w_kv=_columns(_unhalve(w_kv)), w_fgate=w_fg.reshape(D_MODEL, LANES),
                    w_in_b=_columns(_unhalve(w_in_b)))

    w = dict(
        pre_norm_g=token, post_norm_g=a["post_norm_g"], mem_norm_g=a["mem_norm_g"],
        kv_norm_g=a["kv_norm_g"], b_fgate=a["b_fgate"],
        lam_re=a["lam_re"][0], lam_im=a["lam_im"][0], log_step=a["log_step"][0],
        b_re=a["b_re"][0], b_im=a["b_im"][0], c_re=a["c_re"][0], c_im=a["c_im"][0])

    sent = {}

    swapping = {}

    def grads_ready(event, g, token):
        tag = event.split("_")[0]
        if event in ("b", "a1"):
            big = {"b": lambda: [g["w_kv"], g["w_in_b"], g["w_mem_kv_1"].reshape(N_CHIPS, -1, 2 * MEM_WIDTH),
                                 g["w_out_1"].reshape(N_CHIPS, -1, D_MODEL)],
                   "a1": lambda: [g["w_glu"].reshape(N_CHIPS, -1, MAIN_WIDTH),
                                  g["w_mem_kv_0"].reshape(N_CHIPS, -1, 2 * MEM_WIDTH),
                                  g["w_out_0"].reshape(N_CHIPS, -1, D_MODEL)]}[tag]()
            views = [b.reshape(N_CHIPS, 2, b.shape[1] // 2, b.shape[2]) for b in big]
            lands = [lax.empty((N_CHIPS,) + v.shape[2:], v.dtype) for v in views]
            swapping[tag], token = _ici_start(views, lands, token, _SWAP_ROUTE, name=f"grad_swap_{tag}_start")
            return token
        if event == "a2":
            sums = _chip_sums([g["w_in_a"]], c_idx, tag)
        else:
            views, arrived = _ici_wait(swapping[tag], token, _SWAP_ROUTE, name=f"grad_swap_{tag}_wait")
            sums = [_pair_sum(v, r, c_idx, name=f"grad_pair_sum_{tag}_{i}")
                    for i, (v, r) in enumerate(zip(views, arrived))]
        lands = [lax.empty((3,) + s.shape[1:], s.dtype) for s in sums]
        sent[tag], token = _ici_start(sums, lands, token, _SCATTER_ROUTE, name=f"grad_send_{tag}_start")
        return token

    loss_row, grad_x, g = _local_step(a["x"][0], a["mem"][0], a["loss_target"][0], w, fetch, grads_ready)

    small_names = _REPLICATED + _SHARDED_SMALL
    pack = _pack([g[n] for n in small_names])
    blocks = lax.empty((N_CHIPS, 2) + pack.shape, F32)
    small_sent, loss_row = _ici_start([pack], [blocks], loss_row, _BLOCK_ROUTE, name="small_sums_start")
    loss = lax.psum(jnp.sum(loss_row), MESH_AXES)

    def totals(tag, after):
        sums, arrived = _ici_wait(sent[tag], after, _SCATTER_ROUTE, name=f"grad_send_{tag}_wait")
        return _owner_totals(sums, arrived, jc_idx, tag)

    r_kv, r_in_b, r_mk1, r_out1 = totals("b", grad_x)
    r_glu, r_mk0, r_out0 = totals("a1", r_out1)
    (r_in_a,) = totals("a2", r_out0)
    grads = {"w_in_a": r_in_a[None], "w_glu": r_glu[None], "w_kv": r_kv, "w_in_b": r_in_b[None],
             "w_mem_kv": jnp.stack([r_mk0, r_mk1]), "w_out": jnp.stack([r_out0, r_out1])}

    delta, new_m, new_v = {}, {}, {}
    for n in _BIG:
        shape = a[n].shape
        d2 = (-1, shape[-1])
        d, m, v = _adamw(a[n].reshape(d2), grads[n].reshape(d2), a["m_" + n].reshape(d2),
                         a["v_" + n].reshape(d2), name="adamw_" + n)
        delta[n], new_m[n], new_v[n] = d.reshape(shape), m.reshape(shape), v.reshape(shape)

    (pack,), (blocks,) = _ici_wait(small_sent, [delta[n] for n in _BIG], _BLOCK_ROUTE, name="small_sums_wait")
    blocks = lax.dynamic_update_slice(blocks, pack[None, None], (chip, ci, 0, 0))
    (blocks,) = _gather_forward([blocks], "small", own=True)
    small = dict(zip(small_names, _unpack(_sum_devices(blocks), [g[n].shape for n in small_names])))
    for n in _REPLICATED:
        grads[n] = small[n].reshape(a[n].shape)
    nd = MAIN_WIDTH // N_CHIPS
    grads["d_skip"] = lax.dynamic_slice(small["d_skip"], (chip * nd,), (nd,))[None]
    grads["b_glu"] = lax.dynamic_slice(small["b_glu"], (chip * nd,), (nd,))[None]
    nf = D_MODEL // N_CHIPS
    grads["w_fgate"] = lax.dynamic_slice(small["w_fgate"], (chip * nf, 0), (nf, FOX_HEADS))

    shapes = [a[n].shape for n in small_names]
    d, m, v = _adamw(_pack([a[n] for n in small_names]), _pack([grads[n] for n in small_names]),
                     _pack([a["m_" + n] for n in small_names]), _pack([a["v_" + n] for n in small_names]),
                     name="adamw_small")
    for n, dd, mm, vv in zip(small_names, _unpack(d, shapes), _unpack(m, shapes), _unpack(v, shapes)):
        delta[n], new_m[n], new_v[n] = dd, mm, vv

    return (loss, grad_x[None], *[grads[n] for n in _WEIGHTS], *[delta[n] for n in _WEIGHTS],
            *[new_m[n] for n in _WEIGHTS], *[new_v[n] for n in _WEIGHTS])
```

```python
import functools
import math

import jax
import jax.numpy as jnp
from jax import lax
from jax.experimental import pallas as pl
from jax.experimental.pallas import tpu as pltpu

F32 = jnp.float32
BF16 = jnp.bfloat16

D_MODEL = 2048
N_MEM = 256
MAIN_WIDTH = 1536
MEM_WIDTH = 512
IN_WIDTH = 2 * MAIN_WIDTH + 2 * MEM_WIDTH
HEAD_DIM = 128
FOX_HEADS = MAIN_WIDTH // HEAD_DIM
MEM_HEADS = MEM_WIDTH // HEAD_DIM
SSM_GROUP = 16
SSM_GROUPS = MAIN_WIDTH // SSM_GROUP
SSM_STATE = 64
GROUPS_PER_BLOCK = 8
SSM_BLOCKS = SSM_GROUPS // GROUPS_PER_BLOCK
STATE_COLS = GROUPS_PER_BLOCK * SSM_STATE
EPS = 1e-6
ADAM_LR = 0.001
ADAM_B1 = 0.9
ADAM_B2 = 0.999
ADAM_EPS = 1e-08
ADAM_WD = 0.01
ADAM_STEP = 10
N_CHIPS = 4
LANES = 128
SUBLANES = 8
VMEM_LIMIT_BYTES = 56 * 1024 * 1024
NEG_BIG = -1e30
MESH_AXES = ("x", "y", "c")


def _params(*sem):
    return pltpu.CompilerParams(dimension_semantics=sem if sem else None,
                                vmem_limit_bytes=VMEM_LIMIT_BYTES)


def _sigmoid(x):
    return 1.0 / (1.0 + jnp.exp(-x))


def _gelu(x):
    c = math.sqrt(2.0 / math.pi)
    return 0.5 * x * (1.0 + jnp.tanh(c * (x + 0.044715 * (x * x * x))))


def _gelu_grad(x):
    c = math.sqrt(2.0 / math.pi)
    t = jnp.tanh(c * (x + 0.044715 * (x * x * x)))
    return 0.5 * (1.0 + t) + 0.5 * x * (1.0 - t * t) * (c * (1.0 + 3.0 * 0.044715 * (x * x)))


def _silu_and_grad(z):
    s = _sigmoid(z)
    return z * s, s * (1.0 + z * (1.0 - s))


_TILE_CHOICES = (2048, 1024, 768, 512, 384, 256, LANES)


def _tile(n, cap):
    return next(c for c in _TILE_CHOICES if c <= cap and n % c == 0)


def _mm(a, b, *, name, ta=False, tb=False, out_dtype=F32, shards=1, tm=1024, tn=1024, tk=2048):
    if ta:
        K, M = a.shape
    else:
        M, K = a.shape
    if tb:
        N, kb = b.shape
    else:
        kb, N = b.shape
    assert K == kb, (a.shape, b.shape)
    ns = N // shards
    tm, tn, tk = _tile(M, tm), _tile(ns, tn), _tile(K, tk)
    assert M % tm == 0 and ns % tn == 0 and K % tk == 0 and N % shards == 0
    nk = K // tk
    dn = (((0 if ta else 1,), (1 if tb else 0,)), ((), ()))

    def body(a_ref, b_ref, o_ref, acc_ref):
        k = pl.program_id(2)

        @pl.when(k == 0)
        def _():
            acc_ref[...] = jnp.zeros_like(acc_ref)

        acc_ref[...] += lax.dot_general(a_ref[...].astype(BF16), b_ref[...].astype(BF16), dn,
                                        preferred_element_type=F32)

        @pl.when(k == nk - 1)
        def _():
            o_ref[...] = acc_ref[...].astype(o_ref.dtype)

    a_spec = (pl.BlockSpec((tk, tm), lambda i, j, k: (k, i)) if ta
              else pl.BlockSpec((tm, tk), lambda i, j, k: (i, k)))
    b_spec = (pl.BlockSpec((tn, tk), lambda i, j, k: (j, k)) if tb
              else pl.BlockSpec((tk, tn), lambda i, j, k: (k, j)))
    if shards == 1:
        out_shape = jax.ShapeDtypeStruct((M, N), out_dtype)
        o_spec = pl.BlockSpec((tm, tn), lambda i, j, k: (i, j))
    else:
        nb = ns // tn
        out_shape = jax.ShapeDtypeStruct((shards, M, ns), out_dtype)
        o_spec = pl.BlockSpec((None, tm, tn), lambda i, j, k: (j // nb, i, j % nb))
    return pl.pallas_call(
        body, name=name, out_shape=out_shape,
        grid=(M // tm, N // tn, nk),
        in_specs=[a_spec, b_spec], out_specs=o_spec,
        scratch_shapes=[pltpu.VMEM((tm, tn), F32)],
        compiler_params=_params("parallel", "parallel", "arbitrary"),
    )(a, b)


def _rmsnorm_fwd(x, g, *, name, res=None, out_dtype=F32, tr=256):
    L, D = x.shape
    tr = min(tr, L)
    has_res = res is not None

    def body(*refs):
        if has_res:
            x_ref, g_ref, r_ref, o_ref = refs
        else:
            x_ref, g_ref, o_ref = refs
        xf = x_ref[...]
        r = lax.rsqrt(jnp.mean(xf * xf, axis=-1, keepdims=True) + EPS)
        y = xf * r * g_ref[...]
        if has_res:
            y = r_ref[...] + y
        o_ref[...] = y.astype(o_ref.dtype)

    row = pl.BlockSpec((tr, D), lambda i: (i, 0))
    vec = pl.BlockSpec((1, D), lambda i: (0, 0))
    ins = [x, g.reshape(1, D)] + ([res] if has_res else [])
    return pl.pallas_call(
        body, name=name, out_shape=jax.ShapeDtypeStruct((L, D), out_dtype),
        grid=(L // tr,), in_specs=[row, vec] + ([row] if has_res else []), out_specs=row,
        compiler_params=_params("parallel"),
    )(*ins)


def _rmsnorm_bwd(x, g, dy, *, name, adds=(), dx_dtype=F32, tr=256):
    L, D = x.shape
    tr = min(tr, L)
    dys = dy if isinstance(dy, tuple) else (dy,)
    n_dy, n_add = len(dys), len(adds)

    def body(*refs):
        x_ref, g_ref = refs[:2]
        dy_refs = refs[2:2 + n_dy]
        add_refs = refs[2 + n_dy:2 + n_dy + n_add]
        dx_ref, dg_ref = refs[2 + n_dy + n_add:]
        xf = x_ref[...]
        dyf = dy_refs[0][...].astype(F32)
        for d_ref in dy_refs[1:]:
            dyf = dyf + d_ref[...].astype(F32)
        r = lax.rsqrt(jnp.mean(xf * xf, axis=-1, keepdims=True) + EPS)
        gy = dyf * g_ref[...]
        c = jnp.mean(xf * gy, axis=-1, keepdims=True) * (r * r * r)
        dx = gy * r - xf * c
        for a_ref in add_refs:
            dx = dx + a_ref[...].astype(F32)
        dx_ref[...] = dx.astype(dx_ref.dtype)

        @pl.when(pl.program_id(0) == 0)
        def _():
            dg_ref[...] = jnp.zeros_like(dg_ref)

        dg_ref[...] += jnp.sum(dyf * xf * r, axis=0, keepdims=True)

    row = pl.BlockSpec((tr, D), lambda i: (i, 0))
    vec = pl.BlockSpec((1, D), lambda i: (0, 0))
    dx, dg = pl.pallas_call(
        body, name=name,
        out_shape=(jax.ShapeDtypeStruct((L, D), dx_dtype), jax.ShapeDtypeStruct((1, D), F32)),
        grid=(L // tr,), in_specs=[row, vec] + [row] * (n_dy + n_add), out_specs=(row, vec),
        compiler_params=_params("arbitrary"),
    )(x, g.reshape(1, D), *dys, *adds)
    return dx, dg.reshape(D)


def _final_norm_loss(o, g, res, target, *, tr=256):
    L, D = o.shape
    tr = min(tr, L)

    def body(o_ref, g_ref, r_ref, t_ref, dh_ref, loss_ref):
        xf = o_ref[...]
        r = lax.rsqrt(jnp.mean(xf * xf, axis=-1, keepdims=True) + EPS)
        e = (r_ref[...] + xf * r * g_ref[...]) - t_ref[...]
        dh_ref[...] = e * (1.0 / D)

        @pl.when(pl.program_id(0) == 0)
        def _():
            loss_ref[...] = jnp.zeros_like(loss_ref)

        loss_ref[...] += jnp.sum(e * e, axis=0, keepdims=True) * (0.5 / D)

    row = pl.BlockSpec((tr, D), lambda i: (i, 0))
    vec = pl.BlockSpec((1, D), lambda i: (0, 0))
    dh, lp = pl.pallas_call(
        body, name="post_norm_1_loss",
        out_shape=(jax.ShapeDtypeStruct((L, D), F32), jax.ShapeDtypeStruct((1, D), F32)),
        grid=(L // tr,), in_specs=[row, vec, row, row], out_specs=(row, vec),
        compiler_params=_params("arbitrary"),
    )(o, g.reshape(1, D), res, target)
    return dh, lp


def _s5_coeffs(lr, li, ls):
    dt = jnp.exp(ls)
    mag = jnp.exp(lr * dt)
    ar = mag * jnp.cos(li * dt)
    ai = mag * jnp.sin(li * dt)
    den = lr * lr + li * li
    cr = ((ar - 1.0) * lr + ai * li) / den
    ci = (ai * lr - (ar - 1.0) * li) / den
    return dt, ar, ai, den, cr, ci


def _s5_prep(lam_re, lam_im, log_step, b_re_t, b_im_t):
    G, P = lam_re.shape
    H = b_re_t.shape[1]

    def body(lr_ref, li_ref, ls_ref, br_ref, bi_ref, ar_ref, ai_ref, bbr_ref, bbi_ref):
        _, ar, ai, _, cr, ci = _s5_coeffs(lr_ref[...], li_ref[...], ls_ref[...])
        ar_ref[...] = ar
        ai_ref[...] = ai
        br, bi = br_ref[...], bi_ref[...]
        crb, cib = cr[:, None, :], ci[:, None, :]
        bbr_ref[...] = crb * br - cib * bi
        bbi_ref[...] = crb * bi + cib * br

    return pl.pallas_call(
        body, name="s5_prep",
        out_shape=(jax.ShapeDtypeStruct((G, P), F32), jax.ShapeDtypeStruct((G, P), F32),
                   jax.ShapeDtypeStruct((G, H, P), F32), jax.ShapeDtypeStruct((G, H, P), F32)),
        compiler_params=_params(),
    )(lam_re, lam_im, log_step.reshape(G, 1), b_re_t, b_im_t)


def _s5_prep_bwd(lam_re, lam_im, log_step, b_re_t, b_im_t, d_ar, d_ai, d_bbr, d_bbi):
    G, P = lam_re.shape
    H = b_re_t.shape[1]

    def body(lr_ref, li_ref, ls_ref, br_ref, bi_ref, dar_ref, dai_ref, dbbr_ref, dbbi_ref,
             dlr_ref, dli_ref, dls_ref, dbr_ref, dbi_ref):
        lr, li = lr_ref[...], li_ref[...]
        dt, ar, ai, den, cr, ci = _s5_coeffs(lr, li, ls_ref[...])
        br, bi = br_ref[...], bi_ref[...]
        gbr, gbi = dbbr_ref[...], dbbi_ref[...]
        crb, cib = cr[:, None, :], ci[:, None, :]
        dbr_ref[...] = crb * gbr + cib * gbi
        dbi_ref[...] = crb * gbi - cib * gbr
        gcr = jnp.sum(br * gbr + bi * gbi, axis=1)
        gci = jnp.sum(br * gbi - bi * gbr, axis=1)
        ilr, ili = lr / den, -li / den
        gar = dar_ref[...] + (ilr * gcr + ili * gci)
        gai = dai_ref[...] + (ilr * gci - ili * gcr)
        qr, qi = cr * ilr - ci * ili, cr * ili + ci * ilr
        glr = -(qr * gcr + qi * gci)
        gli = -(qr * gci - qi * gcr)
        glr = glr + dt * (ar * gar + ai * gai)
        gli = gli + dt * (ar * gai - ai * gar)
        wr, wi = lr * ar - li * ai, lr * ai + li * ar
        gdt = jnp.sum(wr * gar + wi * gai, axis=1, keepdims=True)
        dlr_ref[...] = glr
        dli_ref[...] = gli
        dls_ref[...] = gdt * dt

    return pl.pallas_call(
        body, name="s5_prep_bwd",
        out_shape=(jax.ShapeDtypeStruct((G, P), F32), jax.ShapeDtypeStruct((G, P), F32),
                   jax.ShapeDtypeStruct((G, 1), F32),
                   jax.ShapeDtypeStruct((G, H, P), F32), jax.ShapeDtypeStruct((G, H, P), F32)),
        compiler_params=_params(),
    )(lam_re, lam_im, log_step.reshape(G, 1), b_re_t, b_im_t, d_ar, d_ai, d_bbr, d_bbi)


def _s5_block_mats(bbr_t, bbi_t, c_re, c_im):
    bmat = _s5_expand(bbr_t, bbi_t)
    cmat = jnp.transpose(_s5_expand(c_re, -c_im), (0, 2, 1))
    return bmat.astype(BF16), cmat.astype(BF16)


def _s5_diag_mask():
    r = lax.broadcasted_iota(jnp.int32, (LANES, 2 * STATE_COLS), 0) // SSM_GROUP
    c = (lax.broadcasted_iota(jnp.int32, (LANES, 2 * STATE_COLS), 1) % STATE_COLS) // SSM_STATE
    return (r == c).astype(F32)


def _s5_expand(re, im):
    re = jnp.tile(re.reshape(SSM_BLOCKS, LANES, SSM_STATE), (1, 1, GROUPS_PER_BLOCK))
    im = jnp.tile(im.reshape(SSM_BLOCKS, LANES, SSM_STATE), (1, 1, GROUPS_PER_BLOCK))
    return jnp.concatenate([re, im], axis=-1) * _s5_diag_mask()[None]


def _s5_block_diag(dmat):
    d = dmat * _s5_diag_mask()[None]
    parts = []
    for ri in range(2):
        acc = 0.0
        for g in range(GROUPS_PER_BLOCK):
            c0 = ri * STATE_COLS + g * SSM_STATE
            acc = acc + d[:, :, c0:c0 + SSM_STATE]
        parts.append(acc.reshape(SSM_GROUPS, SSM_GROUP, SSM_STATE))
    return jnp.stack(parts)


def _s5_a_rows(ar, ai):
    a = jnp.concatenate([ar.reshape(SSM_BLOCKS, STATE_COLS), ai.reshape(SSM_BLOCKS, STATE_COLS)], axis=1)
    return jnp.broadcast_to(a[:, None, :], (SSM_BLOCKS, SUBLANES, 2 * STATE_COLS))


def _to_step_major(src_ref, dst_ref, seg):
    for s in range(SUBLANES):
        dst_ref[pl.ds(s, seg, stride=SUBLANES), :] = src_ref[pl.ds(seg * s, seg), :]


def _segment_rows(ref, s, seg):
    return ref[pl.ds(s, seg, stride=SUBLANES), :]


def _cmul(ar, ai, xr, xi):
    return ar * xr - ai * xi, ar * xi + ai * xr


def _s5_tables(a_ref, pw_s, pwr_s, S, seg):
    ar, ai = a_ref[:, :S], a_ref[:, S:]

    def step(i, c):
        pr, pi = c
        pw_s[i, :, :S] = pr
        pw_s[i, :, S:] = pi
        nr, ni = _cmul(ar, ai, pr, pi)
        pwr_s[seg - 1 - i, :, :S] = nr
        pwr_s[seg - 1 - i, :, S:] = ni
        return nr, ni

    pr, pi = lax.fori_loop(0, seg, step, (jnp.ones_like(ar), jnp.zeros_like(ai)))
    pw_s[seg, :, :S] = pr
    pw_s[seg, :, S:] = pi


def _s5_fwd(proj, bmat, cmat, a_rows, d_skip, *, tc=512):
    L = proj.shape[0]
    tc = min(tc, L)
    nt = L // tc
    seg = tc // SUBLANES
    S = STATE_COLS

    def body(u_ref, b_ref, c_ref, a_ref, d_ref, y_ref, yg_ref, xp_ref,
             bu_s, xp_s, pw_s, pwr_s, carry_s, e_s, up_s, yc_s):
        @pl.when(pl.program_id(1) == 0)
        def _():
            carry_s[...] = jnp.zeros_like(carry_s)
            _s5_tables(a_ref, pw_s, pwr_s, S, seg)

        ar, ai = a_ref[:, :S], a_ref[:, S:]
        _to_step_major(u_ref, up_s, seg)
        bu = jnp.dot(up_s[...].astype(BF16), b_ref[...], preferred_element_type=F32)
        bu_s[...] = bu.reshape(seg, SUBLANES, 2 * S)

        def step(i, carry):
            cr, ci = carry
            xp_s[i, :, :S] = cr
            xp_s[i, :, S:] = ci
            return ar * cr - ai * ci + bu_s[i, :, :S], ar * ci + ai * cr + bu_s[i, :, S:]

        zero = jnp.zeros((SUBLANES, S), F32)
        fr, fi = lax.fori_loop(0, seg, step, (zero, zero))
        pr, pi = pw_s[seg, 0:1, :S], pw_s[seg, 0:1, S:]
        er, ei = carry_s[0:1, :S], carry_s[0:1, S:]
        for s in range(SUBLANES):
            e_s[s:s + 1, :S] = er
            e_s[s:s + 1, S:] = ei
            tr, ti = _cmul(pr, pi, er, ei)
            er, ei = fr[s:s + 1] + tr, fi[s:s + 1] + ti
        carry_s[0:1, :S] = er
        carry_s[0:1, S:] = ei
        pw = pw_s[0:seg]
        tr, ti = _cmul(pw[:, :, :S], pw[:, :, S:], e_s[:, :S][None], e_s[:, S:][None])
        xl = xp_s[...]
        xp = jnp.concatenate([xl[:, :, :S] + tr, xl[:, :, S:] + ti], axis=-1).reshape(tc, 2 * S)
        xp_ref[...] = xp
        a1r, a1i = ar[0:1], ai[0:1]
        x_re = a1r * xp[:, :S] - a1i * xp[:, S:] + bu[:, :S]
        x_im = a1r * xp[:, S:] + a1i * xp[:, :S] + bu[:, S:]
        xs = jnp.concatenate([x_re, x_im], axis=1).astype(BF16)
        yc_s[...] = jnp.dot(xs, c_ref[...], preferred_element_type=F32)
        for s in range(SUBLANES):
            rows = pl.ds(seg * s, seg)
            y = _segment_rows(yc_s, s, seg) + d_ref[...] * u_ref[rows, :]
            y_ref[rows, :] = y
            yg_ref[rows, :] = _gelu(y).astype(BF16)

    return pl.pallas_call(
        body, name="s5_fwd",
        out_shape=(jax.ShapeDtypeStruct((L, MAIN_WIDTH), F32),
                   jax.ShapeDtypeStruct((L, MAIN_WIDTH), BF16),
                   jax.ShapeDtypeStruct((L, SSM_BLOCKS * 2 * S), F32)),
        grid=(SSM_BLOCKS, nt),
        in_specs=[pl.BlockSpec((tc, LANES), lambda b, t: (t, b)),
                  pl.BlockSpec((None, LANES, 2 * S), lambda b, t: (b, 0, 0)),
                  pl.BlockSpec((None, 2 * S, LANES), lambda b, t: (b, 0, 0)),
                  pl.BlockSpec((None, SUBLANES, 2 * S), lambda b, t: (b, 0, 0)),
                  pl.BlockSpec((1, LANES), lambda b, t: (0, b))],
        out_specs=(pl.BlockSpec((tc, LANES), lambda b, t: (t, b)),
                   pl.BlockSpec((tc, LANES), lambda b, t: (t, b)),
                   pl.BlockSpec((tc, 2 * S), lambda b, t: (t, b))),
        scratch_shapes=[pltpu.VMEM((seg, SUBLANES, 2 * S), F32),
                        pltpu.VMEM((seg, SUBLANES, 2 * S), F32),
                        pltpu.VMEM((seg + 1, SUBLANES, 2 * S), F32),
                        pltpu.VMEM((seg, SUBLANES, 2 * S), F32),
                        pltpu.VMEM((SUBLANES, 2 * S), F32),
                        pltpu.VMEM((SUBLANES, 2 * S), F32),
                        pltpu.VMEM((tc, LANES), F32),
                        pltpu.VMEM((tc, LANES), F32)],
        compiler_params=_params("parallel", "arbitrary"),
    )(proj, bmat, cmat, a_rows, d_skip.reshape(1, MAIN_WIDTH))


def _s5_bwd(proj, dyg_a, dyg_b, y, xp, bmat, cmat, a_rows, d_skip, dproj, *, tc=512):
    L = proj.shape[0]
    tc = min(tc, L)
    nt = L // tc
    seg = tc // SUBLANES
    S = STATE_COLS
    nn = (((1,), (1,)), ((), ()))
    tn = (((0,), (0,)), ((), ()))

    def body(u_ref, dyga_ref, dygb_ref, y_ref, xp_ref, b_ref, c_ref, a_ref, d_ref, dp_hbm,
             du_ref, db_ref, dc_ref, da_ref, dd_ref, dl_s, pw_s, pwr_s, carry_s, e_s, up_s, dy_s, dyp_s, dup_s):
        @pl.when(pl.program_id(1) == 0)
        def _():
            carry_s[...] = jnp.zeros_like(carry_s)
            db_ref[...] = jnp.zeros_like(db_ref)
            dc_ref[...] = jnp.zeros_like(dc_ref)
            da_ref[...] = jnp.zeros_like(da_ref)
            dd_ref[...] = jnp.zeros_like(dd_ref)
            _s5_tables(a_ref, pw_s, pwr_s, S, seg)

        ar, ai = a_ref[:, :S], a_ref[:, S:]
        a1r, a1i = ar[0:1], ai[0:1]
        u = u_ref[...]
        dy = (dyga_ref[...] + dygb_ref[...]) * _gelu_grad(y_ref[...])
        dy_s[...] = dy
        xp = xp_ref[...]
        _to_step_major(u_ref, up_s, seg)
        _to_step_major(dy_s, dyp_s, seg)
        ubp = up_s[...].astype(BF16)
        dyp = dyp_s[...].astype(BF16)
        bu = jnp.dot(ubp, b_ref[...], preferred_element_type=F32)
        x_re = a1r * xp[:, :S] - a1i * xp[:, S:] + bu[:, :S]
        x_im = a1r * xp[:, S:] + a1i * xp[:, :S] + bu[:, S:]
        xs = jnp.concatenate([x_re, x_im], axis=1).astype(BF16)
        dc_ref[...] += lax.dot_general(dyp, xs, tn, preferred_element_type=F32)
        dx = lax.dot_general(dyp, c_ref[...], nn, preferred_element_type=F32)
        dl_s[...] = dx.reshape(seg, SUBLANES, 2 * S)

        def step(k, carry):
            cr, ci = carry
            i = seg - 1 - k
            lr = dl_s[i, :, :S] + (ar * cr + ai * ci)
            li = dl_s[i, :, S:] + (ar * ci - ai * cr)
            dl_s[i, :, :S] = lr
            dl_s[i, :, S:] = li
            return lr, li

        zero = jnp.zeros((SUBLANES, S), F32)
        fr, fi = lax.fori_loop(0, seg, step, (zero, zero))
        pr, pi = pw_s[seg, 0:1, :S], pw_s[seg, 0:1, S:]
        er, ei = carry_s[0:1, :S], carry_s[0:1, S:]
        for s in range(SUBLANES - 1, -1, -1):
            e_s[s:s + 1, :S] = er
            e_s[s:s + 1, S:] = ei
            er, ei = fr[s:s + 1] + (pr * er + pi * ei), fi[s:s + 1] + (pr * ei - pi * er)
        carry_s[0:1, :S] = er
        carry_s[0:1, S:] = ei
        er, ei = e_s[:, :S][None], e_s[:, S:][None]
        pw = pwr_s[...]
        pwr, pwi = pw[:, :, :S], pw[:, :, S:]
        ll = dl_s[...]
        lam = jnp.concatenate([ll[:, :, :S] + (pwr * er + pwi * ei), ll[:, :, S:] + (pwr * ei - pwi * er)],
                              axis=-1).reshape(tc, 2 * S)
        l_re, l_im = lam[:, :S], lam[:, S:]
        da_ref[0:1, :S] += jnp.sum(l_re * xp[:, :S] + l_im * xp[:, S:], axis=0, keepdims=True)
        da_ref[0:1, S:] += jnp.sum(l_im * xp[:, :S] - l_re * xp[:, S:], axis=0, keepdims=True)
        lamb = lam.astype(BF16)
        dup_s[...] = lax.dot_general(lamb, b_ref[...], nn, preferred_element_type=F32)
        for s in range(SUBLANES):
            rows = pl.ds(seg * s, seg)
            du = _segment_rows(dup_s, s, seg) + d_ref[...] * dy_s[rows, :]
            du_ref[rows, :] = du.astype(du_ref.dtype)
        db_ref[...] += lax.dot_general(ubp, lamb, tn, preferred_element_type=F32)
        dd_ref[0:1, :] += jnp.sum(dy * u, axis=0, keepdims=True)

    rev = lambda b, t: (nt - 1 - t, b)
    return pl.pallas_call(
        body, name="s5_bwd",
        out_shape=(jax.ShapeDtypeStruct(dproj.shape, dproj.dtype),
                   jax.ShapeDtypeStruct((SSM_BLOCKS, LANES, 2 * S), F32),
                   jax.ShapeDtypeStruct((SSM_BLOCKS, LANES, 2 * S), F32),
                   jax.ShapeDtypeStruct((SSM_BLOCKS, SUBLANES, 2 * S), F32),
                   jax.ShapeDtypeStruct((SUBLANES, MAIN_WIDTH), F32)),
        input_output_aliases={9: 0},
        grid=(SSM_BLOCKS, nt),
        in_specs=[pl.BlockSpec((tc, LANES), rev),
                  pl.BlockSpec((tc, LANES), rev),
                  pl.BlockSpec((tc, LANES), rev),
                  pl.BlockSpec((tc, LANES), rev),
                  pl.BlockSpec((tc, 2 * S), rev),
                  pl.BlockSpec((None, LANES, 2 * S), lambda b, t: (b, 0, 0)),
                  pl.BlockSpec((None, 2 * S, LANES), lambda b, t: (b, 0, 0)),
                  pl.BlockSpec((None, SUBLANES, 2 * S), lambda b, t: (b, 0, 0)),
                  pl.BlockSpec((1, LANES), lambda b, t: (0, b)),
                  _ANY],
        out_specs=(pl.BlockSpec((tc, LANES), rev),
                   pl.BlockSpec((None, LANES, 2 * S), lambda b, t: (b, 0, 0)),
                   pl.BlockSpec((None, LANES, 2 * S), lambda b, t: (b, 0, 0)),
                   pl.BlockSpec((None, SUBLANES, 2 * S), lambda b, t: (b, 0, 0)),
                   pl.BlockSpec((SUBLANES, LANES), lambda b, t: (0, b))),
        scratch_shapes=[pltpu.VMEM((seg, SUBLANES, 2 * S), F32),
                        pltpu.VMEM((seg + 1, SUBLANES, 2 * S), F32),
                        pltpu.VMEM((seg, SUBLANES, 2 * S), F32),
                        pltpu.VMEM((SUBLANES, 2 * S), F32),
                        pltpu.VMEM((SUBLANES, 2 * S), F32),
                        pltpu.VMEM((tc, LANES), F32),
                        pltpu.VMEM((tc, LANES), F32),
                        pltpu.VMEM((tc, LANES), F32),
                        pltpu.VMEM((tc, LANES), F32)],
        compiler_params=_params("parallel", "arbitrary"),
    )(proj, dyg_a, dyg_b, y, xp, bmat, cmat, a_rows, d_skip.reshape(1, MAIN_WIDTH), dproj)


_Z_COLS = slice(MAIN_WIDTH, 2 * MAIN_WIDTH)
_ZM_COLS = slice(2 * MAIN_WIDTH + MEM_WIDTH, IN_WIDTH)


def _proj_rows(tr):
    return pl.BlockSpec((tr, IN_WIDTH), lambda i: (i, 0))


def _row_specs(tr):
    main = pl.BlockSpec((tr, MAIN_WIDTH), lambda i: (i, 0))
    z = pl.BlockSpec((tr, MAIN_WIDTH), lambda i: (i, 1))
    zm = pl.BlockSpec((tr, MEM_WIDTH), lambda i: (i, IN_WIDTH // MEM_WIDTH - 1))
    mem = pl.BlockSpec((tr, MEM_WIDTH), lambda i: (i, 0))
    cat = pl.BlockSpec((tr, D_MODEL), lambda i: (i, 0))
    vec = pl.BlockSpec((1, MAIN_WIDTH), lambda i: (0, 0))
    return main, z, zm, mem, cat, vec


def _gate_a_fwd(y, t, b_glu, proj, o_mem, *, tr=256):
    L = y.shape[0]
    tr = min(tr, L)

    def body(y_ref, t_ref, b_ref, z_ref, zm_ref, om_ref, o_ref):
        yg = _gelu(y_ref[...])
        sz, _ = _silu_and_grad(z_ref[...])
        o_ref[:, :MAIN_WIDTH] = (yg * _sigmoid(t_ref[...] + b_ref[...]) * sz).astype(BF16)
        szm, _ = _silu_and_grad(zm_ref[...])
        o_ref[:, MAIN_WIDTH:] = (om_ref[...] * szm).astype(BF16)

    main, z, zm, mem, cat, vec = _row_specs(tr)
    return pl.pallas_call(
        body, name="gate_a_fwd", out_shape=jax.ShapeDtypeStruct((L, D_MODEL), BF16),
        grid=(L // tr,), in_specs=[main, main, vec, z, zm, mem], out_specs=cat,
        compiler_params=_params("parallel"),
    )(y, t, b_glu.reshape(1, MAIN_WIDTH), proj, proj, o_mem)


def _gate_a_bwd(dcat, y, t, b_glu, proj, o_mem, *, tr=256):
    L = y.shape[0]
    tr = min(tr, L)

    def body(dc_ref, y_ref, t_ref, b_ref, z_ref, zm_ref, om_ref,
             dp_ref, dt_ref, dyg_ref, dom_ref, db_ref):
        dmain = dc_ref[:, :MAIN_WIDTH]
        dmemo = dc_ref[:, MAIN_WIDTH:]
        yg = _gelu(y_ref[...])
        sg = _sigmoid(t_ref[...] + b_ref[...])
        sz, gz = _silu_and_grad(z_ref[...])
        dp_ref[:, _Z_COLS] = (dmain * (yg * sg) * gz).astype(BF16)
        dy2 = dmain * sz
        dyg_ref[...] = dy2 * sg
        dt = dy2 * yg * (sg * (1.0 - sg))
        dt_ref[...] = dt.astype(BF16)

        @pl.when(pl.program_id(0) == 0)
        def _():
            db_ref[...] = jnp.zeros_like(db_ref)

        db_ref[...] += jnp.sum(dt, axis=0, keepdims=True)
        szm, gzm = _silu_and_grad(zm_ref[...])
        dom_ref[...] = dmemo * szm
        dp_ref[:, _ZM_COLS] = (dmemo * om_ref[...] * gzm).astype(BF16)

    main, z, zm, mem, cat, vec = _row_specs(tr)
    outs = pl.pallas_call(
        body, name="gate_a_bwd",
        out_shape=(jax.ShapeDtypeStruct((L, IN_WIDTH), BF16),
                   jax.ShapeDtypeStruct((L, MAIN_WIDTH), BF16), jax.ShapeDtypeStruct((L, MAIN_WIDTH), F32),
                   jax.ShapeDtypeStruct((L, MEM_WIDTH), F32), jax.ShapeDtypeStruct((1, MAIN_WIDTH), F32)),
        grid=(L // tr,), in_specs=[cat, main, main, vec, z, zm, mem],
        out_specs=(_proj_rows(tr), main, main, mem, vec),
        compiler_params=_params("arbitrary"),
    )(dcat, y, t, b_glu.reshape(1, MAIN_WIDTH), proj, proj, o_mem)
    return outs


def _gate_b_fwd(att, proj, o_mem, *, tr=256):
    L = att.shape[0]
    tr = min(tr, L)

    def body(a_ref, z_ref, zm_ref, om_ref, o_ref):
        sz, _ = _silu_and_grad(z_ref[...])
        o_ref[:, :MAIN_WIDTH] = (a_ref[...] * sz).astype(BF16)
        szm, _ = _silu_and_grad(zm_ref[...])
        o_ref[:, MAIN_WIDTH:] = (om_ref[...] * szm).astype(BF16)

    main, z, zm, mem, cat, _ = _row_specs(tr)
    return pl.pallas_call(
        body, name="gate_b_fwd", out_shape=jax.ShapeDtypeStruct((L, D_MODEL), BF16),
        grid=(L // tr,), in_specs=[main, z, zm, mem], out_specs=cat,
        compiler_params=_params("parallel"),
    )(att, proj, proj, o_mem)


def _gate_b_bwd(dcat, att, proj, o_mem, *, tr=256):
    L = att.shape[0]
    tr = min(tr, L)

    def body(dc_ref, a_ref, z_ref, zm_ref, om_ref, da_ref, dp_ref, dom_ref, dl_ref):
        dmain = dc_ref[:, :MAIN_WIDTH]
        dmemo = dc_ref[:, MAIN_WIDTH:]
        att = a_ref[...]
        sz, gz = _silu_and_grad(z_ref[...])
        datt = dmain * sz
        da_ref[...] = datt
        dp_ref[:, _Z_COLS] = (dmain * att * gz).astype(BF16)
        szm, gzm = _silu_and_grad(zm_ref[...])
        dom_ref[...] = dmemo * szm
        dp_ref[:, _ZM_COLS] = (dmemo * om_ref[...] * gzm).astype(BF16)
        prod = datt * att
        for h in range(FOX_HEADS):
            dl_ref[h] = jnp.sum(prod[:, h * HEAD_DIM:(h + 1) * HEAD_DIM], axis=1, keepdims=True)

    main, z, zm, mem, cat, _ = _row_specs(tr)
    delta = pl.BlockSpec((FOX_HEADS, tr, 1), lambda i: (0, i, 0))
    return pl.pallas_call(
        body, name="gate_b_bwd",
        out_shape=(jax.ShapeDtypeStruct((L, MAIN_WIDTH), F32), jax.ShapeDtypeStruct((L, IN_WIDTH), BF16),
                   jax.ShapeDtypeStruct((L, MEM_WIDTH), F32), jax.ShapeDtypeStruct((FOX_HEADS, L, 1), F32)),
        grid=(L // tr,), in_specs=[cat, main, z, zm, mem], out_specs=(main, _proj_rows(tr), mem, delta),
        compiler_params=_params("parallel"),
    )(dcat, att, proj, proj, o_mem)


_MEM_Q_COL = (2 * MAIN_WIDTH) // HEAD_DIM
_NT = (((1,), (1,)), ((), ()))
_TN = (((0,), (0,)), ((), ()))


def _mem_probs(q_ref, k_ref):
    qs = (q_ref[...] * (HEAD_DIM ** -0.5)).astype(BF16)
    s = lax.dot_general(qs, k_ref[...].astype(BF16), _NT, preferred_element_type=F32)
    e = jnp.exp(s - jnp.max(s, axis=-1, keepdims=True))
    return qs, e / jnp.sum(e, axis=-1, keepdims=True)


def _mem_attn_fwd(proj, kvm, *, tq=2048):
    L = proj.shape[0]
    tq = min(tq, L)

    def body(q_ref, k_ref, v_ref, o_ref):
        _, p = _mem_probs(q_ref, k_ref)
        o_ref[...] = jnp.dot(p.astype(BF16), v_ref[...].astype(BF16), preferred_element_type=F32)

    return pl.pallas_call(
        body, name="mem_attn_fwd", out_shape=jax.ShapeDtypeStruct((L, MEM_WIDTH), F32),
        grid=(MEM_HEADS, L // tq),
        in_specs=[pl.BlockSpec((tq, HEAD_DIM), lambda h, i: (i, _MEM_Q_COL + h)),
                  pl.BlockSpec((N_MEM, HEAD_DIM), lambda h, i: (0, h)),
                  pl.BlockSpec((N_MEM, HEAD_DIM), lambda h, i: (0, MEM_HEADS + h))],
        out_specs=pl.BlockSpec((tq, HEAD_DIM), lambda h, i: (i, h)),
        compiler_params=_params("parallel", "parallel"),
    )(proj, kvm, kvm)


def _mem_attn_bwd(proj, kvm, do, dproj, *, tq=2048):
    L = proj.shape[0]
    tq = min(tq, L)

    def body(q_ref, k_ref, v_ref, do_ref, dp_hbm, dq_ref, dk_ref, dv_ref):
        @pl.when(pl.program_id(1) == 0)
        def _():
            dk_ref[...] = jnp.zeros_like(dk_ref)
            dv_ref[...] = jnp.zeros_like(dv_ref)

        qs, p = _mem_probs(q_ref, k_ref)
        dob = do_ref[...].astype(BF16)
        dp = lax.dot_general(dob, v_ref[...].astype(BF16), _NT, preferred_element_type=F32)
        ds = p * (dp - jnp.sum(p * dp, axis=-1, keepdims=True))
        dsb = ds.astype(BF16)
        dq = jnp.dot(dsb, k_ref[...].astype(BF16), preferred_element_type=F32) * (HEAD_DIM ** -0.5)
        dq_ref[...] = dq.astype(BF16)
        dk_ref[...] += lax.dot_general(dsb, qs, _TN, preferred_element_type=F32)
        dv_ref[...] += lax.dot_general(p.astype(BF16), dob, _TN, preferred_element_type=F32)

    dproj, dk, dv = pl.pallas_call(
        body, name="mem_attn_bwd",
        out_shape=(jax.ShapeDtypeStruct(dproj.shape, dproj.dtype),
                   jax.ShapeDtypeStruct((N_MEM, MEM_WIDTH), F32),
                   jax.ShapeDtypeStruct((N_MEM, MEM_WIDTH), F32)),
        grid=(MEM_HEADS, L // tq),
        in_specs=[pl.BlockSpec((tq, HEAD_DIM), lambda h, i: (i, _MEM_Q_COL + h)),
                  pl.BlockSpec((N_MEM, HEAD_DIM), lambda h, i: (0, h)),
                  pl.BlockSpec((N_MEM, HEAD_DIM), lambda h, i: (0, MEM_HEADS + h)),
                  pl.BlockSpec((tq, HEAD_DIM), lambda h, i: (i, h)),
                  _ANY],
        out_specs=(pl.BlockSpec((tq, HEAD_DIM), lambda h, i: (i, _MEM_Q_COL + h)),
                   pl.BlockSpec((N_MEM, HEAD_DIM), lambda h, i: (0, h)),
                   pl.BlockSpec((N_MEM, HEAD_DIM), lambda h, i: (0, h))),
        input_output_aliases={4: 0},
        compiler_params=_params("parallel", "arbitrary"),
    )(proj, kvm, kvm, do, dproj)
    return dproj, jnp.concatenate([dk, dv], axis=1)


def _tile_cumsum(x, row, reverse):
    for sh in (1, 2, 4):
        if reverse:
            x = x + jnp.where(row < SUBLANES - sh, pltpu.roll(x, SUBLANES - sh, 0), 0.0)
        else:
            x = x + jnp.where(row >= sh, pltpu.roll(x, sh, 0), 0.0)
    return x


def _fgate_fwd(pre, b_pad):
    L = pre.shape[0]
    n8 = L // SUBLANES

    def body(p_ref, b_ref, o_ref):
        row = lax.broadcasted_iota(jnp.int32, (SUBLANES, LANES), 0)
        b = b_ref[...]

        def step(i, carry):
            x = p_ref[i] + b
            logf = jnp.minimum(x, 0.0) - jnp.log(1.0 + jnp.exp(-jnp.abs(x)))
            t = _tile_cumsum(logf, row, False) + carry
            o_ref[i] = t
            return t[SUBLANES - 1:SUBLANES, :]

        lax.fori_loop(0, n8, step, jnp.zeros((1, LANES), F32))

    out = pl.pallas_call(
        body, name="fgate_fwd", out_shape=jax.ShapeDtypeStruct((n8, SUBLANES, LANES), F32),
        compiler_params=_params(),
    )(pre.reshape(n8, SUBLANES, LANES), b_pad.reshape(1, LANES))
    return out.reshape(L, LANES)


def _fgate_bwd(dfcum, pre, b_pad):
    L = pre.shape[0]
    n8 = L // SUBLANES

    def body(d_ref, p_ref, b_ref, o_ref, s_ref):
        row = lax.broadcasted_iota(jnp.int32, (SUBLANES, LANES), 0)
        b = b_ref[...]

        def step(k, carry):
            c, acc = carry
            i = n8 - 1 - k
            t = _tile_cumsum(d_ref[i], row, True) + c
            dpre = t * _sigmoid(-(p_ref[i] + b))
            o_ref[i] = dpre
            return t[0:1, :], acc + dpre

        _, acc = lax.fori_loop(0, n8, step, (jnp.zeros((1, LANES), F32), jnp.zeros((SUBLANES, LANES), F32)))
        s_ref[...] = jnp.sum(acc, axis=0, keepdims=True)

    dpre, db = pl.pallas_call(
        body, name="fgate_bwd",
        out_shape=(jax.ShapeDtypeStruct((n8, SUBLANES, LANES), F32), jax.ShapeDtypeStruct((1, LANES), F32)),
        compiler_params=_params(),
    )(dfcum.reshape(n8, SUBLANES, LANES), pre.reshape(n8, SUBLANES, LANES), b_pad.reshape(1, LANES))
    return dpre.reshape(L, LANES), db


FOX_BLOCK = 512


def _fox_scores(qs, k, fk, diagonal):
    s = lax.dot_general(qs, k, _NT, preferred_element_type=F32) - fk
    if diagonal:
        row = lax.broadcasted_iota(jnp.int32, s.shape, 0)
        col = lax.broadcasted_iota(jnp.int32, s.shape, 1)
        s = jnp.where(row >= col, s, NEG_BIG)
    return s


def _fox_specs(tq, L):
    nq = L // tq
    return dict(
        rows=lambda off: pl.BlockSpec((tq, HEAD_DIM), lambda h, i: (i, off + h)),
        seq=lambda off: pl.BlockSpec((L, HEAD_DIM), lambda h, i: (0, off + h)),
        col=pl.BlockSpec((None, None, tq, 1), lambda h, i: (h, i, 0, 0)),
        col_all=pl.BlockSpec((None, nq, tq, 1), lambda h, i: (h, 0, 0, 0)),
        row=pl.BlockSpec((None, None, 1, tq), lambda h, i: (h, i, 0, 0)),
        row_all=pl.BlockSpec((None, nq, 1, tq), lambda h, i: (h, 0, 0, 0)))


FOX_FWD_HEADS = 2


def _fox_fwd(proj, kv, fk):
    L = proj.shape[0]
    tq = min(FOX_BLOCK, L)
    nq = L // tq
    nh = FOX_FWD_HEADS
    W = nh * HEAD_DIM

    def body(q_ref, k_ref, v_ref, fk_ref, o_ref, lse_ref, m_s, l_s, acc_s):
        qi = pl.program_id(1)
        cols = [slice(a * HEAD_DIM, (a + 1) * HEAD_DIM) for a in range(nh)]
        qs = [(q_ref[:, cs] * (HEAD_DIM ** -0.5)).astype(BF16) for cs in cols]
        m_s[...] = jnp.full_like(m_s, NEG_BIG)
        l_s[...] = jnp.zeros_like(l_s)
        acc_s[...] = jnp.zeros_like(acc_s)

        def block(j, diagonal):
            r0 = pl.multiple_of(j * tq, tq)
            for a, cs in enumerate(cols):
                s = _fox_scores(qs[a], k_ref[pl.ds(r0, tq), cs], fk_ref[a, j], diagonal)
                m_new = jnp.maximum(m_s[a], jnp.max(s, axis=-1, keepdims=True))
                alpha = jnp.exp(m_s[a] - m_new)
                p = jnp.exp(s - m_new)
                l_s[a] = alpha * l_s[a] + jnp.sum(p, axis=-1, keepdims=True)
                acc_s[a] = alpha * acc_s[a] + jnp.dot(p.astype(BF16), v_ref[pl.ds(r0, tq), cs],
                                                      preferred_element_type=F32)
                m_s[a] = m_new

        def below(j, carry):
            block(j, False)
            return carry

        lax.fori_loop(0, qi, below, 0)
        block(qi, True)
        for a, cs in enumerate(cols):
            o_ref[:, cs] = acc_s[a] / l_s[a]
            lse_ref[a] = m_s[a] + jnp.log(l_s[a])

    return pl.pallas_call(
        body, name="fox_fwd",
        out_shape=(jax.ShapeDtypeStruct((L, MAIN_WIDTH), F32),
                   jax.ShapeDtypeStruct((FOX_HEADS, nq, tq, 1), F32)),
        grid=(FOX_HEADS // nh, nq),
        in_specs=[pl.BlockSpec((tq, W), lambda h, i: (i, h)),
                  pl.BlockSpec((L, W), lambda h, i: (0, h)),
                  pl.BlockSpec((L, W), lambda h, i: (0, FOX_HEADS // nh + h)),
                  pl.BlockSpec((nh, nq, 1, tq), lambda h, i: (h, 0, 0, 0))],
        out_specs=(pl.BlockSpec((tq, W), lambda h, i: (i, h)),
                   pl.BlockSpec((nh, None, tq, 1), lambda h, i: (h, i, 0, 0))),
        scratch_shapes=[pltpu.VMEM((nh, tq, 1), F32), pltpu.VMEM((nh, tq, 1), F32),
                        pltpu.VMEM((nh, tq, HEAD_DIM), F32)],
        compiler_params=_params("parallel", "parallel"),
    )(proj, kv, kv, fk)


def _fox_bwd_dq(proj, kv, fk, lse, delta, datt, dproj):
    L = proj.shape[0]
    tq = min(FOX_BLOCK, L)
    nq = L // tq
    sp = _fox_specs(tq, L)

    def body(q_ref, k_ref, v_ref, fk_ref, lse_ref, dl_ref, do_ref, dp_hbm, dq_ref, df_ref, acc_s, df_s):
        qi = pl.program_id(1)
        qs = (q_ref[...] * (HEAD_DIM ** -0.5)).astype(BF16)
        dob = do_ref[...].astype(BF16)
        lse, dl = lse_ref[...], dl_ref[...]
        acc_s[...] = jnp.zeros_like(acc_s)
        df_s[...] = jnp.zeros_like(df_s)

        def block(j, diagonal):
            r0 = pl.multiple_of(j * tq, tq)
            k = k_ref[pl.ds(r0, tq), :]
            p = jnp.exp(_fox_scores(qs, k, fk_ref[j], diagonal) - lse)
            dp = lax.dot_general(dob, v_ref[pl.ds(r0, tq), :], _NT, preferred_element_type=F32)
            ds = p * (dp - dl)
            acc_s[...] += jnp.dot(ds.astype(BF16), k, preferred_element_type=F32)
            df_s[...] += jnp.sum(ds, axis=1, keepdims=True)

        def below(j, carry):
            block(j, False)
            return carry

        lax.fori_loop(0, qi, below, 0)
        block(qi, True)
        dq_ref[...] = (acc_s[...] * (HEAD_DIM ** -0.5)).astype(BF16)
        df_ref[...] = df_s[...]

    return pl.pallas_call(
        body, name="fox_bwd_dq",
        out_shape=(jax.ShapeDtypeStruct(dproj.shape, dproj.dtype),
                   jax.ShapeDtypeStruct((FOX_HEADS, nq, tq, 1), F32)),
        grid=(FOX_HEADS, nq),
        in_specs=[sp["rows"](0), sp["seq"](0), sp["seq"](FOX_HEADS), sp["row_all"],
                  sp["col"], sp["col"], sp["rows"](0), _ANY],
        out_specs=(sp["rows"](0), sp["col"]),
        input_output_aliases={7: 0},
        scratch_shapes=[pltpu.VMEM((tq, HEAD_DIM), F32), pltpu.VMEM((tq, 1), F32)],
        compiler_params=_params("parallel", "parallel"),
    )(proj, kv, kv, fk, lse, delta, datt, dproj)


def _fox_bwd_dkv(proj, kv, fk, lse, delta, datt):
    L = proj.shape[0]
    tq = min(FOX_BLOCK, L)
    nq = L // tq
    sp = _fox_specs(tq, L)

    def body(q_ref, k_ref, v_ref, fk_ref, lse_ref, dl_ref, do_ref,
             dk_ref, dv_ref, df_ref, dk_s, dv_s, df_s):
        ki = pl.program_id(1)
        k, v, fk = k_ref[...], v_ref[...], fk_ref[...]
        dk_s[...] = jnp.zeros_like(dk_s)
        dv_s[...] = jnp.zeros_like(dv_s)
        df_s[...] = jnp.zeros_like(df_s)

        def block(i, diagonal):
            r0 = pl.multiple_of(i * tq, tq)
            qs = (q_ref[pl.ds(r0, tq), :] * (HEAD_DIM ** -0.5)).astype(BF16)
            dob = do_ref[pl.ds(r0, tq), :].astype(BF16)
            p = jnp.exp(_fox_scores(qs, k, fk, diagonal) - lse_ref[i])
            dp = lax.dot_general(dob, v, _NT, preferred_element_type=F32)
            ds = p * (dp - dl_ref[i])
            dv_s[...] += lax.dot_general(p.astype(BF16), dob, _TN, preferred_element_type=F32)
            dk_s[...] += lax.dot_general(ds.astype(BF16), qs, _TN, preferred_element_type=F32)
            df_s[...] -= jnp.sum(ds, axis=0, keepdims=True)

        def above(i, carry):
            block(i, False)
            return carry

        block(ki, True)
        lax.fori_loop(ki + 1, nq, above, 0)
        dk_ref[...] = dk_s[...].astype(BF16)
        dv_ref[...] = dv_s[...].astype(BF16)
        df_ref[...] = df_s[...]

    return pl.pallas_call(
        body, name="fox_bwd_dkv",
        out_shape=(jax.ShapeDtypeStruct((L, MAIN_WIDTH), BF16),
                   jax.ShapeDtypeStruct((L, MAIN_WIDTH), BF16),
                   jax.ShapeDtypeStruct((FOX_HEADS, nq, 1, tq), F32)),
        grid=(FOX_HEADS, nq),
        in_specs=[sp["seq"](0), sp["rows"](0), sp["rows"](FOX_HEADS), sp["row"],
                  sp["col_all"], sp["col_all"], sp["seq"](0)],
        out_specs=(sp["rows"](0), sp["rows"](0), sp["row"]),
        scratch_shapes=[pltpu.VMEM((tq, HEAD_DIM), F32), pltpu.VMEM((tq, HEAD_DIM), F32),
                        pltpu.VMEM((1, tq), F32)],
        compiler_params=_params("parallel", "parallel"),
    )(proj, kv, kv, fk, lse, delta, datt)


def _pad_lanes(a):
    return jnp.pad(a, ((0, 0), (0, LANES - a.shape[1])))


def _mem_branch_fwd(memn, w_mk, proj, tag):
    kvm = _mm(memn, w_mk, name="mem_kv_" + tag)
    return kvm, _mem_attn_fwd(proj, kvm)


def _mem_branch_bwd(mem, g, w_mk, proj, memn, kvm, do_mem, dproj, tag):
    dproj, dkvm = _mem_attn_bwd(proj, kvm, do_mem, dproj)
    dkvm = dkvm.astype(BF16)
    dw_mk = _mm(memn, dkvm, ta=True, name="dw_mem_kv_" + tag, out_dtype=BF16)
    dmemn = _mm(dkvm, w_mk, tb=True, name="dmemn_" + tag)
    _, dg = _rmsnorm_bwd(mem, g, dmemn, name="mem_norm_bwd_" + tag, dx_dtype=BF16)
    return dproj, dw_mk, dg


def _local_step(x, mem, target, w, fetch=None, grads_ready=None):
    if grads_ready is None:
        grads_ready = lambda group, grads, token: token
    L = x.shape[0]
    g = {}
    w = dict(w)

    b_re_t = jnp.transpose(w["b_re"], (0, 2, 1))
    b_im_t = jnp.transpose(w["b_im"], (0, 2, 1))
    ar, ai, bbr_t, bbi_t = _s5_prep(w["lam_re"], w["lam_im"], w["log_step"], b_re_t, b_im_t)
    bmat, cmat = _s5_block_mats(bbr_t, bbi_t, w["c_re"], w["c_im"])
    a_rows = _s5_a_rows(ar, ai)

    hn0 = _rmsnorm_fwd(x, w["pre_norm_g"][0], name="pre_norm_0", out_dtype=BF16)
    memn0 = _rmsnorm_fwd(mem, w["mem_norm_g"][0], name="mem_norm_0", out_dtype=BF16)
    memn1 = _rmsnorm_fwd(mem, w["mem_norm_g"][1], name="mem_norm_1", out_dtype=BF16)
    if fetch is not None:
        w.update(fetch("a", [hn0, memn0, memn1, bmat, cmat, a_rows]))
    proj_a = _mm(hn0, w["w_in_a"], name="in_proj_a")
    y, yg, xp = _s5_fwd(proj_a, bmat, cmat, a_rows, w["d_skip"])
    if fetch is not None:
        w.update(fetch("b", yg))
    t = _mm(yg, w["w_glu"], name="glu_proj")
    kvm0, om0 = _mem_branch_fwd(memn0, w["w_mem_kv"][0], proj_a, "0")
    cat0 = _gate_a_fwd(y, t, w["b_glu"], proj_a, om0)
    o0 = _mm(cat0, w["w_out"][0], name="out_proj_0")
    h1 = _rmsnorm_fwd(o0, w["post_norm_g"][0], res=x, name="post_norm_0")

    kv_in = _rmsnorm_fwd(h1, w["kv_norm_g"], name="kv_norm", out_dtype=BF16)
    if fetch is not None:
        w.update(fetch("c", kv_in))
    kv = _mm(kv_in, w["w_kv"], name="kv_proj", out_dtype=BF16)
    pre_f = _mm(kv_in, w["w_fgate"], name="fgate_proj")
    b_f = jnp.pad(w["b_fgate"], (0, LANES - FOX_HEADS))
    fcum = _fgate_fwd(pre_f, b_f)
    fc = jnp.transpose(fcum[:, :FOX_HEADS])
    tq = min(FOX_BLOCK, L)
    fk = fc.reshape(FOX_HEADS, L // tq, 1, tq)

    hn1 = _rmsnorm_fwd(h1, w["pre_norm_g"][1], name="pre_norm_1", out_dtype=BF16)
    proj_b = _mm(hn1, w["w_in_b"], name="in_proj_b")
    att, lse = _fox_fwd(proj_b, kv, fk)
    kvm1, om1 = _mem_branch_fwd(memn1, w["w_mem_kv"][1], proj_b, "1")
    cat1 = _gate_b_fwd(att, proj_b, om1)
    o1 = _mm(cat1, w["w_out"][1], name="out_proj_1")
    dh2, loss_row = _final_norm_loss(o1, w["post_norm_g"][1], h1, target)

    do1, dpost1 = _rmsnorm_bwd(o1, w["post_norm_g"][1], dh2, name="post_norm_bwd_1", dx_dtype=BF16)
    dcat1 = _mm(do1, w["w_out"][1], tb=True, name="dcat_1")
    g["w_out_1"] = _mm(cat1, do1, ta=True, name="dw_out_1", out_dtype=BF16)
    datt, dproj_b, dom1, delta = _gate_b_bwd(dcat1, att, proj_b, om1)
    dproj_b, g["w_mem_kv_1"], dmemg1 = _mem_branch_bwd(mem, w["mem_norm_g"][1], w["w_mem_kv"][1], proj_b,
                                                      memn1, kvm1, dom1, dproj_b, "1")
    delta = delta.reshape(lse.shape)
    dproj_b, dfq = _fox_bwd_dq(proj_b, kv, fk, lse, delta, datt, dproj_b)
    dk, dv, dfk = _fox_bwd_dkv(proj_b, kv, fk, lse, delta, datt)
    g["w_in_b"] = _mm(hn1, dproj_b, ta=True, name="dw_in_b", out_dtype=BF16, shards=N_CHIPS)
    dhn1 = _mm(dproj_b, w["w_in_b"], tb=True, name="dhn_1")

    dkv = jnp.concatenate([dk, dv], axis=1)
    g["w_kv"] = _mm(kv_in, dkv, ta=True, name="dw_kv", out_dtype=BF16, shards=N_CHIPS)
    dkv_in_a = _mm(dkv, w["w_kv"], tb=True, name="dkv_in_kv")
    dfcum = _pad_lanes(jnp.transpose(dfq.reshape(FOX_HEADS, L) + dfk.reshape(FOX_HEADS, L)))
    dpre_f, db_f = _fgate_bwd(dfcum, pre_f, b_f)
    g["b_fgate"] = db_f[0, :FOX_HEADS]
    g["w_fgate"] = _mm(kv_in, dpre_f, ta=True, name="dw_fgate")[:, :FOX_HEADS]
    dkv_in_b = _mm(dpre_f, w["w_fgate"], tb=True, name="dkv_in_fgate")
    dh1_kv, g["kv_norm_g"] = _rmsnorm_bwd(h1, w["kv_norm_g"], (dkv_in_a, dkv_in_b), name="kv_norm_bwd")
    dh1, dpre1 = _rmsnorm_bwd(h1, w["pre_norm_g"][1], dhn1, adds=(dh2, dh1_kv), name="pre_norm_bwd_1")
    dh1 = grads_ready("b", g, dh1)

    do0, dpost0 = _rmsnorm_bwd(o0, w["post_norm_g"][0], dh1, name="post_norm_bwd_0", dx_dtype=BF16)
    dcat0 = _mm(do0, w["w_out"][0], tb=True, name="dcat_0")
    g["w_out_0"] = _mm(cat0, do0, ta=True, name="dw_out_0", out_dtype=BF16)
    dcat0 = grads_ready("b_send", g, dcat0)
    dproj_a, dt, dyg_a, dom0, db_glu = _gate_a_bwd(dcat0, y, t, w["b_glu"], proj_a, om0)
    g["b_glu"] = db_glu[0]
    g["w_glu"] = _mm(yg, dt, ta=True, name="dw_glu", out_dtype=BF16)
    dyg_b = _mm(dt, w["w_glu"], tb=True, name="dyg")
    dproj_a, g["w_mem_kv_0"], dmemg0 = _mem_branch_bwd(mem, w["mem_norm_g"][0], w["w_mem_kv"][0], proj_a,
                                                      memn0, kvm0, dom0, dproj_a, "0")
    dyg_b = grads_ready("a1", g, dyg_b)
    dproj_a, db_blk, dc_blk, da_rows, dd_skip = _s5_bwd(proj_a, dyg_a, dyg_b, y, xp, bmat, cmat, a_rows,
                                                        w["d_skip"], dproj_a)
    dproj_a = grads_ready("a1_send", g, dproj_a)
    g["d_skip"] = dd_skip[0]
    g["w_in_a"] = _mm(hn0, dproj_a, ta=True, name="dw_in_a", out_dtype=BF16, shards=N_CHIPS)
    dproj_a = grads_ready("a2", g, dproj_a)
    dhn0 = _mm(dproj_a, w["w_in_a"], tb=True, name="dhn_0")
    grad_x, dpre0 = _rmsnorm_bwd(x, w["pre_norm_g"][0], dhn0, adds=(dh1,), name="pre_norm_bwd_0")

    dbb = _s5_block_diag(db_blk)
    dcc = _s5_block_diag(dc_blk)
    g["c_re"], g["c_im"] = dcc[0], -dcc[1]
    d_ar = da_rows[:, 0, :STATE_COLS].reshape(SSM_GROUPS, SSM_STATE)
    d_ai = da_rows[:, 0, STATE_COLS:].reshape(SSM_GROUPS, SSM_STATE)
    dlr, dli, dls, dbr_t, dbi_t = _s5_prep_bwd(w["lam_re"], w["lam_im"], w["log_step"], b_re_t, b_im_t,
                                               d_ar, d_ai, dbb[0], dbb[1])
    g["lam_re"], g["lam_im"], g["log_step"] = dlr, dli, dls[:, 0]
    g["b_re"] = jnp.transpose(dbr_t, (0, 2, 1))
    g["b_im"] = jnp.transpose(dbi_t, (0, 2, 1))
    g["pre_norm_g"] = jnp.stack([dpre0, dpre1])
    g["post_norm_g"] = jnp.stack([dpost0, dpost1])
    g["mem_norm_g"] = jnp.stack([dmemg0, dmemg1])
    return loss_row, grad_x, g


_MESH = pl.DeviceIdType.MESH
_ANY = pl.BlockSpec(memory_space=pl.ANY)


def _place():
    x, y, c = lax.axis_index("x"), lax.axis_index("y"), lax.axis_index("c")
    chips = [(1 - x, y), (x, 1 - y), (1 - x, 1 - y)]
    return x, y, c, chips


_HBM = pl.BlockSpec(memory_space=pltpu.HBM)
_SEM = pl.BlockSpec(memory_space=pltpu.SEMAPHORE)
_SIDE = pltpu.SideEffectType.DATAFLOW_SIDE_EFFECTING


def _in_hbm(a):
    return pltpu.with_memory_space_constraint(a, pltpu.HBM)


def _hbm_like(a):
    return pltpu.HBM(a.shape, a.dtype)


def _ici_copies(srcs, lands, send_sem, recv_sem, src_at, dst_at, wait_at, to_sibling=False):
    x, y, c, chips = _place()
    peers = [(x, y, 1 - c)] if to_sibling else [(cx, cy, c) for cx, cy in chips]
    m = len(peers)
    start, wait = [], []
    for i in range(len(srcs)):
        for k, (px, py, pc) in enumerate(peers):
            sem = dict(send_sem=send_sem.at[m * i + k], recv_sem=recv_sem.at[m * i + k],
                       device_id=(px, py, pc), device_id_type=_MESH)
            src = src_at(srcs[i], 2 * px + py, c)
            start.append(pltpu.make_async_remote_copy(src_ref=src, dst_ref=dst_at(lands[i], 2 * x + y, k, c), **sem))
            wait.append(pltpu.make_async_remote_copy(src_ref=src, dst_ref=wait_at(lands[i], 2 * px + py, k, c), **sem))
    return start, wait


def _route_peers(route):
    return 1 if len(route) == 4 else 3


_BLOCK_ROUTE = (lambda s, j, c: s, lambda l, me, k, c: l.at[me, c], lambda l, j, k, c: l.at[j, c])


def _ici_start(srcs, lands, token, route, *, name):
    n = len(srcs)

    def body(*refs):
        start, _ = _ici_copies(refs[:n], refs[n:2 * n], refs[2 * n + 1], refs[2 * n + 2], *route)
        for cp in start:
            cp.start()

    sems = pltpu.SemaphoreType.DMA((_route_peers(route) * n,))
    outs = pl.pallas_call(
        body, name=name,
        out_shape=(sems, sems, *[_hbm_like(a) for a in srcs], *[_hbm_like(a) for a in lands], _hbm_like(token)),
        in_specs=[_HBM] * (2 * n + 1), out_specs=(_SEM, _SEM, *[_HBM] * (2 * n + 1)),
        input_output_aliases={i: 2 + i for i in range(2 * n + 1)},
        compiler_params=pltpu.CompilerParams(has_side_effects=_SIDE),
    )(*[_in_hbm(a) for a in srcs], *[_in_hbm(a) for a in lands], _in_hbm(token))
    return (outs[0], outs[1], list(outs[2:2 + n]), list(outs[2 + n:2 + 2 * n])), outs[2 + 2 * n]


def _ici_wait(handle, after, route, *, name):
    send_sem, recv_sem, srcs, lands = handle
    n = len(srcs)
    after = list(after) if isinstance(after, (list, tuple)) else [after]

    def body(*refs):
        _, wait = _ici_copies(refs[:n], refs[n:2 * n], refs[2 * n], refs[2 * n + 1], *route)
        for cp in wait:
            cp.wait_send()
            cp.wait_recv()

    outs = pl.pallas_call(
        body, name=name,
        out_shape=(*[_hbm_like(a) for a in srcs], *[_hbm_like(a) for a in lands]),
        in_specs=[_HBM] * (2 * n) + [_SEM, _SEM] + [_ANY] * len(after), out_specs=tuple([_HBM] * (2 * n)),
        input_output_aliases={i: i for i in range(2 * n)},
        compiler_params=pltpu.CompilerParams(has_side_effects=_SIDE),
    )(*srcs, *lands, send_sem, recv_sem, *after)
    return list(outs[:n]), list(outs[n:])


_GATHER_ROUTE = (lambda s, j, c: s.at[c], lambda l, me, k, c: l.at[me, c], lambda l, j, k, c: l.at[j, c])
_SCATTER_ROUTE = (lambda s, j, c: s.at[j], lambda l, me, k, c: l.at[k], lambda l, j, k, c: l.at[k])
_SWAP_ROUTE = (lambda s, j, c: s.at[:, 1 - c], lambda l, me, k, c: l, lambda l, j, k, c: l, True)


def _gather_forward(lands, tag, own=False):
    n = len(lands)
    m = 4 if own else 3

    def body(*refs):
        ins, outs = refs[:n], refs[n:2 * n]
        send_sem, recv_sem = refs[2 * n:]
        x, y, c, chips = _place()
        slots = [2 * cx + cy for cx, cy in chips] + [2 * x + y]

        def copy(i, k, half):
            return pltpu.make_async_remote_copy(
                src_ref=ins[i].at[slots[k], half], dst_ref=outs[i].at[slots[k], half],
                send_sem=send_sem.at[m * i + k], recv_sem=recv_sem.at[m * i + k],
                device_id=(x, y, 1 - c), device_id_type=_MESH)

        copies = [copy(i, k, c) for i in range(n) for k in range(m)]
        for cp in copies:
            cp.start()
        for i in range(n):
            for k in range(m):
                copy(i, k, 1 - c).wait_recv()
        for cp in copies:
            cp.wait_send()

    return pl.pallas_call(
        body, name="gather_forward_to_sibling_" + tag,
        out_shape=[jax.ShapeDtypeStruct(a.shape, a.dtype) for a in lands],
        in_specs=[_ANY] * n, out_specs=[_ANY] * n,
        input_output_aliases={i: i for i in range(n)},
        scratch_shapes=[pltpu.SemaphoreType.DMA((m * n,)), pltpu.SemaphoreType.DMA((m * n,))],
    )(*lands)


def _swap_halves(grads, tag):
    n = len(grads)

    def body(*refs):
        ins, outs = refs[:n], refs[n:2 * n]
        send_sem, recv_sem = refs[2 * n:]
        x, y, c, _ = _place()
        copies = [pltpu.make_async_remote_copy(
            src_ref=ins[i].at[:, 1 - c], dst_ref=outs[i],
            send_sem=send_sem.at[i], recv_sem=recv_sem.at[i],
            device_id=(x, y, 1 - c), device_id_type=_MESH) for i in range(n)]
        for cp in copies:
            cp.start()
        for cp in copies:
            cp.wait()

    return pl.pallas_call(
        body, name="grad_swap_halves_" + tag,
        out_shape=[jax.ShapeDtypeStruct((N_CHIPS,) + g.shape[2:], g.dtype) for g in grads],
        in_specs=[_ANY] * n, out_specs=[_ANY] * n,
        scratch_shapes=[pltpu.SemaphoreType.DMA((n,)), pltpu.SemaphoreType.DMA((n,))],
    )(*grads)


def _sum_rows(h, C):
    return max(d for d in range(SUBLANES, h + 1, SUBLANES) if h % d == 0 and d * C <= 1 << 20)


def _pair_sum(g, r, c_idx, *, name):
    _, _, h, C = g.shape
    tr = _sum_rows(h, C)

    def body(c_ref, g_ref, r_ref, o_ref):
        o_ref[...] = (g_ref[...].astype(F32) + r_ref[...].astype(F32)).astype(o_ref.dtype)

    return pl.pallas_call(
        body, name=name, out_shape=jax.ShapeDtypeStruct((N_CHIPS, h, C), g.dtype),
        grid_spec=pltpu.PrefetchScalarGridSpec(
            num_scalar_prefetch=1, grid=(N_CHIPS, h // tr),
            in_specs=[pl.BlockSpec((None, None, tr, C), lambda j, i, s: (j, s[0], i, 0)),
                      pl.BlockSpec((None, tr, C), lambda j, i, s: (j, i, 0))],
            out_specs=pl.BlockSpec((None, tr, C), lambda j, i, s: (j, i, 0))),
        compiler_params=_params("parallel", "parallel"),
    )(c_idx, g, r)


def _owner_sum(s, r, jc_idx, *, name):
    _, h, C = s.shape
    tr = _sum_rows(h, C)

    def body(jc_ref, s_ref, r_ref, o_ref):
        acc = s_ref[...].astype(F32)
        for k in range(3):
            acc = acc + r_ref[k].astype(F32)
        o_ref[...] = acc

    return pl.pallas_call(
        body, name=name, out_shape=jax.ShapeDtypeStruct((2, h, C), F32),
        grid_spec=pltpu.PrefetchScalarGridSpec(
            num_scalar_prefetch=1, grid=(h // tr,),
            in_specs=[pl.BlockSpec((None, tr, C), lambda i, s: (s[0], i, 0)),
                      pl.BlockSpec((3, tr, C), lambda i, s: (0, i, 0))],
            out_specs=pl.BlockSpec((None, tr, C), lambda i, s: (s[1], i, 0))),
        compiler_params=_params("parallel"),
    )(jc_idx, s, r)


def _share_with_sibling(bufs, tag):
    n = len(bufs)

    def body(*refs):
        ins, outs = refs[:n], refs[n:2 * n]
        send_sem, recv_sem = refs[2 * n:]
        x, y, c, _ = _place()

        def copy(i, half):
            return pltpu.make_async_remote_copy(
                src_ref=ins[i].at[half], dst_ref=outs[i].at[half],
                send_sem=send_sem.at[i], recv_sem=recv_sem.at[i],
                device_id=(x, y, 1 - c), device_id_type=_MESH)

        copies = [copy(i, c) for i in range(n)]
        for cp in copies:
            cp.start()
        for i in range(n):
            copy(i, 1 - c).wait_recv()
        for cp in copies:
            cp.wait_send()

    return pl.pallas_call(
        body, name="grad_share_with_sibling_" + tag,
        out_shape=[jax.ShapeDtypeStruct(b.shape, b.dtype) for b in bufs],
        in_specs=[_ANY] * n, out_specs=[_ANY] * n,
        input_output_aliases={i: i for i in range(n)},
        scratch_shapes=[pltpu.SemaphoreType.DMA((n,)), pltpu.SemaphoreType.DMA((n,))],
    )(*bufs)


def _chip_sums(grads, c_idx, tag):
    views = [g.reshape(N_CHIPS, 2, g.shape[1] // 2, g.shape[2]) for g in grads]
    arrived = _swap_halves(views, tag)
    return [_pair_sum(v, r, c_idx, name=f"grad_pair_sum_{tag}_{i}") for i, (v, r) in enumerate(zip(views, arrived))]


def _owner_totals(sums, arrived, jc_idx, tag):
    halves = [_owner_sum(s, r, jc_idx, name=f"grad_owner_sum_{tag}_{i}") for i, (s, r) in enumerate(zip(sums, arrived))]
    return [f.reshape(-1, f.shape[2]) for f in _share_with_sibling(halves, tag)]


def _sum_devices(blocks):
    R = blocks.shape[2]
    tr = _sum_rows(R, 2 * N_CHIPS * LANES)

    def body(b_ref, o_ref):
        acc = b_ref[0, 0]
        for d in range(1, 2 * N_CHIPS):
            acc = acc + b_ref[d // 2, d % 2]
        o_ref[...] = acc

    return pl.pallas_call(
        body, name="sum_small_over_devices", out_shape=jax.ShapeDtypeStruct((R, LANES), F32),
        grid=(R // tr,),
        in_specs=[pl.BlockSpec((N_CHIPS, 2, tr, LANES), lambda i: (0, 0, i, 0))],
        out_specs=pl.BlockSpec((tr, LANES), lambda i: (i, 0)),
        compiler_params=_params("parallel"),
    )(blocks)


def _adamw(w, g, m, v, *, name):
    R, C = w.shape
    whole_fits = 7 * 2 * R * C * 4 <= VMEM_LIMIT_BYTES // 2
    tr = R if whole_fits else next(c for c in (256, 192, 128, 64, 32, 16, 8) if R % c == 0)

    def body(w_ref, g_ref, m_ref, v_ref, d_ref, nm_ref, nv_ref):
        g = g_ref[...]
        m = ADAM_B1 * m_ref[...] + (1.0 - ADAM_B1) * g
        v = ADAM_B2 * v_ref[...] + (1.0 - ADAM_B2) * (g * g)
        nm_ref[...] = m
        nv_ref[...] = v
        m_hat = m / (1.0 - ADAM_B1 ** ADAM_STEP)
        v_hat = v / (1.0 - ADAM_B2 ** ADAM_STEP)
        d_ref[...] = -ADAM_LR * (m_hat / (jnp.sqrt(v_hat) + ADAM_EPS) + ADAM_WD * w_ref[...])

    blk = pl.BlockSpec((tr, C), lambda i: (i, 0))
    sds = jax.ShapeDtypeStruct((R, C), F32)
    return pl.pallas_call(
        body, name=name, out_shape=(sds, sds, sds), grid=(R // tr,),
        in_specs=[blk] * 4, out_specs=(blk, blk, blk),
        compiler_params=_params("parallel"),
    )(w, g, m, v)


_TILE = SUBLANES * LANES


def _pack(arrays):
    rows = []
    for a in arrays:
        flat = a.reshape(-1)
        flat = jnp.pad(flat, (0, (-flat.shape[0]) % _TILE))
        rows.append(flat.reshape(-1, LANES))
    return jnp.concatenate(rows, axis=0)


def _unpack(buf, shapes):
    out, r = [], 0
    for s in shapes:
        size = math.prod(s)
        nr = -(-size // _TILE) * SUBLANES
        out.append(buf[r:r + nr].reshape(-1)[:size].reshape(s))
        r += nr
    return out


_BIG = ("w_in_a", "w_glu", "w_kv", "w_in_b", "w_mem_kv", "w_out")
_REPLICATED = ("pre_norm_g", "post_norm_g", "lam_re", "lam_im", "log_step", "b_re", "b_im", "c_re", "c_im",
               "kv_norm_g", "b_fgate", "mem_norm_g")
_SHARDED_SMALL = ("d_skip", "b_glu", "w_fgate")
_WEIGHTS = ("pre_norm_g", "post_norm_g", "w_in_a", "lam_re", "lam_im", "log_step", "b_re", "b_im", "c_re",
            "c_im", "d_skip", "w_glu", "b_glu", "kv_norm_g", "w_kv", "w_fgate", "b_fgate", "w_in_b",
            "mem_norm_g", "w_mem_kv", "w_out")


def _halves(a):
    return a.reshape(2, a.shape[0] // 2, a.shape[1])


def _unhalve(a):
    return a.reshape(N_CHIPS, 2 * a.shape[2], a.shape[3])


def _columns(a):
    return jnp.transpose(a, (1, 0, 2)).reshape(a.shape[1], N_CHIPS * a.shape[2])


def kernel(x, mem, pre_norm_g, post_norm_g, w_in_a, lam_re, lam_im, log_step, b_re, b_im, c_re, c_im, d_skip, w_glu, b_glu, kv_norm_g, w_kv, w_fgate, b_fgate, w_in_b, mem_norm_g, w_mem_kv, w_out, loss_target, m_pre_norm_g, m_post_norm_g, m_w_in_a, m_lam_re, m_lam_im, m_log_step, m_b_re, m_b_im, m_c_re, m_c_im, m_d_skip, m_w_glu, m_b_glu, m_kv_norm_g, m_w_kv, m_w_fgate, m_b_fgate, m_w_in_b, m_mem_norm_g, m_w_mem_kv, m_w_out, v_pre_norm_g, v_post_norm_g, v_w_in_a, v_lam_re, v_lam_im, v_log_step, v_b_re, v_b_im, v_c_re, v_c_im, v_d_skip, v_w_glu, v_b_glu, v_kv_norm_g, v_w_kv, v_w_fgate, v_b_fgate, v_w_in_b, v_mem_norm_g, v_w_mem_kv, v_w_out):
    a = dict(locals())
    xi, yi, ci = lax.axis_index("x"), lax.axis_index("y"), lax.axis_index("c")
    chip = 2 * xi + yi
    c_idx = jnp.reshape(ci, (1,)).astype(jnp.int32)
    jc_idx = jnp.stack([chip, ci]).astype(jnp.int32)

    vec = jnp.zeros((2 * SUBLANES, MAIN_WIDTH // N_CHIPS), F32)
    vec = vec.at[0].set(a["d_skip"][0]).at[1].set(a["b_glu"][0])
    def own_slot(gathered, parts):
        return [lax.dynamic_update_index_in_dim(g, p, chip, 0) for g, p in zip(gathered, parts)]

    parts_a = [_halves(a["w_in_a"][0].astype(BF16)), _halves(vec)]
    parts_b = [_halves(a["w_glu"][0].astype(BF16)), _halves(a["w_mem_kv"].reshape(-1, 2 * MEM_WIDTH).astype(BF16)),
               _halves(a["w_out"].reshape(-1, D_MODEL).astype(BF16))]
    parts_c = [_halves(a["w_kv"].astype(BF16)), _halves(_pad_lanes(a["w_fgate"]).astype(BF16)),
               _halves(a["w_in_b"][0].astype(BF16))]
    travelling, token = {}, a["pre_norm_g"]
    for tag, parts in (("a", parts_a), ("b", parts_b), ("c", parts_c)):
        lands = [lax.empty((N_CHIPS,) + p.shape, p.dtype) for p in parts]
        travelling[tag], token = _ici_start(parts, lands, token, _GATHER_ROUTE, name=f"gather_{tag}_start")

    def fetch(tag, after):
        parts, lands = _ici_wait(travelling[tag], after, _GATHER_ROUTE, name=f"gather_{tag}_wait")
        full = own_slot(_gather_forward(lands, tag), parts)
        if tag == "a":
            w_in_a, vecs = full
            return dict(w_in_a=_columns(_unhalve(w_in_a)), d_skip=vecs[:, 0, 0, :].reshape(MAIN_WIDTH),
                        b_glu=vecs[:, 0, 1, :].reshape(MAIN_WIDTH))
        if tag == "b":
            w_glu, w_mk, w_out = full
            return dict(w_glu=w_glu.reshape(MAIN_WIDTH, MAIN_WIDTH),
                        w_mem_kv=[w_mk[:, i].reshape(D_MODEL, 2 * MEM_WIDTH) for i in range(2)],
                        w_out=[w_out[:, i].reshape(D_MODEL, D_MODEL) for i in range(2)])
        w_kv, w_fg, w_in_b = full
        return dict(w_kv=_columns(_unhalve(w_kv)), w_fgate=w_fg.reshape(D_MODEL, LANES),
                    w_in_b=_columns(_unhalve(w_in_b)))

    w = dict(
        pre_norm_g=token, post_norm_g=a["post_norm_g"], mem_norm_g=a["mem_norm_g"],
        kv_norm_g=a["kv_norm_g"], b_fgate=a["b_fgate"],
        lam_re=a["lam_re"][0], lam_im=a["lam_im"][0], log_step=a["log_step"][0],
        b_re=a["b_re"][0], b_im=a["b_im"][0], c_re=a["c_re"][0], c_im=a["c_im"][0])

    sent = {}

    swapping = {}

    def grads_ready(event, g, token):
        tag = event.split("_")[0]
        if event in ("b", "a1"):
            big = {"b": lambda: [g["w_kv"], g["w_in_b"], g["w_mem_kv_1"].reshape(N_CHIPS, -1, 2 * MEM_WIDTH),
                                 g["w_out_1"].reshape(N_CHIPS, -1, D_MODEL)],
                   "a1": lambda: [g["w_glu"].reshape(N_CHIPS, -1, MAIN_WIDTH),
                                  g["w_mem_kv_0"].reshape(N_CHIPS, -1, 2 * MEM_WIDTH),
                                  g["w_out_0"].reshape(N_CHIPS, -1, D_MODEL)]}[tag]()
            views = [b.reshape(N_CHIPS, 2, b.shape[1] // 2, b.shape[2]) for b in big]
            lands = [lax.empty((N_CHIPS,) + v.shape[2:], v.dtype) for v in views]
            swapping[tag], token = _ici_start(views, lands, token, _SWAP_ROUTE, name=f"grad_swap_{tag}_start")
            return token
        if event == "a2":
            sums = _chip_sums([g["w_in_a"]], c_idx, tag)
        else:
            views, arrived = _ici_wait(swapping[tag], token, _SWAP_ROUTE, name=f"grad_swap_{tag}_wait")
            sums = [_pair_sum(v, r, c_idx, name=f"grad_pair_sum_{tag}_{i}")
                    for i, (v, r) in enumerate(zip(views, arrived))]
        lands = [lax.empty((3,) + s.shape[1:], s.dtype) for s in sums]
        sent[tag], token = _ici_start(sums, lands, token, _SCATTER_ROUTE, name=f"grad_send_{tag}_start")
        return token

    loss_row, grad_x, g = _local_step(a["x"][0], a["mem"][0], a["loss_target"][0], w, fetch, grads_ready)

    small_names = _REPLICATED + _SHARDED_SMALL
    pack = _pack([g[n] for n in small_names])
    blocks = lax.empty((N_CHIPS, 2) + pack.shape, F32)
    small_sent, loss_row = _ici_start([pack], [blocks], loss_row, _BLOCK_ROUTE, name="small_sums_start")
    loss = lax.psum(jnp.sum(loss_row), MESH_AXES)

    def totals(tag, after):
        sums, arrived = _ici_wait(sent[tag], after, _SCATTER_ROUTE, name=f"grad_send_{tag}_wait")
        return _owner_totals(sums, arrived, jc_idx, tag)

    r_kv, r_in_b, r_mk1, r_out1 = totals("b", grad_x)
    r_glu, r_mk0, r_out0 = totals("a1", r_out1)
    (r_in_a,) = totals("a2", r_out0)
    grads = {"w_in_a": r_in_a[None], "w_glu": r_glu[None], "w_kv": r_kv, "w_in_b": r_in_b[None],
             "w_mem_kv": jnp.stack([r_mk0, r_mk1]), "w_out": jnp.stack([r_out0, r_out1])}

    delta, new_m, new_v = {}, {}, {}
    for n in _BIG:
        shape = a[n].shape
        d2 = (-1, shape[-1])
        d, m, v = _adamw(a[n].reshape(d2), grads[n].reshape(d2), a["m_" + n].reshape(d2),
                         a["v_" + n].reshape(d2), name="adamw_" + n)
        delta[n], new_m[n], new_v[n] = d.reshape(shape), m.reshape(shape), v.reshape(shape)

    (pack,), (blocks,) = _ici_wait(small_sent, [delta[n] for n in _BIG], _BLOCK_ROUTE, name="small_sums_wait")
    blocks = lax.dynamic_update_slice(blocks, pack[None, None], (chip, ci, 0, 0))
    (blocks,) = _gather_forward([blocks], "small", own=True)
    small = dict(zip(small_names, _unpack(_sum_devices(blocks), [g[n].shape for n in small_names])))
    for n in _REPLICATED:
        grads[n] = small[n].reshape(a[n].shape)
    nd = MAIN_WIDTH // N_CHIPS
    grads["d_skip"] = lax.dynamic_slice(small["d_skip"], (chip * nd,), (nd,))[None]
    grads["b_glu"] = lax.dynamic_slice(small["b_glu"], (chip * nd,), (nd,))[None]
    nf = D_MODEL // N_CHIPS
    grads["w_fgate"] = lax.dynamic_slice(small["w_fgate"], (chip * nf, 0), (nf, FOX_HEADS))

    shapes = [a[n].shape for n in small_names]
    d, m, v = _adamw(_pack([a[n] for n in small_names]), _pack([grads[n] for n in small_names]),
                     _pack([a["m_" + n] for n in small_names]), _pack([a["v_" + n] for n in small_names]),
                     name="adamw_small")
    for n, dd, mm, vv in zip(small_names, _unpack(d, shapes), _unpack(m, shapes), _unpack(v, shapes)):
        delta[n], new_m[n], new_v[n] = dd, mm, vv

    return (loss, grad_x[None], *[grads[n] for n in _WEIGHTS], *[delta[n] for n in _WEIGHTS],
            *[new_m[n] for n in _WEIGHTS], *[new_v[n] for n in _WEIGHTS])
```

```python
import functools
import math

import jax
import jax.numpy as jnp
from jax import lax
from jax.experimental import pallas as pl
from jax.experimental.pallas import tpu as pltpu

F32 = jnp.float32
BF16 = jnp.bfloat16

D_MODEL = 2048
N_MEM = 256
MAIN_WIDTH = 1536
MEM_WIDTH = 512
IN_WIDTH = 2 * MAIN_WIDTH + 2 * MEM_WIDTH
HEAD_DIM = 128
FOX_HEADS = MAIN_WIDTH // HEAD_DIM
MEM_HEADS = MEM_WIDTH // HEAD_DIM
SSM_GROUP = 16
SSM_GROUPS = MAIN_WIDTH // SSM_GROUP
SSM_STATE = 64
GROUPS_PER_BLOCK = 8
SSM_BLOCKS = SSM_GROUPS // GROUPS_PER_BLOCK
STATE_COLS = GROUPS_PER_BLOCK * SSM_STATE
EPS = 1e-6
ADAM_LR = 0.001
ADAM_B1 = 0.9
ADAM_B2 = 0.999
ADAM_EPS = 1e-08
ADAM_WD = 0.01
ADAM_STEP = 10
N_CHIPS = 4
LANES = 128
SUBLANES = 8
VMEM_LIMIT_BYTES = 56 * 1024 * 1024
NEG_BIG = -1e30
MESH_AXES = ("x", "y", "c")


def _params(*sem):
    return pltpu.CompilerParams(dimension_semantics=sem if sem else None,
                                vmem_limit_bytes=VMEM_LIMIT_BYTES)


def _sigmoid(x):
    return 1.0 / (1.0 + jnp.exp(-x))


def _gelu(x):
    c = math.sqrt(2.0 / math.pi)
    return 0.5 * x * (1.0 + jnp.tanh(c * (x + 0.044715 * (x * x * x))))


def _gelu_grad(x):
    c = math.sqrt(2.0 / math.pi)
    t = jnp.tanh(c * (x + 0.044715 * (x * x * x)))
    return 0.5 * (1.0 + t) + 0.5 * x * (1.0 - t * t) * (c * (1.0 + 3.0 * 0.044715 * (x * x)))


def _silu_and_grad(z):
    s = _sigmoid(z)
    return z * s, s * (1.0 + z * (1.0 - s))


_TILE_CHOICES = (2048, 1024, 768, 512, 384, 256, LANES)


def _tile(n, cap):
    return next(c for c in _TILE_CHOICES if c <= cap and n % c == 0)


def _mm(a, b, *, name, ta=False, tb=False, out_dtype=F32, shards=1, tm=1024, tn=1024, tk=2048):
    if ta:
        K, M = a.shape
    else:
        M, K = a.shape
    if tb:
        N, kb = b.shape
    else:
        kb, N = b.shape
    assert K == kb, (a.shape, b.shape)
    ns = N // shards
    tm, tn, tk = _tile(M, tm), _tile(ns, tn), _tile(K, tk)
    assert M % tm == 0 and ns % tn == 0 and K % tk == 0 and N % shards == 0
    nk = K // tk
    dn = (((0 if ta else 1,), (1 if tb else 0,)), ((), ()))

    def body(a_ref, b_ref, o_ref, acc_ref):
        k = pl.program_id(2)

        @pl.when(k == 0)
        def _():
            acc_ref[...] = jnp.zeros_like(acc_ref)

        acc_ref[...] += lax.dot_general(a_ref[...].astype(BF16), b_ref[...].astype(BF16), dn,
                                        preferred_element_type=F32)

        @pl.when(k == nk - 1)
        def _():
            o_ref[...] = acc_ref[...].astype(o_ref.dtype)

    a_spec = (pl.BlockSpec((tk, tm), lambda i, j, k: (k, i)) if ta
              else pl.BlockSpec((tm, tk), lambda i, j, k: (i, k)))
    b_spec = (pl.BlockSpec((tn, tk), lambda i, j, k: (j, k)) if tb
              else pl.BlockSpec((tk, tn), lambda i, j, k: (k, j)))
    if shards == 1:
        out_shape = jax.ShapeDtypeStruct((M, N), out_dtype)
        o_spec = pl.BlockSpec((tm, tn), lambda i, j, k: (i, j))
    else:
        nb = ns // tn
        out_shape = jax.ShapeDtypeStruct((shards, M, ns), out_dtype)
        o_spec = pl.BlockSpec((None, tm, tn), lambda i, j, k: (j // nb, i, j % nb))
    return pl.pallas_call(
        body, name=name, out_shape=out_shape,
        grid=(M // tm, N // tn, nk),
        in_specs=[a_spec, b_spec], out_specs=o_spec,
        scratch_shapes=[pltpu.VMEM((tm, tn), F32)],
        compiler_params=_params("parallel", "parallel", "arbitrary"),
    )(a, b)


def _rmsnorm_fwd(x, g, *, name, res=None, out_dtype=F32, tr=256):
    L, D = x.shape
    tr = min(tr, L)
    has_res = res is not None

    def body(*refs):
        if has_res:
            x_ref, g_ref, r_ref, o_ref = refs
        else:
            x_ref, g_ref, o_ref = refs
        xf = x_ref[...]
        r = lax.rsqrt(jnp.mean(xf * xf, axis=-1, keepdims=True) + EPS)
        y = xf * r * g_ref[...]
        if has_res:
            y = r_ref[...] + y
        o_ref[...] = y.astype(o_ref.dtype)

    row = pl.BlockSpec((tr, D), lambda i: (i, 0))
    vec = pl.BlockSpec((1, D), lambda i: (0, 0))
    ins = [x, g.reshape(1, D)] + ([res] if has_res else [])
    return pl.pallas_call(
        body, name=name, out_shape=jax.ShapeDtypeStruct((L, D), out_dtype),
        grid=(L // tr,), in_specs=[row, vec] + ([row] if has_res else []), out_specs=row,
        compiler_params=_params("parallel"),
    )(*ins)


def _rmsnorm_bwd(x, g, dy, *, name, adds=(), dx_dtype=F32, tr=256):
    L, D = x.shape
    tr = min(tr, L)
    dys = dy if isinstance(dy, tuple) else (dy,)
    n_dy, n_add = len(dys), len(adds)

    def body(*refs):
        x_ref, g_ref = refs[:2]
        dy_refs = refs[2:2 + n_dy]
        add_refs = refs[2 + n_dy:2 + n_dy + n_add]
        dx_ref, dg_ref = refs[2 + n_dy + n_add:]
        xf = x_ref[...]
        dyf = dy_refs[0][...].astype(F32)
        for d_ref in dy_refs[1:]:
            dyf = dyf + d_ref[...].astype(F32)
        r = lax.rsqrt(jnp.mean(xf * xf, axis=-1, keepdims=True) + EPS)
        gy = dyf * g_ref[...]
        c = jnp.mean(xf * gy, axis=-1, keepdims=True) * (r * r * r)
        dx = gy * r - xf * c
        for a_ref in add_refs:
            dx = dx + a_ref[...].astype(F32)
        dx_ref[...] = dx.astype(dx_ref.dtype)

        @pl.when(pl.program_id(0) == 0)
        def _():
            dg_ref[...] = jnp.zeros_like(dg_ref)

        dg_ref[...] += jnp.sum(dyf * xf * r, axis=0, keepdims=True)

    row = pl.BlockSpec((tr, D), lambda i: (i, 0))
    vec = pl.BlockSpec((1, D), lambda i: (0, 0))
    dx, dg = pl.pallas_call(
        body, name=name,
        out_shape=(jax.ShapeDtypeStruct((L, D), dx_dtype), jax.ShapeDtypeStruct((1, D), F32)),
        grid=(L // tr,), in_specs=[row, vec] + [row] * (n_dy + n_add), out_specs=(row, vec),
        compiler_params=_params("arbitrary"),
    )(x, g.reshape(1, D), *dys, *adds)
    return dx, dg.reshape(D)


def _rmsnorm_bwd_pair(x, g1, dy1, g2, dy2, *, name, adds=(), tr=256):
    L, D = x.shape
    tr = min(tr, L)
    dy1s = dy1 if isinstance(dy1, tuple) else (dy1,)
    n1, n_add = len(dy1s), len(adds)

    def body(*refs):
        x_ref, g1_ref, g2_ref = refs[:3]
        dy1_refs = refs[3:3 + n1]
        dy2_ref = refs[3 + n1]
        add_refs = refs[4 + n1:4 + n1 + n_add]
        dx_ref, dg1_ref, dg2_ref = refs[4 + n1 + n_add:]
        xf = x_ref[...]
        d1 = dy1_refs[0][...].astype(F32)
        for d_ref in dy1_refs[1:]:
            d1 = d1 + d_ref[...].astype(F32)
        d2 = dy2_ref[...].astype(F32)
        r = lax.rsqrt(jnp.mean(xf * xf, axis=-1, keepdims=True) + EPS)
        gy = d1 * g1_ref[...] + d2 * g2_ref[...]
        c = jnp.mean(xf * gy, axis=-1, keepdims=True) * (r * r * r)
        dx = gy * r - xf * c
        for a_ref in add_refs:
            dx = dx + a_ref[...].astype(F32)
        dx_ref[...] = dx

        @pl.when(pl.program_id(0) == 0)
        def _():
            dg1_ref[...] = jnp.zeros_like(dg1_ref)
            dg2_ref[...] = jnp.zeros_like(dg2_ref)

        xr = xf * r
        dg1_ref[...] += jnp.sum(d1 * xr, axis=0, keepdims=True)
        dg2_ref[...] += jnp.sum(d2 * xr, axis=0, keepdims=True)

    row = pl.BlockSpec((tr, D), lambda i: (i, 0))
    vec = pl.BlockSpec((1, D), lambda i: (0, 0))
    dx, dg1, dg2 = pl.pallas_call(
        body, name=name,
        out_shape=(jax.ShapeDtypeStruct((L, D), F32), jax.ShapeDtypeStruct((1, D), F32),
                   jax.ShapeDtypeStruct((1, D), F32)),
        grid=(L // tr,), in_specs=[row, vec, vec] + [row] * (n1 + 1 + n_add), out_specs=(row, vec, vec),
        compiler_params=_params("arbitrary"),
    )(x, g1.reshape(1, D), g2.reshape(1, D), *dy1s, dy2, *adds)
    return dx, dg1.reshape(D), dg2.reshape(D)


def _final_norm_loss(o, g, res, target, *, tr=256):
    L, D = o.shape
    tr = min(tr, L)

    def body(o_ref, g_ref, r_ref, t_ref, dh_ref, loss_ref):
        xf = o_ref[...]
        r = lax.rsqrt(jnp.mean(xf * xf, axis=-1, keepdims=True) + EPS)
        e = (r_ref[...] + xf * r * g_ref[...]) - t_ref[...]
        dh_ref[...] = e * (1.0 / D)

        @pl.when(pl.program_id(0) == 0)
        def _():
            loss_ref[...] = jnp.zeros_like(loss_ref)

        loss_ref[...] += jnp.sum(e * e, axis=0, keepdims=True) * (0.5 / D)

    row = pl.BlockSpec((tr, D), lambda i: (i, 0))
    vec = pl.BlockSpec((1, D), lambda i: (0, 0))
    dh, lp = pl.pallas_call(
        body, name="post_norm_1_loss",
        out_shape=(jax.ShapeDtypeStruct((L, D), F32), jax.ShapeDtypeStruct((1, D), F32)),
        grid=(L // tr,), in_specs=[row, vec, row, row], out_specs=(row, vec),
        compiler_params=_params("arbitrary"),
    )(o, g.reshape(1, D), res, target)
    return dh, lp


def _s5_coeffs(lr, li, ls):
    dt = jnp.exp(ls)
    mag = jnp.exp(lr * dt)
    ar = mag * jnp.cos(li * dt)
    ai = mag * jnp.sin(li * dt)
    den = lr * lr + li * li
    cr = ((ar - 1.0) * lr + ai * li) / den
    ci = (ai * lr - (ar - 1.0) * li) / den
    return dt, ar, ai, den, cr, ci


def _s5_prep(lam_re, lam_im, log_step, b_re_t, b_im_t):
    G, P = lam_re.shape
    H = b_re_t.shape[1]

    def body(lr_ref, li_ref, ls_ref, br_ref, bi_ref, ar_ref, ai_ref, bbr_ref, bbi_ref):
        _, ar, ai, _, cr, ci = _s5_coeffs(lr_ref[...], li_ref[...], ls_ref[...])
        ar_ref[...] = ar
        ai_ref[...] = ai
        br, bi = br_ref[...], bi_ref[...]
        crb, cib = cr[:, None, :], ci[:, None, :]
        bbr_ref[...] = crb * br - cib * bi
        bbi_ref[...] = crb * bi + cib * br

    return pl.pallas_call(
        body, name="s5_prep",
        out_shape=(jax.ShapeDtypeStruct((G, P), F32), jax.ShapeDtypeStruct((G, P), F32),
                   jax.ShapeDtypeStruct((G, H, P), F32), jax.ShapeDtypeStruct((G, H, P), F32)),
        compiler_params=_params(),
    )(lam_re, lam_im, log_step.reshape(G, 1), b_re_t, b_im_t)


def _s5_prep_bwd(lam_re, lam_im, log_step, b_re_t, b_im_t, d_ar, d_ai, d_bbr, d_bbi):
    G, P = lam_re.shape
    H = b_re_t.shape[1]

    def body(lr_ref, li_ref, ls_ref, br_ref, bi_ref, dar_ref, dai_ref, dbbr_ref, dbbi_ref,
             dlr_ref, dli_ref, dls_ref, dbr_ref, dbi_ref):
        lr, li = lr_ref[...], li_ref[...]
        dt, ar, ai, den, cr, ci = _s5_coeffs(lr, li, ls_ref[...])
        br, bi = br_ref[...], bi_ref[...]
        gbr, gbi = dbbr_ref[...], dbbi_ref[...]
        crb, cib = cr[:, None, :], ci[:, None, :]
        dbr_ref[...] = crb * gbr + cib * gbi
        dbi_ref[...] = crb * gbi - cib * gbr
        gcr = jnp.sum(br * gbr + bi * gbi, axis=1)
        gci = jnp.sum(br * gbi - bi * gbr, axis=1)
        ilr, ili = lr / den, -li / den
        gar = dar_ref[...] + (ilr * gcr + ili * gci)
        gai = dai_ref[...] + (ilr * gci - ili * gcr)
        qr, qi = cr * ilr - ci * ili, cr * ili + ci * ilr
        glr = -(qr * gcr + qi * gci)
        gli = -(qr * gci - qi * gcr)
        glr = glr + dt * (ar * gar + ai * gai)
        gli = gli + dt * (ar * gai - ai * gar)
        wr, wi = lr * ar - li * ai, lr * ai + li * ar
        gdt = jnp.sum(wr * gar + wi * gai, axis=1, keepdims=True)
        dlr_ref[...] = glr
        dli_ref[...] = gli
        dls_ref[...] = gdt * dt

    return pl.pallas_call(
        body, name="s5_prep_bwd",
        out_shape=(jax.ShapeDtypeStruct((G, P), F32), jax.ShapeDtypeStruct((G, P), F32),
                   jax.ShapeDtypeStruct((G, 1), F32),
                   jax.ShapeDtypeStruct((G, H, P), F32), jax.ShapeDtypeStruct((G, H, P), F32)),
        compiler_params=_params(),
    )(lam_re, lam_im, log_step.reshape(G, 1), b_re_t, b_im_t, d_ar, d_ai, d_bbr, d_bbi)


def _s5_block_mats(bbr_t, bbi_t, c_re, c_im):
    bmat = _s5_expand(bbr_t, bbi_t)
    cmat = jnp.transpose(_s5_expand(c_re, -c_im), (0, 2, 1))
    return bmat.astype(BF16), cmat.astype(BF16)


def _s5_diag_mask():
    r = lax.broadcasted_iota(jnp.int32, (LANES, 2 * STATE_COLS), 0) // SSM_GROUP
    c = (lax.broadcasted_iota(jnp.int32, (LANES, 2 * STATE_COLS), 1) % STATE_COLS) // SSM_STATE
    return (r == c).astype(F32)


def _s5_expand(re, im):
    re = jnp.tile(re.reshape(SSM_BLOCKS, LANES, SSM_STATE), (1, 1, GROUPS_PER_BLOCK))
    im = jnp.tile(im.reshape(SSM_BLOCKS, LANES, SSM_STATE), (1, 1, GROUPS_PER_BLOCK))
    return jnp.concatenate([re, im], axis=-1) * _s5_diag_mask()[None]


def _s5_block_diag(dmat):
    d = dmat * _s5_diag_mask()[None]
    parts = []
    for ri in range(2):
        acc = 0.0
        for g in range(GROUPS_PER_BLOCK):
            c0 = ri * STATE_COLS + g * SSM_STATE
            acc = acc + d[:, :, c0:c0 + SSM_STATE]
        parts.append(acc.reshape(SSM_GROUPS, SSM_GROUP, SSM_STATE))
    return jnp.stack(parts)


def _s5_a_rows(ar, ai):
    a = jnp.concatenate([ar.reshape(SSM_BLOCKS, STATE_COLS), ai.reshape(SSM_BLOCKS, STATE_COLS)], axis=1)
    return jnp.broadcast_to(a[:, None, :], (SSM_BLOCKS, SUBLANES, 2 * STATE_COLS))


def _to_step_major(src_ref, dst_ref, seg):
    for s in range(SUBLANES):
        dst_ref[pl.ds(s, seg, stride=SUBLANES), :] = src_ref[pl.ds(seg * s, seg), :]


def _segment_rows(ref, s, seg):
    return ref[pl.ds(s, seg, stride=SUBLANES), :]


def _cmul(ar, ai, xr, xi):
    return ar * xr - ai * xi, ar * xi + ai * xr


def _s5_tables(a_ref, pw_s, pwr_s, S, seg):
    ar, ai = a_ref[:, :S], a_ref[:, S:]

    def step(i, c):
        pr, pi = c
        pw_s[i, :, :S] = pr
        pw_s[i, :, S:] = pi
        nr, ni = _cmul(ar, ai, pr, pi)
        pwr_s[seg - 1 - i, :, :S] = nr
        pwr_s[seg - 1 - i, :, S:] = ni
        return nr, ni

    pr, pi = lax.fori_loop(0, seg, step, (jnp.ones_like(ar), jnp.zeros_like(ai)))
    pw_s[seg, :, :S] = pr
    pw_s[seg, :, S:] = pi


def _s5_fwd(proj, bmat, cmat, a_rows, d_skip, *, tc=512):
    L = proj.shape[0]
    tc = min(tc, L)
    nt = L // tc
    seg = tc // SUBLANES
    S = STATE_COLS

    def body(u_ref, b_ref, c_ref, a_ref, d_ref, y_ref, yg_ref, xp_ref,
             bu_s, xp_s, pw_s, pwr_s, carry_s, e_s, up_s, yc_s):
        @pl.when(pl.program_id(1) == 0)
        def _():
            carry_s[...] = jnp.zeros_like(carry_s)
            _s5_tables(a_ref, pw_s, pwr_s, S, seg)

        ar, ai = a_ref[:, :S], a_ref[:, S:]
        _to_step_major(u_ref, up_s, seg)
        bu = jnp.dot(up_s[...].astype(BF16), b_ref[...], preferred_element_type=F32)
        bu_s[...] = bu.reshape(seg, SUBLANES, 2 * S)

        def step(i, carry):
            cr, ci = carry
            xp_s[i, :, :S] = cr
            xp_s[i, :, S:] = ci
            return ar * cr - ai * ci + bu_s[i, :, :S], ar * ci + ai * cr + bu_s[i, :, S:]

        zero = jnp.zeros((SUBLANES, S), F32)
        fr, fi = lax.fori_loop(0, seg, step, (zero, zero))
        pr, pi = pw_s[seg, 0:1, :S], pw_s[seg, 0:1, S:]
        er, ei = carry_s[0:1, :S], carry_s[0:1, S:]
        for s in range(SUBLANES):
            e_s[s:s + 1, :S] = er
            e_s[s:s + 1, S:] = ei
            tr, ti = _cmul(pr, pi, er, ei)
            er, ei = fr[s:s + 1] + tr, fi[s:s + 1] + ti
        carry_s[0:1, :S] = er
        carry_s[0:1, S:] = ei
        pw = pw_s[0:seg]
        tr, ti = _cmul(pw[:, :, :S], pw[:, :, S:], e_s[:, :S][None], e_s[:, S:][None])
        xl = xp_s[...]
        xp = jnp.concatenate([xl[:, :, :S] + tr, xl[:, :, S:] + ti], axis=-1).reshape(tc, 2 * S)
        xp_ref[...] = xp
        a1r, a1i = ar[0:1], ai[0:1]
        x_re = a1r * xp[:, :S] - a1i * xp[:, S:] + bu[:, :S]
        x_im = a1r * xp[:, S:] + a1i * xp[:, :S] + bu[:, S:]
        xs = jnp.concatenate([x_re, x_im], axis=1).astype(BF16)
        yc_s[...] = jnp.dot(xs, c_ref[...], preferred_element_type=F32)
        for s in range(SUBLANES):
            rows = pl.ds(seg * s, seg)
            y = _segment_rows(yc_s, s, seg) + d_ref[...] * u_ref[rows, :]
            y_ref[rows, :] = y
            yg_ref[rows, :] = _gelu(y).astype(BF16)

    return pl.pallas_call(
        body, name="s5_fwd",
        out_shape=(jax.ShapeDtypeStruct((L, MAIN_WIDTH), F32),
                   jax.ShapeDtypeStruct((L, MAIN_WIDTH), BF16),
                   jax.ShapeDtypeStruct((L, SSM_BLOCKS * 2 * S), F32)),
        grid=(SSM_BLOCKS, nt),
        in_specs=[pl.BlockSpec((tc, LANES), lambda b, t: (t, b)),
                  pl.BlockSpec((None, LANES, 2 * S), lambda b, t: (b, 0, 0)),
                  pl.BlockSpec((None, 2 * S, LANES), lambda b, t: (b, 0, 0)),
                  pl.BlockSpec((None, SUBLANES, 2 * S), lambda b, t: (b, 0, 0)),
                  pl.BlockSpec((1, LANES), lambda b, t: (0, b))],
        out_specs=(pl.BlockSpec((tc, LANES), lambda b, t: (t, b)),
                   pl.BlockSpec((tc, LANES), lambda b, t: (t, b)),
                   pl.BlockSpec((tc, 2 * S), lambda b, t: (t, b))),
        scratch_shapes=[pltpu.VMEM((seg, SUBLANES, 2 * S), F32),
                        pltpu.VMEM((seg, SUBLANES, 2 * S), F32),
                        pltpu.VMEM((seg + 1, SUBLANES, 2 * S), F32),
                        pltpu.VMEM((seg, SUBLANES, 2 * S), F32),
                        pltpu.VMEM((SUBLANES, 2 * S), F32),
                        pltpu.VMEM((SUBLANES, 2 * S), F32),
                        pltpu.VMEM((tc, LANES), F32),
                        pltpu.VMEM((tc, LANES), F32)],
        compiler_params=_params("parallel", "arbitrary"),
    )(proj, bmat, cmat, a_rows, d_skip.reshape(1, MAIN_WIDTH))


def _s5_bwd(proj, dyg_a, dyg_b, y, xp, bmat, cmat, a_rows, d_skip, dproj, *, tc=512):
    L = proj.shape[0]
    tc = min(tc, L)
    nt = L // tc
    seg = tc // SUBLANES
    S = STATE_COLS
    nn = (((1,), (1,)), ((), ()))
    tn = (((0,), (0,)), ((), ()))

    def body(u_ref, dyga_ref, dygb_ref, y_ref, xp_ref, b_ref, c_ref, a_ref, d_ref, dp_hbm,
             du_ref, db_ref, dc_ref, da_ref, dd_ref, dl_s, pw_s, pwr_s, carry_s, e_s, up_s, dy_s, dyp_s, dup_s):
        @pl.when(pl.program_id(1) == 0)
        def _():
            carry_s[...] = jnp.zeros_like(carry_s)
            db_ref[...] = jnp.zeros_like(db_ref)
            dc_ref[...] = jnp.zeros_like(dc_ref)
            da_ref[...] = jnp.zeros_like(da_ref)
            dd_ref[...] = jnp.zeros_like(dd_ref)
            _s5_tables(a_ref, pw_s, pwr_s, S, seg)

        ar, ai = a_ref[:, :S], a_ref[:, S:]
        a1r, a1i = ar[0:1], ai[0:1]
        u = u_ref[...]
        dy = (dyga_ref[...] + dygb_ref[...]) * _gelu_grad(y_ref[...])
        dy_s[...] = dy
        xp = xp_ref[...]
        _to_step_major(u_ref, up_s, seg)
        _to_step_major(dy_s, dyp_s, seg)
        ubp = up_s[...].astype(BF16)
        dyp = dyp_s[...].astype(BF16)
        bu = jnp.dot(ubp, b_ref[...], preferred_element_type=F32)
        x_re = a1r * xp[:, :S] - a1i * xp[:, S:] + bu[:, :S]
        x_im = a1r * xp[:, S:] + a1i * xp[:, :S] + bu[:, S:]
        xs = jnp.concatenate([x_re, x_im], axis=1).astype(BF16)
        dc_ref[...] += lax.dot_general(dyp, xs, tn, preferred_element_type=F32)
        dx = lax.dot_general(dyp, c_ref[...], nn, preferred_element_type=F32)
        dl_s[...] = dx.reshape(seg, SUBLANES, 2 * S)

        def step(k, carry):
            cr, ci = carry
            i = seg - 1 - k
            lr = dl_s[i, :, :S] + (ar * cr + ai * ci)
            li = dl_s[i, :, S:] + (ar * ci - ai * cr)
            dl_s[i, :, :S] = lr
            dl_s[i, :, S:] = li
            return lr, li

        zero = jnp.zeros((SUBLANES, S), F32)
        fr, fi = lax.fori_loop(0, seg, step, (zero, zero))
        pr, pi = pw_s[seg, 0:1, :S], pw_s[seg, 0:1, S:]
        er, ei = carry_s[0:1, :S], carry_s[0:1, S:]
        for s in range(SUBLANES - 1, -1, -1):
            e_s[s:s + 1, :S] = er
            e_s[s:s + 1, S:] = ei
            er, ei = fr[s:s + 1] + (pr * er + pi * ei), fi[s:s + 1] + (pr * ei - pi * er)
        carry_s[0:1, :S] = er
        carry_s[0:1, S:] = ei
        er, ei = e_s[:, :S][None], e_s[:, S:][None]
        pw = pwr_s[...]
        pwr, pwi = pw[:, :, :S], pw[:, :, S:]
        ll = dl_s[...]
        lam = jnp.concatenate([ll[:, :, :S] + (pwr * er + pwi * ei), ll[:, :, S:] + (pwr * ei - pwi * er)],
                              axis=-1).reshape(tc, 2 * S)
        l_re, l_im = lam[:, :S], lam[:, S:]
        da_ref[0:1, :S] += jnp.sum(l_re * xp[:, :S] + l_im * xp[:, S:], axis=0, keepdims=True)
        da_ref[0:1, S:] += jnp.sum(l_im * xp[:, :S] - l_re * xp[:, S:], axis=0, keepdims=True)
        lamb = lam.astype(BF16)
        dup_s[...] = lax.dot_general(lamb, b_ref[...], nn, preferred_element_type=F32)
        for s in range(SUBLANES):
            rows = pl.ds(seg * s, seg)
            du = _segment_rows(dup_s, s, seg) + d_ref[...] * dy_s[rows, :]
            du_ref[rows, :] = du.astype(du_ref.dtype)
        db_ref[...] += lax.dot_general(ubp, lamb, tn, preferred_element_type=F32)
        dd_ref[0:1, :] += jnp.sum(dy * u, axis=0, keepdims=True)

    rev = lambda b, t: (nt - 1 - t, b)
    return pl.pallas_call(
        body, name="s5_bwd",
        out_shape=(jax.ShapeDtypeStruct(dproj.shape, dproj.dtype),
                   jax.ShapeDtypeStruct((SSM_BLOCKS, LANES, 2 * S), F32),
                   jax.ShapeDtypeStruct((SSM_BLOCKS, LANES, 2 * S), F32),
                   jax.ShapeDtypeStruct((SSM_BLOCKS, SUBLANES, 2 * S), F32),
                   jax.ShapeDtypeStruct((SUBLANES, MAIN_WIDTH), F32)),
        input_output_aliases={9: 0},
        grid=(SSM_BLOCKS, nt),
        in_specs=[pl.BlockSpec((tc, LANES), rev),
                  pl.BlockSpec((tc, LANES), rev),
                  pl.BlockSpec((tc, LANES), rev),
                  pl.BlockSpec((tc, LANES), rev),
                  pl.BlockSpec((tc, 2 * S), rev),
                  pl.BlockSpec((None, LANES, 2 * S), lambda b, t: (b, 0, 0)),
                  pl.BlockSpec((None, 2 * S, LANES), lambda b, t: (b, 0, 0)),
                  pl.BlockSpec((None, SUBLANES, 2 * S), lambda b, t: (b, 0, 0)),
                  pl.BlockSpec((1, LANES), lambda b, t: (0, b)),
                  _ANY],
        out_specs=(pl.BlockSpec((tc, LANES), rev),
                   pl.BlockSpec((None, LANES, 2 * S), lambda b, t: (b, 0, 0)),
                   pl.BlockSpec((None, LANES, 2 * S), lambda b, t: (b, 0, 0)),
                   pl.BlockSpec((None, SUBLANES, 2 * S), lambda b, t: (b, 0, 0)),
                   pl.BlockSpec((SUBLANES, LANES), lambda b, t: (0, b))),
        scratch_shapes=[pltpu.VMEM((seg, SUBLANES, 2 * S), F32),
                        pltpu.VMEM((seg + 1, SUBLANES, 2 * S), F32),
                        pltpu.VMEM((seg, SUBLANES, 2 * S), F32),
                        pltpu.VMEM((SUBLANES, 2 * S), F32),
                        pltpu.VMEM((SUBLANES, 2 * S), F32),
                        pltpu.VMEM((tc, LANES), F32),
                        pltpu.VMEM((tc, LANES), F32),
                        pltpu.VMEM((tc, LANES), F32),
                        pltpu.VMEM((tc, LANES), F32)],
        compiler_params=_params("parallel", "arbitrary"),
    )(proj, dyg_a, dyg_b, y, xp, bmat, cmat, a_rows, d_skip.reshape(1, MAIN_WIDTH), dproj)


_Z_COLS = slice(MAIN_WIDTH, 2 * MAIN_WIDTH)
_ZM_COLS = slice(2 * MAIN_WIDTH + MEM_WIDTH, IN_WIDTH)


def _proj_rows(tr):
    return pl.BlockSpec((tr, IN_WIDTH), lambda i: (i, 0))


def _row_specs(tr):
    main = pl.BlockSpec((tr, MAIN_WIDTH), lambda i: (i, 0))
    z = pl.BlockSpec((tr, MAIN_WIDTH), lambda i: (i, 1))
    zm = pl.BlockSpec((tr, MEM_WIDTH), lambda i: (i, IN_WIDTH // MEM_WIDTH - 1))
    mem = pl.BlockSpec((tr, MEM_WIDTH), lambda i: (i, 0))
    cat = pl.BlockSpec((tr, D_MODEL), lambda i: (i, 0))
    vec = pl.BlockSpec((1, MAIN_WIDTH), lambda i: (0, 0))
    return main, z, zm, mem, cat, vec


def _gate_a_fwd(y, t, b_glu, proj, o_mem, *, tr=256):
    L = y.shape[0]
    tr = min(tr, L)

    def body(y_ref, t_ref, b_ref, z_ref, zm_ref, om_ref, o_ref):
        yg = _gelu(y_ref[...])
        sz, _ = _silu_and_grad(z_ref[...])
        o_ref[:, :MAIN_WIDTH] = (yg * _sigmoid(t_ref[...] + b_ref[...]) * sz).astype(BF16)
        szm, _ = _silu_and_grad(zm_ref[...])
        o_ref[:, MAIN_WIDTH:] = (om_ref[...] * szm).astype(BF16)

    main, z, zm, mem, cat, vec = _row_specs(tr)
    return pl.pallas_call(
        body, name="gate_a_fwd", out_shape=jax.ShapeDtypeStruct((L, D_MODEL), BF16),
        grid=(L // tr,), in_specs=[main, main, vec, z, zm, mem], out_specs=cat,
        compiler_params=_params("parallel"),
    )(y, t, b_glu.reshape(1, MAIN_WIDTH), proj, proj, o_mem)


def _gate_a_bwd(dcat, y, t, b_glu, proj, o_mem, *, tr=256):
    L = y.shape[0]
    tr = min(tr, L)

    def body(dc_ref, y_ref, t_ref, b_ref, z_ref, zm_ref, om_ref,
             dp_ref, dt_ref, dyg_ref, dom_ref, db_ref):
        dmain = dc_ref[:, :MAIN_WIDTH]
        dmemo = dc_ref[:, MAIN_WIDTH:]
        yg = _gelu(y_ref[...])
        sg = _sigmoid(t_ref[...] + b_ref[...])
        sz, gz = _silu_and_grad(z_ref[...])
        dp_ref[:, _Z_COLS] = (dmain * (yg * sg) * gz).astype(BF16)
        dy2 = dmain * sz
        dyg_ref[...] = dy2 * sg
        dt = dy2 * yg * (sg * (1.0 - sg))
        dt_ref[...] = dt.astype(BF16)

        @pl.when(pl.program_id(0) == 0)
        def _():
            db_ref[...] = jnp.zeros_like(db_ref)

        db_ref[...] += jnp.sum(dt, axis=0, keepdims=True)
        szm, gzm = _silu_and_grad(zm_ref[...])
        dom_ref[...] = dmemo * szm
        dp_ref[:, _ZM_COLS] = (dmemo * om_ref[...] * gzm).astype(BF16)

    main, z, zm, mem, cat, vec = _row_specs(tr)
    outs = pl.pallas_call(
        body, name="gate_a_bwd",
        out_shape=(jax.ShapeDtypeStruct((L, IN_WIDTH), BF16),
                   jax.ShapeDtypeStruct((L, MAIN_WIDTH), BF16), jax.ShapeDtypeStruct((L, MAIN_WIDTH), F32),
                   jax.ShapeDtypeStruct((L, MEM_WIDTH), F32), jax.ShapeDtypeStruct((1, MAIN_WIDTH), F32)),
        grid=(L // tr,), in_specs=[cat, main, main, vec, z, zm, mem],
        out_specs=(_proj_rows(tr), main, main, mem, vec),
        compiler_params=_params("arbitrary"),
    )(dcat, y, t, b_glu.reshape(1, MAIN_WIDTH), proj, proj, o_mem)
    return outs


def _gate_b_fwd(att, proj, o_mem, *, tr=256):
    L = att.shape[0]
    tr = min(tr, L)

    def body(a_ref, z_ref, zm_ref, om_ref, o_ref):
        sz, _ = _silu_and_grad(z_ref[...])
        o_ref[:, :MAIN_WIDTH] = (a_ref[...] * sz).astype(BF16)
        szm, _ = _silu_and_grad(zm_ref[...])
        o_ref[:, MAIN_WIDTH:] = (om_ref[...] * szm).astype(BF16)

    main, z, zm, mem, cat, _ = _row_specs(tr)
    return pl.pallas_call(
        body, name="gate_b_fwd", out_shape=jax.ShapeDtypeStruct((L, D_MODEL), BF16),
        grid=(L // tr,), in_specs=[main, z, zm, mem], out_specs=cat,
        compiler_params=_params("parallel"),
    )(att, proj, proj, o_mem)


def _gate_b_bwd(dcat, att, proj, o_mem, *, tr=256):
    L = att.shape[0]
    tr = min(tr, L)

    def body(dc_ref, a_ref, z_ref, zm_ref, om_ref, da_ref, dp_ref, dom_ref, dl_ref):
        dmain = dc_ref[:, :MAIN_WIDTH]
        dmemo = dc_ref[:, MAIN_WIDTH:]
        att = a_ref[...]
        sz, gz = _silu_and_grad(z_ref[...])
        datt = dmain * sz
        da_ref[...] = datt
        dp_ref[:, _Z_COLS] = (dmain * att * gz).astype(BF16)
        szm, gzm = _silu_and_grad(zm_ref[...])
        dom_ref[...] = dmemo * szm
        dp_ref[:, _ZM_COLS] = (dmemo * om_ref[...] * gzm).astype(BF16)
        prod = datt * att
        for h in range(FOX_HEADS):
            dl_ref[h] = jnp.sum(prod[:, h * HEAD_DIM:(h + 1) * HEAD_DIM], axis=1, keepdims=True)

    main, z, zm, mem, cat, _ = _row_specs(tr)
    delta = pl.BlockSpec((FOX_HEADS, tr, 1), lambda i: (0, i, 0))
    return pl.pallas_call(
        body, name="gate_b_bwd",
        out_shape=(jax.ShapeDtypeStruct((L, MAIN_WIDTH), F32), jax.ShapeDtypeStruct((L, IN_WIDTH), BF16),
                   jax.ShapeDtypeStruct((L, MEM_WIDTH), F32), jax.ShapeDtypeStruct((FOX_HEADS, L, 1), F32)),
        grid=(L // tr,), in_specs=[cat, main, z, zm, mem], out_specs=(main, _proj_rows(tr), mem, delta),
        compiler_params=_params("parallel"),
    )(dcat, att, proj, proj, o_mem)


_MEM_Q_COL = (2 * MAIN_WIDTH) // HEAD_DIM
_NT = (((1,), (1,)), ((), ()))
_TN = (((0,), (0,)), ((), ()))


def _mem_probs(q_ref, k_ref):
    qs = (q_ref[...] * (HEAD_DIM ** -0.5)).astype(BF16)
    s = lax.dot_general(qs, k_ref[...].astype(BF16), _NT, preferred_element_type=F32)
    e = jnp.exp(s - jnp.max(s, axis=-1, keepdims=True))
    return qs, e / jnp.sum(e, axis=-1, keepdims=True)


def _mem_attn_fwd(proj, kvm, *, tq=2048):
    L = proj.shape[0]
    tq = min(tq, L)

    def body(q_ref, k_ref, v_ref, o_ref):
        _, p = _mem_probs(q_ref, k_ref)
        o_ref[...] = jnp.dot(p.astype(BF16), v_ref[...].astype(BF16), preferred_element_type=F32)

    return pl.pallas_call(
        body, name="mem_attn_fwd", out_shape=jax.ShapeDtypeStruct((L, MEM_WIDTH), F32),
        grid=(MEM_HEADS, L // tq),
        in_specs=[pl.BlockSpec((tq, HEAD_DIM), lambda h, i: (i, _MEM_Q_COL + h)),
                  pl.BlockSpec((N_MEM, HEAD_DIM), lambda h, i: (0, h)),
                  pl.BlockSpec((N_MEM, HEAD_DIM), lambda h, i: (0, MEM_HEADS + h))],
        out_specs=pl.BlockSpec((tq, HEAD_DIM), lambda h, i: (i, h)),
        compiler_params=_params("parallel", "parallel"),
    )(proj, kvm, kvm)


def _mem_attn_bwd(proj, kvm, do, dproj, *, tq=2048):
    L = proj.shape[0]
    tq = min(tq, L)

    def body(q_ref, k_ref, v_ref, do_ref, dp_hbm, dq_ref, dk_ref, dv_ref):
        @pl.when(pl.program_id(1) == 0)
        def _():
            dk_ref[...] = jnp.zeros_like(dk_ref)
            dv_ref[...] = jnp.zeros_like(dv_ref)

        qs, p = _mem_probs(q_ref, k_ref)
        dob = do_ref[...].astype(BF16)
        dp = lax.dot_general(dob, v_ref[...].astype(BF16), _NT, preferred_element_type=F32)
        ds = p * (dp - jnp.sum(p * dp, axis=-1, keepdims=True))
        dsb = ds.astype(BF16)
        dq = jnp.dot(dsb, k_ref[...].astype(BF16), preferred_element_type=F32) * (HEAD_DIM ** -0.5)
        dq_ref[...] = dq.astype(BF16)
        dk_ref[...] += lax.dot_general(dsb, qs, _TN, preferred_element_type=F32)
        dv_ref[...] += lax.dot_general(p.astype(BF16), dob, _TN, preferred_element_type=F32)

    dproj, dk, dv = pl.pallas_call(
        body, name="mem_attn_bwd",
        out_shape=(jax.ShapeDtypeStruct(dproj.shape, dproj.dtype),
                   jax.ShapeDtypeStruct((N_MEM, MEM_WIDTH), F32),
                   jax.ShapeDtypeStruct((N_MEM, MEM_WIDTH), F32)),
        grid=(MEM_HEADS, L // tq),
        in_specs=[pl.BlockSpec((tq, HEAD_DIM), lambda h, i: (i, _MEM_Q_COL + h)),
                  pl.BlockSpec((N_MEM, HEAD_DIM), lambda h, i: (0, h)),
                  pl.BlockSpec((N_MEM, HEAD_DIM), lambda h, i: (0, MEM_HEADS + h)),
                  pl.BlockSpec((tq, HEAD_DIM), lambda h, i: (i, h)),
                  _ANY],
        out_specs=(pl.BlockSpec((tq, HEAD_DIM), lambda h, i: (i, _MEM_Q_COL + h)),
                   pl.BlockSpec((N_MEM, HEAD_DIM), lambda h, i: (0, h)),
                   pl.BlockSpec((N_MEM, HEAD_DIM), lambda h, i: (0, h))),
        input_output_aliases={4: 0},
        compiler_params=_params("parallel", "arbitrary"),
    )(proj, kvm, kvm, do, dproj)
    return dproj, jnp.concatenate([dk, dv], axis=1)


def _tile_cumsum(x, row, reverse):
    for sh in (1, 2, 4):
        if reverse:
            x = x + jnp.where(row < SUBLANES - sh, pltpu.roll(x, SUBLANES - sh, 0), 0.0)
        else:
            x = x + jnp.where(row >= sh, pltpu.roll(x, sh, 0), 0.0)
    return x


def _fgate_fwd(pre, b_pad):
    L = pre.shape[0]
    n8 = L // SUBLANES

    def body(p_ref, b_ref, o_ref):
        row = lax.broadcasted_iota(jnp.int32, (SUBLANES, LANES), 0)
        b = b_ref[...]

        def step(i, carry):
            x = p_ref[i] + b
            logf = jnp.minimum(x, 0.0) - jnp.log(1.0 + jnp.exp(-jnp.abs(x)))
            t = _tile_cumsum(logf, row, False) + carry
            o_ref[i] = t
            return t[SUBLANES - 1:SUBLANES, :]

        lax.fori_loop(0, n8, step, jnp.zeros((1, LANES), F32))

    out = pl.pallas_call(
        body, name="fgate_fwd", out_shape=jax.ShapeDtypeStruct((n8, SUBLANES, LANES), F32),
        compiler_params=_params(),
    )(pre.reshape(n8, SUBLANES, LANES), b_pad.reshape(1, LANES))
    return out.reshape(L, LANES)


def _fgate_bwd(dfcum, pre, b_pad):
    L = pre.shape[0]
    n8 = L // SUBLANES

    def body(d_ref, p_ref, b_ref, o_ref, s_ref):
        row = lax.broadcasted_iota(jnp.int32, (SUBLANES, LANES), 0)
        b = b_ref[...]

        def step(k, carry):
            c, acc = carry
            i = n8 - 1 - k
            t = _tile_cumsum(d_ref[i], row, True) + c
            dpre = t * _sigmoid(-(p_ref[i] + b))
            o_ref[i] = dpre
            return t[0:1, :], acc + dpre

        _, acc = lax.fori_loop(0, n8, step, (jnp.zeros((1, LANES), F32), jnp.zeros((SUBLANES, LANES), F32)))
        s_ref[...] = jnp.sum(acc, axis=0, keepdims=True)

    dpre, db = pl.pallas_call(
        body, name="fgate_bwd",
        out_shape=(jax.ShapeDtypeStruct((n8, SUBLANES, LANES), F32), jax.ShapeDtypeStruct((1, LANES), F32)),
        compiler_params=_params(),
    )(dfcum.reshape(n8, SUBLANES, LANES), pre.reshape(n8, SUBLANES, LANES), b_pad.reshape(1, LANES))
    return dpre.reshape(L, LANES), db


FOX_BLOCK = 512


def _fox_scores(qs, k, fk, diagonal):
    s = lax.dot_general(qs, k, _NT, preferred_element_type=F32) - fk
    if diagonal:
        row = lax.broadcasted_iota(jnp.int32, s.shape, 0)
        col = lax.broadcasted_iota(jnp.int32, s.shape, 1)
        s = jnp.where(row >= col, s, NEG_BIG)
    return s


def _fox_specs(tq, L):
    nq = L // tq
    return dict(
        rows=lambda off: pl.BlockSpec((tq, HEAD_DIM), lambda h, i: (i, off + h)),
        seq=lambda off: pl.BlockSpec((L, HEAD_DIM), lambda h, i: (0, off + h)),
        col=pl.BlockSpec((None, None, tq, 1), lambda h, i: (h, i, 0, 0)),
        col_all=pl.BlockSpec((None, nq, tq, 1), lambda h, i: (h, 0, 0, 0)),
        row=pl.BlockSpec((None, None, 1, tq), lambda h, i: (h, i, 0, 0)),
        row_all=pl.BlockSpec((None, nq, 1, tq), lambda h, i: (h, 0, 0, 0)))


FOX_FWD_HEADS = 2


def _fox_fwd(proj, kv, fk):
    L = proj.shape[0]
    tq = min(FOX_BLOCK, L)
    nq = L // tq
    nh = FOX_FWD_HEADS
    W = nh * HEAD_DIM

    def body(q_ref, k_ref, v_ref, fk_ref, o_ref, lse_ref, m_s, l_s, acc_s):
        qi = pl.program_id(1)
        cols = [slice(a * HEAD_DIM, (a + 1) * HEAD_DIM) for a in range(nh)]
        qs = [(q_ref[:, cs] * (HEAD_DIM ** -0.5)).astype(BF16) for cs in cols]
        m_s[...] = jnp.full_like(m_s, NEG_BIG)
        l_s[...] = jnp.zeros_like(l_s)
        acc_s[...] = jnp.zeros_like(acc_s)

        def block(j, diagonal):
            r0 = pl.multiple_of(j * tq, tq)
            for a, cs in enumerate(cols):
                s = _fox_scores(qs[a], k_ref[pl.ds(r0, tq), cs], fk_ref[a, j], diagonal)
                m_new = jnp.maximum(m_s[a], jnp.max(s, axis=-1, keepdims=True))
                alpha = jnp.exp(m_s[a] - m_new)
                p = jnp.exp(s - m_new)
                l_s[a] = alpha * l_s[a] + jnp.sum(p, axis=-1, keepdims=True)
                acc_s[a] = alpha * acc_s[a] + jnp.dot(p.astype(BF16), v_ref[pl.ds(r0, tq), cs],
                                                      preferred_element_type=F32)
                m_s[a] = m_new

        def below(j, carry):
            block(j, False)
            return carry

        lax.fori_loop(0, qi, below, 0)
        block(qi, True)
        for a, cs in enumerate(cols):
            o_ref[:, cs] = acc_s[a] / l_s[a]
            lse_ref[a] = m_s[a] + jnp.log(l_s[a])

    return pl.pallas_call(
        body, name="fox_fwd",
        out_shape=(jax.ShapeDtypeStruct((L, MAIN_WIDTH), F32),
                   jax.ShapeDtypeStruct((FOX_HEADS, nq, tq, 1), F32)),
        grid=(FOX_HEADS // nh, nq),
        in_specs=[pl.BlockSpec((tq, W), lambda h, i: (i, h)),
                  pl.BlockSpec((L, W), lambda h, i: (0, h)),
                  pl.BlockSpec((L, W), lambda h, i: (0, FOX_HEADS // nh + h)),
                  pl.BlockSpec((nh, nq, 1, tq), lambda h, i: (h, 0, 0, 0))],
        out_specs=(pl.BlockSpec((tq, W), lambda h, i: (i, h)),
                   pl.BlockSpec((nh, None, tq, 1), lambda h, i: (h, i, 0, 0))),
        scratch_shapes=[pltpu.VMEM((nh, tq, 1), F32), pltpu.VMEM((nh, tq, 1), F32),
                        pltpu.VMEM((nh, tq, HEAD_DIM), F32)],
        compiler_params=_params("parallel", "parallel"),
    )(proj, kv, kv, fk)


def _fox_bwd_dq(proj, kv, fk, lse, delta, datt, dproj):
    L = proj.shape[0]
    tq = min(FOX_BLOCK, L)
    nq = L // tq
    sp = _fox_specs(tq, L)

    def body(q_ref, k_ref, v_ref, fk_ref, lse_ref, dl_ref, do_ref, dp_hbm, dq_ref, df_ref, acc_s, df_s):
        qi = pl.program_id(1)
        qs = (q_ref[...] * (HEAD_DIM ** -0.5)).astype(BF16)
        dob = do_ref[...].astype(BF16)
        lse, dl = lse_ref[...], dl_ref[...]
        acc_s[...] = jnp.zeros_like(acc_s)
        df_s[...] = jnp.zeros_like(df_s)

        def block(j, diagonal):
            r0 = pl.multiple_of(j * tq, tq)
            k = k_ref[pl.ds(r0, tq), :]
            p = jnp.exp(_fox_scores(qs, k, fk_ref[j], diagonal) - lse)
            dp = lax.dot_general(dob, v_ref[pl.ds(r0, tq), :], _NT, preferred_element_type=F32)
            ds = p * (dp - dl)
            acc_s[...] += jnp.dot(ds.astype(BF16), k, preferred_element_type=F32)
            df_s[...] += jnp.sum(ds, axis=1, keepdims=True)

        def below(j, carry):
            block(j, False)
            return carry

        lax.fori_loop(0, qi, below, 0)
        block(qi, True)
        dq_ref[...] = (acc_s[...] * (HEAD_DIM ** -0.5)).astype(BF16)
        df_ref[...] = df_s[...]

    return pl.pallas_call(
        body, name="fox_bwd_dq",
        out_shape=(jax.ShapeDtypeStruct(dproj.shape, dproj.dtype),
                   jax.ShapeDtypeStruct((FOX_HEADS, nq, tq, 1), F32)),
        grid=(FOX_HEADS, nq),
        in_specs=[sp["rows"](0), sp["seq"](0), sp["seq"](FOX_HEADS), sp["row_all"],
                  sp["col"], sp["col"], sp["rows"](0), _ANY],
        out_specs=(sp["rows"](0), sp["col"]),
        input_output_aliases={7: 0},
        scratch_shapes=[pltpu.VMEM((tq, HEAD_DIM), F32), pltpu.VMEM((tq, 1), F32)],
        compiler_params=_params("parallel", "parallel"),
    )(proj, kv, kv, fk, lse, delta, datt, dproj)


def _fox_bwd_dkv(proj, kv, fk, lse, delta, datt):
    L = proj.shape[0]
    tq = min(FOX_BLOCK, L)
    nq = L // tq
    sp = _fox_specs(tq, L)

    def body(q_ref, k_ref, v_ref, fk_ref, lse_ref, dl_ref, do_ref,
             dk_ref, dv_ref, df_ref, dk_s, dv_s, df_s):
        ki = pl.program_id(1)
        k, v, fk = k_ref[...], v_ref[...], fk_ref[...]
        dk_s[...] = jnp.zeros_like(dk_s)
        dv_s[...] = jnp.zeros_like(dv_s)
        df_s[...] = jnp.zeros_like(df_s)

        def block(i, diagonal):
            r0 = pl.multiple_of(i * tq, tq)
            qs = (q_ref[pl.ds(r0, tq), :] * (HEAD_DIM ** -0.5)).astype(BF16)
            dob = do_ref[pl.ds(r0, tq), :].astype(BF16)
            p = jnp.exp(_fox_scores(qs, k, fk, diagonal) - lse_ref[i])
            dp = lax.dot_general(dob, v, _NT, preferred_element_type=F32)
            ds = p * (dp - dl_ref[i])
            dv_s[...] += lax.dot_general(p.astype(BF16), dob, _TN, preferred_element_type=F32)
            dk_s[...] += lax.dot_general(ds.astype(BF16), qs, _TN, preferred_element_type=F32)
            df_s[...] -= jnp.sum(ds, axis=0, keepdims=True)

        def above(i, carry):
            block(i, False)
            return carry

        block(ki, True)
        lax.fori_loop(ki + 1, nq, above, 0)
        dk_ref[...] = dk_s[...].astype(BF16)
        dv_ref[...] = dv_s[...].astype(BF16)
        df_ref[...] = df_s[...]

    return pl.pallas_call(
        body, name="fox_bwd_dkv",
        out_shape=(jax.ShapeDtypeStruct((L, MAIN_WIDTH), BF16),
                   jax.ShapeDtypeStruct((L, MAIN_WIDTH), BF16),
                   jax.ShapeDtypeStruct((FOX_HEADS, nq, 1, tq), F32)),
        grid=(FOX_HEADS, nq),
        in_specs=[sp["seq"](0), sp["rows"](0), sp["rows"](FOX_HEADS), sp["row"],
                  sp["col_all"], sp["col_all"], sp["seq"](0)],
        out_specs=(sp["rows"](0), sp["rows"](0), sp["row"]),
        scratch_shapes=[pltpu.VMEM((tq, HEAD_DIM), F32), pltpu.VMEM((tq, HEAD_DIM), F32),
                        pltpu.VMEM((1, tq), F32)],
        compiler_params=_params("parallel", "parallel"),
    )(proj, kv, kv, fk, lse, delta, datt)


def _pad_lanes(a):
    return jnp.pad(a, ((0, 0), (0, LANES - a.shape[1])))


def _mem_branch_fwd(memn, w_mk, proj, tag):
    kvm = _mm(memn, w_mk, name="mem_kv_" + tag)
    return kvm, _mem_attn_fwd(proj, kvm)


def _mem_branch_bwd(mem, g, w_mk, proj, memn, kvm, do_mem, dproj, tag):
    dproj, dkvm = _mem_attn_bwd(proj, kvm, do_mem, dproj)
    dkvm = dkvm.astype(BF16)
    dw_mk = _mm(memn, dkvm, ta=True, name="dw_mem_kv_" + tag, out_dtype=BF16)
    dmemn = _mm(dkvm, w_mk, tb=True, name="dmemn_" + tag)
    _, dg = _rmsnorm_bwd(mem, g, dmemn, name="mem_norm_bwd_" + tag, dx_dtype=BF16)
    return dproj, dw_mk, dg


def _local_step(x, mem, target, w, fetch=None, grads_ready=None):
    if grads_ready is None:
        grads_ready = lambda group, grads, token: token
    L = x.shape[0]
    g = {}
    w = dict(w)

    b_re_t = jnp.transpose(w["b_re"], (0, 2, 1))
    b_im_t = jnp.transpose(w["b_im"], (0, 2, 1))
    ar, ai, bbr_t, bbi_t = _s5_prep(w["lam_re"], w["lam_im"], w["log_step"], b_re_t, b_im_t)
    bmat, cmat = _s5_block_mats(bbr_t, bbi_t, w["c_re"], w["c_im"])
    a_rows = _s5_a_rows(ar, ai)

    hn0 = _rmsnorm_fwd(x, w["pre_norm_g"][0], name="pre_norm_0", out_dtype=BF16)
    memn0 = _rmsnorm_fwd(mem, w["mem_norm_g"][0], name="mem_norm_0", out_dtype=BF16)
    memn1 = _rmsnorm_fwd(mem, w["mem_norm_g"][1], name="mem_norm_1", out_dtype=BF16)
    if fetch is not None:
        w.update(fetch("a", [hn0, memn0, memn1, bmat, cmat, a_rows]))
    proj_a = _mm(hn0, w["w_in_a"], name="in_proj_a")
    y, yg, xp = _s5_fwd(proj_a, bmat, cmat, a_rows, w["d_skip"])
    if fetch is not None:
        w.update(fetch("b", yg))
    t = _mm(yg, w["w_glu"], name="glu_proj")
    kvm0, om0 = _mem_branch_fwd(memn0, w["w_mem_kv"][0], proj_a, "0")
    cat0 = _gate_a_fwd(y, t, w["b_glu"], proj_a, om0)
    o0 = _mm(cat0, w["w_out"][0], name="out_proj_0")
    h1 = _rmsnorm_fwd(o0, w["post_norm_g"][0], res=x, name="post_norm_0")

    kv_in = _rmsnorm_fwd(h1, w["kv_norm_g"], name="kv_norm", out_dtype=BF16)
    if fetch is not None:
        w.update(fetch("c", kv_in))
    kv = _mm(kv_in, w["w_kv"], name="kv_proj", out_dtype=BF16)
    pre_f = _mm(kv_in, w["w_fgate"], name="fgate_proj")
    b_f = jnp.pad(w["b_fgate"], (0, LANES - FOX_HEADS))
    fcum = _fgate_fwd(pre_f, b_f)
    fc = jnp.transpose(fcum[:, :FOX_HEADS])
    tq = min(FOX_BLOCK, L)
    fk = fc.reshape(FOX_HEADS, L // tq, 1, tq)

    hn1 = _rmsnorm_fwd(h1, w["pre_norm_g"][1], name="pre_norm_1", out_dtype=BF16)
    proj_b = _mm(hn1, w["w_in_b"], name="in_proj_b")
    att, lse = _fox_fwd(proj_b, kv, fk)
    kvm1, om1 = _mem_branch_fwd(memn1, w["w_mem_kv"][1], proj_b, "1")
    cat1 = _gate_b_fwd(att, proj_b, om1)
    o1 = _mm(cat1, w["w_out"][1], name="out_proj_1")
    dh2, loss_row = _final_norm_loss(o1, w["post_norm_g"][1], h1, target)

    do1, dpost1 = _rmsnorm_bwd(o1, w["post_norm_g"][1], dh2, name="post_norm_bwd_1", dx_dtype=BF16)
    dcat1 = _mm(do1, w["w_out"][1], tb=True, name="dcat_1", out_dtype=BF16)
    g["w_out_1"] = _mm(cat1, do1, ta=True, name="dw_out_1", out_dtype=BF16)
    datt, dproj_b, dom1, delta = _gate_b_bwd(dcat1, att, proj_b, om1)
    dproj_b, g["w_mem_kv_1"], dmemg1 = _mem_branch_bwd(mem, w["mem_norm_g"][1], w["w_mem_kv"][1], proj_b,
                                                      memn1, kvm1, dom1, dproj_b, "1")
    delta = delta.reshape(lse.shape)
    dproj_b, dfq = _fox_bwd_dq(proj_b, kv, fk, lse, delta, datt, dproj_b)
    dk, dv, dfk = _fox_bwd_dkv(proj_b, kv, fk, lse, delta, datt)
    g["w_in_b"] = _mm(hn1, dproj_b, ta=True, name="dw_in_b", out_dtype=BF16, shards=N_CHIPS)
    dhn1 = _mm(dproj_b, w["w_in_b"], tb=True, name="dhn_1")

    dkv = jnp.concatenate([dk, dv], axis=1)
    g["w_kv"] = _mm(kv_in, dkv, ta=True, name="dw_kv", out_dtype=BF16, shards=N_CHIPS)
    dkv_in_a = _mm(dkv, w["w_kv"], tb=True, name="dkv_in_kv")
    dfcum = _pad_lanes(jnp.transpose(dfq.reshape(FOX_HEADS, L) + dfk.reshape(FOX_HEADS, L)))
    dpre_f, db_f = _fgate_bwd(dfcum, pre_f, b_f)
    g["b_fgate"] = db_f[0, :FOX_HEADS]
    g["w_fgate"] = _mm(kv_in, dpre_f, ta=True, name="dw_fgate")[:, :FOX_HEADS]
    dkv_in_b = _mm(dpre_f, w["w_fgate"], tb=True, name="dkv_in_fgate")
    dh1, g["kv_norm_g"], dpre1 = _rmsnorm_bwd_pair(h1, w["kv_norm_g"], (dkv_in_a, dkv_in_b), w["pre_norm_g"][1],
                                                   dhn1, adds=(dh2,), name="kv_pre_norm_bwd")
    dh1 = grads_ready("b", g, dh1)

    do0, dpost0 = _rmsnorm_bwd(o0, w["post_norm_g"][0], dh1, name="post_norm_bwd_0", dx_dtype=BF16)
    dcat0 = _mm(do0, w["w_out"][0], tb=True, name="dcat_0", out_dtype=BF16)
    g["w_out_0"] = _mm(cat0, do0, ta=True, name="dw_out_0", out_dtype=BF16)
    dcat0 = grads_ready("b_send", g, dcat0)
    dproj_a, dt, dyg_a, dom0, db_glu = _gate_a_bwd(dcat0, y, t, w["b_glu"], proj_a, om0)
    g["b_glu"] = db_glu[0]
    g["w_glu"] = _mm(yg, dt, ta=True, name="dw_glu", out_dtype=BF16)
    dyg_b = _mm(dt, w["w_glu"], tb=True, name="dyg")
    dproj_a, g["w_mem_kv_0"], dmemg0 = _mem_branch_bwd(mem, w["mem_norm_g"][0], w["w_mem_kv"][0], proj_a,
                                                      memn0, kvm0, dom0, dproj_a, "0")
    dyg_b = grads_ready("a1", g, dyg_b)
    dproj_a, db_blk, dc_blk, da_rows, dd_skip = _s5_bwd(proj_a, dyg_a, dyg_b, y, xp, bmat, cmat, a_rows,
                                                        w["d_skip"], dproj_a)
    dproj_a = grads_ready("a1_send", g, dproj_a)
    g["d_skip"] = dd_skip[0]
    g["w_in_a"] = _mm(hn0, dproj_a, ta=True, name="dw_in_a", out_dtype=BF16, shards=N_CHIPS)
    dproj_a = grads_ready("a2", g, dproj_a)
    dhn0 = _mm(dproj_a, w["w_in_a"], tb=True, name="dhn_0")
    grad_x, dpre0 = _rmsnorm_bwd(x, w["pre_norm_g"][0], dhn0, adds=(dh1,), name="pre_norm_bwd_0")

    dbb = _s5_block_diag(db_blk)
    dcc = _s5_block_diag(dc_blk)
    g["c_re"], g["c_im"] = dcc[0], -dcc[1]
    d_ar = da_rows[:, 0, :STATE_COLS].reshape(SSM_GROUPS, SSM_STATE)
    d_ai = da_rows[:, 0, STATE_COLS:].reshape(SSM_GROUPS, SSM_STATE)
    dlr, dli, dls, dbr_t, dbi_t = _s5_prep_bwd(w["lam_re"], w["lam_im"], w["log_step"], b_re_t, b_im_t,
                                               d_ar, d_ai, dbb[0], dbb[1])
    g["lam_re"], g["lam_im"], g["log_step"] = dlr, dli, dls[:, 0]
    g["b_re"] = jnp.transpose(dbr_t, (0, 2, 1))
    g["b_im"] = jnp.transpose(dbi_t, (0, 2, 1))
    g["pre_norm_g"] = jnp.stack([dpre0, dpre1])
    g["post_norm_g"] = jnp.stack([dpost0, dpost1])
    g["mem_norm_g"] = jnp.stack([dmemg0, dmemg1])
    return loss_row, grad_x, g


_MESH = pl.DeviceIdType.MESH
_ANY = pl.BlockSpec(memory_space=pl.ANY)


def _place():
    x, y, c = lax.axis_index("x"), lax.axis_index("y"), lax.axis_index("c")
    chips = [(1 - x, y), (x, 1 - y), (1 - x, 1 - y)]
    return x, y, c, chips


_HBM = pl.BlockSpec(memory_space=pltpu.HBM)
_SEM = pl.BlockSpec(memory_space=pltpu.SEMAPHORE)
_SIDE = pltpu.SideEffectType.DATAFLOW_SIDE_EFFECTING


def _in_hbm(a):
    return pltpu.with_memory_space_constraint(a, pltpu.HBM)


def _hbm_like(a):
    return pltpu.HBM(a.shape, a.dtype)


def _ici_copies(srcs, lands, send_sem, recv_sem, src_at, dst_at, wait_at, to_sibling=False):
    x, y, c, chips = _place()
    peers = [(x, y, 1 - c)] if to_sibling else [(cx, cy, c) for cx, cy in chips]
    m = len(peers)
    start, wait = [], []
    for i in range(len(srcs)):
        for k, (px, py, pc) in enumerate(peers):
            sem = dict(send_sem=send_sem.at[m * i + k], recv_sem=recv_sem.at[m * i + k],
                       device_id=(px, py, pc), device_id_type=_MESH)
            src = src_at(srcs[i], 2 * px + py, c)
            start.append(pltpu.make_async_remote_copy(src_ref=src, dst_ref=dst_at(lands[i], 2 * x + y, k, c), **sem))
            wait.append(pltpu.make_async_remote_copy(src_ref=src, dst_ref=wait_at(lands[i], 2 * px + py, k, c), **sem))
    return start, wait


def _route_peers(route):
    return 1 if len(route) == 4 else 3


_BLOCK_ROUTE = (lambda s, j, c: s, lambda l, me, k, c: l.at[me, c], lambda l, j, k, c: l.at[j, c])


def _ici_start(srcs, lands, token, route, *, name):
    n = len(srcs)

    def body(*refs):
        start, _ = _ici_copies(refs[:n], refs[n:2 * n], refs[2 * n + 1], refs[2 * n + 2], *route)
        for cp in start:
            cp.start()

    sems = pltpu.SemaphoreType.DMA((_route_peers(route) * n,))
    outs = pl.pallas_call(
        body, name=name,
        out_shape=(sems, sems, *[_hbm_like(a) for a in srcs], *[_hbm_like(a) for a in lands], _hbm_like(token)),
        in_specs=[_HBM] * (2 * n + 1), out_specs=(_SEM, _SEM, *[_HBM] * (2 * n + 1)),
        input_output_aliases={i: 2 + i for i in range(2 * n + 1)},
        compiler_params=pltpu.CompilerParams(has_side_effects=_SIDE),
    )(*[_in_hbm(a) for a in srcs], *[_in_hbm(a) for a in lands], _in_hbm(token))
    return (outs[0], outs[1], list(outs[2:2 + n]), list(outs[2 + n:2 + 2 * n])), outs[2 + 2 * n]


def _ici_wait(handle, after, route, *, name):
    send_sem, recv_sem, srcs, lands = handle
    n = len(srcs)
    after = list(after) if isinstance(after, (list, tuple)) else [after]

    def body(*refs):
        _, wait = _ici_copies(refs[:n], refs[n:2 * n], refs[2 * n], refs[2 * n + 1], *route)
        for cp in wait:
            cp.wait_send()
            cp.wait_recv()

    outs = pl.pallas_call(
        body, name=name,
        out_shape=(*[_hbm_like(a) for a in srcs], *[_hbm_like(a) for a in lands]),
        in_specs=[_HBM] * (2 * n) + [_SEM, _SEM] + [_ANY] * len(after), out_specs=tuple([_HBM] * (2 * n)),
        input_output_aliases={i: i for i in range(2 * n)},
        compiler_params=pltpu.CompilerParams(has_side_effects=_SIDE),
    )(*srcs, *lands, send_sem, recv_sem, *after)
    return list(outs[:n]), list(outs[n:])


_GATHER_ROUTE = (lambda s, j, c: s.at[c], lambda l, me, k, c: l.at[me, c], lambda l, j, k, c: l.at[j, c])
_SCATTER_ROUTE = (lambda s, j, c: s.at[j], lambda l, me, k, c: l.at[k], lambda l, j, k, c: l.at[k])
_SWAP_ROUTE = (lambda s, j, c: s.at[:, 1 - c], lambda l, me, k, c: l, lambda l, j, k, c: l, True)


def _gather_forward(lands, tag, own=False):
    n = len(lands)
    m = 4 if own else 3

    def body(*refs):
        ins, outs = refs[:n], refs[n:2 * n]
        send_sem, recv_sem = refs[2 * n:]
        x, y, c, chips = _place()
        slots = [2 * cx + cy for cx, cy in chips] + [2 * x + y]

        def copy(i, k, half):
            return pltpu.make_async_remote_copy(
                src_ref=ins[i].at[slots[k], half], dst_ref=outs[i].at[slots[k], half],
                send_sem=send_sem.at[m * i + k], recv_sem=recv_sem.at[m * i + k],
                device_id=(x, y, 1 - c), device_id_type=_MESH)

        copies = [copy(i, k, c) for i in range(n) for k in range(m)]
        for cp in copies:
            cp.start()
        for i in range(n):
            for k in range(m):
                copy(i, k, 1 - c).wait_recv()
        for cp in copies:
            cp.wait_send()

    return pl.pallas_call(
        body, name="gather_forward_to_sibling_" + tag,
        out_shape=[jax.ShapeDtypeStruct(a.shape, a.dtype) for a in lands],
        in_specs=[_ANY] * n, out_specs=[_ANY] * n,
        input_output_aliases={i: i for i in range(n)},
        scratch_shapes=[pltpu.SemaphoreType.DMA((m * n,)), pltpu.SemaphoreType.DMA((m * n,))],
    )(*lands)


def _swap_halves(grads, tag):
    n = len(grads)

    def body(*refs):
        ins, outs = refs[:n], refs[n:2 * n]
        send_sem, recv_sem = refs[2 * n:]
        x, y, c, _ = _place()
        copies = [pltpu.make_async_remote_copy(
            src_ref=ins[i].at[:, 1 - c], dst_ref=outs[i],
            send_sem=send_sem.at[i], recv_sem=recv_sem.at[i],
            device_id=(x, y, 1 - c), device_id_type=_MESH) for i in range(n)]
        for cp in copies:
            cp.start()
        for cp in copies:
            cp.wait()

    return pl.pallas_call(
        body, name="grad_swap_halves_" + tag,
        out_shape=[jax.ShapeDtypeStruct((N_CHIPS,) + g.shape[2:], g.dtype) for g in grads],
        in_specs=[_ANY] * n, out_specs=[_ANY] * n,
        scratch_shapes=[pltpu.SemaphoreType.DMA((n,)), pltpu.SemaphoreType.DMA((n,))],
    )(*grads)


def _sum_rows(h, C):
    return max(d for d in range(SUBLANES, h + 1, SUBLANES) if h % d == 0 and d * C <= 1 << 20)


def _pair_sum(g, r, c_idx, *, name):
    _, _, h, C = g.shape
    tr = _sum_rows(h, C)

    def body(c_ref, g_ref, r_ref, o_ref):
        o_ref[...] = (g_ref[...].astype(F32) + r_ref[...].astype(F32)).astype(o_ref.dtype)

    return pl.pallas_call(
        body, name=name, out_shape=jax.ShapeDtypeStruct((N_CHIPS, h, C), g.dtype),
        grid_spec=pltpu.PrefetchScalarGridSpec(
            num_scalar_prefetch=1, grid=(N_CHIPS, h // tr),
            in_specs=[pl.BlockSpec((None, None, tr, C), lambda j, i, s: (j, s[0], i, 0)),
                      pl.BlockSpec((None, tr, C), lambda j, i, s: (j, i, 0))],
            out_specs=pl.BlockSpec((None, tr, C), lambda j, i, s: (j, i, 0))),
        compiler_params=_params("parallel", "parallel"),
    )(c_idx, g, r)


def _owner_sum(s, r, jc_idx, *, name):
    _, h, C = s.shape
    tr = _sum_rows(h, C)

    def body(jc_ref, s_ref, r_ref, o_ref):
        acc = s_ref[...].astype(F32)
        for k in range(3):
            acc = acc + r_ref[k].astype(F32)
        o_ref[...] = acc

    return pl.pallas_call(
        body, name=name, out_shape=jax.ShapeDtypeStruct((2, h, C), F32),
        grid_spec=pltpu.PrefetchScalarGridSpec(
            num_scalar_prefetch=1, grid=(h // tr,),
            in_specs=[pl.BlockSpec((None, tr, C), lambda i, s: (s[0], i, 0)),
                      pl.BlockSpec((3, tr, C), lambda i, s: (0, i, 0))],
            out_specs=pl.BlockSpec((None, tr, C), lambda i, s: (s[1], i, 0))),
        compiler_params=_params("parallel"),
    )(jc_idx, s, r)


def _share_with_sibling(bufs, tag):
    n = len(bufs)

    def body(*refs):
        ins, outs = refs[:n], refs[n:2 * n]
        send_sem, recv_sem = refs[2 * n:]
        x, y, c, _ = _place()

        def copy(i, half):
            return pltpu.make_async_remote_copy(
                src_ref=ins[i].at[half], dst_ref=outs[i].at[half],
                send_sem=send_sem.at[i], recv_sem=recv_sem.at[i],
                device_id=(x, y, 1 - c), device_id_type=_MESH)

        copies = [copy(i, c) for i in range(n)]
        for cp in copies:
            cp.start()
        for i in range(n):
            copy(i, 1 - c).wait_recv()
        for cp in copies:
            cp.wait_send()

    return pl.pallas_call(
        body, name="grad_share_with_sibling_" + tag,
        out_shape=[jax.ShapeDtypeStruct(b.shape, b.dtype) for b in bufs],
        in_specs=[_ANY] * n, out_specs=[_ANY] * n,
        input_output_aliases={i: i for i in range(n)},
        scratch_shapes=[pltpu.SemaphoreType.DMA((n,)), pltpu.SemaphoreType.DMA((n,))],
    )(*bufs)


def _chip_sums(grads, c_idx, tag):
    views = [g.reshape(N_CHIPS, 2, g.shape[1] // 2, g.shape[2]) for g in grads]
    arrived = _swap_halves(views, tag)
    return [_pair_sum(v, r, c_idx, name=f"grad_pair_sum_{tag}_{i}") for i, (v, r) in enumerate(zip(views, arrived))]


def _owner_totals(sums, arrived, jc_idx, tag):
    halves = [_owner_sum(s, r, jc_idx, name=f"grad_owner_sum_{tag}_{i}") for i, (s, r) in enumerate(zip(sums, arrived))]
    return [f.reshape(-1, f.shape[2]) for f in _share_with_sibling(halves, tag)]


def _sum_devices(blocks):
    R = blocks.shape[2]
    tr = _sum_rows(R, 2 * N_CHIPS * LANES)

    def body(b_ref, o_ref):
        acc = b_ref[0, 0]
        for d in range(1, 2 * N_CHIPS):
            acc = acc + b_ref[d // 2, d % 2]
        o_ref[...] = acc

    return pl.pallas_call(
        body, name="sum_small_over_devices", out_shape=jax.ShapeDtypeStruct((R, LANES), F32),
        grid=(R // tr,),
        in_specs=[pl.BlockSpec((N_CHIPS, 2, tr, LANES), lambda i: (0, 0, i, 0))],
        out_specs=pl.BlockSpec((tr, LANES), lambda i: (i, 0)),
        compiler_params=_params("parallel"),
    )(blocks)


def _adamw(w, g, m, v, *, name):
    R, C = w.shape
    whole_fits = 7 * 2 * R * C * 4 <= VMEM_LIMIT_BYTES // 2
    tr = R if whole_fits else next(c for c in (256, 192, 128, 64, 32, 16, 8) if R % c == 0)

    def body(w_ref, g_ref, m_ref, v_ref, d_ref, nm_ref, nv_ref):
        g = g_ref[...]
        m = ADAM_B1 * m_ref[...] + (1.0 - ADAM_B1) * g
        v = ADAM_B2 * v_ref[...] + (1.0 - ADAM_B2) * (g * g)
        nm_ref[...] = m
        nv_ref[...] = v
        m_hat = m / (1.0 - ADAM_B1 ** ADAM_STEP)
        v_hat = v / (1.0 - ADAM_B2 ** ADAM_STEP)
        d_ref[...] = -ADAM_LR * (m_hat / (jnp.sqrt(v_hat) + ADAM_EPS) + ADAM_WD * w_ref[...])

    blk = pl.BlockSpec((tr, C), lambda i: (i, 0))
    sds = jax.ShapeDtypeStruct((R, C), F32)
    return pl.pallas_call(
        body, name=name, out_shape=(sds, sds, sds), grid=(R // tr,),
        in_specs=[blk] * 4, out_specs=(blk, blk, blk),
        compiler_params=_params("parallel"),
    )(w, g, m, v)


_TILE = SUBLANES * LANES


def _pack(arrays):
    rows = []
    for a in arrays:
        flat = a.reshape(-1)
        flat = jnp.pad(flat, (0, (-flat.shape[0]) % _TILE))
        rows.append(flat.reshape(-1, LANES))
    return jnp.concatenate(rows, axis=0)


def _unpack(buf, shapes):
    out, r = [], 0
    for s in shapes:
        size = math.prod(s)
        nr = -(-size // _TILE) * SUBLANES
        out.append(buf[r:r + nr].reshape(-1)[:size].reshape(s))
        r += nr
    return out


_BIG = ("w_in_a", "w_glu", "w_kv", "w_in_b", "w_mem_kv", "w_out")
_REPLICATED = ("pre_norm_g", "post_norm_g", "lam_re", "lam_im", "log_step", "b_re", "b_im", "c_re", "c_im",
               "kv_norm_g", "b_fgate", "mem_norm_g")
_SHARDED_SMALL = ("d_skip", "b_glu", "w_fgate")
_WEIGHTS = ("pre_norm_g", "post_norm_g", "w_in_a", "lam_re", "lam_im", "log_step", "b_re", "b_im", "c_re",
            "c_im", "d_skip", "w_glu", "b_glu", "kv_norm_g", "w_kv", "w_fgate", "b_fgate", "w_in_b",
            "mem_norm_g", "w_mem_kv", "w_out")


def _halves(a):
    return a.reshape(2, a.shape[0] // 2, a.shape[1])


def _unhalve(a):
    return a.reshape(N_CHIPS, 2 * a.shape[2], a.shape[3])


def _columns(a):
    return jnp.transpose(a, (1, 0, 2)).reshape(a.shape[1], N_CHIPS * a.shape[2])


def kernel(x, mem, pre_norm_g, post_norm_g, w_in_a, lam_re, lam_im, log_step, b_re, b_im, c_re, c_im, d_skip, w_glu, b_glu, kv_norm_g, w_kv, w_fgate, b_fgate, w_in_b, mem_norm_g, w_mem_kv, w_out, loss_target, m_pre_norm_g, m_post_norm_g, m_w_in_a, m_lam_re, m_lam_im, m_log_step, m_b_re, m_b_im, m_c_re, m_c_im, m_d_skip, m_w_glu, m_b_glu, m_kv_norm_g, m_w_kv, m_w_fgate, m_b_fgate, m_w_in_b, m_mem_norm_g, m_w_mem_kv, m_w_out, v_pre_norm_g, v_post_norm_g, v_w_in_a, v_lam_re, v_lam_im, v_log_step, v_b_re, v_b_im, v_c_re, v_c_im, v_d_skip, v_w_glu, v_b_glu, v_kv_norm_g, v_w_kv, v_w_fgate, v_b_fgate, v_w_in_b, v_mem_norm_g, v_w_mem_kv, v_w_out):
    a = dict(locals())
    xi, yi, ci = lax.axis_index("x"), lax.axis_index("y"), lax.axis_index("c")
    chip = 2 * xi + yi
    c_idx = jnp.reshape(ci, (1,)).astype(jnp.int32)
    jc_idx = jnp.stack([chip, ci]).astype(jnp.int32)

    vec = jnp.zeros((2 * SUBLANES, MAIN_WIDTH // N_CHIPS), F32)
    vec = vec.at[0].set(a["d_skip"][0]).at[1].set(a["b_glu"][0])
    def own_slot(gathered, parts):
        return [lax.dynamic_update_index_in_dim(g, p, chip, 0) for g, p in zip(gathered, parts)]

    parts_a = [_halves(a["w_in_a"][0].astype(BF16)), _halves(vec)]
    parts_b = [_halves(a["w_glu"][0].astype(BF16)), _halves(a["w_mem_kv"].reshape(-1, 2 * MEM_WIDTH).astype(BF16)),
               _halves(a["w_out"].reshape(-1, D_MODEL).astype(BF16))]
    parts_c = [_halves(a["w_kv"].astype(BF16)), _halves(_pad_lanes(a["w_fgate"]).astype(BF16)),
               _halves(a["w_in_b"][0].astype(BF16))]
    travelling, token = {}, a["pre_norm_g"]
    for tag, parts in (("a", parts_a), ("b", parts_b), ("c", parts_c)):
        lands = [lax.empty((N_CHIPS,) + p.shape, p.dtype) for p in parts]
        travelling[tag], token = _ici_start(parts, lands, token, _GATHER_ROUTE, name=f"gather_{tag}_start")

    def fetch(tag, after):
        parts, lands = _ici_wait(travelling[tag], after, _GATHER_ROUTE, name=f"gather_{tag}_wait")
        full = own_slot(_gather_forward(lands, tag), parts)
        if tag == "a":
            w_in_a, vecs = full
            return dict(w_in_a=_columns(_unhalve(w_in_a)), d_skip=vecs[:, 0, 0, :].reshape(MAIN_WIDTH),
                        b_glu=vecs[:, 0, 1, :].reshape(MAIN_WIDTH))
        if tag == "b":
            w_glu, w_mk, w_out = full
            return dict(w_glu=w_glu.reshape(MAIN_WIDTH, MAIN_WIDTH),
                        w_mem_kv=[w_mk[:, i].reshape(D_MODEL, 2 * MEM_WIDTH) for i in range(2)],
                        w_out=[w_out[:, i].reshape(D_MODEL, D_MODEL) for i in range(2)])
        w_kv, w_fg, w_in_b = full
        return dict(w_kv=_columns(_unhalve(w_kv)), w_fgate=w_fg.reshape(D_MODEL, LANES),
                    w_in_b=_columns(_unhalve(w_in_b)))

    w = dict(
        pre_norm_g=token, post_norm_g=a["post_norm_g"], mem_norm_g=a["mem_norm_g"],
        kv_norm_g=a["kv_norm_g"], b_fgate=a["b_fgate"],
        lam_re=a["lam_re"][0], lam_im=a["lam_im"][0], log_step=a["log_step"][0],
        b_re=a["b_re"][0], b_im=a["b_im"][0], c_re=a["c_re"][0], c_im=a["c_im"][0])

    sent = {}

    swapping = {}

    def grads_ready(event, g, token):
        tag = event.split("_")[0]
        if event in ("b", "a1"):
            big = {"b": lambda: [g["w_kv"], g["w_in_b"], g["w_mem_kv_1"].reshape(N_CHIPS, -1, 2 * MEM_WIDTH),
                                 g["w_out_1"].reshape(N_CHIPS, -1, D_MODEL)],
                   "a1": lambda: [g["w_glu"].reshape(N_CHIPS, -1, MAIN_WIDTH),
                                  g["w_mem_kv_0"].reshape(N_CHIPS, -1, 2 * MEM_WIDTH),
                                  g["w_out_0"].reshape(N_CHIPS, -1, D_MODEL)]}[tag]()
            views = [b.reshape(N_CHIPS, 2, b.shape[1] // 2, b.shape[2]) for b in big]
            lands = [lax.empty((N_CHIPS,) + v.shape[2:], v.dtype) for v in views]
            swapping[tag], token = _ici_start(views, lands, token, _SWAP_ROUTE, name=f"grad_swap_{tag}_start")
            return token
        if event == "a2":
            sums = _chip_sums([g["w_in_a"]], c_idx, tag)
        else:
            views, arrived = _ici_wait(swapping[tag], token, _SWAP_ROUTE, name=f"grad_swap_{tag}_wait")
            sums = [_pair_sum(v, r, c_idx, name=f"grad_pair_sum_{tag}_{i}")
                    for i, (v, r) in enumerate(zip(views, arrived))]
        lands = [lax.empty((3,) + s.shape[1:], s.dtype) for s in sums]
        sent[tag], token = _ici_start(sums, lands, token, _SCATTER_ROUTE, name=f"grad_send_{tag}_start")
        return token

    loss_row, grad_x, g = _local_step(a["x"][0], a["mem"][0], a["loss_target"][0], w, fetch, grads_ready)

    small_names = _REPLICATED + _SHARDED_SMALL
    pack = _pack([g[n] for n in small_names])
    blocks = lax.empty((N_CHIPS, 2) + pack.shape, F32)
    small_sent, loss_row = _ici_start([pack], [blocks], loss_row, _BLOCK_ROUTE, name="small_sums_start")
    loss = lax.psum(jnp.sum(loss_row), MESH_AXES)

    def totals(tag, after):
        sums, arrived = _ici_wait(sent[tag], after, _SCATTER_ROUTE, name=f"grad_send_{tag}_wait")
        return _owner_totals(sums, arrived, jc_idx, tag)

    r_kv, r_in_b, r_mk1, r_out1 = totals("b", grad_x)
    r_glu, r_mk0, r_out0 = totals("a1", r_out1)
    (r_in_a,) = totals("a2", r_out0)
    grads = {"w_in_a": r_in_a[None], "w_glu": r_glu[None], "w_kv": r_kv, "w_in_b": r_in_b[None],
             "w_mem_kv": jnp.stack([r_mk0, r_mk1]), "w_out": jnp.stack([r_out0, r_out1])}

    delta, new_m, new_v = {}, {}, {}
    for n in _BIG:
        shape = a[n].shape
        d2 = (-1, shape[-1])
        d, m, v = _adamw(a[n].reshape(d2), grads[n].reshape(d2), a["m_" + n].reshape(d2),
                         a["v_" + n].reshape(d2), name="adamw_" + n)
        delta[n], new_m[n], new_v[n] = d.reshape(shape), m.reshape(shape), v.reshape(shape)

    (pack,), (blocks,) = _ici_wait(small_sent, [delta[n] for n in _BIG], _BLOCK_ROUTE, name="small_sums_wait")
    blocks = lax.dynamic_update_slice(blocks, pack[None, None], (chip, ci, 0, 0))
    (blocks,) = _gather_forward([blocks], "small", own=True)
    small = dict(zip(small_names, _unpack(_sum_devices(blocks), [g[n].shape for n in small_names])))
    for n in _REPLICATED:
        grads[n] = small[n].reshape(a[n].shape)
    nd = MAIN_WIDTH // N_CHIPS
    grads["d_skip"] = lax.dynamic_slice(small["d_skip"], (chip * nd,), (nd,))[None]
    grads["b_glu"] = lax.dynamic_slice(small["b_glu"], (chip * nd,), (nd,))[None]
    nf = D_MODEL // N_CHIPS
    grads["w_fgate"] = lax.dynamic_slice(small["w_fgate"], (chip * nf, 0), (nf, FOX_HEADS))

    shapes = [a[n].shape for n in small_names]
    d, m, v = _adamw(_pack([a[n] for n in small_names]), _pack([grads[n] for n in small_names]),
                     _pack([a["m_" + n] for n in small_names]), _pack([a["v_" + n] for n in small_names]),
                     name="adamw_small")
    for n, dd, mm, vv in zip(small_names, _unpack(d, shapes), _unpack(m, shapes), _unpack(v, shapes)):
        delta[n], new_m[n], new_v[n] = dd, mm, vv

    return (loss, grad_x[None], *[grads[n] for n in _WEIGHTS], *[delta[n] for n in _WEIGHTS],
            *[new_m[n] for n in _WEIGHTS], *[new_v[n] for n in _WEIGHTS])
```

```python
import math

import jax
import jax.numpy as jnp
from jax import lax
from jax.experimental import pallas as pl
from jax.experimental.pallas import tpu as pltpu

F32 = jnp.float32
BF16 = jnp.bfloat16

D_MODEL = 2048
N_MEM = 256
MAIN_WIDTH = 1536
MEM_WIDTH = 512
IN_WIDTH = 2 * MAIN_WIDTH + 2 * MEM_WIDTH
HEAD_DIM = 128
FOX_HEADS = MAIN_WIDTH // HEAD_DIM
MEM_HEADS = MEM_WIDTH // HEAD_DIM
SSM_GROUP = 16
SSM_GROUPS = MAIN_WIDTH // SSM_GROUP
SSM_STATE = 64
GROUPS_PER_BLOCK = 8
SSM_BLOCKS = SSM_GROUPS // GROUPS_PER_BLOCK
STATE_COLS = GROUPS_PER_BLOCK * SSM_STATE
EPS = 1e-6
ADAM_LR = 0.001
ADAM_B1 = 0.9
ADAM_B2 = 0.999
ADAM_EPS = 1e-08
ADAM_WD = 0.01
ADAM_STEP = 10
N_CHIPS = 4
LANES = 128
SUBLANES = 8
VMEM_LIMIT_BYTES = 56 * 1024 * 1024
NEG_BIG = -1e30
MESH_AXES = ("x", "y", "c")


def _params(*sem):
    return pltpu.CompilerParams(dimension_semantics=sem if sem else None,
                                vmem_limit_bytes=VMEM_LIMIT_BYTES)


def _sigmoid(x):
    return 1.0 / (1.0 + jnp.exp(-x))


def _gelu(x):
    c = math.sqrt(2.0 / math.pi)
    return 0.5 * x * (1.0 + jnp.tanh(c * (x + 0.044715 * (x * x * x))))


def _gelu_grad(x):
    c = math.sqrt(2.0 / math.pi)
    t = jnp.tanh(c * (x + 0.044715 * (x * x * x)))
    return 0.5 * (1.0 + t) + 0.5 * x * (1.0 - t * t) * (c * (1.0 + 3.0 * 0.044715 * (x * x)))


def _silu_and_grad(z):
    s = _sigmoid(z)
    return z * s, s * (1.0 + z * (1.0 - s))


_TILE_CHOICES = (2048, 1024, 768, 512, 384, 256, LANES)


def _tile(n, cap):
    return next(c for c in _TILE_CHOICES if c <= cap and n % c == 0)


def _mm(a, b, *, name, ta=False, tb=False, out_dtype=F32, shards=1, tm=1024, tn=1024, tk=2048):
    if ta:
        K, M = a.shape
    else:
        M, K = a.shape
    if tb:
        N, kb = b.shape
    else:
        kb, N = b.shape
    assert K == kb, (a.shape, b.shape)
    ns = N // shards
    tm, tn, tk = _tile(M, tm), _tile(ns, tn), _tile(K, tk)
    assert M % tm == 0 and ns % tn == 0 and K % tk == 0 and N % shards == 0
    nk = K // tk
    dn = (((0 if ta else 1,), (1 if tb else 0,)), ((), ()))

    def body(a_ref, b_ref, o_ref, acc_ref):
        k = pl.program_id(2)

        @pl.when(k == 0)
        def _():
            acc_ref[...] = jnp.zeros_like(acc_ref)

        acc_ref[...] += lax.dot_general(a_ref[...].astype(BF16), b_ref[...].astype(BF16), dn,
                                        preferred_element_type=F32)

        @pl.when(k == nk - 1)
        def _():
            o_ref[...] = acc_ref[...].astype(o_ref.dtype)

    a_spec = (pl.BlockSpec((tk, tm), lambda i, j, k: (k, i)) if ta
              else pl.BlockSpec((tm, tk), lambda i, j, k: (i, k)))
    b_spec = (pl.BlockSpec((tn, tk), lambda i, j, k: (j, k)) if tb
              else pl.BlockSpec((tk, tn), lambda i, j, k: (k, j)))
    if shards == 1:
        out_shape = jax.ShapeDtypeStruct((M, N), out_dtype)
        o_spec = pl.BlockSpec((tm, tn), lambda i, j, k: (i, j))
    else:
        nb = ns // tn
        out_shape = jax.ShapeDtypeStruct((shards, M, ns), out_dtype)
        o_spec = pl.BlockSpec((None, tm, tn), lambda i, j, k: (j // nb, i, j % nb))
    return pl.pallas_call(
        body, name=name, out_shape=out_shape,
        grid=(M // tm, N // tn, nk),
        in_specs=[a_spec, b_spec], out_specs=o_spec,
        scratch_shapes=[pltpu.VMEM((tm, tn), F32)],
        compiler_params=_params("parallel", "parallel", "arbitrary"),
    )(a, b)


def _rmsnorm_fwd(x, g, *, name, res=None, out_dtype=F32, tr=256):
    L, D = x.shape
    tr = min(tr, L)
    has_res = res is not None

    def body(*refs):
        if has_res:
            x_ref, g_ref, r_ref, o_ref = refs
        else:
            x_ref, g_ref, o_ref = refs
        xf = x_ref[...]
        r = lax.rsqrt(jnp.mean(xf * xf, axis=-1, keepdims=True) + EPS)
        y = xf * r * g_ref[...]
        if has_res:
            y = r_ref[...] + y
        o_ref[...] = y.astype(o_ref.dtype)

    row = pl.BlockSpec((tr, D), lambda i: (i, 0))
    vec = pl.BlockSpec((1, D), lambda i: (0, 0))
    ins = [x, g.reshape(1, D)] + ([res] if has_res else [])
    return pl.pallas_call(
        body, name=name, out_shape=jax.ShapeDtypeStruct((L, D), out_dtype),
        grid=(L // tr,), in_specs=[row, vec] + ([row] if has_res else []), out_specs=row,
        compiler_params=_params("parallel"),
    )(*ins)


def _rmsnorm_bwd(x, g, dy, *, name, adds=(), dx_dtype=F32, tr=256):
    L, D = x.shape
    tr = min(tr, L)
    dys = dy if isinstance(dy, tuple) else (dy,)
    n_dy, n_add = len(dys), len(adds)

    def body(*refs):
        x_ref, g_ref = refs[:2]
        dy_refs = refs[2:2 + n_dy]
        add_refs = refs[2 + n_dy:2 + n_dy + n_add]
        dx_ref, dg_ref = refs[2 + n_dy + n_add:]
        xf = x_ref[...]
        dyf = dy_refs[0][...].astype(F32)
        for d_ref in dy_refs[1:]:
            dyf = dyf + d_ref[...].astype(F32)
        r = lax.rsqrt(jnp.mean(xf * xf, axis=-1, keepdims=True) + EPS)
        gy = dyf * g_ref[...]
        c = jnp.mean(xf * gy, axis=-1, keepdims=True) * (r * r * r)
        dx = gy * r - xf * c
        for a_ref in add_refs:
            dx = dx + a_ref[...].astype(F32)
        dx_ref[...] = dx.astype(dx_ref.dtype)

        @pl.when(pl.program_id(0) == 0)
        def _():
            dg_ref[...] = jnp.zeros_like(dg_ref)

        dg_ref[...] += jnp.sum(dyf * xf * r, axis=0, keepdims=True)

    row = pl.BlockSpec((tr, D), lambda i: (i, 0))
    vec = pl.BlockSpec((1, D), lambda i: (0, 0))
    dx, dg = pl.pallas_call(
        body, name=name,
        out_shape=(jax.ShapeDtypeStruct((L, D), dx_dtype), jax.ShapeDtypeStruct((1, D), F32)),
        grid=(L // tr,), in_specs=[row, vec] + [row] * (n_dy + n_add), out_specs=(row, vec),
        compiler_params=_params("arbitrary"),
    )(x, g.reshape(1, D), *dys, *adds)
    return dx, dg.reshape(D)


def _rmsnorm_bwd_pair(x, g1, dy1, g2, dy2, *, name, adds=(), tr=256):
    L, D = x.shape
    tr = min(tr, L)
    dy1s = dy1 if isinstance(dy1, tuple) else (dy1,)
    n1, n_add = len(dy1s), len(adds)

    def body(*refs):
        x_ref, g1_ref, g2_ref = refs[:3]
        dy1_refs = refs[3:3 + n1]
        dy2_ref = refs[3 + n1]
        add_refs = refs[4 + n1:4 + n1 + n_add]
        dx_ref, dg1_ref, dg2_ref = refs[4 + n1 + n_add:]
        xf = x_ref[...]
        d1 = dy1_refs[0][...].astype(F32)
        for d_ref in dy1_refs[1:]:
            d1 = d1 + d_ref[...].astype(F32)
        d2 = dy2_ref[...].astype(F32)
        r = lax.rsqrt(jnp.mean(xf * xf, axis=-1, keepdims=True) + EPS)
        gy = d1 * g1_ref[...] + d2 * g2_ref[...]
        c = jnp.mean(xf * gy, axis=-1, keepdims=True) * (r * r * r)
        dx = gy * r - xf * c
        for a_ref in add_refs:
            dx = dx + a_ref[...].astype(F32)
        dx_ref[...] = dx

        @pl.when(pl.program_id(0) == 0)
        def _():
            dg1_ref[...] = jnp.zeros_like(dg1_ref)
            dg2_ref[...] = jnp.zeros_like(dg2_ref)

        xr = xf * r
        dg1_ref[...] += jnp.sum(d1 * xr, axis=0, keepdims=True)
        dg2_ref[...] += jnp.sum(d2 * xr, axis=0, keepdims=True)

    row = pl.BlockSpec((tr, D), lambda i: (i, 0))
    vec = pl.BlockSpec((1, D), lambda i: (0, 0))
    dx, dg1, dg2 = pl.pallas_call(
        body, name=name,
        out_shape=(jax.ShapeDtypeStruct((L, D), F32), jax.ShapeDtypeStruct((1, D), F32),
                   jax.ShapeDtypeStruct((1, D), F32)),
        grid=(L // tr,), in_specs=[row, vec, vec] + [row] * (n1 + 1 + n_add), out_specs=(row, vec, vec),
        compiler_params=_params("arbitrary"),
    )(x, g1.reshape(1, D), g2.reshape(1, D), *dy1s, dy2, *adds)
    return dx, dg1.reshape(D), dg2.reshape(D)


def _final_norm_loss(o, g, res, target, *, tr=256):
    L, D = o.shape
    tr = min(tr, L)

    def body(o_ref, g_ref, r_ref, t_ref, dh_ref, loss_ref):
        xf = o_ref[...]
        r = lax.rsqrt(jnp.mean(xf * xf, axis=-1, keepdims=True) + EPS)
        e = (r_ref[...] + xf * r * g_ref[...]) - t_ref[...]
        dh_ref[...] = e * (1.0 / D)

        @pl.when(pl.program_id(0) == 0)
        def _():
            loss_ref[...] = jnp.zeros_like(loss_ref)

        loss_ref[...] += jnp.sum(e * e, axis=0, keepdims=True) * (0.5 / D)

    row = pl.BlockSpec((tr, D), lambda i: (i, 0))
    vec = pl.BlockSpec((1, D), lambda i: (0, 0))
    dh, lp = pl.pallas_call(
        body, name="post_norm_1_loss",
        out_shape=(jax.ShapeDtypeStruct((L, D), F32), jax.ShapeDtypeStruct((1, D), F32)),
        grid=(L // tr,), in_specs=[row, vec, row, row], out_specs=(row, vec),
        compiler_params=_params("arbitrary"),
    )(o, g.reshape(1, D), res, target)
    return dh, lp


def _s5_coeffs(lr, li, ls):
    dt = jnp.exp(ls)
    mag = jnp.exp(lr * dt)
    ar = mag * jnp.cos(li * dt)
    ai = mag * jnp.sin(li * dt)
    den = lr * lr + li * li
    cr = ((ar - 1.0) * lr + ai * li) / den
    ci = (ai * lr - (ar - 1.0) * li) / den
    return dt, ar, ai, den, cr, ci


def _s5_prep(lam_re, lam_im, log_step, b_re_t, b_im_t):
    G, P = lam_re.shape
    H = b_re_t.shape[1]

    def body(lr_ref, li_ref, ls_ref, br_ref, bi_ref, ar_ref, ai_ref, bbr_ref, bbi_ref):
        _, ar, ai, _, cr, ci = _s5_coeffs(lr_ref[...], li_ref[...], ls_ref[...])
        ar_ref[...] = ar
        ai_ref[...] = ai
        br, bi = br_ref[...], bi_ref[...]
        crb, cib = cr[:, None, :], ci[:, None, :]
        bbr_ref[...] = crb * br - cib * bi
        bbi_ref[...] = crb * bi + cib * br

    return pl.pallas_call(
        body, name="s5_prep",
        out_shape=(jax.ShapeDtypeStruct((G, P), F32), jax.ShapeDtypeStruct((G, P), F32),
                   jax.ShapeDtypeStruct((G, H, P), F32), jax.ShapeDtypeStruct((G, H, P), F32)),
        compiler_params=_params(),
    )(lam_re, lam_im, log_step.reshape(G, 1), b_re_t, b_im_t)


def _s5_prep_bwd(lam_re, lam_im, log_step, b_re_t, b_im_t, d_ar, d_ai, d_bbr, d_bbi):
    G, P = lam_re.shape
    H = b_re_t.shape[1]

    def body(lr_ref, li_ref, ls_ref, br_ref, bi_ref, dar_ref, dai_ref, dbbr_ref, dbbi_ref,
             dlr_ref, dli_ref, dls_ref, dbr_ref, dbi_ref):
        lr, li = lr_ref[...], li_ref[...]
        dt, ar, ai, den, cr, ci = _s5_coeffs(lr, li, ls_ref[...])
        br, bi = br_ref[...], bi_ref[...]
        gbr, gbi = dbbr_ref[...], dbbi_ref[...]
        crb, cib = cr[:, None, :], ci[:, None, :]
        dbr_ref[...] = crb * gbr + cib * gbi
        dbi_ref[...] = crb * gbi - cib * gbr
        gcr = jnp.sum(br * gbr + bi * gbi, axis=1)
        gci = jnp.sum(br * gbi - bi * gbr, axis=1)
        ilr, ili = lr / den, -li / den
        gar = dar_ref[...] + (ilr * gcr + ili * gci)
        gai = dai_ref[...] + (ilr * gci - ili * gcr)
        qr, qi = cr * ilr - ci * ili, cr * ili + ci * ilr
        glr = -(qr * gcr + qi * gci)
        gli = -(qr * gci - qi * gcr)
        glr = glr + dt * (ar * gar + ai * gai)
        gli = gli + dt * (ar * gai - ai * gar)
        wr, wi = lr * ar - li * ai, lr * ai + li * ar
        gdt = jnp.sum(wr * gar + wi * gai, axis=1, keepdims=True)
        dlr_ref[...] = glr
        dli_ref[...] = gli
        dls_ref[...] = gdt * dt

    return pl.pallas_call(
        body, name="s5_prep_bwd",
        out_shape=(jax.ShapeDtypeStruct((G, P), F32), jax.ShapeDtypeStruct((G, P), F32),
                   jax.ShapeDtypeStruct((G, 1), F32),
                   jax.ShapeDtypeStruct((G, H, P), F32), jax.ShapeDtypeStruct((G, H, P), F32)),
        compiler_params=_params(),
    )(lam_re, lam_im, log_step.reshape(G, 1), b_re_t, b_im_t, d_ar, d_ai, d_bbr, d_bbi)


def _s5_block_mats(bbr_t, bbi_t, c_re, c_im):
    bmat = _s5_expand(bbr_t, bbi_t)
    cmat = jnp.transpose(_s5_expand(c_re, -c_im), (0, 2, 1))
    return bmat.astype(BF16), cmat.astype(BF16)


def _s5_diag_mask():
    r = lax.broadcasted_iota(jnp.int32, (LANES, 2 * STATE_COLS), 0) // SSM_GROUP
    c = (lax.broadcasted_iota(jnp.int32, (LANES, 2 * STATE_COLS), 1) % STATE_COLS) // SSM_STATE
    return (r == c).astype(F32)


def _s5_expand(re, im):
    re = jnp.tile(re.reshape(SSM_BLOCKS, LANES, SSM_STATE), (1, 1, GROUPS_PER_BLOCK))
    im = jnp.tile(im.reshape(SSM_BLOCKS, LANES, SSM_STATE), (1, 1, GROUPS_PER_BLOCK))
    return jnp.concatenate([re, im], axis=-1) * _s5_diag_mask()[None]


def _s5_block_diag(dmat):
    d = dmat * _s5_diag_mask()[None]
    parts = []
    for ri in range(2):
        acc = 0.0
        for g in range(GROUPS_PER_BLOCK):
            c0 = ri * STATE_COLS + g * SSM_STATE
            acc = acc + d[:, :, c0:c0 + SSM_STATE]
        parts.append(acc.reshape(SSM_GROUPS, SSM_GROUP, SSM_STATE))
    return jnp.stack(parts)


def _s5_a_rows(ar, ai):
    a = jnp.concatenate([ar.reshape(SSM_BLOCKS, STATE_COLS), ai.reshape(SSM_BLOCKS, STATE_COLS)], axis=1)
    return jnp.broadcast_to(a[:, None, :], (SSM_BLOCKS, SUBLANES, 2 * STATE_COLS))


def _to_step_major(src_ref, dst_ref, seg):
    for s in range(SUBLANES):
        dst_ref[pl.ds(s, seg, stride=SUBLANES), :] = src_ref[pl.ds(seg * s, seg), :]


def _segment_rows(ref, s, seg):
    return ref[pl.ds(s, seg, stride=SUBLANES), :]


def _cmul(ar, ai, xr, xi):
    return ar * xr - ai * xi, ar * xi + ai * xr


def _s5_tables(a_ref, pw_s, pwr_s, S, seg):
    ar, ai = a_ref[:, :S], a_ref[:, S:]

    def step(i, c):
        pr, pi = c
        pw_s[i, :, :S] = pr
        pw_s[i, :, S:] = pi
        nr, ni = _cmul(ar, ai, pr, pi)
        pwr_s[seg - 1 - i, :, :S] = nr
        pwr_s[seg - 1 - i, :, S:] = ni
        return nr, ni

    pr, pi = lax.fori_loop(0, seg, step, (jnp.ones_like(ar), jnp.zeros_like(ai)))
    pw_s[seg, :, :S] = pr
    pw_s[seg, :, S:] = pi


def _s5_fwd(proj, bmat, cmat, a_rows, d_skip, *, tc=512):
    L = proj.shape[0]
    tc = min(tc, L)
    nt = L // tc
    seg = tc // SUBLANES
    S = STATE_COLS

    def body(u_ref, b_ref, c_ref, a_ref, d_ref, y_ref, yg_ref, xp_ref,
             bu_s, xp_s, pw_s, pwr_s, carry_s, e_s, up_s, yc_s):
        @pl.when(pl.program_id(1) == 0)
        def _():
            carry_s[...] = jnp.zeros_like(carry_s)
            _s5_tables(a_ref, pw_s, pwr_s, S, seg)

        ar, ai = a_ref[:, :S], a_ref[:, S:]
        _to_step_major(u_ref, up_s, seg)
        bu = jnp.dot(up_s[...].astype(BF16), b_ref[...], preferred_element_type=F32)
        bu_s[...] = bu.reshape(seg, SUBLANES, 2 * S)

        def step(i, carry):
            cr, ci = carry
            xp_s[i, :, :S] = cr
            xp_s[i, :, S:] = ci
            return ar * cr - ai * ci + bu_s[i, :, :S], ar * ci + ai * cr + bu_s[i, :, S:]

        zero = jnp.zeros((SUBLANES, S), F32)
        fr, fi = lax.fori_loop(0, seg, step, (zero, zero))
        pr, pi = pw_s[seg, 0:1, :S], pw_s[seg, 0:1, S:]
        er, ei = carry_s[0:1, :S], carry_s[0:1, S:]
        for s in range(SUBLANES):
            e_s[s:s + 1, :S] = er
            e_s[s:s + 1, S:] = ei
            tr, ti = _cmul(pr, pi, er, ei)
            er, ei = fr[s:s + 1] + tr, fi[s:s + 1] + ti
        carry_s[0:1, :S] = er
        carry_s[0:1, S:] = ei
        pw = pw_s[0:seg]
        tr, ti = _cmul(pw[:, :, :S], pw[:, :, S:], e_s[:, :S][None], e_s[:, S:][None])
        xl = xp_s[...]
        xp = jnp.concatenate([xl[:, :, :S] + tr, xl[:, :, S:] + ti], axis=-1).reshape(tc, 2 * S)
        xp_ref[...] = xp
        a1r, a1i = ar[0:1], ai[0:1]
        x_re = a1r * xp[:, :S] - a1i * xp[:, S:] + bu[:, :S]
        x_im = a1r * xp[:, S:] + a1i * xp[:, :S] + bu[:, S:]
        xs = jnp.concatenate([x_re, x_im], axis=1).astype(BF16)
        yc_s[...] = jnp.dot(xs, c_ref[...], preferred_element_type=F32)
        for s in range(SUBLANES):
            rows = pl.ds(seg * s, seg)
            y = _segment_rows(yc_s, s, seg) + d_ref[...] * u_ref[rows, :]
            y_ref[rows, :] = y
            yg_ref[rows, :] = _gelu(y).astype(BF16)

    return pl.pallas_call(
        body, name="s5_fwd",
        out_shape=(jax.ShapeDtypeStruct((L, MAIN_WIDTH), F32),
                   jax.ShapeDtypeStruct((L, MAIN_WIDTH), BF16),
                   jax.ShapeDtypeStruct((L, SSM_BLOCKS * 2 * S), F32)),
        grid=(SSM_BLOCKS, nt),
        in_specs=[pl.BlockSpec((tc, LANES), lambda b, t: (t, b)),
                  pl.BlockSpec((None, LANES, 2 * S), lambda b, t: (b, 0, 0)),
                  pl.BlockSpec((None, 2 * S, LANES), lambda b, t: (b, 0, 0)),
                  pl.BlockSpec((None, SUBLANES, 2 * S), lambda b, t: (b, 0, 0)),
                  pl.BlockSpec((1, LANES), lambda b, t: (0, b))],
        out_specs=(pl.BlockSpec((tc, LANES), lambda b, t: (t, b)),
                   pl.BlockSpec((tc, LANES), lambda b, t: (t, b)),
                   pl.BlockSpec((tc, 2 * S), lambda b, t: (t, b))),
        scratch_shapes=[pltpu.VMEM((seg, SUBLANES, 2 * S), F32),
                        pltpu.VMEM((seg, SUBLANES, 2 * S), F32),
                        pltpu.VMEM((seg + 1, SUBLANES, 2 * S), F32),
                        pltpu.VMEM((seg, SUBLANES, 2 * S), F32),
                        pltpu.VMEM((SUBLANES, 2 * S), F32),
                        pltpu.VMEM((SUBLANES, 2 * S), F32),
                        pltpu.VMEM((tc, LANES), F32),
                        pltpu.VMEM((tc, LANES), F32)],
        compiler_params=_params("parallel", "arbitrary"),
    )(proj, bmat, cmat, a_rows, d_skip.reshape(1, MAIN_WIDTH))


def _s5_bwd(proj, dyg_a, dyg_b, y, xp, bmat, cmat, a_rows, d_skip, dproj, *, tc=512):
    L = proj.shape[0]
    tc = min(tc, L)
    nt = L // tc
    seg = tc // SUBLANES
    S = STATE_COLS
    nn = (((1,), (1,)), ((), ()))
    tn = (((0,), (0,)), ((), ()))

    def body(u_ref, dyga_ref, dygb_ref, y_ref, xp_ref, b_ref, c_ref, a_ref, d_ref, dp_hbm,
             du_ref, db_ref, dc_ref, da_ref, dd_ref, dl_s, pw_s, pwr_s, carry_s, e_s, up_s, dy_s, dyp_s, dup_s):
        @pl.when(pl.program_id(1) == 0)
        def _():
            carry_s[...] = jnp.zeros_like(carry_s)
            db_ref[...] = jnp.zeros_like(db_ref)
            dc_ref[...] = jnp.zeros_like(dc_ref)
            da_ref[...] = jnp.zeros_like(da_ref)
            dd_ref[...] = jnp.zeros_like(dd_ref)
            _s5_tables(a_ref, pw_s, pwr_s, S, seg)

        ar, ai = a_ref[:, :S], a_ref[:, S:]
        a1r, a1i = ar[0:1], ai[0:1]
        u = u_ref[...]
        dy = (dyga_ref[...] + dygb_ref[...]) * _gelu_grad(y_ref[...])
        dy_s[...] = dy
        xp = xp_ref[...]
        _to_step_major(u_ref, up_s, seg)
        _to_step_major(dy_s, dyp_s, seg)
        ubp = up_s[...].astype(BF16)
        dyp = dyp_s[...].astype(BF16)
        bu = jnp.dot(ubp, b_ref[...], preferred_element_type=F32)
        x_re = a1r * xp[:, :S] - a1i * xp[:, S:] + bu[:, :S]
        x_im = a1r * xp[:, S:] + a1i * xp[:, :S] + bu[:, S:]
        xs = jnp.concatenate([x_re, x_im], axis=1).astype(BF16)
        dc_ref[...] += lax.dot_general(dyp, xs, tn, preferred_element_type=F32)
        dx = lax.dot_general(dyp, c_ref[...], nn, preferred_element_type=F32)
        dl_s[...] = dx.reshape(seg, SUBLANES, 2 * S)

        def step(k, carry):
            cr, ci = carry
            i = seg - 1 - k
            lr = dl_s[i, :, :S] + (ar * cr + ai * ci)
            li = dl_s[i, :, S:] + (ar * ci - ai * cr)
            dl_s[i, :, :S] = lr
            dl_s[i, :, S:] = li
            return lr, li

        zero = jnp.zeros((SUBLANES, S), F32)
        fr, fi = lax.fori_loop(0, seg, step, (zero, zero))
        pr, pi = pw_s[seg, 0:1, :S], pw_s[seg, 0:1, S:]
        er, ei = carry_s[0:1, :S], carry_s[0:1, S:]
        for s in range(SUBLANES - 1, -1, -1):
            e_s[s:s + 1, :S] = er
            e_s[s:s + 1, S:] = ei
            er, ei = fr[s:s + 1] + (pr * er + pi * ei), fi[s:s + 1] + (pr * ei - pi * er)
        carry_s[0:1, :S] = er
        carry_s[0:1, S:] = ei
        er, ei = e_s[:, :S][None], e_s[:, S:][None]
        pw = pwr_s[...]
        pwr, pwi = pw[:, :, :S], pw[:, :, S:]
        ll = dl_s[...]
        lam = jnp.concatenate([ll[:, :, :S] + (pwr * er + pwi * ei), ll[:, :, S:] + (pwr * ei - pwi * er)],
                              axis=-1).reshape(tc, 2 * S)
        l_re, l_im = lam[:, :S], lam[:, S:]
        da_ref[0:1, :S] += jnp.sum(l_re * xp[:, :S] + l_im * xp[:, S:], axis=0, keepdims=True)
        da_ref[0:1, S:] += jnp.sum(l_im * xp[:, :S] - l_re * xp[:, S:], axis=0, keepdims=True)
        lamb = lam.astype(BF16)
        dup_s[...] = lax.dot_general(lamb, b_ref[...], nn, preferred_element_type=F32)
        for s in range(SUBLANES):
            rows = pl.ds(seg * s, seg)
            du = _segment_rows(dup_s, s, seg) + d_ref[...] * dy_s[rows, :]
            du_ref[rows, :] = du.astype(du_ref.dtype)
        db_ref[...] += lax.dot_general(ubp, lamb, tn, preferred_element_type=F32)
        dd_ref[0:1, :] += jnp.sum(dy * u, axis=0, keepdims=True)

    rev = lambda b, t: (nt - 1 - t, b)
    return pl.pallas_call(
        body, name="s5_bwd",
        out_shape=(jax.ShapeDtypeStruct(dproj.shape, dproj.dtype),
                   jax.ShapeDtypeStruct((SSM_BLOCKS, LANES, 2 * S), F32),
                   jax.ShapeDtypeStruct((SSM_BLOCKS, LANES, 2 * S), F32),
                   jax.ShapeDtypeStruct((SSM_BLOCKS, SUBLANES, 2 * S), F32),
                   jax.ShapeDtypeStruct((SUBLANES, MAIN_WIDTH), F32)),
        input_output_aliases={9: 0},
        grid=(SSM_BLOCKS, nt),
        in_specs=[pl.BlockSpec((tc, LANES), rev),
                  pl.BlockSpec((tc, LANES), rev),
                  pl.BlockSpec((tc, LANES), rev),
                  pl.BlockSpec((tc, LANES), rev),
                  pl.BlockSpec((tc, 2 * S), rev),
                  pl.BlockSpec((None, LANES, 2 * S), lambda b, t: (b, 0, 0)),
                  pl.BlockSpec((None, 2 * S, LANES), lambda b, t: (b, 0, 0)),
                  pl.BlockSpec((None, SUBLANES, 2 * S), lambda b, t: (b, 0, 0)),
                  pl.BlockSpec((1, LANES), lambda b, t: (0, b)),
                  _ANY],
        out_specs=(pl.BlockSpec((tc, LANES), rev),
                   pl.BlockSpec((None, LANES, 2 * S), lambda b, t: (b, 0, 0)),
                   pl.BlockSpec((None, LANES, 2 * S), lambda b, t: (b, 0, 0)),
                   pl.BlockSpec((None, SUBLANES, 2 * S), lambda b, t: (b, 0, 0)),
                   pl.BlockSpec((SUBLANES, LANES), lambda b, t: (0, b))),
        scratch_shapes=[pltpu.VMEM((seg, SUBLANES, 2 * S), F32),
                        pltpu.VMEM((seg + 1, SUBLANES, 2 * S), F32),
                        pltpu.VMEM((seg, SUBLANES, 2 * S), F32),
                        pltpu.VMEM((SUBLANES, 2 * S), F32),
                        pltpu.VMEM((SUBLANES, 2 * S), F32),
                        pltpu.VMEM((tc, LANES), F32),
                        pltpu.VMEM((tc, LANES), F32),
                        pltpu.VMEM((tc, LANES), F32),
                        pltpu.VMEM((tc, LANES), F32)],
        compiler_params=_params("parallel", "arbitrary"),
    )(proj, dyg_a, dyg_b, y, xp, bmat, cmat, a_rows, d_skip.reshape(1, MAIN_WIDTH), dproj)


_Z_COLS = slice(MAIN_WIDTH, 2 * MAIN_WIDTH)
_ZM_COLS = slice(2 * MAIN_WIDTH + MEM_WIDTH, IN_WIDTH)


def _proj_rows(tr):
    return pl.BlockSpec((tr, IN_WIDTH), lambda i: (i, 0))


def _row_specs(tr):
    main = pl.BlockSpec((tr, MAIN_WIDTH), lambda i: (i, 0))
    z = pl.BlockSpec((tr, MAIN_WIDTH), lambda i: (i, 1))
    zm = pl.BlockSpec((tr, MEM_WIDTH), lambda i: (i, IN_WIDTH // MEM_WIDTH - 1))
    mem = pl.BlockSpec((tr, MEM_WIDTH), lambda i: (i, 0))
    cat = pl.BlockSpec((tr, D_MODEL), lambda i: (i, 0))
    vec = pl.BlockSpec((1, MAIN_WIDTH), lambda i: (0, 0))
    return main, z, zm, mem, cat, vec


def _gate_a_fwd(y, t, b_glu, proj, o_mem, *, tr=256):
    L = y.shape[0]
    tr = min(tr, L)

    def body(y_ref, t_ref, b_ref, z_ref, zm_ref, om_ref, o_ref):
        yg = _gelu(y_ref[...])
        sz, _ = _silu_and_grad(z_ref[...])
        o_ref[:, :MAIN_WIDTH] = (yg * _sigmoid(t_ref[...] + b_ref[...]) * sz).astype(BF16)
        szm, _ = _silu_and_grad(zm_ref[...])
        o_ref[:, MAIN_WIDTH:] = (om_ref[...] * szm).astype(BF16)

    main, z, zm, mem, cat, vec = _row_specs(tr)
    return pl.pallas_call(
        body, name="gate_a_fwd", out_shape=jax.ShapeDtypeStruct((L, D_MODEL), BF16),
        grid=(L // tr,), in_specs=[main, main, vec, z, zm, mem], out_specs=cat,
        compiler_params=_params("parallel"),
    )(y, t, b_glu.reshape(1, MAIN_WIDTH), proj, proj, o_mem)


def _gate_a_bwd(dcat, y, t, b_glu, proj, o_mem, *, tr=256):
    L = y.shape[0]
    tr = min(tr, L)

    def body(dc_ref, y_ref, t_ref, b_ref, z_ref, zm_ref, om_ref,
             dp_ref, dt_ref, dyg_ref, dom_ref, db_ref):
        dmain = dc_ref[:, :MAIN_WIDTH]
        dmemo = dc_ref[:, MAIN_WIDTH:]
        yg = _gelu(y_ref[...])
        sg = _sigmoid(t_ref[...] + b_ref[...])
        sz, gz = _silu_and_grad(z_ref[...])
        dp_ref[:, _Z_COLS] = (dmain * (yg * sg) * gz).astype(BF16)
        dy2 = dmain * sz
        dyg_ref[...] = dy2 * sg
        dt = dy2 * yg * (sg * (1.0 - sg))
        dt_ref[...] = dt.astype(BF16)

        @pl.when(pl.program_id(0) == 0)
        def _():
            db_ref[...] = jnp.zeros_like(db_ref)

        db_ref[...] += jnp.sum(dt, axis=0, keepdims=True)
        szm, gzm = _silu_and_grad(zm_ref[...])
        dom_ref[...] = dmemo * szm
        dp_ref[:, _ZM_COLS] = (dmemo * om_ref[...] * gzm).astype(BF16)

    main, z, zm, mem, cat, vec = _row_specs(tr)
    outs = pl.pallas_call(
        body, name="gate_a_bwd",
        out_shape=(jax.ShapeDtypeStruct((L, IN_WIDTH), BF16),
                   jax.ShapeDtypeStruct((L, MAIN_WIDTH), BF16), jax.ShapeDtypeStruct((L, MAIN_WIDTH), F32),
                   jax.ShapeDtypeStruct((L, MEM_WIDTH), F32), jax.ShapeDtypeStruct((1, MAIN_WIDTH), F32)),
        grid=(L // tr,), in_specs=[cat, main, main, vec, z, zm, mem],
        out_specs=(_proj_rows(tr), main, main, mem, vec),
        compiler_params=_params("arbitrary"),
    )(dcat, y, t, b_glu.reshape(1, MAIN_WIDTH), proj, proj, o_mem)
    return outs


def _gate_b_fwd(att, proj, o_mem, *, tr=256):
    L = att.shape[0]
    tr = min(tr, L)

    def body(a_ref, z_ref, zm_ref, om_ref, o_ref):
        sz, _ = _silu_and_grad(z_ref[...])
        o_ref[:, :MAIN_WIDTH] = (a_ref[...] * sz).astype(BF16)
        szm, _ = _silu_and_grad(zm_ref[...])
        o_ref[:, MAIN_WIDTH:] = (om_ref[...] * szm).astype(BF16)

    main, z, zm, mem, cat, _ = _row_specs(tr)
    return pl.pallas_call(
        body, name="gate_b_fwd", out_shape=jax.ShapeDtypeStruct((L, D_MODEL), BF16),
        grid=(L // tr,), in_specs=[main, z, zm, mem], out_specs=cat,
        compiler_params=_params("parallel"),
    )(att, proj, proj, o_mem)


def _gate_b_bwd(dcat, att, proj, o_mem, *, tr=256):
    L = att.shape[0]
    tr = min(tr, L)

    def body(dc_ref, a_ref, z_ref, zm_ref, om_ref, da_ref, dp_ref, dom_ref, dl_ref):
        dmain = dc_ref[:, :MAIN_WIDTH]
        dmemo = dc_ref[:, MAIN_WIDTH:]
        att = a_ref[...]
        sz, gz = _silu_and_grad(z_ref[...])
        datt = dmain * sz
        da_ref[...] = datt
        dp_ref[:, _Z_COLS] = (dmain * att * gz).astype(BF16)
        szm, gzm = _silu_and_grad(zm_ref[...])
        dom_ref[...] = dmemo * szm
        dp_ref[:, _ZM_COLS] = (dmemo * om_ref[...] * gzm).astype(BF16)
        prod = datt * att
        for h in range(FOX_HEADS):
            dl_ref[h] = jnp.sum(prod[:, h * HEAD_DIM:(h + 1) * HEAD_DIM], axis=1, keepdims=True)

    main, z, zm, mem, cat, _ = _row_specs(tr)
    delta = pl.BlockSpec((FOX_HEADS, tr, 1), lambda i: (0, i, 0))
    return pl.pallas_call(
        body, name="gate_b_bwd",
        out_shape=(jax.ShapeDtypeStruct((L, MAIN_WIDTH), F32), jax.ShapeDtypeStruct((L, IN_WIDTH), BF16),
                   jax.ShapeDtypeStruct((L, MEM_WIDTH), F32), jax.ShapeDtypeStruct((FOX_HEADS, L, 1), F32)),
        grid=(L // tr,), in_specs=[cat, main, z, zm, mem], out_specs=(main, _proj_rows(tr), mem, delta),
        compiler_params=_params("parallel"),
    )(dcat, att, proj, proj, o_mem)


_MEM_Q_COL = (2 * MAIN_WIDTH) // HEAD_DIM
_NT = (((1,), (1,)), ((), ()))
_TN = (((0,), (0,)), ((), ()))


def _mem_probs(q_ref, k_ref):
    qs = (q_ref[...] * (HEAD_DIM ** -0.5)).astype(BF16)
    s = lax.dot_general(qs, k_ref[...].astype(BF16), _NT, preferred_element_type=F32)
    e = jnp.exp(s - jnp.max(s, axis=-1, keepdims=True))
    return qs, e / jnp.sum(e, axis=-1, keepdims=True)


def _mem_attn_fwd(proj, kvm, *, tq=2048):
    L = proj.shape[0]
    tq = min(tq, L)

    def body(q_ref, k_ref, v_ref, o_ref):
        _, p = _mem_probs(q_ref, k_ref)
        o_ref[...] = jnp.dot(p.astype(BF16), v_ref[...].astype(BF16), preferred_element_type=F32)

    return pl.pallas_call(
        body, name="mem_attn_fwd", out_shape=jax.ShapeDtypeStruct((L, MEM_WIDTH), F32),
        grid=(MEM_HEADS, L // tq),
        in_specs=[pl.BlockSpec((tq, HEAD_DIM), lambda h, i: (i, _MEM_Q_COL + h)),
                  pl.BlockSpec((N_MEM, HEAD_DIM), lambda h, i: (0, h)),
                  pl.BlockSpec((N_MEM, HEAD_DIM), lambda h, i: (0, MEM_HEADS + h))],
        out_specs=pl.BlockSpec((tq, HEAD_DIM), lambda h, i: (i, h)),
        compiler_params=_params("parallel", "parallel"),
    )(proj, kvm, kvm)


def _mem_attn_bwd(proj, kvm, do, dproj, *, tq=2048):
    L = proj.shape[0]
    tq = min(tq, L)

    def body(q_ref, k_ref, v_ref, do_ref, dp_hbm, dq_ref, dk_ref, dv_ref):
        @pl.when(pl.program_id(1) == 0)
        def _():
            dk_ref[...] = jnp.zeros_like(dk_ref)
            dv_ref[...] = jnp.zeros_like(dv_ref)

        qs, p = _mem_probs(q_ref, k_ref)
        dob = do_ref[...].astype(BF16)
        dp = lax.dot_general(dob, v_ref[...].astype(BF16), _NT, preferred_element_type=F32)
        ds = p * (dp - jnp.sum(p * dp, axis=-1, keepdims=True))
        dsb = ds.astype(BF16)
        dq = jnp.dot(dsb, k_ref[...].astype(BF16), preferred_element_type=F32) * (HEAD_DIM ** -0.5)
        dq_ref[...] = dq.astype(BF16)
        dk_ref[...] += lax.dot_general(dsb, qs, _TN, preferred_element_type=F32)
        dv_ref[...] += lax.dot_general(p.astype(BF16), dob, _TN, preferred_element_type=F32)

    dproj, dk, dv = pl.pallas_call(
        body, name="mem_attn_bwd",
        out_shape=(jax.ShapeDtypeStruct(dproj.shape, dproj.dtype),
                   jax.ShapeDtypeStruct((N_MEM, MEM_WIDTH), F32),
                   jax.ShapeDtypeStruct((N_MEM, MEM_WIDTH), F32)),
        grid=(MEM_HEADS, L // tq),
        in_specs=[pl.BlockSpec((tq, HEAD_DIM), lambda h, i: (i, _MEM_Q_COL + h)),
                  pl.BlockSpec((N_MEM, HEAD_DIM), lambda h, i: (0, h)),
                  pl.BlockSpec((N_MEM, HEAD_DIM), lambda h, i: (0, MEM_HEADS + h)),
                  pl.BlockSpec((tq, HEAD_DIM), lambda h, i: (i, h)),
                  _ANY],
        out_specs=(pl.BlockSpec((tq, HEAD_DIM), lambda h, i: (i, _MEM_Q_COL + h)),
                   pl.BlockSpec((N_MEM, HEAD_DIM), lambda h, i: (0, h)),
                   pl.BlockSpec((N_MEM, HEAD_DIM), lambda h, i: (0, h))),
        input_output_aliases={4: 0},
        compiler_params=_params("parallel", "arbitrary"),
    )(proj, kvm, kvm, do, dproj)
    return dproj, jnp.concatenate([dk, dv], axis=1)


def _tile_cumsum(x, row, reverse):
    for sh in (1, 2, 4):
        if reverse:
            x = x + jnp.where(row < SUBLANES - sh, pltpu.roll(x, SUBLANES - sh, 0), 0.0)
        else:
            x = x + jnp.where(row >= sh, pltpu.roll(x, sh, 0), 0.0)
    return x


def _fgate_fwd(pre, b_pad):
    L = pre.shape[0]
    n8 = L // SUBLANES

    def body(p_ref, b_ref, o_ref):
        row = lax.broadcasted_iota(jnp.int32, (SUBLANES, LANES), 0)
        b = b_ref[...]

        def step(i, carry):
            x = p_ref[i] + b
            logf = jnp.minimum(x, 0.0) - jnp.log(1.0 + jnp.exp(-jnp.abs(x)))
            t = _tile_cumsum(logf, row, False) + carry
            o_ref[i] = t
            return t[SUBLANES - 1:SUBLANES, :]

        lax.fori_loop(0, n8, step, jnp.zeros((1, LANES), F32))

    out = pl.pallas_call(
        body, name="fgate_fwd", out_shape=jax.ShapeDtypeStruct((n8, SUBLANES, LANES), F32),
        compiler_params=_params(),
    )(pre.reshape(n8, SUBLANES, LANES), b_pad.reshape(1, LANES))
    return out.reshape(L, LANES)


def _fgate_bwd(dfcum, pre, b_pad):
    L = pre.shape[0]
    n8 = L // SUBLANES

    def body(d_ref, p_ref, b_ref, o_ref, s_ref):
        row = lax.broadcasted_iota(jnp.int32, (SUBLANES, LANES), 0)
        b = b_ref[...]

        def step(k, carry):
            c, acc = carry
            i = n8 - 1 - k
            t = _tile_cumsum(d_ref[i], row, True) + c
            dpre = t * _sigmoid(-(p_ref[i] + b))
            o_ref[i] = dpre
            return t[0:1, :], acc + dpre

        _, acc = lax.fori_loop(0, n8, step, (jnp.zeros((1, LANES), F32), jnp.zeros((SUBLANES, LANES), F32)))
        s_ref[...] = jnp.sum(acc, axis=0, keepdims=True)

    dpre, db = pl.pallas_call(
        body, name="fgate_bwd",
        out_shape=(jax.ShapeDtypeStruct((n8, SUBLANES, LANES), F32), jax.ShapeDtypeStruct((1, LANES), F32)),
        compiler_params=_params(),
    )(dfcum.reshape(n8, SUBLANES, LANES), pre.reshape(n8, SUBLANES, LANES), b_pad.reshape(1, LANES))
    return dpre.reshape(L, LANES), db


FOX_BLOCK = 512


def _fox_scores(qs, k, fk, diagonal):
    s = lax.dot_general(qs, k, _NT, preferred_element_type=F32) - fk
    if diagonal:
        row = lax.broadcasted_iota(jnp.int32, s.shape, 0)
        col = lax.broadcasted_iota(jnp.int32, s.shape, 1)
        s = jnp.where(row >= col, s, NEG_BIG)
    return s


def _fox_specs(tq, L):
    nq = L // tq
    return dict(
        rows=lambda off: pl.BlockSpec((tq, HEAD_DIM), lambda h, i: (i, off + h)),
        seq=lambda off: pl.BlockSpec((L, HEAD_DIM), lambda h, i: (0, off + h)),
        col=pl.BlockSpec((None, None, tq, 1), lambda h, i: (h, i, 0, 0)),
        col_all=pl.BlockSpec((None, nq, tq, 1), lambda h, i: (h, 0, 0, 0)),
        row=pl.BlockSpec((None, None, 1, tq), lambda h, i: (h, i, 0, 0)),
        row_all=pl.BlockSpec((None, nq, 1, tq), lambda h, i: (h, 0, 0, 0)))


FOX_FWD_HEADS = 2


def _fox_fwd(proj, kv, fk):
    L = proj.shape[0]
    tq = min(FOX_BLOCK, L)
    nq = L // tq
    nh = FOX_FWD_HEADS
    W = nh * HEAD_DIM

    def body(q_ref, k_ref, v_ref, fk_ref, o_ref, lse_ref, m_s, l_s, acc_s):
        qi = pl.program_id(1)
        cols = [slice(a * HEAD_DIM, (a + 1) * HEAD_DIM) for a in range(nh)]
        qs = [(q_ref[:, cs] * (HEAD_DIM ** -0.5)).astype(BF16) for cs in cols]
        m_s[...] = jnp.full_like(m_s, NEG_BIG)
        l_s[...] = jnp.zeros_like(l_s)
        acc_s[...] = jnp.zeros_like(acc_s)

        def block(j, diagonal):
            r0 = pl.multiple_of(j * tq, tq)
            for a, cs in enumerate(cols):
                s = _fox_scores(qs[a], k_ref[pl.ds(r0, tq), cs], fk_ref[a, j], diagonal)
                m_new = jnp.maximum(m_s[a], jnp.max(s, axis=-1, keepdims=True))
                alpha = jnp.exp(m_s[a] - m_new)
                p = jnp.exp(s - m_new)
                l_s[a] = alpha * l_s[a] + jnp.sum(p, axis=-1, keepdims=True)
                acc_s[a] = alpha * acc_s[a] + jnp.dot(p.astype(BF16), v_ref[pl.ds(r0, tq), cs],
                                                      preferred_element_type=F32)
                m_s[a] = m_new

        def below(j, carry):
            block(j, False)
            return carry

        lax.fori_loop(0, qi, below, 0)
        block(qi, True)
        for a, cs in enumerate(cols):
            o_ref[:, cs] = acc_s[a] / l_s[a]
            lse_ref[a] = m_s[a] + jnp.log(l_s[a])

    return pl.pallas_call(
        body, name="fox_fwd",
        out_shape=(jax.ShapeDtypeStruct((L, MAIN_WIDTH), F32),
                   jax.ShapeDtypeStruct((FOX_HEADS, nq, tq, 1), F32)),
        grid=(FOX_HEADS // nh, nq),
        in_specs=[pl.BlockSpec((tq, W), lambda h, i: (i, h)),
                  pl.BlockSpec((L, W), lambda h, i: (0, h)),
                  pl.BlockSpec((L, W), lambda h, i: (0, FOX_HEADS // nh + h)),
                  pl.BlockSpec((nh, nq, 1, tq), lambda h, i: (h, 0, 0, 0))],
        out_specs=(pl.BlockSpec((tq, W), lambda h, i: (i, h)),
                   pl.BlockSpec((nh, None, tq, 1), lambda h, i: (h, i, 0, 0))),
        scratch_shapes=[pltpu.VMEM((nh, tq, 1), F32), pltpu.VMEM((nh, tq, 1), F32),
                        pltpu.VMEM((nh, tq, HEAD_DIM), F32)],
        compiler_params=_params("parallel", "parallel"),
    )(proj, kv, kv, fk)


def _fox_bwd_dq(proj, kv, fk, lse, delta, datt, dproj):
    L = proj.shape[0]
    tq = min(FOX_BLOCK, L)
    nq = L // tq
    sp = _fox_specs(tq, L)

    def body(q_ref, k_ref, v_ref, fk_ref, lse_ref, dl_ref, do_ref, dp_hbm, dq_ref, df_ref, acc_s, df_s):
        qi = pl.program_id(1)
        qs = (q_ref[...] * (HEAD_DIM ** -0.5)).astype(BF16)
        dob = do_ref[...].astype(BF16)
        lse, dl = lse_ref[...], dl_ref[...]
        acc_s[...] = jnp.zeros_like(acc_s)
        df_s[...] = jnp.zeros_like(df_s)

        def block(j, diagonal):
            r0 = pl.multiple_of(j * tq, tq)
            k = k_ref[pl.ds(r0, tq), :]
            p = jnp.exp(_fox_scores(qs, k, fk_ref[j], diagonal) - lse)
            dp = lax.dot_general(dob, v_ref[pl.ds(r0, tq), :], _NT, preferred_element_type=F32)
            ds = p * (dp - dl)
            acc_s[...] += jnp.dot(ds.astype(BF16), k, preferred_element_type=F32)
            df_s[...] += jnp.sum(ds, axis=1, keepdims=True)

        def below(j, carry):
            block(j, False)
            return carry

        lax.fori_loop(0, qi, below, 0)
        block(qi, True)
        dq_ref[...] = (acc_s[...] * (HEAD_DIM ** -0.5)).astype(BF16)
        df_ref[...] = df_s[...]

    return pl.pallas_call(
        body, name="fox_bwd_dq",
        out_shape=(jax.ShapeDtypeStruct(dproj.shape, dproj.dtype),
                   jax.ShapeDtypeStruct((FOX_HEADS, nq, tq, 1), F32)),
        grid=(FOX_HEADS, nq),
        in_specs=[sp["rows"](0), sp["seq"](0), sp["seq"](FOX_HEADS), sp["row_all"],
                  sp["col"], sp["col"], sp["rows"](0), _ANY],
        out_specs=(sp["rows"](0), sp["col"]),
        input_output_aliases={7: 0},
        scratch_shapes=[pltpu.VMEM((tq, HEAD_DIM), F32), pltpu.VMEM((tq, 1), F32)],
        compiler_params=_params("parallel", "parallel"),
    )(proj, kv, kv, fk, lse, delta, datt, dproj)


def _fox_bwd_dkv(proj, kv, fk, lse, delta, datt):
    L = proj.shape[0]
    tq = min(FOX_BLOCK, L)
    nq = L // tq
    sp = _fox_specs(tq, L)

    def body(q_ref, k_ref, v_ref, fk_ref, lse_ref, dl_ref, do_ref,
             dk_ref, dv_ref, df_ref, dk_s, dv_s, df_s):
        ki = pl.program_id(1)
        k, v, fk = k_ref[...], v_ref[...], fk_ref[...]
        dk_s[...] = jnp.zeros_like(dk_s)
        dv_s[...] = jnp.zeros_like(dv_s)
        df_s[...] = jnp.zeros_like(df_s)

        def block(i, diagonal):
            r0 = pl.multiple_of(i * tq, tq)
            qs = (q_ref[pl.ds(r0, tq), :] * (HEAD_DIM ** -0.5)).astype(BF16)
            dob = do_ref[pl.ds(r0, tq), :].astype(BF16)
            p = jnp.exp(_fox_scores(qs, k, fk, diagonal) - lse_ref[i])
            dp = lax.dot_general(dob, v, _NT, preferred_element_type=F32)
            ds = p * (dp - dl_ref[i])
            dv_s[...] += lax.dot_general(p.astype(BF16), dob, _TN, preferred_element_type=F32)
            dk_s[...] += lax.dot_general(ds.astype(BF16), qs, _TN, preferred_element_type=F32)
            df_s[...] -= jnp.sum(ds, axis=0, keepdims=True)

        def above(i, carry):
            block(i, False)
            return carry

        block(ki, True)
        lax.fori_loop(ki + 1, nq, above, 0)
        dk_ref[...] = dk_s[...].astype(BF16)
        dv_ref[...] = dv_s[...].astype(BF16)
        df_ref[...] = df_s[...]

    return pl.pallas_call(
        body, name="fox_bwd_dkv",
        out_shape=(jax.ShapeDtypeStruct((L, MAIN_WIDTH), BF16),
                   jax.ShapeDtypeStruct((L, MAIN_WIDTH), BF16),
                   jax.ShapeDtypeStruct((FOX_HEADS, nq, 1, tq), F32)),
        grid=(FOX_HEADS, nq),
        in_specs=[sp["seq"](0), sp["rows"](0), sp["rows"](FOX_HEADS), sp["row"],
                  sp["col_all"], sp["col_all"], sp["seq"](0)],
        out_specs=(sp["rows"](0), sp["rows"](0), sp["row"]),
        scratch_shapes=[pltpu.VMEM((tq, HEAD_DIM), F32), pltpu.VMEM((tq, HEAD_DIM), F32),
                        pltpu.VMEM((1, tq), F32)],
        compiler_params=_params("parallel", "parallel"),
    )(proj, kv, kv, fk, lse, delta, datt)


def _pad_lanes(a):
    return jnp.pad(a, ((0, 0), (0, LANES - a.shape[1])))


def _mem_branch_fwd(memn, w_mk, proj, tag):
    kvm = _mm(memn, w_mk, name="mem_kv_" + tag)
    return kvm, _mem_attn_fwd(proj, kvm)


def _mem_branch_bwd(mem, g, w_mk, proj, memn, kvm, do_mem, dproj, tag):
    dproj, dkvm = _mem_attn_bwd(proj, kvm, do_mem, dproj)
    dkvm = dkvm.astype(BF16)
    dw_mk = _mm(memn, dkvm, ta=True, name="dw_mem_kv_" + tag, out_dtype=BF16)
    dmemn = _mm(dkvm, w_mk, tb=True, name="dmemn_" + tag)
    _, dg = _rmsnorm_bwd(mem, g, dmemn, name="mem_norm_bwd_" + tag, dx_dtype=BF16)
    return dproj, dw_mk, dg


def _local_step(x, mem, target, w, fetch=None, grads_ready=None):
    if grads_ready is None:
        grads_ready = lambda group, grads, token: token
    L = x.shape[0]
    g = {}
    w = dict(w)

    b_re_t = jnp.transpose(w["b_re"], (0, 2, 1))
    b_im_t = jnp.transpose(w["b_im"], (0, 2, 1))
    ar, ai, bbr_t, bbi_t = _s5_prep(w["lam_re"], w["lam_im"], w["log_step"], b_re_t, b_im_t)
    bmat, cmat = _s5_block_mats(bbr_t, bbi_t, w["c_re"], w["c_im"])
    a_rows = _s5_a_rows(ar, ai)

    hn0 = _rmsnorm_fwd(x, w["pre_norm_g"][0], name="pre_norm_0", out_dtype=BF16)
    memn0 = _rmsnorm_fwd(mem, w["mem_norm_g"][0], name="mem_norm_0", out_dtype=BF16)
    memn1 = _rmsnorm_fwd(mem, w["mem_norm_g"][1], name="mem_norm_1", out_dtype=BF16)
    if fetch is not None:
        w.update(fetch("a", [hn0, memn0, memn1, bmat, cmat, a_rows]))
    proj_a = _mm(hn0, w["w_in_a"], name="in_proj_a")
    y, yg, xp = _s5_fwd(proj_a, bmat, cmat, a_rows, w["d_skip"])
    if fetch is not None:
        w.update(fetch("b", yg))
    t = _mm(yg, w["w_glu"], name="glu_proj")
    kvm0, om0 = _mem_branch_fwd(memn0, w["w_mem_kv"][0], proj_a, "0")
    cat0 = _gate_a_fwd(y, t, w["b_glu"], proj_a, om0)
    o0 = _mm(cat0, w["w_out"][0], name="out_proj_0")
    h1 = _rmsnorm_fwd(o0, w["post_norm_g"][0], res=x, name="post_norm_0")

    kv_in = _rmsnorm_fwd(h1, w["kv_norm_g"], name="kv_norm", out_dtype=BF16)
    if fetch is not None:
        w.update(fetch("c", kv_in))
    kv = _mm(kv_in, w["w_kv"], name="kv_proj", out_dtype=BF16)
    pre_f = _mm(kv_in, w["w_fgate"], name="fgate_proj")
    b_f = jnp.pad(w["b_fgate"], (0, LANES - FOX_HEADS))
    fcum = _fgate_fwd(pre_f, b_f)
    fc = jnp.transpose(fcum[:, :FOX_HEADS])
    tq = min(FOX_BLOCK, L)
    fk = fc.reshape(FOX_HEADS, L // tq, 1, tq)

    hn1 = _rmsnorm_fwd(h1, w["pre_norm_g"][1], name="pre_norm_1", out_dtype=BF16)
    proj_b = _mm(hn1, w["w_in_b"], name="in_proj_b")
    att, lse = _fox_fwd(proj_b, kv, fk)
    kvm1, om1 = _mem_branch_fwd(memn1, w["w_mem_kv"][1], proj_b, "1")
    cat1 = _gate_b_fwd(att, proj_b, om1)
    o1 = _mm(cat1, w["w_out"][1], name="out_proj_1")
    dh2, loss_row = _final_norm_loss(o1, w["post_norm_g"][1], h1, target)

    do1, dpost1 = _rmsnorm_bwd(o1, w["post_norm_g"][1], dh2, name="post_norm_bwd_1", dx_dtype=BF16)
    dcat1 = _mm(do1, w["w_out"][1], tb=True, name="dcat_1", out_dtype=BF16)
    g["w_out_1"] = _mm(cat1, do1, ta=True, name="dw_out_1", out_dtype=BF16)
    datt, dproj_b, dom1, delta = _gate_b_bwd(dcat1, att, proj_b, om1)
    dproj_b, g["w_mem_kv_1"], dmemg1 = _mem_branch_bwd(mem, w["mem_norm_g"][1], w["w_mem_kv"][1], proj_b,
                                                      memn1, kvm1, dom1, dproj_b, "1")
    delta = delta.reshape(lse.shape)
    dproj_b, dfq = _fox_bwd_dq(proj_b, kv, fk, lse, delta, datt, dproj_b)
    dk, dv, dfk = _fox_bwd_dkv(proj_b, kv, fk, lse, delta, datt)
    g["w_in_b"] = _mm(hn1, dproj_b, ta=True, name="dw_in_b", out_dtype=BF16, shards=N_CHIPS)
    dhn1 = _mm(dproj_b, w["w_in_b"], tb=True, name="dhn_1")

    dkv = jnp.concatenate([dk, dv], axis=1)
    g["w_kv"] = _mm(kv_in, dkv, ta=True, name="dw_kv", out_dtype=BF16, shards=N_CHIPS)
    dkv_in_a = _mm(dkv, w["w_kv"], tb=True, name="dkv_in_kv")
    dfcum = _pad_lanes(jnp.transpose(dfq.reshape(FOX_HEADS, L) + dfk.reshape(FOX_HEADS, L)))
    dpre_f, db_f = _fgate_bwd(dfcum, pre_f, b_f)
    g["b_fgate"] = db_f[0, :FOX_HEADS]
    g["w_fgate"] = _mm(kv_in, dpre_f, ta=True, name="dw_fgate")[:, :FOX_HEADS]
    dkv_in_b = _mm(dpre_f, w["w_fgate"], tb=True, name="dkv_in_fgate")
    dh1, g["kv_norm_g"], dpre1 = _rmsnorm_bwd_pair(h1, w["kv_norm_g"], (dkv_in_a, dkv_in_b), w["pre_norm_g"][1],
                                                   dhn1, adds=(dh2,), name="kv_pre_norm_bwd")
    dh1 = grads_ready("b", g, dh1)

    do0, dpost0 = _rmsnorm_bwd(o0, w["post_norm_g"][0], dh1, name="post_norm_bwd_0", dx_dtype=BF16)
    dcat0 = _mm(do0, w["w_out"][0], tb=True, name="dcat_0", out_dtype=BF16)
    g["w_out_0"] = _mm(cat0, do0, ta=True, name="dw_out_0", out_dtype=BF16)
    dcat0 = grads_ready("b_send", g, dcat0)
    dproj_a, dt, dyg_a, dom0, db_glu = _gate_a_bwd(dcat0, y, t, w["b_glu"], proj_a, om0)
    g["b_glu"] = db_glu[0]
    g["w_glu"] = _mm(yg, dt, ta=True, name="dw_glu", out_dtype=BF16)
    dyg_b = _mm(dt, w["w_glu"], tb=True, name="dyg")
    dproj_a, g["w_mem_kv_0"], dmemg0 = _mem_branch_bwd(mem, w["mem_norm_g"][0], w["w_mem_kv"][0], proj_a,
                                                      memn0, kvm0, dom0, dproj_a, "0")
    dyg_b = grads_ready("a1", g, dyg_b)
    dproj_a, db_blk, dc_blk, da_rows, dd_skip = _s5_bwd(proj_a, dyg_a, dyg_b, y, xp, bmat, cmat, a_rows,
                                                        w["d_skip"], dproj_a)
    dproj_a = grads_ready("a1_send", g, dproj_a)
    g["d_skip"] = dd_skip[0]
    g["w_in_a"] = _mm(hn0, dproj_a, ta=True, name="dw_in_a", out_dtype=BF16, shards=N_CHIPS)
    dproj_a = grads_ready("a2", g, dproj_a)
    dhn0 = _mm(dproj_a, w["w_in_a"], tb=True, name="dhn_0")
    grad_x, dpre0 = _rmsnorm_bwd(x, w["pre_norm_g"][0], dhn0, adds=(dh1,), name="pre_norm_bwd_0")

    dbb = _s5_block_diag(db_blk)
    dcc = _s5_block_diag(dc_blk)
    g["c_re"], g["c_im"] = dcc[0], -dcc[1]
    d_ar = da_rows[:, 0, :STATE_COLS].reshape(SSM_GROUPS, SSM_STATE)
    d_ai = da_rows[:, 0, STATE_COLS:].reshape(SSM_GROUPS, SSM_STATE)
    dlr, dli, dls, dbr_t, dbi_t = _s5_prep_bwd(w["lam_re"], w["lam_im"], w["log_step"], b_re_t, b_im_t,
                                               d_ar, d_ai, dbb[0], dbb[1])
    g["lam_re"], g["lam_im"], g["log_step"] = dlr, dli, dls[:, 0]
    g["b_re"] = jnp.transpose(dbr_t, (0, 2, 1))
    g["b_im"] = jnp.transpose(dbi_t, (0, 2, 1))
    g["pre_norm_g"] = jnp.stack([dpre0, dpre1])
    g["post_norm_g"] = jnp.stack([dpost0, dpost1])
    g["mem_norm_g"] = jnp.stack([dmemg0, dmemg1])
    return loss_row, grad_x, g


_MESH = pl.DeviceIdType.MESH
_ANY = pl.BlockSpec(memory_space=pl.ANY)


def _place():
    x, y, c = lax.axis_index("x"), lax.axis_index("y"), lax.axis_index("c")
    chips = [(1 - x, y), (x, 1 - y), (1 - x, 1 - y)]
    return x, y, c, chips


_HBM = pl.BlockSpec(memory_space=pltpu.HBM)
_SEM = pl.BlockSpec(memory_space=pltpu.SEMAPHORE)
_SIDE = pltpu.SideEffectType.DATAFLOW_SIDE_EFFECTING


def _in_hbm(a):
    return pltpu.with_memory_space_constraint(a, pltpu.HBM)


def _hbm_like(a):
    return pltpu.HBM(a.shape, a.dtype)


def _ici_copies(srcs, lands, send_sem, recv_sem, src_at, dst_at, wait_at, to_sibling=False):
    x, y, c, chips = _place()
    peers = [(x, y, 1 - c)] if to_sibling else [(cx, cy, c) for cx, cy in chips]
    m = len(peers)
    start, wait = [], []
    for i in range(len(srcs)):
        for k, (px, py, pc) in enumerate(peers):
            sem = dict(send_sem=send_sem.at[m * i + k], recv_sem=recv_sem.at[m * i + k],
                       device_id=(px, py, pc), device_id_type=_MESH)
            src = src_at(srcs[i], 2 * px + py, c)
            start.append(pltpu.make_async_remote_copy(src_ref=src, dst_ref=dst_at(lands[i], 2 * x + y, k, c), **sem))
            wait.append(pltpu.make_async_remote_copy(src_ref=src, dst_ref=wait_at(lands[i], 2 * px + py, k, c), **sem))
    return start, wait


def _route_peers(route):
    return 1 if len(route) == 4 else 3


_BLOCK_ROUTE = (lambda s, j, c: s, lambda l, me, k, c: l.at[me, c], lambda l, j, k, c: l.at[j, c])


def _ici_start(srcs, lands, token, route, *, name):
    n = len(srcs)

    def body(*refs):
        start, _ = _ici_copies(refs[:n], refs[n:2 * n], refs[2 * n + 1], refs[2 * n + 2], *route)
        for cp in start:
            cp.start()

    sems = pltpu.SemaphoreType.DMA((_route_peers(route) * n,))
    outs = pl.pallas_call(
        body, name=name,
        out_shape=(sems, sems, *[_hbm_like(a) for a in srcs], *[_hbm_like(a) for a in lands], _hbm_like(token)),
        in_specs=[_HBM] * (2 * n + 1), out_specs=(_SEM, _SEM, *[_HBM] * (2 * n + 1)),
        input_output_aliases={i: 2 + i for i in range(2 * n + 1)},
        compiler_params=pltpu.CompilerParams(has_side_effects=_SIDE),
    )(*[_in_hbm(a) for a in srcs], *[_in_hbm(a) for a in lands], _in_hbm(token))
    return (outs[0], outs[1], list(outs[2:2 + n]), list(outs[2 + n:2 + 2 * n])), outs[2 + 2 * n]


def _ici_wait(handle, after, route, *, name):
    send_sem, recv_sem, srcs, lands = handle
    n = len(srcs)
    after = list(after) if isinstance(after, (list, tuple)) else [after]

    def body(*refs):
        _, wait = _ici_copies(refs[:n], refs[n:2 * n], refs[2 * n], refs[2 * n + 1], *route)
        for cp in wait:
            cp.wait_send()
            cp.wait_recv()

    outs = pl.pallas_call(
        body, name=name,
        out_shape=(*[_hbm_like(a) for a in srcs], *[_hbm_like(a) for a in lands]),
        in_specs=[_HBM] * (2 * n) + [_SEM, _SEM] + [_ANY] * len(after), out_specs=tuple([_HBM] * (2 * n)),
        input_output_aliases={i: i for i in range(2 * n)},
        compiler_params=pltpu.CompilerParams(has_side_effects=_SIDE),
    )(*srcs, *lands, send_sem, recv_sem, *after)
    return list(outs[:n]), list(outs[n:])


_GATHER_ROUTE = (lambda s, j, c: s.at[c], lambda l, me, k, c: l.at[me, c], lambda l, j, k, c: l.at[j, c])
_SCATTER_ROUTE = (lambda s, j, c: s.at[j], lambda l, me, k, c: l.at[k], lambda l, j, k, c: l.at[k])
_SWAP_ROUTE = (lambda s, j, c: s.at[:, 1 - c], lambda l, me, k, c: l, lambda l, j, k, c: l, True)


def _gather_forward(lands, tag, own=False):
    n = len(lands)
    m = 4 if own else 3

    def body(*refs):
        ins, outs = refs[:n], refs[n:2 * n]
        send_sem, recv_sem = refs[2 * n:]
        x, y, c, chips = _place()
        slots = [2 * cx + cy for cx, cy in chips] + [2 * x + y]

        def copy(i, k, half):
            return pltpu.make_async_remote_copy(
                src_ref=ins[i].at[slots[k], half], dst_ref=outs[i].at[slots[k], half],
                send_sem=send_sem.at[m * i + k], recv_sem=recv_sem.at[m * i + k],
                device_id=(x, y, 1 - c), device_id_type=_MESH)

        copies = [copy(i, k, c) for i in range(n) for k in range(m)]
        for cp in copies:
            cp.start()
        for i in range(n):
            for k in range(m):
                copy(i, k, 1 - c).wait_recv()
        for cp in copies:
            cp.wait_send()

    return pl.pallas_call(
        body, name="gather_forward_to_sibling_" + tag,
        out_shape=[jax.ShapeDtypeStruct(a.shape, a.dtype) for a in lands],
        in_specs=[_ANY] * n, out_specs=[_ANY] * n,
        input_output_aliases={i: i for i in range(n)},
        scratch_shapes=[pltpu.SemaphoreType.DMA((m * n,)), pltpu.SemaphoreType.DMA((m * n,))],
    )(*lands)


def _swap_halves(grads, tag):
    n = len(grads)

    def body(*refs):
        ins, outs = refs[:n], refs[n:2 * n]
        send_sem, recv_sem = refs[2 * n:]
        x, y, c, _ = _place()
        copies = [pltpu.make_async_remote_copy(
            src_ref=ins[i].at[:, 1 - c], dst_ref=outs[i],
            send_sem=send_sem.at[i], recv_sem=recv_sem.at[i],
            device_id=(x, y, 1 - c), device_id_type=_MESH) for i in range(n)]
        for cp in copies:
            cp.start()
        for cp in copies:
            cp.wait()

    return pl.pallas_call(
        body, name="grad_swap_halves_" + tag,
        out_shape=[jax.ShapeDtypeStruct((N_CHIPS,) + g.shape[2:], g.dtype) for g in grads],
        in_specs=[_ANY] * n, out_specs=[_ANY] * n,
        scratch_shapes=[pltpu.SemaphoreType.DMA((n,)), pltpu.SemaphoreType.DMA((n,))],
    )(*grads)


def _sum_rows(h, C):
    return max(d for d in range(SUBLANES, h + 1, SUBLANES) if h % d == 0 and d * C <= 1 << 20)


SUM_STEPS = 4


def _pair_sums(gs, rs, c_idx, *, name):
    n = len(gs)
    rows = [g.shape[2] // SUM_STEPS for g in gs]

    def body(c_ref, *refs):
        for g_ref, r_ref, o_ref in zip(refs[:n], refs[n:2 * n], refs[2 * n:]):
            o_ref[...] = (g_ref[...].astype(F32) + r_ref[...].astype(F32)).astype(o_ref.dtype)

    return pl.pallas_call(
        body, name=name,
        out_shape=[jax.ShapeDtypeStruct((N_CHIPS,) + g.shape[2:], g.dtype) for g in gs],
        grid_spec=pltpu.PrefetchScalarGridSpec(
            num_scalar_prefetch=1, grid=(N_CHIPS, SUM_STEPS),
            in_specs=[pl.BlockSpec((None, None, tr, g.shape[3]), lambda j, i, s: (j, s[0], i, 0))
                      for g, tr in zip(gs, rows)]
            + [pl.BlockSpec((None, tr, g.shape[3]), lambda j, i, s: (j, i, 0)) for g, tr in zip(gs, rows)],
            out_specs=[pl.BlockSpec((None, tr, g.shape[3]), lambda j, i, s: (j, i, 0)) for g, tr in zip(gs, rows)]),
        compiler_params=_params("parallel", "parallel"),
    )(c_idx, *gs, *rs)


def _owner_sums(ss, rs, jc_idx, *, name):
    n = len(ss)
    rows = [s.shape[1] // SUM_STEPS for s in ss]

    def body(jc_ref, *refs):
        for s_ref, r_ref, o_ref in zip(refs[:n], refs[n:2 * n], refs[2 * n:]):
            acc = s_ref[...].astype(F32)
            for k in range(3):
                acc = acc + r_ref[k].astype(F32)
            o_ref[...] = acc

    return pl.pallas_call(
        body, name=name,
        out_shape=[jax.ShapeDtypeStruct((2,) + s.shape[1:], F32) for s in ss],
        grid_spec=pltpu.PrefetchScalarGridSpec(
            num_scalar_prefetch=1, grid=(SUM_STEPS,),
            in_specs=[pl.BlockSpec((None, tr, s.shape[2]), lambda i, p: (p[0], i, 0)) for s, tr in zip(ss, rows)]
            + [pl.BlockSpec((3, tr, s.shape[2]), lambda i, p: (0, i, 0)) for s, tr in zip(ss, rows)],
            out_specs=[pl.BlockSpec((None, tr, s.shape[2]), lambda i, p: (p[1], i, 0)) for s, tr in zip(ss, rows)]),
        compiler_params=_params("parallel"),
    )(jc_idx, *ss, *rs)


def _share_with_sibling(bufs, tag):
    n = len(bufs)

    def body(*refs):
        ins, outs = refs[:n], refs[n:2 * n]
        send_sem, recv_sem = refs[2 * n:]
        x, y, c, _ = _place()

        def copy(i, half):
            return pltpu.make_async_remote_copy(
                src_ref=ins[i].at[half], dst_ref=outs[i].at[half],
                send_sem=send_sem.at[i], recv_sem=recv_sem.at[i],
                device_id=(x, y, 1 - c), device_id_type=_MESH)

        copies = [copy(i, c) for i in range(n)]
        for cp in copies:
            cp.start()
        for i in range(n):
            copy(i, 1 - c).wait_recv()
        for cp in copies:
            cp.wait_send()

    return pl.pallas_call(
        body, name="grad_share_with_sibling_" + tag,
        out_shape=[jax.ShapeDtypeStruct(b.shape, b.dtype) for b in bufs],
        in_specs=[_ANY] * n, out_specs=[_ANY] * n,
        input_output_aliases={i: i for i in range(n)},
        scratch_shapes=[pltpu.SemaphoreType.DMA((n,)), pltpu.SemaphoreType.DMA((n,))],
    )(*bufs)


def _chip_sums(grads, c_idx, tag):
    views = [g.reshape(N_CHIPS, 2, g.shape[1] // 2, g.shape[2]) for g in grads]
    arrived = _swap_halves(views, tag)
    return _pair_sums(views, arrived, c_idx, name=f"grad_pair_sums_{tag}")


def _owner_totals(sums, arrived, jc_idx, tag):
    halves = _owner_sums(sums, arrived, jc_idx, name=f"grad_owner_sums_{tag}")
    return [f.reshape(-1, f.shape[2]) for f in _share_with_sibling(halves, tag)]


def _sum_devices(blocks):
    R = blocks.shape[2]
    tr = _sum_rows(R, 2 * N_CHIPS * LANES)

    def body(b_ref, o_ref):
        acc = b_ref[0, 0]
        for d in range(1, 2 * N_CHIPS):
            acc = acc + b_ref[d // 2, d % 2]
        o_ref[...] = acc

    return pl.pallas_call(
        body, name="sum_small_over_devices", out_shape=jax.ShapeDtypeStruct((R, LANES), F32),
        grid=(R // tr,),
        in_specs=[pl.BlockSpec((N_CHIPS, 2, tr, LANES), lambda i: (0, 0, i, 0))],
        out_specs=pl.BlockSpec((tr, LANES), lambda i: (i, 0)),
        compiler_params=_params("parallel"),
    )(blocks)


def _adamw(w, g, m, v, *, name):
    R, C = w.shape
    whole_fits = 7 * 2 * R * C * 4 <= VMEM_LIMIT_BYTES // 2
    tr = R if whole_fits else next(c for c in (256, 192, 128, 64, 32, 16, 8) if R % c == 0)

    def body(w_ref, g_ref, m_ref, v_ref, d_ref, nm_ref, nv_ref):
        g = g_ref[...]
        m = ADAM_B1 * m_ref[...] + (1.0 - ADAM_B1) * g
        v = ADAM_B2 * v_ref[...] + (1.0 - ADAM_B2) * (g * g)
        nm_ref[...] = m
        nv_ref[...] = v
        m_hat = m / (1.0 - ADAM_B1 ** ADAM_STEP)
        v_hat = v / (1.0 - ADAM_B2 ** ADAM_STEP)
        d_ref[...] = -ADAM_LR * (m_hat / (jnp.sqrt(v_hat) + ADAM_EPS) + ADAM_WD * w_ref[...])

    blk = pl.BlockSpec((tr, C), lambda i: (i, 0))
    sds = jax.ShapeDtypeStruct((R, C), F32)
    return pl.pallas_call(
        body, name=name, out_shape=(sds, sds, sds), grid=(R // tr,),
        in_specs=[blk] * 4, out_specs=(blk, blk, blk),
        compiler_params=_params("parallel"),
    )(w, g, m, v)


_TILE = SUBLANES * LANES


def _pack(arrays):
    rows = []
    for a in arrays:
        flat = a.reshape(-1)
        flat = jnp.pad(flat, (0, (-flat.shape[0]) % _TILE))
        rows.append(flat.reshape(-1, LANES))
    return jnp.concatenate(rows, axis=0)


def _unpack(buf, shapes):
    out, r = [], 0
    for s in shapes:
        size = math.prod(s)
        nr = -(-size // _TILE) * SUBLANES
        out.append(buf[r:r + nr].reshape(-1)[:size].reshape(s))
        r += nr
    return out


_BIG = ("w_in_a", "w_glu", "w_kv", "w_in_b", "w_mem_kv", "w_out")
_REPLICATED = ("pre_norm_g", "post_norm_g", "lam_re", "lam_im", "log_step", "b_re", "b_im", "c_re", "c_im",
               "kv_norm_g", "b_fgate", "mem_norm_g")
_SHARDED_SMALL = ("d_skip", "b_glu", "w_fgate")
_WEIGHTS = ("pre_norm_g", "post_norm_g", "w_in_a", "lam_re", "lam_im", "log_step", "b_re", "b_im", "c_re",
            "c_im", "d_skip", "w_glu", "b_glu", "kv_norm_g", "w_kv", "w_fgate", "b_fgate", "w_in_b",
            "mem_norm_g", "w_mem_kv", "w_out")


def _halves(a):
    return a.reshape(2, a.shape[0] // 2, a.shape[1])


def _unhalve(a):
    return a.reshape(N_CHIPS, 2 * a.shape[2], a.shape[3])


def _columns(a):
    return jnp.transpose(a, (1, 0, 2)).reshape(a.shape[1], N_CHIPS * a.shape[2])


def kernel(x, mem, pre_norm_g, post_norm_g, w_in_a, lam_re, lam_im, log_step, b_re, b_im, c_re, c_im, d_skip, w_glu, b_glu, kv_norm_g, w_kv, w_fgate, b_fgate, w_in_b, mem_norm_g, w_mem_kv, w_out, loss_target, m_pre_norm_g, m_post_norm_g, m_w_in_a, m_lam_re, m_lam_im, m_log_step, m_b_re, m_b_im, m_c_re, m_c_im, m_d_skip, m_w_glu, m_b_glu, m_kv_norm_g, m_w_kv, m_w_fgate, m_b_fgate, m_w_in_b, m_mem_norm_g, m_w_mem_kv, m_w_out, v_pre_norm_g, v_post_norm_g, v_w_in_a, v_lam_re, v_lam_im, v_log_step, v_b_re, v_b_im, v_c_re, v_c_im, v_d_skip, v_w_glu, v_b_glu, v_kv_norm_g, v_w_kv, v_w_fgate, v_b_fgate, v_w_in_b, v_mem_norm_g, v_w_mem_kv, v_w_out):
    a = dict(locals())
    xi, yi, ci = lax.axis_index("x"), lax.axis_index("y"), lax.axis_index("c")
    chip = 2 * xi + yi
    c_idx = jnp.reshape(ci, (1,)).astype(jnp.int32)
    jc_idx = jnp.stack([chip, ci]).astype(jnp.int32)

    vec = jnp.zeros((2 * SUBLANES, MAIN_WIDTH // N_CHIPS), F32)
    vec = vec.at[0].set(a["d_skip"][0]).at[1].set(a["b_glu"][0])
    def own_slot(gathered, parts):
        return [lax.dynamic_update_index_in_dim(g, p, chip, 0) for g, p in zip(gathered, parts)]

    parts_a = [_halves(a["w_in_a"][0].astype(BF16)), _halves(vec)]
    parts_b = [_halves(a["w_glu"][0].astype(BF16)), _halves(a["w_mem_kv"].reshape(-1, 2 * MEM_WIDTH).astype(BF16)),
               _halves(a["w_out"].reshape(-1, D_MODEL).astype(BF16))]
    parts_c = [_halves(a["w_kv"].astype(BF16)), _halves(_pad_lanes(a["w_fgate"]).astype(BF16)),
               _halves(a["w_in_b"][0].astype(BF16))]
    travelling, token = {}, a["pre_norm_g"]
    for tag, parts in (("a", parts_a), ("b", parts_b), ("c", parts_c)):
        lands = [lax.empty((N_CHIPS,) + p.shape, p.dtype) for p in parts]
        travelling[tag], token = _ici_start(parts, lands, token, _GATHER_ROUTE, name=f"gather_{tag}_start")

    def fetch(tag, after):
        parts, lands = _ici_wait(travelling[tag], after, _GATHER_ROUTE, name=f"gather_{tag}_wait")
        full = own_slot(_gather_forward(lands, tag), parts)
        if tag == "a":
            w_in_a, vecs = full
            return dict(w_in_a=_columns(_unhalve(w_in_a)), d_skip=vecs[:, 0, 0, :].reshape(MAIN_WIDTH),
                        b_glu=vecs[:, 0, 1, :].reshape(MAIN_WIDTH))
        if tag == "b":
            w_glu, w_mk, w_out = full
            return dict(w_glu=w_glu.reshape(MAIN_WIDTH, MAIN_WIDTH),
                        w_mem_kv=[w_mk[:, i].reshape(D_MODEL, 2 * MEM_WIDTH) for i in range(2)],
                        w_out=[w_out[:, i].reshape(D_MODEL, D_MODEL) for i in range(2)])
        w_kv, w_fg, w_in_b = full
        return dict(w_kv=_columns(_unhalve(w_kv)), w_fgate=w_fg.reshape(D_MODEL, LANES),
                    w_in_b=_columns(_unhalve(w_in_b)))

    w = dict(
        pre_norm_g=token, post_norm_g=a["post_norm_g"], mem_norm_g=a["mem_norm_g"],
        kv_norm_g=a["kv_norm_g"], b_fgate=a["b_fgate"],
        lam_re=a["lam_re"][0], lam_im=a["lam_im"][0], log_step=a["log_step"][0],
        b_re=a["b_re"][0], b_im=a["b_im"][0], c_re=a["c_re"][0], c_im=a["c_im"][0])

    sent = {}

    swapping = {}

    def grads_ready(event, g, token):
        tag = event.split("_")[0]
        if event in ("b", "a1"):
            big = {"b": lambda: [g["w_kv"], g["w_in_b"], g["w_mem_kv_1"].reshape(N_CHIPS, -1, 2 * MEM_WIDTH),
                                 g["w_out_1"].reshape(N_CHIPS, -1, D_MODEL)],
                   "a1": lambda: [g["w_glu"].reshape(N_CHIPS, -1, MAIN_WIDTH),
                                  g["w_mem_kv_0"].reshape(N_CHIPS, -1, 2 * MEM_WIDTH),
                                  g["w_out_0"].reshape(N_CHIPS, -1, D_MODEL)]}[tag]()
            views = [b.reshape(N_CHIPS, 2, b.shape[1] // 2, b.shape[2]) for b in big]
            lands = [lax.empty((N_CHIPS,) + v.shape[2:], v.dtype) for v in views]
            swapping[tag], token = _ici_start(views, lands, token, _SWAP_ROUTE, name=f"grad_swap_{tag}_start")
            return token
        if event == "a2":
            sums = _chip_sums([g["w_in_a"]], c_idx, tag)
        else:
            views, arrived = _ici_wait(swapping[tag], token, _SWAP_ROUTE, name=f"grad_swap_{tag}_wait")
            sums = _pair_sums(views, arrived, c_idx, name=f"grad_pair_sums_{tag}")
        lands = [lax.empty((3,) + s.shape[1:], s.dtype) for s in sums]
        sent[tag], token = _ici_start(sums, lands, token, _SCATTER_ROUTE, name=f"grad_send_{tag}_start")
        return token

    loss_row, grad_x, g = _local_step(a["x"][0], a["mem"][0], a["loss_target"][0], w, fetch, grads_ready)

    small_names = _REPLICATED + _SHARDED_SMALL
    pack = _pack([g[n] for n in small_names])
    blocks = lax.empty((N_CHIPS, 2) + pack.shape, F32)
    small_sent, loss_row = _ici_start([pack], [blocks], loss_row, _BLOCK_ROUTE, name="small_sums_start")
    loss = lax.psum(jnp.sum(loss_row), MESH_AXES)

    def totals(tag, after):
        sums, arrived = _ici_wait(sent[tag], after, _SCATTER_ROUTE, name=f"grad_send_{tag}_wait")
        return _owner_totals(sums, arrived, jc_idx, tag)

    r_kv, r_in_b, r_mk1, r_out1 = totals("b", grad_x)
    r_glu, r_mk0, r_out0 = totals("a1", r_out1)
    (r_in_a,) = totals("a2", r_out0)
    grads = {"w_in_a": r_in_a[None], "w_glu": r_glu[None], "w_kv": r_kv, "w_in_b": r_in_b[None],
             "w_mem_kv": jnp.stack([r_mk0, r_mk1]), "w_out": jnp.stack([r_out0, r_out1])}

    delta, new_m, new_v = {}, {}, {}
    for n in _BIG:
        shape = a[n].shape
        d2 = (-1, shape[-1])
        d, m, v = _adamw(a[n].reshape(d2), grads[n].reshape(d2), a["m_" + n].reshape(d2),
                         a["v_" + n].reshape(d2), name="adamw_" + n)
        delta[n], new_m[n], new_v[n] = d.reshape(shape), m.reshape(shape), v.reshape(shape)

    (pack,), (blocks,) = _ici_wait(small_sent, [delta[n] for n in _BIG], _BLOCK_ROUTE, name="small_sums_wait")
    blocks = lax.dynamic_update_slice(blocks, pack[None, None], (chip, ci, 0, 0))
    (blocks,) = _gather_forward([blocks], "small", own=True)
    small = dict(zip(small_names, _unpack(_sum_devices(blocks), [g[n].shape for n in small_names])))
    for n in _REPLICATED:
        grads[n] = small[n].reshape(a[n].shape)
    nd = MAIN_WIDTH // N_CHIPS
    grads["d_skip"] = lax.dynamic_slice(small["d_skip"], (chip * nd,), (nd,))[None]
    grads["b_glu"] = lax.dynamic_slice(small["b_glu"], (chip * nd,), (nd,))[None]
    nf = D_MODEL // N_CHIPS
    grads["w_fgate"] = lax.dynamic_slice(small["w_fgate"], (chip * nf, 0), (nf, FOX_HEADS))

    shapes = [a[n].shape for n in small_names]
    d, m, v = _adamw(_pack([a[n] for n in small_names]), _pack([grads[n] for n in small_names]),
                     _pack([a["m_" + n] for n in small_names]), _pack([a["v_" + n] for n in small_names]),
                     name="adamw_small")
    for n, dd, mm, vv in zip(small_names, _unpack(d, shapes), _unpack(m, shapes), _unpack(v, shapes)):
        delta[n], new_m[n], new_v[n] = dd, mm, vv

    return (loss, grad_x[None], *[grads[n] for n in _WEIGHTS], *[delta[n] for n in _WEIGHTS],
            *[new_m[n] for n in _WEIGHTS], *[new_v[n] for n in _WEIGHTS])
```

```python
import math

import jax
import jax.numpy as jnp
from jax import lax
from jax.experimental import pallas as pl
from jax.experimental.pallas import tpu as pltpu

F32 = jnp.float32
BF16 = jnp.bfloat16

D_MODEL = 2048
N_MEM = 256
MAIN_WIDTH = 1536
MEM_WIDTH = 512
IN_WIDTH = 2 * MAIN_WIDTH + 2 * MEM_WIDTH
HEAD_DIM = 128
FOX_HEADS = MAIN_WIDTH // HEAD_DIM
MEM_HEADS = MEM_WIDTH // HEAD_DIM
SSM_GROUP = 16
SSM_GROUPS = MAIN_WIDTH // SSM_GROUP
SSM_STATE = 64
GROUPS_PER_BLOCK = 8
SSM_BLOCKS = SSM_GROUPS // GROUPS_PER_BLOCK
STATE_COLS = GROUPS_PER_BLOCK * SSM_STATE
EPS = 1e-6
ADAM_LR = 0.001
ADAM_B1 = 0.9
ADAM_B2 = 0.999
ADAM_EPS = 1e-08
ADAM_WD = 0.01
ADAM_STEP = 10
N_CHIPS = 4
LANES = 128
SUBLANES = 8
VMEM_LIMIT_BYTES = 56 * 1024 * 1024
NEG_BIG = -1e30
MESH_AXES = ("x", "y", "c")


def _params(*sem):
    return pltpu.CompilerParams(dimension_semantics=sem if sem else None,
                                vmem_limit_bytes=VMEM_LIMIT_BYTES)


def _sigmoid(x):
    return 1.0 / (1.0 + jnp.exp(-x))


def _gelu(x):
    c = math.sqrt(2.0 / math.pi)
    return 0.5 * x * (1.0 + jnp.tanh(c * (x + 0.044715 * (x * x * x))))


def _gelu_grad(x):
    c = math.sqrt(2.0 / math.pi)
    t = jnp.tanh(c * (x + 0.044715 * (x * x * x)))
    return 0.5 * (1.0 + t) + 0.5 * x * (1.0 - t * t) * (c * (1.0 + 3.0 * 0.044715 * (x * x)))


def _silu_and_grad(z):
    s = _sigmoid(z)
    return z * s, s * (1.0 + z * (1.0 - s))


_TILE_CHOICES = (4096, 3072, 2048, 1536, 1024, 768, 512, 384, 256, LANES)


def _tile(n, cap):
    return next(c for c in _TILE_CHOICES if c <= cap and n % c == 0)


def _mm(a, b, *, name, ta=False, tb=False, out_dtype=F32, shards=1, tm=1024, tn=1024, tk=4096):
    if ta:
        K, M = a.shape
    else:
        M, K = a.shape
    if tb:
        N, kb = b.shape
    else:
        kb, N = b.shape
    assert K == kb, (a.shape, b.shape)
    ns = N // shards
    tm, tn, tk = _tile(M, tm), _tile(ns, tn), _tile(K, tk)
    assert M % tm == 0 and ns % tn == 0 and K % tk == 0 and N % shards == 0
    nk = K // tk
    dn = (((0 if ta else 1,), (1 if tb else 0,)), ((), ()))

    def body(a_ref, b_ref, o_ref, *acc):
        prod = lax.dot_general(a_ref[...].astype(BF16), b_ref[...].astype(BF16), dn, preferred_element_type=F32)
        if nk == 1:
            o_ref[...] = prod.astype(o_ref.dtype)
            return
        acc_ref, = acc
        k = pl.program_id(2)

        @pl.when(k == 0)
        def _():
            acc_ref[...] = jnp.zeros_like(acc_ref)

        acc_ref[...] += prod

        @pl.when(k == nk - 1)
        def _():
            o_ref[...] = acc_ref[...].astype(o_ref.dtype)

    a_spec = (pl.BlockSpec((tk, tm), lambda i, j, k: (k, i)) if ta
              else pl.BlockSpec((tm, tk), lambda i, j, k: (i, k)))
    b_spec = (pl.BlockSpec((tn, tk), lambda i, j, k: (j, k)) if tb
              else pl.BlockSpec((tk, tn), lambda i, j, k: (k, j)))
    if shards == 1:
        out_shape = jax.ShapeDtypeStruct((M, N), out_dtype)
        o_spec = pl.BlockSpec((tm, tn), lambda i, j, k: (i, j))
    else:
        nb = ns // tn
        out_shape = jax.ShapeDtypeStruct((shards, M, ns), out_dtype)
        o_spec = pl.BlockSpec((None, tm, tn), lambda i, j, k: (j // nb, i, j % nb))
    return pl.pallas_call(
        body, name=name, out_shape=out_shape,
        grid=(M // tm, N // tn, nk),
        in_specs=[a_spec, b_spec], out_specs=o_spec,
        scratch_shapes=[] if nk == 1 else [pltpu.VMEM((tm, tn), F32)],
        compiler_params=_params("parallel", "parallel", "arbitrary"),
    )(a, b)


def _rmsnorm_fwd(x, g, *, name, res=None, out_dtype=F32, tr=256):
    L, D = x.shape
    tr = min(tr, L)
    has_res = res is not None

    def body(*refs):
        if has_res:
            x_ref, g_ref, r_ref, o_ref = refs
        else:
            x_ref, g_ref, o_ref = refs
        xf = x_ref[...]
        r = lax.rsqrt(jnp.mean(xf * xf, axis=-1, keepdims=True) + EPS)
        y = xf * r * g_ref[...]
        if has_res:
            y = r_ref[...] + y
        o_ref[...] = y.astype(o_ref.dtype)

    row = pl.BlockSpec((tr, D), lambda i: (i, 0))
    vec = pl.BlockSpec((1, D), lambda i: (0, 0))
    ins = [x, g.reshape(1, D)] + ([res] if has_res else [])
    return pl.pallas_call(
        body, name=name, out_shape=jax.ShapeDtypeStruct((L, D), out_dtype),
        grid=(L // tr,), in_specs=[row, vec] + ([row] if has_res else []), out_specs=row,
        compiler_params=_params("parallel"),
    )(*ins)


def _rmsnorm_bwd(x, g, dy, *, name, adds=(), dx_dtype=F32, tr=256):
    L, D = x.shape
    tr = min(tr, L)
    dys = dy if isinstance(dy, tuple) else (dy,)
    n_dy, n_add = len(dys), len(adds)

    def body(*refs):
        x_ref, g_ref = refs[:2]
        dy_refs = refs[2:2 + n_dy]
        add_refs = refs[2 + n_dy:2 + n_dy + n_add]
        dx_ref, dg_ref = refs[2 + n_dy + n_add:]
        xf = x_ref[...]
        dyf = dy_refs[0][...].astype(F32)
        for d_ref in dy_refs[1:]:
            dyf = dyf + d_ref[...].astype(F32)
        r = lax.rsqrt(jnp.mean(xf * xf, axis=-1, keepdims=True) + EPS)
        gy = dyf * g_ref[...]
        c = jnp.mean(xf * gy, axis=-1, keepdims=True) * (r * r * r)
        dx = gy * r - xf * c
        for a_ref in add_refs:
            dx = dx + a_ref[...].astype(F32)
        dx_ref[...] = dx.astype(dx_ref.dtype)

        @pl.when(pl.program_id(0) == 0)
        def _():
            dg_ref[...] = jnp.zeros_like(dg_ref)

        dg_ref[...] += jnp.sum(dyf * xf * r, axis=0, keepdims=True)

    row = pl.BlockSpec((tr, D), lambda i: (i, 0))
    vec = pl.BlockSpec((1, D), lambda i: (0, 0))
    dx, dg = pl.pallas_call(
        body, name=name,
        out_shape=(jax.ShapeDtypeStruct((L, D), dx_dtype), jax.ShapeDtypeStruct((1, D), F32)),
        grid=(L // tr,), in_specs=[row, vec] + [row] * (n_dy + n_add), out_specs=(row, vec),
        compiler_params=_params("arbitrary"),
    )(x, g.reshape(1, D), *dys, *adds)
    return dx, dg.reshape(D)


def _rmsnorm_bwd_pair(x, g1, dy1, g2, dy2, *, name, adds=(), tr=256):
    L, D = x.shape
    tr = min(tr, L)
    dy1s = dy1 if isinstance(dy1, tuple) else (dy1,)
    n1, n_add = len(dy1s), len(adds)

    def body(*refs):
        x_ref, g1_ref, g2_ref = refs[:3]
        dy1_refs = refs[3:3 + n1]
        dy2_ref = refs[3 + n1]
        add_refs = refs[4 + n1:4 + n1 + n_add]
        dx_ref, dg1_ref, dg2_ref = refs[4 + n1 + n_add:]
        xf = x_ref[...]
        d1 = dy1_refs[0][...].astype(F32)
        for d_ref in dy1_refs[1:]:
            d1 = d1 + d_ref[...].astype(F32)
        d2 = dy2_ref[...].astype(F32)
        r = lax.rsqrt(jnp.mean(xf * xf, axis=-1, keepdims=True) + EPS)
        gy = d1 * g1_ref[...] + d2 * g2_ref[...]
        c = jnp.mean(xf * gy, axis=-1, keepdims=True) * (r * r * r)
        dx = gy * r - xf * c
        for a_ref in add_refs:
            dx = dx + a_ref[...].astype(F32)
        dx_ref[...] = dx

        @pl.when(pl.program_id(0) == 0)
        def _():
            dg1_ref[...] = jnp.zeros_like(dg1_ref)
            dg2_ref[...] = jnp.zeros_like(dg2_ref)

        xr = xf * r
        dg1_ref[...] += jnp.sum(d1 * xr, axis=0, keepdims=True)
        dg2_ref[...] += jnp.sum(d2 * xr, axis=0, keepdims=True)

    row = pl.BlockSpec((tr, D), lambda i: (i, 0))
    vec = pl.BlockSpec((1, D), lambda i: (0, 0))
    dx, dg1, dg2 = pl.pallas_call(
        body, name=name,
        out_shape=(jax.ShapeDtypeStruct((L, D), F32), jax.ShapeDtypeStruct((1, D), F32),
                   jax.ShapeDtypeStruct((1, D), F32)),
        grid=(L // tr,), in_specs=[row, vec, vec] + [row] * (n1 + 1 + n_add), out_specs=(row, vec, vec),
        compiler_params=_params("arbitrary"),
    )(x, g1.reshape(1, D), g2.reshape(1, D), *dy1s, dy2, *adds)
    return dx, dg1.reshape(D), dg2.reshape(D)


def _final_norm_loss(o, g, res, target, *, tr=256):
    L, D = o.shape
    tr = min(tr, L)

    def body(o_ref, g_ref, r_ref, t_ref, dh_ref, loss_ref):
        xf = o_ref[...]
        r = lax.rsqrt(jnp.mean(xf * xf, axis=-1, keepdims=True) + EPS)
        e = (r_ref[...] + xf * r * g_ref[...]) - t_ref[...]
        dh_ref[...] = e * (1.0 / D)

        @pl.when(pl.program_id(0) == 0)
        def _():
            loss_ref[...] = jnp.zeros_like(loss_ref)

        loss_ref[...] += jnp.sum(e * e, axis=0, keepdims=True) * (0.5 / D)

    row = pl.BlockSpec((tr, D), lambda i: (i, 0))
    vec = pl.BlockSpec((1, D), lambda i: (0, 0))
    dh, lp = pl.pallas_call(
        body, name="post_norm_1_loss",
        out_shape=(jax.ShapeDtypeStruct((L, D), F32), jax.ShapeDtypeStruct((1, D), F32)),
        grid=(L // tr,), in_specs=[row, vec, row, row], out_specs=(row, vec),
        compiler_params=_params("arbitrary"),
    )(o, g.reshape(1, D), res, target)
    return dh, lp


def _s5_coeffs(lr, li, ls):
    dt = jnp.exp(ls)
    mag = jnp.exp(lr * dt)
    ar = mag * jnp.cos(li * dt)
    ai = mag * jnp.sin(li * dt)
    den = lr * lr + li * li
    cr = ((ar - 1.0) * lr + ai * li) / den
    ci = (ai * lr - (ar - 1.0) * li) / den
    return dt, ar, ai, den, cr, ci


def _s5_prep(lam_re, lam_im, log_step, b_re_t, b_im_t):
    G, P = lam_re.shape
    H = b_re_t.shape[1]

    def body(lr_ref, li_ref, ls_ref, br_ref, bi_ref, ar_ref, ai_ref, bbr_ref, bbi_ref):
        _, ar, ai, _, cr, ci = _s5_coeffs(lr_ref[...], li_ref[...], ls_ref[...])
        ar_ref[...] = ar
        ai_ref[...] = ai
        br, bi = br_ref[...], bi_ref[...]
        crb, cib = cr[:, None, :], ci[:, None, :]
        bbr_ref[...] = crb * br - cib * bi
        bbi_ref[...] = crb * bi + cib * br

    return pl.pallas_call(
        body, name="s5_prep",
        out_shape=(jax.ShapeDtypeStruct((G, P), F32), jax.ShapeDtypeStruct((G, P), F32),
                   jax.ShapeDtypeStruct((G, H, P), F32), jax.ShapeDtypeStruct((G, H, P), F32)),
        compiler_params=_params(),
    )(lam_re, lam_im, log_step.reshape(G, 1), b_re_t, b_im_t)


def _s5_prep_bwd(lam_re, lam_im, log_step, b_re_t, b_im_t, d_ar, d_ai, d_bbr, d_bbi):
    G, P = lam_re.shape
    H = b_re_t.shape[1]

    def body(lr_ref, li_ref, ls_ref, br_ref, bi_ref, dar_ref, dai_ref, dbbr_ref, dbbi_ref,
             dlr_ref, dli_ref, dls_ref, dbr_ref, dbi_ref):
        lr, li = lr_ref[...], li_ref[...]
        dt, ar, ai, den, cr, ci = _s5_coeffs(lr, li, ls_ref[...])
        br, bi = br_ref[...], bi_ref[...]
        gbr, gbi = dbbr_ref[...], dbbi_ref[...]
        crb, cib = cr[:, None, :], ci[:, None, :]
        dbr_ref[...] = crb * gbr + cib * gbi
        dbi_ref[...] = crb * gbi - cib * gbr
        gcr = jnp.sum(br * gbr + bi * gbi, axis=1)
        gci = jnp.sum(br * gbi - bi * gbr, axis=1)
        ilr, ili = lr / den, -li / den
        gar = dar_ref[...] + (ilr * gcr + ili * gci)
        gai = dai_ref[...] + (ilr * gci - ili * gcr)
        qr, qi = cr * ilr - ci * ili, cr * ili + ci * ilr
        glr = -(qr * gcr + qi * gci)
        gli = -(qr * gci - qi * gcr)
        glr = glr + dt * (ar * gar + ai * gai)
        gli = gli + dt * (ar * gai - ai * gar)
        wr, wi = lr * ar - li * ai, lr * ai + li * ar
        gdt = jnp.sum(wr * gar + wi * gai, axis=1, keepdims=True)
        dlr_ref[...] = glr
        dli_ref[...] = gli
        dls_ref[...] = gdt * dt

    return pl.pallas_call(
        body, name="s5_prep_bwd",
        out_shape=(jax.ShapeDtypeStruct((G, P), F32), jax.ShapeDtypeStruct((G, P), F32),
                   jax.ShapeDtypeStruct((G, 1), F32),
                   jax.ShapeDtypeStruct((G, H, P), F32), jax.ShapeDtypeStruct((G, H, P), F32)),
        compiler_params=_params(),
    )(lam_re, lam_im, log_step.reshape(G, 1), b_re_t, b_im_t, d_ar, d_ai, d_bbr, d_bbi)


def _s5_block_mats(bbr_t, bbi_t, c_re, c_im):
    bmat = _s5_expand(bbr_t, bbi_t)
    cmat = jnp.transpose(_s5_expand(c_re, -c_im), (0, 2, 1))
    return bmat.astype(BF16), cmat.astype(BF16)


def _s5_diag_mask():
    r = lax.broadcasted_iota(jnp.int32, (LANES, 2 * STATE_COLS), 0) // SSM_GROUP
    c = (lax.broadcasted_iota(jnp.int32, (LANES, 2 * STATE_COLS), 1) % STATE_COLS) // SSM_STATE
    return (r == c).astype(F32)


def _s5_expand(re, im):
    re = jnp.tile(re.reshape(SSM_BLOCKS, LANES, SSM_STATE), (1, 1, GROUPS_PER_BLOCK))
    im = jnp.tile(im.reshape(SSM_BLOCKS, LANES, SSM_STATE), (1, 1, GROUPS_PER_BLOCK))
    return jnp.concatenate([re, im], axis=-1) * _s5_diag_mask()[None]


def _s5_block_diag(dmat):
    d = dmat * _s5_diag_mask()[None]
    parts = []
    for ri in range(2):
        acc = 0.0
        for g in range(GROUPS_PER_BLOCK):
            c0 = ri * STATE_COLS + g * SSM_STATE
            acc = acc + d[:, :, c0:c0 + SSM_STATE]
        parts.append(acc.reshape(SSM_GROUPS, SSM_GROUP, SSM_STATE))
    return jnp.stack(parts)


def _s5_a_rows(ar, ai):
    a = jnp.concatenate([ar.reshape(SSM_BLOCKS, STATE_COLS), ai.reshape(SSM_BLOCKS, STATE_COLS)], axis=1)
    return jnp.broadcast_to(a[:, None, :], (SSM_BLOCKS, SUBLANES, 2 * STATE_COLS))


def _to_step_major(src_ref, dst_ref, seg):
    for s in range(SUBLANES):
        dst_ref[pl.ds(s, seg, stride=SUBLANES), :] = src_ref[pl.ds(seg * s, seg), :]


def _segment_rows(ref, s, seg):
    return ref[pl.ds(s, seg, stride=SUBLANES), :]


def _cmul(ar, ai, xr, xi):
    return ar * xr - ai * xi, ar * xi + ai * xr


def _s5_tables(a_ref, pw_s, pwr_s, S, seg):
    ar, ai = a_ref[:, :S], a_ref[:, S:]

    def step(i, c):
        pr, pi = c
        pw_s[i, :, :S] = pr
        pw_s[i, :, S:] = pi
        nr, ni = _cmul(ar, ai, pr, pi)
        pwr_s[seg - 1 - i, :, :S] = nr
        pwr_s[seg - 1 - i, :, S:] = ni
        return nr, ni

    pr, pi = lax.fori_loop(0, seg, step, (jnp.ones_like(ar), jnp.zeros_like(ai)))
    pw_s[seg, :, :S] = pr
    pw_s[seg, :, S:] = pi


def _s5_fwd(proj, bmat, cmat, a_rows, d_skip, *, tc=512):
    L = proj.shape[0]
    tc = min(tc, L)
    nt = L // tc
    seg = tc // SUBLANES
    S = STATE_COLS

    def body(u_ref, b_ref, c_ref, a_ref, d_ref, y_ref, yg_ref, xp_ref,
             bu_s, xp_s, pw_s, pwr_s, carry_s, e_s, up_s, yc_s):
        @pl.when(pl.program_id(1) == 0)
        def _():
            carry_s[...] = jnp.zeros_like(carry_s)
            _s5_tables(a_ref, pw_s, pwr_s, S, seg)

        ar, ai = a_ref[:, :S], a_ref[:, S:]
        _to_step_major(u_ref, up_s, seg)
        bu = jnp.dot(up_s[...].astype(BF16), b_ref[...], preferred_element_type=F32)
        bu_s[...] = bu.reshape(seg, SUBLANES, 2 * S)

        def step(i, carry):
            cr, ci = carry
            xp_s[i, :, :S] = cr
            xp_s[i, :, S:] = ci
            return ar * cr - ai * ci + bu_s[i, :, :S], ar * ci + ai * cr + bu_s[i, :, S:]

        zero = jnp.zeros((SUBLANES, S), F32)
        fr, fi = lax.fori_loop(0, seg, step, (zero, zero))
        pr, pi = pw_s[seg, 0:1, :S], pw_s[seg, 0:1, S:]
        er, ei = carry_s[0:1, :S], carry_s[0:1, S:]
        for s in range(SUBLANES):
            e_s[s:s + 1, :S] = er
            e_s[s:s + 1, S:] = ei
            tr, ti = _cmul(pr, pi, er, ei)
            er, ei = fr[s:s + 1] + tr, fi[s:s + 1] + ti
        carry_s[0:1, :S] = er
        carry_s[0:1, S:] = ei
        pw = pw_s[0:seg]
        tr, ti = _cmul(pw[:, :, :S], pw[:, :, S:], e_s[:, :S][None], e_s[:, S:][None])
        xl = xp_s[...]
        xp = jnp.concatenate([xl[:, :, :S] + tr, xl[:, :, S:] + ti], axis=-1).reshape(tc, 2 * S)
        xp_ref[...] = xp
        a1r, a1i = ar[0:1], ai[0:1]
        x_re = a1r * xp[:, :S] - a1i * xp[:, S:] + bu[:, :S]
        x_im = a1r * xp[:, S:] + a1i * xp[:, :S] + bu[:, S:]
        xs = jnp.concatenate([x_re, x_im], axis=1).astype(BF16)
        yc_s[...] = jnp.dot(xs, c_ref[...], preferred_element_type=F32)
        for s in range(SUBLANES):
            rows = pl.ds(seg * s, seg)
            y = _segment_rows(yc_s, s, seg) + d_ref[...] * u_ref[rows, :]
            y_ref[rows, :] = y
            yg_ref[rows, :] = _gelu(y).astype(BF16)

    return pl.pallas_call(
        body, name="s5_fwd",
        out_shape=(jax.ShapeDtypeStruct((L, MAIN_WIDTH), F32),
                   jax.ShapeDtypeStruct((L, MAIN_WIDTH), BF16),
                   jax.ShapeDtypeStruct((L, SSM_BLOCKS * 2 * S), F32)),
        grid=(SSM_BLOCKS, nt),
        in_specs=[pl.BlockSpec((tc, LANES), lambda b, t: (t, b)),
                  pl.BlockSpec((None, LANES, 2 * S), lambda b, t: (b, 0, 0)),
                  pl.BlockSpec((None, 2 * S, LANES), lambda b, t: (b, 0, 0)),
                  pl.BlockSpec((None, SUBLANES, 2 * S), lambda b, t: (b, 0, 0)),
                  pl.BlockSpec((1, LANES), lambda b, t: (0, b))],
        out_specs=(pl.BlockSpec((tc, LANES), lambda b, t: (t, b)),
                   pl.BlockSpec((tc, LANES), lambda b, t: (t, b)),
                   pl.BlockSpec((tc, 2 * S), lambda b, t: (t, b))),
        scratch_shapes=[pltpu.VMEM((seg, SUBLANES, 2 * S), F32),
                        pltpu.VMEM((seg, SUBLANES, 2 * S), F32),
                        pltpu.VMEM((seg + 1, SUBLANES, 2 * S), F32),
                        pltpu.VMEM((seg, SUBLANES, 2 * S), F32),
                        pltpu.VMEM((SUBLANES, 2 * S), F32),
                        pltpu.VMEM((SUBLANES, 2 * S), F32),
                        pltpu.VMEM((tc, LANES), F32),
                        pltpu.VMEM((tc, LANES), F32)],
        compiler_params=_params("parallel", "arbitrary"),
    )(proj, bmat, cmat, a_rows, d_skip.reshape(1, MAIN_WIDTH))


def _s5_bwd(proj, dyg_a, dyg_b, y, xp, bmat, cmat, a_rows, d_skip, dproj, *, tc=512):
    L = proj.shape[0]
    tc = min(tc, L)
    nt = L // tc
    seg = tc // SUBLANES
    S = STATE_COLS
    nn = (((1,), (1,)), ((), ()))
    tn = (((0,), (0,)), ((), ()))

    def body(u_ref, dyga_ref, dygb_ref, y_ref, xp_ref, b_ref, c_ref, a_ref, d_ref, dp_hbm,
             du_ref, db_ref, dc_ref, da_ref, dd_ref, dl_s, pw_s, pwr_s, carry_s, e_s, up_s, dy_s, dyp_s, dup_s):
        @pl.when(pl.program_id(1) == 0)
        def _():
            carry_s[...] = jnp.zeros_like(carry_s)
            db_ref[...] = jnp.zeros_like(db_ref)
            dc_ref[...] = jnp.zeros_like(dc_ref)
            da_ref[...] = jnp.zeros_like(da_ref)
            dd_ref[...] = jnp.zeros_like(dd_ref)
            _s5_tables(a_ref, pw_s, pwr_s, S, seg)

        ar, ai = a_ref[:, :S], a_ref[:, S:]
        a1r, a1i = ar[0:1], ai[0:1]
        u = u_ref[...]
        dy = (dyga_ref[...] + dygb_ref[...]) * _gelu_grad(y_ref[...])
        dy_s[...] = dy
        xp = xp_ref[...]
        _to_step_major(u_ref, up_s, seg)
        _to_step_major(dy_s, dyp_s, seg)
        ubp = up_s[...].astype(BF16)
        dyp = dyp_s[...].astype(BF16)
        bu = jnp.dot(ubp, b_ref[...], preferred_element_type=F32)
        x_re = a1r * xp[:, :S] - a1i * xp[:, S:] + bu[:, :S]
        x_im = a1r * xp[:, S:] + a1i * xp[:, :S] + bu[:, S:]
        xs = jnp.concatenate([x_re, x_im], axis=1).astype(BF16)
        dc_ref[...] += lax.dot_general(dyp, xs, tn, preferred_element_type=F32)
        dx = lax.dot_general(dyp, c_ref[...], nn, preferred_element_type=F32)
        dl_s[...] = dx.reshape(seg, SUBLANES, 2 * S)

        def step(k, carry):
            cr, ci = carry
            i = seg - 1 - k
            lr = dl_s[i, :, :S] + (ar * cr + ai * ci)
            li = dl_s[i, :, S:] + (ar * ci - ai * cr)
            dl_s[i, :, :S] = lr
            dl_s[i, :, S:] = li
            return lr, li

        zero = jnp.zeros((SUBLANES, S), F32)
        fr, fi = lax.fori_loop(0, seg, step, (zero, zero))
        pr, pi = pw_s[seg, 0:1, :S], pw_s[seg, 0:1, S:]
        er, ei = carry_s[0:1, :S], carry_s[0:1, S:]
        for s in range(SUBLANES - 1, -1, -1):
            e_s[s:s + 1, :S] = er
            e_s[s:s + 1, S:] = ei
            er, ei = fr[s:s + 1] + (pr * er + pi * ei), fi[s:s + 1] + (pr * ei - pi * er)
        carry_s[0:1, :S] = er
        carry_s[0:1, S:] = ei
        er, ei = e_s[:, :S][None], e_s[:, S:][None]
        pw = pwr_s[...]
        pwr, pwi = pw[:, :, :S], pw[:, :, S:]
        ll = dl_s[...]
        lam = jnp.concatenate([ll[:, :, :S] + (pwr * er + pwi * ei), ll[:, :, S:] + (pwr * ei - pwi * er)],
                              axis=-1).reshape(tc, 2 * S)
        l_re, l_im = lam[:, :S], lam[:, S:]
        da_ref[0:1, :S] += jnp.sum(l_re * xp[:, :S] + l_im * xp[:, S:], axis=0, keepdims=True)
        da_ref[0:1, S:] += jnp.sum(l_im * xp[:, :S] - l_re * xp[:, S:], axis=0, keepdims=True)
        lamb = lam.astype(BF16)
        dup_s[...] = lax.dot_general(lamb, b_ref[...], nn, preferred_element_type=F32)
        for s in range(SUBLANES):
            rows = pl.ds(seg * s, seg)
            du = _segment_rows(dup_s, s, seg) + d_ref[...] * dy_s[rows, :]
            du_ref[rows, :] = du.astype(du_ref.dtype)
        db_ref[...] += lax.dot_general(ubp, lamb, tn, preferred_element_type=F32)
        dd_ref[0:1, :] += jnp.sum(dy * u, axis=0, keepdims=True)

    rev = lambda b, t: (nt - 1 - t, b)
    return pl.pallas_call(
        body, name="s5_bwd",
        out_shape=(jax.ShapeDtypeStruct(dproj.shape, dproj.dtype),
                   jax.ShapeDtypeStruct((SSM_BLOCKS, LANES, 2 * S), F32),
                   jax.ShapeDtypeStruct((SSM_BLOCKS, LANES, 2 * S), F32),
                   jax.ShapeDtypeStruct((SSM_BLOCKS, SUBLANES, 2 * S), F32),
                   jax.ShapeDtypeStruct((SUBLANES, MAIN_WIDTH), F32)),
        input_output_aliases={9: 0},
        grid=(SSM_BLOCKS, nt),
        in_specs=[pl.BlockSpec((tc, LANES), rev),
                  pl.BlockSpec((tc, LANES), rev),
                  pl.BlockSpec((tc, LANES), rev),
                  pl.BlockSpec((tc, LANES), rev),
                  pl.BlockSpec((tc, 2 * S), rev),
                  pl.BlockSpec((None, LANES, 2 * S), lambda b, t: (b, 0, 0)),
                  pl.BlockSpec((None, 2 * S, LANES), lambda b, t: (b, 0, 0)),
                  pl.BlockSpec((None, SUBLANES, 2 * S), lambda b, t: (b, 0, 0)),
                  pl.BlockSpec((1, LANES), lambda b, t: (0, b)),
                  _ANY],
        out_specs=(pl.BlockSpec((tc, LANES), rev),
                   pl.BlockSpec((None, LANES, 2 * S), lambda b, t: (b, 0, 0)),
                   pl.BlockSpec((None, LANES, 2 * S), lambda b, t: (b, 0, 0)),
                   pl.BlockSpec((None, SUBLANES, 2 * S), lambda b, t: (b, 0, 0)),
                   pl.BlockSpec((SUBLANES, LANES), lambda b, t: (0, b))),
        scratch_shapes=[pltpu.VMEM((seg, SUBLANES, 2 * S), F32),
                        pltpu.VMEM((seg + 1, SUBLANES, 2 * S), F32),
                        pltpu.VMEM((seg, SUBLANES, 2 * S), F32),
                        pltpu.VMEM((SUBLANES, 2 * S), F32),
                        pltpu.VMEM((SUBLANES, 2 * S), F32),
                        pltpu.VMEM((tc, LANES), F32),
                        pltpu.VMEM((tc, LANES), F32),
                        pltpu.VMEM((tc, LANES), F32),
                        pltpu.VMEM((tc, LANES), F32)],
        compiler_params=_params("parallel", "arbitrary"),
    )(proj, dyg_a, dyg_b, y, xp, bmat, cmat, a_rows, d_skip.reshape(1, MAIN_WIDTH), dproj)


_Z_COLS = slice(MAIN_WIDTH, 2 * MAIN_WIDTH)
_ZM_COLS = slice(2 * MAIN_WIDTH + MEM_WIDTH, IN_WIDTH)


def _proj_rows(tr):
    return pl.BlockSpec((tr, IN_WIDTH), lambda i: (i, 0))


def _row_specs(tr):
    main = pl.BlockSpec((tr, MAIN_WIDTH), lambda i: (i, 0))
    z = pl.BlockSpec((tr, MAIN_WIDTH), lambda i: (i, 1))
    zm = pl.BlockSpec((tr, MEM_WIDTH), lambda i: (i, IN_WIDTH // MEM_WIDTH - 1))
    mem = pl.BlockSpec((tr, MEM_WIDTH), lambda i: (i, 0))
    cat = pl.BlockSpec((tr, D_MODEL), lambda i: (i, 0))
    vec = pl.BlockSpec((1, MAIN_WIDTH), lambda i: (0, 0))
    return main, z, zm, mem, cat, vec


def _gate_a_fwd(y, t, b_glu, proj, o_mem, *, tr=256):
    L = y.shape[0]
    tr = min(tr, L)

    def body(y_ref, t_ref, b_ref, z_ref, zm_ref, om_ref, o_ref):
        yg = _gelu(y_ref[...])
        sz, _ = _silu_and_grad(z_ref[...])
        o_ref[:, :MAIN_WIDTH] = (yg * _sigmoid(t_ref[...] + b_ref[...]) * sz).astype(BF16)
        szm, _ = _silu_and_grad(zm_ref[...])
        o_ref[:, MAIN_WIDTH:] = (om_ref[...] * szm).astype(BF16)

    main, z, zm, mem, cat, vec = _row_specs(tr)
    return pl.pallas_call(
        body, name="gate_a_fwd", out_shape=jax.ShapeDtypeStruct((L, D_MODEL), BF16),
        grid=(L // tr,), in_specs=[main, main, vec, z, zm, mem], out_specs=cat,
        compiler_params=_params("parallel"),
    )(y, t, b_glu.reshape(1, MAIN_WIDTH), proj, proj, o_mem)


def _gate_a_bwd(dcat, y, t, b_glu, proj, o_mem, *, tr=256):
    L = y.shape[0]
    tr = min(tr, L)

    def body(dc_ref, y_ref, t_ref, b_ref, z_ref, zm_ref, om_ref,
             dp_ref, dt_ref, dyg_ref, dom_ref, db_ref):
        dmain = dc_ref[:, :MAIN_WIDTH]
        dmemo = dc_ref[:, MAIN_WIDTH:]
        yg = _gelu(y_ref[...])
        sg = _sigmoid(t_ref[...] + b_ref[...])
        sz, gz = _silu_and_grad(z_ref[...])
        dp_ref[:, _Z_COLS] = (dmain * (yg * sg) * gz).astype(BF16)
        dy2 = dmain * sz
        dyg_ref[...] = dy2 * sg
        dt = dy2 * yg * (sg * (1.0 - sg))
        dt_ref[...] = dt.astype(BF16)

        @pl.when(pl.program_id(0) == 0)
        def _():
            db_ref[...] = jnp.zeros_like(db_ref)

        db_ref[...] += jnp.sum(dt, axis=0, keepdims=True)
        szm, gzm = _silu_and_grad(zm_ref[...])
        dom_ref[...] = dmemo * szm
        dp_ref[:, _ZM_COLS] = (dmemo * om_ref[...] * gzm).astype(BF16)

    main, z, zm, mem, cat, vec = _row_specs(tr)
    outs = pl.pallas_call(
        body, name="gate_a_bwd",
        out_shape=(jax.ShapeDtypeStruct((L, IN_WIDTH), BF16),
                   jax.ShapeDtypeStruct((L, MAIN_WIDTH), BF16), jax.ShapeDtypeStruct((L, MAIN_WIDTH), F32),
                   jax.ShapeDtypeStruct((L, MEM_WIDTH), F32), jax.ShapeDtypeStruct((1, MAIN_WIDTH), F32)),
        grid=(L // tr,), in_specs=[cat, main, main, vec, z, zm, mem],
        out_specs=(_proj_rows(tr), main, main, mem, vec),
        compiler_params=_params("arbitrary"),
    )(dcat, y, t, b_glu.reshape(1, MAIN_WIDTH), proj, proj, o_mem)
    return outs


def _gate_b_fwd(att, proj, o_mem, *, tr=256):
    L = att.shape[0]
    tr = min(tr, L)

    def body(a_ref, z_ref, zm_ref, om_ref, o_ref):
        sz, _ = _silu_and_grad(z_ref[...])
        o_ref[:, :MAIN_WIDTH] = (a_ref[...] * sz).astype(BF16)
        szm, _ = _silu_and_grad(zm_ref[...])
        o_ref[:, MAIN_WIDTH:] = (om_ref[...] * szm).astype(BF16)

    main, z, zm, mem, cat, _ = _row_specs(tr)
    return pl.pallas_call(
        body, name="gate_b_fwd", out_shape=jax.ShapeDtypeStruct((L, D_MODEL), BF16),
        grid=(L // tr,), in_specs=[main, z, zm, mem], out_specs=cat,
        compiler_params=_params("parallel"),
    )(att, proj, proj, o_mem)


def _gate_b_bwd(dcat, att, proj, o_mem, *, tr=256):
    L = att.shape[0]
    tr = min(tr, L)

    def body(dc_ref, a_ref, z_ref, zm_ref, om_ref, da_ref, dp_ref, dom_ref, dl_ref):
        dmain = dc_ref[:, :MAIN_WIDTH]
        dmemo = dc_ref[:, MAIN_WIDTH:]
        att = a_ref[...]
        sz, gz = _silu_and_grad(z_ref[...])
        datt = dmain * sz
        da_ref[...] = datt
        dp_ref[:, _Z_COLS] = (dmain * att * gz).astype(BF16)
        szm, gzm = _silu_and_grad(zm_ref[...])
        dom_ref[...] = dmemo * szm
        dp_ref[:, _ZM_COLS] = (dmemo * om_ref[...] * gzm).astype(BF16)
        prod = datt * att
        for h in range(FOX_HEADS):
            dl_ref[h] = jnp.sum(prod[:, h * HEAD_DIM:(h + 1) * HEAD_DIM], axis=1, keepdims=True)

    main, z, zm, mem, cat, _ = _row_specs(tr)
    delta = pl.BlockSpec((FOX_HEADS, tr, 1), lambda i: (0, i, 0))
    return pl.pallas_call(
        body, name="gate_b_bwd",
        out_shape=(jax.ShapeDtypeStruct((L, MAIN_WIDTH), F32), jax.ShapeDtypeStruct((L, IN_WIDTH), BF16),
                   jax.ShapeDtypeStruct((L, MEM_WIDTH), F32), jax.ShapeDtypeStruct((FOX_HEADS, L, 1), F32)),
        grid=(L // tr,), in_specs=[cat, main, z, zm, mem], out_specs=(main, _proj_rows(tr), mem, delta),
        compiler_params=_params("parallel"),
    )(dcat, att, proj, proj, o_mem)


_MEM_Q_COL = (2 * MAIN_WIDTH) // HEAD_DIM
_NT = (((1,), (1,)), ((), ()))
_TN = (((0,), (0,)), ((), ()))


def _mem_probs(q_ref, k_ref):
    qs = (q_ref[...] * (HEAD_DIM ** -0.5)).astype(BF16)
    s = lax.dot_general(qs, k_ref[...].astype(BF16), _NT, preferred_element_type=F32)
    e = jnp.exp(s - jnp.max(s, axis=-1, keepdims=True))
    return qs, e / jnp.sum(e, axis=-1, keepdims=True)


def _mem_attn_fwd(proj, kvm, *, tq=2048):
    L = proj.shape[0]
    tq = min(tq, L)

    def body(q_ref, k_ref, v_ref, o_ref):
        _, p = _mem_probs(q_ref, k_ref)
        o_ref[...] = jnp.dot(p.astype(BF16), v_ref[...].astype(BF16), preferred_element_type=F32)

    return pl.pallas_call(
        body, name="mem_attn_fwd", out_shape=jax.ShapeDtypeStruct((L, MEM_WIDTH), F32),
        grid=(MEM_HEADS, L // tq),
        in_specs=[pl.BlockSpec((tq, HEAD_DIM), lambda h, i: (i, _MEM_Q_COL + h)),
                  pl.BlockSpec((N_MEM, HEAD_DIM), lambda h, i: (0, h)),
                  pl.BlockSpec((N_MEM, HEAD_DIM), lambda h, i: (0, MEM_HEADS + h))],
        out_specs=pl.BlockSpec((tq, HEAD_DIM), lambda h, i: (i, h)),
        compiler_params=_params("parallel", "parallel"),
    )(proj, kvm, kvm)


def _mem_attn_bwd(proj, kvm, do, dproj, *, tq=2048):
    L = proj.shape[0]
    tq = min(tq, L)

    def body(q_ref, k_ref, v_ref, do_ref, dp_hbm, dq_ref, dk_ref, dv_ref):
        @pl.when(pl.program_id(1) == 0)
        def _():
            dk_ref[...] = jnp.zeros_like(dk_ref)
            dv_ref[...] = jnp.zeros_like(dv_ref)

        qs, p = _mem_probs(q_ref, k_ref)
        dob = do_ref[...].astype(BF16)
        dp = lax.dot_general(dob, v_ref[...].astype(BF16), _NT, preferred_element_type=F32)
        ds = p * (dp - jnp.sum(p * dp, axis=-1, keepdims=True))
        dsb = ds.astype(BF16)
        dq = jnp.dot(dsb, k_ref[...].astype(BF16), preferred_element_type=F32) * (HEAD_DIM ** -0.5)
        dq_ref[...] = dq.astype(BF16)
        dk_ref[...] += lax.dot_general(dsb, qs, _TN, preferred_element_type=F32)
        dv_ref[...] += lax.dot_general(p.astype(BF16), dob, _TN, preferred_element_type=F32)

    dproj, dk, dv = pl.pallas_call(
        body, name="mem_attn_bwd",
        out_shape=(jax.ShapeDtypeStruct(dproj.shape, dproj.dtype),
                   jax.ShapeDtypeStruct((N_MEM, MEM_WIDTH), F32),
                   jax.ShapeDtypeStruct((N_MEM, MEM_WIDTH), F32)),
        grid=(MEM_HEADS, L // tq),
        in_specs=[pl.BlockSpec((tq, HEAD_DIM), lambda h, i: (i, _MEM_Q_COL + h)),
                  pl.BlockSpec((N_MEM, HEAD_DIM), lambda h, i: (0, h)),
                  pl.BlockSpec((N_MEM, HEAD_DIM), lambda h, i: (0, MEM_HEADS + h)),
                  pl.BlockSpec((tq, HEAD_DIM), lambda h, i: (i, h)),
                  _ANY],
        out_specs=(pl.BlockSpec((tq, HEAD_DIM), lambda h, i: (i, _MEM_Q_COL + h)),
                   pl.BlockSpec((N_MEM, HEAD_DIM), lambda h, i: (0, h)),
                   pl.BlockSpec((N_MEM, HEAD_DIM), lambda h, i: (0, h))),
        input_output_aliases={4: 0},
        compiler_params=_params("parallel", "arbitrary"),
    )(proj, kvm, kvm, do, dproj)
    return dproj, jnp.concatenate([dk, dv], axis=1)


def _tile_cumsum(x, row, reverse):
    for sh in (1, 2, 4):
        if reverse:
            x = x + jnp.where(row < SUBLANES - sh, pltpu.roll(x, SUBLANES - sh, 0), 0.0)
        else:
            x = x + jnp.where(row >= sh, pltpu.roll(x, sh, 0), 0.0)
    return x


def _fgate_fwd(pre, b_pad):
    L = pre.shape[0]
    n8 = L // SUBLANES

    def body(p_ref, b_ref, o_ref):
        row = lax.broadcasted_iota(jnp.int32, (SUBLANES, LANES), 0)
        b = b_ref[...]

        def step(i, carry):
            x = p_ref[i] + b
            logf = jnp.minimum(x, 0.0) - jnp.log(1.0 + jnp.exp(-jnp.abs(x)))
            t = _tile_cumsum(logf, row, False) + carry
            o_ref[i] = t
            return t[SUBLANES - 1:SUBLANES, :]

        lax.fori_loop(0, n8, step, jnp.zeros((1, LANES), F32))

    out = pl.pallas_call(
        body, name="fgate_fwd", out_shape=jax.ShapeDtypeStruct((n8, SUBLANES, LANES), F32),
        compiler_params=_params(),
    )(pre.reshape(n8, SUBLANES, LANES), b_pad.reshape(1, LANES))
    return out.reshape(L, LANES)


def _fgate_bwd(dfcum, pre, b_pad):
    L = pre.shape[0]
    n8 = L // SUBLANES

    def body(d_ref, p_ref, b_ref, o_ref, s_ref):
        row = lax.broadcasted_iota(jnp.int32, (SUBLANES, LANES), 0)
        b = b_ref[...]

        def step(k, carry):
            c, acc = carry
            i = n8 - 1 - k
            t = _tile_cumsum(d_ref[i], row, True) + c
            dpre = t * _sigmoid(-(p_ref[i] + b))
            o_ref[i] = dpre
            return t[0:1, :], acc + dpre

        _, acc = lax.fori_loop(0, n8, step, (jnp.zeros((1, LANES), F32), jnp.zeros((SUBLANES, LANES), F32)))
        s_ref[...] = jnp.sum(acc, axis=0, keepdims=True)

    dpre, db = pl.pallas_call(
        body, name="fgate_bwd",
        out_shape=(jax.ShapeDtypeStruct((n8, SUBLANES, LANES), F32), jax.ShapeDtypeStruct((1, LANES), F32)),
        compiler_params=_params(),
    )(dfcum.reshape(n8, SUBLANES, LANES), pre.reshape(n8, SUBLANES, LANES), b_pad.reshape(1, LANES))
    return dpre.reshape(L, LANES), db


FOX_BLOCK = 512


def _fox_scores(qs, k, fk, diagonal):
    s = lax.dot_general(qs, k, _NT, preferred_element_type=F32) - fk
    if diagonal:
        row = lax.broadcasted_iota(jnp.int32, s.shape, 0)
        col = lax.broadcasted_iota(jnp.int32, s.shape, 1)
        s = jnp.where(row >= col, s, NEG_BIG)
    return s


def _fox_specs(tq, L):
    nq = L // tq
    return dict(
        rows=lambda off: pl.BlockSpec((tq, HEAD_DIM), lambda h, i: (i, off + h)),
        seq=lambda off: pl.BlockSpec((L, HEAD_DIM), lambda h, i: (0, off + h)),
        col=pl.BlockSpec((None, None, tq, 1), lambda h, i: (h, i, 0, 0)),
        col_all=pl.BlockSpec((None, nq, tq, 1), lambda h, i: (h, 0, 0, 0)),
        row=pl.BlockSpec((None, None, 1, tq), lambda h, i: (h, i, 0, 0)),
        row_all=pl.BlockSpec((None, nq, 1, tq), lambda h, i: (h, 0, 0, 0)))


FOX_FWD_HEADS = 2


def _fox_fwd(proj, kv, fk):
    L = proj.shape[0]
    tq = min(FOX_BLOCK, L)
    nq = L // tq
    nh = FOX_FWD_HEADS
    W = nh * HEAD_DIM

    def body(q_ref, k_ref, v_ref, fk_ref, o_ref, lse_ref, m_s, l_s, acc_s):
        qi = pl.program_id(1)
        cols = [slice(a * HEAD_DIM, (a + 1) * HEAD_DIM) for a in range(nh)]
        qs = [(q_ref[:, cs] * (HEAD_DIM ** -0.5)).astype(BF16) for cs in cols]
        m_s[...] = jnp.full_like(m_s, NEG_BIG)
        l_s[...] = jnp.zeros_like(l_s)
        acc_s[...] = jnp.zeros_like(acc_s)

        def block(j, diagonal):
            r0 = pl.multiple_of(j * tq, tq)
            for a, cs in enumerate(cols):
                s = _fox_scores(qs[a], k_ref[pl.ds(r0, tq), cs], fk_ref[a, j], diagonal)
                m_new = jnp.maximum(m_s[a], jnp.max(s, axis=-1, keepdims=True))
                alpha = jnp.exp(m_s[a] - m_new)
                p = jnp.exp(s - m_new)
                l_s[a] = alpha * l_s[a] + jnp.sum(p, axis=-1, keepdims=True)
                acc_s[a] = alpha * acc_s[a] + jnp.dot(p.astype(BF16), v_ref[pl.ds(r0, tq), cs],
                                                      preferred_element_type=F32)
                m_s[a] = m_new

        def below(j, carry):
            block(j, False)
            return carry

        lax.fori_loop(0, qi, below, 0)
        block(qi, True)
        for a, cs in enumerate(cols):
            o_ref[:, cs] = acc_s[a] / l_s[a]
            lse_ref[a] = m_s[a] + jnp.log(l_s[a])

    return pl.pallas_call(
        body, name="fox_fwd",
        out_shape=(jax.ShapeDtypeStruct((L, MAIN_WIDTH), F32),
                   jax.ShapeDtypeStruct((FOX_HEADS, nq, tq, 1), F32)),
        grid=(FOX_HEADS // nh, nq),
        in_specs=[pl.BlockSpec((tq, W), lambda h, i: (i, h)),
                  pl.BlockSpec((L, W), lambda h, i: (0, h)),
                  pl.BlockSpec((L, W), lambda h, i: (0, FOX_HEADS // nh + h)),
                  pl.BlockSpec((nh, nq, 1, tq), lambda h, i: (h, 0, 0, 0))],
        out_specs=(pl.BlockSpec((tq, W), lambda h, i: (i, h)),
                   pl.BlockSpec((nh, None, tq, 1), lambda h, i: (h, i, 0, 0))),
        scratch_shapes=[pltpu.VMEM((nh, tq, 1), F32), pltpu.VMEM((nh, tq, 1), F32),
                        pltpu.VMEM((nh, tq, HEAD_DIM), F32)],
        compiler_params=_params("parallel", "parallel"),
    )(proj, kv, kv, fk)


def _fox_bwd_dq(proj, kv, fk, lse, delta, datt, dproj):
    L = proj.shape[0]
    tq = min(FOX_BLOCK, L)
    nq = L // tq
    sp = _fox_specs(tq, L)

    def body(q_ref, k_ref, v_ref, fk_ref, lse_ref, dl_ref, do_ref, dp_hbm, dq_ref, df_ref, acc_s, df_s):
        qi = pl.program_id(1)
        qs = (q_ref[...] * (HEAD_DIM ** -0.5)).astype(BF16)
        dob = do_ref[...].astype(BF16)
        lse, dl = lse_ref[...], dl_ref[...]
        acc_s[...] = jnp.zeros_like(acc_s)
        df_s[...] = jnp.zeros_like(df_s)

        def block(j, diagonal):
            r0 = pl.multiple_of(j * tq, tq)
            k = k_ref[pl.ds(r0, tq), :]
            p = jnp.exp(_fox_scores(qs, k, fk_ref[j], diagonal) - lse)
            dp = lax.dot_general(dob, v_ref[pl.ds(r0, tq), :], _NT, preferred_element_type=F32)
            ds = p * (dp - dl)
            acc_s[...] += jnp.dot(ds.astype(BF16), k, preferred_element_type=F32)
            df_s[...] += jnp.sum(ds, axis=1, keepdims=True)

        def below(j, carry):
            block(j, False)
            return carry

        lax.fori_loop(0, qi, below, 0)
        block(qi, True)
        dq_ref[...] = (acc_s[...] * (HEAD_DIM ** -0.5)).astype(BF16)
        df_ref[...] = df_s[...]

    return pl.pallas_call(
        body, name="fox_bwd_dq",
        out_shape=(jax.ShapeDtypeStruct(dproj.shape, dproj.dtype),
                   jax.ShapeDtypeStruct((FOX_HEADS, nq, tq, 1), F32)),
        grid=(FOX_HEADS, nq),
        in_specs=[sp["rows"](0), sp["seq"](0), sp["seq"](FOX_HEADS), sp["row_all"],
                  sp["col"], sp["col"], sp["rows"](0), _ANY],
        out_specs=(sp["rows"](0), sp["col"]),
        input_output_aliases={7: 0},
        scratch_shapes=[pltpu.VMEM((tq, HEAD_DIM), F32), pltpu.VMEM((tq, 1), F32)],
        compiler_params=_params("parallel", "parallel"),
    )(proj, kv, kv, fk, lse, delta, datt, dproj)


def _fox_bwd_dkv(proj, kv, fk, lse, delta, datt):
    L = proj.shape[0]
    tq = min(FOX_BLOCK, L)
    nq = L // tq
    sp = _fox_specs(tq, L)

    def body(q_ref, k_ref, v_ref, fk_ref, lse_ref, dl_ref, do_ref,
             dk_ref, dv_ref, df_ref, dk_s, dv_s, df_s):
        ki = pl.program_id(1)
        k, v, fk = k_ref[...], v_ref[...], fk_ref[...]
        dk_s[...] = jnp.zeros_like(dk_s)
        dv_s[...] = jnp.zeros_like(dv_s)
        df_s[...] = jnp.zeros_like(df_s)

        def block(i, diagonal):
            r0 = pl.multiple_of(i * tq, tq)
            qs = (q_ref[pl.ds(r0, tq), :] * (HEAD_DIM ** -0.5)).astype(BF16)
            dob = do_ref[pl.ds(r0, tq), :].astype(BF16)
            p = jnp.exp(_fox_scores(qs, k, fk, diagonal) - lse_ref[i])
            dp = lax.dot_general(dob, v, _NT, preferred_element_type=F32)
            ds = p * (dp - dl_ref[i])
            dv_s[...] += lax.dot_general(p.astype(BF16), dob, _TN, preferred_element_type=F32)
            dk_s[...] += lax.dot_general(ds.astype(BF16), qs, _TN, preferred_element_type=F32)
            df_s[...] -= jnp.sum(ds, axis=0, keepdims=True)

        def above(i, carry):
            block(i, False)
            return carry

        block(ki, True)
        lax.fori_loop(ki + 1, nq, above, 0)
        dk_ref[...] = dk_s[...].astype(BF16)
        dv_ref[...] = dv_s[...].astype(BF16)
        df_ref[...] = df_s[...]

    return pl.pallas_call(
        body, name="fox_bwd_dkv",
        out_shape=(jax.ShapeDtypeStruct((L, MAIN_WIDTH), BF16),
                   jax.ShapeDtypeStruct((L, MAIN_WIDTH), BF16),
                   jax.ShapeDtypeStruct((FOX_HEADS, nq, 1, tq), F32)),
        grid=(FOX_HEADS, nq),
        in_specs=[sp["seq"](0), sp["rows"](0), sp["rows"](FOX_HEADS), sp["row"],
                  sp["col_all"], sp["col_all"], sp["seq"](0)],
        out_specs=(sp["rows"](0), sp["rows"](0), sp["row"]),
        scratch_shapes=[pltpu.VMEM((tq, HEAD_DIM), F32), pltpu.VMEM((tq, HEAD_DIM), F32),
                        pltpu.VMEM((1, tq), F32)],
        compiler_params=_params("parallel", "parallel"),
    )(proj, kv, kv, fk, lse, delta, datt)


def _pad_lanes(a):
    return jnp.pad(a, ((0, 0), (0, LANES - a.shape[1])))


def _mem_branch_fwd(memn, w_mk, proj, tag):
    kvm = _mm(memn, w_mk, name="mem_kv_" + tag)
    return kvm, _mem_attn_fwd(proj, kvm)


def _mem_branch_bwd(mem, g, w_mk, proj, memn, kvm, do_mem, dproj, tag):
    dproj, dkvm = _mem_attn_bwd(proj, kvm, do_mem, dproj)
    dkvm = dkvm.astype(BF16)
    dw_mk = _mm(memn, dkvm, ta=True, name="dw_mem_kv_" + tag, out_dtype=BF16)
    dmemn = _mm(dkvm, w_mk, tb=True, name="dmemn_" + tag)
    _, dg = _rmsnorm_bwd(mem, g, dmemn, name="mem_norm_bwd_" + tag, dx_dtype=BF16)
    return dproj, dw_mk, dg


def _local_step(x, mem, target, w, fetch=None, grads_ready=None):
    if grads_ready is None:
        grads_ready = lambda group, grads, token: token
    L = x.shape[0]
    g = {}
    w = dict(w)

    b_re_t = jnp.transpose(w["b_re"], (0, 2, 1))
    b_im_t = jnp.transpose(w["b_im"], (0, 2, 1))
    ar, ai, bbr_t, bbi_t = _s5_prep(w["lam_re"], w["lam_im"], w["log_step"], b_re_t, b_im_t)
    bmat, cmat = _s5_block_mats(bbr_t, bbi_t, w["c_re"], w["c_im"])
    a_rows = _s5_a_rows(ar, ai)

    hn0 = _rmsnorm_fwd(x, w["pre_norm_g"][0], name="pre_norm_0", out_dtype=BF16)
    memn0 = _rmsnorm_fwd(mem, w["mem_norm_g"][0], name="mem_norm_0", out_dtype=BF16)
    memn1 = _rmsnorm_fwd(mem, w["mem_norm_g"][1], name="mem_norm_1", out_dtype=BF16)
    if fetch is not None:
        w.update(fetch("a", [hn0, memn0, memn1, bmat, cmat, a_rows]))
    proj_a = _mm(hn0, w["w_in_a"], name="in_proj_a")
    y, yg, xp = _s5_fwd(proj_a, bmat, cmat, a_rows, w["d_skip"])
    if fetch is not None:
        w.update(fetch("b", yg))
    t = _mm(yg, w["w_glu"], name="glu_proj")
    kvm0, om0 = _mem_branch_fwd(memn0, w["w_mem_kv"][0], proj_a, "0")
    cat0 = _gate_a_fwd(y, t, w["b_glu"], proj_a, om0)
    o0 = _mm(cat0, w["w_out"][0], name="out_proj_0")
    h1 = _rmsnorm_fwd(o0, w["post_norm_g"][0], res=x, name="post_norm_0")

    kv_in = _rmsnorm_fwd(h1, w["kv_norm_g"], name="kv_norm", out_dtype=BF16)
    if fetch is not None:
        w.update(fetch("c", kv_in))
    kv = _mm(kv_in, w["w_kv"], name="kv_proj", out_dtype=BF16)
    pre_f = _mm(kv_in, w["w_fgate"], name="fgate_proj")
    b_f = jnp.pad(w["b_fgate"], (0, LANES - FOX_HEADS))
    fcum = _fgate_fwd(pre_f, b_f)
    fc = jnp.transpose(fcum[:, :FOX_HEADS])
    tq = min(FOX_BLOCK, L)
    fk = fc.reshape(FOX_HEADS, L // tq, 1, tq)

    hn1 = _rmsnorm_fwd(h1, w["pre_norm_g"][1], name="pre_norm_1", out_dtype=BF16)
    proj_b = _mm(hn1, w["w_in_b"], name="in_proj_b")
    att, lse = _fox_fwd(proj_b, kv, fk)
    kvm1, om1 = _mem_branch_fwd(memn1, w["w_mem_kv"][1], proj_b, "1")
    cat1 = _gate_b_fwd(att, proj_b, om1)
    o1 = _mm(cat1, w["w_out"][1], name="out_proj_1")
    dh2, loss_row = _final_norm_loss(o1, w["post_norm_g"][1], h1, target)

    do1, dpost1 = _rmsnorm_bwd(o1, w["post_norm_g"][1], dh2, name="post_norm_bwd_1", dx_dtype=BF16)
    dcat1 = _mm(do1, w["w_out"][1], tb=True, name="dcat_1", out_dtype=BF16)
    g["w_out_1"] = _mm(cat1, do1, ta=True, name="dw_out_1", out_dtype=BF16)
    datt, dproj_b, dom1, delta = _gate_b_bwd(dcat1, att, proj_b, om1)
    dproj_b, g["w_mem_kv_1"], dmemg1 = _mem_branch_bwd(mem, w["mem_norm_g"][1], w["w_mem_kv"][1], proj_b,
                                                      memn1, kvm1, dom1, dproj_b, "1")
    delta = delta.reshape(lse.shape)
    dproj_b, dfq = _fox_bwd_dq(proj_b, kv, fk, lse, delta, datt, dproj_b)
    dk, dv, dfk = _fox_bwd_dkv(proj_b, kv, fk, lse, delta, datt)
    g["w_in_b"] = _mm(hn1, dproj_b, ta=True, name="dw_in_b", out_dtype=BF16, shards=N_CHIPS)
    dhn1 = _mm(dproj_b, w["w_in_b"], tb=True, name="dhn_1")

    dkv = jnp.concatenate([dk, dv], axis=1)
    g["w_kv"] = _mm(kv_in, dkv, ta=True, name="dw_kv", out_dtype=BF16, shards=N_CHIPS)
    dkv_in_a = _mm(dkv, w["w_kv"], tb=True, name="dkv_in_kv")
    dfcum = _pad_lanes(jnp.transpose(dfq.reshape(FOX_HEADS, L) + dfk.reshape(FOX_HEADS, L)))
    dpre_f, db_f = _fgate_bwd(dfcum, pre_f, b_f)
    g["b_fgate"] = db_f[0, :FOX_HEADS]
    g["w_fgate"] = _mm(kv_in, dpre_f, ta=True, name="dw_fgate")[:, :FOX_HEADS]
    dkv_in_b = _mm(dpre_f, w["w_fgate"], tb=True, name="dkv_in_fgate")
    dh1, g["kv_norm_g"], dpre1 = _rmsnorm_bwd_pair(h1, w["kv_norm_g"], (dkv_in_a, dkv_in_b), w["pre_norm_g"][1],
                                                   dhn1, adds=(dh2,), name="kv_pre_norm_bwd")
    dh1 = grads_ready("b", g, dh1)

    do0, dpost0 = _rmsnorm_bwd(o0, w["post_norm_g"][0], dh1, name="post_norm_bwd_0", dx_dtype=BF16)
    dcat0 = _mm(do0, w["w_out"][0], tb=True, name="dcat_0", out_dtype=BF16)
    g["w_out_0"] = _mm(cat0, do0, ta=True, name="dw_out_0", out_dtype=BF16)
    dcat0 = grads_ready("b_send", g, dcat0)
    dproj_a, dt, dyg_a, dom0, db_glu = _gate_a_bwd(dcat0, y, t, w["b_glu"], proj_a, om0)
    g["b_glu"] = db_glu[0]
    g["w_glu"] = _mm(yg, dt, ta=True, name="dw_glu", out_dtype=BF16)
    dyg_b = _mm(dt, w["w_glu"], tb=True, name="dyg")
    dproj_a, g["w_mem_kv_0"], dmemg0 = _mem_branch_bwd(mem, w["mem_norm_g"][0], w["w_mem_kv"][0], proj_a,
                                                      memn0, kvm0, dom0, dproj_a, "0")
    dyg_b = grads_ready("a1", g, dyg_b)
    dproj_a, db_blk, dc_blk, da_rows, dd_skip = _s5_bwd(proj_a, dyg_a, dyg_b, y, xp, bmat, cmat, a_rows,
                                                        w["d_skip"], dproj_a)
    dproj_a = grads_ready("a1_send", g, dproj_a)
    g["d_skip"] = dd_skip[0]
    g["w_in_a"] = _mm(hn0, dproj_a, ta=True, name="dw_in_a", out_dtype=BF16, shards=N_CHIPS)
    dproj_a = grads_ready("a2", g, dproj_a)
    dhn0 = _mm(dproj_a, w["w_in_a"], tb=True, name="dhn_0")
    grad_x, dpre0 = _rmsnorm_bwd(x, w["pre_norm_g"][0], dhn0, adds=(dh1,), name="pre_norm_bwd_0")

    dbb = _s5_block_diag(db_blk)
    dcc = _s5_block_diag(dc_blk)
    g["c_re"], g["c_im"] = dcc[0], -dcc[1]
    d_ar = da_rows[:, 0, :STATE_COLS].reshape(SSM_GROUPS, SSM_STATE)
    d_ai = da_rows[:, 0, STATE_COLS:].reshape(SSM_GROUPS, SSM_STATE)
    dlr, dli, dls, dbr_t, dbi_t = _s5_prep_bwd(w["lam_re"], w["lam_im"], w["log_step"], b_re_t, b_im_t,
                                               d_ar, d_ai, dbb[0], dbb[1])
    g["lam_re"], g["lam_im"], g["log_step"] = dlr, dli, dls[:, 0]
    g["b_re"] = jnp.transpose(dbr_t, (0, 2, 1))
    g["b_im"] = jnp.transpose(dbi_t, (0, 2, 1))
    g["pre_norm_g"] = jnp.stack([dpre0, dpre1])
    g["post_norm_g"] = jnp.stack([dpost0, dpost1])
    g["mem_norm_g"] = jnp.stack([dmemg0, dmemg1])
    return loss_row, grad_x, g


_MESH = pl.DeviceIdType.MESH
_ANY = pl.BlockSpec(memory_space=pl.ANY)


def _place():
    x, y, c = lax.axis_index("x"), lax.axis_index("y"), lax.axis_index("c")
    chips = [(1 - x, y), (x, 1 - y), (1 - x, 1 - y)]
    return x, y, c, chips


_HBM = pl.BlockSpec(memory_space=pltpu.HBM)
_SEM = pl.BlockSpec(memory_space=pltpu.SEMAPHORE)
_SIDE = pltpu.SideEffectType.DATAFLOW_SIDE_EFFECTING


def _in_hbm(a):
    return pltpu.with_memory_space_constraint(a, pltpu.HBM)


def _hbm_like(a):
    return pltpu.HBM(a.shape, a.dtype)


def _ici_copies(srcs, lands, send_sem, recv_sem, src_at, dst_at, wait_at, to_sibling=False):
    x, y, c, chips = _place()
    peers = [(x, y, 1 - c)] if to_sibling else [(cx, cy, c) for cx, cy in chips]
    m = len(peers)
    start, wait = [], []
    for i in range(len(srcs)):
        for k, (px, py, pc) in enumerate(peers):
            sem = dict(send_sem=send_sem.at[m * i + k], recv_sem=recv_sem.at[m * i + k],
                       device_id=(px, py, pc), device_id_type=_MESH)
            src = src_at(srcs[i], 2 * px + py, c)
            start.append(pltpu.make_async_remote_copy(src_ref=src, dst_ref=dst_at(lands[i], 2 * x + y, k, c), **sem))
            wait.append(pltpu.make_async_remote_copy(src_ref=src, dst_ref=wait_at(lands[i], 2 * px + py, k, c), **sem))
    return start, wait


def _route_peers(route):
    return 1 if len(route) == 4 else 3


_BLOCK_ROUTE = (lambda s, j, c: s, lambda l, me, k, c: l.at[me, c], lambda l, j, k, c: l.at[j, c])


def _ici_start(srcs, lands, token, route, *, name):
    n = len(srcs)

    def body(*refs):
        start, _ = _ici_copies(refs[:n], refs[n:2 * n], refs[2 * n + 1], refs[2 * n + 2], *route)
        for cp in start:
            cp.start()

    sems = pltpu.SemaphoreType.DMA((_route_peers(route) * n,))
    outs = pl.pallas_call(
        body, name=name,
        out_shape=(sems, sems, *[_hbm_like(a) for a in srcs], *[_hbm_like(a) for a in lands], _hbm_like(token)),
        in_specs=[_HBM] * (2 * n + 1), out_specs=(_SEM, _SEM, *[_HBM] * (2 * n + 1)),
        input_output_aliases={i: 2 + i for i in range(2 * n + 1)},
        compiler_params=pltpu.CompilerParams(has_side_effects=_SIDE),
    )(*[_in_hbm(a) for a in srcs], *[_in_hbm(a) for a in lands], _in_hbm(token))
    return (outs[0], outs[1], list(outs[2:2 + n]), list(outs[2 + n:2 + 2 * n])), outs[2 + 2 * n]


def _ici_wait(handle, after, route, *, name):
    send_sem, recv_sem, srcs, lands = handle
    n = len(srcs)
    after = list(after) if isinstance(after, (list, tuple)) else [after]

    def body(*refs):
        _, wait = _ici_copies(refs[:n], refs[n:2 * n], refs[2 * n], refs[2 * n + 1], *route)
        for cp in wait:
            cp.wait_send()
            cp.wait_recv()

    outs = pl.pallas_call(
        body, name=name,
        out_shape=(*[_hbm_like(a) for a in srcs], *[_hbm_like(a) for a in lands]),
        in_specs=[_HBM] * (2 * n) + [_SEM, _SEM] + [_ANY] * len(after), out_specs=tuple([_HBM] * (2 * n)),
        input_output_aliases={i: i for i in range(2 * n)},
        compiler_params=pltpu.CompilerParams(has_side_effects=_SIDE),
    )(*srcs, *lands, send_sem, recv_sem, *after)
    return list(outs[:n]), list(outs[n:])


_GATHER_ROUTE = (lambda s, j, c: s.at[c], lambda l, me, k, c: l.at[me, c], lambda l, j, k, c: l.at[j, c])
_SCATTER_ROUTE = (lambda s, j, c: s.at[j], lambda l, me, k, c: l.at[k], lambda l, j, k, c: l.at[k])
_SWAP_ROUTE = (lambda s, j, c: s.at[:, 1 - c], lambda l, me, k, c: l, lambda l, j, k, c: l, True)


def _gather_forward(lands, tag, own=False):
    n = len(lands)
    m = 4 if own else 3

    def body(*refs):
        ins, outs = refs[:n], refs[n:2 * n]
        send_sem, recv_sem = refs[2 * n:]
        x, y, c, chips = _place()
        slots = [2 * cx + cy for cx, cy in chips] + [2 * x + y]

        def copy(i, k, half):
            return pltpu.make_async_remote_copy(
                src_ref=ins[i].at[slots[k], half], dst_ref=outs[i].at[slots[k], half],
                send_sem=send_sem.at[m * i + k], recv_sem=recv_sem.at[m * i + k],
                device_id=(x, y, 1 - c), device_id_type=_MESH)

        copies = [copy(i, k, c) for i in range(n) for k in range(m)]
        for cp in copies:
            cp.start()
        for i in range(n):
            for k in range(m):
                copy(i, k, 1 - c).wait_recv()
        for cp in copies:
            cp.wait_send()

    return pl.pallas_call(
        body, name="gather_forward_to_sibling_" + tag,
        out_shape=[jax.ShapeDtypeStruct(a.shape, a.dtype) for a in lands],
        in_specs=[_ANY] * n, out_specs=[_ANY] * n,
        input_output_aliases={i: i for i in range(n)},
        scratch_shapes=[pltpu.SemaphoreType.DMA((m * n,)), pltpu.SemaphoreType.DMA((m * n,))],
    )(*lands)


def _swap_halves(grads, tag):
    n = len(grads)

    def body(*refs):
        ins, outs = refs[:n], refs[n:2 * n]
        send_sem, recv_sem = refs[2 * n:]
        x, y, c, _ = _place()
        copies = [pltpu.make_async_remote_copy(
            src_ref=ins[i].at[:, 1 - c], dst_ref=outs[i],
            send_sem=send_sem.at[i], recv_sem=recv_sem.at[i],
            device_id=(x, y, 1 - c), device_id_type=_MESH) for i in range(n)]
        for cp in copies:
            cp.start()
        for cp in copies:
            cp.wait()

    return pl.pallas_call(
        body, name="grad_swap_halves_" + tag,
        out_shape=[jax.ShapeDtypeStruct((N_CHIPS,) + g.shape[2:], g.dtype) for g in grads],
        in_specs=[_ANY] * n, out_specs=[_ANY] * n,
        scratch_shapes=[pltpu.SemaphoreType.DMA((n,)), pltpu.SemaphoreType.DMA((n,))],
    )(*grads)


def _sum_rows(h, C):
    return max(d for d in range(SUBLANES, h + 1, SUBLANES) if h % d == 0 and d * C <= 1 << 20)


SUM_STEPS = 4


def _pair_sums(gs, rs, c_idx, *, name):
    n = len(gs)
    rows = [g.shape[2] // SUM_STEPS for g in gs]

    def body(c_ref, *refs):
        for g_ref, r_ref, o_ref in zip(refs[:n], refs[n:2 * n], refs[2 * n:]):
            o_ref[...] = (g_ref[...].astype(F32) + r_ref[...].astype(F32)).astype(o_ref.dtype)

    return pl.pallas_call(
        body, name=name,
        out_shape=[jax.ShapeDtypeStruct((N_CHIPS,) + g.shape[2:], g.dtype) for g in gs],
        grid_spec=pltpu.PrefetchScalarGridSpec(
            num_scalar_prefetch=1, grid=(N_CHIPS, SUM_STEPS),
            in_specs=[pl.BlockSpec((None, None, tr, g.shape[3]), lambda j, i, s: (j, s[0], i, 0))
                      for g, tr in zip(gs, rows)]
            + [pl.BlockSpec((None, tr, g.shape[3]), lambda j, i, s: (j, i, 0)) for g, tr in zip(gs, rows)],
            out_specs=[pl.BlockSpec((None, tr, g.shape[3]), lambda j, i, s: (j, i, 0)) for g, tr in zip(gs, rows)]),
        compiler_params=_params("parallel", "parallel"),
    )(c_idx, *gs, *rs)


def _owner_sums(ss, rs, jc_idx, *, name):
    n = len(ss)
    rows = [s.shape[1] // SUM_STEPS for s in ss]

    def body(jc_ref, *refs):
        for s_ref, r_ref, o_ref in zip(refs[:n], refs[n:2 * n], refs[2 * n:]):
            acc = s_ref[...].astype(F32)
            for k in range(3):
                acc = acc + r_ref[k].astype(F32)
            o_ref[...] = acc

    return pl.pallas_call(
        body, name=name,
        out_shape=[jax.ShapeDtypeStruct((2,) + s.shape[1:], F32) for s in ss],
        grid_spec=pltpu.PrefetchScalarGridSpec(
            num_scalar_prefetch=1, grid=(SUM_STEPS,),
            in_specs=[pl.BlockSpec((None, tr, s.shape[2]), lambda i, p: (p[0], i, 0)) for s, tr in zip(ss, rows)]
            + [pl.BlockSpec((3, tr, s.shape[2]), lambda i, p: (0, i, 0)) for s, tr in zip(ss, rows)],
            out_specs=[pl.BlockSpec((None, tr, s.shape[2]), lambda i, p: (p[1], i, 0)) for s, tr in zip(ss, rows)]),
        compiler_params=_params("parallel"),
    )(jc_idx, *ss, *rs)


def _share_with_sibling(bufs, tag):
    n = len(bufs)

    def body(*refs):
        ins, outs = refs[:n], refs[n:2 * n]
        send_sem, recv_sem = refs[2 * n:]
        x, y, c, _ = _place()

        def copy(i, half):
            return pltpu.make_async_remote_copy(
                src_ref=ins[i].at[half], dst_ref=outs[i].at[half],
                send_sem=send_sem.at[i], recv_sem=recv_sem.at[i],
                device_id=(x, y, 1 - c), device_id_type=_MESH)

        copies = [copy(i, c) for i in range(n)]
        for cp in copies:
            cp.start()
        for i in range(n):
            copy(i, 1 - c).wait_recv()
        for cp in copies:
            cp.wait_send()

    return pl.pallas_call(
        body, name="grad_share_with_sibling_" + tag,
        out_shape=[jax.ShapeDtypeStruct(b.shape, b.dtype) for b in bufs],
        in_specs=[_ANY] * n, out_specs=[_ANY] * n,
        input_output_aliases={i: i for i in range(n)},
        scratch_shapes=[pltpu.SemaphoreType.DMA((n,)), pltpu.SemaphoreType.DMA((n,))],
    )(*bufs)


def _chip_sums(grads, c_idx, tag):
    views = [g.reshape(N_CHIPS, 2, g.shape[1] // 2, g.shape[2]) for g in grads]
    arrived = _swap_halves(views, tag)
    return _pair_sums(views, arrived, c_idx, name=f"grad_pair_sums_{tag}")


def _owner_totals(sums, arrived, jc_idx, tag):
    halves = _owner_sums(sums, arrived, jc_idx, name=f"grad_owner_sums_{tag}")
    return [f.reshape(-1, f.shape[2]) for f in _share_with_sibling(halves, tag)]


def _sum_devices(blocks):
    R = blocks.shape[2]
    tr = _sum_rows(R, 2 * N_CHIPS * LANES)

    def body(b_ref, o_ref):
        acc = b_ref[0, 0]
        for d in range(1, 2 * N_CHIPS):
            acc = acc + b_ref[d // 2, d % 2]
        o_ref[...] = acc

    return pl.pallas_call(
        body, name="sum_small_over_devices", out_shape=jax.ShapeDtypeStruct((R, LANES), F32),
        grid=(R // tr,),
        in_specs=[pl.BlockSpec((N_CHIPS, 2, tr, LANES), lambda i: (0, 0, i, 0))],
        out_specs=pl.BlockSpec((tr, LANES), lambda i: (i, 0)),
        compiler_params=_params("parallel"),
    )(blocks)


def _adamw(w, g, m, v, *, name):
    R, C = w.shape
    whole_fits = 7 * 2 * R * C * 4 <= VMEM_LIMIT_BYTES // 2
    tr = R if whole_fits else next(c for c in (256, 192, 128, 64, 32, 16, 8) if R % c == 0)

    def body(w_ref, g_ref, m_ref, v_ref, d_ref, nm_ref, nv_ref):
        g = g_ref[...]
        m = ADAM_B1 * m_ref[...] + (1.0 - ADAM_B1) * g
        v = ADAM_B2 * v_ref[...] + (1.0 - ADAM_B2) * (g * g)
        nm_ref[...] = m
        nv_ref[...] = v
        m_hat = m / (1.0 - ADAM_B1 ** ADAM_STEP)
        v_hat = v / (1.0 - ADAM_B2 ** ADAM_STEP)
        d_ref[...] = -ADAM_LR * (m_hat / (jnp.sqrt(v_hat) + ADAM_EPS) + ADAM_WD * w_ref[...])

    blk = pl.BlockSpec((tr, C), lambda i: (i, 0))
    sds = jax.ShapeDtypeStruct((R, C), F32)
    return pl.pallas_call(
        body, name=name, out_shape=(sds, sds, sds), grid=(R // tr,),
        in_specs=[blk] * 4, out_specs=(blk, blk, blk),
        compiler_params=_params("parallel"),
    )(w, g, m, v)


_TILE = SUBLANES * LANES


def _pack(arrays):
    rows = []
    for a in arrays:
        flat = a.reshape(-1)
        flat = jnp.pad(flat, (0, (-flat.shape[0]) % _TILE))
        rows.append(flat.reshape(-1, LANES))
    return jnp.concatenate(rows, axis=0)


def _unpack(buf, shapes):
    out, r = [], 0
    for s in shapes:
        size = math.prod(s)
        nr = -(-size // _TILE) * SUBLANES
        out.append(buf[r:r + nr].reshape(-1)[:size].reshape(s))
        r += nr
    return out


_BIG = ("w_in_a", "w_glu", "w_kv", "w_in_b", "w_mem_kv", "w_out")
_REPLICATED = ("pre_norm_g", "post_norm_g", "lam_re", "lam_im", "log_step", "b_re", "b_im", "c_re", "c_im",
               "kv_norm_g", "b_fgate", "mem_norm_g")
_SHARDED_SMALL = ("d_skip", "b_glu", "w_fgate")
_WEIGHTS = ("pre_norm_g", "post_norm_g", "w_in_a", "lam_re", "lam_im", "log_step", "b_re", "b_im", "c_re",
            "c_im", "d_skip", "w_glu", "b_glu", "kv_norm_g", "w_kv", "w_fgate", "b_fgate", "w_in_b",
            "mem_norm_g", "w_mem_kv", "w_out")


def _halves(a):
    return a.reshape(2, a.shape[0] // 2, a.shape[1])


def _unhalve(a):
    return a.reshape(N_CHIPS, 2 * a.shape[2], a.shape[3])


def _columns(a):
    return jnp.transpose(a, (1, 0, 2)).reshape(a.shape[1], N_CHIPS * a.shape[2])


def kernel(x, mem, pre_norm_g, post_norm_g, w_in_a, lam_re, lam_im, log_step, b_re, b_im, c_re, c_im, d_skip, w_glu, b_glu, kv_norm_g, w_kv, w_fgate, b_fgate, w_in_b, mem_norm_g, w_mem_kv, w_out, loss_target, m_pre_norm_g, m_post_norm_g, m_w_in_a, m_lam_re, m_lam_im, m_log_step, m_b_re, m_b_im, m_c_re, m_c_im, m_d_skip, m_w_glu, m_b_glu, m_kv_norm_g, m_w_kv, m_w_fgate, m_b_fgate, m_w_in_b, m_mem_norm_g, m_w_mem_kv, m_w_out, v_pre_norm_g, v_post_norm_g, v_w_in_a, v_lam_re, v_lam_im, v_log_step, v_b_re, v_b_im, v_c_re, v_c_im, v_d_skip, v_w_glu, v_b_glu, v_kv_norm_g, v_w_kv, v_w_fgate, v_b_fgate, v_w_in_b, v_mem_norm_g, v_w_mem_kv, v_w_out):
    a = dict(locals())
    xi, yi, ci = lax.axis_index("x"), lax.axis_index("y"), lax.axis_index("c")
    chip = 2 * xi + yi
    c_idx = jnp.reshape(ci, (1,)).astype(jnp.int32)
    jc_idx = jnp.stack([chip, ci]).astype(jnp.int32)

    vec = jnp.zeros((2 * SUBLANES, MAIN_WIDTH // N_CHIPS), F32)
    vec = vec.at[0].set(a["d_skip"][0]).at[1].set(a["b_glu"][0])
    def own_slot(gathered, parts):
        return [lax.dynamic_update_index_in_dim(g, p, chip, 0) for g, p in zip(gathered, parts)]

    parts_a = [_halves(a["w_in_a"][0].astype(BF16)), _halves(vec)]
    parts_b = [_halves(a["w_glu"][0].astype(BF16)), _halves(a["w_mem_kv"].reshape(-1, 2 * MEM_WIDTH).astype(BF16)),
               _halves(a["w_out"].reshape(-1, D_MODEL).astype(BF16))]
    parts_c = [_halves(a["w_kv"].astype(BF16)), _halves(_pad_lanes(a["w_fgate"]).astype(BF16)),
               _halves(a["w_in_b"][0].astype(BF16))]
    travelling, token = {}, a["pre_norm_g"]
    for tag, parts in (("a", parts_a), ("b", parts_b), ("c", parts_c)):
        lands = [lax.empty((N_CHIPS,) + p.shape, p.dtype) for p in parts]
        travelling[tag], token = _ici_start(parts, lands, token, _GATHER_ROUTE, name=f"gather_{tag}_start")

    def fetch(tag, after):
        parts, lands = _ici_wait(travelling[tag], after, _GATHER_ROUTE, name=f"gather_{tag}_wait")
        full = own_slot(_gather_forward(lands, tag), parts)
        if tag == "a":
            w_in_a, vecs = full
            return dict(w_in_a=_columns(_unhalve(w_in_a)), d_skip=vecs[:, 0, 0, :].reshape(MAIN_WIDTH),
                        b_glu=vecs[:, 0, 1, :].reshape(MAIN_WIDTH))
        if tag == "b":
            w_glu, w_mk, w_out = full
            return dict(w_glu=w_glu.reshape(MAIN_WIDTH, MAIN_WIDTH),
                        w_mem_kv=[w_mk[:, i].reshape(D_MODEL, 2 * MEM_WIDTH) for i in range(2)],
                        w_out=[w_out[:, i].reshape(D_MODEL, D_MODEL) for i in range(2)])
        w_kv, w_fg, w_in_b = full
        return dict(w_kv=_columns(_unhalve(w_kv)), w_fgate=w_fg.reshape(D_MODEL, LANES),
                    w_in_b=_columns(_unhalve(w_in_b)))

    w = dict(
        pre_norm_g=token, post_norm_g=a["post_norm_g"], mem_norm_g=a["mem_norm_g"],
        kv_norm_g=a["kv_norm_g"], b_fgate=a["b_fgate"],
        lam_re=a["lam_re"][0], lam_im=a["lam_im"][0], log_step=a["log_step"][0],
        b_re=a["b_re"][0], b_im=a["b_im"][0], c_re=a["c_re"][0], c_im=a["c_im"][0])

    sent = {}

    swapping = {}

    def grads_ready(event, g, token):
        tag = event.split("_")[0]
        if event in ("b", "a1"):
            big = {"b": lambda: [g["w_kv"], g["w_in_b"], g["w_mem_kv_1"].reshape(N_CHIPS, -1, 2 * MEM_WIDTH),
                                 g["w_out_1"].reshape(N_CHIPS, -1, D_MODEL)],
                   "a1": lambda: [g["w_glu"].reshape(N_CHIPS, -1, MAIN_WIDTH),
                                  g["w_mem_kv_0"].reshape(N_CHIPS, -1, 2 * MEM_WIDTH),
                                  g["w_out_0"].reshape(N_CHIPS, -1, D_MODEL)]}[tag]()
            views = [b.reshape(N_CHIPS, 2, b.shape[1] // 2, b.shape[2]) for b in big]
            lands = [lax.empty((N_CHIPS,) + v.shape[2:], v.dtype) for v in views]
            swapping[tag], token = _ici_start(views, lands, token, _SWAP_ROUTE, name=f"grad_swap_{tag}_start")
            return token
        if event == "a2":
            sums = _chip_sums([g["w_in_a"]], c_idx, tag)
        else:
            views, arrived = _ici_wait(swapping[tag], token, _SWAP_ROUTE, name=f"grad_swap_{tag}_wait")
            sums = _pair_sums(views, arrived, c_idx, name=f"grad_pair_sums_{tag}")
        lands = [lax.empty((3,) + s.shape[1:], s.dtype) for s in sums]
        sent[tag], token = _ici_start(sums, lands, token, _SCATTER_ROUTE, name=f"grad_send_{tag}_start")
        return token

    loss_row, grad_x, g = _local_step(a["x"][0], a["mem"][0], a["loss_target"][0], w, fetch, grads_ready)

    small_names = _REPLICATED + _SHARDED_SMALL
    pack = _pack([g[n] for n in small_names])
    blocks = lax.empty((N_CHIPS, 2) + pack.shape, F32)
    small_sent, loss_row = _ici_start([pack], [blocks], loss_row, _BLOCK_ROUTE, name="small_sums_start")
    loss = lax.psum(jnp.sum(loss_row), MESH_AXES)

    def totals(tag, after):
        sums, arrived = _ici_wait(sent[tag], after, _SCATTER_ROUTE, name=f"grad_send_{tag}_wait")
        return _owner_totals(sums, arrived, jc_idx, tag)

    r_kv, r_in_b, r_mk1, r_out1 = totals("b", grad_x)
    r_glu, r_mk0, r_out0 = totals("a1", r_out1)
    (r_in_a,) = totals("a2", r_out0)
    grads = {"w_in_a": r_in_a[None], "w_glu": r_glu[None], "w_kv": r_kv, "w_in_b": r_in_b[None],
             "w_mem_kv": jnp.stack([r_mk0, r_mk1]), "w_out": jnp.stack([r_out0, r_out1])}

    delta, new_m, new_v = {}, {}, {}
    for n in _BIG:
        shape = a[n].shape
        d2 = (-1, shape[-1])
        d, m, v = _adamw(a[n].reshape(d2), grads[n].reshape(d2), a["m_" + n].reshape(d2),
                         a["v_" + n].reshape(d2), name="adamw_" + n)
        delta[n], new_m[n], new_v[n] = d.reshape(shape), m.reshape(shape), v.reshape(shape)

    (pack,), (blocks,) = _ici_wait(small_sent, [delta[n] for n in _BIG], _BLOCK_ROUTE, name="small_sums_wait")
    blocks = lax.dynamic_update_slice(blocks, pack[None, None], (chip, ci, 0, 0))
    (blocks,) = _gather_forward([blocks], "small", own=True)
    small = dict(zip(small_names, _unpack(_sum_devices(blocks), [g[n].shape for n in small_names])))
    for n in _REPLICATED:
        grads[n] = small[n].reshape(a[n].shape)
    nd = MAIN_WIDTH // N_CHIPS
    grads["d_skip"] = lax.dynamic_slice(small["d_skip"], (chip * nd,), (nd,))[None]
    grads["b_glu"] = lax.dynamic_slice(small["b_glu"], (chip * nd,), (nd,))[None]
    nf = D_MODEL // N_CHIPS
    grads["w_fgate"] = lax.dynamic_slice(small["w_fgate"], (chip * nf, 0), (nf, FOX_HEADS))

    shapes = [a[n].shape for n in small_names]
    d, m, v = _adamw(_pack([a[n] for n in small_names]), _pack([grads[n] for n in small_names]),
                     _pack([a["m_" + n] for n in small_names]), _pack([a["v_" + n] for n in small_names]),
                     name="adamw_small")
    for n, dd, mm, vv in zip(small_names, _unpack(d, shapes), _unpack(m, shapes), _unpack(v, shapes)):
        delta[n], new_m[n], new_v[n] = dd, mm, vv

    return (loss, grad_x[None], *[grads[n] for n in _WEIGHTS], *[delta[n] for n in _WEIGHTS],
            *[new_m[n] for n in _WEIGHTS], *[new_v[n] for n in _WEIGHTS])
```

```python
import math

import jax
import jax.numpy as jnp
from jax import lax
from jax.experimental import pallas as pl
from jax.experimental.pallas import tpu as pltpu

F32 = jnp.float32
BF16 = jnp.bfloat16

D_MODEL = 2048
N_MEM = 256
MAIN_WIDTH = 1536
MEM_WIDTH = 512
IN_WIDTH = 2 * MAIN_WIDTH + 2 * MEM_WIDTH
HEAD_DIM = 128
FOX_HEADS = MAIN_WIDTH // HEAD_DIM
MEM_HEADS = MEM_WIDTH // HEAD_DIM
SSM_GROUP = 16
SSM_GROUPS = MAIN_WIDTH // SSM_GROUP
SSM_STATE = 64
GROUPS_PER_BLOCK = 8
SSM_BLOCKS = SSM_GROUPS // GROUPS_PER_BLOCK
STATE_COLS = GROUPS_PER_BLOCK * SSM_STATE
EPS = 1e-6
ADAM_LR = 0.001
ADAM_B1 = 0.9
ADAM_B2 = 0.999
ADAM_EPS = 1e-08
ADAM_WD = 0.01
ADAM_STEP = 10
N_CHIPS = 4
LANES = 128
SUBLANES = 8
VMEM_LIMIT_BYTES = 56 * 1024 * 1024
NEG_BIG = -1e30
MESH_AXES = ("x", "y", "c")


def _params(*sem):
    return pltpu.CompilerParams(dimension_semantics=sem if sem else None,
                                vmem_limit_bytes=VMEM_LIMIT_BYTES)


def _sigmoid(x):
    return 1.0 / (1.0 + jnp.exp(-x))


def _gelu(x):
    c = math.sqrt(2.0 / math.pi)
    return 0.5 * x * (1.0 + jnp.tanh(c * (x + 0.044715 * (x * x * x))))


def _gelu_grad(x):
    c = math.sqrt(2.0 / math.pi)
    t = jnp.tanh(c * (x + 0.044715 * (x * x * x)))
    return 0.5 * (1.0 + t) + 0.5 * x * (1.0 - t * t) * (c * (1.0 + 3.0 * 0.044715 * (x * x)))


def _silu_and_grad(z):
    s = _sigmoid(z)
    return z * s, s * (1.0 + z * (1.0 - s))


_TILE_CHOICES = (4096, 3072, 2048, 1536, 1024, 768, 512, 384, 256, LANES)


def _tile(n, cap):
    return next(c for c in _TILE_CHOICES if c <= cap and n % c == 0)


def _mm(a, b, *, name, ta=False, tb=False, out_dtype=F32, shards=1, tm=1024, tn=1024, tk=4096):
    if ta:
        K, M = a.shape
    else:
        M, K = a.shape
    if tb:
        N, kb = b.shape
    else:
        kb, N = b.shape
    assert K == kb, (a.shape, b.shape)
    ns = N // shards
    tm, tn, tk = _tile(M, tm), _tile(ns, tn), _tile(K, tk)
    assert M % tm == 0 and ns % tn == 0 and K % tk == 0 and N % shards == 0
    nk = K // tk
    dn = (((0 if ta else 1,), (1 if tb else 0,)), ((), ()))

    def body(a_ref, b_ref, o_ref, *acc):
        prod = lax.dot_general(a_ref[...].astype(BF16), b_ref[...].astype(BF16), dn, preferred_element_type=F32)
        if nk == 1:
            o_ref[...] = prod.astype(o_ref.dtype)
            return
        acc_ref, = acc
        k = pl.program_id(2)

        @pl.when(k == 0)
        def _():
            acc_ref[...] = jnp.zeros_like(acc_ref)

        acc_ref[...] += prod

        @pl.when(k == nk - 1)
        def _():
            o_ref[...] = acc_ref[...].astype(o_ref.dtype)

    a_spec = (pl.BlockSpec((tk, tm), lambda i, j, k: (k, i)) if ta
              else pl.BlockSpec((tm, tk), lambda i, j, k: (i, k)))
    b_spec = (pl.BlockSpec((tn, tk), lambda i, j, k: (j, k)) if tb
              else pl.BlockSpec((tk, tn), lambda i, j, k: (k, j)))
    if shards == 1:
        out_shape = jax.ShapeDtypeStruct((M, N), out_dtype)
        o_spec = pl.BlockSpec((tm, tn), lambda i, j, k: (i, j))
    else:
        nb = ns // tn
        out_shape = jax.ShapeDtypeStruct((shards, M, ns), out_dtype)
        o_spec = pl.BlockSpec((None, tm, tn), lambda i, j, k: (j // nb, i, j % nb))
    return pl.pallas_call(
        body, name=name, out_shape=out_shape,
        grid=(M // tm, N // tn, nk),
        in_specs=[a_spec, b_spec], out_specs=o_spec,
        scratch_shapes=[] if nk == 1 else [pltpu.VMEM((tm, tn), F32)],
        compiler_params=_params("parallel", "parallel", "arbitrary"),
    )(a, b)


def _rmsnorm_fwd(x, g, *, name, res=None, out_dtype=F32, tr=256):
    L, D = x.shape
    tr = min(tr, L)
    has_res = res is not None

    def body(*refs):
        if has_res:
            x_ref, g_ref, r_ref, o_ref = refs
        else:
            x_ref, g_ref, o_ref = refs
        xf = x_ref[...]
        r = lax.rsqrt(jnp.mean(xf * xf, axis=-1, keepdims=True) + EPS)
        y = xf * r * g_ref[...]
        if has_res:
            y = r_ref[...] + y
        o_ref[...] = y.astype(o_ref.dtype)

    row = pl.BlockSpec((tr, D), lambda i: (i, 0))
    vec = pl.BlockSpec((1, D), lambda i: (0, 0))
    ins = [x, g.reshape(1, D)] + ([res] if has_res else [])
    return pl.pallas_call(
        body, name=name, out_shape=jax.ShapeDtypeStruct((L, D), out_dtype),
        grid=(L // tr,), in_specs=[row, vec] + ([row] if has_res else []), out_specs=row,
        compiler_params=_params("parallel"),
    )(*ins)


def _rmsnorm_bwd(x, g, dy, *, name, adds=(), dx_dtype=F32, tr=256):
    L, D = x.shape
    tr = min(tr, L)
    dys = dy if isinstance(dy, tuple) else (dy,)
    n_dy, n_add = len(dys), len(adds)

    def body(*refs):
        x_ref, g_ref = refs[:2]
        dy_refs = refs[2:2 + n_dy]
        add_refs = refs[2 + n_dy:2 + n_dy + n_add]
        dx_ref, dg_ref = refs[2 + n_dy + n_add:]
        xf = x_ref[...]
        dyf = dy_refs[0][...].astype(F32)
        for d_ref in dy_refs[1:]:
            dyf = dyf + d_ref[...].astype(F32)
        r = lax.rsqrt(jnp.mean(xf * xf, axis=-1, keepdims=True) + EPS)
        gy = dyf * g_ref[...]
        c = jnp.mean(xf * gy, axis=-1, keepdims=True) * (r * r * r)
        dx = gy * r - xf * c
        for a_ref in add_refs:
            dx = dx + a_ref[...].astype(F32)
        dx_ref[...] = dx.astype(dx_ref.dtype)

        @pl.when(pl.program_id(0) == 0)
        def _():
            dg_ref[...] = jnp.zeros_like(dg_ref)

        dg_ref[...] += jnp.sum(dyf * xf * r, axis=0, keepdims=True)

    row = pl.BlockSpec((tr, D), lambda i: (i, 0))
    vec = pl.BlockSpec((1, D), lambda i: (0, 0))
    dx, dg = pl.pallas_call(
        body, name=name,
        out_shape=(jax.ShapeDtypeStruct((L, D), dx_dtype), jax.ShapeDtypeStruct((1, D), F32)),
        grid=(L // tr,), in_specs=[row, vec] + [row] * (n_dy + n_add), out_specs=(row, vec),
        compiler_params=_params("arbitrary"),
    )(x, g.reshape(1, D), *dys, *adds)
    return dx, dg.reshape(D)


def _rmsnorm_bwd_pair(x, g1, dy1, g2, dy2, *, name, adds=(), tr=256):
    L, D = x.shape
    tr = min(tr, L)
    dy1s = dy1 if isinstance(dy1, tuple) else (dy1,)
    n1, n_add = len(dy1s), len(adds)

    def body(*refs):
        x_ref, g1_ref, g2_ref = refs[:3]
        dy1_refs = refs[3:3 + n1]
        dy2_ref = refs[3 + n1]
        add_refs = refs[4 + n1:4 + n1 + n_add]
        dx_ref, dg1_ref, dg2_ref = refs[4 + n1 + n_add:]
        xf = x_ref[...]
        d1 = dy1_refs[0][...].astype(F32)
        for d_ref in dy1_refs[1:]:
            d1 = d1 + d_ref[...].astype(F32)
        d2 = dy2_ref[...].astype(F32)
        r = lax.rsqrt(jnp.mean(xf * xf, axis=-1, keepdims=True) + EPS)
        gy = d1 * g1_ref[...] + d2 * g2_ref[...]
        c = jnp.mean(xf * gy, axis=-1, keepdims=True) * (r * r * r)
        dx = gy * r - xf * c
        for a_ref in add_refs:
            dx = dx + a_ref[...].astype(F32)
        dx_ref[...] = dx

        @pl.when(pl.program_id(0) == 0)
        def _():
            dg1_ref[...] = jnp.zeros_like(dg1_ref)
            dg2_ref[...] = jnp.zeros_like(dg2_ref)

        xr = xf * r
        dg1_ref[...] += jnp.sum(d1 * xr, axis=0, keepdims=True)
        dg2_ref[...] += jnp.sum(d2 * xr, axis=0, keepdims=True)

    row = pl.BlockSpec((tr, D), lambda i: (i, 0))
    vec = pl.BlockSpec((1, D), lambda i: (0, 0))
    dx, dg1, dg2 = pl.pallas_call(
        body, name=name,
        out_shape=(jax.ShapeDtypeStruct((L, D), F32), jax.ShapeDtypeStruct((1, D), F32),
                   jax.ShapeDtypeStruct((1, D), F32)),
        grid=(L // tr,), in_specs=[row, vec, vec] + [row] * (n1 + 1 + n_add), out_specs=(row, vec, vec),
        compiler_params=_params("arbitrary"),
    )(x, g1.reshape(1, D), g2.reshape(1, D), *dy1s, dy2, *adds)
    return dx, dg1.reshape(D), dg2.reshape(D)


def _final_norm_loss(o, g, res, target, *, tr=256):
    L, D = o.shape
    tr = min(tr, L)

    def body(o_ref, g_ref, r_ref, t_ref, dh_ref, loss_ref):
        xf = o_ref[...]
        r = lax.rsqrt(jnp.mean(xf * xf, axis=-1, keepdims=True) + EPS)
        e = (r_ref[...] + xf * r * g_ref[...]) - t_ref[...]
        dh_ref[...] = e * (1.0 / D)

        @pl.when(pl.program_id(0) == 0)
        def _():
            loss_ref[...] = jnp.zeros_like(loss_ref)

        loss_ref[...] += jnp.sum(e * e, axis=0, keepdims=True) * (0.5 / D)

    row = pl.BlockSpec((tr, D), lambda i: (i, 0))
    vec = pl.BlockSpec((1, D), lambda i: (0, 0))
    dh, lp = pl.pallas_call(
        body, name="post_norm_1_loss",
        out_shape=(jax.ShapeDtypeStruct((L, D), F32), jax.ShapeDtypeStruct((1, D), F32)),
        grid=(L // tr,), in_specs=[row, vec, row, row], out_specs=(row, vec),
        compiler_params=_params("arbitrary"),
    )(o, g.reshape(1, D), res, target)
    return dh, lp


def _s5_coeffs(lr, li, ls):
    dt = jnp.exp(ls)
    mag = jnp.exp(lr * dt)
    ar = mag * jnp.cos(li * dt)
    ai = mag * jnp.sin(li * dt)
    den = lr * lr + li * li
    cr = ((ar - 1.0) * lr + ai * li) / den
    ci = (ai * lr - (ar - 1.0) * li) / den
    return dt, ar, ai, den, cr, ci


def _s5_prep(lam_re, lam_im, log_step, b_re_t, b_im_t):
    G, P = lam_re.shape
    H = b_re_t.shape[1]

    def body(lr_ref, li_ref, ls_ref, br_ref, bi_ref, ar_ref, ai_ref, bbr_ref, bbi_ref):
        _, ar, ai, _, cr, ci = _s5_coeffs(lr_ref[...], li_ref[...], ls_ref[...])
        ar_ref[...] = ar
        ai_ref[...] = ai
        br, bi = br_ref[...], bi_ref[...]
        crb, cib = cr[:, None, :], ci[:, None, :]
        bbr_ref[...] = crb * br - cib * bi
        bbi_ref[...] = crb * bi + cib * br

    return pl.pallas_call(
        body, name="s5_prep",
        out_shape=(jax.ShapeDtypeStruct((G, P), F32), jax.ShapeDtypeStruct((G, P), F32),
                   jax.ShapeDtypeStruct((G, H, P), F32), jax.ShapeDtypeStruct((G, H, P), F32)),
        compiler_params=_params(),
    )(lam_re, lam_im, log_step.reshape(G, 1), b_re_t, b_im_t)


def _s5_prep_bwd(lam_re, lam_im, log_step, b_re_t, b_im_t, d_ar, d_ai, d_bbr, d_bbi):
    G, P = lam_re.shape
    H = b_re_t.shape[1]

    def body(lr_ref, li_ref, ls_ref, br_ref, bi_ref, dar_ref, dai_ref, dbbr_ref, dbbi_ref,
             dlr_ref, dli_ref, dls_ref, dbr_ref, dbi_ref):
        lr, li = lr_ref[...], li_ref[...]
        dt, ar, ai, den, cr, ci = _s5_coeffs(lr, li, ls_ref[...])
        br, bi = br_ref[...], bi_ref[...]
        gbr, gbi = dbbr_ref[...], dbbi_ref[...]
        crb, cib = cr[:, None, :], ci[:, None, :]
        dbr_ref[...] = crb * gbr + cib * gbi
        dbi_ref[...] = crb * gbi - cib * gbr
        gcr = jnp.sum(br * gbr + bi * gbi, axis=1)
        gci = jnp.sum(br * gbi - bi * gbr, axis=1)
        ilr, ili = lr / den, -li / den
        gar = dar_ref[...] + (ilr * gcr + ili * gci)
        gai = dai_ref[...] + (ilr * gci - ili * gcr)
        qr, qi = cr * ilr - ci * ili, cr * ili + ci * ilr
        glr = -(qr * gcr + qi * gci)
        gli = -(qr * gci - qi * gcr)
        glr = glr + dt * (ar * gar + ai * gai)
        gli = gli + dt * (ar * gai - ai * gar)
        wr, wi = lr * ar - li * ai, lr * ai + li * ar
        gdt = jnp.sum(wr * gar + wi * gai, axis=1, keepdims=True)
        dlr_ref[...] = glr
        dli_ref[...] = gli
        dls_ref[...] = gdt * dt

    return pl.pallas_call(
        body, name="s5_prep_bwd",
        out_shape=(jax.ShapeDtypeStruct((G, P), F32), jax.ShapeDtypeStruct((G, P), F32),
                   jax.ShapeDtypeStruct((G, 1), F32),
                   jax.ShapeDtypeStruct((G, H, P), F32), jax.ShapeDtypeStruct((G, H, P), F32)),
        compiler_params=_params(),
    )(lam_re, lam_im, log_step.reshape(G, 1), b_re_t, b_im_t, d_ar, d_ai, d_bbr, d_bbi)


def _s5_block_mats(bbr_t, bbi_t, c_re, c_im):
    bmat = _s5_expand(bbr_t, bbi_t)
    cmat = jnp.transpose(_s5_expand(c_re, -c_im), (0, 2, 1))
    return bmat.astype(BF16), cmat.astype(BF16)


def _s5_diag_mask():
    r = lax.broadcasted_iota(jnp.int32, (LANES, 2 * STATE_COLS), 0) // SSM_GROUP
    c = (lax.broadcasted_iota(jnp.int32, (LANES, 2 * STATE_COLS), 1) % STATE_COLS) // SSM_STATE
    return (r == c).astype(F32)


def _s5_expand(re, im):
    re = jnp.tile(re.reshape(SSM_BLOCKS, LANES, SSM_STATE), (1, 1, GROUPS_PER_BLOCK))
    im = jnp.tile(im.reshape(SSM_BLOCKS, LANES, SSM_STATE), (1, 1, GROUPS_PER_BLOCK))
    return jnp.concatenate([re, im], axis=-1) * _s5_diag_mask()[None]


def _s5_block_diag(dmat):
    d = dmat * _s5_diag_mask()[None]
    parts = []
    for ri in range(2):
        acc = 0.0
        for g in range(GROUPS_PER_BLOCK):
            c0 = ri * STATE_COLS + g * SSM_STATE
            acc = acc + d[:, :, c0:c0 + SSM_STATE]
        parts.append(acc.reshape(SSM_GROUPS, SSM_GROUP, SSM_STATE))
    return jnp.stack(parts)


def _s5_a_rows(ar, ai):
    a = jnp.concatenate([ar.reshape(SSM_BLOCKS, STATE_COLS), ai.reshape(SSM_BLOCKS, STATE_COLS)], axis=1)
    return jnp.broadcast_to(a[:, None, :], (SSM_BLOCKS, SUBLANES, 2 * STATE_COLS))


def _to_step_major(src_ref, dst_ref, seg):
    for s in range(SUBLANES):
        dst_ref[pl.ds(s, seg, stride=SUBLANES), :] = src_ref[pl.ds(seg * s, seg), :]


def _segment_rows(ref, s, seg):
    return ref[pl.ds(s, seg, stride=SUBLANES), :]


def _cmul(ar, ai, xr, xi):
    return ar * xr - ai * xi, ar * xi + ai * xr


def _s5_tables(a_ref, pw_s, pwr_s, S, seg):
    ar, ai = a_ref[:, :S], a_ref[:, S:]

    def step(i, c):
        pr, pi = c
        pw_s[i, :, :S] = pr
        pw_s[i, :, S:] = pi
        nr, ni = _cmul(ar, ai, pr, pi)
        pwr_s[seg - 1 - i, :, :S] = nr
        pwr_s[seg - 1 - i, :, S:] = ni
        return nr, ni

    pr, pi = lax.fori_loop(0, seg, step, (jnp.ones_like(ar), jnp.zeros_like(ai)))
    pw_s[seg, :, :S] = pr
    pw_s[seg, :, S:] = pi


def _s5_fwd(proj, bmat, cmat, a_rows, d_skip, *, tc=512):
    L = proj.shape[0]
    tc = min(tc, L)
    nt = L // tc
    seg = tc // SUBLANES
    S = STATE_COLS

    def body(u_ref, b_ref, c_ref, a_ref, d_ref, y_ref, yg_ref, xp_ref,
             bu_s, xp_s, pw_s, pwr_s, carry_s, e_s, up_s, yc_s):
        @pl.when(pl.program_id(1) == 0)
        def _():
            carry_s[...] = jnp.zeros_like(carry_s)
            _s5_tables(a_ref, pw_s, pwr_s, S, seg)

        ar, ai = a_ref[:, :S], a_ref[:, S:]
        _to_step_major(u_ref, up_s, seg)
        bu = jnp.dot(up_s[...].astype(BF16), b_ref[...], preferred_element_type=F32)
        bu_s[...] = bu.reshape(seg, SUBLANES, 2 * S)

        def step(i, carry):
            cr, ci = carry
            xp_s[i, :, :S] = cr
            xp_s[i, :, S:] = ci
            return ar * cr - ai * ci + bu_s[i, :, :S], ar * ci + ai * cr + bu_s[i, :, S:]

        zero = jnp.zeros((SUBLANES, S), F32)
        fr, fi = lax.fori_loop(0, seg, step, (zero, zero))
        pr, pi = pw_s[seg, 0:1, :S], pw_s[seg, 0:1, S:]
        er, ei = carry_s[0:1, :S], carry_s[0:1, S:]
        for s in range(SUBLANES):
            e_s[s:s + 1, :S] = er
            e_s[s:s + 1, S:] = ei
            tr, ti = _cmul(pr, pi, er, ei)
            er, ei = fr[s:s + 1] + tr, fi[s:s + 1] + ti
        carry_s[0:1, :S] = er
        carry_s[0:1, S:] = ei
        pw = pw_s[0:seg]
        tr, ti = _cmul(pw[:, :, :S], pw[:, :, S:], e_s[:, :S][None], e_s[:, S:][None])
        xl = xp_s[...]
        xp = jnp.concatenate([xl[:, :, :S] + tr, xl[:, :, S:] + ti], axis=-1).reshape(tc, 2 * S)
        xp_ref[...] = xp.astype(xp_ref.dtype)
        a1r, a1i = ar[0:1], ai[0:1]
        x_re = a1r * xp[:, :S] - a1i * xp[:, S:] + bu[:, :S]
        x_im = a1r * xp[:, S:] + a1i * xp[:, :S] + bu[:, S:]
        xs = jnp.concatenate([x_re, x_im], axis=1).astype(BF16)
        yc_s[...] = jnp.dot(xs, c_ref[...], preferred_element_type=F32)
        for s in range(SUBLANES):
            rows = pl.ds(seg * s, seg)
            y = _segment_rows(yc_s, s, seg) + d_ref[...] * u_ref[rows, :]
            y_ref[rows, :] = y
            yg_ref[rows, :] = _gelu(y).astype(BF16)

    return pl.pallas_call(
        body, name="s5_fwd",
        out_shape=(jax.ShapeDtypeStruct((L, MAIN_WIDTH), F32),
                   jax.ShapeDtypeStruct((L, MAIN_WIDTH), BF16),
                   jax.ShapeDtypeStruct((L, SSM_BLOCKS * 2 * S), BF16)),
        grid=(SSM_BLOCKS, nt),
        in_specs=[pl.BlockSpec((tc, LANES), lambda b, t: (t, b)),
                  pl.BlockSpec((None, LANES, 2 * S), lambda b, t: (b, 0, 0)),
                  pl.BlockSpec((None, 2 * S, LANES), lambda b, t: (b, 0, 0)),
                  pl.BlockSpec((None, SUBLANES, 2 * S), lambda b, t: (b, 0, 0)),
                  pl.BlockSpec((1, LANES), lambda b, t: (0, b))],
        out_specs=(pl.BlockSpec((tc, LANES), lambda b, t: (t, b)),
                   pl.BlockSpec((tc, LANES), lambda b, t: (t, b)),
                   pl.BlockSpec((tc, 2 * S), lambda b, t: (t, b))),
        scratch_shapes=[pltpu.VMEM((seg, SUBLANES, 2 * S), F32),
                        pltpu.VMEM((seg, SUBLANES, 2 * S), F32),
                        pltpu.VMEM((seg + 1, SUBLANES, 2 * S), F32),
                        pltpu.VMEM((seg, SUBLANES, 2 * S), F32),
                        pltpu.VMEM((SUBLANES, 2 * S), F32),
                        pltpu.VMEM((SUBLANES, 2 * S), F32),
                        pltpu.VMEM((tc, LANES), F32),
                        pltpu.VMEM((tc, LANES), F32)],
        compiler_params=_params("parallel", "arbitrary"),
    )(proj, bmat, cmat, a_rows, d_skip.reshape(1, MAIN_WIDTH))


def _s5_bwd(proj, dyg_a, dyg_b, y, xp, bmat, cmat, a_rows, d_skip, dproj, *, tc=512):
    L = proj.shape[0]
    tc = min(tc, L)
    nt = L // tc
    seg = tc // SUBLANES
    S = STATE_COLS
    nn = (((1,), (1,)), ((), ()))
    tn = (((0,), (0,)), ((), ()))

    def body(u_ref, dyga_ref, dygb_ref, y_ref, xp_ref, b_ref, c_ref, a_ref, d_ref, dp_hbm,
             du_ref, db_ref, dc_ref, da_ref, dd_ref, dl_s, pw_s, pwr_s, carry_s, e_s, up_s, dy_s, dyp_s, dup_s):
        @pl.when(pl.program_id(1) == 0)
        def _():
            carry_s[...] = jnp.zeros_like(carry_s)
            db_ref[...] = jnp.zeros_like(db_ref)
            dc_ref[...] = jnp.zeros_like(dc_ref)
            da_ref[...] = jnp.zeros_like(da_ref)
            dd_ref[...] = jnp.zeros_like(dd_ref)
            _s5_tables(a_ref, pw_s, pwr_s, S, seg)

        ar, ai = a_ref[:, :S], a_ref[:, S:]
        a1r, a1i = ar[0:1], ai[0:1]
        u = u_ref[...]
        dy = (dyga_ref[...] + dygb_ref[...]) * _gelu_grad(y_ref[...])
        dy_s[...] = dy
        xp = xp_ref[...].astype(F32)
        _to_step_major(u_ref, up_s, seg)
        _to_step_major(dy_s, dyp_s, seg)
        ubp = up_s[...].astype(BF16)
        dyp = dyp_s[...].astype(BF16)
        bu = jnp.dot(ubp, b_ref[...], preferred_element_type=F32)
        x_re = a1r * xp[:, :S] - a1i * xp[:, S:] + bu[:, :S]
        x_im = a1r * xp[:, S:] + a1i * xp[:, :S] + bu[:, S:]
        xs = jnp.concatenate([x_re, x_im], axis=1).astype(BF16)
        dc_ref[...] += lax.dot_general(dyp, xs, tn, preferred_element_type=F32)
        dx = lax.dot_general(dyp, c_ref[...], nn, preferred_element_type=F32)
        dl_s[...] = dx.reshape(seg, SUBLANES, 2 * S)

        def step(k, carry):
            cr, ci = carry
            i = seg - 1 - k
            lr = dl_s[i, :, :S] + (ar * cr + ai * ci)
            li = dl_s[i, :, S:] + (ar * ci - ai * cr)
            dl_s[i, :, :S] = lr
            dl_s[i, :, S:] = li
            return lr, li

        zero = jnp.zeros((SUBLANES, S), F32)
        fr, fi = lax.fori_loop(0, seg, step, (zero, zero))
        pr, pi = pw_s[seg, 0:1, :S], pw_s[seg, 0:1, S:]
        er, ei = carry_s[0:1, :S], carry_s[0:1, S:]
        for s in range(SUBLANES - 1, -1, -1):
            e_s[s:s + 1, :S] = er
            e_s[s:s + 1, S:] = ei
            er, ei = fr[s:s + 1] + (pr * er + pi * ei), fi[s:s + 1] + (pr * ei - pi * er)
        carry_s[0:1, :S] = er
        carry_s[0:1, S:] = ei
        er, ei = e_s[:, :S][None], e_s[:, S:][None]
        pw = pwr_s[...]
        pwr, pwi = pw[:, :, :S], pw[:, :, S:]
        ll = dl_s[...]
        lam = jnp.concatenate([ll[:, :, :S] + (pwr * er + pwi * ei), ll[:, :, S:] + (pwr * ei - pwi * er)],
                              axis=-1).reshape(tc, 2 * S)
        l_re, l_im = lam[:, :S], lam[:, S:]
        da_ref[0:1, :S] += jnp.sum(l_re * xp[:, :S] + l_im * xp[:, S:], axis=0, keepdims=True)
        da_ref[0:1, S:] += jnp.sum(l_im * xp[:, :S] - l_re * xp[:, S:], axis=0, keepdims=True)
        lamb = lam.astype(BF16)
        dup_s[...] = lax.dot_general(lamb, b_ref[...], nn, preferred_element_type=F32)
        for s in range(SUBLANES):
            rows = pl.ds(seg * s, seg)
            du = _segment_rows(dup_s, s, seg) + d_ref[...] * dy_s[rows, :]
            du_ref[rows, :] = du.astype(du_ref.dtype)
        db_ref[...] += lax.dot_general(ubp, lamb, tn, preferred_element_type=F32)
        dd_ref[0:1, :] += jnp.sum(dy * u, axis=0, keepdims=True)

    rev = lambda b, t: (nt - 1 - t, b)
    return pl.pallas_call(
        body, name="s5_bwd",
        out_shape=(jax.ShapeDtypeStruct(dproj.shape, dproj.dtype),
                   jax.ShapeDtypeStruct((SSM_BLOCKS, LANES, 2 * S), F32),
                   jax.ShapeDtypeStruct((SSM_BLOCKS, LANES, 2 * S), F32),
                   jax.ShapeDtypeStruct((SSM_BLOCKS, SUBLANES, 2 * S), F32),
                   jax.ShapeDtypeStruct((SUBLANES, MAIN_WIDTH), F32)),
        input_output_aliases={9: 0},
        grid=(SSM_BLOCKS, nt),
        in_specs=[pl.BlockSpec((tc, LANES), rev),
                  pl.BlockSpec((tc, LANES), rev),
                  pl.BlockSpec((tc, LANES), rev),
                  pl.BlockSpec((tc, LANES), rev),
                  pl.BlockSpec((tc, 2 * S), rev),
                  pl.BlockSpec((None, LANES, 2 * S), lambda b, t: (b, 0, 0)),
                  pl.BlockSpec((None, 2 * S, LANES), lambda b, t: (b, 0, 0)),
                  pl.BlockSpec((None, SUBLANES, 2 * S), lambda b, t: (b, 0, 0)),
                  pl.BlockSpec((1, LANES), lambda b, t: (0, b)),
                  _ANY],
        out_specs=(pl.BlockSpec((tc, LANES), rev),
                   pl.BlockSpec((None, LANES, 2 * S), lambda b, t: (b, 0, 0)),
                   pl.BlockSpec((None, LANES, 2 * S), lambda b, t: (b, 0, 0)),
                   pl.BlockSpec((None, SUBLANES, 2 * S), lambda b, t: (b, 0, 0)),
                   pl.BlockSpec((SUBLANES, LANES), lambda b, t: (0, b))),
        scratch_shapes=[pltpu.VMEM((seg, SUBLANES, 2 * S), F32),
                        pltpu.VMEM((seg + 1, SUBLANES, 2 * S), F32),
                        pltpu.VMEM((seg, SUBLANES, 2 * S), F32),
                        pltpu.VMEM((SUBLANES, 2 * S), F32),
                        pltpu.VMEM((SUBLANES, 2 * S), F32),
                        pltpu.VMEM((tc, LANES), F32),
                        pltpu.VMEM((tc, LANES), F32),
                        pltpu.VMEM((tc, LANES), F32),
                        pltpu.VMEM((tc, LANES), F32)],
        compiler_params=_params("parallel", "arbitrary"),
    )(proj, dyg_a, dyg_b, y, xp, bmat, cmat, a_rows, d_skip.reshape(1, MAIN_WIDTH), dproj)


_Z_COLS = slice(MAIN_WIDTH, 2 * MAIN_WIDTH)
_ZM_COLS = slice(2 * MAIN_WIDTH + MEM_WIDTH, IN_WIDTH)


def _proj_rows(tr):
    return pl.BlockSpec((tr, IN_WIDTH), lambda i: (i, 0))


def _row_specs(tr):
    main = pl.BlockSpec((tr, MAIN_WIDTH), lambda i: (i, 0))
    z = pl.BlockSpec((tr, MAIN_WIDTH), lambda i: (i, 1))
    zm = pl.BlockSpec((tr, MEM_WIDTH), lambda i: (i, IN_WIDTH // MEM_WIDTH - 1))
    mem = pl.BlockSpec((tr, MEM_WIDTH), lambda i: (i, 0))
    cat = pl.BlockSpec((tr, D_MODEL), lambda i: (i, 0))
    vec = pl.BlockSpec((1, MAIN_WIDTH), lambda i: (0, 0))
    return main, z, zm, mem, cat, vec


def _gate_a_fwd(y, t, b_glu, proj, o_mem, *, tr=256):
    L = y.shape[0]
    tr = min(tr, L)

    def body(y_ref, t_ref, b_ref, z_ref, zm_ref, om_ref, o_ref):
        yg = _gelu(y_ref[...])
        sz, _ = _silu_and_grad(z_ref[...])
        o_ref[:, :MAIN_WIDTH] = (yg * _sigmoid(t_ref[...] + b_ref[...]) * sz).astype(BF16)
        szm, _ = _silu_and_grad(zm_ref[...])
        o_ref[:, MAIN_WIDTH:] = (om_ref[...] * szm).astype(BF16)

    main, z, zm, mem, cat, vec = _row_specs(tr)
    return pl.pallas_call(
        body, name="gate_a_fwd", out_shape=jax.ShapeDtypeStruct((L, D_MODEL), BF16),
        grid=(L // tr,), in_specs=[main, main, vec, z, zm, mem], out_specs=cat,
        compiler_params=_params("parallel"),
    )(y, t, b_glu.reshape(1, MAIN_WIDTH), proj, proj, o_mem)


def _gate_a_bwd(dcat, y, t, b_glu, proj, o_mem, *, tr=256):
    L = y.shape[0]
    tr = min(tr, L)

    def body(dc_ref, y_ref, t_ref, b_ref, z_ref, zm_ref, om_ref,
             dp_ref, dt_ref, dyg_ref, dom_ref, db_ref):
        dmain = dc_ref[:, :MAIN_WIDTH]
        dmemo = dc_ref[:, MAIN_WIDTH:]
        yg = _gelu(y_ref[...])
        sg = _sigmoid(t_ref[...] + b_ref[...])
        sz, gz = _silu_and_grad(z_ref[...])
        dp_ref[:, _Z_COLS] = (dmain * (yg * sg) * gz).astype(BF16)
        dy2 = dmain * sz
        dyg_ref[...] = dy2 * sg
        dt = dy2 * yg * (sg * (1.0 - sg))
        dt_ref[...] = dt.astype(BF16)

        @pl.when(pl.program_id(0) == 0)
        def _():
            db_ref[...] = jnp.zeros_like(db_ref)

        db_ref[...] += jnp.sum(dt, axis=0, keepdims=True)
        szm, gzm = _silu_and_grad(zm_ref[...])
        dom_ref[...] = dmemo * szm
        dp_ref[:, _ZM_COLS] = (dmemo * om_ref[...] * gzm).astype(BF16)

    main, z, zm, mem, cat, vec = _row_specs(tr)
    outs = pl.pallas_call(
        body, name="gate_a_bwd",
        out_shape=(jax.ShapeDtypeStruct((L, IN_WIDTH), BF16),
                   jax.ShapeDtypeStruct((L, MAIN_WIDTH), BF16), jax.ShapeDtypeStruct((L, MAIN_WIDTH), F32),
                   jax.ShapeDtypeStruct((L, MEM_WIDTH), F32), jax.ShapeDtypeStruct((1, MAIN_WIDTH), F32)),
        grid=(L // tr,), in_specs=[cat, main, main, vec, z, zm, mem],
        out_specs=(_proj_rows(tr), main, main, mem, vec),
        compiler_params=_params("arbitrary"),
    )(dcat, y, t, b_glu.reshape(1, MAIN_WIDTH), proj, proj, o_mem)
    return outs


def _gate_b_fwd(att, proj, o_mem, *, tr=256):
    L = att.shape[0]
    tr = min(tr, L)

    def body(a_ref, z_ref, zm_ref, om_ref, o_ref):
        sz, _ = _silu_and_grad(z_ref[...])
        o_ref[:, :MAIN_WIDTH] = (a_ref[...] * sz).astype(BF16)
        szm, _ = _silu_and_grad(zm_ref[...])
        o_ref[:, MAIN_WIDTH:] = (om_ref[...] * szm).astype(BF16)

    main, z, zm, mem, cat, _ = _row_specs(tr)
    return pl.pallas_call(
        body, name="gate_b_fwd", out_shape=jax.ShapeDtypeStruct((L, D_MODEL), BF16),
        grid=(L // tr,), in_specs=[main, z, zm, mem], out_specs=cat,
        compiler_params=_params("parallel"),
    )(att, proj, proj, o_mem)


def _gate_b_bwd(dcat, att, proj, o_mem, *, tr=256):
    L = att.shape[0]
    tr = min(tr, L)

    def body(dc_ref, a_ref, z_ref, zm_ref, om_ref, da_ref, dp_ref, dom_ref, dl_ref):
        dmain = dc_ref[:, :MAIN_WIDTH]
        dmemo = dc_ref[:, MAIN_WIDTH:]
        att = a_ref[...]
        sz, gz = _silu_and_grad(z_ref[...])
        datt = dmain * sz
        da_ref[...] = datt
        dp_ref[:, _Z_COLS] = (dmain * att * gz).astype(BF16)
        szm, gzm = _silu_and_grad(zm_ref[...])
        dom_ref[...] = dmemo * szm
        dp_ref[:, _ZM_COLS] = (dmemo * om_ref[...] * gzm).astype(BF16)
        prod = datt * att
        for h in range(FOX_HEADS):
            dl_ref[h] = jnp.sum(prod[:, h * HEAD_DIM:(h + 1) * HEAD_DIM], axis=1, keepdims=True)

    main, z, zm, mem, cat, _ = _row_specs(tr)
    delta = pl.BlockSpec((FOX_HEADS, tr, 1), lambda i: (0, i, 0))
    return pl.pallas_call(
        body, name="gate_b_bwd",
        out_shape=(jax.ShapeDtypeStruct((L, MAIN_WIDTH), F32), jax.ShapeDtypeStruct((L, IN_WIDTH), BF16),
                   jax.ShapeDtypeStruct((L, MEM_WIDTH), F32), jax.ShapeDtypeStruct((FOX_HEADS, L, 1), F32)),
        grid=(L // tr,), in_specs=[cat, main, z, zm, mem], out_specs=(main, _proj_rows(tr), mem, delta),
        compiler_params=_params("parallel"),
    )(dcat, att, proj, proj, o_mem)


_MEM_Q_COL = (2 * MAIN_WIDTH) // HEAD_DIM
_NT = (((1,), (1,)), ((), ()))
_TN = (((0,), (0,)), ((), ()))


def _mem_probs(q_ref, k_ref):
    qs = (q_ref[...] * (HEAD_DIM ** -0.5)).astype(BF16)
    s = lax.dot_general(qs, k_ref[...].astype(BF16), _NT, preferred_element_type=F32)
    e = jnp.exp(s - jnp.max(s, axis=-1, keepdims=True))
    return qs, e / jnp.sum(e, axis=-1, keepdims=True)


def _mem_attn_fwd(proj, kvm, *, tq=2048):
    L = proj.shape[0]
    tq = min(tq, L)

    def body(q_ref, k_ref, v_ref, o_ref):
        _, p = _mem_probs(q_ref, k_ref)
        o_ref[...] = jnp.dot(p.astype(BF16), v_ref[...].astype(BF16), preferred_element_type=F32)

    return pl.pallas_call(
        body, name="mem_attn_fwd", out_shape=jax.ShapeDtypeStruct((L, MEM_WIDTH), F32),
        grid=(MEM_HEADS, L // tq),
        in_specs=[pl.BlockSpec((tq, HEAD_DIM), lambda h, i: (i, _MEM_Q_COL + h)),
                  pl.BlockSpec((N_MEM, HEAD_DIM), lambda h, i: (0, h)),
                  pl.BlockSpec((N_MEM, HEAD_DIM), lambda h, i: (0, MEM_HEADS + h))],
        out_specs=pl.BlockSpec((tq, HEAD_DIM), lambda h, i: (i, h)),
        compiler_params=_params("parallel", "parallel"),
    )(proj, kvm, kvm)


def _mem_attn_bwd(proj, kvm, do, dproj, *, tq=2048):
    L = proj.shape[0]
    tq = min(tq, L)

    def body(q_ref, k_ref, v_ref, do_ref, dp_hbm, dq_ref, dk_ref, dv_ref):
        @pl.when(pl.program_id(1) == 0)
        def _():
            dk_ref[...] = jnp.zeros_like(dk_ref)
            dv_ref[...] = jnp.zeros_like(dv_ref)

        qs, p = _mem_probs(q_ref, k_ref)
        dob = do_ref[...].astype(BF16)
        dp = lax.dot_general(dob, v_ref[...].astype(BF16), _NT, preferred_element_type=F32)
        ds = p * (dp - jnp.sum(p * dp, axis=-1, keepdims=True))
        dsb = ds.astype(BF16)
        dq = jnp.dot(dsb, k_ref[...].astype(BF16), preferred_element_type=F32) * (HEAD_DIM ** -0.5)
        dq_ref[...] = dq.astype(BF16)
        dk_ref[...] += lax.dot_general(dsb, qs, _TN, preferred_element_type=F32)
        dv_ref[...] += lax.dot_general(p.astype(BF16), dob, _TN, preferred_element_type=F32)

    dproj, dk, dv = pl.pallas_call(
        body, name="mem_attn_bwd",
        out_shape=(jax.ShapeDtypeStruct(dproj.shape, dproj.dtype),
                   jax.ShapeDtypeStruct((N_MEM, MEM_WIDTH), F32),
                   jax.ShapeDtypeStruct((N_MEM, MEM_WIDTH), F32)),
        grid=(MEM_HEADS, L // tq),
        in_specs=[pl.BlockSpec((tq, HEAD_DIM), lambda h, i: (i, _MEM_Q_COL + h)),
                  pl.BlockSpec((N_MEM, HEAD_DIM), lambda h, i: (0, h)),
                  pl.BlockSpec((N_MEM, HEAD_DIM), lambda h, i: (0, MEM_HEADS + h)),
                  pl.BlockSpec((tq, HEAD_DIM), lambda h, i: (i, h)),
                  _ANY],
        out_specs=(pl.BlockSpec((tq, HEAD_DIM), lambda h, i: (i, _MEM_Q_COL + h)),
                   pl.BlockSpec((N_MEM, HEAD_DIM), lambda h, i: (0, h)),
                   pl.BlockSpec((N_MEM, HEAD_DIM), lambda h, i: (0, h))),
        input_output_aliases={4: 0},
        compiler_params=_params("parallel", "arbitrary"),
    )(proj, kvm, kvm, do, dproj)
    return dproj, jnp.concatenate([dk, dv], axis=1)


def _tile_cumsum(x, row, reverse):
    for sh in (1, 2, 4):
        if reverse:
            x = x + jnp.where(row < SUBLANES - sh, pltpu.roll(x, SUBLANES - sh, 0), 0.0)
        else:
            x = x + jnp.where(row >= sh, pltpu.roll(x, sh, 0), 0.0)
    return x


def _fgate_fwd(pre, b_pad):
    L = pre.shape[0]
    n8 = L // SUBLANES

    def body(p_ref, b_ref, o_ref):
        row = lax.broadcasted_iota(jnp.int32, (SUBLANES, LANES), 0)
        b = b_ref[...]

        def step(i, carry):
            x = p_ref[i] + b
            logf = jnp.minimum(x, 0.0) - jnp.log(1.0 + jnp.exp(-jnp.abs(x)))
            t = _tile_cumsum(logf, row, False) + carry
            o_ref[i] = t
            return t[SUBLANES - 1:SUBLANES, :]

        lax.fori_loop(0, n8, step, jnp.zeros((1, LANES), F32))

    out = pl.pallas_call(
        body, name="fgate_fwd", out_shape=jax.ShapeDtypeStruct((n8, SUBLANES, LANES), F32),
        compiler_params=_params(),
    )(pre.reshape(n8, SUBLANES, LANES), b_pad.reshape(1, LANES))
    return out.reshape(L, LANES)


def _fgate_bwd(dfcum, pre, b_pad):
    L = pre.shape[0]
    n8 = L // SUBLANES

    def body(d_ref, p_ref, b_ref, o_ref, s_ref):
        row = lax.broadcasted_iota(jnp.int32, (SUBLANES, LANES), 0)
        b = b_ref[...]

        def step(k, carry):
            c, acc = carry
            i = n8 - 1 - k
            t = _tile_cumsum(d_ref[i], row, True) + c
            dpre = t * _sigmoid(-(p_ref[i] + b))
            o_ref[i] = dpre
            return t[0:1, :], acc + dpre

        _, acc = lax.fori_loop(0, n8, step, (jnp.zeros((1, LANES), F32), jnp.zeros((SUBLANES, LANES), F32)))
        s_ref[...] = jnp.sum(acc, axis=0, keepdims=True)

    dpre, db = pl.pallas_call(
        body, name="fgate_bwd",
        out_shape=(jax.ShapeDtypeStruct((n8, SUBLANES, LANES), F32), jax.ShapeDtypeStruct((1, LANES), F32)),
        compiler_params=_params(),
    )(dfcum.reshape(n8, SUBLANES, LANES), pre.reshape(n8, SUBLANES, LANES), b_pad.reshape(1, LANES))
    return dpre.reshape(L, LANES), db


FOX_BLOCK = 512


def _fox_scores(qs, k, fk, diagonal):
    s = lax.dot_general(qs, k, _NT, preferred_element_type=F32) - fk
    if diagonal:
        row = lax.broadcasted_iota(jnp.int32, s.shape, 0)
        col = lax.broadcasted_iota(jnp.int32, s.shape, 1)
        s = jnp.where(row >= col, s, NEG_BIG)
    return s


def _fox_specs(tq, L):
    nq = L // tq
    return dict(
        rows=lambda off: pl.BlockSpec((tq, HEAD_DIM), lambda h, i: (i, off + h)),
        seq=lambda off: pl.BlockSpec((L, HEAD_DIM), lambda h, i: (0, off + h)),
        col=pl.BlockSpec((None, None, tq, 1), lambda h, i: (h, i, 0, 0)),
        col_all=pl.BlockSpec((None, nq, tq, 1), lambda h, i: (h, 0, 0, 0)),
        row=pl.BlockSpec((None, None, 1, tq), lambda h, i: (h, i, 0, 0)),
        row_all=pl.BlockSpec((None, nq, 1, tq), lambda h, i: (h, 0, 0, 0)))


FOX_FWD_HEADS = 2


def _fox_fwd(proj, kv, fk):
    L = proj.shape[0]
    tq = min(FOX_BLOCK, L)
    nq = L // tq
    nh = FOX_FWD_HEADS
    W = nh * HEAD_DIM

    def body(q_ref, k_ref, v_ref, fk_ref, o_ref, lse_ref, m_s, l_s, acc_s):
        qi = pl.program_id(1)
        cols = [slice(a * HEAD_DIM, (a + 1) * HEAD_DIM) for a in range(nh)]
        qs = [(q_ref[:, cs] * (HEAD_DIM ** -0.5)).astype(BF16) for cs in cols]
        m_s[...] = jnp.full_like(m_s, NEG_BIG)
        l_s[...] = jnp.zeros_like(l_s)
        acc_s[...] = jnp.zeros_like(acc_s)

        def block(j, diagonal):
            r0 = pl.multiple_of(j * tq, tq)
            for a, cs in enumerate(cols):
                s = _fox_scores(qs[a], k_ref[pl.ds(r0, tq), cs], fk_ref[a, j], diagonal)
                m_new = jnp.maximum(m_s[a], jnp.max(s, axis=-1, keepdims=True))
                alpha = jnp.exp(m_s[a] - m_new)
                p = jnp.exp(s - m_new)
                l_s[a] = alpha * l_s[a] + jnp.sum(p, axis=-1, keepdims=True)
                acc_s[a] = alpha * acc_s[a] + jnp.dot(p.astype(BF16), v_ref[pl.ds(r0, tq), cs],
                                                      preferred_element_type=F32)
                m_s[a] = m_new

        def below(j, carry):
            block(j, False)
            return carry

        lax.fori_loop(0, qi, below, 0)
        block(qi, True)
        for a, cs in enumerate(cols):
            o_ref[:, cs] = acc_s[a] / l_s[a]
            lse_ref[a] = m_s[a] + jnp.log(l_s[a])

    return pl.pallas_call(
        body, name="fox_fwd",
        out_shape=(jax.ShapeDtypeStruct((L, MAIN_WIDTH), F32),
                   jax.ShapeDtypeStruct((FOX_HEADS, nq, tq, 1), F32)),
        grid=(FOX_HEADS // nh, nq),
        in_specs=[pl.BlockSpec((tq, W), lambda h, i: (i, h)),
                  pl.BlockSpec((L, W), lambda h, i: (0, h)),
                  pl.BlockSpec((L, W), lambda h, i: (0, FOX_HEADS // nh + h)),
                  pl.BlockSpec((nh, nq, 1, tq), lambda h, i: (h, 0, 0, 0))],
        out_specs=(pl.BlockSpec((tq, W), lambda h, i: (i, h)),
                   pl.BlockSpec((nh, None, tq, 1), lambda h, i: (h, i, 0, 0))),
        scratch_shapes=[pltpu.VMEM((nh, tq, 1), F32), pltpu.VMEM((nh, tq, 1), F32),
                        pltpu.VMEM((nh, tq, HEAD_DIM), F32)],
        compiler_params=_params("parallel", "parallel"),
    )(proj, kv, kv, fk)


def _fox_bwd_dq(proj, kv, fk, lse, delta, datt, dproj):
    L = proj.shape[0]
    tq = min(FOX_BLOCK, L)
    nq = L // tq
    sp = _fox_specs(tq, L)

    def body(q_ref, k_ref, v_ref, fk_ref, lse_ref, dl_ref, do_ref, dp_hbm, dq_ref, df_ref, acc_s, df_s):
        qi = pl.program_id(1)
        qs = (q_ref[...] * (HEAD_DIM ** -0.5)).astype(BF16)
        dob = do_ref[...].astype(BF16)
        lse, dl = lse_ref[...], dl_ref[...]
        acc_s[...] = jnp.zeros_like(acc_s)
        df_s[...] = jnp.zeros_like(df_s)

        def block(j, diagonal):
            r0 = pl.multiple_of(j * tq, tq)
            k = k_ref[pl.ds(r0, tq), :]
            p = jnp.exp(_fox_scores(qs, k, fk_ref[j], diagonal) - lse)
            dp = lax.dot_general(dob, v_ref[pl.ds(r0, tq), :], _NT, preferred_element_type=F32)
            ds = p * (dp - dl)
            acc_s[...] += jnp.dot(ds.astype(BF16), k, preferred_element_type=F32)
            df_s[...] += jnp.sum(ds, axis=1, keepdims=True)

        def below(j, carry):
            block(j, False)
            return carry

        lax.fori_loop(0, qi, below, 0)
        block(qi, True)
        dq_ref[...] = (acc_s[...] * (HEAD_DIM ** -0.5)).astype(BF16)
        df_ref[...] = df_s[...]

    return pl.pallas_call(
        body, name="fox_bwd_dq",
        out_shape=(jax.ShapeDtypeStruct(dproj.shape, dproj.dtype),
                   jax.ShapeDtypeStruct((FOX_HEADS, nq, tq, 1), F32)),
        grid=(FOX_HEADS, nq),
        in_specs=[sp["rows"](0), sp["seq"](0), sp["seq"](FOX_HEADS), sp["row_all"],
                  sp["col"], sp["col"], sp["rows"](0), _ANY],
        out_specs=(sp["rows"](0), sp["col"]),
        input_output_aliases={7: 0},
        scratch_shapes=[pltpu.VMEM((tq, HEAD_DIM), F32), pltpu.VMEM((tq, 1), F32)],
        compiler_params=_params("parallel", "parallel"),
    )(proj, kv, kv, fk, lse, delta, datt, dproj)


def _fox_bwd_dkv(proj, kv, fk, lse, delta, datt):
    L = proj.shape[0]
    tq = min(FOX_BLOCK, L)
    nq = L // tq
    sp = _fox_specs(tq, L)

    def body(q_ref, k_ref, v_ref, fk_ref, lse_ref, dl_ref, do_ref,
             dk_ref, dv_ref, df_ref, dk_s, dv_s, df_s):
        ki = pl.program_id(1)
        k, v, fk = k_ref[...], v_ref[...], fk_ref[...]
        dk_s[...] = jnp.zeros_like(dk_s)
        dv_s[...] = jnp.zeros_like(dv_s)
        df_s[...] = jnp.zeros_like(df_s)

        def block(i, diagonal):
            r0 = pl.multiple_of(i * tq, tq)
            qs = (q_ref[pl.ds(r0, tq), :] * (HEAD_DIM ** -0.5)).astype(BF16)
            dob = do_ref[pl.ds(r0, tq), :].astype(BF16)
            p = jnp.exp(_fox_scores(qs, k, fk, diagonal) - lse_ref[i])
            dp = lax.dot_general(dob, v, _NT, preferred_element_type=F32)
            ds = p * (dp - dl_ref[i])
            dv_s[...] += lax.dot_general(p.astype(BF16), dob, _TN, preferred_element_type=F32)
            dk_s[...] += lax.dot_general(ds.astype(BF16), qs, _TN, preferred_element_type=F32)
            df_s[...] -= jnp.sum(ds, axis=0, keepdims=True)

        def above(i, carry):
            block(i, False)
            return carry

        block(ki, True)
        lax.fori_loop(ki + 1, nq, above, 0)
        dk_ref[...] = dk_s[...].astype(BF16)
        dv_ref[...] = dv_s[...].astype(BF16)
        df_ref[...] = df_s[...]

    return pl.pallas_call(
        body, name="fox_bwd_dkv",
        out_shape=(jax.ShapeDtypeStruct((L, MAIN_WIDTH), BF16),
                   jax.ShapeDtypeStruct((L, MAIN_WIDTH), BF16),
                   jax.ShapeDtypeStruct((FOX_HEADS, nq, 1, tq), F32)),
        grid=(FOX_HEADS, nq),
        in_specs=[sp["seq"](0), sp["rows"](0), sp["rows"](FOX_HEADS), sp["row"],
                  sp["col_all"], sp["col_all"], sp["seq"](0)],
        out_specs=(sp["rows"](0), sp["rows"](0), sp["row"]),
        scratch_shapes=[pltpu.VMEM((tq, HEAD_DIM), F32), pltpu.VMEM((tq, HEAD_DIM), F32),
                        pltpu.VMEM((1, tq), F32)],
        compiler_params=_params("parallel", "parallel"),
    )(proj, kv, kv, fk, lse, delta, datt)


def _pad_lanes(a):
    return jnp.pad(a, ((0, 0), (0, LANES - a.shape[1])))


def _mem_branch_fwd(memn, w_mk, proj, tag):
    kvm = _mm(memn, w_mk, name="mem_kv_" + tag)
    return kvm, _mem_attn_fwd(proj, kvm)


def _mem_branch_bwd(mem, g, w_mk, proj, memn, kvm, do_mem, dproj, tag):
    dproj, dkvm = _mem_attn_bwd(proj, kvm, do_mem, dproj)
    dkvm = dkvm.astype(BF16)
    dw_mk = _mm(memn, dkvm, ta=True, name="dw_mem_kv_" + tag, out_dtype=BF16)
    dmemn = _mm(dkvm, w_mk, tb=True, name="dmemn_" + tag)
    _, dg = _rmsnorm_bwd(mem, g, dmemn, name="mem_norm_bwd_" + tag, dx_dtype=BF16)
    return dproj, dw_mk, dg


def _local_step(x, mem, target, w, fetch=None, grads_ready=None):
    if grads_ready is None:
        grads_ready = lambda group, grads, token: token
    L = x.shape[0]
    g = {}
    w = dict(w)

    b_re_t = jnp.transpose(w["b_re"], (0, 2, 1))
    b_im_t = jnp.transpose(w["b_im"], (0, 2, 1))
    ar, ai, bbr_t, bbi_t = _s5_prep(w["lam_re"], w["lam_im"], w["log_step"], b_re_t, b_im_t)
    bmat, cmat = _s5_block_mats(bbr_t, bbi_t, w["c_re"], w["c_im"])
    a_rows = _s5_a_rows(ar, ai)

    hn0 = _rmsnorm_fwd(x, w["pre_norm_g"][0], name="pre_norm_0", out_dtype=BF16)
    memn0 = _rmsnorm_fwd(mem, w["mem_norm_g"][0], name="mem_norm_0", out_dtype=BF16)
    memn1 = _rmsnorm_fwd(mem, w["mem_norm_g"][1], name="mem_norm_1", out_dtype=BF16)
    if fetch is not None:
        w.update(fetch("a", [hn0, memn0, memn1, bmat, cmat, a_rows]))
    proj_a = _mm(hn0, w["w_in_a"], name="in_proj_a")
    y, yg, xp = _s5_fwd(proj_a, bmat, cmat, a_rows, w["d_skip"])
    if fetch is not None:
        w.update(fetch("b", yg))
    t = _mm(yg, w["w_glu"], name="glu_proj")
    kvm0, om0 = _mem_branch_fwd(memn0, w["w_mem_kv"][0], proj_a, "0")
    cat0 = _gate_a_fwd(y, t, w["b_glu"], proj_a, om0)
    o0 = _mm(cat0, w["w_out"][0], name="out_proj_0")
    h1 = _rmsnorm_fwd(o0, w["post_norm_g"][0], res=x, name="post_norm_0")

    kv_in = _rmsnorm_fwd(h1, w["kv_norm_g"], name="kv_norm", out_dtype=BF16)
    if fetch is not None:
        w.update(fetch("c", kv_in))
    kv = _mm(kv_in, w["w_kv"], name="kv_proj", out_dtype=BF16)
    pre_f = _mm(kv_in, w["w_fgate"], name="fgate_proj")
    b_f = jnp.pad(w["b_fgate"], (0, LANES - FOX_HEADS))
    fcum = _fgate_fwd(pre_f, b_f)
    fc = jnp.transpose(fcum[:, :FOX_HEADS])
    tq = min(FOX_BLOCK, L)
    fk = fc.reshape(FOX_HEADS, L // tq, 1, tq)

    hn1 = _rmsnorm_fwd(h1, w["pre_norm_g"][1], name="pre_norm_1", out_dtype=BF16)
    proj_b = _mm(hn1, w["w_in_b"], name="in_proj_b")
    att, lse = _fox_fwd(proj_b, kv, fk)
    kvm1, om1 = _mem_branch_fwd(memn1, w["w_mem_kv"][1], proj_b, "1")
    cat1 = _gate_b_fwd(att, proj_b, om1)
    o1 = _mm(cat1, w["w_out"][1], name="out_proj_1")
    dh2, loss_row = _final_norm_loss(o1, w["post_norm_g"][1], h1, target)

    do1, dpost1 = _rmsnorm_bwd(o1, w["post_norm_g"][1], dh2, name="post_norm_bwd_1", dx_dtype=BF16)
    dcat1 = _mm(do1, w["w_out"][1], tb=True, name="dcat_1", out_dtype=BF16)
    g["w_out_1"] = _mm(cat1, do1, ta=True, name="dw_out_1", out_dtype=BF16)
    datt, dproj_b, dom1, delta = _gate_b_bwd(dcat1, att, proj_b, om1)
    dproj_b, g["w_mem_kv_1"], dmemg1 = _mem_branch_bwd(mem, w["mem_norm_g"][1], w["w_mem_kv"][1], proj_b,
                                                      memn1, kvm1, dom1, dproj_b, "1")
    delta = delta.reshape(lse.shape)
    dproj_b, dfq = _fox_bwd_dq(proj_b, kv, fk, lse, delta, datt, dproj_b)
    dk, dv, dfk = _fox_bwd_dkv(proj_b, kv, fk, lse, delta, datt)
    g["w_in_b"] = _mm(hn1, dproj_b, ta=True, name="dw_in_b", out_dtype=BF16, shards=N_CHIPS)
    dhn1 = _mm(dproj_b, w["w_in_b"], tb=True, name="dhn_1")

    dkv = jnp.concatenate([dk, dv], axis=1)
    g["w_kv"] = _mm(kv_in, dkv, ta=True, name="dw_kv", out_dtype=BF16, shards=N_CHIPS)
    dkv_in_a = _mm(dkv, w["w_kv"], tb=True, name="dkv_in_kv")
    dfcum = _pad_lanes(jnp.transpose(dfq.reshape(FOX_HEADS, L) + dfk.reshape(FOX_HEADS, L)))
    dpre_f, db_f = _fgate_bwd(dfcum, pre_f, b_f)
    g["b_fgate"] = db_f[0, :FOX_HEADS]
    g["w_fgate"] = _mm(kv_in, dpre_f, ta=True, name="dw_fgate")[:, :FOX_HEADS]
    dkv_in_b = _mm(dpre_f, w["w_fgate"], tb=True, name="dkv_in_fgate")
    dh1, g["kv_norm_g"], dpre1 = _rmsnorm_bwd_pair(h1, w["kv_norm_g"], (dkv_in_a, dkv_in_b), w["pre_norm_g"][1],
                                                   dhn1, adds=(dh2,), name="kv_pre_norm_bwd")
    dh1 = grads_ready("b", g, dh1)

    do0, dpost0 = _rmsnorm_bwd(o0, w["post_norm_g"][0], dh1, name="post_norm_bwd_0", dx_dtype=BF16)
    dcat0 = _mm(do0, w["w_out"][0], tb=True, name="dcat_0", out_dtype=BF16)
    g["w_out_0"] = _mm(cat0, do0, ta=True, name="dw_out_0", out_dtype=BF16)
    dcat0 = grads_ready("b_send", g, dcat0)
    dproj_a, dt, dyg_a, dom0, db_glu = _gate_a_bwd(dcat0, y, t, w["b_glu"], proj_a, om0)
    g["b_glu"] = db_glu[0]
    g["w_glu"] = _mm(yg, dt, ta=True, name="dw_glu", out_dtype=BF16)
    dyg_b = _mm(dt, w["w_glu"], tb=True, name="dyg")
    dproj_a, g["w_mem_kv_0"], dmemg0 = _mem_branch_bwd(mem, w["mem_norm_g"][0], w["w_mem_kv"][0], proj_a,
                                                      memn0, kvm0, dom0, dproj_a, "0")
    dyg_b = grads_ready("a1", g, dyg_b)
    dproj_a, db_blk, dc_blk, da_rows, dd_skip = _s5_bwd(proj_a, dyg_a, dyg_b, y, xp, bmat, cmat, a_rows,
                                                        w["d_skip"], dproj_a)
    dproj_a = grads_ready("a1_send", g, dproj_a)
    g["d_skip"] = dd_skip[0]
    g["w_in_a"] = _mm(hn0, dproj_a, ta=True, name="dw_in_a", out_dtype=BF16, shards=N_CHIPS)
    dproj_a = grads_ready("a2", g, dproj_a)
    dhn0 = _mm(dproj_a, w["w_in_a"], tb=True, name="dhn_0")
    grad_x, dpre0 = _rmsnorm_bwd(x, w["pre_norm_g"][0], dhn0, adds=(dh1,), name="pre_norm_bwd_0")

    dbb = _s5_block_diag(db_blk)
    dcc = _s5_block_diag(dc_blk)
    g["c_re"], g["c_im"] = dcc[0], -dcc[1]
    d_ar = da_rows[:, 0, :STATE_COLS].reshape(SSM_GROUPS, SSM_STATE)
    d_ai = da_rows[:, 0, STATE_COLS:].reshape(SSM_GROUPS, SSM_STATE)
    dlr, dli, dls, dbr_t, dbi_t = _s5_prep_bwd(w["lam_re"], w["lam_im"], w["log_step"], b_re_t, b_im_t,
                                               d_ar, d_ai, dbb[0], dbb[1])
    g["lam_re"], g["lam_im"], g["log_step"] = dlr, dli, dls[:, 0]
    g["b_re"] = jnp.transpose(dbr_t, (0, 2, 1))
    g["b_im"] = jnp.transpose(dbi_t, (0, 2, 1))
    g["pre_norm_g"] = jnp.stack([dpre0, dpre1])
    g["post_norm_g"] = jnp.stack([dpost0, dpost1])
    g["mem_norm_g"] = jnp.stack([dmemg0, dmemg1])
    return loss_row, grad_x, g


_MESH = pl.DeviceIdType.MESH
_ANY = pl.BlockSpec(memory_space=pl.ANY)


def _place():
    x, y, c = lax.axis_index("x"), lax.axis_index("y"), lax.axis_index("c")
    chips = [(1 - x, y), (x, 1 - y), (1 - x, 1 - y)]
    return x, y, c, chips


_HBM = pl.BlockSpec(memory_space=pltpu.HBM)
_SEM = pl.BlockSpec(memory_space=pltpu.SEMAPHORE)
_SIDE = pltpu.SideEffectType.DATAFLOW_SIDE_EFFECTING


def _in_hbm(a):
    return pltpu.with_memory_space_constraint(a, pltpu.HBM)


def _hbm_like(a):
    return pltpu.HBM(a.shape, a.dtype)


def _ici_copies(srcs, lands, send_sem, recv_sem, src_at, dst_at, wait_at, to_sibling=False):
    x, y, c, chips = _place()
    peers = [(x, y, 1 - c)] if to_sibling else [(cx, cy, c) for cx, cy in chips]
    m = len(peers)
    start, wait = [], []
    for i in range(len(srcs)):
        for k, (px, py, pc) in enumerate(peers):
            sem = dict(send_sem=send_sem.at[m * i + k], recv_sem=recv_sem.at[m * i + k],
                       device_id=(px, py, pc), device_id_type=_MESH)
            src = src_at(srcs[i], 2 * px + py, c)
            start.append(pltpu.make_async_remote_copy(src_ref=src, dst_ref=dst_at(lands[i], 2 * x + y, k, c), **sem))
            wait.append(pltpu.make_async_remote_copy(src_ref=src, dst_ref=wait_at(lands[i], 2 * px + py, k, c), **sem))
    return start, wait


def _route_peers(route):
    return 1 if len(route) == 4 else 3


_BLOCK_ROUTE = (lambda s, j, c: s, lambda l, me, k, c: l.at[me, c], lambda l, j, k, c: l.at[j, c])


def _ici_start(srcs, lands, token, route, *, name):
    n = len(srcs)

    def body(*refs):
        start, _ = _ici_copies(refs[:n], refs[n:2 * n], refs[2 * n + 1], refs[2 * n + 2], *route)
        for cp in start:
            cp.start()

    sems = pltpu.SemaphoreType.DMA((_route_peers(route) * n,))
    outs = pl.pallas_call(
        body, name=name,
        out_shape=(sems, sems, *[_hbm_like(a) for a in srcs], *[_hbm_like(a) for a in lands], _hbm_like(token)),
        in_specs=[_HBM] * (2 * n + 1), out_specs=(_SEM, _SEM, *[_HBM] * (2 * n + 1)),
        input_output_aliases={i: 2 + i for i in range(2 * n + 1)},
        compiler_params=pltpu.CompilerParams(has_side_effects=_SIDE),
    )(*[_in_hbm(a) for a in srcs], *[_in_hbm(a) for a in lands], _in_hbm(token))
    return (outs[0], outs[1], list(outs[2:2 + n]), list(outs[2 + n:2 + 2 * n])), outs[2 + 2 * n]


def _ici_wait(handle, after, route, *, name):
    send_sem, recv_sem, srcs, lands = handle
    n = len(srcs)
    after = list(after) if isinstance(after, (list, tuple)) else [after]

    def body(*refs):
        _, wait = _ici_copies(refs[:n], refs[n:2 * n], refs[2 * n], refs[2 * n + 1], *route)
        for cp in wait:
            cp.wait_send()
            cp.wait_recv()

    outs = pl.pallas_call(
        body, name=name,
        out_shape=(*[_hbm_like(a) for a in srcs], *[_hbm_like(a) for a in lands]),
        in_specs=[_HBM] * (2 * n) + [_SEM, _SEM] + [_ANY] * len(after), out_specs=tuple([_HBM] * (2 * n)),
        input_output_aliases={i: i for i in range(2 * n)},
        compiler_params=pltpu.CompilerParams(has_side_effects=_SIDE),
    )(*srcs, *lands, send_sem, recv_sem, *after)
    return list(outs[:n]), list(outs[n:])


_GATHER_ROUTE = (lambda s, j, c: s.at[c], lambda l, me, k, c: l.at[me, c], lambda l, j, k, c: l.at[j, c])
_SCATTER_ROUTE = (lambda s, j, c: s.at[j], lambda l, me, k, c: l.at[k], lambda l, j, k, c: l.at[k])
_SWAP_ROUTE = (lambda s, j, c: s.at[:, 1 - c], lambda l, me, k, c: l, lambda l, j, k, c: l, True)


def _gather_forward(lands, tag, own=False):
    n = len(lands)
    m = 4 if own else 3

    def body(*refs):
        ins, outs = refs[:n], refs[n:2 * n]
        send_sem, recv_sem = refs[2 * n:]
        x, y, c, chips = _place()
        slots = [2 * cx + cy for cx, cy in chips] + [2 * x + y]

        def copy(i, k, half):
            return pltpu.make_async_remote_copy(
                src_ref=ins[i].at[slots[k], half], dst_ref=outs[i].at[slots[k], half],
                send_sem=send_sem.at[m * i + k], recv_sem=recv_sem.at[m * i + k],
                device_id=(x, y, 1 - c), device_id_type=_MESH)

        copies = [copy(i, k, c) for i in range(n) for k in range(m)]
        for cp in copies:
            cp.start()
        for i in range(n):
            for k in range(m):
                copy(i, k, 1 - c).wait_recv()
        for cp in copies:
            cp.wait_send()

    return pl.pallas_call(
        body, name="gather_forward_to_sibling_" + tag,
        out_shape=[jax.ShapeDtypeStruct(a.shape, a.dtype) for a in lands],
        in_specs=[_ANY] * n, out_specs=[_ANY] * n,
        input_output_aliases={i: i for i in range(n)},
        scratch_shapes=[pltpu.SemaphoreType.DMA((m * n,)), pltpu.SemaphoreType.DMA((m * n,))],
    )(*lands)


def _swap_halves(grads, tag):
    n = len(grads)

    def body(*refs):
        ins, outs = refs[:n], refs[n:2 * n]
        send_sem, recv_sem = refs[2 * n:]
        x, y, c, _ = _place()
        copies = [pltpu.make_async_remote_copy(
            src_ref=ins[i].at[:, 1 - c], dst_ref=outs[i],
            send_sem=send_sem.at[i], recv_sem=recv_sem.at[i],
            device_id=(x, y, 1 - c), device_id_type=_MESH) for i in range(n)]
        for cp in copies:
            cp.start()
        for cp in copies:
            cp.wait()

    return pl.pallas_call(
        body, name="grad_swap_halves_" + tag,
        out_shape=[jax.ShapeDtypeStruct((N_CHIPS,) + g.shape[2:], g.dtype) for g in grads],
        in_specs=[_ANY] * n, out_specs=[_ANY] * n,
        scratch_shapes=[pltpu.SemaphoreType.DMA((n,)), pltpu.SemaphoreType.DMA((n,))],
    )(*grads)


def _sum_rows(h, C):
    return max(d for d in range(SUBLANES, h + 1, SUBLANES) if h % d == 0 and d * C <= 1 << 20)


SUM_STEPS = 4


def _pair_sums(gs, rs, c_idx, *, name):
    n = len(gs)
    rows = [g.shape[2] // SUM_STEPS for g in gs]

    def body(c_ref, *refs):
        for g_ref, r_ref, o_ref in zip(refs[:n], refs[n:2 * n], refs[2 * n:]):
            o_ref[...] = (g_ref[...].astype(F32) + r_ref[...].astype(F32)).astype(o_ref.dtype)

    return pl.pallas_call(
        body, name=name,
        out_shape=[jax.ShapeDtypeStruct((N_CHIPS,) + g.shape[2:], g.dtype) for g in gs],
        grid_spec=pltpu.PrefetchScalarGridSpec(
            num_scalar_prefetch=1, grid=(N_CHIPS, SUM_STEPS),
            in_specs=[pl.BlockSpec((None, None, tr, g.shape[3]), lambda j, i, s: (j, s[0], i, 0))
                      for g, tr in zip(gs, rows)]
            + [pl.BlockSpec((None, tr, g.shape[3]), lambda j, i, s: (j, i, 0)) for g, tr in zip(gs, rows)],
            out_specs=[pl.BlockSpec((None, tr, g.shape[3]), lambda j, i, s: (j, i, 0)) for g, tr in zip(gs, rows)]),
        compiler_params=_params("parallel", "parallel"),
    )(c_idx, *gs, *rs)


def _owner_sums(ss, rs, jc_idx, *, name):
    n = len(ss)
    rows = [s.shape[1] // SUM_STEPS for s in ss]

    def body(jc_ref, *refs):
        for s_ref, r_ref, o_ref in zip(refs[:n], refs[n:2 * n], refs[2 * n:]):
            acc = s_ref[...].astype(F32)
            for k in range(3):
                acc = acc + r_ref[k].astype(F32)
            o_ref[...] = acc

    return pl.pallas_call(
        body, name=name,
        out_shape=[jax.ShapeDtypeStruct((2,) + s.shape[1:], F32) for s in ss],
        grid_spec=pltpu.PrefetchScalarGridSpec(
            num_scalar_prefetch=1, grid=(SUM_STEPS,),
            in_specs=[pl.BlockSpec((None, tr, s.shape[2]), lambda i, p: (p[0], i, 0)) for s, tr in zip(ss, rows)]
            + [pl.BlockSpec((3, tr, s.shape[2]), lambda i, p: (0, i, 0)) for s, tr in zip(ss, rows)],
            out_specs=[pl.BlockSpec((None, tr, s.shape[2]), lambda i, p: (p[1], i, 0)) for s, tr in zip(ss, rows)]),
        compiler_params=_params("parallel"),
    )(jc_idx, *ss, *rs)


def _share_with_sibling(bufs, tag):
    n = len(bufs)

    def body(*refs):
        ins, outs = refs[:n], refs[n:2 * n]
        send_sem, recv_sem = refs[2 * n:]
        x, y, c, _ = _place()

        def copy(i, half):
            return pltpu.make_async_remote_copy(
                src_ref=ins[i].at[half], dst_ref=outs[i].at[half],
                send_sem=send_sem.at[i], recv_sem=recv_sem.at[i],
                device_id=(x, y, 1 - c), device_id_type=_MESH)

        copies = [copy(i, c) for i in range(n)]
        for cp in copies:
            cp.start()
        for i in range(n):
            copy(i, 1 - c).wait_recv()
        for cp in copies:
            cp.wait_send()

    return pl.pallas_call(
        body, name="grad_share_with_sibling_" + tag,
        out_shape=[jax.ShapeDtypeStruct(b.shape, b.dtype) for b in bufs],
        in_specs=[_ANY] * n, out_specs=[_ANY] * n,
        input_output_aliases={i: i for i in range(n)},
        scratch_shapes=[pltpu.SemaphoreType.DMA((n,)), pltpu.SemaphoreType.DMA((n,))],
    )(*bufs)


def _chip_sums(grads, c_idx, tag):
    views = [g.reshape(N_CHIPS, 2, g.shape[1] // 2, g.shape[2]) for g in grads]
    arrived = _swap_halves(views, tag)
    return _pair_sums(views, arrived, c_idx, name=f"grad_pair_sums_{tag}")


def _owner_totals(sums, arrived, jc_idx, tag):
    halves = _owner_sums(sums, arrived, jc_idx, name=f"grad_owner_sums_{tag}")
    return [f.reshape(-1, f.shape[2]) for f in _share_with_sibling(halves, tag)]


def _sum_devices(blocks):
    R = blocks.shape[2]
    tr = _sum_rows(R, 2 * N_CHIPS * LANES)

    def body(b_ref, o_ref):
        acc = b_ref[0, 0]
        for d in range(1, 2 * N_CHIPS):
            acc = acc + b_ref[d // 2, d % 2]
        o_ref[...] = acc

    return pl.pallas_call(
        body, name="sum_small_over_devices", out_shape=jax.ShapeDtypeStruct((R, LANES), F32),
        grid=(R // tr,),
        in_specs=[pl.BlockSpec((N_CHIPS, 2, tr, LANES), lambda i: (0, 0, i, 0))],
        out_specs=pl.BlockSpec((tr, LANES), lambda i: (i, 0)),
        compiler_params=_params("parallel"),
    )(blocks)


def _adamw(w, g, m, v, *, name):
    R, C = w.shape
    whole_fits = 7 * 2 * R * C * 4 <= VMEM_LIMIT_BYTES // 2
    tr = R if whole_fits else next(c for c in (256, 192, 128, 64, 32, 16, 8) if R % c == 0)

    def body(w_ref, g_ref, m_ref, v_ref, d_ref, nm_ref, nv_ref):
        g = g_ref[...]
        m = ADAM_B1 * m_ref[...] + (1.0 - ADAM_B1) * g
        v = ADAM_B2 * v_ref[...] + (1.0 - ADAM_B2) * (g * g)
        nm_ref[...] = m
        nv_ref[...] = v
        m_hat = m / (1.0 - ADAM_B1 ** ADAM_STEP)
        v_hat = v / (1.0 - ADAM_B2 ** ADAM_STEP)
        d_ref[...] = -ADAM_LR * (m_hat / (jnp.sqrt(v_hat) + ADAM_EPS) + ADAM_WD * w_ref[...])

    blk = pl.BlockSpec((tr, C), lambda i: (i, 0))
    sds = jax.ShapeDtypeStruct((R, C), F32)
    return pl.pallas_call(
        body, name=name, out_shape=(sds, sds, sds), grid=(R // tr,),
        in_specs=[blk] * 4, out_specs=(blk, blk, blk),
        compiler_params=_params("parallel"),
    )(w, g, m, v)


_TILE = SUBLANES * LANES


def _pack(arrays):
    rows = []
    for a in arrays:
        flat = a.reshape(-1)
        flat = jnp.pad(flat, (0, (-flat.shape[0]) % _TILE))
        rows.append(flat.reshape(-1, LANES))
    return jnp.concatenate(rows, axis=0)


def _unpack(buf, shapes):
    out, r = [], 0
    for s in shapes:
        size = math.prod(s)
        nr = -(-size // _TILE) * SUBLANES
        out.append(buf[r:r + nr].reshape(-1)[:size].reshape(s))
        r += nr
    return out


_BIG = ("w_in_a", "w_glu", "w_kv", "w_in_b", "w_mem_kv", "w_out")
_REPLICATED = ("pre_norm_g", "post_norm_g", "lam_re", "lam_im", "log_step", "b_re", "b_im", "c_re", "c_im",
               "kv_norm_g", "b_fgate", "mem_norm_g")
_SHARDED_SMALL = ("d_skip", "b_glu", "w_fgate")
_WEIGHTS = ("pre_norm_g", "post_norm_g", "w_in_a", "lam_re", "lam_im", "log_step", "b_re", "b_im", "c_re",
            "c_im", "d_skip", "w_glu", "b_glu", "kv_norm_g", "w_kv", "w_fgate", "b_fgate", "w_in_b",
            "mem_norm_g", "w_mem_kv", "w_out")


def _halves(a):
    return a.reshape(2, a.shape[0] // 2, a.shape[1])


def _unhalve(a):
    return a.reshape(N_CHIPS, 2 * a.shape[2], a.shape[3])


def _columns(a):
    return jnp.transpose(a, (1, 0, 2)).reshape(a.shape[1], N_CHIPS * a.shape[2])


def kernel(x, mem, pre_norm_g, post_norm_g, w_in_a, lam_re, lam_im, log_step, b_re, b_im, c_re, c_im, d_skip, w_glu, b_glu, kv_norm_g, w_kv, w_fgate, b_fgate, w_in_b, mem_norm_g, w_mem_kv, w_out, loss_target, m_pre_norm_g, m_post_norm_g, m_w_in_a, m_lam_re, m_lam_im, m_log_step, m_b_re, m_b_im, m_c_re, m_c_im, m_d_skip, m_w_glu, m_b_glu, m_kv_norm_g, m_w_kv, m_w_fgate, m_b_fgate, m_w_in_b, m_mem_norm_g, m_w_mem_kv, m_w_out, v_pre_norm_g, v_post_norm_g, v_w_in_a, v_lam_re, v_lam_im, v_log_step, v_b_re, v_b_im, v_c_re, v_c_im, v_d_skip, v_w_glu, v_b_glu, v_kv_norm_g, v_w_kv, v_w_fgate, v_b_fgate, v_w_in_b, v_mem_norm_g, v_w_mem_kv, v_w_out):
    a = dict(locals())
    xi, yi, ci = lax.axis_index("x"), lax.axis_index("y"), lax.axis_index("c")
    chip = 2 * xi + yi
    c_idx = jnp.reshape(ci, (1,)).astype(jnp.int32)
    jc_idx = jnp.stack([chip, ci]).astype(jnp.int32)

    vec = jnp.zeros((2 * SUBLANES, MAIN_WIDTH // N_CHIPS), F32)
    vec = vec.at[0].set(a["d_skip"][0]).at[1].set(a["b_glu"][0])
    def own_slot(gathered, parts):
        return [lax.dynamic_update_index_in_dim(g, p, chip, 0) for g, p in zip(gathered, parts)]

    parts_a = [_halves(a["w_in_a"][0].astype(BF16)), _halves(vec)]
    parts_b = [_halves(a["w_glu"][0].astype(BF16)), _halves(a["w_mem_kv"].reshape(-1, 2 * MEM_WIDTH).astype(BF16)),
               _halves(a["w_out"].reshape(-1, D_MODEL).astype(BF16))]
    parts_c = [_halves(a["w_kv"].astype(BF16)), _halves(_pad_lanes(a["w_fgate"]).astype(BF16)),
               _halves(a["w_in_b"][0].astype(BF16))]
    travelling, token = {}, a["pre_norm_g"]
    for tag, parts in (("a", parts_a), ("b", parts_b), ("c", parts_c)):
        lands = [lax.empty((N_CHIPS,) + p.shape, p.dtype) for p in parts]
        travelling[tag], token = _ici_start(parts, lands, token, _GATHER_ROUTE, name=f"gather_{tag}_start")

    def fetch(tag, after):
        parts, lands = _ici_wait(travelling[tag], after, _GATHER_ROUTE, name=f"gather_{tag}_wait")
        full = own_slot(_gather_forward(lands, tag), parts)
        if tag == "a":
            w_in_a, vecs = full
            return dict(w_in_a=_columns(_unhalve(w_in_a)), d_skip=vecs[:, 0, 0, :].reshape(MAIN_WIDTH),
                        b_glu=vecs[:, 0, 1, :].reshape(MAIN_WIDTH))
        if tag == "b":
            w_glu, w_mk, w_out = full
            return dict(w_glu=w_glu.reshape(MAIN_WIDTH, MAIN_WIDTH),
                        w_mem_kv=[w_mk[:, i].reshape(D_MODEL, 2 * MEM_WIDTH) for i in range(2)],
                        w_out=[w_out[:, i].reshape(D_MODEL, D_MODEL) for i in range(2)])
        w_kv, w_fg, w_in_b = full
        return dict(w_kv=_columns(_unhalve(w_kv)), w_fgate=w_fg.reshape(D_MODEL, LANES),
                    w_in_b=_columns(_unhalve(w_in_b)))

    w = dict(
        pre_norm_g=token, post_norm_g=a["post_norm_g"], mem_norm_g=a["mem_norm_g"],
        kv_norm_g=a["kv_norm_g"], b_fgate=a["b_fgate"],
        lam_re=a["lam_re"][0], lam_im=a["lam_im"][0], log_step=a["log_step"][0],
        b_re=a["b_re"][0], b_im=a["b_im"][0], c_re=a["c_re"][0], c_im=a["c_im"][0])

    sent = {}

    swapping = {}

    def grads_ready(event, g, token):
        tag = event.split("_")[0]
        if event in ("b", "a1"):
            big = {"b": lambda: [g["w_kv"], g["w_in_b"], g["w_mem_kv_1"].reshape(N_CHIPS, -1, 2 * MEM_WIDTH),
                                 g["w_out_1"].reshape(N_CHIPS, -1, D_MODEL)],
                   "a1": lambda: [g["w_glu"].reshape(N_CHIPS, -1, MAIN_WIDTH),
                                  g["w_mem_kv_0"].reshape(N_CHIPS, -1, 2 * MEM_WIDTH),
                                  g["w_out_0"].reshape(N_CHIPS, -1, D_MODEL)]}[tag]()
            views = [b.reshape(N_CHIPS, 2, b.shape[1] // 2, b.shape[2]) for b in big]
            lands = [lax.empty((N_CHIPS,) + v.shape[2:], v.dtype) for v in views]
            swapping[tag], token = _ici_start(views, lands, token, _SWAP_ROUTE, name=f"grad_swap_{tag}_start")
            return token
        if event == "a2":
            sums = _chip_sums([g["w_in_a"]], c_idx, tag)
        else:
            views, arrived = _ici_wait(swapping[tag], token, _SWAP_ROUTE, name=f"grad_swap_{tag}_wait")
            sums = _pair_sums(views, arrived, c_idx, name=f"grad_pair_sums_{tag}")
        lands = [lax.empty((3,) + s.shape[1:], s.dtype) for s in sums]
        sent[tag], token = _ici_start(sums, lands, token, _SCATTER_ROUTE, name=f"grad_send_{tag}_start")
        return token

    loss_row, grad_x, g = _local_step(a["x"][0], a["mem"][0], a["loss_target"][0], w, fetch, grads_ready)

    small_names = _REPLICATED + _SHARDED_SMALL
    pack = _pack([g[n] for n in small_names])
    blocks = lax.empty((N_CHIPS, 2) + pack.shape, F32)
    small_sent, loss_row = _ici_start([pack], [blocks], loss_row, _BLOCK_ROUTE, name="small_sums_start")
    loss = lax.psum(jnp.sum(loss_row), MESH_AXES)

    def totals(tag, after):
        sums, arrived = _ici_wait(sent[tag], after, _SCATTER_ROUTE, name=f"grad_send_{tag}_wait")
        return _owner_totals(sums, arrived, jc_idx, tag)

    r_kv, r_in_b, r_mk1, r_out1 = totals("b", grad_x)
    r_glu, r_mk0, r_out0 = totals("a1", r_out1)
    (r_in_a,) = totals("a2", r_out0)
    grads = {"w_in_a": r_in_a[None], "w_glu": r_glu[None], "w_kv": r_kv, "w_in_b": r_in_b[None],
             "w_mem_kv": jnp.stack([r_mk0, r_mk1]), "w_out": jnp.stack([r_out0, r_out1])}

    delta, new_m, new_v = {}, {}, {}
    for n in _BIG:
        shape = a[n].shape
        d2 = (-1, shape[-1])
        d, m, v = _adamw(a[n].reshape(d2), grads[n].reshape(d2), a["m_" + n].reshape(d2),
                         a["v_" + n].reshape(d2), name="adamw_" + n)
        delta[n], new_m[n], new_v[n] = d.reshape(shape), m.reshape(shape), v.reshape(shape)

    (pack,), (blocks,) = _ici_wait(small_sent, [delta[n] for n in _BIG], _BLOCK_ROUTE, name="small_sums_wait")
    blocks = lax.dynamic_update_slice(blocks, pack[None, None], (chip, ci, 0, 0))
    (blocks,) = _gather_forward([blocks], "small", own=True)
    small = dict(zip(small_names, _unpack(_sum_devices(blocks), [g[n].shape for n in small_names])))
    for n in _REPLICATED:
        grads[n] = small[n].reshape(a[n].shape)
    nd = MAIN_WIDTH // N_CHIPS
    grads["d_skip"] = lax.dynamic_slice(small["d_skip"], (chip * nd,), (nd,))[None]
    grads["b_glu"] = lax.dynamic_slice(small["b_glu"], (chip * nd,), (nd,))[None]
    nf = D_MODEL // N_CHIPS
    grads["w_fgate"] = lax.dynamic_slice(small["w_fgate"], (chip * nf, 0), (nf, FOX_HEADS))

    shapes = [a[n].shape for n in small_names]
    d, m, v = _adamw(_pack([a[n] for n in small_names]), _pack([grads[n] for n in small_names]),
                     _pack([a["m_" + n] for n in small_names]), _pack([a["v_" + n] for n in small_names]),
                     name="adamw_small")
    for n, dd, mm, vv in zip(small_names, _unpack(d, shapes), _unpack(m, shapes), _unpack(v, shapes)):
        delta[n], new_m[n], new_v[n] = dd, mm, vv

    return (loss, grad_x[None], *[grads[n] for n in _WEIGHTS], *[delta[n] for n in _WEIGHTS],
            *[new_m[n] for n in _WEIGHTS], *[new_v[n] for n in _WEIGHTS])
```

```python
import math

import jax
import jax.numpy as jnp
from jax import lax
from jax.experimental import pallas as pl
from jax.experimental.pallas import tpu as pltpu

F32 = jnp.float32
BF16 = jnp.bfloat16

D_MODEL = 2048
N_MEM = 256
MAIN_WIDTH = 1536
MEM_WIDTH = 512
IN_WIDTH = 2 * MAIN_WIDTH + 2 * MEM_WIDTH
HEAD_DIM = 128
FOX_HEADS = MAIN_WIDTH // HEAD_DIM
MEM_HEADS = MEM_WIDTH // HEAD_DIM
SSM_GROUP = 16
SSM_GROUPS = MAIN_WIDTH // SSM_GROUP
SSM_STATE = 64
GROUPS_PER_BLOCK = 8
SSM_BLOCKS = SSM_GROUPS // GROUPS_PER_BLOCK
STATE_COLS = GROUPS_PER_BLOCK * SSM_STATE
EPS = 1e-6
ADAM_LR = 0.001
ADAM_B1 = 0.9
ADAM_B2 = 0.999
ADAM_EPS = 1e-08
ADAM_WD = 0.01
ADAM_STEP = 10
N_CHIPS = 4
LANES = 128
SUBLANES = 8
VMEM_LIMIT_BYTES = 56 * 1024 * 1024
NEG_BIG = -1e30
MESH_AXES = ("x", "y", "c")


def _params(*sem):
    return pltpu.CompilerParams(dimension_semantics=sem if sem else None,
                                vmem_limit_bytes=VMEM_LIMIT_BYTES)


def _sigmoid(x):
    return 1.0 / (1.0 + jnp.exp(-x))


def _gelu(x):
    c = math.sqrt(2.0 / math.pi)
    return 0.5 * x * (1.0 + jnp.tanh(c * (x + 0.044715 * (x * x * x))))


def _gelu_grad(x):
    c = math.sqrt(2.0 / math.pi)
    t = jnp.tanh(c * (x + 0.044715 * (x * x * x)))
    return 0.5 * (1.0 + t) + 0.5 * x * (1.0 - t * t) * (c * (1.0 + 3.0 * 0.044715 * (x * x)))


def _silu_and_grad(z):
    s = _sigmoid(z)
    return z * s, s * (1.0 + z * (1.0 - s))


_TILE_CHOICES = (4096, 3072, 2048, 1536, 1024, 768, 512, 384, 256, LANES)


def _tile(n, cap):
    return next(c for c in _TILE_CHOICES if c <= cap and n % c == 0)


def _mm(a, b, *, name, ta=False, tb=False, out_dtype=F32, shards=1, tm=1024, tn=1024, tk=4096):
    if ta:
        K, M = a.shape
    else:
        M, K = a.shape
    if tb:
        N, kb = b.shape
    else:
        kb, N = b.shape
    assert K == kb, (a.shape, b.shape)
    ns = N // shards
    tm, tn, tk = _tile(M, tm), _tile(ns, tn), _tile(K, tk)
    assert M % tm == 0 and ns % tn == 0 and K % tk == 0 and N % shards == 0
    nk = K // tk
    dn = (((0 if ta else 1,), (1 if tb else 0,)), ((), ()))

    def body(a_ref, b_ref, o_ref, *acc):
        prod = lax.dot_general(a_ref[...].astype(BF16), b_ref[...].astype(BF16), dn, preferred_element_type=F32)
        if nk == 1:
            o_ref[...] = prod.astype(o_ref.dtype)
            return
        acc_ref, = acc
        k = pl.program_id(2)

        @pl.when(k == 0)
        def _():
            acc_ref[...] = jnp.zeros_like(acc_ref)

        acc_ref[...] += prod

        @pl.when(k == nk - 1)
        def _():
            o_ref[...] = acc_ref[...].astype(o_ref.dtype)

    a_spec = (pl.BlockSpec((tk, tm), lambda i, j, k: (k, i)) if ta
              else pl.BlockSpec((tm, tk), lambda i, j, k: (i, k)))
    b_spec = (pl.BlockSpec((tn, tk), lambda i, j, k: (j, k)) if tb
              else pl.BlockSpec((tk, tn), lambda i, j, k: (k, j)))
    if shards == 1:
        out_shape = jax.ShapeDtypeStruct((M, N), out_dtype)
        o_spec = pl.BlockSpec((tm, tn), lambda i, j, k: (i, j))
    else:
        nb = ns // tn
        out_shape = jax.ShapeDtypeStruct((shards, M, ns), out_dtype)
        o_spec = pl.BlockSpec((None, tm, tn), lambda i, j, k: (j // nb, i, j % nb))
    return pl.pallas_call(
        body, name=name, out_shape=out_shape,
        grid=(M // tm, N // tn, nk),
        in_specs=[a_spec, b_spec], out_specs=o_spec,
        scratch_shapes=[] if nk == 1 else [pltpu.VMEM((tm, tn), F32)],
        compiler_params=_params("parallel", "parallel", "arbitrary"),
    )(a, b)


def _rmsnorm_fwd(x, g, *, name, res=None, out_dtype=F32, tr=256):
    L, D = x.shape
    tr = min(tr, L)
    has_res = res is not None

    def body(*refs):
        if has_res:
            x_ref, g_ref, r_ref, o_ref = refs
        else:
            x_ref, g_ref, o_ref = refs
        xf = x_ref[...]
        r = lax.rsqrt(jnp.mean(xf * xf, axis=-1, keepdims=True) + EPS)
        y = xf * r * g_ref[...]
        if has_res:
            y = r_ref[...] + y
        o_ref[...] = y.astype(o_ref.dtype)

    row = pl.BlockSpec((tr, D), lambda i: (i, 0))
    vec = pl.BlockSpec((1, D), lambda i: (0, 0))
    ins = [x, g.reshape(1, D)] + ([res] if has_res else [])
    return pl.pallas_call(
        body, name=name, out_shape=jax.ShapeDtypeStruct((L, D), out_dtype),
        grid=(L // tr,), in_specs=[row, vec] + ([row] if has_res else []), out_specs=row,
        compiler_params=_params("parallel"),
    )(*ins)


def _rmsnorm_bwd(x, g, dy, *, name, adds=(), dx_dtype=F32, tr=256):
    L, D = x.shape
    tr = min(tr, L)
    dys = dy if isinstance(dy, tuple) else (dy,)
    n_dy, n_add = len(dys), len(adds)

    def body(*refs):
        x_ref, g_ref = refs[:2]
        dy_refs = refs[2:2 + n_dy]
        add_refs = refs[2 + n_dy:2 + n_dy + n_add]
        dx_ref, dg_ref = refs[2 + n_dy + n_add:]
        xf = x_ref[...]
        dyf = dy_refs[0][...].astype(F32)
        for d_ref in dy_refs[1:]:
            dyf = dyf + d_ref[...].astype(F32)
        r = lax.rsqrt(jnp.mean(xf * xf, axis=-1, keepdims=True) + EPS)
        gy = dyf * g_ref[...]
        c = jnp.mean(xf * gy, axis=-1, keepdims=True) * (r * r * r)
        dx = gy * r - xf * c
        for a_ref in add_refs:
            dx = dx + a_ref[...].astype(F32)
        dx_ref[...] = dx.astype(dx_ref.dtype)

        @pl.when(pl.program_id(0) == 0)
        def _():
            dg_ref[...] = jnp.zeros_like(dg_ref)

        dg_ref[...] += jnp.sum(dyf * xf * r, axis=0, keepdims=True)

    row = pl.BlockSpec((tr, D), lambda i: (i, 0))
    vec = pl.BlockSpec((1, D), lambda i: (0, 0))
    dx, dg = pl.pallas_call(
        body, name=name,
        out_shape=(jax.ShapeDtypeStruct((L, D), dx_dtype), jax.ShapeDtypeStruct((1, D), F32)),
        grid=(L // tr,), in_specs=[row, vec] + [row] * (n_dy + n_add), out_specs=(row, vec),
        compiler_params=_params("arbitrary"),
    )(x, g.reshape(1, D), *dys, *adds)
    return dx, dg.reshape(D)


def _rmsnorm_bwd_pair(x, g1, dy1, g2, dy2, *, name, adds=(), tr=256):
    L, D = x.shape
    tr = min(tr, L)
    dy1s = dy1 if isinstance(dy1, tuple) else (dy1,)
    n1, n_add = len(dy1s), len(adds)

    def body(*refs):
        x_ref, g1_ref, g2_ref = refs[:3]
        dy1_refs = refs[3:3 + n1]
        dy2_ref = refs[3 + n1]
        add_refs = refs[4 + n1:4 + n1 + n_add]
        dx_ref, dg1_ref, dg2_ref = refs[4 + n1 + n_add:]
        xf = x_ref[...]
        d1 = dy1_refs[0][...].astype(F32)
        for d_ref in dy1_refs[1:]:
            d1 = d1 + d_ref[...].astype(F32)
        d2 = dy2_ref[...].astype(F32)
        r = lax.rsqrt(jnp.mean(xf * xf, axis=-1, keepdims=True) + EPS)
        gy = d1 * g1_ref[...] + d2 * g2_ref[...]
        c = jnp.mean(xf * gy, axis=-1, keepdims=True) * (r * r * r)
        dx = gy * r - xf * c
        for a_ref in add_refs:
            dx = dx + a_ref[...].astype(F32)
        dx_ref[...] = dx

        @pl.when(pl.program_id(0) == 0)
        def _():
            dg1_ref[...] = jnp.zeros_like(dg1_ref)
            dg2_ref[...] = jnp.zeros_like(dg2_ref)

        xr = xf * r
        dg1_ref[...] += jnp.sum(d1 * xr, axis=0, keepdims=True)
        dg2_ref[...] += jnp.sum(d2 * xr, axis=0, keepdims=True)

    row = pl.BlockSpec((tr, D), lambda i: (i, 0))
    vec = pl.BlockSpec((1, D), lambda i: (0, 0))
    dx, dg1, dg2 = pl.pallas_call(
        body, name=name,
        out_shape=(jax.ShapeDtypeStruct((L, D), F32), jax.ShapeDtypeStruct((1, D), F32),
                   jax.ShapeDtypeStruct((1, D), F32)),
        grid=(L // tr,), in_specs=[row, vec, vec] + [row] * (n1 + 1 + n_add), out_specs=(row, vec, vec),
        compiler_params=_params("arbitrary"),
    )(x, g1.reshape(1, D), g2.reshape(1, D), *dy1s, dy2, *adds)
    return dx, dg1.reshape(D), dg2.reshape(D)


def _final_norm_loss(o, g, res, target, *, tr=256):
    L, D = o.shape
    tr = min(tr, L)

    def body(o_ref, g_ref, r_ref, t_ref, dh_ref, loss_ref):
        xf = o_ref[...]
        r = lax.rsqrt(jnp.mean(xf * xf, axis=-1, keepdims=True) + EPS)
        e = (r_ref[...] + xf * r * g_ref[...]) - t_ref[...]
        dh_ref[...] = e * (1.0 / D)

        @pl.when(pl.program_id(0) == 0)
        def _():
            loss_ref[...] = jnp.zeros_like(loss_ref)

        loss_ref[...] += jnp.sum(e * e, axis=0, keepdims=True) * (0.5 / D)

    row = pl.BlockSpec((tr, D), lambda i: (i, 0))
    vec = pl.BlockSpec((1, D), lambda i: (0, 0))
    dh, lp = pl.pallas_call(
        body, name="post_norm_1_loss",
        out_shape=(jax.ShapeDtypeStruct((L, D), F32), jax.ShapeDtypeStruct((1, D), F32)),
        grid=(L // tr,), in_specs=[row, vec, row, row], out_specs=(row, vec),
        compiler_params=_params("arbitrary"),
    )(o, g.reshape(1, D), res, target)
    return dh, lp


def _s5_coeffs(lr, li, ls):
    dt = jnp.exp(ls)
    mag = jnp.exp(lr * dt)
    ar = mag * jnp.cos(li * dt)
    ai = mag * jnp.sin(li * dt)
    den = lr * lr + li * li
    cr = ((ar - 1.0) * lr + ai * li) / den
    ci = (ai * lr - (ar - 1.0) * li) / den
    return dt, ar, ai, den, cr, ci


def _s5_prep(lam_re, lam_im, log_step, b_re_t, b_im_t):
    G, P = lam_re.shape
    H = b_re_t.shape[1]

    def body(lr_ref, li_ref, ls_ref, br_ref, bi_ref, ar_ref, ai_ref, bbr_ref, bbi_ref):
        _, ar, ai, _, cr, ci = _s5_coeffs(lr_ref[...], li_ref[...], ls_ref[...])
        ar_ref[...] = ar
        ai_ref[...] = ai
        br, bi = br_ref[...], bi_ref[...]
        crb, cib = cr[:, None, :], ci[:, None, :]
        bbr_ref[...] = crb * br - cib * bi
        bbi_ref[...] = crb * bi + cib * br

    return pl.pallas_call(
        body, name="s5_prep",
        out_shape=(jax.ShapeDtypeStruct((G, P), F32), jax.ShapeDtypeStruct((G, P), F32),
                   jax.ShapeDtypeStruct((G, H, P), F32), jax.ShapeDtypeStruct((G, H, P), F32)),
        compiler_params=_params(),
    )(lam_re, lam_im, log_step.reshape(G, 1), b_re_t, b_im_t)


def _s5_prep_bwd(lam_re, lam_im, log_step, b_re_t, b_im_t, d_ar, d_ai, d_bbr, d_bbi):
    G, P = lam_re.shape
    H = b_re_t.shape[1]

    def body(lr_ref, li_ref, ls_ref, br_ref, bi_ref, dar_ref, dai_ref, dbbr_ref, dbbi_ref,
             dlr_ref, dli_ref, dls_ref, dbr_ref, dbi_ref):
        lr, li = lr_ref[...], li_ref[...]
        dt, ar, ai, den, cr, ci = _s5_coeffs(lr, li, ls_ref[...])
        br, bi = br_ref[...], bi_ref[...]
        gbr, gbi = dbbr_ref[...], dbbi_ref[...]
        crb, cib = cr[:, None, :], ci[:, None, :]
        dbr_ref[...] = crb * gbr + cib * gbi
        dbi_ref[...] = crb * gbi - cib * gbr
        gcr = jnp.sum(br * gbr + bi * gbi, axis=1)
        gci = jnp.sum(br * gbi - bi * gbr, axis=1)
        ilr, ili = lr / den, -li / den
        gar = dar_ref[...] + (ilr * gcr + ili * gci)
        gai = dai_ref[...] + (ilr * gci - ili * gcr)
        qr, qi = cr * ilr - ci * ili, cr * ili + ci * ilr
        glr = -(qr * gcr + qi * gci)
        gli = -(qr * gci - qi * gcr)
        glr = glr + dt * (ar * gar + ai * gai)
        gli = gli + dt * (ar * gai - ai * gar)
        wr, wi = lr * ar - li * ai, lr * ai + li * ar
        gdt = jnp.sum(wr * gar + wi * gai, axis=1, keepdims=True)
        dlr_ref[...] = glr
        dli_ref[...] = gli
        dls_ref[...] = gdt * dt

    return pl.pallas_call(
        body, name="s5_prep_bwd",
        out_shape=(jax.ShapeDtypeStruct((G, P), F32), jax.ShapeDtypeStruct((G, P), F32),
                   jax.ShapeDtypeStruct((G, 1), F32),
                   jax.ShapeDtypeStruct((G, H, P), F32), jax.ShapeDtypeStruct((G, H, P), F32)),
        compiler_params=_params(),
    )(lam_re, lam_im, log_step.reshape(G, 1), b_re_t, b_im_t, d_ar, d_ai, d_bbr, d_bbi)


def _s5_block_mats(bbr_t, bbi_t, c_re, c_im):
    bmat = _s5_expand(bbr_t, bbi_t)
    cmat = jnp.transpose(_s5_expand(c_re, -c_im), (0, 2, 1))
    return bmat.astype(BF16), cmat.astype(BF16)


def _s5_diag_mask():
    r = lax.broadcasted_iota(jnp.int32, (LANES, 2 * STATE_COLS), 0) // SSM_GROUP
    c = (lax.broadcasted_iota(jnp.int32, (LANES, 2 * STATE_COLS), 1) % STATE_COLS) // SSM_STATE
    return (r == c).astype(F32)


def _s5_expand(re, im):
    re = jnp.tile(re.reshape(SSM_BLOCKS, LANES, SSM_STATE), (1, 1, GROUPS_PER_BLOCK))
    im = jnp.tile(im.reshape(SSM_BLOCKS, LANES, SSM_STATE), (1, 1, GROUPS_PER_BLOCK))
    return jnp.concatenate([re, im], axis=-1) * _s5_diag_mask()[None]


def _s5_block_diag(dmat):
    d = dmat * _s5_diag_mask()[None]
    parts = []
    for ri in range(2):
        acc = 0.0
        for g in range(GROUPS_PER_BLOCK):
            c0 = ri * STATE_COLS + g * SSM_STATE
            acc = acc + d[:, :, c0:c0 + SSM_STATE]
        parts.append(acc.reshape(SSM_GROUPS, SSM_GROUP, SSM_STATE))
    return jnp.stack(parts)


def _s5_a_rows(ar, ai):
    a = jnp.concatenate([ar.reshape(SSM_BLOCKS, STATE_COLS), ai.reshape(SSM_BLOCKS, STATE_COLS)], axis=1)
    return jnp.broadcast_to(a[:, None, :], (SSM_BLOCKS, SUBLANES, 2 * STATE_COLS))


def _to_step_major(src_ref, dst_ref, seg):
    for s in range(SUBLANES):
        dst_ref[pl.ds(s, seg, stride=SUBLANES), :] = src_ref[pl.ds(seg * s, seg), :]


def _segment_rows(ref, s, seg):
    return ref[pl.ds(s, seg, stride=SUBLANES), :]


def _cmul(ar, ai, xr, xi):
    return ar * xr - ai * xi, ar * xi + ai * xr


def _s5_tables(a_ref, pw_s, pwr_s, S, seg):
    ar, ai = a_ref[:, :S], a_ref[:, S:]

    def step(i, c):
        pr, pi = c
        pw_s[i, :, :S] = pr
        pw_s[i, :, S:] = pi
        nr, ni = _cmul(ar, ai, pr, pi)
        pwr_s[seg - 1 - i, :, :S] = nr
        pwr_s[seg - 1 - i, :, S:] = ni
        return nr, ni

    pr, pi = lax.fori_loop(0, seg, step, (jnp.ones_like(ar), jnp.zeros_like(ai)))
    pw_s[seg, :, :S] = pr
    pw_s[seg, :, S:] = pi


def _s5_fwd(proj, bmat, cmat, a_rows, d_skip, *, tc=512):
    L = proj.shape[0]
    tc = min(tc, L)
    nt = L // tc
    seg = tc // SUBLANES
    S = STATE_COLS

    def body(u_ref, b_ref, c_ref, a_ref, d_ref, y_ref, yg_ref, xp_ref,
             bu_s, xp_s, pw_s, pwr_s, carry_s, e_s, up_s, yc_s):
        @pl.when(pl.program_id(1) == 0)
        def _():
            carry_s[...] = jnp.zeros_like(carry_s)
            _s5_tables(a_ref, pw_s, pwr_s, S, seg)

        ar, ai = a_ref[:, :S], a_ref[:, S:]
        _to_step_major(u_ref, up_s, seg)
        bu = jnp.dot(up_s[...].astype(BF16), b_ref[...], preferred_element_type=F32)
        bu_s[...] = bu.reshape(seg, SUBLANES, 2 * S)

        def step(i, carry):
            cr, ci = carry
            xp_s[i, :, :S] = cr
            xp_s[i, :, S:] = ci
            return ar * cr - ai * ci + bu_s[i, :, :S], ar * ci + ai * cr + bu_s[i, :, S:]

        zero = jnp.zeros((SUBLANES, S), F32)
        fr, fi = lax.fori_loop(0, seg, step, (zero, zero))
        pr, pi = pw_s[seg, 0:1, :S], pw_s[seg, 0:1, S:]
        er, ei = carry_s[0:1, :S], carry_s[0:1, S:]
        for s in range(SUBLANES):
            e_s[s:s + 1, :S] = er
            e_s[s:s + 1, S:] = ei
            tr, ti = _cmul(pr, pi, er, ei)
            er, ei = fr[s:s + 1] + tr, fi[s:s + 1] + ti
        carry_s[0:1, :S] = er
        carry_s[0:1, S:] = ei
        pw = pw_s[0:seg]
        tr, ti = _cmul(pw[:, :, :S], pw[:, :, S:], e_s[:, :S][None], e_s[:, S:][None])
        xl = xp_s[...]
        xp = jnp.concatenate([xl[:, :, :S] + tr, xl[:, :, S:] + ti], axis=-1).reshape(tc, 2 * S)
        xp_ref[...] = xp
        a1r, a1i = ar[0:1], ai[0:1]
        x_re = a1r * xp[:, :S] - a1i * xp[:, S:] + bu[:, :S]
        x_im = a1r * xp[:, S:] + a1i * xp[:, :S] + bu[:, S:]
        xs = jnp.concatenate([x_re, x_im], axis=1).astype(BF16)
        yc_s[...] = jnp.dot(xs, c_ref[...], preferred_element_type=F32)
        for s in range(SUBLANES):
            rows = pl.ds(seg * s, seg)
            y = _segment_rows(yc_s, s, seg) + d_ref[...] * u_ref[rows, :]
            y_ref[rows, :] = y
            yg_ref[rows, :] = _gelu(y).astype(BF16)

    return pl.pallas_call(
        body, name="s5_fwd",
        out_shape=(jax.ShapeDtypeStruct((L, MAIN_WIDTH), F32),
                   jax.ShapeDtypeStruct((L, MAIN_WIDTH), BF16),
                   jax.ShapeDtypeStruct((L, SSM_BLOCKS * 2 * S), F32)),
        grid=(SSM_BLOCKS, nt),
        in_specs=[pl.BlockSpec((tc, LANES), lambda b, t: (t, b)),
                  pl.BlockSpec((None, LANES, 2 * S), lambda b, t: (b, 0, 0)),
                  pl.BlockSpec((None, 2 * S, LANES), lambda b, t: (b, 0, 0)),
                  pl.BlockSpec((None, SUBLANES, 2 * S), lambda b, t: (b, 0, 0)),
                  pl.BlockSpec((1, LANES), lambda b, t: (0, b))],
        out_specs=(pl.BlockSpec((tc, LANES), lambda b, t: (t, b)),
                   pl.BlockSpec((tc, LANES), lambda b, t: (t, b)),
                   pl.BlockSpec((tc, 2 * S), lambda b, t: (t, b))),
        scratch_shapes=[pltpu.VMEM((seg, SUBLANES, 2 * S), F32),
                        pltpu.VMEM((seg, SUBLANES, 2 * S), F32),
                        pltpu.VMEM((seg + 1, SUBLANES, 2 * S), F32),
                        pltpu.VMEM((seg, SUBLANES, 2 * S), F32),
                        pltpu.VMEM((SUBLANES, 2 * S), F32),
                        pltpu.VMEM((SUBLANES, 2 * S), F32),
                        pltpu.VMEM((tc, LANES), F32),
                        pltpu.VMEM((tc, LANES), F32)],
        compiler_params=_params("parallel", "arbitrary"),
    )(proj, bmat, cmat, a_rows, d_skip.reshape(1, MAIN_WIDTH))


def _s5_bwd(proj, dyg_a, dyg_b, y, xp, bmat, cmat, a_rows, d_skip, dproj, *, tc=512):
    L = proj.shape[0]
    tc = min(tc, L)
    nt = L // tc
    seg = tc // SUBLANES
    S = STATE_COLS
    nn = (((1,), (1,)), ((), ()))
    tn = (((0,), (0,)), ((), ()))

    def body(u_ref, dyga_ref, dygb_ref, y_ref, xp_ref, b_ref, c_ref, a_ref, d_ref, dp_hbm,
             du_ref, db_ref, dc_ref, da_ref, dd_ref, dl_s, pw_s, pwr_s, carry_s, e_s, up_s, dy_s, dyp_s, dup_s):
        @pl.when(pl.program_id(1) == 0)
        def _():
            carry_s[...] = jnp.zeros_like(carry_s)
            db_ref[...] = jnp.zeros_like(db_ref)
            dc_ref[...] = jnp.zeros_like(dc_ref)
            da_ref[...] = jnp.zeros_like(da_ref)
            dd_ref[...] = jnp.zeros_like(dd_ref)
            _s5_tables(a_ref, pw_s, pwr_s, S, seg)

        ar, ai = a_ref[:, :S], a_ref[:, S:]
        a1r, a1i = ar[0:1], ai[0:1]
        u = u_ref[...]
        dy = (dyga_ref[...] + dygb_ref[...]) * _gelu_grad(y_ref[...])
        dy_s[...] = dy
        xp = xp_ref[...]
        _to_step_major(u_ref, up_s, seg)
        _to_step_major(dy_s, dyp_s, seg)
        ubp = up_s[...].astype(BF16)
        dyp = dyp_s[...].astype(BF16)
        bu = jnp.dot(ubp, b_ref[...], preferred_element_type=F32)
        x_re = a1r * xp[:, :S] - a1i * xp[:, S:] + bu[:, :S]
        x_im = a1r * xp[:, S:] + a1i * xp[:, :S] + bu[:, S:]
        xs = jnp.concatenate([x_re, x_im], axis=1).astype(BF16)
        dc_ref[...] += lax.dot_general(dyp, xs, tn, preferred_element_type=F32)
        dx = lax.dot_general(dyp, c_ref[...], nn, preferred_element_type=F32)
        dl_s[...] = dx.reshape(seg, SUBLANES, 2 * S)

        def step(k, carry):
            cr, ci = carry
            i = seg - 1 - k
            lr = dl_s[i, :, :S] + (ar * cr + ai * ci)
            li = dl_s[i, :, S:] + (ar * ci - ai * cr)
            dl_s[i, :, :S] = lr
            dl_s[i, :, S:] = li
            return lr, li

        zero = jnp.zeros((SUBLANES, S), F32)
        fr, fi = lax.fori_loop(0, seg, step, (zero, zero))
        pr, pi = pw_s[seg, 0:1, :S], pw_s[seg, 0:1, S:]
        er, ei = carry_s[0:1, :S], carry_s[0:1, S:]
        for s in range(SUBLANES - 1, -1, -1):
            e_s[s:s + 1, :S] = er
            e_s[s:s + 1, S:] = ei
            er, ei = fr[s:s + 1] + (pr * er + pi * ei), fi[s:s + 1] + (pr * ei - pi * er)
        carry_s[0:1, :S] = er
        carry_s[0:1, S:] = ei
        er, ei = e_s[:, :S][None], e_s[:, S:][None]
        pw = pwr_s[...]
        pwr, pwi = pw[:, :, :S], pw[:, :, S:]
        ll = dl_s[...]
        lam = jnp.concatenate([ll[:, :, :S] + (pwr * er + pwi * ei), ll[:, :, S:] + (pwr * ei - pwi * er)],
                              axis=-1).reshape(tc, 2 * S)
        l_re, l_im = lam[:, :S], lam[:, S:]
        da_ref[0:1, :S] += jnp.sum(l_re * xp[:, :S] + l_im * xp[:, S:], axis=0, keepdims=True)
        da_ref[0:1, S:] += jnp.sum(l_im * xp[:, :S] - l_re * xp[:, S:], axis=0, keepdims=True)
        lamb = lam.astype(BF16)
        dup_s[...] = lax.dot_general(lamb, b_ref[...], nn, preferred_element_type=F32)
        for s in range(SUBLANES):
            rows = pl.ds(seg * s, seg)
            du = _segment_rows(dup_s, s, seg) + d_ref[...] * dy_s[rows, :]
            du_ref[rows, :] = du.astype(du_ref.dtype)
        db_ref[...] += lax.dot_general(ubp, lamb, tn, preferred_element_type=F32)
        dd_ref[0:1, :] += jnp.sum(dy * u, axis=0, keepdims=True)

    rev = lambda b, t: (nt - 1 - t, b)
    return pl.pallas_call(
        body, name="s5_bwd",
        out_shape=(jax.ShapeDtypeStruct(dproj.shape, dproj.dtype),
                   jax.ShapeDtypeStruct((SSM_BLOCKS, LANES, 2 * S), F32),
                   jax.ShapeDtypeStruct((SSM_BLOCKS, LANES, 2 * S), F32),
                   jax.ShapeDtypeStruct((SSM_BLOCKS, SUBLANES, 2 * S), F32),
                   jax.ShapeDtypeStruct((SUBLANES, MAIN_WIDTH), F32)),
        input_output_aliases={9: 0},
        grid=(SSM_BLOCKS, nt),
        in_specs=[pl.BlockSpec((tc, LANES), rev),
                  pl.BlockSpec((tc, LANES), rev),
                  pl.BlockSpec((tc, LANES), rev),
                  pl.BlockSpec((tc, LANES), rev),
                  pl.BlockSpec((tc, 2 * S), rev),
                  pl.BlockSpec((None, LANES, 2 * S), lambda b, t: (b, 0, 0)),
                  pl.BlockSpec((None, 2 * S, LANES), lambda b, t: (b, 0, 0)),
                  pl.BlockSpec((None, SUBLANES, 2 * S), lambda b, t: (b, 0, 0)),
                  pl.BlockSpec((1, LANES), lambda b, t: (0, b)),
                  _ANY],
        out_specs=(pl.BlockSpec((tc, LANES), rev),
                   pl.BlockSpec((None, LANES, 2 * S), lambda b, t: (b, 0, 0)),
                   pl.BlockSpec((None, LANES, 2 * S), lambda b, t: (b, 0, 0)),
                   pl.BlockSpec((None, SUBLANES, 2 * S), lambda b, t: (b, 0, 0)),
                   pl.BlockSpec((SUBLANES, LANES), lambda b, t: (0, b))),
        scratch_shapes=[pltpu.VMEM((seg, SUBLANES, 2 * S), F32),
                        pltpu.VMEM((seg + 1, SUBLANES, 2 * S), F32),
                        pltpu.VMEM((seg, SUBLANES, 2 * S), F32),
                        pltpu.VMEM((SUBLANES, 2 * S), F32),
                        pltpu.VMEM((SUBLANES, 2 * S), F32),
                        pltpu.VMEM((tc, LANES), F32),
                        pltpu.VMEM((tc, LANES), F32),
                        pltpu.VMEM((tc, LANES), F32),
                        pltpu.VMEM((tc, LANES), F32)],
        compiler_params=_params("parallel", "arbitrary"),
    )(proj, dyg_a, dyg_b, y, xp, bmat, cmat, a_rows, d_skip.reshape(1, MAIN_WIDTH), dproj)


_Z_COLS = slice(MAIN_WIDTH, 2 * MAIN_WIDTH)
_ZM_COLS = slice(2 * MAIN_WIDTH + MEM_WIDTH, IN_WIDTH)


def _proj_rows(tr):
    return pl.BlockSpec((tr, IN_WIDTH), lambda i: (i, 0))


def _row_specs(tr):
    main = pl.BlockSpec((tr, MAIN_WIDTH), lambda i: (i, 0))
    z = pl.BlockSpec((tr, MAIN_WIDTH), lambda i: (i, 1))
    zm = pl.BlockSpec((tr, MEM_WIDTH), lambda i: (i, IN_WIDTH // MEM_WIDTH - 1))
    mem = pl.BlockSpec((tr, MEM_WIDTH), lambda i: (i, 0))
    cat = pl.BlockSpec((tr, D_MODEL), lambda i: (i, 0))
    vec = pl.BlockSpec((1, MAIN_WIDTH), lambda i: (0, 0))
    return main, z, zm, mem, cat, vec


def _gate_a_fwd(y, t, b_glu, proj, o_mem, *, tr=256):
    L = y.shape[0]
    tr = min(tr, L)

    def body(y_ref, t_ref, b_ref, z_ref, zm_ref, om_ref, o_ref):
        yg = _gelu(y_ref[...])
        sz, _ = _silu_and_grad(z_ref[...])
        o_ref[:, :MAIN_WIDTH] = (yg * _sigmoid(t_ref[...] + b_ref[...]) * sz).astype(BF16)
        szm, _ = _silu_and_grad(zm_ref[...])
        o_ref[:, MAIN_WIDTH:] = (om_ref[...] * szm).astype(BF16)

    main, z, zm, mem, cat, vec = _row_specs(tr)
    return pl.pallas_call(
        body, name="gate_a_fwd", out_shape=jax.ShapeDtypeStruct((L, D_MODEL), BF16),
        grid=(L // tr,), in_specs=[main, main, vec, z, zm, mem], out_specs=cat,
        compiler_params=_params("parallel"),
    )(y, t, b_glu.reshape(1, MAIN_WIDTH), proj, proj, o_mem)


def _gate_a_bwd(dcat, y, t, b_glu, proj, o_mem, *, tr=256):
    L = y.shape[0]
    tr = min(tr, L)

    def body(dc_ref, y_ref, t_ref, b_ref, z_ref, zm_ref, om_ref,
             dp_ref, dt_ref, dyg_ref, dom_ref, db_ref):
        dmain = dc_ref[:, :MAIN_WIDTH]
        dmemo = dc_ref[:, MAIN_WIDTH:]
        yg = _gelu(y_ref[...])
        sg = _sigmoid(t_ref[...] + b_ref[...])
        sz, gz = _silu_and_grad(z_ref[...])
        dp_ref[:, _Z_COLS] = (dmain * (yg * sg) * gz).astype(BF16)
        dy2 = dmain * sz
        dyg_ref[...] = dy2 * sg
        dt = dy2 * yg * (sg * (1.0 - sg))
        dt_ref[...] = dt.astype(BF16)

        @pl.when(pl.program_id(0) == 0)
        def _():
            db_ref[...] = jnp.zeros_like(db_ref)

        db_ref[...] += jnp.sum(dt, axis=0, keepdims=True)
        szm, gzm = _silu_and_grad(zm_ref[...])
        dom_ref[...] = dmemo * szm
        dp_ref[:, _ZM_COLS] = (dmemo * om_ref[...] * gzm).astype(BF16)

    main, z, zm, mem, cat, vec = _row_specs(tr)
    outs = pl.pallas_call(
        body, name="gate_a_bwd",
        out_shape=(jax.ShapeDtypeStruct((L, IN_WIDTH), BF16),
                   jax.ShapeDtypeStruct((L, MAIN_WIDTH), BF16), jax.ShapeDtypeStruct((L, MAIN_WIDTH), F32),
                   jax.ShapeDtypeStruct((L, MEM_WIDTH), F32), jax.ShapeDtypeStruct((1, MAIN_WIDTH), F32)),
        grid=(L // tr,), in_specs=[cat, main, main, vec, z, zm, mem],
        out_specs=(_proj_rows(tr), main, main, mem, vec),
        compiler_params=_params("arbitrary"),
    )(dcat, y, t, b_glu.reshape(1, MAIN_WIDTH), proj, proj, o_mem)
    return outs


def _gate_b_fwd(att, proj, o_mem, *, tr=256):
    L = att.shape[0]
    tr = min(tr, L)

    def body(a_ref, z_ref, zm_ref, om_ref, o_ref):
        sz, _ = _silu_and_grad(z_ref[...])
        o_ref[:, :MAIN_WIDTH] = (a_ref[...] * sz).astype(BF16)
        szm, _ = _silu_and_grad(zm_ref[...])
        o_ref[:, MAIN_WIDTH:] = (om_ref[...] * szm).astype(BF16)

    main, z, zm, mem, cat, _ = _row_specs(tr)
    return pl.pallas_call(
        body, name="gate_b_fwd", out_shape=jax.ShapeDtypeStruct((L, D_MODEL), BF16),
        grid=(L // tr,), in_specs=[main, z, zm, mem], out_specs=cat,
        compiler_params=_params("parallel"),
    )(att, proj, proj, o_mem)


def _gate_b_bwd(dcat, att, proj, o_mem, *, tr=256):
    L = att.shape[0]
    tr = min(tr, L)

    def body(dc_ref, a_ref, z_ref, zm_ref, om_ref, da_ref, dp_ref, dom_ref, dl_ref):
        dmain = dc_ref[:, :MAIN_WIDTH]
        dmemo = dc_ref[:, MAIN_WIDTH:]
        att = a_ref[...]
        sz, gz = _silu_and_grad(z_ref[...])
        datt = dmain * sz
        da_ref[...] = datt
        dp_ref[:, _Z_COLS] = (dmain * att * gz).astype(BF16)
        szm, gzm = _silu_and_grad(zm_ref[...])
        dom_ref[...] = dmemo * szm
        dp_ref[:, _ZM_COLS] = (dmemo * om_ref[...] * gzm).astype(BF16)
        prod = datt * att
        for h in range(FOX_HEADS):
            dl_ref[h] = jnp.sum(prod[:, h * HEAD_DIM:(h + 1) * HEAD_DIM], axis=1, keepdims=True)

    main, z, zm, mem, cat, _ = _row_specs(tr)
    delta = pl.BlockSpec((FOX_HEADS, tr, 1), lambda i: (0, i, 0))
    return pl.pallas_call(
        body, name="gate_b_bwd",
        out_shape=(jax.ShapeDtypeStruct((L, MAIN_WIDTH), F32), jax.ShapeDtypeStruct((L, IN_WIDTH), BF16),
                   jax.ShapeDtypeStruct((L, MEM_WIDTH), F32), jax.ShapeDtypeStruct((FOX_HEADS, L, 1), F32)),
        grid=(L // tr,), in_specs=[cat, main, z, zm, mem], out_specs=(main, _proj_rows(tr), mem, delta),
        compiler_params=_params("parallel"),
    )(dcat, att, proj, proj, o_mem)


_MEM_Q_COL = (2 * MAIN_WIDTH) // HEAD_DIM
_NT = (((1,), (1,)), ((), ()))
_TN = (((0,), (0,)), ((), ()))


def _mem_probs(q_ref, k_ref):
    qs = (q_ref[...] * (HEAD_DIM ** -0.5)).astype(BF16)
    s = lax.dot_general(qs, k_ref[...].astype(BF16), _NT, preferred_element_type=F32)
    e = jnp.exp(s - jnp.max(s, axis=-1, keepdims=True))
    return qs, e / jnp.sum(e, axis=-1, keepdims=True)


def _mem_attn_fwd(proj, kvm, *, tq=2048):
    L = proj.shape[0]
    tq = min(tq, L)

    def body(q_ref, k_ref, v_ref, o_ref):
        _, p = _mem_probs(q_ref, k_ref)
        o_ref[...] = jnp.dot(p.astype(BF16), v_ref[...].astype(BF16), preferred_element_type=F32)

    return pl.pallas_call(
        body, name="mem_attn_fwd", out_shape=jax.ShapeDtypeStruct((L, MEM_WIDTH), F32),
        grid=(MEM_HEADS, L // tq),
        in_specs=[pl.BlockSpec((tq, HEAD_DIM), lambda h, i: (i, _MEM_Q_COL + h)),
                  pl.BlockSpec((N_MEM, HEAD_DIM), lambda h, i: (0, h)),
                  pl.BlockSpec((N_MEM, HEAD_DIM), lambda h, i: (0, MEM_HEADS + h))],
        out_specs=pl.BlockSpec((tq, HEAD_DIM), lambda h, i: (i, h)),
        compiler_params=_params("parallel", "parallel"),
    )(proj, kvm, kvm)


def _mem_attn_bwd(proj, kvm, do, dproj, *, tq=2048):
    L = proj.shape[0]
    tq = min(tq, L)

    def body(q_ref, k_ref, v_ref, do_ref, dp_hbm, dq_ref, dk_ref, dv_ref):
        @pl.when(pl.program_id(1) == 0)
        def _():
            dk_ref[...] = jnp.zeros_like(dk_ref)
            dv_ref[...] = jnp.zeros_like(dv_ref)

        qs, p = _mem_probs(q_ref, k_ref)
        dob = do_ref[...].astype(BF16)
        dp = lax.dot_general(dob, v_ref[...].astype(BF16), _NT, preferred_element_type=F32)
        ds = p * (dp - jnp.sum(p * dp, axis=-1, keepdims=True))
        dsb = ds.astype(BF16)
        dq = jnp.dot(dsb, k_ref[...].astype(BF16), preferred_element_type=F32) * (HEAD_DIM ** -0.5)
        dq_ref[...] = dq.astype(BF16)
        dk_ref[...] += lax.dot_general(dsb, qs, _TN, preferred_element_type=F32)
        dv_ref[...] += lax.dot_general(p.astype(BF16), dob, _TN, preferred_element_type=F32)

    dproj, dk, dv = pl.pallas_call(
        body, name="mem_attn_bwd",
        out_shape=(jax.ShapeDtypeStruct(dproj.shape, dproj.dtype),
                   jax.ShapeDtypeStruct((N_MEM, MEM_WIDTH), F32),
                   jax.ShapeDtypeStruct((N_MEM, MEM_WIDTH), F32)),
        grid=(MEM_HEADS, L // tq),
        in_specs=[pl.BlockSpec((tq, HEAD_DIM), lambda h, i: (i, _MEM_Q_COL + h)),
                  pl.BlockSpec((N_MEM, HEAD_DIM), lambda h, i: (0, h)),
                  pl.BlockSpec((N_MEM, HEAD_DIM), lambda h, i: (0, MEM_HEADS + h)),
                  pl.BlockSpec((tq, HEAD_DIM), lambda h, i: (i, h)),
                  _ANY],
        out_specs=(pl.BlockSpec((tq, HEAD_DIM), lambda h, i: (i, _MEM_Q_COL + h)),
                   pl.BlockSpec((N_MEM, HEAD_DIM), lambda h, i: (0, h)),
                   pl.BlockSpec((N_MEM, HEAD_DIM), lambda h, i: (0, h))),
        input_output_aliases={4: 0},
        compiler_params=_params("parallel", "arbitrary"),
    )(proj, kvm, kvm, do, dproj)
    return dproj, jnp.concatenate([dk, dv], axis=1)


def _tile_cumsum(x, row, reverse):
    for sh in (1, 2, 4):
        if reverse:
            x = x + jnp.where(row < SUBLANES - sh, pltpu.roll(x, SUBLANES - sh, 0), 0.0)
        else:
            x = x + jnp.where(row >= sh, pltpu.roll(x, sh, 0), 0.0)
    return x


def _fgate_fwd(pre, b_pad):
    L = pre.shape[0]
    n8 = L // SUBLANES

    def body(p_ref, b_ref, o_ref):
        row = lax.broadcasted_iota(jnp.int32, (SUBLANES, LANES), 0)
        b = b_ref[...]

        def step(i, carry):
            x = p_ref[i] + b
            logf = jnp.minimum(x, 0.0) - jnp.log(1.0 + jnp.exp(-jnp.abs(x)))
            t = _tile_cumsum(logf, row, False) + carry
            o_ref[i] = t
            return t[SUBLANES - 1:SUBLANES, :]

        lax.fori_loop(0, n8, step, jnp.zeros((1, LANES), F32))

    out = pl.pallas_call(
        body, name="fgate_fwd", out_shape=jax.ShapeDtypeStruct((n8, SUBLANES, LANES), F32),
        compiler_params=_params(),
    )(pre.reshape(n8, SUBLANES, LANES), b_pad.reshape(1, LANES))
    return out.reshape(L, LANES)


def _fgate_bwd(dfcum, pre, b_pad):
    L = pre.shape[0]
    n8 = L // SUBLANES

    def body(d_ref, p_ref, b_ref, o_ref, s_ref):
        row = lax.broadcasted_iota(jnp.int32, (SUBLANES, LANES), 0)
        b = b_ref[...]

        def step(k, carry):
            c, acc = carry
            i = n8 - 1 - k
            t = _tile_cumsum(d_ref[i], row, True) + c
            dpre = t * _sigmoid(-(p_ref[i] + b))
            o_ref[i] = dpre
            return t[0:1, :], acc + dpre

        _, acc = lax.fori_loop(0, n8, step, (jnp.zeros((1, LANES), F32), jnp.zeros((SUBLANES, LANES), F32)))
        s_ref[...] = jnp.sum(acc, axis=0, keepdims=True)

    dpre, db = pl.pallas_call(
        body, name="fgate_bwd",
        out_shape=(jax.ShapeDtypeStruct((n8, SUBLANES, LANES), F32), jax.ShapeDtypeStruct((1, LANES), F32)),
        compiler_params=_params(),
    )(dfcum.reshape(n8, SUBLANES, LANES), pre.reshape(n8, SUBLANES, LANES), b_pad.reshape(1, LANES))
    return dpre.reshape(L, LANES), db


FOX_BLOCK = 512


def _fox_scores(qs, k, fk, diagonal):
    s = lax.dot_general(qs, k, _NT, preferred_element_type=F32) - fk
    if diagonal:
        row = lax.broadcasted_iota(jnp.int32, s.shape, 0)
        col = lax.broadcasted_iota(jnp.int32, s.shape, 1)
        s = jnp.where(row >= col, s, NEG_BIG)
    return s


def _fox_specs(tq, L):
    nq = L // tq
    return dict(
        rows=lambda off: pl.BlockSpec((tq, HEAD_DIM), lambda h, i: (i, off + h)),
        seq=lambda off: pl.BlockSpec((L, HEAD_DIM), lambda h, i: (0, off + h)),
        col=pl.BlockSpec((None, None, tq, 1), lambda h, i: (h, i, 0, 0)),
        col_all=pl.BlockSpec((None, nq, tq, 1), lambda h, i: (h, 0, 0, 0)),
        row=pl.BlockSpec((None, None, 1, tq), lambda h, i: (h, i, 0, 0)),
        row_all=pl.BlockSpec((None, nq, 1, tq), lambda h, i: (h, 0, 0, 0)))


FOX_FWD_HEADS = 2


def _fox_fwd(proj, kv, fk):
    L = proj.shape[0]
    tq = min(FOX_BLOCK, L)
    nq = L // tq
    nh = FOX_FWD_HEADS
    W = nh * HEAD_DIM

    def body(q_ref, k_ref, v_ref, fk_ref, o_ref, lse_ref, m_s, l_s, acc_s):
        qi = pl.program_id(1)
        cols = [slice(a * HEAD_DIM, (a + 1) * HEAD_DIM) for a in range(nh)]
        qs = [(q_ref[:, cs] * (HEAD_DIM ** -0.5)).astype(BF16) for cs in cols]
        m_s[...] = jnp.full_like(m_s, NEG_BIG)
        l_s[...] = jnp.zeros_like(l_s)
        acc_s[...] = jnp.zeros_like(acc_s)

        def block(j, diagonal):
            r0 = pl.multiple_of(j * tq, tq)
            for a, cs in enumerate(cols):
                s = _fox_scores(qs[a], k_ref[pl.ds(r0, tq), cs], fk_ref[a, j], diagonal)
                m_new = jnp.maximum(m_s[a], jnp.max(s, axis=-1, keepdims=True))
                alpha = jnp.exp(m_s[a] - m_new)
                p = jnp.exp(s - m_new)
                l_s[a] = alpha * l_s[a] + jnp.sum(p, axis=-1, keepdims=True)
                acc_s[a] = alpha * acc_s[a] + jnp.dot(p.astype(BF16), v_ref[pl.ds(r0, tq), cs],
                                                      preferred_element_type=F32)
                m_s[a] = m_new

        def below(j, carry):
            block(j, False)
            return carry

        lax.fori_loop(0, qi, below, 0)
        block(qi, True)
        for a, cs in enumerate(cols):
            o_ref[:, cs] = acc_s[a] / l_s[a]
            lse_ref[a] = m_s[a] + jnp.log(l_s[a])

    return pl.pallas_call(
        body, name="fox_fwd",
        out_shape=(jax.ShapeDtypeStruct((L, MAIN_WIDTH), F32),
                   jax.ShapeDtypeStruct((FOX_HEADS, nq, tq, 1), F32)),
        grid=(FOX_HEADS // nh, nq),
        in_specs=[pl.BlockSpec((tq, W), lambda h, i: (i, h)),
                  pl.BlockSpec((L, W), lambda h, i: (0, h)),
                  pl.BlockSpec((L, W), lambda h, i: (0, FOX_HEADS // nh + h)),
                  pl.BlockSpec((nh, nq, 1, tq), lambda h, i: (h, 0, 0, 0))],
        out_specs=(pl.BlockSpec((tq, W), lambda h, i: (i, h)),
                   pl.BlockSpec((nh, None, tq, 1), lambda h, i: (h, i, 0, 0))),
        scratch_shapes=[pltpu.VMEM((nh, tq, 1), F32), pltpu.VMEM((nh, tq, 1), F32),
                        pltpu.VMEM((nh, tq, HEAD_DIM), F32)],
        compiler_params=_params("parallel", "parallel"),
    )(proj, kv, kv, fk)


def _fox_bwd_dq(proj, kv, fk, lse, delta, datt, dproj):
    L = proj.shape[0]
    tq = min(FOX_BLOCK, L)
    nq = L // tq
    sp = _fox_specs(tq, L)

    def body(q_ref, k_ref, v_ref, fk_ref, lse_ref, dl_ref, do_ref, dp_hbm, dq_ref, df_ref, acc_s, df_s):
        qi = pl.program_id(1)
        qs = (q_ref[...] * (HEAD_DIM ** -0.5)).astype(BF16)
        dob = do_ref[...].astype(BF16)
        lse, dl = lse_ref[...], dl_ref[...]
        acc_s[...] = jnp.zeros_like(acc_s)
        df_s[...] = jnp.zeros_like(df_s)

        def block(j, diagonal):
            r0 = pl.multiple_of(j * tq, tq)
            k = k_ref[pl.ds(r0, tq), :]
            p = jnp.exp(_fox_scores(qs, k, fk_ref[j], diagonal) - lse)
            dp = lax.dot_general(dob, v_ref[pl.ds(r0, tq), :], _NT, preferred_element_type=F32)
            ds = p * (dp - dl)
            acc_s[...] += jnp.dot(ds.astype(BF16), k, preferred_element_type=F32)
            df_s[...] += jnp.sum(ds, axis=1, keepdims=True)

        def below(j, carry):
            block(j, False)
            return carry

        lax.fori_loop(0, qi, below, 0)
        block(qi, True)
        dq_ref[...] = (acc_s[...] * (HEAD_DIM ** -0.5)).astype(BF16)
        df_ref[...] = df_s[...]

    return pl.pallas_call(
        body, name="fox_bwd_dq",
        out_shape=(jax.ShapeDtypeStruct(dproj.shape, dproj.dtype),
                   jax.ShapeDtypeStruct((FOX_HEADS, nq, tq, 1), F32)),
        grid=(FOX_HEADS, nq),
        in_specs=[sp["rows"](0), sp["seq"](0), sp["seq"](FOX_HEADS), sp["row_all"],
                  sp["col"], sp["col"], sp["rows"](0), _ANY],
        out_specs=(sp["rows"](0), sp["col"]),
        input_output_aliases={7: 0},
        scratch_shapes=[pltpu.VMEM((tq, HEAD_DIM), F32), pltpu.VMEM((tq, 1), F32)],
        compiler_params=_params("parallel", "parallel"),
    )(proj, kv, kv, fk, lse, delta, datt, dproj)


def _fox_bwd_dkv(proj, kv, fk, lse, delta, datt):
    L = proj.shape[0]
    tq = min(FOX_BLOCK, L)
    nq = L // tq
    sp = _fox_specs(tq, L)

    def body(q_ref, k_ref, v_ref, fk_ref, lse_ref, dl_ref, do_ref,
             dk_ref, dv_ref, df_ref, dk_s, dv_s, df_s):
        ki = pl.program_id(1)
        k, v, fk = k_ref[...], v_ref[...], fk_ref[...]
        dk_s[...] = jnp.zeros_like(dk_s)
        dv_s[...] = jnp.zeros_like(dv_s)
        df_s[...] = jnp.zeros_like(df_s)

        def block(i, diagonal):
            r0 = pl.multiple_of(i * tq, tq)
            qs = (q_ref[pl.ds(r0, tq), :] * (HEAD_DIM ** -0.5)).astype(BF16)
            dob = do_ref[pl.ds(r0, tq), :].astype(BF16)
            p = jnp.exp(_fox_scores(qs, k, fk, diagonal) - lse_ref[i])
            dp = lax.dot_general(dob, v, _NT, preferred_element_type=F32)
            ds = p * (dp - dl_ref[i])
            dv_s[...] += lax.dot_general(p.astype(BF16), dob, _TN, preferred_element_type=F32)
            dk_s[...] += lax.dot_general(ds.astype(BF16), qs, _TN, preferred_element_type=F32)
            df_s[...] -= jnp.sum(ds, axis=0, keepdims=True)

        def above(i, carry):
            block(i, False)
            return carry

        block(ki, True)
        lax.fori_loop(ki + 1, nq, above, 0)
        dk_ref[...] = dk_s[...].astype(BF16)
        dv_ref[...] = dv_s[...].astype(BF16)
        df_ref[...] = df_s[...]

    return pl.pallas_call(
        body, name="fox_bwd_dkv",
        out_shape=(jax.ShapeDtypeStruct((L, MAIN_WIDTH), BF16),
                   jax.ShapeDtypeStruct((L, MAIN_WIDTH), BF16),
                   jax.ShapeDtypeStruct((FOX_HEADS, nq, 1, tq), F32)),
        grid=(FOX_HEADS, nq),
        in_specs=[sp["seq"](0), sp["rows"](0), sp["rows"](FOX_HEADS), sp["row"],
                  sp["col_all"], sp["col_all"], sp["seq"](0)],
        out_specs=(sp["rows"](0), sp["rows"](0), sp["row"]),
        scratch_shapes=[pltpu.VMEM((tq, HEAD_DIM), F32), pltpu.VMEM((tq, HEAD_DIM), F32),
                        pltpu.VMEM((1, tq), F32)],
        compiler_params=_params("parallel", "parallel"),
    )(proj, kv, kv, fk, lse, delta, datt)


def _pad_lanes(a):
    return jnp.pad(a, ((0, 0), (0, LANES - a.shape[1])))


def _mem_branch_fwd(memn, w_mk, proj, tag):
    kvm = _mm(memn, w_mk, name="mem_kv_" + tag)
    return kvm, _mem_attn_fwd(proj, kvm)


def _mem_branch_bwd(mem, g, w_mk, proj, memn, kvm, do_mem, dproj, tag):
    dproj, dkvm = _mem_attn_bwd(proj, kvm, do_mem, dproj)
    dkvm = dkvm.astype(BF16)
    dw_mk = _mm(memn, dkvm, ta=True, name="dw_mem_kv_" + tag, out_dtype=BF16)
    dmemn = _mm(dkvm, w_mk, tb=True, name="dmemn_" + tag)
    _, dg = _rmsnorm_bwd(mem, g, dmemn, name="mem_norm_bwd_" + tag, dx_dtype=BF16)
    return dproj, dw_mk, dg


def _local_step(x, mem, target, w, fetch=None, grads_ready=None):
    if grads_ready is None:
        grads_ready = lambda group, grads, token: token
    L = x.shape[0]
    g = {}
    w = dict(w)

    b_re_t = jnp.transpose(w["b_re"], (0, 2, 1))
    b_im_t = jnp.transpose(w["b_im"], (0, 2, 1))
    ar, ai, bbr_t, bbi_t = _s5_prep(w["lam_re"], w["lam_im"], w["log_step"], b_re_t, b_im_t)
    bmat, cmat = _s5_block_mats(bbr_t, bbi_t, w["c_re"], w["c_im"])
    a_rows = _s5_a_rows(ar, ai)

    hn0 = _rmsnorm_fwd(x, w["pre_norm_g"][0], name="pre_norm_0", out_dtype=BF16)
    memn0 = _rmsnorm_fwd(mem, w["mem_norm_g"][0], name="mem_norm_0", out_dtype=BF16)
    memn1 = _rmsnorm_fwd(mem, w["mem_norm_g"][1], name="mem_norm_1", out_dtype=BF16)
    if fetch is not None:
        w.update(fetch("a", [hn0, memn0, memn1, bmat, cmat, a_rows]))
    proj_a = _mm(hn0, w["w_in_a"], name="in_proj_a")
    y, yg, xp = _s5_fwd(proj_a, bmat, cmat, a_rows, w["d_skip"])
    if fetch is not None:
        w.update(fetch("b", yg))
    t = _mm(yg, w["w_glu"], name="glu_proj")
    kvm0, om0 = _mem_branch_fwd(memn0, w["w_mem_kv"][0], proj_a, "0")
    cat0 = _gate_a_fwd(y, t, w["b_glu"], proj_a, om0)
    o0 = _mm(cat0, w["w_out"][0], name="out_proj_0")
    h1 = _rmsnorm_fwd(o0, w["post_norm_g"][0], res=x, name="post_norm_0")

    kv_in = _rmsnorm_fwd(h1, w["kv_norm_g"], name="kv_norm", out_dtype=BF16)
    if fetch is not None:
        w.update(fetch("c", kv_in))
    kv = _mm(kv_in, w["w_kv"], name="kv_proj", out_dtype=BF16)
    pre_f = _mm(kv_in, w["w_fgate"], name="fgate_proj")
    b_f = jnp.pad(w["b_fgate"], (0, LANES - FOX_HEADS))
    fcum = _fgate_fwd(pre_f, b_f)
    fc = jnp.transpose(fcum[:, :FOX_HEADS])
    tq = min(FOX_BLOCK, L)
    fk = fc.reshape(FOX_HEADS, L // tq, 1, tq)

    hn1 = _rmsnorm_fwd(h1, w["pre_norm_g"][1], name="pre_norm_1", out_dtype=BF16)
    proj_b = _mm(hn1, w["w_in_b"], name="in_proj_b")
    att, lse = _fox_fwd(proj_b, kv, fk)
    kvm1, om1 = _mem_branch_fwd(memn1, w["w_mem_kv"][1], proj_b, "1")
    cat1 = _gate_b_fwd(att, proj_b, om1)
    o1 = _mm(cat1, w["w_out"][1], name="out_proj_1")
    dh2, loss_row = _final_norm_loss(o1, w["post_norm_g"][1], h1, target)

    do1, dpost1 = _rmsnorm_bwd(o1, w["post_norm_g"][1], dh2, name="post_norm_bwd_1", dx_dtype=BF16)
    dcat1 = _mm(do1, w["w_out"][1], tb=True, name="dcat_1", out_dtype=BF16)
    g["w_out_1"] = _mm(cat1, do1, ta=True, name="dw_out_1", out_dtype=BF16)
    datt, dproj_b, dom1, delta = _gate_b_bwd(dcat1, att, proj_b, om1)
    dproj_b, g["w_mem_kv_1"], dmemg1 = _mem_branch_bwd(mem, w["mem_norm_g"][1], w["w_mem_kv"][1], proj_b,
                                                      memn1, kvm1, dom1, dproj_b, "1")
    delta = delta.reshape(lse.shape)
    dproj_b, dfq = _fox_bwd_dq(proj_b, kv, fk, lse, delta, datt, dproj_b)
    dk, dv, dfk = _fox_bwd_dkv(proj_b, kv, fk, lse, delta, datt)
    g["w_in_b"] = _mm(hn1, dproj_b, ta=True, name="dw_in_b", out_dtype=BF16, shards=N_CHIPS)
    dhn1 = _mm(dproj_b, w["w_in_b"], tb=True, name="dhn_1")

    dkv = jnp.concatenate([dk, dv], axis=1)
    g["w_kv"] = _mm(kv_in, dkv, ta=True, name="dw_kv", out_dtype=BF16, shards=N_CHIPS)
    dkv_in_a = _mm(dkv, w["w_kv"], tb=True, name="dkv_in_kv")
    dfcum = _pad_lanes(jnp.transpose(dfq.reshape(FOX_HEADS, L) + dfk.reshape(FOX_HEADS, L)))
    dpre_f, db_f = _fgate_bwd(dfcum, pre_f, b_f)
    g["b_fgate"] = db_f[0, :FOX_HEADS]
    g["w_fgate"] = _mm(kv_in, dpre_f, ta=True, name="dw_fgate")[:, :FOX_HEADS]
    dkv_in_b = _mm(dpre_f, w["w_fgate"], tb=True, name="dkv_in_fgate")
    dh1, g["kv_norm_g"], dpre1 = _rmsnorm_bwd_pair(h1, w["kv_norm_g"], (dkv_in_a, dkv_in_b), w["pre_norm_g"][1],
                                                   dhn1, adds=(dh2,), name="kv_pre_norm_bwd")
    dh1 = grads_ready("b", g, dh1)

    do0, dpost0 = _rmsnorm_bwd(o0, w["post_norm_g"][0], dh1, name="post_norm_bwd_0", dx_dtype=BF16)
    dcat0 = _mm(do0, w["w_out"][0], tb=True, name="dcat_0", out_dtype=BF16)
    g["w_out_0"] = _mm(cat0, do0, ta=True, name="dw_out_0", out_dtype=BF16)
    dcat0 = grads_ready("b_send", g, dcat0)
    dproj_a, dt, dyg_a, dom0, db_glu = _gate_a_bwd(dcat0, y, t, w["b_glu"], proj_a, om0)
    g["b_glu"] = db_glu[0]
    g["w_glu"] = _mm(yg, dt, ta=True, name="dw_glu", out_dtype=BF16)
    dyg_b = _mm(dt, w["w_glu"], tb=True, name="dyg")
    dproj_a, g["w_mem_kv_0"], dmemg0 = _mem_branch_bwd(mem, w["mem_norm_g"][0], w["w_mem_kv"][0], proj_a,
                                                      memn0, kvm0, dom0, dproj_a, "0")
    dyg_b = grads_ready("a1", g, dyg_b)
    dproj_a, db_blk, dc_blk, da_rows, dd_skip = _s5_bwd(proj_a, dyg_a, dyg_b, y, xp, bmat, cmat, a_rows,
                                                        w["d_skip"], dproj_a)
    dproj_a = grads_ready("a1_send", g, dproj_a)
    g["d_skip"] = dd_skip[0]
    g["w_in_a"] = _mm(hn0, dproj_a, ta=True, name="dw_in_a", out_dtype=BF16, shards=N_CHIPS)
    dproj_a = grads_ready("a2", g, dproj_a)
    dhn0 = _mm(dproj_a, w["w_in_a"], tb=True, name="dhn_0")
    grad_x, dpre0 = _rmsnorm_bwd(x, w["pre_norm_g"][0], dhn0, adds=(dh1,), name="pre_norm_bwd_0")

    dbb = _s5_block_diag(db_blk)
    dcc = _s5_block_diag(dc_blk)
    g["c_re"], g["c_im"] = dcc[0], -dcc[1]
    d_ar = da_rows[:, 0, :STATE_COLS].reshape(SSM_GROUPS, SSM_STATE)
    d_ai = da_rows[:, 0, STATE_COLS:].reshape(SSM_GROUPS, SSM_STATE)
    dlr, dli, dls, dbr_t, dbi_t = _s5_prep_bwd(w["lam_re"], w["lam_im"], w["log_step"], b_re_t, b_im_t,
                                               d_ar, d_ai, dbb[0], dbb[1])
    g["lam_re"], g["lam_im"], g["log_step"] = dlr, dli, dls[:, 0]
    g["b_re"] = jnp.transpose(dbr_t, (0, 2, 1))
    g["b_im"] = jnp.transpose(dbi_t, (0, 2, 1))
    g["pre_norm_g"] = jnp.stack([dpre0, dpre1])
    g["post_norm_g"] = jnp.stack([dpost0, dpost1])
    g["mem_norm_g"] = jnp.stack([dmemg0, dmemg1])
    return loss_row, grad_x, g


_MESH = pl.DeviceIdType.MESH
_ANY = pl.BlockSpec(memory_space=pl.ANY)


def _place():
    x, y, c = lax.axis_index("x"), lax.axis_index("y"), lax.axis_index("c")
    chips = [(1 - x, y), (x, 1 - y), (1 - x, 1 - y)]
    return x, y, c, chips


_HBM = pl.BlockSpec(memory_space=pltpu.HBM)
_SEM = pl.BlockSpec(memory_space=pltpu.SEMAPHORE)
_SIDE = pltpu.SideEffectType.DATAFLOW_SIDE_EFFECTING


def _in_hbm(a):
    return pltpu.with_memory_space_constraint(a, pltpu.HBM)


def _hbm_like(a):
    return pltpu.HBM(a.shape, a.dtype)


def _ici_copies(srcs, lands, send_sem, recv_sem, src_at, dst_at, wait_at, to_sibling=False):
    x, y, c, chips = _place()
    peers = [(x, y, 1 - c)] if to_sibling else [(cx, cy, c) for cx, cy in chips]
    m = len(peers)
    start, wait = [], []
    for i in range(len(srcs)):
        for k, (px, py, pc) in enumerate(peers):
            sem = dict(send_sem=send_sem.at[m * i + k], recv_sem=recv_sem.at[m * i + k],
                       device_id=(px, py, pc), device_id_type=_MESH)
            src = src_at(srcs[i], 2 * px + py, c)
            start.append(pltpu.make_async_remote_copy(src_ref=src, dst_ref=dst_at(lands[i], 2 * x + y, k, c), **sem))
            wait.append(pltpu.make_async_remote_copy(src_ref=src, dst_ref=wait_at(lands[i], 2 * px + py, k, c), **sem))
    return start, wait


def _route_peers(route):
    return 1 if len(route) == 4 else 3


_BLOCK_ROUTE = (lambda s, j, c: s, lambda l, me, k, c: l.at[me, c], lambda l, j, k, c: l.at[j, c])


def _ici_start(srcs, lands, token, route, *, name):
    n = len(srcs)

    def body(*refs):
        start, _ = _ici_copies(refs[:n], refs[n:2 * n], refs[2 * n + 1], refs[2 * n + 2], *route)
        for cp in start:
            cp.start()

    sems = pltpu.SemaphoreType.DMA((_route_peers(route) * n,))
    outs = pl.pallas_call(
        body, name=name,
        out_shape=(sems, sems, *[_hbm_like(a) for a in srcs], *[_hbm_like(a) for a in lands], _hbm_like(token)),
        in_specs=[_HBM] * (2 * n + 1), out_specs=(_SEM, _SEM, *[_HBM] * (2 * n + 1)),
        input_output_aliases={i: 2 + i for i in range(2 * n + 1)},
        compiler_params=pltpu.CompilerParams(has_side_effects=_SIDE),
    )(*[_in_hbm(a) for a in srcs], *[_in_hbm(a) for a in lands], _in_hbm(token))
    return (outs[0], outs[1], list(outs[2:2 + n]), list(outs[2 + n:2 + 2 * n])), outs[2 + 2 * n]


def _ici_wait(handle, after, route, *, name):
    send_sem, recv_sem, srcs, lands = handle
    n = len(srcs)
    after = list(after) if isinstance(after, (list, tuple)) else [after]

    def body(*refs):
        _, wait = _ici_copies(refs[:n], refs[n:2 * n], refs[2 * n], refs[2 * n + 1], *route)
        for cp in wait:
            cp.wait_send()
            cp.wait_recv()

    outs = pl.pallas_call(
        body, name=name,
        out_shape=(*[_hbm_like(a) for a in srcs], *[_hbm_like(a) for a in lands]),
        in_specs=[_HBM] * (2 * n) + [_SEM, _SEM] + [_ANY] * len(after), out_specs=tuple([_HBM] * (2 * n)),
        input_output_aliases={i: i for i in range(2 * n)},
        compiler_params=pltpu.CompilerParams(has_side_effects=_SIDE),
    )(*srcs, *lands, send_sem, recv_sem, *after)
    return list(outs[:n]), list(outs[n:])


_GATHER_ROUTE = (lambda s, j, c: s.at[c], lambda l, me, k, c: l.at[me, c], lambda l, j, k, c: l.at[j, c])
_SCATTER_ROUTE = (lambda s, j, c: s.at[j], lambda l, me, k, c: l.at[k], lambda l, j, k, c: l.at[k])
_SHARE_ROUTE = (lambda s, j, c: s, lambda l, me, k, c: l.at[c], lambda l, j, k, c: l.at[1 - c], True)
_SWAP_ROUTE = (lambda s, j, c: s.at[:, 1 - c], lambda l, me, k, c: l, lambda l, j, k, c: l, True)


def _gather_forward(lands, tag, own=False):
    n = len(lands)
    m = 4 if own else 3

    def body(*refs):
        ins, outs = refs[:n], refs[n:2 * n]
        send_sem, recv_sem = refs[2 * n:]
        x, y, c, chips = _place()
        slots = [2 * cx + cy for cx, cy in chips] + [2 * x + y]

        def copy(i, k, half):
            return pltpu.make_async_remote_copy(
                src_ref=ins[i].at[slots[k], half], dst_ref=outs[i].at[slots[k], half],
                send_sem=send_sem.at[m * i + k], recv_sem=recv_sem.at[m * i + k],
                device_id=(x, y, 1 - c), device_id_type=_MESH)

        copies = [copy(i, k, c) for i in range(n) for k in range(m)]
        for cp in copies:
            cp.start()
        for i in range(n):
            for k in range(m):
                copy(i, k, 1 - c).wait_recv()
        for cp in copies:
            cp.wait_send()

    return pl.pallas_call(
        body, name="gather_forward_to_sibling_" + tag,
        out_shape=[jax.ShapeDtypeStruct(a.shape, a.dtype) for a in lands],
        in_specs=[_ANY] * n, out_specs=[_ANY] * n,
        input_output_aliases={i: i for i in range(n)},
        scratch_shapes=[pltpu.SemaphoreType.DMA((m * n,)), pltpu.SemaphoreType.DMA((m * n,))],
    )(*lands)


def _swap_halves(grads, tag):
    n = len(grads)

    def body(*refs):
        ins, outs = refs[:n], refs[n:2 * n]
        send_sem, recv_sem = refs[2 * n:]
        x, y, c, _ = _place()
        copies = [pltpu.make_async_remote_copy(
            src_ref=ins[i].at[:, 1 - c], dst_ref=outs[i],
            send_sem=send_sem.at[i], recv_sem=recv_sem.at[i],
            device_id=(x, y, 1 - c), device_id_type=_MESH) for i in range(n)]
        for cp in copies:
            cp.start()
        for cp in copies:
            cp.wait()

    return pl.pallas_call(
        body, name="grad_swap_halves_" + tag,
        out_shape=[jax.ShapeDtypeStruct((N_CHIPS,) + g.shape[2:], g.dtype) for g in grads],
        in_specs=[_ANY] * n, out_specs=[_ANY] * n,
        scratch_shapes=[pltpu.SemaphoreType.DMA((n,)), pltpu.SemaphoreType.DMA((n,))],
    )(*grads)


def _sum_rows(h, C):
    return max(d for d in range(SUBLANES, h + 1, SUBLANES) if h % d == 0 and d * C <= 1 << 20)


SUM_STEPS = 4


def _pair_sums(gs, rs, c_idx, *, name):
    n = len(gs)
    rows = [g.shape[2] // SUM_STEPS for g in gs]

    def body(c_ref, *refs):
        for g_ref, r_ref, o_ref in zip(refs[:n], refs[n:2 * n], refs[2 * n:]):
            o_ref[...] = (g_ref[...].astype(F32) + r_ref[...].astype(F32)).astype(o_ref.dtype)

    return pl.pallas_call(
        body, name=name,
        out_shape=[jax.ShapeDtypeStruct((N_CHIPS,) + g.shape[2:], g.dtype) for g in gs],
        grid_spec=pltpu.PrefetchScalarGridSpec(
            num_scalar_prefetch=1, grid=(N_CHIPS, SUM_STEPS),
            in_specs=[pl.BlockSpec((None, None, tr, g.shape[3]), lambda j, i, s: (j, s[0], i, 0))
                      for g, tr in zip(gs, rows)]
            + [pl.BlockSpec((None, tr, g.shape[3]), lambda j, i, s: (j, i, 0)) for g, tr in zip(gs, rows)],
            out_specs=[pl.BlockSpec((None, tr, g.shape[3]), lambda j, i, s: (j, i, 0)) for g, tr in zip(gs, rows)]),
        compiler_params=_params("parallel", "parallel"),
    )(c_idx, *gs, *rs)


def _owner_sums(ss, rs, jc_idx, *, name):
    n = len(ss)
    rows = [s.shape[1] // SUM_STEPS for s in ss]

    def body(jc_ref, *refs):
        for s_ref, r_ref, m_ref, o_ref in zip(refs[:n], refs[n:2 * n], refs[2 * n:3 * n], refs[3 * n:]):
            acc = s_ref[...].astype(F32)
            for k in range(3):
                acc = acc + r_ref[k].astype(F32)
            m_ref[...] = acc
            o_ref[...] = acc

    outs = pl.pallas_call(
        body, name=name,
        out_shape=[jax.ShapeDtypeStruct(s.shape[1:], F32) for s in ss]
        + [jax.ShapeDtypeStruct((2,) + s.shape[1:], F32) for s in ss],
        grid_spec=pltpu.PrefetchScalarGridSpec(
            num_scalar_prefetch=1, grid=(SUM_STEPS,),
            in_specs=[pl.BlockSpec((None, tr, s.shape[2]), lambda i, p: (p[0], i, 0)) for s, tr in zip(ss, rows)]
            + [pl.BlockSpec((3, tr, s.shape[2]), lambda i, p: (0, i, 0)) for s, tr in zip(ss, rows)],
            out_specs=[pl.BlockSpec((tr, s.shape[2]), lambda i, p: (i, 0)) for s, tr in zip(ss, rows)]
            + [pl.BlockSpec((None, tr, s.shape[2]), lambda i, p: (p[1], i, 0)) for s, tr in zip(ss, rows)]),
        compiler_params=_params("parallel"),
    )(jc_idx, *ss, *rs)
    return outs[:n], outs[n:]


def _chip_sums(grads, c_idx, tag):
    views = [g.reshape(N_CHIPS, 2, g.shape[1] // 2, g.shape[2]) for g in grads]
    arrived = _swap_halves(views, tag)
    return _pair_sums(views, arrived, c_idx, name=f"grad_pair_sums_{tag}")


def _sum_devices(blocks):
    R = blocks.shape[2]
    tr = _sum_rows(R, 2 * N_CHIPS * LANES)

    def body(b_ref, o_ref):
        acc = b_ref[0, 0]
        for d in range(1, 2 * N_CHIPS):
            acc = acc + b_ref[d // 2, d % 2]
        o_ref[...] = acc

    return pl.pallas_call(
        body, name="sum_small_over_devices", out_shape=jax.ShapeDtypeStruct((R, LANES), F32),
        grid=(R // tr,),
        in_specs=[pl.BlockSpec((N_CHIPS, 2, tr, LANES), lambda i: (0, 0, i, 0))],
        out_specs=pl.BlockSpec((tr, LANES), lambda i: (i, 0)),
        compiler_params=_params("parallel"),
    )(blocks)


def _adamw(w, g, m, v, *, name):
    R, C = w.shape
    whole_fits = 7 * 2 * R * C * 4 <= VMEM_LIMIT_BYTES // 2
    tr = R if whole_fits else next(c for c in (256, 192, 128, 64, 32, 16, 8) if R % c == 0)

    def body(w_ref, g_ref, m_ref, v_ref, d_ref, nm_ref, nv_ref):
        g = g_ref[...]
        m = ADAM_B1 * m_ref[...] + (1.0 - ADAM_B1) * g
        v = ADAM_B2 * v_ref[...] + (1.0 - ADAM_B2) * (g * g)
        nm_ref[...] = m
        nv_ref[...] = v
        m_hat = m / (1.0 - ADAM_B1 ** ADAM_STEP)
        v_hat = v / (1.0 - ADAM_B2 ** ADAM_STEP)
        d_ref[...] = -ADAM_LR * (m_hat / (jnp.sqrt(v_hat) + ADAM_EPS) + ADAM_WD * w_ref[...])

    blk = pl.BlockSpec((tr, C), lambda i: (i, 0))
    sds = jax.ShapeDtypeStruct((R, C), F32)
    return pl.pallas_call(
        body, name=name, out_shape=(sds, sds, sds), grid=(R // tr,),
        in_specs=[blk] * 4, out_specs=(blk, blk, blk),
        compiler_params=_params("parallel"),
    )(w, g, m, v)


_TILE = SUBLANES * LANES


def _pack(arrays):
    rows = []
    for a in arrays:
        flat = a.reshape(-1)
        flat = jnp.pad(flat, (0, (-flat.shape[0]) % _TILE))
        rows.append(flat.reshape(-1, LANES))
    return jnp.concatenate(rows, axis=0)


def _unpack(buf, shapes):
    out, r = [], 0
    for s in shapes:
        size = math.prod(s)
        nr = -(-size // _TILE) * SUBLANES
        out.append(buf[r:r + nr].reshape(-1)[:size].reshape(s))
        r += nr
    return out


_BIG = ("w_in_a", "w_glu", "w_kv", "w_in_b", "w_mem_kv", "w_out")
_REPLICATED = ("pre_norm_g", "post_norm_g", "lam_re", "lam_im", "log_step", "b_re", "b_im", "c_re", "c_im",
               "kv_norm_g", "b_fgate", "mem_norm_g")
_SHARDED_SMALL = ("d_skip", "b_glu", "w_fgate")
_WEIGHTS = ("pre_norm_g", "post_norm_g", "w_in_a", "lam_re", "lam_im", "log_step", "b_re", "b_im", "c_re",
            "c_im", "d_skip", "w_glu", "b_glu", "kv_norm_g", "w_kv", "w_fgate", "b_fgate", "w_in_b",
            "mem_norm_g", "w_mem_kv", "w_out")


def _halves(a):
    return a.reshape(2, a.shape[0] // 2, a.shape[1])


def _unhalve(a):
    return a.reshape(N_CHIPS, 2 * a.shape[2], a.shape[3])


def _columns(a):
    return jnp.transpose(a, (1, 0, 2)).reshape(a.shape[1], N_CHIPS * a.shape[2])


def kernel(x, mem, pre_norm_g, post_norm_g, w_in_a, lam_re, lam_im, log_step, b_re, b_im, c_re, c_im, d_skip, w_glu, b_glu, kv_norm_g, w_kv, w_fgate, b_fgate, w_in_b, mem_norm_g, w_mem_kv, w_out, loss_target, m_pre_norm_g, m_post_norm_g, m_w_in_a, m_lam_re, m_lam_im, m_log_step, m_b_re, m_b_im, m_c_re, m_c_im, m_d_skip, m_w_glu, m_b_glu, m_kv_norm_g, m_w_kv, m_w_fgate, m_b_fgate, m_w_in_b, m_mem_norm_g, m_w_mem_kv, m_w_out, v_pre_norm_g, v_post_norm_g, v_w_in_a, v_lam_re, v_lam_im, v_log_step, v_b_re, v_b_im, v_c_re, v_c_im, v_d_skip, v_w_glu, v_b_glu, v_kv_norm_g, v_w_kv, v_w_fgate, v_b_fgate, v_w_in_b, v_mem_norm_g, v_w_mem_kv, v_w_out):
    a = dict(locals())
    xi, yi, ci = lax.axis_index("x"), lax.axis_index("y"), lax.axis_index("c")
    chip = 2 * xi + yi
    c_idx = jnp.reshape(ci, (1,)).astype(jnp.int32)
    jc_idx = jnp.stack([chip, ci]).astype(jnp.int32)

    vec = jnp.zeros((2 * SUBLANES, MAIN_WIDTH // N_CHIPS), F32)
    vec = vec.at[0].set(a["d_skip"][0]).at[1].set(a["b_glu"][0])
    def own_slot(gathered, parts):
        return [lax.dynamic_update_index_in_dim(g, p, chip, 0) for g, p in zip(gathered, parts)]

    parts_a = [_halves(a["w_in_a"][0].astype(BF16)), _halves(vec)]
    parts_b = [_halves(a["w_glu"][0].astype(BF16)), _halves(a["w_mem_kv"].reshape(-1, 2 * MEM_WIDTH).astype(BF16)),
               _halves(a["w_out"].reshape(-1, D_MODEL).astype(BF16))]
    parts_c = [_halves(a["w_kv"].astype(BF16)), _halves(_pad_lanes(a["w_fgate"]).astype(BF16)),
               _halves(a["w_in_b"][0].astype(BF16))]
    travelling, token = {}, a["pre_norm_g"]
    for tag, parts in (("a", parts_a), ("b", parts_b), ("c", parts_c)):
        lands = [lax.empty((N_CHIPS,) + p.shape, p.dtype) for p in parts]
        travelling[tag], token = _ici_start(parts, lands, token, _GATHER_ROUTE, name=f"gather_{tag}_start")

    def fetch(tag, after):
        parts, lands = _ici_wait(travelling[tag], after, _GATHER_ROUTE, name=f"gather_{tag}_wait")
        full = own_slot(_gather_forward(lands, tag), parts)
        if tag == "a":
            w_in_a, vecs = full
            return dict(w_in_a=_columns(_unhalve(w_in_a)), d_skip=vecs[:, 0, 0, :].reshape(MAIN_WIDTH),
                        b_glu=vecs[:, 0, 1, :].reshape(MAIN_WIDTH))
        if tag == "b":
            w_glu, w_mk, w_out = full
            return dict(w_glu=w_glu.reshape(MAIN_WIDTH, MAIN_WIDTH),
                        w_mem_kv=[w_mk[:, i].reshape(D_MODEL, 2 * MEM_WIDTH) for i in range(2)],
                        w_out=[w_out[:, i].reshape(D_MODEL, D_MODEL) for i in range(2)])
        w_kv, w_fg, w_in_b = full
        return dict(w_kv=_columns(_unhalve(w_kv)), w_fgate=w_fg.reshape(D_MODEL, LANES),
                    w_in_b=_columns(_unhalve(w_in_b)))

    w = dict(
        pre_norm_g=token, post_norm_g=a["post_norm_g"], mem_norm_g=a["mem_norm_g"],
        kv_norm_g=a["kv_norm_g"], b_fgate=a["b_fgate"],
        lam_re=a["lam_re"][0], lam_im=a["lam_im"][0], log_step=a["log_step"][0],
        b_re=a["b_re"][0], b_im=a["b_im"][0], c_re=a["c_re"][0], c_im=a["c_im"][0])

    sent = {}

    swapping = {}

    def grads_ready(event, g, token):
        tag = event.split("_")[0]
        if event in ("b", "a1"):
            big = {"b": lambda: [g["w_kv"], g["w_in_b"], g["w_mem_kv_1"].reshape(N_CHIPS, -1, 2 * MEM_WIDTH),
                                 g["w_out_1"].reshape(N_CHIPS, -1, D_MODEL)],
                   "a1": lambda: [g["w_glu"].reshape(N_CHIPS, -1, MAIN_WIDTH),
                                  g["w_mem_kv_0"].reshape(N_CHIPS, -1, 2 * MEM_WIDTH),
                                  g["w_out_0"].reshape(N_CHIPS, -1, D_MODEL)]}[tag]()
            views = [b.reshape(N_CHIPS, 2, b.shape[1] // 2, b.shape[2]) for b in big]
            lands = [lax.empty((N_CHIPS,) + v.shape[2:], v.dtype) for v in views]
            swapping[tag], token = _ici_start(views, lands, token, _SWAP_ROUTE, name=f"grad_swap_{tag}_start")
            return token
        if event == "a2":
            sums = _chip_sums([g["w_in_a"]], c_idx, tag)
        else:
            views, arrived = _ici_wait(swapping[tag], token, _SWAP_ROUTE, name=f"grad_swap_{tag}_wait")
            sums = _pair_sums(views, arrived, c_idx, name=f"grad_pair_sums_{tag}")
        lands = [lax.empty((3,) + s.shape[1:], s.dtype) for s in sums]
        sent[tag], token = _ici_start(sums, lands, token, _SCATTER_ROUTE, name=f"grad_send_{tag}_start")
        return token

    loss_row, grad_x, g = _local_step(a["x"][0], a["mem"][0], a["loss_target"][0], w, fetch, grads_ready)

    small_names = _REPLICATED + _SHARDED_SMALL
    pack = _pack([g[n] for n in small_names])
    blocks = lax.empty((N_CHIPS, 2) + pack.shape, F32)
    small_sent, token = _ici_start([pack], [blocks], loss_row, _BLOCK_ROUTE, name="small_sums_start")

    sharing = {}
    for tag in ("b", "a1", "a2"):
        sums, arrived = _ici_wait(sent[tag], [grad_x, token], _SCATTER_ROUTE, name=f"grad_send_{tag}_wait")
        mine, bufs = _owner_sums(sums, arrived, jc_idx, name=f"grad_owner_sums_{tag}")
        sharing[tag], token = _ici_start(mine, bufs, token, _SHARE_ROUTE, name=f"grad_share_{tag}_start")
    loss = lax.psum(jnp.sum(token), MESH_AXES)

    def shared(tag, after):
        _, bufs = _ici_wait(sharing[tag], after, _SHARE_ROUTE, name=f"grad_share_{tag}_wait")
        return [b.reshape(-1, b.shape[2]) for b in bufs]

    grads, delta, new_m, new_v = {}, {}, {}, {}

    def adam(n):
        shape = a[n].shape
        d2 = (-1, shape[-1])
        d, m, v = _adamw(a[n].reshape(d2), grads[n].reshape(d2), a["m_" + n].reshape(d2),
                         a["v_" + n].reshape(d2), name="adamw_" + n)
        delta[n], new_m[n], new_v[n] = d.reshape(shape), m.reshape(shape), v.reshape(shape)
        return d

    r_kv, r_in_b, r_mk1, r_out1 = shared("b", token)
    grads["w_kv"], grads["w_in_b"] = r_kv, r_in_b[None]
    done = [adam("w_kv"), adam("w_in_b")]
    r_glu, r_mk0, r_out0 = shared("a1", done)
    grads["w_glu"], grads["w_mem_kv"], grads["w_out"] = r_glu[None], jnp.stack([r_mk0, r_mk1]), jnp.stack([r_out0, r_out1])
    done = [adam("w_glu"), adam("w_mem_kv"), adam("w_out")]
    (r_in_a,) = shared("a2", done)
    grads["w_in_a"] = r_in_a[None]
    adam("w_in_a")

    (pack,), (blocks,) = _ici_wait(small_sent, [delta[n] for n in _BIG], _BLOCK_ROUTE, name="small_sums_wait")
    blocks = lax.dynamic_update_slice(blocks, pack[None, None], (chip, ci, 0, 0))
    (blocks,) = _gather_forward([blocks], "small", own=True)
    small = dict(zip(small_names, _unpack(_sum_devices(blocks), [g[n].shape for n in small_names])))
    for n in _REPLICATED:
        grads[n] = small[n].reshape(a[n].shape)
    nd = MAIN_WIDTH // N_CHIPS
    grads["d_skip"] = lax.dynamic_slice(small["d_skip"], (chip * nd,), (nd,))[None]
    grads["b_glu"] = lax.dynamic_slice(small["b_glu"], (chip * nd,), (nd,))[None]
    nf = D_MODEL // N_CHIPS
    grads["w_fgate"] = lax.dynamic_slice(small["w_fgate"], (chip * nf, 0), (nf, FOX_HEADS))

    shapes = [a[n].shape for n in small_names]
    d, m, v = _adamw(_pack([a[n] for n in small_names]), _pack([grads[n] for n in small_names]),
                     _pack([a["m_" + n] for n in small_names]), _pack([a["v_" + n] for n in small_names]),
                     name="adamw_small")
    for n, dd, mm, vv in zip(small_names, _unpack(d, shapes), _unpack(m, shapes), _unpack(v, shapes)):
        delta[n], new_m[n], new_v[n] = dd, mm, vv

    return (loss, grad_x[None], *[grads[n] for n in _WEIGHTS], *[delta[n] for n in _WEIGHTS],
            *[new_m[n] for n in _WEIGHTS], *[new_v[n] for n in _WEIGHTS])
```

```python
import math

import jax
import jax.numpy as jnp
from jax import lax
from jax.experimental import pallas as pl
from jax.experimental.pallas import tpu as pltpu

F32 = jnp.float32
BF16 = jnp.bfloat16

D_MODEL = 2048
N_MEM = 256
MAIN_WIDTH = 1536
MEM_WIDTH = 512
IN_WIDTH = 2 * MAIN_WIDTH + 2 * MEM_WIDTH
HEAD_DIM = 128
FOX_HEADS = MAIN_WIDTH // HEAD_DIM
MEM_HEADS = MEM_WIDTH // HEAD_DIM
SSM_GROUP = 16
SSM_GROUPS = MAIN_WIDTH // SSM_GROUP
SSM_STATE = 64
GROUPS_PER_BLOCK = 8
SSM_BLOCKS = SSM_GROUPS // GROUPS_PER_BLOCK
STATE_COLS = GROUPS_PER_BLOCK * SSM_STATE
EPS = 1e-6
ADAM_LR = 0.001
ADAM_B1 = 0.9
ADAM_B2 = 0.999
ADAM_EPS = 1e-08
ADAM_WD = 0.01
ADAM_STEP = 10
N_CHIPS = 4
LANES = 128
SUBLANES = 8
VMEM_LIMIT_BYTES = 56 * 1024 * 1024
NEG_BIG = -1e30
MESH_AXES = ("x", "y", "c")


def _params(*sem):
    return pltpu.CompilerParams(dimension_semantics=sem if sem else None,
                                vmem_limit_bytes=VMEM_LIMIT_BYTES)


def _sigmoid(x):
    return 1.0 / (1.0 + jnp.exp(-x))


def _gelu(x):
    c = math.sqrt(2.0 / math.pi)
    return 0.5 * x * (1.0 + jnp.tanh(c * (x + 0.044715 * (x * x * x))))


def _gelu_grad(x):
    c = math.sqrt(2.0 / math.pi)
    t = jnp.tanh(c * (x + 0.044715 * (x * x * x)))
    return 0.5 * (1.0 + t) + 0.5 * x * (1.0 - t * t) * (c * (1.0 + 3.0 * 0.044715 * (x * x)))


def _silu_and_grad(z):
    s = _sigmoid(z)
    return z * s, s * (1.0 + z * (1.0 - s))


_TILE_CHOICES = (4096, 3072, 2048, 1536, 1024, 768, 512, 384, 256, LANES)


def _tile(n, cap):
    return next(c for c in _TILE_CHOICES if c <= cap and n % c == 0)


def _mm(a, b, *, name, ta=False, tb=False, out_dtype=F32, shards=1, tm=1024, tn=1024, tk=4096):
    if ta:
        K, M = a.shape
    else:
        M, K = a.shape
    if tb:
        N, kb = b.shape
    else:
        kb, N = b.shape
    assert K == kb, (a.shape, b.shape)
    ns = N // shards
    tm, tn, tk = _tile(M, tm), _tile(ns, tn), _tile(K, tk)
    assert M % tm == 0 and ns % tn == 0 and K % tk == 0 and N % shards == 0
    nk = K // tk
    dn = (((0 if ta else 1,), (1 if tb else 0,)), ((), ()))

    def body(a_ref, b_ref, o_ref, *acc):
        prod = lax.dot_general(a_ref[...].astype(BF16), b_ref[...].astype(BF16), dn, preferred_element_type=F32)
        if nk == 1:
            o_ref[...] = prod.astype(o_ref.dtype)
            return
        acc_ref, = acc
        k = pl.program_id(2)

        @pl.when(k == 0)
        def _():
            acc_ref[...] = jnp.zeros_like(acc_ref)

        acc_ref[...] += prod

        @pl.when(k == nk - 1)
        def _():
            o_ref[...] = acc_ref[...].astype(o_ref.dtype)

    a_spec = (pl.BlockSpec((tk, tm), lambda i, j, k: (k, i)) if ta
              else pl.BlockSpec((tm, tk), lambda i, j, k: (i, k)))
    b_spec = (pl.BlockSpec((tn, tk), lambda i, j, k: (j, k)) if tb
              else pl.BlockSpec((tk, tn), lambda i, j, k: (k, j)))
    if shards == 1:
        out_shape = jax.ShapeDtypeStruct((M, N), out_dtype)
        o_spec = pl.BlockSpec((tm, tn), lambda i, j, k: (i, j))
    else:
        nb = ns // tn
        out_shape = jax.ShapeDtypeStruct((shards, M, ns), out_dtype)
        o_spec = pl.BlockSpec((None, tm, tn), lambda i, j, k: (j // nb, i, j % nb))
    return pl.pallas_call(
        body, name=name, out_shape=out_shape,
        grid=(M // tm, N // tn, nk),
        in_specs=[a_spec, b_spec], out_specs=o_spec,
        scratch_shapes=[] if nk == 1 else [pltpu.VMEM((tm, tn), F32)],
        compiler_params=_params("parallel", "parallel", "arbitrary"),
    )(a, b)


def _rmsnorm_fwd(x, g, *, name, res=None, out_dtype=F32, tr=256):
    L, D = x.shape
    tr = min(tr, L)
    has_res = res is not None

    def body(*refs):
        if has_res:
            x_ref, g_ref, r_ref, o_ref = refs
        else:
            x_ref, g_ref, o_ref = refs
        xf = x_ref[...]
        r = lax.rsqrt(jnp.mean(xf * xf, axis=-1, keepdims=True) + EPS)
        y = xf * r * g_ref[...]
        if has_res:
            y = r_ref[...] + y
        o_ref[...] = y.astype(o_ref.dtype)

    row = pl.BlockSpec((tr, D), lambda i: (i, 0))
    vec = pl.BlockSpec((1, D), lambda i: (0, 0))
    ins = [x, g.reshape(1, D)] + ([res] if has_res else [])
    return pl.pallas_call(
        body, name=name, out_shape=jax.ShapeDtypeStruct((L, D), out_dtype),
        grid=(L // tr,), in_specs=[row, vec] + ([row] if has_res else []), out_specs=row,
        compiler_params=_params("parallel"),
    )(*ins)


def _rmsnorm_bwd(x, g, dy, *, name, adds=(), dx_dtype=F32, tr=256):
    L, D = x.shape
    tr = min(tr, L)
    dys = dy if isinstance(dy, tuple) else (dy,)
    n_dy, n_add = len(dys), len(adds)

    def body(*refs):
        x_ref, g_ref = refs[:2]
        dy_refs = refs[2:2 + n_dy]
        add_refs = refs[2 + n_dy:2 + n_dy + n_add]
        dx_ref, dg_ref = refs[2 + n_dy + n_add:]
        xf = x_ref[...]
        dyf = dy_refs[0][...].astype(F32)
        for d_ref in dy_refs[1:]:
            dyf = dyf + d_ref[...].astype(F32)
        r = lax.rsqrt(jnp.mean(xf * xf, axis=-1, keepdims=True) + EPS)
        gy = dyf * g_ref[...]
        c = jnp.mean(xf * gy, axis=-1, keepdims=True) * (r * r * r)
        dx = gy * r - xf * c
        for a_ref in add_refs:
            dx = dx + a_ref[...].astype(F32)
        dx_ref[...] = dx.astype(dx_ref.dtype)

        @pl.when(pl.program_id(0) == 0)
        def _():
            dg_ref[...] = jnp.zeros_like(dg_ref)

        dg_ref[...] += jnp.sum(dyf * xf * r, axis=0, keepdims=True)

    row = pl.BlockSpec((tr, D), lambda i: (i, 0))
    vec = pl.BlockSpec((1, D), lambda i: (0, 0))
    dx, dg = pl.pallas_call(
        body, name=name,
        out_shape=(jax.ShapeDtypeStruct((L, D), dx_dtype), jax.ShapeDtypeStruct((1, D), F32)),
        grid=(L // tr,), in_specs=[row, vec] + [row] * (n_dy + n_add), out_specs=(row, vec),
        compiler_params=_params("arbitrary"),
    )(x, g.reshape(1, D), *dys, *adds)
    return dx, dg.reshape(D)


def _rmsnorm_bwd_pair(x, g1, dy1, g2, dy2, *, name, adds=(), tr=256):
    L, D = x.shape
    tr = min(tr, L)
    dy1s = dy1 if isinstance(dy1, tuple) else (dy1,)
    n1, n_add = len(dy1s), len(adds)

    def body(*refs):
        x_ref, g1_ref, g2_ref = refs[:3]
        dy1_refs = refs[3:3 + n1]
        dy2_ref = refs[3 + n1]
        add_refs = refs[4 + n1:4 + n1 + n_add]
        dx_ref, dg1_ref, dg2_ref = refs[4 + n1 + n_add:]
        xf = x_ref[...]
        d1 = dy1_refs[0][...].astype(F32)
        for d_ref in dy1_refs[1:]:
            d1 = d1 + d_ref[...].astype(F32)
        d2 = dy2_ref[...].astype(F32)
        r = lax.rsqrt(jnp.mean(xf * xf, axis=-1, keepdims=True) + EPS)
        gy = d1 * g1_ref[...] + d2 * g2_ref[...]
        c = jnp.mean(xf * gy, axis=-1, keepdims=True) * (r * r * r)
        dx = gy * r - xf * c
        for a_ref in add_refs:
            dx = dx + a_ref[...].astype(F32)
        dx_ref[...] = dx

        @pl.when(pl.program_id(0) == 0)
        def _():
            dg1_ref[...] = jnp.zeros_like(dg1_ref)
            dg2_ref[...] = jnp.zeros_like(dg2_ref)

        xr = xf * r
        dg1_ref[...] += jnp.sum(d1 * xr, axis=0, keepdims=True)
        dg2_ref[...] += jnp.sum(d2 * xr, axis=0, keepdims=True)

    row = pl.BlockSpec((tr, D), lambda i: (i, 0))
    vec = pl.BlockSpec((1, D), lambda i: (0, 0))
    dx, dg1, dg2 = pl.pallas_call(
        body, name=name,
        out_shape=(jax.ShapeDtypeStruct((L, D), F32), jax.ShapeDtypeStruct((1, D), F32),
                   jax.ShapeDtypeStruct((1, D), F32)),
        grid=(L // tr,), in_specs=[row, vec, vec] + [row] * (n1 + 1 + n_add), out_specs=(row, vec, vec),
        compiler_params=_params("arbitrary"),
    )(x, g1.reshape(1, D), g2.reshape(1, D), *dy1s, dy2, *adds)
    return dx, dg1.reshape(D), dg2.reshape(D)


def _final_norm_loss(o, g, res, target, *, tr=256):
    L, D = o.shape
    tr = min(tr, L)

    def body(o_ref, g_ref, r_ref, t_ref, dh_ref, loss_ref):
        xf = o_ref[...]
        r = lax.rsqrt(jnp.mean(xf * xf, axis=-1, keepdims=True) + EPS)
        e = (r_ref[...] + xf * r * g_ref[...]) - t_ref[...]
        dh_ref[...] = e * (1.0 / D)

        @pl.when(pl.program_id(0) == 0)
        def _():
            loss_ref[...] = jnp.zeros_like(loss_ref)

        loss_ref[...] += jnp.sum(e * e, axis=0, keepdims=True) * (0.5 / D)

    row = pl.BlockSpec((tr, D), lambda i: (i, 0))
    vec = pl.BlockSpec((1, D), lambda i: (0, 0))
    dh, lp = pl.pallas_call(
        body, name="post_norm_1_loss",
        out_shape=(jax.ShapeDtypeStruct((L, D), F32), jax.ShapeDtypeStruct((1, D), F32)),
        grid=(L // tr,), in_specs=[row, vec, row, row], out_specs=(row, vec),
        compiler_params=_params("arbitrary"),
    )(o, g.reshape(1, D), res, target)
    return dh, lp


def _s5_coeffs(lr, li, ls):
    dt = jnp.exp(ls)
    mag = jnp.exp(lr * dt)
    ar = mag * jnp.cos(li * dt)
    ai = mag * jnp.sin(li * dt)
    den = lr * lr + li * li
    cr = ((ar - 1.0) * lr + ai * li) / den
    ci = (ai * lr - (ar - 1.0) * li) / den
    return dt, ar, ai, den, cr, ci


def _s5_prep(lam_re, lam_im, log_step, b_re_t, b_im_t):
    G, P = lam_re.shape
    H = b_re_t.shape[1]

    def body(lr_ref, li_ref, ls_ref, br_ref, bi_ref, ar_ref, ai_ref, bbr_ref, bbi_ref):
        _, ar, ai, _, cr, ci = _s5_coeffs(lr_ref[...], li_ref[...], ls_ref[...])
        ar_ref[...] = ar
        ai_ref[...] = ai
        br, bi = br_ref[...], bi_ref[...]
        crb, cib = cr[:, None, :], ci[:, None, :]
        bbr_ref[...] = crb * br - cib * bi
        bbi_ref[...] = crb * bi + cib * br

    return pl.pallas_call(
        body, name="s5_prep",
        out_shape=(jax.ShapeDtypeStruct((G, P), F32), jax.ShapeDtypeStruct((G, P), F32),
                   jax.ShapeDtypeStruct((G, H, P), F32), jax.ShapeDtypeStruct((G, H, P), F32)),
        compiler_params=_params(),
    )(lam_re, lam_im, log_step.reshape(G, 1), b_re_t, b_im_t)


def _s5_prep_bwd(lam_re, lam_im, log_step, b_re_t, b_im_t, d_ar, d_ai, d_bbr, d_bbi):
    G, P = lam_re.shape
    H = b_re_t.shape[1]

    def body(lr_ref, li_ref, ls_ref, br_ref, bi_ref, dar_ref, dai_ref, dbbr_ref, dbbi_ref,
             dlr_ref, dli_ref, dls_ref, dbr_ref, dbi_ref):
        lr, li = lr_ref[...], li_ref[...]
        dt, ar, ai, den, cr, ci = _s5_coeffs(lr, li, ls_ref[...])
        br, bi = br_ref[...], bi_ref[...]
        gbr, gbi = dbbr_ref[...], dbbi_ref[...]
        crb, cib = cr[:, None, :], ci[:, None, :]
        dbr_ref[...] = crb * gbr + cib * gbi
        dbi_ref[...] = crb * gbi - cib * gbr
        gcr = jnp.sum(br * gbr + bi * gbi, axis=1)
        gci = jnp.sum(br * gbi - bi * gbr, axis=1)
        ilr, ili = lr / den, -li / den
        gar = dar_ref[...] + (ilr * gcr + ili * gci)
        gai = dai_ref[...] + (ilr * gci - ili * gcr)
        qr, qi = cr * ilr - ci * ili, cr * ili + ci * ilr
        glr = -(qr * gcr + qi * gci)
        gli = -(qr * gci - qi * gcr)
        glr = glr + dt * (ar * gar + ai * gai)
        gli = gli + dt * (ar * gai - ai * gar)
        wr, wi = lr * ar - li * ai, lr * ai + li * ar
        gdt = jnp.sum(wr * gar + wi * gai, axis=1, keepdims=True)
        dlr_ref[...] = glr
        dli_ref[...] = gli
        dls_ref[...] = gdt * dt

    return pl.pallas_call(
        body, name="s5_prep_bwd",
        out_shape=(jax.ShapeDtypeStruct((G, P), F32), jax.ShapeDtypeStruct((G, P), F32),
                   jax.ShapeDtypeStruct((G, 1), F32),
                   jax.ShapeDtypeStruct((G, H, P), F32), jax.ShapeDtypeStruct((G, H, P), F32)),
        compiler_params=_params(),
    )(lam_re, lam_im, log_step.reshape(G, 1), b_re_t, b_im_t, d_ar, d_ai, d_bbr, d_bbi)


def _s5_block_mats(bbr_t, bbi_t, c_re, c_im):
    bmat = _s5_expand(bbr_t, bbi_t)
    cmat = jnp.transpose(_s5_expand(c_re, -c_im), (0, 2, 1))
    return bmat.astype(BF16), cmat.astype(BF16)


def _s5_diag_mask():
    r = lax.broadcasted_iota(jnp.int32, (LANES, 2 * STATE_COLS), 0) // SSM_GROUP
    c = (lax.broadcasted_iota(jnp.int32, (LANES, 2 * STATE_COLS), 1) % STATE_COLS) // SSM_STATE
    return (r == c).astype(F32)


def _s5_expand(re, im):
    re = jnp.tile(re.reshape(SSM_BLOCKS, LANES, SSM_STATE), (1, 1, GROUPS_PER_BLOCK))
    im = jnp.tile(im.reshape(SSM_BLOCKS, LANES, SSM_STATE), (1, 1, GROUPS_PER_BLOCK))
    return jnp.concatenate([re, im], axis=-1) * _s5_diag_mask()[None]


def _s5_unfold(dmat):
    d = dmat.reshape(SSM_GROUPS, SSM_GROUP, 2, SSM_STATE)
    return jnp.transpose(d, (2, 0, 1, 3))


def _s5_a_rows(ar, ai):
    a = jnp.concatenate([ar.reshape(SSM_BLOCKS, STATE_COLS), ai.reshape(SSM_BLOCKS, STATE_COLS)], axis=1)
    return jnp.broadcast_to(a[:, None, :], (SSM_BLOCKS, SUBLANES, 2 * STATE_COLS))


def _to_step_major(src_ref, dst_ref, seg):
    for s in range(SUBLANES):
        dst_ref[pl.ds(s, seg, stride=SUBLANES), :] = src_ref[pl.ds(seg * s, seg), :]


def _segment_rows(ref, s, seg):
    return ref[pl.ds(s, seg, stride=SUBLANES), :]


def _cmul(ar, ai, xr, xi):
    return ar * xr - ai * xi, ar * xi + ai * xr


def _s5_tables(a_ref, pw_s, pwr_s, S, seg):
    ar, ai = a_ref[:, :S], a_ref[:, S:]

    def step(i, c):
        pr, pi = c
        pw_s[i, :, :S] = pr
        pw_s[i, :, S:] = pi
        nr, ni = _cmul(ar, ai, pr, pi)
        pwr_s[seg - 1 - i, :, :S] = nr
        pwr_s[seg - 1 - i, :, S:] = ni
        return nr, ni

    pr, pi = lax.fori_loop(0, seg, step, (jnp.ones_like(ar), jnp.zeros_like(ai)))
    pw_s[seg, :, :S] = pr
    pw_s[seg, :, S:] = pi


def _s5_fwd(proj, bmat, cmat, a_rows, d_skip, *, tc=512):
    L = proj.shape[0]
    tc = min(tc, L)
    nt = L // tc
    seg = tc // SUBLANES
    S = STATE_COLS

    def body(u_ref, b_ref, c_ref, a_ref, d_ref, y_ref, yg_ref, xp_ref,
             bu_s, xp_s, pw_s, pwr_s, carry_s, e_s, up_s, yc_s):
        @pl.when(pl.program_id(1) == 0)
        def _():
            carry_s[...] = jnp.zeros_like(carry_s)
            _s5_tables(a_ref, pw_s, pwr_s, S, seg)

        ar, ai = a_ref[:, :S], a_ref[:, S:]
        _to_step_major(u_ref, up_s, seg)
        bu = jnp.dot(up_s[...].astype(BF16), b_ref[...], preferred_element_type=F32)
        bu_s[...] = bu.reshape(seg, SUBLANES, 2 * S)

        def step(i, carry):
            cr, ci = carry
            xp_s[i, :, :S] = cr
            xp_s[i, :, S:] = ci
            return ar * cr - ai * ci + bu_s[i, :, :S], ar * ci + ai * cr + bu_s[i, :, S:]

        zero = jnp.zeros((SUBLANES, S), F32)
        fr, fi = lax.fori_loop(0, seg, step, (zero, zero))
        pr, pi = pw_s[seg, 0:1, :S], pw_s[seg, 0:1, S:]
        er, ei = carry_s[0:1, :S], carry_s[0:1, S:]
        for s in range(SUBLANES):
            e_s[s:s + 1, :S] = er
            e_s[s:s + 1, S:] = ei
            tr, ti = _cmul(pr, pi, er, ei)
            er, ei = fr[s:s + 1] + tr, fi[s:s + 1] + ti
        carry_s[0:1, :S] = er
        carry_s[0:1, S:] = ei
        pw = pw_s[0:seg]
        tr, ti = _cmul(pw[:, :, :S], pw[:, :, S:], e_s[:, :S][None], e_s[:, S:][None])
        xl = xp_s[...]
        xp = jnp.concatenate([xl[:, :, :S] + tr, xl[:, :, S:] + ti], axis=-1).reshape(tc, 2 * S)
        xp_ref[...] = xp
        a1r, a1i = ar[0:1], ai[0:1]
        x_re = a1r * xp[:, :S] - a1i * xp[:, S:] + bu[:, :S]
        x_im = a1r * xp[:, S:] + a1i * xp[:, :S] + bu[:, S:]
        xs = jnp.concatenate([x_re, x_im], axis=1).astype(BF16)
        yc_s[...] = jnp.dot(xs, c_ref[...], preferred_element_type=F32)
        for s in range(SUBLANES):
            rows = pl.ds(seg * s, seg)
            y = _segment_rows(yc_s, s, seg) + d_ref[...] * u_ref[rows, :]
            y_ref[rows, :] = y
            yg_ref[rows, :] = _gelu(y).astype(BF16)

    return pl.pallas_call(
        body, name="s5_fwd",
        out_shape=(jax.ShapeDtypeStruct((L, MAIN_WIDTH), F32),
                   jax.ShapeDtypeStruct((L, MAIN_WIDTH), BF16),
                   jax.ShapeDtypeStruct((L, SSM_BLOCKS * 2 * S), F32)),
        grid=(SSM_BLOCKS, nt),
        in_specs=[pl.BlockSpec((tc, LANES), lambda b, t: (t, b)),
                  pl.BlockSpec((None, LANES, 2 * S), lambda b, t: (b, 0, 0)),
                  pl.BlockSpec((None, 2 * S, LANES), lambda b, t: (b, 0, 0)),
                  pl.BlockSpec((None, SUBLANES, 2 * S), lambda b, t: (b, 0, 0)),
                  pl.BlockSpec((1, LANES), lambda b, t: (0, b))],
        out_specs=(pl.BlockSpec((tc, LANES), lambda b, t: (t, b)),
                   pl.BlockSpec((tc, LANES), lambda b, t: (t, b)),
                   pl.BlockSpec((tc, 2 * S), lambda b, t: (t, b))),
        scratch_shapes=[pltpu.VMEM((seg, SUBLANES, 2 * S), F32),
                        pltpu.VMEM((seg, SUBLANES, 2 * S), F32),
                        pltpu.VMEM((seg + 1, SUBLANES, 2 * S), F32),
                        pltpu.VMEM((seg, SUBLANES, 2 * S), F32),
                        pltpu.VMEM((SUBLANES, 2 * S), F32),
                        pltpu.VMEM((SUBLANES, 2 * S), F32),
                        pltpu.VMEM((tc, LANES), F32),
                        pltpu.VMEM((tc, LANES), F32)],
        compiler_params=_params("parallel", "arbitrary"),
    )(proj, bmat, cmat, a_rows, d_skip.reshape(1, MAIN_WIDTH))


def _s5_bwd(proj, dyg_a, dyg_b, y, xp, bmat, cmat, a_rows, d_skip, dproj, *, tc=512):
    L = proj.shape[0]
    tc = min(tc, L)
    nt = L // tc
    seg = tc // SUBLANES
    S = STATE_COLS
    nn = (((1,), (1,)), ((), ()))
    tn = (((0,), (0,)), ((), ()))

    def fold_diagonal(acc_ref, mask_ref, fold_ref):
        x = acc_ref[...] * mask_ref[...]
        hi = x.astype(BF16)
        rest = x - hi.astype(F32)
        mid = rest.astype(BF16)
        low = (rest - mid.astype(F32)).astype(BF16)
        return sum(jnp.dot(piece, fold_ref[...], preferred_element_type=F32) for piece in (hi, mid, low))

    def body(u_ref, dyga_ref, dygb_ref, y_ref, xp_ref, b_ref, c_ref, a_ref, d_ref, mask_ref, fold_ref, dp_hbm,
             du_ref, dbd_ref, dcd_ref, da_ref, dd_ref,
             dl_s, pw_s, pwr_s, carry_s, e_s, up_s, dy_s, dyp_s, dup_s, db_ref, dc_ref):
        @pl.when(pl.program_id(1) == 0)
        def _():
            carry_s[...] = jnp.zeros_like(carry_s)
            db_ref[...] = jnp.zeros_like(db_ref)
            dc_ref[...] = jnp.zeros_like(dc_ref)
            da_ref[...] = jnp.zeros_like(da_ref)
            dd_ref[...] = jnp.zeros_like(dd_ref)
            _s5_tables(a_ref, pw_s, pwr_s, S, seg)

        ar, ai = a_ref[:, :S], a_ref[:, S:]
        a1r, a1i = ar[0:1], ai[0:1]
        u = u_ref[...]
        dy = (dyga_ref[...] + dygb_ref[...]) * _gelu_grad(y_ref[...])
        dy_s[...] = dy
        xp = xp_ref[...]
        _to_step_major(u_ref, up_s, seg)
        _to_step_major(dy_s, dyp_s, seg)
        ubp = up_s[...].astype(BF16)
        dyp = dyp_s[...].astype(BF16)
        bu = jnp.dot(ubp, b_ref[...], preferred_element_type=F32)
        x_re = a1r * xp[:, :S] - a1i * xp[:, S:] + bu[:, :S]
        x_im = a1r * xp[:, S:] + a1i * xp[:, :S] + bu[:, S:]
        xs = jnp.concatenate([x_re, x_im], axis=1).astype(BF16)
        dc_ref[...] += lax.dot_general(dyp, xs, tn, preferred_element_type=F32)
        dx = lax.dot_general(dyp, c_ref[...], nn, preferred_element_type=F32)
        dl_s[...] = dx.reshape(seg, SUBLANES, 2 * S)

        def step(k, carry):
            cr, ci = carry
            i = seg - 1 - k
            lr = dl_s[i, :, :S] + (ar * cr + ai * ci)
            li = dl_s[i, :, S:] + (ar * ci - ai * cr)
            dl_s[i, :, :S] = lr
            dl_s[i, :, S:] = li
            return lr, li

        zero = jnp.zeros((SUBLANES, S), F32)
        fr, fi = lax.fori_loop(0, seg, step, (zero, zero))
        pr, pi = pw_s[seg, 0:1, :S], pw_s[seg, 0:1, S:]
        er, ei = carry_s[0:1, :S], carry_s[0:1, S:]
        for s in range(SUBLANES - 1, -1, -1):
            e_s[s:s + 1, :S] = er
            e_s[s:s + 1, S:] = ei
            er, ei = fr[s:s + 1] + (pr * er + pi * ei), fi[s:s + 1] + (pr * ei - pi * er)
        carry_s[0:1, :S] = er
        carry_s[0:1, S:] = ei
        er, ei = e_s[:, :S][None], e_s[:, S:][None]
        pw = pwr_s[...]
        pwr, pwi = pw[:, :, :S], pw[:, :, S:]
        ll = dl_s[...]
        lam = jnp.concatenate([ll[:, :, :S] + (pwr * er + pwi * ei), ll[:, :, S:] + (pwr * ei - pwi * er)],
                              axis=-1).reshape(tc, 2 * S)
        l_re, l_im = lam[:, :S], lam[:, S:]
        da_ref[0:1, :S] += jnp.sum(l_re * xp[:, :S] + l_im * xp[:, S:], axis=0, keepdims=True)
        da_ref[0:1, S:] += jnp.sum(l_im * xp[:, :S] - l_re * xp[:, S:], axis=0, keepdims=True)
        lamb = lam.astype(BF16)
        dup_s[...] = lax.dot_general(lamb, b_ref[...], nn, preferred_element_type=F32)
        for s in range(SUBLANES):
            rows = pl.ds(seg * s, seg)
            du = _segment_rows(dup_s, s, seg) + d_ref[...] * dy_s[rows, :]
            du_ref[rows, :] = du.astype(du_ref.dtype)
        db_ref[...] += lax.dot_general(ubp, lamb, tn, preferred_element_type=F32)
        dd_ref[0:1, :] += jnp.sum(dy * u, axis=0, keepdims=True)

        @pl.when(pl.program_id(1) == nt - 1)
        def _():
            dbd_ref[...] = fold_diagonal(db_ref, mask_ref, fold_ref)
            dcd_ref[...] = fold_diagonal(dc_ref, mask_ref, fold_ref)

    rev = lambda b, t: (nt - 1 - t, b)
    col = jnp.arange(2 * S)
    fold = ((col // S * SSM_STATE + col % SSM_STATE)[:, None] == jnp.arange(LANES)[None, :]).astype(BF16)
    return pl.pallas_call(
        body, name="s5_bwd",
        out_shape=(jax.ShapeDtypeStruct(dproj.shape, dproj.dtype),
                   jax.ShapeDtypeStruct((SSM_BLOCKS, LANES, LANES), F32),
                   jax.ShapeDtypeStruct((SSM_BLOCKS, LANES, LANES), F32),
                   jax.ShapeDtypeStruct((SSM_BLOCKS, SUBLANES, 2 * S), F32),
                   jax.ShapeDtypeStruct((SUBLANES, MAIN_WIDTH), F32)),
        input_output_aliases={11: 0},
        grid=(SSM_BLOCKS, nt),
        in_specs=[pl.BlockSpec((tc, LANES), rev),
                  pl.BlockSpec((tc, LANES), rev),
                  pl.BlockSpec((tc, LANES), rev),
                  pl.BlockSpec((tc, LANES), rev),
                  pl.BlockSpec((tc, 2 * S), rev),
                  pl.BlockSpec((None, LANES, 2 * S), lambda b, t: (b, 0, 0)),
                  pl.BlockSpec((None, 2 * S, LANES), lambda b, t: (b, 0, 0)),
                  pl.BlockSpec((None, SUBLANES, 2 * S), lambda b, t: (b, 0, 0)),
                  pl.BlockSpec((1, LANES), lambda b, t: (0, b)),
                  pl.BlockSpec((LANES, 2 * S), lambda b, t: (0, 0)),
                  pl.BlockSpec((2 * S, LANES), lambda b, t: (0, 0)),
                  _ANY],
        out_specs=(pl.BlockSpec((tc, LANES), rev),
                   pl.BlockSpec((None, LANES, LANES), lambda b, t: (b, 0, 0)),
                   pl.BlockSpec((None, LANES, LANES), lambda b, t: (b, 0, 0)),
                   pl.BlockSpec((None, SUBLANES, 2 * S), lambda b, t: (b, 0, 0)),
                   pl.BlockSpec((SUBLANES, LANES), lambda b, t: (0, b))),
        scratch_shapes=[pltpu.VMEM((seg, SUBLANES, 2 * S), F32),
                        pltpu.VMEM((seg + 1, SUBLANES, 2 * S), F32),
                        pltpu.VMEM((seg, SUBLANES, 2 * S), F32),
                        pltpu.VMEM((SUBLANES, 2 * S), F32),
                        pltpu.VMEM((SUBLANES, 2 * S), F32),
                        pltpu.VMEM((tc, LANES), F32),
                        pltpu.VMEM((tc, LANES), F32),
                        pltpu.VMEM((tc, LANES), F32),
                        pltpu.VMEM((tc, LANES), F32),
                        pltpu.VMEM((LANES, 2 * S), F32),
                        pltpu.VMEM((LANES, 2 * S), F32)],
        compiler_params=_params("parallel", "arbitrary"),
    )(proj, dyg_a, dyg_b, y, xp, bmat, cmat, a_rows, d_skip.reshape(1, MAIN_WIDTH), _s5_diag_mask(), fold, dproj)


_Z_COLS = slice(MAIN_WIDTH, 2 * MAIN_WIDTH)
_ZM_COLS = slice(2 * MAIN_WIDTH + MEM_WIDTH, IN_WIDTH)


def _proj_rows(tr):
    return pl.BlockSpec((tr, IN_WIDTH), lambda i: (i, 0))


def _row_specs(tr):
    main = pl.BlockSpec((tr, MAIN_WIDTH), lambda i: (i, 0))
    z = pl.BlockSpec((tr, MAIN_WIDTH), lambda i: (i, 1))
    zm = pl.BlockSpec((tr, MEM_WIDTH), lambda i: (i, IN_WIDTH // MEM_WIDTH - 1))
    mem = pl.BlockSpec((tr, MEM_WIDTH), lambda i: (i, 0))
    cat = pl.BlockSpec((tr, D_MODEL), lambda i: (i, 0))
    vec = pl.BlockSpec((1, MAIN_WIDTH), lambda i: (0, 0))
    return main, z, zm, mem, cat, vec


def _gate_a_fwd(y, t, b_glu, proj, o_mem, *, tr=256):
    L = y.shape[0]
    tr = min(tr, L)

    def body(y_ref, t_ref, b_ref, z_ref, zm_ref, om_ref, o_ref):
        yg = _gelu(y_ref[...])
        sz, _ = _silu_and_grad(z_ref[...])
        o_ref[:, :MAIN_WIDTH] = (yg * _sigmoid(t_ref[...] + b_ref[...]) * sz).astype(BF16)
        szm, _ = _silu_and_grad(zm_ref[...])
        o_ref[:, MAIN_WIDTH:] = (om_ref[...] * szm).astype(BF16)

    main, z, zm, mem, cat, vec = _row_specs(tr)
    return pl.pallas_call(
        body, name="gate_a_fwd", out_shape=jax.ShapeDtypeStruct((L, D_MODEL), BF16),
        grid=(L // tr,), in_specs=[main, main, vec, z, zm, mem], out_specs=cat,
        compiler_params=_params("parallel"),
    )(y, t, b_glu.reshape(1, MAIN_WIDTH), proj, proj, o_mem)


def _gate_a_bwd(dcat, y, t, b_glu, proj, o_mem, *, tr=256):
    L = y.shape[0]
    tr = min(tr, L)

    def body(dc_ref, y_ref, t_ref, b_ref, z_ref, zm_ref, om_ref,
             dp_ref, dt_ref, dyg_ref, dom_ref, db_ref):
        dmain = dc_ref[:, :MAIN_WIDTH]
        dmemo = dc_ref[:, MAIN_WIDTH:]
        yg = _gelu(y_ref[...])
        sg = _sigmoid(t_ref[...] + b_ref[...])
        sz, gz = _silu_and_grad(z_ref[...])
        dp_ref[:, _Z_COLS] = (dmain * (yg * sg) * gz).astype(BF16)
        dy2 = dmain * sz
        dyg_ref[...] = dy2 * sg
        dt = dy2 * yg * (sg * (1.0 - sg))
        dt_ref[...] = dt.astype(BF16)

        @pl.when(pl.program_id(0) == 0)
        def _():
            db_ref[...] = jnp.zeros_like(db_ref)

        db_ref[...] += jnp.sum(dt, axis=0, keepdims=True)
        szm, gzm = _silu_and_grad(zm_ref[...])
        dom_ref[...] = dmemo * szm
        dp_ref[:, _ZM_COLS] = (dmemo * om_ref[...] * gzm).astype(BF16)

    main, z, zm, mem, cat, vec = _row_specs(tr)
    outs = pl.pallas_call(
        body, name="gate_a_bwd",
        out_shape=(jax.ShapeDtypeStruct((L, IN_WIDTH), BF16),
                   jax.ShapeDtypeStruct((L, MAIN_WIDTH), BF16), jax.ShapeDtypeStruct((L, MAIN_WIDTH), F32),
                   jax.ShapeDtypeStruct((L, MEM_WIDTH), F32), jax.ShapeDtypeStruct((1, MAIN_WIDTH), F32)),
        grid=(L // tr,), in_specs=[cat, main, main, vec, z, zm, mem],
        out_specs=(_proj_rows(tr), main, main, mem, vec),
        compiler_params=_params("arbitrary"),
    )(dcat, y, t, b_glu.reshape(1, MAIN_WIDTH), proj, proj, o_mem)
    return outs


def _gate_b_fwd(att, proj, o_mem, *, tr=256):
    L = att.shape[0]
    tr = min(tr, L)

    def body(a_ref, z_ref, zm_ref, om_ref, o_ref):
        sz, _ = _silu_and_grad(z_ref[...])
        o_ref[:, :MAIN_WIDTH] = (a_ref[...] * sz).astype(BF16)
        szm, _ = _silu_and_grad(zm_ref[...])
        o_ref[:, MAIN_WIDTH:] = (om_ref[...] * szm).astype(BF16)

    main, z, zm, mem, cat, _ = _row_specs(tr)
    return pl.pallas_call(
        body, name="gate_b_fwd", out_shape=jax.ShapeDtypeStruct((L, D_MODEL), BF16),
        grid=(L // tr,), in_specs=[main, z, zm, mem], out_specs=cat,
        compiler_params=_params("parallel"),
    )(att, proj, proj, o_mem)


def _gate_b_bwd(dcat, att, proj, o_mem, *, tr=256):
    L = att.shape[0]
    tr = min(tr, L)

    def body(dc_ref, a_ref, z_ref, zm_ref, om_ref, da_ref, dp_ref, dom_ref, dl_ref):
        dmain = dc_ref[:, :MAIN_WIDTH]
        dmemo = dc_ref[:, MAIN_WIDTH:]
        att = a_ref[...]
        sz, gz = _silu_and_grad(z_ref[...])
        datt = dmain * sz
        da_ref[...] = datt
        dp_ref[:, _Z_COLS] = (dmain * att * gz).astype(BF16)
        szm, gzm = _silu_and_grad(zm_ref[...])
        dom_ref[...] = dmemo * szm
        dp_ref[:, _ZM_COLS] = (dmemo * om_ref[...] * gzm).astype(BF16)
        prod = datt * att
        for h in range(FOX_HEADS):
            dl_ref[h] = jnp.sum(prod[:, h * HEAD_DIM:(h + 1) * HEAD_DIM], axis=1, keepdims=True)

    main, z, zm, mem, cat, _ = _row_specs(tr)
    delta = pl.BlockSpec((FOX_HEADS, tr, 1), lambda i: (0, i, 0))
    return pl.pallas_call(
        body, name="gate_b_bwd",
        out_shape=(jax.ShapeDtypeStruct((L, MAIN_WIDTH), F32), jax.ShapeDtypeStruct((L, IN_WIDTH), BF16),
                   jax.ShapeDtypeStruct((L, MEM_WIDTH), F32), jax.ShapeDtypeStruct((FOX_HEADS, L, 1), F32)),
        grid=(L // tr,), in_specs=[cat, main, z, zm, mem], out_specs=(main, _proj_rows(tr), mem, delta),
        compiler_params=_params("parallel"),
    )(dcat, att, proj, proj, o_mem)


_MEM_Q_COL = (2 * MAIN_WIDTH) // HEAD_DIM
_NT = (((1,), (1,)), ((), ()))
_TN = (((0,), (0,)), ((), ()))


def _mem_probs(q_ref, k_ref):
    qs = (q_ref[...] * (HEAD_DIM ** -0.5)).astype(BF16)
    s = lax.dot_general(qs, k_ref[...].astype(BF16), _NT, preferred_element_type=F32)
    e = jnp.exp(s - jnp.max(s, axis=-1, keepdims=True))
    return qs, e / jnp.sum(e, axis=-1, keepdims=True)


def _mem_attn_fwd(proj, kvm, *, tq=2048):
    L = proj.shape[0]
    tq = min(tq, L)

    def body(q_ref, k_ref, v_ref, o_ref):
        _, p = _mem_probs(q_ref, k_ref)
        o_ref[...] = jnp.dot(p.astype(BF16), v_ref[...].astype(BF16), preferred_element_type=F32)

    return pl.pallas_call(
        body, name="mem_attn_fwd", out_shape=jax.ShapeDtypeStruct((L, MEM_WIDTH), F32),
        grid=(MEM_HEADS, L // tq),
        in_specs=[pl.BlockSpec((tq, HEAD_DIM), lambda h, i: (i, _MEM_Q_COL + h)),
                  pl.BlockSpec((N_MEM, HEAD_DIM), lambda h, i: (0, h)),
                  pl.BlockSpec((N_MEM, HEAD_DIM), lambda h, i: (0, MEM_HEADS + h))],
        out_specs=pl.BlockSpec((tq, HEAD_DIM), lambda h, i: (i, h)),
        compiler_params=_params("parallel", "parallel"),
    )(proj, kvm, kvm)


def _mem_attn_bwd(proj, kvm, do, dproj, *, tq=2048):
    L = proj.shape[0]
    tq = min(tq, L)

    def body(q_ref, k_ref, v_ref, do_ref, dp_hbm, dq_ref, dk_ref, dv_ref):
        @pl.when(pl.program_id(1) == 0)
        def _():
            dk_ref[...] = jnp.zeros_like(dk_ref)
            dv_ref[...] = jnp.zeros_like(dv_ref)

        qs, p = _mem_probs(q_ref, k_ref)
        dob = do_ref[...].astype(BF16)
        dp = lax.dot_general(dob, v_ref[...].astype(BF16), _NT, preferred_element_type=F32)
        ds = p * (dp - jnp.sum(p * dp, axis=-1, keepdims=True))
        dsb = ds.astype(BF16)
        dq = jnp.dot(dsb, k_ref[...].astype(BF16), preferred_element_type=F32) * (HEAD_DIM ** -0.5)
        dq_ref[...] = dq.astype(BF16)
        dk_ref[...] += lax.dot_general(dsb, qs, _TN, preferred_element_type=F32)
        dv_ref[...] += lax.dot_general(p.astype(BF16), dob, _TN, preferred_element_type=F32)

    dproj, dk, dv = pl.pallas_call(
        body, name="mem_attn_bwd",
        out_shape=(jax.ShapeDtypeStruct(dproj.shape, dproj.dtype),
                   jax.ShapeDtypeStruct((N_MEM, MEM_WIDTH), F32),
                   jax.ShapeDtypeStruct((N_MEM, MEM_WIDTH), F32)),
        grid=(MEM_HEADS, L // tq),
        in_specs=[pl.BlockSpec((tq, HEAD_DIM), lambda h, i: (i, _MEM_Q_COL + h)),
                  pl.BlockSpec((N_MEM, HEAD_DIM), lambda h, i: (0, h)),
                  pl.BlockSpec((N_MEM, HEAD_DIM), lambda h, i: (0, MEM_HEADS + h)),
                  pl.BlockSpec((tq, HEAD_DIM), lambda h, i: (i, h)),
                  _ANY],
        out_specs=(pl.BlockSpec((tq, HEAD_DIM), lambda h, i: (i, _MEM_Q_COL + h)),
                   pl.BlockSpec((N_MEM, HEAD_DIM), lambda h, i: (0, h)),
                   pl.BlockSpec((N_MEM, HEAD_DIM), lambda h, i: (0, h))),
        input_output_aliases={4: 0},
        compiler_params=_params("parallel", "arbitrary"),
    )(proj, kvm, kvm, do, dproj)
    return dproj, jnp.concatenate([dk, dv], axis=1)


def _tile_cumsum(x, row, reverse):
    for sh in (1, 2, 4):
        if reverse:
            x = x + jnp.where(row < SUBLANES - sh, pltpu.roll(x, SUBLANES - sh, 0), 0.0)
        else:
            x = x + jnp.where(row >= sh, pltpu.roll(x, sh, 0), 0.0)
    return x


def _fgate_fwd(pre, b_pad):
    L = pre.shape[0]
    n8 = L // SUBLANES

    def body(p_ref, b_ref, o_ref):
        row = lax.broadcasted_iota(jnp.int32, (SUBLANES, LANES), 0)
        b = b_ref[...]

        def step(i, carry):
            x = p_ref[i] + b
            logf = jnp.minimum(x, 0.0) - jnp.log(1.0 + jnp.exp(-jnp.abs(x)))
            t = _tile_cumsum(logf, row, False) + carry
            o_ref[i] = t
            return t[SUBLANES - 1:SUBLANES, :]

        lax.fori_loop(0, n8, step, jnp.zeros((1, LANES), F32))

    out = pl.pallas_call(
        body, name="fgate_fwd", out_shape=jax.ShapeDtypeStruct((n8, SUBLANES, LANES), F32),
        compiler_params=_params(),
    )(pre.reshape(n8, SUBLANES, LANES), b_pad.reshape(1, LANES))
    return out.reshape(L, LANES)


def _fgate_bwd(dfcum, pre, b_pad):
    L = pre.shape[0]
    n8 = L // SUBLANES

    def body(d_ref, p_ref, b_ref, o_ref, s_ref):
        row = lax.broadcasted_iota(jnp.int32, (SUBLANES, LANES), 0)
        b = b_ref[...]

        def step(k, carry):
            c, acc = carry
            i = n8 - 1 - k
            t = _tile_cumsum(d_ref[i], row, True) + c
            dpre = t * _sigmoid(-(p_ref[i] + b))
            o_ref[i] = dpre
            return t[0:1, :], acc + dpre

        _, acc = lax.fori_loop(0, n8, step, (jnp.zeros((1, LANES), F32), jnp.zeros((SUBLANES, LANES), F32)))
        s_ref[...] = jnp.sum(acc, axis=0, keepdims=True)

    dpre, db = pl.pallas_call(
        body, name="fgate_bwd",
        out_shape=(jax.ShapeDtypeStruct((n8, SUBLANES, LANES), F32), jax.ShapeDtypeStruct((1, LANES), F32)),
        compiler_params=_params(),
    )(dfcum.reshape(n8, SUBLANES, LANES), pre.reshape(n8, SUBLANES, LANES), b_pad.reshape(1, LANES))
    return dpre.reshape(L, LANES), db


FOX_BLOCK = 512


def _fox_scores(qs, k, fk, diagonal):
    s = lax.dot_general(qs, k, _NT, preferred_element_type=F32) - fk
    if diagonal:
        row = lax.broadcasted_iota(jnp.int32, s.shape, 0)
        col = lax.broadcasted_iota(jnp.int32, s.shape, 1)
        s = jnp.where(row >= col, s, NEG_BIG)
    return s


def _fox_specs(tq, L):
    nq = L // tq
    return dict(
        rows=lambda off: pl.BlockSpec((tq, HEAD_DIM), lambda h, i: (i, off + h)),
        seq=lambda off: pl.BlockSpec((L, HEAD_DIM), lambda h, i: (0, off + h)),
        col=pl.BlockSpec((None, None, tq, 1), lambda h, i: (h, i, 0, 0)),
        col_all=pl.BlockSpec((None, nq, tq, 1), lambda h, i: (h, 0, 0, 0)),
        row=pl.BlockSpec((None, None, 1, tq), lambda h, i: (h, i, 0, 0)),
        row_all=pl.BlockSpec((None, nq, 1, tq), lambda h, i: (h, 0, 0, 0)))


FOX_FWD_HEADS = 2


def _fox_fwd(proj, kv, fk):
    L = proj.shape[0]
    tq = min(FOX_BLOCK, L)
    nq = L // tq
    nh = FOX_FWD_HEADS
    W = nh * HEAD_DIM

    def body(q_ref, k_ref, v_ref, fk_ref, o_ref, lse_ref, m_s, l_s, acc_s):
        qi = pl.program_id(1)
        cols = [slice(a * HEAD_DIM, (a + 1) * HEAD_DIM) for a in range(nh)]
        qs = [(q_ref[:, cs] * (HEAD_DIM ** -0.5)).astype(BF16) for cs in cols]
        m_s[...] = jnp.full_like(m_s, NEG_BIG)
        l_s[...] = jnp.zeros_like(l_s)
        acc_s[...] = jnp.zeros_like(acc_s)

        def block(j, diagonal):
            r0 = pl.multiple_of(j * tq, tq)
            for a, cs in enumerate(cols):
                s = _fox_scores(qs[a], k_ref[pl.ds(r0, tq), cs], fk_ref[a, j], diagonal)
                m_new = jnp.maximum(m_s[a], jnp.max(s, axis=-1, keepdims=True))
                alpha = jnp.exp(m_s[a] - m_new)
                p = jnp.exp(s - m_new)
                l_s[a] = alpha * l_s[a] + jnp.sum(p, axis=-1, keepdims=True)
                acc_s[a] = alpha * acc_s[a] + jnp.dot(p.astype(BF16), v_ref[pl.ds(r0, tq), cs],
                                                      preferred_element_type=F32)
                m_s[a] = m_new

        def below(j, carry):
            block(j, False)
            return carry

        lax.fori_loop(0, qi, below, 0)
        block(qi, True)
        for a, cs in enumerate(cols):
            o_ref[:, cs] = acc_s[a] / l_s[a]
            lse_ref[a] = m_s[a] + jnp.log(l_s[a])

    return pl.pallas_call(
        body, name="fox_fwd",
        out_shape=(jax.ShapeDtypeStruct((L, MAIN_WIDTH), F32),
                   jax.ShapeDtypeStruct((FOX_HEADS, nq, tq, 1), F32)),
        grid=(FOX_HEADS // nh, nq),
        in_specs=[pl.BlockSpec((tq, W), lambda h, i: (i, h)),
                  pl.BlockSpec((L, W), lambda h, i: (0, h)),
                  pl.BlockSpec((L, W), lambda h, i: (0, FOX_HEADS // nh + h)),
                  pl.BlockSpec((nh, nq, 1, tq), lambda h, i: (h, 0, 0, 0))],
        out_specs=(pl.BlockSpec((tq, W), lambda h, i: (i, h)),
                   pl.BlockSpec((nh, None, tq, 1), lambda h, i: (h, i, 0, 0))),
        scratch_shapes=[pltpu.VMEM((nh, tq, 1), F32), pltpu.VMEM((nh, tq, 1), F32),
                        pltpu.VMEM((nh, tq, HEAD_DIM), F32)],
        compiler_params=_params("parallel", "parallel"),
    )(proj, kv, kv, fk)


def _fox_bwd_dq(proj, kv, fk, lse, delta, datt, dproj):
    L = proj.shape[0]
    tq = min(FOX_BLOCK, L)
    nq = L // tq
    sp = _fox_specs(tq, L)

    def body(q_ref, k_ref, v_ref, fk_ref, lse_ref, dl_ref, do_ref, dp_hbm, dq_ref, df_ref, acc_s, df_s):
        qi = pl.program_id(1)
        qs = (q_ref[...] * (HEAD_DIM ** -0.5)).astype(BF16)
        dob = do_ref[...].astype(BF16)
        lse, dl = lse_ref[...], dl_ref[...]
        acc_s[...] = jnp.zeros_like(acc_s)
        df_s[...] = jnp.zeros_like(df_s)

        def block(j, diagonal):
            r0 = pl.multiple_of(j * tq, tq)
            k = k_ref[pl.ds(r0, tq), :]
            p = jnp.exp(_fox_scores(qs, k, fk_ref[j], diagonal) - lse)
            dp = lax.dot_general(dob, v_ref[pl.ds(r0, tq), :], _NT, preferred_element_type=F32)
            ds = p * (dp - dl)
            acc_s[...] += jnp.dot(ds.astype(BF16), k, preferred_element_type=F32)
            df_s[...] += jnp.sum(ds, axis=1, keepdims=True)

        def below(j, carry):
            block(j, False)
            return carry

        lax.fori_loop(0, qi, below, 0)
        block(qi, True)
        dq_ref[...] = (acc_s[...] * (HEAD_DIM ** -0.5)).astype(BF16)
        df_ref[...] = df_s[...]

    return pl.pallas_call(
        body, name="fox_bwd_dq",
        out_shape=(jax.ShapeDtypeStruct(dproj.shape, dproj.dtype),
                   jax.ShapeDtypeStruct((FOX_HEADS, nq, tq, 1), F32)),
        grid=(FOX_HEADS, nq),
        in_specs=[sp["rows"](0), sp["seq"](0), sp["seq"](FOX_HEADS), sp["row_all"],
                  sp["col"], sp["col"], sp["rows"](0), _ANY],
        out_specs=(sp["rows"](0), sp["col"]),
        input_output_aliases={7: 0},
        scratch_shapes=[pltpu.VMEM((tq, HEAD_DIM), F32), pltpu.VMEM((tq, 1), F32)],
        compiler_params=_params("parallel", "parallel"),
    )(proj, kv, kv, fk, lse, delta, datt, dproj)


def _fox_bwd_dkv(proj, kv, fk, lse, delta, datt):
    L = proj.shape[0]
    tq = min(FOX_BLOCK, L)
    nq = L // tq
    sp = _fox_specs(tq, L)

    def body(q_ref, k_ref, v_ref, fk_ref, lse_ref, dl_ref, do_ref,
             dk_ref, dv_ref, df_ref, dk_s, dv_s, df_s):
        ki = pl.program_id(1)
        k, v, fk = k_ref[...], v_ref[...], fk_ref[...]
        dk_s[...] = jnp.zeros_like(dk_s)
        dv_s[...] = jnp.zeros_like(dv_s)
        df_s[...] = jnp.zeros_like(df_s)

        def block(i, diagonal):
            r0 = pl.multiple_of(i * tq, tq)
            qs = (q_ref[pl.ds(r0, tq), :] * (HEAD_DIM ** -0.5)).astype(BF16)
            dob = do_ref[pl.ds(r0, tq), :].astype(BF16)
            p = jnp.exp(_fox_scores(qs, k, fk, diagonal) - lse_ref[i])
            dp = lax.dot_general(dob, v, _NT, preferred_element_type=F32)
            ds = p * (dp - dl_ref[i])
            dv_s[...] += lax.dot_general(p.astype(BF16), dob, _TN, preferred_element_type=F32)
            dk_s[...] += lax.dot_general(ds.astype(BF16), qs, _TN, preferred_element_type=F32)
            df_s[...] -= jnp.sum(ds, axis=0, keepdims=True)

        def above(i, carry):
            block(i, False)
            return carry

        block(ki, True)
        lax.fori_loop(ki + 1, nq, above, 0)
        dk_ref[...] = dk_s[...].astype(BF16)
        dv_ref[...] = dv_s[...].astype(BF16)
        df_ref[...] = df_s[...]

    return pl.pallas_call(
        body, name="fox_bwd_dkv",
        out_shape=(jax.ShapeDtypeStruct((L, MAIN_WIDTH), BF16),
                   jax.ShapeDtypeStruct((L, MAIN_WIDTH), BF16),
                   jax.ShapeDtypeStruct((FOX_HEADS, nq, 1, tq), F32)),
        grid=(FOX_HEADS, nq),
        in_specs=[sp["seq"](0), sp["rows"](0), sp["rows"](FOX_HEADS), sp["row"],
                  sp["col_all"], sp["col_all"], sp["seq"](0)],
        out_specs=(sp["rows"](0), sp["rows"](0), sp["row"]),
        scratch_shapes=[pltpu.VMEM((tq, HEAD_DIM), F32), pltpu.VMEM((tq, HEAD_DIM), F32),
                        pltpu.VMEM((1, tq), F32)],
        compiler_params=_params("parallel", "parallel"),
    )(proj, kv, kv, fk, lse, delta, datt)


def _pad_lanes(a):
    return jnp.pad(a, ((0, 0), (0, LANES - a.shape[1])))


def _mem_branch_fwd(memn, w_mk, proj, tag):
    kvm = _mm(memn, w_mk, name="mem_kv_" + tag)
    return kvm, _mem_attn_fwd(proj, kvm)


def _mem_branch_bwd(mem, g, w_mk, proj, memn, kvm, do_mem, dproj, tag):
    dproj, dkvm = _mem_attn_bwd(proj, kvm, do_mem, dproj)
    dkvm = dkvm.astype(BF16)
    dw_mk = _mm(memn, dkvm, ta=True, name="dw_mem_kv_" + tag, out_dtype=BF16)
    dmemn = _mm(dkvm, w_mk, tb=True, name="dmemn_" + tag)
    _, dg = _rmsnorm_bwd(mem, g, dmemn, name="mem_norm_bwd_" + tag, dx_dtype=BF16)
    return dproj, dw_mk, dg


def _local_step(x, mem, target, w, fetch=None, grads_ready=None):
    if grads_ready is None:
        grads_ready = lambda group, grads, token: token
    L = x.shape[0]
    g = {}
    w = dict(w)

    b_re_t = jnp.transpose(w["b_re"], (0, 2, 1))
    b_im_t = jnp.transpose(w["b_im"], (0, 2, 1))
    ar, ai, bbr_t, bbi_t = _s5_prep(w["lam_re"], w["lam_im"], w["log_step"], b_re_t, b_im_t)
    bmat, cmat = _s5_block_mats(bbr_t, bbi_t, w["c_re"], w["c_im"])
    a_rows = _s5_a_rows(ar, ai)

    hn0 = _rmsnorm_fwd(x, w["pre_norm_g"][0], name="pre_norm_0", out_dtype=BF16)
    memn0 = _rmsnorm_fwd(mem, w["mem_norm_g"][0], name="mem_norm_0", out_dtype=BF16)
    memn1 = _rmsnorm_fwd(mem, w["mem_norm_g"][1], name="mem_norm_1", out_dtype=BF16)
    if fetch is not None:
        w.update(fetch("a", [hn0, memn0, memn1, bmat, cmat, a_rows]))
    proj_a = _mm(hn0, w["w_in_a"], name="in_proj_a")
    y, yg, xp = _s5_fwd(proj_a, bmat, cmat, a_rows, w["d_skip"])
    if fetch is not None:
        w.update(fetch("b", yg))
    t = _mm(yg, w["w_glu"], name="glu_proj")
    kvm0, om0 = _mem_branch_fwd(memn0, w["w_mem_kv"][0], proj_a, "0")
    cat0 = _gate_a_fwd(y, t, w["b_glu"], proj_a, om0)
    o0 = _mm(cat0, w["w_out"][0], name="out_proj_0")
    h1 = _rmsnorm_fwd(o0, w["post_norm_g"][0], res=x, name="post_norm_0")

    kv_in = _rmsnorm_fwd(h1, w["kv_norm_g"], name="kv_norm", out_dtype=BF16)
    if fetch is not None:
        w.update(fetch("c", kv_in))
    kv = _mm(kv_in, w["w_kv"], name="kv_proj", out_dtype=BF16)
    pre_f = _mm(kv_in, w["w_fgate"], name="fgate_proj")
    b_f = jnp.pad(w["b_fgate"], (0, LANES - FOX_HEADS))
    fcum = _fgate_fwd(pre_f, b_f)
    fc = jnp.transpose(fcum[:, :FOX_HEADS])
    tq = min(FOX_BLOCK, L)
    fk = fc.reshape(FOX_HEADS, L // tq, 1, tq)

    hn1 = _rmsnorm_fwd(h1, w["pre_norm_g"][1], name="pre_norm_1", out_dtype=BF16)
    proj_b = _mm(hn1, w["w_in_b"], name="in_proj_b")
    att, lse = _fox_fwd(proj_b, kv, fk)
    kvm1, om1 = _mem_branch_fwd(memn1, w["w_mem_kv"][1], proj_b, "1")
    cat1 = _gate_b_fwd(att, proj_b, om1)
    o1 = _mm(cat1, w["w_out"][1], name="out_proj_1")
    dh2, loss_row = _final_norm_loss(o1, w["post_norm_g"][1], h1, target)

    do1, dpost1 = _rmsnorm_bwd(o1, w["post_norm_g"][1], dh2, name="post_norm_bwd_1", dx_dtype=BF16)
    dcat1 = _mm(do1, w["w_out"][1], tb=True, name="dcat_1", out_dtype=BF16)
    g["w_out_1"] = _mm(cat1, do1, ta=True, name="dw_out_1", out_dtype=BF16)
    datt, dproj_b, dom1, delta = _gate_b_bwd(dcat1, att, proj_b, om1)
    dproj_b, g["w_mem_kv_1"], dmemg1 = _mem_branch_bwd(mem, w["mem_norm_g"][1], w["w_mem_kv"][1], proj_b,
                                                      memn1, kvm1, dom1, dproj_b, "1")
    delta = delta.reshape(lse.shape)
    dproj_b, dfq = _fox_bwd_dq(proj_b, kv, fk, lse, delta, datt, dproj_b)
    dk, dv, dfk = _fox_bwd_dkv(proj_b, kv, fk, lse, delta, datt)
    g["w_in_b"] = _mm(hn1, dproj_b, ta=True, name="dw_in_b", out_dtype=BF16, shards=N_CHIPS)
    dhn1 = _mm(dproj_b, w["w_in_b"], tb=True, name="dhn_1")

    dkv = jnp.concatenate([dk, dv], axis=1)
    g["w_kv"] = _mm(kv_in, dkv, ta=True, name="dw_kv", out_dtype=BF16, shards=N_CHIPS)
    dkv_in_a = _mm(dkv, w["w_kv"], tb=True, name="dkv_in_kv")
    dfcum = _pad_lanes(jnp.transpose(dfq.reshape(FOX_HEADS, L) + dfk.reshape(FOX_HEADS, L)))
    dpre_f, db_f = _fgate_bwd(dfcum, pre_f, b_f)
    g["b_fgate"] = db_f[0, :FOX_HEADS]
    g["w_fgate"] = _mm(kv_in, dpre_f, ta=True, name="dw_fgate")[:, :FOX_HEADS]
    dkv_in_b = _mm(dpre_f, w["w_fgate"], tb=True, name="dkv_in_fgate")
    dh1, g["kv_norm_g"], dpre1 = _rmsnorm_bwd_pair(h1, w["kv_norm_g"], (dkv_in_a, dkv_in_b), w["pre_norm_g"][1],
                                                   dhn1, adds=(dh2,), name="kv_pre_norm_bwd")
    dh1 = grads_ready("b", g, dh1)

    do0, dpost0 = _rmsnorm_bwd(o0, w["post_norm_g"][0], dh1, name="post_norm_bwd_0", dx_dtype=BF16)
    dcat0 = _mm(do0, w["w_out"][0], tb=True, name="dcat_0", out_dtype=BF16)
    g["w_out_0"] = _mm(cat0, do0, ta=True, name="dw_out_0", out_dtype=BF16)
    dcat0 = grads_ready("b_send", g, dcat0)
    dproj_a, dt, dyg_a, dom0, db_glu = _gate_a_bwd(dcat0, y, t, w["b_glu"], proj_a, om0)
    g["b_glu"] = db_glu[0]
    g["w_glu"] = _mm(yg, dt, ta=True, name="dw_glu", out_dtype=BF16)
    dyg_b = _mm(dt, w["w_glu"], tb=True, name="dyg")
    dproj_a, g["w_mem_kv_0"], dmemg0 = _mem_branch_bwd(mem, w["mem_norm_g"][0], w["w_mem_kv"][0], proj_a,
                                                      memn0, kvm0, dom0, dproj_a, "0")
    dyg_b = grads_ready("a1", g, dyg_b)
    dproj_a, db_blk, dc_blk, da_rows, dd_skip = _s5_bwd(proj_a, dyg_a, dyg_b, y, xp, bmat, cmat, a_rows,
                                                        w["d_skip"], dproj_a)
    dproj_a = grads_ready("a1_send", g, dproj_a)
    g["d_skip"] = dd_skip[0]
    g["w_in_a"] = _mm(hn0, dproj_a, ta=True, name="dw_in_a", out_dtype=BF16, shards=N_CHIPS)
    dproj_a = grads_ready("a2", g, dproj_a)
    dhn0 = _mm(dproj_a, w["w_in_a"], tb=True, name="dhn_0")
    grad_x, dpre0 = _rmsnorm_bwd(x, w["pre_norm_g"][0], dhn0, adds=(dh1,), name="pre_norm_bwd_0")

    dbb = _s5_unfold(db_blk)
    dcc = _s5_unfold(dc_blk)
    g["c_re"], g["c_im"] = dcc[0], -dcc[1]
    d_ar = da_rows[:, 0, :STATE_COLS].reshape(SSM_GROUPS, SSM_STATE)
    d_ai = da_rows[:, 0, STATE_COLS:].reshape(SSM_GROUPS, SSM_STATE)
    dlr, dli, dls, dbr_t, dbi_t = _s5_prep_bwd(w["lam_re"], w["lam_im"], w["log_step"], b_re_t, b_im_t,
                                               d_ar, d_ai, dbb[0], dbb[1])
    g["lam_re"], g["lam_im"], g["log_step"] = dlr, dli, dls[:, 0]
    g["b_re"] = jnp.transpose(dbr_t, (0, 2, 1))
    g["b_im"] = jnp.transpose(dbi_t, (0, 2, 1))
    g["pre_norm_g"] = jnp.stack([dpre0, dpre1])
    g["post_norm_g"] = jnp.stack([dpost0, dpost1])
    g["mem_norm_g"] = jnp.stack([dmemg0, dmemg1])
    return loss_row, grad_x, g


_MESH = pl.DeviceIdType.MESH
_ANY = pl.BlockSpec(memory_space=pl.ANY)


def _place():
    x, y, c = lax.axis_index("x"), lax.axis_index("y"), lax.axis_index("c")
    chips = [(1 - x, y), (x, 1 - y), (1 - x, 1 - y)]
    return x, y, c, chips


_HBM = pl.BlockSpec(memory_space=pltpu.HBM)
_SEM = pl.BlockSpec(memory_space=pltpu.SEMAPHORE)
_SIDE = pltpu.SideEffectType.DATAFLOW_SIDE_EFFECTING


def _in_hbm(a):
    return pltpu.with_memory_space_constraint(a, pltpu.HBM)


def _hbm_like(a):
    return pltpu.HBM(a.shape, a.dtype)


def _ici_copies(srcs, lands, send_sem, recv_sem, src_at, dst_at, wait_at, to_sibling=False):
    x, y, c, chips = _place()
    peers = [(x, y, 1 - c)] if to_sibling else [(cx, cy, c) for cx, cy in chips]
    m = len(peers)
    start, wait = [], []
    for i in range(len(srcs)):
        for k, (px, py, pc) in enumerate(peers):
            sem = dict(send_sem=send_sem.at[m * i + k], recv_sem=recv_sem.at[m * i + k],
                       device_id=(px, py, pc), device_id_type=_MESH)
            src = src_at(srcs[i], 2 * px + py, c)
            start.append(pltpu.make_async_remote_copy(src_ref=src, dst_ref=dst_at(lands[i], 2 * x + y, k, c), **sem))
            wait.append(pltpu.make_async_remote_copy(src_ref=src, dst_ref=wait_at(lands[i], 2 * px + py, k, c), **sem))
    return start, wait


def _route_peers(route):
    return 1 if len(route) == 4 else 3


_BLOCK_ROUTE = (lambda s, j, c: s, lambda l, me, k, c: l.at[me, c], lambda l, j, k, c: l.at[j, c])


def _ici_start(srcs, lands, token, route, *, name):
    n = len(srcs)

    def body(*refs):
        start, _ = _ici_copies(refs[:n], refs[n:2 * n], refs[2 * n + 1], refs[2 * n + 2], *route)
        for cp in start:
            cp.start()

    sems = pltpu.SemaphoreType.DMA((_route_peers(route) * n,))
    outs = pl.pallas_call(
        body, name=name,
        out_shape=(sems, sems, *[_hbm_like(a) for a in srcs], *[_hbm_like(a) for a in lands], _hbm_like(token)),
        in_specs=[_HBM] * (2 * n + 1), out_specs=(_SEM, _SEM, *[_HBM] * (2 * n + 1)),
        input_output_aliases={i: 2 + i for i in range(2 * n + 1)},
        compiler_params=pltpu.CompilerParams(has_side_effects=_SIDE),
    )(*[_in_hbm(a) for a in srcs], *[_in_hbm(a) for a in lands], _in_hbm(token))
    return (outs[0], outs[1], list(outs[2:2 + n]), list(outs[2 + n:2 + 2 * n])), outs[2 + 2 * n]


def _ici_wait(handle, after, route, *, name):
    send_sem, recv_sem, srcs, lands = handle
    n = len(srcs)
    after = list(after) if isinstance(after, (list, tuple)) else [after]

    def body(*refs):
        _, wait = _ici_copies(refs[:n], refs[n:2 * n], refs[2 * n], refs[2 * n + 1], *route)
        for cp in wait:
            cp.wait_send()
            cp.wait_recv()

    outs = pl.pallas_call(
        body, name=name,
        out_shape=(*[_hbm_like(a) for a in srcs], *[_hbm_like(a) for a in lands]),
        in_specs=[_HBM] * (2 * n) + [_SEM, _SEM] + [_ANY] * len(after), out_specs=tuple([_HBM] * (2 * n)),
        input_output_aliases={i: i for i in range(2 * n)},
        compiler_params=pltpu.CompilerParams(has_side_effects=_SIDE),
    )(*srcs, *lands, send_sem, recv_sem, *after)
    return list(outs[:n]), list(outs[n:])


_GATHER_ROUTE = (lambda s, j, c: s.at[c], lambda l, me, k, c: l.at[me, c], lambda l, j, k, c: l.at[j, c])
_SCATTER_ROUTE = (lambda s, j, c: s.at[j], lambda l, me, k, c: l.at[k], lambda l, j, k, c: l.at[k])
_SHARE_ROUTE = (lambda s, j, c: s, lambda l, me, k, c: l.at[c], lambda l, j, k, c: l.at[1 - c], True)
_SWAP_ROUTE = (lambda s, j, c: s.at[:, 1 - c], lambda l, me, k, c: l, lambda l, j, k, c: l, True)


def _gather_forward(lands, tag, own=False):
    n = len(lands)
    m = 4 if own else 3

    def body(*refs):
        ins, outs = refs[:n], refs[n:2 * n]
        send_sem, recv_sem = refs[2 * n:]
        x, y, c, chips = _place()
        slots = [2 * cx + cy for cx, cy in chips] + [2 * x + y]

        def copy(i, k, half):
            return pltpu.make_async_remote_copy(
                src_ref=ins[i].at[slots[k], half], dst_ref=outs[i].at[slots[k], half],
                send_sem=send_sem.at[m * i + k], recv_sem=recv_sem.at[m * i + k],
                device_id=(x, y, 1 - c), device_id_type=_MESH)

        copies = [copy(i, k, c) for i in range(n) for k in range(m)]
        for cp in copies:
            cp.start()
        for i in range(n):
            for k in range(m):
                copy(i, k, 1 - c).wait_recv()
        for cp in copies:
            cp.wait_send()

    return pl.pallas_call(
        body, name="gather_forward_to_sibling_" + tag,
        out_shape=[jax.ShapeDtypeStruct(a.shape, a.dtype) for a in lands],
        in_specs=[_ANY] * n, out_specs=[_ANY] * n,
        input_output_aliases={i: i for i in range(n)},
        scratch_shapes=[pltpu.SemaphoreType.DMA((m * n,)), pltpu.SemaphoreType.DMA((m * n,))],
    )(*lands)


def _swap_halves(grads, tag):
    n = len(grads)

    def body(*refs):
        ins, outs = refs[:n], refs[n:2 * n]
        send_sem, recv_sem = refs[2 * n:]
        x, y, c, _ = _place()
        copies = [pltpu.make_async_remote_copy(
            src_ref=ins[i].at[:, 1 - c], dst_ref=outs[i],
            send_sem=send_sem.at[i], recv_sem=recv_sem.at[i],
            device_id=(x, y, 1 - c), device_id_type=_MESH) for i in range(n)]
        for cp in copies:
            cp.start()
        for cp in copies:
            cp.wait()

    return pl.pallas_call(
        body, name="grad_swap_halves_" + tag,
        out_shape=[jax.ShapeDtypeStruct((N_CHIPS,) + g.shape[2:], g.dtype) for g in grads],
        in_specs=[_ANY] * n, out_specs=[_ANY] * n,
        scratch_shapes=[pltpu.SemaphoreType.DMA((n,)), pltpu.SemaphoreType.DMA((n,))],
    )(*grads)


def _sum_rows(h, C):
    return max(d for d in range(SUBLANES, h + 1, SUBLANES) if h % d == 0 and d * C <= 1 << 20)


SUM_STEPS = 4


def _pair_sums(gs, rs, c_idx, *, name):
    n = len(gs)
    rows = [g.shape[2] // SUM_STEPS for g in gs]

    def body(c_ref, *refs):
        for g_ref, r_ref, o_ref in zip(refs[:n], refs[n:2 * n], refs[2 * n:]):
            o_ref[...] = (g_ref[...].astype(F32) + r_ref[...].astype(F32)).astype(o_ref.dtype)

    return pl.pallas_call(
        body, name=name,
        out_shape=[jax.ShapeDtypeStruct((N_CHIPS,) + g.shape[2:], g.dtype) for g in gs],
        grid_spec=pltpu.PrefetchScalarGridSpec(
            num_scalar_prefetch=1, grid=(N_CHIPS, SUM_STEPS),
            in_specs=[pl.BlockSpec((None, None, tr, g.shape[3]), lambda j, i, s: (j, s[0], i, 0))
                      for g, tr in zip(gs, rows)]
            + [pl.BlockSpec((None, tr, g.shape[3]), lambda j, i, s: (j, i, 0)) for g, tr in zip(gs, rows)],
            out_specs=[pl.BlockSpec((None, tr, g.shape[3]), lambda j, i, s: (j, i, 0)) for g, tr in zip(gs, rows)]),
        compiler_params=_params("parallel", "parallel"),
    )(c_idx, *gs, *rs)


def _owner_sums(ss, rs, jc_idx, *, name):
    n = len(ss)
    rows = [s.shape[1] // SUM_STEPS for s in ss]

    def body(jc_ref, *refs):
        for s_ref, r_ref, m_ref, o_ref in zip(refs[:n], refs[n:2 * n], refs[2 * n:3 * n], refs[3 * n:]):
            acc = s_ref[...].astype(F32)
            for k in range(3):
                acc = acc + r_ref[k].astype(F32)
            m_ref[...] = acc
            o_ref[...] = acc

    outs = pl.pallas_call(
        body, name=name,
        out_shape=[jax.ShapeDtypeStruct(s.shape[1:], F32) for s in ss]
        + [jax.ShapeDtypeStruct((2,) + s.shape[1:], F32) for s in ss],
        grid_spec=pltpu.PrefetchScalarGridSpec(
            num_scalar_prefetch=1, grid=(SUM_STEPS,),
            in_specs=[pl.BlockSpec((None, tr, s.shape[2]), lambda i, p: (p[0], i, 0)) for s, tr in zip(ss, rows)]
            + [pl.BlockSpec((3, tr, s.shape[2]), lambda i, p: (0, i, 0)) for s, tr in zip(ss, rows)],
            out_specs=[pl.BlockSpec((tr, s.shape[2]), lambda i, p: (i, 0)) for s, tr in zip(ss, rows)]
            + [pl.BlockSpec((None, tr, s.shape[2]), lambda i, p: (p[1], i, 0)) for s, tr in zip(ss, rows)]),
        compiler_params=_params("parallel"),
    )(jc_idx, *ss, *rs)
    return outs[:n], outs[n:]


def _chip_sums(grads, c_idx, tag):
    views = [g.reshape(N_CHIPS, 2, g.shape[1] // 2, g.shape[2]) for g in grads]
    arrived = _swap_halves(views, tag)
    return _pair_sums(views, arrived, c_idx, name=f"grad_pair_sums_{tag}")


def _sum_devices(blocks):
    R = blocks.shape[2]
    tr = _sum_rows(R, 2 * N_CHIPS * LANES)

    def body(b_ref, o_ref):
        acc = b_ref[0, 0]
        for d in range(1, 2 * N_CHIPS):
            acc = acc + b_ref[d // 2, d % 2]
        o_ref[...] = acc

    return pl.pallas_call(
        body, name="sum_small_over_devices", out_shape=jax.ShapeDtypeStruct((R, LANES), F32),
        grid=(R // tr,),
        in_specs=[pl.BlockSpec((N_CHIPS, 2, tr, LANES), lambda i: (0, 0, i, 0))],
        out_specs=pl.BlockSpec((tr, LANES), lambda i: (i, 0)),
        compiler_params=_params("parallel"),
    )(blocks)


def _adamw(w, g, m, v, *, name):
    R, C = w.shape
    whole_fits = 7 * 2 * R * C * 4 <= VMEM_LIMIT_BYTES // 2
    tr = R if whole_fits else next(c for c in (256, 192, 128, 64, 32, 16, 8) if R % c == 0)

    def body(w_ref, g_ref, m_ref, v_ref, d_ref, nm_ref, nv_ref):
        g = g_ref[...]
        m = ADAM_B1 * m_ref[...] + (1.0 - ADAM_B1) * g
        v = ADAM_B2 * v_ref[...] + (1.0 - ADAM_B2) * (g * g)
        nm_ref[...] = m
        nv_ref[...] = v
        m_hat = m / (1.0 - ADAM_B1 ** ADAM_STEP)
        v_hat = v / (1.0 - ADAM_B2 ** ADAM_STEP)
        d_ref[...] = -ADAM_LR * (m_hat / (jnp.sqrt(v_hat) + ADAM_EPS) + ADAM_WD * w_ref[...])

    blk = pl.BlockSpec((tr, C), lambda i: (i, 0))
    sds = jax.ShapeDtypeStruct((R, C), F32)
    return pl.pallas_call(
        body, name=name, out_shape=(sds, sds, sds), grid=(R // tr,),
        in_specs=[blk] * 4, out_specs=(blk, blk, blk),
        compiler_params=_params("parallel"),
    )(w, g, m, v)


_TILE = SUBLANES * LANES


def _pack(arrays):
    rows = []
    for a in arrays:
        flat = a.reshape(-1)
        flat = jnp.pad(flat, (0, (-flat.shape[0]) % _TILE))
        rows.append(flat.reshape(-1, LANES))
    return jnp.concatenate(rows, axis=0)


def _unpack(buf, shapes):
    out, r = [], 0
    for s in shapes:
        size = math.prod(s)
        nr = -(-size // _TILE) * SUBLANES
        out.append(buf[r:r + nr].reshape(-1)[:size].reshape(s))
        r += nr
    return out


_BIG = ("w_in_a", "w_glu", "w_kv", "w_in_b", "w_mem_kv", "w_out")
_REPLICATED = ("pre_norm_g", "post_norm_g", "lam_re", "lam_im", "log_step", "b_re", "b_im", "c_re", "c_im",
               "kv_norm_g", "b_fgate", "mem_norm_g")
_SHARDED_SMALL = ("d_skip", "b_glu", "w_fgate")
_WEIGHTS = ("pre_norm_g", "post_norm_g", "w_in_a", "lam_re", "lam_im", "log_step", "b_re", "b_im", "c_re",
            "c_im", "d_skip", "w_glu", "b_glu", "kv_norm_g", "w_kv", "w_fgate", "b_fgate", "w_in_b",
            "mem_norm_g", "w_mem_kv", "w_out")


def _halves(a):
    return a.reshape(2, a.shape[0] // 2, a.shape[1])


def _unhalve(a):
    return a.reshape(N_CHIPS, 2 * a.shape[2], a.shape[3])


def _columns(a):
    return jnp.transpose(a, (1, 0, 2)).reshape(a.shape[1], N_CHIPS * a.shape[2])


def kernel(x, mem, pre_norm_g, post_norm_g, w_in_a, lam_re, lam_im, log_step, b_re, b_im, c_re, c_im, d_skip, w_glu, b_glu, kv_norm_g, w_kv, w_fgate, b_fgate, w_in_b, mem_norm_g, w_mem_kv, w_out, loss_target, m_pre_norm_g, m_post_norm_g, m_w_in_a, m_lam_re, m_lam_im, m_log_step, m_b_re, m_b_im, m_c_re, m_c_im, m_d_skip, m_w_glu, m_b_glu, m_kv_norm_g, m_w_kv, m_w_fgate, m_b_fgate, m_w_in_b, m_mem_norm_g, m_w_mem_kv, m_w_out, v_pre_norm_g, v_post_norm_g, v_w_in_a, v_lam_re, v_lam_im, v_log_step, v_b_re, v_b_im, v_c_re, v_c_im, v_d_skip, v_w_glu, v_b_glu, v_kv_norm_g, v_w_kv, v_w_fgate, v_b_fgate, v_w_in_b, v_mem_norm_g, v_w_mem_kv, v_w_out):
    a = dict(locals())
    xi, yi, ci = lax.axis_index("x"), lax.axis_index("y"), lax.axis_index("c")
    chip = 2 * xi + yi
    c_idx = jnp.reshape(ci, (1,)).astype(jnp.int32)
    jc_idx = jnp.stack([chip, ci]).astype(jnp.int32)

    vec = jnp.zeros((2 * SUBLANES, MAIN_WIDTH // N_CHIPS), F32)
    vec = vec.at[0].set(a["d_skip"][0]).at[1].set(a["b_glu"][0])
    def own_slot(gathered, parts):
        return [lax.dynamic_update_index_in_dim(g, p, chip, 0) for g, p in zip(gathered, parts)]

    parts_a = [_halves(a["w_in_a"][0].astype(BF16)), _halves(vec)]
    parts_b = [_halves(a["w_glu"][0].astype(BF16)),
               *[_halves(a["w_mem_kv"][i].astype(BF16)) for i in range(2)],
               *[_halves(a["w_out"][i].astype(BF16)) for i in range(2)]]
    parts_c = [_halves(a["w_kv"].astype(BF16)), _halves(_pad_lanes(a["w_fgate"]).astype(BF16)),
               _halves(a["w_in_b"][0].astype(BF16))]
    travelling, token = {}, a["pre_norm_g"]
    for tag, parts in (("a", parts_a), ("b", parts_b), ("c", parts_c)):
        lands = [lax.empty((N_CHIPS,) + p.shape, p.dtype) for p in parts]
        travelling[tag], token = _ici_start(parts, lands, token, _GATHER_ROUTE, name=f"gather_{tag}_start")

    def fetch(tag, after):
        parts, lands = _ici_wait(travelling[tag], after, _GATHER_ROUTE, name=f"gather_{tag}_wait")
        full = own_slot(_gather_forward(lands, tag), parts)
        if tag == "a":
            w_in_a, vecs = full
            return dict(w_in_a=_columns(_unhalve(w_in_a)), d_skip=vecs[:, 0, 0, :].reshape(MAIN_WIDTH),
                        b_glu=vecs[:, 0, 1, :].reshape(MAIN_WIDTH))
        if tag == "b":
            w_glu, w_mk0, w_mk1, w_out0, w_out1 = full
            return dict(w_glu=w_glu.reshape(MAIN_WIDTH, MAIN_WIDTH),
                        w_mem_kv=[m.reshape(D_MODEL, 2 * MEM_WIDTH) for m in (w_mk0, w_mk1)],
                        w_out=[o.reshape(D_MODEL, D_MODEL) for o in (w_out0, w_out1)])
        w_kv, w_fg, w_in_b = full
        return dict(w_kv=_columns(_unhalve(w_kv)), w_fgate=w_fg.reshape(D_MODEL, LANES),
                    w_in_b=_columns(_unhalve(w_in_b)))

    w = dict(
        pre_norm_g=token, post_norm_g=a["post_norm_g"], mem_norm_g=a["mem_norm_g"],
        kv_norm_g=a["kv_norm_g"], b_fgate=a["b_fgate"],
        lam_re=a["lam_re"][0], lam_im=a["lam_im"][0], log_step=a["log_step"][0],
        b_re=a["b_re"][0], b_im=a["b_im"][0], c_re=a["c_re"][0], c_im=a["c_im"][0])

    sent = {}

    swapping = {}

    def grads_ready(event, g, token):
        tag = event.split("_")[0]
        if event in ("b", "a1"):
            big = {"b": lambda: [g["w_kv"], g["w_in_b"], g["w_mem_kv_1"].reshape(N_CHIPS, -1, 2 * MEM_WIDTH),
                                 g["w_out_1"].reshape(N_CHIPS, -1, D_MODEL)],
                   "a1": lambda: [g["w_glu"].reshape(N_CHIPS, -1, MAIN_WIDTH),
                                  g["w_mem_kv_0"].reshape(N_CHIPS, -1, 2 * MEM_WIDTH),
                                  g["w_out_0"].reshape(N_CHIPS, -1, D_MODEL)]}[tag]()
            views = [b.reshape(N_CHIPS, 2, b.shape[1] // 2, b.shape[2]) for b in big]
            lands = [lax.empty((N_CHIPS,) + v.shape[2:], v.dtype) for v in views]
            swapping[tag], token = _ici_start(views, lands, token, _SWAP_ROUTE, name=f"grad_swap_{tag}_start")
            return token
        if event == "a2":
            sums = _chip_sums([g["w_in_a"]], c_idx, tag)
        else:
            views, arrived = _ici_wait(swapping[tag], token, _SWAP_ROUTE, name=f"grad_swap_{tag}_wait")
            sums = _pair_sums(views, arrived, c_idx, name=f"grad_pair_sums_{tag}")
        lands = [lax.empty((3,) + s.shape[1:], s.dtype) for s in sums]
        sent[tag], token = _ici_start(sums, lands, token, _SCATTER_ROUTE, name=f"grad_send_{tag}_start")
        return token

    loss_row, grad_x, g = _local_step(a["x"][0], a["mem"][0], a["loss_target"][0], w, fetch, grads_ready)

    small_names = _REPLICATED + _SHARDED_SMALL
    pack = _pack([g[n] for n in small_names])
    blocks = lax.empty((N_CHIPS, 2) + pack.shape, F32)
    small_sent, token = _ici_start([pack], [blocks], loss_row, _BLOCK_ROUTE, name="small_sums_start")

    sharing = {}
    for tag in ("b", "a1", "a2"):
        sums, arrived = _ici_wait(sent[tag], [grad_x, token], _SCATTER_ROUTE, name=f"grad_send_{tag}_wait")
        mine, bufs = _owner_sums(sums, arrived, jc_idx, name=f"grad_owner_sums_{tag}")
        sharing[tag], token = _ici_start(mine, bufs, token, _SHARE_ROUTE, name=f"grad_share_{tag}_start")
    loss = lax.psum(jnp.sum(token), MESH_AXES)

    def shared(tag, after):
        _, bufs = _ici_wait(sharing[tag], after, _SHARE_ROUTE, name=f"grad_share_{tag}_wait")
        return [b.reshape(-1, b.shape[2]) for b in bufs]

    grads, delta, new_m, new_v = {}, {}, {}, {}

    def adam(n):
        shape = a[n].shape
        d2 = (-1, shape[-1])
        d, m, v = _adamw(a[n].reshape(d2), grads[n].reshape(d2), a["m_" + n].reshape(d2),
                         a["v_" + n].reshape(d2), name="adamw_" + n)
        delta[n], new_m[n], new_v[n] = d.reshape(shape), m.reshape(shape), v.reshape(shape)
        return d

    r_kv, r_in_b, r_mk1, r_out1 = shared("b", token)
    grads["w_kv"], grads["w_in_b"] = r_kv, r_in_b[None]
    done = [adam("w_kv"), adam("w_in_b")]
    r_glu, r_mk0, r_out0 = shared("a1", done)
    grads["w_glu"], grads["w_mem_kv"], grads["w_out"] = r_glu[None], jnp.stack([r_mk0, r_mk1]), jnp.stack([r_out0, r_out1])
    done = [adam("w_glu"), adam("w_mem_kv"), adam("w_out")]
    (r_in_a,) = shared("a2", done)
    grads["w_in_a"] = r_in_a[None]
    adam("w_in_a")

    (pack,), (blocks,) = _ici_wait(small_sent, [delta[n] for n in _BIG], _BLOCK_ROUTE, name="small_sums_wait")
    blocks = lax.dynamic_update_slice(blocks, pack[None, None], (chip, ci, 0, 0))
    (blocks,) = _gather_forward([blocks], "small", own=True)
    small = dict(zip(small_names, _unpack(_sum_devices(blocks), [g[n].shape for n in small_names])))
    for n in _REPLICATED:
        grads[n] = small[n].reshape(a[n].shape)
    nd = MAIN_WIDTH // N_CHIPS
    grads["d_skip"] = lax.dynamic_slice(small["d_skip"], (chip * nd,), (nd,))[None]
    grads["b_glu"] = lax.dynamic_slice(small["b_glu"], (chip * nd,), (nd,))[None]
    nf = D_MODEL // N_CHIPS
    grads["w_fgate"] = lax.dynamic_slice(small["w_fgate"], (chip * nf, 0), (nf, FOX_HEADS))

    shapes = [a[n].shape for n in small_names]
    d, m, v = _adamw(_pack([a[n] for n in small_names]), _pack([grads[n] for n in small_names]),
                     _pack([a["m_" + n] for n in small_names]), _pack([a["v_" + n] for n in small_names]),
                     name="adamw_small")
    for n, dd, mm, vv in zip(small_names, _unpack(d, shapes), _unpack(m, shapes), _unpack(v, shapes)):
        delta[n], new_m[n], new_v[n] = dd, mm, vv

    return (loss, grad_x[None], *[grads[n] for n in _WEIGHTS], *[delta[n] for n in _WEIGHTS],
            *[new_m[n] for n in _WEIGHTS], *[new_v[n] for n in _WEIGHTS])
```

```python
import math

import jax
import jax.numpy as jnp
from jax import lax
from jax.experimental import pallas as pl
from jax.experimental.pallas import tpu as pltpu

F32 = jnp.float32
BF16 = jnp.bfloat16

D_MODEL = 2048
N_MEM = 256
MAIN_WIDTH = 1536
MEM_WIDTH = 512
IN_WIDTH = 2 * MAIN_WIDTH + 2 * MEM_WIDTH
HEAD_DIM = 128
FOX_HEADS = MAIN_WIDTH // HEAD_DIM
MEM_HEADS = MEM_WIDTH // HEAD_DIM
SSM_GROUP = 16
SSM_GROUPS = MAIN_WIDTH // SSM_GROUP
SSM_STATE = 64
GROUPS_PER_BLOCK = 8
SSM_BLOCKS = SSM_GROUPS // GROUPS_PER_BLOCK
STATE_COLS = GROUPS_PER_BLOCK * SSM_STATE
EPS = 1e-6
ADAM_LR = 0.001
ADAM_B1 = 0.9
ADAM_B2 = 0.999
ADAM_EPS = 1e-08
ADAM_WD = 0.01
ADAM_STEP = 10
N_CHIPS = 4
LANES = 128
SUBLANES = 8
VMEM_LIMIT_BYTES = 56 * 1024 * 1024
NEG_BIG = -1e30
MESH_AXES = ("x", "y", "c")


def _params(*sem):
    return pltpu.CompilerParams(dimension_semantics=sem if sem else None,
                                vmem_limit_bytes=VMEM_LIMIT_BYTES)


def _sigmoid(x):
    return 1.0 / (1.0 + jnp.exp(-x))


def _gelu(x):
    c = math.sqrt(2.0 / math.pi)
    return 0.5 * x * (1.0 + jnp.tanh(c * (x + 0.044715 * (x * x * x))))


def _gelu_grad(x):
    c = math.sqrt(2.0 / math.pi)
    t = jnp.tanh(c * (x + 0.044715 * (x * x * x)))
    return 0.5 * (1.0 + t) + 0.5 * x * (1.0 - t * t) * (c * (1.0 + 3.0 * 0.044715 * (x * x)))


def _silu_and_grad(z):
    s = _sigmoid(z)
    return z * s, s * (1.0 + z * (1.0 - s))


_TILE_CHOICES = (4096, 3072, 2048, 1536, 1024, 768, 512, 384, 256, LANES)


def _tile(n, cap):
    return next(c for c in _TILE_CHOICES if c <= cap and n % c == 0)


def _mm(a, b, *, name, ta=False, tb=False, out_dtype=F32, shards=1, tm=1024, tn=1024, tk=4096):
    if ta:
        K, M = a.shape
    else:
        M, K = a.shape
    b_shards = b.shape[0] if b.ndim == 3 else 1
    if b.ndim == 3:
        assert not tb
        kb, N = b.shape[1], b.shape[0] * b.shape[2]
    elif tb:
        N, kb = b.shape
    else:
        kb, N = b.shape
    assert K == kb, (a.shape, b.shape)
    ns = N // max(shards, b_shards)
    tm, tn, tk = _tile(M, tm), _tile(ns, tn), _tile(K, tk)
    assert M % tm == 0 and ns % tn == 0 and K % tk == 0 and N % shards == 0
    nk = K // tk
    dn = (((0 if ta else 1,), (1 if tb else 0,)), ((), ()))

    def body(a_ref, b_ref, o_ref, *acc):
        prod = lax.dot_general(a_ref[...].astype(BF16), b_ref[...].astype(BF16), dn, preferred_element_type=F32)
        if nk == 1:
            o_ref[...] = prod.astype(o_ref.dtype)
            return
        acc_ref, = acc
        k = pl.program_id(2)

        @pl.when(k == 0)
        def _():
            acc_ref[...] = jnp.zeros_like(acc_ref)

        acc_ref[...] += prod

        @pl.when(k == nk - 1)
        def _():
            o_ref[...] = acc_ref[...].astype(o_ref.dtype)

    a_spec = (pl.BlockSpec((tk, tm), lambda i, j, k: (k, i)) if ta
              else pl.BlockSpec((tm, tk), lambda i, j, k: (i, k)))
    b_spec = (pl.BlockSpec((tn, tk), lambda i, j, k: (j, k)) if tb
              else pl.BlockSpec((tk, tn), lambda i, j, k: (k, j)))
    if b_shards > 1:
        nbb = N // b_shards // tn
        b_spec = pl.BlockSpec((None, tk, tn), lambda i, j, k: (j // nbb, k, j % nbb))
    if shards == 1:
        out_shape = jax.ShapeDtypeStruct((M, N), out_dtype)
        o_spec = pl.BlockSpec((tm, tn), lambda i, j, k: (i, j))
    else:
        nb = ns // tn
        out_shape = jax.ShapeDtypeStruct((shards, M, ns), out_dtype)
        o_spec = pl.BlockSpec((None, tm, tn), lambda i, j, k: (j // nb, i, j % nb))
    return pl.pallas_call(
        body, name=name, out_shape=out_shape,
        grid=(M // tm, N // tn, nk),
        in_specs=[a_spec, b_spec], out_specs=o_spec,
        scratch_shapes=[] if nk == 1 else [pltpu.VMEM((tm, tn), F32)],
        compiler_params=_params("parallel", "parallel", "arbitrary"),
    )(a, b)


def _rmsnorm_fwd(x, g, *, name, res=None, out_dtype=F32, tr=256):
    L, D = x.shape
    tr = min(tr, L)
    has_res = res is not None

    def body(*refs):
        if has_res:
            x_ref, g_ref, r_ref, o_ref = refs
        else:
            x_ref, g_ref, o_ref = refs
        xf = x_ref[...]
        r = lax.rsqrt(jnp.mean(xf * xf, axis=-1, keepdims=True) + EPS)
        y = xf * r * g_ref[...]
        if has_res:
            y = r_ref[...] + y
        o_ref[...] = y.astype(o_ref.dtype)

    row = pl.BlockSpec((tr, D), lambda i: (i, 0))
    vec = pl.BlockSpec((1, D), lambda i: (0, 0))
    ins = [x, g.reshape(1, D)] + ([res] if has_res else [])
    return pl.pallas_call(
        body, name=name, out_shape=jax.ShapeDtypeStruct((L, D), out_dtype),
        grid=(L // tr,), in_specs=[row, vec] + ([row] if has_res else []), out_specs=row,
        compiler_params=_params("parallel"),
    )(*ins)


def _rmsnorm_bwd(x, g, dy, *, name, adds=(), dx_dtype=F32, tr=256):
    L, D = x.shape
    tr = min(tr, L)
    dys = dy if isinstance(dy, tuple) else (dy,)
    n_dy, n_add = len(dys), len(adds)

    def body(*refs):
        x_ref, g_ref = refs[:2]
        dy_refs = refs[2:2 + n_dy]
        add_refs = refs[2 + n_dy:2 + n_dy + n_add]
        dx_ref, dg_ref = refs[2 + n_dy + n_add:]
        xf = x_ref[...]
        dyf = dy_refs[0][...].astype(F32)
        for d_ref in dy_refs[1:]:
            dyf = dyf + d_ref[...].astype(F32)
        r = lax.rsqrt(jnp.mean(xf * xf, axis=-1, keepdims=True) + EPS)
        gy = dyf * g_ref[...]
        c = jnp.mean(xf * gy, axis=-1, keepdims=True) * (r * r * r)
        dx = gy * r - xf * c
        for a_ref in add_refs:
            dx = dx + a_ref[...].astype(F32)
        dx_ref[...] = dx.astype(dx_ref.dtype)

        @pl.when(pl.program_id(0) == 0)
        def _():
            dg_ref[...] = jnp.zeros_like(dg_ref)

        dg_ref[...] += jnp.sum(dyf * xf * r, axis=0, keepdims=True)

    row = pl.BlockSpec((tr, D), lambda i: (i, 0))
    vec = pl.BlockSpec((1, D), lambda i: (0, 0))
    dx, dg = pl.pallas_call(
        body, name=name,
        out_shape=(jax.ShapeDtypeStruct((L, D), dx_dtype), jax.ShapeDtypeStruct((1, D), F32)),
        grid=(L // tr,), in_specs=[row, vec] + [row] * (n_dy + n_add), out_specs=(row, vec),
        compiler_params=_params("arbitrary"),
    )(x, g.reshape(1, D), *dys, *adds)
    return dx, dg.reshape(D)


def _rmsnorm_bwd_pair(x, g1, dy1, g2, dy2, *, name, adds=(), tr=256):
    L, D = x.shape
    tr = min(tr, L)
    dy1s = dy1 if isinstance(dy1, tuple) else (dy1,)
    n1, n_add = len(dy1s), len(adds)

    def body(*refs):
        x_ref, g1_ref, g2_ref = refs[:3]
        dy1_refs = refs[3:3 + n1]
        dy2_ref = refs[3 + n1]
        add_refs = refs[4 + n1:4 + n1 + n_add]
        dx_ref, dg1_ref, dg2_ref = refs[4 + n1 + n_add:]
        xf = x_ref[...]
        d1 = dy1_refs[0][...].astype(F32)
        for d_ref in dy1_refs[1:]:
            d1 = d1 + d_ref[...].astype(F32)
        d2 = dy2_ref[...].astype(F32)
        r = lax.rsqrt(jnp.mean(xf * xf, axis=-1, keepdims=True) + EPS)
        gy = d1 * g1_ref[...] + d2 * g2_ref[...]
        c = jnp.mean(xf * gy, axis=-1, keepdims=True) * (r * r * r)
        dx = gy * r - xf * c
        for a_ref in add_refs:
            dx = dx + a_ref[...].astype(F32)
        dx_ref[...] = dx

        @pl.when(pl.program_id(0) == 0)
        def _():
            dg1_ref[...] = jnp.zeros_like(dg1_ref)
            dg2_ref[...] = jnp.zeros_like(dg2_ref)

        xr = xf * r
        dg1_ref[...] += jnp.sum(d1 * xr, axis=0, keepdims=True)
        dg2_ref[...] += jnp.sum(d2 * xr, axis=0, keepdims=True)

    row = pl.BlockSpec((tr, D), lambda i: (i, 0))
    vec = pl.BlockSpec((1, D), lambda i: (0, 0))
    dx, dg1, dg2 = pl.pallas_call(
        body, name=name,
        out_shape=(jax.ShapeDtypeStruct((L, D), F32), jax.ShapeDtypeStruct((1, D), F32),
                   jax.ShapeDtypeStruct((1, D), F32)),
        grid=(L // tr,), in_specs=[row, vec, vec] + [row] * (n1 + 1 + n_add), out_specs=(row, vec, vec),
        compiler_params=_params("arbitrary"),
    )(x, g1.reshape(1, D), g2.reshape(1, D), *dy1s, dy2, *adds)
    return dx, dg1.reshape(D), dg2.reshape(D)


def _final_norm_loss(o, g, res, target, *, tr=256):
    L, D = o.shape
    tr = min(tr, L)

    def body(o_ref, g_ref, r_ref, t_ref, dh_ref, loss_ref):
        xf = o_ref[...]
        r = lax.rsqrt(jnp.mean(xf * xf, axis=-1, keepdims=True) + EPS)
        e = (r_ref[...] + xf * r * g_ref[...]) - t_ref[...]
        dh_ref[...] = e * (1.0 / D)

        @pl.when(pl.program_id(0) == 0)
        def _():
            loss_ref[...] = jnp.zeros_like(loss_ref)

        loss_ref[...] += jnp.sum(e * e, axis=0, keepdims=True) * (0.5 / D)

    row = pl.BlockSpec((tr, D), lambda i: (i, 0))
    vec = pl.BlockSpec((1, D), lambda i: (0, 0))
    dh, lp = pl.pallas_call(
        body, name="post_norm_1_loss",
        out_shape=(jax.ShapeDtypeStruct((L, D), F32), jax.ShapeDtypeStruct((1, D), F32)),
        grid=(L // tr,), in_specs=[row, vec, row, row], out_specs=(row, vec),
        compiler_params=_params("arbitrary"),
    )(o, g.reshape(1, D), res, target)
    return dh, lp


def _s5_coeffs(lr, li, ls):
    dt = jnp.exp(ls)
    mag = jnp.exp(lr * dt)
    ar = mag * jnp.cos(li * dt)
    ai = mag * jnp.sin(li * dt)
    den = lr * lr + li * li
    cr = ((ar - 1.0) * lr + ai * li) / den
    ci = (ai * lr - (ar - 1.0) * li) / den
    return dt, ar, ai, den, cr, ci


def _s5_prep(lam_re, lam_im, log_step, b_re_t, b_im_t):
    G, P = lam_re.shape
    H = b_re_t.shape[1]

    def body(lr_ref, li_ref, ls_ref, br_ref, bi_ref, ar_ref, ai_ref, bbr_ref, bbi_ref):
        _, ar, ai, _, cr, ci = _s5_coeffs(lr_ref[...], li_ref[...], ls_ref[...])
        ar_ref[...] = ar
        ai_ref[...] = ai
        br, bi = br_ref[...], bi_ref[...]
        crb, cib = cr[:, None, :], ci[:, None, :]
        bbr_ref[...] = crb * br - cib * bi
        bbi_ref[...] = crb * bi + cib * br

    return pl.pallas_call(
        body, name="s5_prep",
        out_shape=(jax.ShapeDtypeStruct((G, P), F32), jax.ShapeDtypeStruct((G, P), F32),
                   jax.ShapeDtypeStruct((G, H, P), F32), jax.ShapeDtypeStruct((G, H, P), F32)),
        compiler_params=_params(),
    )(lam_re, lam_im, log_step.reshape(G, 1), b_re_t, b_im_t)


def _s5_prep_bwd(lam_re, lam_im, log_step, b_re_t, b_im_t, d_ar, d_ai, d_bbr, d_bbi):
    G, P = lam_re.shape
    H = b_re_t.shape[1]

    def body(lr_ref, li_ref, ls_ref, br_ref, bi_ref, dar_ref, dai_ref, dbbr_ref, dbbi_ref,
             dlr_ref, dli_ref, dls_ref, dbr_ref, dbi_ref):
        lr, li = lr_ref[...], li_ref[...]
        dt, ar, ai, den, cr, ci = _s5_coeffs(lr, li, ls_ref[...])
        br, bi = br_ref[...], bi_ref[...]
        gbr, gbi = dbbr_ref[...], dbbi_ref[...]
        crb, cib = cr[:, None, :], ci[:, None, :]
        dbr_ref[...] = crb * gbr + cib * gbi
        dbi_ref[...] = crb * gbi - cib * gbr
        gcr = jnp.sum(br * gbr + bi * gbi, axis=1)
        gci = jnp.sum(br * gbi - bi * gbr, axis=1)
        ilr, ili = lr / den, -li / den
        gar = dar_ref[...] + (ilr * gcr + ili * gci)
        gai = dai_ref[...] + (ilr * gci - ili * gcr)
        qr, qi = cr * ilr - ci * ili, cr * ili + ci * ilr
        glr = -(qr * gcr + qi * gci)
        gli = -(qr * gci - qi * gcr)
        glr = glr + dt * (ar * gar + ai * gai)
        gli = gli + dt * (ar * gai - ai * gar)
        wr, wi = lr * ar - li * ai, lr * ai + li * ar
        gdt = jnp.sum(wr * gar + wi * gai, axis=1, keepdims=True)
        dlr_ref[...] = glr
        dli_ref[...] = gli
        dls_ref[...] = gdt * dt

    return pl.pallas_call(
        body, name="s5_prep_bwd",
        out_shape=(jax.ShapeDtypeStruct((G, P), F32), jax.ShapeDtypeStruct((G, P), F32),
                   jax.ShapeDtypeStruct((G, 1), F32),
                   jax.ShapeDtypeStruct((G, H, P), F32), jax.ShapeDtypeStruct((G, H, P), F32)),
        compiler_params=_params(),
    )(lam_re, lam_im, log_step.reshape(G, 1), b_re_t, b_im_t, d_ar, d_ai, d_bbr, d_bbi)


def _s5_block_mats(bbr_t, bbi_t, c_re, c_im):
    bmat = _s5_expand(bbr_t, bbi_t)
    cmat = jnp.transpose(_s5_expand(c_re, -c_im), (0, 2, 1))
    return bmat.astype(BF16), cmat.astype(BF16)


def _s5_diag_mask():
    r = lax.broadcasted_iota(jnp.int32, (LANES, 2 * STATE_COLS), 0) // SSM_GROUP
    c = (lax.broadcasted_iota(jnp.int32, (LANES, 2 * STATE_COLS), 1) % STATE_COLS) // SSM_STATE
    return (r == c).astype(F32)


def _s5_expand(re, im):
    re = jnp.tile(re.reshape(SSM_BLOCKS, LANES, SSM_STATE), (1, 1, GROUPS_PER_BLOCK))
    im = jnp.tile(im.reshape(SSM_BLOCKS, LANES, SSM_STATE), (1, 1, GROUPS_PER_BLOCK))
    return jnp.concatenate([re, im], axis=-1) * _s5_diag_mask()[None]


def _s5_unfold(dmat):
    d = dmat.reshape(SSM_GROUPS, SSM_GROUP, 2, SSM_STATE)
    return jnp.transpose(d, (2, 0, 1, 3))


def _s5_a_rows(ar, ai):
    a = jnp.concatenate([ar.reshape(SSM_BLOCKS, STATE_COLS), ai.reshape(SSM_BLOCKS, STATE_COLS)], axis=1)
    return jnp.broadcast_to(a[:, None, :], (SSM_BLOCKS, SUBLANES, 2 * STATE_COLS))


def _to_step_major(src_ref, dst_ref, seg):
    for s in range(SUBLANES):
        dst_ref[pl.ds(s, seg, stride=SUBLANES), :] = src_ref[pl.ds(seg * s, seg), :]


def _segment_rows(ref, s, seg):
    return ref[pl.ds(s, seg, stride=SUBLANES), :]


def _cmul(ar, ai, xr, xi):
    return ar * xr - ai * xi, ar * xi + ai * xr


def _s5_tables(a_ref, pw_s, pwr_s, S, seg):
    ar, ai = a_ref[:, :S], a_ref[:, S:]

    def step(i, c):
        pr, pi = c
        pw_s[i, :, :S] = pr
        pw_s[i, :, S:] = pi
        nr, ni = _cmul(ar, ai, pr, pi)
        pwr_s[seg - 1 - i, :, :S] = nr
        pwr_s[seg - 1 - i, :, S:] = ni
        return nr, ni

    pr, pi = lax.fori_loop(0, seg, step, (jnp.ones_like(ar), jnp.zeros_like(ai)))
    pw_s[seg, :, :S] = pr
    pw_s[seg, :, S:] = pi


def _s5_fwd(proj, bmat, cmat, a_rows, d_skip, *, tc=512):
    L = proj.shape[0]
    tc = min(tc, L)
    nt = L // tc
    seg = tc // SUBLANES
    S = STATE_COLS

    def body(u_ref, b_ref, c_ref, a_ref, d_ref, y_ref, yg_ref, xp_ref,
             bu_s, xp_s, pw_s, pwr_s, carry_s, e_s, up_s, yc_s):
        @pl.when(pl.program_id(1) == 0)
        def _():
            carry_s[...] = jnp.zeros_like(carry_s)
            _s5_tables(a_ref, pw_s, pwr_s, S, seg)

        ar, ai = a_ref[:, :S], a_ref[:, S:]
        _to_step_major(u_ref, up_s, seg)
        bu = jnp.dot(up_s[...].astype(BF16), b_ref[...], preferred_element_type=F32)
        bu_s[...] = bu.reshape(seg, SUBLANES, 2 * S)

        def step(i, carry):
            cr, ci = carry
            xp_s[i, :, :S] = cr
            xp_s[i, :, S:] = ci
            return ar * cr - ai * ci + bu_s[i, :, :S], ar * ci + ai * cr + bu_s[i, :, S:]

        zero = jnp.zeros((SUBLANES, S), F32)
        fr, fi = lax.fori_loop(0, seg, step, (zero, zero))
        pr, pi = pw_s[seg, 0:1, :S], pw_s[seg, 0:1, S:]
        er, ei = carry_s[0:1, :S], carry_s[0:1, S:]
        for s in range(SUBLANES):
            e_s[s:s + 1, :S] = er
            e_s[s:s + 1, S:] = ei
            tr, ti = _cmul(pr, pi, er, ei)
            er, ei = fr[s:s + 1] + tr, fi[s:s + 1] + ti
        carry_s[0:1, :S] = er
        carry_s[0:1, S:] = ei
        pw = pw_s[0:seg]
        tr, ti = _cmul(pw[:, :, :S], pw[:, :, S:], e_s[:, :S][None], e_s[:, S:][None])
        xl = xp_s[...]
        xp = jnp.concatenate([xl[:, :, :S] + tr, xl[:, :, S:] + ti], axis=-1).reshape(tc, 2 * S)
        xp_ref[...] = xp
        a1r, a1i = ar[0:1], ai[0:1]
        x_re = a1r * xp[:, :S] - a1i * xp[:, S:] + bu[:, :S]
        x_im = a1r * xp[:, S:] + a1i * xp[:, :S] + bu[:, S:]
        xs = jnp.concatenate([x_re, x_im], axis=1).astype(BF16)
        yc_s[...] = jnp.dot(xs, c_ref[...], preferred_element_type=F32)
        for s in range(SUBLANES):
            rows = pl.ds(seg * s, seg)
            y = _segment_rows(yc_s, s, seg) + d_ref[...] * u_ref[rows, :]
            y_ref[rows, :] = y
            yg_ref[rows, :] = _gelu(y).astype(BF16)

    return pl.pallas_call(
        body, name="s5_fwd",
        out_shape=(jax.ShapeDtypeStruct((L, MAIN_WIDTH), F32),
                   jax.ShapeDtypeStruct((L, MAIN_WIDTH), BF16),
                   jax.ShapeDtypeStruct((L, SSM_BLOCKS * 2 * S), F32)),
        grid=(SSM_BLOCKS, nt),
        in_specs=[pl.BlockSpec((tc, LANES), lambda b, t: (t, b)),
                  pl.BlockSpec((None, LANES, 2 * S), lambda b, t: (b, 0, 0)),
                  pl.BlockSpec((None, 2 * S, LANES), lambda b, t: (b, 0, 0)),
                  pl.BlockSpec((None, SUBLANES, 2 * S), lambda b, t: (b, 0, 0)),
                  pl.BlockSpec((1, LANES), lambda b, t: (0, b))],
        out_specs=(pl.BlockSpec((tc, LANES), lambda b, t: (t, b)),
                   pl.BlockSpec((tc, LANES), lambda b, t: (t, b)),
                   pl.BlockSpec((tc, 2 * S), lambda b, t: (t, b))),
        scratch_shapes=[pltpu.VMEM((seg, SUBLANES, 2 * S), F32),
                        pltpu.VMEM((seg, SUBLANES, 2 * S), F32),
                        pltpu.VMEM((seg + 1, SUBLANES, 2 * S), F32),
                        pltpu.VMEM((seg, SUBLANES, 2 * S), F32),
                        pltpu.VMEM((SUBLANES, 2 * S), F32),
                        pltpu.VMEM((SUBLANES, 2 * S), F32),
                        pltpu.VMEM((tc, LANES), F32),
                        pltpu.VMEM((tc, LANES), F32)],
        compiler_params=_params("parallel", "arbitrary"),
    )(proj, bmat, cmat, a_rows, d_skip.reshape(1, MAIN_WIDTH))


def _s5_bwd(proj, dyg_a, dyg_b, y, xp, bmat, cmat, a_rows, d_skip, dproj, *, tc=512):
    L = proj.shape[0]
    tc = min(tc, L)
    nt = L // tc
    seg = tc // SUBLANES
    S = STATE_COLS
    nn = (((1,), (1,)), ((), ()))
    tn = (((0,), (0,)), ((), ()))

    def fold_diagonal(acc_ref, mask_ref, fold_ref):
        x = acc_ref[...] * mask_ref[...]
        hi = x.astype(BF16)
        rest = x - hi.astype(F32)
        mid = rest.astype(BF16)
        low = (rest - mid.astype(F32)).astype(BF16)
        return sum(jnp.dot(piece, fold_ref[...], preferred_element_type=F32) for piece in (hi, mid, low))

    def body(u_ref, dyga_ref, dygb_ref, y_ref, xp_ref, b_ref, c_ref, a_ref, d_ref, mask_ref, fold_ref, dp_hbm,
             du_ref, dbd_ref, dcd_ref, da_ref, dd_ref,
             dl_s, pw_s, pwr_s, carry_s, e_s, up_s, dy_s, dyp_s, dup_s, db_ref, dc_ref):
        @pl.when(pl.program_id(1) == 0)
        def _():
            carry_s[...] = jnp.zeros_like(carry_s)
            db_ref[...] = jnp.zeros_like(db_ref)
            dc_ref[...] = jnp.zeros_like(dc_ref)
            da_ref[...] = jnp.zeros_like(da_ref)
            dd_ref[...] = jnp.zeros_like(dd_ref)
            _s5_tables(a_ref, pw_s, pwr_s, S, seg)

        ar, ai = a_ref[:, :S], a_ref[:, S:]
        a1r, a1i = ar[0:1], ai[0:1]
        u = u_ref[...]
        dy = (dyga_ref[...] + dygb_ref[...]) * _gelu_grad(y_ref[...])
        dy_s[...] = dy
        xp = xp_ref[...]
        _to_step_major(u_ref, up_s, seg)
        _to_step_major(dy_s, dyp_s, seg)
        ubp = up_s[...].astype(BF16)
        dyp = dyp_s[...].astype(BF16)
        bu = jnp.dot(ubp, b_ref[...], preferred_element_type=F32)
        x_re = a1r * xp[:, :S] - a1i * xp[:, S:] + bu[:, :S]
        x_im = a1r * xp[:, S:] + a1i * xp[:, :S] + bu[:, S:]
        xs = jnp.concatenate([x_re, x_im], axis=1).astype(BF16)
        dc_ref[...] += lax.dot_general(dyp, xs, tn, preferred_element_type=F32)
        dx = lax.dot_general(dyp, c_ref[...], nn, preferred_element_type=F32)
        dl_s[...] = dx.reshape(seg, SUBLANES, 2 * S)

        def step(k, carry):
            cr, ci = carry
            i = seg - 1 - k
            lr = dl_s[i, :, :S] + (ar * cr + ai * ci)
            li = dl_s[i, :, S:] + (ar * ci - ai * cr)
            dl_s[i, :, :S] = lr
            dl_s[i, :, S:] = li
            return lr, li

        zero = jnp.zeros((SUBLANES, S), F32)
        fr, fi = lax.fori_loop(0, seg, step, (zero, zero))
        pr, pi = pw_s[seg, 0:1, :S], pw_s[seg, 0:1, S:]
        er, ei = carry_s[0:1, :S], carry_s[0:1, S:]
        for s in range(SUBLANES - 1, -1, -1):
            e_s[s:s + 1, :S] = er
            e_s[s:s + 1, S:] = ei
            er, ei = fr[s:s + 1] + (pr * er + pi * ei), fi[s:s + 1] + (pr * ei - pi * er)
        carry_s[0:1, :S] = er
        carry_s[0:1, S:] = ei
        er, ei = e_s[:, :S][None], e_s[:, S:][None]
        pw = pwr_s[...]
        pwr, pwi = pw[:, :, :S], pw[:, :, S:]
        ll = dl_s[...]
        lam = jnp.concatenate([ll[:, :, :S] + (pwr * er + pwi * ei), ll[:, :, S:] + (pwr * ei - pwi * er)],
                              axis=-1).reshape(tc, 2 * S)
        l_re, l_im = lam[:, :S], lam[:, S:]
        da_ref[0:1, :S] += jnp.sum(l_re * xp[:, :S] + l_im * xp[:, S:], axis=0, keepdims=True)
        da_ref[0:1, S:] += jnp.sum(l_im * xp[:, :S] - l_re * xp[:, S:], axis=0, keepdims=True)
        lamb = lam.astype(BF16)
        dup_s[...] = lax.dot_general(lamb, b_ref[...], nn, preferred_element_type=F32)
        for s in range(SUBLANES):
            rows = pl.ds(seg * s, seg)
            du = _segment_rows(dup_s, s, seg) + d_ref[...] * dy_s[rows, :]
            du_ref[rows, :] = du.astype(du_ref.dtype)
        db_ref[...] += lax.dot_general(ubp, lamb, tn, preferred_element_type=F32)
        dd_ref[0:1, :] += jnp.sum(dy * u, axis=0, keepdims=True)

        @pl.when(pl.program_id(1) == nt - 1)
        def _():
            dbd_ref[...] = fold_diagonal(db_ref, mask_ref, fold_ref)
            dcd_ref[...] = fold_diagonal(dc_ref, mask_ref, fold_ref)

    rev = lambda b, t: (nt - 1 - t, b)
    col = jnp.arange(2 * S)
    fold = ((col // S * SSM_STATE + col % SSM_STATE)[:, None] == jnp.arange(LANES)[None, :]).astype(BF16)
    return pl.pallas_call(
        body, name="s5_bwd",
        out_shape=(jax.ShapeDtypeStruct(dproj.shape, dproj.dtype),
                   jax.ShapeDtypeStruct((SSM_BLOCKS, LANES, LANES), F32),
                   jax.ShapeDtypeStruct((SSM_BLOCKS, LANES, LANES), F32),
                   jax.ShapeDtypeStruct((SSM_BLOCKS, SUBLANES, 2 * S), F32),
                   jax.ShapeDtypeStruct((SUBLANES, MAIN_WIDTH), F32)),
        input_output_aliases={11: 0},
        grid=(SSM_BLOCKS, nt),
        in_specs=[pl.BlockSpec((tc, LANES), rev),
                  pl.BlockSpec((tc, LANES), rev),
                  pl.BlockSpec((tc, LANES), rev),
                  pl.BlockSpec((tc, LANES), rev),
                  pl.BlockSpec((tc, 2 * S), rev),
                  pl.BlockSpec((None, LANES, 2 * S), lambda b, t: (b, 0, 0)),
                  pl.BlockSpec((None, 2 * S, LANES), lambda b, t: (b, 0, 0)),
                  pl.BlockSpec((None, SUBLANES, 2 * S), lambda b, t: (b, 0, 0)),
                  pl.BlockSpec((1, LANES), lambda b, t: (0, b)),
                  pl.BlockSpec((LANES, 2 * S), lambda b, t: (0, 0)),
                  pl.BlockSpec((2 * S, LANES), lambda b, t: (0, 0)),
                  _ANY],
        out_specs=(pl.BlockSpec((tc, LANES), rev),
                   pl.BlockSpec((None, LANES, LANES), lambda b, t: (b, 0, 0)),
                   pl.BlockSpec((None, LANES, LANES), lambda b, t: (b, 0, 0)),
                   pl.BlockSpec((None, SUBLANES, 2 * S), lambda b, t: (b, 0, 0)),
                   pl.BlockSpec((SUBLANES, LANES), lambda b, t: (0, b))),
        scratch_shapes=[pltpu.VMEM((seg, SUBLANES, 2 * S), F32),
                        pltpu.VMEM((seg + 1, SUBLANES, 2 * S), F32),
                        pltpu.VMEM((seg, SUBLANES, 2 * S), F32),
                        pltpu.VMEM((SUBLANES, 2 * S), F32),
                        pltpu.VMEM((SUBLANES, 2 * S), F32),
                        pltpu.VMEM((tc, LANES), F32),
                        pltpu.VMEM((tc, LANES), F32),
                        pltpu.VMEM((tc, LANES), F32),
                        pltpu.VMEM((tc, LANES), F32),
                        pltpu.VMEM((LANES, 2 * S), F32),
                        pltpu.VMEM((LANES, 2 * S), F32)],
        compiler_params=_params("parallel", "arbitrary"),
    )(proj, dyg_a, dyg_b, y, xp, bmat, cmat, a_rows, d_skip.reshape(1, MAIN_WIDTH), _s5_diag_mask(), fold, dproj)


_Z_COLS = slice(MAIN_WIDTH, 2 * MAIN_WIDTH)
_ZM_COLS = slice(2 * MAIN_WIDTH + MEM_WIDTH, IN_WIDTH)


def _proj_rows(tr):
    return pl.BlockSpec((tr, IN_WIDTH), lambda i: (i, 0))


def _row_specs(tr):
    main = pl.BlockSpec((tr, MAIN_WIDTH), lambda i: (i, 0))
    z = pl.BlockSpec((tr, MAIN_WIDTH), lambda i: (i, 1))
    zm = pl.BlockSpec((tr, MEM_WIDTH), lambda i: (i, IN_WIDTH // MEM_WIDTH - 1))
    mem = pl.BlockSpec((tr, MEM_WIDTH), lambda i: (i, 0))
    cat = pl.BlockSpec((tr, D_MODEL), lambda i: (i, 0))
    vec = pl.BlockSpec((1, MAIN_WIDTH), lambda i: (0, 0))
    return main, z, zm, mem, cat, vec


def _gate_a_fwd(y, t, b_glu, proj, o_mem, *, tr=256):
    L = y.shape[0]
    tr = min(tr, L)

    def body(y_ref, t_ref, b_ref, z_ref, zm_ref, om_ref, o_ref):
        yg = _gelu(y_ref[...])
        sz, _ = _silu_and_grad(z_ref[...])
        o_ref[:, :MAIN_WIDTH] = (yg * _sigmoid(t_ref[...] + b_ref[...]) * sz).astype(BF16)
        szm, _ = _silu_and_grad(zm_ref[...])
        o_ref[:, MAIN_WIDTH:] = (om_ref[...] * szm).astype(BF16)

    main, z, zm, mem, cat, vec = _row_specs(tr)
    return pl.pallas_call(
        body, name="gate_a_fwd", out_shape=jax.ShapeDtypeStruct((L, D_MODEL), BF16),
        grid=(L // tr,), in_specs=[main, main, vec, z, zm, mem], out_specs=cat,
        compiler_params=_params("parallel"),
    )(y, t, b_glu.reshape(1, MAIN_WIDTH), proj, proj, o_mem)


def _gate_a_bwd(dcat, y, t, b_glu, proj, o_mem, *, tr=256):
    L = y.shape[0]
    tr = min(tr, L)

    def body(dc_ref, y_ref, t_ref, b_ref, z_ref, zm_ref, om_ref,
             dp_ref, dt_ref, dyg_ref, dom_ref, db_ref):
        dmain = dc_ref[:, :MAIN_WIDTH]
        dmemo = dc_ref[:, MAIN_WIDTH:]
        yg = _gelu(y_ref[...])
        sg = _sigmoid(t_ref[...] + b_ref[...])
        sz, gz = _silu_and_grad(z_ref[...])
        dp_ref[:, _Z_COLS] = (dmain * (yg * sg) * gz).astype(BF16)
        dy2 = dmain * sz
        dyg_ref[...] = dy2 * sg
        dt = dy2 * yg * (sg * (1.0 - sg))
        dt_ref[...] = dt.astype(BF16)

        @pl.when(pl.program_id(0) == 0)
        def _():
            db_ref[...] = jnp.zeros_like(db_ref)

        db_ref[...] += jnp.sum(dt, axis=0, keepdims=True)
        szm, gzm = _silu_and_grad(zm_ref[...])
        dom_ref[...] = dmemo * szm
        dp_ref[:, _ZM_COLS] = (dmemo * om_ref[...] * gzm).astype(BF16)

    main, z, zm, mem, cat, vec = _row_specs(tr)
    outs = pl.pallas_call(
        body, name="gate_a_bwd",
        out_shape=(jax.ShapeDtypeStruct((L, IN_WIDTH), BF16),
                   jax.ShapeDtypeStruct((L, MAIN_WIDTH), BF16), jax.ShapeDtypeStruct((L, MAIN_WIDTH), F32),
                   jax.ShapeDtypeStruct((L, MEM_WIDTH), F32), jax.ShapeDtypeStruct((1, MAIN_WIDTH), F32)),
        grid=(L // tr,), in_specs=[cat, main, main, vec, z, zm, mem],
        out_specs=(_proj_rows(tr), main, main, mem, vec),
        compiler_params=_params("arbitrary"),
    )(dcat, y, t, b_glu.reshape(1, MAIN_WIDTH), proj, proj, o_mem)
    return outs


def _gate_b_fwd(att, proj, o_mem, *, tr=256):
    L = att.shape[0]
    tr = min(tr, L)

    def body(a_ref, z_ref, zm_ref, om_ref, o_ref):
        sz, _ = _silu_and_grad(z_ref[...])
        o_ref[:, :MAIN_WIDTH] = (a_ref[...] * sz).astype(BF16)
        szm, _ = _silu_and_grad(zm_ref[...])
        o_ref[:, MAIN_WIDTH:] = (om_ref[...] * szm).astype(BF16)

    main, z, zm, mem, cat, _ = _row_specs(tr)
    return pl.pallas_call(
        body, name="gate_b_fwd", out_shape=jax.ShapeDtypeStruct((L, D_MODEL), BF16),
        grid=(L // tr,), in_specs=[main, z, zm, mem], out_specs=cat,
        compiler_params=_params("parallel"),
    )(att, proj, proj, o_mem)


def _gate_b_bwd(dcat, att, proj, o_mem, *, tr=256):
    L = att.shape[0]
    tr = min(tr, L)

    def body(dc_ref, a_ref, z_ref, zm_ref, om_ref, da_ref, dp_ref, dom_ref, dl_ref):
        dmain = dc_ref[:, :MAIN_WIDTH]
        dmemo = dc_ref[:, MAIN_WIDTH:]
        att = a_ref[...]
        sz, gz = _silu_and_grad(z_ref[...])
        datt = dmain * sz
        da_ref[...] = datt
        dp_ref[:, _Z_COLS] = (dmain * att * gz).astype(BF16)
        szm, gzm = _silu_and_grad(zm_ref[...])
        dom_ref[...] = dmemo * szm
        dp_ref[:, _ZM_COLS] = (dmemo * om_ref[...] * gzm).astype(BF16)
        prod = datt * att
        for h in range(FOX_HEADS):
            dl_ref[h] = jnp.sum(prod[:, h * HEAD_DIM:(h + 1) * HEAD_DIM], axis=1, keepdims=True)

    main, z, zm, mem, cat, _ = _row_specs(tr)
    delta = pl.BlockSpec((FOX_HEADS, tr, 1), lambda i: (0, i, 0))
    return pl.pallas_call(
        body, name="gate_b_bwd",
        out_shape=(jax.ShapeDtypeStruct((L, MAIN_WIDTH), F32), jax.ShapeDtypeStruct((L, IN_WIDTH), BF16),
                   jax.ShapeDtypeStruct((L, MEM_WIDTH), F32), jax.ShapeDtypeStruct((FOX_HEADS, L, 1), F32)),
        grid=(L // tr,), in_specs=[cat, main, z, zm, mem], out_specs=(main, _proj_rows(tr), mem, delta),
        compiler_params=_params("parallel"),
    )(dcat, att, proj, proj, o_mem)


_MEM_Q_COL = (2 * MAIN_WIDTH) // HEAD_DIM
_NT = (((1,), (1,)), ((), ()))
_TN = (((0,), (0,)), ((), ()))


def _mem_probs(q_ref, k_ref):
    qs = (q_ref[...] * (HEAD_DIM ** -0.5)).astype(BF16)
    s = lax.dot_general(qs, k_ref[...].astype(BF16), _NT, preferred_element_type=F32)
    e = jnp.exp(s - jnp.max(s, axis=-1, keepdims=True))
    return qs, e / jnp.sum(e, axis=-1, keepdims=True)


def _mem_attn_fwd(proj, kvm, *, tq=2048):
    L = proj.shape[0]
    tq = min(tq, L)

    def body(q_ref, k_ref, v_ref, o_ref):
        _, p = _mem_probs(q_ref, k_ref)
        o_ref[...] = jnp.dot(p.astype(BF16), v_ref[...].astype(BF16), preferred_element_type=F32)

    return pl.pallas_call(
        body, name="mem_attn_fwd", out_shape=jax.ShapeDtypeStruct((L, MEM_WIDTH), F32),
        grid=(MEM_HEADS, L // tq),
        in_specs=[pl.BlockSpec((tq, HEAD_DIM), lambda h, i: (i, _MEM_Q_COL + h)),
                  pl.BlockSpec((N_MEM, HEAD_DIM), lambda h, i: (0, h)),
                  pl.BlockSpec((N_MEM, HEAD_DIM), lambda h, i: (0, MEM_HEADS + h))],
        out_specs=pl.BlockSpec((tq, HEAD_DIM), lambda h, i: (i, h)),
        compiler_params=_params("parallel", "parallel"),
    )(proj, kvm, kvm)


def _mem_attn_bwd(proj, kvm, do, dproj, *, tq=2048):
    L = proj.shape[0]
    tq = min(tq, L)

    def body(q_ref, k_ref, v_ref, do_ref, dp_hbm, dq_ref, dk_ref, dv_ref):
        @pl.when(pl.program_id(1) == 0)
        def _():
            dk_ref[...] = jnp.zeros_like(dk_ref)
            dv_ref[...] = jnp.zeros_like(dv_ref)

        qs, p = _mem_probs(q_ref, k_ref)
        dob = do_ref[...].astype(BF16)
        dp = lax.dot_general(dob, v_ref[...].astype(BF16), _NT, preferred_element_type=F32)
        ds = p * (dp - jnp.sum(p * dp, axis=-1, keepdims=True))
        dsb = ds.astype(BF16)
        dq = jnp.dot(dsb, k_ref[...].astype(BF16), preferred_element_type=F32) * (HEAD_DIM ** -0.5)
        dq_ref[...] = dq.astype(BF16)
        dk_ref[...] += lax.dot_general(dsb, qs, _TN, preferred_element_type=F32)
        dv_ref[...] += lax.dot_general(p.astype(BF16), dob, _TN, preferred_element_type=F32)

    dproj, dk, dv = pl.pallas_call(
        body, name="mem_attn_bwd",
        out_shape=(jax.ShapeDtypeStruct(dproj.shape, dproj.dtype),
                   jax.ShapeDtypeStruct((N_MEM, MEM_WIDTH), F32),
                   jax.ShapeDtypeStruct((N_MEM, MEM_WIDTH), F32)),
        grid=(MEM_HEADS, L // tq),
        in_specs=[pl.BlockSpec((tq, HEAD_DIM), lambda h, i: (i, _MEM_Q_COL + h)),
                  pl.BlockSpec((N_MEM, HEAD_DIM), lambda h, i: (0, h)),
                  pl.BlockSpec((N_MEM, HEAD_DIM), lambda h, i: (0, MEM_HEADS + h)),
                  pl.BlockSpec((tq, HEAD_DIM), lambda h, i: (i, h)),
                  _ANY],
        out_specs=(pl.BlockSpec((tq, HEAD_DIM), lambda h, i: (i, _MEM_Q_COL + h)),
                   pl.BlockSpec((N_MEM, HEAD_DIM), lambda h, i: (0, h)),
                   pl.BlockSpec((N_MEM, HEAD_DIM), lambda h, i: (0, h))),
        input_output_aliases={4: 0},
        compiler_params=_params("parallel", "arbitrary"),
    )(proj, kvm, kvm, do, dproj)
    return dproj, jnp.concatenate([dk, dv], axis=1)


def _tile_cumsum(x, row, reverse):
    for sh in (1, 2, 4):
        if reverse:
            x = x + jnp.where(row < SUBLANES - sh, pltpu.roll(x, SUBLANES - sh, 0), 0.0)
        else:
            x = x + jnp.where(row >= sh, pltpu.roll(x, sh, 0), 0.0)
    return x


def _fgate_fwd(pre, b_pad):
    L = pre.shape[0]
    n8 = L // SUBLANES

    def body(p_ref, b_ref, o_ref):
        row = lax.broadcasted_iota(jnp.int32, (SUBLANES, LANES), 0)
        b = b_ref[...]

        def step(i, carry):
            x = p_ref[i] + b
            logf = jnp.minimum(x, 0.0) - jnp.log(1.0 + jnp.exp(-jnp.abs(x)))
            t = _tile_cumsum(logf, row, False) + carry
            o_ref[i] = t
            return t[SUBLANES - 1:SUBLANES, :]

        lax.fori_loop(0, n8, step, jnp.zeros((1, LANES), F32))

    out = pl.pallas_call(
        body, name="fgate_fwd", out_shape=jax.ShapeDtypeStruct((n8, SUBLANES, LANES), F32),
        compiler_params=_params(),
    )(pre.reshape(n8, SUBLANES, LANES), b_pad.reshape(1, LANES))
    return out.reshape(L, LANES)


def _fgate_bwd(dfcum, pre, b_pad):
    L = pre.shape[0]
    n8 = L // SUBLANES

    def body(d_ref, p_ref, b_ref, o_ref, s_ref):
        row = lax.broadcasted_iota(jnp.int32, (SUBLANES, LANES), 0)
        b = b_ref[...]

        def step(k, carry):
            c, acc = carry
            i = n8 - 1 - k
            t = _tile_cumsum(d_ref[i], row, True) + c
            dpre = t * _sigmoid(-(p_ref[i] + b))
            o_ref[i] = dpre
            return t[0:1, :], acc + dpre

        _, acc = lax.fori_loop(0, n8, step, (jnp.zeros((1, LANES), F32), jnp.zeros((SUBLANES, LANES), F32)))
        s_ref[...] = jnp.sum(acc, axis=0, keepdims=True)

    dpre, db = pl.pallas_call(
        body, name="fgate_bwd",
        out_shape=(jax.ShapeDtypeStruct((n8, SUBLANES, LANES), F32), jax.ShapeDtypeStruct((1, LANES), F32)),
        compiler_params=_params(),
    )(dfcum.reshape(n8, SUBLANES, LANES), pre.reshape(n8, SUBLANES, LANES), b_pad.reshape(1, LANES))
    return dpre.reshape(L, LANES), db


FOX_BLOCK = 512


def _fox_scores(qs, k, fk, diagonal):
    s = lax.dot_general(qs, k, _NT, preferred_element_type=F32) - fk
    if diagonal:
        row = lax.broadcasted_iota(jnp.int32, s.shape, 0)
        col = lax.broadcasted_iota(jnp.int32, s.shape, 1)
        s = jnp.where(row >= col, s, NEG_BIG)
    return s


def _fox_specs(tq, L):
    nq = L // tq
    return dict(
        rows=lambda off: pl.BlockSpec((tq, HEAD_DIM), lambda h, i: (i, off + h)),
        seq=lambda off: pl.BlockSpec((L, HEAD_DIM), lambda h, i: (0, off + h)),
        col=pl.BlockSpec((None, None, tq, 1), lambda h, i: (h, i, 0, 0)),
        col_all=pl.BlockSpec((None, nq, tq, 1), lambda h, i: (h, 0, 0, 0)),
        row=pl.BlockSpec((None, None, 1, tq), lambda h, i: (h, i, 0, 0)),
        row_all=pl.BlockSpec((None, nq, 1, tq), lambda h, i: (h, 0, 0, 0)))


FOX_FWD_HEADS = 2


def _fox_fwd(proj, kv, fk):
    L = proj.shape[0]
    tq = min(FOX_BLOCK, L)
    nq = L // tq
    nh = FOX_FWD_HEADS
    W = nh * HEAD_DIM

    def body(q_ref, k_ref, v_ref, fk_ref, o_ref, lse_ref, m_s, l_s, acc_s):
        qi = pl.program_id(1)
        cols = [slice(a * HEAD_DIM, (a + 1) * HEAD_DIM) for a in range(nh)]
        qs = [(q_ref[:, cs] * (HEAD_DIM ** -0.5)).astype(BF16) for cs in cols]
        m_s[...] = jnp.full_like(m_s, NEG_BIG)
        l_s[...] = jnp.zeros_like(l_s)
        acc_s[...] = jnp.zeros_like(acc_s)

        def block(j, diagonal):
            r0 = pl.multiple_of(j * tq, tq)
            for a, cs in enumerate(cols):
                s = _fox_scores(qs[a], k_ref[pl.ds(r0, tq), cs], fk_ref[a, j], diagonal)
                m_new = jnp.maximum(m_s[a], jnp.max(s, axis=-1, keepdims=True))
                alpha = jnp.exp(m_s[a] - m_new)
                p = jnp.exp(s - m_new)
                l_s[a] = alpha * l_s[a] + jnp.sum(p, axis=-1, keepdims=True)
                acc_s[a] = alpha * acc_s[a] + jnp.dot(p.astype(BF16), v_ref[pl.ds(r0, tq), cs],
                                                      preferred_element_type=F32)
                m_s[a] = m_new

        def below(j, carry):
            block(j, False)
            return carry

        lax.fori_loop(0, qi, below, 0)
        block(qi, True)
        for a, cs in enumerate(cols):
            o_ref[:, cs] = acc_s[a] / l_s[a]
            lse_ref[a] = m_s[a] + jnp.log(l_s[a])

    return pl.pallas_call(
        body, name="fox_fwd",
        out_shape=(jax.ShapeDtypeStruct((L, MAIN_WIDTH), F32),
                   jax.ShapeDtypeStruct((FOX_HEADS, nq, tq, 1), F32)),
        grid=(FOX_HEADS // nh, nq),
        in_specs=[pl.BlockSpec((tq, W), lambda h, i: (i, h)),
                  pl.BlockSpec((L, W), lambda h, i: (0, h)),
                  pl.BlockSpec((L, W), lambda h, i: (0, FOX_HEADS // nh + h)),
                  pl.BlockSpec((nh, nq, 1, tq), lambda h, i: (h, 0, 0, 0))],
        out_specs=(pl.BlockSpec((tq, W), lambda h, i: (i, h)),
                   pl.BlockSpec((nh, None, tq, 1), lambda h, i: (h, i, 0, 0))),
        scratch_shapes=[pltpu.VMEM((nh, tq, 1), F32), pltpu.VMEM((nh, tq, 1), F32),
                        pltpu.VMEM((nh, tq, HEAD_DIM), F32)],
        compiler_params=_params("parallel", "parallel"),
    )(proj, kv, kv, fk)


def _fox_bwd_dq(proj, kv, fk, lse, delta, datt, dproj):
    L = proj.shape[0]
    tq = min(FOX_BLOCK, L)
    nq = L // tq
    sp = _fox_specs(tq, L)

    def body(q_ref, k_ref, v_ref, fk_ref, lse_ref, dl_ref, do_ref, dp_hbm, dq_ref, df_ref, acc_s, df_s):
        qi = pl.program_id(1)
        qs = (q_ref[...] * (HEAD_DIM ** -0.5)).astype(BF16)
        dob = do_ref[...].astype(BF16)
        lse, dl = lse_ref[...], dl_ref[...]
        acc_s[...] = jnp.zeros_like(acc_s)
        df_s[...] = jnp.zeros_like(df_s)

        def block(j, diagonal):
            r0 = pl.multiple_of(j * tq, tq)
            k = k_ref[pl.ds(r0, tq), :]
            p = jnp.exp(_fox_scores(qs, k, fk_ref[j], diagonal) - lse)
            dp = lax.dot_general(dob, v_ref[pl.ds(r0, tq), :], _NT, preferred_element_type=F32)
            ds = p * (dp - dl)
            acc_s[...] += jnp.dot(ds.astype(BF16), k, preferred_element_type=F32)
            df_s[...] += jnp.sum(ds, axis=1, keepdims=True)

        def below(j, carry):
            block(j, False)
            return carry

        lax.fori_loop(0, qi, below, 0)
        block(qi, True)
        dq_ref[...] = (acc_s[...] * (HEAD_DIM ** -0.5)).astype(BF16)
        df_ref[...] = df_s[...]

    return pl.pallas_call(
        body, name="fox_bwd_dq",
        out_shape=(jax.ShapeDtypeStruct(dproj.shape, dproj.dtype),
                   jax.ShapeDtypeStruct((FOX_HEADS, nq, tq, 1), F32)),
        grid=(FOX_HEADS, nq),
        in_specs=[sp["rows"](0), sp["seq"](0), sp["seq"](FOX_HEADS), sp["row_all"],
                  sp["col"], sp["col"], sp["rows"](0), _ANY],
        out_specs=(sp["rows"](0), sp["col"]),
        input_output_aliases={7: 0},
        scratch_shapes=[pltpu.VMEM((tq, HEAD_DIM), F32), pltpu.VMEM((tq, 1), F32)],
        compiler_params=_params("parallel", "parallel"),
    )(proj, kv, kv, fk, lse, delta, datt, dproj)


def _fox_bwd_dkv(proj, kv, fk, lse, delta, datt):
    L = proj.shape[0]
    tq = min(FOX_BLOCK, L)
    nq = L // tq
    sp = _fox_specs(tq, L)

    def body(q_ref, k_ref, v_ref, fk_ref, lse_ref, dl_ref, do_ref,
             dk_ref, dv_ref, df_ref, dk_s, dv_s, df_s):
        ki = pl.program_id(1)
        k, v, fk = k_ref[...], v_ref[...], fk_ref[...]
        dk_s[...] = jnp.zeros_like(dk_s)
        dv_s[...] = jnp.zeros_like(dv_s)
        df_s[...] = jnp.zeros_like(df_s)

        def block(i, diagonal):
            r0 = pl.multiple_of(i * tq, tq)
            qs = (q_ref[pl.ds(r0, tq), :] * (HEAD_DIM ** -0.5)).astype(BF16)
            dob = do_ref[pl.ds(r0, tq), :].astype(BF16)
            p = jnp.exp(_fox_scores(qs, k, fk, diagonal) - lse_ref[i])
            dp = lax.dot_general(dob, v, _NT, preferred_element_type=F32)
            ds = p * (dp - dl_ref[i])
            dv_s[...] += lax.dot_general(p.astype(BF16), dob, _TN, preferred_element_type=F32)
            dk_s[...] += lax.dot_general(ds.astype(BF16), qs, _TN, preferred_element_type=F32)
            df_s[...] -= jnp.sum(ds, axis=0, keepdims=True)

        def above(i, carry):
            block(i, False)
            return carry

        block(ki, True)
        lax.fori_loop(ki + 1, nq, above, 0)
        dk_ref[...] = dk_s[...].astype(BF16)
        dv_ref[...] = dv_s[...].astype(BF16)
        df_ref[...] = df_s[...]

    return pl.pallas_call(
        body, name="fox_bwd_dkv",
        out_shape=(jax.ShapeDtypeStruct((L, MAIN_WIDTH), BF16),
                   jax.ShapeDtypeStruct((L, MAIN_WIDTH), BF16),
                   jax.ShapeDtypeStruct((FOX_HEADS, nq, 1, tq), F32)),
        grid=(FOX_HEADS, nq),
        in_specs=[sp["seq"](0), sp["rows"](0), sp["rows"](FOX_HEADS), sp["row"],
                  sp["col_all"], sp["col_all"], sp["seq"](0)],
        out_specs=(sp["rows"](0), sp["rows"](0), sp["row"]),
        scratch_shapes=[pltpu.VMEM((tq, HEAD_DIM), F32), pltpu.VMEM((tq, HEAD_DIM), F32),
                        pltpu.VMEM((1, tq), F32)],
        compiler_params=_params("parallel", "parallel"),
    )(proj, kv, kv, fk, lse, delta, datt)


def _pad_lanes(a):
    return jnp.pad(a, ((0, 0), (0, LANES - a.shape[1])))


def _mem_branch_fwd(memn, w_mk, proj, tag):
    kvm = _mm(memn, w_mk, name="mem_kv_" + tag)
    return kvm, _mem_attn_fwd(proj, kvm)


def _mem_branch_bwd(mem, g, w_mk, proj, memn, kvm, do_mem, dproj, tag):
    dproj, dkvm = _mem_attn_bwd(proj, kvm, do_mem, dproj)
    dkvm = dkvm.astype(BF16)
    dw_mk = _mm(memn, dkvm, ta=True, name="dw_mem_kv_" + tag, out_dtype=BF16)
    dmemn = _mm(dkvm, w_mk, tb=True, name="dmemn_" + tag)
    _, dg = _rmsnorm_bwd(mem, g, dmemn, name="mem_norm_bwd_" + tag, dx_dtype=BF16)
    return dproj, dw_mk, dg


def _local_step(x, mem, target, w, fetch=None, grads_ready=None):
    if grads_ready is None:
        grads_ready = lambda group, grads, token: token
    L = x.shape[0]
    g = {}
    w = dict(w)

    b_re_t = jnp.transpose(w["b_re"], (0, 2, 1))
    b_im_t = jnp.transpose(w["b_im"], (0, 2, 1))
    ar, ai, bbr_t, bbi_t = _s5_prep(w["lam_re"], w["lam_im"], w["log_step"], b_re_t, b_im_t)
    bmat, cmat = _s5_block_mats(bbr_t, bbi_t, w["c_re"], w["c_im"])
    a_rows = _s5_a_rows(ar, ai)

    hn0 = _rmsnorm_fwd(x, w["pre_norm_g"][0], name="pre_norm_0", out_dtype=BF16)
    memn0 = _rmsnorm_fwd(mem, w["mem_norm_g"][0], name="mem_norm_0", out_dtype=BF16)
    memn1 = _rmsnorm_fwd(mem, w["mem_norm_g"][1], name="mem_norm_1", out_dtype=BF16)
    if fetch is not None:
        w.update(fetch("a", [hn0, memn0, memn1, bmat, cmat, a_rows]))
    proj_a = _mm(hn0, w.get("w_in_a_sharded", w["w_in_a"]), name="in_proj_a")
    y, yg, xp = _s5_fwd(proj_a, bmat, cmat, a_rows, w["d_skip"])
    if fetch is not None:
        w.update(fetch("b", yg))
    t = _mm(yg, w["w_glu"], name="glu_proj")
    kvm0, om0 = _mem_branch_fwd(memn0, w["w_mem_kv"][0], proj_a, "0")
    cat0 = _gate_a_fwd(y, t, w["b_glu"], proj_a, om0)
    o0 = _mm(cat0, w["w_out"][0], name="out_proj_0")
    h1 = _rmsnorm_fwd(o0, w["post_norm_g"][0], res=x, name="post_norm_0")

    kv_in = _rmsnorm_fwd(h1, w["kv_norm_g"], name="kv_norm", out_dtype=BF16)
    if fetch is not None:
        w.update(fetch("c", kv_in))
    kv = _mm(kv_in, w["w_kv"], name="kv_proj", out_dtype=BF16)
    pre_f = _mm(kv_in, w["w_fgate"], name="fgate_proj")
    b_f = jnp.pad(w["b_fgate"], (0, LANES - FOX_HEADS))
    fcum = _fgate_fwd(pre_f, b_f)
    fc = jnp.transpose(fcum[:, :FOX_HEADS])
    tq = min(FOX_BLOCK, L)
    fk = fc.reshape(FOX_HEADS, L // tq, 1, tq)

    hn1 = _rmsnorm_fwd(h1, w["pre_norm_g"][1], name="pre_norm_1", out_dtype=BF16)
    proj_b = _mm(hn1, w["w_in_b"], name="in_proj_b")
    att, lse = _fox_fwd(proj_b, kv, fk)
    kvm1, om1 = _mem_branch_fwd(memn1, w["w_mem_kv"][1], proj_b, "1")
    cat1 = _gate_b_fwd(att, proj_b, om1)
    o1 = _mm(cat1, w["w_out"][1], name="out_proj_1")
    dh2, loss_row = _final_norm_loss(o1, w["post_norm_g"][1], h1, target)

    do1, dpost1 = _rmsnorm_bwd(o1, w["post_norm_g"][1], dh2, name="post_norm_bwd_1", dx_dtype=BF16)
    dcat1 = _mm(do1, w["w_out"][1], tb=True, name="dcat_1", out_dtype=BF16)
    g["w_out_1"] = _mm(cat1, do1, ta=True, name="dw_out_1", out_dtype=BF16)
    datt, dproj_b, dom1, delta = _gate_b_bwd(dcat1, att, proj_b, om1)
    dproj_b, g["w_mem_kv_1"], dmemg1 = _mem_branch_bwd(mem, w["mem_norm_g"][1], w["w_mem_kv"][1], proj_b,
                                                      memn1, kvm1, dom1, dproj_b, "1")
    delta = delta.reshape(lse.shape)
    dproj_b, dfq = _fox_bwd_dq(proj_b, kv, fk, lse, delta, datt, dproj_b)
    dk, dv, dfk = _fox_bwd_dkv(proj_b, kv, fk, lse, delta, datt)
    g["w_in_b"] = _mm(hn1, dproj_b, ta=True, name="dw_in_b", out_dtype=BF16, shards=N_CHIPS)
    dhn1 = _mm(dproj_b, w["w_in_b"], tb=True, name="dhn_1")

    dkv = jnp.concatenate([dk, dv], axis=1)
    g["w_kv"] = _mm(kv_in, dkv, ta=True, name="dw_kv", out_dtype=BF16, shards=N_CHIPS)
    dkv_in_a = _mm(dkv, w["w_kv"], tb=True, name="dkv_in_kv")
    dfcum = _pad_lanes(jnp.transpose(dfq.reshape(FOX_HEADS, L) + dfk.reshape(FOX_HEADS, L)))
    dpre_f, db_f = _fgate_bwd(dfcum, pre_f, b_f)
    g["b_fgate"] = db_f[0, :FOX_HEADS]
    g["w_fgate"] = _mm(kv_in, dpre_f, ta=True, name="dw_fgate")[:, :FOX_HEADS]
    dkv_in_b = _mm(dpre_f, w["w_fgate"], tb=True, name="dkv_in_fgate")
    dh1, g["kv_norm_g"], dpre1 = _rmsnorm_bwd_pair(h1, w["kv_norm_g"], (dkv_in_a, dkv_in_b), w["pre_norm_g"][1],
                                                   dhn1, adds=(dh2,), name="kv_pre_norm_bwd")
    dh1 = grads_ready("b", g, dh1)

    do0, dpost0 = _rmsnorm_bwd(o0, w["post_norm_g"][0], dh1, name="post_norm_bwd_0", dx_dtype=BF16)
    dcat0 = _mm(do0, w["w_out"][0], tb=True, name="dcat_0", out_dtype=BF16)
    g["w_out_0"] = _mm(cat0, do0, ta=True, name="dw_out_0", out_dtype=BF16)
    dcat0 = grads_ready("b_send", g, dcat0)
    dproj_a, dt, dyg_a, dom0, db_glu = _gate_a_bwd(dcat0, y, t, w["b_glu"], proj_a, om0)
    g["b_glu"] = db_glu[0]
    g["w_glu"] = _mm(yg, dt, ta=True, name="dw_glu", out_dtype=BF16)
    dyg_b = _mm(dt, w["w_glu"], tb=True, name="dyg")
    dproj_a, g["w_mem_kv_0"], dmemg0 = _mem_branch_bwd(mem, w["mem_norm_g"][0], w["w_mem_kv"][0], proj_a,
                                                      memn0, kvm0, dom0, dproj_a, "0")
    dyg_b = grads_ready("a1", g, dyg_b)
    dproj_a, db_blk, dc_blk, da_rows, dd_skip = _s5_bwd(proj_a, dyg_a, dyg_b, y, xp, bmat, cmat, a_rows,
                                                        w["d_skip"], dproj_a)
    dproj_a = grads_ready("a1_send", g, dproj_a)
    g["d_skip"] = dd_skip[0]
    g["w_in_a"] = _mm(hn0, dproj_a, ta=True, name="dw_in_a", out_dtype=BF16, shards=N_CHIPS)
    dproj_a = grads_ready("a2", g, dproj_a)
    dhn0 = _mm(dproj_a, w["w_in_a"], tb=True, name="dhn_0")
    grad_x, dpre0 = _rmsnorm_bwd(x, w["pre_norm_g"][0], dhn0, adds=(dh1,), name="pre_norm_bwd_0")

    dbb = _s5_unfold(db_blk)
    dcc = _s5_unfold(dc_blk)
    g["c_re"], g["c_im"] = dcc[0], -dcc[1]
    d_ar = da_rows[:, 0, :STATE_COLS].reshape(SSM_GROUPS, SSM_STATE)
    d_ai = da_rows[:, 0, STATE_COLS:].reshape(SSM_GROUPS, SSM_STATE)
    dlr, dli, dls, dbr_t, dbi_t = _s5_prep_bwd(w["lam_re"], w["lam_im"], w["log_step"], b_re_t, b_im_t,
                                               d_ar, d_ai, dbb[0], dbb[1])
    g["lam_re"], g["lam_im"], g["log_step"] = dlr, dli, dls[:, 0]
    g["b_re"] = jnp.transpose(dbr_t, (0, 2, 1))
    g["b_im"] = jnp.transpose(dbi_t, (0, 2, 1))
    g["pre_norm_g"] = jnp.stack([dpre0, dpre1])
    g["post_norm_g"] = jnp.stack([dpost0, dpost1])
    g["mem_norm_g"] = jnp.stack([dmemg0, dmemg1])
    return loss_row, grad_x, g


_MESH = pl.DeviceIdType.MESH
_ANY = pl.BlockSpec(memory_space=pl.ANY)


def _place():
    x, y, c = lax.axis_index("x"), lax.axis_index("y"), lax.axis_index("c")
    chips = [(1 - x, y), (x, 1 - y), (1 - x, 1 - y)]
    return x, y, c, chips


_HBM = pl.BlockSpec(memory_space=pltpu.HBM)
_SEM = pl.BlockSpec(memory_space=pltpu.SEMAPHORE)
_SIDE = pltpu.SideEffectType.DATAFLOW_SIDE_EFFECTING


def _in_hbm(a):
    return pltpu.with_memory_space_constraint(a, pltpu.HBM)


def _hbm_like(a):
    return pltpu.HBM(a.shape, a.dtype)


def _ici_copies(srcs, lands, send_sem, recv_sem, src_at, dst_at, wait_at, to_sibling=False):
    x, y, c, chips = _place()
    peers = [(x, y, 1 - c)] if to_sibling else [(cx, cy, c) for cx, cy in chips]
    m = len(peers)
    start, wait = [], []
    for i in range(len(srcs)):
        for k, (px, py, pc) in enumerate(peers):
            sem = dict(send_sem=send_sem.at[m * i + k], recv_sem=recv_sem.at[m * i + k],
                       device_id=(px, py, pc), device_id_type=_MESH)
            src = src_at(srcs[i], 2 * px + py, c)
            start.append(pltpu.make_async_remote_copy(src_ref=src, dst_ref=dst_at(lands[i], 2 * x + y, k, c), **sem))
            wait.append(pltpu.make_async_remote_copy(src_ref=src, dst_ref=wait_at(lands[i], 2 * px + py, k, c), **sem))
    return start, wait


def _route_peers(route):
    return 1 if len(route) == 4 else 3


_BLOCK_ROUTE = (lambda s, j, c: s, lambda l, me, k, c: l.at[me, c], lambda l, j, k, c: l.at[j, c])


def _ici_start(srcs, lands, token, route, *, name):
    n = len(srcs)

    def body(*refs):
        start, _ = _ici_copies(refs[:n], refs[n:2 * n], refs[2 * n + 1], refs[2 * n + 2], *route)
        for cp in start:
            cp.start()

    sems = pltpu.SemaphoreType.DMA((_route_peers(route) * n,))
    outs = pl.pallas_call(
        body, name=name,
        out_shape=(sems, sems, *[_hbm_like(a) for a in srcs], *[_hbm_like(a) for a in lands], _hbm_like(token)),
        in_specs=[_HBM] * (2 * n + 1), out_specs=(_SEM, _SEM, *[_HBM] * (2 * n + 1)),
        input_output_aliases={i: 2 + i for i in range(2 * n + 1)},
        compiler_params=pltpu.CompilerParams(has_side_effects=_SIDE),
    )(*[_in_hbm(a) for a in srcs], *[_in_hbm(a) for a in lands], _in_hbm(token))
    return (outs[0], outs[1], list(outs[2:2 + n]), list(outs[2 + n:2 + 2 * n])), outs[2 + 2 * n]


def _ici_wait(handle, after, route, *, name):
    send_sem, recv_sem, srcs, lands = handle
    n = len(srcs)
    after = list(after) if isinstance(after, (list, tuple)) else [after]

    def body(*refs):
        _, wait = _ici_copies(refs[:n], refs[n:2 * n], refs[2 * n], refs[2 * n + 1], *route)
        for cp in wait:
            cp.wait_send()
            cp.wait_recv()

    outs = pl.pallas_call(
        body, name=name,
        out_shape=(*[_hbm_like(a) for a in srcs], *[_hbm_like(a) for a in lands]),
        in_specs=[_HBM] * (2 * n) + [_SEM, _SEM] + [_ANY] * len(after), out_specs=tuple([_HBM] * (2 * n)),
        input_output_aliases={i: i for i in range(2 * n)},
        compiler_params=pltpu.CompilerParams(has_side_effects=_SIDE),
    )(*srcs, *lands, send_sem, recv_sem, *after)
    return list(outs[:n]), list(outs[n:])


_GATHER_ROUTE = (lambda s, j, c: s.at[c], lambda l, me, k, c: l.at[me, c], lambda l, j, k, c: l.at[j, c])
_SCATTER_ROUTE = (lambda s, j, c: s.at[j], lambda l, me, k, c: l.at[k], lambda l, j, k, c: l.at[k])
_SHARE_ROUTE = (lambda s, j, c: s, lambda l, me, k, c: l.at[c], lambda l, j, k, c: l.at[1 - c], True)
_SWAP_ROUTE = (lambda s, j, c: s.at[:, 1 - c], lambda l, me, k, c: l, lambda l, j, k, c: l, True)


def _gather_forward(lands, tag, own=False):
    n = len(lands)
    m = 4 if own else 3

    def body(*refs):
        ins, outs = refs[:n], refs[n:2 * n]
        send_sem, recv_sem = refs[2 * n:]
        x, y, c, chips = _place()
        slots = [2 * cx + cy for cx, cy in chips] + [2 * x + y]

        def copy(i, k, half):
            return pltpu.make_async_remote_copy(
                src_ref=ins[i].at[slots[k], half], dst_ref=outs[i].at[slots[k], half],
                send_sem=send_sem.at[m * i + k], recv_sem=recv_sem.at[m * i + k],
                device_id=(x, y, 1 - c), device_id_type=_MESH)

        copies = [copy(i, k, c) for i in range(n) for k in range(m)]
        for cp in copies:
            cp.start()
        for i in range(n):
            for k in range(m):
                copy(i, k, 1 - c).wait_recv()
        for cp in copies:
            cp.wait_send()

    return pl.pallas_call(
        body, name="gather_forward_to_sibling_" + tag,
        out_shape=[jax.ShapeDtypeStruct(a.shape, a.dtype) for a in lands],
        in_specs=[_ANY] * n, out_specs=[_ANY] * n,
        input_output_aliases={i: i for i in range(n)},
        scratch_shapes=[pltpu.SemaphoreType.DMA((m * n,)), pltpu.SemaphoreType.DMA((m * n,))],
    )(*lands)


def _swap_halves(grads, tag):
    n = len(grads)

    def body(*refs):
        ins, outs = refs[:n], refs[n:2 * n]
        send_sem, recv_sem = refs[2 * n:]
        x, y, c, _ = _place()
        copies = [pltpu.make_async_remote_copy(
            src_ref=ins[i].at[:, 1 - c], dst_ref=outs[i],
            send_sem=send_sem.at[i], recv_sem=recv_sem.at[i],
            device_id=(x, y, 1 - c), device_id_type=_MESH) for i in range(n)]
        for cp in copies:
            cp.start()
        for cp in copies:
            cp.wait()

    return pl.pallas_call(
        body, name="grad_swap_halves_" + tag,
        out_shape=[jax.ShapeDtypeStruct((N_CHIPS,) + g.shape[2:], g.dtype) for g in grads],
        in_specs=[_ANY] * n, out_specs=[_ANY] * n,
        scratch_shapes=[pltpu.SemaphoreType.DMA((n,)), pltpu.SemaphoreType.DMA((n,))],
    )(*grads)


def _sum_rows(h, C):
    return max(d for d in range(SUBLANES, h + 1, SUBLANES) if h % d == 0 and d * C <= 1 << 20)


SUM_STEPS = 4


def _pair_sums(gs, rs, c_idx, *, name):
    n = len(gs)
    rows = [g.shape[2] // SUM_STEPS for g in gs]

    def body(c_ref, *refs):
        for g_ref, r_ref, o_ref in zip(refs[:n], refs[n:2 * n], refs[2 * n:]):
            o_ref[...] = (g_ref[...].astype(F32) + r_ref[...].astype(F32)).astype(o_ref.dtype)

    return pl.pallas_call(
        body, name=name,
        out_shape=[jax.ShapeDtypeStruct((N_CHIPS,) + g.shape[2:], g.dtype) for g in gs],
        grid_spec=pltpu.PrefetchScalarGridSpec(
            num_scalar_prefetch=1, grid=(N_CHIPS, SUM_STEPS),
            in_specs=[pl.BlockSpec((None, None, tr, g.shape[3]), lambda j, i, s: (j, s[0], i, 0))
                      for g, tr in zip(gs, rows)]
            + [pl.BlockSpec((None, tr, g.shape[3]), lambda j, i, s: (j, i, 0)) for g, tr in zip(gs, rows)],
            out_specs=[pl.BlockSpec((None, tr, g.shape[3]), lambda j, i, s: (j, i, 0)) for g, tr in zip(gs, rows)]),
        compiler_params=_params("parallel", "parallel"),
    )(c_idx, *gs, *rs)


def _owner_sums(ss, rs, jc_idx, *, name):
    n = len(ss)
    rows = [s.shape[1] // SUM_STEPS for s in ss]

    def body(jc_ref, *refs):
        for s_ref, r_ref, m_ref, o_ref in zip(refs[:n], refs[n:2 * n], refs[2 * n:3 * n], refs[3 * n:]):
            acc = s_ref[...].astype(F32)
            for k in range(3):
                acc = acc + r_ref[k].astype(F32)
            m_ref[...] = acc
            o_ref[...] = acc

    outs = pl.pallas_call(
        body, name=name,
        out_shape=[jax.ShapeDtypeStruct(s.shape[1:], F32) for s in ss]
        + [jax.ShapeDtypeStruct((2,) + s.shape[1:], F32) for s in ss],
        grid_spec=pltpu.PrefetchScalarGridSpec(
            num_scalar_prefetch=1, grid=(SUM_STEPS,),
            in_specs=[pl.BlockSpec((None, tr, s.shape[2]), lambda i, p: (p[0], i, 0)) for s, tr in zip(ss, rows)]
            + [pl.BlockSpec((3, tr, s.shape[2]), lambda i, p: (0, i, 0)) for s, tr in zip(ss, rows)],
            out_specs=[pl.BlockSpec((tr, s.shape[2]), lambda i, p: (i, 0)) for s, tr in zip(ss, rows)]
            + [pl.BlockSpec((None, tr, s.shape[2]), lambda i, p: (p[1], i, 0)) for s, tr in zip(ss, rows)]),
        compiler_params=_params("parallel"),
    )(jc_idx, *ss, *rs)
    return outs[:n], outs[n:]


def _chip_sums(grads, c_idx, tag):
    views = [g.reshape(N_CHIPS, 2, g.shape[1] // 2, g.shape[2]) for g in grads]
    arrived = _swap_halves(views, tag)
    return _pair_sums(views, arrived, c_idx, name=f"grad_pair_sums_{tag}")


def _sum_devices(blocks):
    R = blocks.shape[2]
    tr = _sum_rows(R, 2 * N_CHIPS * LANES)

    def body(b_ref, o_ref):
        acc = b_ref[0, 0]
        for d in range(1, 2 * N_CHIPS):
            acc = acc + b_ref[d // 2, d % 2]
        o_ref[...] = acc

    return pl.pallas_call(
        body, name="sum_small_over_devices", out_shape=jax.ShapeDtypeStruct((R, LANES), F32),
        grid=(R // tr,),
        in_specs=[pl.BlockSpec((N_CHIPS, 2, tr, LANES), lambda i: (0, 0, i, 0))],
        out_specs=pl.BlockSpec((tr, LANES), lambda i: (i, 0)),
        compiler_params=_params("parallel"),
    )(blocks)


def _adamw(w, g, m, v, *, name):
    R, C = w.shape
    whole_fits = 7 * 2 * R * C * 4 <= VMEM_LIMIT_BYTES // 2
    tr = R if whole_fits else next(c for c in (256, 192, 128, 64, 32, 16, 8) if R % c == 0)

    def body(w_ref, g_ref, m_ref, v_ref, d_ref, nm_ref, nv_ref):
        g = g_ref[...]
        m = ADAM_B1 * m_ref[...] + (1.0 - ADAM_B1) * g
        v = ADAM_B2 * v_ref[...] + (1.0 - ADAM_B2) * (g * g)
        nm_ref[...] = m
        nv_ref[...] = v
        m_hat = m / (1.0 - ADAM_B1 ** ADAM_STEP)
        v_hat = v / (1.0 - ADAM_B2 ** ADAM_STEP)
        d_ref[...] = -ADAM_LR * (m_hat / (jnp.sqrt(v_hat) + ADAM_EPS) + ADAM_WD * w_ref[...])

    blk = pl.BlockSpec((tr, C), lambda i: (i, 0))
    sds = jax.ShapeDtypeStruct((R, C), F32)
    return pl.pallas_call(
        body, name=name, out_shape=(sds, sds, sds), grid=(R // tr,),
        in_specs=[blk] * 4, out_specs=(blk, blk, blk),
        compiler_params=_params("parallel"),
    )(w, g, m, v)


_TILE = SUBLANES * LANES


def _pack(arrays):
    rows = []
    for a in arrays:
        flat = a.reshape(-1)
        flat = jnp.pad(flat, (0, (-flat.shape[0]) % _TILE))
        rows.append(flat.reshape(-1, LANES))
    return jnp.concatenate(rows, axis=0)


def _unpack(buf, shapes):
    out, r = [], 0
    for s in shapes:
        size = math.prod(s)
        nr = -(-size // _TILE) * SUBLANES
        out.append(buf[r:r + nr].reshape(-1)[:size].reshape(s))
        r += nr
    return out


_BIG = ("w_in_a", "w_glu", "w_kv", "w_in_b", "w_mem_kv", "w_out")
_REPLICATED = ("pre_norm_g", "post_norm_g", "lam_re", "lam_im", "log_step", "b_re", "b_im", "c_re", "c_im",
               "kv_norm_g", "b_fgate", "mem_norm_g")
_SHARDED_SMALL = ("d_skip", "b_glu", "w_fgate")
_WEIGHTS = ("pre_norm_g", "post_norm_g", "w_in_a", "lam_re", "lam_im", "log_step", "b_re", "b_im", "c_re",
            "c_im", "d_skip", "w_glu", "b_glu", "kv_norm_g", "w_kv", "w_fgate", "b_fgate", "w_in_b",
            "mem_norm_g", "w_mem_kv", "w_out")


def _halves(a):
    return a.reshape(2, a.shape[0] // 2, a.shape[1])


def _unhalve(a):
    return a.reshape(N_CHIPS, 2 * a.shape[2], a.shape[3])


def _columns(a):
    return jnp.transpose(a, (1, 0, 2)).reshape(a.shape[1], N_CHIPS * a.shape[2])


def kernel(x, mem, pre_norm_g, post_norm_g, w_in_a, lam_re, lam_im, log_step, b_re, b_im, c_re, c_im, d_skip, w_glu, b_glu, kv_norm_g, w_kv, w_fgate, b_fgate, w_in_b, mem_norm_g, w_mem_kv, w_out, loss_target, m_pre_norm_g, m_post_norm_g, m_w_in_a, m_lam_re, m_lam_im, m_log_step, m_b_re, m_b_im, m_c_re, m_c_im, m_d_skip, m_w_glu, m_b_glu, m_kv_norm_g, m_w_kv, m_w_fgate, m_b_fgate, m_w_in_b, m_mem_norm_g, m_w_mem_kv, m_w_out, v_pre_norm_g, v_post_norm_g, v_w_in_a, v_lam_re, v_lam_im, v_log_step, v_b_re, v_b_im, v_c_re, v_c_im, v_d_skip, v_w_glu, v_b_glu, v_kv_norm_g, v_w_kv, v_w_fgate, v_b_fgate, v_w_in_b, v_mem_norm_g, v_w_mem_kv, v_w_out):
    a = dict(locals())
    xi, yi, ci = lax.axis_index("x"), lax.axis_index("y"), lax.axis_index("c")
    chip = 2 * xi + yi
    c_idx = jnp.reshape(ci, (1,)).astype(jnp.int32)
    jc_idx = jnp.stack([chip, ci]).astype(jnp.int32)

    vec = jnp.zeros((2 * SUBLANES, MAIN_WIDTH // N_CHIPS), F32)
    vec = vec.at[0].set(a["d_skip"][0]).at[1].set(a["b_glu"][0])
    def own_slot(gathered, parts):
        return [lax.dynamic_update_index_in_dim(g, p, chip, 0) for g, p in zip(gathered, parts)]

    parts_a = [_halves(a["w_in_a"][0].astype(BF16)), _halves(vec)]
    parts_b = [_halves(a["w_glu"][0].astype(BF16)),
               *[_halves(a["w_mem_kv"][i].astype(BF16)) for i in range(2)],
               *[_halves(a["w_out"][i].astype(BF16)) for i in range(2)]]
    parts_c = [_halves(a["w_kv"].astype(BF16)), _halves(_pad_lanes(a["w_fgate"]).astype(BF16)),
               _halves(a["w_in_b"][0].astype(BF16))]
    travelling, token = {}, a["pre_norm_g"]
    for tag, parts in (("a", parts_a), ("b", parts_b), ("c", parts_c)):
        lands = [lax.empty((N_CHIPS,) + p.shape, p.dtype) for p in parts]
        travelling[tag], token = _ici_start(parts, lands, token, _GATHER_ROUTE, name=f"gather_{tag}_start")

    def fetch(tag, after):
        parts, lands = _ici_wait(travelling[tag], after, _GATHER_ROUTE, name=f"gather_{tag}_wait")
        full = own_slot(_gather_forward(lands, tag), parts)
        if tag == "a":
            w_in_a, vecs = full
            return dict(w_in_a_sharded=_unhalve(w_in_a), w_in_a=_columns(_unhalve(w_in_a)),
                        d_skip=vecs[:, 0, 0, :].reshape(MAIN_WIDTH), b_glu=vecs[:, 0, 1, :].reshape(MAIN_WIDTH))
        if tag == "b":
            w_glu, w_mk0, w_mk1, w_out0, w_out1 = full
            return dict(w_glu=w_glu.reshape(MAIN_WIDTH, MAIN_WIDTH),
                        w_mem_kv=[m.reshape(D_MODEL, 2 * MEM_WIDTH) for m in (w_mk0, w_mk1)],
                        w_out=[o.reshape(D_MODEL, D_MODEL) for o in (w_out0, w_out1)])
        w_kv, w_fg, w_in_b = full
        return dict(w_kv=_columns(_unhalve(w_kv)), w_fgate=w_fg.reshape(D_MODEL, LANES),
                    w_in_b=_columns(_unhalve(w_in_b)))

    w = dict(
        pre_norm_g=token, post_norm_g=a["post_norm_g"], mem_norm_g=a["mem_norm_g"],
        kv_norm_g=a["kv_norm_g"], b_fgate=a["b_fgate"],
        lam_re=a["lam_re"][0], lam_im=a["lam_im"][0], log_step=a["log_step"][0],
        b_re=a["b_re"][0], b_im=a["b_im"][0], c_re=a["c_re"][0], c_im=a["c_im"][0])

    sent = {}

    swapping = {}

    def grads_ready(event, g, token):
        tag = event.split("_")[0]
        if event in ("b", "a1"):
            big = {"b": lambda: [g["w_kv"], g["w_in_b"], g["w_mem_kv_1"].reshape(N_CHIPS, -1, 2 * MEM_WIDTH),
                                 g["w_out_1"].reshape(N_CHIPS, -1, D_MODEL)],
                   "a1": lambda: [g["w_glu"].reshape(N_CHIPS, -1, MAIN_WIDTH),
                                  g["w_mem_kv_0"].reshape(N_CHIPS, -1, 2 * MEM_WIDTH),
                                  g["w_out_0"].reshape(N_CHIPS, -1, D_MODEL)]}[tag]()
            views = [b.reshape(N_CHIPS, 2, b.shape[1] // 2, b.shape[2]) for b in big]
            lands = [lax.empty((N_CHIPS,) + v.shape[2:], v.dtype) for v in views]
            swapping[tag], token = _ici_start(views, lands, token, _SWAP_ROUTE, name=f"grad_swap_{tag}_start")
            return token
        if event == "a2":
            sums = _chip_sums([g["w_in_a"]], c_idx, tag)
        else:
            views, arrived = _ici_wait(swapping[tag], token, _SWAP_ROUTE, name=f"grad_swap_{tag}_wait")
            sums = _pair_sums(views, arrived, c_idx, name=f"grad_pair_sums_{tag}")
        lands = [lax.empty((3,) + s.shape[1:], s.dtype) for s in sums]
        sent[tag], token = _ici_start(sums, lands, token, _SCATTER_ROUTE, name=f"grad_send_{tag}_start")
        return token

    loss_row, grad_x, g = _local_step(a["x"][0], a["mem"][0], a["loss_target"][0], w, fetch, grads_ready)

    small_names = _REPLICATED + _SHARDED_SMALL
    pack = _pack([g[n] for n in small_names])
    blocks = lax.empty((N_CHIPS, 2) + pack.shape, F32)
    small_sent, token = _ici_start([pack], [blocks], loss_row, _BLOCK_ROUTE, name="small_sums_start")

    sharing = {}
    for tag in ("b", "a1", "a2"):
        sums, arrived = _ici_wait(sent[tag], [grad_x, token], _SCATTER_ROUTE, name=f"grad_send_{tag}_wait")
        mine, bufs = _owner_sums(sums, arrived, jc_idx, name=f"grad_owner_sums_{tag}")
        sharing[tag], token = _ici_start(mine, bufs, token, _SHARE_ROUTE, name=f"grad_share_{tag}_start")
    loss = lax.psum(jnp.sum(token), MESH_AXES)

    def shared(tag, after):
        _, bufs = _ici_wait(sharing[tag], after, _SHARE_ROUTE, name=f"grad_share_{tag}_wait")
        return [b.reshape(-1, b.shape[2]) for b in bufs]

    grads, delta, new_m, new_v = {}, {}, {}, {}

    def adam(n):
        shape = a[n].shape
        d2 = (-1, shape[-1])
        d, m, v = _adamw(a[n].reshape(d2), grads[n].reshape(d2), a["m_" + n].reshape(d2),
                         a["v_" + n].reshape(d2), name="adamw_" + n)
        delta[n], new_m[n], new_v[n] = d.reshape(shape), m.reshape(shape), v.reshape(shape)
        return d

    r_kv, r_in_b, r_mk1, r_out1 = shared("b", token)
    grads["w_kv"], grads["w_in_b"] = r_kv, r_in_b[None]
    done = [adam("w_kv"), adam("w_in_b")]
    r_glu, r_mk0, r_out0 = shared("a1", done)
    grads["w_glu"], grads["w_mem_kv"], grads["w_out"] = r_glu[None], jnp.stack([r_mk0, r_mk1]), jnp.stack([r_out0, r_out1])
    done = [adam("w_glu"), adam("w_mem_kv"), adam("w_out")]
    (r_in_a,) = shared("a2", done)
    grads["w_in_a"] = r_in_a[None]
    adam("w_in_a")

    (pack,), (blocks,) = _ici_wait(small_sent, [delta[n] for n in _BIG], _BLOCK_ROUTE, name="small_sums_wait")
    blocks = lax.dynamic_update_slice(blocks, pack[None, None], (chip, ci, 0, 0))
    (blocks,) = _gather_forward([blocks], "small", own=True)
    small = dict(zip(small_names, _unpack(_sum_devices(blocks), [g[n].shape for n in small_names])))
    for n in _REPLICATED:
        grads[n] = small[n].reshape(a[n].shape)
    nd = MAIN_WIDTH // N_CHIPS
    grads["d_skip"] = lax.dynamic_slice(small["d_skip"], (chip * nd,), (nd,))[None]
    grads["b_glu"] = lax.dynamic_slice(small["b_glu"], (chip * nd,), (nd,))[None]
    nf = D_MODEL // N_CHIPS
    grads["w_fgate"] = lax.dynamic_slice(small["w_fgate"], (chip * nf, 0), (nf, FOX_HEADS))

    shapes = [a[n].shape for n in small_names]
    d, m, v = _adamw(_pack([a[n] for n in small_names]), _pack([grads[n] for n in small_names]),
                     _pack([a["m_" + n] for n in small_names]), _pack([a["v_" + n] for n in small_names]),
                     name="adamw_small")
    for n, dd, mm, vv in zip(small_names, _unpack(d, shapes), _unpack(m, shapes), _unpack(v, shapes)):
        delta[n], new_m[n], new_v[n] = dd, mm, vv

    return (loss, grad_x[None], *[grads[n] for n in _WEIGHTS], *[delta[n] for n in _WEIGHTS],
            *[new_m[n] for n in _WEIGHTS], *[new_v[n] for n in _WEIGHTS])
```

```python
import math

import jax
import jax.numpy as jnp
from jax import lax
from jax.experimental import pallas as pl
from jax.experimental.pallas import tpu as pltpu

F32 = jnp.float32
BF16 = jnp.bfloat16

D_MODEL = 2048
N_MEM = 256
MAIN_WIDTH = 1536
MEM_WIDTH = 512
IN_WIDTH = 2 * MAIN_WIDTH + 2 * MEM_WIDTH
HEAD_DIM = 128
FOX_HEADS = MAIN_WIDTH // HEAD_DIM
MEM_HEADS = MEM_WIDTH // HEAD_DIM
SSM_GROUP = 16
SSM_GROUPS = MAIN_WIDTH // SSM_GROUP
SSM_STATE = 64
GROUPS_PER_BLOCK = 8
SSM_BLOCKS = SSM_GROUPS // GROUPS_PER_BLOCK
STATE_COLS = GROUPS_PER_BLOCK * SSM_STATE
EPS = 1e-6
ADAM_LR = 0.001
ADAM_B1 = 0.9
ADAM_B2 = 0.999
ADAM_EPS = 1e-08
ADAM_WD = 0.01
ADAM_STEP = 10
N_CHIPS = 4
LANES = 128
SUBLANES = 8
VMEM_LIMIT_BYTES = 56 * 1024 * 1024
NEG_BIG = -1e30
MESH_AXES = ("x", "y", "c")


def _params(*sem):
    return pltpu.CompilerParams(dimension_semantics=sem if sem else None,
                                vmem_limit_bytes=VMEM_LIMIT_BYTES)


def _sigmoid(x):
    return 1.0 / (1.0 + jnp.exp(-x))


def _gelu(x):
    c = math.sqrt(2.0 / math.pi)
    return 0.5 * x * (1.0 + jnp.tanh(c * (x + 0.044715 * (x * x * x))))


def _gelu_grad(x):
    c = math.sqrt(2.0 / math.pi)
    t = jnp.tanh(c * (x + 0.044715 * (x * x * x)))
    return 0.5 * (1.0 + t) + 0.5 * x * (1.0 - t * t) * (c * (1.0 + 3.0 * 0.044715 * (x * x)))


def _silu_and_grad(z):
    s = _sigmoid(z)
    return z * s, s * (1.0 + z * (1.0 - s))


_TILE_CHOICES = (4096, 3072, 2048, 1536, 1024, 768, 512, 384, 256, LANES)


def _tile(n, cap):
    return next(c for c in _TILE_CHOICES if c <= cap and n % c == 0)


def _mm(a, b, *, name, ta=False, tb=False, out_dtype=F32, shards=1, tm=1024, tn=1024, tk=4096):
    if ta:
        K, M = a.shape
    else:
        M, K = a.shape
    if tb:
        N, kb = b.shape
    else:
        kb, N = b.shape
    assert K == kb, (a.shape, b.shape)
    ns = N // shards
    tm, tn, tk = _tile(M, tm), _tile(ns, tn), _tile(K, tk)
    assert M % tm == 0 and ns % tn == 0 and K % tk == 0 and N % shards == 0
    nk = K // tk
    dn = (((0 if ta else 1,), (1 if tb else 0,)), ((), ()))

    def body(a_ref, b_ref, o_ref, *acc):
        prod = lax.dot_general(a_ref[...].astype(BF16), b_ref[...].astype(BF16), dn, preferred_element_type=F32)
        if nk == 1:
            o_ref[...] = prod.astype(o_ref.dtype)
            return
        acc_ref, = acc
        k = pl.program_id(2)

        @pl.when(k == 0)
        def _():
            acc_ref[...] = jnp.zeros_like(acc_ref)

        acc_ref[...] += prod

        @pl.when(k == nk - 1)
        def _():
            o_ref[...] = acc_ref[...].astype(o_ref.dtype)

    a_spec = (pl.BlockSpec((tk, tm), lambda i, j, k: (k, i)) if ta
              else pl.BlockSpec((tm, tk), lambda i, j, k: (i, k)))
    b_spec = (pl.BlockSpec((tn, tk), lambda i, j, k: (j, k)) if tb
              else pl.BlockSpec((tk, tn), lambda i, j, k: (k, j)))
    if shards == 1:
        out_shape = jax.ShapeDtypeStruct((M, N), out_dtype)
        o_spec = pl.BlockSpec((tm, tn), lambda i, j, k: (i, j))
    else:
        nb = ns // tn
        out_shape = jax.ShapeDtypeStruct((shards, M, ns), out_dtype)
        o_spec = pl.BlockSpec((None, tm, tn), lambda i, j, k: (j // nb, i, j % nb))
    return pl.pallas_call(
        body, name=name, out_shape=out_shape,
        grid=(M // tm, N // tn, nk),
        in_specs=[a_spec, b_spec], out_specs=o_spec,
        scratch_shapes=[] if nk == 1 else [pltpu.VMEM((tm, tn), F32)],
        compiler_params=_params("parallel", "parallel", "arbitrary"),
    )(a, b)


def _rmsnorm_fwd(x, g, *, name, res=None, out_dtype=F32, tr=256):
    L, D = x.shape
    tr = min(tr, L)
    has_res = res is not None

    def body(*refs):
        if has_res:
            x_ref, g_ref, r_ref, o_ref = refs
        else:
            x_ref, g_ref, o_ref = refs
        xf = x_ref[...]
        r = lax.rsqrt(jnp.mean(xf * xf, axis=-1, keepdims=True) + EPS)
        y = xf * r * g_ref[...]
        if has_res:
            y = r_ref[...] + y
        o_ref[...] = y.astype(o_ref.dtype)

    row = pl.BlockSpec((tr, D), lambda i: (i, 0))
    vec = pl.BlockSpec((1, D), lambda i: (0, 0))
    ins = [x, g.reshape(1, D)] + ([res] if has_res else [])
    return pl.pallas_call(
        body, name=name, out_shape=jax.ShapeDtypeStruct((L, D), out_dtype),
        grid=(L // tr,), in_specs=[row, vec] + ([row] if has_res else []), out_specs=row,
        compiler_params=_params("parallel"),
    )(*ins)


def _rmsnorm_bwd(x, g, dy, *, name, adds=(), dx_dtype=F32, tr=256):
    L, D = x.shape
    tr = min(tr, L)
    dys = dy if isinstance(dy, tuple) else (dy,)
    n_dy, n_add = len(dys), len(adds)

    def body(*refs):
        x_ref, g_ref = refs[:2]
        dy_refs = refs[2:2 + n_dy]
        add_refs = refs[2 + n_dy:2 + n_dy + n_add]
        dx_ref, dg_ref = refs[2 + n_dy + n_add:]
        xf = x_ref[...]
        dyf = dy_refs[0][...].astype(F32)
        for d_ref in dy_refs[1:]:
            dyf = dyf + d_ref[...].astype(F32)
        r = lax.rsqrt(jnp.mean(xf * xf, axis=-1, keepdims=True) + EPS)
        gy = dyf * g_ref[...]
        c = jnp.mean(xf * gy, axis=-1, keepdims=True) * (r * r * r)
        dx = gy * r - xf * c
        for a_ref in add_refs:
            dx = dx + a_ref[...].astype(F32)
        dx_ref[...] = dx.astype(dx_ref.dtype)

        @pl.when(pl.program_id(0) == 0)
        def _():
            dg_ref[...] = jnp.zeros_like(dg_ref)

        dg_ref[...] += jnp.sum(dyf * xf * r, axis=0, keepdims=True)

    row = pl.BlockSpec((tr, D), lambda i: (i, 0))
    vec = pl.BlockSpec((1, D), lambda i: (0, 0))
    dx, dg = pl.pallas_call(
        body, name=name,
        out_shape=(jax.ShapeDtypeStruct((L, D), dx_dtype), jax.ShapeDtypeStruct((1, D), F32)),
        grid=(L // tr,), in_specs=[row, vec] + [row] * (n_dy + n_add), out_specs=(row, vec),
        compiler_params=_params("arbitrary"),
    )(x, g.reshape(1, D), *dys, *adds)
    return dx, dg.reshape(D)


def _rmsnorm_bwd_pair(x, g1, dy1, g2, dy2, *, name, adds=(), tr=256):
    L, D = x.shape
    tr = min(tr, L)
    dy1s = dy1 if isinstance(dy1, tuple) else (dy1,)
    n1, n_add = len(dy1s), len(adds)

    def body(*refs):
        x_ref, g1_ref, g2_ref = refs[:3]
        dy1_refs = refs[3:3 + n1]
        dy2_ref = refs[3 + n1]
        add_refs = refs[4 + n1:4 + n1 + n_add]
        dx_ref, dg1_ref, dg2_ref = refs[4 + n1 + n_add:]
        xf = x_ref[...]
        d1 = dy1_refs[0][...].astype(F32)
        for d_ref in dy1_refs[1:]:
            d1 = d1 + d_ref[...].astype(F32)
        d2 = dy2_ref[...].astype(F32)
        r = lax.rsqrt(jnp.mean(xf * xf, axis=-1, keepdims=True) + EPS)
        gy = d1 * g1_ref[...] + d2 * g2_ref[...]
        c = jnp.mean(xf * gy, axis=-1, keepdims=True) * (r * r * r)
        dx = gy * r - xf * c
        for a_ref in add_refs:
            dx = dx + a_ref[...].astype(F32)
        dx_ref[...] = dx

        @pl.when(pl.program_id(0) == 0)
        def _():
            dg1_ref[...] = jnp.zeros_like(dg1_ref)
            dg2_ref[...] = jnp.zeros_like(dg2_ref)

        xr = xf * r
        dg1_ref[...] += jnp.sum(d1 * xr, axis=0, keepdims=True)
        dg2_ref[...] += jnp.sum(d2 * xr, axis=0, keepdims=True)

    row = pl.BlockSpec((tr, D), lambda i: (i, 0))
    vec = pl.BlockSpec((1, D), lambda i: (0, 0))
    dx, dg1, dg2 = pl.pallas_call(
        body, name=name,
        out_shape=(jax.ShapeDtypeStruct((L, D), F32), jax.ShapeDtypeStruct((1, D), F32),
                   jax.ShapeDtypeStruct((1, D), F32)),
        grid=(L // tr,), in_specs=[row, vec, vec] + [row] * (n1 + 1 + n_add), out_specs=(row, vec, vec),
        compiler_params=_params("arbitrary"),
    )(x, g1.reshape(1, D), g2.reshape(1, D), *dy1s, dy2, *adds)
    return dx, dg1.reshape(D), dg2.reshape(D)


def _final_norm_loss(o, g, res, target, *, tr=256):
    L, D = o.shape
    tr = min(tr, L)

    def body(o_ref, g_ref, r_ref, t_ref, dh_ref, loss_ref):
        xf = o_ref[...]
        r = lax.rsqrt(jnp.mean(xf * xf, axis=-1, keepdims=True) + EPS)
        e = (r_ref[...] + xf * r * g_ref[...]) - t_ref[...]
        dh_ref[...] = e * (1.0 / D)

        @pl.when(pl.program_id(0) == 0)
        def _():
            loss_ref[...] = jnp.zeros_like(loss_ref)

        loss_ref[...] += jnp.sum(e * e, axis=0, keepdims=True) * (0.5 / D)

    row = pl.BlockSpec((tr, D), lambda i: (i, 0))
    vec = pl.BlockSpec((1, D), lambda i: (0, 0))
    dh, lp = pl.pallas_call(
        body, name="post_norm_1_loss",
        out_shape=(jax.ShapeDtypeStruct((L, D), F32), jax.ShapeDtypeStruct((1, D), F32)),
        grid=(L // tr,), in_specs=[row, vec, row, row], out_specs=(row, vec),
        compiler_params=_params("arbitrary"),
    )(o, g.reshape(1, D), res, target)
    return dh, lp


def _s5_coeffs(lr, li, ls):
    dt = jnp.exp(ls)
    mag = jnp.exp(lr * dt)
    ar = mag * jnp.cos(li * dt)
    ai = mag * jnp.sin(li * dt)
    den = lr * lr + li * li
    cr = ((ar - 1.0) * lr + ai * li) / den
    ci = (ai * lr - (ar - 1.0) * li) / den
    return dt, ar, ai, den, cr, ci


def _s5_prep(lam_re, lam_im, log_step, b_re_t, b_im_t):
    G, P = lam_re.shape
    H = b_re_t.shape[1]

    def body(lr_ref, li_ref, ls_ref, br_ref, bi_ref, ar_ref, ai_ref, bbr_ref, bbi_ref):
        _, ar, ai, _, cr, ci = _s5_coeffs(lr_ref[...], li_ref[...], ls_ref[...])
        ar_ref[...] = ar
        ai_ref[...] = ai
        br, bi = br_ref[...], bi_ref[...]
        crb, cib = cr[:, None, :], ci[:, None, :]
        bbr_ref[...] = crb * br - cib * bi
        bbi_ref[...] = crb * bi + cib * br

    return pl.pallas_call(
        body, name="s5_prep",
        out_shape=(jax.ShapeDtypeStruct((G, P), F32), jax.ShapeDtypeStruct((G, P), F32),
                   jax.ShapeDtypeStruct((G, H, P), F32), jax.ShapeDtypeStruct((G, H, P), F32)),
        compiler_params=_params(),
    )(lam_re, lam_im, log_step.reshape(G, 1), b_re_t, b_im_t)


def _s5_prep_bwd(lam_re, lam_im, log_step, b_re_t, b_im_t, d_ar, d_ai, d_bbr, d_bbi):
    G, P = lam_re.shape
    H = b_re_t.shape[1]

    def body(lr_ref, li_ref, ls_ref, br_ref, bi_ref, dar_ref, dai_ref, dbbr_ref, dbbi_ref,
             dlr_ref, dli_ref, dls_ref, dbr_ref, dbi_ref):
        lr, li = lr_ref[...], li_ref[...]
        dt, ar, ai, den, cr, ci = _s5_coeffs(lr, li, ls_ref[...])
        br, bi = br_ref[...], bi_ref[...]
        gbr, gbi = dbbr_ref[...], dbbi_ref[...]
        crb, cib = cr[:, None, :], ci[:, None, :]
        dbr_ref[...] = crb * gbr + cib * gbi
        dbi_ref[...] = crb * gbi - cib * gbr
        gcr = jnp.sum(br * gbr + bi * gbi, axis=1)
        gci = jnp.sum(br * gbi - bi * gbr, axis=1)
        ilr, ili = lr / den, -li / den
        gar = dar_ref[...] + (ilr * gcr + ili * gci)
        gai = dai_ref[...] + (ilr * gci - ili * gcr)
        qr, qi = cr * ilr - ci * ili, cr * ili + ci * ilr
        glr = -(qr * gcr + qi * gci)
        gli = -(qr * gci - qi * gcr)
        glr = glr + dt * (ar * gar + ai * gai)
        gli = gli + dt * (ar * gai - ai * gar)
        wr, wi = lr * ar - li * ai, lr * ai + li * ar
        gdt = jnp.sum(wr * gar + wi * gai, axis=1, keepdims=True)
        dlr_ref[...] = glr
        dli_ref[...] = gli
        dls_ref[...] = gdt * dt

    return pl.pallas_call(
        body, name="s5_prep_bwd",
        out_shape=(jax.ShapeDtypeStruct((G, P), F32), jax.ShapeDtypeStruct((G, P), F32),
                   jax.ShapeDtypeStruct((G, 1), F32),
                   jax.ShapeDtypeStruct((G, H, P), F32), jax.ShapeDtypeStruct((G, H, P), F32)),
        compiler_params=_params(),
    )(lam_re, lam_im, log_step.reshape(G, 1), b_re_t, b_im_t, d_ar, d_ai, d_bbr, d_bbi)


def _s5_block_mats(bbr_t, bbi_t, c_re, c_im):
    bmat = _s5_expand(bbr_t, bbi_t)
    cmat = jnp.transpose(_s5_expand(c_re, -c_im), (0, 2, 1))
    return bmat.astype(BF16), cmat.astype(BF16)


def _s5_diag_mask():
    r = lax.broadcasted_iota(jnp.int32, (LANES, 2 * STATE_COLS), 0) // SSM_GROUP
    c = (lax.broadcasted_iota(jnp.int32, (LANES, 2 * STATE_COLS), 1) % STATE_COLS) // SSM_STATE
    return (r == c).astype(F32)


def _s5_expand(re, im):
    re = jnp.tile(re.reshape(SSM_BLOCKS, LANES, SSM_STATE), (1, 1, GROUPS_PER_BLOCK))
    im = jnp.tile(im.reshape(SSM_BLOCKS, LANES, SSM_STATE), (1, 1, GROUPS_PER_BLOCK))
    return jnp.concatenate([re, im], axis=-1) * _s5_diag_mask()[None]


def _s5_unfold(dmat):
    d = dmat.reshape(SSM_GROUPS, SSM_GROUP, 2, SSM_STATE)
    return jnp.transpose(d, (2, 0, 1, 3))


def _s5_a_rows(ar, ai):
    a = jnp.concatenate([ar.reshape(SSM_BLOCKS, STATE_COLS), ai.reshape(SSM_BLOCKS, STATE_COLS)], axis=1)
    return jnp.broadcast_to(a[:, None, :], (SSM_BLOCKS, SUBLANES, 2 * STATE_COLS))


def _to_step_major(src_ref, dst_ref, seg):
    for s in range(SUBLANES):
        dst_ref[pl.ds(s, seg, stride=SUBLANES), :] = src_ref[pl.ds(seg * s, seg), :]


def _segment_rows(ref, s, seg):
    return ref[pl.ds(s, seg, stride=SUBLANES), :]


def _cmul(ar, ai, xr, xi):
    return ar * xr - ai * xi, ar * xi + ai * xr


def _s5_tables(a_ref, pw_s, pwr_s, S, seg):
    ar, ai = a_ref[:, :S], a_ref[:, S:]

    def step(i, c):
        pr, pi = c
        pw_s[i, :, :S] = pr
        pw_s[i, :, S:] = pi
        nr, ni = _cmul(ar, ai, pr, pi)
        pwr_s[seg - 1 - i, :, :S] = nr
        pwr_s[seg - 1 - i, :, S:] = ni
        return nr, ni

    pr, pi = lax.fori_loop(0, seg, step, (jnp.ones_like(ar), jnp.zeros_like(ai)))
    pw_s[seg, :, :S] = pr
    pw_s[seg, :, S:] = pi


def _s5_fwd(proj, bmat, cmat, a_rows, d_skip, *, tc=512):
    L = proj.shape[0]
    tc = min(tc, L)
    nt = L // tc
    seg = tc // SUBLANES
    S = STATE_COLS

    def body(u_ref, b_ref, c_ref, a_ref, d_ref, y_ref, yg_ref, xp_ref,
             bu_s, xp_s, pw_s, pwr_s, carry_s, e_s, up_s, yc_s):
        @pl.when(pl.program_id(1) == 0)
        def _():
            carry_s[...] = jnp.zeros_like(carry_s)
            _s5_tables(a_ref, pw_s, pwr_s, S, seg)

        ar, ai = a_ref[:, :S], a_ref[:, S:]
        _to_step_major(u_ref, up_s, seg)
        bu = jnp.dot(up_s[...].astype(BF16), b_ref[...], preferred_element_type=F32)
        bu_s[...] = bu.reshape(seg, SUBLANES, 2 * S)

        def step(i, carry):
            cr, ci = carry
            xp_s[i, :, :S] = cr
            xp_s[i, :, S:] = ci
            return ar * cr - ai * ci + bu_s[i, :, :S], ar * ci + ai * cr + bu_s[i, :, S:]

        zero = jnp.zeros((SUBLANES, S), F32)
        fr, fi = lax.fori_loop(0, seg, step, (zero, zero))
        pr, pi = pw_s[seg, 0:1, :S], pw_s[seg, 0:1, S:]
        er, ei = carry_s[0:1, :S], carry_s[0:1, S:]
        for s in range(SUBLANES):
            e_s[s:s + 1, :S] = er
            e_s[s:s + 1, S:] = ei
            tr, ti = _cmul(pr, pi, er, ei)
            er, ei = fr[s:s + 1] + tr, fi[s:s + 1] + ti
        carry_s[0:1, :S] = er
        carry_s[0:1, S:] = ei
        pw = pw_s[0:seg]
        tr, ti = _cmul(pw[:, :, :S], pw[:, :, S:], e_s[:, :S][None], e_s[:, S:][None])
        xl = xp_s[...]
        xp = jnp.concatenate([xl[:, :, :S] + tr, xl[:, :, S:] + ti], axis=-1).reshape(tc, 2 * S)
        xp_ref[...] = xp
        a1r, a1i = ar[0:1], ai[0:1]
        x_re = a1r * xp[:, :S] - a1i * xp[:, S:] + bu[:, :S]
        x_im = a1r * xp[:, S:] + a1i * xp[:, :S] + bu[:, S:]
        xs = jnp.concatenate([x_re, x_im], axis=1).astype(BF16)
        yc_s[...] = jnp.dot(xs, c_ref[...], preferred_element_type=F32)
        for s in range(SUBLANES):
            rows = pl.ds(seg * s, seg)
            y = _segment_rows(yc_s, s, seg) + d_ref[...] * u_ref[rows, :]
            y_ref[rows, :] = y
            yg_ref[rows, :] = _gelu(y).astype(BF16)

    return pl.pallas_call(
        body, name="s5_fwd",
        out_shape=(jax.ShapeDtypeStruct((L, MAIN_WIDTH), F32),
                   jax.ShapeDtypeStruct((L, MAIN_WIDTH), BF16),
                   jax.ShapeDtypeStruct((L, SSM_BLOCKS * 2 * S), F32)),
        grid=(SSM_BLOCKS, nt),
        in_specs=[pl.BlockSpec((tc, LANES), lambda b, t: (t, b)),
                  pl.BlockSpec((None, LANES, 2 * S), lambda b, t: (b, 0, 0)),
                  pl.BlockSpec((None, 2 * S, LANES), lambda b, t: (b, 0, 0)),
                  pl.BlockSpec((None, SUBLANES, 2 * S), lambda b, t: (b, 0, 0)),
                  pl.BlockSpec((1, LANES), lambda b, t: (0, b))],
        out_specs=(pl.BlockSpec((tc, LANES), lambda b, t: (t, b)),
                   pl.BlockSpec((tc, LANES), lambda b, t: (t, b)),
                   pl.BlockSpec((tc, 2 * S), lambda b, t: (t, b))),
        scratch_shapes=[pltpu.VMEM((seg, SUBLANES, 2 * S), F32),
                        pltpu.VMEM((seg, SUBLANES, 2 * S), F32),
                        pltpu.VMEM((seg + 1, SUBLANES, 2 * S), F32),
                        pltpu.VMEM((seg, SUBLANES, 2 * S), F32),
                        pltpu.VMEM((SUBLANES, 2 * S), F32),
                        pltpu.VMEM((SUBLANES, 2 * S), F32),
                        pltpu.VMEM((tc, LANES), F32),
                        pltpu.VMEM((tc, LANES), F32)],
        compiler_params=_params("parallel", "arbitrary"),
    )(proj, bmat, cmat, a_rows, d_skip.reshape(1, MAIN_WIDTH))


def _s5_bwd(proj, dyg_a, dyg_b, y, xp, bmat, cmat, a_rows, d_skip, dproj, *, tc=512):
    L = proj.shape[0]
    tc = min(tc, L)
    nt = L // tc
    seg = tc // SUBLANES
    S = STATE_COLS
    nn = (((1,), (1,)), ((), ()))
    tn = (((0,), (0,)), ((), ()))

    def fold_diagonal(acc_ref, mask_ref, fold_ref):
        x = acc_ref[...] * mask_ref[...]
        hi = x.astype(BF16)
        rest = x - hi.astype(F32)
        mid = rest.astype(BF16)
        low = (rest - mid.astype(F32)).astype(BF16)
        return sum(jnp.dot(piece, fold_ref[...], preferred_element_type=F32) for piece in (hi, mid, low))

    def body(u_ref, dyga_ref, dygb_ref, y_ref, xp_ref, b_ref, c_ref, a_ref, d_ref, mask_ref, fold_ref, dp_hbm,
             du_ref, dbd_ref, dcd_ref, da_ref, dd_ref,
             dl_s, pw_s, pwr_s, carry_s, e_s, up_s, dy_s, dyp_s, dup_s, db_ref, dc_ref):
        @pl.when(pl.program_id(1) == 0)
        def _():
            carry_s[...] = jnp.zeros_like(carry_s)
            db_ref[...] = jnp.zeros_like(db_ref)
            dc_ref[...] = jnp.zeros_like(dc_ref)
            da_ref[...] = jnp.zeros_like(da_ref)
            dd_ref[...] = jnp.zeros_like(dd_ref)
            _s5_tables(a_ref, pw_s, pwr_s, S, seg)

        ar, ai = a_ref[:, :S], a_ref[:, S:]
        a1r, a1i = ar[0:1], ai[0:1]
        u = u_ref[...]
        dy = (dyga_ref[...] + dygb_ref[...]) * _gelu_grad(y_ref[...])
        dy_s[...] = dy
        xp = xp_ref[...]
        _to_step_major(u_ref, up_s, seg)
        _to_step_major(dy_s, dyp_s, seg)
        ubp = up_s[...].astype(BF16)
        dyp = dyp_s[...].astype(BF16)
        bu = jnp.dot(ubp, b_ref[...], preferred_element_type=F32)
        x_re = a1r * xp[:, :S] - a1i * xp[:, S:] + bu[:, :S]
        x_im = a1r * xp[:, S:] + a1i * xp[:, :S] + bu[:, S:]
        xs = jnp.concatenate([x_re, x_im], axis=1).astype(BF16)
        dc_ref[...] += lax.dot_general(dyp, xs, tn, preferred_element_type=F32)
        dx = lax.dot_general(dyp, c_ref[...], nn, preferred_element_type=F32)
        dl_s[...] = dx.reshape(seg, SUBLANES, 2 * S)

        def step(k, carry):
            cr, ci = carry
            i = seg - 1 - k
            lr = dl_s[i, :, :S] + (ar * cr + ai * ci)
            li = dl_s[i, :, S:] + (ar * ci - ai * cr)
            dl_s[i, :, :S] = lr
            dl_s[i, :, S:] = li
            return lr, li

        zero = jnp.zeros((SUBLANES, S), F32)
        fr, fi = lax.fori_loop(0, seg, step, (zero, zero))
        pr, pi = pw_s[seg, 0:1, :S], pw_s[seg, 0:1, S:]
        er, ei = carry_s[0:1, :S], carry_s[0:1, S:]
        for s in range(SUBLANES - 1, -1, -1):
            e_s[s:s + 1, :S] = er
            e_s[s:s + 1, S:] = ei
            er, ei = fr[s:s + 1] + (pr * er + pi * ei), fi[s:s + 1] + (pr * ei - pi * er)
        carry_s[0:1, :S] = er
        carry_s[0:1, S:] = ei
        er, ei = e_s[:, :S][None], e_s[:, S:][None]
        pw = pwr_s[...]
        pwr, pwi = pw[:, :, :S], pw[:, :, S:]
        ll = dl_s[...]
        lam = jnp.concatenate([ll[:, :, :S] + (pwr * er + pwi * ei), ll[:, :, S:] + (pwr * ei - pwi * er)],
                              axis=-1).reshape(tc, 2 * S)
        l_re, l_im = lam[:, :S], lam[:, S:]
        da_ref[0:1, :S] += jnp.sum(l_re * xp[:, :S] + l_im * xp[:, S:], axis=0, keepdims=True)
        da_ref[0:1, S:] += jnp.sum(l_im * xp[:, :S] - l_re * xp[:, S:], axis=0, keepdims=True)
        lamb = lam.astype(BF16)
        dup_s[...] = lax.dot_general(lamb, b_ref[...], nn, preferred_element_type=F32)
        for s in range(SUBLANES):
            rows = pl.ds(seg * s, seg)
            du = _segment_rows(dup_s, s, seg) + d_ref[...] * dy_s[rows, :]
            du_ref[rows, :] = du.astype(du_ref.dtype)
        db_ref[...] += lax.dot_general(ubp, lamb, tn, preferred_element_type=F32)
        dd_ref[0:1, :] += jnp.sum(dy * u, axis=0, keepdims=True)

        @pl.when(pl.program_id(1) == nt - 1)
        def _():
            dbd_ref[...] = fold_diagonal(db_ref, mask_ref, fold_ref)
            dcd_ref[...] = fold_diagonal(dc_ref, mask_ref, fold_ref)

    rev = lambda b, t: (nt - 1 - t, b)
    col = jnp.arange(2 * S)
    fold = ((col // S * SSM_STATE + col % SSM_STATE)[:, None] == jnp.arange(LANES)[None, :]).astype(BF16)
    return pl.pallas_call(
        body, name="s5_bwd",
        out_shape=(jax.ShapeDtypeStruct(dproj.shape, dproj.dtype),
                   jax.ShapeDtypeStruct((SSM_BLOCKS, LANES, LANES), F32),
                   jax.ShapeDtypeStruct((SSM_BLOCKS, LANES, LANES), F32),
                   jax.ShapeDtypeStruct((SSM_BLOCKS, SUBLANES, 2 * S), F32),
                   jax.ShapeDtypeStruct((SUBLANES, MAIN_WIDTH), F32)),
        input_output_aliases={11: 0},
        grid=(SSM_BLOCKS, nt),
        in_specs=[pl.BlockSpec((tc, LANES), rev),
                  pl.BlockSpec((tc, LANES), rev),
                  pl.BlockSpec((tc, LANES), rev),
                  pl.BlockSpec((tc, LANES), rev),
                  pl.BlockSpec((tc, 2 * S), rev),
                  pl.BlockSpec((None, LANES, 2 * S), lambda b, t: (b, 0, 0)),
                  pl.BlockSpec((None, 2 * S, LANES), lambda b, t: (b, 0, 0)),
                  pl.BlockSpec((None, SUBLANES, 2 * S), lambda b, t: (b, 0, 0)),
                  pl.BlockSpec((1, LANES), lambda b, t: (0, b)),
                  pl.BlockSpec((LANES, 2 * S), lambda b, t: (0, 0)),
                  pl.BlockSpec((2 * S, LANES), lambda b, t: (0, 0)),
                  _ANY],
        out_specs=(pl.BlockSpec((tc, LANES), rev),
                   pl.BlockSpec((None, LANES, LANES), lambda b, t: (b, 0, 0)),
                   pl.BlockSpec((None, LANES, LANES), lambda b, t: (b, 0, 0)),
                   pl.BlockSpec((None, SUBLANES, 2 * S), lambda b, t: (b, 0, 0)),
                   pl.BlockSpec((SUBLANES, LANES), lambda b, t: (0, b))),
        scratch_shapes=[pltpu.VMEM((seg, SUBLANES, 2 * S), F32),
                        pltpu.VMEM((seg + 1, SUBLANES, 2 * S), F32),
                        pltpu.VMEM((seg, SUBLANES, 2 * S), F32),
                        pltpu.VMEM((SUBLANES, 2 * S), F32),
                        pltpu.VMEM((SUBLANES, 2 * S), F32),
                        pltpu.VMEM((tc, LANES), F32),
                        pltpu.VMEM((tc, LANES), F32),
                        pltpu.VMEM((tc, LANES), F32),
                        pltpu.VMEM((tc, LANES), F32),
                        pltpu.VMEM((LANES, 2 * S), F32),
                        pltpu.VMEM((LANES, 2 * S), F32)],
        compiler_params=_params("parallel", "arbitrary"),
    )(proj, dyg_a, dyg_b, y, xp, bmat, cmat, a_rows, d_skip.reshape(1, MAIN_WIDTH), _s5_diag_mask(), fold, dproj)


_Z_COLS = slice(MAIN_WIDTH, 2 * MAIN_WIDTH)
_ZM_COLS = slice(2 * MAIN_WIDTH + MEM_WIDTH, IN_WIDTH)


def _proj_rows(tr):
    return pl.BlockSpec((tr, IN_WIDTH), lambda i: (i, 0))


def _row_specs(tr):
    main = pl.BlockSpec((tr, MAIN_WIDTH), lambda i: (i, 0))
    z = pl.BlockSpec((tr, MAIN_WIDTH), lambda i: (i, 1))
    zm = pl.BlockSpec((tr, MEM_WIDTH), lambda i: (i, IN_WIDTH // MEM_WIDTH - 1))
    mem = pl.BlockSpec((tr, MEM_WIDTH), lambda i: (i, 0))
    cat = pl.BlockSpec((tr, D_MODEL), lambda i: (i, 0))
    vec = pl.BlockSpec((1, MAIN_WIDTH), lambda i: (0, 0))
    return main, z, zm, mem, cat, vec


def _gate_a_fwd(y, t, b_glu, proj, o_mem, *, tr=256):
    L = y.shape[0]
    tr = min(tr, L)

    def body(y_ref, t_ref, b_ref, z_ref, zm_ref, om_ref, o_ref):
        yg = _gelu(y_ref[...])
        sz, _ = _silu_and_grad(z_ref[...])
        o_ref[:, :MAIN_WIDTH] = (yg * _sigmoid(t_ref[...] + b_ref[...]) * sz).astype(BF16)
        szm, _ = _silu_and_grad(zm_ref[...])
        o_ref[:, MAIN_WIDTH:] = (om_ref[...] * szm).astype(BF16)

    main, z, zm, mem, cat, vec = _row_specs(tr)
    return pl.pallas_call(
        body, name="gate_a_fwd", out_shape=jax.ShapeDtypeStruct((L, D_MODEL), BF16),
        grid=(L // tr,), in_specs=[main, main, vec, z, zm, mem], out_specs=cat,
        compiler_params=_params("parallel"),
    )(y, t, b_glu.reshape(1, MAIN_WIDTH), proj, proj, o_mem)


def _gate_a_bwd(dcat, y, t, b_glu, proj, o_mem, *, tr=256):
    L = y.shape[0]
    tr = min(tr, L)

    def body(dc_ref, y_ref, t_ref, b_ref, z_ref, zm_ref, om_ref,
             dp_ref, dt_ref, dyg_ref, dom_ref, db_ref):
        dmain = dc_ref[:, :MAIN_WIDTH]
        dmemo = dc_ref[:, MAIN_WIDTH:]
        yg = _gelu(y_ref[...])
        sg = _sigmoid(t_ref[...] + b_ref[...])
        sz, gz = _silu_and_grad(z_ref[...])
        dp_ref[:, _Z_COLS] = (dmain * (yg * sg) * gz).astype(BF16)
        dy2 = dmain * sz
        dyg_ref[...] = dy2 * sg
        dt = dy2 * yg * (sg * (1.0 - sg))
        dt_ref[...] = dt.astype(BF16)

        @pl.when(pl.program_id(0) == 0)
        def _():
            db_ref[...] = jnp.zeros_like(db_ref)

        db_ref[...] += jnp.sum(dt, axis=0, keepdims=True)
        szm, gzm = _silu_and_grad(zm_ref[...])
        dom_ref[...] = dmemo * szm
        dp_ref[:, _ZM_COLS] = (dmemo * om_ref[...] * gzm).astype(BF16)

    main, z, zm, mem, cat, vec = _row_specs(tr)
    outs = pl.pallas_call(
        body, name="gate_a_bwd",
        out_shape=(jax.ShapeDtypeStruct((L, IN_WIDTH), BF16),
                   jax.ShapeDtypeStruct((L, MAIN_WIDTH), BF16), jax.ShapeDtypeStruct((L, MAIN_WIDTH), F32),
                   jax.ShapeDtypeStruct((L, MEM_WIDTH), F32), jax.ShapeDtypeStruct((1, MAIN_WIDTH), F32)),
        grid=(L // tr,), in_specs=[cat, main, main, vec, z, zm, mem],
        out_specs=(_proj_rows(tr), main, main, mem, vec),
        compiler_params=_params("arbitrary"),
    )(dcat, y, t, b_glu.reshape(1, MAIN_WIDTH), proj, proj, o_mem)
    return outs


def _gate_b_fwd(att, proj, o_mem, *, tr=256):
    L = att.shape[0]
    tr = min(tr, L)

    def body(a_ref, z_ref, zm_ref, om_ref, o_ref):
        sz, _ = _silu_and_grad(z_ref[...])
        o_ref[:, :MAIN_WIDTH] = (a_ref[...] * sz).astype(BF16)
        szm, _ = _silu_and_grad(zm_ref[...])
        o_ref[:, MAIN_WIDTH:] = (om_ref[...] * szm).astype(BF16)

    main, z, zm, mem, cat, _ = _row_specs(tr)
    return pl.pallas_call(
        body, name="gate_b_fwd", out_shape=jax.ShapeDtypeStruct((L, D_MODEL), BF16),
        grid=(L // tr,), in_specs=[main, z, zm, mem], out_specs=cat,
        compiler_params=_params("parallel"),
    )(att, proj, proj, o_mem)


def _gate_b_bwd(dcat, att, proj, o_mem, *, tr=256):
    L = att.shape[0]
    tr = min(tr, L)

    def body(dc_ref, a_ref, z_ref, zm_ref, om_ref, da_ref, dp_ref, dom_ref, dl_ref):
        dmain = dc_ref[:, :MAIN_WIDTH]
        dmemo = dc_ref[:, MAIN_WIDTH:]
        att = a_ref[...]
        sz, gz = _silu_and_grad(z_ref[...])
        datt = dmain * sz
        da_ref[...] = datt
        dp_ref[:, _Z_COLS] = (dmain * att * gz).astype(BF16)
        szm, gzm = _silu_and_grad(zm_ref[...])
        dom_ref[...] = dmemo * szm
        dp_ref[:, _ZM_COLS] = (dmemo * om_ref[...] * gzm).astype(BF16)
        prod = datt * att
        for h in range(FOX_HEADS):
            dl_ref[h] = jnp.sum(prod[:, h * HEAD_DIM:(h + 1) * HEAD_DIM], axis=1, keepdims=True)

    main, z, zm, mem, cat, _ = _row_specs(tr)
    delta = pl.BlockSpec((FOX_HEADS, tr, 1), lambda i: (0, i, 0))
    return pl.pallas_call(
        body, name="gate_b_bwd",
        out_shape=(jax.ShapeDtypeStruct((L, MAIN_WIDTH), F32), jax.ShapeDtypeStruct((L, IN_WIDTH), BF16),
                   jax.ShapeDtypeStruct((L, MEM_WIDTH), F32), jax.ShapeDtypeStruct((FOX_HEADS, L, 1), F32)),
        grid=(L // tr,), in_specs=[cat, main, z, zm, mem], out_specs=(main, _proj_rows(tr), mem, delta),
        compiler_params=_params("parallel"),
    )(dcat, att, proj, proj, o_mem)


_MEM_Q_COL = (2 * MAIN_WIDTH) // HEAD_DIM
_NT = (((1,), (1,)), ((), ()))
_TN = (((0,), (0,)), ((), ()))


def _mem_probs(q_ref, k_ref):
    qs = (q_ref[...] * (HEAD_DIM ** -0.5)).astype(BF16)
    s = lax.dot_general(qs, k_ref[...].astype(BF16), _NT, preferred_element_type=F32)
    e = jnp.exp(s - jnp.max(s, axis=-1, keepdims=True))
    return qs, e / jnp.sum(e, axis=-1, keepdims=True)


def _mem_attn_fwd(proj, kvm, *, tq=2048):
    L = proj.shape[0]
    tq = min(tq, L)

    def body(q_ref, k_ref, v_ref, o_ref):
        _, p = _mem_probs(q_ref, k_ref)
        o_ref[...] = jnp.dot(p.astype(BF16), v_ref[...].astype(BF16), preferred_element_type=F32)

    return pl.pallas_call(
        body, name="mem_attn_fwd", out_shape=jax.ShapeDtypeStruct((L, MEM_WIDTH), F32),
        grid=(MEM_HEADS, L // tq),
        in_specs=[pl.BlockSpec((tq, HEAD_DIM), lambda h, i: (i, _MEM_Q_COL + h)),
                  pl.BlockSpec((N_MEM, HEAD_DIM), lambda h, i: (0, h)),
                  pl.BlockSpec((N_MEM, HEAD_DIM), lambda h, i: (0, MEM_HEADS + h))],
        out_specs=pl.BlockSpec((tq, HEAD_DIM), lambda h, i: (i, h)),
        compiler_params=_params("parallel", "parallel"),
    )(proj, kvm, kvm)


def _mem_attn_bwd(proj, kvm, do, dproj, *, tq=2048):
    L = proj.shape[0]
    tq = min(tq, L)

    def body(q_ref, k_ref, v_ref, do_ref, dp_hbm, dq_ref, dk_ref, dv_ref):
        @pl.when(pl.program_id(1) == 0)
        def _():
            dk_ref[...] = jnp.zeros_like(dk_ref)
            dv_ref[...] = jnp.zeros_like(dv_ref)

        qs, p = _mem_probs(q_ref, k_ref)
        dob = do_ref[...].astype(BF16)
        dp = lax.dot_general(dob, v_ref[...].astype(BF16), _NT, preferred_element_type=F32)
        ds = p * (dp - jnp.sum(p * dp, axis=-1, keepdims=True))
        dsb = ds.astype(BF16)
        dq = jnp.dot(dsb, k_ref[...].astype(BF16), preferred_element_type=F32) * (HEAD_DIM ** -0.5)
        dq_ref[...] = dq.astype(BF16)
        dk_ref[...] += lax.dot_general(dsb, qs, _TN, preferred_element_type=F32)
        dv_ref[...] += lax.dot_general(p.astype(BF16), dob, _TN, preferred_element_type=F32)

    dproj, dk, dv = pl.pallas_call(
        body, name="mem_attn_bwd",
        out_shape=(jax.ShapeDtypeStruct(dproj.shape, dproj.dtype),
                   jax.ShapeDtypeStruct((N_MEM, MEM_WIDTH), F32),
                   jax.ShapeDtypeStruct((N_MEM, MEM_WIDTH), F32)),
        grid=(MEM_HEADS, L // tq),
        in_specs=[pl.BlockSpec((tq, HEAD_DIM), lambda h, i: (i, _MEM_Q_COL + h)),
                  pl.BlockSpec((N_MEM, HEAD_DIM), lambda h, i: (0, h)),
                  pl.BlockSpec((N_MEM, HEAD_DIM), lambda h, i: (0, MEM_HEADS + h)),
                  pl.BlockSpec((tq, HEAD_DIM), lambda h, i: (i, h)),
                  _ANY],
        out_specs=(pl.BlockSpec((tq, HEAD_DIM), lambda h, i: (i, _MEM_Q_COL + h)),
                   pl.BlockSpec((N_MEM, HEAD_DIM), lambda h, i: (0, h)),
                   pl.BlockSpec((N_MEM, HEAD_DIM), lambda h, i: (0, h))),
        input_output_aliases={4: 0},
        compiler_params=_params("parallel", "arbitrary"),
    )(proj, kvm, kvm, do, dproj)
    return dproj, jnp.concatenate([dk, dv], axis=1)


def _tile_cumsum(x, row, reverse):
    for sh in (1, 2, 4):
        if reverse:
            x = x + jnp.where(row < SUBLANES - sh, pltpu.roll(x, SUBLANES - sh, 0), 0.0)
        else:
            x = x + jnp.where(row >= sh, pltpu.roll(x, sh, 0), 0.0)
    return x


def _fgate_fwd(pre, b_pad):
    L = pre.shape[0]
    n8 = L // SUBLANES

    def body(p_ref, b_ref, o_ref):
        row = lax.broadcasted_iota(jnp.int32, (SUBLANES, LANES), 0)
        b = b_ref[...]

        def step(i, carry):
            x = p_ref[i] + b
            logf = jnp.minimum(x, 0.0) - jnp.log(1.0 + jnp.exp(-jnp.abs(x)))
            t = _tile_cumsum(logf, row, False) + carry
            o_ref[i] = t
            return t[SUBLANES - 1:SUBLANES, :]

        lax.fori_loop(0, n8, step, jnp.zeros((1, LANES), F32))

    out = pl.pallas_call(
        body, name="fgate_fwd", out_shape=jax.ShapeDtypeStruct((n8, SUBLANES, LANES), F32),
        compiler_params=_params(),
    )(pre.reshape(n8, SUBLANES, LANES), b_pad.reshape(1, LANES))
    return out.reshape(L, LANES)


def _fgate_bwd(dfcum, pre, b_pad):
    L = pre.shape[0]
    n8 = L // SUBLANES

    def body(d_ref, p_ref, b_ref, o_ref, s_ref):
        row = lax.broadcasted_iota(jnp.int32, (SUBLANES, LANES), 0)
        b = b_ref[...]

        def step(k, carry):
            c, acc = carry
            i = n8 - 1 - k
            t = _tile_cumsum(d_ref[i], row, True) + c
            dpre = t * _sigmoid(-(p_ref[i] + b))
            o_ref[i] = dpre
            return t[0:1, :], acc + dpre

        _, acc = lax.fori_loop(0, n8, step, (jnp.zeros((1, LANES), F32), jnp.zeros((SUBLANES, LANES), F32)))
        s_ref[...] = jnp.sum(acc, axis=0, keepdims=True)

    dpre, db = pl.pallas_call(
        body, name="fgate_bwd",
        out_shape=(jax.ShapeDtypeStruct((n8, SUBLANES, LANES), F32), jax.ShapeDtypeStruct((1, LANES), F32)),
        compiler_params=_params(),
    )(dfcum.reshape(n8, SUBLANES, LANES), pre.reshape(n8, SUBLANES, LANES), b_pad.reshape(1, LANES))
    return dpre.reshape(L, LANES), db


FOX_BLOCK = 512


def _fox_scores(qs, k, fk, diagonal):
    s = lax.dot_general(qs, k, _NT, preferred_element_type=F32) - fk
    if diagonal:
        row = lax.broadcasted_iota(jnp.int32, s.shape, 0)
        col = lax.broadcasted_iota(jnp.int32, s.shape, 1)
        s = jnp.where(row >= col, s, NEG_BIG)
    return s


def _fox_specs(tq, L):
    nq = L // tq
    return dict(
        rows=lambda off: pl.BlockSpec((tq, HEAD_DIM), lambda h, i: (i, off + h)),
        seq=lambda off: pl.BlockSpec((L, HEAD_DIM), lambda h, i: (0, off + h)),
        col=pl.BlockSpec((None, None, tq, 1), lambda h, i: (h, i, 0, 0)),
        col_all=pl.BlockSpec((None, nq, tq, 1), lambda h, i: (h, 0, 0, 0)),
        row=pl.BlockSpec((None, None, 1, tq), lambda h, i: (h, i, 0, 0)),
        row_all=pl.BlockSpec((None, nq, 1, tq), lambda h, i: (h, 0, 0, 0)))


FOX_FWD_HEADS = 1
FOX_FWD_BLOCK = 1024


def _fox_fwd(proj, kv, fk):
    L = proj.shape[0]
    tq = min(FOX_FWD_BLOCK, L)
    nq = L // tq
    nh = FOX_FWD_HEADS
    W = nh * HEAD_DIM
    lse_shape = fk.shape[:2] + (fk.shape[3], 1)
    fk = fk.reshape(FOX_HEADS, nq, 1, tq)

    def body(q_ref, k_ref, v_ref, fk_ref, o_ref, lse_ref, m_s, l_s, acc_s):
        qi = pl.program_id(1)
        cols = [slice(a * HEAD_DIM, (a + 1) * HEAD_DIM) for a in range(nh)]
        qs = [(q_ref[:, cs] * (HEAD_DIM ** -0.5)).astype(BF16) for cs in cols]
        m_s[...] = jnp.full_like(m_s, NEG_BIG)
        l_s[...] = jnp.zeros_like(l_s)
        acc_s[...] = jnp.zeros_like(acc_s)

        def block(j, diagonal):
            r0 = pl.multiple_of(j * tq, tq)
            for a, cs in enumerate(cols):
                s = _fox_scores(qs[a], k_ref[pl.ds(r0, tq), cs], fk_ref[a, j], diagonal)
                m_new = jnp.maximum(m_s[a], jnp.max(s, axis=-1, keepdims=True))
                alpha = jnp.exp(m_s[a] - m_new)
                p = jnp.exp(s - m_new)
                l_s[a] = alpha * l_s[a] + jnp.sum(p, axis=-1, keepdims=True)
                acc_s[a] = alpha * acc_s[a] + jnp.dot(p.astype(BF16), v_ref[pl.ds(r0, tq), cs],
                                                      preferred_element_type=F32)
                m_s[a] = m_new

        def below(j, carry):
            block(j, False)
            return carry

        lax.fori_loop(0, qi, below, 0)
        block(qi, True)
        for a, cs in enumerate(cols):
            o_ref[:, cs] = acc_s[a] / l_s[a]
            lse_ref[a] = m_s[a] + jnp.log(l_s[a])

    att, lse = pl.pallas_call(
        body, name="fox_fwd",
        out_shape=(jax.ShapeDtypeStruct((L, MAIN_WIDTH), F32),
                   jax.ShapeDtypeStruct((FOX_HEADS, nq, tq, 1), F32)),
        grid=(FOX_HEADS // nh, nq),
        in_specs=[pl.BlockSpec((tq, W), lambda h, i: (i, h)),
                  pl.BlockSpec((L, W), lambda h, i: (0, h)),
                  pl.BlockSpec((L, W), lambda h, i: (0, FOX_HEADS // nh + h)),
                  pl.BlockSpec((nh, nq, 1, tq), lambda h, i: (h, 0, 0, 0))],
        out_specs=(pl.BlockSpec((tq, W), lambda h, i: (i, h)),
                   pl.BlockSpec((nh, None, tq, 1), lambda h, i: (h, i, 0, 0))),
        scratch_shapes=[pltpu.VMEM((nh, tq, 1), F32), pltpu.VMEM((nh, tq, 1), F32),
                        pltpu.VMEM((nh, tq, HEAD_DIM), F32)],
        compiler_params=_params("parallel", "parallel"),
    )(proj, kv, kv, fk)
    return att, lse.reshape(lse_shape)


def _fox_bwd_dq(proj, kv, fk, lse, delta, datt, dproj):
    L = proj.shape[0]
    tq = min(FOX_BLOCK, L)
    nq = L // tq
    sp = _fox_specs(tq, L)

    def body(q_ref, k_ref, v_ref, fk_ref, lse_ref, dl_ref, do_ref, dp_hbm, dq_ref, df_ref, acc_s, df_s):
        qi = pl.program_id(1)
        qs = (q_ref[...] * (HEAD_DIM ** -0.5)).astype(BF16)
        dob = do_ref[...].astype(BF16)
        lse, dl = lse_ref[...], dl_ref[...]
        acc_s[...] = jnp.zeros_like(acc_s)
        df_s[...] = jnp.zeros_like(df_s)

        def block(j, diagonal):
            r0 = pl.multiple_of(j * tq, tq)
            k = k_ref[pl.ds(r0, tq), :]
            p = jnp.exp(_fox_scores(qs, k, fk_ref[j], diagonal) - lse)
            dp = lax.dot_general(dob, v_ref[pl.ds(r0, tq), :], _NT, preferred_element_type=F32)
            ds = p * (dp - dl)
            acc_s[...] += jnp.dot(ds.astype(BF16), k, preferred_element_type=F32)
            df_s[...] += jnp.sum(ds, axis=1, keepdims=True)

        def below(j, carry):
            block(j, False)
            return carry

        lax.fori_loop(0, qi, below, 0)
        block(qi, True)
        dq_ref[...] = (acc_s[...] * (HEAD_DIM ** -0.5)).astype(BF16)
        df_ref[...] = df_s[...]

    return pl.pallas_call(
        body, name="fox_bwd_dq",
        out_shape=(jax.ShapeDtypeStruct(dproj.shape, dproj.dtype),
                   jax.ShapeDtypeStruct((FOX_HEADS, nq, tq, 1), F32)),
        grid=(FOX_HEADS, nq),
        in_specs=[sp["rows"](0), sp["seq"](0), sp["seq"](FOX_HEADS), sp["row_all"],
                  sp["col"], sp["col"], sp["rows"](0), _ANY],
        out_specs=(sp["rows"](0), sp["col"]),
        input_output_aliases={7: 0},
        scratch_shapes=[pltpu.VMEM((tq, HEAD_DIM), F32), pltpu.VMEM((tq, 1), F32)],
        compiler_params=_params("parallel", "parallel"),
    )(proj, kv, kv, fk, lse, delta, datt, dproj)


def _fox_bwd_dkv(proj, kv, fk, lse, delta, datt):
    L = proj.shape[0]
    tq = min(FOX_BLOCK, L)
    nq = L // tq
    sp = _fox_specs(tq, L)

    def body(q_ref, k_ref, v_ref, fk_ref, lse_ref, dl_ref, do_ref,
             dk_ref, dv_ref, df_ref, dk_s, dv_s, df_s):
        ki = pl.program_id(1)
        k, v, fk = k_ref[...], v_ref[...], fk_ref[...]
        dk_s[...] = jnp.zeros_like(dk_s)
        dv_s[...] = jnp.zeros_like(dv_s)
        df_s[...] = jnp.zeros_like(df_s)

        def block(i, diagonal):
            r0 = pl.multiple_of(i * tq, tq)
            qs = (q_ref[pl.ds(r0, tq), :] * (HEAD_DIM ** -0.5)).astype(BF16)
            dob = do_ref[pl.ds(r0, tq), :].astype(BF16)
            p = jnp.exp(_fox_scores(qs, k, fk, diagonal) - lse_ref[i])
            dp = lax.dot_general(dob, v, _NT, preferred_element_type=F32)
            ds = p * (dp - dl_ref[i])
            dv_s[...] += lax.dot_general(p.astype(BF16), dob, _TN, preferred_element_type=F32)
            dk_s[...] += lax.dot_general(ds.astype(BF16), qs, _TN, preferred_element_type=F32)
            df_s[...] -= jnp.sum(ds, axis=0, keepdims=True)

        def above(i, carry):
            block(i, False)
            return carry

        block(ki, True)
        lax.fori_loop(ki + 1, nq, above, 0)
        dk_ref[...] = dk_s[...].astype(BF16)
        dv_ref[...] = dv_s[...].astype(BF16)
        df_ref[...] = df_s[...]

    return pl.pallas_call(
        body, name="fox_bwd_dkv",
        out_shape=(jax.ShapeDtypeStruct((L, MAIN_WIDTH), BF16),
                   jax.ShapeDtypeStruct((L, MAIN_WIDTH), BF16),
                   jax.ShapeDtypeStruct((FOX_HEADS, nq, 1, tq), F32)),
        grid=(FOX_HEADS, nq),
        in_specs=[sp["seq"](0), sp["rows"](0), sp["rows"](FOX_HEADS), sp["row"],
                  sp["col_all"], sp["col_all"], sp["seq"](0)],
        out_specs=(sp["rows"](0), sp["rows"](0), sp["row"]),
        scratch_shapes=[pltpu.VMEM((tq, HEAD_DIM), F32), pltpu.VMEM((tq, HEAD_DIM), F32),
                        pltpu.VMEM((1, tq), F32)],
        compiler_params=_params("parallel", "parallel"),
    )(proj, kv, kv, fk, lse, delta, datt)


def _pad_lanes(a):
    return jnp.pad(a, ((0, 0), (0, LANES - a.shape[1])))


def _mem_branch_fwd(memn, w_mk, proj, tag):
    kvm = _mm(memn, w_mk, name="mem_kv_" + tag)
    return kvm, _mem_attn_fwd(proj, kvm)


def _mem_branch_bwd(mem, g, w_mk, proj, memn, kvm, do_mem, dproj, tag):
    dproj, dkvm = _mem_attn_bwd(proj, kvm, do_mem, dproj)
    dkvm = dkvm.astype(BF16)
    dw_mk = _mm(memn, dkvm, ta=True, name="dw_mem_kv_" + tag, out_dtype=BF16)
    dmemn = _mm(dkvm, w_mk, tb=True, name="dmemn_" + tag)
    _, dg = _rmsnorm_bwd(mem, g, dmemn, name="mem_norm_bwd_" + tag, dx_dtype=BF16)
    return dproj, dw_mk, dg


def _local_step(x, mem, target, w, fetch=None, grads_ready=None):
    if grads_ready is None:
        grads_ready = lambda group, grads, token: token
    L = x.shape[0]
    g = {}
    w = dict(w)

    b_re_t = jnp.transpose(w["b_re"], (0, 2, 1))
    b_im_t = jnp.transpose(w["b_im"], (0, 2, 1))
    ar, ai, bbr_t, bbi_t = _s5_prep(w["lam_re"], w["lam_im"], w["log_step"], b_re_t, b_im_t)
    bmat, cmat = _s5_block_mats(bbr_t, bbi_t, w["c_re"], w["c_im"])
    a_rows = _s5_a_rows(ar, ai)

    hn0 = _rmsnorm_fwd(x, w["pre_norm_g"][0], name="pre_norm_0", out_dtype=BF16)
    memn0 = _rmsnorm_fwd(mem, w["mem_norm_g"][0], name="mem_norm_0", out_dtype=BF16)
    memn1 = _rmsnorm_fwd(mem, w["mem_norm_g"][1], name="mem_norm_1", out_dtype=BF16)
    if fetch is not None:
        w.update(fetch("a", [hn0, memn0, memn1, bmat, cmat, a_rows]))
    proj_a = _mm(hn0, w["w_in_a"], name="in_proj_a")
    y, yg, xp = _s5_fwd(proj_a, bmat, cmat, a_rows, w["d_skip"])
    if fetch is not None:
        w.update(fetch("b", yg))
    t = _mm(yg, w["w_glu"], name="glu_proj")
    kvm0, om0 = _mem_branch_fwd(memn0, w["w_mem_kv"][0], proj_a, "0")
    cat0 = _gate_a_fwd(y, t, w["b_glu"], proj_a, om0)
    o0 = _mm(cat0, w["w_out"][0], name="out_proj_0")
    h1 = _rmsnorm_fwd(o0, w["post_norm_g"][0], res=x, name="post_norm_0")

    kv_in = _rmsnorm_fwd(h1, w["kv_norm_g"], name="kv_norm", out_dtype=BF16)
    if fetch is not None:
        w.update(fetch("c", kv_in))
    kv = _mm(kv_in, w["w_kv"], name="kv_proj", out_dtype=BF16)
    pre_f = _mm(kv_in, w["w_fgate"], name="fgate_proj")
    b_f = jnp.pad(w["b_fgate"], (0, LANES - FOX_HEADS))
    fcum = _fgate_fwd(pre_f, b_f)
    fc = jnp.transpose(fcum[:, :FOX_HEADS])
    tq = min(FOX_BLOCK, L)
    fk = fc.reshape(FOX_HEADS, L // tq, 1, tq)

    hn1 = _rmsnorm_fwd(h1, w["pre_norm_g"][1], name="pre_norm_1", out_dtype=BF16)
    proj_b = _mm(hn1, w["w_in_b"], name="in_proj_b")
    att, lse = _fox_fwd(proj_b, kv, fk)
    kvm1, om1 = _mem_branch_fwd(memn1, w["w_mem_kv"][1], proj_b, "1")
    cat1 = _gate_b_fwd(att, proj_b, om1)
    o1 = _mm(cat1, w["w_out"][1], name="out_proj_1")
    dh2, loss_row = _final_norm_loss(o1, w["post_norm_g"][1], h1, target)

    do1, dpost1 = _rmsnorm_bwd(o1, w["post_norm_g"][1], dh2, name="post_norm_bwd_1", dx_dtype=BF16)
    dcat1 = _mm(do1, w["w_out"][1], tb=True, name="dcat_1", out_dtype=BF16)
    g["w_out_1"] = _mm(cat1, do1, ta=True, name="dw_out_1", out_dtype=BF16)
    datt, dproj_b, dom1, delta = _gate_b_bwd(dcat1, att, proj_b, om1)
    dproj_b, g["w_mem_kv_1"], dmemg1 = _mem_branch_bwd(mem, w["mem_norm_g"][1], w["w_mem_kv"][1], proj_b,
                                                      memn1, kvm1, dom1, dproj_b, "1")
    delta = delta.reshape(lse.shape)
    dproj_b, dfq = _fox_bwd_dq(proj_b, kv, fk, lse, delta, datt, dproj_b)
    dk, dv, dfk = _fox_bwd_dkv(proj_b, kv, fk, lse, delta, datt)
    g["w_in_b"] = _mm(hn1, dproj_b, ta=True, name="dw_in_b", out_dtype=BF16, shards=N_CHIPS)
    dhn1 = _mm(dproj_b, w["w_in_b"], tb=True, name="dhn_1")

    dkv = jnp.concatenate([dk, dv], axis=1)
    g["w_kv"] = _mm(kv_in, dkv, ta=True, name="dw_kv", out_dtype=BF16, shards=N_CHIPS)
    dkv_in_a = _mm(dkv, w["w_kv"], tb=True, name="dkv_in_kv")
    dfcum = _pad_lanes(jnp.transpose(dfq.reshape(FOX_HEADS, L) + dfk.reshape(FOX_HEADS, L)))
    dpre_f, db_f = _fgate_bwd(dfcum, pre_f, b_f)
    g["b_fgate"] = db_f[0, :FOX_HEADS]
    g["w_fgate"] = _mm(kv_in, dpre_f, ta=True, name="dw_fgate")[:, :FOX_HEADS]
    dkv_in_b = _mm(dpre_f, w["w_fgate"], tb=True, name="dkv_in_fgate")
    dh1, g["kv_norm_g"], dpre1 = _rmsnorm_bwd_pair(h1, w["kv_norm_g"], (dkv_in_a, dkv_in_b), w["pre_norm_g"][1],
                                                   dhn1, adds=(dh2,), name="kv_pre_norm_bwd")
    dh1 = grads_ready("b", g, dh1)

    do0, dpost0 = _rmsnorm_bwd(o0, w["post_norm_g"][0], dh1, name="post_norm_bwd_0", dx_dtype=BF16)
    dcat0 = _mm(do0, w["w_out"][0], tb=True, name="dcat_0", out_dtype=BF16)
    g["w_out_0"] = _mm(cat0, do0, ta=True, name="dw_out_0", out_dtype=BF16)
    dcat0 = grads_ready("b_send", g, dcat0)
    dproj_a, dt, dyg_a, dom0, db_glu = _gate_a_bwd(dcat0, y, t, w["b_glu"], proj_a, om0)
    g["b_glu"] = db_glu[0]
    g["w_glu"] = _mm(yg, dt, ta=True, name="dw_glu", out_dtype=BF16)
    dyg_b = _mm(dt, w["w_glu"], tb=True, name="dyg")
    dproj_a, g["w_mem_kv_0"], dmemg0 = _mem_branch_bwd(mem, w["mem_norm_g"][0], w["w_mem_kv"][0], proj_a,
                                                      memn0, kvm0, dom0, dproj_a, "0")
    dyg_b = grads_ready("a1", g, dyg_b)
    dproj_a, db_blk, dc_blk, da_rows, dd_skip = _s5_bwd(proj_a, dyg_a, dyg_b, y, xp, bmat, cmat, a_rows,
                                                        w["d_skip"], dproj_a)
    dproj_a = grads_ready("a1_send", g, dproj_a)
    g["d_skip"] = dd_skip[0]
    g["w_in_a"] = _mm(hn0, dproj_a, ta=True, name="dw_in_a", out_dtype=BF16, shards=N_CHIPS)
    dproj_a = grads_ready("a2", g, dproj_a)
    dhn0 = _mm(dproj_a, w["w_in_a"], tb=True, name="dhn_0")
    grad_x, dpre0 = _rmsnorm_bwd(x, w["pre_norm_g"][0], dhn0, adds=(dh1,), name="pre_norm_bwd_0")

    dbb = _s5_unfold(db_blk)
    dcc = _s5_unfold(dc_blk)
    g["c_re"], g["c_im"] = dcc[0], -dcc[1]
    d_ar = da_rows[:, 0, :STATE_COLS].reshape(SSM_GROUPS, SSM_STATE)
    d_ai = da_rows[:, 0, STATE_COLS:].reshape(SSM_GROUPS, SSM_STATE)
    dlr, dli, dls, dbr_t, dbi_t = _s5_prep_bwd(w["lam_re"], w["lam_im"], w["log_step"], b_re_t, b_im_t,
                                               d_ar, d_ai, dbb[0], dbb[1])
    g["lam_re"], g["lam_im"], g["log_step"] = dlr, dli, dls[:, 0]
    g["b_re"] = jnp.transpose(dbr_t, (0, 2, 1))
    g["b_im"] = jnp.transpose(dbi_t, (0, 2, 1))
    g["pre_norm_g"] = jnp.stack([dpre0, dpre1])
    g["post_norm_g"] = jnp.stack([dpost0, dpost1])
    g["mem_norm_g"] = jnp.stack([dmemg0, dmemg1])
    return loss_row, grad_x, g


_MESH = pl.DeviceIdType.MESH
_ANY = pl.BlockSpec(memory_space=pl.ANY)


def _place():
    x, y, c = lax.axis_index("x"), lax.axis_index("y"), lax.axis_index("c")
    chips = [(1 - x, y), (x, 1 - y), (1 - x, 1 - y)]
    return x, y, c, chips


_HBM = pl.BlockSpec(memory_space=pltpu.HBM)
_SEM = pl.BlockSpec(memory_space=pltpu.SEMAPHORE)
_SIDE = pltpu.SideEffectType.DATAFLOW_SIDE_EFFECTING


def _in_hbm(a):
    return pltpu.with_memory_space_constraint(a, pltpu.HBM)


def _hbm_like(a):
    return pltpu.HBM(a.shape, a.dtype)


def _ici_copies(srcs, lands, send_sem, recv_sem, src_at, dst_at, wait_at, to_sibling=False):
    x, y, c, chips = _place()
    peers = [(x, y, 1 - c)] if to_sibling else [(cx, cy, c) for cx, cy in chips]
    m = len(peers)
    start, wait = [], []
    for i in range(len(srcs)):
        for k, (px, py, pc) in enumerate(peers):
            sem = dict(send_sem=send_sem.at[m * i + k], recv_sem=recv_sem.at[m * i + k],
                       device_id=(px, py, pc), device_id_type=_MESH)
            src = src_at(srcs[i], 2 * px + py, c)
            start.append(pltpu.make_async_remote_copy(src_ref=src, dst_ref=dst_at(lands[i], 2 * x + y, k, c), **sem))
            wait.append(pltpu.make_async_remote_copy(src_ref=src, dst_ref=wait_at(lands[i], 2 * px + py, k, c), **sem))
    return start, wait


def _route_peers(route):
    return 1 if len(route) == 4 else 3


_BLOCK_ROUTE = (lambda s, j, c: s, lambda l, me, k, c: l.at[me, c], lambda l, j, k, c: l.at[j, c])


def _ici_start(srcs, lands, token, route, *, name):
    n = len(srcs)

    def body(*refs):
        start, _ = _ici_copies(refs[:n], refs[n:2 * n], refs[2 * n + 1], refs[2 * n + 2], *route)
        for cp in start:
            cp.start()

    sems = pltpu.SemaphoreType.DMA((_route_peers(route) * n,))
    outs = pl.pallas_call(
        body, name=name,
        out_shape=(sems, sems, *[_hbm_like(a) for a in srcs], *[_hbm_like(a) for a in lands], _hbm_like(token)),
        in_specs=[_HBM] * (2 * n + 1), out_specs=(_SEM, _SEM, *[_HBM] * (2 * n + 1)),
        input_output_aliases={i: 2 + i for i in range(2 * n + 1)},
        compiler_params=pltpu.CompilerParams(has_side_effects=_SIDE),
    )(*[_in_hbm(a) for a in srcs], *[_in_hbm(a) for a in lands], _in_hbm(token))
    return (outs[0], outs[1], list(outs[2:2 + n]), list(outs[2 + n:2 + 2 * n])), outs[2 + 2 * n]


def _ici_wait(handle, after, route, *, name):
    send_sem, recv_sem, srcs, lands = handle
    n = len(srcs)
    after = list(after) if isinstance(after, (list, tuple)) else [after]

    def body(*refs):
        _, wait = _ici_copies(refs[:n], refs[n:2 * n], refs[2 * n], refs[2 * n + 1], *route)
        for cp in wait:
            cp.wait_send()
            cp.wait_recv()

    outs = pl.pallas_call(
        body, name=name,
        out_shape=(*[_hbm_like(a) for a in srcs], *[_hbm_like(a) for a in lands]),
        in_specs=[_HBM] * (2 * n) + [_SEM, _SEM] + [_ANY] * len(after), out_specs=tuple([_HBM] * (2 * n)),
        input_output_aliases={i: i for i in range(2 * n)},
        compiler_params=pltpu.CompilerParams(has_side_effects=_SIDE),
    )(*srcs, *lands, send_sem, recv_sem, *after)
    return list(outs[:n]), list(outs[n:])


_GATHER_ROUTE = (lambda s, j, c: s.at[c], lambda l, me, k, c: l.at[me, c], lambda l, j, k, c: l.at[j, c])
_SCATTER_ROUTE = (lambda s, j, c: s.at[j], lambda l, me, k, c: l.at[k], lambda l, j, k, c: l.at[k])
_SHARE_ROUTE = (lambda s, j, c: s, lambda l, me, k, c: l.at[c], lambda l, j, k, c: l.at[1 - c], True)
_SWAP_ROUTE = (lambda s, j, c: s.at[:, 1 - c], lambda l, me, k, c: l, lambda l, j, k, c: l, True)


def _gather_forward(lands, tag, own=False):
    n = len(lands)
    m = 4 if own else 3

    def body(*refs):
        ins, outs = refs[:n], refs[n:2 * n]
        send_sem, recv_sem = refs[2 * n:]
        x, y, c, chips = _place()
        slots = [2 * cx + cy for cx, cy in chips] + [2 * x + y]

        def copy(i, k, half):
            return pltpu.make_async_remote_copy(
                src_ref=ins[i].at[slots[k], half], dst_ref=outs[i].at[slots[k], half],
                send_sem=send_sem.at[m * i + k], recv_sem=recv_sem.at[m * i + k],
                device_id=(x, y, 1 - c), device_id_type=_MESH)

        copies = [copy(i, k, c) for i in range(n) for k in range(m)]
        for cp in copies:
            cp.start()
        for i in range(n):
            for k in range(m):
                copy(i, k, 1 - c).wait_recv()
        for cp in copies:
            cp.wait_send()

    return pl.pallas_call(
        body, name="gather_forward_to_sibling_" + tag,
        out_shape=[jax.ShapeDtypeStruct(a.shape, a.dtype) for a in lands],
        in_specs=[_ANY] * n, out_specs=[_ANY] * n,
        input_output_aliases={i: i for i in range(n)},
        scratch_shapes=[pltpu.SemaphoreType.DMA((m * n,)), pltpu.SemaphoreType.DMA((m * n,))],
    )(*lands)


def _swap_halves(grads, tag):
    n = len(grads)

    def body(*refs):
        ins, outs = refs[:n], refs[n:2 * n]
        send_sem, recv_sem = refs[2 * n:]
        x, y, c, _ = _place()
        copies = [pltpu.make_async_remote_copy(
            src_ref=ins[i].at[:, 1 - c], dst_ref=outs[i],
            send_sem=send_sem.at[i], recv_sem=recv_sem.at[i],
            device_id=(x, y, 1 - c), device_id_type=_MESH) for i in range(n)]
        for cp in copies:
            cp.start()
        for cp in copies:
            cp.wait()

    return pl.pallas_call(
        body, name="grad_swap_halves_" + tag,
        out_shape=[jax.ShapeDtypeStruct((N_CHIPS,) + g.shape[2:], g.dtype) for g in grads],
        in_specs=[_ANY] * n, out_specs=[_ANY] * n,
        scratch_shapes=[pltpu.SemaphoreType.DMA((n,)), pltpu.SemaphoreType.DMA((n,))],
    )(*grads)


def _sum_rows(h, C):
    return max(d for d in range(SUBLANES, h + 1, SUBLANES) if h % d == 0 and d * C <= 1 << 20)


SUM_STEPS = 4


def _pair_sums(gs, rs, c_idx, *, name):
    n = len(gs)
    rows = [g.shape[2] // SUM_STEPS for g in gs]

    def body(c_ref, *refs):
        for g_ref, r_ref, o_ref in zip(refs[:n], refs[n:2 * n], refs[2 * n:]):
            o_ref[...] = (g_ref[...].astype(F32) + r_ref[...].astype(F32)).astype(o_ref.dtype)

    return pl.pallas_call(
        body, name=name,
        out_shape=[jax.ShapeDtypeStruct((N_CHIPS,) + g.shape[2:], g.dtype) for g in gs],
        grid_spec=pltpu.PrefetchScalarGridSpec(
            num_scalar_prefetch=1, grid=(N_CHIPS, SUM_STEPS),
            in_specs=[pl.BlockSpec((None, None, tr, g.shape[3]), lambda j, i, s: (j, s[0], i, 0))
                      for g, tr in zip(gs, rows)]
            + [pl.BlockSpec((None, tr, g.shape[3]), lambda j, i, s: (j, i, 0)) for g, tr in zip(gs, rows)],
            out_specs=[pl.BlockSpec((None, tr, g.shape[3]), lambda j, i, s: (j, i, 0)) for g, tr in zip(gs, rows)]),
        compiler_params=_params("parallel", "parallel"),
    )(c_idx, *gs, *rs)


def _owner_sums(ss, rs, jc_idx, *, name):
    n = len(ss)
    rows = [s.shape[1] // SUM_STEPS for s in ss]

    def body(jc_ref, *refs):
        for s_ref, r_ref, m_ref, o_ref in zip(refs[:n], refs[n:2 * n], refs[2 * n:3 * n], refs[3 * n:]):
            acc = s_ref[...].astype(F32)
            for k in range(3):
                acc = acc + r_ref[k].astype(F32)
            m_ref[...] = acc
            o_ref[...] = acc

    outs = pl.pallas_call(
        body, name=name,
        out_shape=[jax.ShapeDtypeStruct(s.shape[1:], F32) for s in ss]
        + [jax.ShapeDtypeStruct((2,) + s.shape[1:], F32) for s in ss],
        grid_spec=pltpu.PrefetchScalarGridSpec(
            num_scalar_prefetch=1, grid=(SUM_STEPS,),
            in_specs=[pl.BlockSpec((None, tr, s.shape[2]), lambda i, p: (p[0], i, 0)) for s, tr in zip(ss, rows)]
            + [pl.BlockSpec((3, tr, s.shape[2]), lambda i, p: (0, i, 0)) for s, tr in zip(ss, rows)],
            out_specs=[pl.BlockSpec((tr, s.shape[2]), lambda i, p: (i, 0)) for s, tr in zip(ss, rows)]
            + [pl.BlockSpec((None, tr, s.shape[2]), lambda i, p: (p[1], i, 0)) for s, tr in zip(ss, rows)]),
        compiler_params=_params("parallel"),
    )(jc_idx, *ss, *rs)
    return outs[:n], outs[n:]


def _chip_sums(grads, c_idx, tag):
    views = [g.reshape(N_CHIPS, 2, g.shape[1] // 2, g.shape[2]) for g in grads]
    arrived = _swap_halves(views, tag)
    return _pair_sums(views, arrived, c_idx, name=f"grad_pair_sums_{tag}")


def _sum_devices(blocks):
    R = blocks.shape[2]
    tr = _sum_rows(R, 2 * N_CHIPS * LANES)

    def body(b_ref, o_ref):
        acc = b_ref[0, 0]
        for d in range(1, 2 * N_CHIPS):
            acc = acc + b_ref[d // 2, d % 2]
        o_ref[...] = acc

    return pl.pallas_call(
        body, name="sum_small_over_devices", out_shape=jax.ShapeDtypeStruct((R, LANES), F32),
        grid=(R // tr,),
        in_specs=[pl.BlockSpec((N_CHIPS, 2, tr, LANES), lambda i: (0, 0, i, 0))],
        out_specs=pl.BlockSpec((tr, LANES), lambda i: (i, 0)),
        compiler_params=_params("parallel"),
    )(blocks)


def _adamw(w, g, m, v, *, name):
    R, C = w.shape
    whole_fits = 7 * 2 * R * C * 4 <= VMEM_LIMIT_BYTES // 2
    tr = R if whole_fits else next(c for c in (256, 192, 128, 64, 32, 16, 8) if R % c == 0)

    def body(w_ref, g_ref, m_ref, v_ref, d_ref, nm_ref, nv_ref):
        g = g_ref[...]
        m = ADAM_B1 * m_ref[...] + (1.0 - ADAM_B1) * g
        v = ADAM_B2 * v_ref[...] + (1.0 - ADAM_B2) * (g * g)
        nm_ref[...] = m
        nv_ref[...] = v
        m_hat = m / (1.0 - ADAM_B1 ** ADAM_STEP)
        v_hat = v / (1.0 - ADAM_B2 ** ADAM_STEP)
        d_ref[...] = -ADAM_LR * (m_hat / (jnp.sqrt(v_hat) + ADAM_EPS) + ADAM_WD * w_ref[...])

    blk = pl.BlockSpec((tr, C), lambda i: (i, 0))
    sds = jax.ShapeDtypeStruct((R, C), F32)
    return pl.pallas_call(
        body, name=name, out_shape=(sds, sds, sds), grid=(R // tr,),
        in_specs=[blk] * 4, out_specs=(blk, blk, blk),
        compiler_params=_params("parallel"),
    )(w, g, m, v)


_TILE = SUBLANES * LANES


def _pack(arrays):
    rows = []
    for a in arrays:
        flat = a.reshape(-1)
        flat = jnp.pad(flat, (0, (-flat.shape[0]) % _TILE))
        rows.append(flat.reshape(-1, LANES))
    return jnp.concatenate(rows, axis=0)


def _unpack(buf, shapes):
    out, r = [], 0
    for s in shapes:
        size = math.prod(s)
        nr = -(-size // _TILE) * SUBLANES
        out.append(buf[r:r + nr].reshape(-1)[:size].reshape(s))
        r += nr
    return out


_BIG = ("w_in_a", "w_glu", "w_kv", "w_in_b", "w_mem_kv", "w_out")
_REPLICATED = ("pre_norm_g", "post_norm_g", "lam_re", "lam_im", "log_step", "b_re", "b_im", "c_re", "c_im",
               "kv_norm_g", "b_fgate", "mem_norm_g")
_SHARDED_SMALL = ("d_skip", "b_glu", "w_fgate")
_WEIGHTS = ("pre_norm_g", "post_norm_g", "w_in_a", "lam_re", "lam_im", "log_step", "b_re", "b_im", "c_re",
            "c_im", "d_skip", "w_glu", "b_glu", "kv_norm_g", "w_kv", "w_fgate", "b_fgate", "w_in_b",
            "mem_norm_g", "w_mem_kv", "w_out")


def _halves(a):
    return a.reshape(2, a.shape[0] // 2, a.shape[1])


def _unhalve(a):
    return a.reshape(N_CHIPS, 2 * a.shape[2], a.shape[3])


def _columns(a):
    return jnp.transpose(a, (1, 0, 2)).reshape(a.shape[1], N_CHIPS * a.shape[2])


def kernel(x, mem, pre_norm_g, post_norm_g, w_in_a, lam_re, lam_im, log_step, b_re, b_im, c_re, c_im, d_skip, w_glu, b_glu, kv_norm_g, w_kv, w_fgate, b_fgate, w_in_b, mem_norm_g, w_mem_kv, w_out, loss_target, m_pre_norm_g, m_post_norm_g, m_w_in_a, m_lam_re, m_lam_im, m_log_step, m_b_re, m_b_im, m_c_re, m_c_im, m_d_skip, m_w_glu, m_b_glu, m_kv_norm_g, m_w_kv, m_w_fgate, m_b_fgate, m_w_in_b, m_mem_norm_g, m_w_mem_kv, m_w_out, v_pre_norm_g, v_post_norm_g, v_w_in_a, v_lam_re, v_lam_im, v_log_step, v_b_re, v_b_im, v_c_re, v_c_im, v_d_skip, v_w_glu, v_b_glu, v_kv_norm_g, v_w_kv, v_w_fgate, v_b_fgate, v_w_in_b, v_mem_norm_g, v_w_mem_kv, v_w_out):
    a = dict(locals())
    xi, yi, ci = lax.axis_index("x"), lax.axis_index("y"), lax.axis_index("c")
    chip = 2 * xi + yi
    c_idx = jnp.reshape(ci, (1,)).astype(jnp.int32)
    jc_idx = jnp.stack([chip, ci]).astype(jnp.int32)

    vec = jnp.zeros((2 * SUBLANES, MAIN_WIDTH // N_CHIPS), F32)
    vec = vec.at[0].set(a["d_skip"][0]).at[1].set(a["b_glu"][0])
    def own_slot(gathered, parts):
        return [lax.dynamic_update_index_in_dim(g, p, chip, 0) for g, p in zip(gathered, parts)]

    parts_a = [_halves(a["w_in_a"][0].astype(BF16)), _halves(vec)]
    parts_b = [_halves(a["w_glu"][0].astype(BF16)),
               *[_halves(a["w_mem_kv"][i].astype(BF16)) for i in range(2)],
               *[_halves(a["w_out"][i].astype(BF16)) for i in range(2)]]
    parts_c = [_halves(a["w_kv"].astype(BF16)), _halves(_pad_lanes(a["w_fgate"]).astype(BF16)),
               _halves(a["w_in_b"][0].astype(BF16))]
    travelling, token = {}, a["pre_norm_g"]
    for tag, parts in (("a", parts_a), ("b", parts_b), ("c", parts_c)):
        lands = [lax.empty((N_CHIPS,) + p.shape, p.dtype) for p in parts]
        travelling[tag], token = _ici_start(parts, lands, token, _GATHER_ROUTE, name=f"gather_{tag}_start")

    def fetch(tag, after):
        parts, lands = _ici_wait(travelling[tag], after, _GATHER_ROUTE, name=f"gather_{tag}_wait")
        full = own_slot(_gather_forward(lands, tag), parts)
        if tag == "a":
            w_in_a, vecs = full
            return dict(w_in_a=_columns(_unhalve(w_in_a)), d_skip=vecs[:, 0, 0, :].reshape(MAIN_WIDTH),
                        b_glu=vecs[:, 0, 1, :].reshape(MAIN_WIDTH))
        if tag == "b":
            w_glu, w_mk0, w_mk1, w_out0, w_out1 = full
            return dict(w_glu=w_glu.reshape(MAIN_WIDTH, MAIN_WIDTH),
                        w_mem_kv=[m.reshape(D_MODEL, 2 * MEM_WIDTH) for m in (w_mk0, w_mk1)],
                        w_out=[o.reshape(D_MODEL, D_MODEL) for o in (w_out0, w_out1)])
        w_kv, w_fg, w_in_b = full
        return dict(w_kv=_columns(_unhalve(w_kv)), w_fgate=w_fg.reshape(D_MODEL, LANES),
                    w_in_b=_columns(_unhalve(w_in_b)))

    w = dict(
        pre_norm_g=token, post_norm_g=a["post_norm_g"], mem_norm_g=a["mem_norm_g"],
        kv_norm_g=a["kv_norm_g"], b_fgate=a["b_fgate"],
        lam_re=a["lam_re"][0], lam_im=a["lam_im"][0], log_step=a["log_step"][0],
        b_re=a["b_re"][0], b_im=a["b_im"][0], c_re=a["c_re"][0], c_im=a["c_im"][0])

    sent = {}

    swapping = {}

    def grads_ready(event, g, token):
        tag = event.split("_")[0]
        if event in ("b", "a1"):
            big = {"b": lambda: [g["w_kv"], g["w_in_b"], g["w_mem_kv_1"].reshape(N_CHIPS, -1, 2 * MEM_WIDTH),
                                 g["w_out_1"].reshape(N_CHIPS, -1, D_MODEL)],
                   "a1": lambda: [g["w_glu"].reshape(N_CHIPS, -1, MAIN_WIDTH),
                                  g["w_mem_kv_0"].reshape(N_CHIPS, -1, 2 * MEM_WIDTH),
                                  g["w_out_0"].reshape(N_CHIPS, -1, D_MODEL)]}[tag]()
            views = [b.reshape(N_CHIPS, 2, b.shape[1] // 2, b.shape[2]) for b in big]
            lands = [lax.empty((N_CHIPS,) + v.shape[2:], v.dtype) for v in views]
            swapping[tag], token = _ici_start(views, lands, token, _SWAP_ROUTE, name=f"grad_swap_{tag}_start")
            return token
        if event == "a2":
            sums = _chip_sums([g["w_in_a"]], c_idx, tag)
        else:
            views, arrived = _ici_wait(swapping[tag], token, _SWAP_ROUTE, name=f"grad_swap_{tag}_wait")
            sums = _pair_sums(views, arrived, c_idx, name=f"grad_pair_sums_{tag}")
        lands = [lax.empty((3,) + s.shape[1:], s.dtype) for s in sums]
        sent[tag], token = _ici_start(sums, lands, token, _SCATTER_ROUTE, name=f"grad_send_{tag}_start")
        return token

    loss_row, grad_x, g = _local_step(a["x"][0], a["mem"][0], a["loss_target"][0], w, fetch, grads_ready)

    small_names = _REPLICATED + _SHARDED_SMALL
    pack = _pack([g[n] for n in small_names])
    blocks = lax.empty((N_CHIPS, 2) + pack.shape, F32)
    small_sent, token = _ici_start([pack], [blocks], loss_row, _BLOCK_ROUTE, name="small_sums_start")

    sharing = {}
    for tag in ("b", "a1", "a2"):
        sums, arrived = _ici_wait(sent[tag], [grad_x, token], _SCATTER_ROUTE, name=f"grad_send_{tag}_wait")
        mine, bufs = _owner_sums(sums, arrived, jc_idx, name=f"grad_owner_sums_{tag}")
        sharing[tag], token = _ici_start(mine, bufs, token, _SHARE_ROUTE, name=f"grad_share_{tag}_start")
    loss = lax.psum(jnp.sum(token), MESH_AXES)

    def shared(tag, after):
        _, bufs = _ici_wait(sharing[tag], after, _SHARE_ROUTE, name=f"grad_share_{tag}_wait")
        return [b.reshape(-1, b.shape[2]) for b in bufs]

    grads, delta, new_m, new_v = {}, {}, {}, {}

    def adam(n):
        shape = a[n].shape
        d2 = (-1, shape[-1])
        d, m, v = _adamw(a[n].reshape(d2), grads[n].reshape(d2), a["m_" + n].reshape(d2),
                         a["v_" + n].reshape(d2), name="adamw_" + n)
        delta[n], new_m[n], new_v[n] = d.reshape(shape), m.reshape(shape), v.reshape(shape)
        return d

    r_kv, r_in_b, r_mk1, r_out1 = shared("b", token)
    grads["w_kv"], grads["w_in_b"] = r_kv, r_in_b[None]
    done = [adam("w_kv"), adam("w_in_b")]
    r_glu, r_mk0, r_out0 = shared("a1", done)
    grads["w_glu"], grads["w_mem_kv"], grads["w_out"] = r_glu[None], jnp.stack([r_mk0, r_mk1]), jnp.stack([r_out0, r_out1])
    done = [adam("w_glu"), adam("w_mem_kv"), adam("w_out")]
    (r_in_a,) = shared("a2", done)
    grads["w_in_a"] = r_in_a[None]
    adam("w_in_a")

    (pack,), (blocks,) = _ici_wait(small_sent, [delta[n] for n in _BIG], _BLOCK_ROUTE, name="small_sums_wait")
    blocks = lax.dynamic_update_slice(blocks, pack[None, None], (chip, ci, 0, 0))
    (blocks,) = _gather_forward([blocks], "small", own=True)
    small = dict(zip(small_names, _unpack(_sum_devices(blocks), [g[n].shape for n in small_names])))
    for n in _REPLICATED:
        grads[n] = small[n].reshape(a[n].shape)
    nd = MAIN_WIDTH // N_CHIPS
    grads["d_skip"] = lax.dynamic_slice(small["d_skip"], (chip * nd,), (nd,))[None]
    grads["b_glu"] = lax.dynamic_slice(small["b_glu"], (chip * nd,), (nd,))[None]
    nf = D_MODEL // N_CHIPS
    grads["w_fgate"] = lax.dynamic_slice(small["w_fgate"], (chip * nf, 0), (nf, FOX_HEADS))

    shapes = [a[n].shape for n in small_names]
    d, m, v = _adamw(_pack([a[n] for n in small_names]), _pack([grads[n] for n in small_names]),
                     _pack([a["m_" + n] for n in small_names]), _pack([a["v_" + n] for n in small_names]),
                     name="adamw_small")
    for n, dd, mm, vv in zip(small_names, _unpack(d, shapes), _unpack(m, shapes), _unpack(v, shapes)):
        delta[n], new_m[n], new_v[n] = dd, mm, vv

    return (loss, grad_x[None], *[grads[n] for n in _WEIGHTS], *[delta[n] for n in _WEIGHTS],
            *[new_m[n] for n in _WEIGHTS], *[new_v[n] for n in _WEIGHTS])
```

```python
import math

import jax
import jax.numpy as jnp
from jax import lax
from jax.experimental import pallas as pl
from jax.experimental.pallas import tpu as pltpu

F32 = jnp.float32
BF16 = jnp.bfloat16

D_MODEL = 2048
N_MEM = 256
MAIN_WIDTH = 1536
MEM_WIDTH = 512
IN_WIDTH = 2 * MAIN_WIDTH + 2 * MEM_WIDTH
HEAD_DIM = 128
FOX_HEADS = MAIN_WIDTH // HEAD_DIM
MEM_HEADS = MEM_WIDTH // HEAD_DIM
SSM_GROUP = 16
SSM_GROUPS = MAIN_WIDTH // SSM_GROUP
SSM_STATE = 64
GROUPS_PER_BLOCK = 8
SSM_BLOCKS = SSM_GROUPS // GROUPS_PER_BLOCK
STATE_COLS = GROUPS_PER_BLOCK * SSM_STATE
EPS = 1e-6
ADAM_LR = 0.001
ADAM_B1 = 0.9
ADAM_B2 = 0.999
ADAM_EPS = 1e-08
ADAM_WD = 0.01
ADAM_STEP = 10
N_CHIPS = 4
LANES = 128
SUBLANES = 8
VMEM_LIMIT_BYTES = 56 * 1024 * 1024
NEG_BIG = -1e30
MESH_AXES = ("x", "y", "c")


def _params(*sem):
    return pltpu.CompilerParams(dimension_semantics=sem if sem else None,
                                vmem_limit_bytes=VMEM_LIMIT_BYTES)


def _sigmoid(x):
    return 1.0 / (1.0 + jnp.exp(-x))


def _gelu(x):
    c = math.sqrt(2.0 / math.pi)
    return 0.5 * x * (1.0 + jnp.tanh(c * (x + 0.044715 * (x * x * x))))


def _gelu_grad(x):
    c = math.sqrt(2.0 / math.pi)
    t = jnp.tanh(c * (x + 0.044715 * (x * x * x)))
    return 0.5 * (1.0 + t) + 0.5 * x * (1.0 - t * t) * (c * (1.0 + 3.0 * 0.044715 * (x * x)))


def _silu_and_grad(z):
    s = _sigmoid(z)
    return z * s, s * (1.0 + z * (1.0 - s))


_TILE_CHOICES = (4096, 3072, 2048, 1536, 1024, 768, 512, 384, 256, LANES)


def _tile(n, cap):
    return next(c for c in _TILE_CHOICES if c <= cap and n % c == 0)


def _mm(a, b, *, name, ta=False, tb=False, out_dtype=F32, shards=1, tm=1024, tn=1024, tk=4096):
    if ta:
        K, M = a.shape
    else:
        M, K = a.shape
    if tb:
        N, kb = b.shape
    else:
        kb, N = b.shape
    assert K == kb, (a.shape, b.shape)
    ns = N // shards
    tm, tn, tk = _tile(M, tm), _tile(ns, tn), _tile(K, tk)
    assert M % tm == 0 and ns % tn == 0 and K % tk == 0 and N % shards == 0
    nk = K // tk
    dn = (((0 if ta else 1,), (1 if tb else 0,)), ((), ()))

    def body(a_ref, b_ref, o_ref, *acc):
        prod = lax.dot_general(a_ref[...].astype(BF16), b_ref[...].astype(BF16), dn, preferred_element_type=F32)
        if nk == 1:
            o_ref[...] = prod.astype(o_ref.dtype)
            return
        acc_ref, = acc
        k = pl.program_id(2)

        @pl.when(k == 0)
        def _():
            acc_ref[...] = jnp.zeros_like(acc_ref)

        acc_ref[...] += prod

        @pl.when(k == nk - 1)
        def _():
            o_ref[...] = acc_ref[...].astype(o_ref.dtype)

    a_spec = (pl.BlockSpec((tk, tm), lambda i, j, k: (k, i)) if ta
              else pl.BlockSpec((tm, tk), lambda i, j, k: (i, k)))
    b_spec = (pl.BlockSpec((tn, tk), lambda i, j, k: (j, k)) if tb
              else pl.BlockSpec((tk, tn), lambda i, j, k: (k, j)))
    if shards == 1:
        out_shape = jax.ShapeDtypeStruct((M, N), out_dtype)
        o_spec = pl.BlockSpec((tm, tn), lambda i, j, k: (i, j))
    else:
        nb = ns // tn
        out_shape = jax.ShapeDtypeStruct((shards, M, ns), out_dtype)
        o_spec = pl.BlockSpec((None, tm, tn), lambda i, j, k: (j // nb, i, j % nb))
    return pl.pallas_call(
        body, name=name, out_shape=out_shape,
        grid=(M // tm, N // tn, nk),
        in_specs=[a_spec, b_spec], out_specs=o_spec,
        scratch_shapes=[] if nk == 1 else [pltpu.VMEM((tm, tn), F32)],
        compiler_params=_params("parallel", "parallel", "arbitrary"),
    )(a, b)


def _rmsnorm_fwd(x, g, *, name, res=None, out_dtype=F32, tr=256):
    L, D = x.shape
    tr = min(tr, L)
    has_res = res is not None

    def body(*refs):
        if has_res:
            x_ref, g_ref, r_ref, o_ref = refs
        else:
            x_ref, g_ref, o_ref = refs
        xf = x_ref[...]
        r = lax.rsqrt(jnp.mean(xf * xf, axis=-1, keepdims=True) + EPS)
        y = xf * r * g_ref[...]
        if has_res:
            y = r_ref[...] + y
        o_ref[...] = y.astype(o_ref.dtype)

    row = pl.BlockSpec((tr, D), lambda i: (i, 0))
    vec = pl.BlockSpec((1, D), lambda i: (0, 0))
    ins = [x, g.reshape(1, D)] + ([res] if has_res else [])
    return pl.pallas_call(
        body, name=name, out_shape=jax.ShapeDtypeStruct((L, D), out_dtype),
        grid=(L // tr,), in_specs=[row, vec] + ([row] if has_res else []), out_specs=row,
        compiler_params=_params("parallel"),
    )(*ins)


def _rmsnorm_bwd(x, g, dy, *, name, adds=(), dx_dtype=F32, tr=256):
    L, D = x.shape
    tr = min(tr, L)
    dys = dy if isinstance(dy, tuple) else (dy,)
    n_dy, n_add = len(dys), len(adds)

    def body(*refs):
        x_ref, g_ref = refs[:2]
        dy_refs = refs[2:2 + n_dy]
        add_refs = refs[2 + n_dy:2 + n_dy + n_add]
        dx_ref, dg_ref = refs[2 + n_dy + n_add:]
        xf = x_ref[...]
        dyf = dy_refs[0][...].astype(F32)
        for d_ref in dy_refs[1:]:
            dyf = dyf + d_ref[...].astype(F32)
        r = lax.rsqrt(jnp.mean(xf * xf, axis=-1, keepdims=True) + EPS)
        gy = dyf * g_ref[...]
        c = jnp.mean(xf * gy, axis=-1, keepdims=True) * (r * r * r)
        dx = gy * r - xf * c
        for a_ref in add_refs:
            dx = dx + a_ref[...].astype(F32)
        dx_ref[...] = dx.astype(dx_ref.dtype)

        @pl.when(pl.program_id(0) == 0)
        def _():
            dg_ref[...] = jnp.zeros_like(dg_ref)

        dg_ref[...] += jnp.sum(dyf * xf * r, axis=0, keepdims=True)

    row = pl.BlockSpec((tr, D), lambda i: (i, 0))
    vec = pl.BlockSpec((1, D), lambda i: (0, 0))
    dx, dg = pl.pallas_call(
        body, name=name,
        out_shape=(jax.ShapeDtypeStruct((L, D), dx_dtype), jax.ShapeDtypeStruct((1, D), F32)),
        grid=(L // tr,), in_specs=[row, vec] + [row] * (n_dy + n_add), out_specs=(row, vec),
        compiler_params=_params("arbitrary"),
    )(x, g.reshape(1, D), *dys, *adds)
    return dx, dg.reshape(D)


def _rmsnorm_bwd_pair(x, g1, dy1, g2, dy2, *, name, adds=(), tr=256):
    L, D = x.shape
    tr = min(tr, L)
    dy1s = dy1 if isinstance(dy1, tuple) else (dy1,)
    n1, n_add = len(dy1s), len(adds)

    def body(*refs):
        x_ref, g1_ref, g2_ref = refs[:3]
        dy1_refs = refs[3:3 + n1]
        dy2_ref = refs[3 + n1]
        add_refs = refs[4 + n1:4 + n1 + n_add]
        dx_ref, dg1_ref, dg2_ref = refs[4 + n1 + n_add:]
        xf = x_ref[...]
        d1 = dy1_refs[0][...].astype(F32)
        for d_ref in dy1_refs[1:]:
            d1 = d1 + d_ref[...].astype(F32)
        d2 = dy2_ref[...].astype(F32)
        r = lax.rsqrt(jnp.mean(xf * xf, axis=-1, keepdims=True) + EPS)
        gy = d1 * g1_ref[...] + d2 * g2_ref[...]
        c = jnp.mean(xf * gy, axis=-1, keepdims=True) * (r * r * r)
        dx = gy * r - xf * c
        for a_ref in add_refs:
            dx = dx + a_ref[...].astype(F32)
        dx_ref[...] = dx

        @pl.when(pl.program_id(0) == 0)
        def _():
            dg1_ref[...] = jnp.zeros_like(dg1_ref)
            dg2_ref[...] = jnp.zeros_like(dg2_ref)

        xr = xf * r
        dg1_ref[...] += jnp.sum(d1 * xr, axis=0, keepdims=True)
        dg2_ref[...] += jnp.sum(d2 * xr, axis=0, keepdims=True)

    row = pl.BlockSpec((tr, D), lambda i: (i, 0))
    vec = pl.BlockSpec((1, D), lambda i: (0, 0))
    dx, dg1, dg2 = pl.pallas_call(
        body, name=name,
        out_shape=(jax.ShapeDtypeStruct((L, D), F32), jax.ShapeDtypeStruct((1, D), F32),
                   jax.ShapeDtypeStruct((1, D), F32)),
        grid=(L // tr,), in_specs=[row, vec, vec] + [row] * (n1 + 1 + n_add), out_specs=(row, vec, vec),
        compiler_params=_params("arbitrary"),
    )(x, g1.reshape(1, D), g2.reshape(1, D), *dy1s, dy2, *adds)
    return dx, dg1.reshape(D), dg2.reshape(D)


def _final_norm_loss(o, g, res, target, *, tr=256):
    L, D = o.shape
    tr = min(tr, L)

    def body(o_ref, g_ref, r_ref, t_ref, dh_ref, loss_ref):
        xf = o_ref[...]
        r = lax.rsqrt(jnp.mean(xf * xf, axis=-1, keepdims=True) + EPS)
        e = (r_ref[...] + xf * r * g_ref[...]) - t_ref[...]
        dh_ref[...] = e * (1.0 / D)

        @pl.when(pl.program_id(0) == 0)
        def _():
            loss_ref[...] = jnp.zeros_like(loss_ref)

        loss_ref[...] += jnp.sum(e * e, axis=0, keepdims=True) * (0.5 / D)

    row = pl.BlockSpec((tr, D), lambda i: (i, 0))
    vec = pl.BlockSpec((1, D), lambda i: (0, 0))
    dh, lp = pl.pallas_call(
        body, name="post_norm_1_loss",
        out_shape=(jax.ShapeDtypeStruct((L, D), F32), jax.ShapeDtypeStruct((1, D), F32)),
        grid=(L // tr,), in_specs=[row, vec, row, row], out_specs=(row, vec),
        compiler_params=_params("arbitrary"),
    )(o, g.reshape(1, D), res, target)
    return dh, lp


def _s5_coeffs(lr, li, ls):
    dt = jnp.exp(ls)
    mag = jnp.exp(lr * dt)
    ar = mag * jnp.cos(li * dt)
    ai = mag * jnp.sin(li * dt)
    den = lr * lr + li * li
    cr = ((ar - 1.0) * lr + ai * li) / den
    ci = (ai * lr - (ar - 1.0) * li) / den
    return dt, ar, ai, den, cr, ci


def _s5_prep(lam_re, lam_im, log_step, b_re_t, b_im_t):
    G, P = lam_re.shape
    H = b_re_t.shape[1]

    def body(lr_ref, li_ref, ls_ref, br_ref, bi_ref, ar_ref, ai_ref, bbr_ref, bbi_ref):
        _, ar, ai, _, cr, ci = _s5_coeffs(lr_ref[...], li_ref[...], ls_ref[...])
        ar_ref[...] = ar
        ai_ref[...] = ai
        br, bi = br_ref[...], bi_ref[...]
        crb, cib = cr[:, None, :], ci[:, None, :]
        bbr_ref[...] = crb * br - cib * bi
        bbi_ref[...] = crb * bi + cib * br

    return pl.pallas_call(
        body, name="s5_prep",
        out_shape=(jax.ShapeDtypeStruct((G, P), F32), jax.ShapeDtypeStruct((G, P), F32),
                   jax.ShapeDtypeStruct((G, H, P), F32), jax.ShapeDtypeStruct((G, H, P), F32)),
        compiler_params=_params(),
    )(lam_re, lam_im, log_step.reshape(G, 1), b_re_t, b_im_t)


def _s5_prep_bwd(lam_re, lam_im, log_step, b_re_t, b_im_t, d_ar, d_ai, d_bbr, d_bbi):
    G, P = lam_re.shape
    H = b_re_t.shape[1]

    def body(lr_ref, li_ref, ls_ref, br_ref, bi_ref, dar_ref, dai_ref, dbbr_ref, dbbi_ref,
             dlr_ref, dli_ref, dls_ref, dbr_ref, dbi_ref):
        lr, li = lr_ref[...], li_ref[...]
        dt, ar, ai, den, cr, ci = _s5_coeffs(lr, li, ls_ref[...])
        br, bi = br_ref[...], bi_ref[...]
        gbr, gbi = dbbr_ref[...], dbbi_ref[...]
        crb, cib = cr[:, None, :], ci[:, None, :]
        dbr_ref[...] = crb * gbr + cib * gbi
        dbi_ref[...] = crb * gbi - cib * gbr
        gcr = jnp.sum(br * gbr + bi * gbi, axis=1)
        gci = jnp.sum(br * gbi - bi * gbr, axis=1)
        ilr, ili = lr / den, -li / den
        gar = dar_ref[...] + (ilr * gcr + ili * gci)
        gai = dai_ref[...] + (ilr * gci - ili * gcr)
        qr, qi = cr * ilr - ci * ili, cr * ili + ci * ilr
        glr = -(qr * gcr + qi * gci)
        gli = -(qr * gci - qi * gcr)
        glr = glr + dt * (ar * gar + ai * gai)
        gli = gli + dt * (ar * gai - ai * gar)
        wr, wi = lr * ar - li * ai, lr * ai + li * ar
        gdt = jnp.sum(wr * gar + wi * gai, axis=1, keepdims=True)
        dlr_ref[...] = glr
        dli_ref[...] = gli
        dls_ref[...] = gdt * dt

    return pl.pallas_call(
        body, name="s5_prep_bwd",
        out_shape=(jax.ShapeDtypeStruct((G, P), F32), jax.ShapeDtypeStruct((G, P), F32),
                   jax.ShapeDtypeStruct((G, 1), F32),
                   jax.ShapeDtypeStruct((G, H, P), F32), jax.ShapeDtypeStruct((G, H, P), F32)),
        compiler_params=_params(),
    )(lam_re, lam_im, log_step.reshape(G, 1), b_re_t, b_im_t, d_ar, d_ai, d_bbr, d_bbi)


def _s5_block_mats(bbr_t, bbi_t, c_re, c_im):
    bmat = _s5_expand(bbr_t, bbi_t)
    cmat = jnp.transpose(_s5_expand(c_re, -c_im), (0, 2, 1))
    return bmat.astype(BF16), cmat.astype(BF16)


def _s5_diag_mask():
    r = lax.broadcasted_iota(jnp.int32, (LANES, 2 * STATE_COLS), 0) // SSM_GROUP
    c = (lax.broadcasted_iota(jnp.int32, (LANES, 2 * STATE_COLS), 1) % STATE_COLS) // SSM_STATE
    return (r == c).astype(F32)


def _s5_expand(re, im):
    re = jnp.tile(re.reshape(SSM_BLOCKS, LANES, SSM_STATE), (1, 1, GROUPS_PER_BLOCK))
    im = jnp.tile(im.reshape(SSM_BLOCKS, LANES, SSM_STATE), (1, 1, GROUPS_PER_BLOCK))
    return jnp.concatenate([re, im], axis=-1) * _s5_diag_mask()[None]


def _s5_unfold(dmat):
    d = dmat.reshape(SSM_GROUPS, SSM_GROUP, 2, SSM_STATE)
    return jnp.transpose(d, (2, 0, 1, 3))


def _s5_a_rows(ar, ai):
    a = jnp.concatenate([ar.reshape(SSM_BLOCKS, STATE_COLS), ai.reshape(SSM_BLOCKS, STATE_COLS)], axis=1)
    return jnp.broadcast_to(a[:, None, :], (SSM_BLOCKS, SUBLANES, 2 * STATE_COLS))


def _to_step_major(src_ref, dst_ref, seg):
    for s in range(SUBLANES):
        dst_ref[pl.ds(s, seg, stride=SUBLANES), :] = src_ref[pl.ds(seg * s, seg), :]


def _segment_rows(ref, s, seg):
    return ref[pl.ds(s, seg, stride=SUBLANES), :]


def _cmul(ar, ai, xr, xi):
    return ar * xr - ai * xi, ar * xi + ai * xr


def _s5_tables(a_ref, pw_s, pwr_s, S, seg):
    ar, ai = a_ref[:, :S], a_ref[:, S:]

    def step(i, c):
        pr, pi = c
        pw_s[i, :, :S] = pr
        pw_s[i, :, S:] = pi
        nr, ni = _cmul(ar, ai, pr, pi)
        pwr_s[seg - 1 - i, :, :S] = nr
        pwr_s[seg - 1 - i, :, S:] = ni
        return nr, ni

    pr, pi = lax.fori_loop(0, seg, step, (jnp.ones_like(ar), jnp.zeros_like(ai)))
    pw_s[seg, :, :S] = pr
    pw_s[seg, :, S:] = pi


def _s5_fwd(proj, bmat, cmat, a_rows, d_skip, *, tc=512):
    L = proj.shape[0]
    tc = min(tc, L)
    nt = L // tc
    seg = tc // SUBLANES
    S = STATE_COLS

    def body(u_ref, b_ref, c_ref, a_ref, d_ref, y_ref, yg_ref, xp_ref,
             bu_s, xp_s, pw_s, pwr_s, carry_s, e_s, up_s, yc_s):
        @pl.when(pl.program_id(1) == 0)
        def _():
            carry_s[...] = jnp.zeros_like(carry_s)
            _s5_tables(a_ref, pw_s, pwr_s, S, seg)

        ar, ai = a_ref[:, :S], a_ref[:, S:]
        _to_step_major(u_ref, up_s, seg)
        bu = jnp.dot(up_s[...].astype(BF16), b_ref[...], preferred_element_type=F32)
        bu_s[...] = bu.reshape(seg, SUBLANES, 2 * S)

        def step(i, carry):
            cr, ci = carry
            xp_s[i, :, :S] = cr
            xp_s[i, :, S:] = ci
            return ar * cr - ai * ci + bu_s[i, :, :S], ar * ci + ai * cr + bu_s[i, :, S:]

        zero = jnp.zeros((SUBLANES, S), F32)
        fr, fi = lax.fori_loop(0, seg, step, (zero, zero))
        pr, pi = pw_s[seg, 0:1, :S], pw_s[seg, 0:1, S:]
        er, ei = carry_s[0:1, :S], carry_s[0:1, S:]
        for s in range(SUBLANES):
            e_s[s:s + 1, :S] = er
            e_s[s:s + 1, S:] = ei
            tr, ti = _cmul(pr, pi, er, ei)
            er, ei = fr[s:s + 1] + tr, fi[s:s + 1] + ti
        carry_s[0:1, :S] = er
        carry_s[0:1, S:] = ei
        pw = pw_s[0:seg]
        tr, ti = _cmul(pw[:, :, :S], pw[:, :, S:], e_s[:, :S][None], e_s[:, S:][None])
        xl = xp_s[...]
        xp = jnp.concatenate([xl[:, :, :S] + tr, xl[:, :, S:] + ti], axis=-1).reshape(tc, 2 * S)
        xp_ref[...] = xp
        a1r, a1i = ar[0:1], ai[0:1]
        x_re = a1r * xp[:, :S] - a1i * xp[:, S:] + bu[:, :S]
        x_im = a1r * xp[:, S:] + a1i * xp[:, :S] + bu[:, S:]
        xs = jnp.concatenate([x_re, x_im], axis=1).astype(BF16)
        yc_s[...] = jnp.dot(xs, c_ref[...], preferred_element_type=F32)
        for s in range(SUBLANES):
            rows = pl.ds(seg * s, seg)
            y = _segment_rows(yc_s, s, seg) + d_ref[...] * u_ref[rows, :]
            y_ref[rows, :] = y
            yg_ref[rows, :] = _gelu(y).astype(BF16)

    return pl.pallas_call(
        body, name="s5_fwd",
        out_shape=(jax.ShapeDtypeStruct((L, MAIN_WIDTH), F32),
                   jax.ShapeDtypeStruct((L, MAIN_WIDTH), BF16),
                   jax.ShapeDtypeStruct((L, SSM_BLOCKS * 2 * S), F32)),
        grid=(SSM_BLOCKS, nt),
        in_specs=[pl.BlockSpec((tc, LANES), lambda b, t: (t, b)),
                  pl.BlockSpec((None, LANES, 2 * S), lambda b, t: (b, 0, 0)),
                  pl.BlockSpec((None, 2 * S, LANES), lambda b, t: (b, 0, 0)),
                  pl.BlockSpec((None, SUBLANES, 2 * S), lambda b, t: (b, 0, 0)),
                  pl.BlockSpec((1, LANES), lambda b, t: (0, b))],
        out_specs=(pl.BlockSpec((tc, LANES), lambda b, t: (t, b)),
                   pl.BlockSpec((tc, LANES), lambda b, t: (t, b)),
                   pl.BlockSpec((tc, 2 * S), lambda b, t: (t, b))),
        scratch_shapes=[pltpu.VMEM((seg, SUBLANES, 2 * S), F32),
                        pltpu.VMEM((seg, SUBLANES, 2 * S), F32),
                        pltpu.VMEM((seg + 1, SUBLANES, 2 * S), F32),
                        pltpu.VMEM((seg, SUBLANES, 2 * S), F32),
                        pltpu.VMEM((SUBLANES, 2 * S), F32),
                        pltpu.VMEM((SUBLANES, 2 * S), F32),
                        pltpu.VMEM((tc, LANES), F32),
                        pltpu.VMEM((tc, LANES), F32)],
        compiler_params=_params("parallel", "arbitrary"),
    )(proj, bmat, cmat, a_rows, d_skip.reshape(1, MAIN_WIDTH))


def _s5_bwd(proj, dyg_a, dyg_b, y, xp, bmat, cmat, a_rows, d_skip, dproj, *, tc=512):
    L = proj.shape[0]
    tc = min(tc, L)
    nt = L // tc
    seg = tc // SUBLANES
    S = STATE_COLS
    nn = (((1,), (1,)), ((), ()))
    tn = (((0,), (0,)), ((), ()))

    def fold_diagonal(acc_ref, mask_ref, fold_ref):
        x = acc_ref[...] * mask_ref[...]
        hi = x.astype(BF16)
        rest = x - hi.astype(F32)
        mid = rest.astype(BF16)
        low = (rest - mid.astype(F32)).astype(BF16)
        return sum(jnp.dot(piece, fold_ref[...], preferred_element_type=F32) for piece in (hi, mid, low))

    def body(u_ref, dyga_ref, dygb_ref, y_ref, xp_ref, b_ref, c_ref, a_ref, d_ref, mask_ref, fold_ref, dp_hbm,
             du_ref, dbd_ref, dcd_ref, da_ref, dd_ref,
             dl_s, pw_s, pwr_s, carry_s, e_s, up_s, dy_s, dyp_s, dup_s, db_ref, dc_ref):
        @pl.when(pl.program_id(1) == 0)
        def _():
            carry_s[...] = jnp.zeros_like(carry_s)
            db_ref[...] = jnp.zeros_like(db_ref)
            dc_ref[...] = jnp.zeros_like(dc_ref)
            da_ref[...] = jnp.zeros_like(da_ref)
            dd_ref[...] = jnp.zeros_like(dd_ref)
            _s5_tables(a_ref, pw_s, pwr_s, S, seg)

        ar, ai = a_ref[:, :S], a_ref[:, S:]
        a1r, a1i = ar[0:1], ai[0:1]
        u = u_ref[...]
        dy = (dyga_ref[...] + dygb_ref[...]) * _gelu_grad(y_ref[...])
        dy_s[...] = dy
        xp = xp_ref[...]
        _to_step_major(u_ref, up_s, seg)
        _to_step_major(dy_s, dyp_s, seg)
        ubp = up_s[...].astype(BF16)
        dyp = dyp_s[...].astype(BF16)
        bu = jnp.dot(ubp, b_ref[...], preferred_element_type=F32)
        x_re = a1r * xp[:, :S] - a1i * xp[:, S:] + bu[:, :S]
        x_im = a1r * xp[:, S:] + a1i * xp[:, :S] + bu[:, S:]
        xs = jnp.concatenate([x_re, x_im], axis=1).astype(BF16)
        dc_ref[...] += lax.dot_general(dyp, xs, tn, preferred_element_type=F32)
        dx = lax.dot_general(dyp, c_ref[...], nn, preferred_element_type=F32)
        dl_s[...] = dx.reshape(seg, SUBLANES, 2 * S)

        def step(k, carry):
            cr, ci = carry
            i = seg - 1 - k
            lr = dl_s[i, :, :S] + (ar * cr + ai * ci)
            li = dl_s[i, :, S:] + (ar * ci - ai * cr)
            dl_s[i, :, :S] = lr
            dl_s[i, :, S:] = li
            return lr, li

        zero = jnp.zeros((SUBLANES, S), F32)
        fr, fi = lax.fori_loop(0, seg, step, (zero, zero))
        pr, pi = pw_s[seg, 0:1, :S], pw_s[seg, 0:1, S:]
        er, ei = carry_s[0:1, :S], carry_s[0:1, S:]
        for s in range(SUBLANES - 1, -1, -1):
            e_s[s:s + 1, :S] = er
            e_s[s:s + 1, S:] = ei
            er, ei = fr[s:s + 1] + (pr * er + pi * ei), fi[s:s + 1] + (pr * ei - pi * er)
        carry_s[0:1, :S] = er
        carry_s[0:1, S:] = ei
        er, ei = e_s[:, :S][None], e_s[:, S:][None]
        pw = pwr_s[...]
        pwr, pwi = pw[:, :, :S], pw[:, :, S:]
        ll = dl_s[...]
        lam = jnp.concatenate([ll[:, :, :S] + (pwr * er + pwi * ei), ll[:, :, S:] + (pwr * ei - pwi * er)],
                              axis=-1).reshape(tc, 2 * S)
        l_re, l_im = lam[:, :S], lam[:, S:]
        da_ref[0:1, :S] += jnp.sum(l_re * xp[:, :S] + l_im * xp[:, S:], axis=0, keepdims=True)
        da_ref[0:1, S:] += jnp.sum(l_im * xp[:, :S] - l_re * xp[:, S:], axis=0, keepdims=True)
        lamb = lam.astype(BF16)
        dup_s[...] = lax.dot_general(lamb, b_ref[...], nn, preferred_element_type=F32)
        for s in range(SUBLANES):
            rows = pl.ds(seg * s, seg)
            du = _segment_rows(dup_s, s, seg) + d_ref[...] * dy_s[rows, :]
            du_ref[rows, :] = du.astype(du_ref.dtype)
        db_ref[...] += lax.dot_general(ubp, lamb, tn, preferred_element_type=F32)
        dd_ref[0:1, :] += jnp.sum(dy * u, axis=0, keepdims=True)

        @pl.when(pl.program_id(1) == nt - 1)
        def _():
            dbd_ref[...] = fold_diagonal(db_ref, mask_ref, fold_ref)
            dcd_ref[...] = fold_diagonal(dc_ref, mask_ref, fold_ref)

    rev = lambda b, t: (nt - 1 - t, b)
    col = jnp.arange(2 * S)
    fold = ((col // S * SSM_STATE + col % SSM_STATE)[:, None] == jnp.arange(LANES)[None, :]).astype(BF16)
    return pl.pallas_call(
        body, name="s5_bwd",
        out_shape=(jax.ShapeDtypeStruct(dproj.shape, dproj.dtype),
                   jax.ShapeDtypeStruct((SSM_BLOCKS, LANES, LANES), F32),
                   jax.ShapeDtypeStruct((SSM_BLOCKS, LANES, LANES), F32),
                   jax.ShapeDtypeStruct((SSM_BLOCKS, SUBLANES, 2 * S), F32),
                   jax.ShapeDtypeStruct((SUBLANES, MAIN_WIDTH), F32)),
        input_output_aliases={11: 0},
        grid=(SSM_BLOCKS, nt),
        in_specs=[pl.BlockSpec((tc, LANES), rev),
                  pl.BlockSpec((tc, LANES), rev),
                  pl.BlockSpec((tc, LANES), rev),
                  pl.BlockSpec((tc, LANES), rev),
                  pl.BlockSpec((tc, 2 * S), rev),
                  pl.BlockSpec((None, LANES, 2 * S), lambda b, t: (b, 0, 0)),
                  pl.BlockSpec((None, 2 * S, LANES), lambda b, t: (b, 0, 0)),
                  pl.BlockSpec((None, SUBLANES, 2 * S), lambda b, t: (b, 0, 0)),
                  pl.BlockSpec((1, LANES), lambda b, t: (0, b)),
                  pl.BlockSpec((LANES, 2 * S), lambda b, t: (0, 0)),
                  pl.BlockSpec((2 * S, LANES), lambda b, t: (0, 0)),
                  _ANY],
        out_specs=(pl.BlockSpec((tc, LANES), rev),
                   pl.BlockSpec((None, LANES, LANES), lambda b, t: (b, 0, 0)),
                   pl.BlockSpec((None, LANES, LANES), lambda b, t: (b, 0, 0)),
                   pl.BlockSpec((None, SUBLANES, 2 * S), lambda b, t: (b, 0, 0)),
                   pl.BlockSpec((SUBLANES, LANES), lambda b, t: (0, b))),
        scratch_shapes=[pltpu.VMEM((seg, SUBLANES, 2 * S), F32),
                        pltpu.VMEM((seg + 1, SUBLANES, 2 * S), F32),
                        pltpu.VMEM((seg, SUBLANES, 2 * S), F32),
                        pltpu.VMEM((SUBLANES, 2 * S), F32),
                        pltpu.VMEM((SUBLANES, 2 * S), F32),
                        pltpu.VMEM((tc, LANES), F32),
                        pltpu.VMEM((tc, LANES), F32),
                        pltpu.VMEM((tc, LANES), F32),
                        pltpu.VMEM((tc, LANES), F32),
                        pltpu.VMEM((LANES, 2 * S), F32),
                        pltpu.VMEM((LANES, 2 * S), F32)],
        compiler_params=_params("parallel", "arbitrary"),
    )(proj, dyg_a, dyg_b, y, xp, bmat, cmat, a_rows, d_skip.reshape(1, MAIN_WIDTH), _s5_diag_mask(), fold, dproj)


_Z_COLS = slice(MAIN_WIDTH, 2 * MAIN_WIDTH)
_ZM_COLS = slice(2 * MAIN_WIDTH + MEM_WIDTH, IN_WIDTH)


def _proj_rows(tr):
    return pl.BlockSpec((tr, IN_WIDTH), lambda i: (i, 0))


def _row_specs(tr):
    main = pl.BlockSpec((tr, MAIN_WIDTH), lambda i: (i, 0))
    z = pl.BlockSpec((tr, MAIN_WIDTH), lambda i: (i, 1))
    zm = pl.BlockSpec((tr, MEM_WIDTH), lambda i: (i, IN_WIDTH // MEM_WIDTH - 1))
    mem = pl.BlockSpec((tr, MEM_WIDTH), lambda i: (i, 0))
    cat = pl.BlockSpec((tr, D_MODEL), lambda i: (i, 0))
    vec = pl.BlockSpec((1, MAIN_WIDTH), lambda i: (0, 0))
    return main, z, zm, mem, cat, vec


def _gate_a_fwd(y, t, b_glu, proj, o_mem, *, tr=256):
    L = y.shape[0]
    tr = min(tr, L)

    def body(y_ref, t_ref, b_ref, z_ref, zm_ref, om_ref, o_ref):
        yg = _gelu(y_ref[...])
        sz, _ = _silu_and_grad(z_ref[...])
        o_ref[:, :MAIN_WIDTH] = (yg * _sigmoid(t_ref[...] + b_ref[...]) * sz).astype(BF16)
        szm, _ = _silu_and_grad(zm_ref[...])
        o_ref[:, MAIN_WIDTH:] = (om_ref[...] * szm).astype(BF16)

    main, z, zm, mem, cat, vec = _row_specs(tr)
    return pl.pallas_call(
        body, name="gate_a_fwd", out_shape=jax.ShapeDtypeStruct((L, D_MODEL), BF16),
        grid=(L // tr,), in_specs=[main, main, vec, z, zm, mem], out_specs=cat,
        compiler_params=_params("parallel"),
    )(y, t, b_glu.reshape(1, MAIN_WIDTH), proj, proj, o_mem)


def _gate_a_bwd(dcat, y, t, b_glu, proj, o_mem, *, tr=256):
    L = y.shape[0]
    tr = min(tr, L)

    def body(dc_ref, y_ref, t_ref, b_ref, z_ref, zm_ref, om_ref,
             dp_ref, dt_ref, dyg_ref, dom_ref, db_ref):
        dmain = dc_ref[:, :MAIN_WIDTH]
        dmemo = dc_ref[:, MAIN_WIDTH:]
        yg = _gelu(y_ref[...])
        sg = _sigmoid(t_ref[...] + b_ref[...])
        sz, gz = _silu_and_grad(z_ref[...])
        dp_ref[:, _Z_COLS] = (dmain * (yg * sg) * gz).astype(BF16)
        dy2 = dmain * sz
        dyg_ref[...] = dy2 * sg
        dt = dy2 * yg * (sg * (1.0 - sg))
        dt_ref[...] = dt.astype(BF16)

        @pl.when(pl.program_id(0) == 0)
        def _():
            db_ref[...] = jnp.zeros_like(db_ref)

        db_ref[...] += jnp.sum(dt, axis=0, keepdims=True)
        szm, gzm = _silu_and_grad(zm_ref[...])
        dom_ref[...] = dmemo * szm
        dp_ref[:, _ZM_COLS] = (dmemo * om_ref[...] * gzm).astype(BF16)

    main, z, zm, mem, cat, vec = _row_specs(tr)
    outs = pl.pallas_call(
        body, name="gate_a_bwd",
        out_shape=(jax.ShapeDtypeStruct((L, IN_WIDTH), BF16),
                   jax.ShapeDtypeStruct((L, MAIN_WIDTH), BF16), jax.ShapeDtypeStruct((L, MAIN_WIDTH), F32),
                   jax.ShapeDtypeStruct((L, MEM_WIDTH), F32), jax.ShapeDtypeStruct((1, MAIN_WIDTH), F32)),
        grid=(L // tr,), in_specs=[cat, main, main, vec, z, zm, mem],
        out_specs=(_proj_rows(tr), main, main, mem, vec),
        compiler_params=_params("arbitrary"),
    )(dcat, y, t, b_glu.reshape(1, MAIN_WIDTH), proj, proj, o_mem)
    return outs


def _gate_b_fwd(att, proj, o_mem, *, tr=256):
    L = att.shape[0]
    tr = min(tr, L)

    def body(a_ref, z_ref, zm_ref, om_ref, o_ref):
        sz, _ = _silu_and_grad(z_ref[...])
        o_ref[:, :MAIN_WIDTH] = (a_ref[...] * sz).astype(BF16)
        szm, _ = _silu_and_grad(zm_ref[...])
        o_ref[:, MAIN_WIDTH:] = (om_ref[...] * szm).astype(BF16)

    main, z, zm, mem, cat, _ = _row_specs(tr)
    return pl.pallas_call(
        body, name="gate_b_fwd", out_shape=jax.ShapeDtypeStruct((L, D_MODEL), BF16),
        grid=(L // tr,), in_specs=[main, z, zm, mem], out_specs=cat,
        compiler_params=_params("parallel"),
    )(att, proj, proj, o_mem)


def _gate_b_bwd(dcat, att, proj, o_mem, *, tr=256):
    L = att.shape[0]
    tr = min(tr, L)

    def body(dc_ref, a_ref, z_ref, zm_ref, om_ref, da_ref, dp_ref, dom_ref, dl_ref):
        dmain = dc_ref[:, :MAIN_WIDTH]
        dmemo = dc_ref[:, MAIN_WIDTH:]
        att = a_ref[...]
        sz, gz = _silu_and_grad(z_ref[...])
        datt = dmain * sz
        da_ref[...] = datt
        dp_ref[:, _Z_COLS] = (dmain * att * gz).astype(BF16)
        szm, gzm = _silu_and_grad(zm_ref[...])
        dom_ref[...] = dmemo * szm
        dp_ref[:, _ZM_COLS] = (dmemo * om_ref[...] * gzm).astype(BF16)
        prod = datt * att
        for h in range(FOX_HEADS):
            dl_ref[h] = jnp.sum(prod[:, h * HEAD_DIM:(h + 1) * HEAD_DIM], axis=1, keepdims=True)

    main, z, zm, mem, cat, _ = _row_specs(tr)
    delta = pl.BlockSpec((FOX_HEADS, tr, 1), lambda i: (0, i, 0))
    return pl.pallas_call(
        body, name="gate_b_bwd",
        out_shape=(jax.ShapeDtypeStruct((L, MAIN_WIDTH), F32), jax.ShapeDtypeStruct((L, IN_WIDTH), BF16),
                   jax.ShapeDtypeStruct((L, MEM_WIDTH), F32), jax.ShapeDtypeStruct((FOX_HEADS, L, 1), F32)),
        grid=(L // tr,), in_specs=[cat, main, z, zm, mem], out_specs=(main, _proj_rows(tr), mem, delta),
        compiler_params=_params("parallel"),
    )(dcat, att, proj, proj, o_mem)


_MEM_Q_COL = (2 * MAIN_WIDTH) // HEAD_DIM
_NT = (((1,), (1,)), ((), ()))
_TN = (((0,), (0,)), ((), ()))


def _mem_probs(q_ref, k_ref):
    qs = (q_ref[...] * (HEAD_DIM ** -0.5)).astype(BF16)
    s = lax.dot_general(qs, k_ref[...].astype(BF16), _NT, preferred_element_type=F32)
    e = jnp.exp(s - jnp.max(s, axis=-1, keepdims=True))
    return qs, e / jnp.sum(e, axis=-1, keepdims=True)


def _mem_attn_fwd(proj, kvm, *, tq=2048):
    L = proj.shape[0]
    tq = min(tq, L)

    def body(q_ref, k_ref, v_ref, o_ref):
        _, p = _mem_probs(q_ref, k_ref)
        o_ref[...] = jnp.dot(p.astype(BF16), v_ref[...].astype(BF16), preferred_element_type=F32)

    return pl.pallas_call(
        body, name="mem_attn_fwd", out_shape=jax.ShapeDtypeStruct((L, MEM_WIDTH), F32),
        grid=(MEM_HEADS, L // tq),
        in_specs=[pl.BlockSpec((tq, HEAD_DIM), lambda h, i: (i, _MEM_Q_COL + h)),
                  pl.BlockSpec((N_MEM, HEAD_DIM), lambda h, i: (0, h)),
                  pl.BlockSpec((N_MEM, HEAD_DIM), lambda h, i: (0, MEM_HEADS + h))],
        out_specs=pl.BlockSpec((tq, HEAD_DIM), lambda h, i: (i, h)),
        compiler_params=_params("parallel", "parallel"),
    )(proj, kvm, kvm)


def _mem_attn_bwd(proj, kvm, do, dproj, *, tq=2048):
    L = proj.shape[0]
    tq = min(tq, L)

    def body(q_ref, k_ref, v_ref, do_ref, dp_hbm, dq_ref, dk_ref, dv_ref):
        @pl.when(pl.program_id(1) == 0)
        def _():
            dk_ref[...] = jnp.zeros_like(dk_ref)
            dv_ref[...] = jnp.zeros_like(dv_ref)

        qs, p = _mem_probs(q_ref, k_ref)
        dob = do_ref[...].astype(BF16)
        dp = lax.dot_general(dob, v_ref[...].astype(BF16), _NT, preferred_element_type=F32)
        ds = p * (dp - jnp.sum(p * dp, axis=-1, keepdims=True))
        dsb = ds.astype(BF16)
        dq = jnp.dot(dsb, k_ref[...].astype(BF16), preferred_element_type=F32) * (HEAD_DIM ** -0.5)
        dq_ref[...] = dq.astype(BF16)
        dk_ref[...] += lax.dot_general(dsb, qs, _TN, preferred_element_type=F32)
        dv_ref[...] += lax.dot_general(p.astype(BF16), dob, _TN, preferred_element_type=F32)

    dproj, dk, dv = pl.pallas_call(
        body, name="mem_attn_bwd",
        out_shape=(jax.ShapeDtypeStruct(dproj.shape, dproj.dtype),
                   jax.ShapeDtypeStruct((N_MEM, MEM_WIDTH), F32),
                   jax.ShapeDtypeStruct((N_MEM, MEM_WIDTH), F32)),
        grid=(MEM_HEADS, L // tq),
        in_specs=[pl.BlockSpec((tq, HEAD_DIM), lambda h, i: (i, _MEM_Q_COL + h)),
                  pl.BlockSpec((N_MEM, HEAD_DIM), lambda h, i: (0, h)),
                  pl.BlockSpec((N_MEM, HEAD_DIM), lambda h, i: (0, MEM_HEADS + h)),
                  pl.BlockSpec((tq, HEAD_DIM), lambda h, i: (i, h)),
                  _ANY],
        out_specs=(pl.BlockSpec((tq, HEAD_DIM), lambda h, i: (i, _MEM_Q_COL + h)),
                   pl.BlockSpec((N_MEM, HEAD_DIM), lambda h, i: (0, h)),
                   pl.BlockSpec((N_MEM, HEAD_DIM), lambda h, i: (0, h))),
        input_output_aliases={4: 0},
        compiler_params=_params("parallel", "arbitrary"),
    )(proj, kvm, kvm, do, dproj)
    return dproj, jnp.concatenate([dk, dv], axis=1)


def _tile_cumsum(x, row, reverse):
    for sh in (1, 2, 4):
        if reverse:
            x = x + jnp.where(row < SUBLANES - sh, pltpu.roll(x, SUBLANES - sh, 0), 0.0)
        else:
            x = x + jnp.where(row >= sh, pltpu.roll(x, sh, 0), 0.0)
    return x


def _fgate_fwd(pre, b_pad):
    L = pre.shape[0]
    n8 = L // SUBLANES

    def body(p_ref, b_ref, o_ref):
        row = lax.broadcasted_iota(jnp.int32, (SUBLANES, LANES), 0)
        b = b_ref[...]

        def step(i, carry):
            x = p_ref[i] + b
            logf = jnp.minimum(x, 0.0) - jnp.log(1.0 + jnp.exp(-jnp.abs(x)))
            t = _tile_cumsum(logf, row, False) + carry
            o_ref[i] = t
            return t[SUBLANES - 1:SUBLANES, :]

        lax.fori_loop(0, n8, step, jnp.zeros((1, LANES), F32))

    out = pl.pallas_call(
        body, name="fgate_fwd", out_shape=jax.ShapeDtypeStruct((n8, SUBLANES, LANES), F32),
        compiler_params=_params(),
    )(pre.reshape(n8, SUBLANES, LANES), b_pad.reshape(1, LANES))
    return out.reshape(L, LANES)


def _fgate_bwd(dfcum, pre, b_pad):
    L = pre.shape[0]
    n8 = L // SUBLANES

    def body(d_ref, p_ref, b_ref, o_ref, s_ref):
        row = lax.broadcasted_iota(jnp.int32, (SUBLANES, LANES), 0)
        b = b_ref[...]

        def step(k, carry):
            c, acc = carry
            i = n8 - 1 - k
            t = _tile_cumsum(d_ref[i], row, True) + c
            dpre = t * _sigmoid(-(p_ref[i] + b))
            o_ref[i] = dpre
            return t[0:1, :], acc + dpre

        _, acc = lax.fori_loop(0, n8, step, (jnp.zeros((1, LANES), F32), jnp.zeros((SUBLANES, LANES), F32)))
        s_ref[...] = jnp.sum(acc, axis=0, keepdims=True)

    dpre, db = pl.pallas_call(
        body, name="fgate_bwd",
        out_shape=(jax.ShapeDtypeStruct((n8, SUBLANES, LANES), F32), jax.ShapeDtypeStruct((1, LANES), F32)),
        compiler_params=_params(),
    )(dfcum.reshape(n8, SUBLANES, LANES), pre.reshape(n8, SUBLANES, LANES), b_pad.reshape(1, LANES))
    return dpre.reshape(L, LANES), db


FOX_BLOCK = 1024


def _fox_scores(qs, k, fk, diagonal):
    s = lax.dot_general(qs, k, _NT, preferred_element_type=F32) - fk
    if diagonal:
        row = lax.broadcasted_iota(jnp.int32, s.shape, 0)
        col = lax.broadcasted_iota(jnp.int32, s.shape, 1)
        s = jnp.where(row >= col, s, NEG_BIG)
    return s


def _fox_specs(tq, L):
    nq = L // tq
    return dict(
        rows=lambda off: pl.BlockSpec((tq, HEAD_DIM), lambda h, i: (i, off + h)),
        seq=lambda off: pl.BlockSpec((L, HEAD_DIM), lambda h, i: (0, off + h)),
        col=pl.BlockSpec((None, None, tq, 1), lambda h, i: (h, i, 0, 0)),
        col_all=pl.BlockSpec((None, nq, tq, 1), lambda h, i: (h, 0, 0, 0)),
        row=pl.BlockSpec((None, None, 1, tq), lambda h, i: (h, i, 0, 0)),
        row_all=pl.BlockSpec((None, nq, 1, tq), lambda h, i: (h, 0, 0, 0)))


FOX_FWD_HEADS = 1
FOX_FWD_BLOCK = 1024


def _fox_fwd(proj, kv, fk):
    L = proj.shape[0]
    tq = min(FOX_FWD_BLOCK, L)
    nq = L // tq
    nh = FOX_FWD_HEADS
    W = nh * HEAD_DIM
    lse_shape = fk.shape[:2] + (fk.shape[3], 1)
    fk = fk.reshape(FOX_HEADS, nq, 1, tq)

    def body(q_ref, k_ref, v_ref, fk_ref, o_ref, lse_ref, m_s, l_s, acc_s):
        qi = pl.program_id(1)
        cols = [slice(a * HEAD_DIM, (a + 1) * HEAD_DIM) for a in range(nh)]
        qs = [(q_ref[:, cs] * (HEAD_DIM ** -0.5)).astype(BF16) for cs in cols]
        m_s[...] = jnp.full_like(m_s, NEG_BIG)
        l_s[...] = jnp.zeros_like(l_s)
        acc_s[...] = jnp.zeros_like(acc_s)

        def block(j, diagonal):
            r0 = pl.multiple_of(j * tq, tq)
            for a, cs in enumerate(cols):
                s = _fox_scores(qs[a], k_ref[pl.ds(r0, tq), cs], fk_ref[a, j], diagonal)
                m_new = jnp.maximum(m_s[a], jnp.max(s, axis=-1, keepdims=True))
                alpha = jnp.exp(m_s[a] - m_new)
                p = jnp.exp(s - m_new)
                l_s[a] = alpha * l_s[a] + jnp.sum(p, axis=-1, keepdims=True)
                acc_s[a] = alpha * acc_s[a] + jnp.dot(p.astype(BF16), v_ref[pl.ds(r0, tq), cs],
                                                      preferred_element_type=F32)
                m_s[a] = m_new

        def below(j, carry):
            block(j, False)
            return carry

        lax.fori_loop(0, qi, below, 0)
        block(qi, True)
        for a, cs in enumerate(cols):
            o_ref[:, cs] = acc_s[a] / l_s[a]
            lse_ref[a] = m_s[a] + jnp.log(l_s[a])

    att, lse = pl.pallas_call(
        body, name="fox_fwd",
        out_shape=(jax.ShapeDtypeStruct((L, MAIN_WIDTH), F32),
                   jax.ShapeDtypeStruct((FOX_HEADS, nq, tq, 1), F32)),
        grid=(FOX_HEADS // nh, nq),
        in_specs=[pl.BlockSpec((tq, W), lambda h, i: (i, h)),
                  pl.BlockSpec((L, W), lambda h, i: (0, h)),
                  pl.BlockSpec((L, W), lambda h, i: (0, FOX_HEADS // nh + h)),
                  pl.BlockSpec((nh, nq, 1, tq), lambda h, i: (h, 0, 0, 0))],
        out_specs=(pl.BlockSpec((tq, W), lambda h, i: (i, h)),
                   pl.BlockSpec((nh, None, tq, 1), lambda h, i: (h, i, 0, 0))),
        scratch_shapes=[pltpu.VMEM((nh, tq, 1), F32), pltpu.VMEM((nh, tq, 1), F32),
                        pltpu.VMEM((nh, tq, HEAD_DIM), F32)],
        compiler_params=_params("parallel", "parallel"),
    )(proj, kv, kv, fk)
    return att, lse.reshape(lse_shape)


def _fox_bwd_dq(proj, kv, fk, lse, delta, datt, dproj):
    L = proj.shape[0]
    tq = min(FOX_BLOCK, L)
    nq = L // tq
    sp = _fox_specs(tq, L)

    def body(q_ref, k_ref, v_ref, fk_ref, lse_ref, dl_ref, do_ref, dp_hbm, dq_ref, df_ref, acc_s, df_s):
        qi = pl.program_id(1)
        qs = (q_ref[...] * (HEAD_DIM ** -0.5)).astype(BF16)
        dob = do_ref[...].astype(BF16)
        lse, dl = lse_ref[...], dl_ref[...]
        acc_s[...] = jnp.zeros_like(acc_s)
        df_s[...] = jnp.zeros_like(df_s)

        def block(j, diagonal):
            r0 = pl.multiple_of(j * tq, tq)
            k = k_ref[pl.ds(r0, tq), :]
            p = jnp.exp(_fox_scores(qs, k, fk_ref[j], diagonal) - lse)
            dp = lax.dot_general(dob, v_ref[pl.ds(r0, tq), :], _NT, preferred_element_type=F32)
            ds = p * (dp - dl)
            acc_s[...] += jnp.dot(ds.astype(BF16), k, preferred_element_type=F32)
            df_s[...] += jnp.sum(ds, axis=1, keepdims=True)

        def below(j, carry):
            block(j, False)
            return carry

        lax.fori_loop(0, qi, below, 0)
        block(qi, True)
        dq_ref[...] = (acc_s[...] * (HEAD_DIM ** -0.5)).astype(BF16)
        df_ref[...] = df_s[...]

    return pl.pallas_call(
        body, name="fox_bwd_dq",
        out_shape=(jax.ShapeDtypeStruct(dproj.shape, dproj.dtype),
                   jax.ShapeDtypeStruct((FOX_HEADS, nq, tq, 1), F32)),
        grid=(FOX_HEADS, nq),
        in_specs=[sp["rows"](0), sp["seq"](0), sp["seq"](FOX_HEADS), sp["row_all"],
                  sp["col"], sp["col"], sp["rows"](0), _ANY],
        out_specs=(sp["rows"](0), sp["col"]),
        input_output_aliases={7: 0},
        scratch_shapes=[pltpu.VMEM((tq, HEAD_DIM), F32), pltpu.VMEM((tq, 1), F32)],
        compiler_params=_params("parallel", "parallel"),
    )(proj, kv, kv, fk, lse, delta, datt, dproj)


def _fox_bwd_dkv(proj, kv, fk, lse, delta, datt):
    L = proj.shape[0]
    tq = min(FOX_BLOCK, L)
    nq = L // tq
    sp = _fox_specs(tq, L)

    def body(q_ref, k_ref, v_ref, fk_ref, lse_ref, dl_ref, do_ref,
             dk_ref, dv_ref, df_ref, dk_s, dv_s, df_s):
        ki = pl.program_id(1)
        k, v, fk = k_ref[...], v_ref[...], fk_ref[...]
        dk_s[...] = jnp.zeros_like(dk_s)
        dv_s[...] = jnp.zeros_like(dv_s)
        df_s[...] = jnp.zeros_like(df_s)

        def block(i, diagonal):
            r0 = pl.multiple_of(i * tq, tq)
            qs = (q_ref[pl.ds(r0, tq), :] * (HEAD_DIM ** -0.5)).astype(BF16)
            dob = do_ref[pl.ds(r0, tq), :].astype(BF16)
            p = jnp.exp(_fox_scores(qs, k, fk, diagonal) - lse_ref[i])
            dp = lax.dot_general(dob, v, _NT, preferred_element_type=F32)
            ds = p * (dp - dl_ref[i])
            dv_s[...] += lax.dot_general(p.astype(BF16), dob, _TN, preferred_element_type=F32)
            dk_s[...] += lax.dot_general(ds.astype(BF16), qs, _TN, preferred_element_type=F32)
            df_s[...] -= jnp.sum(ds, axis=0, keepdims=True)

        def above(i, carry):
            block(i, False)
            return carry

        block(ki, True)
        lax.fori_loop(ki + 1, nq, above, 0)
        dk_ref[...] = dk_s[...].astype(BF16)
        dv_ref[...] = dv_s[...].astype(BF16)
        df_ref[...] = df_s[...]

    return pl.pallas_call(
        body, name="fox_bwd_dkv",
        out_shape=(jax.ShapeDtypeStruct((L, MAIN_WIDTH), BF16),
                   jax.ShapeDtypeStruct((L, MAIN_WIDTH), BF16),
                   jax.ShapeDtypeStruct((FOX_HEADS, nq, 1, tq), F32)),
        grid=(FOX_HEADS, nq),
        in_specs=[sp["seq"](0), sp["rows"](0), sp["rows"](FOX_HEADS), sp["row"],
                  sp["col_all"], sp["col_all"], sp["seq"](0)],
        out_specs=(sp["rows"](0), sp["rows"](0), sp["row"]),
        scratch_shapes=[pltpu.VMEM((tq, HEAD_DIM), F32), pltpu.VMEM((tq, HEAD_DIM), F32),
                        pltpu.VMEM((1, tq), F32)],
        compiler_params=_params("parallel", "parallel"),
    )(proj, kv, kv, fk, lse, delta, datt)


def _pad_lanes(a):
    return jnp.pad(a, ((0, 0), (0, LANES - a.shape[1])))


def _mem_branch_fwd(memn, w_mk, proj, tag):
    kvm = _mm(memn, w_mk, name="mem_kv_" + tag)
    return kvm, _mem_attn_fwd(proj, kvm)


def _mem_branch_bwd(mem, g, w_mk, proj, memn, kvm, do_mem, dproj, tag):
    dproj, dkvm = _mem_attn_bwd(proj, kvm, do_mem, dproj)
    dkvm = dkvm.astype(BF16)
    dw_mk = _mm(memn, dkvm, ta=True, name="dw_mem_kv_" + tag, out_dtype=BF16)
    dmemn = _mm(dkvm, w_mk, tb=True, name="dmemn_" + tag)
    _, dg = _rmsnorm_bwd(mem, g, dmemn, name="mem_norm_bwd_" + tag, dx_dtype=BF16)
    return dproj, dw_mk, dg


def _local_step(x, mem, target, w, fetch=None, grads_ready=None):
    if grads_ready is None:
        grads_ready = lambda group, grads, token: token
    L = x.shape[0]
    g = {}
    w = dict(w)

    b_re_t = jnp.transpose(w["b_re"], (0, 2, 1))
    b_im_t = jnp.transpose(w["b_im"], (0, 2, 1))
    ar, ai, bbr_t, bbi_t = _s5_prep(w["lam_re"], w["lam_im"], w["log_step"], b_re_t, b_im_t)
    bmat, cmat = _s5_block_mats(bbr_t, bbi_t, w["c_re"], w["c_im"])
    a_rows = _s5_a_rows(ar, ai)

    hn0 = _rmsnorm_fwd(x, w["pre_norm_g"][0], name="pre_norm_0", out_dtype=BF16)
    memn0 = _rmsnorm_fwd(mem, w["mem_norm_g"][0], name="mem_norm_0", out_dtype=BF16)
    memn1 = _rmsnorm_fwd(mem, w["mem_norm_g"][1], name="mem_norm_1", out_dtype=BF16)
    if fetch is not None:
        w.update(fetch("a", [hn0, memn0, memn1, bmat, cmat, a_rows]))
    proj_a = _mm(hn0, w["w_in_a"], name="in_proj_a")
    y, yg, xp = _s5_fwd(proj_a, bmat, cmat, a_rows, w["d_skip"])
    if fetch is not None:
        w.update(fetch("b", yg))
    t = _mm(yg, w["w_glu"], name="glu_proj")
    kvm0, om0 = _mem_branch_fwd(memn0, w["w_mem_kv"][0], proj_a, "0")
    cat0 = _gate_a_fwd(y, t, w["b_glu"], proj_a, om0)
    o0 = _mm(cat0, w["w_out"][0], name="out_proj_0")
    h1 = _rmsnorm_fwd(o0, w["post_norm_g"][0], res=x, name="post_norm_0")

    kv_in = _rmsnorm_fwd(h1, w["kv_norm_g"], name="kv_norm", out_dtype=BF16)
    if fetch is not None:
        w.update(fetch("c", kv_in))
    kv = _mm(kv_in, w["w_kv"], name="kv_proj", out_dtype=BF16)
    pre_f = _mm(kv_in, w["w_fgate"], name="fgate_proj")
    b_f = jnp.pad(w["b_fgate"], (0, LANES - FOX_HEADS))
    fcum = _fgate_fwd(pre_f, b_f)
    fc = jnp.transpose(fcum[:, :FOX_HEADS])
    tq = min(FOX_BLOCK, L)
    fk = fc.reshape(FOX_HEADS, L // tq, 1, tq)

    hn1 = _rmsnorm_fwd(h1, w["pre_norm_g"][1], name="pre_norm_1", out_dtype=BF16)
    proj_b = _mm(hn1, w["w_in_b"], name="in_proj_b")
    att, lse = _fox_fwd(proj_b, kv, fk)
    kvm1, om1 = _mem_branch_fwd(memn1, w["w_mem_kv"][1], proj_b, "1")
    cat1 = _gate_b_fwd(att, proj_b, om1)
    o1 = _mm(cat1, w["w_out"][1], name="out_proj_1")
    dh2, loss_row = _final_norm_loss(o1, w["post_norm_g"][1], h1, target)

    do1, dpost1 = _rmsnorm_bwd(o1, w["post_norm_g"][1], dh2, name="post_norm_bwd_1", dx_dtype=BF16)
    dcat1 = _mm(do1, w["w_out"][1], tb=True, name="dcat_1", out_dtype=BF16)
    g["w_out_1"] = _mm(cat1, do1, ta=True, name="dw_out_1", out_dtype=BF16)
    datt, dproj_b, dom1, delta = _gate_b_bwd(dcat1, att, proj_b, om1)
    dproj_b, g["w_mem_kv_1"], dmemg1 = _mem_branch_bwd(mem, w["mem_norm_g"][1], w["w_mem_kv"][1], proj_b,
                                                      memn1, kvm1, dom1, dproj_b, "1")
    delta = delta.reshape(lse.shape)
    dproj_b, dfq = _fox_bwd_dq(proj_b, kv, fk, lse, delta, datt, dproj_b)
    dk, dv, dfk = _fox_bwd_dkv(proj_b, kv, fk, lse, delta, datt)
    g["w_in_b"] = _mm(hn1, dproj_b, ta=True, name="dw_in_b", out_dtype=BF16, shards=N_CHIPS)
    dhn1 = _mm(dproj_b, w["w_in_b"], tb=True, name="dhn_1")

    dkv = jnp.concatenate([dk, dv], axis=1)
    g["w_kv"] = _mm(kv_in, dkv, ta=True, name="dw_kv", out_dtype=BF16, shards=N_CHIPS)
    dkv_in_a = _mm(dkv, w["w_kv"], tb=True, name="dkv_in_kv")
    dfcum = _pad_lanes(jnp.transpose(dfq.reshape(FOX_HEADS, L) + dfk.reshape(FOX_HEADS, L)))
    dpre_f, db_f = _fgate_bwd(dfcum, pre_f, b_f)
    g["b_fgate"] = db_f[0, :FOX_HEADS]
    g["w_fgate"] = _mm(kv_in, dpre_f, ta=True, name="dw_fgate")[:, :FOX_HEADS]
    dkv_in_b = _mm(dpre_f, w["w_fgate"], tb=True, name="dkv_in_fgate")
    dh1, g["kv_norm_g"], dpre1 = _rmsnorm_bwd_pair(h1, w["kv_norm_g"], (dkv_in_a, dkv_in_b), w["pre_norm_g"][1],
                                                   dhn1, adds=(dh2,), name="kv_pre_norm_bwd")
    dh1 = grads_ready("b", g, dh1)

    do0, dpost0 = _rmsnorm_bwd(o0, w["post_norm_g"][0], dh1, name="post_norm_bwd_0", dx_dtype=BF16)
    dcat0 = _mm(do0, w["w_out"][0], tb=True, name="dcat_0", out_dtype=BF16)
    g["w_out_0"] = _mm(cat0, do0, ta=True, name="dw_out_0", out_dtype=BF16)
    dcat0 = grads_ready("b_send", g, dcat0)
    dproj_a, dt, dyg_a, dom0, db_glu = _gate_a_bwd(dcat0, y, t, w["b_glu"], proj_a, om0)
    g["b_glu"] = db_glu[0]
    g["w_glu"] = _mm(yg, dt, ta=True, name="dw_glu", out_dtype=BF16)
    dyg_b = _mm(dt, w["w_glu"], tb=True, name="dyg")
    dproj_a, g["w_mem_kv_0"], dmemg0 = _mem_branch_bwd(mem, w["mem_norm_g"][0], w["w_mem_kv"][0], proj_a,
                                                      memn0, kvm0, dom0, dproj_a, "0")
    dyg_b = grads_ready("a1", g, dyg_b)
    dproj_a, db_blk, dc_blk, da_rows, dd_skip = _s5_bwd(proj_a, dyg_a, dyg_b, y, xp, bmat, cmat, a_rows,
                                                        w["d_skip"], dproj_a)
    dproj_a = grads_ready("a1_send", g, dproj_a)
    g["d_skip"] = dd_skip[0]
    g["w_in_a"] = _mm(hn0, dproj_a, ta=True, name="dw_in_a", out_dtype=BF16, shards=N_CHIPS)
    dproj_a = grads_ready("a2", g, dproj_a)
    dhn0 = _mm(dproj_a, w["w_in_a"], tb=True, name="dhn_0")
    grad_x, dpre0 = _rmsnorm_bwd(x, w["pre_norm_g"][0], dhn0, adds=(dh1,), name="pre_norm_bwd_0")

    dbb = _s5_unfold(db_blk)
    dcc = _s5_unfold(dc_blk)
    g["c_re"], g["c_im"] = dcc[0], -dcc[1]
    d_ar = da_rows[:, 0, :STATE_COLS].reshape(SSM_GROUPS, SSM_STATE)
    d_ai = da_rows[:, 0, STATE_COLS:].reshape(SSM_GROUPS, SSM_STATE)
    dlr, dli, dls, dbr_t, dbi_t = _s5_prep_bwd(w["lam_re"], w["lam_im"], w["log_step"], b_re_t, b_im_t,
                                               d_ar, d_ai, dbb[0], dbb[1])
    g["lam_re"], g["lam_im"], g["log_step"] = dlr, dli, dls[:, 0]
    g["b_re"] = jnp.transpose(dbr_t, (0, 2, 1))
    g["b_im"] = jnp.transpose(dbi_t, (0, 2, 1))
    g["pre_norm_g"] = jnp.stack([dpre0, dpre1])
    g["post_norm_g"] = jnp.stack([dpost0, dpost1])
    g["mem_norm_g"] = jnp.stack([dmemg0, dmemg1])
    return loss_row, grad_x, g


_MESH = pl.DeviceIdType.MESH
_ANY = pl.BlockSpec(memory_space=pl.ANY)


def _place():
    x, y, c = lax.axis_index("x"), lax.axis_index("y"), lax.axis_index("c")
    chips = [(1 - x, y), (x, 1 - y), (1 - x, 1 - y)]
    return x, y, c, chips


_HBM = pl.BlockSpec(memory_space=pltpu.HBM)
_SEM = pl.BlockSpec(memory_space=pltpu.SEMAPHORE)
_SIDE = pltpu.SideEffectType.DATAFLOW_SIDE_EFFECTING


def _in_hbm(a):
    return pltpu.with_memory_space_constraint(a, pltpu.HBM)


def _hbm_like(a):
    return pltpu.HBM(a.shape, a.dtype)


def _ici_copies(srcs, lands, send_sem, recv_sem, src_at, dst_at, wait_at, to_sibling=False):
    x, y, c, chips = _place()
    peers = [(x, y, 1 - c)] if to_sibling else [(cx, cy, c) for cx, cy in chips]
    m = len(peers)
    start, wait = [], []
    for i in range(len(srcs)):
        for k, (px, py, pc) in enumerate(peers):
            sem = dict(send_sem=send_sem.at[m * i + k], recv_sem=recv_sem.at[m * i + k],
                       device_id=(px, py, pc), device_id_type=_MESH)
            src = src_at(srcs[i], 2 * px + py, c)
            start.append(pltpu.make_async_remote_copy(src_ref=src, dst_ref=dst_at(lands[i], 2 * x + y, k, c), **sem))
            wait.append(pltpu.make_async_remote_copy(src_ref=src, dst_ref=wait_at(lands[i], 2 * px + py, k, c), **sem))
    return start, wait


def _route_peers(route):
    return 1 if len(route) == 4 else 3


_BLOCK_ROUTE = (lambda s, j, c: s, lambda l, me, k, c: l.at[me, c], lambda l, j, k, c: l.at[j, c])


def _ici_start(srcs, lands, token, route, *, name):
    n = len(srcs)

    def body(*refs):
        start, _ = _ici_copies(refs[:n], refs[n:2 * n], refs[2 * n + 1], refs[2 * n + 2], *route)
        for cp in start:
            cp.start()

    sems = pltpu.SemaphoreType.DMA((_route_peers(route) * n,))
    outs = pl.pallas_call(
        body, name=name,
        out_shape=(sems, sems, *[_hbm_like(a) for a in srcs], *[_hbm_like(a) for a in lands], _hbm_like(token)),
        in_specs=[_HBM] * (2 * n + 1), out_specs=(_SEM, _SEM, *[_HBM] * (2 * n + 1)),
        input_output_aliases={i: 2 + i for i in range(2 * n + 1)},
        compiler_params=pltpu.CompilerParams(has_side_effects=_SIDE),
    )(*[_in_hbm(a) for a in srcs], *[_in_hbm(a) for a in lands], _in_hbm(token))
    return (outs[0], outs[1], list(outs[2:2 + n]), list(outs[2 + n:2 + 2 * n])), outs[2 + 2 * n]


def _ici_wait(handle, after, route, *, name):
    send_sem, recv_sem, srcs, lands = handle
    n = len(srcs)
    after = list(after) if isinstance(after, (list, tuple)) else [after]

    def body(*refs):
        _, wait = _ici_copies(refs[:n], refs[n:2 * n], refs[2 * n], refs[2 * n + 1], *route)
        for cp in wait:
            cp.wait_send()
            cp.wait_recv()

    outs = pl.pallas_call(
        body, name=name,
        out_shape=(*[_hbm_like(a) for a in srcs], *[_hbm_like(a) for a in lands]),
        in_specs=[_HBM] * (2 * n) + [_SEM, _SEM] + [_ANY] * len(after), out_specs=tuple([_HBM] * (2 * n)),
        input_output_aliases={i: i for i in range(2 * n)},
        compiler_params=pltpu.CompilerParams(has_side_effects=_SIDE),
    )(*srcs, *lands, send_sem, recv_sem, *after)
    return list(outs[:n]), list(outs[n:])


_GATHER_ROUTE = (lambda s, j, c: s.at[c], lambda l, me, k, c: l.at[me, c], lambda l, j, k, c: l.at[j, c])
_SCATTER_ROUTE = (lambda s, j, c: s.at[j], lambda l, me, k, c: l.at[k], lambda l, j, k, c: l.at[k])
_SHARE_ROUTE = (lambda s, j, c: s, lambda l, me, k, c: l.at[c], lambda l, j, k, c: l.at[1 - c], True)
_SWAP_ROUTE = (lambda s, j, c: s.at[:, 1 - c], lambda l, me, k, c: l, lambda l, j, k, c: l, True)


def _gather_forward(lands, tag, own=False):
    n = len(lands)
    m = 4 if own else 3

    def body(*refs):
        ins, outs = refs[:n], refs[n:2 * n]
        send_sem, recv_sem = refs[2 * n:]
        x, y, c, chips = _place()
        slots = [2 * cx + cy for cx, cy in chips] + [2 * x + y]

        def copy(i, k, half):
            return pltpu.make_async_remote_copy(
                src_ref=ins[i].at[slots[k], half], dst_ref=outs[i].at[slots[k], half],
                send_sem=send_sem.at[m * i + k], recv_sem=recv_sem.at[m * i + k],
                device_id=(x, y, 1 - c), device_id_type=_MESH)

        copies = [copy(i, k, c) for i in range(n) for k in range(m)]
        for cp in copies:
            cp.start()
        for i in range(n):
            for k in range(m):
                copy(i, k, 1 - c).wait_recv()
        for cp in copies:
            cp.wait_send()

    return pl.pallas_call(
        body, name="gather_forward_to_sibling_" + tag,
        out_shape=[jax.ShapeDtypeStruct(a.shape, a.dtype) for a in lands],
        in_specs=[_ANY] * n, out_specs=[_ANY] * n,
        input_output_aliases={i: i for i in range(n)},
        scratch_shapes=[pltpu.SemaphoreType.DMA((m * n,)), pltpu.SemaphoreType.DMA((m * n,))],
    )(*lands)


def _swap_halves(grads, tag):
    n = len(grads)

    def body(*refs):
        ins, outs = refs[:n], refs[n:2 * n]
        send_sem, recv_sem = refs[2 * n:]
        x, y, c, _ = _place()
        copies = [pltpu.make_async_remote_copy(
            src_ref=ins[i].at[:, 1 - c], dst_ref=outs[i],
            send_sem=send_sem.at[i], recv_sem=recv_sem.at[i],
            device_id=(x, y, 1 - c), device_id_type=_MESH) for i in range(n)]
        for cp in copies:
            cp.start()
        for cp in copies:
            cp.wait()

    return pl.pallas_call(
        body, name="grad_swap_halves_" + tag,
        out_shape=[jax.ShapeDtypeStruct((N_CHIPS,) + g.shape[2:], g.dtype) for g in grads],
        in_specs=[_ANY] * n, out_specs=[_ANY] * n,
        scratch_shapes=[pltpu.SemaphoreType.DMA((n,)), pltpu.SemaphoreType.DMA((n,))],
    )(*grads)


def _sum_rows(h, C):
    return max(d for d in range(SUBLANES, h + 1, SUBLANES) if h % d == 0 and d * C <= 1 << 20)


SUM_STEPS = 4


def _pair_sums(gs, rs, c_idx, *, name):
    n = len(gs)
    rows = [g.shape[2] // SUM_STEPS for g in gs]

    def body(c_ref, *refs):
        for g_ref, r_ref, o_ref in zip(refs[:n], refs[n:2 * n], refs[2 * n:]):
            o_ref[...] = (g_ref[...].astype(F32) + r_ref[...].astype(F32)).astype(o_ref.dtype)

    return pl.pallas_call(
        body, name=name,
        out_shape=[jax.ShapeDtypeStruct((N_CHIPS,) + g.shape[2:], g.dtype) for g in gs],
        grid_spec=pltpu.PrefetchScalarGridSpec(
            num_scalar_prefetch=1, grid=(N_CHIPS, SUM_STEPS),
            in_specs=[pl.BlockSpec((None, None, tr, g.shape[3]), lambda j, i, s: (j, s[0], i, 0))
                      for g, tr in zip(gs, rows)]
            + [pl.BlockSpec((None, tr, g.shape[3]), lambda j, i, s: (j, i, 0)) for g, tr in zip(gs, rows)],
            out_specs=[pl.BlockSpec((None, tr, g.shape[3]), lambda j, i, s: (j, i, 0)) for g, tr in zip(gs, rows)]),
        compiler_params=_params("parallel", "parallel"),
    )(c_idx, *gs, *rs)


def _owner_sums(ss, rs, jc_idx, *, name):
    n = len(ss)
    rows = [s.shape[1] // SUM_STEPS for s in ss]

    def body(jc_ref, *refs):
        for s_ref, r_ref, m_ref, o_ref in zip(refs[:n], refs[n:2 * n], refs[2 * n:3 * n], refs[3 * n:]):
            acc = s_ref[...].astype(F32)
            for k in range(3):
                acc = acc + r_ref[k].astype(F32)
            m_ref[...] = acc
            o_ref[...] = acc

    outs = pl.pallas_call(
        body, name=name,
        out_shape=[jax.ShapeDtypeStruct(s.shape[1:], F32) for s in ss]
        + [jax.ShapeDtypeStruct((2,) + s.shape[1:], F32) for s in ss],
        grid_spec=pltpu.PrefetchScalarGridSpec(
            num_scalar_prefetch=1, grid=(SUM_STEPS,),
            in_specs=[pl.BlockSpec((None, tr, s.shape[2]), lambda i, p: (p[0], i, 0)) for s, tr in zip(ss, rows)]
            + [pl.BlockSpec((3, tr, s.shape[2]), lambda i, p: (0, i, 0)) for s, tr in zip(ss, rows)],
            out_specs=[pl.BlockSpec((tr, s.shape[2]), lambda i, p: (i, 0)) for s, tr in zip(ss, rows)]
            + [pl.BlockSpec((None, tr, s.shape[2]), lambda i, p: (p[1], i, 0)) for s, tr in zip(ss, rows)]),
        compiler_params=_params("parallel"),
    )(jc_idx, *ss, *rs)
    return outs[:n], outs[n:]


def _chip_sums(grads, c_idx, tag):
    views = [g.reshape(N_CHIPS, 2, g.shape[1] // 2, g.shape[2]) for g in grads]
    arrived = _swap_halves(views, tag)
    return _pair_sums(views, arrived, c_idx, name=f"grad_pair_sums_{tag}")


def _sum_devices(blocks):
    R = blocks.shape[2]
    tr = _sum_rows(R, 2 * N_CHIPS * LANES)

    def body(b_ref, o_ref):
        acc = b_ref[0, 0]
        for d in range(1, 2 * N_CHIPS):
            acc = acc + b_ref[d // 2, d % 2]
        o_ref[...] = acc

    return pl.pallas_call(
        body, name="sum_small_over_devices", out_shape=jax.ShapeDtypeStruct((R, LANES), F32),
        grid=(R // tr,),
        in_specs=[pl.BlockSpec((N_CHIPS, 2, tr, LANES), lambda i: (0, 0, i, 0))],
        out_specs=pl.BlockSpec((tr, LANES), lambda i: (i, 0)),
        compiler_params=_params("parallel"),
    )(blocks)


def _adamw(w, g, m, v, *, name):
    R, C = w.shape
    whole_fits = 7 * 2 * R * C * 4 <= VMEM_LIMIT_BYTES // 2
    tr = R if whole_fits else next(c for c in (256, 192, 128, 64, 32, 16, 8) if R % c == 0)

    def body(w_ref, g_ref, m_ref, v_ref, d_ref, nm_ref, nv_ref):
        g = g_ref[...]
        m = ADAM_B1 * m_ref[...] + (1.0 - ADAM_B1) * g
        v = ADAM_B2 * v_ref[...] + (1.0 - ADAM_B2) * (g * g)
        nm_ref[...] = m
        nv_ref[...] = v
        m_hat = m / (1.0 - ADAM_B1 ** ADAM_STEP)
        v_hat = v / (1.0 - ADAM_B2 ** ADAM_STEP)
        d_ref[...] = -ADAM_LR * (m_hat / (jnp.sqrt(v_hat) + ADAM_EPS) + ADAM_WD * w_ref[...])

    blk = pl.BlockSpec((tr, C), lambda i: (i, 0))
    sds = jax.ShapeDtypeStruct((R, C), F32)
    return pl.pallas_call(
        body, name=name, out_shape=(sds, sds, sds), grid=(R // tr,),
        in_specs=[blk] * 4, out_specs=(blk, blk, blk),
        compiler_params=_params("parallel"),
    )(w, g, m, v)


_TILE = SUBLANES * LANES


def _pack(arrays):
    rows = []
    for a in arrays:
        flat = a.reshape(-1)
        flat = jnp.pad(flat, (0, (-flat.shape[0]) % _TILE))
        rows.append(flat.reshape(-1, LANES))
    return jnp.concatenate(rows, axis=0)


def _unpack(buf, shapes):
    out, r = [], 0
    for s in shapes:
        size = math.prod(s)
        nr = -(-size // _TILE) * SUBLANES
        out.append(buf[r:r + nr].reshape(-1)[:size].reshape(s))
        r += nr
    return out


_BIG = ("w_in_a", "w_glu", "w_kv", "w_in_b", "w_mem_kv", "w_out")
_REPLICATED = ("pre_norm_g", "post_norm_g", "lam_re", "lam_im", "log_step", "b_re", "b_im", "c_re", "c_im",
               "kv_norm_g", "b_fgate", "mem_norm_g")
_SHARDED_SMALL = ("d_skip", "b_glu", "w_fgate")
_WEIGHTS = ("pre_norm_g", "post_norm_g", "w_in_a", "lam_re", "lam_im", "log_step", "b_re", "b_im", "c_re",
            "c_im", "d_skip", "w_glu", "b_glu", "kv_norm_g", "w_kv", "w_fgate", "b_fgate", "w_in_b",
            "mem_norm_g", "w_mem_kv", "w_out")


def _halves(a):
    return a.reshape(2, a.shape[0] // 2, a.shape[1])


def _unhalve(a):
    return a.reshape(N_CHIPS, 2 * a.shape[2], a.shape[3])


def _columns(a):
    return jnp.transpose(a, (1, 0, 2)).reshape(a.shape[1], N_CHIPS * a.shape[2])


def kernel(x, mem, pre_norm_g, post_norm_g, w_in_a, lam_re, lam_im, log_step, b_re, b_im, c_re, c_im, d_skip, w_glu, b_glu, kv_norm_g, w_kv, w_fgate, b_fgate, w_in_b, mem_norm_g, w_mem_kv, w_out, loss_target, m_pre_norm_g, m_post_norm_g, m_w_in_a, m_lam_re, m_lam_im, m_log_step, m_b_re, m_b_im, m_c_re, m_c_im, m_d_skip, m_w_glu, m_b_glu, m_kv_norm_g, m_w_kv, m_w_fgate, m_b_fgate, m_w_in_b, m_mem_norm_g, m_w_mem_kv, m_w_out, v_pre_norm_g, v_post_norm_g, v_w_in_a, v_lam_re, v_lam_im, v_log_step, v_b_re, v_b_im, v_c_re, v_c_im, v_d_skip, v_w_glu, v_b_glu, v_kv_norm_g, v_w_kv, v_w_fgate, v_b_fgate, v_w_in_b, v_mem_norm_g, v_w_mem_kv, v_w_out):
    a = dict(locals())
    xi, yi, ci = lax.axis_index("x"), lax.axis_index("y"), lax.axis_index("c")
    chip = 2 * xi + yi
    c_idx = jnp.reshape(ci, (1,)).astype(jnp.int32)
    jc_idx = jnp.stack([chip, ci]).astype(jnp.int32)

    vec = jnp.zeros((2 * SUBLANES, MAIN_WIDTH // N_CHIPS), F32)
    vec = vec.at[0].set(a["d_skip"][0]).at[1].set(a["b_glu"][0])
    def own_slot(gathered, parts):
        return [lax.dynamic_update_index_in_dim(g, p, chip, 0) for g, p in zip(gathered, parts)]

    parts_a = [_halves(a["w_in_a"][0].astype(BF16)), _halves(vec)]
    parts_b = [_halves(a["w_glu"][0].astype(BF16)),
               *[_halves(a["w_mem_kv"][i].astype(BF16)) for i in range(2)],
               *[_halves(a["w_out"][i].astype(BF16)) for i in range(2)]]
    parts_c = [_halves(a["w_kv"].astype(BF16)), _halves(_pad_lanes(a["w_fgate"]).astype(BF16)),
               _halves(a["w_in_b"][0].astype(BF16))]
    travelling, token = {}, a["pre_norm_g"]
    for tag, parts in (("a", parts_a), ("b", parts_b), ("c", parts_c)):
        lands = [lax.empty((N_CHIPS,) + p.shape, p.dtype) for p in parts]
        travelling[tag], token = _ici_start(parts, lands, token, _GATHER_ROUTE, name=f"gather_{tag}_start")

    def fetch(tag, after):
        parts, lands = _ici_wait(travelling[tag], after, _GATHER_ROUTE, name=f"gather_{tag}_wait")
        full = own_slot(_gather_forward(lands, tag), parts)
        if tag == "a":
            w_in_a, vecs = full
            return dict(w_in_a=_columns(_unhalve(w_in_a)), d_skip=vecs[:, 0, 0, :].reshape(MAIN_WIDTH),
                        b_glu=vecs[:, 0, 1, :].reshape(MAIN_WIDTH))
        if tag == "b":
            w_glu, w_mk0, w_mk1, w_out0, w_out1 = full
            return dict(w_glu=w_glu.reshape(MAIN_WIDTH, MAIN_WIDTH),
                        w_mem_kv=[m.reshape(D_MODEL, 2 * MEM_WIDTH) for m in (w_mk0, w_mk1)],
                        w_out=[o.reshape(D_MODEL, D_MODEL) for o in (w_out0, w_out1)])
        w_kv, w_fg, w_in_b = full
        return dict(w_kv=_columns(_unhalve(w_kv)), w_fgate=w_fg.reshape(D_MODEL, LANES),
                    w_in_b=_columns(_unhalve(w_in_b)))

    w = dict(
        pre_norm_g=token, post_norm_g=a["post_norm_g"], mem_norm_g=a["mem_norm_g"],
        kv_norm_g=a["kv_norm_g"], b_fgate=a["b_fgate"],
        lam_re=a["lam_re"][0], lam_im=a["lam_im"][0], log_step=a["log_step"][0],
        b_re=a["b_re"][0], b_im=a["b_im"][0], c_re=a["c_re"][0], c_im=a["c_im"][0])

    sent = {}

    swapping = {}

    def grads_ready(event, g, token):
        tag = event.split("_")[0]
        if event in ("b", "a1"):
            big = {"b": lambda: [g["w_kv"], g["w_in_b"], g["w_mem_kv_1"].reshape(N_CHIPS, -1, 2 * MEM_WIDTH),
                                 g["w_out_1"].reshape(N_CHIPS, -1, D_MODEL)],
                   "a1": lambda: [g["w_glu"].reshape(N_CHIPS, -1, MAIN_WIDTH),
                                  g["w_mem_kv_0"].reshape(N_CHIPS, -1, 2 * MEM_WIDTH),
                                  g["w_out_0"].reshape(N_CHIPS, -1, D_MODEL)]}[tag]()
            views = [b.reshape(N_CHIPS, 2, b.shape[1] // 2, b.shape[2]) for b in big]
            lands = [lax.empty((N_CHIPS,) + v.shape[2:], v.dtype) for v in views]
            swapping[tag], token = _ici_start(views, lands, token, _SWAP_ROUTE, name=f"grad_swap_{tag}_start")
            return token
        if event == "a2":
            sums = _chip_sums([g["w_in_a"]], c_idx, tag)
        else:
            views, arrived = _ici_wait(swapping[tag], token, _SWAP_ROUTE, name=f"grad_swap_{tag}_wait")
            sums = _pair_sums(views, arrived, c_idx, name=f"grad_pair_sums_{tag}")
        lands = [lax.empty((3,) + s.shape[1:], s.dtype) for s in sums]
        sent[tag], token = _ici_start(sums, lands, token, _SCATTER_ROUTE, name=f"grad_send_{tag}_start")
        return token

    loss_row, grad_x, g = _local_step(a["x"][0], a["mem"][0], a["loss_target"][0], w, fetch, grads_ready)

    small_names = _REPLICATED + _SHARDED_SMALL
    pack = _pack([g[n] for n in small_names])
    blocks = lax.empty((N_CHIPS, 2) + pack.shape, F32)
    small_sent, token = _ici_start([pack], [blocks], loss_row, _BLOCK_ROUTE, name="small_sums_start")

    sharing = {}
    for tag in ("b", "a1", "a2"):
        sums, arrived = _ici_wait(sent[tag], [grad_x, token], _SCATTER_ROUTE, name=f"grad_send_{tag}_wait")
        mine, bufs = _owner_sums(sums, arrived, jc_idx, name=f"grad_owner_sums_{tag}")
        sharing[tag], token = _ici_start(mine, bufs, token, _SHARE_ROUTE, name=f"grad_share_{tag}_start")
    loss = lax.psum(jnp.sum(token), MESH_AXES)

    def shared(tag, after):
        _, bufs = _ici_wait(sharing[tag], after, _SHARE_ROUTE, name=f"grad_share_{tag}_wait")
        return [b.reshape(-1, b.shape[2]) for b in bufs]

    grads, delta, new_m, new_v = {}, {}, {}, {}

    def adam(n):
        shape = a[n].shape
        d2 = (-1, shape[-1])
        d, m, v = _adamw(a[n].reshape(d2), grads[n].reshape(d2), a["m_" + n].reshape(d2),
                         a["v_" + n].reshape(d2), name="adamw_" + n)
        delta[n], new_m[n], new_v[n] = d.reshape(shape), m.reshape(shape), v.reshape(shape)
        return d

    r_kv, r_in_b, r_mk1, r_out1 = shared("b", token)
    grads["w_kv"], grads["w_in_b"] = r_kv, r_in_b[None]
    done = [adam("w_kv"), adam("w_in_b")]
    r_glu, r_mk0, r_out0 = shared("a1", done)
    grads["w_glu"], grads["w_mem_kv"], grads["w_out"] = r_glu[None], jnp.stack([r_mk0, r_mk1]), jnp.stack([r_out0, r_out1])
    done = [adam("w_glu"), adam("w_mem_kv"), adam("w_out")]
    (r_in_a,) = shared("a2", done)
    grads["w_in_a"] = r_in_a[None]
    adam("w_in_a")

    (pack,), (blocks,) = _ici_wait(small_sent, [delta[n] for n in _BIG], _BLOCK_ROUTE, name="small_sums_wait")
    blocks = lax.dynamic_update_slice(blocks, pack[None, None], (chip, ci, 0, 0))
    (blocks,) = _gather_forward([blocks], "small", own=True)
    small = dict(zip(small_names, _unpack(_sum_devices(blocks), [g[n].shape for n in small_names])))
    for n in _REPLICATED:
        grads[n] = small[n].reshape(a[n].shape)
    nd = MAIN_WIDTH // N_CHIPS
    grads["d_skip"] = lax.dynamic_slice(small["d_skip"], (chip * nd,), (nd,))[None]
    grads["b_glu"] = lax.dynamic_slice(small["b_glu"], (chip * nd,), (nd,))[None]
    nf = D_MODEL // N_CHIPS
    grads["w_fgate"] = lax.dynamic_slice(small["w_fgate"], (chip * nf, 0), (nf, FOX_HEADS))

    shapes = [a[n].shape for n in small_names]
    d, m, v = _adamw(_pack([a[n] for n in small_names]), _pack([grads[n] for n in small_names]),
                     _pack([a["m_" + n] for n in small_names]), _pack([a["v_" + n] for n in small_names]),
                     name="adamw_small")
    for n, dd, mm, vv in zip(small_names, _unpack(d, shapes), _unpack(m, shapes), _unpack(v, shapes)):
        delta[n], new_m[n], new_v[n] = dd, mm, vv

    return (loss, grad_x[None], *[grads[n] for n in _WEIGHTS], *[delta[n] for n in _WEIGHTS],
            *[new_m[n] for n in _WEIGHTS], *[new_v[n] for n in _WEIGHTS])
```

```python
import math

import jax
import jax.numpy as jnp
from jax import lax
from jax.experimental import pallas as pl
from jax.experimental.pallas import tpu as pltpu

F32 = jnp.float32
BF16 = jnp.bfloat16

D_MODEL = 2048
N_MEM = 256
MAIN_WIDTH = 1536
MEM_WIDTH = 512
IN_WIDTH = 2 * MAIN_WIDTH + 2 * MEM_WIDTH
HEAD_DIM = 128
FOX_HEADS = MAIN_WIDTH // HEAD_DIM
MEM_HEADS = MEM_WIDTH // HEAD_DIM
SSM_GROUP = 16
SSM_GROUPS = MAIN_WIDTH // SSM_GROUP
SSM_STATE = 64
GROUPS_PER_BLOCK = 8
SSM_BLOCKS = SSM_GROUPS // GROUPS_PER_BLOCK
STATE_COLS = GROUPS_PER_BLOCK * SSM_STATE
EPS = 1e-6
ADAM_LR = 0.001
ADAM_B1 = 0.9
ADAM_B2 = 0.999
ADAM_EPS = 1e-08
ADAM_WD = 0.01
ADAM_STEP = 10
N_CHIPS = 4
LANES = 128
SUBLANES = 8
VMEM_LIMIT_BYTES = 56 * 1024 * 1024
NEG_BIG = -1e30
MESH_AXES = ("x", "y", "c")


def _params(*sem):
    return pltpu.CompilerParams(dimension_semantics=sem if sem else None,
                                vmem_limit_bytes=VMEM_LIMIT_BYTES)


def _sigmoid(x):
    return 1.0 / (1.0 + jnp.exp(-x))


def _gelu(x):
    c = math.sqrt(2.0 / math.pi)
    return 0.5 * x * (1.0 + jnp.tanh(c * (x + 0.044715 * (x * x * x))))


def _gelu_grad(x):
    c = math.sqrt(2.0 / math.pi)
    t = jnp.tanh(c * (x + 0.044715 * (x * x * x)))
    return 0.5 * (1.0 + t) + 0.5 * x * (1.0 - t * t) * (c * (1.0 + 3.0 * 0.044715 * (x * x)))


def _silu_and_grad(z):
    s = _sigmoid(z)
    return z * s, s * (1.0 + z * (1.0 - s))


_TILE_CHOICES = (4096, 3072, 2048, 1536, 1024, 768, 512, 384, 256, LANES)


def _tile(n, cap):
    return next(c for c in _TILE_CHOICES if c <= cap and n % c == 0)


def _mm(a, b, *, name, ta=False, tb=False, out_dtype=F32, shards=1, tm=1024, tn=1024, tk=4096):
    if ta:
        K, M = a.shape
    else:
        M, K = a.shape
    if tb:
        N, kb = b.shape
    else:
        kb, N = b.shape
    assert K == kb, (a.shape, b.shape)
    ns = N // shards
    tm, tn, tk = _tile(M, tm), _tile(ns, tn), _tile(K, tk)
    assert M % tm == 0 and ns % tn == 0 and K % tk == 0 and N % shards == 0
    nk = K // tk
    dn = (((0 if ta else 1,), (1 if tb else 0,)), ((), ()))

    def body(a_ref, b_ref, o_ref, *acc):
        prod = lax.dot_general(a_ref[...].astype(BF16), b_ref[...].astype(BF16), dn, preferred_element_type=F32)
        if nk == 1:
            o_ref[...] = prod.astype(o_ref.dtype)
            return
        acc_ref, = acc
        k = pl.program_id(2)

        @pl.when(k == 0)
        def _():
            acc_ref[...] = jnp.zeros_like(acc_ref)

        acc_ref[...] += prod

        @pl.when(k == nk - 1)
        def _():
            o_ref[...] = acc_ref[...].astype(o_ref.dtype)

    a_spec = (pl.BlockSpec((tk, tm), lambda i, j, k: (k, i)) if ta
              else pl.BlockSpec((tm, tk), lambda i, j, k: (i, k)))
    b_spec = (pl.BlockSpec((tn, tk), lambda i, j, k: (j, k)) if tb
              else pl.BlockSpec((tk, tn), lambda i, j, k: (k, j)))
    if shards == 1:
        out_shape = jax.ShapeDtypeStruct((M, N), out_dtype)
        o_spec = pl.BlockSpec((tm, tn), lambda i, j, k: (i, j))
    else:
        nb = ns // tn
        out_shape = jax.ShapeDtypeStruct((shards, M, ns), out_dtype)
        o_spec = pl.BlockSpec((None, tm, tn), lambda i, j, k: (j // nb, i, j % nb))
    return pl.pallas_call(
        body, name=name, out_shape=out_shape,
        grid=(M // tm, N // tn, nk),
        in_specs=[a_spec, b_spec], out_specs=o_spec,
        scratch_shapes=[] if nk == 1 else [pltpu.VMEM((tm, tn), F32)],
        compiler_params=_params("parallel", "parallel", "arbitrary"),
    )(a, b)


def _rmsnorm_fwd(x, g, *, name, res=None, out_dtype=F32, tr=256):
    L, D = x.shape
    tr = min(tr, L)
    has_res = res is not None

    def body(*refs):
        if has_res:
            x_ref, g_ref, r_ref, o_ref = refs
        else:
            x_ref, g_ref, o_ref = refs
        xf = x_ref[...]
        r = lax.rsqrt(jnp.mean(xf * xf, axis=-1, keepdims=True) + EPS)
        y = xf * r * g_ref[...]
        if has_res:
            y = r_ref[...] + y
        o_ref[...] = y.astype(o_ref.dtype)

    row = pl.BlockSpec((tr, D), lambda i: (i, 0))
    vec = pl.BlockSpec((1, D), lambda i: (0, 0))
    ins = [x, g.reshape(1, D)] + ([res] if has_res else [])
    return pl.pallas_call(
        body, name=name, out_shape=jax.ShapeDtypeStruct((L, D), out_dtype),
        grid=(L // tr,), in_specs=[row, vec] + ([row] if has_res else []), out_specs=row,
        compiler_params=_params("parallel"),
    )(*ins)


def _rmsnorm_bwd(x, g, dy, *, name, adds=(), dx_dtype=F32, tr=256):
    L, D = x.shape
    tr = min(tr, L)
    dys = dy if isinstance(dy, tuple) else (dy,)
    n_dy, n_add = len(dys), len(adds)

    def body(*refs):
        x_ref, g_ref = refs[:2]
        dy_refs = refs[2:2 + n_dy]
        add_refs = refs[2 + n_dy:2 + n_dy + n_add]
        dx_ref, dg_ref = refs[2 + n_dy + n_add:]
        xf = x_ref[...]
        dyf = dy_refs[0][...].astype(F32)
        for d_ref in dy_refs[1:]:
            dyf = dyf + d_ref[...].astype(F32)
        r = lax.rsqrt(jnp.mean(xf * xf, axis=-1, keepdims=True) + EPS)
        gy = dyf * g_ref[...]
        c = jnp.mean(xf * gy, axis=-1, keepdims=True) * (r * r * r)
        dx = gy * r - xf * c
        for a_ref in add_refs:
            dx = dx + a_ref[...].astype(F32)
        dx_ref[...] = dx.astype(dx_ref.dtype)

        @pl.when(pl.program_id(0) == 0)
        def _():
            dg_ref[...] = jnp.zeros_like(dg_ref)

        dg_ref[...] += jnp.sum(dyf * xf * r, axis=0, keepdims=True)

    row = pl.BlockSpec((tr, D), lambda i: (i, 0))
    vec = pl.BlockSpec((1, D), lambda i: (0, 0))
    dx, dg = pl.pallas_call(
        body, name=name,
        out_shape=(jax.ShapeDtypeStruct((L, D), dx_dtype), jax.ShapeDtypeStruct((1, D), F32)),
        grid=(L // tr,), in_specs=[row, vec] + [row] * (n_dy + n_add), out_specs=(row, vec),
        compiler_params=_params("arbitrary"),
    )(x, g.reshape(1, D), *dys, *adds)
    return dx, dg.reshape(D)


def _rmsnorm_bwd_pair(x, g1, dy1, g2, dy2, *, name, adds=(), tr=256):
    L, D = x.shape
    tr = min(tr, L)
    dy1s = dy1 if isinstance(dy1, tuple) else (dy1,)
    n1, n_add = len(dy1s), len(adds)

    def body(*refs):
        x_ref, g1_ref, g2_ref = refs[:3]
        dy1_refs = refs[3:3 + n1]
        dy2_ref = refs[3 + n1]
        add_refs = refs[4 + n1:4 + n1 + n_add]
        dx_ref, dg1_ref, dg2_ref = refs[4 + n1 + n_add:]
        xf = x_ref[...]
        d1 = dy1_refs[0][...].astype(F32)
        for d_ref in dy1_refs[1:]:
            d1 = d1 + d_ref[...].astype(F32)
        d2 = dy2_ref[...].astype(F32)
        r = lax.rsqrt(jnp.mean(xf * xf, axis=-1, keepdims=True) + EPS)
        gy = d1 * g1_ref[...] + d2 * g2_ref[...]
        c = jnp.mean(xf * gy, axis=-1, keepdims=True) * (r * r * r)
        dx = gy * r - xf * c
        for a_ref in add_refs:
            dx = dx + a_ref[...].astype(F32)
        dx_ref[...] = dx

        @pl.when(pl.program_id(0) == 0)
        def _():
            dg1_ref[...] = jnp.zeros_like(dg1_ref)
            dg2_ref[...] = jnp.zeros_like(dg2_ref)

        xr = xf * r
        dg1_ref[...] += jnp.sum(d1 * xr, axis=0, keepdims=True)
        dg2_ref[...] += jnp.sum(d2 * xr, axis=0, keepdims=True)

    row = pl.BlockSpec((tr, D), lambda i: (i, 0))
    vec = pl.BlockSpec((1, D), lambda i: (0, 0))
    dx, dg1, dg2 = pl.pallas_call(
        body, name=name,
        out_shape=(jax.ShapeDtypeStruct((L, D), F32), jax.ShapeDtypeStruct((1, D), F32),
                   jax.ShapeDtypeStruct((1, D), F32)),
        grid=(L // tr,), in_specs=[row, vec, vec] + [row] * (n1 + 1 + n_add), out_specs=(row, vec, vec),
        compiler_params=_params("arbitrary"),
    )(x, g1.reshape(1, D), g2.reshape(1, D), *dy1s, dy2, *adds)
    return dx, dg1.reshape(D), dg2.reshape(D)


def _final_norm_loss(o, g, res, target, *, tr=256):
    L, D = o.shape
    tr = min(tr, L)

    def body(o_ref, g_ref, r_ref, t_ref, dh_ref, loss_ref):
        xf = o_ref[...]
        r = lax.rsqrt(jnp.mean(xf * xf, axis=-1, keepdims=True) + EPS)
        e = (r_ref[...] + xf * r * g_ref[...]) - t_ref[...]
        dh_ref[...] = e * (1.0 / D)

        @pl.when(pl.program_id(0) == 0)
        def _():
            loss_ref[...] = jnp.zeros_like(loss_ref)

        loss_ref[...] += jnp.sum(e * e, axis=0, keepdims=True) * (0.5 / D)

    row = pl.BlockSpec((tr, D), lambda i: (i, 0))
    vec = pl.BlockSpec((1, D), lambda i: (0, 0))
    dh, lp = pl.pallas_call(
        body, name="post_norm_1_loss",
        out_shape=(jax.ShapeDtypeStruct((L, D), F32), jax.ShapeDtypeStruct((1, D), F32)),
        grid=(L // tr,), in_specs=[row, vec, row, row], out_specs=(row, vec),
        compiler_params=_params("arbitrary"),
    )(o, g.reshape(1, D), res, target)
    return dh, lp


def _s5_coeffs(lr, li, ls):
    dt = jnp.exp(ls)
    mag = jnp.exp(lr * dt)
    ar = mag * jnp.cos(li * dt)
    ai = mag * jnp.sin(li * dt)
    den = lr * lr + li * li
    cr = ((ar - 1.0) * lr + ai * li) / den
    ci = (ai * lr - (ar - 1.0) * li) / den
    return dt, ar, ai, den, cr, ci


def _s5_prep(lam_re, lam_im, log_step, b_re_t, b_im_t):
    G, P = lam_re.shape
    H = b_re_t.shape[1]

    def body(lr_ref, li_ref, ls_ref, br_ref, bi_ref, ar_ref, ai_ref, bbr_ref, bbi_ref):
        _, ar, ai, _, cr, ci = _s5_coeffs(lr_ref[...], li_ref[...], ls_ref[...])
        ar_ref[...] = ar
        ai_ref[...] = ai
        br, bi = br_ref[...], bi_ref[...]
        crb, cib = cr[:, None, :], ci[:, None, :]
        bbr_ref[...] = crb * br - cib * bi
        bbi_ref[...] = crb * bi + cib * br

    return pl.pallas_call(
        body, name="s5_prep",
        out_shape=(jax.ShapeDtypeStruct((G, P), F32), jax.ShapeDtypeStruct((G, P), F32),
                   jax.ShapeDtypeStruct((G, H, P), F32), jax.ShapeDtypeStruct((G, H, P), F32)),
        compiler_params=_params(),
    )(lam_re, lam_im, log_step.reshape(G, 1), b_re_t, b_im_t)


def _s5_prep_bwd(lam_re, lam_im, log_step, b_re_t, b_im_t, d_ar, d_ai, d_bbr, d_bbi):
    G, P = lam_re.shape
    H = b_re_t.shape[1]

    def body(lr_ref, li_ref, ls_ref, br_ref, bi_ref, dar_ref, dai_ref, dbbr_ref, dbbi_ref,
             dlr_ref, dli_ref, dls_ref, dbr_ref, dbi_ref):
        lr, li = lr_ref[...], li_ref[...]
        dt, ar, ai, den, cr, ci = _s5_coeffs(lr, li, ls_ref[...])
        br, bi = br_ref[...], bi_ref[...]
        gbr, gbi = dbbr_ref[...], dbbi_ref[...]
        crb, cib = cr[:, None, :], ci[:, None, :]
        dbr_ref[...] = crb * gbr + cib * gbi
        dbi_ref[...] = crb * gbi - cib * gbr
        gcr = jnp.sum(br * gbr + bi * gbi, axis=1)
        gci = jnp.sum(br * gbi - bi * gbr, axis=1)
        ilr, ili = lr / den, -li / den
        gar = dar_ref[...] + (ilr * gcr + ili * gci)
        gai = dai_ref[...] + (ilr * gci - ili * gcr)
        qr, qi = cr * ilr - ci * ili, cr * ili + ci * ilr
        glr = -(qr * gcr + qi * gci)
        gli = -(qr * gci - qi * gcr)
        glr = glr + dt * (ar * gar + ai * gai)
        gli = gli + dt * (ar * gai - ai * gar)
        wr, wi = lr * ar - li * ai, lr * ai + li * ar
        gdt = jnp.sum(wr * gar + wi * gai, axis=1, keepdims=True)
        dlr_ref[...] = glr
        dli_ref[...] = gli
        dls_ref[...] = gdt * dt

    return pl.pallas_call(
        body, name="s5_prep_bwd",
        out_shape=(jax.ShapeDtypeStruct((G, P), F32), jax.ShapeDtypeStruct((G, P), F32),
                   jax.ShapeDtypeStruct((G, 1), F32),
                   jax.ShapeDtypeStruct((G, H, P), F32), jax.ShapeDtypeStruct((G, H, P), F32)),
        compiler_params=_params(),
    )(lam_re, lam_im, log_step.reshape(G, 1), b_re_t, b_im_t, d_ar, d_ai, d_bbr, d_bbi)


def _s5_block_mats(bbr_t, bbi_t, c_re, c_im):
    bmat = _s5_expand(bbr_t, bbi_t)
    cmat = jnp.transpose(_s5_expand(c_re, -c_im), (0, 2, 1))
    return bmat.astype(BF16), cmat.astype(BF16)


def _s5_diag_mask():
    r = lax.broadcasted_iota(jnp.int32, (LANES, 2 * STATE_COLS), 0) // SSM_GROUP
    c = (lax.broadcasted_iota(jnp.int32, (LANES, 2 * STATE_COLS), 1) % STATE_COLS) // SSM_STATE
    return (r == c).astype(F32)


def _s5_expand(re, im):
    re = jnp.tile(re.reshape(SSM_BLOCKS, LANES, SSM_STATE), (1, 1, GROUPS_PER_BLOCK))
    im = jnp.tile(im.reshape(SSM_BLOCKS, LANES, SSM_STATE), (1, 1, GROUPS_PER_BLOCK))
    return jnp.concatenate([re, im], axis=-1) * _s5_diag_mask()[None]


def _s5_unfold(dmat):
    d = dmat.reshape(SSM_GROUPS, SSM_GROUP, 2, SSM_STATE)
    return jnp.transpose(d, (2, 0, 1, 3))


def _s5_a_rows(ar, ai):
    a = jnp.concatenate([ar.reshape(SSM_BLOCKS, STATE_COLS), ai.reshape(SSM_BLOCKS, STATE_COLS)], axis=1)
    return jnp.broadcast_to(a[:, None, :], (SSM_BLOCKS, SUBLANES, 2 * STATE_COLS))


def _to_step_major(src_ref, dst_ref, seg):
    for s in range(SUBLANES):
        dst_ref[pl.ds(s, seg, stride=SUBLANES), :] = src_ref[pl.ds(seg * s, seg), :]


def _segment_rows(ref, s, seg):
    return ref[pl.ds(s, seg, stride=SUBLANES), :]


def _cmul(ar, ai, xr, xi):
    return ar * xr - ai * xi, ar * xi + ai * xr


def _s5_tables(a_ref, pw_s, pwr_s, S, seg):
    ar, ai = a_ref[:, :S], a_ref[:, S:]

    def step(i, c):
        pr, pi = c
        pw_s[i, :, :S] = pr
        pw_s[i, :, S:] = pi
        nr, ni = _cmul(ar, ai, pr, pi)
        pwr_s[seg - 1 - i, :, :S] = nr
        pwr_s[seg - 1 - i, :, S:] = ni
        return nr, ni

    pr, pi = lax.fori_loop(0, seg, step, (jnp.ones_like(ar), jnp.zeros_like(ai)))
    pw_s[seg, :, :S] = pr
    pw_s[seg, :, S:] = pi


def _s5_fwd(proj, bmat, cmat, a_rows, d_skip, *, tc=512):
    L = proj.shape[0]
    tc = min(tc, L)
    nt = L // tc
    seg = tc // SUBLANES
    S = STATE_COLS

    def body(u_ref, b_ref, c_ref, a_ref, d_ref, y_ref, yg_ref, xp_ref,
             bu_s, xp_s, pw_s, pwr_s, carry_s, e_s, up_s, yc_s):
        @pl.when(pl.program_id(1) == 0)
        def _():
            carry_s[...] = jnp.zeros_like(carry_s)
            _s5_tables(a_ref, pw_s, pwr_s, S, seg)

        ar, ai = a_ref[:, :S], a_ref[:, S:]
        _to_step_major(u_ref, up_s, seg)
        bu = jnp.dot(up_s[...].astype(BF16), b_ref[...], preferred_element_type=F32)
        bu_s[...] = bu.reshape(seg, SUBLANES, 2 * S)

        def step(i, carry):
            cr, ci = carry
            xp_s[i, :, :S] = cr
            xp_s[i, :, S:] = ci
            return ar * cr - ai * ci + bu_s[i, :, :S], ar * ci + ai * cr + bu_s[i, :, S:]

        zero = jnp.zeros((SUBLANES, S), F32)
        fr, fi = lax.fori_loop(0, seg, step, (zero, zero))
        pr, pi = pw_s[seg, 0:1, :S], pw_s[seg, 0:1, S:]
        er, ei = carry_s[0:1, :S], carry_s[0:1, S:]
        for s in range(SUBLANES):
            e_s[s:s + 1, :S] = er
            e_s[s:s + 1, S:] = ei
            tr, ti = _cmul(pr, pi, er, ei)
            er, ei = fr[s:s + 1] + tr, fi[s:s + 1] + ti
        carry_s[0:1, :S] = er
        carry_s[0:1, S:] = ei
        pw = pw_s[0:seg]
        tr, ti = _cmul(pw[:, :, :S], pw[:, :, S:], e_s[:, :S][None], e_s[:, S:][None])
        xl = xp_s[...]
        xp = jnp.concatenate([xl[:, :, :S] + tr, xl[:, :, S:] + ti], axis=-1).reshape(tc, 2 * S)
        xp_ref[...] = xp
        a1r, a1i = ar[0:1], ai[0:1]
        x_re = a1r * xp[:, :S] - a1i * xp[:, S:] + bu[:, :S]
        x_im = a1r * xp[:, S:] + a1i * xp[:, :S] + bu[:, S:]
        xs = jnp.concatenate([x_re, x_im], axis=1).astype(BF16)
        yc_s[...] = jnp.dot(xs, c_ref[...], preferred_element_type=F32)
        for s in range(SUBLANES):
            rows = pl.ds(seg * s, seg)
            y = _segment_rows(yc_s, s, seg) + d_ref[...] * u_ref[rows, :]
            y_ref[rows, :] = y
            yg_ref[rows, :] = _gelu(y).astype(BF16)

    return pl.pallas_call(
        body, name="s5_fwd",
        out_shape=(jax.ShapeDtypeStruct((L, MAIN_WIDTH), F32),
                   jax.ShapeDtypeStruct((L, MAIN_WIDTH), BF16),
                   jax.ShapeDtypeStruct((L, SSM_BLOCKS * 2 * S), F32)),
        grid=(SSM_BLOCKS, nt),
        in_specs=[pl.BlockSpec((tc, LANES), lambda b, t: (t, b)),
                  pl.BlockSpec((None, LANES, 2 * S), lambda b, t: (b, 0, 0)),
                  pl.BlockSpec((None, 2 * S, LANES), lambda b, t: (b, 0, 0)),
                  pl.BlockSpec((None, SUBLANES, 2 * S), lambda b, t: (b, 0, 0)),
                  pl.BlockSpec((1, LANES), lambda b, t: (0, b))],
        out_specs=(pl.BlockSpec((tc, LANES), lambda b, t: (t, b)),
                   pl.BlockSpec((tc, LANES), lambda b, t: (t, b)),
                   pl.BlockSpec((tc, 2 * S), lambda b, t: (t, b))),
        scratch_shapes=[pltpu.VMEM((seg, SUBLANES, 2 * S), F32),
                        pltpu.VMEM((seg, SUBLANES, 2 * S), F32),
                        pltpu.VMEM((seg + 1, SUBLANES, 2 * S), F32),
                        pltpu.VMEM((seg, SUBLANES, 2 * S), F32),
                        pltpu.VMEM((SUBLANES, 2 * S), F32),
                        pltpu.VMEM((SUBLANES, 2 * S), F32),
                        pltpu.VMEM((tc, LANES), F32),
                        pltpu.VMEM((tc, LANES), F32)],
        compiler_params=_params("parallel", "arbitrary"),
    )(proj, bmat, cmat, a_rows, d_skip.reshape(1, MAIN_WIDTH))


def _s5_bwd(proj, dyg_a, dyg_b, y, xp, bmat, cmat, a_rows, d_skip, dproj, *, tc=512):
    L = proj.shape[0]
    tc = min(tc, L)
    nt = L // tc
    seg = tc // SUBLANES
    S = STATE_COLS
    nn = (((1,), (1,)), ((), ()))
    tn = (((0,), (0,)), ((), ()))

    def fold_diagonal(acc_ref, mask_ref, fold_ref):
        x = acc_ref[...] * mask_ref[...]
        hi = x.astype(BF16)
        rest = x - hi.astype(F32)
        mid = rest.astype(BF16)
        low = (rest - mid.astype(F32)).astype(BF16)
        return sum(jnp.dot(piece, fold_ref[...], preferred_element_type=F32) for piece in (hi, mid, low))

    def body(u_ref, dyga_ref, dygb_ref, y_ref, xp_ref, b_ref, c_ref, a_ref, d_ref, mask_ref, fold_ref, dp_hbm,
             du_ref, dbd_ref, dcd_ref, da_ref, dd_ref,
             dl_s, pw_s, pwr_s, carry_s, e_s, up_s, dy_s, dyp_s, dup_s, db_ref, dc_ref):
        @pl.when(pl.program_id(1) == 0)
        def _():
            carry_s[...] = jnp.zeros_like(carry_s)
            db_ref[...] = jnp.zeros_like(db_ref)
            dc_ref[...] = jnp.zeros_like(dc_ref)
            da_ref[...] = jnp.zeros_like(da_ref)
            dd_ref[...] = jnp.zeros_like(dd_ref)
            _s5_tables(a_ref, pw_s, pwr_s, S, seg)

        ar, ai = a_ref[:, :S], a_ref[:, S:]
        a1r, a1i = ar[0:1], ai[0:1]
        u = u_ref[...]
        dy = (dyga_ref[...] + dygb_ref[...]) * _gelu_grad(y_ref[...])
        dy_s[...] = dy
        xp = xp_ref[...]
        _to_step_major(u_ref, up_s, seg)
        _to_step_major(dy_s, dyp_s, seg)
        ubp = up_s[...].astype(BF16)
        dyp = dyp_s[...].astype(BF16)
        bu = jnp.dot(ubp, b_ref[...], preferred_element_type=F32)
        x_re = a1r * xp[:, :S] - a1i * xp[:, S:] + bu[:, :S]
        x_im = a1r * xp[:, S:] + a1i * xp[:, :S] + bu[:, S:]
        xs = jnp.concatenate([x_re, x_im], axis=1).astype(BF16)
        dc_ref[...] += lax.dot_general(dyp, xs, tn, preferred_element_type=F32)
        dx = lax.dot_general(dyp, c_ref[...], nn, preferred_element_type=F32)
        dl_s[...] = dx.reshape(seg, SUBLANES, 2 * S)

        def step(k, carry):
            cr, ci = carry
            i = seg - 1 - k
            lr = dl_s[i, :, :S] + (ar * cr + ai * ci)
            li = dl_s[i, :, S:] + (ar * ci - ai * cr)
            dl_s[i, :, :S] = lr
            dl_s[i, :, S:] = li
            return lr, li

        zero = jnp.zeros((SUBLANES, S), F32)
        fr, fi = lax.fori_loop(0, seg, step, (zero, zero))
        pr, pi = pw_s[seg, 0:1, :S], pw_s[seg, 0:1, S:]
        er, ei = carry_s[0:1, :S], carry_s[0:1, S:]
        for s in range(SUBLANES - 1, -1, -1):
            e_s[s:s + 1, :S] = er
            e_s[s:s + 1, S:] = ei
            er, ei = fr[s:s + 1] + (pr * er + pi * ei), fi[s:s + 1] + (pr * ei - pi * er)
        carry_s[0:1, :S] = er
        carry_s[0:1, S:] = ei
        er, ei = e_s[:, :S][None], e_s[:, S:][None]
        pw = pwr_s[...]
        pwr, pwi = pw[:, :, :S], pw[:, :, S:]
        ll = dl_s[...]
        lam = jnp.concatenate([ll[:, :, :S] + (pwr * er + pwi * ei), ll[:, :, S:] + (pwr * ei - pwi * er)],
                              axis=-1).reshape(tc, 2 * S)
        l_re, l_im = lam[:, :S], lam[:, S:]
        da_ref[0:1, :S] += jnp.sum(l_re * xp[:, :S] + l_im * xp[:, S:], axis=0, keepdims=True)
        da_ref[0:1, S:] += jnp.sum(l_im * xp[:, :S] - l_re * xp[:, S:], axis=0, keepdims=True)
        lamb = lam.astype(BF16)
        dup_s[...] = lax.dot_general(lamb, b_ref[...], nn, preferred_element_type=F32)
        for s in range(SUBLANES):
            rows = pl.ds(seg * s, seg)
            du = _segment_rows(dup_s, s, seg) + d_ref[...] * dy_s[rows, :]
            du_ref[rows, :] = du.astype(du_ref.dtype)
        db_ref[...] += lax.dot_general(ubp, lamb, tn, preferred_element_type=F32)
        dd_ref[0:1, :] += jnp.sum(dy * u, axis=0, keepdims=True)

        @pl.when(pl.program_id(1) == nt - 1)
        def _():
            dbd_ref[...] = fold_diagonal(db_ref, mask_ref, fold_ref)
            dcd_ref[...] = fold_diagonal(dc_ref, mask_ref, fold_ref)

    rev = lambda b, t: (nt - 1 - t, b)
    col = jnp.arange(2 * S)
    fold = ((col // S * SSM_STATE + col % SSM_STATE)[:, None] == jnp.arange(LANES)[None, :]).astype(BF16)
    return pl.pallas_call(
        body, name="s5_bwd",
        out_shape=(jax.ShapeDtypeStruct(dproj.shape, dproj.dtype),
                   jax.ShapeDtypeStruct((SSM_BLOCKS, LANES, LANES), F32),
                   jax.ShapeDtypeStruct((SSM_BLOCKS, LANES, LANES), F32),
                   jax.ShapeDtypeStruct((SSM_BLOCKS, SUBLANES, 2 * S), F32),
                   jax.ShapeDtypeStruct((SUBLANES, MAIN_WIDTH), F32)),
        input_output_aliases={11: 0},
        grid=(SSM_BLOCKS, nt),
        in_specs=[pl.BlockSpec((tc, LANES), rev),
                  pl.BlockSpec((tc, LANES), rev),
                  pl.BlockSpec((tc, LANES), rev),
                  pl.BlockSpec((tc, LANES), rev),
                  pl.BlockSpec((tc, 2 * S), rev),
                  pl.BlockSpec((None, LANES, 2 * S), lambda b, t: (b, 0, 0)),
                  pl.BlockSpec((None, 2 * S, LANES), lambda b, t: (b, 0, 0)),
                  pl.BlockSpec((None, SUBLANES, 2 * S), lambda b, t: (b, 0, 0)),
                  pl.BlockSpec((1, LANES), lambda b, t: (0, b)),
                  pl.BlockSpec((LANES, 2 * S), lambda b, t: (0, 0)),
                  pl.BlockSpec((2 * S, LANES), lambda b, t: (0, 0)),
                  _ANY],
        out_specs=(pl.BlockSpec((tc, LANES), rev),
                   pl.BlockSpec((None, LANES, LANES), lambda b, t: (b, 0, 0)),
                   pl.BlockSpec((None, LANES, LANES), lambda b, t: (b, 0, 0)),
                   pl.BlockSpec((None, SUBLANES, 2 * S), lambda b, t: (b, 0, 0)),
                   pl.BlockSpec((SUBLANES, LANES), lambda b, t: (0, b))),
        scratch_shapes=[pltpu.VMEM((seg, SUBLANES, 2 * S), F32),
                        pltpu.VMEM((seg + 1, SUBLANES, 2 * S), F32),
                        pltpu.VMEM((seg, SUBLANES, 2 * S), F32),
                        pltpu.VMEM((SUBLANES, 2 * S), F32),
                        pltpu.VMEM((SUBLANES, 2 * S), F32),
                        pltpu.VMEM((tc, LANES), F32),
                        pltpu.VMEM((tc, LANES), F32),
                        pltpu.VMEM((tc, LANES), F32),
                        pltpu.VMEM((tc, LANES), F32),
                        pltpu.VMEM((LANES, 2 * S), F32),
                        pltpu.VMEM((LANES, 2 * S), F32)],
        compiler_params=_params("parallel", "arbitrary"),
    )(proj, dyg_a, dyg_b, y, xp, bmat, cmat, a_rows, d_skip.reshape(1, MAIN_WIDTH), _s5_diag_mask(), fold, dproj)


_Z_COLS = slice(MAIN_WIDTH, 2 * MAIN_WIDTH)
_ZM_COLS = slice(2 * MAIN_WIDTH + MEM_WIDTH, IN_WIDTH)


def _proj_rows(tr):
    return pl.BlockSpec((tr, IN_WIDTH), lambda i: (i, 0))


def _row_specs(tr):
    main = pl.BlockSpec((tr, MAIN_WIDTH), lambda i: (i, 0))
    z = pl.BlockSpec((tr, MAIN_WIDTH), lambda i: (i, 1))
    zm = pl.BlockSpec((tr, MEM_WIDTH), lambda i: (i, IN_WIDTH // MEM_WIDTH - 1))
    mem = pl.BlockSpec((tr, MEM_WIDTH), lambda i: (i, 0))
    cat = pl.BlockSpec((tr, D_MODEL), lambda i: (i, 0))
    vec = pl.BlockSpec((1, MAIN_WIDTH), lambda i: (0, 0))
    return main, z, zm, mem, cat, vec


def _gate_a_fwd(y, t, b_glu, proj, o_mem, *, tr=256):
    L = y.shape[0]
    tr = min(tr, L)

    def body(y_ref, t_ref, b_ref, z_ref, zm_ref, om_ref, o_ref):
        yg = _gelu(y_ref[...])
        sz, _ = _silu_and_grad(z_ref[...])
        o_ref[:, :MAIN_WIDTH] = (yg * _sigmoid(t_ref[...] + b_ref[...]) * sz).astype(BF16)
        szm, _ = _silu_and_grad(zm_ref[...])
        o_ref[:, MAIN_WIDTH:] = (om_ref[...] * szm).astype(BF16)

    main, z, zm, mem, cat, vec = _row_specs(tr)
    return pl.pallas_call(
        body, name="gate_a_fwd", out_shape=jax.ShapeDtypeStruct((L, D_MODEL), BF16),
        grid=(L // tr,), in_specs=[main, main, vec, z, zm, mem], out_specs=cat,
        compiler_params=_params("parallel"),
    )(y, t, b_glu.reshape(1, MAIN_WIDTH), proj, proj, o_mem)


def _gate_a_bwd(dcat, y, t, b_glu, proj, o_mem, *, tr=256):
    L = y.shape[0]
    tr = min(tr, L)

    def body(dc_ref, y_ref, t_ref, b_ref, z_ref, zm_ref, om_ref,
             dp_ref, dt_ref, dyg_ref, dom_ref, db_ref):
        dmain = dc_ref[:, :MAIN_WIDTH]
        dmemo = dc_ref[:, MAIN_WIDTH:]
        yg = _gelu(y_ref[...])
        sg = _sigmoid(t_ref[...] + b_ref[...])
        sz, gz = _silu_and_grad(z_ref[...])
        dp_ref[:, _Z_COLS] = (dmain * (yg * sg) * gz).astype(BF16)
        dy2 = dmain * sz
        dyg_ref[...] = dy2 * sg
        dt = dy2 * yg * (sg * (1.0 - sg))
        dt_ref[...] = dt.astype(BF16)

        @pl.when(pl.program_id(0) == 0)
        def _():
            db_ref[...] = jnp.zeros_like(db_ref)

        db_ref[...] += jnp.sum(dt, axis=0, keepdims=True)
        szm, gzm = _silu_and_grad(zm_ref[...])
        dom_ref[...] = dmemo * szm
        dp_ref[:, _ZM_COLS] = (dmemo * om_ref[...] * gzm).astype(BF16)

    main, z, zm, mem, cat, vec = _row_specs(tr)
    outs = pl.pallas_call(
        body, name="gate_a_bwd",
        out_shape=(jax.ShapeDtypeStruct((L, IN_WIDTH), BF16),
                   jax.ShapeDtypeStruct((L, MAIN_WIDTH), BF16), jax.ShapeDtypeStruct((L, MAIN_WIDTH), F32),
                   jax.ShapeDtypeStruct((L, MEM_WIDTH), F32), jax.ShapeDtypeStruct((1, MAIN_WIDTH), F32)),
        grid=(L // tr,), in_specs=[cat, main, main, vec, z, zm, mem],
        out_specs=(_proj_rows(tr), main, main, mem, vec),
        compiler_params=_params("arbitrary"),
    )(dcat, y, t, b_glu.reshape(1, MAIN_WIDTH), proj, proj, o_mem)
    return outs


def _gate_b_fwd(att, proj, o_mem, *, tr=256):
    L = att.shape[0]
    tr = min(tr, L)

    def body(a_ref, z_ref, zm_ref, om_ref, o_ref):
        sz, _ = _silu_and_grad(z_ref[...])
        o_ref[:, :MAIN_WIDTH] = (a_ref[...] * sz).astype(BF16)
        szm, _ = _silu_and_grad(zm_ref[...])
        o_ref[:, MAIN_WIDTH:] = (om_ref[...] * szm).astype(BF16)

    main, z, zm, mem, cat, _ = _row_specs(tr)
    return pl.pallas_call(
        body, name="gate_b_fwd", out_shape=jax.ShapeDtypeStruct((L, D_MODEL), BF16),
        grid=(L // tr,), in_specs=[main, z, zm, mem], out_specs=cat,
        compiler_params=_params("parallel"),
    )(att, proj, proj, o_mem)


def _gate_b_bwd(dcat, att, proj, o_mem, *, tr=256):
    L = att.shape[0]
    tr = min(tr, L)

    def body(dc_ref, a_ref, z_ref, zm_ref, om_ref, da_ref, dp_ref, dom_ref, dl_ref):
        dmain = dc_ref[:, :MAIN_WIDTH]
        dmemo = dc_ref[:, MAIN_WIDTH:]
        att = a_ref[...]
        sz, gz = _silu_and_grad(z_ref[...])
        datt = dmain * sz
        da_ref[...] = datt
        dp_ref[:, _Z_COLS] = (dmain * att * gz).astype(BF16)
        szm, gzm = _silu_and_grad(zm_ref[...])
        dom_ref[...] = dmemo * szm
        dp_ref[:, _ZM_COLS] = (dmemo * om_ref[...] * gzm).astype(BF16)
        prod = datt * att
        for h in range(FOX_HEADS):
            dl_ref[h] = jnp.sum(prod[:, h * HEAD_DIM:(h + 1) * HEAD_DIM], axis=1, keepdims=True)

    main, z, zm, mem, cat, _ = _row_specs(tr)
    delta = pl.BlockSpec((FOX_HEADS, tr, 1), lambda i: (0, i, 0))
    return pl.pallas_call(
        body, name="gate_b_bwd",
        out_shape=(jax.ShapeDtypeStruct((L, MAIN_WIDTH), F32), jax.ShapeDtypeStruct((L, IN_WIDTH), BF16),
                   jax.ShapeDtypeStruct((L, MEM_WIDTH), F32), jax.ShapeDtypeStruct((FOX_HEADS, L, 1), F32)),
        grid=(L // tr,), in_specs=[cat, main, z, zm, mem], out_specs=(main, _proj_rows(tr), mem, delta),
        compiler_params=_params("parallel"),
    )(dcat, att, proj, proj, o_mem)


_MEM_Q_COL = (2 * MAIN_WIDTH) // HEAD_DIM
_NT = (((1,), (1,)), ((), ()))
_TN = (((0,), (0,)), ((), ()))


def _mem_probs(q_ref, k_ref):
    qs = (q_ref[...] * (HEAD_DIM ** -0.5)).astype(BF16)
    s = lax.dot_general(qs, k_ref[...].astype(BF16), _NT, preferred_element_type=F32)
    e = jnp.exp(s - jnp.max(s, axis=-1, keepdims=True))
    return qs, e / jnp.sum(e, axis=-1, keepdims=True)


def _mem_attn_fwd(proj, kvm, *, tq=2048):
    L = proj.shape[0]
    tq = min(tq, L)

    def body(q_ref, k_ref, v_ref, o_ref):
        _, p = _mem_probs(q_ref, k_ref)
        o_ref[...] = jnp.dot(p.astype(BF16), v_ref[...].astype(BF16), preferred_element_type=F32)

    return pl.pallas_call(
        body, name="mem_attn_fwd", out_shape=jax.ShapeDtypeStruct((L, MEM_WIDTH), F32),
        grid=(MEM_HEADS, L // tq),
        in_specs=[pl.BlockSpec((tq, HEAD_DIM), lambda h, i: (i, _MEM_Q_COL + h)),
                  pl.BlockSpec((N_MEM, HEAD_DIM), lambda h, i: (0, h)),
                  pl.BlockSpec((N_MEM, HEAD_DIM), lambda h, i: (0, MEM_HEADS + h))],
        out_specs=pl.BlockSpec((tq, HEAD_DIM), lambda h, i: (i, h)),
        compiler_params=_params("parallel", "parallel"),
    )(proj, kvm, kvm)


def _mem_attn_bwd(proj, kvm, do, dproj, *, tq=2048):
    L = proj.shape[0]
    tq = min(tq, L)

    def body(q_ref, k_ref, v_ref, do_ref, dp_hbm, dq_ref, dk_ref, dv_ref):
        @pl.when(pl.program_id(1) == 0)
        def _():
            dk_ref[...] = jnp.zeros_like(dk_ref)
            dv_ref[...] = jnp.zeros_like(dv_ref)

        qs, p = _mem_probs(q_ref, k_ref)
        dob = do_ref[...].astype(BF16)
        dp = lax.dot_general(dob, v_ref[...].astype(BF16), _NT, preferred_element_type=F32)
        ds = p * (dp - jnp.sum(p * dp, axis=-1, keepdims=True))
        dsb = ds.astype(BF16)
        dq = jnp.dot(dsb, k_ref[...].astype(BF16), preferred_element_type=F32) * (HEAD_DIM ** -0.5)
        dq_ref[...] = dq.astype(BF16)
        dk_ref[...] += lax.dot_general(dsb, qs, _TN, preferred_element_type=F32)
        dv_ref[...] += lax.dot_general(p.astype(BF16), dob, _TN, preferred_element_type=F32)

    dproj, dk, dv = pl.pallas_call(
        body, name="mem_attn_bwd",
        out_shape=(jax.ShapeDtypeStruct(dproj.shape, dproj.dtype),
                   jax.ShapeDtypeStruct((N_MEM, MEM_WIDTH), F32),
                   jax.ShapeDtypeStruct((N_MEM, MEM_WIDTH), F32)),
        grid=(MEM_HEADS, L // tq),
        in_specs=[pl.BlockSpec((tq, HEAD_DIM), lambda h, i: (i, _MEM_Q_COL + h)),
                  pl.BlockSpec((N_MEM, HEAD_DIM), lambda h, i: (0, h)),
                  pl.BlockSpec((N_MEM, HEAD_DIM), lambda h, i: (0, MEM_HEADS + h)),
                  pl.BlockSpec((tq, HEAD_DIM), lambda h, i: (i, h)),
                  _ANY],
        out_specs=(pl.BlockSpec((tq, HEAD_DIM), lambda h, i: (i, _MEM_Q_COL + h)),
                   pl.BlockSpec((N_MEM, HEAD_DIM), lambda h, i: (0, h)),
                   pl.BlockSpec((N_MEM, HEAD_DIM), lambda h, i: (0, h))),
        input_output_aliases={4: 0},
        compiler_params=_params("parallel", "arbitrary"),
    )(proj, kvm, kvm, do, dproj)
    return dproj, jnp.concatenate([dk, dv], axis=1)


def _tile_cumsum(x, row, reverse):
    for sh in (1, 2, 4):
        if reverse:
            x = x + jnp.where(row < SUBLANES - sh, pltpu.roll(x, SUBLANES - sh, 0), 0.0)
        else:
            x = x + jnp.where(row >= sh, pltpu.roll(x, sh, 0), 0.0)
    return x


def _fgate_fwd(pre, b_pad):
    L = pre.shape[0]
    n8 = L // SUBLANES

    def body(p_ref, b_ref, o_ref):
        row = lax.broadcasted_iota(jnp.int32, (SUBLANES, LANES), 0)
        b = b_ref[...]

        def step(i, carry):
            x = p_ref[i] + b
            logf = jnp.minimum(x, 0.0) - jnp.log(1.0 + jnp.exp(-jnp.abs(x)))
            t = _tile_cumsum(logf, row, False) + carry
            o_ref[i] = t
            return t[SUBLANES - 1:SUBLANES, :]

        lax.fori_loop(0, n8, step, jnp.zeros((1, LANES), F32))

    out = pl.pallas_call(
        body, name="fgate_fwd", out_shape=jax.ShapeDtypeStruct((n8, SUBLANES, LANES), F32),
        compiler_params=_params(),
    )(pre.reshape(n8, SUBLANES, LANES), b_pad.reshape(1, LANES))
    return out.reshape(L, LANES)


def _fgate_bwd(dfcum, pre, b_pad):
    L = pre.shape[0]
    n8 = L // SUBLANES

    def body(d_ref, p_ref, b_ref, o_ref, s_ref):
        row = lax.broadcasted_iota(jnp.int32, (SUBLANES, LANES), 0)
        b = b_ref[...]

        def step(k, carry):
            c, acc = carry
            i = n8 - 1 - k
            t = _tile_cumsum(d_ref[i], row, True) + c
            dpre = t * _sigmoid(-(p_ref[i] + b))
            o_ref[i] = dpre
            return t[0:1, :], acc + dpre

        _, acc = lax.fori_loop(0, n8, step, (jnp.zeros((1, LANES), F32), jnp.zeros((SUBLANES, LANES), F32)))
        s_ref[...] = jnp.sum(acc, axis=0, keepdims=True)

    dpre, db = pl.pallas_call(
        body, name="fgate_bwd",
        out_shape=(jax.ShapeDtypeStruct((n8, SUBLANES, LANES), F32), jax.ShapeDtypeStruct((1, LANES), F32)),
        compiler_params=_params(),
    )(dfcum.reshape(n8, SUBLANES, LANES), pre.reshape(n8, SUBLANES, LANES), b_pad.reshape(1, LANES))
    return dpre.reshape(L, LANES), db


FOX_BLOCK = 1024


def _fox_scores(qs, k, fk, diagonal):
    s = lax.dot_general(qs, k, _NT, preferred_element_type=F32) - fk
    if diagonal:
        row = lax.broadcasted_iota(jnp.int32, s.shape, 0)
        col = lax.broadcasted_iota(jnp.int32, s.shape, 1)
        s = jnp.where(row >= col, s, NEG_BIG)
    return s


def _fox_specs(tq, L):
    nq = L // tq
    return dict(
        rows=lambda off: pl.BlockSpec((tq, HEAD_DIM), lambda h, i: (i, off + h)),
        seq=lambda off: pl.BlockSpec((L, HEAD_DIM), lambda h, i: (0, off + h)),
        col=pl.BlockSpec((None, None, tq, 1), lambda h, i: (h, i, 0, 0)),
        col_all=pl.BlockSpec((None, nq, tq, 1), lambda h, i: (h, 0, 0, 0)),
        row=pl.BlockSpec((None, None, 1, tq), lambda h, i: (h, i, 0, 0)),
        row_all=pl.BlockSpec((None, nq, 1, tq), lambda h, i: (h, 0, 0, 0)))


FOX_FWD_HEADS = 1
FOX_FWD_BLOCK = 1024


def _fox_fwd(proj, kv, fk):
    L = proj.shape[0]
    tq = min(FOX_FWD_BLOCK, L)
    nq = L // tq
    nh = FOX_FWD_HEADS
    W = nh * HEAD_DIM
    lse_shape = fk.shape[:2] + (fk.shape[3], 1)
    fk = fk.reshape(FOX_HEADS, nq, 1, tq)

    def body(q_ref, k_ref, v_ref, fk_ref, o_ref, lse_ref, m_s, l_s, acc_s):
        qi = pl.program_id(1)
        cols = [slice(a * HEAD_DIM, (a + 1) * HEAD_DIM) for a in range(nh)]
        qs = [(q_ref[:, cs] * (HEAD_DIM ** -0.5)).astype(BF16) for cs in cols]
        m_s[...] = jnp.full_like(m_s, NEG_BIG)
        l_s[...] = jnp.zeros_like(l_s)
        acc_s[...] = jnp.zeros_like(acc_s)

        def block(j, diagonal):
            r0 = pl.multiple_of(j * tq, tq)
            for a, cs in enumerate(cols):
                s = _fox_scores(qs[a], k_ref[pl.ds(r0, tq), cs], fk_ref[a, j], diagonal)
                m_new = jnp.maximum(m_s[a], jnp.max(s, axis=-1, keepdims=True))
                alpha = jnp.exp(m_s[a] - m_new)
                p = jnp.exp(s - m_new)
                l_s[a] = alpha * l_s[a] + jnp.sum(p, axis=-1, keepdims=True)
                acc_s[a] = alpha * acc_s[a] + jnp.dot(p.astype(BF16), v_ref[pl.ds(r0, tq), cs],
                                                      preferred_element_type=F32)
                m_s[a] = m_new

        def below(j, carry):
            block(j, False)
            return carry

        lax.fori_loop(0, qi, below, 0)
        block(qi, True)
        for a, cs in enumerate(cols):
            o_ref[:, cs] = acc_s[a] / l_s[a]
            lse_ref[a] = m_s[a] + jnp.log(l_s[a])

    att, lse = pl.pallas_call(
        body, name="fox_fwd",
        out_shape=(jax.ShapeDtypeStruct((L, MAIN_WIDTH), F32),
                   jax.ShapeDtypeStruct((FOX_HEADS, nq, tq, 1), F32)),
        grid=(FOX_HEADS // nh, nq),
        in_specs=[pl.BlockSpec((tq, W), lambda h, i: (i, h)),
                  pl.BlockSpec((L, W), lambda h, i: (0, h)),
                  pl.BlockSpec((L, W), lambda h, i: (0, FOX_HEADS // nh + h)),
                  pl.BlockSpec((nh, nq, 1, tq), lambda h, i: (h, 0, 0, 0))],
        out_specs=(pl.BlockSpec((tq, W), lambda h, i: (i, h)),
                   pl.BlockSpec((nh, None, tq, 1), lambda h, i: (h, i, 0, 0))),
        scratch_shapes=[pltpu.VMEM((nh, tq, 1), F32), pltpu.VMEM((nh, tq, 1), F32),
                        pltpu.VMEM((nh, tq, HEAD_DIM), F32)],
        compiler_params=_params("parallel", "parallel"),
    )(proj, kv, kv, fk)
    return att, lse.reshape(lse_shape)


def _fox_bwd_dq(proj, kv, fk, lse, delta, datt, dproj):
    L = proj.shape[0]
    tq = min(FOX_BLOCK, L)
    nq = L // tq
    sp = _fox_specs(tq, L)

    def body(q_ref, k_ref, v_ref, fk_ref, lse_ref, dl_ref, do_ref, dp_hbm, dq_ref, df_ref, acc_s, df_s):
        qi = pl.program_id(1)
        qs = (q_ref[...] * (HEAD_DIM ** -0.5)).astype(BF16)
        dob = do_ref[...].astype(BF16)
        lse, dl = lse_ref[...], dl_ref[...]
        acc_s[...] = jnp.zeros_like(acc_s)
        df_s[...] = jnp.zeros_like(df_s)

        def block(j, diagonal):
            r0 = pl.multiple_of(j * tq, tq)
            k = k_ref[pl.ds(r0, tq), :]
            p = jnp.exp(_fox_scores(qs, k, fk_ref[j], diagonal) - lse)
            dp = lax.dot_general(dob, v_ref[pl.ds(r0, tq), :], _NT, preferred_element_type=F32)
            ds = p * (dp - dl)
            acc_s[...] += jnp.dot(ds.astype(BF16), k, preferred_element_type=F32)
            df_s[...] += jnp.sum(ds, axis=1, keepdims=True)

        def below(j, carry):
            block(j, False)
            return carry

        lax.fori_loop(0, qi, below, 0)
        block(qi, True)
        dq_ref[...] = (acc_s[...] * (HEAD_DIM ** -0.5)).astype(BF16)
        df_ref[...] = df_s[...]

    return pl.pallas_call(
        body, name="fox_bwd_dq",
        out_shape=(jax.ShapeDtypeStruct(dproj.shape, dproj.dtype),
                   jax.ShapeDtypeStruct((FOX_HEADS, nq, tq, 1), F32)),
        grid=(FOX_HEADS, nq),
        in_specs=[sp["rows"](0), sp["seq"](0), sp["seq"](FOX_HEADS), sp["row_all"],
                  sp["col"], sp["col"], sp["rows"](0), _ANY],
        out_specs=(sp["rows"](0), sp["col"]),
        input_output_aliases={7: 0},
        scratch_shapes=[pltpu.VMEM((tq, HEAD_DIM), F32), pltpu.VMEM((tq, 1), F32)],
        compiler_params=_params("parallel", "parallel"),
    )(proj, kv, kv, fk, lse, delta, datt, dproj)


def _fox_bwd_dkv(proj, kv, fk, lse, delta, datt):
    L = proj.shape[0]
    tq = min(FOX_BLOCK, L)
    nq = L // tq
    sp = _fox_specs(tq, L)

    def body(q_ref, k_ref, v_ref, fk_ref, lse_ref, dl_ref, do_ref,
             dk_ref, dv_ref, df_ref, dk_s, dv_s, df_s):
        ki = pl.program_id(1)
        k, v, fk = k_ref[...], v_ref[...], fk_ref[...]
        dk_s[...] = jnp.zeros_like(dk_s)
        dv_s[...] = jnp.zeros_like(dv_s)
        df_s[...] = jnp.zeros_like(df_s)

        def block(i, diagonal):
            r0 = pl.multiple_of(i * tq, tq)
            qs = (q_ref[pl.ds(r0, tq), :] * (HEAD_DIM ** -0.5)).astype(BF16)
            dob = do_ref[pl.ds(r0, tq), :].astype(BF16)
            p = jnp.exp(_fox_scores(qs, k, fk, diagonal) - lse_ref[i])
            dp = lax.dot_general(dob, v, _NT, preferred_element_type=F32)
            ds = p * (dp - dl_ref[i])
            dv_s[...] += lax.dot_general(p.astype(BF16), dob, _TN, preferred_element_type=F32)
            dk_s[...] += lax.dot_general(ds.astype(BF16), qs, _TN, preferred_element_type=F32)
            df_s[...] -= jnp.sum(ds, axis=0, keepdims=True)

        def above(i, carry):
            block(i, False)
            return carry

        block(ki, True)
        lax.fori_loop(ki + 1, nq, above, 0)
        dk_ref[...] = dk_s[...].astype(BF16)
        dv_ref[...] = dv_s[...].astype(BF16)
        df_ref[...] = df_s[...]

    return pl.pallas_call(
        body, name="fox_bwd_dkv",
        out_shape=(jax.ShapeDtypeStruct((L, MAIN_WIDTH), BF16),
                   jax.ShapeDtypeStruct((L, MAIN_WIDTH), BF16),
                   jax.ShapeDtypeStruct((FOX_HEADS, nq, 1, tq), F32)),
        grid=(FOX_HEADS, nq),
        in_specs=[sp["seq"](0), sp["rows"](0), sp["rows"](FOX_HEADS), sp["row"],
                  sp["col_all"], sp["col_all"], sp["seq"](0)],
        out_specs=(sp["rows"](0), sp["rows"](0), sp["row"]),
        scratch_shapes=[pltpu.VMEM((tq, HEAD_DIM), F32), pltpu.VMEM((tq, HEAD_DIM), F32),
                        pltpu.VMEM((1, tq), F32)],
        compiler_params=_params("parallel", "parallel"),
    )(proj, kv, kv, fk, lse, delta, datt)


def _pad_lanes(a):
    return jnp.pad(a, ((0, 0), (0, LANES - a.shape[1])))


def _mem_branch_fwd(memn, w_mk, proj, tag):
    kvm = _mm(memn, w_mk, name="mem_kv_" + tag)
    return kvm, _mem_attn_fwd(proj, kvm)


def _mem_branch_bwd(mem, g, w_mk, proj, memn, kvm, do_mem, dproj, tag):
    dproj, dkvm = _mem_attn_bwd(proj, kvm, do_mem, dproj)
    dkvm = dkvm.astype(BF16)
    dw_mk = _mm(memn, dkvm, ta=True, name="dw_mem_kv_" + tag, out_dtype=BF16)
    dmemn = _mm(dkvm, w_mk, tb=True, name="dmemn_" + tag)
    _, dg = _rmsnorm_bwd(mem, g, dmemn, name="mem_norm_bwd_" + tag, dx_dtype=BF16)
    return dproj, dw_mk, dg


def _local_step(x, mem, target, w, fetch=None, grads_ready=None):
    if grads_ready is None:
        grads_ready = lambda group, grads, token: token
    L = x.shape[0]
    g = {}
    w = dict(w)

    b_re_t = jnp.transpose(w["b_re"], (0, 2, 1))
    b_im_t = jnp.transpose(w["b_im"], (0, 2, 1))
    ar, ai, bbr_t, bbi_t = _s5_prep(w["lam_re"], w["lam_im"], w["log_step"], b_re_t, b_im_t)
    bmat, cmat = _s5_block_mats(bbr_t, bbi_t, w["c_re"], w["c_im"])
    a_rows = _s5_a_rows(ar, ai)

    hn0 = _rmsnorm_fwd(x, w["pre_norm_g"][0], name="pre_norm_0", out_dtype=BF16)
    memn0 = _rmsnorm_fwd(mem, w["mem_norm_g"][0], name="mem_norm_0", out_dtype=BF16)
    memn1 = _rmsnorm_fwd(mem, w["mem_norm_g"][1], name="mem_norm_1", out_dtype=BF16)
    if fetch is not None:
        w.update(fetch("a", [hn0, memn0, memn1, bmat, cmat, a_rows]))
    proj_a = _mm(hn0, w["w_in_a"], name="in_proj_a")
    y, yg, xp = _s5_fwd(proj_a, bmat, cmat, a_rows, w["d_skip"])
    if fetch is not None:
        w.update(fetch("b", yg))
    t = _mm(yg, w["w_glu"], name="glu_proj")
    kvm0, om0 = _mem_branch_fwd(memn0, w["w_mem_kv"][0], proj_a, "0")
    cat0 = _gate_a_fwd(y, t, w["b_glu"], proj_a, om0)
    o0 = _mm(cat0, w["w_out"][0], name="out_proj_0")
    h1 = _rmsnorm_fwd(o0, w["post_norm_g"][0], res=x, name="post_norm_0")

    kv_in = _rmsnorm_fwd(h1, w["kv_norm_g"], name="kv_norm", out_dtype=BF16)
    if fetch is not None:
        w.update(fetch("c", kv_in))
    kv = _mm(kv_in, w["w_kv"], name="kv_proj", out_dtype=BF16)
    pre_f = _mm(kv_in, w["w_fgate"], name="fgate_proj")
    b_f = jnp.pad(w["b_fgate"], (0, LANES - FOX_HEADS))
    fcum = _fgate_fwd(pre_f, b_f)
    fc = jnp.transpose(fcum[:, :FOX_HEADS])
    tq = min(FOX_BLOCK, L)
    fk = fc.reshape(FOX_HEADS, L // tq, 1, tq)

    hn1 = _rmsnorm_fwd(h1, w["pre_norm_g"][1], name="pre_norm_1", out_dtype=BF16)
    proj_b = _mm(hn1, w["w_in_b"], name="in_proj_b")
    att, lse = _fox_fwd(proj_b, kv, fk)
    kvm1, om1 = _mem_branch_fwd(memn1, w["w_mem_kv"][1], proj_b, "1")
    cat1 = _gate_b_fwd(att, proj_b, om1)
    o1 = _mm(cat1, w["w_out"][1], name="out_proj_1")
    dh2, loss_row = _final_norm_loss(o1, w["post_norm_g"][1], h1, target)

    do1, dpost1 = _rmsnorm_bwd(o1, w["post_norm_g"][1], dh2, name="post_norm_bwd_1", dx_dtype=BF16)
    dcat1 = _mm(do1, w["w_out"][1], tb=True, name="dcat_1", out_dtype=BF16)
    g["w_out_1"] = _mm(cat1, do1, ta=True, name="dw_out_1", out_dtype=BF16)
    datt, dproj_b, dom1, delta = _gate_b_bwd(dcat1, att, proj_b, om1)
    dproj_b, g["w_mem_kv_1"], dmemg1 = _mem_branch_bwd(mem, w["mem_norm_g"][1], w["w_mem_kv"][1], proj_b,
                                                      memn1, kvm1, dom1, dproj_b, "1")
    delta = delta.reshape(lse.shape)
    dproj_b, dfq = _fox_bwd_dq(proj_b, kv, fk, lse, delta, datt, dproj_b)
    dk, dv, dfk = _fox_bwd_dkv(proj_b, kv, fk, lse, delta, datt)
    g["w_in_b"] = _mm(hn1, dproj_b, ta=True, name="dw_in_b", out_dtype=BF16, shards=N_CHIPS)
    dhn1 = _mm(dproj_b, w["w_in_b"], tb=True, name="dhn_1")

    dkv = jnp.concatenate([dk, dv], axis=1)
    g["w_kv"] = _mm(kv_in, dkv, ta=True, name="dw_kv", out_dtype=BF16, shards=N_CHIPS)
    dkv_in_a = _mm(dkv, w["w_kv"], tb=True, name="dkv_in_kv")
    dfcum = _pad_lanes(jnp.transpose(dfq.reshape(FOX_HEADS, L) + dfk.reshape(FOX_HEADS, L)))
    dpre_f, db_f = _fgate_bwd(dfcum, pre_f, b_f)
    g["b_fgate"] = db_f[0, :FOX_HEADS]
    g["w_fgate"] = _mm(kv_in, dpre_f, ta=True, name="dw_fgate")[:, :FOX_HEADS]
    dkv_in_b = _mm(dpre_f, w["w_fgate"], tb=True, name="dkv_in_fgate")
    dh1, g["kv_norm_g"], dpre1 = _rmsnorm_bwd_pair(h1, w["kv_norm_g"], (dkv_in_a, dkv_in_b), w["pre_norm_g"][1],
                                                   dhn1, adds=(dh2,), name="kv_pre_norm_bwd")
    dh1 = grads_ready("b", g, dh1)

    do0, dpost0 = _rmsnorm_bwd(o0, w["post_norm_g"][0], dh1, name="post_norm_bwd_0", dx_dtype=BF16)
    dcat0 = _mm(do0, w["w_out"][0], tb=True, name="dcat_0", out_dtype=BF16)
    g["w_out_0"] = _mm(cat0, do0, ta=True, name="dw_out_0", out_dtype=BF16)
    dcat0 = grads_ready("b_send", g, dcat0)
    dproj_a, dt, dyg_a, dom0, db_glu = _gate_a_bwd(dcat0, y, t, w["b_glu"], proj_a, om0)
    g["b_glu"] = db_glu[0]
    g["w_glu"] = _mm(yg, dt, ta=True, name="dw_glu", out_dtype=BF16)
    dyg_b = _mm(dt, w["w_glu"], tb=True, name="dyg")
    dproj_a, g["w_mem_kv_0"], dmemg0 = _mem_branch_bwd(mem, w["mem_norm_g"][0], w["w_mem_kv"][0], proj_a,
                                                      memn0, kvm0, dom0, dproj_a, "0")
    dyg_b = grads_ready("a1", g, dyg_b)
    dproj_a, db_blk, dc_blk, da_rows, dd_skip = _s5_bwd(proj_a, dyg_a, dyg_b, y, xp, bmat, cmat, a_rows,
                                                        w["d_skip"], dproj_a)
    dproj_a = grads_ready("a1_send", g, dproj_a)
    g["d_skip"] = dd_skip[0]
    g["w_in_a"] = _mm(hn0, dproj_a, ta=True, name="dw_in_a", out_dtype=BF16, shards=N_CHIPS)
    dproj_a = grads_ready("a2", g, dproj_a)
    dhn0 = _mm(dproj_a, w["w_in_a"], tb=True, name="dhn_0")
    grad_x, dpre0 = _rmsnorm_bwd(x, w["pre_norm_g"][0], dhn0, adds=(dh1,), name="pre_norm_bwd_0")

    dbb = _s5_unfold(db_blk)
    dcc = _s5_unfold(dc_blk)
    g["c_re"], g["c_im"] = dcc[0], -dcc[1]
    d_ar = da_rows[:, 0, :STATE_COLS].reshape(SSM_GROUPS, SSM_STATE)
    d_ai = da_rows[:, 0, STATE_COLS:].reshape(SSM_GROUPS, SSM_STATE)
    dlr, dli, dls, dbr_t, dbi_t = _s5_prep_bwd(w["lam_re"], w["lam_im"], w["log_step"], b_re_t, b_im_t,
                                               d_ar, d_ai, dbb[0], dbb[1])
    g["lam_re"], g["lam_im"], g["log_step"] = dlr, dli, dls[:, 0]
    g["b_re"] = jnp.transpose(dbr_t, (0, 2, 1))
    g["b_im"] = jnp.transpose(dbi_t, (0, 2, 1))
    g["pre_norm_g"] = jnp.stack([dpre0, dpre1])
    g["post_norm_g"] = jnp.stack([dpost0, dpost1])
    g["mem_norm_g"] = jnp.stack([dmemg0, dmemg1])
    return loss_row, grad_x, g


_MESH = pl.DeviceIdType.MESH
_ANY = pl.BlockSpec(memory_space=pl.ANY)


def _place():
    x, y, c = lax.axis_index("x"), lax.axis_index("y"), lax.axis_index("c")
    chips = [(1 - x, y), (x, 1 - y), (1 - x, 1 - y)]
    return x, y, c, chips


_HBM = pl.BlockSpec(memory_space=pltpu.HBM)
_SEM = pl.BlockSpec(memory_space=pltpu.SEMAPHORE)
_SIDE = pltpu.SideEffectType.DATAFLOW_SIDE_EFFECTING


def _in_hbm(a):
    return pltpu.with_memory_space_constraint(a, pltpu.HBM)


def _hbm_like(a):
    return pltpu.HBM(a.shape, a.dtype)


def _ici_copies(srcs, lands, send_sem, recv_sem, src_at, dst_at, wait_at, to_sibling=False):
    x, y, c, chips = _place()
    peers = [(x, y, 1 - c)] if to_sibling else [(cx, cy, c) for cx, cy in chips]
    m = len(peers)
    start, wait = [], []
    for i in range(len(srcs)):
        for k, (px, py, pc) in enumerate(peers):
            sem = dict(send_sem=send_sem.at[m * i + k], recv_sem=recv_sem.at[m * i + k],
                       device_id=(px, py, pc), device_id_type=_MESH)
            src = src_at(srcs[i], 2 * px + py, c)
            start.append(pltpu.make_async_remote_copy(src_ref=src, dst_ref=dst_at(lands[i], 2 * x + y, k, c), **sem))
            wait.append(pltpu.make_async_remote_copy(src_ref=src, dst_ref=wait_at(lands[i], 2 * px + py, k, c), **sem))
    return start, wait


def _route_peers(route):
    return 1 if len(route) == 4 else 3


_BLOCK_ROUTE = (lambda s, j, c: s, lambda l, me, k, c: l.at[me, c], lambda l, j, k, c: l.at[j, c])


def _ici_start(srcs, lands, token, route, *, name):
    n = len(srcs)

    def body(*refs):
        start, _ = _ici_copies(refs[:n], refs[n:2 * n], refs[2 * n + 1], refs[2 * n + 2], *route)
        for cp in start:
            cp.start()

    sems = pltpu.SemaphoreType.DMA((_route_peers(route) * n,))
    outs = pl.pallas_call(
        body, name=name,
        out_shape=(sems, sems, *[_hbm_like(a) for a in srcs], *[_hbm_like(a) for a in lands], _hbm_like(token)),
        in_specs=[_HBM] * (2 * n + 1), out_specs=(_SEM, _SEM, *[_HBM] * (2 * n + 1)),
        input_output_aliases={i: 2 + i for i in range(2 * n + 1)},
        compiler_params=pltpu.CompilerParams(has_side_effects=_SIDE),
    )(*[_in_hbm(a) for a in srcs], *[_in_hbm(a) for a in lands], _in_hbm(token))
    return (outs[0], outs[1], list(outs[2:2 + n]), list(outs[2 + n:2 + 2 * n])), outs[2 + 2 * n]


def _ici_wait(handle, after, route, *, name):
    send_sem, recv_sem, srcs, lands = handle
    n = len(srcs)
    after = list(after) if isinstance(after, (list, tuple)) else [after]

    def body(*refs):
        _, wait = _ici_copies(refs[:n], refs[n:2 * n], refs[2 * n], refs[2 * n + 1], *route)
        for cp in wait:
            cp.wait_send()
            cp.wait_recv()

    outs = pl.pallas_call(
        body, name=name,
        out_shape=(*[_hbm_like(a) for a in srcs], *[_hbm_like(a) for a in lands]),
        in_specs=[_HBM] * (2 * n) + [_SEM, _SEM] + [_ANY] * len(after), out_specs=tuple([_HBM] * (2 * n)),
        input_output_aliases={i: i for i in range(2 * n)},
        compiler_params=pltpu.CompilerParams(has_side_effects=_SIDE),
    )(*srcs, *lands, send_sem, recv_sem, *after)
    return list(outs[:n]), list(outs[n:])


_GATHER_ROUTE = (lambda s, j, c: s.at[c], lambda l, me, k, c: l.at[me, c], lambda l, j, k, c: l.at[j, c])
_SCATTER_ROUTE = (lambda s, j, c: s.at[j], lambda l, me, k, c: l.at[k], lambda l, j, k, c: l.at[k])
_SHARE_ROUTE = (lambda s, j, c: s, lambda l, me, k, c: l.at[c], lambda l, j, k, c: l.at[1 - c], True)
_SWAP_ROUTE = (lambda s, j, c: s.at[:, 1 - c], lambda l, me, k, c: l, lambda l, j, k, c: l, True)


def _gather_forward(lands, tag, own=False):
    n = len(lands)
    m = 4 if own else 3

    def body(*refs):
        ins, outs = refs[:n], refs[n:2 * n]
        send_sem, recv_sem = refs[2 * n:]
        x, y, c, chips = _place()
        slots = [2 * cx + cy for cx, cy in chips] + [2 * x + y]

        def copy(i, k, half):
            return pltpu.make_async_remote_copy(
                src_ref=ins[i].at[slots[k], half], dst_ref=outs[i].at[slots[k], half],
                send_sem=send_sem.at[m * i + k], recv_sem=recv_sem.at[m * i + k],
                device_id=(x, y, 1 - c), device_id_type=_MESH)

        copies = [copy(i, k, c) for i in range(n) for k in range(m)]
        for cp in copies:
            cp.start()
        for i in range(n):
            for k in range(m):
                copy(i, k, 1 - c).wait_recv()
        for cp in copies:
            cp.wait_send()

    return pl.pallas_call(
        body, name="gather_forward_to_sibling_" + tag,
        out_shape=[jax.ShapeDtypeStruct(a.shape, a.dtype) for a in lands],
        in_specs=[_ANY] * n, out_specs=[_ANY] * n,
        input_output_aliases={i: i for i in range(n)},
        scratch_shapes=[pltpu.SemaphoreType.DMA((m * n,)), pltpu.SemaphoreType.DMA((m * n,))],
    )(*lands)


def _swap_halves(grads, tag):
    n = len(grads)

    def body(*refs):
        ins, outs = refs[:n], refs[n:2 * n]
        send_sem, recv_sem = refs[2 * n:]
        x, y, c, _ = _place()
        copies = [pltpu.make_async_remote_copy(
            src_ref=ins[i].at[:, 1 - c], dst_ref=outs[i],
            send_sem=send_sem.at[i], recv_sem=recv_sem.at[i],
            device_id=(x, y, 1 - c), device_id_type=_MESH) for i in range(n)]
        for cp in copies:
            cp.start()
        for cp in copies:
            cp.wait()

    return pl.pallas_call(
        body, name="grad_swap_halves_" + tag,
        out_shape=[jax.ShapeDtypeStruct((N_CHIPS,) + g.shape[2:], g.dtype) for g in grads],
        in_specs=[_ANY] * n, out_specs=[_ANY] * n,
        scratch_shapes=[pltpu.SemaphoreType.DMA((n,)), pltpu.SemaphoreType.DMA((n,))],
    )(*grads)


def _sum_rows(h, C):
    return max(d for d in range(SUBLANES, h + 1, SUBLANES) if h % d == 0 and d * C <= 1 << 20)


SUM_STEPS = 4


def _pair_sums(gs, rs, c_idx, *, name):
    n = len(gs)
    rows = [g.shape[2] // SUM_STEPS for g in gs]

    def body(c_ref, *refs):
        for g_ref, r_ref, o_ref in zip(refs[:n], refs[n:2 * n], refs[2 * n:]):
            o_ref[...] = (g_ref[...].astype(F32) + r_ref[...].astype(F32)).astype(o_ref.dtype)

    return pl.pallas_call(
        body, name=name,
        out_shape=[jax.ShapeDtypeStruct((N_CHIPS,) + g.shape[2:], g.dtype) for g in gs],
        grid_spec=pltpu.PrefetchScalarGridSpec(
            num_scalar_prefetch=1, grid=(N_CHIPS, SUM_STEPS),
            in_specs=[pl.BlockSpec((None, None, tr, g.shape[3]), lambda j, i, s: (j, s[0], i, 0))
                      for g, tr in zip(gs, rows)]
            + [pl.BlockSpec((None, tr, g.shape[3]), lambda j, i, s: (j, i, 0)) for g, tr in zip(gs, rows)],
            out_specs=[pl.BlockSpec((None, tr, g.shape[3]), lambda j, i, s: (j, i, 0)) for g, tr in zip(gs, rows)]),
        compiler_params=_params("parallel", "parallel"),
    )(c_idx, *gs, *rs)


def _owner_sums(ss, rs, jc_idx, *, name):
    n = len(ss)
    rows = [s.shape[1] // SUM_STEPS for s in ss]

    def body(jc_ref, *refs):
        for s_ref, r_ref, m_ref, o_ref in zip(refs[:n], refs[n:2 * n], refs[2 * n:3 * n], refs[3 * n:]):
            acc = s_ref[...].astype(F32)
            for k in range(3):
                acc = acc + r_ref[k].astype(F32)
            m_ref[...] = acc
            o_ref[...] = acc

    outs = pl.pallas_call(
        body, name=name,
        out_shape=[jax.ShapeDtypeStruct(s.shape[1:], F32) for s in ss]
        + [jax.ShapeDtypeStruct((2,) + s.shape[1:], F32) for s in ss],
        grid_spec=pltpu.PrefetchScalarGridSpec(
            num_scalar_prefetch=1, grid=(SUM_STEPS,),
            in_specs=[pl.BlockSpec((None, tr, s.shape[2]), lambda i, p: (p[0], i, 0)) for s, tr in zip(ss, rows)]
            + [pl.BlockSpec((3, tr, s.shape[2]), lambda i, p: (0, i, 0)) for s, tr in zip(ss, rows)],
            out_specs=[pl.BlockSpec((tr, s.shape[2]), lambda i, p: (i, 0)) for s, tr in zip(ss, rows)]
            + [pl.BlockSpec((None, tr, s.shape[2]), lambda i, p: (p[1], i, 0)) for s, tr in zip(ss, rows)]),
        compiler_params=_params("parallel"),
    )(jc_idx, *ss, *rs)
    return outs[:n], outs[n:]


def _chip_sums(grads, c_idx, tag):
    views = [g.reshape(N_CHIPS, 2, g.shape[1] // 2, g.shape[2]) for g in grads]
    arrived = _swap_halves(views, tag)
    return _pair_sums(views, arrived, c_idx, name=f"grad_pair_sums_{tag}")


def _sum_devices(blocks):
    R = blocks.shape[2]
    tr = _sum_rows(R, 2 * N_CHIPS * LANES)

    def body(b_ref, o_ref):
        acc = b_ref[0, 0]
        for d in range(1, 2 * N_CHIPS):
            acc = acc + b_ref[d // 2, d % 2]
        o_ref[...] = acc

    return pl.pallas_call(
        body, name="sum_small_over_devices", out_shape=jax.ShapeDtypeStruct((R, LANES), F32),
        grid=(R // tr,),
        in_specs=[pl.BlockSpec((N_CHIPS, 2, tr, LANES), lambda i: (0, 0, i, 0))],
        out_specs=pl.BlockSpec((tr, LANES), lambda i: (i, 0)),
        compiler_params=_params("parallel"),
    )(blocks)


def _adamw(w, g, m, v, *, name):
    R, C = w.shape
    tr = max(d for d in range(SUBLANES, R + 1, SUBLANES)
             if R % d == 0 and 7 * 2 * d * C * 4 <= VMEM_LIMIT_BYTES // 2)

    def body(w_ref, g_ref, m_ref, v_ref, d_ref, nm_ref, nv_ref):
        g = g_ref[...]
        m = ADAM_B1 * m_ref[...] + (1.0 - ADAM_B1) * g
        v = ADAM_B2 * v_ref[...] + (1.0 - ADAM_B2) * (g * g)
        nm_ref[...] = m
        nv_ref[...] = v
        m_hat = m / (1.0 - ADAM_B1 ** ADAM_STEP)
        v_hat = v / (1.0 - ADAM_B2 ** ADAM_STEP)
        d_ref[...] = -ADAM_LR * (m_hat / (jnp.sqrt(v_hat) + ADAM_EPS) + ADAM_WD * w_ref[...])

    blk = pl.BlockSpec((tr, C), lambda i: (i, 0))
    sds = jax.ShapeDtypeStruct((R, C), F32)
    return pl.pallas_call(
        body, name=name, out_shape=(sds, sds, sds), grid=(R // tr,),
        in_specs=[blk] * 4, out_specs=(blk, blk, blk),
        compiler_params=_params("parallel"),
    )(w, g, m, v)


_TILE = SUBLANES * LANES


def _pack(arrays):
    rows = []
    for a in arrays:
        flat = a.reshape(-1)
        flat = jnp.pad(flat, (0, (-flat.shape[0]) % _TILE))
        rows.append(flat.reshape(-1, LANES))
    return jnp.concatenate(rows, axis=0)


def _unpack(buf, shapes):
    out, r = [], 0
    for s in shapes:
        size = math.prod(s)
        nr = -(-size // _TILE) * SUBLANES
        out.append(buf[r:r + nr].reshape(-1)[:size].reshape(s))
        r += nr
    return out


_BIG = ("w_in_a", "w_glu", "w_kv", "w_in_b", "w_mem_kv", "w_out")
_REPLICATED = ("pre_norm_g", "post_norm_g", "lam_re", "lam_im", "log_step", "b_re", "b_im", "c_re", "c_im",
               "kv_norm_g", "b_fgate", "mem_norm_g")
_SHARDED_SMALL = ("d_skip", "b_glu", "w_fgate")
_WEIGHTS = ("pre_norm_g", "post_norm_g", "w_in_a", "lam_re", "lam_im", "log_step", "b_re", "b_im", "c_re",
            "c_im", "d_skip", "w_glu", "b_glu", "kv_norm_g", "w_kv", "w_fgate", "b_fgate", "w_in_b",
            "mem_norm_g", "w_mem_kv", "w_out")


def _halves(a):
    return a.reshape(2, a.shape[0] // 2, a.shape[1])


def _unhalve(a):
    return a.reshape(N_CHIPS, 2 * a.shape[2], a.shape[3])


def _columns(a):
    return jnp.transpose(a, (1, 0, 2)).reshape(a.shape[1], N_CHIPS * a.shape[2])


def kernel(x, mem, pre_norm_g, post_norm_g, w_in_a, lam_re, lam_im, log_step, b_re, b_im, c_re, c_im, d_skip, w_glu, b_glu, kv_norm_g, w_kv, w_fgate, b_fgate, w_in_b, mem_norm_g, w_mem_kv, w_out, loss_target, m_pre_norm_g, m_post_norm_g, m_w_in_a, m_lam_re, m_lam_im, m_log_step, m_b_re, m_b_im, m_c_re, m_c_im, m_d_skip, m_w_glu, m_b_glu, m_kv_norm_g, m_w_kv, m_w_fgate, m_b_fgate, m_w_in_b, m_mem_norm_g, m_w_mem_kv, m_w_out, v_pre_norm_g, v_post_norm_g, v_w_in_a, v_lam_re, v_lam_im, v_log_step, v_b_re, v_b_im, v_c_re, v_c_im, v_d_skip, v_w_glu, v_b_glu, v_kv_norm_g, v_w_kv, v_w_fgate, v_b_fgate, v_w_in_b, v_mem_norm_g, v_w_mem_kv, v_w_out):
    a = dict(locals())
    xi, yi, ci = lax.axis_index("x"), lax.axis_index("y"), lax.axis_index("c")
    chip = 2 * xi + yi
    c_idx = jnp.reshape(ci, (1,)).astype(jnp.int32)
    jc_idx = jnp.stack([chip, ci]).astype(jnp.int32)

    vec = jnp.zeros((2 * SUBLANES, MAIN_WIDTH // N_CHIPS), F32)
    vec = vec.at[0].set(a["d_skip"][0]).at[1].set(a["b_glu"][0])
    def own_slot(gathered, parts):
        return [lax.dynamic_update_index_in_dim(g, p, chip, 0) for g, p in zip(gathered, parts)]

    parts_a = [_halves(a["w_in_a"][0].astype(BF16)), _halves(vec)]
    parts_b = [_halves(a["w_glu"][0].astype(BF16)),
               *[_halves(a["w_mem_kv"][i].astype(BF16)) for i in range(2)],
               *[_halves(a["w_out"][i].astype(BF16)) for i in range(2)]]
    parts_c = [_halves(a["w_kv"].astype(BF16)), _halves(_pad_lanes(a["w_fgate"]).astype(BF16)),
               _halves(a["w_in_b"][0].astype(BF16))]
    travelling, token = {}, a["pre_norm_g"]
    for tag, parts in (("a", parts_a), ("b", parts_b), ("c", parts_c)):
        lands = [lax.empty((N_CHIPS,) + p.shape, p.dtype) for p in parts]
        travelling[tag], token = _ici_start(parts, lands, token, _GATHER_ROUTE, name=f"gather_{tag}_start")

    def fetch(tag, after):
        parts, lands = _ici_wait(travelling[tag], after, _GATHER_ROUTE, name=f"gather_{tag}_wait")
        full = own_slot(_gather_forward(lands, tag), parts)
        if tag == "a":
            w_in_a, vecs = full
            return dict(w_in_a=_columns(_unhalve(w_in_a)), d_skip=vecs[:, 0, 0, :].reshape(MAIN_WIDTH),
                        b_glu=vecs[:, 0, 1, :].reshape(MAIN_WIDTH))
        if tag == "b":
            w_glu, w_mk0, w_mk1, w_out0, w_out1 = full
            return dict(w_glu=w_glu.reshape(MAIN_WIDTH, MAIN_WIDTH),
                        w_mem_kv=[m.reshape(D_MODEL, 2 * MEM_WIDTH) for m in (w_mk0, w_mk1)],
                        w_out=[o.reshape(D_MODEL, D_MODEL) for o in (w_out0, w_out1)])
        w_kv, w_fg, w_in_b = full
        return dict(w_kv=_columns(_unhalve(w_kv)), w_fgate=w_fg.reshape(D_MODEL, LANES),
                    w_in_b=_columns(_unhalve(w_in_b)))

    w = dict(
        pre_norm_g=token, post_norm_g=a["post_norm_g"], mem_norm_g=a["mem_norm_g"],
        kv_norm_g=a["kv_norm_g"], b_fgate=a["b_fgate"],
        lam_re=a["lam_re"][0], lam_im=a["lam_im"][0], log_step=a["log_step"][0],
        b_re=a["b_re"][0], b_im=a["b_im"][0], c_re=a["c_re"][0], c_im=a["c_im"][0])

    sent = {}

    swapping = {}

    def grads_ready(event, g, token):
        tag = event.split("_")[0]
        if event in ("b", "a1"):
            big = {"b": lambda: [g["w_kv"], g["w_in_b"], g["w_mem_kv_1"].reshape(N_CHIPS, -1, 2 * MEM_WIDTH),
                                 g["w_out_1"].reshape(N_CHIPS, -1, D_MODEL)],
                   "a1": lambda: [g["w_glu"].reshape(N_CHIPS, -1, MAIN_WIDTH),
                                  g["w_mem_kv_0"].reshape(N_CHIPS, -1, 2 * MEM_WIDTH),
                                  g["w_out_0"].reshape(N_CHIPS, -1, D_MODEL)]}[tag]()
            views = [b.reshape(N_CHIPS, 2, b.shape[1] // 2, b.shape[2]) for b in big]
            lands = [lax.empty((N_CHIPS,) + v.shape[2:], v.dtype) for v in views]
            swapping[tag], token = _ici_start(views, lands, token, _SWAP_ROUTE, name=f"grad_swap_{tag}_start")
            return token
        if event == "a2":
            sums = _chip_sums([g["w_in_a"]], c_idx, tag)
        else:
            views, arrived = _ici_wait(swapping[tag], token, _SWAP_ROUTE, name=f"grad_swap_{tag}_wait")
            sums = _pair_sums(views, arrived, c_idx, name=f"grad_pair_sums_{tag}")
        lands = [lax.empty((3,) + s.shape[1:], s.dtype) for s in sums]
        sent[tag], token = _ici_start(sums, lands, token, _SCATTER_ROUTE, name=f"grad_send_{tag}_start")
        return token

    loss_row, grad_x, g = _local_step(a["x"][0], a["mem"][0], a["loss_target"][0], w, fetch, grads_ready)

    small_names = _REPLICATED + _SHARDED_SMALL
    pack = _pack([g[n] for n in small_names])
    blocks = lax.empty((N_CHIPS, 2) + pack.shape, F32)
    small_sent, token = _ici_start([pack], [blocks], loss_row, _BLOCK_ROUTE, name="small_sums_start")

    sharing = {}
    for tag in ("b", "a1", "a2"):
        sums, arrived = _ici_wait(sent[tag], [grad_x, token], _SCATTER_ROUTE, name=f"grad_send_{tag}_wait")
        mine, bufs = _owner_sums(sums, arrived, jc_idx, name=f"grad_owner_sums_{tag}")
        sharing[tag], token = _ici_start(mine, bufs, token, _SHARE_ROUTE, name=f"grad_share_{tag}_start")
    loss = lax.psum(jnp.sum(token), MESH_AXES)

    def shared(tag, after):
        _, bufs = _ici_wait(sharing[tag], after, _SHARE_ROUTE, name=f"grad_share_{tag}_wait")
        return [b.reshape(-1, b.shape[2]) for b in bufs]

    grads, delta, new_m, new_v = {}, {}, {}, {}

    def adam(n):
        shape = a[n].shape
        d2 = (-1, shape[-1])
        d, m, v = _adamw(a[n].reshape(d2), grads[n].reshape(d2), a["m_" + n].reshape(d2),
                         a["v_" + n].reshape(d2), name="adamw_" + n)
        delta[n], new_m[n], new_v[n] = d.reshape(shape), m.reshape(shape), v.reshape(shape)
        return d

    r_kv, r_in_b, r_mk1, r_out1 = shared("b", token)
    grads["w_kv"], grads["w_in_b"] = r_kv, r_in_b[None]
    done = [adam("w_kv"), adam("w_in_b")]
    r_glu, r_mk0, r_out0 = shared("a1", done)
    grads["w_glu"], grads["w_mem_kv"], grads["w_out"] = r_glu[None], jnp.stack([r_mk0, r_mk1]), jnp.stack([r_out0, r_out1])
    done = [adam("w_glu"), adam("w_mem_kv"), adam("w_out")]
    (r_in_a,) = shared("a2", done)
    grads["w_in_a"] = r_in_a[None]
    adam("w_in_a")

    (pack,), (blocks,) = _ici_wait(small_sent, [delta[n] for n in _BIG], _BLOCK_ROUTE, name="small_sums_wait")
    blocks = lax.dynamic_update_slice(blocks, pack[None, None], (chip, ci, 0, 0))
    (blocks,) = _gather_forward([blocks], "small", own=True)
    small = dict(zip(small_names, _unpack(_sum_devices(blocks), [g[n].shape for n in small_names])))
    for n in _REPLICATED:
        grads[n] = small[n].reshape(a[n].shape)
    nd = MAIN_WIDTH // N_CHIPS
    grads["d_skip"] = lax.dynamic_slice(small["d_skip"], (chip * nd,), (nd,))[None]
    grads["b_glu"] = lax.dynamic_slice(small["b_glu"], (chip * nd,), (nd,))[None]
    nf = D_MODEL // N_CHIPS
    grads["w_fgate"] = lax.dynamic_slice(small["w_fgate"], (chip * nf, 0), (nf, FOX_HEADS))

    shapes = [a[n].shape for n in small_names]
    d, m, v = _adamw(_pack([a[n] for n in small_names]), _pack([grads[n] for n in small_names]),
                     _pack([a["m_" + n] for n in small_names]), _pack([a["v_" + n] for n in small_names]),
                     name="adamw_small")
    for n, dd, mm, vv in zip(small_names, _unpack(d, shapes), _unpack(m, shapes), _unpack(v, shapes)):
        delta[n], new_m[n], new_v[n] = dd, mm, vv

    return (loss, grad_x[None], *[grads[n] for n in _WEIGHTS], *[delta[n] for n in _WEIGHTS],
            *[new_m[n] for n in _WEIGHTS], *[new_v[n] for n in _WEIGHTS])
```

```python
import math

import jax
import jax.numpy as jnp
from jax import lax
from jax.experimental import pallas as pl
from jax.experimental.pallas import tpu as pltpu

F32 = jnp.float32
BF16 = jnp.bfloat16

D_MODEL = 2048
N_MEM = 256
MAIN_WIDTH = 1536
MEM_WIDTH = 512
IN_WIDTH = 2 * MAIN_WIDTH + 2 * MEM_WIDTH
HEAD_DIM = 128
FOX_HEADS = MAIN_WIDTH // HEAD_DIM
MEM_HEADS = MEM_WIDTH // HEAD_DIM
SSM_GROUP = 16
SSM_GROUPS = MAIN_WIDTH // SSM_GROUP
SSM_STATE = 64
GROUPS_PER_BLOCK = 8
SSM_BLOCKS = SSM_GROUPS // GROUPS_PER_BLOCK
STATE_COLS = GROUPS_PER_BLOCK * SSM_STATE
EPS = 1e-6
ADAM_LR = 0.001
ADAM_B1 = 0.9
ADAM_B2 = 0.999
ADAM_EPS = 1e-08
ADAM_WD = 0.01
ADAM_STEP = 10
N_CHIPS = 4
LANES = 128
SUBLANES = 8
VMEM_LIMIT_BYTES = 56 * 1024 * 1024
NEG_BIG = -1e30
MESH_AXES = ("x", "y", "c")


def _params(*sem):
    return pltpu.CompilerParams(dimension_semantics=sem if sem else None,
                                vmem_limit_bytes=VMEM_LIMIT_BYTES)


def _sigmoid(x):
    return 1.0 / (1.0 + jnp.exp(-x))


def _gelu(x):
    c = math.sqrt(2.0 / math.pi)
    return 0.5 * x * (1.0 + jnp.tanh(c * (x + 0.044715 * (x * x * x))))


def _gelu_grad(x):
    c = math.sqrt(2.0 / math.pi)
    t = jnp.tanh(c * (x + 0.044715 * (x * x * x)))
    return 0.5 * (1.0 + t) + 0.5 * x * (1.0 - t * t) * (c * (1.0 + 3.0 * 0.044715 * (x * x)))


def _silu_and_grad(z):
    s = _sigmoid(z)
    return z * s, s * (1.0 + z * (1.0 - s))


_TILE_CHOICES = (4096, 3072, 2048, 1536, 1024, 768, 512, 384, 256, LANES)


def _tile(n, cap):
    return next(c for c in _TILE_CHOICES if c <= cap and n % c == 0)


def _mm(a, b, *, name, ta=False, tb=False, out_dtype=F32, shards=1, tm=1024, tn=1024, tk=4096):
    if ta:
        K, M = a.shape
    else:
        M, K = a.shape
    if tb:
        N, kb = b.shape
    else:
        kb, N = b.shape
    assert K == kb, (a.shape, b.shape)
    ns = N // shards
    tm, tn, tk = _tile(M, tm), _tile(ns, tn), _tile(K, tk)
    assert M % tm == 0 and ns % tn == 0 and K % tk == 0 and N % shards == 0
    nk = K // tk
    dn = (((0 if ta else 1,), (1 if tb else 0,)), ((), ()))

    def body(a_ref, b_ref, o_ref, *acc):
        prod = lax.dot_general(a_ref[...].astype(BF16), b_ref[...].astype(BF16), dn, preferred_element_type=F32)
        if nk == 1:
            o_ref[...] = prod.astype(o_ref.dtype)
            return
        acc_ref, = acc
        k = pl.program_id(2)

        @pl.when(k == 0)
        def _():
            acc_ref[...] = jnp.zeros_like(acc_ref)

        acc_ref[...] += prod

        @pl.when(k == nk - 1)
        def _():
            o_ref[...] = acc_ref[...].astype(o_ref.dtype)

    a_spec = (pl.BlockSpec((tk, tm), lambda i, j, k: (k, i)) if ta
              else pl.BlockSpec((tm, tk), lambda i, j, k: (i, k)))
    b_spec = (pl.BlockSpec((tn, tk), lambda i, j, k: (j, k)) if tb
              else pl.BlockSpec((tk, tn), lambda i, j, k: (k, j)))
    if shards == 1:
        out_shape = jax.ShapeDtypeStruct((M, N), out_dtype)
        o_spec = pl.BlockSpec((tm, tn), lambda i, j, k: (i, j))
    else:
        nb = ns // tn
        out_shape = jax.ShapeDtypeStruct((shards, M, ns), out_dtype)
        o_spec = pl.BlockSpec((None, tm, tn), lambda i, j, k: (j // nb, i, j % nb))
    return pl.pallas_call(
        body, name=name, out_shape=out_shape,
        grid=(M // tm, N // tn, nk),
        in_specs=[a_spec, b_spec], out_specs=o_spec,
        scratch_shapes=[] if nk == 1 else [pltpu.VMEM((tm, tn), F32)],
        compiler_params=_params("parallel", "parallel", "arbitrary"),
    )(a, b)


def _rmsnorm_fwd(x, g, *, name, res=None, out_dtype=F32, tr=256):
    L, D = x.shape
    tr = min(tr, L)
    has_res = res is not None

    def body(*refs):
        if has_res:
            x_ref, g_ref, r_ref, o_ref = refs
        else:
            x_ref, g_ref, o_ref = refs
        xf = x_ref[...]
        r = lax.rsqrt(jnp.mean(xf * xf, axis=-1, keepdims=True) + EPS)
        y = xf * r * g_ref[...]
        if has_res:
            y = r_ref[...] + y
        o_ref[...] = y.astype(o_ref.dtype)

    row = pl.BlockSpec((tr, D), lambda i: (i, 0))
    vec = pl.BlockSpec((1, D), lambda i: (0, 0))
    ins = [x, g.reshape(1, D)] + ([res] if has_res else [])
    return pl.pallas_call(
        body, name=name, out_shape=jax.ShapeDtypeStruct((L, D), out_dtype),
        grid=(L // tr,), in_specs=[row, vec] + ([row] if has_res else []), out_specs=row,
        compiler_params=_params("parallel"),
    )(*ins)


def _rmsnorm_bwd(x, g, dy, *, name, adds=(), dx_dtype=F32, tr=256):
    L, D = x.shape
    tr = min(tr, L)
    dys = dy if isinstance(dy, tuple) else (dy,)
    n_dy, n_add = len(dys), len(adds)

    def body(*refs):
        x_ref, g_ref = refs[:2]
        dy_refs = refs[2:2 + n_dy]
        add_refs = refs[2 + n_dy:2 + n_dy + n_add]
        dx_ref, dg_ref = refs[2 + n_dy + n_add:]
        xf = x_ref[...]
        dyf = dy_refs[0][...].astype(F32)
        for d_ref in dy_refs[1:]:
            dyf = dyf + d_ref[...].astype(F32)
        r = lax.rsqrt(jnp.mean(xf * xf, axis=-1, keepdims=True) + EPS)
        gy = dyf * g_ref[...]
        c = jnp.mean(xf * gy, axis=-1, keepdims=True) * (r * r * r)
        dx = gy * r - xf * c
        for a_ref in add_refs:
            dx = dx + a_ref[...].astype(F32)
        dx_ref[...] = dx.astype(dx_ref.dtype)

        @pl.when(pl.program_id(0) == 0)
        def _():
            dg_ref[...] = jnp.zeros_like(dg_ref)

        dg_ref[...] += jnp.sum(dyf * xf * r, axis=0, keepdims=True)

    row = pl.BlockSpec((tr, D), lambda i: (i, 0))
    vec = pl.BlockSpec((1, D), lambda i: (0, 0))
    dx, dg = pl.pallas_call(
        body, name=name,
        out_shape=(jax.ShapeDtypeStruct((L, D), dx_dtype), jax.ShapeDtypeStruct((1, D), F32)),
        grid=(L // tr,), in_specs=[row, vec] + [row] * (n_dy + n_add), out_specs=(row, vec),
        compiler_params=_params("arbitrary"),
    )(x, g.reshape(1, D), *dys, *adds)
    return dx, dg.reshape(D)


def _rmsnorm_bwd_pair(x, g1, dy1, g2, dy2, *, name, adds=(), tr=256):
    L, D = x.shape
    tr = min(tr, L)
    dy1s = dy1 if isinstance(dy1, tuple) else (dy1,)
    n1, n_add = len(dy1s), len(adds)

    def body(*refs):
        x_ref, g1_ref, g2_ref = refs[:3]
        dy1_refs = refs[3:3 + n1]
        dy2_ref = refs[3 + n1]
        add_refs = refs[4 + n1:4 + n1 + n_add]
        dx_ref, dg1_ref, dg2_ref = refs[4 + n1 + n_add:]
        xf = x_ref[...]
        d1 = dy1_refs[0][...].astype(F32)
        for d_ref in dy1_refs[1:]:
            d1 = d1 + d_ref[...].astype(F32)
        d2 = dy2_ref[...].astype(F32)
        r = lax.rsqrt(jnp.mean(xf * xf, axis=-1, keepdims=True) + EPS)
        gy = d1 * g1_ref[...] + d2 * g2_ref[...]
        c = jnp.mean(xf * gy, axis=-1, keepdims=True) * (r * r * r)
        dx = gy * r - xf * c
        for a_ref in add_refs:
            dx = dx + a_ref[...].astype(F32)
        dx_ref[...] = dx

        @pl.when(pl.program_id(0) == 0)
        def _():
            dg1_ref[...] = jnp.zeros_like(dg1_ref)
            dg2_ref[...] = jnp.zeros_like(dg2_ref)

        xr = xf * r
        dg1_ref[...] += jnp.sum(d1 * xr, axis=0, keepdims=True)
        dg2_ref[...] += jnp.sum(d2 * xr, axis=0, keepdims=True)

    row = pl.BlockSpec((tr, D), lambda i: (i, 0))
    vec = pl.BlockSpec((1, D), lambda i: (0, 0))
    dx, dg1, dg2 = pl.pallas_call(
        body, name=name,
        out_shape=(jax.ShapeDtypeStruct((L, D), F32), jax.ShapeDtypeStruct((1, D), F32),
                   jax.ShapeDtypeStruct((1, D), F32)),
        grid=(L // tr,), in_specs=[row, vec, vec] + [row] * (n1 + 1 + n_add), out_specs=(row, vec, vec),
        compiler_params=_params("arbitrary"),
    )(x, g1.reshape(1, D), g2.reshape(1, D), *dy1s, dy2, *adds)
    return dx, dg1.reshape(D), dg2.reshape(D)


def _final_norm_loss(o, g, res, target, *, tr=256):
    L, D = o.shape
    tr = min(tr, L)

    def body(o_ref, g_ref, r_ref, t_ref, dh_ref, loss_ref):
        xf = o_ref[...]
        r = lax.rsqrt(jnp.mean(xf * xf, axis=-1, keepdims=True) + EPS)
        e = (r_ref[...] + xf * r * g_ref[...]) - t_ref[...]
        dh_ref[...] = e * (1.0 / D)

        @pl.when(pl.program_id(0) == 0)
        def _():
            loss_ref[...] = jnp.zeros_like(loss_ref)

        loss_ref[...] += jnp.sum(e * e, axis=0, keepdims=True) * (0.5 / D)

    row = pl.BlockSpec((tr, D), lambda i: (i, 0))
    vec = pl.BlockSpec((1, D), lambda i: (0, 0))
    dh, lp = pl.pallas_call(
        body, name="post_norm_1_loss",
        out_shape=(jax.ShapeDtypeStruct((L, D), F32), jax.ShapeDtypeStruct((1, D), F32)),
        grid=(L // tr,), in_specs=[row, vec, row, row], out_specs=(row, vec),
        compiler_params=_params("arbitrary"),
    )(o, g.reshape(1, D), res, target)
    return dh, lp


def _s5_coeffs(lr, li, ls):
    dt = jnp.exp(ls)
    mag = jnp.exp(lr * dt)
    ar = mag * jnp.cos(li * dt)
    ai = mag * jnp.sin(li * dt)
    den = lr * lr + li * li
    cr = ((ar - 1.0) * lr + ai * li) / den
    ci = (ai * lr - (ar - 1.0) * li) / den
    return dt, ar, ai, den, cr, ci


def _s5_prep(lam_re, lam_im, log_step, b_re_t, b_im_t):
    G, P = lam_re.shape
    H = b_re_t.shape[1]

    def body(lr_ref, li_ref, ls_ref, br_ref, bi_ref, ar_ref, ai_ref, bbr_ref, bbi_ref):
        _, ar, ai, _, cr, ci = _s5_coeffs(lr_ref[...], li_ref[...], ls_ref[...])
        ar_ref[...] = ar
        ai_ref[...] = ai
        br, bi = br_ref[...], bi_ref[...]
        crb, cib = cr[:, None, :], ci[:, None, :]
        bbr_ref[...] = crb * br - cib * bi
        bbi_ref[...] = crb * bi + cib * br

    return pl.pallas_call(
        body, name="s5_prep",
        out_shape=(jax.ShapeDtypeStruct((G, P), F32), jax.ShapeDtypeStruct((G, P), F32),
                   jax.ShapeDtypeStruct((G, H, P), F32), jax.ShapeDtypeStruct((G, H, P), F32)),
        compiler_params=_params(),
    )(lam_re, lam_im, log_step.reshape(G, 1), b_re_t, b_im_t)


def _s5_prep_bwd(lam_re, lam_im, log_step, b_re_t, b_im_t, d_ar, d_ai, d_bbr, d_bbi):
    G, P = lam_re.shape
    H = b_re_t.shape[1]

    def body(lr_ref, li_ref, ls_ref, br_ref, bi_ref, dar_ref, dai_ref, dbbr_ref, dbbi_ref,
             dlr_ref, dli_ref, dls_ref, dbr_ref, dbi_ref):
        lr, li = lr_ref[...], li_ref[...]
        dt, ar, ai, den, cr, ci = _s5_coeffs(lr, li, ls_ref[...])
        br, bi = br_ref[...], bi_ref[...]
        gbr, gbi = dbbr_ref[...], dbbi_ref[...]
        crb, cib = cr[:, None, :], ci[:, None, :]
        dbr_ref[...] = crb * gbr + cib * gbi
        dbi_ref[...] = crb * gbi - cib * gbr
        gcr = jnp.sum(br * gbr + bi * gbi, axis=1)
        gci = jnp.sum(br * gbi - bi * gbr, axis=1)
        ilr, ili = lr / den, -li / den
        gar = dar_ref[...] + (ilr * gcr + ili * gci)
        gai = dai_ref[...] + (ilr * gci - ili * gcr)
        qr, qi = cr * ilr - ci * ili, cr * ili + ci * ilr
        glr = -(qr * gcr + qi * gci)
        gli = -(qr * gci - qi * gcr)
        glr = glr + dt * (ar * gar + ai * gai)
        gli = gli + dt * (ar * gai - ai * gar)
        wr, wi = lr * ar - li * ai, lr * ai + li * ar
        gdt = jnp.sum(wr * gar + wi * gai, axis=1, keepdims=True)
        dlr_ref[...] = glr
        dli_ref[...] = gli
        dls_ref[...] = gdt * dt

    return pl.pallas_call(
        body, name="s5_prep_bwd",
        out_shape=(jax.ShapeDtypeStruct((G, P), F32), jax.ShapeDtypeStruct((G, P), F32),
                   jax.ShapeDtypeStruct((G, 1), F32),
                   jax.ShapeDtypeStruct((G, H, P), F32), jax.ShapeDtypeStruct((G, H, P), F32)),
        compiler_params=_params(),
    )(lam_re, lam_im, log_step.reshape(G, 1), b_re_t, b_im_t, d_ar, d_ai, d_bbr, d_bbi)


def _s5_block_mats(bbr_t, bbi_t, c_re, c_im):
    bmat = _s5_expand(bbr_t, bbi_t)
    cmat = jnp.transpose(_s5_expand(c_re, -c_im), (0, 2, 1))
    return bmat.astype(BF16), cmat.astype(BF16)


def _s5_diag_mask():
    r = lax.broadcasted_iota(jnp.int32, (LANES, 2 * STATE_COLS), 0) // SSM_GROUP
    c = (lax.broadcasted_iota(jnp.int32, (LANES, 2 * STATE_COLS), 1) % STATE_COLS) // SSM_STATE
    return (r == c).astype(F32)


def _s5_expand(re, im):
    re = jnp.tile(re.reshape(SSM_BLOCKS, LANES, SSM_STATE), (1, 1, GROUPS_PER_BLOCK))
    im = jnp.tile(im.reshape(SSM_BLOCKS, LANES, SSM_STATE), (1, 1, GROUPS_PER_BLOCK))
    return jnp.concatenate([re, im], axis=-1) * _s5_diag_mask()[None]


def _s5_unfold(dmat):
    d = dmat.reshape(SSM_GROUPS, SSM_GROUP, 2, SSM_STATE)
    return jnp.transpose(d, (2, 0, 1, 3))


def _s5_a_rows(ar, ai):
    a = jnp.concatenate([ar.reshape(SSM_BLOCKS, STATE_COLS), ai.reshape(SSM_BLOCKS, STATE_COLS)], axis=1)
    return jnp.broadcast_to(a[:, None, :], (SSM_BLOCKS, SUBLANES, 2 * STATE_COLS))


def _to_step_major(src_ref, dst_ref, seg):
    for s in range(SUBLANES):
        dst_ref[pl.ds(s, seg, stride=SUBLANES), :] = src_ref[pl.ds(seg * s, seg), :]


def _segment_rows(ref, s, seg):
    return ref[pl.ds(s, seg, stride=SUBLANES), :]


def _cmul(ar, ai, xr, xi):
    return ar * xr - ai * xi, ar * xi + ai * xr


def _s5_tables(a_ref, pw_s, pwr_s, S, seg):
    ar, ai = a_ref[:, :S], a_ref[:, S:]

    def step(i, c):
        pr, pi = c
        pw_s[i, :, :S] = pr
        pw_s[i, :, S:] = pi
        nr, ni = _cmul(ar, ai, pr, pi)
        pwr_s[seg - 1 - i, :, :S] = nr
        pwr_s[seg - 1 - i, :, S:] = ni
        return nr, ni

    pr, pi = lax.fori_loop(0, seg, step, (jnp.ones_like(ar), jnp.zeros_like(ai)))
    pw_s[seg, :, :S] = pr
    pw_s[seg, :, S:] = pi


def _s5_fwd(proj, bmat, cmat, a_rows, d_skip, *, tc=512):
    L = proj.shape[0]
    tc = min(tc, L)
    nt = L // tc
    seg = tc // SUBLANES
    S = STATE_COLS

    def body(u_ref, b_ref, c_ref, a_ref, d_ref, y_ref, yg_ref, xp_ref,
             bu_s, xp_s, pw_s, pwr_s, carry_s, e_s, up_s, yc_s):
        @pl.when(pl.program_id(1) == 0)
        def _():
            carry_s[...] = jnp.zeros_like(carry_s)
            _s5_tables(a_ref, pw_s, pwr_s, S, seg)

        ar, ai = a_ref[:, :S], a_ref[:, S:]
        _to_step_major(u_ref, up_s, seg)
        bu = jnp.dot(up_s[...].astype(BF16), b_ref[...], preferred_element_type=F32)
        bu_s[...] = bu.reshape(seg, SUBLANES, 2 * S)

        def step(i, carry):
            cr, ci = carry
            xp_s[i, :, :S] = cr
            xp_s[i, :, S:] = ci
            return ar * cr - ai * ci + bu_s[i, :, :S], ar * ci + ai * cr + bu_s[i, :, S:]

        zero = jnp.zeros((SUBLANES, S), F32)
        fr, fi = lax.fori_loop(0, seg, step, (zero, zero))
        pr, pi = pw_s[seg, 0:1, :S], pw_s[seg, 0:1, S:]
        er, ei = carry_s[0:1, :S], carry_s[0:1, S:]
        for s in range(SUBLANES):
            e_s[s:s + 1, :S] = er
            e_s[s:s + 1, S:] = ei
            tr, ti = _cmul(pr, pi, er, ei)
            er, ei = fr[s:s + 1] + tr, fi[s:s + 1] + ti
        carry_s[0:1, :S] = er
        carry_s[0:1, S:] = ei
        pw = pw_s[0:seg]
        tr, ti = _cmul(pw[:, :, :S], pw[:, :, S:], e_s[:, :S][None], e_s[:, S:][None])
        xl = xp_s[...]
        xp = jnp.concatenate([xl[:, :, :S] + tr, xl[:, :, S:] + ti], axis=-1).reshape(tc, 2 * S)
        xp_ref[...] = xp
        a1r, a1i = ar[0:1], ai[0:1]
        x_re = a1r * xp[:, :S] - a1i * xp[:, S:] + bu[:, :S]
        x_im = a1r * xp[:, S:] + a1i * xp[:, :S] + bu[:, S:]
        xs = jnp.concatenate([x_re, x_im], axis=1).astype(BF16)
        yc_s[...] = jnp.dot(xs, c_ref[...], preferred_element_type=F32)
        for s in range(SUBLANES):
            rows = pl.ds(seg * s, seg)
            y = _segment_rows(yc_s, s, seg) + d_ref[...] * u_ref[rows, :]
            y_ref[rows, :] = y
            yg_ref[rows, :] = _gelu(y).astype(BF16)

    return pl.pallas_call(
        body, name="s5_fwd",
        out_shape=(jax.ShapeDtypeStruct((L, MAIN_WIDTH), F32),
                   jax.ShapeDtypeStruct((L, MAIN_WIDTH), BF16),
                   jax.ShapeDtypeStruct((L, SSM_BLOCKS * 2 * S), F32)),
        grid=(SSM_BLOCKS, nt),
        in_specs=[pl.BlockSpec((tc, LANES), lambda b, t: (t, b)),
                  pl.BlockSpec((None, LANES, 2 * S), lambda b, t: (b, 0, 0)),
                  pl.BlockSpec((None, 2 * S, LANES), lambda b, t: (b, 0, 0)),
                  pl.BlockSpec((None, SUBLANES, 2 * S), lambda b, t: (b, 0, 0)),
                  pl.BlockSpec((1, LANES), lambda b, t: (0, b))],
        out_specs=(pl.BlockSpec((tc, LANES), lambda b, t: (t, b)),
                   pl.BlockSpec((tc, LANES), lambda b, t: (t, b)),
                   pl.BlockSpec((tc, 2 * S), lambda b, t: (t, b))),
        scratch_shapes=[pltpu.VMEM((seg, SUBLANES, 2 * S), F32),
                        pltpu.VMEM((seg, SUBLANES, 2 * S), F32),
                        pltpu.VMEM((seg + 1, SUBLANES, 2 * S), F32),
                        pltpu.VMEM((seg, SUBLANES, 2 * S), F32),
                        pltpu.VMEM((SUBLANES, 2 * S), F32),
                        pltpu.VMEM((SUBLANES, 2 * S), F32),
                        pltpu.VMEM((tc, LANES), F32),
                        pltpu.VMEM((tc, LANES), F32)],
        compiler_params=_params("parallel", "arbitrary"),
    )(proj, bmat, cmat, a_rows, d_skip.reshape(1, MAIN_WIDTH))


def _s5_bwd(proj, dyg_a, dyg_b, y, xp, bmat, cmat, a_rows, d_skip, dproj, *, tc=512):
    L = proj.shape[0]
    tc = min(tc, L)
    nt = L // tc
    seg = tc // SUBLANES
    S = STATE_COLS
    nn = (((1,), (1,)), ((), ()))
    tn = (((0,), (0,)), ((), ()))

    def fold_diagonal(acc_ref, mask_ref, fold_ref):
        x = acc_ref[...] * mask_ref[...]
        hi = x.astype(BF16)
        rest = x - hi.astype(F32)
        mid = rest.astype(BF16)
        low = (rest - mid.astype(F32)).astype(BF16)
        return sum(jnp.dot(piece, fold_ref[...], preferred_element_type=F32) for piece in (hi, mid, low))

    def body(u_ref, dyga_ref, dygb_ref, y_ref, xp_ref, b_ref, c_ref, a_ref, d_ref, mask_ref, fold_ref, dp_hbm,
             du_ref, dbd_ref, dcd_ref, da_ref, dd_ref,
             dl_s, pw_s, pwr_s, carry_s, e_s, up_s, dy_s, dyp_s, dup_s, db_ref, dc_ref):
        @pl.when(pl.program_id(1) == 0)
        def _():
            carry_s[...] = jnp.zeros_like(carry_s)
            db_ref[...] = jnp.zeros_like(db_ref)
            dc_ref[...] = jnp.zeros_like(dc_ref)
            da_ref[...] = jnp.zeros_like(da_ref)
            dd_ref[...] = jnp.zeros_like(dd_ref)
            _s5_tables(a_ref, pw_s, pwr_s, S, seg)

        ar, ai = a_ref[:, :S], a_ref[:, S:]
        a1r, a1i = ar[0:1], ai[0:1]
        u = u_ref[...]
        dy = (dyga_ref[...] + dygb_ref[...]) * _gelu_grad(y_ref[...])
        dy_s[...] = dy
        xp = xp_ref[...]
        _to_step_major(u_ref, up_s, seg)
        _to_step_major(dy_s, dyp_s, seg)
        ubp = up_s[...].astype(BF16)
        dyp = dyp_s[...].astype(BF16)
        bu = jnp.dot(ubp, b_ref[...], preferred_element_type=F32)
        x_re = a1r * xp[:, :S] - a1i * xp[:, S:] + bu[:, :S]
        x_im = a1r * xp[:, S:] + a1i * xp[:, :S] + bu[:, S:]
        xs = jnp.concatenate([x_re, x_im], axis=1).astype(BF16)
        dc_ref[...] += lax.dot_general(dyp, xs, tn, preferred_element_type=F32)
        dx = lax.dot_general(dyp, c_ref[...], nn, preferred_element_type=F32)
        dl_s[...] = dx.reshape(seg, SUBLANES, 2 * S)

        def step(k, carry):
            cr, ci = carry
            i = seg - 1 - k
            lr = dl_s[i, :, :S] + (ar * cr + ai * ci)
            li = dl_s[i, :, S:] + (ar * ci - ai * cr)
            dl_s[i, :, :S] = lr
            dl_s[i, :, S:] = li
            return lr, li

        zero = jnp.zeros((SUBLANES, S), F32)
        fr, fi = lax.fori_loop(0, seg, step, (zero, zero))
        pr, pi = pw_s[seg, 0:1, :S], pw_s[seg, 0:1, S:]
        er, ei = carry_s[0:1, :S], carry_s[0:1, S:]
        for s in range(SUBLANES - 1, -1, -1):
            e_s[s:s + 1, :S] = er
            e_s[s:s + 1, S:] = ei
            er, ei = fr[s:s + 1] + (pr * er + pi * ei), fi[s:s + 1] + (pr * ei - pi * er)
        carry_s[0:1, :S] = er
        carry_s[0:1, S:] = ei
        er, ei = e_s[:, :S][None], e_s[:, S:][None]
        pw = pwr_s[...]
        pwr, pwi = pw[:, :, :S], pw[:, :, S:]
        ll = dl_s[...]
        lam = jnp.concatenate([ll[:, :, :S] + (pwr * er + pwi * ei), ll[:, :, S:] + (pwr * ei - pwi * er)],
                              axis=-1).reshape(tc, 2 * S)
        l_re, l_im = lam[:, :S], lam[:, S:]
        da_ref[0:1, :S] += jnp.sum(l_re * xp[:, :S] + l_im * xp[:, S:], axis=0, keepdims=True)
        da_ref[0:1, S:] += jnp.sum(l_im * xp[:, :S] - l_re * xp[:, S:], axis=0, keepdims=True)
        lamb = lam.astype(BF16)
        dup_s[...] = lax.dot_general(lamb, b_ref[...], nn, preferred_element_type=F32)
        for s in range(SUBLANES):
            rows = pl.ds(seg * s, seg)
            du = _segment_rows(dup_s, s, seg) + d_ref[...] * dy_s[rows, :]
            du_ref[rows, :] = du.astype(du_ref.dtype)
        db_ref[...] += lax.dot_general(ubp, lamb, tn, preferred_element_type=F32)
        dd_ref[0:1, :] += jnp.sum(dy * u, axis=0, keepdims=True)

        @pl.when(pl.program_id(1) == nt - 1)
        def _():
            dbd_ref[...] = fold_diagonal(db_ref, mask_ref, fold_ref)
            dcd_ref[...] = fold_diagonal(dc_ref, mask_ref, fold_ref)

    rev = lambda b, t: (nt - 1 - t, b)
    col = jnp.arange(2 * S)
    fold = ((col // S * SSM_STATE + col % SSM_STATE)[:, None] == jnp.arange(LANES)[None, :]).astype(BF16)
    return pl.pallas_call(
        body, name="s5_bwd",
        out_shape=(jax.ShapeDtypeStruct(dproj.shape, dproj.dtype),
                   jax.ShapeDtypeStruct((SSM_BLOCKS, LANES, LANES), F32),
                   jax.ShapeDtypeStruct((SSM_BLOCKS, LANES, LANES), F32),
                   jax.ShapeDtypeStruct((SSM_BLOCKS, SUBLANES, 2 * S), F32),
                   jax.ShapeDtypeStruct((SUBLANES, MAIN_WIDTH), F32)),
        input_output_aliases={11: 0},
        grid=(SSM_BLOCKS, nt),
        in_specs=[pl.BlockSpec((tc, LANES), rev),
                  pl.BlockSpec((tc, LANES), rev),
                  pl.BlockSpec((tc, LANES), rev),
                  pl.BlockSpec((tc, LANES), rev),
                  pl.BlockSpec((tc, 2 * S), rev),
                  pl.BlockSpec((None, LANES, 2 * S), lambda b, t: (b, 0, 0)),
                  pl.BlockSpec((None, 2 * S, LANES), lambda b, t: (b, 0, 0)),
                  pl.BlockSpec((None, SUBLANES, 2 * S), lambda b, t: (b, 0, 0)),
                  pl.BlockSpec((1, LANES), lambda b, t: (0, b)),
                  pl.BlockSpec((LANES, 2 * S), lambda b, t: (0, 0)),
                  pl.BlockSpec((2 * S, LANES), lambda b, t: (0, 0)),
                  _ANY],
        out_specs=(pl.BlockSpec((tc, LANES), rev),
                   pl.BlockSpec((None, LANES, LANES), lambda b, t: (b, 0, 0)),
                   pl.BlockSpec((None, LANES, LANES), lambda b, t: (b, 0, 0)),
                   pl.BlockSpec((None, SUBLANES, 2 * S), lambda b, t: (b, 0, 0)),
                   pl.BlockSpec((SUBLANES, LANES), lambda b, t: (0, b))),
        scratch_shapes=[pltpu.VMEM((seg, SUBLANES, 2 * S), F32),
                        pltpu.VMEM((seg + 1, SUBLANES, 2 * S), F32),
                        pltpu.VMEM((seg, SUBLANES, 2 * S), F32),
                        pltpu.VMEM((SUBLANES, 2 * S), F32),
                        pltpu.VMEM((SUBLANES, 2 * S), F32),
                        pltpu.VMEM((tc, LANES), F32),
                        pltpu.VMEM((tc, LANES), F32),
                        pltpu.VMEM((tc, LANES), F32),
                        pltpu.VMEM((tc, LANES), F32),
                        pltpu.VMEM((LANES, 2 * S), F32),
                        pltpu.VMEM((LANES, 2 * S), F32)],
        compiler_params=_params("parallel", "arbitrary"),
    )(proj, dyg_a, dyg_b, y, xp, bmat, cmat, a_rows, d_skip.reshape(1, MAIN_WIDTH), _s5_diag_mask(), fold, dproj)


_Z_COLS = slice(MAIN_WIDTH, 2 * MAIN_WIDTH)
_ZM_COLS = slice(2 * MAIN_WIDTH + MEM_WIDTH, IN_WIDTH)


def _proj_rows(tr):
    return pl.BlockSpec((tr, IN_WIDTH), lambda i: (i, 0))


def _row_specs(tr):
    main = pl.BlockSpec((tr, MAIN_WIDTH), lambda i: (i, 0))
    z = pl.BlockSpec((tr, MAIN_WIDTH), lambda i: (i, 1))
    zm = pl.BlockSpec((tr, MEM_WIDTH), lambda i: (i, IN_WIDTH // MEM_WIDTH - 1))
    mem = pl.BlockSpec((tr, MEM_WIDTH), lambda i: (i, 0))
    cat = pl.BlockSpec((tr, D_MODEL), lambda i: (i, 0))
    vec = pl.BlockSpec((1, MAIN_WIDTH), lambda i: (0, 0))
    return main, z, zm, mem, cat, vec


def _gate_a_fwd(y, t, b_glu, proj, o_mem, *, tr=256):
    L = y.shape[0]
    tr = min(tr, L)

    def body(y_ref, t_ref, b_ref, z_ref, zm_ref, om_ref, o_ref):
        yg = _gelu(y_ref[...])
        sz, _ = _silu_and_grad(z_ref[...])
        o_ref[:, :MAIN_WIDTH] = (yg * _sigmoid(t_ref[...] + b_ref[...]) * sz).astype(BF16)
        szm, _ = _silu_and_grad(zm_ref[...])
        o_ref[:, MAIN_WIDTH:] = (om_ref[...] * szm).astype(BF16)

    main, z, zm, mem, cat, vec = _row_specs(tr)
    return pl.pallas_call(
        body, name="gate_a_fwd", out_shape=jax.ShapeDtypeStruct((L, D_MODEL), BF16),
        grid=(L // tr,), in_specs=[main, main, vec, z, zm, mem], out_specs=cat,
        compiler_params=_params("parallel"),
    )(y, t, b_glu.reshape(1, MAIN_WIDTH), proj, proj, o_mem)


def _gate_a_bwd(dcat, y, t, b_glu, proj, o_mem, *, tr=256):
    L = y.shape[0]
    tr = min(tr, L)

    def body(dc_ref, y_ref, t_ref, b_ref, z_ref, zm_ref, om_ref,
             dp_ref, dt_ref, dyg_ref, dom_ref, db_ref):
        dmain = dc_ref[:, :MAIN_WIDTH]
        dmemo = dc_ref[:, MAIN_WIDTH:]
        yg = _gelu(y_ref[...])
        sg = _sigmoid(t_ref[...] + b_ref[...])
        sz, gz = _silu_and_grad(z_ref[...])
        dp_ref[:, _Z_COLS] = (dmain * (yg * sg) * gz).astype(BF16)
        dy2 = dmain * sz
        dyg_ref[...] = dy2 * sg
        dt = dy2 * yg * (sg * (1.0 - sg))
        dt_ref[...] = dt.astype(BF16)

        @pl.when(pl.program_id(0) == 0)
        def _():
            db_ref[...] = jnp.zeros_like(db_ref)

        db_ref[...] += jnp.sum(dt, axis=0, keepdims=True)
        szm, gzm = _silu_and_grad(zm_ref[...])
        dom_ref[...] = dmemo * szm
        dp_ref[:, _ZM_COLS] = (dmemo * om_ref[...] * gzm).astype(BF16)

    main, z, zm, mem, cat, vec = _row_specs(tr)
    outs = pl.pallas_call(
        body, name="gate_a_bwd",
        out_shape=(jax.ShapeDtypeStruct((L, IN_WIDTH), BF16),
                   jax.ShapeDtypeStruct((L, MAIN_WIDTH), BF16), jax.ShapeDtypeStruct((L, MAIN_WIDTH), F32),
                   jax.ShapeDtypeStruct((L, MEM_WIDTH), F32), jax.ShapeDtypeStruct((1, MAIN_WIDTH), F32)),
        grid=(L // tr,), in_specs=[cat, main, main, vec, z, zm, mem],
        out_specs=(_proj_rows(tr), main, main, mem, vec),
        compiler_params=_params("arbitrary"),
    )(dcat, y, t, b_glu.reshape(1, MAIN_WIDTH), proj, proj, o_mem)
    return outs


def _gate_b_fwd(att, proj, o_mem, *, tr=256):
    L = att.shape[0]
    tr = min(tr, L)

    def body(a_ref, z_ref, zm_ref, om_ref, o_ref):
        sz, _ = _silu_and_grad(z_ref[...])
        o_ref[:, :MAIN_WIDTH] = (a_ref[...] * sz).astype(BF16)
        szm, _ = _silu_and_grad(zm_ref[...])
        o_ref[:, MAIN_WIDTH:] = (om_ref[...] * szm).astype(BF16)

    main, z, zm, mem, cat, _ = _row_specs(tr)
    return pl.pallas_call(
        body, name="gate_b_fwd", out_shape=jax.ShapeDtypeStruct((L, D_MODEL), BF16),
        grid=(L // tr,), in_specs=[main, z, zm, mem], out_specs=cat,
        compiler_params=_params("parallel"),
    )(att, proj, proj, o_mem)


def _gate_b_bwd(dcat, att, proj, o_mem, *, tr=256):
    L = att.shape[0]
    tr = min(tr, L)

    def body(dc_ref, a_ref, z_ref, zm_ref, om_ref, da_ref, dp_ref, dom_ref, dl_ref):
        dmain = dc_ref[:, :MAIN_WIDTH]
        dmemo = dc_ref[:, MAIN_WIDTH:]
        att = a_ref[...]
        sz, gz = _silu_and_grad(z_ref[...])
        datt = dmain * sz
        da_ref[...] = datt
        dp_ref[:, _Z_COLS] = (dmain * att * gz).astype(BF16)
        szm, gzm = _silu_and_grad(zm_ref[...])
        dom_ref[...] = dmemo * szm
        dp_ref[:, _ZM_COLS] = (dmemo * om_ref[...] * gzm).astype(BF16)
        prod = datt * att
        for h in range(FOX_HEADS):
            dl_ref[h] = jnp.sum(prod[:, h * HEAD_DIM:(h + 1) * HEAD_DIM], axis=1, keepdims=True)

    main, z, zm, mem, cat, _ = _row_specs(tr)
    delta = pl.BlockSpec((FOX_HEADS, tr, 1), lambda i: (0, i, 0))
    return pl.pallas_call(
        body, name="gate_b_bwd",
        out_shape=(jax.ShapeDtypeStruct((L, MAIN_WIDTH), F32), jax.ShapeDtypeStruct((L, IN_WIDTH), BF16),
                   jax.ShapeDtypeStruct((L, MEM_WIDTH), F32), jax.ShapeDtypeStruct((FOX_HEADS, L, 1), F32)),
        grid=(L // tr,), in_specs=[cat, main, z, zm, mem], out_specs=(main, _proj_rows(tr), mem, delta),
        compiler_params=_params("parallel"),
    )(dcat, att, proj, proj, o_mem)


_MEM_Q_COL = (2 * MAIN_WIDTH) // HEAD_DIM
_NT = (((1,), (1,)), ((), ()))
_TN = (((0,), (0,)), ((), ()))


def _mem_probs(q_ref, k_ref):
    qs = (q_ref[...] * (HEAD_DIM ** -0.5)).astype(BF16)
    s = lax.dot_general(qs, k_ref[...].astype(BF16), _NT, preferred_element_type=F32)
    e = jnp.exp(s - jnp.max(s, axis=-1, keepdims=True))
    return qs, e / jnp.sum(e, axis=-1, keepdims=True)


def _mem_attn_fwd(proj, kvm, *, tq=2048):
    L = proj.shape[0]
    tq = min(tq, L)

    def body(q_ref, k_ref, v_ref, o_ref):
        _, p = _mem_probs(q_ref, k_ref)
        o_ref[...] = jnp.dot(p.astype(BF16), v_ref[...].astype(BF16), preferred_element_type=F32)

    return pl.pallas_call(
        body, name="mem_attn_fwd", out_shape=jax.ShapeDtypeStruct((L, MEM_WIDTH), F32),
        grid=(MEM_HEADS, L // tq),
        in_specs=[pl.BlockSpec((tq, HEAD_DIM), lambda h, i: (i, _MEM_Q_COL + h)),
                  pl.BlockSpec((N_MEM, HEAD_DIM), lambda h, i: (0, h)),
                  pl.BlockSpec((N_MEM, HEAD_DIM), lambda h, i: (0, MEM_HEADS + h))],
        out_specs=pl.BlockSpec((tq, HEAD_DIM), lambda h, i: (i, h)),
        compiler_params=_params("parallel", "parallel"),
    )(proj, kvm, kvm)


def _mem_attn_bwd(proj, kvm, do, dproj, *, tq=2048):
    L = proj.shape[0]
    tq = min(tq, L)

    def body(q_ref, k_ref, v_ref, do_ref, dp_hbm, dq_ref, dk_ref, dv_ref):
        @pl.when(pl.program_id(1) == 0)
        def _():
            dk_ref[...] = jnp.zeros_like(dk_ref)
            dv_ref[...] = jnp.zeros_like(dv_ref)

        qs, p = _mem_probs(q_ref, k_ref)
        dob = do_ref[...].astype(BF16)
        dp = lax.dot_general(dob, v_ref[...].astype(BF16), _NT, preferred_element_type=F32)
        ds = p * (dp - jnp.sum(p * dp, axis=-1, keepdims=True))
        dsb = ds.astype(BF16)
        dq = jnp.dot(dsb, k_ref[...].astype(BF16), preferred_element_type=F32) * (HEAD_DIM ** -0.5)
        dq_ref[...] = dq.astype(BF16)
        dk_ref[...] += lax.dot_general(dsb, qs, _TN, preferred_element_type=F32)
        dv_ref[...] += lax.dot_general(p.astype(BF16), dob, _TN, preferred_element_type=F32)

    dproj, dk, dv = pl.pallas_call(
        body, name="mem_attn_bwd",
        out_shape=(jax.ShapeDtypeStruct(dproj.shape, dproj.dtype),
                   jax.ShapeDtypeStruct((N_MEM, MEM_WIDTH), F32),
                   jax.ShapeDtypeStruct((N_MEM, MEM_WIDTH), F32)),
        grid=(MEM_HEADS, L // tq),
        in_specs=[pl.BlockSpec((tq, HEAD_DIM), lambda h, i: (i, _MEM_Q_COL + h)),
                  pl.BlockSpec((N_MEM, HEAD_DIM), lambda h, i: (0, h)),
                  pl.BlockSpec((N_MEM, HEAD_DIM), lambda h, i: (0, MEM_HEADS + h)),
                  pl.BlockSpec((tq, HEAD_DIM), lambda h, i: (i, h)),
                  _ANY],
        out_specs=(pl.BlockSpec((tq, HEAD_DIM), lambda h, i: (i, _MEM_Q_COL + h)),
                   pl.BlockSpec((N_MEM, HEAD_DIM), lambda h, i: (0, h)),
                   pl.BlockSpec((N_MEM, HEAD_DIM), lambda h, i: (0, h))),
        input_output_aliases={4: 0},
        compiler_params=_params("parallel", "arbitrary"),
    )(proj, kvm, kvm, do, dproj)
    return dproj, jnp.concatenate([dk, dv], axis=1)


def _tile_cumsum(x, row, reverse):
    for sh in (1, 2, 4):
        if reverse:
            x = x + jnp.where(row < SUBLANES - sh, pltpu.roll(x, SUBLANES - sh, 0), 0.0)
        else:
            x = x + jnp.where(row >= sh, pltpu.roll(x, sh, 0), 0.0)
    return x


def _fgate_fwd(pre, b_pad):
    L = pre.shape[0]
    n8 = L // SUBLANES

    def body(p_ref, b_ref, o_ref):
        row = lax.broadcasted_iota(jnp.int32, (SUBLANES, LANES), 0)
        b = b_ref[...]

        def step(i, carry):
            x = p_ref[i] + b
            logf = jnp.minimum(x, 0.0) - jnp.log(1.0 + jnp.exp(-jnp.abs(x)))
            t = _tile_cumsum(logf, row, False) + carry
            o_ref[i] = t
            return t[SUBLANES - 1:SUBLANES, :]

        lax.fori_loop(0, n8, step, jnp.zeros((1, LANES), F32))

    out = pl.pallas_call(
        body, name="fgate_fwd", out_shape=jax.ShapeDtypeStruct((n8, SUBLANES, LANES), F32),
        compiler_params=_params(),
    )(pre.reshape(n8, SUBLANES, LANES), b_pad.reshape(1, LANES))
    return out.reshape(L, LANES)


def _fgate_bwd(dfcum, pre, b_pad):
    L = pre.shape[0]
    n8 = L // SUBLANES

    def body(d_ref, p_ref, b_ref, o_ref, s_ref):
        row = lax.broadcasted_iota(jnp.int32, (SUBLANES, LANES), 0)
        b = b_ref[...]

        def step(k, carry):
            c, acc = carry
            i = n8 - 1 - k
            t = _tile_cumsum(d_ref[i], row, True) + c
            dpre = t * _sigmoid(-(p_ref[i] + b))
            o_ref[i] = dpre
            return t[0:1, :], acc + dpre

        _, acc = lax.fori_loop(0, n8, step, (jnp.zeros((1, LANES), F32), jnp.zeros((SUBLANES, LANES), F32)))
        s_ref[...] = jnp.sum(acc, axis=0, keepdims=True)

    dpre, db = pl.pallas_call(
        body, name="fgate_bwd",
        out_shape=(jax.ShapeDtypeStruct((n8, SUBLANES, LANES), F32), jax.ShapeDtypeStruct((1, LANES), F32)),
        compiler_params=_params(),
    )(dfcum.reshape(n8, SUBLANES, LANES), pre.reshape(n8, SUBLANES, LANES), b_pad.reshape(1, LANES))
    return dpre.reshape(L, LANES), db


FOX_BLOCK = 1024


def _fox_scores(qs, k, fk, diagonal):
    s = lax.dot_general(qs, k, _NT, preferred_element_type=F32) - fk
    if diagonal:
        row = lax.broadcasted_iota(jnp.int32, s.shape, 0)
        col = lax.broadcasted_iota(jnp.int32, s.shape, 1)
        s = jnp.where(row >= col, s, NEG_BIG)
    return s


def _fox_specs(tq, L):
    nq = L // tq
    return dict(
        rows=lambda off: pl.BlockSpec((tq, HEAD_DIM), lambda h, i: (i, off + h)),
        seq=lambda off: pl.BlockSpec((L, HEAD_DIM), lambda h, i: (0, off + h)),
        col=pl.BlockSpec((None, None, tq, 1), lambda h, i: (h, i, 0, 0)),
        col_all=pl.BlockSpec((None, nq, tq, 1), lambda h, i: (h, 0, 0, 0)),
        row=pl.BlockSpec((None, None, 1, tq), lambda h, i: (h, i, 0, 0)),
        row_all=pl.BlockSpec((None, nq, 1, tq), lambda h, i: (h, 0, 0, 0)))


FOX_FWD_HEADS = 2
FOX_FWD_BLOCK = 1024


def _fox_fwd(proj, kv, fk):
    L = proj.shape[0]
    tq = min(FOX_FWD_BLOCK, L)
    nq = L // tq
    nh = FOX_FWD_HEADS
    W = nh * HEAD_DIM
    lse_shape = fk.shape[:2] + (fk.shape[3], 1)
    fk = fk.reshape(FOX_HEADS, nq, 1, tq)

    def body(q_ref, k_ref, v_ref, fk_ref, o_ref, lse_ref, m_s, l_s, acc_s):
        qi = pl.program_id(1)
        cols = [slice(a * HEAD_DIM, (a + 1) * HEAD_DIM) for a in range(nh)]
        qs = [(q_ref[:, cs] * (HEAD_DIM ** -0.5)).astype(BF16) for cs in cols]
        m_s[...] = jnp.full_like(m_s, NEG_BIG)
        l_s[...] = jnp.zeros_like(l_s)
        acc_s[...] = jnp.zeros_like(acc_s)

        def block(j, diagonal):
            r0 = pl.multiple_of(j * tq, tq)
            for a, cs in enumerate(cols):
                s = _fox_scores(qs[a], k_ref[pl.ds(r0, tq), cs], fk_ref[a, j], diagonal)
                m_new = jnp.maximum(m_s[a], jnp.max(s, axis=-1, keepdims=True))
                alpha = jnp.exp(m_s[a] - m_new)
                p = jnp.exp(s - m_new)
                l_s[a] = alpha * l_s[a] + jnp.sum(p, axis=-1, keepdims=True)
                acc_s[a] = alpha * acc_s[a] + jnp.dot(p.astype(BF16), v_ref[pl.ds(r0, tq), cs],
                                                      preferred_element_type=F32)
                m_s[a] = m_new

        def below(j, carry):
            block(j, False)
            return carry

        lax.fori_loop(0, qi, below, 0)
        block(qi, True)
        for a, cs in enumerate(cols):
            o_ref[:, cs] = acc_s[a] / l_s[a]
            lse_ref[a] = m_s[a] + jnp.log(l_s[a])

    att, lse = pl.pallas_call(
        body, name="fox_fwd",
        out_shape=(jax.ShapeDtypeStruct((L, MAIN_WIDTH), F32),
                   jax.ShapeDtypeStruct((FOX_HEADS, nq, tq, 1), F32)),
        grid=(FOX_HEADS // nh, nq),
        in_specs=[pl.BlockSpec((tq, W), lambda h, i: (i, h)),
                  pl.BlockSpec((L, W), lambda h, i: (0, h)),
                  pl.BlockSpec((L, W), lambda h, i: (0, FOX_HEADS // nh + h)),
                  pl.BlockSpec((nh, nq, 1, tq), lambda h, i: (h, 0, 0, 0))],
        out_specs=(pl.BlockSpec((tq, W), lambda h, i: (i, h)),
                   pl.BlockSpec((nh, None, tq, 1), lambda h, i: (h, i, 0, 0))),
        scratch_shapes=[pltpu.VMEM((nh, tq, 1), F32), pltpu.VMEM((nh, tq, 1), F32),
                        pltpu.VMEM((nh, tq, HEAD_DIM), F32)],
        compiler_params=_params("parallel", "parallel"),
    )(proj, kv, kv, fk)
    return att, lse.reshape(lse_shape)


def _fox_bwd_dq(proj, kv, fk, lse, delta, datt, dproj):
    L = proj.shape[0]
    tq = min(FOX_BLOCK, L)
    nq = L // tq
    sp = _fox_specs(tq, L)

    def body(q_ref, k_ref, v_ref, fk_ref, lse_ref, dl_ref, do_ref, dp_hbm, dq_ref, df_ref, acc_s, df_s):
        qi = pl.program_id(1)
        qs = (q_ref[...] * (HEAD_DIM ** -0.5)).astype(BF16)
        dob = do_ref[...].astype(BF16)
        lse, dl = lse_ref[...], dl_ref[...]
        acc_s[...] = jnp.zeros_like(acc_s)
        df_s[...] = jnp.zeros_like(df_s)

        def block(j, diagonal):
            r0 = pl.multiple_of(j * tq, tq)
            k = k_ref[pl.ds(r0, tq), :]
            p = jnp.exp(_fox_scores(qs, k, fk_ref[j], diagonal) - lse)
            dp = lax.dot_general(dob, v_ref[pl.ds(r0, tq), :], _NT, preferred_element_type=F32)
            ds = p * (dp - dl)
            acc_s[...] += jnp.dot(ds.astype(BF16), k, preferred_element_type=F32)
            df_s[...] += jnp.sum(ds, axis=1, keepdims=True)

        def below(j, carry):
            block(j, False)
            return carry

        lax.fori_loop(0, qi, below, 0)
        block(qi, True)
        dq_ref[...] = (acc_s[...] * (HEAD_DIM ** -0.5)).astype(BF16)
        df_ref[...] = df_s[...]

    return pl.pallas_call(
        body, name="fox_bwd_dq",
        out_shape=(jax.ShapeDtypeStruct(dproj.shape, dproj.dtype),
                   jax.ShapeDtypeStruct((FOX_HEADS, nq, tq, 1), F32)),
        grid=(FOX_HEADS, nq),
        in_specs=[sp["rows"](0), sp["seq"](0), sp["seq"](FOX_HEADS), sp["row_all"],
                  sp["col"], sp["col"], sp["rows"](0), _ANY],
        out_specs=(sp["rows"](0), sp["col"]),
        input_output_aliases={7: 0},
        scratch_shapes=[pltpu.VMEM((tq, HEAD_DIM), F32), pltpu.VMEM((tq, 1), F32)],
        compiler_params=_params("parallel", "parallel"),
    )(proj, kv, kv, fk, lse, delta, datt, dproj)


def _fox_bwd_dkv(proj, kv, fk, lse, delta, datt):
    L = proj.shape[0]
    tq = min(FOX_BLOCK, L)
    nq = L // tq
    sp = _fox_specs(tq, L)

    def body(q_ref, k_ref, v_ref, fk_ref, lse_ref, dl_ref, do_ref,
             dk_ref, dv_ref, df_ref, dk_s, dv_s, df_s):
        ki = pl.program_id(1)
        k, v, fk = k_ref[...], v_ref[...], fk_ref[...]
        dk_s[...] = jnp.zeros_like(dk_s)
        dv_s[...] = jnp.zeros_like(dv_s)
        df_s[...] = jnp.zeros_like(df_s)

        def block(i, diagonal):
            r0 = pl.multiple_of(i * tq, tq)
            qs = (q_ref[pl.ds(r0, tq), :] * (HEAD_DIM ** -0.5)).astype(BF16)
            dob = do_ref[pl.ds(r0, tq), :].astype(BF16)
            p = jnp.exp(_fox_scores(qs, k, fk, diagonal) - lse_ref[i])
            dp = lax.dot_general(dob, v, _NT, preferred_element_type=F32)
            ds = p * (dp - dl_ref[i])
            dv_s[...] += lax.dot_general(p.astype(BF16), dob, _TN, preferred_element_type=F32)
            dk_s[...] += lax.dot_general(ds.astype(BF16), qs, _TN, preferred_element_type=F32)
            df_s[...] -= jnp.sum(ds, axis=0, keepdims=True)

        def above(i, carry):
            block(i, False)
            return carry

        block(ki, True)
        lax.fori_loop(ki + 1, nq, above, 0)
        dk_ref[...] = dk_s[...].astype(BF16)
        dv_ref[...] = dv_s[...].astype(BF16)
        df_ref[...] = df_s[...]

    return pl.pallas_call(
        body, name="fox_bwd_dkv",
        out_shape=(jax.ShapeDtypeStruct((L, MAIN_WIDTH), BF16),
                   jax.ShapeDtypeStruct((L, MAIN_WIDTH), BF16),
                   jax.ShapeDtypeStruct((FOX_HEADS, nq, 1, tq), F32)),
        grid=(FOX_HEADS, nq),
        in_specs=[sp["seq"](0), sp["rows"](0), sp["rows"](FOX_HEADS), sp["row"],
                  sp["col_all"], sp["col_all"], sp["seq"](0)],
        out_specs=(sp["rows"](0), sp["rows"](0), sp["row"]),
        scratch_shapes=[pltpu.VMEM((tq, HEAD_DIM), F32), pltpu.VMEM((tq, HEAD_DIM), F32),
                        pltpu.VMEM((1, tq), F32)],
        compiler_params=_params("parallel", "parallel"),
    )(proj, kv, kv, fk, lse, delta, datt)


def _pad_lanes(a):
    return jnp.pad(a, ((0, 0), (0, LANES - a.shape[1])))


def _mem_branch_fwd(memn, w_mk, proj, tag):
    kvm = _mm(memn, w_mk, name="mem_kv_" + tag)
    return kvm, _mem_attn_fwd(proj, kvm)


def _mem_branch_bwd(mem, g, w_mk, proj, memn, kvm, do_mem, dproj, tag):
    dproj, dkvm = _mem_attn_bwd(proj, kvm, do_mem, dproj)
    dkvm = dkvm.astype(BF16)
    dw_mk = _mm(memn, dkvm, ta=True, name="dw_mem_kv_" + tag, out_dtype=BF16)
    dmemn = _mm(dkvm, w_mk, tb=True, name="dmemn_" + tag)
    _, dg = _rmsnorm_bwd(mem, g, dmemn, name="mem_norm_bwd_" + tag, dx_dtype=BF16)
    return dproj, dw_mk, dg


def _local_step(x, mem, target, w, fetch=None, grads_ready=None):
    if grads_ready is None:
        grads_ready = lambda group, grads, token: token
    L = x.shape[0]
    g = {}
    w = dict(w)

    b_re_t = jnp.transpose(w["b_re"], (0, 2, 1))
    b_im_t = jnp.transpose(w["b_im"], (0, 2, 1))
    ar, ai, bbr_t, bbi_t = _s5_prep(w["lam_re"], w["lam_im"], w["log_step"], b_re_t, b_im_t)
    bmat, cmat = _s5_block_mats(bbr_t, bbi_t, w["c_re"], w["c_im"])
    a_rows = _s5_a_rows(ar, ai)

    hn0 = _rmsnorm_fwd(x, w["pre_norm_g"][0], name="pre_norm_0", out_dtype=BF16)
    memn0 = _rmsnorm_fwd(mem, w["mem_norm_g"][0], name="mem_norm_0", out_dtype=BF16)
    memn1 = _rmsnorm_fwd(mem, w["mem_norm_g"][1], name="mem_norm_1", out_dtype=BF16)
    if fetch is not None:
        w.update(fetch("a", [hn0, memn0, memn1, bmat, cmat, a_rows]))
    proj_a = _mm(hn0, w["w_in_a"], name="in_proj_a")
    y, yg, xp = _s5_fwd(proj_a, bmat, cmat, a_rows, w["d_skip"])
    if fetch is not None:
        w.update(fetch("b", yg))
    t = _mm(yg, w["w_glu"], name="glu_proj")
    kvm0, om0 = _mem_branch_fwd(memn0, w["w_mem_kv"][0], proj_a, "0")
    cat0 = _gate_a_fwd(y, t, w["b_glu"], proj_a, om0)
    o0 = _mm(cat0, w["w_out"][0], name="out_proj_0")
    h1 = _rmsnorm_fwd(o0, w["post_norm_g"][0], res=x, name="post_norm_0")

    kv_in = _rmsnorm_fwd(h1, w["kv_norm_g"], name="kv_norm", out_dtype=BF16)
    if fetch is not None:
        w.update(fetch("c", kv_in))
    kv = _mm(kv_in, w["w_kv"], name="kv_proj", out_dtype=BF16)
    pre_f = _mm(kv_in, w["w_fgate"], name="fgate_proj")
    b_f = jnp.pad(w["b_fgate"], (0, LANES - FOX_HEADS))
    fcum = _fgate_fwd(pre_f, b_f)
    fc = jnp.transpose(fcum[:, :FOX_HEADS])
    tq = min(FOX_BLOCK, L)
    fk = fc.reshape(FOX_HEADS, L // tq, 1, tq)

    hn1 = _rmsnorm_fwd(h1, w["pre_norm_g"][1], name="pre_norm_1", out_dtype=BF16)
    proj_b = _mm(hn1, w["w_in_b"], name="in_proj_b")
    att, lse = _fox_fwd(proj_b, kv, fk)
    kvm1, om1 = _mem_branch_fwd(memn1, w["w_mem_kv"][1], proj_b, "1")
    cat1 = _gate_b_fwd(att, proj_b, om1)
    o1 = _mm(cat1, w["w_out"][1], name="out_proj_1")
    dh2, loss_row = _final_norm_loss(o1, w["post_norm_g"][1], h1, target)

    do1, dpost1 = _rmsnorm_bwd(o1, w["post_norm_g"][1], dh2, name="post_norm_bwd_1", dx_dtype=BF16)
    dcat1 = _mm(do1, w["w_out"][1], tb=True, name="dcat_1", out_dtype=BF16)
    g["w_out_1"] = _mm(cat1, do1, ta=True, name="dw_out_1", out_dtype=BF16)
    datt, dproj_b, dom1, delta = _gate_b_bwd(dcat1, att, proj_b, om1)
    dproj_b, g["w_mem_kv_1"], dmemg1 = _mem_branch_bwd(mem, w["mem_norm_g"][1], w["w_mem_kv"][1], proj_b,
                                                      memn1, kvm1, dom1, dproj_b, "1")
    delta = delta.reshape(lse.shape)
    dproj_b, dfq = _fox_bwd_dq(proj_b, kv, fk, lse, delta, datt, dproj_b)
    dk, dv, dfk = _fox_bwd_dkv(proj_b, kv, fk, lse, delta, datt)
    g["w_in_b"] = _mm(hn1, dproj_b, ta=True, name="dw_in_b", out_dtype=BF16, shards=N_CHIPS)
    dhn1 = _mm(dproj_b, w["w_in_b"], tb=True, name="dhn_1")

    dkv = jnp.concatenate([dk, dv], axis=1)
    g["w_kv"] = _mm(kv_in, dkv, ta=True, name="dw_kv", out_dtype=BF16, shards=N_CHIPS)
    dkv_in_a = _mm(dkv, w["w_kv"], tb=True, name="dkv_in_kv")
    dfcum = _pad_lanes(jnp.transpose(dfq.reshape(FOX_HEADS, L) + dfk.reshape(FOX_HEADS, L)))
    dpre_f, db_f = _fgate_bwd(dfcum, pre_f, b_f)
    g["b_fgate"] = db_f[0, :FOX_HEADS]
    g["w_fgate"] = _mm(kv_in, dpre_f, ta=True, name="dw_fgate")[:, :FOX_HEADS]
    dkv_in_b = _mm(dpre_f, w["w_fgate"], tb=True, name="dkv_in_fgate")
    dh1, g["kv_norm_g"], dpre1 = _rmsnorm_bwd_pair(h1, w["kv_norm_g"], (dkv_in_a, dkv_in_b), w["pre_norm_g"][1],
                                                   dhn1, adds=(dh2,), name="kv_pre_norm_bwd")
    dh1 = grads_ready("b", g, dh1)

    do0, dpost0 = _rmsnorm_bwd(o0, w["post_norm_g"][0], dh1, name="post_norm_bwd_0", dx_dtype=BF16)
    dcat0 = _mm(do0, w["w_out"][0], tb=True, name="dcat_0", out_dtype=BF16)
    g["w_out_0"] = _mm(cat0, do0, ta=True, name="dw_out_0", out_dtype=BF16)
    dcat0 = grads_ready("b_send", g, dcat0)
    dproj_a, dt, dyg_a, dom0, db_glu = _gate_a_bwd(dcat0, y, t, w["b_glu"], proj_a, om0)
    g["b_glu"] = db_glu[0]
    g["w_glu"] = _mm(yg, dt, ta=True, name="dw_glu", out_dtype=BF16)
    dyg_b = _mm(dt, w["w_glu"], tb=True, name="dyg")
    dproj_a, g["w_mem_kv_0"], dmemg0 = _mem_branch_bwd(mem, w["mem_norm_g"][0], w["w_mem_kv"][0], proj_a,
                                                      memn0, kvm0, dom0, dproj_a, "0")
    dyg_b = grads_ready("a1", g, dyg_b)
    dproj_a, db_blk, dc_blk, da_rows, dd_skip = _s5_bwd(proj_a, dyg_a, dyg_b, y, xp, bmat, cmat, a_rows,
                                                        w["d_skip"], dproj_a)
    dproj_a = grads_ready("a1_send", g, dproj_a)
    g["d_skip"] = dd_skip[0]
    g["w_in_a"] = _mm(hn0, dproj_a, ta=True, name="dw_in_a", out_dtype=BF16, shards=N_CHIPS)
    dproj_a = grads_ready("a2", g, dproj_a)
    dhn0 = _mm(dproj_a, w["w_in_a"], tb=True, name="dhn_0")
    grad_x, dpre0 = _rmsnorm_bwd(x, w["pre_norm_g"][0], dhn0, adds=(dh1,), name="pre_norm_bwd_0")

    dbb = _s5_unfold(db_blk)
    dcc = _s5_unfold(dc_blk)
    g["c_re"], g["c_im"] = dcc[0], -dcc[1]
    d_ar = da_rows[:, 0, :STATE_COLS].reshape(SSM_GROUPS, SSM_STATE)
    d_ai = da_rows[:, 0, STATE_COLS:].reshape(SSM_GROUPS, SSM_STATE)
    dlr, dli, dls, dbr_t, dbi_t = _s5_prep_bwd(w["lam_re"], w["lam_im"], w["log_step"], b_re_t, b_im_t,
                                               d_ar, d_ai, dbb[0], dbb[1])
    g["lam_re"], g["lam_im"], g["log_step"] = dlr, dli, dls[:, 0]
    g["b_re"] = jnp.transpose(dbr_t, (0, 2, 1))
    g["b_im"] = jnp.transpose(dbi_t, (0, 2, 1))
    g["pre_norm_g"] = jnp.stack([dpre0, dpre1])
    g["post_norm_g"] = jnp.stack([dpost0, dpost1])
    g["mem_norm_g"] = jnp.stack([dmemg0, dmemg1])
    return loss_row, grad_x, g


_MESH = pl.DeviceIdType.MESH
_ANY = pl.BlockSpec(memory_space=pl.ANY)


def _place():
    x, y, c = lax.axis_index("x"), lax.axis_index("y"), lax.axis_index("c")
    chips = [(1 - x, y), (x, 1 - y), (1 - x, 1 - y)]
    return x, y, c, chips


_HBM = pl.BlockSpec(memory_space=pltpu.HBM)
_SEM = pl.BlockSpec(memory_space=pltpu.SEMAPHORE)
_SIDE = pltpu.SideEffectType.DATAFLOW_SIDE_EFFECTING


def _in_hbm(a):
    return pltpu.with_memory_space_constraint(a, pltpu.HBM)


def _hbm_like(a):
    return pltpu.HBM(a.shape, a.dtype)


def _ici_copies(srcs, lands, send_sem, recv_sem, src_at, dst_at, wait_at, to_sibling=False):
    x, y, c, chips = _place()
    peers = [(x, y, 1 - c)] if to_sibling else [(cx, cy, c) for cx, cy in chips]
    m = len(peers)
    start, wait = [], []
    for i in range(len(srcs)):
        for k, (px, py, pc) in enumerate(peers):
            sem = dict(send_sem=send_sem.at[m * i + k], recv_sem=recv_sem.at[m * i + k],
                       device_id=(px, py, pc), device_id_type=_MESH)
            src = src_at(srcs[i], 2 * px + py, c)
            start.append(pltpu.make_async_remote_copy(src_ref=src, dst_ref=dst_at(lands[i], 2 * x + y, k, c), **sem))
            wait.append(pltpu.make_async_remote_copy(src_ref=src, dst_ref=wait_at(lands[i], 2 * px + py, k, c), **sem))
    return start, wait


def _route_peers(route):
    return 1 if len(route) == 4 else 3


_BLOCK_ROUTE = (lambda s, j, c: s, lambda l, me, k, c: l.at[me, c], lambda l, j, k, c: l.at[j, c])


def _ici_start(srcs, lands, token, route, *, name):
    n = len(srcs)

    def body(*refs):
        start, _ = _ici_copies(refs[:n], refs[n:2 * n], refs[2 * n + 1], refs[2 * n + 2], *route)
        for cp in start:
            cp.start()

    sems = pltpu.SemaphoreType.DMA((_route_peers(route) * n,))
    outs = pl.pallas_call(
        body, name=name,
        out_shape=(sems, sems, *[_hbm_like(a) for a in srcs], *[_hbm_like(a) for a in lands], _hbm_like(token)),
        in_specs=[_HBM] * (2 * n + 1), out_specs=(_SEM, _SEM, *[_HBM] * (2 * n + 1)),
        input_output_aliases={i: 2 + i for i in range(2 * n + 1)},
        compiler_params=pltpu.CompilerParams(has_side_effects=_SIDE),
    )(*[_in_hbm(a) for a in srcs], *[_in_hbm(a) for a in lands], _in_hbm(token))
    return (outs[0], outs[1], list(outs[2:2 + n]), list(outs[2 + n:2 + 2 * n])), outs[2 + 2 * n]


def _ici_wait(handle, after, route, *, name):
    send_sem, recv_sem, srcs, lands = handle
    n = len(srcs)
    after = list(after) if isinstance(after, (list, tuple)) else [after]

    def body(*refs):
        _, wait = _ici_copies(refs[:n], refs[n:2 * n], refs[2 * n], refs[2 * n + 1], *route)
        for cp in wait:
            cp.wait_send()
            cp.wait_recv()

    outs = pl.pallas_call(
        body, name=name,
        out_shape=(*[_hbm_like(a) for a in srcs], *[_hbm_like(a) for a in lands]),
        in_specs=[_HBM] * (2 * n) + [_SEM, _SEM] + [_ANY] * len(after), out_specs=tuple([_HBM] * (2 * n)),
        input_output_aliases={i: i for i in range(2 * n)},
        compiler_params=pltpu.CompilerParams(has_side_effects=_SIDE),
    )(*srcs, *lands, send_sem, recv_sem, *after)
    return list(outs[:n]), list(outs[n:])


_GATHER_ROUTE = (lambda s, j, c: s.at[c], lambda l, me, k, c: l.at[me, c], lambda l, j, k, c: l.at[j, c])
_SCATTER_ROUTE = (lambda s, j, c: s.at[j], lambda l, me, k, c: l.at[k], lambda l, j, k, c: l.at[k])
_SHARE_ROUTE = (lambda s, j, c: s, lambda l, me, k, c: l.at[c], lambda l, j, k, c: l.at[1 - c], True)
_SWAP_ROUTE = (lambda s, j, c: s.at[:, 1 - c], lambda l, me, k, c: l, lambda l, j, k, c: l, True)


def _gather_forward(lands, tag, own=False):
    n = len(lands)
    m = 4 if own else 3

    def body(*refs):
        ins, outs = refs[:n], refs[n:2 * n]
        send_sem, recv_sem = refs[2 * n:]
        x, y, c, chips = _place()
        slots = [2 * cx + cy for cx, cy in chips] + [2 * x + y]

        def copy(i, k, half):
            return pltpu.make_async_remote_copy(
                src_ref=ins[i].at[slots[k], half], dst_ref=outs[i].at[slots[k], half],
                send_sem=send_sem.at[m * i + k], recv_sem=recv_sem.at[m * i + k],
                device_id=(x, y, 1 - c), device_id_type=_MESH)

        copies = [copy(i, k, c) for i in range(n) for k in range(m)]
        for cp in copies:
            cp.start()
        for i in range(n):
            for k in range(m):
                copy(i, k, 1 - c).wait_recv()
        for cp in copies:
            cp.wait_send()

    return pl.pallas_call(
        body, name="gather_forward_to_sibling_" + tag,
        out_shape=[jax.ShapeDtypeStruct(a.shape, a.dtype) for a in lands],
        in_specs=[_ANY] * n, out_specs=[_ANY] * n,
        input_output_aliases={i: i for i in range(n)},
        scratch_shapes=[pltpu.SemaphoreType.DMA((m * n,)), pltpu.SemaphoreType.DMA((m * n,))],
    )(*lands)


def _swap_halves(grads, tag):
    n = len(grads)

    def body(*refs):
        ins, outs = refs[:n], refs[n:2 * n]
        send_sem, recv_sem = refs[2 * n:]
        x, y, c, _ = _place()
        copies = [pltpu.make_async_remote_copy(
            src_ref=ins[i].at[:, 1 - c], dst_ref=outs[i],
            send_sem=send_sem.at[i], recv_sem=recv_sem.at[i],
            device_id=(x, y, 1 - c), device_id_type=_MESH) for i in range(n)]
        for cp in copies:
            cp.start()
        for cp in copies:
            cp.wait()

    return pl.pallas_call(
        body, name="grad_swap_halves_" + tag,
        out_shape=[jax.ShapeDtypeStruct((N_CHIPS,) + g.shape[2:], g.dtype) for g in grads],
        in_specs=[_ANY] * n, out_specs=[_ANY] * n,
        scratch_shapes=[pltpu.SemaphoreType.DMA((n,)), pltpu.SemaphoreType.DMA((n,))],
    )(*grads)


def _sum_rows(h, C):
    return max(d for d in range(SUBLANES, h + 1, SUBLANES) if h % d == 0 and d * C <= 1 << 20)


SUM_STEPS = 4


def _pair_sums(gs, rs, c_idx, *, name):
    n = len(gs)
    rows = [g.shape[2] // SUM_STEPS for g in gs]

    def body(c_ref, *refs):
        for g_ref, r_ref, o_ref in zip(refs[:n], refs[n:2 * n], refs[2 * n:]):
            o_ref[...] = (g_ref[...].astype(F32) + r_ref[...].astype(F32)).astype(o_ref.dtype)

    return pl.pallas_call(
        body, name=name,
        out_shape=[jax.ShapeDtypeStruct((N_CHIPS,) + g.shape[2:], g.dtype) for g in gs],
        grid_spec=pltpu.PrefetchScalarGridSpec(
            num_scalar_prefetch=1, grid=(N_CHIPS, SUM_STEPS),
            in_specs=[pl.BlockSpec((None, None, tr, g.shape[3]), lambda j, i, s: (j, s[0], i, 0))
                      for g, tr in zip(gs, rows)]
            + [pl.BlockSpec((None, tr, g.shape[3]), lambda j, i, s: (j, i, 0)) for g, tr in zip(gs, rows)],
            out_specs=[pl.BlockSpec((None, tr, g.shape[3]), lambda j, i, s: (j, i, 0)) for g, tr in zip(gs, rows)]),
        compiler_params=_params("parallel", "parallel"),
    )(c_idx, *gs, *rs)


def _owner_sums(ss, rs, jc_idx, *, name):
    n = len(ss)
    rows = [s.shape[1] // SUM_STEPS for s in ss]

    def body(jc_ref, *refs):
        for s_ref, r_ref, m_ref, o_ref in zip(refs[:n], refs[n:2 * n], refs[2 * n:3 * n], refs[3 * n:]):
            acc = s_ref[...].astype(F32)
            for k in range(3):
                acc = acc + r_ref[k].astype(F32)
            m_ref[...] = acc
            o_ref[...] = acc

    outs = pl.pallas_call(
        body, name=name,
        out_shape=[jax.ShapeDtypeStruct(s.shape[1:], F32) for s in ss]
        + [jax.ShapeDtypeStruct((2,) + s.shape[1:], F32) for s in ss],
        grid_spec=pltpu.PrefetchScalarGridSpec(
            num_scalar_prefetch=1, grid=(SUM_STEPS,),
            in_specs=[pl.BlockSpec((None, tr, s.shape[2]), lambda i, p: (p[0], i, 0)) for s, tr in zip(ss, rows)]
            + [pl.BlockSpec((3, tr, s.shape[2]), lambda i, p: (0, i, 0)) for s, tr in zip(ss, rows)],
            out_specs=[pl.BlockSpec((tr, s.shape[2]), lambda i, p: (i, 0)) for s, tr in zip(ss, rows)]
            + [pl.BlockSpec((None, tr, s.shape[2]), lambda i, p: (p[1], i, 0)) for s, tr in zip(ss, rows)]),
        compiler_params=_params("parallel"),
    )(jc_idx, *ss, *rs)
    return outs[:n], outs[n:]


def _chip_sums(grads, c_idx, tag):
    views = [g.reshape(N_CHIPS, 2, g.shape[1] // 2, g.shape[2]) for g in grads]
    arrived = _swap_halves(views, tag)
    return _pair_sums(views, arrived, c_idx, name=f"grad_pair_sums_{tag}")


def _sum_devices(blocks):
    R = blocks.shape[2]
    tr = _sum_rows(R, 2 * N_CHIPS * LANES)

    def body(b_ref, o_ref):
        acc = b_ref[0, 0]
        for d in range(1, 2 * N_CHIPS):
            acc = acc + b_ref[d // 2, d % 2]
        o_ref[...] = acc

    return pl.pallas_call(
        body, name="sum_small_over_devices", out_shape=jax.ShapeDtypeStruct((R, LANES), F32),
        grid=(R // tr,),
        in_specs=[pl.BlockSpec((N_CHIPS, 2, tr, LANES), lambda i: (0, 0, i, 0))],
        out_specs=pl.BlockSpec((tr, LANES), lambda i: (i, 0)),
        compiler_params=_params("parallel"),
    )(blocks)


def _adamw(w, g, m, v, *, name):
    R, C = w.shape
    tr = max(d for d in range(SUBLANES, R + 1, SUBLANES)
             if R % d == 0 and 7 * 2 * d * C * 4 <= VMEM_LIMIT_BYTES // 2)

    def body(w_ref, g_ref, m_ref, v_ref, d_ref, nm_ref, nv_ref):
        g = g_ref[...]
        m = ADAM_B1 * m_ref[...] + (1.0 - ADAM_B1) * g
        v = ADAM_B2 * v_ref[...] + (1.0 - ADAM_B2) * (g * g)
        nm_ref[...] = m
        nv_ref[...] = v
        m_hat = m / (1.0 - ADAM_B1 ** ADAM_STEP)
        v_hat = v / (1.0 - ADAM_B2 ** ADAM_STEP)
        d_ref[...] = -ADAM_LR * (m_hat / (jnp.sqrt(v_hat) + ADAM_EPS) + ADAM_WD * w_ref[...])

    blk = pl.BlockSpec((tr, C), lambda i: (i, 0))
    sds = jax.ShapeDtypeStruct((R, C), F32)
    return pl.pallas_call(
        body, name=name, out_shape=(sds, sds, sds), grid=(R // tr,),
        in_specs=[blk] * 4, out_specs=(blk, blk, blk),
        compiler_params=_params("parallel"),
    )(w, g, m, v)


_TILE = SUBLANES * LANES


def _pack(arrays):
    rows = []
    for a in arrays:
        flat = a.reshape(-1)
        flat = jnp.pad(flat, (0, (-flat.shape[0]) % _TILE))
        rows.append(flat.reshape(-1, LANES))
    return jnp.concatenate(rows, axis=0)


def _unpack(buf, shapes):
    out, r = [], 0
    for s in shapes:
        size = math.prod(s)
        nr = -(-size // _TILE) * SUBLANES
        out.append(buf[r:r + nr].reshape(-1)[:size].reshape(s))
        r += nr
    return out


_BIG = ("w_in_a", "w_glu", "w_kv", "w_in_b", "w_mem_kv", "w_out")
_REPLICATED = ("pre_norm_g", "post_norm_g", "lam_re", "lam_im", "log_step", "b_re", "b_im", "c_re", "c_im",
               "kv_norm_g", "b_fgate", "mem_norm_g")
_SHARDED_SMALL = ("d_skip", "b_glu", "w_fgate")
_WEIGHTS = ("pre_norm_g", "post_norm_g", "w_in_a", "lam_re", "lam_im", "log_step", "b_re", "b_im", "c_re",
            "c_im", "d_skip", "w_glu", "b_glu", "kv_norm_g", "w_kv", "w_fgate", "b_fgate", "w_in_b",
            "mem_norm_g", "w_mem_kv", "w_out")


def _halves(a):
    return a.reshape(2, a.shape[0] // 2, a.shape[1])


def _unhalve(a):
    return a.reshape(N_CHIPS, 2 * a.shape[2], a.shape[3])


def _columns(a):
    return jnp.transpose(a, (1, 0, 2)).reshape(a.shape[1], N_CHIPS * a.shape[2])


def kernel(x, mem, pre_norm_g, post_norm_g, w_in_a, lam_re, lam_im, log_step, b_re, b_im, c_re, c_im, d_skip, w_glu, b_glu, kv_norm_g, w_kv, w_fgate, b_fgate, w_in_b, mem_norm_g, w_mem_kv, w_out, loss_target, m_pre_norm_g, m_post_norm_g, m_w_in_a, m_lam_re, m_lam_im, m_log_step, m_b_re, m_b_im, m_c_re, m_c_im, m_d_skip, m_w_glu, m_b_glu, m_kv_norm_g, m_w_kv, m_w_fgate, m_b_fgate, m_w_in_b, m_mem_norm_g, m_w_mem_kv, m_w_out, v_pre_norm_g, v_post_norm_g, v_w_in_a, v_lam_re, v_lam_im, v_log_step, v_b_re, v_b_im, v_c_re, v_c_im, v_d_skip, v_w_glu, v_b_glu, v_kv_norm_g, v_w_kv, v_w_fgate, v_b_fgate, v_w_in_b, v_mem_norm_g, v_w_mem_kv, v_w_out):
    a = dict(locals())
    xi, yi, ci = lax.axis_index("x"), lax.axis_index("y"), lax.axis_index("c")
    chip = 2 * xi + yi
    c_idx = jnp.reshape(ci, (1,)).astype(jnp.int32)
    jc_idx = jnp.stack([chip, ci]).astype(jnp.int32)

    vec = jnp.zeros((2 * SUBLANES, MAIN_WIDTH // N_CHIPS), F32)
    vec = vec.at[0].set(a["d_skip"][0]).at[1].set(a["b_glu"][0])
    def own_slot(gathered, parts):
        return [lax.dynamic_update_index_in_dim(g, p, chip, 0) for g, p in zip(gathered, parts)]

    parts_a = [_halves(a["w_in_a"][0].astype(BF16)), _halves(vec)]
    parts_b = [_halves(a["w_glu"][0].astype(BF16)),
               *[_halves(a["w_mem_kv"][i].astype(BF16)) for i in range(2)],
               *[_halves(a["w_out"][i].astype(BF16)) for i in range(2)]]
    parts_c = [_halves(a["w_kv"].astype(BF16)), _halves(_pad_lanes(a["w_fgate"]).astype(BF16)),
               _halves(a["w_in_b"][0].astype(BF16))]
    travelling, token = {}, a["pre_norm_g"]
    for tag, parts in (("a", parts_a), ("b", parts_b), ("c", parts_c)):
        lands = [lax.empty((N_CHIPS,) + p.shape, p.dtype) for p in parts]
        travelling[tag], token = _ici_start(parts, lands, token, _GATHER_ROUTE, name=f"gather_{tag}_start")

    def fetch(tag, after):
        parts, lands = _ici_wait(travelling[tag], after, _GATHER_ROUTE, name=f"gather_{tag}_wait")
        full = own_slot(_gather_forward(lands, tag), parts)
        if tag == "a":
            w_in_a, vecs = full
            return dict(w_in_a=_columns(_unhalve(w_in_a)), d_skip=vecs[:, 0, 0, :].reshape(MAIN_WIDTH),
                        b_glu=vecs[:, 0, 1, :].reshape(MAIN_WIDTH))
        if tag == "b":
            w_glu, w_mk0, w_mk1, w_out0, w_out1 = full
            return dict(w_glu=w_glu.reshape(MAIN_WIDTH, MAIN_WIDTH),
                        w_mem_kv=[m.reshape(D_MODEL, 2 * MEM_WIDTH) for m in (w_mk0, w_mk1)],
                        w_out=[o.reshape(D_MODEL, D_MODEL) for o in (w_out0, w_out1)])
        w_kv, w_fg, w_in_b = full
        return dict(w_kv=_columns(_unhalve(w_kv)), w_fgate=w_fg.reshape(D_MODEL, LANES),
                    w_in_b=_columns(_unhalve(w_in_b)))

    w = dict(
        pre_norm_g=token, post_norm_g=a["post_norm_g"], mem_norm_g=a["mem_norm_g"],
        kv_norm_g=a["kv_norm_g"], b_fgate=a["b_fgate"],
        lam_re=a["lam_re"][0], lam_im=a["lam_im"][0], log_step=a["log_step"][0],
        b_re=a["b_re"][0], b_im=a["b_im"][0], c_re=a["c_re"][0], c_im=a["c_im"][0])

    sent = {}

    swapping = {}

    def grads_ready(event, g, token):
        tag = event.split("_")[0]
        if event in ("b", "a1"):
            big = {"b": lambda: [g["w_kv"], g["w_in_b"], g["w_mem_kv_1"].reshape(N_CHIPS, -1, 2 * MEM_WIDTH),
                                 g["w_out_1"].reshape(N_CHIPS, -1, D_MODEL)],
                   "a1": lambda: [g["w_glu"].reshape(N_CHIPS, -1, MAIN_WIDTH),
                                  g["w_mem_kv_0"].reshape(N_CHIPS, -1, 2 * MEM_WIDTH),
                                  g["w_out_0"].reshape(N_CHIPS, -1, D_MODEL)]}[tag]()
            views = [b.reshape(N_CHIPS, 2, b.shape[1] // 2, b.shape[2]) for b in big]
            lands = [lax.empty((N_CHIPS,) + v.shape[2:], v.dtype) for v in views]
            swapping[tag], token = _ici_start(views, lands, token, _SWAP_ROUTE, name=f"grad_swap_{tag}_start")
            return token
        if event == "a2":
            sums = _chip_sums([g["w_in_a"]], c_idx, tag)
        else:
            views, arrived = _ici_wait(swapping[tag], token, _SWAP_ROUTE, name=f"grad_swap_{tag}_wait")
            sums = _pair_sums(views, arrived, c_idx, name=f"grad_pair_sums_{tag}")
        lands = [lax.empty((3,) + s.shape[1:], s.dtype) for s in sums]
        sent[tag], token = _ici_start(sums, lands, token, _SCATTER_ROUTE, name=f"grad_send_{tag}_start")
        return token

    loss_row, grad_x, g = _local_step(a["x"][0], a["mem"][0], a["loss_target"][0], w, fetch, grads_ready)

    small_names = _REPLICATED + _SHARDED_SMALL
    pack = _pack([g[n] for n in small_names])
    blocks = lax.empty((N_CHIPS, 2) + pack.shape, F32)
    small_sent, token = _ici_start([pack], [blocks], loss_row, _BLOCK_ROUTE, name="small_sums_start")

    sharing = {}
    for tag in ("b", "a1", "a2"):
        sums, arrived = _ici_wait(sent[tag], [grad_x, token], _SCATTER_ROUTE, name=f"grad_send_{tag}_wait")
        mine, bufs = _owner_sums(sums, arrived, jc_idx, name=f"grad_owner_sums_{tag}")
        sharing[tag], token = _ici_start(mine, bufs, token, _SHARE_ROUTE, name=f"grad_share_{tag}_start")
    loss = lax.psum(jnp.sum(token), MESH_AXES)

    def shared(tag, after):
        _, bufs = _ici_wait(sharing[tag], after, _SHARE_ROUTE, name=f"grad_share_{tag}_wait")
        return [b.reshape(-1, b.shape[2]) for b in bufs]

    grads, delta, new_m, new_v = {}, {}, {}, {}

    def adam(n):
        shape = a[n].shape
        d2 = (-1, shape[-1])
        d, m, v = _adamw(a[n].reshape(d2), grads[n].reshape(d2), a["m_" + n].reshape(d2),
                         a["v_" + n].reshape(d2), name="adamw_" + n)
        delta[n], new_m[n], new_v[n] = d.reshape(shape), m.reshape(shape), v.reshape(shape)
        return d

    r_kv, r_in_b, r_mk1, r_out1 = shared("b", token)
    grads["w_kv"], grads["w_in_b"] = r_kv, r_in_b[None]
    done = [adam("w_kv"), adam("w_in_b")]
    r_glu, r_mk0, r_out0 = shared("a1", done)
    grads["w_glu"], grads["w_mem_kv"], grads["w_out"] = r_glu[None], jnp.stack([r_mk0, r_mk1]), jnp.stack([r_out0, r_out1])
    done = [adam("w_glu"), adam("w_mem_kv"), adam("w_out")]
    (r_in_a,) = shared("a2", done)
    grads["w_in_a"] = r_in_a[None]
    adam("w_in_a")

    (pack,), (blocks,) = _ici_wait(small_sent, [delta[n] for n in _BIG], _BLOCK_ROUTE, name="small_sums_wait")
    blocks = lax.dynamic_update_slice(blocks, pack[None, None], (chip, ci, 0, 0))
    (blocks,) = _gather_forward([blocks], "small", own=True)
    small = dict(zip(small_names, _unpack(_sum_devices(blocks), [g[n].shape for n in small_names])))
    for n in _REPLICATED:
        grads[n] = small[n].reshape(a[n].shape)
    nd = MAIN_WIDTH // N_CHIPS
    grads["d_skip"] = lax.dynamic_slice(small["d_skip"], (chip * nd,), (nd,))[None]
    grads["b_glu"] = lax.dynamic_slice(small["b_glu"], (chip * nd,), (nd,))[None]
    nf = D_MODEL // N_CHIPS
    grads["w_fgate"] = lax.dynamic_slice(small["w_fgate"], (chip * nf, 0), (nf, FOX_HEADS))

    shapes = [a[n].shape for n in small_names]
    d, m, v = _adamw(_pack([a[n] for n in small_names]), _pack([grads[n] for n in small_names]),
                     _pack([a["m_" + n] for n in small_names]), _pack([a["v_" + n] for n in small_names]),
                     name="adamw_small")
    for n, dd, mm, vv in zip(small_names, _unpack(d, shapes), _unpack(m, shapes), _unpack(v, shapes)):
        delta[n], new_m[n], new_v[n] = dd, mm, vv

    return (loss, grad_x[None], *[grads[n] for n in _WEIGHTS], *[delta[n] for n in _WEIGHTS],
            *[new_m[n] for n in _WEIGHTS], *[new_v[n] for n in _WEIGHTS])
```

```python
import math

import jax
import jax.numpy as jnp
from jax import lax
from jax.experimental import pallas as pl
from jax.experimental.pallas import tpu as pltpu

F32 = jnp.float32
BF16 = jnp.bfloat16

D_MODEL = 2048
N_MEM = 256
MAIN_WIDTH = 1536
MEM_WIDTH = 512
IN_WIDTH = 2 * MAIN_WIDTH + 2 * MEM_WIDTH
HEAD_DIM = 128
FOX_HEADS = MAIN_WIDTH // HEAD_DIM
MEM_HEADS = MEM_WIDTH // HEAD_DIM
SSM_GROUP = 16
SSM_GROUPS = MAIN_WIDTH // SSM_GROUP
SSM_STATE = 64
GROUPS_PER_BLOCK = 8
SSM_BLOCKS = SSM_GROUPS // GROUPS_PER_BLOCK
STATE_COLS = GROUPS_PER_BLOCK * SSM_STATE
EPS = 1e-6
ADAM_LR = 0.001
ADAM_B1 = 0.9
ADAM_B2 = 0.999
ADAM_EPS = 1e-08
ADAM_WD = 0.01
ADAM_STEP = 10
N_CHIPS = 4
LANES = 128
SUBLANES = 8
VMEM_LIMIT_BYTES = 56 * 1024 * 1024
NEG_BIG = -1e30
MESH_AXES = ("x", "y", "c")


def _params(*sem):
    return pltpu.CompilerParams(dimension_semantics=sem if sem else None,
                                vmem_limit_bytes=VMEM_LIMIT_BYTES)


def _sigmoid(x):
    return 1.0 / (1.0 + jnp.exp(-x))


def _gelu(x):
    c = math.sqrt(2.0 / math.pi)
    return 0.5 * x * (1.0 + jnp.tanh(c * (x + 0.044715 * (x * x * x))))


def _gelu_grad(x):
    c = math.sqrt(2.0 / math.pi)
    t = jnp.tanh(c * (x + 0.044715 * (x * x * x)))
    return 0.5 * (1.0 + t) + 0.5 * x * (1.0 - t * t) * (c * (1.0 + 3.0 * 0.044715 * (x * x)))


def _silu_and_grad(z):
    s = _sigmoid(z)
    return z * s, s * (1.0 + z * (1.0 - s))


_TILE_CHOICES = (4096, 3072, 2048, 1536, 1024, 768, 512, 384, 256, LANES)


def _tile(n, cap):
    return next(c for c in _TILE_CHOICES if c <= cap and n % c == 0)


def _mm(a, b, *, name, ta=False, tb=False, out_dtype=F32, shards=1, tm=1024, tn=1024, tk=4096):
    if ta:
        K, M = a.shape
    else:
        M, K = a.shape
    if tb:
        N, kb = b.shape
    else:
        kb, N = b.shape
    assert K == kb, (a.shape, b.shape)
    ns = N // shards
    tm, tn, tk = _tile(M, tm), _tile(ns, tn), _tile(K, tk)
    assert M % tm == 0 and ns % tn == 0 and K % tk == 0 and N % shards == 0
    nk = K // tk
    dn = (((0 if ta else 1,), (1 if tb else 0,)), ((), ()))

    def body(a_ref, b_ref, o_ref, *acc):
        prod = lax.dot_general(a_ref[...].astype(BF16), b_ref[...].astype(BF16), dn, preferred_element_type=F32)
        if nk == 1:
            o_ref[...] = prod.astype(o_ref.dtype)
            return
        acc_ref, = acc
        k = pl.program_id(2)

        @pl.when(k == 0)
        def _():
            acc_ref[...] = jnp.zeros_like(acc_ref)

        acc_ref[...] += prod

        @pl.when(k == nk - 1)
        def _():
            o_ref[...] = acc_ref[...].astype(o_ref.dtype)

    a_spec = (pl.BlockSpec((tk, tm), lambda i, j, k: (k, i)) if ta
              else pl.BlockSpec((tm, tk), lambda i, j, k: (i, k)))
    b_spec = (pl.BlockSpec((tn, tk), lambda i, j, k: (j, k)) if tb
              else pl.BlockSpec((tk, tn), lambda i, j, k: (k, j)))
    if shards == 1:
        out_shape = jax.ShapeDtypeStruct((M, N), out_dtype)
        o_spec = pl.BlockSpec((tm, tn), lambda i, j, k: (i, j))
    else:
        nb = ns // tn
        out_shape = jax.ShapeDtypeStruct((shards, M, ns), out_dtype)
        o_spec = pl.BlockSpec((None, tm, tn), lambda i, j, k: (j // nb, i, j % nb))
    return pl.pallas_call(
        body, name=name, out_shape=out_shape,
        grid=(M // tm, N // tn, nk),
        in_specs=[a_spec, b_spec], out_specs=o_spec,
        scratch_shapes=[] if nk == 1 else [pltpu.VMEM((tm, tn), F32)],
        compiler_params=_params("parallel", "parallel", "arbitrary"),
    )(a, b)


def _rmsnorm_fwd(x, g, *, name, res=None, out_dtype=F32, tr=256):
    L, D = x.shape
    tr = min(tr, L)
    has_res = res is not None

    def body(*refs):
        if has_res:
            x_ref, g_ref, r_ref, o_ref = refs
        else:
            x_ref, g_ref, o_ref = refs
        xf = x_ref[...]
        r = lax.rsqrt(jnp.mean(xf * xf, axis=-1, keepdims=True) + EPS)
        y = xf * r * g_ref[...]
        if has_res:
            y = r_ref[...] + y
        o_ref[...] = y.astype(o_ref.dtype)

    row = pl.BlockSpec((tr, D), lambda i: (i, 0))
    vec = pl.BlockSpec((1, D), lambda i: (0, 0))
    ins = [x, g.reshape(1, D)] + ([res] if has_res else [])
    return pl.pallas_call(
        body, name=name, out_shape=jax.ShapeDtypeStruct((L, D), out_dtype),
        grid=(L // tr,), in_specs=[row, vec] + ([row] if has_res else []), out_specs=row,
        compiler_params=_params("parallel"),
    )(*ins)


def _rmsnorm_bwd(x, g, dy, *, name, adds=(), dx_dtype=F32, tr=256):
    L, D = x.shape
    tr = min(tr, L)
    dys = dy if isinstance(dy, tuple) else (dy,)
    n_dy, n_add = len(dys), len(adds)

    def body(*refs):
        x_ref, g_ref = refs[:2]
        dy_refs = refs[2:2 + n_dy]
        add_refs = refs[2 + n_dy:2 + n_dy + n_add]
        dx_ref, dg_ref = refs[2 + n_dy + n_add:]
        xf = x_ref[...]
        dyf = dy_refs[0][...].astype(F32)
        for d_ref in dy_refs[1:]:
            dyf = dyf + d_ref[...].astype(F32)
        r = lax.rsqrt(jnp.mean(xf * xf, axis=-1, keepdims=True) + EPS)
        gy = dyf * g_ref[...]
        c = jnp.mean(xf * gy, axis=-1, keepdims=True) * (r * r * r)
        dx = gy * r - xf * c
        for a_ref in add_refs:
            dx = dx + a_ref[...].astype(F32)
        dx_ref[...] = dx.astype(dx_ref.dtype)

        @pl.when(pl.program_id(0) == 0)
        def _():
            dg_ref[...] = jnp.zeros_like(dg_ref)

        dg_ref[...] += jnp.sum(dyf * xf * r, axis=0, keepdims=True)

    row = pl.BlockSpec((tr, D), lambda i: (i, 0))
    vec = pl.BlockSpec((1, D), lambda i: (0, 0))
    dx, dg = pl.pallas_call(
        body, name=name,
        out_shape=(jax.ShapeDtypeStruct((L, D), dx_dtype), jax.ShapeDtypeStruct((1, D), F32)),
        grid=(L // tr,), in_specs=[row, vec] + [row] * (n_dy + n_add), out_specs=(row, vec),
        compiler_params=_params("arbitrary"),
    )(x, g.reshape(1, D), *dys, *adds)
    return dx, dg.reshape(D)


def _rmsnorm_bwd_pair(x, g1, dy1, g2, dy2, *, name, adds=(), tr=256):
    L, D = x.shape
    tr = min(tr, L)
    dy1s = dy1 if isinstance(dy1, tuple) else (dy1,)
    n1, n_add = len(dy1s), len(adds)

    def body(*refs):
        x_ref, g1_ref, g2_ref = refs[:3]
        dy1_refs = refs[3:3 + n1]
        dy2_ref = refs[3 + n1]
        add_refs = refs[4 + n1:4 + n1 + n_add]
        dx_ref, dg1_ref, dg2_ref = refs[4 + n1 + n_add:]
        xf = x_ref[...]
        d1 = dy1_refs[0][...].astype(F32)
        for d_ref in dy1_refs[1:]:
            d1 = d1 + d_ref[...].astype(F32)
        d2 = dy2_ref[...].astype(F32)
        r = lax.rsqrt(jnp.mean(xf * xf, axis=-1, keepdims=True) + EPS)
        gy = d1 * g1_ref[...] + d2 * g2_ref[...]
        c = jnp.mean(xf * gy, axis=-1, keepdims=True) * (r * r * r)
        dx = gy * r - xf * c
        for a_ref in add_refs:
            dx = dx + a_ref[...].astype(F32)
        dx_ref[...] = dx

        @pl.when(pl.program_id(0) == 0)
        def _():
            dg1_ref[...] = jnp.zeros_like(dg1_ref)
            dg2_ref[...] = jnp.zeros_like(dg2_ref)

        xr = xf * r
        dg1_ref[...] += jnp.sum(d1 * xr, axis=0, keepdims=True)
        dg2_ref[...] += jnp.sum(d2 * xr, axis=0, keepdims=True)

    row = pl.BlockSpec((tr, D), lambda i: (i, 0))
    vec = pl.BlockSpec((1, D), lambda i: (0, 0))
    dx, dg1, dg2 = pl.pallas_call(
        body, name=name,
        out_shape=(jax.ShapeDtypeStruct((L, D), F32), jax.ShapeDtypeStruct((1, D), F32),
                   jax.ShapeDtypeStruct((1, D), F32)),
        grid=(L // tr,), in_specs=[row, vec, vec] + [row] * (n1 + 1 + n_add), out_specs=(row, vec, vec),
        compiler_params=_params("arbitrary"),
    )(x, g1.reshape(1, D), g2.reshape(1, D), *dy1s, dy2, *adds)
    return dx, dg1.reshape(D), dg2.reshape(D)


def _final_norm_loss(o, g, res, target, *, tr=256):
    L, D = o.shape
    tr = min(tr, L)

    def body(o_ref, g_ref, r_ref, t_ref, dh_ref, loss_ref):
        xf = o_ref[...]
        r = lax.rsqrt(jnp.mean(xf * xf, axis=-1, keepdims=True) + EPS)
        e = (r_ref[...] + xf * r * g_ref[...]) - t_ref[...]
        dh_ref[...] = e * (1.0 / D)

        @pl.when(pl.program_id(0) == 0)
        def _():
            loss_ref[...] = jnp.zeros_like(loss_ref)

        loss_ref[...] += jnp.sum(e * e, axis=0, keepdims=True) * (0.5 / D)

    row = pl.BlockSpec((tr, D), lambda i: (i, 0))
    vec = pl.BlockSpec((1, D), lambda i: (0, 0))
    dh, lp = pl.pallas_call(
        body, name="post_norm_1_loss",
        out_shape=(jax.ShapeDtypeStruct((L, D), F32), jax.ShapeDtypeStruct((1, D), F32)),
        grid=(L // tr,), in_specs=[row, vec, row, row], out_specs=(row, vec),
        compiler_params=_params("arbitrary"),
    )(o, g.reshape(1, D), res, target)
    return dh, lp


def _s5_coeffs(lr, li, ls):
    dt = jnp.exp(ls)
    mag = jnp.exp(lr * dt)
    ar = mag * jnp.cos(li * dt)
    ai = mag * jnp.sin(li * dt)
    den = lr * lr + li * li
    cr = ((ar - 1.0) * lr + ai * li) / den
    ci = (ai * lr - (ar - 1.0) * li) / den
    return dt, ar, ai, den, cr, ci


def _s5_prep(lam_re, lam_im, log_step, b_re_t, b_im_t):
    G, P = lam_re.shape
    H = b_re_t.shape[1]

    def body(lr_ref, li_ref, ls_ref, br_ref, bi_ref, ar_ref, ai_ref, bbr_ref, bbi_ref):
        _, ar, ai, _, cr, ci = _s5_coeffs(lr_ref[...], li_ref[...], ls_ref[...])
        ar_ref[...] = ar
        ai_ref[...] = ai
        br, bi = br_ref[...], bi_ref[...]
        crb, cib = cr[:, None, :], ci[:, None, :]
        bbr_ref[...] = crb * br - cib * bi
        bbi_ref[...] = crb * bi + cib * br

    return pl.pallas_call(
        body, name="s5_prep",
        out_shape=(jax.ShapeDtypeStruct((G, P), F32), jax.ShapeDtypeStruct((G, P), F32),
                   jax.ShapeDtypeStruct((G, H, P), F32), jax.ShapeDtypeStruct((G, H, P), F32)),
        compiler_params=_params(),
    )(lam_re, lam_im, log_step.reshape(G, 1), b_re_t, b_im_t)


def _s5_prep_bwd(lam_re, lam_im, log_step, b_re_t, b_im_t, d_ar, d_ai, d_bbr, d_bbi):
    G, P = lam_re.shape
    H = b_re_t.shape[1]

    def body(lr_ref, li_ref, ls_ref, br_ref, bi_ref, dar_ref, dai_ref, dbbr_ref, dbbi_ref,
             dlr_ref, dli_ref, dls_ref, dbr_ref, dbi_ref):
        lr, li = lr_ref[...], li_ref[...]
        dt, ar, ai, den, cr, ci = _s5_coeffs(lr, li, ls_ref[...])
        br, bi = br_ref[...], bi_ref[...]
        gbr, gbi = dbbr_ref[...], dbbi_ref[...]
        crb, cib = cr[:, None, :], ci[:, None, :]
        dbr_ref[...] = crb * gbr + cib * gbi
        dbi_ref[...] = crb * gbi - cib * gbr
        gcr = jnp.sum(br * gbr + bi * gbi, axis=1)
        gci = jnp.sum(br * gbi - bi * gbr, axis=1)
        ilr, ili = lr / den, -li / den
        gar = dar_ref[...] + (ilr * gcr + ili * gci)
        gai = dai_ref[...] + (ilr * gci - ili * gcr)
        qr, qi = cr * ilr - ci * ili, cr * ili + ci * ilr
        glr = -(qr * gcr + qi * gci)
        gli = -(qr * gci - qi * gcr)
        glr = glr + dt * (ar * gar + ai * gai)
        gli = gli + dt * (ar * gai - ai * gar)
        wr, wi = lr * ar - li * ai, lr * ai + li * ar
        gdt = jnp.sum(wr * gar + wi * gai, axis=1, keepdims=True)
        dlr_ref[...] = glr
        dli_ref[...] = gli
        dls_ref[...] = gdt * dt

    return pl.pallas_call(
        body, name="s5_prep_bwd",
        out_shape=(jax.ShapeDtypeStruct((G, P), F32), jax.ShapeDtypeStruct((G, P), F32),
                   jax.ShapeDtypeStruct((G, 1), F32),
                   jax.ShapeDtypeStruct((G, H, P), F32), jax.ShapeDtypeStruct((G, H, P), F32)),
        compiler_params=_params(),
    )(lam_re, lam_im, log_step.reshape(G, 1), b_re_t, b_im_t, d_ar, d_ai, d_bbr, d_bbi)


def _s5_block_mats(bbr_t, bbi_t, c_re, c_im):
    bmat = _s5_expand(bbr_t, bbi_t)
    cmat = jnp.transpose(_s5_expand(c_re, -c_im), (0, 2, 1))
    return bmat.astype(BF16), cmat.astype(BF16)


def _s5_diag_mask():
    r = lax.broadcasted_iota(jnp.int32, (LANES, 2 * STATE_COLS), 0) // SSM_GROUP
    c = (lax.broadcasted_iota(jnp.int32, (LANES, 2 * STATE_COLS), 1) % STATE_COLS) // SSM_STATE
    return (r == c).astype(F32)


def _s5_expand(re, im):
    re = jnp.tile(re.reshape(SSM_BLOCKS, LANES, SSM_STATE), (1, 1, GROUPS_PER_BLOCK))
    im = jnp.tile(im.reshape(SSM_BLOCKS, LANES, SSM_STATE), (1, 1, GROUPS_PER_BLOCK))
    return jnp.concatenate([re, im], axis=-1) * _s5_diag_mask()[None]


def _s5_unfold(dmat):
    d = dmat.reshape(SSM_GROUPS, SSM_GROUP, 2, SSM_STATE)
    return jnp.transpose(d, (2, 0, 1, 3))


def _s5_a_rows(ar, ai):
    a = jnp.concatenate([ar.reshape(SSM_BLOCKS, STATE_COLS), ai.reshape(SSM_BLOCKS, STATE_COLS)], axis=1)
    return jnp.broadcast_to(a[:, None, :], (SSM_BLOCKS, SUBLANES, 2 * STATE_COLS))


def _to_step_major(src_ref, dst_ref, seg):
    for s in range(SUBLANES):
        dst_ref[pl.ds(s, seg, stride=SUBLANES), :] = src_ref[pl.ds(seg * s, seg), :]


def _segment_rows(ref, s, seg):
    return ref[pl.ds(s, seg, stride=SUBLANES), :]


def _cmul(ar, ai, xr, xi):
    return ar * xr - ai * xi, ar * xi + ai * xr


def _s5_tables(a_ref, pw_s, pwr_s, S, seg):
    ar, ai = a_ref[:, :S], a_ref[:, S:]

    def step(i, c):
        pr, pi = c
        pw_s[i, :, :S] = pr
        pw_s[i, :, S:] = pi
        nr, ni = _cmul(ar, ai, pr, pi)
        pwr_s[seg - 1 - i, :, :S] = nr
        pwr_s[seg - 1 - i, :, S:] = ni
        return nr, ni

    pr, pi = lax.fori_loop(0, seg, step, (jnp.ones_like(ar), jnp.zeros_like(ai)))
    pw_s[seg, :, :S] = pr
    pw_s[seg, :, S:] = pi


def _s5_fwd(proj, bmat, cmat, a_rows, d_skip, *, tc=512):
    L = proj.shape[0]
    tc = min(tc, L)
    nt = L // tc
    seg = tc // SUBLANES
    S = STATE_COLS

    def body(u_ref, b_ref, c_ref, a_ref, d_ref, y_ref, yg_ref, xp_ref,
             bu_s, xp_s, pw_s, pwr_s, carry_s, e_s, up_s, yc_s):
        @pl.when(pl.program_id(1) == 0)
        def _():
            carry_s[...] = jnp.zeros_like(carry_s)
            _s5_tables(a_ref, pw_s, pwr_s, S, seg)

        ar, ai = a_ref[:, :S], a_ref[:, S:]
        _to_step_major(u_ref, up_s, seg)
        bu = jnp.dot(up_s[...].astype(BF16), b_ref[...], preferred_element_type=F32)
        bu_s[...] = bu.reshape(seg, SUBLANES, 2 * S)

        def step(i, carry):
            cr, ci = carry
            xp_s[i, :, :S] = cr
            xp_s[i, :, S:] = ci
            return ar * cr - ai * ci + bu_s[i, :, :S], ar * ci + ai * cr + bu_s[i, :, S:]

        zero = jnp.zeros((SUBLANES, S), F32)
        fr, fi = lax.fori_loop(0, seg, step, (zero, zero))
        pr, pi = pw_s[seg, 0:1, :S], pw_s[seg, 0:1, S:]
        er, ei = carry_s[0:1, :S], carry_s[0:1, S:]
        for s in range(SUBLANES):
            e_s[s:s + 1, :S] = er
            e_s[s:s + 1, S:] = ei
            tr, ti = _cmul(pr, pi, er, ei)
            er, ei = fr[s:s + 1] + tr, fi[s:s + 1] + ti
        carry_s[0:1, :S] = er
        carry_s[0:1, S:] = ei
        pw = pw_s[0:seg]
        tr, ti = _cmul(pw[:, :, :S], pw[:, :, S:], e_s[:, :S][None], e_s[:, S:][None])
        xl = xp_s[...]
        xp = jnp.concatenate([xl[:, :, :S] + tr, xl[:, :, S:] + ti], axis=-1).reshape(tc, 2 * S)
        xp_ref[...] = xp
        a1r, a1i = ar[0:1], ai[0:1]
        x_re = a1r * xp[:, :S] - a1i * xp[:, S:] + bu[:, :S]
        x_im = a1r * xp[:, S:] + a1i * xp[:, :S] + bu[:, S:]
        xs = jnp.concatenate([x_re, x_im], axis=1).astype(BF16)
        yc_s[...] = jnp.dot(xs, c_ref[...], preferred_element_type=F32)
        for s in range(SUBLANES):
            rows = pl.ds(seg * s, seg)
            y = _segment_rows(yc_s, s, seg) + d_ref[...] * u_ref[rows, :]
            y_ref[rows, :] = y
            yg_ref[rows, :] = _gelu(y).astype(BF16)

    return pl.pallas_call(
        body, name="s5_fwd",
        out_shape=(jax.ShapeDtypeStruct((L, MAIN_WIDTH), F32),
                   jax.ShapeDtypeStruct((L, MAIN_WIDTH), BF16),
                   jax.ShapeDtypeStruct((L, SSM_BLOCKS * 2 * S), F32)),
        grid=(SSM_BLOCKS, nt),
        in_specs=[pl.BlockSpec((tc, LANES), lambda b, t: (t, b)),
                  pl.BlockSpec((None, LANES, 2 * S), lambda b, t: (b, 0, 0)),
                  pl.BlockSpec((None, 2 * S, LANES), lambda b, t: (b, 0, 0)),
                  pl.BlockSpec((None, SUBLANES, 2 * S), lambda b, t: (b, 0, 0)),
                  pl.BlockSpec((1, LANES), lambda b, t: (0, b))],
        out_specs=(pl.BlockSpec((tc, LANES), lambda b, t: (t, b)),
                   pl.BlockSpec((tc, LANES), lambda b, t: (t, b)),
                   pl.BlockSpec((tc, 2 * S), lambda b, t: (t, b))),
        scratch_shapes=[pltpu.VMEM((seg, SUBLANES, 2 * S), F32),
                        pltpu.VMEM((seg, SUBLANES, 2 * S), F32),
                        pltpu.VMEM((seg + 1, SUBLANES, 2 * S), F32),
                        pltpu.VMEM((seg, SUBLANES, 2 * S), F32),
                        pltpu.VMEM((SUBLANES, 2 * S), F32),
                        pltpu.VMEM((SUBLANES, 2 * S), F32),
                        pltpu.VMEM((tc, LANES), F32),
                        pltpu.VMEM((tc, LANES), F32)],
        compiler_params=_params("parallel", "arbitrary"),
    )(proj, bmat, cmat, a_rows, d_skip.reshape(1, MAIN_WIDTH))


def _s5_bwd(proj, dyg_a, dyg_b, y, xp, bmat, cmat, a_rows, d_skip, dproj, *, tc=512):
    L = proj.shape[0]
    tc = min(tc, L)
    nt = L // tc
    seg = tc // SUBLANES
    S = STATE_COLS
    nn = (((1,), (1,)), ((), ()))
    tn = (((0,), (0,)), ((), ()))

    def fold_diagonal(acc_ref, mask_ref, fold_ref):
        x = acc_ref[...] * mask_ref[...]
        hi = x.astype(BF16)
        rest = x - hi.astype(F32)
        mid = rest.astype(BF16)
        low = (rest - mid.astype(F32)).astype(BF16)
        return sum(jnp.dot(piece, fold_ref[...], preferred_element_type=F32) for piece in (hi, mid, low))

    def body(u_ref, dyga_ref, dygb_ref, y_ref, xp_ref, b_ref, c_ref, a_ref, d_ref, mask_ref, fold_ref, dp_hbm,
             du_ref, dbd_ref, dcd_ref, da_ref, dd_ref,
             dl_s, pw_s, pwr_s, carry_s, e_s, up_s, dy_s, dyp_s, dup_s, db_ref, dc_ref):
        @pl.when(pl.program_id(1) == 0)
        def _():
            carry_s[...] = jnp.zeros_like(carry_s)
            db_ref[...] = jnp.zeros_like(db_ref)
            dc_ref[...] = jnp.zeros_like(dc_ref)
            da_ref[...] = jnp.zeros_like(da_ref)
            dd_ref[...] = jnp.zeros_like(dd_ref)
            _s5_tables(a_ref, pw_s, pwr_s, S, seg)

        ar, ai = a_ref[:, :S], a_ref[:, S:]
        a1r, a1i = ar[0:1], ai[0:1]
        u = u_ref[...]
        dy = (dyga_ref[...] + dygb_ref[...]) * _gelu_grad(y_ref[...])
        dy_s[...] = dy
        xp = xp_ref[...]
        _to_step_major(u_ref, up_s, seg)
        _to_step_major(dy_s, dyp_s, seg)
        ubp = up_s[...].astype(BF16)
        dyp = dyp_s[...].astype(BF16)
        bu = jnp.dot(ubp, b_ref[...], preferred_element_type=F32)
        x_re = a1r * xp[:, :S] - a1i * xp[:, S:] + bu[:, :S]
        x_im = a1r * xp[:, S:] + a1i * xp[:, :S] + bu[:, S:]
        xs = jnp.concatenate([x_re, x_im], axis=1).astype(BF16)
        dc_ref[...] += lax.dot_general(dyp, xs, tn, preferred_element_type=F32)
        dx = lax.dot_general(dyp, c_ref[...], nn, preferred_element_type=F32)
        dl_s[...] = dx.reshape(seg, SUBLANES, 2 * S)

        def step(k, carry):
            cr, ci = carry
            i = seg - 1 - k
            lr = dl_s[i, :, :S] + (ar * cr + ai * ci)
            li = dl_s[i, :, S:] + (ar * ci - ai * cr)
            dl_s[i, :, :S] = lr
            dl_s[i, :, S:] = li
            return lr, li

        zero = jnp.zeros((SUBLANES, S), F32)
        fr, fi = lax.fori_loop(0, seg, step, (zero, zero))
        pr, pi = pw_s[seg, 0:1, :S], pw_s[seg, 0:1, S:]
        er, ei = carry_s[0:1, :S], carry_s[0:1, S:]
        for s in range(SUBLANES - 1, -1, -1):
            e_s[s:s + 1, :S] = er
            e_s[s:s + 1, S:] = ei
            er, ei = fr[s:s + 1] + (pr * er + pi * ei), fi[s:s + 1] + (pr * ei - pi * er)
        carry_s[0:1, :S] = er
        carry_s[0:1, S:] = ei
        er, ei = e_s[:, :S][None], e_s[:, S:][None]
        pw = pwr_s[...]
        pwr, pwi = pw[:, :, :S], pw[:, :, S:]
        ll = dl_s[...]
        lam = jnp.concatenate([ll[:, :, :S] + (pwr * er + pwi * ei), ll[:, :, S:] + (pwr * ei - pwi * er)],
                              axis=-1).reshape(tc, 2 * S)
        l_re, l_im = lam[:, :S], lam[:, S:]
        da_ref[0:1, :S] += jnp.sum(l_re * xp[:, :S] + l_im * xp[:, S:], axis=0, keepdims=True)
        da_ref[0:1, S:] += jnp.sum(l_im * xp[:, :S] - l_re * xp[:, S:], axis=0, keepdims=True)
        lamb = lam.astype(BF16)
        dup_s[...] = lax.dot_general(lamb, b_ref[...], nn, preferred_element_type=F32)
        for s in range(SUBLANES):
            rows = pl.ds(seg * s, seg)
            du = _segment_rows(dup_s, s, seg) + d_ref[...] * dy_s[rows, :]
            du_ref[rows, :] = du.astype(du_ref.dtype)
        db_ref[...] += lax.dot_general(ubp, lamb, tn, preferred_element_type=F32)
        dd_ref[0:1, :] += jnp.sum(dy * u, axis=0, keepdims=True)

        @pl.when(pl.program_id(1) == nt - 1)
        def _():
            dbd_ref[...] = fold_diagonal(db_ref, mask_ref, fold_ref)
            dcd_ref[...] = fold_diagonal(dc_ref, mask_ref, fold_ref)

    rev = lambda b, t: (nt - 1 - t, b)
    col = jnp.arange(2 * S)
    fold = ((col // S * SSM_STATE + col % SSM_STATE)[:, None] == jnp.arange(LANES)[None, :]).astype(BF16)
    return pl.pallas_call(
        body, name="s5_bwd",
        out_shape=(jax.ShapeDtypeStruct(dproj.shape, dproj.dtype),
                   jax.ShapeDtypeStruct((SSM_BLOCKS, LANES, LANES), F32),
                   jax.ShapeDtypeStruct((SSM_BLOCKS, LANES, LANES), F32),
                   jax.ShapeDtypeStruct((SSM_BLOCKS, SUBLANES, 2 * S), F32),
                   jax.ShapeDtypeStruct((SUBLANES, MAIN_WIDTH), F32)),
        input_output_aliases={11: 0},
        grid=(SSM_BLOCKS, nt),
        in_specs=[pl.BlockSpec((tc, LANES), rev),
                  pl.BlockSpec((tc, LANES), rev),
                  pl.BlockSpec((tc, LANES), rev),
                  pl.BlockSpec((tc, LANES), rev),
                  pl.BlockSpec((tc, 2 * S), rev),
                  pl.BlockSpec((None, LANES, 2 * S), lambda b, t: (b, 0, 0)),
                  pl.BlockSpec((None, 2 * S, LANES), lambda b, t: (b, 0, 0)),
                  pl.BlockSpec((None, SUBLANES, 2 * S), lambda b, t: (b, 0, 0)),
                  pl.BlockSpec((1, LANES), lambda b, t: (0, b)),
                  pl.BlockSpec((LANES, 2 * S), lambda b, t: (0, 0)),
                  pl.BlockSpec((2 * S, LANES), lambda b, t: (0, 0)),
                  _ANY],
        out_specs=(pl.BlockSpec((tc, LANES), rev),
                   pl.BlockSpec((None, LANES, LANES), lambda b, t: (b, 0, 0)),
                   pl.BlockSpec((None, LANES, LANES), lambda b, t: (b, 0, 0)),
                   pl.BlockSpec((None, SUBLANES, 2 * S), lambda b, t: (b, 0, 0)),
                   pl.BlockSpec((SUBLANES, LANES), lambda b, t: (0, b))),
        scratch_shapes=[pltpu.VMEM((seg, SUBLANES, 2 * S), F32),
                        pltpu.VMEM((seg + 1, SUBLANES, 2 * S), F32),
                        pltpu.VMEM((seg, SUBLANES, 2 * S), F32),
                        pltpu.VMEM((SUBLANES, 2 * S), F32),
                        pltpu.VMEM((SUBLANES, 2 * S), F32),
                        pltpu.VMEM((tc, LANES), F32),
                        pltpu.VMEM((tc, LANES), F32),
                        pltpu.VMEM((tc, LANES), F32),
                        pltpu.VMEM((tc, LANES), F32),
                        pltpu.VMEM((LANES, 2 * S), F32),
                        pltpu.VMEM((LANES, 2 * S), F32)],
        compiler_params=_params("parallel", "arbitrary"),
    )(proj, dyg_a, dyg_b, y, xp, bmat, cmat, a_rows, d_skip.reshape(1, MAIN_WIDTH), _s5_diag_mask(), fold, dproj)


_Z_COLS = slice(MAIN_WIDTH, 2 * MAIN_WIDTH)
_ZM_COLS = slice(2 * MAIN_WIDTH + MEM_WIDTH, IN_WIDTH)


def _proj_rows(tr):
    return pl.BlockSpec((tr, IN_WIDTH), lambda i: (i, 0))


def _row_specs(tr):
    main = pl.BlockSpec((tr, MAIN_WIDTH), lambda i: (i, 0))
    z = pl.BlockSpec((tr, MAIN_WIDTH), lambda i: (i, 1))
    zm = pl.BlockSpec((tr, MEM_WIDTH), lambda i: (i, IN_WIDTH // MEM_WIDTH - 1))
    mem = pl.BlockSpec((tr, MEM_WIDTH), lambda i: (i, 0))
    cat = pl.BlockSpec((tr, D_MODEL), lambda i: (i, 0))
    vec = pl.BlockSpec((1, MAIN_WIDTH), lambda i: (0, 0))
    return main, z, zm, mem, cat, vec


def _gate_a_fwd(y, t, b_glu, proj, o_mem, *, tr=256):
    L = y.shape[0]
    tr = min(tr, L)

    def body(y_ref, t_ref, b_ref, z_ref, zm_ref, om_ref, o_ref):
        yg = _gelu(y_ref[...])
        sz, _ = _silu_and_grad(z_ref[...])
        o_ref[:, :MAIN_WIDTH] = (yg * _sigmoid(t_ref[...] + b_ref[...]) * sz).astype(BF16)
        szm, _ = _silu_and_grad(zm_ref[...])
        o_ref[:, MAIN_WIDTH:] = (om_ref[...] * szm).astype(BF16)

    main, z, zm, mem, cat, vec = _row_specs(tr)
    return pl.pallas_call(
        body, name="gate_a_fwd", out_shape=jax.ShapeDtypeStruct((L, D_MODEL), BF16),
        grid=(L // tr,), in_specs=[main, main, vec, z, zm, mem], out_specs=cat,
        compiler_params=_params("parallel"),
    )(y, t, b_glu.reshape(1, MAIN_WIDTH), proj, proj, o_mem)


def _gate_a_bwd(dcat, y, t, b_glu, proj, o_mem, *, tr=256):
    L = y.shape[0]
    tr = min(tr, L)

    def body(dc_ref, y_ref, t_ref, b_ref, z_ref, zm_ref, om_ref,
             dp_ref, dt_ref, dyg_ref, dom_ref, db_ref):
        dmain = dc_ref[:, :MAIN_WIDTH]
        dmemo = dc_ref[:, MAIN_WIDTH:]
        yg = _gelu(y_ref[...])
        sg = _sigmoid(t_ref[...] + b_ref[...])
        sz, gz = _silu_and_grad(z_ref[...])
        dp_ref[:, _Z_COLS] = (dmain * (yg * sg) * gz).astype(BF16)
        dy2 = dmain * sz
        dyg_ref[...] = dy2 * sg
        dt = dy2 * yg * (sg * (1.0 - sg))
        dt_ref[...] = dt.astype(BF16)

        @pl.when(pl.program_id(0) == 0)
        def _():
            db_ref[...] = jnp.zeros_like(db_ref)

        db_ref[...] += jnp.sum(dt, axis=0, keepdims=True)
        szm, gzm = _silu_and_grad(zm_ref[...])
        dom_ref[...] = dmemo * szm
        dp_ref[:, _ZM_COLS] = (dmemo * om_ref[...] * gzm).astype(BF16)

    main, z, zm, mem, cat, vec = _row_specs(tr)
    outs = pl.pallas_call(
        body, name="gate_a_bwd",
        out_shape=(jax.ShapeDtypeStruct((L, IN_WIDTH), BF16),
                   jax.ShapeDtypeStruct((L, MAIN_WIDTH), BF16), jax.ShapeDtypeStruct((L, MAIN_WIDTH), F32),
                   jax.ShapeDtypeStruct((L, MEM_WIDTH), F32), jax.ShapeDtypeStruct((1, MAIN_WIDTH), F32)),
        grid=(L // tr,), in_specs=[cat, main, main, vec, z, zm, mem],
        out_specs=(_proj_rows(tr), main, main, mem, vec),
        compiler_params=_params("arbitrary"),
    )(dcat, y, t, b_glu.reshape(1, MAIN_WIDTH), proj, proj, o_mem)
    return outs


def _gate_b_fwd(att, proj, o_mem, *, tr=256):
    L = att.shape[0]
    tr = min(tr, L)

    def body(a_ref, z_ref, zm_ref, om_ref, o_ref):
        sz, _ = _silu_and_grad(z_ref[...])
        o_ref[:, :MAIN_WIDTH] = (a_ref[...] * sz).astype(BF16)
        szm, _ = _silu_and_grad(zm_ref[...])
        o_ref[:, MAIN_WIDTH:] = (om_ref[...] * szm).astype(BF16)

    main, z, zm, mem, cat, _ = _row_specs(tr)
    return pl.pallas_call(
        body, name="gate_b_fwd", out_shape=jax.ShapeDtypeStruct((L, D_MODEL), BF16),
        grid=(L // tr,), in_specs=[main, z, zm, mem], out_specs=cat,
        compiler_params=_params("parallel"),
    )(att, proj, proj, o_mem)


def _gate_b_bwd(dcat, att, proj, o_mem, *, tr=256):
    L = att.shape[0]
    tr = min(tr, L)

    def body(dc_ref, a_ref, z_ref, zm_ref, om_ref, da_ref, dp_ref, dom_ref, dl_ref):
        dmain = dc_ref[:, :MAIN_WIDTH]
        dmemo = dc_ref[:, MAIN_WIDTH:]
        att = a_ref[...]
        sz, gz = _silu_and_grad(z_ref[...])
        datt = dmain * sz
        da_ref[...] = datt
        dp_ref[:, _Z_COLS] = (dmain * att * gz).astype(BF16)
        szm, gzm = _silu_and_grad(zm_ref[...])
        dom_ref[...] = dmemo * szm
        dp_ref[:, _ZM_COLS] = (dmemo * om_ref[...] * gzm).astype(BF16)
        prod = datt * att
        for h in range(FOX_HEADS):
            dl_ref[h] = jnp.sum(prod[:, h * HEAD_DIM:(h + 1) * HEAD_DIM], axis=1, keepdims=True)

    main, z, zm, mem, cat, _ = _row_specs(tr)
    delta = pl.BlockSpec((FOX_HEADS, tr, 1), lambda i: (0, i, 0))
    return pl.pallas_call(
        body, name="gate_b_bwd",
        out_shape=(jax.ShapeDtypeStruct((L, MAIN_WIDTH), F32), jax.ShapeDtypeStruct((L, IN_WIDTH), BF16),
                   jax.ShapeDtypeStruct((L, MEM_WIDTH), F32), jax.ShapeDtypeStruct((FOX_HEADS, L, 1), F32)),
        grid=(L // tr,), in_specs=[cat, main, z, zm, mem], out_specs=(main, _proj_rows(tr), mem, delta),
        compiler_params=_params("parallel"),
    )(dcat, att, proj, proj, o_mem)


_MEM_Q_COL = (2 * MAIN_WIDTH) // HEAD_DIM
_NT = (((1,), (1,)), ((), ()))
_TN = (((0,), (0,)), ((), ()))


def _mem_probs(q_ref, k_ref):
    qs = (q_ref[...] * (HEAD_DIM ** -0.5)).astype(BF16)
    s = lax.dot_general(qs, k_ref[...].astype(BF16), _NT, preferred_element_type=F32)
    e = jnp.exp(s - jnp.max(s, axis=-1, keepdims=True))
    return qs, e / jnp.sum(e, axis=-1, keepdims=True)


def _mem_attn_fwd(proj, kvm, *, tq=2048):
    L = proj.shape[0]
    tq = min(tq, L)

    def body(q_ref, k_ref, v_ref, o_ref):
        _, p = _mem_probs(q_ref, k_ref)
        o_ref[...] = jnp.dot(p.astype(BF16), v_ref[...].astype(BF16), preferred_element_type=F32)

    return pl.pallas_call(
        body, name="mem_attn_fwd", out_shape=jax.ShapeDtypeStruct((L, MEM_WIDTH), F32),
        grid=(MEM_HEADS, L // tq),
        in_specs=[pl.BlockSpec((tq, HEAD_DIM), lambda h, i: (i, _MEM_Q_COL + h)),
                  pl.BlockSpec((N_MEM, HEAD_DIM), lambda h, i: (0, h)),
                  pl.BlockSpec((N_MEM, HEAD_DIM), lambda h, i: (0, MEM_HEADS + h))],
        out_specs=pl.BlockSpec((tq, HEAD_DIM), lambda h, i: (i, h)),
        compiler_params=_params("parallel", "parallel"),
    )(proj, kvm, kvm)


def _mem_attn_bwd(proj, kvm, do, dproj, *, tq=2048):
    L = proj.shape[0]
    tq = min(tq, L)

    def body(q_ref, k_ref, v_ref, do_ref, dp_hbm, dq_ref, dk_ref, dv_ref):
        @pl.when(pl.program_id(1) == 0)
        def _():
            dk_ref[...] = jnp.zeros_like(dk_ref)
            dv_ref[...] = jnp.zeros_like(dv_ref)

        qs, p = _mem_probs(q_ref, k_ref)
        dob = do_ref[...].astype(BF16)
        dp = lax.dot_general(dob, v_ref[...].astype(BF16), _NT, preferred_element_type=F32)
        ds = p * (dp - jnp.sum(p * dp, axis=-1, keepdims=True))
        dsb = ds.astype(BF16)
        dq = jnp.dot(dsb, k_ref[...].astype(BF16), preferred_element_type=F32) * (HEAD_DIM ** -0.5)
        dq_ref[...] = dq.astype(BF16)
        dk_ref[...] += lax.dot_general(dsb, qs, _TN, preferred_element_type=F32)
        dv_ref[...] += lax.dot_general(p.astype(BF16), dob, _TN, preferred_element_type=F32)

    dproj, dk, dv = pl.pallas_call(
        body, name="mem_attn_bwd",
        out_shape=(jax.ShapeDtypeStruct(dproj.shape, dproj.dtype),
                   jax.ShapeDtypeStruct((N_MEM, MEM_WIDTH), F32),
                   jax.ShapeDtypeStruct((N_MEM, MEM_WIDTH), F32)),
        grid=(MEM_HEADS, L // tq),
        in_specs=[pl.BlockSpec((tq, HEAD_DIM), lambda h, i: (i, _MEM_Q_COL + h)),
                  pl.BlockSpec((N_MEM, HEAD_DIM), lambda h, i: (0, h)),
                  pl.BlockSpec((N_MEM, HEAD_DIM), lambda h, i: (0, MEM_HEADS + h)),
                  pl.BlockSpec((tq, HEAD_DIM), lambda h, i: (i, h)),
                  _ANY],
        out_specs=(pl.BlockSpec((tq, HEAD_DIM), lambda h, i: (i, _MEM_Q_COL + h)),
                   pl.BlockSpec((N_MEM, HEAD_DIM), lambda h, i: (0, h)),
                   pl.BlockSpec((N_MEM, HEAD_DIM), lambda h, i: (0, h))),
        input_output_aliases={4: 0},
        compiler_params=_params("parallel", "arbitrary"),
    )(proj, kvm, kvm, do, dproj)
    return dproj, jnp.concatenate([dk, dv], axis=1)


def _tile_cumsum(x, row, reverse):
    for sh in (1, 2, 4):
        if reverse:
            x = x + jnp.where(row < SUBLANES - sh, pltpu.roll(x, SUBLANES - sh, 0), 0.0)
        else:
            x = x + jnp.where(row >= sh, pltpu.roll(x, sh, 0), 0.0)
    return x


def _fgate_fwd(pre, b_pad):
    L = pre.shape[0]
    n8 = L // SUBLANES

    def body(p_ref, b_ref, o_ref):
        row = lax.broadcasted_iota(jnp.int32, (SUBLANES, LANES), 0)
        b = b_ref[...]

        def step(i, carry):
            x = p_ref[i] + b
            logf = jnp.minimum(x, 0.0) - jnp.log(1.0 + jnp.exp(-jnp.abs(x)))
            t = _tile_cumsum(logf, row, False) + carry
            o_ref[i] = t
            return t[SUBLANES - 1:SUBLANES, :]

        lax.fori_loop(0, n8, step, jnp.zeros((1, LANES), F32))

    out = pl.pallas_call(
        body, name="fgate_fwd", out_shape=jax.ShapeDtypeStruct((n8, SUBLANES, LANES), F32),
        compiler_params=_params(),
    )(pre.reshape(n8, SUBLANES, LANES), b_pad.reshape(1, LANES))
    return out.reshape(L, LANES)


def _fgate_bwd(dfcum, pre, b_pad):
    L = pre.shape[0]
    n8 = L // SUBLANES

    def body(d_ref, p_ref, b_ref, o_ref, s_ref):
        row = lax.broadcasted_iota(jnp.int32, (SUBLANES, LANES), 0)
        b = b_ref[...]

        def step(k, carry):
            c, acc = carry
            i = n8 - 1 - k
            t = _tile_cumsum(d_ref[i], row, True) + c
            dpre = t * _sigmoid(-(p_ref[i] + b))
            o_ref[i] = dpre
            return t[0:1, :], acc + dpre

        _, acc = lax.fori_loop(0, n8, step, (jnp.zeros((1, LANES), F32), jnp.zeros((SUBLANES, LANES), F32)))
        s_ref[...] = jnp.sum(acc, axis=0, keepdims=True)

    dpre, db = pl.pallas_call(
        body, name="fgate_bwd",
        out_shape=(jax.ShapeDtypeStruct((n8, SUBLANES, LANES), F32), jax.ShapeDtypeStruct((1, LANES), F32)),
        compiler_params=_params(),
    )(dfcum.reshape(n8, SUBLANES, LANES), pre.reshape(n8, SUBLANES, LANES), b_pad.reshape(1, LANES))
    return dpre.reshape(L, LANES), db


FOX_BLOCK = 1024


def _fox_scores(qs, k, fk, diagonal):
    s = lax.dot_general(qs, k, _NT, preferred_element_type=F32) - fk
    if diagonal:
        row = lax.broadcasted_iota(jnp.int32, s.shape, 0)
        col = lax.broadcasted_iota(jnp.int32, s.shape, 1)
        s = jnp.where(row >= col, s, NEG_BIG)
    return s


def _fox_specs(tq, L):
    nq = L // tq
    return dict(
        rows=lambda off: pl.BlockSpec((tq, HEAD_DIM), lambda h, i: (i, off + h)),
        seq=lambda off: pl.BlockSpec((L, HEAD_DIM), lambda h, i: (0, off + h)),
        col=pl.BlockSpec((None, None, tq, 1), lambda h, i: (h, i, 0, 0)),
        col_all=pl.BlockSpec((None, nq, tq, 1), lambda h, i: (h, 0, 0, 0)),
        row=pl.BlockSpec((None, None, 1, tq), lambda h, i: (h, i, 0, 0)),
        row_all=pl.BlockSpec((None, nq, 1, tq), lambda h, i: (h, 0, 0, 0)))


FOX_FWD_HEADS = 2
FOX_FWD_BLOCK = 1024


def _fox_fwd(proj, kv, fk):
    L = proj.shape[0]
    tq = min(FOX_FWD_BLOCK, L)
    nq = L // tq
    nh = FOX_FWD_HEADS
    W = nh * HEAD_DIM
    lse_shape = fk.shape[:2] + (fk.shape[3], 1)
    fk = fk.reshape(FOX_HEADS, nq, 1, tq)

    def body(q_ref, k_ref, v_ref, fk_ref, o_ref, lse_ref, m_s, l_s, acc_s):
        qi = pl.program_id(1)
        cols = [slice(a * HEAD_DIM, (a + 1) * HEAD_DIM) for a in range(nh)]
        qs = [(q_ref[:, cs] * (HEAD_DIM ** -0.5)).astype(BF16) for cs in cols]
        m_s[...] = jnp.full_like(m_s, NEG_BIG)
        l_s[...] = jnp.zeros_like(l_s)
        acc_s[...] = jnp.zeros_like(acc_s)

        def block(j, diagonal):
            r0 = pl.multiple_of(j * tq, tq)
            for a, cs in enumerate(cols):
                s = _fox_scores(qs[a], k_ref[pl.ds(r0, tq), cs], fk_ref[a, j], diagonal)
                m_new = jnp.maximum(m_s[a], jnp.max(s, axis=-1, keepdims=True))
                alpha = jnp.exp(m_s[a] - m_new)
                p = jnp.exp(s - m_new)
                l_s[a] = alpha * l_s[a] + jnp.sum(p, axis=-1, keepdims=True)
                acc_s[a] = alpha * acc_s[a] + jnp.dot(p.astype(BF16), v_ref[pl.ds(r0, tq), cs],
                                                      preferred_element_type=F32)
                m_s[a] = m_new

        def below(j, carry):
            block(j, False)
            return carry

        lax.fori_loop(0, qi, below, 0)
        block(qi, True)
        for a, cs in enumerate(cols):
            o_ref[:, cs] = acc_s[a] / l_s[a]
            lse_ref[a] = m_s[a] + jnp.log(l_s[a])

    att, lse = pl.pallas_call(
        body, name="fox_fwd",
        out_shape=(jax.ShapeDtypeStruct((L, MAIN_WIDTH), F32),
                   jax.ShapeDtypeStruct((FOX_HEADS, nq, tq, 1), F32)),
        grid=(FOX_HEADS // nh, nq),
        in_specs=[pl.BlockSpec((tq, W), lambda h, i: (i, h)),
                  pl.BlockSpec((L, W), lambda h, i: (0, h)),
                  pl.BlockSpec((L, W), lambda h, i: (0, FOX_HEADS // nh + h)),
                  pl.BlockSpec((nh, nq, 1, tq), lambda h, i: (h, 0, 0, 0))],
        out_specs=(pl.BlockSpec((tq, W), lambda h, i: (i, h)),
                   pl.BlockSpec((nh, None, tq, 1), lambda h, i: (h, i, 0, 0))),
        scratch_shapes=[pltpu.VMEM((nh, tq, 1), F32), pltpu.VMEM((nh, tq, 1), F32),
                        pltpu.VMEM((nh, tq, HEAD_DIM), F32)],
        compiler_params=_params("parallel", "parallel"),
    )(proj, kv, kv, fk)
    return att, lse.reshape(lse_shape)


def _fox_bwd(proj, kv, fk, lse, delta, datt, dproj):
    L = proj.shape[0]
    tq = min(FOX_BLOCK, L)
    nq = L // tq
    sp = _fox_specs(tq, L)

    def body(q_ref, k_ref, v_ref, fk_ref, lse_ref, dl_ref, do_ref, dp_hbm,
             dq_ref, dk_ref, dv_ref, dfq_ref, dfk_ref, dk_s, dv_s, df_s, dq_s, dfq_s):
        ki = pl.program_id(1)

        @pl.when(ki == 0)
        def _():
            dq_s[...] = jnp.zeros_like(dq_s)
            dfq_s[...] = jnp.zeros_like(dfq_s)

        k, v, fk = k_ref[...], v_ref[...], fk_ref[...]
        dk_s[...] = jnp.zeros_like(dk_s)
        dv_s[...] = jnp.zeros_like(dv_s)
        df_s[...] = jnp.zeros_like(df_s)

        def block(i, diagonal):
            r0 = pl.multiple_of(i * tq, tq)
            qs = (q_ref[pl.ds(r0, tq), :] * (HEAD_DIM ** -0.5)).astype(BF16)
            dob = do_ref[pl.ds(r0, tq), :].astype(BF16)
            p = jnp.exp(_fox_scores(qs, k, fk, diagonal) - lse_ref[i])
            dp = lax.dot_general(dob, v, _NT, preferred_element_type=F32)
            ds = p * (dp - dl_ref[i])
            dsb = ds.astype(BF16)
            dv_s[...] += lax.dot_general(p.astype(BF16), dob, _TN, preferred_element_type=F32)
            dk_s[...] += lax.dot_general(dsb, qs, _TN, preferred_element_type=F32)
            df_s[...] -= jnp.sum(ds, axis=0, keepdims=True)
            dq_s[i] += jnp.dot(dsb, k, preferred_element_type=F32)
            dfq_s[i] += jnp.sum(ds, axis=1, keepdims=True)

        def above(i, carry):
            block(i, False)
            return carry

        block(ki, True)
        lax.fori_loop(ki + 1, nq, above, 0)
        dk_ref[...] = dk_s[...].astype(BF16)
        dv_ref[...] = dv_s[...].astype(BF16)
        dfk_ref[...] = df_s[...]

        @pl.when(ki == nq - 1)
        def _():
            dq_ref[...] = (dq_s[...].reshape(L, HEAD_DIM) * (HEAD_DIM ** -0.5)).astype(BF16)
            dfq_ref[...] = dfq_s[...]

    return pl.pallas_call(
        body, name="fox_bwd",
        out_shape=(jax.ShapeDtypeStruct(dproj.shape, dproj.dtype),
                   jax.ShapeDtypeStruct((L, MAIN_WIDTH), BF16),
                   jax.ShapeDtypeStruct((L, MAIN_WIDTH), BF16),
                   jax.ShapeDtypeStruct((FOX_HEADS, nq, tq, 1), F32),
                   jax.ShapeDtypeStruct((FOX_HEADS, nq, 1, tq), F32)),
        grid=(FOX_HEADS, nq),
        in_specs=[sp["seq"](0), sp["rows"](0), sp["rows"](FOX_HEADS), sp["row"],
                  sp["col_all"], sp["col_all"], sp["seq"](0), _ANY],
        out_specs=(sp["seq"](0), sp["rows"](0), sp["rows"](0), sp["col_all"], sp["row"]),
        input_output_aliases={7: 0},
        scratch_shapes=[pltpu.VMEM((tq, HEAD_DIM), F32), pltpu.VMEM((tq, HEAD_DIM), F32),
                        pltpu.VMEM((1, tq), F32), pltpu.VMEM((nq, tq, HEAD_DIM), F32),
                        pltpu.VMEM((nq, tq, 1), F32)],
        compiler_params=_params("parallel", "arbitrary"),
    )(proj, kv, kv, fk, lse, delta, datt, dproj)


def _pad_lanes(a):
    return jnp.pad(a, ((0, 0), (0, LANES - a.shape[1])))


def _mem_branch_fwd(memn, w_mk, proj, tag):
    kvm = _mm(memn, w_mk, name="mem_kv_" + tag)
    return kvm, _mem_attn_fwd(proj, kvm)


def _mem_branch_bwd(mem, g, w_mk, proj, memn, kvm, do_mem, dproj, tag):
    dproj, dkvm = _mem_attn_bwd(proj, kvm, do_mem, dproj)
    dkvm = dkvm.astype(BF16)
    dw_mk = _mm(memn, dkvm, ta=True, name="dw_mem_kv_" + tag, out_dtype=BF16)
    dmemn = _mm(dkvm, w_mk, tb=True, name="dmemn_" + tag)
    _, dg = _rmsnorm_bwd(mem, g, dmemn, name="mem_norm_bwd_" + tag, dx_dtype=BF16)
    return dproj, dw_mk, dg


def _local_step(x, mem, target, w, fetch=None, grads_ready=None):
    if grads_ready is None:
        grads_ready = lambda group, grads, token: token
    L = x.shape[0]
    g = {}
    w = dict(w)

    b_re_t = jnp.transpose(w["b_re"], (0, 2, 1))
    b_im_t = jnp.transpose(w["b_im"], (0, 2, 1))
    ar, ai, bbr_t, bbi_t = _s5_prep(w["lam_re"], w["lam_im"], w["log_step"], b_re_t, b_im_t)
    bmat, cmat = _s5_block_mats(bbr_t, bbi_t, w["c_re"], w["c_im"])
    a_rows = _s5_a_rows(ar, ai)

    hn0 = _rmsnorm_fwd(x, w["pre_norm_g"][0], name="pre_norm_0", out_dtype=BF16)
    memn0 = _rmsnorm_fwd(mem, w["mem_norm_g"][0], name="mem_norm_0", out_dtype=BF16)
    memn1 = _rmsnorm_fwd(mem, w["mem_norm_g"][1], name="mem_norm_1", out_dtype=BF16)
    if fetch is not None:
        w.update(fetch("a", [hn0, memn0, memn1, bmat, cmat, a_rows]))
    proj_a = _mm(hn0, w["w_in_a"], name="in_proj_a")
    y, yg, xp = _s5_fwd(proj_a, bmat, cmat, a_rows, w["d_skip"])
    if fetch is not None:
        w.update(fetch("b", yg))
    t = _mm(yg, w["w_glu"], name="glu_proj")
    kvm0, om0 = _mem_branch_fwd(memn0, w["w_mem_kv"][0], proj_a, "0")
    cat0 = _gate_a_fwd(y, t, w["b_glu"], proj_a, om0)
    o0 = _mm(cat0, w["w_out"][0], name="out_proj_0")
    h1 = _rmsnorm_fwd(o0, w["post_norm_g"][0], res=x, name="post_norm_0")

    kv_in = _rmsnorm_fwd(h1, w["kv_norm_g"], name="kv_norm", out_dtype=BF16)
    if fetch is not None:
        w.update(fetch("c", kv_in))
    kv = _mm(kv_in, w["w_kv"], name="kv_proj", out_dtype=BF16)
    pre_f = _mm(kv_in, w["w_fgate"], name="fgate_proj")
    b_f = jnp.pad(w["b_fgate"], (0, LANES - FOX_HEADS))
    fcum = _fgate_fwd(pre_f, b_f)
    fc = jnp.transpose(fcum[:, :FOX_HEADS])
    tq = min(FOX_BLOCK, L)
    fk = fc.reshape(FOX_HEADS, L // tq, 1, tq)

    hn1 = _rmsnorm_fwd(h1, w["pre_norm_g"][1], name="pre_norm_1", out_dtype=BF16)
    proj_b = _mm(hn1, w["w_in_b"], name="in_proj_b")
    att, lse = _fox_fwd(proj_b, kv, fk)
    kvm1, om1 = _mem_branch_fwd(memn1, w["w_mem_kv"][1], proj_b, "1")
    cat1 = _gate_b_fwd(att, proj_b, om1)
    o1 = _mm(cat1, w["w_out"][1], name="out_proj_1")
    dh2, loss_row = _final_norm_loss(o1, w["post_norm_g"][1], h1, target)

    do1, dpost1 = _rmsnorm_bwd(o1, w["post_norm_g"][1], dh2, name="post_norm_bwd_1", dx_dtype=BF16)
    dcat1 = _mm(do1, w["w_out"][1], tb=True, name="dcat_1", out_dtype=BF16)
    g["w_out_1"] = _mm(cat1, do1, ta=True, name="dw_out_1", out_dtype=BF16)
    datt, dproj_b, dom1, delta = _gate_b_bwd(dcat1, att, proj_b, om1)
    dproj_b, g["w_mem_kv_1"], dmemg1 = _mem_branch_bwd(mem, w["mem_norm_g"][1], w["w_mem_kv"][1], proj_b,
                                                      memn1, kvm1, dom1, dproj_b, "1")
    delta = delta.reshape(lse.shape)
    dproj_b, dk, dv, dfq, dfk = _fox_bwd(proj_b, kv, fk, lse, delta, datt, dproj_b)
    g["w_in_b"] = _mm(hn1, dproj_b, ta=True, name="dw_in_b", out_dtype=BF16, shards=N_CHIPS)
    dhn1 = _mm(dproj_b, w["w_in_b"], tb=True, name="dhn_1")

    dkv = jnp.concatenate([dk, dv], axis=1)
    g["w_kv"] = _mm(kv_in, dkv, ta=True, name="dw_kv", out_dtype=BF16, shards=N_CHIPS)
    dkv_in_a = _mm(dkv, w["w_kv"], tb=True, name="dkv_in_kv")
    dfcum = _pad_lanes(jnp.transpose(dfq.reshape(FOX_HEADS, L) + dfk.reshape(FOX_HEADS, L)))
    dpre_f, db_f = _fgate_bwd(dfcum, pre_f, b_f)
    g["b_fgate"] = db_f[0, :FOX_HEADS]
    g["w_fgate"] = _mm(kv_in, dpre_f, ta=True, name="dw_fgate")[:, :FOX_HEADS]
    dkv_in_b = _mm(dpre_f, w["w_fgate"], tb=True, name="dkv_in_fgate")
    dh1, g["kv_norm_g"], dpre1 = _rmsnorm_bwd_pair(h1, w["kv_norm_g"], (dkv_in_a, dkv_in_b), w["pre_norm_g"][1],
                                                   dhn1, adds=(dh2,), name="kv_pre_norm_bwd")
    dh1 = grads_ready("b", g, dh1)

    do0, dpost0 = _rmsnorm_bwd(o0, w["post_norm_g"][0], dh1, name="post_norm_bwd_0", dx_dtype=BF16)
    dcat0 = _mm(do0, w["w_out"][0], tb=True, name="dcat_0", out_dtype=BF16)
    g["w_out_0"] = _mm(cat0, do0, ta=True, name="dw_out_0", out_dtype=BF16)
    dcat0 = grads_ready("b_send", g, dcat0)
    dproj_a, dt, dyg_a, dom0, db_glu = _gate_a_bwd(dcat0, y, t, w["b_glu"], proj_a, om0)
    g["b_glu"] = db_glu[0]
    g["w_glu"] = _mm(yg, dt, ta=True, name="dw_glu", out_dtype=BF16)
    dyg_b = _mm(dt, w["w_glu"], tb=True, name="dyg")
    dproj_a, g["w_mem_kv_0"], dmemg0 = _mem_branch_bwd(mem, w["mem_norm_g"][0], w["w_mem_kv"][0], proj_a,
                                                      memn0, kvm0, dom0, dproj_a, "0")
    dyg_b = grads_ready("a1", g, dyg_b)
    dproj_a, db_blk, dc_blk, da_rows, dd_skip = _s5_bwd(proj_a, dyg_a, dyg_b, y, xp, bmat, cmat, a_rows,
                                                        w["d_skip"], dproj_a)
    dproj_a = grads_ready("a1_send", g, dproj_a)
    g["d_skip"] = dd_skip[0]
    g["w_in_a"] = _mm(hn0, dproj_a, ta=True, name="dw_in_a", out_dtype=BF16, shards=N_CHIPS)
    dproj_a = grads_ready("a2", g, dproj_a)
    dhn0 = _mm(dproj_a, w["w_in_a"], tb=True, name="dhn_0")
    grad_x, dpre0 = _rmsnorm_bwd(x, w["pre_norm_g"][0], dhn0, adds=(dh1,), name="pre_norm_bwd_0")

    dbb = _s5_unfold(db_blk)
    dcc = _s5_unfold(dc_blk)
    g["c_re"], g["c_im"] = dcc[0], -dcc[1]
    d_ar = da_rows[:, 0, :STATE_COLS].reshape(SSM_GROUPS, SSM_STATE)
    d_ai = da_rows[:, 0, STATE_COLS:].reshape(SSM_GROUPS, SSM_STATE)
    dlr, dli, dls, dbr_t, dbi_t = _s5_prep_bwd(w["lam_re"], w["lam_im"], w["log_step"], b_re_t, b_im_t,
                                               d_ar, d_ai, dbb[0], dbb[1])
    g["lam_re"], g["lam_im"], g["log_step"] = dlr, dli, dls[:, 0]
    g["b_re"] = jnp.transpose(dbr_t, (0, 2, 1))
    g["b_im"] = jnp.transpose(dbi_t, (0, 2, 1))
    g["pre_norm_g"] = jnp.stack([dpre0, dpre1])
    g["post_norm_g"] = jnp.stack([dpost0, dpost1])
    g["mem_norm_g"] = jnp.stack([dmemg0, dmemg1])
    return loss_row, grad_x, g


_MESH = pl.DeviceIdType.MESH
_ANY = pl.BlockSpec(memory_space=pl.ANY)


def _place():
    x, y, c = lax.axis_index("x"), lax.axis_index("y"), lax.axis_index("c")
    chips = [(1 - x, y), (x, 1 - y), (1 - x, 1 - y)]
    return x, y, c, chips


_HBM = pl.BlockSpec(memory_space=pltpu.HBM)
_SEM = pl.BlockSpec(memory_space=pltpu.SEMAPHORE)
_SIDE = pltpu.SideEffectType.DATAFLOW_SIDE_EFFECTING


def _in_hbm(a):
    return pltpu.with_memory_space_constraint(a, pltpu.HBM)


def _hbm_like(a):
    return pltpu.HBM(a.shape, a.dtype)


def _ici_copies(srcs, lands, send_sem, recv_sem, src_at, dst_at, wait_at, to_sibling=False):
    x, y, c, chips = _place()
    peers = [(x, y, 1 - c)] if to_sibling else [(cx, cy, c) for cx, cy in chips]
    m = len(peers)
    start, wait = [], []
    for i in range(len(srcs)):
        for k, (px, py, pc) in enumerate(peers):
            sem = dict(send_sem=send_sem.at[m * i + k], recv_sem=recv_sem.at[m * i + k],
                       device_id=(px, py, pc), device_id_type=_MESH)
            src = src_at(srcs[i], 2 * px + py, c)
            start.append(pltpu.make_async_remote_copy(src_ref=src, dst_ref=dst_at(lands[i], 2 * x + y, k, c), **sem))
            wait.append(pltpu.make_async_remote_copy(src_ref=src, dst_ref=wait_at(lands[i], 2 * px + py, k, c), **sem))
    return start, wait


def _route_peers(route):
    return 1 if len(route) == 4 else 3


_BLOCK_ROUTE = (lambda s, j, c: s, lambda l, me, k, c: l.at[me, c], lambda l, j, k, c: l.at[j, c])


def _ici_start(srcs, lands, token, route, *, name):
    n = len(srcs)

    def body(*refs):
        start, _ = _ici_copies(refs[:n], refs[n:2 * n], refs[2 * n + 1], refs[2 * n + 2], *route)
        for cp in start:
            cp.start()

    sems = pltpu.SemaphoreType.DMA((_route_peers(route) * n,))
    outs = pl.pallas_call(
        body, name=name,
        out_shape=(sems, sems, *[_hbm_like(a) for a in srcs], *[_hbm_like(a) for a in lands], _hbm_like(token)),
        in_specs=[_HBM] * (2 * n + 1), out_specs=(_SEM, _SEM, *[_HBM] * (2 * n + 1)),
        input_output_aliases={i: 2 + i for i in range(2 * n + 1)},
        compiler_params=pltpu.CompilerParams(has_side_effects=_SIDE),
    )(*[_in_hbm(a) for a in srcs], *[_in_hbm(a) for a in lands], _in_hbm(token))
    return (outs[0], outs[1], list(outs[2:2 + n]), list(outs[2 + n:2 + 2 * n])), outs[2 + 2 * n]


def _ici_wait(handle, after, route, *, name):
    send_sem, recv_sem, srcs, lands = handle
    n = len(srcs)
    after = list(after) if isinstance(after, (list, tuple)) else [after]

    def body(*refs):
        _, wait = _ici_copies(refs[:n], refs[n:2 * n], refs[2 * n], refs[2 * n + 1], *route)
        for cp in wait:
            cp.wait_send()
            cp.wait_recv()

    outs = pl.pallas_call(
        body, name=name,
        out_shape=(*[_hbm_like(a) for a in srcs], *[_hbm_like(a) for a in lands]),
        in_specs=[_HBM] * (2 * n) + [_SEM, _SEM] + [_ANY] * len(after), out_specs=tuple([_HBM] * (2 * n)),
        input_output_aliases={i: i for i in range(2 * n)},
        compiler_params=pltpu.CompilerParams(has_side_effects=_SIDE),
    )(*srcs, *lands, send_sem, recv_sem, *after)
    return list(outs[:n]), list(outs[n:])


_GATHER_ROUTE = (lambda s, j, c: s.at[c], lambda l, me, k, c: l.at[me, c], lambda l, j, k, c: l.at[j, c])
_SCATTER_ROUTE = (lambda s, j, c: s.at[j], lambda l, me, k, c: l.at[k], lambda l, j, k, c: l.at[k])
_SHARE_ROUTE = (lambda s, j, c: s, lambda l, me, k, c: l.at[c], lambda l, j, k, c: l.at[1 - c], True)
_SWAP_ROUTE = (lambda s, j, c: s.at[:, 1 - c], lambda l, me, k, c: l, lambda l, j, k, c: l, True)


def _gather_forward(lands, tag, own=False):
    n = len(lands)
    m = 4 if own else 3

    def body(*refs):
        ins, outs = refs[:n], refs[n:2 * n]
        send_sem, recv_sem = refs[2 * n:]
        x, y, c, chips = _place()
        slots = [2 * cx + cy for cx, cy in chips] + [2 * x + y]

        def copy(i, k, half):
            return pltpu.make_async_remote_copy(
                src_ref=ins[i].at[slots[k], half], dst_ref=outs[i].at[slots[k], half],
                send_sem=send_sem.at[m * i + k], recv_sem=recv_sem.at[m * i + k],
                device_id=(x, y, 1 - c), device_id_type=_MESH)

        copies = [copy(i, k, c) for i in range(n) for k in range(m)]
        for cp in copies:
            cp.start()
        for i in range(n):
            for k in range(m):
                copy(i, k, 1 - c).wait_recv()
        for cp in copies:
            cp.wait_send()

    return pl.pallas_call(
        body, name="gather_forward_to_sibling_" + tag,
        out_shape=[jax.ShapeDtypeStruct(a.shape, a.dtype) for a in lands],
        in_specs=[_ANY] * n, out_specs=[_ANY] * n,
        input_output_aliases={i: i for i in range(n)},
        scratch_shapes=[pltpu.SemaphoreType.DMA((m * n,)), pltpu.SemaphoreType.DMA((m * n,))],
    )(*lands)


def _swap_halves(grads, tag):
    n = len(grads)

    def body(*refs):
        ins, outs = refs[:n], refs[n:2 * n]
        send_sem, recv_sem = refs[2 * n:]
        x, y, c, _ = _place()
        copies = [pltpu.make_async_remote_copy(
            src_ref=ins[i].at[:, 1 - c], dst_ref=outs[i],
            send_sem=send_sem.at[i], recv_sem=recv_sem.at[i],
            device_id=(x, y, 1 - c), device_id_type=_MESH) for i in range(n)]
        for cp in copies:
            cp.start()
        for cp in copies:
            cp.wait()

    return pl.pallas_call(
        body, name="grad_swap_halves_" + tag,
        out_shape=[jax.ShapeDtypeStruct((N_CHIPS,) + g.shape[2:], g.dtype) for g in grads],
        in_specs=[_ANY] * n, out_specs=[_ANY] * n,
        scratch_shapes=[pltpu.SemaphoreType.DMA((n,)), pltpu.SemaphoreType.DMA((n,))],
    )(*grads)


def _sum_rows(h, C):
    return max(d for d in range(SUBLANES, h + 1, SUBLANES) if h % d == 0 and d * C <= 1 << 20)


SUM_STEPS = 4


def _pair_sums(gs, rs, c_idx, *, name):
    n = len(gs)
    rows = [g.shape[2] // SUM_STEPS for g in gs]

    def body(c_ref, *refs):
        for g_ref, r_ref, o_ref in zip(refs[:n], refs[n:2 * n], refs[2 * n:]):
            o_ref[...] = (g_ref[...].astype(F32) + r_ref[...].astype(F32)).astype(o_ref.dtype)

    return pl.pallas_call(
        body, name=name,
        out_shape=[jax.ShapeDtypeStruct((N_CHIPS,) + g.shape[2:], g.dtype) for g in gs],
        grid_spec=pltpu.PrefetchScalarGridSpec(
            num_scalar_prefetch=1, grid=(N_CHIPS, SUM_STEPS),
            in_specs=[pl.BlockSpec((None, None, tr, g.shape[3]), lambda j, i, s: (j, s[0], i, 0))
                      for g, tr in zip(gs, rows)]
            + [pl.BlockSpec((None, tr, g.shape[3]), lambda j, i, s: (j, i, 0)) for g, tr in zip(gs, rows)],
            out_specs=[pl.BlockSpec((None, tr, g.shape[3]), lambda j, i, s: (j, i, 0)) for g, tr in zip(gs, rows)]),
        compiler_params=_params("parallel", "parallel"),
    )(c_idx, *gs, *rs)


def _owner_sums(ss, rs, jc_idx, *, name):
    n = len(ss)
    rows = [s.shape[1] // SUM_STEPS for s in ss]

    def body(jc_ref, *refs):
        for s_ref, r_ref, m_ref, o_ref in zip(refs[:n], refs[n:2 * n], refs[2 * n:3 * n], refs[3 * n:]):
            acc = s_ref[...].astype(F32)
            for k in range(3):
                acc = acc + r_ref[k].astype(F32)
            m_ref[...] = acc
            o_ref[...] = acc

    outs = pl.pallas_call(
        body, name=name,
        out_shape=[jax.ShapeDtypeStruct(s.shape[1:], F32) for s in ss]
        + [jax.ShapeDtypeStruct((2,) + s.shape[1:], F32) for s in ss],
        grid_spec=pltpu.PrefetchScalarGridSpec(
            num_scalar_prefetch=1, grid=(SUM_STEPS,),
            in_specs=[pl.BlockSpec((None, tr, s.shape[2]), lambda i, p: (p[0], i, 0)) for s, tr in zip(ss, rows)]
            + [pl.BlockSpec((3, tr, s.shape[2]), lambda i, p: (0, i, 0)) for s, tr in zip(ss, rows)],
            out_specs=[pl.BlockSpec((tr, s.shape[2]), lambda i, p: (i, 0)) for s, tr in zip(ss, rows)]
            + [pl.BlockSpec((None, tr, s.shape[2]), lambda i, p: (p[1], i, 0)) for s, tr in zip(ss, rows)]),
        compiler_params=_params("parallel"),
    )(jc_idx, *ss, *rs)
    return outs[:n], outs[n:]


def _chip_sums(grads, c_idx, tag):
    views = [g.reshape(N_CHIPS, 2, g.shape[1] // 2, g.shape[2]) for g in grads]
    arrived = _swap_halves(views, tag)
    return _pair_sums(views, arrived, c_idx, name=f"grad_pair_sums_{tag}")


def _sum_devices(blocks):
    R = blocks.shape[2]
    tr = _sum_rows(R, 2 * N_CHIPS * LANES)

    def body(b_ref, o_ref):
        acc = b_ref[0, 0]
        for d in range(1, 2 * N_CHIPS):
            acc = acc + b_ref[d // 2, d % 2]
        o_ref[...] = acc

    return pl.pallas_call(
        body, name="sum_small_over_devices", out_shape=jax.ShapeDtypeStruct((R, LANES), F32),
        grid=(R // tr,),
        in_specs=[pl.BlockSpec((N_CHIPS, 2, tr, LANES), lambda i: (0, 0, i, 0))],
        out_specs=pl.BlockSpec((tr, LANES), lambda i: (i, 0)),
        compiler_params=_params("parallel"),
    )(blocks)


def _adamw(w, g, m, v, *, name):
    R, C = w.shape
    tr = max(d for d in range(SUBLANES, R + 1, SUBLANES)
             if R % d == 0 and 7 * 2 * d * C * 4 <= VMEM_LIMIT_BYTES // 2)

    def body(w_ref, g_ref, m_ref, v_ref, d_ref, nm_ref, nv_ref):
        g = g_ref[...]
        m = ADAM_B1 * m_ref[...] + (1.0 - ADAM_B1) * g
        v = ADAM_B2 * v_ref[...] + (1.0 - ADAM_B2) * (g * g)
        nm_ref[...] = m
        nv_ref[...] = v
        m_hat = m / (1.0 - ADAM_B1 ** ADAM_STEP)
        v_hat = v / (1.0 - ADAM_B2 ** ADAM_STEP)
        d_ref[...] = -ADAM_LR * (m_hat / (jnp.sqrt(v_hat) + ADAM_EPS) + ADAM_WD * w_ref[...])

    blk = pl.BlockSpec((tr, C), lambda i: (i, 0))
    sds = jax.ShapeDtypeStruct((R, C), F32)
    return pl.pallas_call(
        body, name=name, out_shape=(sds, sds, sds), grid=(R // tr,),
        in_specs=[blk] * 4, out_specs=(blk, blk, blk),
        compiler_params=_params("parallel"),
    )(w, g, m, v)


_TILE = SUBLANES * LANES


def _pack(arrays):
    rows = []
    for a in arrays:
        flat = a.reshape(-1)
        flat = jnp.pad(flat, (0, (-flat.shape[0]) % _TILE))
        rows.append(flat.reshape(-1, LANES))
    return jnp.concatenate(rows, axis=0)


def _unpack(buf, shapes):
    out, r = [], 0
    for s in shapes:
        size = math.prod(s)
        nr = -(-size // _TILE) * SUBLANES
        out.append(buf[r:r + nr].reshape(-1)[:size].reshape(s))
        r += nr
    return out


_BIG = ("w_in_a", "w_glu", "w_kv", "w_in_b", "w_mem_kv", "w_out")
_REPLICATED = ("pre_norm_g", "post_norm_g", "lam_re", "lam_im", "log_step", "b_re", "b_im", "c_re", "c_im",
               "kv_norm_g", "b_fgate", "mem_norm_g")
_SHARDED_SMALL = ("d_skip", "b_glu", "w_fgate")
_WEIGHTS = ("pre_norm_g", "post_norm_g", "w_in_a", "lam_re", "lam_im", "log_step", "b_re", "b_im", "c_re",
            "c_im", "d_skip", "w_glu", "b_glu", "kv_norm_g", "w_kv", "w_fgate", "b_fgate", "w_in_b",
            "mem_norm_g", "w_mem_kv", "w_out")


def _halves(a):
    return a.reshape(2, a.shape[0] // 2, a.shape[1])


def _unhalve(a):
    return a.reshape(N_CHIPS, 2 * a.shape[2], a.shape[3])


def _columns(a):
    return jnp.transpose(a, (1, 0, 2)).reshape(a.shape[1], N_CHIPS * a.shape[2])


def kernel(x, mem, pre_norm_g, post_norm_g, w_in_a, lam_re, lam_im, log_step, b_re, b_im, c_re, c_im, d_skip, w_glu, b_glu, kv_norm_g, w_kv, w_fgate, b_fgate, w_in_b, mem_norm_g, w_mem_kv, w_out, loss_target, m_pre_norm_g, m_post_norm_g, m_w_in_a, m_lam_re, m_lam_im, m_log_step, m_b_re, m_b_im, m_c_re, m_c_im, m_d_skip, m_w_glu, m_b_glu, m_kv_norm_g, m_w_kv, m_w_fgate, m_b_fgate, m_w_in_b, m_mem_norm_g, m_w_mem_kv, m_w_out, v_pre_norm_g, v_post_norm_g, v_w_in_a, v_lam_re, v_lam_im, v_log_step, v_b_re, v_b_im, v_c_re, v_c_im, v_d_skip, v_w_glu, v_b_glu, v_kv_norm_g, v_w_kv, v_w_fgate, v_b_fgate, v_w_in_b, v_mem_norm_g, v_w_mem_kv, v_w_out):
    a = dict(locals())
    xi, yi, ci = lax.axis_index("x"), lax.axis_index("y"), lax.axis_index("c")
    chip = 2 * xi + yi
    c_idx = jnp.reshape(ci, (1,)).astype(jnp.int32)
    jc_idx = jnp.stack([chip, ci]).astype(jnp.int32)

    vec = jnp.zeros((2 * SUBLANES, MAIN_WIDTH // N_CHIPS), F32)
    vec = vec.at[0].set(a["d_skip"][0]).at[1].set(a["b_glu"][0])
    def own_slot(gathered, parts):
        return [lax.dynamic_update_index_in_dim(g, p, chip, 0) for g, p in zip(gathered, parts)]

    parts_a = [_halves(a["w_in_a"][0].astype(BF16)), _halves(vec)]
    parts_b = [_halves(a["w_glu"][0].astype(BF16)),
               *[_halves(a["w_mem_kv"][i].astype(BF16)) for i in range(2)],
               *[_halves(a["w_out"][i].astype(BF16)) for i in range(2)]]
    parts_c = [_halves(a["w_kv"].astype(BF16)), _halves(_pad_lanes(a["w_fgate"]).astype(BF16)),
               _halves(a["w_in_b"][0].astype(BF16))]
    travelling, token = {}, a["pre_norm_g"]
    for tag, parts in (("a", parts_a), ("b", parts_b), ("c", parts_c)):
        lands = [lax.empty((N_CHIPS,) + p.shape, p.dtype) for p in parts]
        travelling[tag], token = _ici_start(parts, lands, token, _GATHER_ROUTE, name=f"gather_{tag}_start")

    def fetch(tag, after):
        parts, lands = _ici_wait(travelling[tag], after, _GATHER_ROUTE, name=f"gather_{tag}_wait")
        full = own_slot(_gather_forward(lands, tag), parts)
        if tag == "a":
            w_in_a, vecs = full
            return dict(w_in_a=_columns(_unhalve(w_in_a)), d_skip=vecs[:, 0, 0, :].reshape(MAIN_WIDTH),
                        b_glu=vecs[:, 0, 1, :].reshape(MAIN_WIDTH))
        if tag == "b":
            w_glu, w_mk0, w_mk1, w_out0, w_out1 = full
            return dict(w_glu=w_glu.reshape(MAIN_WIDTH, MAIN_WIDTH),
                        w_mem_kv=[m.reshape(D_MODEL, 2 * MEM_WIDTH) for m in (w_mk0, w_mk1)],
                        w_out=[o.reshape(D_MODEL, D_MODEL) for o in (w_out0, w_out1)])
        w_kv, w_fg, w_in_b = full
        return dict(w_kv=_columns(_unhalve(w_kv)), w_fgate=w_fg.reshape(D_MODEL, LANES),
                    w_in_b=_columns(_unhalve(w_in_b)))

    w = dict(
        pre_norm_g=token, post_norm_g=a["post_norm_g"], mem_norm_g=a["mem_norm_g"],
        kv_norm_g=a["kv_norm_g"], b_fgate=a["b_fgate"],
        lam_re=a["lam_re"][0], lam_im=a["lam_im"][0], log_step=a["log_step"][0],
        b_re=a["b_re"][0], b_im=a["b_im"][0], c_re=a["c_re"][0], c_im=a["c_im"][0])

    sent = {}

    swapping = {}

    def grads_ready(event, g, token):
        tag = event.split("_")[0]
        if event in ("b", "a1"):
            big = {"b": lambda: [g["w_kv"], g["w_in_b"], g["w_mem_kv_1"].reshape(N_CHIPS, -1, 2 * MEM_WIDTH),
                                 g["w_out_1"].reshape(N_CHIPS, -1, D_MODEL)],
                   "a1": lambda: [g["w_glu"].reshape(N_CHIPS, -1, MAIN_WIDTH),
                                  g["w_mem_kv_0"].reshape(N_CHIPS, -1, 2 * MEM_WIDTH),
                                  g["w_out_0"].reshape(N_CHIPS, -1, D_MODEL)]}[tag]()
            views = [b.reshape(N_CHIPS, 2, b.shape[1] // 2, b.shape[2]) for b in big]
            lands = [lax.empty((N_CHIPS,) + v.shape[2:], v.dtype) for v in views]
            swapping[tag], token = _ici_start(views, lands, token, _SWAP_ROUTE, name=f"grad_swap_{tag}_start")
            return token
        if event == "a2":
            sums = _chip_sums([g["w_in_a"]], c_idx, tag)
        else:
            views, arrived = _ici_wait(swapping[tag], token, _SWAP_ROUTE, name=f"grad_swap_{tag}_wait")
            sums = _pair_sums(views, arrived, c_idx, name=f"grad_pair_sums_{tag}")
        lands = [lax.empty((3,) + s.shape[1:], s.dtype) for s in sums]
        sent[tag], token = _ici_start(sums, lands, token, _SCATTER_ROUTE, name=f"grad_send_{tag}_start")
        return token

    loss_row, grad_x, g = _local_step(a["x"][0], a["mem"][0], a["loss_target"][0], w, fetch, grads_ready)

    small_names = _REPLICATED + _SHARDED_SMALL
    pack = _pack([g[n] for n in small_names])
    blocks = lax.empty((N_CHIPS, 2) + pack.shape, F32)
    small_sent, token = _ici_start([pack], [blocks], loss_row, _BLOCK_ROUTE, name="small_sums_start")

    sharing = {}
    for tag in ("b", "a1", "a2"):
        sums, arrived = _ici_wait(sent[tag], [grad_x, token], _SCATTER_ROUTE, name=f"grad_send_{tag}_wait")
        mine, bufs = _owner_sums(sums, arrived, jc_idx, name=f"grad_owner_sums_{tag}")
        sharing[tag], token = _ici_start(mine, bufs, token, _SHARE_ROUTE, name=f"grad_share_{tag}_start")
    loss = lax.psum(jnp.sum(token), MESH_AXES)

    def shared(tag, after):
        _, bufs = _ici_wait(sharing[tag], after, _SHARE_ROUTE, name=f"grad_share_{tag}_wait")
        return [b.reshape(-1, b.shape[2]) for b in bufs]

    grads, delta, new_m, new_v = {}, {}, {}, {}

    def adam(n):
        shape = a[n].shape
        d2 = (-1, shape[-1])
        d, m, v = _adamw(a[n].reshape(d2), grads[n].reshape(d2), a["m_" + n].reshape(d2),
                         a["v_" + n].reshape(d2), name="adamw_" + n)
        delta[n], new_m[n], new_v[n] = d.reshape(shape), m.reshape(shape), v.reshape(shape)
        return d

    r_kv, r_in_b, r_mk1, r_out1 = shared("b", token)
    grads["w_kv"], grads["w_in_b"] = r_kv, r_in_b[None]
    done = [adam("w_kv"), adam("w_in_b")]
    r_glu, r_mk0, r_out0 = shared("a1", done)
    grads["w_glu"], grads["w_mem_kv"], grads["w_out"] = r_glu[None], jnp.stack([r_mk0, r_mk1]), jnp.stack([r_out0, r_out1])
    done = [adam("w_glu"), adam("w_mem_kv"), adam("w_out")]
    (r_in_a,) = shared("a2", done)
    grads["w_in_a"] = r_in_a[None]
    adam("w_in_a")

    (pack,), (blocks,) = _ici_wait(small_sent, [delta[n] for n in _BIG], _BLOCK_ROUTE, name="small_sums_wait")
    blocks = lax.dynamic_update_slice(blocks, pack[None, None], (chip, ci, 0, 0))
    (blocks,) = _gather_forward([blocks], "small", own=True)
    small = dict(zip(small_names, _unpack(_sum_devices(blocks), [g[n].shape for n in small_names])))
    for n in _REPLICATED:
        grads[n] = small[n].reshape(a[n].shape)
    nd = MAIN_WIDTH // N_CHIPS
    grads["d_skip"] = lax.dynamic_slice(small["d_skip"], (chip * nd,), (nd,))[None]
    grads["b_glu"] = lax.dynamic_slice(small["b_glu"], (chip * nd,), (nd,))[None]
    nf = D_MODEL // N_CHIPS
    grads["w_fgate"] = lax.dynamic_slice(small["w_fgate"], (chip * nf, 0), (nf, FOX_HEADS))

    shapes = [a[n].shape for n in small_names]
    d, m, v = _adamw(_pack([a[n] for n in small_names]), _pack([grads[n] for n in small_names]),
                     _pack([a["m_" + n] for n in small_names]), _pack([a["v_" + n] for n in small_names]),
                     name="adamw_small")
    for n, dd, mm, vv in zip(small_names, _unpack(d, shapes), _unpack(m, shapes), _unpack(v, shapes)):
        delta[n], new_m[n], new_v[n] = dd, mm, vv

    return (loss, grad_x[None], *[grads[n] for n in _WEIGHTS], *[delta[n] for n in _WEIGHTS],
            *[new_m[n] for n in _WEIGHTS], *[new_v[n] for n in _WEIGHTS])
```

```python
import math

import jax
import jax.numpy as jnp
from jax import lax
from jax.experimental import pallas as pl
from jax.experimental.pallas import tpu as pltpu

F32 = jnp.float32
BF16 = jnp.bfloat16

D_MODEL = 2048
N_MEM = 256
MAIN_WIDTH = 1536
MEM_WIDTH = 512
IN_WIDTH = 2 * MAIN_WIDTH + 2 * MEM_WIDTH
HEAD_DIM = 128
FOX_HEADS = MAIN_WIDTH // HEAD_DIM
MEM_HEADS = MEM_WIDTH // HEAD_DIM
SSM_GROUP = 16
SSM_GROUPS = MAIN_WIDTH // SSM_GROUP
SSM_STATE = 64
GROUPS_PER_BLOCK = 8
SSM_BLOCKS = SSM_GROUPS // GROUPS_PER_BLOCK
STATE_COLS = GROUPS_PER_BLOCK * SSM_STATE
EPS = 1e-6
ADAM_LR = 0.001
ADAM_B1 = 0.9
ADAM_B2 = 0.999
ADAM_EPS = 1e-08
ADAM_WD = 0.01
ADAM_STEP = 10
N_CHIPS = 4
LANES = 128
SUBLANES = 8
VMEM_LIMIT_BYTES = 56 * 1024 * 1024
NEG_BIG = -1e30
MESH_AXES = ("x", "y", "c")


def _params(*sem):
    return pltpu.CompilerParams(dimension_semantics=sem if sem else None,
                                vmem_limit_bytes=VMEM_LIMIT_BYTES)


def _sigmoid(x):
    return 1.0 / (1.0 + jnp.exp(-x))


def _gelu(x):
    c = math.sqrt(2.0 / math.pi)
    return 0.5 * x * (1.0 + jnp.tanh(c * (x + 0.044715 * (x * x * x))))


def _gelu_grad(x):
    c = math.sqrt(2.0 / math.pi)
    t = jnp.tanh(c * (x + 0.044715 * (x * x * x)))
    return 0.5 * (1.0 + t) + 0.5 * x * (1.0 - t * t) * (c * (1.0 + 3.0 * 0.044715 * (x * x)))


def _silu_and_grad(z):
    s = _sigmoid(z)
    return z * s, s * (1.0 + z * (1.0 - s))


_TILE_CHOICES = (4096, 3072, 2048, 1536, 1024, 768, 512, 384, 256, LANES)


def _tile(n, cap):
    return next(c for c in _TILE_CHOICES if c <= cap and n % c == 0)


def _mm(a, b, *, name, ta=False, tb=False, out_dtype=F32, shards=1, tm=1024, tn=1024, tk=4096):
    if ta:
        K, M = a.shape
    else:
        M, K = a.shape
    if tb:
        N, kb = b.shape
    else:
        kb, N = b.shape
    assert K == kb, (a.shape, b.shape)
    ns = N // shards
    tm, tn, tk = _tile(M, tm), _tile(ns, tn), _tile(K, tk)
    assert M % tm == 0 and ns % tn == 0 and K % tk == 0 and N % shards == 0
    nk = K // tk
    dn = (((0 if ta else 1,), (1 if tb else 0,)), ((), ()))

    def body(a_ref, b_ref, o_ref, *acc):
        prod = lax.dot_general(a_ref[...].astype(BF16), b_ref[...].astype(BF16), dn, preferred_element_type=F32)
        if nk == 1:
            o_ref[...] = prod.astype(o_ref.dtype)
            return
        acc_ref, = acc
        k = pl.program_id(2)

        @pl.when(k == 0)
        def _():
            acc_ref[...] = jnp.zeros_like(acc_ref)

        acc_ref[...] += prod

        @pl.when(k == nk - 1)
        def _():
            o_ref[...] = acc_ref[...].astype(o_ref.dtype)

    a_spec = (pl.BlockSpec((tk, tm), lambda i, j, k: (k, i)) if ta
              else pl.BlockSpec((tm, tk), lambda i, j, k: (i, k)))
    b_spec = (pl.BlockSpec((tn, tk), lambda i, j, k: (j, k)) if tb
              else pl.BlockSpec((tk, tn), lambda i, j, k: (k, j)))
    if shards == 1:
        out_shape = jax.ShapeDtypeStruct((M, N), out_dtype)
        o_spec = pl.BlockSpec((tm, tn), lambda i, j, k: (i, j))
    else:
        nb = ns // tn
        out_shape = jax.ShapeDtypeStruct((shards, M, ns), out_dtype)
        o_spec = pl.BlockSpec((None, tm, tn), lambda i, j, k: (j // nb, i, j % nb))
    return pl.pallas_call(
        body, name=name, out_shape=out_shape,
        grid=(M // tm, N // tn, nk),
        in_specs=[a_spec, b_spec], out_specs=o_spec,
        scratch_shapes=[] if nk == 1 else [pltpu.VMEM((tm, tn), F32)],
        compiler_params=_params("parallel", "parallel", "arbitrary"),
    )(a, b)


def _rmsnorm_fwd(x, g, *, name, res=None, out_dtype=F32, tr=256):
    L, D = x.shape
    tr = min(tr, L)
    has_res = res is not None

    def body(*refs):
        if has_res:
            x_ref, g_ref, r_ref, o_ref = refs
        else:
            x_ref, g_ref, o_ref = refs
        xf = x_ref[...]
        r = lax.rsqrt(jnp.mean(xf * xf, axis=-1, keepdims=True) + EPS)
        y = xf * r * g_ref[...]
        if has_res:
            y = r_ref[...] + y
        o_ref[...] = y.astype(o_ref.dtype)

    row = pl.BlockSpec((tr, D), lambda i: (i, 0))
    vec = pl.BlockSpec((1, D), lambda i: (0, 0))
    ins = [x, g.reshape(1, D)] + ([res] if has_res else [])
    return pl.pallas_call(
        body, name=name, out_shape=jax.ShapeDtypeStruct((L, D), out_dtype),
        grid=(L // tr,), in_specs=[row, vec] + ([row] if has_res else []), out_specs=row,
        compiler_params=_params("parallel"),
    )(*ins)


def _rmsnorm_bwd(x, g, dy, *, name, adds=(), dx_dtype=F32, tr=256):
    L, D = x.shape
    tr = min(tr, L)
    dys = dy if isinstance(dy, tuple) else (dy,)
    n_dy, n_add = len(dys), len(adds)

    def body(*refs):
        x_ref, g_ref = refs[:2]
        dy_refs = refs[2:2 + n_dy]
        add_refs = refs[2 + n_dy:2 + n_dy + n_add]
        dx_ref, dg_ref = refs[2 + n_dy + n_add:]
        xf = x_ref[...]
        dyf = dy_refs[0][...].astype(F32)
        for d_ref in dy_refs[1:]:
            dyf = dyf + d_ref[...].astype(F32)
        r = lax.rsqrt(jnp.mean(xf * xf, axis=-1, keepdims=True) + EPS)
        gy = dyf * g_ref[...]
        c = jnp.mean(xf * gy, axis=-1, keepdims=True) * (r * r * r)
        dx = gy * r - xf * c
        for a_ref in add_refs:
            dx = dx + a_ref[...].astype(F32)
        dx_ref[...] = dx.astype(dx_ref.dtype)

        @pl.when(pl.program_id(0) == 0)
        def _():
            dg_ref[...] = jnp.zeros_like(dg_ref)

        dg_ref[...] += jnp.sum(dyf * xf * r, axis=0, keepdims=True)

    row = pl.BlockSpec((tr, D), lambda i: (i, 0))
    vec = pl.BlockSpec((1, D), lambda i: (0, 0))
    dx, dg = pl.pallas_call(
        body, name=name,
        out_shape=(jax.ShapeDtypeStruct((L, D), dx_dtype), jax.ShapeDtypeStruct((1, D), F32)),
        grid=(L // tr,), in_specs=[row, vec] + [row] * (n_dy + n_add), out_specs=(row, vec),
        compiler_params=_params("arbitrary"),
    )(x, g.reshape(1, D), *dys, *adds)
    return dx, dg.reshape(D)


def _rmsnorm_bwd_pair(x, g1, dy1, g2, dy2, *, name, adds=(), tr=256):
    L, D = x.shape
    tr = min(tr, L)
    dy1s = dy1 if isinstance(dy1, tuple) else (dy1,)
    n1, n_add = len(dy1s), len(adds)

    def body(*refs):
        x_ref, g1_ref, g2_ref = refs[:3]
        dy1_refs = refs[3:3 + n1]
        dy2_ref = refs[3 + n1]
        add_refs = refs[4 + n1:4 + n1 + n_add]
        dx_ref, dg1_ref, dg2_ref = refs[4 + n1 + n_add:]
        xf = x_ref[...]
        d1 = dy1_refs[0][...].astype(F32)
        for d_ref in dy1_refs[1:]:
            d1 = d1 + d_ref[...].astype(F32)
        d2 = dy2_ref[...].astype(F32)
        r = lax.rsqrt(jnp.mean(xf * xf, axis=-1, keepdims=True) + EPS)
        gy = d1 * g1_ref[...] + d2 * g2_ref[...]
        c = jnp.mean(xf * gy, axis=-1, keepdims=True) * (r * r * r)
        dx = gy * r - xf * c
        for a_ref in add_refs:
            dx = dx + a_ref[...].astype(F32)
        dx_ref[...] = dx

        @pl.when(pl.program_id(0) == 0)
        def _():
            dg1_ref[...] = jnp.zeros_like(dg1_ref)
            dg2_ref[...] = jnp.zeros_like(dg2_ref)

        xr = xf * r
        dg1_ref[...] += jnp.sum(d1 * xr, axis=0, keepdims=True)
        dg2_ref[...] += jnp.sum(d2 * xr, axis=0, keepdims=True)

    row = pl.BlockSpec((tr, D), lambda i: (i, 0))
    vec = pl.BlockSpec((1, D), lambda i: (0, 0))
    dx, dg1, dg2 = pl.pallas_call(
        body, name=name,
        out_shape=(jax.ShapeDtypeStruct((L, D), F32), jax.ShapeDtypeStruct((1, D), F32),
                   jax.ShapeDtypeStruct((1, D), F32)),
        grid=(L // tr,), in_specs=[row, vec, vec] + [row] * (n1 + 1 + n_add), out_specs=(row, vec, vec),
        compiler_params=_params("arbitrary"),
    )(x, g1.reshape(1, D), g2.reshape(1, D), *dy1s, dy2, *adds)
    return dx, dg1.reshape(D), dg2.reshape(D)


def _final_norm_loss(o, g, res, target, *, tr=256):
    L, D = o.shape
    tr = min(tr, L)

    def body(o_ref, g_ref, r_ref, t_ref, dh_ref, loss_ref):
        xf = o_ref[...]
        r = lax.rsqrt(jnp.mean(xf * xf, axis=-1, keepdims=True) + EPS)
        e = (r_ref[...] + xf * r * g_ref[...]) - t_ref[...]
        dh_ref[...] = e * (1.0 / D)

        @pl.when(pl.program_id(0) == 0)
        def _():
            loss_ref[...] = jnp.zeros_like(loss_ref)

        loss_ref[...] += jnp.sum(e * e, axis=0, keepdims=True) * (0.5 / D)

    row = pl.BlockSpec((tr, D), lambda i: (i, 0))
    vec = pl.BlockSpec((1, D), lambda i: (0, 0))
    dh, lp = pl.pallas_call(
        body, name="post_norm_1_loss",
        out_shape=(jax.ShapeDtypeStruct((L, D), F32), jax.ShapeDtypeStruct((1, D), F32)),
        grid=(L // tr,), in_specs=[row, vec, row, row], out_specs=(row, vec),
        compiler_params=_params("arbitrary"),
    )(o, g.reshape(1, D), res, target)
    return dh, lp


def _s5_coeffs(lr, li, ls):
    dt = jnp.exp(ls)
    mag = jnp.exp(lr * dt)
    ar = mag * jnp.cos(li * dt)
    ai = mag * jnp.sin(li * dt)
    den = lr * lr + li * li
    cr = ((ar - 1.0) * lr + ai * li) / den
    ci = (ai * lr - (ar - 1.0) * li) / den
    return dt, ar, ai, den, cr, ci


def _s5_prep(lam_re, lam_im, log_step, b_re_t, b_im_t):
    G, P = lam_re.shape
    H = b_re_t.shape[1]

    def body(lr_ref, li_ref, ls_ref, br_ref, bi_ref, ar_ref, ai_ref, bbr_ref, bbi_ref):
        _, ar, ai, _, cr, ci = _s5_coeffs(lr_ref[...], li_ref[...], ls_ref[...])
        ar_ref[...] = ar
        ai_ref[...] = ai
        br, bi = br_ref[...], bi_ref[...]
        crb, cib = cr[:, None, :], ci[:, None, :]
        bbr_ref[...] = crb * br - cib * bi
        bbi_ref[...] = crb * bi + cib * br

    return pl.pallas_call(
        body, name="s5_prep",
        out_shape=(jax.ShapeDtypeStruct((G, P), F32), jax.ShapeDtypeStruct((G, P), F32),
                   jax.ShapeDtypeStruct((G, H, P), F32), jax.ShapeDtypeStruct((G, H, P), F32)),
        compiler_params=_params(),
    )(lam_re, lam_im, log_step.reshape(G, 1), b_re_t, b_im_t)


def _s5_prep_bwd(lam_re, lam_im, log_step, b_re_t, b_im_t, d_ar, d_ai, d_bbr, d_bbi):
    G, P = lam_re.shape
    H = b_re_t.shape[1]

    def body(lr_ref, li_ref, ls_ref, br_ref, bi_ref, dar_ref, dai_ref, dbbr_ref, dbbi_ref,
             dlr_ref, dli_ref, dls_ref, dbr_ref, dbi_ref):
        lr, li = lr_ref[...], li_ref[...]
        dt, ar, ai, den, cr, ci = _s5_coeffs(lr, li, ls_ref[...])
        br, bi = br_ref[...], bi_ref[...]
        gbr, gbi = dbbr_ref[...], dbbi_ref[...]
        crb, cib = cr[:, None, :], ci[:, None, :]
        dbr_ref[...] = crb * gbr + cib * gbi
        dbi_ref[...] = crb * gbi - cib * gbr
        gcr = jnp.sum(br * gbr + bi * gbi, axis=1)
        gci = jnp.sum(br * gbi - bi * gbr, axis=1)
        ilr, ili = lr / den, -li / den
        gar = dar_ref[...] + (ilr * gcr + ili * gci)
        gai = dai_ref[...] + (ilr * gci - ili * gcr)
        qr, qi = cr * ilr - ci * ili, cr * ili + ci * ilr
        glr = -(qr * gcr + qi * gci)
        gli = -(qr * gci - qi * gcr)
        glr = glr + dt * (ar * gar + ai * gai)
        gli = gli + dt * (ar * gai - ai * gar)
        wr, wi = lr * ar - li * ai, lr * ai + li * ar
        gdt = jnp.sum(wr * gar + wi * gai, axis=1, keepdims=True)
        dlr_ref[...] = glr
        dli_ref[...] = gli
        dls_ref[...] = gdt * dt

    return pl.pallas_call(
        body, name="s5_prep_bwd",
        out_shape=(jax.ShapeDtypeStruct((G, P), F32), jax.ShapeDtypeStruct((G, P), F32),
                   jax.ShapeDtypeStruct((G, 1), F32),
                   jax.ShapeDtypeStruct((G, H, P), F32), jax.ShapeDtypeStruct((G, H, P), F32)),
        compiler_params=_params(),
    )(lam_re, lam_im, log_step.reshape(G, 1), b_re_t, b_im_t, d_ar, d_ai, d_bbr, d_bbi)


def _s5_block_mats(bbr_t, bbi_t, c_re, c_im):
    bmat = _s5_expand(bbr_t, bbi_t)
    cmat = jnp.transpose(_s5_expand(c_re, -c_im), (0, 2, 1))
    return bmat.astype(BF16), cmat.astype(BF16)


def _s5_diag_mask():
    r = lax.broadcasted_iota(jnp.int32, (LANES, 2 * STATE_COLS), 0) // SSM_GROUP
    c = (lax.broadcasted_iota(jnp.int32, (LANES, 2 * STATE_COLS), 1) % STATE_COLS) // SSM_STATE
    return (r == c).astype(F32)


def _s5_expand(re, im):
    re = jnp.tile(re.reshape(SSM_BLOCKS, LANES, SSM_STATE), (1, 1, GROUPS_PER_BLOCK))
    im = jnp.tile(im.reshape(SSM_BLOCKS, LANES, SSM_STATE), (1, 1, GROUPS_PER_BLOCK))
    return jnp.concatenate([re, im], axis=-1) * _s5_diag_mask()[None]


def _s5_unfold(dmat):
    d = dmat.reshape(SSM_GROUPS, SSM_GROUP, 2, SSM_STATE)
    return jnp.transpose(d, (2, 0, 1, 3))


def _s5_a_rows(ar, ai):
    a = jnp.concatenate([ar.reshape(SSM_BLOCKS, STATE_COLS), ai.reshape(SSM_BLOCKS, STATE_COLS)], axis=1)
    return jnp.broadcast_to(a[:, None, :], (SSM_BLOCKS, SUBLANES, 2 * STATE_COLS))


def _to_step_major(src_ref, dst_ref, seg):
    for s in range(SUBLANES):
        dst_ref[pl.ds(s, seg, stride=SUBLANES), :] = src_ref[pl.ds(seg * s, seg), :]


def _segment_rows(ref, s, seg):
    return ref[pl.ds(s, seg, stride=SUBLANES), :]


def _cmul(ar, ai, xr, xi):
    return ar * xr - ai * xi, ar * xi + ai * xr


def _s5_tables(a_ref, pw_s, pwr_s, S, seg):
    ar, ai = a_ref[:, :S], a_ref[:, S:]

    def step(i, c):
        pr, pi = c
        pw_s[i, :, :S] = pr
        pw_s[i, :, S:] = pi
        nr, ni = _cmul(ar, ai, pr, pi)
        pwr_s[seg - 1 - i, :, :S] = nr
        pwr_s[seg - 1 - i, :, S:] = ni
        return nr, ni

    pr, pi = lax.fori_loop(0, seg, step, (jnp.ones_like(ar), jnp.zeros_like(ai)))
    pw_s[seg, :, :S] = pr
    pw_s[seg, :, S:] = pi


def _s5_fwd(proj, bmat, cmat, a_rows, d_skip, *, tc=512):
    L = proj.shape[0]
    tc = min(tc, L)
    nt = L // tc
    seg = tc // SUBLANES
    S = STATE_COLS

    def body(u_ref, b_ref, c_ref, a_ref, d_ref, y_ref, yg_ref, xp_ref,
             bu_s, xp_s, pw_s, pwr_s, carry_s, e_s, up_s, yc_s):
        @pl.when(pl.program_id(1) == 0)
        def _():
            carry_s[...] = jnp.zeros_like(carry_s)
            _s5_tables(a_ref, pw_s, pwr_s, S, seg)

        ar, ai = a_ref[:, :S], a_ref[:, S:]
        _to_step_major(u_ref, up_s, seg)
        bu = jnp.dot(up_s[...].astype(BF16), b_ref[...], preferred_element_type=F32)
        bu_s[...] = bu.reshape(seg, SUBLANES, 2 * S)

        def step(i, carry):
            cr, ci = carry
            xp_s[i, :, :S] = cr
            xp_s[i, :, S:] = ci
            return ar * cr - ai * ci + bu_s[i, :, :S], ar * ci + ai * cr + bu_s[i, :, S:]

        zero = jnp.zeros((SUBLANES, S), F32)
        fr, fi = lax.fori_loop(0, seg, step, (zero, zero))
        pr, pi = pw_s[seg, 0:1, :S], pw_s[seg, 0:1, S:]
        er, ei = carry_s[0:1, :S], carry_s[0:1, S:]
        for s in range(SUBLANES):
            e_s[s:s + 1, :S] = er
            e_s[s:s + 1, S:] = ei
            tr, ti = _cmul(pr, pi, er, ei)
            er, ei = fr[s:s + 1] + tr, fi[s:s + 1] + ti
        carry_s[0:1, :S] = er
        carry_s[0:1, S:] = ei
        pw = pw_s[0:seg]
        tr, ti = _cmul(pw[:, :, :S], pw[:, :, S:], e_s[:, :S][None], e_s[:, S:][None])
        xl = xp_s[...]
        xp = jnp.concatenate([xl[:, :, :S] + tr, xl[:, :, S:] + ti], axis=-1).reshape(tc, 2 * S)
        xp_ref[...] = xp
        a1r, a1i = ar[0:1], ai[0:1]
        x_re = a1r * xp[:, :S] - a1i * xp[:, S:] + bu[:, :S]
        x_im = a1r * xp[:, S:] + a1i * xp[:, :S] + bu[:, S:]
        xs = jnp.concatenate([x_re, x_im], axis=1).astype(BF16)
        yc_s[...] = jnp.dot(xs, c_ref[...], preferred_element_type=F32)
        for s in range(SUBLANES):
            rows = pl.ds(seg * s, seg)
            y = _segment_rows(yc_s, s, seg) + d_ref[...] * u_ref[rows, :]
            y_ref[rows, :] = y
            yg_ref[rows, :] = _gelu(y).astype(BF16)

    return pl.pallas_call(
        body, name="s5_fwd",
        out_shape=(jax.ShapeDtypeStruct((L, MAIN_WIDTH), F32),
                   jax.ShapeDtypeStruct((L, MAIN_WIDTH), BF16),
                   jax.ShapeDtypeStruct((L, SSM_BLOCKS * 2 * S), F32)),
        grid=(SSM_BLOCKS, nt),
        in_specs=[pl.BlockSpec((tc, LANES), lambda b, t: (t, b)),
                  pl.BlockSpec((None, LANES, 2 * S), lambda b, t: (b, 0, 0)),
                  pl.BlockSpec((None, 2 * S, LANES), lambda b, t: (b, 0, 0)),
                  pl.BlockSpec((None, SUBLANES, 2 * S), lambda b, t: (b, 0, 0)),
                  pl.BlockSpec((1, LANES), lambda b, t: (0, b))],
        out_specs=(pl.BlockSpec((tc, LANES), lambda b, t: (t, b)),
                   pl.BlockSpec((tc, LANES), lambda b, t: (t, b)),
                   pl.BlockSpec((tc, 2 * S), lambda b, t: (t, b))),
        scratch_shapes=[pltpu.VMEM((seg, SUBLANES, 2 * S), F32),
                        pltpu.VMEM((seg, SUBLANES, 2 * S), F32),
                        pltpu.VMEM((seg + 1, SUBLANES, 2 * S), F32),
                        pltpu.VMEM((seg, SUBLANES, 2 * S), F32),
                        pltpu.VMEM((SUBLANES, 2 * S), F32),
                        pltpu.VMEM((SUBLANES, 2 * S), F32),
                        pltpu.VMEM((tc, LANES), F32),
                        pltpu.VMEM((tc, LANES), F32)],
        compiler_params=_params("parallel", "arbitrary"),
    )(proj, bmat, cmat, a_rows, d_skip.reshape(1, MAIN_WIDTH))


def _s5_bwd(proj, dyg_a, dyg_b, y, xp, bmat, cmat, a_rows, d_skip, dproj, *, tc=512):
    L = proj.shape[0]
    tc = min(tc, L)
    nt = L // tc
    seg = tc // SUBLANES
    S = STATE_COLS
    nn = (((1,), (1,)), ((), ()))
    tn = (((0,), (0,)), ((), ()))

    def fold_diagonal(acc_ref, mask_ref, fold_ref):
        x = acc_ref[...] * mask_ref[...]
        hi = x.astype(BF16)
        rest = x - hi.astype(F32)
        mid = rest.astype(BF16)
        low = (rest - mid.astype(F32)).astype(BF16)
        return sum(jnp.dot(piece, fold_ref[...], preferred_element_type=F32) for piece in (hi, mid, low))

    def body(u_ref, dyga_ref, dygb_ref, y_ref, xp_ref, b_ref, c_ref, a_ref, d_ref, mask_ref, fold_ref, dp_hbm,
             du_ref, dbd_ref, dcd_ref, da_ref, dd_ref,
             dl_s, pw_s, pwr_s, carry_s, e_s, up_s, dy_s, dyp_s, dup_s, db_ref, dc_ref):
        @pl.when(pl.program_id(1) == 0)
        def _():
            carry_s[...] = jnp.zeros_like(carry_s)
            db_ref[...] = jnp.zeros_like(db_ref)
            dc_ref[...] = jnp.zeros_like(dc_ref)
            da_ref[...] = jnp.zeros_like(da_ref)
            dd_ref[...] = jnp.zeros_like(dd_ref)
            _s5_tables(a_ref, pw_s, pwr_s, S, seg)

        ar, ai = a_ref[:, :S], a_ref[:, S:]
        a1r, a1i = ar[0:1], ai[0:1]
        u = u_ref[...]
        dy = (dyga_ref[...] + dygb_ref[...]) * _gelu_grad(y_ref[...])
        dy_s[...] = dy
        xp = xp_ref[...]
        _to_step_major(u_ref, up_s, seg)
        _to_step_major(dy_s, dyp_s, seg)
        ubp = up_s[...].astype(BF16)
        dyp = dyp_s[...].astype(BF16)
        bu = jnp.dot(ubp, b_ref[...], preferred_element_type=F32)
        x_re = a1r * xp[:, :S] - a1i * xp[:, S:] + bu[:, :S]
        x_im = a1r * xp[:, S:] + a1i * xp[:, :S] + bu[:, S:]
        xs = jnp.concatenate([x_re, x_im], axis=1).astype(BF16)
        dc_ref[...] += lax.dot_general(dyp, xs, tn, preferred_element_type=F32)
        dx = lax.dot_general(dyp, c_ref[...], nn, preferred_element_type=F32)
        dl_s[...] = dx.reshape(seg, SUBLANES, 2 * S)

        def step(k, carry):
            cr, ci = carry
            i = seg - 1 - k
            lr = dl_s[i, :, :S] + (ar * cr + ai * ci)
            li = dl_s[i, :, S:] + (ar * ci - ai * cr)
            dl_s[i, :, :S] = lr
            dl_s[i, :, S:] = li
            return lr, li

        zero = jnp.zeros((SUBLANES, S), F32)
        fr, fi = lax.fori_loop(0, seg, step, (zero, zero))
        pr, pi = pw_s[seg, 0:1, :S], pw_s[seg, 0:1, S:]
        er, ei = carry_s[0:1, :S], carry_s[0:1, S:]
        for s in range(SUBLANES - 1, -1, -1):
            e_s[s:s + 1, :S] = er
            e_s[s:s + 1, S:] = ei
            er, ei = fr[s:s + 1] + (pr * er + pi * ei), fi[s:s + 1] + (pr * ei - pi * er)
        carry_s[0:1, :S] = er
        carry_s[0:1, S:] = ei
        er, ei = e_s[:, :S][None], e_s[:, S:][None]
        pw = pwr_s[...]
        pwr, pwi = pw[:, :, :S], pw[:, :, S:]
        ll = dl_s[...]
        lam = jnp.concatenate([ll[:, :, :S] + (pwr * er + pwi * ei), ll[:, :, S:] + (pwr * ei - pwi * er)],
                              axis=-1).reshape(tc, 2 * S)
        l_re, l_im = lam[:, :S], lam[:, S:]
        da_ref[0:1, :S] += jnp.sum(l_re * xp[:, :S] + l_im * xp[:, S:], axis=0, keepdims=True)
        da_ref[0:1, S:] += jnp.sum(l_im * xp[:, :S] - l_re * xp[:, S:], axis=0, keepdims=True)
        lamb = lam.astype(BF16)
        dup_s[...] = lax.dot_general(lamb, b_ref[...], nn, preferred_element_type=F32)
        for s in range(SUBLANES):
            rows = pl.ds(seg * s, seg)
            du = _segment_rows(dup_s, s, seg) + d_ref[...] * dy_s[rows, :]
            du_ref[rows, :] = du.astype(du_ref.dtype)
        db_ref[...] += lax.dot_general(ubp, lamb, tn, preferred_element_type=F32)
        dd_ref[0:1, :] += jnp.sum(dy * u, axis=0, keepdims=True)

        @pl.when(pl.program_id(1) == nt - 1)
        def _():
            dbd_ref[...] = fold_diagonal(db_ref, mask_ref, fold_ref)
            dcd_ref[...] = fold_diagonal(dc_ref, mask_ref, fold_ref)

    rev = lambda b, t: (nt - 1 - t, b)
    col = jnp.arange(2 * S)
    fold = ((col // S * SSM_STATE + col % SSM_STATE)[:, None] == jnp.arange(LANES)[None, :]).astype(BF16)
    return pl.pallas_call(
        body, name="s5_bwd",
        out_shape=(jax.ShapeDtypeStruct(dproj.shape, dproj.dtype),
                   jax.ShapeDtypeStruct((SSM_BLOCKS, LANES, LANES), F32),
                   jax.ShapeDtypeStruct((SSM_BLOCKS, LANES, LANES), F32),
                   jax.ShapeDtypeStruct((SSM_BLOCKS, SUBLANES, 2 * S), F32),
                   jax.ShapeDtypeStruct((SUBLANES, MAIN_WIDTH), F32)),
        input_output_aliases={11: 0},
        grid=(SSM_BLOCKS, nt),
        in_specs=[pl.BlockSpec((tc, LANES), rev),
                  pl.BlockSpec((tc, LANES), rev),
                  pl.BlockSpec((tc, LANES), rev),
                  pl.BlockSpec((tc, LANES), rev),
                  pl.BlockSpec((tc, 2 * S), rev),
                  pl.BlockSpec((None, LANES, 2 * S), lambda b, t: (b, 0, 0)),
                  pl.BlockSpec((None, 2 * S, LANES), lambda b, t: (b, 0, 0)),
                  pl.BlockSpec((None, SUBLANES, 2 * S), lambda b, t: (b, 0, 0)),
                  pl.BlockSpec((1, LANES), lambda b, t: (0, b)),
                  pl.BlockSpec((LANES, 2 * S), lambda b, t: (0, 0)),
                  pl.BlockSpec((2 * S, LANES), lambda b, t: (0, 0)),
                  _ANY],
        out_specs=(pl.BlockSpec((tc, LANES), rev),
                   pl.BlockSpec((None, LANES, LANES), lambda b, t: (b, 0, 0)),
                   pl.BlockSpec((None, LANES, LANES), lambda b, t: (b, 0, 0)),
                   pl.BlockSpec((None, SUBLANES, 2 * S), lambda b, t: (b, 0, 0)),
                   pl.BlockSpec((SUBLANES, LANES), lambda b, t: (0, b))),
        scratch_shapes=[pltpu.VMEM((seg, SUBLANES, 2 * S), F32),
                        pltpu.VMEM((seg + 1, SUBLANES, 2 * S), F32),
                        pltpu.VMEM((seg, SUBLANES, 2 * S), F32),
                        pltpu.VMEM((SUBLANES, 2 * S), F32),
                        pltpu.VMEM((SUBLANES, 2 * S), F32),
                        pltpu.VMEM((tc, LANES), F32),
                        pltpu.VMEM((tc, LANES), F32),
                        pltpu.VMEM((tc, LANES), F32),
                        pltpu.VMEM((tc, LANES), F32),
                        pltpu.VMEM((LANES, 2 * S), F32),
                        pltpu.VMEM((LANES, 2 * S), F32)],
        compiler_params=_params("parallel", "arbitrary"),
    )(proj, dyg_a, dyg_b, y, xp, bmat, cmat, a_rows, d_skip.reshape(1, MAIN_WIDTH), _s5_diag_mask(), fold, dproj)


_Z_COLS = slice(MAIN_WIDTH, 2 * MAIN_WIDTH)
_ZM_COLS = slice(2 * MAIN_WIDTH + MEM_WIDTH, IN_WIDTH)


def _proj_rows(tr):
    return pl.BlockSpec((tr, IN_WIDTH), lambda i: (i, 0))


def _row_specs(tr):
    main = pl.BlockSpec((tr, MAIN_WIDTH), lambda i: (i, 0))
    z = pl.BlockSpec((tr, MAIN_WIDTH), lambda i: (i, 1))
    zm = pl.BlockSpec((tr, MEM_WIDTH), lambda i: (i, IN_WIDTH // MEM_WIDTH - 1))
    mem = pl.BlockSpec((tr, MEM_WIDTH), lambda i: (i, 0))
    cat = pl.BlockSpec((tr, D_MODEL), lambda i: (i, 0))
    vec = pl.BlockSpec((1, MAIN_WIDTH), lambda i: (0, 0))
    return main, z, zm, mem, cat, vec


def _gate_a_fwd(y, t, b_glu, proj, o_mem, *, tr=256):
    L = y.shape[0]
    tr = min(tr, L)

    def body(y_ref, t_ref, b_ref, z_ref, zm_ref, om_ref, o_ref):
        yg = _gelu(y_ref[...])
        sz, _ = _silu_and_grad(z_ref[...])
        o_ref[:, :MAIN_WIDTH] = (yg * _sigmoid(t_ref[...] + b_ref[...]) * sz).astype(BF16)
        szm, _ = _silu_and_grad(zm_ref[...])
        o_ref[:, MAIN_WIDTH:] = (om_ref[...] * szm).astype(BF16)

    main, z, zm, mem, cat, vec = _row_specs(tr)
    return pl.pallas_call(
        body, name="gate_a_fwd", out_shape=jax.ShapeDtypeStruct((L, D_MODEL), BF16),
        grid=(L // tr,), in_specs=[main, main, vec, z, zm, mem], out_specs=cat,
        compiler_params=_params("parallel"),
    )(y, t, b_glu.reshape(1, MAIN_WIDTH), proj, proj, o_mem)


def _gate_a_bwd(dcat, y, t, b_glu, proj, o_mem, *, tr=256):
    L = y.shape[0]
    tr = min(tr, L)

    def body(dc_ref, y_ref, t_ref, b_ref, z_ref, zm_ref, om_ref,
             dp_ref, dt_ref, dyg_ref, dom_ref, db_ref):
        dmain = dc_ref[:, :MAIN_WIDTH]
        dmemo = dc_ref[:, MAIN_WIDTH:]
        yg = _gelu(y_ref[...])
        sg = _sigmoid(t_ref[...] + b_ref[...])
        sz, gz = _silu_and_grad(z_ref[...])
        dp_ref[:, _Z_COLS] = (dmain * (yg * sg) * gz).astype(BF16)
        dy2 = dmain * sz
        dyg_ref[...] = dy2 * sg
        dt = dy2 * yg * (sg * (1.0 - sg))
        dt_ref[...] = dt.astype(BF16)

        @pl.when(pl.program_id(0) == 0)
        def _():
            db_ref[...] = jnp.zeros_like(db_ref)

        db_ref[...] += jnp.sum(dt, axis=0, keepdims=True)
        szm, gzm = _silu_and_grad(zm_ref[...])
        dom_ref[...] = dmemo * szm
        dp_ref[:, _ZM_COLS] = (dmemo * om_ref[...] * gzm).astype(BF16)

    main, z, zm, mem, cat, vec = _row_specs(tr)
    outs = pl.pallas_call(
        body, name="gate_a_bwd",
        out_shape=(jax.ShapeDtypeStruct((L, IN_WIDTH), BF16),
                   jax.ShapeDtypeStruct((L, MAIN_WIDTH), BF16), jax.ShapeDtypeStruct((L, MAIN_WIDTH), F32),
                   jax.ShapeDtypeStruct((L, MEM_WIDTH), F32), jax.ShapeDtypeStruct((1, MAIN_WIDTH), F32)),
        grid=(L // tr,), in_specs=[cat, main, main, vec, z, zm, mem],
        out_specs=(_proj_rows(tr), main, main, mem, vec),
        compiler_params=_params("arbitrary"),
    )(dcat, y, t, b_glu.reshape(1, MAIN_WIDTH), proj, proj, o_mem)
    return outs


def _gate_b_fwd(att, proj, o_mem, *, tr=256):
    L = att.shape[0]
    tr = min(tr, L)

    def body(a_ref, z_ref, zm_ref, om_ref, o_ref):
        sz, _ = _silu_and_grad(z_ref[...])
        o_ref[:, :MAIN_WIDTH] = (a_ref[...] * sz).astype(BF16)
        szm, _ = _silu_and_grad(zm_ref[...])
        o_ref[:, MAIN_WIDTH:] = (om_ref[...] * szm).astype(BF16)

    main, z, zm, mem, cat, _ = _row_specs(tr)
    return pl.pallas_call(
        body, name="gate_b_fwd", out_shape=jax.ShapeDtypeStruct((L, D_MODEL), BF16),
        grid=(L // tr,), in_specs=[main, z, zm, mem], out_specs=cat,
        compiler_params=_params("parallel"),
    )(att, proj, proj, o_mem)


def _gate_b_bwd(dcat, att, proj, o_mem, *, tr=256):
    L = att.shape[0]
    tr = min(tr, L)

    def body(dc_ref, a_ref, z_ref, zm_ref, om_ref, da_ref, dp_ref, dom_ref, dl_ref):
        dmain = dc_ref[:, :MAIN_WIDTH]
        dmemo = dc_ref[:, MAIN_WIDTH:]
        att = a_ref[...]
        sz, gz = _silu_and_grad(z_ref[...])
        datt = dmain * sz
        da_ref[...] = datt
        dp_ref[:, _Z_COLS] = (dmain * att * gz).astype(BF16)
        szm, gzm = _silu_and_grad(zm_ref[...])
        dom_ref[...] = dmemo * szm
        dp_ref[:, _ZM_COLS] = (dmemo * om_ref[...] * gzm).astype(BF16)
        prod = datt * att
        for h in range(FOX_HEADS):
            dl_ref[h] = jnp.sum(prod[:, h * HEAD_DIM:(h + 1) * HEAD_DIM], axis=1, keepdims=True)

    main, z, zm, mem, cat, _ = _row_specs(tr)
    delta = pl.BlockSpec((FOX_HEADS, tr, 1), lambda i: (0, i, 0))
    return pl.pallas_call(
        body, name="gate_b_bwd",
        out_shape=(jax.ShapeDtypeStruct((L, MAIN_WIDTH), F32), jax.ShapeDtypeStruct((L, IN_WIDTH), BF16),
                   jax.ShapeDtypeStruct((L, MEM_WIDTH), F32), jax.ShapeDtypeStruct((FOX_HEADS, L, 1), F32)),
        grid=(L // tr,), in_specs=[cat, main, z, zm, mem], out_specs=(main, _proj_rows(tr), mem, delta),
        compiler_params=_params("parallel"),
    )(dcat, att, proj, proj, o_mem)


_MEM_Q_COL = (2 * MAIN_WIDTH) // HEAD_DIM
_NT = (((1,), (1,)), ((), ()))
_TN = (((0,), (0,)), ((), ()))


def _mem_probs(q_ref, k_ref):
    qs = (q_ref[...] * (HEAD_DIM ** -0.5)).astype(BF16)
    s = lax.dot_general(qs, k_ref[...].astype(BF16), _NT, preferred_element_type=F32)
    e = jnp.exp(s - jnp.max(s, axis=-1, keepdims=True))
    return qs, e / jnp.sum(e, axis=-1, keepdims=True)


def _mem_attn_fwd(proj, kvm, *, tq=2048):
    L = proj.shape[0]
    tq = min(tq, L)

    def body(q_ref, k_ref, v_ref, o_ref):
        _, p = _mem_probs(q_ref, k_ref)
        o_ref[...] = jnp.dot(p.astype(BF16), v_ref[...].astype(BF16), preferred_element_type=F32)

    return pl.pallas_call(
        body, name="mem_attn_fwd", out_shape=jax.ShapeDtypeStruct((L, MEM_WIDTH), F32),
        grid=(MEM_HEADS, L // tq),
        in_specs=[pl.BlockSpec((tq, HEAD_DIM), lambda h, i: (i, _MEM_Q_COL + h)),
                  pl.BlockSpec((N_MEM, HEAD_DIM), lambda h, i: (0, h)),
                  pl.BlockSpec((N_MEM, HEAD_DIM), lambda h, i: (0, MEM_HEADS + h))],
        out_specs=pl.BlockSpec((tq, HEAD_DIM), lambda h, i: (i, h)),
        compiler_params=_params("parallel", "parallel"),
    )(proj, kvm, kvm)


def _mem_attn_bwd(proj, kvm, do, dproj, *, tq=2048):
    L = proj.shape[0]
    tq = min(tq, L)

    def body(q_ref, k_ref, v_ref, do_ref, dp_hbm, dq_ref, dk_ref, dv_ref):
        @pl.when(pl.program_id(1) == 0)
        def _():
            dk_ref[...] = jnp.zeros_like(dk_ref)
            dv_ref[...] = jnp.zeros_like(dv_ref)

        qs, p = _mem_probs(q_ref, k_ref)
        dob = do_ref[...].astype(BF16)
        dp = lax.dot_general(dob, v_ref[...].astype(BF16), _NT, preferred_element_type=F32)
        ds = p * (dp - jnp.sum(p * dp, axis=-1, keepdims=True))
        dsb = ds.astype(BF16)
        dq = jnp.dot(dsb, k_ref[...].astype(BF16), preferred_element_type=F32) * (HEAD_DIM ** -0.5)
        dq_ref[...] = dq.astype(BF16)
        dk_ref[...] += lax.dot_general(dsb, qs, _TN, preferred_element_type=F32)
        dv_ref[...] += lax.dot_general(p.astype(BF16), dob, _TN, preferred_element_type=F32)

    dproj, dk, dv = pl.pallas_call(
        body, name="mem_attn_bwd",
        out_shape=(jax.ShapeDtypeStruct(dproj.shape, dproj.dtype),
                   jax.ShapeDtypeStruct((N_MEM, MEM_WIDTH), F32),
                   jax.ShapeDtypeStruct((N_MEM, MEM_WIDTH), F32)),
        grid=(MEM_HEADS, L // tq),
        in_specs=[pl.BlockSpec((tq, HEAD_DIM), lambda h, i: (i, _MEM_Q_COL + h)),
                  pl.BlockSpec((N_MEM, HEAD_DIM), lambda h, i: (0, h)),
                  pl.BlockSpec((N_MEM, HEAD_DIM), lambda h, i: (0, MEM_HEADS + h)),
                  pl.BlockSpec((tq, HEAD_DIM), lambda h, i: (i, h)),
                  _ANY],
        out_specs=(pl.BlockSpec((tq, HEAD_DIM), lambda h, i: (i, _MEM_Q_COL + h)),
                   pl.BlockSpec((N_MEM, HEAD_DIM), lambda h, i: (0, h)),
                   pl.BlockSpec((N_MEM, HEAD_DIM), lambda h, i: (0, h))),
        input_output_aliases={4: 0},
        compiler_params=_params("parallel", "arbitrary"),
    )(proj, kvm, kvm, do, dproj)
    return dproj, jnp.concatenate([dk, dv], axis=1)


def _tile_cumsum(x, row, reverse):
    for sh in (1, 2, 4):
        if reverse:
            x = x + jnp.where(row < SUBLANES - sh, pltpu.roll(x, SUBLANES - sh, 0), 0.0)
        else:
            x = x + jnp.where(row >= sh, pltpu.roll(x, sh, 0), 0.0)
    return x


def _fgate_fwd(pre, b_pad):
    L = pre.shape[0]
    n8 = L // SUBLANES

    def body(p_ref, b_ref, o_ref):
        row = lax.broadcasted_iota(jnp.int32, (SUBLANES, LANES), 0)
        b = b_ref[...]

        def step(i, carry):
            x = p_ref[i] + b
            logf = jnp.minimum(x, 0.0) - jnp.log(1.0 + jnp.exp(-jnp.abs(x)))
            t = _tile_cumsum(logf, row, False) + carry
            o_ref[i] = t
            return t[SUBLANES - 1:SUBLANES, :]

        lax.fori_loop(0, n8, step, jnp.zeros((1, LANES), F32))

    out = pl.pallas_call(
        body, name="fgate_fwd", out_shape=jax.ShapeDtypeStruct((n8, SUBLANES, LANES), F32),
        compiler_params=_params(),
    )(pre.reshape(n8, SUBLANES, LANES), b_pad.reshape(1, LANES))
    return out.reshape(L, LANES)


def _fgate_bwd(dfcum, pre, b_pad):
    L = pre.shape[0]
    n8 = L // SUBLANES

    def body(d_ref, p_ref, b_ref, o_ref, s_ref):
        row = lax.broadcasted_iota(jnp.int32, (SUBLANES, LANES), 0)
        b = b_ref[...]

        def step(k, carry):
            c, acc = carry
            i = n8 - 1 - k
            t = _tile_cumsum(d_ref[i], row, True) + c
            dpre = t * _sigmoid(-(p_ref[i] + b))
            o_ref[i] = dpre
            return t[0:1, :], acc + dpre

        _, acc = lax.fori_loop(0, n8, step, (jnp.zeros((1, LANES), F32), jnp.zeros((SUBLANES, LANES), F32)))
        s_ref[...] = jnp.sum(acc, axis=0, keepdims=True)

    dpre, db = pl.pallas_call(
        body, name="fgate_bwd",
        out_shape=(jax.ShapeDtypeStruct((n8, SUBLANES, LANES), F32), jax.ShapeDtypeStruct((1, LANES), F32)),
        compiler_params=_params(),
    )(dfcum.reshape(n8, SUBLANES, LANES), pre.reshape(n8, SUBLANES, LANES), b_pad.reshape(1, LANES))
    return dpre.reshape(L, LANES), db


FOX_BLOCK = 1024


def _fox_scores(qs, k, fk, diagonal, row0=0):
    s = lax.dot_general(qs, k, _NT, preferred_element_type=F32) - fk
    if diagonal:
        row = row0 + lax.broadcasted_iota(jnp.int32, s.shape, 0)
        col = lax.broadcasted_iota(jnp.int32, s.shape, 1)
        s = jnp.where(row >= col, s, NEG_BIG)
    return s


def _fox_diagonal_parts(tq):
    half = tq // 2
    return ((slice(0, half), half), (slice(half, tq), tq))


def _fox_specs(tq, L):
    nq = L // tq
    return dict(
        rows=lambda off: pl.BlockSpec((tq, HEAD_DIM), lambda h, i: (i, off + h)),
        seq=lambda off: pl.BlockSpec((L, HEAD_DIM), lambda h, i: (0, off + h)),
        col=pl.BlockSpec((None, None, tq, 1), lambda h, i: (h, i, 0, 0)),
        col_all=pl.BlockSpec((None, nq, tq, 1), lambda h, i: (h, 0, 0, 0)),
        row=pl.BlockSpec((None, None, 1, tq), lambda h, i: (h, i, 0, 0)),
        row_all=pl.BlockSpec((None, nq, 1, tq), lambda h, i: (h, 0, 0, 0)))


FOX_FWD_HEADS = 2
FOX_FWD_BLOCK = 1024


def _fox_fwd(proj, kv, fk):
    L = proj.shape[0]
    tq = min(FOX_FWD_BLOCK, L)
    nq = L // tq
    nh = FOX_FWD_HEADS
    W = nh * HEAD_DIM
    lse_shape = fk.shape[:2] + (fk.shape[3], 1)
    fk = fk.reshape(FOX_HEADS, nq, 1, tq)

    def body(q_ref, k_ref, v_ref, fk_ref, o_ref, lse_ref, m_s, l_s, acc_s):
        qi = pl.program_id(1)
        cols = [slice(a * HEAD_DIM, (a + 1) * HEAD_DIM) for a in range(nh)]
        qs = [(q_ref[:, cs] * (HEAD_DIM ** -0.5)).astype(BF16) for cs in cols]
        m_s[...] = jnp.full_like(m_s, NEG_BIG)
        l_s[...] = jnp.zeros_like(l_s)
        acc_s[...] = jnp.zeros_like(acc_s)

        def block(j, diagonal):
            r0 = pl.multiple_of(j * tq, tq)
            for a, cs in enumerate(cols):
                s = _fox_scores(qs[a], k_ref[pl.ds(r0, tq), cs], fk_ref[a, j], diagonal)
                m_new = jnp.maximum(m_s[a], jnp.max(s, axis=-1, keepdims=True))
                alpha = jnp.exp(m_s[a] - m_new)
                p = jnp.exp(s - m_new)
                l_s[a] = alpha * l_s[a] + jnp.sum(p, axis=-1, keepdims=True)
                acc_s[a] = alpha * acc_s[a] + jnp.dot(p.astype(BF16), v_ref[pl.ds(r0, tq), cs],
                                                      preferred_element_type=F32)
                m_s[a] = m_new

        def below(j, carry):
            block(j, False)
            return carry

        lax.fori_loop(0, qi, below, 0)
        block(qi, True)
        for a, cs in enumerate(cols):
            o_ref[:, cs] = acc_s[a] / l_s[a]
            lse_ref[a] = m_s[a] + jnp.log(l_s[a])

    att, lse = pl.pallas_call(
        body, name="fox_fwd",
        out_shape=(jax.ShapeDtypeStruct((L, MAIN_WIDTH), F32),
                   jax.ShapeDtypeStruct((FOX_HEADS, nq, tq, 1), F32)),
        grid=(FOX_HEADS // nh, nq),
        in_specs=[pl.BlockSpec((tq, W), lambda h, i: (i, h)),
                  pl.BlockSpec((L, W), lambda h, i: (0, h)),
                  pl.BlockSpec((L, W), lambda h, i: (0, FOX_HEADS // nh + h)),
                  pl.BlockSpec((nh, nq, 1, tq), lambda h, i: (h, 0, 0, 0))],
        out_specs=(pl.BlockSpec((tq, W), lambda h, i: (i, h)),
                   pl.BlockSpec((nh, None, tq, 1), lambda h, i: (h, i, 0, 0))),
        scratch_shapes=[pltpu.VMEM((nh, tq, 1), F32), pltpu.VMEM((nh, tq, 1), F32),
                        pltpu.VMEM((nh, tq, HEAD_DIM), F32)],
        compiler_params=_params("parallel", "parallel"),
    )(proj, kv, kv, fk)
    return att, lse.reshape(lse_shape)


def _fox_bwd(proj, kv, fk, lse, delta, datt, dproj):
    L = proj.shape[0]
    tq = min(FOX_BLOCK, L)
    nq = L // tq
    sp = _fox_specs(tq, L)

    def body(q_ref, k_ref, v_ref, fk_ref, lse_ref, dl_ref, do_ref, dp_hbm,
             dq_ref, dk_ref, dv_ref, dfq_ref, dfk_ref, dk_s, dv_s, df_s, dq_s, dfq_s):
        ki = pl.program_id(1)

        @pl.when(ki == 0)
        def _():
            dq_s[...] = jnp.zeros_like(dq_s)
            dfq_s[...] = jnp.zeros_like(dfq_s)

        k, v, fk = k_ref[...], v_ref[...], fk_ref[...]
        dk_s[...] = jnp.zeros_like(dk_s)
        dv_s[...] = jnp.zeros_like(dv_s)
        df_s[...] = jnp.zeros_like(df_s)

        def block(i, rows, width, diagonal):
            n = rows.stop - rows.start
            r0 = pl.multiple_of(i * tq + rows.start, n)
            qs = (q_ref[pl.ds(r0, n), :] * (HEAD_DIM ** -0.5)).astype(BF16)
            dob = do_ref[pl.ds(r0, n), :].astype(BF16)
            kw, vw = k[:width], v[:width]
            p = jnp.exp(_fox_scores(qs, kw, fk[:, :width], diagonal, rows.start) - lse_ref[i][rows])
            dp = lax.dot_general(dob, vw, _NT, preferred_element_type=F32)
            ds = p * (dp - dl_ref[i][rows])
            dsb = ds.astype(BF16)
            dv_s[:width] += lax.dot_general(p.astype(BF16), dob, _TN, preferred_element_type=F32)
            dk_s[:width] += lax.dot_general(dsb, qs, _TN, preferred_element_type=F32)
            df_s[:, :width] -= jnp.sum(ds, axis=0, keepdims=True)
            dq_s[i, rows] += jnp.dot(dsb, kw, preferred_element_type=F32)
            dfq_s[i, rows] += jnp.sum(ds, axis=1, keepdims=True)

        def above(i, carry):
            block(i, slice(0, tq), tq, False)
            return carry

        for rows, width in _fox_diagonal_parts(tq):
            block(ki, rows, width, True)
        lax.fori_loop(ki + 1, nq, above, 0)
        dk_ref[...] = dk_s[...].astype(BF16)
        dv_ref[...] = dv_s[...].astype(BF16)
        dfk_ref[...] = df_s[...]

        @pl.when(ki == nq - 1)
        def _():
            dq_ref[...] = (dq_s[...].reshape(L, HEAD_DIM) * (HEAD_DIM ** -0.5)).astype(BF16)
            dfq_ref[...] = dfq_s[...]

    return pl.pallas_call(
        body, name="fox_bwd",
        out_shape=(jax.ShapeDtypeStruct(dproj.shape, dproj.dtype),
                   jax.ShapeDtypeStruct((L, MAIN_WIDTH), BF16),
                   jax.ShapeDtypeStruct((L, MAIN_WIDTH), BF16),
                   jax.ShapeDtypeStruct((FOX_HEADS, nq, tq, 1), F32),
                   jax.ShapeDtypeStruct((FOX_HEADS, nq, 1, tq), F32)),
        grid=(FOX_HEADS, nq),
        in_specs=[sp["seq"](0), sp["rows"](0), sp["rows"](FOX_HEADS), sp["row"],
                  sp["col_all"], sp["col_all"], sp["seq"](0), _ANY],
        out_specs=(sp["seq"](0), sp["rows"](0), sp["rows"](0), sp["col_all"], sp["row"]),
        input_output_aliases={7: 0},
        scratch_shapes=[pltpu.VMEM((tq, HEAD_DIM), F32), pltpu.VMEM((tq, HEAD_DIM), F32),
                        pltpu.VMEM((1, tq), F32), pltpu.VMEM((nq, tq, HEAD_DIM), F32),
                        pltpu.VMEM((nq, tq, 1), F32)],
        compiler_params=_params("parallel", "arbitrary"),
    )(proj, kv, kv, fk, lse, delta, datt, dproj)


def _pad_lanes(a):
    return jnp.pad(a, ((0, 0), (0, LANES - a.shape[1])))


def _mem_branch_fwd(memn, w_mk, proj, tag):
    kvm = _mm(memn, w_mk, name="mem_kv_" + tag)
    return kvm, _mem_attn_fwd(proj, kvm)


def _mem_branch_bwd(mem, g, w_mk, proj, memn, kvm, do_mem, dproj, tag):
    dproj, dkvm = _mem_attn_bwd(proj, kvm, do_mem, dproj)
    dkvm = dkvm.astype(BF16)
    dw_mk = _mm(memn, dkvm, ta=True, name="dw_mem_kv_" + tag, out_dtype=BF16)
    dmemn = _mm(dkvm, w_mk, tb=True, name="dmemn_" + tag)
    _, dg = _rmsnorm_bwd(mem, g, dmemn, name="mem_norm_bwd_" + tag, dx_dtype=BF16)
    return dproj, dw_mk, dg


def _local_step(x, mem, target, w, fetch=None, grads_ready=None):
    if grads_ready is None:
        grads_ready = lambda group, grads, token: token
    L = x.shape[0]
    g = {}
    w = dict(w)

    b_re_t = jnp.transpose(w["b_re"], (0, 2, 1))
    b_im_t = jnp.transpose(w["b_im"], (0, 2, 1))
    ar, ai, bbr_t, bbi_t = _s5_prep(w["lam_re"], w["lam_im"], w["log_step"], b_re_t, b_im_t)
    bmat, cmat = _s5_block_mats(bbr_t, bbi_t, w["c_re"], w["c_im"])
    a_rows = _s5_a_rows(ar, ai)

    hn0 = _rmsnorm_fwd(x, w["pre_norm_g"][0], name="pre_norm_0", out_dtype=BF16)
    memn0 = _rmsnorm_fwd(mem, w["mem_norm_g"][0], name="mem_norm_0", out_dtype=BF16)
    memn1 = _rmsnorm_fwd(mem, w["mem_norm_g"][1], name="mem_norm_1", out_dtype=BF16)
    if fetch is not None:
        w.update(fetch("a", [hn0, memn0, memn1, bmat, cmat, a_rows]))
    proj_a = _mm(hn0, w["w_in_a"], name="in_proj_a")
    y, yg, xp = _s5_fwd(proj_a, bmat, cmat, a_rows, w["d_skip"])
    if fetch is not None:
        w.update(fetch("b", yg))
    t = _mm(yg, w["w_glu"], name="glu_proj")
    kvm0, om0 = _mem_branch_fwd(memn0, w["w_mem_kv"][0], proj_a, "0")
    cat0 = _gate_a_fwd(y, t, w["b_glu"], proj_a, om0)
    o0 = _mm(cat0, w["w_out"][0], name="out_proj_0")
    h1 = _rmsnorm_fwd(o0, w["post_norm_g"][0], res=x, name="post_norm_0")

    kv_in = _rmsnorm_fwd(h1, w["kv_norm_g"], name="kv_norm", out_dtype=BF16)
    if fetch is not None:
        w.update(fetch("c", kv_in))
    kv = _mm(kv_in, w["w_kv"], name="kv_proj", out_dtype=BF16)
    pre_f = _mm(kv_in, w["w_fgate"], name="fgate_proj")
    b_f = jnp.pad(w["b_fgate"], (0, LANES - FOX_HEADS))
    fcum = _fgate_fwd(pre_f, b_f)
    fc = jnp.transpose(fcum[:, :FOX_HEADS])
    tq = min(FOX_BLOCK, L)
    fk = fc.reshape(FOX_HEADS, L // tq, 1, tq)

    hn1 = _rmsnorm_fwd(h1, w["pre_norm_g"][1], name="pre_norm_1", out_dtype=BF16)
    proj_b = _mm(hn1, w["w_in_b"], name="in_proj_b")
    att, lse = _fox_fwd(proj_b, kv, fk)
    kvm1, om1 = _mem_branch_fwd(memn1, w["w_mem_kv"][1], proj_b, "1")
    cat1 = _gate_b_fwd(att, proj_b, om1)
    o1 = _mm(cat1, w["w_out"][1], name="out_proj_1")
    dh2, loss_row = _final_norm_loss(o1, w["post_norm_g"][1], h1, target)

    do1, dpost1 = _rmsnorm_bwd(o1, w["post_norm_g"][1], dh2, name="post_norm_bwd_1", dx_dtype=BF16)
    dcat1 = _mm(do1, w["w_out"][1], tb=True, name="dcat_1", out_dtype=BF16)
    g["w_out_1"] = _mm(cat1, do1, ta=True, name="dw_out_1", out_dtype=BF16)
    datt, dproj_b, dom1, delta = _gate_b_bwd(dcat1, att, proj_b, om1)
    dproj_b, g["w_mem_kv_1"], dmemg1 = _mem_branch_bwd(mem, w["mem_norm_g"][1], w["w_mem_kv"][1], proj_b,
                                                      memn1, kvm1, dom1, dproj_b, "1")
    delta = delta.reshape(lse.shape)
    dproj_b, dk, dv, dfq, dfk = _fox_bwd(proj_b, kv, fk, lse, delta, datt, dproj_b)
    g["w_in_b"] = _mm(hn1, dproj_b, ta=True, name="dw_in_b", out_dtype=BF16, shards=N_CHIPS)
    dhn1 = _mm(dproj_b, w["w_in_b"], tb=True, name="dhn_1")

    dkv = jnp.concatenate([dk, dv], axis=1)
    g["w_kv"] = _mm(kv_in, dkv, ta=True, name="dw_kv", out_dtype=BF16, shards=N_CHIPS)
    dkv_in_a = _mm(dkv, w["w_kv"], tb=True, name="dkv_in_kv")
    dfcum = _pad_lanes(jnp.transpose(dfq.reshape(FOX_HEADS, L) + dfk.reshape(FOX_HEADS, L)))
    dpre_f, db_f = _fgate_bwd(dfcum, pre_f, b_f)
    g["b_fgate"] = db_f[0, :FOX_HEADS]
    g["w_fgate"] = _mm(kv_in, dpre_f, ta=True, name="dw_fgate")[:, :FOX_HEADS]
    dkv_in_b = _mm(dpre_f, w["w_fgate"], tb=True, name="dkv_in_fgate")
    dh1, g["kv_norm_g"], dpre1 = _rmsnorm_bwd_pair(h1, w["kv_norm_g"], (dkv_in_a, dkv_in_b), w["pre_norm_g"][1],
                                                   dhn1, adds=(dh2,), name="kv_pre_norm_bwd")
    dh1 = grads_ready("b", g, dh1)

    do0, dpost0 = _rmsnorm_bwd(o0, w["post_norm_g"][0], dh1, name="post_norm_bwd_0", dx_dtype=BF16)
    dcat0 = _mm(do0, w["w_out"][0], tb=True, name="dcat_0", out_dtype=BF16)
    g["w_out_0"] = _mm(cat0, do0, ta=True, name="dw_out_0", out_dtype=BF16)
    dcat0 = grads_ready("b_send", g, dcat0)
    dproj_a, dt, dyg_a, dom0, db_glu = _gate_a_bwd(dcat0, y, t, w["b_glu"], proj_a, om0)
    g["b_glu"] = db_glu[0]
    g["w_glu"] = _mm(yg, dt, ta=True, name="dw_glu", out_dtype=BF16)
    dyg_b = _mm(dt, w["w_glu"], tb=True, name="dyg")
    dproj_a, g["w_mem_kv_0"], dmemg0 = _mem_branch_bwd(mem, w["mem_norm_g"][0], w["w_mem_kv"][0], proj_a,
                                                      memn0, kvm0, dom0, dproj_a, "0")
    dyg_b = grads_ready("a1", g, dyg_b)
    dproj_a, db_blk, dc_blk, da_rows, dd_skip = _s5_bwd(proj_a, dyg_a, dyg_b, y, xp, bmat, cmat, a_rows,
                                                        w["d_skip"], dproj_a)
    dproj_a = grads_ready("a1_send", g, dproj_a)
    g["d_skip"] = dd_skip[0]
    g["w_in_a"] = _mm(hn0, dproj_a, ta=True, name="dw_in_a", out_dtype=BF16, shards=N_CHIPS)
    dproj_a = grads_ready("a2", g, dproj_a)
    dhn0 = _mm(dproj_a, w["w_in_a"], tb=True, name="dhn_0")
    grad_x, dpre0 = _rmsnorm_bwd(x, w["pre_norm_g"][0], dhn0, adds=(dh1,), name="pre_norm_bwd_0")

    dbb = _s5_unfold(db_blk)
    dcc = _s5_unfold(dc_blk)
    g["c_re"], g["c_im"] = dcc[0], -dcc[1]
    d_ar = da_rows[:, 0, :STATE_COLS].reshape(SSM_GROUPS, SSM_STATE)
    d_ai = da_rows[:, 0, STATE_COLS:].reshape(SSM_GROUPS, SSM_STATE)
    dlr, dli, dls, dbr_t, dbi_t = _s5_prep_bwd(w["lam_re"], w["lam_im"], w["log_step"], b_re_t, b_im_t,
                                               d_ar, d_ai, dbb[0], dbb[1])
    g["lam_re"], g["lam_im"], g["log_step"] = dlr, dli, dls[:, 0]
    g["b_re"] = jnp.transpose(dbr_t, (0, 2, 1))
    g["b_im"] = jnp.transpose(dbi_t, (0, 2, 1))
    g["pre_norm_g"] = jnp.stack([dpre0, dpre1])
    g["post_norm_g"] = jnp.stack([dpost0, dpost1])
    g["mem_norm_g"] = jnp.stack([dmemg0, dmemg1])
    return loss_row, grad_x, g


_MESH = pl.DeviceIdType.MESH
_ANY = pl.BlockSpec(memory_space=pl.ANY)


def _place():
    x, y, c = lax.axis_index("x"), lax.axis_index("y"), lax.axis_index("c")
    chips = [(1 - x, y), (x, 1 - y), (1 - x, 1 - y)]
    return x, y, c, chips


_HBM = pl.BlockSpec(memory_space=pltpu.HBM)
_SEM = pl.BlockSpec(memory_space=pltpu.SEMAPHORE)
_SIDE = pltpu.SideEffectType.DATAFLOW_SIDE_EFFECTING


def _in_hbm(a):
    return pltpu.with_memory_space_constraint(a, pltpu.HBM)


def _hbm_like(a):
    return pltpu.HBM(a.shape, a.dtype)


def _ici_copies(srcs, lands, send_sem, recv_sem, src_at, dst_at, wait_at, to_sibling=False):
    x, y, c, chips = _place()
    peers = [(x, y, 1 - c)] if to_sibling else [(cx, cy, c) for cx, cy in chips]
    m = len(peers)
    start, wait = [], []
    for i in range(len(srcs)):
        for k, (px, py, pc) in enumerate(peers):
            sem = dict(send_sem=send_sem.at[m * i + k], recv_sem=recv_sem.at[m * i + k],
                       device_id=(px, py, pc), device_id_type=_MESH)
            src = src_at(srcs[i], 2 * px + py, c)
            start.append(pltpu.make_async_remote_copy(src_ref=src, dst_ref=dst_at(lands[i], 2 * x + y, k, c), **sem))
            wait.append(pltpu.make_async_remote_copy(src_ref=src, dst_ref=wait_at(lands[i], 2 * px + py, k, c), **sem))
    return start, wait


def _route_peers(route):
    return 1 if len(route) == 4 else 3


_BLOCK_ROUTE = (lambda s, j, c: s, lambda l, me, k, c: l.at[me, c], lambda l, j, k, c: l.at[j, c])


def _ici_start(srcs, lands, token, route, *, name):
    n = len(srcs)

    def body(*refs):
        start, _ = _ici_copies(refs[:n], refs[n:2 * n], refs[2 * n + 1], refs[2 * n + 2], *route)
        for cp in start:
            cp.start()

    sems = pltpu.SemaphoreType.DMA((_route_peers(route) * n,))
    outs = pl.pallas_call(
        body, name=name,
        out_shape=(sems, sems, *[_hbm_like(a) for a in srcs], *[_hbm_like(a) for a in lands], _hbm_like(token)),
        in_specs=[_HBM] * (2 * n + 1), out_specs=(_SEM, _SEM, *[_HBM] * (2 * n + 1)),
        input_output_aliases={i: 2 + i for i in range(2 * n + 1)},
        compiler_params=pltpu.CompilerParams(has_side_effects=_SIDE),
    )(*[_in_hbm(a) for a in srcs], *[_in_hbm(a) for a in lands], _in_hbm(token))
    return (outs[0], outs[1], list(outs[2:2 + n]), list(outs[2 + n:2 + 2 * n])), outs[2 + 2 * n]


def _ici_wait(handle, after, route, *, name):
    send_sem, recv_sem, srcs, lands = handle
    n = len(srcs)
    after = list(after) if isinstance(after, (list, tuple)) else [after]

    def body(*refs):
        _, wait = _ici_copies(refs[:n], refs[n:2 * n], refs[2 * n], refs[2 * n + 1], *route)
        for cp in wait:
            cp.wait_send()
            cp.wait_recv()

    outs = pl.pallas_call(
        body, name=name,
        out_shape=(*[_hbm_like(a) for a in srcs], *[_hbm_like(a) for a in lands]),
        in_specs=[_HBM] * (2 * n) + [_SEM, _SEM] + [_ANY] * len(after), out_specs=tuple([_HBM] * (2 * n)),
        input_output_aliases={i: i for i in range(2 * n)},
        compiler_params=pltpu.CompilerParams(has_side_effects=_SIDE),
    )(*srcs, *lands, send_sem, recv_sem, *after)
    return list(outs[:n]), list(outs[n:])


_GATHER_ROUTE = (lambda s, j, c: s.at[c], lambda l, me, k, c: l.at[me, c], lambda l, j, k, c: l.at[j, c])
_SCATTER_ROUTE = (lambda s, j, c: s.at[j], lambda l, me, k, c: l.at[k], lambda l, j, k, c: l.at[k])
_SHARE_ROUTE = (lambda s, j, c: s, lambda l, me, k, c: l.at[c], lambda l, j, k, c: l.at[1 - c], True)
_SWAP_ROUTE = (lambda s, j, c: s.at[:, 1 - c], lambda l, me, k, c: l, lambda l, j, k, c: l, True)


def _gather_forward(lands, tag, own=False):
    n = len(lands)
    m = 4 if own else 3

    def body(*refs):
        ins, outs = refs[:n], refs[n:2 * n]
        send_sem, recv_sem = refs[2 * n:]
        x, y, c, chips = _place()
        slots = [2 * cx + cy for cx, cy in chips] + [2 * x + y]

        def copy(i, k, half):
            return pltpu.make_async_remote_copy(
                src_ref=ins[i].at[slots[k], half], dst_ref=outs[i].at[slots[k], half],
                send_sem=send_sem.at[m * i + k], recv_sem=recv_sem.at[m * i + k],
                device_id=(x, y, 1 - c), device_id_type=_MESH)

        copies = [copy(i, k, c) for i in range(n) for k in range(m)]
        for cp in copies:
            cp.start()
        for i in range(n):
            for k in range(m):
                copy(i, k, 1 - c).wait_recv()
        for cp in copies:
            cp.wait_send()

    return pl.pallas_call(
        body, name="gather_forward_to_sibling_" + tag,
        out_shape=[jax.ShapeDtypeStruct(a.shape, a.dtype) for a in lands],
        in_specs=[_ANY] * n, out_specs=[_ANY] * n,
        input_output_aliases={i: i for i in range(n)},
        scratch_shapes=[pltpu.SemaphoreType.DMA((m * n,)), pltpu.SemaphoreType.DMA((m * n,))],
    )(*lands)


def _swap_halves(grads, tag):
    n = len(grads)

    def body(*refs):
        ins, outs = refs[:n], refs[n:2 * n]
        send_sem, recv_sem = refs[2 * n:]
        x, y, c, _ = _place()
        copies = [pltpu.make_async_remote_copy(
            src_ref=ins[i].at[:, 1 - c], dst_ref=outs[i],
            send_sem=send_sem.at[i], recv_sem=recv_sem.at[i],
            device_id=(x, y, 1 - c), device_id_type=_MESH) for i in range(n)]
        for cp in copies:
            cp.start()
        for cp in copies:
            cp.wait()

    return pl.pallas_call(
        body, name="grad_swap_halves_" + tag,
        out_shape=[jax.ShapeDtypeStruct((N_CHIPS,) + g.shape[2:], g.dtype) for g in grads],
        in_specs=[_ANY] * n, out_specs=[_ANY] * n,
        scratch_shapes=[pltpu.SemaphoreType.DMA((n,)), pltpu.SemaphoreType.DMA((n,))],
    )(*grads)


def _sum_rows(h, C):
    return max(d for d in range(SUBLANES, h + 1, SUBLANES) if h % d == 0 and d * C <= 1 << 20)


SUM_STEPS = 4


def _pair_sums(gs, rs, c_idx, *, name):
    n = len(gs)
    rows = [g.shape[2] // SUM_STEPS for g in gs]

    def body(c_ref, *refs):
        for g_ref, r_ref, o_ref in zip(refs[:n], refs[n:2 * n], refs[2 * n:]):
            o_ref[...] = (g_ref[...].astype(F32) + r_ref[...].astype(F32)).astype(o_ref.dtype)

    return pl.pallas_call(
        body, name=name,
        out_shape=[jax.ShapeDtypeStruct((N_CHIPS,) + g.shape[2:], g.dtype) for g in gs],
        grid_spec=pltpu.PrefetchScalarGridSpec(
            num_scalar_prefetch=1, grid=(N_CHIPS, SUM_STEPS),
            in_specs=[pl.BlockSpec((None, None, tr, g.shape[3]), lambda j, i, s: (j, s[0], i, 0))
                      for g, tr in zip(gs, rows)]
            + [pl.BlockSpec((None, tr, g.shape[3]), lambda j, i, s: (j, i, 0)) for g, tr in zip(gs, rows)],
            out_specs=[pl.BlockSpec((None, tr, g.shape[3]), lambda j, i, s: (j, i, 0)) for g, tr in zip(gs, rows)]),
        compiler_params=_params("parallel", "parallel"),
    )(c_idx, *gs, *rs)


def _owner_sums(ss, rs, jc_idx, *, name):
    n = len(ss)
    rows = [s.shape[1] // SUM_STEPS for s in ss]

    def body(jc_ref, *refs):
        for s_ref, r_ref, m_ref, o_ref in zip(refs[:n], refs[n:2 * n], refs[2 * n:3 * n], refs[3 * n:]):
            acc = s_ref[...].astype(F32)
            for k in range(3):
                acc = acc + r_ref[k].astype(F32)
            m_ref[...] = acc
            o_ref[...] = acc

    outs = pl.pallas_call(
        body, name=name,
        out_shape=[jax.ShapeDtypeStruct(s.shape[1:], F32) for s in ss]
        + [jax.ShapeDtypeStruct((2,) + s.shape[1:], F32) for s in ss],
        grid_spec=pltpu.PrefetchScalarGridSpec(
            num_scalar_prefetch=1, grid=(SUM_STEPS,),
            in_specs=[pl.BlockSpec((None, tr, s.shape[2]), lambda i, p: (p[0], i, 0)) for s, tr in zip(ss, rows)]
            + [pl.BlockSpec((3, tr, s.shape[2]), lambda i, p: (0, i, 0)) for s, tr in zip(ss, rows)],
            out_specs=[pl.BlockSpec((tr, s.shape[2]), lambda i, p: (i, 0)) for s, tr in zip(ss, rows)]
            + [pl.BlockSpec((None, tr, s.shape[2]), lambda i, p: (p[1], i, 0)) for s, tr in zip(ss, rows)]),
        compiler_params=_params("parallel"),
    )(jc_idx, *ss, *rs)
    return outs[:n], outs[n:]


def _chip_sums(grads, c_idx, tag):
    views = [g.reshape(N_CHIPS, 2, g.shape[1] // 2, g.shape[2]) for g in grads]
    arrived = _swap_halves(views, tag)
    return _pair_sums(views, arrived, c_idx, name=f"grad_pair_sums_{tag}")


def _sum_devices(blocks):
    R = blocks.shape[2]
    tr = _sum_rows(R, 2 * N_CHIPS * LANES)

    def body(b_ref, o_ref):
        acc = b_ref[0, 0]
        for d in range(1, 2 * N_CHIPS):
            acc = acc + b_ref[d // 2, d % 2]
        o_ref[...] = acc

    return pl.pallas_call(
        body, name="sum_small_over_devices", out_shape=jax.ShapeDtypeStruct((R, LANES), F32),
        grid=(R // tr,),
        in_specs=[pl.BlockSpec((N_CHIPS, 2, tr, LANES), lambda i: (0, 0, i, 0))],
        out_specs=pl.BlockSpec((tr, LANES), lambda i: (i, 0)),
        compiler_params=_params("parallel"),
    )(blocks)


def _adamw(w, g, m, v, *, name):
    R, C = w.shape
    tr = max(d for d in range(SUBLANES, R + 1, SUBLANES)
             if R % d == 0 and 7 * 2 * d * C * 4 <= VMEM_LIMIT_BYTES // 2)

    def body(w_ref, g_ref, m_ref, v_ref, d_ref, nm_ref, nv_ref):
        g = g_ref[...]
        m = ADAM_B1 * m_ref[...] + (1.0 - ADAM_B1) * g
        v = ADAM_B2 * v_ref[...] + (1.0 - ADAM_B2) * (g * g)
        nm_ref[...] = m
        nv_ref[...] = v
        m_hat = m / (1.0 - ADAM_B1 ** ADAM_STEP)
        v_hat = v / (1.0 - ADAM_B2 ** ADAM_STEP)
        d_ref[...] = -ADAM_LR * (m_hat / (jnp.sqrt(v_hat) + ADAM_EPS) + ADAM_WD * w_ref[...])

    blk = pl.BlockSpec((tr, C), lambda i: (i, 0))
    sds = jax.ShapeDtypeStruct((R, C), F32)
    return pl.pallas_call(
        body, name=name, out_shape=(sds, sds, sds), grid=(R // tr,),
        in_specs=[blk] * 4, out_specs=(blk, blk, blk),
        compiler_params=_params("parallel"),
    )(w, g, m, v)


_TILE = SUBLANES * LANES


def _pack(arrays):
    rows = []
    for a in arrays:
        flat = a.reshape(-1)
        flat = jnp.pad(flat, (0, (-flat.shape[0]) % _TILE))
        rows.append(flat.reshape(-1, LANES))
    return jnp.concatenate(rows, axis=0)


def _unpack(buf, shapes):
    out, r = [], 0
    for s in shapes:
        size = math.prod(s)
        nr = -(-size // _TILE) * SUBLANES
        out.append(buf[r:r + nr].reshape(-1)[:size].reshape(s))
        r += nr
    return out


_BIG = ("w_in_a", "w_glu", "w_kv", "w_in_b", "w_mem_kv", "w_out")
_REPLICATED = ("pre_norm_g", "post_norm_g", "lam_re", "lam_im", "log_step", "b_re", "b_im", "c_re", "c_im",
               "kv_norm_g", "b_fgate", "mem_norm_g")
_SHARDED_SMALL = ("d_skip", "b_glu", "w_fgate")
_WEIGHTS = ("pre_norm_g", "post_norm_g", "w_in_a", "lam_re", "lam_im", "log_step", "b_re", "b_im", "c_re",
            "c_im", "d_skip", "w_glu", "b_glu", "kv_norm_g", "w_kv", "w_fgate", "b_fgate", "w_in_b",
            "mem_norm_g", "w_mem_kv", "w_out")


def _halves(a):
    return a.reshape(2, a.shape[0] // 2, a.shape[1])


def _unhalve(a):
    return a.reshape(N_CHIPS, 2 * a.shape[2], a.shape[3])


def _columns(a):
    return jnp.transpose(a, (1, 0, 2)).reshape(a.shape[1], N_CHIPS * a.shape[2])


def kernel(x, mem, pre_norm_g, post_norm_g, w_in_a, lam_re, lam_im, log_step, b_re, b_im, c_re, c_im, d_skip, w_glu, b_glu, kv_norm_g, w_kv, w_fgate, b_fgate, w_in_b, mem_norm_g, w_mem_kv, w_out, loss_target, m_pre_norm_g, m_post_norm_g, m_w_in_a, m_lam_re, m_lam_im, m_log_step, m_b_re, m_b_im, m_c_re, m_c_im, m_d_skip, m_w_glu, m_b_glu, m_kv_norm_g, m_w_kv, m_w_fgate, m_b_fgate, m_w_in_b, m_mem_norm_g, m_w_mem_kv, m_w_out, v_pre_norm_g, v_post_norm_g, v_w_in_a, v_lam_re, v_lam_im, v_log_step, v_b_re, v_b_im, v_c_re, v_c_im, v_d_skip, v_w_glu, v_b_glu, v_kv_norm_g, v_w_kv, v_w_fgate, v_b_fgate, v_w_in_b, v_mem_norm_g, v_w_mem_kv, v_w_out):
    a = dict(locals())
    xi, yi, ci = lax.axis_index("x"), lax.axis_index("y"), lax.axis_index("c")
    chip = 2 * xi + yi
    c_idx = jnp.reshape(ci, (1,)).astype(jnp.int32)
    jc_idx = jnp.stack([chip, ci]).astype(jnp.int32)

    vec = jnp.zeros((2 * SUBLANES, MAIN_WIDTH // N_CHIPS), F32)
    vec = vec.at[0].set(a["d_skip"][0]).at[1].set(a["b_glu"][0])
    def own_slot(gathered, parts):
        return [lax.dynamic_update_index_in_dim(g, p, chip, 0) for g, p in zip(gathered, parts)]

    parts_a = [_halves(a["w_in_a"][0].astype(BF16)), _halves(vec)]
    parts_b = [_halves(a["w_glu"][0].astype(BF16)),
               *[_halves(a["w_mem_kv"][i].astype(BF16)) for i in range(2)],
               *[_halves(a["w_out"][i].astype(BF16)) for i in range(2)]]
    parts_c = [_halves(a["w_kv"].astype(BF16)), _halves(_pad_lanes(a["w_fgate"]).astype(BF16)),
               _halves(a["w_in_b"][0].astype(BF16))]
    travelling, token = {}, a["pre_norm_g"]
    for tag, parts in (("a", parts_a), ("b", parts_b), ("c", parts_c)):
        lands = [lax.empty((N_CHIPS,) + p.shape, p.dtype) for p in parts]
        travelling[tag], token = _ici_start(parts, lands, token, _GATHER_ROUTE, name=f"gather_{tag}_start")

    def fetch(tag, after):
        parts, lands = _ici_wait(travelling[tag], after, _GATHER_ROUTE, name=f"gather_{tag}_wait")
        full = own_slot(_gather_forward(lands, tag), parts)
        if tag == "a":
            w_in_a, vecs = full
            return dict(w_in_a=_columns(_unhalve(w_in_a)), d_skip=vecs[:, 0, 0, :].reshape(MAIN_WIDTH),
                        b_glu=vecs[:, 0, 1, :].reshape(MAIN_WIDTH))
        if tag == "b":
            w_glu, w_mk0, w_mk1, w_out0, w_out1 = full
            return dict(w_glu=w_glu.reshape(MAIN_WIDTH, MAIN_WIDTH),
                        w_mem_kv=[m.reshape(D_MODEL, 2 * MEM_WIDTH) for m in (w_mk0, w_mk1)],
                        w_out=[o.reshape(D_MODEL, D_MODEL) for o in (w_out0, w_out1)])
        w_kv, w_fg, w_in_b = full
        return dict(w_kv=_columns(_unhalve(w_kv)), w_fgate=w_fg.reshape(D_MODEL, LANES),
                    w_in_b=_columns(_unhalve(w_in_b)))

    w = dict(
        pre_norm_g=token, post_norm_g=a["post_norm_g"], mem_norm_g=a["mem_norm_g"],
        kv_norm_g=a["kv_norm_g"], b_fgate=a["b_fgate"],
        lam_re=a["lam_re"][0], lam_im=a["lam_im"][0], log_step=a["log_step"][0],
        b_re=a["b_re"][0], b_im=a["b_im"][0], c_re=a["c_re"][0], c_im=a["c_im"][0])

    sent = {}

    swapping = {}

    def grads_ready(event, g, token):
        tag = event.split("_")[0]
        if event in ("b", "a1"):
            big = {"b": lambda: [g["w_kv"], g["w_in_b"], g["w_mem_kv_1"].reshape(N_CHIPS, -1, 2 * MEM_WIDTH),
                                 g["w_out_1"].reshape(N_CHIPS, -1, D_MODEL)],
                   "a1": lambda: [g["w_glu"].reshape(N_CHIPS, -1, MAIN_WIDTH),
                                  g["w_mem_kv_0"].reshape(N_CHIPS, -1, 2 * MEM_WIDTH),
                                  g["w_out_0"].reshape(N_CHIPS, -1, D_MODEL)]}[tag]()
            views = [b.reshape(N_CHIPS, 2, b.shape[1] // 2, b.shape[2]) for b in big]
            lands = [lax.empty((N_CHIPS,) + v.shape[2:], v.dtype) for v in views]
            swapping[tag], token = _ici_start(views, lands, token, _SWAP_ROUTE, name=f"grad_swap_{tag}_start")
            return token
        if event == "a2":
            sums = _chip_sums([g["w_in_a"]], c_idx, tag)
        else:
            views, arrived = _ici_wait(swapping[tag], token, _SWAP_ROUTE, name=f"grad_swap_{tag}_wait")
            sums = _pair_sums(views, arrived, c_idx, name=f"grad_pair_sums_{tag}")
        lands = [lax.empty((3,) + s.shape[1:], s.dtype) for s in sums]
        sent[tag], token = _ici_start(sums, lands, token, _SCATTER_ROUTE, name=f"grad_send_{tag}_start")
        return token

    loss_row, grad_x, g = _local_step(a["x"][0], a["mem"][0], a["loss_target"][0], w, fetch, grads_ready)

    small_names = _REPLICATED + _SHARDED_SMALL
    pack = _pack([g[n] for n in small_names])
    blocks = lax.empty((N_CHIPS, 2) + pack.shape, F32)
    small_sent, token = _ici_start([pack], [blocks], loss_row, _BLOCK_ROUTE, name="small_sums_start")

    sharing = {}
    for tag in ("b", "a1", "a2"):
        sums, arrived = _ici_wait(sent[tag], [grad_x, token], _SCATTER_ROUTE, name=f"grad_send_{tag}_wait")
        mine, bufs = _owner_sums(sums, arrived, jc_idx, name=f"grad_owner_sums_{tag}")
        sharing[tag], token = _ici_start(mine, bufs, token, _SHARE_ROUTE, name=f"grad_share_{tag}_start")
    loss = lax.psum(jnp.sum(token), MESH_AXES)

    def shared(tag, after):
        _, bufs = _ici_wait(sharing[tag], after, _SHARE_ROUTE, name=f"grad_share_{tag}_wait")
        return [b.reshape(-1, b.shape[2]) for b in bufs]

    grads, delta, new_m, new_v = {}, {}, {}, {}

    def adam(n):
        shape = a[n].shape
        d2 = (-1, shape[-1])
        d, m, v = _adamw(a[n].reshape(d2), grads[n].reshape(d2), a["m_" + n].reshape(d2),
                         a["v_" + n].reshape(d2), name="adamw_" + n)
        delta[n], new_m[n], new_v[n] = d.reshape(shape), m.reshape(shape), v.reshape(shape)
        return d

    r_kv, r_in_b, r_mk1, r_out1 = shared("b", token)
    grads["w_kv"], grads["w_in_b"] = r_kv, r_in_b[None]
    done = [adam("w_kv"), adam("w_in_b")]
    r_glu, r_mk0, r_out0 = shared("a1", done)
    grads["w_glu"], grads["w_mem_kv"], grads["w_out"] = r_glu[None], jnp.stack([r_mk0, r_mk1]), jnp.stack([r_out0, r_out1])
    done = [adam("w_glu"), adam("w_mem_kv"), adam("w_out")]
    (r_in_a,) = shared("a2", done)
    grads["w_in_a"] = r_in_a[None]
    adam("w_in_a")

    (pack,), (blocks,) = _ici_wait(small_sent, [delta[n] for n in _BIG], _BLOCK_ROUTE, name="small_sums_wait")
    blocks = lax.dynamic_update_slice(blocks, pack[None, None], (chip, ci, 0, 0))
    (blocks,) = _gather_forward([blocks], "small", own=True)
    small = dict(zip(small_names, _unpack(_sum_devices(blocks), [g[n].shape for n in small_names])))
    for n in _REPLICATED:
        grads[n] = small[n].reshape(a[n].shape)
    nd = MAIN_WIDTH // N_CHIPS
    grads["d_skip"] = lax.dynamic_slice(small["d_skip"], (chip * nd,), (nd,))[None]
    grads["b_glu"] = lax.dynamic_slice(small["b_glu"], (chip * nd,), (nd,))[None]
    nf = D_MODEL // N_CHIPS
    grads["w_fgate"] = lax.dynamic_slice(small["w_fgate"], (chip * nf, 0), (nf, FOX_HEADS))

    shapes = [a[n].shape for n in small_names]
    d, m, v = _adamw(_pack([a[n] for n in small_names]), _pack([grads[n] for n in small_names]),
                     _pack([a["m_" + n] for n in small_names]), _pack([a["v_" + n] for n in small_names]),
                     name="adamw_small")
    for n, dd, mm, vv in zip(small_names, _unpack(d, shapes), _unpack(m, shapes), _unpack(v, shapes)):
        delta[n], new_m[n], new_v[n] = dd, mm, vv

    return (loss, grad_x[None], *[grads[n] for n in _WEIGHTS], *[delta[n] for n in _WEIGHTS],
            *[new_m[n] for n in _WEIGHTS], *[new_v[n] for n in _WEIGHTS])
```

```python
import math

import jax
import jax.numpy as jnp
from jax import lax
from jax.experimental import pallas as pl
from jax.experimental.pallas import tpu as pltpu

F32 = jnp.float32
BF16 = jnp.bfloat16

D_MODEL = 2048
N_MEM = 256
MAIN_WIDTH = 1536
MEM_WIDTH = 512
IN_WIDTH = 2 * MAIN_WIDTH + 2 * MEM_WIDTH
HEAD_DIM = 128
FOX_HEADS = MAIN_WIDTH // HEAD_DIM
MEM_HEADS = MEM_WIDTH // HEAD_DIM
SSM_GROUP = 16
SSM_GROUPS = MAIN_WIDTH // SSM_GROUP
SSM_STATE = 64
GROUPS_PER_BLOCK = 8
SSM_BLOCKS = SSM_GROUPS // GROUPS_PER_BLOCK
STATE_COLS = GROUPS_PER_BLOCK * SSM_STATE
EPS = 1e-6
ADAM_LR = 0.001
ADAM_B1 = 0.9
ADAM_B2 = 0.999
ADAM_EPS = 1e-08
ADAM_WD = 0.01
ADAM_STEP = 10
N_CHIPS = 4
LANES = 128
SUBLANES = 8
VMEM_LIMIT_BYTES = 56 * 1024 * 1024
NEG_BIG = -1e30
MESH_AXES = ("x", "y", "c")


def _params(*sem):
    return pltpu.CompilerParams(dimension_semantics=sem if sem else None,
                                vmem_limit_bytes=VMEM_LIMIT_BYTES)


def _sigmoid(x):
    return 1.0 / (1.0 + jnp.exp(-x))


def _gelu(x):
    c = math.sqrt(2.0 / math.pi)
    return 0.5 * x * (1.0 + jnp.tanh(c * (x + 0.044715 * (x * x * x))))


def _gelu_grad(x):
    c = math.sqrt(2.0 / math.pi)
    t = jnp.tanh(c * (x + 0.044715 * (x * x * x)))
    return 0.5 * (1.0 + t) + 0.5 * x * (1.0 - t * t) * (c * (1.0 + 3.0 * 0.044715 * (x * x)))


def _silu_and_grad(z):
    s = _sigmoid(z)
    return z * s, s * (1.0 + z * (1.0 - s))


_TILE_CHOICES = (4096, 3072, 2048, 1536, 1024, 768, 512, 384, 256, LANES)


def _tile(n, cap):
    return next(c for c in _TILE_CHOICES if c <= cap and n % c == 0)


def _mm(a, b, *, name, ta=False, tb=False, out_dtype=F32, shards=1, tm=1024, tn=1024, tk=4096):
    if ta:
        K, M = a.shape
    else:
        M, K = a.shape
    if tb:
        N, kb = b.shape
    else:
        kb, N = b.shape
    assert K == kb, (a.shape, b.shape)
    ns = N // shards
    tm, tn, tk = _tile(M, tm), _tile(ns, tn), _tile(K, tk)
    assert M % tm == 0 and ns % tn == 0 and K % tk == 0 and N % shards == 0
    nk = K // tk
    dn = (((0 if ta else 1,), (1 if tb else 0,)), ((), ()))

    def body(a_ref, b_ref, o_ref, *acc):
        prod = lax.dot_general(a_ref[...].astype(BF16), b_ref[...].astype(BF16), dn, preferred_element_type=F32)
        if nk == 1:
            o_ref[...] = prod.astype(o_ref.dtype)
            return
        acc_ref, = acc
        k = pl.program_id(2)

        @pl.when(k == 0)
        def _():
            acc_ref[...] = jnp.zeros_like(acc_ref)

        acc_ref[...] += prod

        @pl.when(k == nk - 1)
        def _():
            o_ref[...] = acc_ref[...].astype(o_ref.dtype)

    a_spec = (pl.BlockSpec((tk, tm), lambda i, j, k: (k, i)) if ta
              else pl.BlockSpec((tm, tk), lambda i, j, k: (i, k)))
    b_spec = (pl.BlockSpec((tn, tk), lambda i, j, k: (j, k)) if tb
              else pl.BlockSpec((tk, tn), lambda i, j, k: (k, j)))
    if shards == 1:
        out_shape = jax.ShapeDtypeStruct((M, N), out_dtype)
        o_spec = pl.BlockSpec((tm, tn), lambda i, j, k: (i, j))
    else:
        nb = ns // tn
        out_shape = jax.ShapeDtypeStruct((shards, M, ns), out_dtype)
        o_spec = pl.BlockSpec((None, tm, tn), lambda i, j, k: (j // nb, i, j % nb))
    return pl.pallas_call(
        body, name=name, out_shape=out_shape,
        grid=(M // tm, N // tn, nk),
        in_specs=[a_spec, b_spec], out_specs=o_spec,
        scratch_shapes=[] if nk == 1 else [pltpu.VMEM((tm, tn), F32)],
        compiler_params=_params("parallel", "parallel", "arbitrary"),
    )(a, b)


def _rmsnorm_fwd(x, g, *, name, out_dtype=F32, tr=256):
    L, D = x.shape
    tr = min(tr, L)

    def body(x_ref, g_ref, o_ref):
        xf = x_ref[...]
        r = lax.rsqrt(jnp.mean(xf * xf, axis=-1, keepdims=True) + EPS)
        o_ref[...] = (xf * r * g_ref[...]).astype(o_ref.dtype)

    row = pl.BlockSpec((tr, D), lambda i: (i, 0))
    vec = pl.BlockSpec((1, D), lambda i: (0, 0))
    return pl.pallas_call(
        body, name=name, out_shape=jax.ShapeDtypeStruct((L, D), out_dtype),
        grid=(L // tr,), in_specs=[row, vec], out_specs=row,
        compiler_params=_params("parallel"),
    )(x, g.reshape(1, D))


def _post_norm_and_next_norms(o, g_post, res, g_kv, g_pre, *, name, tr=256):
    L, D = o.shape
    tr = min(tr, L)

    def body(o_ref, gp_ref, r_ref, gk_ref, gn_ref, h_ref, kv_ref, hn_ref):
        of = o_ref[...]
        r = lax.rsqrt(jnp.mean(of * of, axis=-1, keepdims=True) + EPS)
        h = r_ref[...] + of * r * gp_ref[...]
        h_ref[...] = h
        hr = h * lax.rsqrt(jnp.mean(h * h, axis=-1, keepdims=True) + EPS)
        kv_ref[...] = (hr * gk_ref[...]).astype(kv_ref.dtype)
        hn_ref[...] = (hr * gn_ref[...]).astype(hn_ref.dtype)

    row = pl.BlockSpec((tr, D), lambda i: (i, 0))
    vec = pl.BlockSpec((1, D), lambda i: (0, 0))
    return pl.pallas_call(
        body, name=name,
        out_shape=(jax.ShapeDtypeStruct((L, D), F32), jax.ShapeDtypeStruct((L, D), BF16),
                   jax.ShapeDtypeStruct((L, D), BF16)),
        grid=(L // tr,), in_specs=[row, vec, row, vec, vec], out_specs=(row, row, row),
        compiler_params=_params("parallel"),
    )(o, g_post.reshape(1, D), res, g_kv.reshape(1, D), g_pre.reshape(1, D))


def _rmsnorm_bwd(x, g, dy, *, name, adds=(), dx_dtype=F32, tr=256):
    L, D = x.shape
    tr = min(tr, L)
    dys = dy if isinstance(dy, tuple) else (dy,)
    n_dy, n_add = len(dys), len(adds)

    def body(*refs):
        x_ref, g_ref = refs[:2]
        dy_refs = refs[2:2 + n_dy]
        add_refs = refs[2 + n_dy:2 + n_dy + n_add]
        dx_ref, dg_ref = refs[2 + n_dy + n_add:]
        xf = x_ref[...]
        dyf = dy_refs[0][...].astype(F32)
        for d_ref in dy_refs[1:]:
            dyf = dyf + d_ref[...].astype(F32)
        r = lax.rsqrt(jnp.mean(xf * xf, axis=-1, keepdims=True) + EPS)
        gy = dyf * g_ref[...]
        c = jnp.mean(xf * gy, axis=-1, keepdims=True) * (r * r * r)
        dx = gy * r - xf * c
        for a_ref in add_refs:
            dx = dx + a_ref[...].astype(F32)
        dx_ref[...] = dx.astype(dx_ref.dtype)

        @pl.when(pl.program_id(0) == 0)
        def _():
            dg_ref[...] = jnp.zeros_like(dg_ref)

        dg_ref[...] += jnp.sum(dyf * xf * r, axis=0, keepdims=True)

    row = pl.BlockSpec((tr, D), lambda i: (i, 0))
    vec = pl.BlockSpec((1, D), lambda i: (0, 0))
    dx, dg = pl.pallas_call(
        body, name=name,
        out_shape=(jax.ShapeDtypeStruct((L, D), dx_dtype), jax.ShapeDtypeStruct((1, D), F32)),
        grid=(L // tr,), in_specs=[row, vec] + [row] * (n_dy + n_add), out_specs=(row, vec),
        compiler_params=_params("arbitrary"),
    )(x, g.reshape(1, D), *dys, *adds)
    return dx, dg.reshape(D)


def _rmsnorm_bwd_pair(x, g1, dy1, g2, dy2, *, name, adds=(), tr=256):
    L, D = x.shape
    tr = min(tr, L)
    dy1s = dy1 if isinstance(dy1, tuple) else (dy1,)
    n1, n_add = len(dy1s), len(adds)

    def body(*refs):
        x_ref, g1_ref, g2_ref = refs[:3]
        dy1_refs = refs[3:3 + n1]
        dy2_ref = refs[3 + n1]
        add_refs = refs[4 + n1:4 + n1 + n_add]
        dx_ref, dg1_ref, dg2_ref = refs[4 + n1 + n_add:]
        xf = x_ref[...]
        d1 = dy1_refs[0][...].astype(F32)
        for d_ref in dy1_refs[1:]:
            d1 = d1 + d_ref[...].astype(F32)
        d2 = dy2_ref[...].astype(F32)
        r = lax.rsqrt(jnp.mean(xf * xf, axis=-1, keepdims=True) + EPS)
        gy = d1 * g1_ref[...] + d2 * g2_ref[...]
        c = jnp.mean(xf * gy, axis=-1, keepdims=True) * (r * r * r)
        dx = gy * r - xf * c
        for a_ref in add_refs:
            dx = dx + a_ref[...].astype(F32)
        dx_ref[...] = dx

        @pl.when(pl.program_id(0) == 0)
        def _():
            dg1_ref[...] = jnp.zeros_like(dg1_ref)
            dg2_ref[...] = jnp.zeros_like(dg2_ref)

        xr = xf * r
        dg1_ref[...] += jnp.sum(d1 * xr, axis=0, keepdims=True)
        dg2_ref[...] += jnp.sum(d2 * xr, axis=0, keepdims=True)

    row = pl.BlockSpec((tr, D), lambda i: (i, 0))
    vec = pl.BlockSpec((1, D), lambda i: (0, 0))
    dx, dg1, dg2 = pl.pallas_call(
        body, name=name,
        out_shape=(jax.ShapeDtypeStruct((L, D), F32), jax.ShapeDtypeStruct((1, D), F32),
                   jax.ShapeDtypeStruct((1, D), F32)),
        grid=(L // tr,), in_specs=[row, vec, vec] + [row] * (n1 + 1 + n_add), out_specs=(row, vec, vec),
        compiler_params=_params("arbitrary"),
    )(x, g1.reshape(1, D), g2.reshape(1, D), *dy1s, dy2, *adds)
    return dx, dg1.reshape(D), dg2.reshape(D)


def _final_norm_loss(o, g, res, target, *, tr=256):
    L, D = o.shape
    tr = min(tr, L)

    def body(o_ref, g_ref, r_ref, t_ref, dh_ref, loss_ref):
        xf = o_ref[...]
        r = lax.rsqrt(jnp.mean(xf * xf, axis=-1, keepdims=True) + EPS)
        e = (r_ref[...] + xf * r * g_ref[...]) - t_ref[...]
        dh_ref[...] = e * (1.0 / D)

        @pl.when(pl.program_id(0) == 0)
        def _():
            loss_ref[...] = jnp.zeros_like(loss_ref)

        loss_ref[...] += jnp.sum(e * e, axis=0, keepdims=True) * (0.5 / D)

    row = pl.BlockSpec((tr, D), lambda i: (i, 0))
    vec = pl.BlockSpec((1, D), lambda i: (0, 0))
    dh, lp = pl.pallas_call(
        body, name="post_norm_1_loss",
        out_shape=(jax.ShapeDtypeStruct((L, D), F32), jax.ShapeDtypeStruct((1, D), F32)),
        grid=(L // tr,), in_specs=[row, vec, row, row], out_specs=(row, vec),
        compiler_params=_params("arbitrary"),
    )(o, g.reshape(1, D), res, target)
    return dh, lp


def _s5_coeffs(lr, li, ls):
    dt = jnp.exp(ls)
    mag = jnp.exp(lr * dt)
    ar = mag * jnp.cos(li * dt)
    ai = mag * jnp.sin(li * dt)
    den = lr * lr + li * li
    cr = ((ar - 1.0) * lr + ai * li) / den
    ci = (ai * lr - (ar - 1.0) * li) / den
    return dt, ar, ai, den, cr, ci


def _s5_prep(lam_re, lam_im, log_step, b_re_t, b_im_t):
    G, P = lam_re.shape
    H = b_re_t.shape[1]

    def body(lr_ref, li_ref, ls_ref, br_ref, bi_ref, ar_ref, ai_ref, bbr_ref, bbi_ref):
        _, ar, ai, _, cr, ci = _s5_coeffs(lr_ref[...], li_ref[...], ls_ref[...])
        ar_ref[...] = ar
        ai_ref[...] = ai
        br, bi = br_ref[...], bi_ref[...]
        crb, cib = cr[:, None, :], ci[:, None, :]
        bbr_ref[...] = crb * br - cib * bi
        bbi_ref[...] = crb * bi + cib * br

    return pl.pallas_call(
        body, name="s5_prep",
        out_shape=(jax.ShapeDtypeStruct((G, P), F32), jax.ShapeDtypeStruct((G, P), F32),
                   jax.ShapeDtypeStruct((G, H, P), F32), jax.ShapeDtypeStruct((G, H, P), F32)),
        compiler_params=_params(),
    )(lam_re, lam_im, log_step.reshape(G, 1), b_re_t, b_im_t)


def _s5_prep_bwd(lam_re, lam_im, log_step, b_re_t, b_im_t, d_ar, d_ai, d_bbr, d_bbi):
    G, P = lam_re.shape
    H = b_re_t.shape[1]

    def body(lr_ref, li_ref, ls_ref, br_ref, bi_ref, dar_ref, dai_ref, dbbr_ref, dbbi_ref,
             dlr_ref, dli_ref, dls_ref, dbr_ref, dbi_ref):
        lr, li = lr_ref[...], li_ref[...]
        dt, ar, ai, den, cr, ci = _s5_coeffs(lr, li, ls_ref[...])
        br, bi = br_ref[...], bi_ref[...]
        gbr, gbi = dbbr_ref[...], dbbi_ref[...]
        crb, cib = cr[:, None, :], ci[:, None, :]
        dbr_ref[...] = crb * gbr + cib * gbi
        dbi_ref[...] = crb * gbi - cib * gbr
        gcr = jnp.sum(br * gbr + bi * gbi, axis=1)
        gci = jnp.sum(br * gbi - bi * gbr, axis=1)
        ilr, ili = lr / den, -li / den
        gar = dar_ref[...] + (ilr * gcr + ili * gci)
        gai = dai_ref[...] + (ilr * gci - ili * gcr)
        qr, qi = cr * ilr - ci * ili, cr * ili + ci * ilr
        glr = -(qr * gcr + qi * gci)
        gli = -(qr * gci - qi * gcr)
        glr = glr + dt * (ar * gar + ai * gai)
        gli = gli + dt * (ar * gai - ai * gar)
        wr, wi = lr * ar - li * ai, lr * ai + li * ar
        gdt = jnp.sum(wr * gar + wi * gai, axis=1, keepdims=True)
        dlr_ref[...] = glr
        dli_ref[...] = gli
        dls_ref[...] = gdt * dt

    return pl.pallas_call(
        body, name="s5_prep_bwd",
        out_shape=(jax.ShapeDtypeStruct((G, P), F32), jax.ShapeDtypeStruct((G, P), F32),
                   jax.ShapeDtypeStruct((G, 1), F32),
                   jax.ShapeDtypeStruct((G, H, P), F32), jax.ShapeDtypeStruct((G, H, P), F32)),
        compiler_params=_params(),
    )(lam_re, lam_im, log_step.reshape(G, 1), b_re_t, b_im_t, d_ar, d_ai, d_bbr, d_bbi)


def _s5_block_mats(bbr_t, bbi_t, c_re, c_im):
    bmat = _s5_expand(bbr_t, bbi_t)
    cmat = jnp.transpose(_s5_expand(c_re, -c_im), (0, 2, 1))
    return bmat.astype(BF16), cmat.astype(BF16)


def _s5_diag_mask():
    r = lax.broadcasted_iota(jnp.int32, (LANES, 2 * STATE_COLS), 0) // SSM_GROUP
    c = (lax.broadcasted_iota(jnp.int32, (LANES, 2 * STATE_COLS), 1) % STATE_COLS) // SSM_STATE
    return (r == c).astype(F32)


def _s5_expand(re, im):
    re = jnp.tile(re.reshape(SSM_BLOCKS, LANES, SSM_STATE), (1, 1, GROUPS_PER_BLOCK))
    im = jnp.tile(im.reshape(SSM_BLOCKS, LANES, SSM_STATE), (1, 1, GROUPS_PER_BLOCK))
    return jnp.concatenate([re, im], axis=-1) * _s5_diag_mask()[None]


def _s5_unfold(dmat):
    d = dmat.reshape(SSM_GROUPS, SSM_GROUP, 2, SSM_STATE)
    return jnp.transpose(d, (2, 0, 1, 3))


def _s5_a_rows(ar, ai):
    a = jnp.concatenate([ar.reshape(SSM_BLOCKS, STATE_COLS), ai.reshape(SSM_BLOCKS, STATE_COLS)], axis=1)
    return jnp.broadcast_to(a[:, None, :], (SSM_BLOCKS, SUBLANES, 2 * STATE_COLS))


def _to_step_major(src_ref, dst_ref, seg):
    for s in range(SUBLANES):
        dst_ref[pl.ds(s, seg, stride=SUBLANES), :] = src_ref[pl.ds(seg * s, seg), :]


def _segment_rows(ref, s, seg):
    return ref[pl.ds(s, seg, stride=SUBLANES), :]


def _cmul(ar, ai, xr, xi):
    return ar * xr - ai * xi, ar * xi + ai * xr


def _s5_tables(a_ref, pw_s, pwr_s, S, seg):
    ar, ai = a_ref[:, :S], a_ref[:, S:]

    def step(i, c):
        pr, pi = c
        pw_s[i, :, :S] = pr
        pw_s[i, :, S:] = pi
        nr, ni = _cmul(ar, ai, pr, pi)
        pwr_s[seg - 1 - i, :, :S] = nr
        pwr_s[seg - 1 - i, :, S:] = ni
        return nr, ni

    pr, pi = lax.fori_loop(0, seg, step, (jnp.ones_like(ar), jnp.zeros_like(ai)))
    pw_s[seg, :, :S] = pr
    pw_s[seg, :, S:] = pi


def _s5_fwd(proj, bmat, cmat, a_rows, d_skip, *, tc=512):
    L = proj.shape[0]
    tc = min(tc, L)
    nt = L // tc
    seg = tc // SUBLANES
    S = STATE_COLS

    def body(u_ref, b_ref, c_ref, a_ref, d_ref, y_ref, yg_ref, xp_ref,
             bu_s, xp_s, pw_s, pwr_s, carry_s, e_s, up_s, yc_s):
        @pl.when(pl.program_id(1) == 0)
        def _():
            carry_s[...] = jnp.zeros_like(carry_s)
            _s5_tables(a_ref, pw_s, pwr_s, S, seg)

        ar, ai = a_ref[:, :S], a_ref[:, S:]
        _to_step_major(u_ref, up_s, seg)
        bu = jnp.dot(up_s[...].astype(BF16), b_ref[...], preferred_element_type=F32)
        bu_s[...] = bu.reshape(seg, SUBLANES, 2 * S)

        def step(i, carry):
            cr, ci = carry
            xp_s[i, :, :S] = cr
            xp_s[i, :, S:] = ci
            return ar * cr - ai * ci + bu_s[i, :, :S], ar * ci + ai * cr + bu_s[i, :, S:]

        zero = jnp.zeros((SUBLANES, S), F32)
        fr, fi = lax.fori_loop(0, seg, step, (zero, zero))
        pr, pi = pw_s[seg, 0:1, :S], pw_s[seg, 0:1, S:]
        er, ei = carry_s[0:1, :S], carry_s[0:1, S:]
        for s in range(SUBLANES):
            e_s[s:s + 1, :S] = er
            e_s[s:s + 1, S:] = ei
            tr, ti = _cmul(pr, pi, er, ei)
            er, ei = fr[s:s + 1] + tr, fi[s:s + 1] + ti
        carry_s[0:1, :S] = er
        carry_s[0:1, S:] = ei
        pw = pw_s[0:seg]
        tr, ti = _cmul(pw[:, :, :S], pw[:, :, S:], e_s[:, :S][None], e_s[:, S:][None])
        xl = xp_s[...]
        xp = jnp.concatenate([xl[:, :, :S] + tr, xl[:, :, S:] + ti], axis=-1).reshape(tc, 2 * S)
        xp_ref[...] = xp
        a1r, a1i = ar[0:1], ai[0:1]
        x_re = a1r * xp[:, :S] - a1i * xp[:, S:] + bu[:, :S]
        x_im = a1r * xp[:, S:] + a1i * xp[:, :S] + bu[:, S:]
        xs = jnp.concatenate([x_re, x_im], axis=1).astype(BF16)
        yc_s[...] = jnp.dot(xs, c_ref[...], preferred_element_type=F32)
        for s in range(SUBLANES):
            rows = pl.ds(seg * s, seg)
            y = _segment_rows(yc_s, s, seg) + d_ref[...] * u_ref[rows, :]
            y_ref[rows, :] = y
            yg_ref[rows, :] = _gelu(y).astype(BF16)

    return pl.pallas_call(
        body, name="s5_fwd",
        out_shape=(jax.ShapeDtypeStruct((L, MAIN_WIDTH), F32),
                   jax.ShapeDtypeStruct((L, MAIN_WIDTH), BF16),
                   jax.ShapeDtypeStruct((L, SSM_BLOCKS * 2 * S), F32)),
        grid=(SSM_BLOCKS, nt),
        in_specs=[pl.BlockSpec((tc, LANES), lambda b, t: (t, b)),
                  pl.BlockSpec((None, LANES, 2 * S), lambda b, t: (b, 0, 0)),
                  pl.BlockSpec((None, 2 * S, LANES), lambda b, t: (b, 0, 0)),
                  pl.BlockSpec((None, SUBLANES, 2 * S), lambda b, t: (b, 0, 0)),
                  pl.BlockSpec((1, LANES), lambda b, t: (0, b))],
        out_specs=(pl.BlockSpec((tc, LANES), lambda b, t: (t, b)),
                   pl.BlockSpec((tc, LANES), lambda b, t: (t, b)),
                   pl.BlockSpec((tc, 2 * S), lambda b, t: (t, b))),
        scratch_shapes=[pltpu.VMEM((seg, SUBLANES, 2 * S), F32),
                        pltpu.VMEM((seg, SUBLANES, 2 * S), F32),
                        pltpu.VMEM((seg + 1, SUBLANES, 2 * S), F32),
                        pltpu.VMEM((seg, SUBLANES, 2 * S), F32),
                        pltpu.VMEM((SUBLANES, 2 * S), F32),
                        pltpu.VMEM((SUBLANES, 2 * S), F32),
                        pltpu.VMEM((tc, LANES), F32),
                        pltpu.VMEM((tc, LANES), F32)],
        compiler_params=_params("parallel", "arbitrary"),
    )(proj, bmat, cmat, a_rows, d_skip.reshape(1, MAIN_WIDTH))


def _s5_bwd(proj, dyg_a, dyg_b, y, xp, bmat, cmat, a_rows, d_skip, dproj, *, tc=512):
    L = proj.shape[0]
    tc = min(tc, L)
    nt = L // tc
    seg = tc // SUBLANES
    S = STATE_COLS
    nn = (((1,), (1,)), ((), ()))
    tn = (((0,), (0,)), ((), ()))

    def fold_diagonal(acc_ref, mask_ref, fold_ref):
        x = acc_ref[...] * mask_ref[...]
        hi = x.astype(BF16)
        rest = x - hi.astype(F32)
        mid = rest.astype(BF16)
        low = (rest - mid.astype(F32)).astype(BF16)
        return sum(jnp.dot(piece, fold_ref[...], preferred_element_type=F32) for piece in (hi, mid, low))

    def body(u_ref, dyga_ref, dygb_ref, y_ref, xp_ref, b_ref, c_ref, a_ref, d_ref, mask_ref, fold_ref, dp_hbm,
             du_ref, dbd_ref, dcd_ref, da_ref, dd_ref,
             dl_s, pw_s, pwr_s, carry_s, e_s, up_s, dy_s, dyp_s, dup_s, db_ref, dc_ref):
        @pl.when(pl.program_id(1) == 0)
        def _():
            carry_s[...] = jnp.zeros_like(carry_s)
            db_ref[...] = jnp.zeros_like(db_ref)
            dc_ref[...] = jnp.zeros_like(dc_ref)
            da_ref[...] = jnp.zeros_like(da_ref)
            dd_ref[...] = jnp.zeros_like(dd_ref)
            _s5_tables(a_ref, pw_s, pwr_s, S, seg)

        ar, ai = a_ref[:, :S], a_ref[:, S:]
        a1r, a1i = ar[0:1], ai[0:1]
        u = u_ref[...]
        dy = (dyga_ref[...] + dygb_ref[...]) * _gelu_grad(y_ref[...])
        dy_s[...] = dy
        xp = xp_ref[...]
        _to_step_major(u_ref, up_s, seg)
        _to_step_major(dy_s, dyp_s, seg)
        ubp = up_s[...].astype(BF16)
        dyp = dyp_s[...].astype(BF16)
        bu = jnp.dot(ubp, b_ref[...], preferred_element_type=F32)
        x_re = a1r * xp[:, :S] - a1i * xp[:, S:] + bu[:, :S]
        x_im = a1r * xp[:, S:] + a1i * xp[:, :S] + bu[:, S:]
        xs = jnp.concatenate([x_re, x_im], axis=1).astype(BF16)
        dc_ref[...] += lax.dot_general(dyp, xs, tn, preferred_element_type=F32)
        dx = lax.dot_general(dyp, c_ref[...], nn, preferred_element_type=F32)
        dl_s[...] = dx.reshape(seg, SUBLANES, 2 * S)

        def step(k, carry):
            cr, ci = carry
            i = seg - 1 - k
            lr = dl_s[i, :, :S] + (ar * cr + ai * ci)
            li = dl_s[i, :, S:] + (ar * ci - ai * cr)
            dl_s[i, :, :S] = lr
            dl_s[i, :, S:] = li
            return lr, li

        zero = jnp.zeros((SUBLANES, S), F32)
        fr, fi = lax.fori_loop(0, seg, step, (zero, zero))
        pr, pi = pw_s[seg, 0:1, :S], pw_s[seg, 0:1, S:]
        er, ei = carry_s[0:1, :S], carry_s[0:1, S:]
        for s in range(SUBLANES - 1, -1, -1):
            e_s[s:s + 1, :S] = er
            e_s[s:s + 1, S:] = ei
            er, ei = fr[s:s + 1] + (pr * er + pi * ei), fi[s:s + 1] + (pr * ei - pi * er)
        carry_s[0:1, :S] = er
        carry_s[0:1, S:] = ei
        er, ei = e_s[:, :S][None], e_s[:, S:][None]
        pw = pwr_s[...]
        pwr, pwi = pw[:, :, :S], pw[:, :, S:]
        ll = dl_s[...]
        lam = jnp.concatenate([ll[:, :, :S] + (pwr * er + pwi * ei), ll[:, :, S:] + (pwr * ei - pwi * er)],
                              axis=-1).reshape(tc, 2 * S)
        l_re, l_im = lam[:, :S], lam[:, S:]
        da_ref[0:1, :S] += jnp.sum(l_re * xp[:, :S] + l_im * xp[:, S:], axis=0, keepdims=True)
        da_ref[0:1, S:] += jnp.sum(l_im * xp[:, :S] - l_re * xp[:, S:], axis=0, keepdims=True)
        lamb = lam.astype(BF16)
        dup_s[...] = lax.dot_general(lamb, b_ref[...], nn, preferred_element_type=F32)
        for s in range(SUBLANES):
            rows = pl.ds(seg * s, seg)
            du = _segment_rows(dup_s, s, seg) + d_ref[...] * dy_s[rows, :]
            du_ref[rows, :] = du.astype(du_ref.dtype)
        db_ref[...] += lax.dot_general(ubp, lamb, tn, preferred_element_type=F32)
        dd_ref[0:1, :] += jnp.sum(dy * u, axis=0, keepdims=True)

        @pl.when(pl.program_id(1) == nt - 1)
        def _():
            dbd_ref[...] = fold_diagonal(db_ref, mask_ref, fold_ref)
            dcd_ref[...] = fold_diagonal(dc_ref, mask_ref, fold_ref)

    rev = lambda b, t: (nt - 1 - t, b)
    col = jnp.arange(2 * S)
    fold = ((col // S * SSM_STATE + col % SSM_STATE)[:, None] == jnp.arange(LANES)[None, :]).astype(BF16)
    return pl.pallas_call(
        body, name="s5_bwd",
        out_shape=(jax.ShapeDtypeStruct(dproj.shape, dproj.dtype),
                   jax.ShapeDtypeStruct((SSM_BLOCKS, LANES, LANES), F32),
                   jax.ShapeDtypeStruct((SSM_BLOCKS, LANES, LANES), F32),
                   jax.ShapeDtypeStruct((SSM_BLOCKS, SUBLANES, 2 * S), F32),
                   jax.ShapeDtypeStruct((SUBLANES, MAIN_WIDTH), F32)),
        input_output_aliases={11: 0},
        grid=(SSM_BLOCKS, nt),
        in_specs=[pl.BlockSpec((tc, LANES), rev),
                  pl.BlockSpec((tc, LANES), rev),
                  pl.BlockSpec((tc, LANES), rev),
                  pl.BlockSpec((tc, LANES), rev),
                  pl.BlockSpec((tc, 2 * S), rev),
                  pl.BlockSpec((None, LANES, 2 * S), lambda b, t: (b, 0, 0)),
                  pl.BlockSpec((None, 2 * S, LANES), lambda b, t: (b, 0, 0)),
                  pl.BlockSpec((None, SUBLANES, 2 * S), lambda b, t: (b, 0, 0)),
                  pl.BlockSpec((1, LANES), lambda b, t: (0, b)),
                  pl.BlockSpec((LANES, 2 * S), lambda b, t: (0, 0)),
                  pl.BlockSpec((2 * S, LANES), lambda b, t: (0, 0)),
                  _ANY],
        out_specs=(pl.BlockSpec((tc, LANES), rev),
                   pl.BlockSpec((None, LANES, LANES), lambda b, t: (b, 0, 0)),
                   pl.BlockSpec((None, LANES, LANES), lambda b, t: (b, 0, 0)),
                   pl.BlockSpec((None, SUBLANES, 2 * S), lambda b, t: (b, 0, 0)),
                   pl.BlockSpec((SUBLANES, LANES), lambda b, t: (0, b))),
        scratch_shapes=[pltpu.VMEM((seg, SUBLANES, 2 * S), F32),
                        pltpu.VMEM((seg + 1, SUBLANES, 2 * S), F32),
                        pltpu.VMEM((seg, SUBLANES, 2 * S), F32),
                        pltpu.VMEM((SUBLANES, 2 * S), F32),
                        pltpu.VMEM((SUBLANES, 2 * S), F32),
                        pltpu.VMEM((tc, LANES), F32),
                        pltpu.VMEM((tc, LANES), F32),
                        pltpu.VMEM((tc, LANES), F32),
                        pltpu.VMEM((tc, LANES), F32),
                        pltpu.VMEM((LANES, 2 * S), F32),
                        pltpu.VMEM((LANES, 2 * S), F32)],
        compiler_params=_params("parallel", "arbitrary"),
    )(proj, dyg_a, dyg_b, y, xp, bmat, cmat, a_rows, d_skip.reshape(1, MAIN_WIDTH), _s5_diag_mask(), fold, dproj)


_Z_COLS = slice(MAIN_WIDTH, 2 * MAIN_WIDTH)
_ZM_COLS = slice(2 * MAIN_WIDTH + MEM_WIDTH, IN_WIDTH)


def _proj_rows(tr):
    return pl.BlockSpec((tr, IN_WIDTH), lambda i: (i, 0))


def _row_specs(tr):
    main = pl.BlockSpec((tr, MAIN_WIDTH), lambda i: (i, 0))
    z = pl.BlockSpec((tr, MAIN_WIDTH), lambda i: (i, 1))
    zm = pl.BlockSpec((tr, MEM_WIDTH), lambda i: (i, IN_WIDTH // MEM_WIDTH - 1))
    mem = pl.BlockSpec((tr, MEM_WIDTH), lambda i: (i, 0))
    cat = pl.BlockSpec((tr, D_MODEL), lambda i: (i, 0))
    vec = pl.BlockSpec((1, MAIN_WIDTH), lambda i: (0, 0))
    return main, z, zm, mem, cat, vec


def _gate_a_fwd(y, t, b_glu, proj, o_mem, *, tr=256):
    L = y.shape[0]
    tr = min(tr, L)

    def body(y_ref, t_ref, b_ref, z_ref, zm_ref, om_ref, o_ref):
        yg = _gelu(y_ref[...])
        sz, _ = _silu_and_grad(z_ref[...])
        o_ref[:, :MAIN_WIDTH] = (yg * _sigmoid(t_ref[...] + b_ref[...]) * sz).astype(BF16)
        szm, _ = _silu_and_grad(zm_ref[...])
        o_ref[:, MAIN_WIDTH:] = (om_ref[...] * szm).astype(BF16)

    main, z, zm, mem, cat, vec = _row_specs(tr)
    return pl.pallas_call(
        body, name="gate_a_fwd", out_shape=jax.ShapeDtypeStruct((L, D_MODEL), BF16),
        grid=(L // tr,), in_specs=[main, main, vec, z, zm, mem], out_specs=cat,
        compiler_params=_params("parallel"),
    )(y, t, b_glu.reshape(1, MAIN_WIDTH), proj, proj, o_mem)


def _gate_a_bwd(dcat, y, t, b_glu, proj, o_mem, *, tr=256):
    L = y.shape[0]
    tr = min(tr, L)

    def body(dc_ref, y_ref, t_ref, b_ref, z_ref, zm_ref, om_ref,
             dp_ref, dt_ref, dyg_ref, dom_ref, db_ref):
        dmain = dc_ref[:, :MAIN_WIDTH]
        dmemo = dc_ref[:, MAIN_WIDTH:]
        yg = _gelu(y_ref[...])
        sg = _sigmoid(t_ref[...] + b_ref[...])
        sz, gz = _silu_and_grad(z_ref[...])
        dp_ref[:, _Z_COLS] = (dmain * (yg * sg) * gz).astype(BF16)
        dy2 = dmain * sz
        dyg_ref[...] = dy2 * sg
        dt = dy2 * yg * (sg * (1.0 - sg))
        dt_ref[...] = dt.astype(BF16)

        @pl.when(pl.program_id(0) == 0)
        def _():
            db_ref[...] = jnp.zeros_like(db_ref)

        db_ref[...] += jnp.sum(dt, axis=0, keepdims=True)
        szm, gzm = _silu_and_grad(zm_ref[...])
        dom_ref[...] = dmemo * szm
        dp_ref[:, _ZM_COLS] = (dmemo * om_ref[...] * gzm).astype(BF16)

    main, z, zm, mem, cat, vec = _row_specs(tr)
    outs = pl.pallas_call(
        body, name="gate_a_bwd",
        out_shape=(jax.ShapeDtypeStruct((L, IN_WIDTH), BF16),
                   jax.ShapeDtypeStruct((L, MAIN_WIDTH), BF16), jax.ShapeDtypeStruct((L, MAIN_WIDTH), F32),
                   jax.ShapeDtypeStruct((L, MEM_WIDTH), F32), jax.ShapeDtypeStruct((1, MAIN_WIDTH), F32)),
        grid=(L // tr,), in_specs=[cat, main, main, vec, z, zm, mem],
        out_specs=(_proj_rows(tr), main, main, mem, vec),
        compiler_params=_params("arbitrary"),
    )(dcat, y, t, b_glu.reshape(1, MAIN_WIDTH), proj, proj, o_mem)
    return outs


def _gate_b_fwd(att, proj, o_mem, *, tr=256):
    L = att.shape[0]
    tr = min(tr, L)

    def body(a_ref, z_ref, zm_ref, om_ref, o_ref):
        sz, _ = _silu_and_grad(z_ref[...])
        o_ref[:, :MAIN_WIDTH] = (a_ref[...] * sz).astype(BF16)
        szm, _ = _silu_and_grad(zm_ref[...])
        o_ref[:, MAIN_WIDTH:] = (om_ref[...] * szm).astype(BF16)

    main, z, zm, mem, cat, _ = _row_specs(tr)
    return pl.pallas_call(
        body, name="gate_b_fwd", out_shape=jax.ShapeDtypeStruct((L, D_MODEL), BF16),
        grid=(L // tr,), in_specs=[main, z, zm, mem], out_specs=cat,
        compiler_params=_params("parallel"),
    )(att, proj, proj, o_mem)


def _gate_b_bwd(dcat, att, proj, o_mem, *, tr=256):
    L = att.shape[0]
    tr = min(tr, L)

    def body(dc_ref, a_ref, z_ref, zm_ref, om_ref, da_ref, dp_ref, dom_ref, dl_ref):
        dmain = dc_ref[:, :MAIN_WIDTH]
        dmemo = dc_ref[:, MAIN_WIDTH:]
        att = a_ref[...]
        sz, gz = _silu_and_grad(z_ref[...])
        datt = dmain * sz
        da_ref[...] = datt
        dp_ref[:, _Z_COLS] = (dmain * att * gz).astype(BF16)
        szm, gzm = _silu_and_grad(zm_ref[...])
        dom_ref[...] = dmemo * szm
        dp_ref[:, _ZM_COLS] = (dmemo * om_ref[...] * gzm).astype(BF16)
        prod = datt * att
        for h in range(FOX_HEADS):
            dl_ref[h] = jnp.sum(prod[:, h * HEAD_DIM:(h + 1) * HEAD_DIM], axis=1, keepdims=True)

    main, z, zm, mem, cat, _ = _row_specs(tr)
    delta = pl.BlockSpec((FOX_HEADS, tr, 1), lambda i: (0, i, 0))
    return pl.pallas_call(
        body, name="gate_b_bwd",
        out_shape=(jax.ShapeDtypeStruct((L, MAIN_WIDTH), F32), jax.ShapeDtypeStruct((L, IN_WIDTH), BF16),
                   jax.ShapeDtypeStruct((L, MEM_WIDTH), F32), jax.ShapeDtypeStruct((FOX_HEADS, L, 1), F32)),
        grid=(L // tr,), in_specs=[cat, main, z, zm, mem], out_specs=(main, _proj_rows(tr), mem, delta),
        compiler_params=_params("parallel"),
    )(dcat, att, proj, proj, o_mem)


_MEM_Q_COL = (2 * MAIN_WIDTH) // HEAD_DIM
_NT = (((1,), (1,)), ((), ()))
_TN = (((0,), (0,)), ((), ()))


def _mem_probs(q_ref, k_ref):
    qs = (q_ref[...] * (HEAD_DIM ** -0.5)).astype(BF16)
    s = lax.dot_general(qs, k_ref[...].astype(BF16), _NT, preferred_element_type=F32)
    e = jnp.exp(s - jnp.max(s, axis=-1, keepdims=True))
    return qs, e / jnp.sum(e, axis=-1, keepdims=True)


def _mem_attn_fwd(proj, kvm, *, tq=2048):
    L = proj.shape[0]
    tq = min(tq, L)

    def body(q_ref, k_ref, v_ref, o_ref):
        _, p = _mem_probs(q_ref, k_ref)
        o_ref[...] = jnp.dot(p.astype(BF16), v_ref[...].astype(BF16), preferred_element_type=F32)

    return pl.pallas_call(
        body, name="mem_attn_fwd", out_shape=jax.ShapeDtypeStruct((L, MEM_WIDTH), F32),
        grid=(MEM_HEADS, L // tq),
        in_specs=[pl.BlockSpec((tq, HEAD_DIM), lambda h, i: (i, _MEM_Q_COL + h)),
                  pl.BlockSpec((N_MEM, HEAD_DIM), lambda h, i: (0, h)),
                  pl.BlockSpec((N_MEM, HEAD_DIM), lambda h, i: (0, MEM_HEADS + h))],
        out_specs=pl.BlockSpec((tq, HEAD_DIM), lambda h, i: (i, h)),
        compiler_params=_params("parallel", "parallel"),
    )(proj, kvm, kvm)


def _mem_attn_bwd(proj, kvm, do, dproj, *, tq=2048):
    L = proj.shape[0]
    tq = min(tq, L)

    def body(q_ref, k_ref, v_ref, do_ref, dp_hbm, dq_ref, dk_ref, dv_ref):
        @pl.when(pl.program_id(1) == 0)
        def _():
            dk_ref[...] = jnp.zeros_like(dk_ref)
            dv_ref[...] = jnp.zeros_like(dv_ref)

        qs, p = _mem_probs(q_ref, k_ref)
        dob = do_ref[...].astype(BF16)
        dp = lax.dot_general(dob, v_ref[...].astype(BF16), _NT, preferred_element_type=F32)
        ds = p * (dp - jnp.sum(p * dp, axis=-1, keepdims=True))
        dsb = ds.astype(BF16)
        dq = jnp.dot(dsb, k_ref[...].astype(BF16), preferred_element_type=F32) * (HEAD_DIM ** -0.5)
        dq_ref[...] = dq.astype(BF16)
        dk_ref[...] += lax.dot_general(dsb, qs, _TN, preferred_element_type=F32)
        dv_ref[...] += lax.dot_general(p.astype(BF16), dob, _TN, preferred_element_type=F32)

    dproj, dk, dv = pl.pallas_call(
        body, name="mem_attn_bwd",
        out_shape=(jax.ShapeDtypeStruct(dproj.shape, dproj.dtype),
                   jax.ShapeDtypeStruct((N_MEM, MEM_WIDTH), F32),
                   jax.ShapeDtypeStruct((N_MEM, MEM_WIDTH), F32)),
        grid=(MEM_HEADS, L // tq),
        in_specs=[pl.BlockSpec((tq, HEAD_DIM), lambda h, i: (i, _MEM_Q_COL + h)),
                  pl.BlockSpec((N_MEM, HEAD_DIM), lambda h, i: (0, h)),
                  pl.BlockSpec((N_MEM, HEAD_DIM), lambda h, i: (0, MEM_HEADS + h)),
                  pl.BlockSpec((tq, HEAD_DIM), lambda h, i: (i, h)),
                  _ANY],
        out_specs=(pl.BlockSpec((tq, HEAD_DIM), lambda h, i: (i, _MEM_Q_COL + h)),
                   pl.BlockSpec((N_MEM, HEAD_DIM), lambda h, i: (0, h)),
                   pl.BlockSpec((N_MEM, HEAD_DIM), lambda h, i: (0, h))),
        input_output_aliases={4: 0},
        compiler_params=_params("parallel", "arbitrary"),
    )(proj, kvm, kvm, do, dproj)
    return dproj, jnp.concatenate([dk, dv], axis=1)


def _tile_cumsum(x, row, reverse):
    for sh in (1, 2, 4):
        if reverse:
            x = x + jnp.where(row < SUBLANES - sh, pltpu.roll(x, SUBLANES - sh, 0), 0.0)
        else:
            x = x + jnp.where(row >= sh, pltpu.roll(x, sh, 0), 0.0)
    return x


def _fgate_fwd(pre, b_pad):
    L = pre.shape[0]
    n8 = L // SUBLANES

    def body(p_ref, b_ref, o_ref):
        row = lax.broadcasted_iota(jnp.int32, (SUBLANES, LANES), 0)
        b = b_ref[...]

        def step(i, carry):
            x = p_ref[i] + b
            logf = jnp.minimum(x, 0.0) - jnp.log(1.0 + jnp.exp(-jnp.abs(x)))
            t = _tile_cumsum(logf, row, False) + carry
            o_ref[i] = t
            return t[SUBLANES - 1:SUBLANES, :]

        lax.fori_loop(0, n8, step, jnp.zeros((1, LANES), F32))

    out = pl.pallas_call(
        body, name="fgate_fwd", out_shape=jax.ShapeDtypeStruct((n8, SUBLANES, LANES), F32),
        compiler_params=_params(),
    )(pre.reshape(n8, SUBLANES, LANES), b_pad.reshape(1, LANES))
    return out.reshape(L, LANES)


def _fgate_bwd(dfcum, pre, b_pad):
    L = pre.shape[0]
    n8 = L // SUBLANES

    def body(d_ref, p_ref, b_ref, o_ref, s_ref):
        row = lax.broadcasted_iota(jnp.int32, (SUBLANES, LANES), 0)
        b = b_ref[...]

        def step(k, carry):
            c, acc = carry
            i = n8 - 1 - k
            t = _tile_cumsum(d_ref[i], row, True) + c
            dpre = t * _sigmoid(-(p_ref[i] + b))
            o_ref[i] = dpre
            return t[0:1, :], acc + dpre

        _, acc = lax.fori_loop(0, n8, step, (jnp.zeros((1, LANES), F32), jnp.zeros((SUBLANES, LANES), F32)))
        s_ref[...] = jnp.sum(acc, axis=0, keepdims=True)

    dpre, db = pl.pallas_call(
        body, name="fgate_bwd",
        out_shape=(jax.ShapeDtypeStruct((n8, SUBLANES, LANES), F32), jax.ShapeDtypeStruct((1, LANES), F32)),
        compiler_params=_params(),
    )(dfcum.reshape(n8, SUBLANES, LANES), pre.reshape(n8, SUBLANES, LANES), b_pad.reshape(1, LANES))
    return dpre.reshape(L, LANES), db


FOX_BLOCK = 1024


def _fox_scores(qs, k, fk, diagonal, row0=0):
    s = lax.dot_general(qs, k, _NT, preferred_element_type=F32) - fk
    if diagonal:
        row = row0 + lax.broadcasted_iota(jnp.int32, s.shape, 0)
        col = lax.broadcasted_iota(jnp.int32, s.shape, 1)
        s = jnp.where(row >= col, s, NEG_BIG)
    return s


def _fox_diagonal_parts(tq):
    half = tq // 2
    return ((slice(0, half), half), (slice(half, tq), tq))


def _fox_specs(tq, L):
    nq = L // tq
    return dict(
        rows=lambda off: pl.BlockSpec((tq, HEAD_DIM), lambda h, i: (i, off + h)),
        seq=lambda off: pl.BlockSpec((L, HEAD_DIM), lambda h, i: (0, off + h)),
        col=pl.BlockSpec((None, None, tq, 1), lambda h, i: (h, i, 0, 0)),
        col_all=pl.BlockSpec((None, nq, tq, 1), lambda h, i: (h, 0, 0, 0)),
        row=pl.BlockSpec((None, None, 1, tq), lambda h, i: (h, i, 0, 0)),
        row_all=pl.BlockSpec((None, nq, 1, tq), lambda h, i: (h, 0, 0, 0)))


FOX_FWD_HEADS = 2
FOX_FWD_BLOCK = 1024


def _fox_fwd(proj, kv, fk):
    L = proj.shape[0]
    tq = min(FOX_FWD_BLOCK, L)
    nq = L // tq
    nh = FOX_FWD_HEADS
    W = nh * HEAD_DIM
    lse_shape = fk.shape[:2] + (fk.shape[3], 1)
    fk = fk.reshape(FOX_HEADS, nq, 1, tq)

    def body(q_ref, k_ref, v_ref, fk_ref, o_ref, lse_ref, m_s, l_s, acc_s):
        qi = pl.program_id(1)
        cols = [slice(a * HEAD_DIM, (a + 1) * HEAD_DIM) for a in range(nh)]
        qs = [(q_ref[:, cs] * (HEAD_DIM ** -0.5)).astype(BF16) for cs in cols]
        m_s[...] = jnp.full_like(m_s, NEG_BIG)
        l_s[...] = jnp.zeros_like(l_s)
        acc_s[...] = jnp.zeros_like(acc_s)

        def block(j, diagonal):
            r0 = pl.multiple_of(j * tq, tq)
            for a, cs in enumerate(cols):
                s = _fox_scores(qs[a], k_ref[pl.ds(r0, tq), cs], fk_ref[a, j], diagonal)
                m_new = jnp.maximum(m_s[a], jnp.max(s, axis=-1, keepdims=True))
                alpha = jnp.exp(m_s[a] - m_new)
                p = jnp.exp(s - m_new)
                l_s[a] = alpha * l_s[a] + jnp.sum(p, axis=-1, keepdims=True)
                acc_s[a] = alpha * acc_s[a] + jnp.dot(p.astype(BF16), v_ref[pl.ds(r0, tq), cs],
                                                      preferred_element_type=F32)
                m_s[a] = m_new

        def below(j, carry):
            block(j, False)
            return carry

        lax.fori_loop(0, qi, below, 0)
        block(qi, True)
        for a, cs in enumerate(cols):
            o_ref[:, cs] = acc_s[a] / l_s[a]
            lse_ref[a] = m_s[a] + jnp.log(l_s[a])

    att, lse = pl.pallas_call(
        body, name="fox_fwd",
        out_shape=(jax.ShapeDtypeStruct((L, MAIN_WIDTH), F32),
                   jax.ShapeDtypeStruct((FOX_HEADS, nq, tq, 1), F32)),
        grid=(FOX_HEADS // nh, nq),
        in_specs=[pl.BlockSpec((tq, W), lambda h, i: (i, h)),
                  pl.BlockSpec((L, W), lambda h, i: (0, h)),
                  pl.BlockSpec((L, W), lambda h, i: (0, FOX_HEADS // nh + h)),
                  pl.BlockSpec((nh, nq, 1, tq), lambda h, i: (h, 0, 0, 0))],
        out_specs=(pl.BlockSpec((tq, W), lambda h, i: (i, h)),
                   pl.BlockSpec((nh, None, tq, 1), lambda h, i: (h, i, 0, 0))),
        scratch_shapes=[pltpu.VMEM((nh, tq, 1), F32), pltpu.VMEM((nh, tq, 1), F32),
                        pltpu.VMEM((nh, tq, HEAD_DIM), F32)],
        compiler_params=_params("parallel", "parallel"),
    )(proj, kv, kv, fk)
    return att, lse.reshape(lse_shape)


def _fox_bwd(proj, kv, fk, lse, delta, datt, dproj):
    L = proj.shape[0]
    tq = min(FOX_BLOCK, L)
    nq = L // tq
    sp = _fox_specs(tq, L)

    def body(q_ref, k_ref, v_ref, fk_ref, lse_ref, dl_ref, do_ref, dp_hbm,
             dq_ref, dk_ref, dv_ref, dfq_ref, dfk_ref, dk_s, dv_s, df_s, dq_s, dfq_s):
        ki = pl.program_id(1)

        @pl.when(ki == 0)
        def _():
            dq_s[...] = jnp.zeros_like(dq_s)
            dfq_s[...] = jnp.zeros_like(dfq_s)

        k, v, fk = k_ref[...], v_ref[...], fk_ref[...]
        dk_s[...] = jnp.zeros_like(dk_s)
        dv_s[...] = jnp.zeros_like(dv_s)
        df_s[...] = jnp.zeros_like(df_s)

        def block(i, rows, width, diagonal):
            n = rows.stop - rows.start
            r0 = pl.multiple_of(i * tq + rows.start, n)
            qs = (q_ref[pl.ds(r0, n), :] * (HEAD_DIM ** -0.5)).astype(BF16)
            dob = do_ref[pl.ds(r0, n), :].astype(BF16)
            kw, vw = k[:width], v[:width]
            p = jnp.exp(_fox_scores(qs, kw, fk[:, :width], diagonal, rows.start) - lse_ref[i][rows])
            dp = lax.dot_general(dob, vw, _NT, preferred_element_type=F32)
            ds = p * (dp - dl_ref[i][rows])
            dsb = ds.astype(BF16)
            dv_s[:width] += lax.dot_general(p.astype(BF16), dob, _TN, preferred_element_type=F32)
            dk_s[:width] += lax.dot_general(dsb, qs, _TN, preferred_element_type=F32)
            df_s[:, :width] -= jnp.sum(ds, axis=0, keepdims=True)
            dq_s[i, rows] += jnp.dot(dsb, kw, preferred_element_type=F32)
            dfq_s[i, rows] += jnp.sum(ds, axis=1, keepdims=True)

        def above(i, carry):
            block(i, slice(0, tq), tq, False)
            return carry

        for rows, width in _fox_diagonal_parts(tq):
            block(ki, rows, width, True)
        lax.fori_loop(ki + 1, nq, above, 0)
        dk_ref[...] = dk_s[...].astype(BF16)
        dv_ref[...] = dv_s[...].astype(BF16)
        dfk_ref[...] = df_s[...]

        @pl.when(ki == nq - 1)
        def _():
            dq_ref[...] = (dq_s[...].reshape(L, HEAD_DIM) * (HEAD_DIM ** -0.5)).astype(BF16)
            dfq_ref[...] = dfq_s[...]

    return pl.pallas_call(
        body, name="fox_bwd",
        out_shape=(jax.ShapeDtypeStruct(dproj.shape, dproj.dtype),
                   jax.ShapeDtypeStruct((L, MAIN_WIDTH), BF16),
                   jax.ShapeDtypeStruct((L, MAIN_WIDTH), BF16),
                   jax.ShapeDtypeStruct((FOX_HEADS, nq, tq, 1), F32),
                   jax.ShapeDtypeStruct((FOX_HEADS, nq, 1, tq), F32)),
        grid=(FOX_HEADS, nq),
        in_specs=[sp["seq"](0), sp["rows"](0), sp["rows"](FOX_HEADS), sp["row"],
                  sp["col_all"], sp["col_all"], sp["seq"](0), _ANY],
        out_specs=(sp["seq"](0), sp["rows"](0), sp["rows"](0), sp["col_all"], sp["row"]),
        input_output_aliases={7: 0},
        scratch_shapes=[pltpu.VMEM((tq, HEAD_DIM), F32), pltpu.VMEM((tq, HEAD_DIM), F32),
                        pltpu.VMEM((1, tq), F32), pltpu.VMEM((nq, tq, HEAD_DIM), F32),
                        pltpu.VMEM((nq, tq, 1), F32)],
        compiler_params=_params("parallel", "arbitrary"),
    )(proj, kv, kv, fk, lse, delta, datt, dproj)


def _pad_lanes(a):
    return jnp.pad(a, ((0, 0), (0, LANES - a.shape[1])))


def _mem_branch_fwd(memn, w_mk, proj, tag):
    kvm = _mm(memn, w_mk, name="mem_kv_" + tag)
    return kvm, _mem_attn_fwd(proj, kvm)


def _mem_branch_bwd(mem, g, w_mk, proj, memn, kvm, do_mem, dproj, tag):
    dproj, dkvm = _mem_attn_bwd(proj, kvm, do_mem, dproj)
    dkvm = dkvm.astype(BF16)
    dw_mk = _mm(memn, dkvm, ta=True, name="dw_mem_kv_" + tag, out_dtype=BF16)
    dmemn = _mm(dkvm, w_mk, tb=True, name="dmemn_" + tag)
    _, dg = _rmsnorm_bwd(mem, g, dmemn, name="mem_norm_bwd_" + tag, dx_dtype=BF16)
    return dproj, dw_mk, dg


def _local_step(x, mem, target, w, fetch=None, grads_ready=None):
    if grads_ready is None:
        grads_ready = lambda group, grads, token: token
    L = x.shape[0]
    g = {}
    w = dict(w)

    b_re_t = jnp.transpose(w["b_re"], (0, 2, 1))
    b_im_t = jnp.transpose(w["b_im"], (0, 2, 1))
    ar, ai, bbr_t, bbi_t = _s5_prep(w["lam_re"], w["lam_im"], w["log_step"], b_re_t, b_im_t)
    bmat, cmat = _s5_block_mats(bbr_t, bbi_t, w["c_re"], w["c_im"])
    a_rows = _s5_a_rows(ar, ai)

    hn0 = _rmsnorm_fwd(x, w["pre_norm_g"][0], name="pre_norm_0", out_dtype=BF16)
    memn0 = _rmsnorm_fwd(mem, w["mem_norm_g"][0], name="mem_norm_0", out_dtype=BF16)
    memn1 = _rmsnorm_fwd(mem, w["mem_norm_g"][1], name="mem_norm_1", out_dtype=BF16)
    if fetch is not None:
        w.update(fetch("a", [hn0, memn0, memn1, bmat, cmat, a_rows]))
    proj_a = _mm(hn0, w["w_in_a"], name="in_proj_a")
    y, yg, xp = _s5_fwd(proj_a, bmat, cmat, a_rows, w["d_skip"])
    if fetch is not None:
        w.update(fetch("b", yg))
    t = _mm(yg, w["w_glu"], name="glu_proj")
    kvm0, om0 = _mem_branch_fwd(memn0, w["w_mem_kv"][0], proj_a, "0")
    cat0 = _gate_a_fwd(y, t, w["b_glu"], proj_a, om0)
    o0 = _mm(cat0, w["w_out"][0], name="out_proj_0")
    h1, kv_in, hn1 = _post_norm_and_next_norms(
        o0, w["post_norm_g"][0], x, w["kv_norm_g"], w["pre_norm_g"][1], name="post_norm_0_kv_pre_norm_1")

    if fetch is not None:
        w.update(fetch("c", kv_in))
    kv = _mm(kv_in, w["w_kv"], name="kv_proj", out_dtype=BF16)
    pre_f = _mm(kv_in, w["w_fgate"], name="fgate_proj")
    b_f = jnp.pad(w["b_fgate"], (0, LANES - FOX_HEADS))
    fcum = _fgate_fwd(pre_f, b_f)
    fc = jnp.transpose(fcum[:, :FOX_HEADS])
    tq = min(FOX_BLOCK, L)
    fk = fc.reshape(FOX_HEADS, L // tq, 1, tq)

    proj_b = _mm(hn1, w["w_in_b"], name="in_proj_b")
    att, lse = _fox_fwd(proj_b, kv, fk)
    kvm1, om1 = _mem_branch_fwd(memn1, w["w_mem_kv"][1], proj_b, "1")
    cat1 = _gate_b_fwd(att, proj_b, om1)
    o1 = _mm(cat1, w["w_out"][1], name="out_proj_1")
    dh2, loss_row = _final_norm_loss(o1, w["post_norm_g"][1], h1, target)

    do1, dpost1 = _rmsnorm_bwd(o1, w["post_norm_g"][1], dh2, name="post_norm_bwd_1", dx_dtype=BF16)
    dcat1 = _mm(do1, w["w_out"][1], tb=True, name="dcat_1", out_dtype=BF16)
    g["w_out_1"] = _mm(cat1, do1, ta=True, name="dw_out_1", out_dtype=BF16)
    datt, dproj_b, dom1, delta = _gate_b_bwd(dcat1, att, proj_b, om1)
    dproj_b, g["w_mem_kv_1"], dmemg1 = _mem_branch_bwd(mem, w["mem_norm_g"][1], w["w_mem_kv"][1], proj_b,
                                                      memn1, kvm1, dom1, dproj_b, "1")
    delta = delta.reshape(lse.shape)
    dproj_b, dk, dv, dfq, dfk = _fox_bwd(proj_b, kv, fk, lse, delta, datt, dproj_b)
    g["w_in_b"] = _mm(hn1, dproj_b, ta=True, name="dw_in_b", out_dtype=BF16, shards=N_CHIPS)
    dhn1 = _mm(dproj_b, w["w_in_b"], tb=True, name="dhn_1")

    dkv = jnp.concatenate([dk, dv], axis=1)
    g["w_kv"] = _mm(kv_in, dkv, ta=True, name="dw_kv", out_dtype=BF16, shards=N_CHIPS)
    dkv_in_a = _mm(dkv, w["w_kv"], tb=True, name="dkv_in_kv")
    dfcum = _pad_lanes(jnp.transpose(dfq.reshape(FOX_HEADS, L) + dfk.reshape(FOX_HEADS, L)))
    dpre_f, db_f = _fgate_bwd(dfcum, pre_f, b_f)
    g["b_fgate"] = db_f[0, :FOX_HEADS]
    g["w_fgate"] = _mm(kv_in, dpre_f, ta=True, name="dw_fgate")[:, :FOX_HEADS]
    dkv_in_b = _mm(dpre_f, w["w_fgate"], tb=True, name="dkv_in_fgate")
    dh1, g["kv_norm_g"], dpre1 = _rmsnorm_bwd_pair(h1, w["kv_norm_g"], (dkv_in_a, dkv_in_b), w["pre_norm_g"][1],
                                                   dhn1, adds=(dh2,), name="kv_pre_norm_bwd")
    dh1 = grads_ready("b", g, dh1)

    do0, dpost0 = _rmsnorm_bwd(o0, w["post_norm_g"][0], dh1, name="post_norm_bwd_0", dx_dtype=BF16)
    dcat0 = _mm(do0, w["w_out"][0], tb=True, name="dcat_0", out_dtype=BF16)
    g["w_out_0"] = _mm(cat0, do0, ta=True, name="dw_out_0", out_dtype=BF16)
    dcat0 = grads_ready("b_send", g, dcat0)
    dproj_a, dt, dyg_a, dom0, db_glu = _gate_a_bwd(dcat0, y, t, w["b_glu"], proj_a, om0)
    g["b_glu"] = db_glu[0]
    g["w_glu"] = _mm(yg, dt, ta=True, name="dw_glu", out_dtype=BF16)
    dyg_b = _mm(dt, w["w_glu"], tb=True, name="dyg")
    dproj_a, g["w_mem_kv_0"], dmemg0 = _mem_branch_bwd(mem, w["mem_norm_g"][0], w["w_mem_kv"][0], proj_a,
                                                      memn0, kvm0, dom0, dproj_a, "0")
    dyg_b = grads_ready("a1", g, dyg_b)
    dproj_a, db_blk, dc_blk, da_rows, dd_skip = _s5_bwd(proj_a, dyg_a, dyg_b, y, xp, bmat, cmat, a_rows,
                                                        w["d_skip"], dproj_a)
    dproj_a = grads_ready("a1_send", g, dproj_a)
    g["d_skip"] = dd_skip[0]
    g["w_in_a"] = _mm(hn0, dproj_a, ta=True, name="dw_in_a", out_dtype=BF16, shards=N_CHIPS)
    dproj_a = grads_ready("a2", g, dproj_a)
    dhn0 = _mm(dproj_a, w["w_in_a"], tb=True, name="dhn_0")
    grad_x, dpre0 = _rmsnorm_bwd(x, w["pre_norm_g"][0], dhn0, adds=(dh1,), name="pre_norm_bwd_0")

    dbb = _s5_unfold(db_blk)
    dcc = _s5_unfold(dc_blk)
    g["c_re"], g["c_im"] = dcc[0], -dcc[1]
    d_ar = da_rows[:, 0, :STATE_COLS].reshape(SSM_GROUPS, SSM_STATE)
    d_ai = da_rows[:, 0, STATE_COLS:].reshape(SSM_GROUPS, SSM_STATE)
    dlr, dli, dls, dbr_t, dbi_t = _s5_prep_bwd(w["lam_re"], w["lam_im"], w["log_step"], b_re_t, b_im_t,
                                               d_ar, d_ai, dbb[0], dbb[1])
    g["lam_re"], g["lam_im"], g["log_step"] = dlr, dli, dls[:, 0]
    g["b_re"] = jnp.transpose(dbr_t, (0, 2, 1))
    g["b_im"] = jnp.transpose(dbi_t, (0, 2, 1))
    g["pre_norm_g"] = jnp.stack([dpre0, dpre1])
    g["post_norm_g"] = jnp.stack([dpost0, dpost1])
    g["mem_norm_g"] = jnp.stack([dmemg0, dmemg1])
    return loss_row, grad_x, g


_MESH = pl.DeviceIdType.MESH
_ANY = pl.BlockSpec(memory_space=pl.ANY)


def _place():
    x, y, c = lax.axis_index("x"), lax.axis_index("y"), lax.axis_index("c")
    chips = [(1 - x, y), (x, 1 - y), (1 - x, 1 - y)]
    return x, y, c, chips


_HBM = pl.BlockSpec(memory_space=pltpu.HBM)
_SEM = pl.BlockSpec(memory_space=pltpu.SEMAPHORE)
_SIDE = pltpu.SideEffectType.DATAFLOW_SIDE_EFFECTING


def _in_hbm(a):
    return pltpu.with_memory_space_constraint(a, pltpu.HBM)


def _hbm_like(a):
    return pltpu.HBM(a.shape, a.dtype)


def _ici_copies(srcs, lands, send_sem, recv_sem, src_at, dst_at, wait_at, to_sibling=False):
    x, y, c, chips = _place()
    peers = [(x, y, 1 - c)] if to_sibling else [(cx, cy, c) for cx, cy in chips]
    m = len(peers)
    start, wait = [], []
    for i in range(len(srcs)):
        for k, (px, py, pc) in enumerate(peers):
            sem = dict(send_sem=send_sem.at[m * i + k], recv_sem=recv_sem.at[m * i + k],
                       device_id=(px, py, pc), device_id_type=_MESH)
            src = src_at(srcs[i], 2 * px + py, c)
            start.append(pltpu.make_async_remote_copy(src_ref=src, dst_ref=dst_at(lands[i], 2 * x + y, k, c), **sem))
            wait.append(pltpu.make_async_remote_copy(src_ref=src, dst_ref=wait_at(lands[i], 2 * px + py, k, c), **sem))
    return start, wait


def _route_peers(route):
    return 1 if len(route) == 4 else 3


_BLOCK_ROUTE = (lambda s, j, c: s, lambda l, me, k, c: l.at[me, c], lambda l, j, k, c: l.at[j, c])


def _ici_start(srcs, lands, token, route, *, name):
    n = len(srcs)

    def body(*refs):
        start, _ = _ici_copies(refs[:n], refs[n:2 * n], refs[2 * n + 1], refs[2 * n + 2], *route)
        for cp in start:
            cp.start()

    sems = pltpu.SemaphoreType.DMA((_route_peers(route) * n,))
    outs = pl.pallas_call(
        body, name=name,
        out_shape=(sems, sems, *[_hbm_like(a) for a in srcs], *[_hbm_like(a) for a in lands], _hbm_like(token)),
        in_specs=[_HBM] * (2 * n + 1), out_specs=(_SEM, _SEM, *[_HBM] * (2 * n + 1)),
        input_output_aliases={i: 2 + i for i in range(2 * n + 1)},
        compiler_params=pltpu.CompilerParams(has_side_effects=_SIDE),
    )(*[_in_hbm(a) for a in srcs], *[_in_hbm(a) for a in lands], _in_hbm(token))
    return (outs[0], outs[1], list(outs[2:2 + n]), list(outs[2 + n:2 + 2 * n])), outs[2 + 2 * n]


def _ici_wait(handle, after, route, *, name):
    send_sem, recv_sem, srcs, lands = handle
    n = len(srcs)
    after = list(after) if isinstance(after, (list, tuple)) else [after]

    def body(*refs):
        _, wait = _ici_copies(refs[:n], refs[n:2 * n], refs[2 * n], refs[2 * n + 1], *route)
        for cp in wait:
            cp.wait_send()
            cp.wait_recv()

    outs = pl.pallas_call(
        body, name=name,
        out_shape=(*[_hbm_like(a) for a in srcs], *[_hbm_like(a) for a in lands]),
        in_specs=[_HBM] * (2 * n) + [_SEM, _SEM] + [_ANY] * len(after), out_specs=tuple([_HBM] * (2 * n)),
        input_output_aliases={i: i for i in range(2 * n)},
        compiler_params=pltpu.CompilerParams(has_side_effects=_SIDE),
    )(*srcs, *lands, send_sem, recv_sem, *after)
    return list(outs[:n]), list(outs[n:])


_GATHER_ROUTE = (lambda s, j, c: s.at[c], lambda l, me, k, c: l.at[me, c], lambda l, j, k, c: l.at[j, c])
_SCATTER_ROUTE = (lambda s, j, c: s.at[j], lambda l, me, k, c: l.at[k], lambda l, j, k, c: l.at[k])
_SHARE_ROUTE = (lambda s, j, c: s, lambda l, me, k, c: l.at[c], lambda l, j, k, c: l.at[1 - c], True)
_SWAP_ROUTE = (lambda s, j, c: s.at[:, 1 - c], lambda l, me, k, c: l, lambda l, j, k, c: l, True)


def _gather_forward(lands, tag, own=False):
    n = len(lands)
    m = 4 if own else 3

    def body(*refs):
        ins, outs = refs[:n], refs[n:2 * n]
        send_sem, recv_sem = refs[2 * n:]
        x, y, c, chips = _place()
        slots = [2 * cx + cy for cx, cy in chips] + [2 * x + y]

        def copy(i, k, half):
            return pltpu.make_async_remote_copy(
                src_ref=ins[i].at[slots[k], half], dst_ref=outs[i].at[slots[k], half],
                send_sem=send_sem.at[m * i + k], recv_sem=recv_sem.at[m * i + k],
                device_id=(x, y, 1 - c), device_id_type=_MESH)

        copies = [copy(i, k, c) for i in range(n) for k in range(m)]
        for cp in copies:
            cp.start()
        for i in range(n):
            for k in range(m):
                copy(i, k, 1 - c).wait_recv()
        for cp in copies:
            cp.wait_send()

    return pl.pallas_call(
        body, name="gather_forward_to_sibling_" + tag,
        out_shape=[jax.ShapeDtypeStruct(a.shape, a.dtype) for a in lands],
        in_specs=[_ANY] * n, out_specs=[_ANY] * n,
        input_output_aliases={i: i for i in range(n)},
        scratch_shapes=[pltpu.SemaphoreType.DMA((m * n,)), pltpu.SemaphoreType.DMA((m * n,))],
    )(*lands)


def _swap_halves(grads, tag):
    n = len(grads)

    def body(*refs):
        ins, outs = refs[:n], refs[n:2 * n]
        send_sem, recv_sem = refs[2 * n:]
        x, y, c, _ = _place()
        copies = [pltpu.make_async_remote_copy(
            src_ref=ins[i].at[:, 1 - c], dst_ref=outs[i],
            send_sem=send_sem.at[i], recv_sem=recv_sem.at[i],
            device_id=(x, y, 1 - c), device_id_type=_MESH) for i in range(n)]
        for cp in copies:
            cp.start()
        for cp in copies:
            cp.wait()

    return pl.pallas_call(
        body, name="grad_swap_halves_" + tag,
        out_shape=[jax.ShapeDtypeStruct((N_CHIPS,) + g.shape[2:], g.dtype) for g in grads],
        in_specs=[_ANY] * n, out_specs=[_ANY] * n,
        scratch_shapes=[pltpu.SemaphoreType.DMA((n,)), pltpu.SemaphoreType.DMA((n,))],
    )(*grads)


def _sum_rows(h, C):
    return max(d for d in range(SUBLANES, h + 1, SUBLANES) if h % d == 0 and d * C <= 1 << 20)


SUM_STEPS = 4


def _pair_sums(gs, rs, c_idx, *, name):
    n = len(gs)
    rows = [g.shape[2] // SUM_STEPS for g in gs]

    def body(c_ref, *refs):
        for g_ref, r_ref, o_ref in zip(refs[:n], refs[n:2 * n], refs[2 * n:]):
            o_ref[...] = (g_ref[...].astype(F32) + r_ref[...].astype(F32)).astype(o_ref.dtype)

    return pl.pallas_call(
        body, name=name,
        out_shape=[jax.ShapeDtypeStruct((N_CHIPS,) + g.shape[2:], g.dtype) for g in gs],
        grid_spec=pltpu.PrefetchScalarGridSpec(
            num_scalar_prefetch=1, grid=(N_CHIPS, SUM_STEPS),
            in_specs=[pl.BlockSpec((None, None, tr, g.shape[3]), lambda j, i, s: (j, s[0], i, 0))
                      for g, tr in zip(gs, rows)]
            + [pl.BlockSpec((None, tr, g.shape[3]), lambda j, i, s: (j, i, 0)) for g, tr in zip(gs, rows)],
            out_specs=[pl.BlockSpec((None, tr, g.shape[3]), lambda j, i, s: (j, i, 0)) for g, tr in zip(gs, rows)]),
        compiler_params=_params("parallel", "parallel"),
    )(c_idx, *gs, *rs)


def _owner_sums(ss, rs, jc_idx, *, name):
    n = len(ss)
    rows = [s.shape[1] // SUM_STEPS for s in ss]

    def body(jc_ref, *refs):
        for s_ref, r_ref, m_ref, o_ref in zip(refs[:n], refs[n:2 * n], refs[2 * n:3 * n], refs[3 * n:]):
            acc = s_ref[...].astype(F32)
            for k in range(3):
                acc = acc + r_ref[k].astype(F32)
            m_ref[...] = acc
            o_ref[...] = acc

    outs = pl.pallas_call(
        body, name=name,
        out_shape=[jax.ShapeDtypeStruct(s.shape[1:], F32) for s in ss]
        + [jax.ShapeDtypeStruct((2,) + s.shape[1:], F32) for s in ss],
        grid_spec=pltpu.PrefetchScalarGridSpec(
            num_scalar_prefetch=1, grid=(SUM_STEPS,),
            in_specs=[pl.BlockSpec((None, tr, s.shape[2]), lambda i, p: (p[0], i, 0)) for s, tr in zip(ss, rows)]
            + [pl.BlockSpec((3, tr, s.shape[2]), lambda i, p: (0, i, 0)) for s, tr in zip(ss, rows)],
            out_specs=[pl.BlockSpec((tr, s.shape[2]), lambda i, p: (i, 0)) for s, tr in zip(ss, rows)]
            + [pl.BlockSpec((None, tr, s.shape[2]), lambda i, p: (p[1], i, 0)) for s, tr in zip(ss, rows)]),
        compiler_params=_params("parallel"),
    )(jc_idx, *ss, *rs)
    return outs[:n], outs[n:]


def _chip_sums(grads, c_idx, tag):
    views = [g.reshape(N_CHIPS, 2, g.shape[1] // 2, g.shape[2]) for g in grads]
    arrived = _swap_halves(views, tag)
    return _pair_sums(views, arrived, c_idx, name=f"grad_pair_sums_{tag}")


def _sum_devices(blocks):
    R = blocks.shape[2]
    tr = _sum_rows(R, 2 * N_CHIPS * LANES)

    def body(b_ref, o_ref):
        acc = b_ref[0, 0]
        for d in range(1, 2 * N_CHIPS):
            acc = acc + b_ref[d // 2, d % 2]
        o_ref[...] = acc

    return pl.pallas_call(
        body, name="sum_small_over_devices", out_shape=jax.ShapeDtypeStruct((R, LANES), F32),
        grid=(R // tr,),
        in_specs=[pl.BlockSpec((N_CHIPS, 2, tr, LANES), lambda i: (0, 0, i, 0))],
        out_specs=pl.BlockSpec((tr, LANES), lambda i: (i, 0)),
        compiler_params=_params("parallel"),
    )(blocks)


def _adamw(w, g, m, v, *, name):
    R, C = w.shape
    tr = max(d for d in range(SUBLANES, R + 1, SUBLANES)
             if R % d == 0 and 7 * 2 * d * C * 4 <= VMEM_LIMIT_BYTES // 2)

    def body(w_ref, g_ref, m_ref, v_ref, d_ref, nm_ref, nv_ref):
        g = g_ref[...]
        m = ADAM_B1 * m_ref[...] + (1.0 - ADAM_B1) * g
        v = ADAM_B2 * v_ref[...] + (1.0 - ADAM_B2) * (g * g)
        nm_ref[...] = m
        nv_ref[...] = v
        m_hat = m / (1.0 - ADAM_B1 ** ADAM_STEP)
        v_hat = v / (1.0 - ADAM_B2 ** ADAM_STEP)
        d_ref[...] = -ADAM_LR * (m_hat / (jnp.sqrt(v_hat) + ADAM_EPS) + ADAM_WD * w_ref[...])

    blk = pl.BlockSpec((tr, C), lambda i: (i, 0))
    sds = jax.ShapeDtypeStruct((R, C), F32)
    return pl.pallas_call(
        body, name=name, out_shape=(sds, sds, sds), grid=(R // tr,),
        in_specs=[blk] * 4, out_specs=(blk, blk, blk),
        compiler_params=_params("parallel"),
    )(w, g, m, v)


_TILE = SUBLANES * LANES


def _pack(arrays):
    rows = []
    for a in arrays:
        flat = a.reshape(-1)
        flat = jnp.pad(flat, (0, (-flat.shape[0]) % _TILE))
        rows.append(flat.reshape(-1, LANES))
    return jnp.concatenate(rows, axis=0)


def _unpack(buf, shapes):
    out, r = [], 0
    for s in shapes:
        size = math.prod(s)
        nr = -(-size // _TILE) * SUBLANES
        out.append(buf[r:r + nr].reshape(-1)[:size].reshape(s))
        r += nr
    return out


_BIG = ("w_in_a", "w_glu", "w_kv", "w_in_b", "w_mem_kv", "w_out")
_REPLICATED = ("pre_norm_g", "post_norm_g", "lam_re", "lam_im", "log_step", "b_re", "b_im", "c_re", "c_im",
               "kv_norm_g", "b_fgate", "mem_norm_g")
_SHARDED_SMALL = ("d_skip", "b_glu", "w_fgate")
_WEIGHTS = ("pre_norm_g", "post_norm_g", "w_in_a", "lam_re", "lam_im", "log_step", "b_re", "b_im", "c_re",
            "c_im", "d_skip", "w_glu", "b_glu", "kv_norm_g", "w_kv", "w_fgate", "b_fgate", "w_in_b",
            "mem_norm_g", "w_mem_kv", "w_out")


def _halves(a):
    return a.reshape(2, a.shape[0] // 2, a.shape[1])


def _unhalve(a):
    return a.reshape(N_CHIPS, 2 * a.shape[2], a.shape[3])


def _columns(a):
    return jnp.transpose(a, (1, 0, 2)).reshape(a.shape[1], N_CHIPS * a.shape[2])


def kernel(x, mem, pre_norm_g, post_norm_g, w_in_a, lam_re, lam_im, log_step, b_re, b_im, c_re, c_im, d_skip, w_glu, b_glu, kv_norm_g, w_kv, w_fgate, b_fgate, w_in_b, mem_norm_g, w_mem_kv, w_out, loss_target, m_pre_norm_g, m_post_norm_g, m_w_in_a, m_lam_re, m_lam_im, m_log_step, m_b_re, m_b_im, m_c_re, m_c_im, m_d_skip, m_w_glu, m_b_glu, m_kv_norm_g, m_w_kv, m_w_fgate, m_b_fgate, m_w_in_b, m_mem_norm_g, m_w_mem_kv, m_w_out, v_pre_norm_g, v_post_norm_g, v_w_in_a, v_lam_re, v_lam_im, v_log_step, v_b_re, v_b_im, v_c_re, v_c_im, v_d_skip, v_w_glu, v_b_glu, v_kv_norm_g, v_w_kv, v_w_fgate, v_b_fgate, v_w_in_b, v_mem_norm_g, v_w_mem_kv, v_w_out):
    a = dict(locals())
    xi, yi, ci = lax.axis_index("x"), lax.axis_index("y"), lax.axis_index("c")
    chip = 2 * xi + yi
    c_idx = jnp.reshape(ci, (1,)).astype(jnp.int32)
    jc_idx = jnp.stack([chip, ci]).astype(jnp.int32)

    vec = jnp.zeros((2 * SUBLANES, MAIN_WIDTH // N_CHIPS), F32)
    vec = vec.at[0].set(a["d_skip"][0]).at[1].set(a["b_glu"][0])
    def own_slot(gathered, parts):
        return [lax.dynamic_update_index_in_dim(g, p, chip, 0) for g, p in zip(gathered, parts)]

    parts_a = [_halves(a["w_in_a"][0].astype(BF16)), _halves(vec)]
    parts_b = [_halves(a["w_glu"][0].astype(BF16)),
               *[_halves(a["w_mem_kv"][i].astype(BF16)) for i in range(2)],
               *[_halves(a["w_out"][i].astype(BF16)) for i in range(2)]]
    parts_c = [_halves(a["w_kv"].astype(BF16)), _halves(_pad_lanes(a["w_fgate"]).astype(BF16)),
               _halves(a["w_in_b"][0].astype(BF16))]
    travelling, token = {}, a["pre_norm_g"]
    for tag, parts in (("a", parts_a), ("b", parts_b), ("c", parts_c)):
        lands = [lax.empty((N_CHIPS,) + p.shape, p.dtype) for p in parts]
        travelling[tag], token = _ici_start(parts, lands, token, _GATHER_ROUTE, name=f"gather_{tag}_start")

    def fetch(tag, after):
        parts, lands = _ici_wait(travelling[tag], after, _GATHER_ROUTE, name=f"gather_{tag}_wait")
        full = own_slot(_gather_forward(lands, tag), parts)
        if tag == "a":
            w_in_a, vecs = full
            return dict(w_in_a=_columns(_unhalve(w_in_a)), d_skip=vecs[:, 0, 0, :].reshape(MAIN_WIDTH),
                        b_glu=vecs[:, 0, 1, :].reshape(MAIN_WIDTH))
        if tag == "b":
            w_glu, w_mk0, w_mk1, w_out0, w_out1 = full
            return dict(w_glu=w_glu.reshape(MAIN_WIDTH, MAIN_WIDTH),
                        w_mem_kv=[m.reshape(D_MODEL, 2 * MEM_WIDTH) for m in (w_mk0, w_mk1)],
                        w_out=[o.reshape(D_MODEL, D_MODEL) for o in (w_out0, w_out1)])
        w_kv, w_fg, w_in_b = full
        return dict(w_kv=_columns(_unhalve(w_kv)), w_fgate=w_fg.reshape(D_MODEL, LANES),
                    w_in_b=_columns(_unhalve(w_in_b)))

    w = dict(
        pre_norm_g=token, post_norm_g=a["post_norm_g"], mem_norm_g=a["mem_norm_g"],
        kv_norm_g=a["kv_norm_g"], b_fgate=a["b_fgate"],
        lam_re=a["lam_re"][0], lam_im=a["lam_im"][0], log_step=a["log_step"][0],
        b_re=a["b_re"][0], b_im=a["b_im"][0], c_re=a["c_re"][0], c_im=a["c_im"][0])

    sent = {}

    swapping = {}

    def grads_ready(event, g, token):
        tag = event.split("_")[0]
        if event in ("b", "a1"):
            big = {"b": lambda: [g["w_kv"], g["w_in_b"], g["w_mem_kv_1"].reshape(N_CHIPS, -1, 2 * MEM_WIDTH),
                                 g["w_out_1"].reshape(N_CHIPS, -1, D_MODEL)],
                   "a1": lambda: [g["w_glu"].reshape(N_CHIPS, -1, MAIN_WIDTH),
                                  g["w_mem_kv_0"].reshape(N_CHIPS, -1, 2 * MEM_WIDTH),
                                  g["w_out_0"].reshape(N_CHIPS, -1, D_MODEL)]}[tag]()
            views = [b.reshape(N_CHIPS, 2, b.shape[1] // 2, b.shape[2]) for b in big]
            lands = [lax.empty((N_CHIPS,) + v.shape[2:], v.dtype) for v in views]
            swapping[tag], token = _ici_start(views, lands, token, _SWAP_ROUTE, name=f"grad_swap_{tag}_start")
            return token
        if event == "a2":
            sums = _chip_sums([g["w_in_a"]], c_idx, tag)
        else:
            views, arrived = _ici_wait(swapping[tag], token, _SWAP_ROUTE, name=f"grad_swap_{tag}_wait")
            sums = _pair_sums(views, arrived, c_idx, name=f"grad_pair_sums_{tag}")
        lands = [lax.empty((3,) + s.shape[1:], s.dtype) for s in sums]
        sent[tag], token = _ici_start(sums, lands, token, _SCATTER_ROUTE, name=f"grad_send_{tag}_start")
        return token

    loss_row, grad_x, g = _local_step(a["x"][0], a["mem"][0], a["loss_target"][0], w, fetch, grads_ready)

    small_names = _REPLICATED + _SHARDED_SMALL
    pack = _pack([g[n] for n in small_names])
    blocks = lax.empty((N_CHIPS, 2) + pack.shape, F32)
    small_sent, token = _ici_start([pack], [blocks], loss_row, _BLOCK_ROUTE, name="small_sums_start")

    sharing = {}
    for tag in ("b", "a1", "a2"):
        sums, arrived = _ici_wait(sent[tag], [grad_x, token], _SCATTER_ROUTE, name=f"grad_send_{tag}_wait")
        mine, bufs = _owner_sums(sums, arrived, jc_idx, name=f"grad_owner_sums_{tag}")
        sharing[tag], token = _ici_start(mine, bufs, token, _SHARE_ROUTE, name=f"grad_share_{tag}_start")
    loss = lax.psum(jnp.sum(token), MESH_AXES)

    def shared(tag, after):
        _, bufs = _ici_wait(sharing[tag], after, _SHARE_ROUTE, name=f"grad_share_{tag}_wait")
        return [b.reshape(-1, b.shape[2]) for b in bufs]

    grads, delta, new_m, new_v = {}, {}, {}, {}

    def adam(n):
        shape = a[n].shape
        d2 = (-1, shape[-1])
        d, m, v = _adamw(a[n].reshape(d2), grads[n].reshape(d2), a["m_" + n].reshape(d2),
                         a["v_" + n].reshape(d2), name="adamw_" + n)
        delta[n], new_m[n], new_v[n] = d.reshape(shape), m.reshape(shape), v.reshape(shape)
        return d

    r_kv, r_in_b, r_mk1, r_out1 = shared("b", token)
    grads["w_kv"], grads["w_in_b"] = r_kv, r_in_b[None]
    done = [adam("w_kv"), adam("w_in_b")]
    r_glu, r_mk0, r_out0 = shared("a1", done)
    grads["w_glu"], grads["w_mem_kv"], grads["w_out"] = r_glu[None], jnp.stack([r_mk0, r_mk1]), jnp.stack([r_out0, r_out1])
    done = [adam("w_glu"), adam("w_mem_kv"), adam("w_out")]
    (r_in_a,) = shared("a2", done)
    grads["w_in_a"] = r_in_a[None]
    adam("w_in_a")

    (pack,), (blocks,) = _ici_wait(small_sent, [delta[n] for n in _BIG], _BLOCK_ROUTE, name="small_sums_wait")
    blocks = lax.dynamic_update_slice(blocks, pack[None, None], (chip, ci, 0, 0))
    (blocks,) = _gather_forward([blocks], "small", own=True)
    small = dict(zip(small_names, _unpack(_sum_devices(blocks), [g[n].shape for n in small_names])))
    for n in _REPLICATED:
        grads[n] = small[n].reshape(a[n].shape)
    nd = MAIN_WIDTH // N_CHIPS
    grads["d_skip"] = lax.dynamic_slice(small["d_skip"], (chip * nd,), (nd,))[None]
    grads["b_glu"] = lax.dynamic_slice(small["b_glu"], (chip * nd,), (nd,))[None]
    nf = D_MODEL // N_CHIPS
    grads["w_fgate"] = lax.dynamic_slice(small["w_fgate"], (chip * nf, 0), (nf, FOX_HEADS))

    shapes = [a[n].shape for n in small_names]
    d, m, v = _adamw(_pack([a[n] for n in small_names]), _pack([grads[n] for n in small_names]),
                     _pack([a["m_" + n] for n in small_names]), _pack([a["v_" + n] for n in small_names]),
                     name="adamw_small")
    for n, dd, mm, vv in zip(small_names, _unpack(d, shapes), _unpack(m, shapes), _unpack(v, shapes)):
        delta[n], new_m[n], new_v[n] = dd, mm, vv

    return (loss, grad_x[None], *[grads[n] for n in _WEIGHTS], *[delta[n] for n in _WEIGHTS],
            *[new_m[n] for n in _WEIGHTS], *[new_v[n] for n in _WEIGHTS])
```

```python
import math

import jax
import jax.numpy as jnp
from jax import lax
from jax.experimental import pallas as pl
from jax.experimental.pallas import tpu as pltpu

F32 = jnp.float32
BF16 = jnp.bfloat16

D_MODEL = 2048
N_MEM = 256
MAIN_WIDTH = 1536
MEM_WIDTH = 512
IN_WIDTH = 2 * MAIN_WIDTH + 2 * MEM_WIDTH
HEAD_DIM = 128
FOX_HEADS = MAIN_WIDTH // HEAD_DIM
MEM_HEADS = MEM_WIDTH // HEAD_DIM
SSM_GROUP = 16
SSM_GROUPS = MAIN_WIDTH // SSM_GROUP
SSM_STATE = 64
GROUPS_PER_BLOCK = 8
SSM_BLOCKS = SSM_GROUPS // GROUPS_PER_BLOCK
STATE_COLS = GROUPS_PER_BLOCK * SSM_STATE
EPS = 1e-6
ADAM_LR = 0.001
ADAM_B1 = 0.9
ADAM_B2 = 0.999
ADAM_EPS = 1e-08
ADAM_WD = 0.01
ADAM_STEP = 10
N_CHIPS = 4
LANES = 128
SUBLANES = 8
VMEM_LIMIT_BYTES = 56 * 1024 * 1024
NEG_BIG = -1e30
MESH_AXES = ("x", "y", "c")


def _params(*sem):
    return pltpu.CompilerParams(dimension_semantics=sem if sem else None,
                                vmem_limit_bytes=VMEM_LIMIT_BYTES)


def _sigmoid(x):
    return 1.0 / (1.0 + jnp.exp(-x))


def _gelu(x):
    c = math.sqrt(2.0 / math.pi)
    return 0.5 * x * (1.0 + jnp.tanh(c * (x + 0.044715 * (x * x * x))))


def _gelu_grad(x):
    c = math.sqrt(2.0 / math.pi)
    t = jnp.tanh(c * (x + 0.044715 * (x * x * x)))
    return 0.5 * (1.0 + t) + 0.5 * x * (1.0 - t * t) * (c * (1.0 + 3.0 * 0.044715 * (x * x)))


def _silu_and_grad(z):
    s = _sigmoid(z)
    return z * s, s * (1.0 + z * (1.0 - s))


_TILE_CHOICES = (4096, 3072, 2048, 1536, 1024, 768, 512, 384, 256, LANES)


def _tile(n, cap):
    return next(c for c in _TILE_CHOICES if c <= cap and n % c == 0)


def _mm(a, b, *, name, ta=False, tb=False, out_dtype=F32, shards=1, tm=1024, tn=1024, tk=4096):
    if ta:
        K, M = a.shape
    else:
        M, K = a.shape
    if tb:
        N, kb = b.shape
    else:
        kb, N = b.shape
    assert K == kb, (a.shape, b.shape)
    ns = N // shards
    tm, tn, tk = _tile(M, tm), _tile(ns, tn), _tile(K, tk)
    assert M % tm == 0 and ns % tn == 0 and K % tk == 0 and N % shards == 0
    nk = K // tk
    dn = (((0 if ta else 1,), (1 if tb else 0,)), ((), ()))

    def body(a_ref, b_ref, o_ref, *acc):
        prod = lax.dot_general(a_ref[...].astype(BF16), b_ref[...].astype(BF16), dn, preferred_element_type=F32)
        if nk == 1:
            o_ref[...] = prod.astype(o_ref.dtype)
            return
        acc_ref, = acc
        k = pl.program_id(2)

        @pl.when(k == 0)
        def _():
            acc_ref[...] = jnp.zeros_like(acc_ref)

        acc_ref[...] += prod

        @pl.when(k == nk - 1)
        def _():
            o_ref[...] = acc_ref[...].astype(o_ref.dtype)

    a_spec = (pl.BlockSpec((tk, tm), lambda i, j, k: (k, i)) if ta
              else pl.BlockSpec((tm, tk), lambda i, j, k: (i, k)))
    b_spec = (pl.BlockSpec((tn, tk), lambda i, j, k: (j, k)) if tb
              else pl.BlockSpec((tk, tn), lambda i, j, k: (k, j)))
    if shards == 1:
        out_shape = jax.ShapeDtypeStruct((M, N), out_dtype)
        o_spec = pl.BlockSpec((tm, tn), lambda i, j, k: (i, j))
    else:
        nb = ns // tn
        out_shape = jax.ShapeDtypeStruct((shards, M, ns), out_dtype)
        o_spec = pl.BlockSpec((None, tm, tn), lambda i, j, k: (j // nb, i, j % nb))
    return pl.pallas_call(
        body, name=name, out_shape=out_shape,
        grid=(M // tm, N // tn, nk),
        in_specs=[a_spec, b_spec], out_specs=o_spec,
        scratch_shapes=[] if nk == 1 else [pltpu.VMEM((tm, tn), F32)],
        compiler_params=_params("parallel", "parallel", "arbitrary"),
    )(a, b)


def _rmsnorm_fwd(x, g, *, name, out_dtype=F32, tr=256):
    L, D = x.shape
    tr = min(tr, L)

    def body(x_ref, g_ref, o_ref):
        xf = x_ref[...]
        r = lax.rsqrt(jnp.mean(xf * xf, axis=-1, keepdims=True) + EPS)
        o_ref[...] = (xf * r * g_ref[...]).astype(o_ref.dtype)

    row = pl.BlockSpec((tr, D), lambda i: (i, 0))
    vec = pl.BlockSpec((1, D), lambda i: (0, 0))
    return pl.pallas_call(
        body, name=name, out_shape=jax.ShapeDtypeStruct((L, D), out_dtype),
        grid=(L // tr,), in_specs=[row, vec], out_specs=row,
        compiler_params=_params("parallel"),
    )(x, g.reshape(1, D))


def _post_norm_and_next_norms(o, g_post, res, g_kv, g_pre, *, name, tr=256):
    L, D = o.shape
    tr = min(tr, L)

    def body(o_ref, gp_ref, r_ref, gk_ref, gn_ref, h_ref, kv_ref, hn_ref):
        of = o_ref[...]
        r = lax.rsqrt(jnp.mean(of * of, axis=-1, keepdims=True) + EPS)
        h = r_ref[...] + of * r * gp_ref[...]
        h_ref[...] = h
        hr = h * lax.rsqrt(jnp.mean(h * h, axis=-1, keepdims=True) + EPS)
        kv_ref[...] = (hr * gk_ref[...]).astype(kv_ref.dtype)
        hn_ref[...] = (hr * gn_ref[...]).astype(hn_ref.dtype)

    row = pl.BlockSpec((tr, D), lambda i: (i, 0))
    vec = pl.BlockSpec((1, D), lambda i: (0, 0))
    return pl.pallas_call(
        body, name=name,
        out_shape=(jax.ShapeDtypeStruct((L, D), F32), jax.ShapeDtypeStruct((L, D), BF16),
                   jax.ShapeDtypeStruct((L, D), BF16)),
        grid=(L // tr,), in_specs=[row, vec, row, vec, vec], out_specs=(row, row, row),
        compiler_params=_params("parallel"),
    )(o, g_post.reshape(1, D), res, g_kv.reshape(1, D), g_pre.reshape(1, D))


def _rmsnorm_bwd(x, g, dy, *, name, adds=(), dx_dtype=F32, tr=256):
    L, D = x.shape
    tr = min(tr, L)
    dys = dy if isinstance(dy, tuple) else (dy,)
    n_dy, n_add = len(dys), len(adds)

    def body(*refs):
        x_ref, g_ref = refs[:2]
        dy_refs = refs[2:2 + n_dy]
        add_refs = refs[2 + n_dy:2 + n_dy + n_add]
        dx_ref, dg_ref = refs[2 + n_dy + n_add:]
        xf = x_ref[...]
        dyf = dy_refs[0][...].astype(F32)
        for d_ref in dy_refs[1:]:
            dyf = dyf + d_ref[...].astype(F32)
        r = lax.rsqrt(jnp.mean(xf * xf, axis=-1, keepdims=True) + EPS)
        gy = dyf * g_ref[...]
        c = jnp.mean(xf * gy, axis=-1, keepdims=True) * (r * r * r)
        dx = gy * r - xf * c
        for a_ref in add_refs:
            dx = dx + a_ref[...].astype(F32)
        dx_ref[...] = dx.astype(dx_ref.dtype)

        @pl.when(pl.program_id(0) == 0)
        def _():
            dg_ref[...] = jnp.zeros_like(dg_ref)

        dg_ref[...] += jnp.sum(dyf * xf * r, axis=0, keepdims=True)

    row = pl.BlockSpec((tr, D), lambda i: (i, 0))
    vec = pl.BlockSpec((1, D), lambda i: (0, 0))
    dx, dg = pl.pallas_call(
        body, name=name,
        out_shape=(jax.ShapeDtypeStruct((L, D), dx_dtype), jax.ShapeDtypeStruct((1, D), F32)),
        grid=(L // tr,), in_specs=[row, vec] + [row] * (n_dy + n_add), out_specs=(row, vec),
        compiler_params=_params("arbitrary"),
    )(x, g.reshape(1, D), *dys, *adds)
    return dx, dg.reshape(D)


def _rmsnorm_bwd_pair(x, g1, dy1, g2, dy2, *, name, adds=(), tr=256):
    L, D = x.shape
    tr = min(tr, L)
    dy1s = dy1 if isinstance(dy1, tuple) else (dy1,)
    n1, n_add = len(dy1s), len(adds)

    def body(*refs):
        x_ref, g1_ref, g2_ref = refs[:3]
        dy1_refs = refs[3:3 + n1]
        dy2_ref = refs[3 + n1]
        add_refs = refs[4 + n1:4 + n1 + n_add]
        dx_ref, dg1_ref, dg2_ref = refs[4 + n1 + n_add:]
        xf = x_ref[...]
        d1 = dy1_refs[0][...].astype(F32)
        for d_ref in dy1_refs[1:]:
            d1 = d1 + d_ref[...].astype(F32)
        d2 = dy2_ref[...].astype(F32)
        r = lax.rsqrt(jnp.mean(xf * xf, axis=-1, keepdims=True) + EPS)
        gy = d1 * g1_ref[...] + d2 * g2_ref[...]
        c = jnp.mean(xf * gy, axis=-1, keepdims=True) * (r * r * r)
        dx = gy * r - xf * c
        for a_ref in add_refs:
            dx = dx + a_ref[...].astype(F32)
        dx_ref[...] = dx

        @pl.when(pl.program_id(0) == 0)
        def _():
            dg1_ref[...] = jnp.zeros_like(dg1_ref)
            dg2_ref[...] = jnp.zeros_like(dg2_ref)

        xr = xf * r
        dg1_ref[...] += jnp.sum(d1 * xr, axis=0, keepdims=True)
        dg2_ref[...] += jnp.sum(d2 * xr, axis=0, keepdims=True)

    row = pl.BlockSpec((tr, D), lambda i: (i, 0))
    vec = pl.BlockSpec((1, D), lambda i: (0, 0))
    dx, dg1, dg2 = pl.pallas_call(
        body, name=name,
        out_shape=(jax.ShapeDtypeStruct((L, D), F32), jax.ShapeDtypeStruct((1, D), F32),
                   jax.ShapeDtypeStruct((1, D), F32)),
        grid=(L // tr,), in_specs=[row, vec, vec] + [row] * (n1 + 1 + n_add), out_specs=(row, vec, vec),
        compiler_params=_params("arbitrary"),
    )(x, g1.reshape(1, D), g2.reshape(1, D), *dy1s, dy2, *adds)
    return dx, dg1.reshape(D), dg2.reshape(D)


def _final_norm_loss(o, g, res, target, *, tr=256):
    L, D = o.shape
    tr = min(tr, L)

    def body(o_ref, g_ref, r_ref, t_ref, dh_ref, loss_ref):
        xf = o_ref[...]
        r = lax.rsqrt(jnp.mean(xf * xf, axis=-1, keepdims=True) + EPS)
        e = (r_ref[...] + xf * r * g_ref[...]) - t_ref[...]
        dh_ref[...] = e * (1.0 / D)

        @pl.when(pl.program_id(0) == 0)
        def _():
            loss_ref[...] = jnp.zeros_like(loss_ref)

        loss_ref[...] += jnp.sum(e * e, axis=0, keepdims=True) * (0.5 / D)

    row = pl.BlockSpec((tr, D), lambda i: (i, 0))
    vec = pl.BlockSpec((1, D), lambda i: (0, 0))
    dh, lp = pl.pallas_call(
        body, name="post_norm_1_loss",
        out_shape=(jax.ShapeDtypeStruct((L, D), F32), jax.ShapeDtypeStruct((1, D), F32)),
        grid=(L // tr,), in_specs=[row, vec, row, row], out_specs=(row, vec),
        compiler_params=_params("arbitrary"),
    )(o, g.reshape(1, D), res, target)
    return dh, lp


def _s5_coeffs(lr, li, ls):
    dt = jnp.exp(ls)
    mag = jnp.exp(lr * dt)
    ar = mag * jnp.cos(li * dt)
    ai = mag * jnp.sin(li * dt)
    den = lr * lr + li * li
    cr = ((ar - 1.0) * lr + ai * li) / den
    ci = (ai * lr - (ar - 1.0) * li) / den
    return dt, ar, ai, den, cr, ci


def _s5_prep(lam_re, lam_im, log_step, b_re_t, b_im_t):
    G, P = lam_re.shape
    H = b_re_t.shape[1]

    def body(lr_ref, li_ref, ls_ref, br_ref, bi_ref, ar_ref, ai_ref, bbr_ref, bbi_ref):
        _, ar, ai, _, cr, ci = _s5_coeffs(lr_ref[...], li_ref[...], ls_ref[...])
        ar_ref[...] = ar
        ai_ref[...] = ai
        br, bi = br_ref[...], bi_ref[...]
        crb, cib = cr[:, None, :], ci[:, None, :]
        bbr_ref[...] = crb * br - cib * bi
        bbi_ref[...] = crb * bi + cib * br

    return pl.pallas_call(
        body, name="s5_prep",
        out_shape=(jax.ShapeDtypeStruct((G, P), F32), jax.ShapeDtypeStruct((G, P), F32),
                   jax.ShapeDtypeStruct((G, H, P), F32), jax.ShapeDtypeStruct((G, H, P), F32)),
        compiler_params=_params(),
    )(lam_re, lam_im, log_step.reshape(G, 1), b_re_t, b_im_t)


def _s5_prep_bwd(lam_re, lam_im, log_step, b_re_t, b_im_t, d_ar, d_ai, d_bbr, d_bbi):
    G, P = lam_re.shape
    H = b_re_t.shape[1]

    def body(lr_ref, li_ref, ls_ref, br_ref, bi_ref, dar_ref, dai_ref, dbbr_ref, dbbi_ref,
             dlr_ref, dli_ref, dls_ref, dbr_ref, dbi_ref):
        lr, li = lr_ref[...], li_ref[...]
        dt, ar, ai, den, cr, ci = _s5_coeffs(lr, li, ls_ref[...])
        br, bi = br_ref[...], bi_ref[...]
        gbr, gbi = dbbr_ref[...], dbbi_ref[...]
        crb, cib = cr[:, None, :], ci[:, None, :]
        dbr_ref[...] = crb * gbr + cib * gbi
        dbi_ref[...] = crb * gbi - cib * gbr
        gcr = jnp.sum(br * gbr + bi * gbi, axis=1)
        gci = jnp.sum(br * gbi - bi * gbr, axis=1)
        ilr, ili = lr / den, -li / den
        gar = dar_ref[...] + (ilr * gcr + ili * gci)
        gai = dai_ref[...] + (ilr * gci - ili * gcr)
        qr, qi = cr * ilr - ci * ili, cr * ili + ci * ilr
        glr = -(qr * gcr + qi * gci)
        gli = -(qr * gci - qi * gcr)
        glr = glr + dt * (ar * gar + ai * gai)
        gli = gli + dt * (ar * gai - ai * gar)
        wr, wi = lr * ar - li * ai, lr * ai + li * ar
        gdt = jnp.sum(wr * gar + wi * gai, axis=1, keepdims=True)
        dlr_ref[...] = glr
        dli_ref[...] = gli
        dls_ref[...] = gdt * dt

    return pl.pallas_call(
        body, name="s5_prep_bwd",
        out_shape=(jax.ShapeDtypeStruct((G, P), F32), jax.ShapeDtypeStruct((G, P), F32),
                   jax.ShapeDtypeStruct((G, 1), F32),
                   jax.ShapeDtypeStruct((G, H, P), F32), jax.ShapeDtypeStruct((G, H, P), F32)),
        compiler_params=_params(),
    )(lam_re, lam_im, log_step.reshape(G, 1), b_re_t, b_im_t, d_ar, d_ai, d_bbr, d_bbi)


def _s5_block_mats(bbr_t, bbi_t, c_re, c_im):
    bmat = _s5_expand(bbr_t, bbi_t)
    cmat = jnp.transpose(_s5_expand(c_re, -c_im), (0, 2, 1))
    return bmat.astype(BF16), cmat.astype(BF16)


def _s5_diag_mask():
    r = lax.broadcasted_iota(jnp.int32, (LANES, 2 * STATE_COLS), 0) // SSM_GROUP
    c = (lax.broadcasted_iota(jnp.int32, (LANES, 2 * STATE_COLS), 1) % STATE_COLS) // SSM_STATE
    return (r == c).astype(F32)


def _s5_expand(re, im):
    re = jnp.tile(re.reshape(SSM_BLOCKS, LANES, SSM_STATE), (1, 1, GROUPS_PER_BLOCK))
    im = jnp.tile(im.reshape(SSM_BLOCKS, LANES, SSM_STATE), (1, 1, GROUPS_PER_BLOCK))
    return jnp.concatenate([re, im], axis=-1) * _s5_diag_mask()[None]


def _s5_unfold(dmat):
    d = dmat.reshape(SSM_GROUPS, SSM_GROUP, 2, SSM_STATE)
    return jnp.transpose(d, (2, 0, 1, 3))


def _s5_a_rows(ar, ai):
    a = jnp.concatenate([ar.reshape(SSM_BLOCKS, STATE_COLS), ai.reshape(SSM_BLOCKS, STATE_COLS)], axis=1)
    return jnp.broadcast_to(a[:, None, :], (SSM_BLOCKS, SUBLANES, 2 * STATE_COLS))


def _to_step_major(src_ref, dst_ref, seg):
    for s in range(SUBLANES):
        dst_ref[pl.ds(s, seg, stride=SUBLANES), :] = src_ref[pl.ds(seg * s, seg), :]


def _segment_rows(ref, s, seg):
    return ref[pl.ds(s, seg, stride=SUBLANES), :]


def _cmul(ar, ai, xr, xi):
    return ar * xr - ai * xi, ar * xi + ai * xr


def _s5_tables(a_ref, pw_s, pwr_s, S, seg):
    ar, ai = a_ref[:, :S], a_ref[:, S:]

    def step(i, c):
        pr, pi = c
        pw_s[i, :, :S] = pr
        pw_s[i, :, S:] = pi
        nr, ni = _cmul(ar, ai, pr, pi)
        pwr_s[seg - 1 - i, :, :S] = nr
        pwr_s[seg - 1 - i, :, S:] = ni
        return nr, ni

    pr, pi = lax.fori_loop(0, seg, step, (jnp.ones_like(ar), jnp.zeros_like(ai)))
    pw_s[seg, :, :S] = pr
    pw_s[seg, :, S:] = pi


def _s5_fwd(proj, bmat, cmat, a_rows, d_skip, *, tc=512):
    L = proj.shape[0]
    tc = min(tc, L)
    nt = L // tc
    seg = tc // SUBLANES
    S = STATE_COLS

    def body(u_ref, b_ref, c_ref, a_ref, d_ref, y_ref, yg_ref, xp_ref,
             bu_s, xp_s, pw_s, pwr_s, carry_s, e_s, up_s, yc_s):
        @pl.when(pl.program_id(1) == 0)
        def _():
            carry_s[...] = jnp.zeros_like(carry_s)
            _s5_tables(a_ref, pw_s, pwr_s, S, seg)

        ar, ai = a_ref[:, :S], a_ref[:, S:]
        _to_step_major(u_ref, up_s, seg)
        bu = jnp.dot(up_s[...].astype(BF16), b_ref[...], preferred_element_type=F32)
        bu_s[...] = bu.reshape(seg, SUBLANES, 2 * S)

        def step(i, carry):
            cr, ci = carry
            xp_s[i, :, :S] = cr
            xp_s[i, :, S:] = ci
            return ar * cr - ai * ci + bu_s[i, :, :S], ar * ci + ai * cr + bu_s[i, :, S:]

        zero = jnp.zeros((SUBLANES, S), F32)
        fr, fi = lax.fori_loop(0, seg, step, (zero, zero))
        pr, pi = pw_s[seg, 0:1, :S], pw_s[seg, 0:1, S:]
        er, ei = carry_s[0:1, :S], carry_s[0:1, S:]
        for s in range(SUBLANES):
            e_s[s:s + 1, :S] = er
            e_s[s:s + 1, S:] = ei
            tr, ti = _cmul(pr, pi, er, ei)
            er, ei = fr[s:s + 1] + tr, fi[s:s + 1] + ti
        carry_s[0:1, :S] = er
        carry_s[0:1, S:] = ei
        pw = pw_s[0:seg]
        tr, ti = _cmul(pw[:, :, :S], pw[:, :, S:], e_s[:, :S][None], e_s[:, S:][None])
        xl = xp_s[...]
        xp = jnp.concatenate([xl[:, :, :S] + tr, xl[:, :, S:] + ti], axis=-1).reshape(tc, 2 * S)
        xp_ref[...] = xp
        a1r, a1i = ar[0:1], ai[0:1]
        x_re = a1r * xp[:, :S] - a1i * xp[:, S:] + bu[:, :S]
        x_im = a1r * xp[:, S:] + a1i * xp[:, :S] + bu[:, S:]
        xs = jnp.concatenate([x_re, x_im], axis=1).astype(BF16)
        yc_s[...] = jnp.dot(xs, c_ref[...], preferred_element_type=F32)
        for s in range(SUBLANES):
            rows = pl.ds(seg * s, seg)
            y = _segment_rows(yc_s, s, seg) + d_ref[...] * u_ref[rows, :]
            y_ref[rows, :] = y
            yg_ref[rows, :] = _gelu(y).astype(BF16)

    return pl.pallas_call(
        body, name="s5_fwd",
        out_shape=(jax.ShapeDtypeStruct((L, MAIN_WIDTH), F32),
                   jax.ShapeDtypeStruct((L, MAIN_WIDTH), BF16),
                   jax.ShapeDtypeStruct((L, SSM_BLOCKS * 2 * S), F32)),
        grid=(SSM_BLOCKS, nt),
        in_specs=[pl.BlockSpec((tc, LANES), lambda b, t: (t, b)),
                  pl.BlockSpec((None, LANES, 2 * S), lambda b, t: (b, 0, 0)),
                  pl.BlockSpec((None, 2 * S, LANES), lambda b, t: (b, 0, 0)),
                  pl.BlockSpec((None, SUBLANES, 2 * S), lambda b, t: (b, 0, 0)),
                  pl.BlockSpec((1, LANES), lambda b, t: (0, b))],
        out_specs=(pl.BlockSpec((tc, LANES), lambda b, t: (t, b)),
                   pl.BlockSpec((tc, LANES), lambda b, t: (t, b)),
                   pl.BlockSpec((tc, 2 * S), lambda b, t: (t, b))),
        scratch_shapes=[pltpu.VMEM((seg, SUBLANES, 2 * S), F32),
                        pltpu.VMEM((seg, SUBLANES, 2 * S), F32),
                        pltpu.VMEM((seg + 1, SUBLANES, 2 * S), F32),
                        pltpu.VMEM((seg, SUBLANES, 2 * S), F32),
                        pltpu.VMEM((SUBLANES, 2 * S), F32),
                        pltpu.VMEM((SUBLANES, 2 * S), F32),
                        pltpu.VMEM((tc, LANES), F32),
                        pltpu.VMEM((tc, LANES), F32)],
        compiler_params=_params("parallel", "arbitrary"),
    )(proj, bmat, cmat, a_rows, d_skip.reshape(1, MAIN_WIDTH))


def _s5_bwd(proj, dyg_a, dyg_b, y, xp, bmat, cmat, a_rows, d_skip, dproj, *, tc=512):
    L = proj.shape[0]
    tc = min(tc, L)
    nt = L // tc
    seg = tc // SUBLANES
    S = STATE_COLS
    nn = (((1,), (1,)), ((), ()))
    tn = (((0,), (0,)), ((), ()))

    def fold_diagonal(acc_ref, mask_ref, fold_ref):
        x = acc_ref[...] * mask_ref[...]
        hi = x.astype(BF16)
        rest = x - hi.astype(F32)
        mid = rest.astype(BF16)
        low = (rest - mid.astype(F32)).astype(BF16)
        return sum(jnp.dot(piece, fold_ref[...], preferred_element_type=F32) for piece in (hi, mid, low))

    def body(u_ref, dyga_ref, dygb_ref, y_ref, xp_ref, b_ref, c_ref, a_ref, d_ref, mask_ref, fold_ref, dp_hbm,
             du_ref, dbd_ref, dcd_ref, da_ref, dd_ref,
             dl_s, pw_s, pwr_s, carry_s, e_s, up_s, dy_s, dyp_s, dup_s, db_ref, dc_ref):
        @pl.when(pl.program_id(1) == 0)
        def _():
            carry_s[...] = jnp.zeros_like(carry_s)
            db_ref[...] = jnp.zeros_like(db_ref)
            dc_ref[...] = jnp.zeros_like(dc_ref)
            da_ref[...] = jnp.zeros_like(da_ref)
            dd_ref[...] = jnp.zeros_like(dd_ref)
            _s5_tables(a_ref, pw_s, pwr_s, S, seg)

        ar, ai = a_ref[:, :S], a_ref[:, S:]
        a1r, a1i = ar[0:1], ai[0:1]
        u = u_ref[...]
        dy = (dyga_ref[...] + dygb_ref[...]) * _gelu_grad(y_ref[...])
        dy_s[...] = dy
        xp = xp_ref[...]
        _to_step_major(u_ref, up_s, seg)
        _to_step_major(dy_s, dyp_s, seg)
        ubp = up_s[...].astype(BF16)
        dyp = dyp_s[...].astype(BF16)
        bu = jnp.dot(ubp, b_ref[...], preferred_element_type=F32)
        x_re = a1r * xp[:, :S] - a1i * xp[:, S:] + bu[:, :S]
        x_im = a1r * xp[:, S:] + a1i * xp[:, :S] + bu[:, S:]
        xs = jnp.concatenate([x_re, x_im], axis=1).astype(BF16)
        dc_ref[...] += lax.dot_general(dyp, xs, tn, preferred_element_type=F32)
        dx = lax.dot_general(dyp, c_ref[...], nn, preferred_element_type=F32)
        dl_s[...] = dx.reshape(seg, SUBLANES, 2 * S)

        def step(k, carry):
            cr, ci = carry
            i = seg - 1 - k
            lr = dl_s[i, :, :S] + (ar * cr + ai * ci)
            li = dl_s[i, :, S:] + (ar * ci - ai * cr)
            dl_s[i, :, :S] = lr
            dl_s[i, :, S:] = li
            return lr, li

        zero = jnp.zeros((SUBLANES, S), F32)
        fr, fi = lax.fori_loop(0, seg, step, (zero, zero))
        pr, pi = pw_s[seg, 0:1, :S], pw_s[seg, 0:1, S:]
        er, ei = carry_s[0:1, :S], carry_s[0:1, S:]
        for s in range(SUBLANES - 1, -1, -1):
            e_s[s:s + 1, :S] = er
            e_s[s:s + 1, S:] = ei
            er, ei = fr[s:s + 1] + (pr * er + pi * ei), fi[s:s + 1] + (pr * ei - pi * er)
        carry_s[0:1, :S] = er
        carry_s[0:1, S:] = ei
        er, ei = e_s[:, :S][None], e_s[:, S:][None]
        pw = pwr_s[...]
        pwr, pwi = pw[:, :, :S], pw[:, :, S:]
        ll = dl_s[...]
        lam = jnp.concatenate([ll[:, :, :S] + (pwr * er + pwi * ei), ll[:, :, S:] + (pwr * ei - pwi * er)],
                              axis=-1).reshape(tc, 2 * S)
        l_re, l_im = lam[:, :S], lam[:, S:]
        da_ref[0:1, :S] += jnp.sum(l_re * xp[:, :S] + l_im * xp[:, S:], axis=0, keepdims=True)
        da_ref[0:1, S:] += jnp.sum(l_im * xp[:, :S] - l_re * xp[:, S:], axis=0, keepdims=True)
        lamb = lam.astype(BF16)
        dup_s[...] = lax.dot_general(lamb, b_ref[...], nn, preferred_element_type=F32)
        for s in range(SUBLANES):
            rows = pl.ds(seg * s, seg)
            du = _segment_rows(dup_s, s, seg) + d_ref[...] * dy_s[rows, :]
            du_ref[rows, :] = du.astype(du_ref.dtype)
        db_ref[...] += lax.dot_general(ubp, lamb, tn, preferred_element_type=F32)
        dd_ref[0:1, :] += jnp.sum(dy * u, axis=0, keepdims=True)

        @pl.when(pl.program_id(1) == nt - 1)
        def _():
            dbd_ref[...] = fold_diagonal(db_ref, mask_ref, fold_ref)
            dcd_ref[...] = fold_diagonal(dc_ref, mask_ref, fold_ref)

    rev = lambda b, t: (nt - 1 - t, b)
    col = jnp.arange(2 * S)
    fold = ((col // S * SSM_STATE + col % SSM_STATE)[:, None] == jnp.arange(LANES)[None, :]).astype(BF16)
    return pl.pallas_call(
        body, name="s5_bwd",
        out_shape=(jax.ShapeDtypeStruct(dproj.shape, dproj.dtype),
                   jax.ShapeDtypeStruct((SSM_BLOCKS, LANES, LANES), F32),
                   jax.ShapeDtypeStruct((SSM_BLOCKS, LANES, LANES), F32),
                   jax.ShapeDtypeStruct((SSM_BLOCKS, SUBLANES, 2 * S), F32),
                   jax.ShapeDtypeStruct((SUBLANES, MAIN_WIDTH), F32)),
        input_output_aliases={11: 0},
        grid=(SSM_BLOCKS, nt),
        in_specs=[pl.BlockSpec((tc, LANES), rev),
                  pl.BlockSpec((tc, LANES), rev),
                  pl.BlockSpec((tc, LANES), rev),
                  pl.BlockSpec((tc, LANES), rev),
                  pl.BlockSpec((tc, 2 * S), rev),
                  pl.BlockSpec((None, LANES, 2 * S), lambda b, t: (b, 0, 0)),
                  pl.BlockSpec((None, 2 * S, LANES), lambda b, t: (b, 0, 0)),
                  pl.BlockSpec((None, SUBLANES, 2 * S), lambda b, t: (b, 0, 0)),
                  pl.BlockSpec((1, LANES), lambda b, t: (0, b)),
                  pl.BlockSpec((LANES, 2 * S), lambda b, t: (0, 0)),
                  pl.BlockSpec((2 * S, LANES), lambda b, t: (0, 0)),
                  _ANY],
        out_specs=(pl.BlockSpec((tc, LANES), rev),
                   pl.BlockSpec((None, LANES, LANES), lambda b, t: (b, 0, 0)),
                   pl.BlockSpec((None, LANES, LANES), lambda b, t: (b, 0, 0)),
                   pl.BlockSpec((None, SUBLANES, 2 * S), lambda b, t: (b, 0, 0)),
                   pl.BlockSpec((SUBLANES, LANES), lambda b, t: (0, b))),
        scratch_shapes=[pltpu.VMEM((seg, SUBLANES, 2 * S), F32),
                        pltpu.VMEM((seg + 1, SUBLANES, 2 * S), F32),
                        pltpu.VMEM((seg, SUBLANES, 2 * S), F32),
                        pltpu.VMEM((SUBLANES, 2 * S), F32),
                        pltpu.VMEM((SUBLANES, 2 * S), F32),
                        pltpu.VMEM((tc, LANES), F32),
                        pltpu.VMEM((tc, LANES), F32),
                        pltpu.VMEM((tc, LANES), F32),
                        pltpu.VMEM((tc, LANES), F32),
                        pltpu.VMEM((LANES, 2 * S), F32),
                        pltpu.VMEM((LANES, 2 * S), F32)],
        compiler_params=_params("parallel", "arbitrary"),
    )(proj, dyg_a, dyg_b, y, xp, bmat, cmat, a_rows, d_skip.reshape(1, MAIN_WIDTH), _s5_diag_mask(), fold, dproj)


_Z_COLS = slice(MAIN_WIDTH, 2 * MAIN_WIDTH)
_ZM_COLS = slice(2 * MAIN_WIDTH + MEM_WIDTH, IN_WIDTH)


def _proj_rows(tr):
    return pl.BlockSpec((tr, IN_WIDTH), lambda i: (i, 0))


def _row_specs(tr):
    main = pl.BlockSpec((tr, MAIN_WIDTH), lambda i: (i, 0))
    z = pl.BlockSpec((tr, MAIN_WIDTH), lambda i: (i, 1))
    zm = pl.BlockSpec((tr, MEM_WIDTH), lambda i: (i, IN_WIDTH // MEM_WIDTH - 1))
    mem = pl.BlockSpec((tr, MEM_WIDTH), lambda i: (i, 0))
    cat = pl.BlockSpec((tr, D_MODEL), lambda i: (i, 0))
    vec = pl.BlockSpec((1, MAIN_WIDTH), lambda i: (0, 0))
    return main, z, zm, mem, cat, vec


def _gate_a_fwd(y, t, b_glu, proj, o_mem, *, tr=256):
    L = y.shape[0]
    tr = min(tr, L)

    def body(y_ref, t_ref, b_ref, z_ref, zm_ref, om_ref, o_ref):
        yg = _gelu(y_ref[...])
        sz, _ = _silu_and_grad(z_ref[...])
        o_ref[:, :MAIN_WIDTH] = (yg * _sigmoid(t_ref[...] + b_ref[...]) * sz).astype(BF16)
        szm, _ = _silu_and_grad(zm_ref[...])
        o_ref[:, MAIN_WIDTH:] = (om_ref[...] * szm).astype(BF16)

    main, z, zm, mem, cat, vec = _row_specs(tr)
    return pl.pallas_call(
        body, name="gate_a_fwd", out_shape=jax.ShapeDtypeStruct((L, D_MODEL), BF16),
        grid=(L // tr,), in_specs=[main, main, vec, z, zm, mem], out_specs=cat,
        compiler_params=_params("parallel"),
    )(y, t, b_glu.reshape(1, MAIN_WIDTH), proj, proj, o_mem)


def _gate_a_bwd(dcat, y, t, b_glu, proj, o_mem, *, tr=256):
    L = y.shape[0]
    tr = min(tr, L)

    def body(dc_ref, y_ref, t_ref, b_ref, z_ref, zm_ref, om_ref,
             dp_ref, dt_ref, dyg_ref, dom_ref, db_ref):
        dmain = dc_ref[:, :MAIN_WIDTH]
        dmemo = dc_ref[:, MAIN_WIDTH:]
        yg = _gelu(y_ref[...])
        sg = _sigmoid(t_ref[...] + b_ref[...])
        sz, gz = _silu_and_grad(z_ref[...])
        dp_ref[:, _Z_COLS] = (dmain * (yg * sg) * gz).astype(BF16)
        dy2 = dmain * sz
        dyg_ref[...] = dy2 * sg
        dt = dy2 * yg * (sg * (1.0 - sg))
        dt_ref[...] = dt.astype(BF16)

        @pl.when(pl.program_id(0) == 0)
        def _():
            db_ref[...] = jnp.zeros_like(db_ref)

        db_ref[...] += jnp.sum(dt, axis=0, keepdims=True)
        szm, gzm = _silu_and_grad(zm_ref[...])
        dom_ref[...] = dmemo * szm
        dp_ref[:, _ZM_COLS] = (dmemo * om_ref[...] * gzm).astype(BF16)

    main, z, zm, mem, cat, vec = _row_specs(tr)
    outs = pl.pallas_call(
        body, name="gate_a_bwd",
        out_shape=(jax.ShapeDtypeStruct((L, IN_WIDTH), BF16),
                   jax.ShapeDtypeStruct((L, MAIN_WIDTH), BF16), jax.ShapeDtypeStruct((L, MAIN_WIDTH), F32),
                   jax.ShapeDtypeStruct((L, MEM_WIDTH), F32), jax.ShapeDtypeStruct((1, MAIN_WIDTH), F32)),
        grid=(L // tr,), in_specs=[cat, main, main, vec, z, zm, mem],
        out_specs=(_proj_rows(tr), main, main, mem, vec),
        compiler_params=_params("arbitrary"),
    )(dcat, y, t, b_glu.reshape(1, MAIN_WIDTH), proj, proj, o_mem)
    return outs


def _gate_b_fwd(att, proj, o_mem, *, tr=256):
    L = att.shape[0]
    tr = min(tr, L)

    def body(a_ref, z_ref, zm_ref, om_ref, o_ref):
        sz, _ = _silu_and_grad(z_ref[...])
        o_ref[:, :MAIN_WIDTH] = (a_ref[...] * sz).astype(BF16)
        szm, _ = _silu_and_grad(zm_ref[...])
        o_ref[:, MAIN_WIDTH:] = (om_ref[...] * szm).astype(BF16)

    main, z, zm, mem, cat, _ = _row_specs(tr)
    return pl.pallas_call(
        body, name="gate_b_fwd", out_shape=jax.ShapeDtypeStruct((L, D_MODEL), BF16),
        grid=(L // tr,), in_specs=[main, z, zm, mem], out_specs=cat,
        compiler_params=_params("parallel"),
    )(att, proj, proj, o_mem)


def _gate_b_bwd(dcat, att, proj, o_mem, *, tr=256):
    L = att.shape[0]
    tr = min(tr, L)

    def body(dc_ref, a_ref, z_ref, zm_ref, om_ref, da_ref, dp_ref, dom_ref, dl_ref):
        dmain = dc_ref[:, :MAIN_WIDTH]
        dmemo = dc_ref[:, MAIN_WIDTH:]
        att = a_ref[...]
        sz, gz = _silu_and_grad(z_ref[...])
        datt = dmain * sz
        da_ref[...] = datt
        dp_ref[:, _Z_COLS] = (dmain * att * gz).astype(BF16)
        szm, gzm = _silu_and_grad(zm_ref[...])
        dom_ref[...] = dmemo * szm
        dp_ref[:, _ZM_COLS] = (dmemo * om_ref[...] * gzm).astype(BF16)
        prod = datt * att
        for h in range(FOX_HEADS):
            dl_ref[h] = jnp.sum(prod[:, h * HEAD_DIM:(h + 1) * HEAD_DIM], axis=1, keepdims=True)

    main, z, zm, mem, cat, _ = _row_specs(tr)
    delta = pl.BlockSpec((FOX_HEADS, tr, 1), lambda i: (0, i, 0))
    return pl.pallas_call(
        body, name="gate_b_bwd",
        out_shape=(jax.ShapeDtypeStruct((L, MAIN_WIDTH), F32), jax.ShapeDtypeStruct((L, IN_WIDTH), BF16),
                   jax.ShapeDtypeStruct((L, MEM_WIDTH), F32), jax.ShapeDtypeStruct((FOX_HEADS, L, 1), F32)),
        grid=(L // tr,), in_specs=[cat, main, z, zm, mem], out_specs=(main, _proj_rows(tr), mem, delta),
        compiler_params=_params("parallel"),
    )(dcat, att, proj, proj, o_mem)


_MEM_Q_COL = (2 * MAIN_WIDTH) // HEAD_DIM
_NT = (((1,), (1,)), ((), ()))
_TN = (((0,), (0,)), ((), ()))


def _mem_probs(q_ref, k_ref):
    qs = (q_ref[...] * (HEAD_DIM ** -0.5)).astype(BF16)
    s = lax.dot_general(qs, k_ref[...].astype(BF16), _NT, preferred_element_type=F32)
    e = jnp.exp(s - jnp.max(s, axis=-1, keepdims=True))
    return qs, e / jnp.sum(e, axis=-1, keepdims=True)


def _mem_attn_fwd(proj, kvm, *, tq=2048):
    L = proj.shape[0]
    tq = min(tq, L)

    def body(q_ref, k_ref, v_ref, o_ref):
        _, p = _mem_probs(q_ref, k_ref)
        o_ref[...] = jnp.dot(p.astype(BF16), v_ref[...].astype(BF16), preferred_element_type=F32)

    return pl.pallas_call(
        body, name="mem_attn_fwd", out_shape=jax.ShapeDtypeStruct((L, MEM_WIDTH), F32),
        grid=(MEM_HEADS, L // tq),
        in_specs=[pl.BlockSpec((tq, HEAD_DIM), lambda h, i: (i, _MEM_Q_COL + h)),
                  pl.BlockSpec((N_MEM, HEAD_DIM), lambda h, i: (0, h)),
                  pl.BlockSpec((N_MEM, HEAD_DIM), lambda h, i: (0, MEM_HEADS + h))],
        out_specs=pl.BlockSpec((tq, HEAD_DIM), lambda h, i: (i, h)),
        compiler_params=_params("parallel", "parallel"),
    )(proj, kvm, kvm)


def _mem_attn_bwd(proj, kvm, do, dproj, *, tq=2048):
    L = proj.shape[0]
    tq = min(tq, L)

    def body(q_ref, k_ref, v_ref, do_ref, dp_hbm, dq_ref, dk_ref, dv_ref):
        @pl.when(pl.program_id(1) == 0)
        def _():
            dk_ref[...] = jnp.zeros_like(dk_ref)
            dv_ref[...] = jnp.zeros_like(dv_ref)

        qs, p = _mem_probs(q_ref, k_ref)
        dob = do_ref[...].astype(BF16)
        dp = lax.dot_general(dob, v_ref[...].astype(BF16), _NT, preferred_element_type=F32)
        ds = p * (dp - jnp.sum(p * dp, axis=-1, keepdims=True))
        dsb = ds.astype(BF16)
        dq = jnp.dot(dsb, k_ref[...].astype(BF16), preferred_element_type=F32) * (HEAD_DIM ** -0.5)
        dq_ref[...] = dq.astype(BF16)
        dk_ref[...] += lax.dot_general(dsb, qs, _TN, preferred_element_type=F32)
        dv_ref[...] += lax.dot_general(p.astype(BF16), dob, _TN, preferred_element_type=F32)

    dproj, dk, dv = pl.pallas_call(
        body, name="mem_attn_bwd",
        out_shape=(jax.ShapeDtypeStruct(dproj.shape, dproj.dtype),
                   jax.ShapeDtypeStruct((N_MEM, MEM_WIDTH), F32),
                   jax.ShapeDtypeStruct((N_MEM, MEM_WIDTH), F32)),
        grid=(MEM_HEADS, L // tq),
        in_specs=[pl.BlockSpec((tq, HEAD_DIM), lambda h, i: (i, _MEM_Q_COL + h)),
                  pl.BlockSpec((N_MEM, HEAD_DIM), lambda h, i: (0, h)),
                  pl.BlockSpec((N_MEM, HEAD_DIM), lambda h, i: (0, MEM_HEADS + h)),
                  pl.BlockSpec((tq, HEAD_DIM), lambda h, i: (i, h)),
                  _ANY],
        out_specs=(pl.BlockSpec((tq, HEAD_DIM), lambda h, i: (i, _MEM_Q_COL + h)),
                   pl.BlockSpec((N_MEM, HEAD_DIM), lambda h, i: (0, h)),
                   pl.BlockSpec((N_MEM, HEAD_DIM), lambda h, i: (0, h))),
        input_output_aliases={4: 0},
        compiler_params=_params("parallel", "arbitrary"),
    )(proj, kvm, kvm, do, dproj)
    return dproj, jnp.concatenate([dk, dv], axis=1)


def _tile_cumsum(x, row, reverse):
    for sh in (1, 2, 4):
        if reverse:
            x = x + jnp.where(row < SUBLANES - sh, pltpu.roll(x, SUBLANES - sh, 0), 0.0)
        else:
            x = x + jnp.where(row >= sh, pltpu.roll(x, sh, 0), 0.0)
    return x


def _fgate_fwd(pre, b_pad):
    L = pre.shape[0]
    n8 = L // SUBLANES

    def body(p_ref, b_ref, o_ref):
        row = lax.broadcasted_iota(jnp.int32, (SUBLANES, LANES), 0)
        b = b_ref[...]

        def step(i, carry):
            x = p_ref[i] + b
            logf = jnp.minimum(x, 0.0) - jnp.log(1.0 + jnp.exp(-jnp.abs(x)))
            t = _tile_cumsum(logf, row, False) + carry
            o_ref[i] = t
            return t[SUBLANES - 1:SUBLANES, :]

        lax.fori_loop(0, n8, step, jnp.zeros((1, LANES), F32))

    out = pl.pallas_call(
        body, name="fgate_fwd", out_shape=jax.ShapeDtypeStruct((n8, SUBLANES, LANES), F32),
        compiler_params=_params(),
    )(pre.reshape(n8, SUBLANES, LANES), b_pad.reshape(1, LANES))
    return out.reshape(L, LANES)


def _fgate_bwd(dfcum, pre, b_pad):
    L = pre.shape[0]
    n8 = L // SUBLANES

    def body(d_ref, p_ref, b_ref, o_ref, s_ref):
        row = lax.broadcasted_iota(jnp.int32, (SUBLANES, LANES), 0)
        b = b_ref[...]

        def step(k, carry):
            c, acc = carry
            i = n8 - 1 - k
            t = _tile_cumsum(d_ref[i], row, True) + c
            dpre = t * _sigmoid(-(p_ref[i] + b))
            o_ref[i] = dpre
            return t[0:1, :], acc + dpre

        _, acc = lax.fori_loop(0, n8, step, (jnp.zeros((1, LANES), F32), jnp.zeros((SUBLANES, LANES), F32)))
        s_ref[...] = jnp.sum(acc, axis=0, keepdims=True)

    dpre, db = pl.pallas_call(
        body, name="fgate_bwd",
        out_shape=(jax.ShapeDtypeStruct((n8, SUBLANES, LANES), F32), jax.ShapeDtypeStruct((1, LANES), F32)),
        compiler_params=_params(),
    )(dfcum.reshape(n8, SUBLANES, LANES), pre.reshape(n8, SUBLANES, LANES), b_pad.reshape(1, LANES))
    return dpre.reshape(L, LANES), db


FOX_BLOCK = 1024


def _fox_scores(qs, k, fk, diagonal, row0=0):
    s = lax.dot_general(qs, k, _NT, preferred_element_type=F32) - fk
    if diagonal:
        row = row0 + lax.broadcasted_iota(jnp.int32, s.shape, 0)
        col = lax.broadcasted_iota(jnp.int32, s.shape, 1)
        s = jnp.where(row >= col, s, NEG_BIG)
    return s


def _fox_diagonal_parts(tq):
    half = tq // 2
    return ((slice(0, half), half), (slice(half, tq), tq))


def _fox_specs(tq, L):
    nq = L // tq
    return dict(
        rows=lambda off: pl.BlockSpec((tq, HEAD_DIM), lambda h, i: (i, off + h)),
        seq=lambda off: pl.BlockSpec((L, HEAD_DIM), lambda h, i: (0, off + h)),
        col=pl.BlockSpec((None, None, tq, 1), lambda h, i: (h, i, 0, 0)),
        col_all=pl.BlockSpec((None, nq, tq, 1), lambda h, i: (h, 0, 0, 0)),
        row=pl.BlockSpec((None, None, 1, tq), lambda h, i: (h, i, 0, 0)),
        row_all=pl.BlockSpec((None, nq, 1, tq), lambda h, i: (h, 0, 0, 0)))


FOX_FWD_HEADS = 2
FOX_FWD_BLOCK = 1024


def _fox_fwd(proj, kv, fk):
    L = proj.shape[0]
    tq = min(FOX_FWD_BLOCK, L)
    nq = L // tq
    nh = FOX_FWD_HEADS
    W = nh * HEAD_DIM
    lse_shape = fk.shape[:2] + (fk.shape[3], 1)
    fk = fk.reshape(FOX_HEADS, nq, 1, tq)

    def body(q_ref, k_ref, v_ref, fk_ref, o_ref, lse_ref, m_s, l_s, acc_s):
        qi = pl.program_id(1)
        cols = [slice(a * HEAD_DIM, (a + 1) * HEAD_DIM) for a in range(nh)]
        qs = [(q_ref[:, cs] * (HEAD_DIM ** -0.5)).astype(BF16) for cs in cols]
        m_s[...] = jnp.full_like(m_s, NEG_BIG)
        l_s[...] = jnp.zeros_like(l_s)
        acc_s[...] = jnp.zeros_like(acc_s)

        def block(j, diagonal):
            r0 = pl.multiple_of(j * tq, tq)
            for a, cs in enumerate(cols):
                s = _fox_scores(qs[a], k_ref[pl.ds(r0, tq), cs], fk_ref[a, j], diagonal)
                m_new = jnp.maximum(m_s[a], jnp.max(s, axis=-1, keepdims=True))
                alpha = jnp.exp(m_s[a] - m_new)
                p = jnp.exp(s - m_new)
                l_s[a] = alpha * l_s[a] + jnp.sum(p, axis=-1, keepdims=True)
                acc_s[a] = alpha * acc_s[a] + jnp.dot(p.astype(BF16), v_ref[pl.ds(r0, tq), cs],
                                                      preferred_element_type=F32)
                m_s[a] = m_new

        def below(j, carry):
            block(j, False)
            return carry

        lax.fori_loop(0, qi, below, 0)
        block(qi, True)
        for a, cs in enumerate(cols):
            o_ref[:, cs] = acc_s[a] / l_s[a]
            lse_ref[a] = m_s[a] + jnp.log(l_s[a])

    att, lse = pl.pallas_call(
        body, name="fox_fwd",
        out_shape=(jax.ShapeDtypeStruct((L, MAIN_WIDTH), F32),
                   jax.ShapeDtypeStruct((FOX_HEADS, nq, tq, 1), F32)),
        grid=(FOX_HEADS // nh, nq),
        in_specs=[pl.BlockSpec((tq, W), lambda h, i: (i, h)),
                  pl.BlockSpec((L, W), lambda h, i: (0, h)),
                  pl.BlockSpec((L, W), lambda h, i: (0, FOX_HEADS // nh + h)),
                  pl.BlockSpec((nh, nq, 1, tq), lambda h, i: (h, 0, 0, 0))],
        out_specs=(pl.BlockSpec((tq, W), lambda h, i: (i, h)),
                   pl.BlockSpec((nh, None, tq, 1), lambda h, i: (h, i, 0, 0))),
        scratch_shapes=[pltpu.VMEM((nh, tq, 1), F32), pltpu.VMEM((nh, tq, 1), F32),
                        pltpu.VMEM((nh, tq, HEAD_DIM), F32)],
        compiler_params=_params("parallel", "parallel"),
    )(proj, kv, kv, fk)
    return att, lse.reshape(lse_shape)


def _fox_bwd(proj, kv, fk, lse, delta, datt, dproj):
    L = proj.shape[0]
    tq = min(FOX_BLOCK, L)
    nq = L // tq
    sp = _fox_specs(tq, L)

    def body(q_ref, k_ref, v_ref, fk_ref, lse_ref, dl_ref, do_ref, dp_hbm,
             dq_ref, dk_ref, dv_ref, dfq_ref, dfk_ref, dk_s, dv_s, df_s, dq_s, dfq_s):
        ki = pl.program_id(1)

        @pl.when(ki == 0)
        def _():
            dq_s[...] = jnp.zeros_like(dq_s)
            dfq_s[...] = jnp.zeros_like(dfq_s)

        k, v, fk = k_ref[...], v_ref[...], fk_ref[...]
        dk_s[...] = jnp.zeros_like(dk_s)
        dv_s[...] = jnp.zeros_like(dv_s)
        df_s[...] = jnp.zeros_like(df_s)

        def block(i, rows, width, diagonal):
            n = rows.stop - rows.start
            r0 = pl.multiple_of(i * tq + rows.start, n)
            qs = (q_ref[pl.ds(r0, n), :] * (HEAD_DIM ** -0.5)).astype(BF16)
            dob = do_ref[pl.ds(r0, n), :].astype(BF16)
            kw, vw = k[:width], v[:width]
            p = jnp.exp(_fox_scores(qs, kw, fk[:, :width], diagonal, rows.start) - lse_ref[i][rows])
            dp = lax.dot_general(dob, vw, _NT, preferred_element_type=F32)
            ds = p * (dp - dl_ref[i][rows])
            dsb = ds.astype(BF16)
            dv_s[:width] += lax.dot_general(p.astype(BF16), dob, _TN, preferred_element_type=F32)
            dk_s[:width] += lax.dot_general(dsb, qs, _TN, preferred_element_type=F32)
            df_s[:, :width] -= jnp.sum(ds, axis=0, keepdims=True)
            dq_s[i, rows] += jnp.dot(dsb, kw, preferred_element_type=F32)
            dfq_s[i, rows] += jnp.sum(ds, axis=1, keepdims=True)

        def above(i, carry):
            block(i, slice(0, tq), tq, False)
            return carry

        for rows, width in _fox_diagonal_parts(tq):
            block(ki, rows, width, True)
        lax.fori_loop(ki + 1, nq, above, 0)
        dk_ref[...] = dk_s[...].astype(BF16)
        dv_ref[...] = dv_s[...].astype(BF16)
        dfk_ref[...] = df_s[...]

        @pl.when(ki == nq - 1)
        def _():
            dq_ref[...] = (dq_s[...].reshape(L, HEAD_DIM) * (HEAD_DIM ** -0.5)).astype(BF16)
            dfq_ref[...] = dfq_s[...]

    return pl.pallas_call(
        body, name="fox_bwd",
        out_shape=(jax.ShapeDtypeStruct(dproj.shape, dproj.dtype),
                   jax.ShapeDtypeStruct((L, MAIN_WIDTH), BF16),
                   jax.ShapeDtypeStruct((L, MAIN_WIDTH), BF16),
                   jax.ShapeDtypeStruct((FOX_HEADS, nq, tq, 1), F32),
                   jax.ShapeDtypeStruct((FOX_HEADS, nq, 1, tq), F32)),
        grid=(FOX_HEADS, nq),
        in_specs=[sp["seq"](0), sp["rows"](0), sp["rows"](FOX_HEADS), sp["row"],
                  sp["col_all"], sp["col_all"], sp["seq"](0), _ANY],
        out_specs=(sp["seq"](0), sp["rows"](0), sp["rows"](0), sp["col_all"], sp["row"]),
        input_output_aliases={7: 0},
        scratch_shapes=[pltpu.VMEM((tq, HEAD_DIM), F32), pltpu.VMEM((tq, HEAD_DIM), F32),
                        pltpu.VMEM((1, tq), F32), pltpu.VMEM((nq, tq, HEAD_DIM), F32),
                        pltpu.VMEM((nq, tq, 1), F32)],
        compiler_params=_params("parallel", "arbitrary"),
    )(proj, kv, kv, fk, lse, delta, datt, dproj)


def _pad_lanes(a):
    return jnp.pad(a, ((0, 0), (0, LANES - a.shape[1])))


def _mem_branch_fwd(memn, w_mk, proj, tag):
    kvm = _mm(memn, w_mk, name="mem_kv_" + tag)
    return kvm, _mem_attn_fwd(proj, kvm)


def _mem_branch_bwd(mem, g, w_mk, proj, memn, kvm, do_mem, dproj, tag):
    dproj, dkvm = _mem_attn_bwd(proj, kvm, do_mem, dproj)
    dkvm = dkvm.astype(BF16)
    dw_mk = _mm(memn, dkvm, ta=True, name="dw_mem_kv_" + tag, out_dtype=BF16)
    dmemn = _mm(dkvm, w_mk, tb=True, name="dmemn_" + tag)
    _, dg = _rmsnorm_bwd(mem, g, dmemn, name="mem_norm_bwd_" + tag, dx_dtype=BF16)
    return dproj, dw_mk, dg


def _local_step(x, mem, target, w, fetch=None, grads_ready=None):
    if grads_ready is None:
        grads_ready = lambda group, grads, token: token
    L = x.shape[0]
    g = {}
    w = dict(w)

    b_re_t = jnp.transpose(w["b_re"], (0, 2, 1))
    b_im_t = jnp.transpose(w["b_im"], (0, 2, 1))
    ar, ai, bbr_t, bbi_t = _s5_prep(w["lam_re"], w["lam_im"], w["log_step"], b_re_t, b_im_t)
    bmat, cmat = _s5_block_mats(bbr_t, bbi_t, w["c_re"], w["c_im"])
    a_rows = _s5_a_rows(ar, ai)

    hn0 = _rmsnorm_fwd(x, w["pre_norm_g"][0], name="pre_norm_0", out_dtype=BF16)
    memn0 = _rmsnorm_fwd(mem, w["mem_norm_g"][0], name="mem_norm_0", out_dtype=BF16)
    memn1 = _rmsnorm_fwd(mem, w["mem_norm_g"][1], name="mem_norm_1", out_dtype=BF16)
    if fetch is not None:
        w.update(fetch("a", [hn0, memn0, memn1, bmat, cmat, a_rows]))
    proj_a = _mm(hn0, w["w_in_a"], name="in_proj_a")
    y, yg, xp = _s5_fwd(proj_a, bmat, cmat, a_rows, w["d_skip"])
    if fetch is not None:
        w.update(fetch("b", yg))
    t = _mm(yg, w["w_glu"], name="glu_proj")
    kvm0, om0 = _mem_branch_fwd(memn0, w["w_mem_kv"][0], proj_a, "0")
    cat0 = _gate_a_fwd(y, t, w["b_glu"], proj_a, om0)
    o0 = _mm(cat0, w["w_out"][0], name="out_proj_0")
    h1, kv_in, hn1 = _post_norm_and_next_norms(
        o0, w["post_norm_g"][0], x, w["kv_norm_g"], w["pre_norm_g"][1], name="post_norm_0_kv_pre_norm_1")

    if fetch is not None:
        w.update(fetch("c", kv_in))
    kv = _mm(kv_in, w["w_kv"], name="kv_proj", out_dtype=BF16)
    pre_f = _mm(kv_in, w["w_fgate"], name="fgate_proj")
    b_f = jnp.pad(w["b_fgate"], (0, LANES - FOX_HEADS))
    fcum = _fgate_fwd(pre_f, b_f)
    fc = jnp.transpose(fcum[:, :FOX_HEADS])
    tq = min(FOX_BLOCK, L)
    fk = fc.reshape(FOX_HEADS, L // tq, 1, tq)

    proj_b = _mm(hn1, w["w_in_b"], name="in_proj_b")
    att, lse = _fox_fwd(proj_b, kv, fk)
    kvm1, om1 = _mem_branch_fwd(memn1, w["w_mem_kv"][1], proj_b, "1")
    cat1 = _gate_b_fwd(att, proj_b, om1)
    o1 = _mm(cat1, w["w_out"][1], name="out_proj_1")
    dh2, loss_row = _final_norm_loss(o1, w["post_norm_g"][1], h1, target)

    do1, dpost1 = _rmsnorm_bwd(o1, w["post_norm_g"][1], dh2, name="post_norm_bwd_1", dx_dtype=BF16)
    dcat1 = _mm(do1, w["w_out"][1], tb=True, name="dcat_1", out_dtype=BF16)
    g["w_out_1"] = _mm(cat1, do1, ta=True, name="dw_out_1", out_dtype=BF16)
    datt, dproj_b, dom1, delta = _gate_b_bwd(dcat1, att, proj_b, om1)
    dproj_b, g["w_mem_kv_1"], dmemg1 = _mem_branch_bwd(mem, w["mem_norm_g"][1], w["w_mem_kv"][1], proj_b,
                                                      memn1, kvm1, dom1, dproj_b, "1")
    delta = delta.reshape(lse.shape)
    dproj_b, dk, dv, dfq, dfk = _fox_bwd(proj_b, kv, fk, lse, delta, datt, dproj_b)
    g["w_in_b"] = _mm(hn1, dproj_b, ta=True, name="dw_in_b", out_dtype=BF16, shards=N_CHIPS)
    dhn1 = _mm(dproj_b, w["w_in_b"], tb=True, name="dhn_1")

    dkv = jnp.concatenate([dk, dv], axis=1)
    g["w_kv"] = _mm(kv_in, dkv, ta=True, name="dw_kv", out_dtype=BF16, shards=N_CHIPS)
    dkv_in_a = _mm(dkv, w["w_kv"], tb=True, name="dkv_in_kv")
    dfcum = _pad_lanes(jnp.transpose(dfq.reshape(FOX_HEADS, L) + dfk.reshape(FOX_HEADS, L)))
    dpre_f, db_f = _fgate_bwd(dfcum, pre_f, b_f)
    g["b_fgate"] = db_f[0, :FOX_HEADS]
    g["w_fgate"] = _mm(kv_in, dpre_f, ta=True, name="dw_fgate")[:, :FOX_HEADS]
    dkv_in_b = _mm(dpre_f, w["w_fgate"], tb=True, name="dkv_in_fgate")
    dh1, g["kv_norm_g"], dpre1 = _rmsnorm_bwd_pair(h1, w["kv_norm_g"], (dkv_in_a, dkv_in_b), w["pre_norm_g"][1],
                                                   dhn1, adds=(dh2,), name="kv_pre_norm_bwd")
    dh1 = grads_ready("b", g, dh1)

    do0, dpost0 = _rmsnorm_bwd(o0, w["post_norm_g"][0], dh1, name="post_norm_bwd_0", dx_dtype=BF16)
    dcat0 = _mm(do0, w["w_out"][0], tb=True, name="dcat_0", out_dtype=BF16)
    g["w_out_0"] = _mm(cat0, do0, ta=True, name="dw_out_0", out_dtype=BF16)
    dcat0 = grads_ready("b_send", g, dcat0)
    dproj_a, dt, dyg_a, dom0, db_glu = _gate_a_bwd(dcat0, y, t, w["b_glu"], proj_a, om0)
    g["b_glu"] = db_glu[0]
    g["w_glu"] = _mm(yg, dt, ta=True, name="dw_glu", out_dtype=BF16)
    dyg_b = _mm(dt, w["w_glu"], tb=True, name="dyg")
    dproj_a, g["w_mem_kv_0"], dmemg0 = _mem_branch_bwd(mem, w["mem_norm_g"][0], w["w_mem_kv"][0], proj_a,
                                                      memn0, kvm0, dom0, dproj_a, "0")
    dyg_b = grads_ready("a1", g, dyg_b)
    dproj_a, db_blk, dc_blk, da_rows, dd_skip = _s5_bwd(proj_a, dyg_a, dyg_b, y, xp, bmat, cmat, a_rows,
                                                        w["d_skip"], dproj_a)
    dproj_a = grads_ready("a1_send", g, dproj_a)
    g["d_skip"] = dd_skip[0]
    g["w_in_a"] = _mm(hn0, dproj_a, ta=True, name="dw_in_a", out_dtype=BF16, shards=N_CHIPS)
    dproj_a = grads_ready("a2", g, dproj_a)
    dhn0 = _mm(dproj_a, w["w_in_a"], tb=True, name="dhn_0")
    grad_x, dpre0 = _rmsnorm_bwd(x, w["pre_norm_g"][0], dhn0, adds=(dh1,), name="pre_norm_bwd_0")

    dbb = _s5_unfold(db_blk)
    dcc = _s5_unfold(dc_blk)
    g["c_re"], g["c_im"] = dcc[0], -dcc[1]
    d_ar = da_rows[:, 0, :STATE_COLS].reshape(SSM_GROUPS, SSM_STATE)
    d_ai = da_rows[:, 0, STATE_COLS:].reshape(SSM_GROUPS, SSM_STATE)
    dlr, dli, dls, dbr_t, dbi_t = _s5_prep_bwd(w["lam_re"], w["lam_im"], w["log_step"], b_re_t, b_im_t,
                                               d_ar, d_ai, dbb[0], dbb[1])
    g["lam_re"], g["lam_im"], g["log_step"] = dlr, dli, dls[:, 0]
    g["b_re"] = jnp.transpose(dbr_t, (0, 2, 1))
    g["b_im"] = jnp.transpose(dbi_t, (0, 2, 1))
    g["pre_norm_g"] = jnp.stack([dpre0, dpre1])
    g["post_norm_g"] = jnp.stack([dpost0, dpost1])
    g["mem_norm_g"] = jnp.stack([dmemg0, dmemg1])
    return loss_row, grad_x, g


_MESH = pl.DeviceIdType.MESH
_ANY = pl.BlockSpec(memory_space=pl.ANY)


def _place():
    x, y, c = lax.axis_index("x"), lax.axis_index("y"), lax.axis_index("c")
    chips = [(1 - x, y), (x, 1 - y), (1 - x, 1 - y)]
    return x, y, c, chips


_HBM = pl.BlockSpec(memory_space=pltpu.HBM)
_SEM = pl.BlockSpec(memory_space=pltpu.SEMAPHORE)
_SIDE = pltpu.SideEffectType.DATAFLOW_SIDE_EFFECTING


def _in_hbm(a):
    return pltpu.with_memory_space_constraint(a, pltpu.HBM)


def _hbm_like(a):
    return pltpu.HBM(a.shape, a.dtype)


def _ici_copies(srcs, lands, send_sem, recv_sem, src_at, dst_at, wait_at, to_sibling=False):
    x, y, c, chips = _place()
    peers = [(x, y, 1 - c)] if to_sibling else [(cx, cy, c) for cx, cy in chips]
    m = len(peers)
    start, wait = [], []
    for i in range(len(srcs)):
        for k, (px, py, pc) in enumerate(peers):
            sem = dict(send_sem=send_sem.at[m * i + k], recv_sem=recv_sem.at[m * i + k],
                       device_id=(px, py, pc), device_id_type=_MESH)
            src = src_at(srcs[i], 2 * px + py, c)
            start.append(pltpu.make_async_remote_copy(src_ref=src, dst_ref=dst_at(lands[i], 2 * x + y, k, c), **sem))
            wait.append(pltpu.make_async_remote_copy(src_ref=src, dst_ref=wait_at(lands[i], 2 * px + py, k, c), **sem))
    return start, wait


def _route_peers(route):
    return 1 if len(route) == 4 else 3


_BLOCK_ROUTE = (lambda s, j, c: s, lambda l, me, k, c: l.at[me, c], lambda l, j, k, c: l.at[j, c])


def _ici_start(srcs, lands, token, route, *, name):
    n = len(srcs)

    def body(*refs):
        start, _ = _ici_copies(refs[:n], refs[n:2 * n], refs[2 * n + 1], refs[2 * n + 2], *route)
        for cp in start:
            cp.start()

    sems = pltpu.SemaphoreType.DMA((_route_peers(route) * n,))
    outs = pl.pallas_call(
        body, name=name,
        out_shape=(sems, sems, *[_hbm_like(a) for a in srcs], *[_hbm_like(a) for a in lands], _hbm_like(token)),
        in_specs=[_HBM] * (2 * n + 1), out_specs=(_SEM, _SEM, *[_HBM] * (2 * n + 1)),
        input_output_aliases={i: 2 + i for i in range(2 * n + 1)},
        compiler_params=pltpu.CompilerParams(has_side_effects=_SIDE),
    )(*[_in_hbm(a) for a in srcs], *[_in_hbm(a) for a in lands], _in_hbm(token))
    return (outs[0], outs[1], list(outs[2:2 + n]), list(outs[2 + n:2 + 2 * n])), outs[2 + 2 * n]


def _ici_wait(handle, after, route, *, name):
    send_sem, recv_sem, srcs, lands = handle
    n = len(srcs)
    after = list(after) if isinstance(after, (list, tuple)) else [after]

    def body(*refs):
        _, wait = _ici_copies(refs[:n], refs[n:2 * n], refs[2 * n], refs[2 * n + 1], *route)
        for cp in wait:
            cp.wait_send()
            cp.wait_recv()

    outs = pl.pallas_call(
        body, name=name,
        out_shape=(*[_hbm_like(a) for a in srcs], *[_hbm_like(a) for a in lands]),
        in_specs=[_HBM] * (2 * n) + [_SEM, _SEM] + [_ANY] * len(after), out_specs=tuple([_HBM] * (2 * n)),
        input_output_aliases={i: i for i in range(2 * n)},
        compiler_params=pltpu.CompilerParams(has_side_effects=_SIDE),
    )(*srcs, *lands, send_sem, recv_sem, *after)
    return list(outs[:n]), list(outs[n:])


_GATHER_ROUTE = (lambda s, j, c: s.at[c], lambda l, me, k, c: l.at[me, c], lambda l, j, k, c: l.at[j, c])
_SCATTER_ROUTE = (lambda s, j, c: s.at[j], lambda l, me, k, c: l.at[k], lambda l, j, k, c: l.at[k])
_SHARE_ROUTE = (lambda s, j, c: s, lambda l, me, k, c: l.at[c], lambda l, j, k, c: l.at[1 - c], True)
_SWAP_ROUTE = (lambda s, j, c: s.at[:, 1 - c], lambda l, me, k, c: l, lambda l, j, k, c: l, True)


def _gather_forward(lands, tag, own=False):
    n = len(lands)
    m = 4 if own else 3

    def body(*refs):
        ins, outs = refs[:n], refs[n:2 * n]
        send_sem, recv_sem = refs[2 * n:]
        x, y, c, chips = _place()
        slots = [2 * cx + cy for cx, cy in chips] + [2 * x + y]

        def copy(i, k, half):
            return pltpu.make_async_remote_copy(
                src_ref=ins[i].at[slots[k], half], dst_ref=outs[i].at[slots[k], half],
                send_sem=send_sem.at[m * i + k], recv_sem=recv_sem.at[m * i + k],
                device_id=(x, y, 1 - c), device_id_type=_MESH)

        copies = [copy(i, k, c) for i in range(n) for k in range(m)]
        for cp in copies:
            cp.start()
        for i in range(n):
            for k in range(m):
                copy(i, k, 1 - c).wait_recv()
        for cp in copies:
            cp.wait_send()

    return pl.pallas_call(
        body, name="gather_forward_to_sibling_" + tag,
        out_shape=[jax.ShapeDtypeStruct(a.shape, a.dtype) for a in lands],
        in_specs=[_ANY] * n, out_specs=[_ANY] * n,
        input_output_aliases={i: i for i in range(n)},
        scratch_shapes=[pltpu.SemaphoreType.DMA((m * n,)), pltpu.SemaphoreType.DMA((m * n,))],
    )(*lands)


def _swap_halves(grads, tag):
    n = len(grads)

    def body(*refs):
        ins, outs = refs[:n], refs[n:2 * n]
        send_sem, recv_sem = refs[2 * n:]
        x, y, c, _ = _place()
        copies = [pltpu.make_async_remote_copy(
            src_ref=ins[i].at[:, 1 - c], dst_ref=outs[i],
            send_sem=send_sem.at[i], recv_sem=recv_sem.at[i],
            device_id=(x, y, 1 - c), device_id_type=_MESH) for i in range(n)]
        for cp in copies:
            cp.start()
        for cp in copies:
            cp.wait()

    return pl.pallas_call(
        body, name="grad_swap_halves_" + tag,
        out_shape=[jax.ShapeDtypeStruct((N_CHIPS,) + g.shape[2:], g.dtype) for g in grads],
        in_specs=[_ANY] * n, out_specs=[_ANY] * n,
        scratch_shapes=[pltpu.SemaphoreType.DMA((n,)), pltpu.SemaphoreType.DMA((n,))],
    )(*grads)


def _sum_rows(h, C):
    return max(d for d in range(SUBLANES, h + 1, SUBLANES) if h % d == 0 and d * C <= 1 << 20)


SUM_STEPS = 4


def _pair_sums(gs, rs, c_idx, *, name):
    n = len(gs)
    rows = [g.shape[2] // SUM_STEPS for g in gs]

    def body(c_ref, *refs):
        for g_ref, r_ref, o_ref in zip(refs[:n], refs[n:2 * n], refs[2 * n:]):
            o_ref[...] = (g_ref[...].astype(F32) + r_ref[...].astype(F32)).astype(o_ref.dtype)

    return pl.pallas_call(
        body, name=name,
        out_shape=[jax.ShapeDtypeStruct((N_CHIPS,) + g.shape[2:], g.dtype) for g in gs],
        grid_spec=pltpu.PrefetchScalarGridSpec(
            num_scalar_prefetch=1, grid=(N_CHIPS, SUM_STEPS),
            in_specs=[pl.BlockSpec((None, None, tr, g.shape[3]), lambda j, i, s: (j, s[0], i, 0))
                      for g, tr in zip(gs, rows)]
            + [pl.BlockSpec((None, tr, g.shape[3]), lambda j, i, s: (j, i, 0)) for g, tr in zip(gs, rows)],
            out_specs=[pl.BlockSpec((None, tr, g.shape[3]), lambda j, i, s: (j, i, 0)) for g, tr in zip(gs, rows)]),
        compiler_params=_params("parallel", "parallel"),
    )(c_idx, *gs, *rs)


def _owner_sums(ss, rs, jc_idx, *, name):
    n = len(ss)
    rows = [s.shape[1] // SUM_STEPS for s in ss]

    def body(jc_ref, *refs):
        for s_ref, r_ref, m_ref, o_ref in zip(refs[:n], refs[n:2 * n], refs[2 * n:3 * n], refs[3 * n:]):
            acc = s_ref[...].astype(F32)
            for k in range(3):
                acc = acc + r_ref[k].astype(F32)
            m_ref[...] = acc
            o_ref[...] = acc

    outs = pl.pallas_call(
        body, name=name,
        out_shape=[jax.ShapeDtypeStruct(s.shape[1:], F32) for s in ss]
        + [jax.ShapeDtypeStruct((2,) + s.shape[1:], F32) for s in ss],
        grid_spec=pltpu.PrefetchScalarGridSpec(
            num_scalar_prefetch=1, grid=(SUM_STEPS,),
            in_specs=[pl.BlockSpec((None, tr, s.shape[2]), lambda i, p: (p[0], i, 0)) for s, tr in zip(ss, rows)]
            + [pl.BlockSpec((3, tr, s.shape[2]), lambda i, p: (0, i, 0)) for s, tr in zip(ss, rows)],
            out_specs=[pl.BlockSpec((tr, s.shape[2]), lambda i, p: (i, 0)) for s, tr in zip(ss, rows)]
            + [pl.BlockSpec((None, tr, s.shape[2]), lambda i, p: (p[1], i, 0)) for s, tr in zip(ss, rows)]),
        compiler_params=_params("parallel"),
    )(jc_idx, *ss, *rs)
    return outs[:n], outs[n:]


def _chip_sums(grads, c_idx, tag):
    views = [g.reshape(N_CHIPS, 2, g.shape[1] // 2, g.shape[2]) for g in grads]
    arrived = _swap_halves(views, tag)
    return _pair_sums(views, arrived, c_idx, name=f"grad_pair_sums_{tag}")


def _sum_devices(blocks):
    R = blocks.shape[2]
    tr = _sum_rows(R, 2 * N_CHIPS * LANES)

    def body(b_ref, o_ref):
        acc = b_ref[0, 0]
        for d in range(1, 2 * N_CHIPS):
            acc = acc + b_ref[d // 2, d % 2]
        o_ref[...] = acc

    return pl.pallas_call(
        body, name="sum_small_over_devices", out_shape=jax.ShapeDtypeStruct((R, LANES), F32),
        grid=(R // tr,),
        in_specs=[pl.BlockSpec((N_CHIPS, 2, tr, LANES), lambda i: (0, 0, i, 0))],
        out_specs=pl.BlockSpec((tr, LANES), lambda i: (i, 0)),
        compiler_params=_params("parallel"),
    )(blocks)


def _adamw(w, g, m, v, *, name):
    R, C = w.shape
    tr = max(d for d in range(SUBLANES, R + 1, SUBLANES)
             if R % d == 0 and 7 * 2 * d * C * 4 <= VMEM_LIMIT_BYTES // 2)

    def body(w_ref, g_ref, m_ref, v_ref, d_ref, nm_ref, nv_ref):
        g = g_ref[...]
        m = ADAM_B1 * m_ref[...] + (1.0 - ADAM_B1) * g
        v = ADAM_B2 * v_ref[...] + (1.0 - ADAM_B2) * (g * g)
        nm_ref[...] = m
        nv_ref[...] = v
        m_hat = m / (1.0 - ADAM_B1 ** ADAM_STEP)
        v_hat = v / (1.0 - ADAM_B2 ** ADAM_STEP)
        d_ref[...] = -ADAM_LR * (m_hat / (jnp.sqrt(v_hat) + ADAM_EPS) + ADAM_WD * w_ref[...])

    blk = pl.BlockSpec((tr, C), lambda i: (i, 0))
    sds = jax.ShapeDtypeStruct((R, C), F32)
    return pl.pallas_call(
        body, name=name, out_shape=(sds, sds, sds), grid=(R // tr,),
        in_specs=[blk] * 4, out_specs=(blk, blk, blk),
        compiler_params=_params("parallel"),
    )(w, g, m, v)


_TILE = SUBLANES * LANES


def _pack(arrays):
    rows = []
    for a in arrays:
        flat = a.reshape(-1)
        flat = jnp.pad(flat, (0, (-flat.shape[0]) % _TILE))
        rows.append(flat.reshape(-1, LANES))
    return jnp.concatenate(rows, axis=0)


def _unpack(buf, shapes):
    out, r = [], 0
    for s in shapes:
        size = math.prod(s)
        nr = -(-size // _TILE) * SUBLANES
        out.append(buf[r:r + nr].reshape(-1)[:size].reshape(s))
        r += nr
    return out


_BIG = ("w_in_a", "w_glu", "w_kv", "w_in_b", "w_mem_kv", "w_out")
_REPLICATED = ("pre_norm_g", "post_norm_g", "lam_re", "lam_im", "log_step", "b_re", "b_im", "c_re", "c_im",
               "kv_norm_g", "b_fgate", "mem_norm_g")
_SHARDED_SMALL = ("d_skip", "b_glu", "w_fgate")
_WEIGHTS = ("pre_norm_g", "post_norm_g", "w_in_a", "lam_re", "lam_im", "log_step", "b_re", "b_im", "c_re",
            "c_im", "d_skip", "w_glu", "b_glu", "kv_norm_g", "w_kv", "w_fgate", "b_fgate", "w_in_b",
            "mem_norm_g", "w_mem_kv", "w_out")


def _halves(a):
    return a.reshape(2, a.shape[0] // 2, a.shape[1])


def _unhalve(a):
    return a.reshape(N_CHIPS, 2 * a.shape[2], a.shape[3])


def _columns(a):
    return jnp.transpose(a, (1, 0, 2)).reshape(a.shape[1], N_CHIPS * a.shape[2])


def kernel(x, mem, pre_norm_g, post_norm_g, w_in_a, lam_re, lam_im, log_step, b_re, b_im, c_re, c_im, d_skip, w_glu, b_glu, kv_norm_g, w_kv, w_fgate, b_fgate, w_in_b, mem_norm_g, w_mem_kv, w_out, loss_target, m_pre_norm_g, m_post_norm_g, m_w_in_a, m_lam_re, m_lam_im, m_log_step, m_b_re, m_b_im, m_c_re, m_c_im, m_d_skip, m_w_glu, m_b_glu, m_kv_norm_g, m_w_kv, m_w_fgate, m_b_fgate, m_w_in_b, m_mem_norm_g, m_w_mem_kv, m_w_out, v_pre_norm_g, v_post_norm_g, v_w_in_a, v_lam_re, v_lam_im, v_log_step, v_b_re, v_b_im, v_c_re, v_c_im, v_d_skip, v_w_glu, v_b_glu, v_kv_norm_g, v_w_kv, v_w_fgate, v_b_fgate, v_w_in_b, v_mem_norm_g, v_w_mem_kv, v_w_out):
    a = dict(locals())
    xi, yi, ci = lax.axis_index("x"), lax.axis_index("y"), lax.axis_index("c")
    chip = 2 * xi + yi
    c_idx = jnp.reshape(ci, (1,)).astype(jnp.int32)
    jc_idx = jnp.stack([chip, ci]).astype(jnp.int32)

    vec = jnp.zeros((2 * SUBLANES, MAIN_WIDTH // N_CHIPS), F32)
    vec = vec.at[0].set(a["d_skip"][0]).at[1].set(a["b_glu"][0])
    def own_slot(gathered, parts):
        return [lax.dynamic_update_index_in_dim(g, p, chip, 0) for g, p in zip(gathered, parts)]

    parts_a = [_halves(a["w_in_a"][0].astype(BF16)), _halves(vec)]
    parts_b = [_halves(a["w_glu"][0].astype(BF16)),
               *[_halves(a["w_mem_kv"][i].astype(BF16)) for i in range(2)],
               *[_halves(a["w_out"][i].astype(BF16)) for i in range(2)]]
    parts_c = [_halves(a["w_kv"].astype(BF16)), _halves(_pad_lanes(a["w_fgate"]).astype(BF16)),
               _halves(a["w_in_b"][0].astype(BF16))]
    travelling, token = {}, a["pre_norm_g"]
    for tag, parts in (("a", parts_a), ("b", parts_b), ("c", parts_c)):
        lands = [lax.empty((N_CHIPS,) + p.shape, p.dtype) for p in parts]
        travelling[tag], token = _ici_start(parts, lands, token, _GATHER_ROUTE, name=f"gather_{tag}_start")

    def fetch(tag, after):
        parts, lands = _ici_wait(travelling[tag], after, _GATHER_ROUTE, name=f"gather_{tag}_wait")
        full = own_slot(_gather_forward(lands, tag), parts)
        if tag == "a":
            w_in_a, vecs = full
            return dict(w_in_a=_columns(_unhalve(w_in_a)), d_skip=vecs[:, 0, 0, :].reshape(MAIN_WIDTH),
                        b_glu=vecs[:, 0, 1, :].reshape(MAIN_WIDTH))
        if tag == "b":
            w_glu, w_mk0, w_mk1, w_out0, w_out1 = full
            return dict(w_glu=w_glu.reshape(MAIN_WIDTH, MAIN_WIDTH),
                        w_mem_kv=[m.reshape(D_MODEL, 2 * MEM_WIDTH) for m in (w_mk0, w_mk1)],
                        w_out=[o.reshape(D_MODEL, D_MODEL) for o in (w_out0, w_out1)])
        w_kv, w_fg, w_in_b = full
        return dict(w_kv=_columns(_unhalve(w_kv)), w_fgate=w_fg.reshape(D_MODEL, LANES),
                    w_in_b=_columns(_unhalve(w_in_b)))

    early = ("mem_norm_g", "lam_re", "lam_im", "log_step", "b_re", "b_im", "c_re", "c_im")
    token, *held = lax.optimization_barrier((token, *[a[n] for n in early]))
    held = dict(zip(early, held))
    w = dict(
        pre_norm_g=token, post_norm_g=a["post_norm_g"], mem_norm_g=held["mem_norm_g"],
        kv_norm_g=a["kv_norm_g"], b_fgate=a["b_fgate"],
        **{n: held[n][0] for n in early[1:]})

    sent = {}

    swapping = {}

    def grads_ready(event, g, token):
        tag = event.split("_")[0]
        if event in ("b", "a1"):
            big = {"b": lambda: [g["w_kv"], g["w_in_b"], g["w_mem_kv_1"].reshape(N_CHIPS, -1, 2 * MEM_WIDTH),
                                 g["w_out_1"].reshape(N_CHIPS, -1, D_MODEL)],
                   "a1": lambda: [g["w_glu"].reshape(N_CHIPS, -1, MAIN_WIDTH),
                                  g["w_mem_kv_0"].reshape(N_CHIPS, -1, 2 * MEM_WIDTH),
                                  g["w_out_0"].reshape(N_CHIPS, -1, D_MODEL)]}[tag]()
            views = [b.reshape(N_CHIPS, 2, b.shape[1] // 2, b.shape[2]) for b in big]
            lands = [lax.empty((N_CHIPS,) + v.shape[2:], v.dtype) for v in views]
            swapping[tag], token = _ici_start(views, lands, token, _SWAP_ROUTE, name=f"grad_swap_{tag}_start")
            return token
        if event == "a2":
            sums = _chip_sums([g["w_in_a"]], c_idx, tag)
        else:
            views, arrived = _ici_wait(swapping[tag], token, _SWAP_ROUTE, name=f"grad_swap_{tag}_wait")
            sums = _pair_sums(views, arrived, c_idx, name=f"grad_pair_sums_{tag}")
        lands = [lax.empty((3,) + s.shape[1:], s.dtype) for s in sums]
        sent[tag], token = _ici_start(sums, lands, token, _SCATTER_ROUTE, name=f"grad_send_{tag}_start")
        return token

    loss_row, grad_x, g = _local_step(a["x"][0], a["mem"][0], a["loss_target"][0], w, fetch, grads_ready)

    small_names = _REPLICATED + _SHARDED_SMALL
    pack = _pack([g[n] for n in small_names])
    blocks = lax.empty((N_CHIPS, 2) + pack.shape, F32)
    small_sent, token = _ici_start([pack], [blocks], loss_row, _BLOCK_ROUTE, name="small_sums_start")

    sharing = {}
    for tag in ("b", "a1", "a2"):
        sums, arrived = _ici_wait(sent[tag], [grad_x, token], _SCATTER_ROUTE, name=f"grad_send_{tag}_wait")
        mine, bufs = _owner_sums(sums, arrived, jc_idx, name=f"grad_owner_sums_{tag}")
        sharing[tag], token = _ici_start(mine, bufs, token, _SHARE_ROUTE, name=f"grad_share_{tag}_start")
    loss = lax.psum(jnp.sum(token), MESH_AXES)

    def shared(tag, after):
        _, bufs = _ici_wait(sharing[tag], after, _SHARE_ROUTE, name=f"grad_share_{tag}_wait")
        return [b.reshape(-1, b.shape[2]) for b in bufs]

    grads, delta, new_m, new_v = {}, {}, {}, {}

    def adam(n):
        shape = a[n].shape
        d2 = (-1, shape[-1])
        d, m, v = _adamw(a[n].reshape(d2), grads[n].reshape(d2), a["m_" + n].reshape(d2),
                         a["v_" + n].reshape(d2), name="adamw_" + n)
        delta[n], new_m[n], new_v[n] = d.reshape(shape), m.reshape(shape), v.reshape(shape)
        return d

    r_kv, r_in_b, r_mk1, r_out1 = shared("b", token)
    grads["w_kv"], grads["w_in_b"] = r_kv, r_in_b[None]
    done = [adam("w_kv"), adam("w_in_b")]
    r_glu, r_mk0, r_out0 = shared("a1", done)
    grads["w_glu"], grads["w_mem_kv"], grads["w_out"] = r_glu[None], jnp.stack([r_mk0, r_mk1]), jnp.stack([r_out0, r_out1])
    done = [adam("w_glu"), adam("w_mem_kv"), adam("w_out")]
    (r_in_a,) = shared("a2", done)
    grads["w_in_a"] = r_in_a[None]
    adam("w_in_a")

    (pack,), (blocks,) = _ici_wait(small_sent, [delta[n] for n in _BIG], _BLOCK_ROUTE, name="small_sums_wait")
    blocks = lax.dynamic_update_slice(blocks, pack[None, None], (chip, ci, 0, 0))
    (blocks,) = _gather_forward([blocks], "small", own=True)
    small = dict(zip(small_names, _unpack(_sum_devices(blocks), [g[n].shape for n in small_names])))
    for n in _REPLICATED:
        grads[n] = small[n].reshape(a[n].shape)
    nd = MAIN_WIDTH // N_CHIPS
    grads["d_skip"] = lax.dynamic_slice(small["d_skip"], (chip * nd,), (nd,))[None]
    grads["b_glu"] = lax.dynamic_slice(small["b_glu"], (chip * nd,), (nd,))[None]
    nf = D_MODEL // N_CHIPS
    grads["w_fgate"] = lax.dynamic_slice(small["w_fgate"], (chip * nf, 0), (nf, FOX_HEADS))

    shapes = [a[n].shape for n in small_names]
    d, m, v = _adamw(_pack([a[n] for n in small_names]), _pack([grads[n] for n in small_names]),
                     _pack([a["m_" + n] for n in small_names]), _pack([a["v_" + n] for n in small_names]),
                     name="adamw_small")
    for n, dd, mm, vv in zip(small_names, _unpack(d, shapes), _unpack(m, shapes), _unpack(v, shapes)):
        delta[n], new_m[n], new_v[n] = dd, mm, vv

    return (loss, grad_x[None], *[grads[n] for n in _WEIGHTS], *[delta[n] for n in _WEIGHTS],
            *[new_m[n] for n in _WEIGHTS], *[new_v[n] for n in _WEIGHTS])
```

```python
import math

import jax
import jax.numpy as jnp
from jax import lax
from jax.experimental import pallas as pl
from jax.experimental.pallas import tpu as pltpu

F32 = jnp.float32
BF16 = jnp.bfloat16

D_MODEL = 2048
N_MEM = 256
MAIN_WIDTH = 1536
MEM_WIDTH = 512
IN_WIDTH = 2 * MAIN_WIDTH + 2 * MEM_WIDTH
HEAD_DIM = 128
FOX_HEADS = MAIN_WIDTH // HEAD_DIM
MEM_HEADS = MEM_WIDTH // HEAD_DIM
SSM_GROUP = 16
SSM_GROUPS = MAIN_WIDTH // SSM_GROUP
SSM_STATE = 64
GROUPS_PER_BLOCK = 8
SSM_BLOCKS = SSM_GROUPS // GROUPS_PER_BLOCK
STATE_COLS = GROUPS_PER_BLOCK * SSM_STATE
EPS = 1e-6
ADAM_LR = 0.001
ADAM_B1 = 0.9
ADAM_B2 = 0.999
ADAM_EPS = 1e-08
ADAM_WD = 0.01
ADAM_STEP = 10
N_CHIPS = 4
LANES = 128
SUBLANES = 8
VMEM_LIMIT_BYTES = 56 * 1024 * 1024
NEG_BIG = -1e30
MESH_AXES = ("x", "y", "c")


def _params(*sem):
    return pltpu.CompilerParams(dimension_semantics=sem if sem else None,
                                vmem_limit_bytes=VMEM_LIMIT_BYTES)


def _sigmoid(x):
    return 1.0 / (1.0 + jnp.exp(-x))


def _gelu(x):
    c = math.sqrt(2.0 / math.pi)
    return 0.5 * x * (1.0 + jnp.tanh(c * (x + 0.044715 * (x * x * x))))


def _gelu_grad(x):
    c = math.sqrt(2.0 / math.pi)
    t = jnp.tanh(c * (x + 0.044715 * (x * x * x)))
    return 0.5 * (1.0 + t) + 0.5 * x * (1.0 - t * t) * (c * (1.0 + 3.0 * 0.044715 * (x * x)))


def _silu_and_grad(z):
    s = _sigmoid(z)
    return z * s, s * (1.0 + z * (1.0 - s))


_TILE_CHOICES = (4096, 3072, 2048, 1536, 1024, 768, 512, 384, 256, LANES)


def _tile(n, cap):
    return next(c for c in _TILE_CHOICES if c <= cap and n % c == 0)


def _mm(a, b, *, name, ta=False, tb=False, out_dtype=F32, shards=1, tm=1024, tn=1024, tk=4096):
    if ta:
        K, M = a.shape
    else:
        M, K = a.shape
    if tb:
        N, kb = b.shape
    else:
        kb, N = b.shape
    assert K == kb, (a.shape, b.shape)
    ns = N // shards
    tm, tn, tk = _tile(M, tm), _tile(ns, tn), _tile(K, tk)
    assert M % tm == 0 and ns % tn == 0 and K % tk == 0 and N % shards == 0
    nk = K // tk
    dn = (((0 if ta else 1,), (1 if tb else 0,)), ((), ()))

    def body(a_ref, b_ref, o_ref, *acc):
        prod = lax.dot_general(a_ref[...].astype(BF16), b_ref[...].astype(BF16), dn, preferred_element_type=F32)
        if nk == 1:
            o_ref[...] = prod.astype(o_ref.dtype)
            return
        acc_ref, = acc
        k = pl.program_id(2)

        @pl.when(k == 0)
        def _():
            acc_ref[...] = jnp.zeros_like(acc_ref)

        acc_ref[...] += prod

        @pl.when(k == nk - 1)
        def _():
            o_ref[...] = acc_ref[...].astype(o_ref.dtype)

    a_spec = (pl.BlockSpec((tk, tm), lambda i, j, k: (k, i)) if ta
              else pl.BlockSpec((tm, tk), lambda i, j, k: (i, k)))
    b_spec = (pl.BlockSpec((tn, tk), lambda i, j, k: (j, k)) if tb
              else pl.BlockSpec((tk, tn), lambda i, j, k: (k, j)))
    if shards == 1:
        out_shape = jax.ShapeDtypeStruct((M, N), out_dtype)
        o_spec = pl.BlockSpec((tm, tn), lambda i, j, k: (i, j))
    else:
        nb = ns // tn
        out_shape = jax.ShapeDtypeStruct((shards, M, ns), out_dtype)
        o_spec = pl.BlockSpec((None, tm, tn), lambda i, j, k: (j // nb, i, j % nb))
    return pl.pallas_call(
        body, name=name, out_shape=out_shape,
        grid=(M // tm, N // tn, nk),
        in_specs=[a_spec, b_spec], out_specs=o_spec,
        scratch_shapes=[] if nk == 1 else [pltpu.VMEM((tm, tn), F32)],
        compiler_params=_params("parallel", "parallel", "arbitrary"),
    )(a, b)


def _rmsnorm_fwd(x, g, *, name, out_dtype=F32, tr=256):
    L, D = x.shape
    tr = min(tr, L)

    def body(x_ref, g_ref, o_ref):
        xf = x_ref[...]
        r = lax.rsqrt(jnp.mean(xf * xf, axis=-1, keepdims=True) + EPS)
        o_ref[...] = (xf * r * g_ref[...]).astype(o_ref.dtype)

    row = pl.BlockSpec((tr, D), lambda i: (i, 0))
    vec = pl.BlockSpec((1, D), lambda i: (0, 0))
    return pl.pallas_call(
        body, name=name, out_shape=jax.ShapeDtypeStruct((L, D), out_dtype),
        grid=(L // tr,), in_specs=[row, vec], out_specs=row,
        compiler_params=_params("parallel"),
    )(x, g.reshape(1, D))


def _post_norm_and_next_norms(o, g_post, res, g_kv, g_pre, *, name, tr=256):
    L, D = o.shape
    tr = min(tr, L)

    def body(o_ref, gp_ref, r_ref, gk_ref, gn_ref, h_ref, kv_ref, hn_ref):
        of = o_ref[...]
        r = lax.rsqrt(jnp.mean(of * of, axis=-1, keepdims=True) + EPS)
        h = r_ref[...] + of * r * gp_ref[...]
        h_ref[...] = h
        hr = h * lax.rsqrt(jnp.mean(h * h, axis=-1, keepdims=True) + EPS)
        kv_ref[...] = (hr * gk_ref[...]).astype(kv_ref.dtype)
        hn_ref[...] = (hr * gn_ref[...]).astype(hn_ref.dtype)

    row = pl.BlockSpec((tr, D), lambda i: (i, 0))
    vec = pl.BlockSpec((1, D), lambda i: (0, 0))
    return pl.pallas_call(
        body, name=name,
        out_shape=(jax.ShapeDtypeStruct((L, D), F32), jax.ShapeDtypeStruct((L, D), BF16),
                   jax.ShapeDtypeStruct((L, D), BF16)),
        grid=(L // tr,), in_specs=[row, vec, row, vec, vec], out_specs=(row, row, row),
        compiler_params=_params("parallel"),
    )(o, g_post.reshape(1, D), res, g_kv.reshape(1, D), g_pre.reshape(1, D))


def _rmsnorm_bwd(x, g, dy, *, name, adds=(), dx_dtype=F32, tr=256):
    L, D = x.shape
    tr = min(tr, L)
    dys = dy if isinstance(dy, tuple) else (dy,)
    n_dy, n_add = len(dys), len(adds)

    def body(*refs):
        x_ref, g_ref = refs[:2]
        dy_refs = refs[2:2 + n_dy]
        add_refs = refs[2 + n_dy:2 + n_dy + n_add]
        dx_ref, dg_ref = refs[2 + n_dy + n_add:]
        xf = x_ref[...]
        dyf = dy_refs[0][...].astype(F32)
        for d_ref in dy_refs[1:]:
            dyf = dyf + d_ref[...].astype(F32)
        r = lax.rsqrt(jnp.mean(xf * xf, axis=-1, keepdims=True) + EPS)
        gy = dyf * g_ref[...]
        c = jnp.mean(xf * gy, axis=-1, keepdims=True) * (r * r * r)
        dx = gy * r - xf * c
        for a_ref in add_refs:
            dx = dx + a_ref[...].astype(F32)
        dx_ref[...] = dx.astype(dx_ref.dtype)

        @pl.when(pl.program_id(0) == 0)
        def _():
            dg_ref[...] = jnp.zeros_like(dg_ref)

        dg_ref[...] += jnp.sum(dyf * xf * r, axis=0, keepdims=True)

    row = pl.BlockSpec((tr, D), lambda i: (i, 0))
    vec = pl.BlockSpec((1, D), lambda i: (0, 0))
    dx, dg = pl.pallas_call(
        body, name=name,
        out_shape=(jax.ShapeDtypeStruct((L, D), dx_dtype), jax.ShapeDtypeStruct((1, D), F32)),
        grid=(L // tr,), in_specs=[row, vec] + [row] * (n_dy + n_add), out_specs=(row, vec),
        compiler_params=_params("arbitrary"),
    )(x, g.reshape(1, D), *dys, *adds)
    return dx, dg.reshape(D)


def _rmsnorm_bwd_pair(x, g1, dy1, g2, dy2, *, name, adds=(), tr=256):
    L, D = x.shape
    tr = min(tr, L)
    dy1s = dy1 if isinstance(dy1, tuple) else (dy1,)
    n1, n_add = len(dy1s), len(adds)

    def body(*refs):
        x_ref, g1_ref, g2_ref = refs[:3]
        dy1_refs = refs[3:3 + n1]
        dy2_ref = refs[3 + n1]
        add_refs = refs[4 + n1:4 + n1 + n_add]
        dx_ref, dg1_ref, dg2_ref = refs[4 + n1 + n_add:]
        xf = x_ref[...]
        d1 = dy1_refs[0][...].astype(F32)
        for d_ref in dy1_refs[1:]:
            d1 = d1 + d_ref[...].astype(F32)
        d2 = dy2_ref[...].astype(F32)
        r = lax.rsqrt(jnp.mean(xf * xf, axis=-1, keepdims=True) + EPS)
        gy = d1 * g1_ref[...] + d2 * g2_ref[...]
        c = jnp.mean(xf * gy, axis=-1, keepdims=True) * (r * r * r)
        dx = gy * r - xf * c
        for a_ref in add_refs:
            dx = dx + a_ref[...].astype(F32)
        dx_ref[...] = dx

        @pl.when(pl.program_id(0) == 0)
        def _():
            dg1_ref[...] = jnp.zeros_like(dg1_ref)
            dg2_ref[...] = jnp.zeros_like(dg2_ref)

        xr = xf * r
        dg1_ref[...] += jnp.sum(d1 * xr, axis=0, keepdims=True)
        dg2_ref[...] += jnp.sum(d2 * xr, axis=0, keepdims=True)

    row = pl.BlockSpec((tr, D), lambda i: (i, 0))
    vec = pl.BlockSpec((1, D), lambda i: (0, 0))
    dx, dg1, dg2 = pl.pallas_call(
        body, name=name,
        out_shape=(jax.ShapeDtypeStruct((L, D), F32), jax.ShapeDtypeStruct((1, D), F32),
                   jax.ShapeDtypeStruct((1, D), F32)),
        grid=(L // tr,), in_specs=[row, vec, vec] + [row] * (n1 + 1 + n_add), out_specs=(row, vec, vec),
        compiler_params=_params("arbitrary"),
    )(x, g1.reshape(1, D), g2.reshape(1, D), *dy1s, dy2, *adds)
    return dx, dg1.reshape(D), dg2.reshape(D)


def _final_norm_loss(o, g, res, target, *, tr=256):
    L, D = o.shape
    tr = min(tr, L)

    def body(o_ref, g_ref, r_ref, t_ref, dh_ref, loss_ref):
        xf = o_ref[...]
        r = lax.rsqrt(jnp.mean(xf * xf, axis=-1, keepdims=True) + EPS)
        e = (r_ref[...] + xf * r * g_ref[...]) - t_ref[...]
        dh_ref[...] = e * (1.0 / D)

        @pl.when(pl.program_id(0) == 0)
        def _():
            loss_ref[...] = jnp.zeros_like(loss_ref)

        loss_ref[...] += jnp.sum(e * e, axis=0, keepdims=True) * (0.5 / D)

    row = pl.BlockSpec((tr, D), lambda i: (i, 0))
    vec = pl.BlockSpec((1, D), lambda i: (0, 0))
    dh, lp = pl.pallas_call(
        body, name="post_norm_1_loss",
        out_shape=(jax.ShapeDtypeStruct((L, D), F32), jax.ShapeDtypeStruct((1, D), F32)),
        grid=(L // tr,), in_specs=[row, vec, row, row], out_specs=(row, vec),
        compiler_params=_params("arbitrary"),
    )(o, g.reshape(1, D), res, target)
    return dh, lp


def _s5_coeffs(lr, li, ls):
    dt = jnp.exp(ls)
    mag = jnp.exp(lr * dt)
    ar = mag * jnp.cos(li * dt)
    ai = mag * jnp.sin(li * dt)
    den = lr * lr + li * li
    cr = ((ar - 1.0) * lr + ai * li) / den
    ci = (ai * lr - (ar - 1.0) * li) / den
    return dt, ar, ai, den, cr, ci


def _s5_prep(lam_re, lam_im, log_step, b_re_t, b_im_t):
    G, P = lam_re.shape
    H = b_re_t.shape[1]

    def body(lr_ref, li_ref, ls_ref, br_ref, bi_ref, ar_ref, ai_ref, bbr_ref, bbi_ref):
        _, ar, ai, _, cr, ci = _s5_coeffs(lr_ref[...], li_ref[...], ls_ref[...])
        ar_ref[...] = ar
        ai_ref[...] = ai
        br, bi = br_ref[...], bi_ref[...]
        crb, cib = cr[:, None, :], ci[:, None, :]
        bbr_ref[...] = crb * br - cib * bi
        bbi_ref[...] = crb * bi + cib * br

    return pl.pallas_call(
        body, name="s5_prep",
        out_shape=(jax.ShapeDtypeStruct((G, P), F32), jax.ShapeDtypeStruct((G, P), F32),
                   jax.ShapeDtypeStruct((G, H, P), F32), jax.ShapeDtypeStruct((G, H, P), F32)),
        compiler_params=_params(),
    )(lam_re, lam_im, log_step.reshape(G, 1), b_re_t, b_im_t)


def _s5_prep_bwd(lam_re, lam_im, log_step, b_re_t, b_im_t, d_ar, d_ai, d_bbr, d_bbi):
    G, P = lam_re.shape
    H = b_re_t.shape[1]

    def body(lr_ref, li_ref, ls_ref, br_ref, bi_ref, dar_ref, dai_ref, dbbr_ref, dbbi_ref,
             dlr_ref, dli_ref, dls_ref, dbr_ref, dbi_ref):
        lr, li = lr_ref[...], li_ref[...]
        dt, ar, ai, den, cr, ci = _s5_coeffs(lr, li, ls_ref[...])
        br, bi = br_ref[...], bi_ref[...]
        gbr, gbi = dbbr_ref[...], dbbi_ref[...]
        crb, cib = cr[:, None, :], ci[:, None, :]
        dbr_ref[...] = crb * gbr + cib * gbi
        dbi_ref[...] = crb * gbi - cib * gbr
        gcr = jnp.sum(br * gbr + bi * gbi, axis=1)
        gci = jnp.sum(br * gbi - bi * gbr, axis=1)
        ilr, ili = lr / den, -li / den
        gar = dar_ref[...] + (ilr * gcr + ili * gci)
        gai = dai_ref[...] + (ilr * gci - ili * gcr)
        qr, qi = cr * ilr - ci * ili, cr * ili + ci * ilr
        glr = -(qr * gcr + qi * gci)
        gli = -(qr * gci - qi * gcr)
        glr = glr + dt * (ar * gar + ai * gai)
        gli = gli + dt * (ar * gai - ai * gar)
        wr, wi = lr * ar - li * ai, lr * ai + li * ar
        gdt = jnp.sum(wr * gar + wi * gai, axis=1, keepdims=True)
        dlr_ref[...] = glr
        dli_ref[...] = gli
        dls_ref[...] = gdt * dt

    return pl.pallas_call(
        body, name="s5_prep_bwd",
        out_shape=(jax.ShapeDtypeStruct((G, P), F32), jax.ShapeDtypeStruct((G, P), F32),
                   jax.ShapeDtypeStruct((G, 1), F32),
                   jax.ShapeDtypeStruct((G, H, P), F32), jax.ShapeDtypeStruct((G, H, P), F32)),
        compiler_params=_params(),
    )(lam_re, lam_im, log_step.reshape(G, 1), b_re_t, b_im_t, d_ar, d_ai, d_bbr, d_bbi)


def _s5_block_mats(bbr_t, bbi_t, c_re, c_im):
    bmat = _s5_expand(bbr_t, bbi_t)
    cmat = jnp.transpose(_s5_expand(c_re, -c_im), (0, 2, 1))
    return bmat.astype(BF16), cmat.astype(BF16)


def _s5_diag_mask():
    r = lax.broadcasted_iota(jnp.int32, (LANES, 2 * STATE_COLS), 0) // SSM_GROUP
    c = (lax.broadcasted_iota(jnp.int32, (LANES, 2 * STATE_COLS), 1) % STATE_COLS) // SSM_STATE
    return (r == c).astype(F32)


def _s5_expand(re, im):
    re = jnp.tile(re.reshape(SSM_BLOCKS, LANES, SSM_STATE), (1, 1, GROUPS_PER_BLOCK))
    im = jnp.tile(im.reshape(SSM_BLOCKS, LANES, SSM_STATE), (1, 1, GROUPS_PER_BLOCK))
    return jnp.concatenate([re, im], axis=-1) * _s5_diag_mask()[None]


def _s5_unfold(dmat):
    d = dmat.reshape(SSM_GROUPS, SSM_GROUP, 2, SSM_STATE)
    return jnp.transpose(d, (2, 0, 1, 3))


def _s5_a_rows(ar, ai):
    a = jnp.concatenate([ar.reshape(SSM_BLOCKS, STATE_COLS), ai.reshape(SSM_BLOCKS, STATE_COLS)], axis=1)
    return jnp.broadcast_to(a[:, None, :], (SSM_BLOCKS, SUBLANES, 2 * STATE_COLS))


def _to_step_major(src_ref, dst_ref, seg):
    for s in range(SUBLANES):
        dst_ref[pl.ds(s, seg, stride=SUBLANES), :] = src_ref[pl.ds(seg * s, seg), :]


def _segment_rows(ref, s, seg):
    return ref[pl.ds(s, seg, stride=SUBLANES), :]


def _cmul(ar, ai, xr, xi):
    return ar * xr - ai * xi, ar * xi + ai * xr


def _s5_tables(a_ref, pw_s, pwr_s, S, seg):
    ar, ai = a_ref[:, :S], a_ref[:, S:]

    def step(i, c):
        pr, pi = c
        pw_s[i, :, :S] = pr
        pw_s[i, :, S:] = pi
        nr, ni = _cmul(ar, ai, pr, pi)
        pwr_s[seg - 1 - i, :, :S] = nr
        pwr_s[seg - 1 - i, :, S:] = ni
        return nr, ni

    pr, pi = lax.fori_loop(0, seg, step, (jnp.ones_like(ar), jnp.zeros_like(ai)))
    pw_s[seg, :, :S] = pr
    pw_s[seg, :, S:] = pi


def _s5_fwd(proj, bmat, cmat, a_rows, d_skip, *, tc=512):
    L = proj.shape[0]
    tc = min(tc, L)
    nt = L // tc
    seg = tc // SUBLANES
    S = STATE_COLS

    def body(u_ref, b_ref, c_ref, a_ref, d_ref, y_ref, yg_ref, xp_ref,
             bu_s, xp_s, pw_s, pwr_s, carry_s, e_s, up_s, yc_s):
        @pl.when(pl.program_id(1) == 0)
        def _():
            carry_s[...] = jnp.zeros_like(carry_s)
            _s5_tables(a_ref, pw_s, pwr_s, S, seg)

        ar, ai = a_ref[:, :S], a_ref[:, S:]
        _to_step_major(u_ref, up_s, seg)
        bu = jnp.dot(up_s[...].astype(BF16), b_ref[...], preferred_element_type=F32)
        bu_s[...] = bu.reshape(seg, SUBLANES, 2 * S)

        def step(i, carry):
            cr, ci = carry
            xp_s[i, :, :S] = cr
            xp_s[i, :, S:] = ci
            return ar * cr - ai * ci + bu_s[i, :, :S], ar * ci + ai * cr + bu_s[i, :, S:]

        zero = jnp.zeros((SUBLANES, S), F32)
        fr, fi = lax.fori_loop(0, seg, step, (zero, zero))
        pr, pi = pw_s[seg, 0:1, :S], pw_s[seg, 0:1, S:]
        er, ei = carry_s[0:1, :S], carry_s[0:1, S:]
        for s in range(SUBLANES):
            e_s[s:s + 1, :S] = er
            e_s[s:s + 1, S:] = ei
            tr, ti = _cmul(pr, pi, er, ei)
            er, ei = fr[s:s + 1] + tr, fi[s:s + 1] + ti
        carry_s[0:1, :S] = er
        carry_s[0:1, S:] = ei
        pw = pw_s[0:seg]
        tr, ti = _cmul(pw[:, :, :S], pw[:, :, S:], e_s[:, :S][None], e_s[:, S:][None])
        xl = xp_s[...]
        xp = jnp.concatenate([xl[:, :, :S] + tr, xl[:, :, S:] + ti], axis=-1).reshape(tc, 2 * S)
        xp_ref[...] = xp
        a1r, a1i = ar[0:1], ai[0:1]
        x_re = a1r * xp[:, :S] - a1i * xp[:, S:] + bu[:, :S]
        x_im = a1r * xp[:, S:] + a1i * xp[:, :S] + bu[:, S:]
        xs = jnp.concatenate([x_re, x_im], axis=1).astype(BF16)
        yc_s[...] = jnp.dot(xs, c_ref[...], preferred_element_type=F32)
        for s in range(SUBLANES):
            rows = pl.ds(seg * s, seg)
            y = _segment_rows(yc_s, s, seg) + d_ref[...] * u_ref[rows, :]
            y_ref[rows, :] = y
            yg_ref[rows, :] = _gelu(y).astype(BF16)

    return pl.pallas_call(
        body, name="s5_fwd",
        out_shape=(jax.ShapeDtypeStruct((L, MAIN_WIDTH), F32),
                   jax.ShapeDtypeStruct((L, MAIN_WIDTH), BF16),
                   jax.ShapeDtypeStruct((L, SSM_BLOCKS * 2 * S), F32)),
        grid=(SSM_BLOCKS, nt),
        in_specs=[pl.BlockSpec((tc, LANES), lambda b, t: (t, b)),
                  pl.BlockSpec((None, LANES, 2 * S), lambda b, t: (b, 0, 0)),
                  pl.BlockSpec((None, 2 * S, LANES), lambda b, t: (b, 0, 0)),
                  pl.BlockSpec((None, SUBLANES, 2 * S), lambda b, t: (b, 0, 0)),
                  pl.BlockSpec((1, LANES), lambda b, t: (0, b))],
        out_specs=(pl.BlockSpec((tc, LANES), lambda b, t: (t, b)),
                   pl.BlockSpec((tc, LANES), lambda b, t: (t, b)),
                   pl.BlockSpec((tc, 2 * S), lambda b, t: (t, b))),
        scratch_shapes=[pltpu.VMEM((seg, SUBLANES, 2 * S), F32),
                        pltpu.VMEM((seg, SUBLANES, 2 * S), F32),
                        pltpu.VMEM((seg + 1, SUBLANES, 2 * S), F32),
                        pltpu.VMEM((seg, SUBLANES, 2 * S), F32),
                        pltpu.VMEM((SUBLANES, 2 * S), F32),
                        pltpu.VMEM((SUBLANES, 2 * S), F32),
                        pltpu.VMEM((tc, LANES), F32),
                        pltpu.VMEM((tc, LANES), F32)],
        compiler_params=_params("parallel", "arbitrary"),
    )(proj, bmat, cmat, a_rows, d_skip.reshape(1, MAIN_WIDTH))


def _s5_bwd(proj, dyg_a, dyg_b, y, xp, bmat, cmat, a_rows, d_skip, dproj, *, tc=512):
    L = proj.shape[0]
    tc = min(tc, L)
    nt = L // tc
    seg = tc // SUBLANES
    S = STATE_COLS
    nn = (((1,), (1,)), ((), ()))
    tn = (((0,), (0,)), ((), ()))

    def fold_diagonal(acc_ref, mask_ref, fold_ref):
        x = acc_ref[...] * mask_ref[...]
        hi = x.astype(BF16)
        rest = x - hi.astype(F32)
        mid = rest.astype(BF16)
        low = (rest - mid.astype(F32)).astype(BF16)
        return sum(jnp.dot(piece, fold_ref[...], preferred_element_type=F32) for piece in (hi, mid, low))

    def body(u_ref, dyga_ref, dygb_ref, y_ref, xp_ref, b_ref, c_ref, a_ref, d_ref, mask_ref, fold_ref, dp_hbm,
             du_ref, dbd_ref, dcd_ref, da_ref, dd_ref,
             dl_s, pw_s, pwr_s, carry_s, e_s, up_s, dy_s, dyp_s, dup_s, db_ref, dc_ref):
        @pl.when(pl.program_id(1) == 0)
        def _():
            carry_s[...] = jnp.zeros_like(carry_s)
            db_ref[...] = jnp.zeros_like(db_ref)
            dc_ref[...] = jnp.zeros_like(dc_ref)
            da_ref[...] = jnp.zeros_like(da_ref)
            dd_ref[...] = jnp.zeros_like(dd_ref)
            _s5_tables(a_ref, pw_s, pwr_s, S, seg)

        ar, ai = a_ref[:, :S], a_ref[:, S:]
        a1r, a1i = ar[0:1], ai[0:1]
        u = u_ref[...]
        dy = (dyga_ref[...] + dygb_ref[...]) * _gelu_grad(y_ref[...])
        dy_s[...] = dy
        xp = xp_ref[...]
        _to_step_major(u_ref, up_s, seg)
        _to_step_major(dy_s, dyp_s, seg)
        ubp = up_s[...].astype(BF16)
        dyp = dyp_s[...].astype(BF16)
        bu = jnp.dot(ubp, b_ref[...], preferred_element_type=F32)
        x_re = a1r * xp[:, :S] - a1i * xp[:, S:] + bu[:, :S]
        x_im = a1r * xp[:, S:] + a1i * xp[:, :S] + bu[:, S:]
        xs = jnp.concatenate([x_re, x_im], axis=1).astype(BF16)
        dc_ref[...] += lax.dot_general(dyp, xs, tn, preferred_element_type=F32)
        dx = lax.dot_general(dyp, c_ref[...], nn, preferred_element_type=F32)
        dl_s[...] = dx.reshape(seg, SUBLANES, 2 * S)

        def step(k, carry):
            cr, ci = carry
            i = seg - 1 - k
            lr = dl_s[i, :, :S] + (ar * cr + ai * ci)
            li = dl_s[i, :, S:] + (ar * ci - ai * cr)
            dl_s[i, :, :S] = lr
            dl_s[i, :, S:] = li
            return lr, li

        zero = jnp.zeros((SUBLANES, S), F32)
        fr, fi = lax.fori_loop(0, seg, step, (zero, zero))
        pr, pi = pw_s[seg, 0:1, :S], pw_s[seg, 0:1, S:]
        er, ei = carry_s[0:1, :S], carry_s[0:1, S:]
        for s in range(SUBLANES - 1, -1, -1):
            e_s[s:s + 1, :S] = er
            e_s[s:s + 1, S:] = ei
            er, ei = fr[s:s + 1] + (pr * er + pi * ei), fi[s:s + 1] + (pr * ei - pi * er)
        carry_s[0:1, :S] = er
        carry_s[0:1, S:] = ei
        er, ei = e_s[:, :S][None], e_s[:, S:][None]
        pw = pwr_s[...]
        pwr, pwi = pw[:, :, :S], pw[:, :, S:]
        ll = dl_s[...]
        lam = jnp.concatenate([ll[:, :, :S] + (pwr * er + pwi * ei), ll[:, :, S:] + (pwr * ei - pwi * er)],
                              axis=-1).reshape(tc, 2 * S)
        l_re, l_im = lam[:, :S], lam[:, S:]
        da_ref[0:1, :S] += jnp.sum(l_re * xp[:, :S] + l_im * xp[:, S:], axis=0, keepdims=True)
        da_ref[0:1, S:] += jnp.sum(l_im * xp[:, :S] - l_re * xp[:, S:], axis=0, keepdims=True)
        lamb = lam.astype(BF16)
        dup_s[...] = lax.dot_general(lamb, b_ref[...], nn, preferred_element_type=F32)
        for s in range(SUBLANES):
            rows = pl.ds(seg * s, seg)
            du = _segment_rows(dup_s, s, seg) + d_ref[...] * dy_s[rows, :]
            du_ref[rows, :] = du.astype(du_ref.dtype)
        db_ref[...] += lax.dot_general(ubp, lamb, tn, preferred_element_type=F32)
        dd_ref[0:1, :] += jnp.sum(dy * u, axis=0, keepdims=True)

        @pl.when(pl.program_id(1) == nt - 1)
        def _():
            dbd_ref[...] = fold_diagonal(db_ref, mask_ref, fold_ref)
            dcd_ref[...] = fold_diagonal(dc_ref, mask_ref, fold_ref)

    rev = lambda b, t: (nt - 1 - t, b)
    col = jnp.arange(2 * S)
    fold = ((col // S * SSM_STATE + col % SSM_STATE)[:, None] == jnp.arange(LANES)[None, :]).astype(BF16)
    return pl.pallas_call(
        body, name="s5_bwd",
        out_shape=(jax.ShapeDtypeStruct(dproj.shape, dproj.dtype),
                   jax.ShapeDtypeStruct((SSM_BLOCKS, LANES, LANES), F32),
                   jax.ShapeDtypeStruct((SSM_BLOCKS, LANES, LANES), F32),
                   jax.ShapeDtypeStruct((SSM_BLOCKS, SUBLANES, 2 * S), F32),
                   jax.ShapeDtypeStruct((SUBLANES, MAIN_WIDTH), F32)),
        input_output_aliases={11: 0},
        grid=(SSM_BLOCKS, nt),
        in_specs=[pl.BlockSpec((tc, LANES), rev),
                  pl.BlockSpec((tc, LANES), rev),
                  pl.BlockSpec((tc, LANES), rev),
                  pl.BlockSpec((tc, LANES), rev),
                  pl.BlockSpec((tc, 2 * S), rev),
                  pl.BlockSpec((None, LANES, 2 * S), lambda b, t: (b, 0, 0)),
                  pl.BlockSpec((None, 2 * S, LANES), lambda b, t: (b, 0, 0)),
                  pl.BlockSpec((None, SUBLANES, 2 * S), lambda b, t: (b, 0, 0)),
                  pl.BlockSpec((1, LANES), lambda b, t: (0, b)),
                  pl.BlockSpec((LANES, 2 * S), lambda b, t: (0, 0)),
                  pl.BlockSpec((2 * S, LANES), lambda b, t: (0, 0)),
                  _ANY],
        out_specs=(pl.BlockSpec((tc, LANES), rev),
                   pl.BlockSpec((None, LANES, LANES), lambda b, t: (b, 0, 0)),
                   pl.BlockSpec((None, LANES, LANES), lambda b, t: (b, 0, 0)),
                   pl.BlockSpec((None, SUBLANES, 2 * S), lambda b, t: (b, 0, 0)),
                   pl.BlockSpec((SUBLANES, LANES), lambda b, t: (0, b))),
        scratch_shapes=[pltpu.VMEM((seg, SUBLANES, 2 * S), F32),
                        pltpu.VMEM((seg + 1, SUBLANES, 2 * S), F32),
                        pltpu.VMEM((seg, SUBLANES, 2 * S), F32),
                        pltpu.VMEM((SUBLANES, 2 * S), F32),
                        pltpu.VMEM((SUBLANES, 2 * S), F32),
                        pltpu.VMEM((tc, LANES), F32),
                        pltpu.VMEM((tc, LANES), F32),
                        pltpu.VMEM((tc, LANES), F32),
                        pltpu.VMEM((tc, LANES), F32),
                        pltpu.VMEM((LANES, 2 * S), F32),
                        pltpu.VMEM((LANES, 2 * S), F32)],
        compiler_params=_params("parallel", "arbitrary"),
    )(proj, dyg_a, dyg_b, y, xp, bmat, cmat, a_rows, d_skip.reshape(1, MAIN_WIDTH), _s5_diag_mask(), fold, dproj)


_Z_COLS = slice(MAIN_WIDTH, 2 * MAIN_WIDTH)
_ZM_COLS = slice(2 * MAIN_WIDTH + MEM_WIDTH, IN_WIDTH)


def _proj_rows(tr):
    return pl.BlockSpec((tr, IN_WIDTH), lambda i: (i, 0))


def _row_specs(tr):
    main = pl.BlockSpec((tr, MAIN_WIDTH), lambda i: (i, 0))
    z = pl.BlockSpec((tr, MAIN_WIDTH), lambda i: (i, 1))
    zm = pl.BlockSpec((tr, MEM_WIDTH), lambda i: (i, IN_WIDTH // MEM_WIDTH - 1))
    mem = pl.BlockSpec((tr, MEM_WIDTH), lambda i: (i, 0))
    cat = pl.BlockSpec((tr, D_MODEL), lambda i: (i, 0))
    vec = pl.BlockSpec((1, MAIN_WIDTH), lambda i: (0, 0))
    return main, z, zm, mem, cat, vec


def _gate_a_fwd(y, t, b_glu, proj, o_mem, *, tr=256):
    L = y.shape[0]
    tr = min(tr, L)

    def body(y_ref, t_ref, b_ref, z_ref, zm_ref, om_ref, o_ref):
        yg = _gelu(y_ref[...])
        sz, _ = _silu_and_grad(z_ref[...])
        o_ref[:, :MAIN_WIDTH] = (yg * _sigmoid(t_ref[...] + b_ref[...]) * sz).astype(BF16)
        szm, _ = _silu_and_grad(zm_ref[...])
        o_ref[:, MAIN_WIDTH:] = (om_ref[...] * szm).astype(BF16)

    main, z, zm, mem, cat, vec = _row_specs(tr)
    return pl.pallas_call(
        body, name="gate_a_fwd", out_shape=jax.ShapeDtypeStruct((L, D_MODEL), BF16),
        grid=(L // tr,), in_specs=[main, main, vec, z, zm, mem], out_specs=cat,
        compiler_params=_params("parallel"),
    )(y, t, b_glu.reshape(1, MAIN_WIDTH), proj, proj, o_mem)


def _gate_a_bwd(dcat, y, t, b_glu, proj, o_mem, *, tr=256):
    L = y.shape[0]
    tr = min(tr, L)

    def body(dc_ref, y_ref, t_ref, b_ref, z_ref, zm_ref, om_ref,
             dp_ref, dt_ref, dyg_ref, dom_ref, db_ref):
        dmain = dc_ref[:, :MAIN_WIDTH]
        dmemo = dc_ref[:, MAIN_WIDTH:]
        yg = _gelu(y_ref[...])
        sg = _sigmoid(t_ref[...] + b_ref[...])
        sz, gz = _silu_and_grad(z_ref[...])
        dp_ref[:, _Z_COLS] = (dmain * (yg * sg) * gz).astype(BF16)
        dy2 = dmain * sz
        dyg_ref[...] = dy2 * sg
        dt = dy2 * yg * (sg * (1.0 - sg))
        dt_ref[...] = dt.astype(BF16)

        @pl.when(pl.program_id(0) == 0)
        def _():
            db_ref[...] = jnp.zeros_like(db_ref)

        db_ref[...] += jnp.sum(dt, axis=0, keepdims=True)
        szm, gzm = _silu_and_grad(zm_ref[...])
        dom_ref[...] = dmemo * szm
        dp_ref[:, _ZM_COLS] = (dmemo * om_ref[...] * gzm).astype(BF16)

    main, z, zm, mem, cat, vec = _row_specs(tr)
    outs = pl.pallas_call(
        body, name="gate_a_bwd",
        out_shape=(jax.ShapeDtypeStruct((L, IN_WIDTH), BF16),
                   jax.ShapeDtypeStruct((L, MAIN_WIDTH), BF16), jax.ShapeDtypeStruct((L, MAIN_WIDTH), F32),
                   jax.ShapeDtypeStruct((L, MEM_WIDTH), F32), jax.ShapeDtypeStruct((1, MAIN_WIDTH), F32)),
        grid=(L // tr,), in_specs=[cat, main, main, vec, z, zm, mem],
        out_specs=(_proj_rows(tr), main, main, mem, vec),
        compiler_params=_params("arbitrary"),
    )(dcat, y, t, b_glu.reshape(1, MAIN_WIDTH), proj, proj, o_mem)
    return outs


def _gate_b_fwd(att, proj, o_mem, *, tr=256):
    L = att.shape[0]
    tr = min(tr, L)

    def body(a_ref, z_ref, zm_ref, om_ref, o_ref):
        sz, _ = _silu_and_grad(z_ref[...])
        o_ref[:, :MAIN_WIDTH] = (a_ref[...] * sz).astype(BF16)
        szm, _ = _silu_and_grad(zm_ref[...])
        o_ref[:, MAIN_WIDTH:] = (om_ref[...] * szm).astype(BF16)

    main, z, zm, mem, cat, _ = _row_specs(tr)
    return pl.pallas_call(
        body, name="gate_b_fwd", out_shape=jax.ShapeDtypeStruct((L, D_MODEL), BF16),
        grid=(L // tr,), in_specs=[main, z, zm, mem], out_specs=cat,
        compiler_params=_params("parallel"),
    )(att, proj, proj, o_mem)


def _gate_b_bwd(dcat, att, proj, o_mem, *, tr=256):
    L = att.shape[0]
    tr = min(tr, L)

    def body(dc_ref, a_ref, z_ref, zm_ref, om_ref, da_ref, dp_ref, dom_ref, dl_ref):
        dmain = dc_ref[:, :MAIN_WIDTH]
        dmemo = dc_ref[:, MAIN_WIDTH:]
        att = a_ref[...]
        sz, gz = _silu_and_grad(z_ref[...])
        datt = dmain * sz
        da_ref[...] = datt
        dp_ref[:, _Z_COLS] = (dmain * att * gz).astype(BF16)
        szm, gzm = _silu_and_grad(zm_ref[...])
        dom_ref[...] = dmemo * szm
        dp_ref[:, _ZM_COLS] = (dmemo * om_ref[...] * gzm).astype(BF16)
        prod = datt * att
        for h in range(FOX_HEADS):
            dl_ref[h] = jnp.sum(prod[:, h * HEAD_DIM:(h + 1) * HEAD_DIM], axis=1, keepdims=True)

    main, z, zm, mem, cat, _ = _row_specs(tr)
    delta = pl.BlockSpec((FOX_HEADS, tr, 1), lambda i: (0, i, 0))
    return pl.pallas_call(
        body, name="gate_b_bwd",
        out_shape=(jax.ShapeDtypeStruct((L, MAIN_WIDTH), F32), jax.ShapeDtypeStruct((L, IN_WIDTH), BF16),
                   jax.ShapeDtypeStruct((L, MEM_WIDTH), F32), jax.ShapeDtypeStruct((FOX_HEADS, L, 1), F32)),
        grid=(L // tr,), in_specs=[cat, main, z, zm, mem], out_specs=(main, _proj_rows(tr), mem, delta),
        compiler_params=_params("parallel"),
    )(dcat, att, proj, proj, o_mem)


_MEM_Q_COL = (2 * MAIN_WIDTH) // HEAD_DIM
_NT = (((1,), (1,)), ((), ()))
_TN = (((0,), (0,)), ((), ()))


def _mem_probs(q_ref, k_ref):
    qs = (q_ref[...] * (HEAD_DIM ** -0.5)).astype(BF16)
    s = lax.dot_general(qs, k_ref[...].astype(BF16), _NT, preferred_element_type=F32)
    e = jnp.exp(s - jnp.max(s, axis=-1, keepdims=True))
    return qs, e / jnp.sum(e, axis=-1, keepdims=True)


def _mem_attn_fwd(proj, kvm, *, tq=2048):
    L = proj.shape[0]
    tq = min(tq, L)

    def body(q_ref, k_ref, v_ref, o_ref):
        _, p = _mem_probs(q_ref, k_ref)
        o_ref[...] = jnp.dot(p.astype(BF16), v_ref[...].astype(BF16), preferred_element_type=F32)

    return pl.pallas_call(
        body, name="mem_attn_fwd", out_shape=jax.ShapeDtypeStruct((L, MEM_WIDTH), F32),
        grid=(MEM_HEADS, L // tq),
        in_specs=[pl.BlockSpec((tq, HEAD_DIM), lambda h, i: (i, _MEM_Q_COL + h)),
                  pl.BlockSpec((N_MEM, HEAD_DIM), lambda h, i: (0, h)),
                  pl.BlockSpec((N_MEM, HEAD_DIM), lambda h, i: (0, MEM_HEADS + h))],
        out_specs=pl.BlockSpec((tq, HEAD_DIM), lambda h, i: (i, h)),
        compiler_params=_params("parallel", "parallel"),
    )(proj, kvm, kvm)


def _mem_attn_bwd(proj, kvm, do, dproj, *, tq=2048):
    L = proj.shape[0]
    tq = min(tq, L)

    def body(q_ref, k_ref, v_ref, do_ref, dp_hbm, dq_ref, dk_ref, dv_ref):
        @pl.when(pl.program_id(1) == 0)
        def _():
            dk_ref[...] = jnp.zeros_like(dk_ref)
            dv_ref[...] = jnp.zeros_like(dv_ref)

        qs, p = _mem_probs(q_ref, k_ref)
        dob = do_ref[...].astype(BF16)
        dp = lax.dot_general(dob, v_ref[...].astype(BF16), _NT, preferred_element_type=F32)
        ds = p * (dp - jnp.sum(p * dp, axis=-1, keepdims=True))
        dsb = ds.astype(BF16)
        dq = jnp.dot(dsb, k_ref[...].astype(BF16), preferred_element_type=F32) * (HEAD_DIM ** -0.5)
        dq_ref[...] = dq.astype(BF16)
        dk_ref[...] += lax.dot_general(dsb, qs, _TN, preferred_element_type=F32)
        dv_ref[...] += lax.dot_general(p.astype(BF16), dob, _TN, preferred_element_type=F32)

    dproj, dk, dv = pl.pallas_call(
        body, name="mem_attn_bwd",
        out_shape=(jax.ShapeDtypeStruct(dproj.shape, dproj.dtype),
                   jax.ShapeDtypeStruct((N_MEM, MEM_WIDTH), F32),
                   jax.ShapeDtypeStruct((N_MEM, MEM_WIDTH), F32)),
        grid=(MEM_HEADS, L // tq),
        in_specs=[pl.BlockSpec((tq, HEAD_DIM), lambda h, i: (i, _MEM_Q_COL + h)),
                  pl.BlockSpec((N_MEM, HEAD_DIM), lambda h, i: (0, h)),
                  pl.BlockSpec((N_MEM, HEAD_DIM), lambda h, i: (0, MEM_HEADS + h)),
                  pl.BlockSpec((tq, HEAD_DIM), lambda h, i: (i, h)),
                  _ANY],
        out_specs=(pl.BlockSpec((tq, HEAD_DIM), lambda h, i: (i, _MEM_Q_COL + h)),
                   pl.BlockSpec((N_MEM, HEAD_DIM), lambda h, i: (0, h)),
                   pl.BlockSpec((N_MEM, HEAD_DIM), lambda h, i: (0, h))),
        input_output_aliases={4: 0},
        compiler_params=_params("parallel", "arbitrary"),
    )(proj, kvm, kvm, do, dproj)
    return dproj, jnp.concatenate([dk, dv], axis=1)


def _tile_cumsum(x, row, reverse):
    for sh in (1, 2, 4):
        if reverse:
            x = x + jnp.where(row < SUBLANES - sh, pltpu.roll(x, SUBLANES - sh, 0), 0.0)
        else:
            x = x + jnp.where(row >= sh, pltpu.roll(x, sh, 0), 0.0)
    return x


def _fgate_fwd(pre, b_pad):
    L = pre.shape[0]
    n8 = L // SUBLANES

    def body(p_ref, b_ref, o_ref):
        row = lax.broadcasted_iota(jnp.int32, (SUBLANES, LANES), 0)
        b = b_ref[...]

        def step(i, carry):
            x = p_ref[i] + b
            logf = jnp.minimum(x, 0.0) - jnp.log(1.0 + jnp.exp(-jnp.abs(x)))
            t = _tile_cumsum(logf, row, False) + carry
            o_ref[i] = t
            return t[SUBLANES - 1:SUBLANES, :]

        lax.fori_loop(0, n8, step, jnp.zeros((1, LANES), F32))

    out = pl.pallas_call(
        body, name="fgate_fwd", out_shape=jax.ShapeDtypeStruct((n8, SUBLANES, LANES), F32),
        compiler_params=_params(),
    )(pre.reshape(n8, SUBLANES, LANES), b_pad.reshape(1, LANES))
    return out.reshape(L, LANES)


def _fgate_bwd(dfcum, pre, b_pad):
    L = pre.shape[0]
    n8 = L // SUBLANES

    def body(d_ref, p_ref, b_ref, o_ref, s_ref):
        row = lax.broadcasted_iota(jnp.int32, (SUBLANES, LANES), 0)
        b = b_ref[...]

        def step(k, carry):
            c, acc = carry
            i = n8 - 1 - k
            t = _tile_cumsum(d_ref[i], row, True) + c
            dpre = t * _sigmoid(-(p_ref[i] + b))
            o_ref[i] = dpre
            return t[0:1, :], acc + dpre

        _, acc = lax.fori_loop(0, n8, step, (jnp.zeros((1, LANES), F32), jnp.zeros((SUBLANES, LANES), F32)))
        s_ref[...] = jnp.sum(acc, axis=0, keepdims=True)

    dpre, db = pl.pallas_call(
        body, name="fgate_bwd",
        out_shape=(jax.ShapeDtypeStruct((n8, SUBLANES, LANES), F32), jax.ShapeDtypeStruct((1, LANES), F32)),
        compiler_params=_params(),
    )(dfcum.reshape(n8, SUBLANES, LANES), pre.reshape(n8, SUBLANES, LANES), b_pad.reshape(1, LANES))
    return dpre.reshape(L, LANES), db


FOX_BLOCK = 1024


def _fox_scores(qs, k, fk, diagonal, row0=0):
    s = lax.dot_general(qs, k, _NT, preferred_element_type=F32) - fk
    if diagonal:
        row = row0 + lax.broadcasted_iota(jnp.int32, s.shape, 0)
        col = lax.broadcasted_iota(jnp.int32, s.shape, 1)
        s = jnp.where(row >= col, s, NEG_BIG)
    return s


def _fox_diagonal_parts(tq):
    half = tq // 2
    return ((slice(0, half), half), (slice(half, tq), tq))


def _fox_specs(tq, L):
    nq = L // tq
    return dict(
        rows=lambda off: pl.BlockSpec((tq, HEAD_DIM), lambda h, i: (i, off + h)),
        seq=lambda off: pl.BlockSpec((L, HEAD_DIM), lambda h, i: (0, off + h)),
        col=pl.BlockSpec((None, None, tq, 1), lambda h, i: (h, i, 0, 0)),
        col_all=pl.BlockSpec((None, nq, tq, 1), lambda h, i: (h, 0, 0, 0)),
        row=pl.BlockSpec((None, None, 1, tq), lambda h, i: (h, i, 0, 0)),
        row_all=pl.BlockSpec((None, nq, 1, tq), lambda h, i: (h, 0, 0, 0)))


FOX_FWD_HEADS = 2
FOX_FWD_BLOCK = 1024


def _fox_fwd(proj, kv, fk):
    L = proj.shape[0]
    tq = min(FOX_FWD_BLOCK, L)
    nq = L // tq
    nh = FOX_FWD_HEADS
    W = nh * HEAD_DIM
    lse_shape = fk.shape[:2] + (fk.shape[3], 1)
    fk = fk.reshape(FOX_HEADS, nq, 1, tq)

    def body(q_ref, k_ref, v_ref, fk_ref, o_ref, lse_ref, m_s, l_s, acc_s):
        qi = pl.program_id(1)
        cols = [slice(a * HEAD_DIM, (a + 1) * HEAD_DIM) for a in range(nh)]
        qs = [(q_ref[:, cs] * (HEAD_DIM ** -0.5)).astype(BF16) for cs in cols]
        m_s[...] = jnp.full_like(m_s, NEG_BIG)
        l_s[...] = jnp.zeros_like(l_s)
        acc_s[...] = jnp.zeros_like(acc_s)

        def block(j, diagonal):
            r0 = pl.multiple_of(j * tq, tq)
            for a, cs in enumerate(cols):
                s = _fox_scores(qs[a], k_ref[pl.ds(r0, tq), cs], fk_ref[a, j], diagonal)
                m_new = jnp.maximum(m_s[a], jnp.max(s, axis=-1, keepdims=True))
                alpha = jnp.exp(m_s[a] - m_new)
                p = jnp.exp(s - m_new)
                l_s[a] = alpha * l_s[a] + jnp.sum(p, axis=-1, keepdims=True)
                acc_s[a] = alpha * acc_s[a] + jnp.dot(p.astype(BF16), v_ref[pl.ds(r0, tq), cs],
                                                      preferred_element_type=F32)
                m_s[a] = m_new

        def below(j, carry):
            block(j, False)
            return carry

        lax.fori_loop(0, qi, below, 0)
        block(qi, True)
        for a, cs in enumerate(cols):
            o_ref[:, cs] = acc_s[a] / l_s[a]
            lse_ref[a] = m_s[a] + jnp.log(l_s[a])

    att, lse = pl.pallas_call(
        body, name="fox_fwd",
        out_shape=(jax.ShapeDtypeStruct((L, MAIN_WIDTH), F32),
                   jax.ShapeDtypeStruct((FOX_HEADS, nq, tq, 1), F32)),
        grid=(FOX_HEADS // nh, nq),
        in_specs=[pl.BlockSpec((tq, W), lambda h, i: (i, h)),
                  pl.BlockSpec((L, W), lambda h, i: (0, h)),
                  pl.BlockSpec((L, W), lambda h, i: (0, FOX_HEADS // nh + h)),
                  pl.BlockSpec((nh, nq, 1, tq), lambda h, i: (h, 0, 0, 0))],
        out_specs=(pl.BlockSpec((tq, W), lambda h, i: (i, h)),
                   pl.BlockSpec((nh, None, tq, 1), lambda h, i: (h, i, 0, 0))),
        scratch_shapes=[pltpu.VMEM((nh, tq, 1), F32), pltpu.VMEM((nh, tq, 1), F32),
                        pltpu.VMEM((nh, tq, HEAD_DIM), F32)],
        compiler_params=_params("parallel", "parallel"),
    )(proj, kv, kv, fk)
    return att, lse.reshape(lse_shape)


def _fox_bwd(proj, kv, fk, lse, delta, datt, dproj):
    L = proj.shape[0]
    tq = min(FOX_BLOCK, L)
    nq = L // tq
    sp = _fox_specs(tq, L)

    def body(q_ref, k_ref, v_ref, fk_ref, lse_ref, dl_ref, do_ref, dp_hbm,
             dq_ref, dk_ref, dv_ref, dfq_ref, dfk_ref, dk_s, dv_s, df_s, dq_s, dfq_s):
        ki = pl.program_id(1)

        @pl.when(ki == 0)
        def _():
            dq_s[...] = jnp.zeros_like(dq_s)
            dfq_s[...] = jnp.zeros_like(dfq_s)

        k, v, fk = k_ref[...], v_ref[...], fk_ref[...]
        dk_s[...] = jnp.zeros_like(dk_s)
        dv_s[...] = jnp.zeros_like(dv_s)
        df_s[...] = jnp.zeros_like(df_s)

        def block(i, rows, width, diagonal):
            n = rows.stop - rows.start
            r0 = pl.multiple_of(i * tq + rows.start, n)
            qs = (q_ref[pl.ds(r0, n), :] * (HEAD_DIM ** -0.5)).astype(BF16)
            dob = do_ref[pl.ds(r0, n), :].astype(BF16)
            kw, vw = k[:width], v[:width]
            p = jnp.exp(_fox_scores(qs, kw, fk[:, :width], diagonal, rows.start) - lse_ref[i][rows])
            dp = lax.dot_general(dob, vw, _NT, preferred_element_type=F32)
            ds = p * (dp - dl_ref[i][rows])
            dsb = ds.astype(BF16)
            dv_s[:width] += lax.dot_general(p.astype(BF16), dob, _TN, preferred_element_type=F32)
            dk_s[:width] += lax.dot_general(dsb, qs, _TN, preferred_element_type=F32)
            df_s[:, :width] -= jnp.sum(ds, axis=0, keepdims=True)
            dq_s[i, rows] += jnp.dot(dsb, kw, preferred_element_type=F32)
            dfq_s[i, rows] += jnp.sum(ds, axis=1, keepdims=True)

        def above(i, carry):
            block(i, slice(0, tq), tq, False)
            return carry

        for rows, width in _fox_diagonal_parts(tq):
            block(ki, rows, width, True)
        lax.fori_loop(ki + 1, nq, above, 0)
        dk_ref[...] = dk_s[...].astype(BF16)
        dv_ref[...] = dv_s[...].astype(BF16)
        dfk_ref[...] = df_s[...]

        @pl.when(ki == nq - 1)
        def _():
            dq_ref[...] = (dq_s[...].reshape(L, HEAD_DIM) * (HEAD_DIM ** -0.5)).astype(BF16)
            dfq_ref[...] = dfq_s[...]

    return pl.pallas_call(
        body, name="fox_bwd",
        out_shape=(jax.ShapeDtypeStruct(dproj.shape, dproj.dtype),
                   jax.ShapeDtypeStruct((L, MAIN_WIDTH), BF16),
                   jax.ShapeDtypeStruct((L, MAIN_WIDTH), BF16),
                   jax.ShapeDtypeStruct((FOX_HEADS, nq, tq, 1), F32),
                   jax.ShapeDtypeStruct((FOX_HEADS, nq, 1, tq), F32)),
        grid=(FOX_HEADS, nq),
        in_specs=[sp["seq"](0), sp["rows"](0), sp["rows"](FOX_HEADS), sp["row"],
                  sp["col_all"], sp["col_all"], sp["seq"](0), _ANY],
        out_specs=(sp["seq"](0), sp["rows"](0), sp["rows"](0), sp["col_all"], sp["row"]),
        input_output_aliases={7: 0},
        scratch_shapes=[pltpu.VMEM((tq, HEAD_DIM), F32), pltpu.VMEM((tq, HEAD_DIM), F32),
                        pltpu.VMEM((1, tq), F32), pltpu.VMEM((nq, tq, HEAD_DIM), F32),
                        pltpu.VMEM((nq, tq, 1), F32)],
        compiler_params=_params("parallel", "arbitrary"),
    )(proj, kv, kv, fk, lse, delta, datt, dproj)


def _pad_lanes(a):
    return jnp.pad(a, ((0, 0), (0, LANES - a.shape[1])))


def _mem_branch_fwd(memn, w_mk, proj, tag):
    kvm = _mm(memn, w_mk, name="mem_kv_" + tag)
    return kvm, _mem_attn_fwd(proj, kvm)


def _mem_branch_bwd(mem, g, w_mk, proj, memn, kvm, do_mem, dproj, tag):
    dproj, dkvm = _mem_attn_bwd(proj, kvm, do_mem, dproj)
    dkvm = dkvm.astype(BF16)
    dw_mk = _mm(memn, dkvm, ta=True, name="dw_mem_kv_" + tag, out_dtype=BF16)
    dmemn = _mm(dkvm, w_mk, tb=True, name="dmemn_" + tag)
    _, dg = _rmsnorm_bwd(mem, g, dmemn, name="mem_norm_bwd_" + tag, dx_dtype=BF16)
    return dproj, dw_mk, dg


def _local_step(x, mem, target, w, fetch=None, grads_ready=None):
    if grads_ready is None:
        grads_ready = lambda group, grads, token: token
    L = x.shape[0]
    g = {}
    w = dict(w)

    b_re_t = jnp.transpose(w["b_re"], (0, 2, 1))
    b_im_t = jnp.transpose(w["b_im"], (0, 2, 1))
    ar, ai, bbr_t, bbi_t = _s5_prep(w["lam_re"], w["lam_im"], w["log_step"], b_re_t, b_im_t)
    bmat, cmat = _s5_block_mats(bbr_t, bbi_t, w["c_re"], w["c_im"])
    a_rows = _s5_a_rows(ar, ai)

    hn0 = _rmsnorm_fwd(x, w["pre_norm_g"][0], name="pre_norm_0", out_dtype=BF16)
    memn0 = _rmsnorm_fwd(mem, w["mem_norm_g"][0], name="mem_norm_0", out_dtype=BF16)
    memn1 = _rmsnorm_fwd(mem, w["mem_norm_g"][1], name="mem_norm_1", out_dtype=BF16)
    if fetch is not None:
        w.update(fetch("a", [hn0, memn0, memn1, bmat, cmat, a_rows]))
    proj_a = _mm(hn0, w["w_in_a"], name="in_proj_a")
    y, yg, xp = _s5_fwd(proj_a, bmat, cmat, a_rows, w["d_skip"])
    if fetch is not None:
        w.update(fetch("b", yg))
    t = _mm(yg, w["w_glu"], name="glu_proj")
    kvm0, om0 = _mem_branch_fwd(memn0, w["w_mem_kv"][0], proj_a, "0")
    cat0 = _gate_a_fwd(y, t, w["b_glu"], proj_a, om0)
    o0 = _mm(cat0, w["w_out"][0], name="out_proj_0")
    h1, kv_in, hn1 = _post_norm_and_next_norms(
        o0, w["post_norm_g"][0], x, w["kv_norm_g"], w["pre_norm_g"][1], name="post_norm_0_kv_pre_norm_1")

    if fetch is not None:
        w.update(fetch("c", kv_in))
    kv = _mm(kv_in, w["w_kv"], name="kv_proj", out_dtype=BF16)
    pre_f = _mm(kv_in, w["w_fgate"], name="fgate_proj")
    b_f = jnp.pad(w["b_fgate"], (0, LANES - FOX_HEADS))
    fcum = _fgate_fwd(pre_f, b_f)
    fc = jnp.transpose(fcum[:, :FOX_HEADS])
    tq = min(FOX_BLOCK, L)
    fk = fc.reshape(FOX_HEADS, L // tq, 1, tq)

    proj_b = _mm(hn1, w["w_in_b"], name="in_proj_b")
    att, lse = _fox_fwd(proj_b, kv, fk)
    kvm1, om1 = _mem_branch_fwd(memn1, w["w_mem_kv"][1], proj_b, "1")
    cat1 = _gate_b_fwd(att, proj_b, om1)
    o1 = _mm(cat1, w["w_out"][1], name="out_proj_1")
    dh2, loss_row = _final_norm_loss(o1, w["post_norm_g"][1], h1, target)

    do1, dpost1 = _rmsnorm_bwd(o1, w["post_norm_g"][1], dh2, name="post_norm_bwd_1", dx_dtype=BF16)
    dcat1 = _mm(do1, w["w_out"][1], tb=True, name="dcat_1", out_dtype=BF16)
    g["w_out_1"] = _mm(cat1, do1, ta=True, name="dw_out_1", out_dtype=BF16)
    datt, dproj_b, dom1, delta = _gate_b_bwd(dcat1, att, proj_b, om1)
    dproj_b, g["w_mem_kv_1"], dmemg1 = _mem_branch_bwd(mem, w["mem_norm_g"][1], w["w_mem_kv"][1], proj_b,
                                                      memn1, kvm1, dom1, dproj_b, "1")
    delta = delta.reshape(lse.shape)
    dproj_b, dk, dv, dfq, dfk = _fox_bwd(proj_b, kv, fk, lse, delta, datt, dproj_b)
    g["w_in_b"] = _mm(hn1, dproj_b, ta=True, name="dw_in_b", out_dtype=BF16, shards=N_CHIPS)
    dhn1 = _mm(dproj_b, w["w_in_b"], tb=True, name="dhn_1")

    dkv = jnp.concatenate([dk, dv], axis=1)
    g["w_kv"] = _mm(kv_in, dkv, ta=True, name="dw_kv", out_dtype=BF16, shards=N_CHIPS)
    dkv_in_a = _mm(dkv, w["w_kv"], tb=True, name="dkv_in_kv")
    dfcum = _pad_lanes(jnp.transpose(dfq.reshape(FOX_HEADS, L) + dfk.reshape(FOX_HEADS, L)))
    dpre_f, db_f = _fgate_bwd(dfcum, pre_f, b_f)
    g["b_fgate"] = db_f[0, :FOX_HEADS]
    g["w_fgate"] = _mm(kv_in, dpre_f, ta=True, name="dw_fgate")[:, :FOX_HEADS]
    dkv_in_b = _mm(dpre_f, w["w_fgate"], tb=True, name="dkv_in_fgate")
    dh1, g["kv_norm_g"], dpre1 = _rmsnorm_bwd_pair(h1, w["kv_norm_g"], (dkv_in_a, dkv_in_b), w["pre_norm_g"][1],
                                                   dhn1, adds=(dh2,), name="kv_pre_norm_bwd")
    dh1 = grads_ready("b", g, dh1)

    do0, dpost0 = _rmsnorm_bwd(o0, w["post_norm_g"][0], dh1, name="post_norm_bwd_0", dx_dtype=BF16)
    dcat0 = _mm(do0, w["w_out"][0], tb=True, name="dcat_0", out_dtype=BF16)
    g["w_out_0"] = _mm(cat0, do0, ta=True, name="dw_out_0", out_dtype=BF16)
    dcat0 = grads_ready("b_send", g, dcat0)
    dproj_a, dt, dyg_a, dom0, db_glu = _gate_a_bwd(dcat0, y, t, w["b_glu"], proj_a, om0)
    g["b_glu"] = db_glu[0]
    g["w_glu"] = _mm(yg, dt, ta=True, name="dw_glu", out_dtype=BF16)
    dyg_b = _mm(dt, w["w_glu"], tb=True, name="dyg")
    dproj_a, g["w_mem_kv_0"], dmemg0 = _mem_branch_bwd(mem, w["mem_norm_g"][0], w["w_mem_kv"][0], proj_a,
                                                      memn0, kvm0, dom0, dproj_a, "0")
    dyg_b = grads_ready("a1", g, dyg_b)
    dproj_a, db_blk, dc_blk, da_rows, dd_skip = _s5_bwd(proj_a, dyg_a, dyg_b, y, xp, bmat, cmat, a_rows,
                                                        w["d_skip"], dproj_a)
    dproj_a = grads_ready("a1_send", g, dproj_a)
    g["d_skip"] = dd_skip[0]
    g["w_in_a"] = _mm(hn0, dproj_a, ta=True, name="dw_in_a", out_dtype=BF16, shards=N_CHIPS)
    dproj_a = grads_ready("a2", g, dproj_a)
    dhn0 = _mm(dproj_a, w["w_in_a"], tb=True, name="dhn_0")
    grad_x, dpre0 = _rmsnorm_bwd(x, w["pre_norm_g"][0], dhn0, adds=(dh1,), name="pre_norm_bwd_0")

    dbb = _s5_unfold(db_blk)
    dcc = _s5_unfold(dc_blk)
    g["c_re"], g["c_im"] = dcc[0], -dcc[1]
    d_ar = da_rows[:, 0, :STATE_COLS].reshape(SSM_GROUPS, SSM_STATE)
    d_ai = da_rows[:, 0, STATE_COLS:].reshape(SSM_GROUPS, SSM_STATE)
    dlr, dli, dls, dbr_t, dbi_t = _s5_prep_bwd(w["lam_re"], w["lam_im"], w["log_step"], b_re_t, b_im_t,
                                               d_ar, d_ai, dbb[0], dbb[1])
    g["lam_re"], g["lam_im"], g["log_step"] = dlr, dli, dls[:, 0]
    g["b_re"] = jnp.transpose(dbr_t, (0, 2, 1))
    g["b_im"] = jnp.transpose(dbi_t, (0, 2, 1))
    g["pre_norm_g"] = jnp.stack([dpre0, dpre1])
    g["post_norm_g"] = jnp.stack([dpost0, dpost1])
    g["mem_norm_g"] = jnp.stack([dmemg0, dmemg1])
    return loss_row, grad_x, g


_MESH = pl.DeviceIdType.MESH
_ANY = pl.BlockSpec(memory_space=pl.ANY)


def _place():
    x, y, c = lax.axis_index("x"), lax.axis_index("y"), lax.axis_index("c")
    chips = [(1 - x, y), (x, 1 - y), (1 - x, 1 - y)]
    return x, y, c, chips


_HBM = pl.BlockSpec(memory_space=pltpu.HBM)
_SEM = pl.BlockSpec(memory_space=pltpu.SEMAPHORE)
_SIDE = pltpu.SideEffectType.DATAFLOW_SIDE_EFFECTING


def _in_hbm(a):
    return pltpu.with_memory_space_constraint(a, pltpu.HBM)


def _hbm_like(a):
    return pltpu.HBM(a.shape, a.dtype)


def _ici_copies(srcs, lands, send_sem, recv_sem, src_at, dst_at, wait_at, to_sibling=False):
    x, y, c, chips = _place()
    peers = [(x, y, 1 - c)] if to_sibling else [(cx, cy, c) for cx, cy in chips]
    m = len(peers)
    start, wait = [], []
    for i in range(len(srcs)):
        for k, (px, py, pc) in enumerate(peers):
            sem = dict(send_sem=send_sem.at[m * i + k], recv_sem=recv_sem.at[m * i + k],
                       device_id=(px, py, pc), device_id_type=_MESH)
            src = src_at(srcs[i], 2 * px + py, c)
            start.append(pltpu.make_async_remote_copy(src_ref=src, dst_ref=dst_at(lands[i], 2 * x + y, k, c), **sem))
            wait.append(pltpu.make_async_remote_copy(src_ref=src, dst_ref=wait_at(lands[i], 2 * px + py, k, c), **sem))
    return start, wait


def _route_peers(route):
    return 1 if len(route) == 4 else 3


_BLOCK_ROUTE = (lambda s, j, c: s, lambda l, me, k, c: l.at[me, c], lambda l, j, k, c: l.at[j, c])


def _ici_start(srcs, lands, token, route, *, name):
    n = len(srcs)

    def body(*refs):
        start, _ = _ici_copies(refs[:n], refs[n:2 * n], refs[2 * n + 1], refs[2 * n + 2], *route)
        for cp in start:
            cp.start()

    sems = pltpu.SemaphoreType.DMA((_route_peers(route) * n,))
    outs = pl.pallas_call(
        body, name=name,
        out_shape=(sems, sems, *[_hbm_like(a) for a in srcs], *[_hbm_like(a) for a in lands], _hbm_like(token)),
        in_specs=[_HBM] * (2 * n + 1), out_specs=(_SEM, _SEM, *[_HBM] * (2 * n + 1)),
        input_output_aliases={i: 2 + i for i in range(2 * n + 1)},
        compiler_params=pltpu.CompilerParams(has_side_effects=_SIDE),
    )(*[_in_hbm(a) for a in srcs], *[_in_hbm(a) for a in lands], _in_hbm(token))
    return (outs[0], outs[1], list(outs[2:2 + n]), list(outs[2 + n:2 + 2 * n])), outs[2 + 2 * n]


def _ici_wait(handle, after, route, *, name):
    send_sem, recv_sem, srcs, lands = handle
    n = len(srcs)
    after = list(after) if isinstance(after, (list, tuple)) else [after]

    def body(*refs):
        _, wait = _ici_copies(refs[:n], refs[n:2 * n], refs[2 * n], refs[2 * n + 1], *route)
        for cp in wait:
            cp.wait_send()
            cp.wait_recv()

    outs = pl.pallas_call(
        body, name=name,
        out_shape=(*[_hbm_like(a) for a in srcs], *[_hbm_like(a) for a in lands]),
        in_specs=[_HBM] * (2 * n) + [_SEM, _SEM] + [_ANY] * len(after), out_specs=tuple([_HBM] * (2 * n)),
        input_output_aliases={i: i for i in range(2 * n)},
        compiler_params=pltpu.CompilerParams(has_side_effects=_SIDE),
    )(*srcs, *lands, send_sem, recv_sem, *after)
    return list(outs[:n]), list(outs[n:])


_GATHER_ROUTE = (lambda s, j, c: s.at[c], lambda l, me, k, c: l.at[me, c], lambda l, j, k, c: l.at[j, c])
_SCATTER_ROUTE = (lambda s, j, c: s.at[j], lambda l, me, k, c: l.at[k], lambda l, j, k, c: l.at[k])
_SHARE_ROUTE = (lambda s, j, c: s, lambda l, me, k, c: l.at[c], lambda l, j, k, c: l.at[1 - c], True)
_SWAP_ROUTE = (lambda s, j, c: s.at[:, 1 - c], lambda l, me, k, c: l, lambda l, j, k, c: l, True)


def _gather_forward(lands, tag, own=False):
    n = len(lands)
    m = 4 if own else 3

    def body(*refs):
        ins, outs = refs[:n], refs[n:2 * n]
        send_sem, recv_sem = refs[2 * n:]
        x, y, c, chips = _place()
        slots = [2 * cx + cy for cx, cy in chips] + [2 * x + y]

        def copy(i, k, half):
            return pltpu.make_async_remote_copy(
                src_ref=ins[i].at[slots[k], half], dst_ref=outs[i].at[slots[k], half],
                send_sem=send_sem.at[m * i + k], recv_sem=recv_sem.at[m * i + k],
                device_id=(x, y, 1 - c), device_id_type=_MESH)

        copies = [copy(i, k, c) for i in range(n) for k in range(m)]
        for cp in copies:
            cp.start()
        for i in range(n):
            for k in range(m):
                copy(i, k, 1 - c).wait_recv()
        for cp in copies:
            cp.wait_send()

    return pl.pallas_call(
        body, name="gather_forward_to_sibling_" + tag,
        out_shape=[jax.ShapeDtypeStruct(a.shape, a.dtype) for a in lands],
        in_specs=[_ANY] * n, out_specs=[_ANY] * n,
        input_output_aliases={i: i for i in range(n)},
        scratch_shapes=[pltpu.SemaphoreType.DMA((m * n,)), pltpu.SemaphoreType.DMA((m * n,))],
    )(*lands)


def _swap_halves(grads, tag):
    n = len(grads)

    def body(*refs):
        ins, outs = refs[:n], refs[n:2 * n]
        send_sem, recv_sem = refs[2 * n:]
        x, y, c, _ = _place()
        copies = [pltpu.make_async_remote_copy(
            src_ref=ins[i].at[:, 1 - c], dst_ref=outs[i],
            send_sem=send_sem.at[i], recv_sem=recv_sem.at[i],
            device_id=(x, y, 1 - c), device_id_type=_MESH) for i in range(n)]
        for cp in copies:
            cp.start()
        for cp in copies:
            cp.wait()

    return pl.pallas_call(
        body, name="grad_swap_halves_" + tag,
        out_shape=[jax.ShapeDtypeStruct((N_CHIPS,) + g.shape[2:], g.dtype) for g in grads],
        in_specs=[_ANY] * n, out_specs=[_ANY] * n,
        scratch_shapes=[pltpu.SemaphoreType.DMA((n,)), pltpu.SemaphoreType.DMA((n,))],
    )(*grads)


def _sum_rows(h, C):
    return max(d for d in range(SUBLANES, h + 1, SUBLANES) if h % d == 0 and d * C <= 1 << 20)


SUM_STEPS = 4


def _pair_sums(gs, rs, c_idx, *, name):
    n = len(gs)
    rows = [g.shape[2] // SUM_STEPS for g in gs]

    def body(c_ref, *refs):
        for g_ref, r_ref, o_ref in zip(refs[:n], refs[n:2 * n], refs[2 * n:]):
            o_ref[...] = (g_ref[...].astype(F32) + r_ref[...].astype(F32)).astype(o_ref.dtype)

    return pl.pallas_call(
        body, name=name,
        out_shape=[jax.ShapeDtypeStruct((N_CHIPS,) + g.shape[2:], g.dtype) for g in gs],
        grid_spec=pltpu.PrefetchScalarGridSpec(
            num_scalar_prefetch=1, grid=(N_CHIPS, SUM_STEPS),
            in_specs=[pl.BlockSpec((None, None, tr, g.shape[3]), lambda j, i, s: (j, s[0], i, 0))
                      for g, tr in zip(gs, rows)]
            + [pl.BlockSpec((None, tr, g.shape[3]), lambda j, i, s: (j, i, 0)) for g, tr in zip(gs, rows)],
            out_specs=[pl.BlockSpec((None, tr, g.shape[3]), lambda j, i, s: (j, i, 0)) for g, tr in zip(gs, rows)]),
        compiler_params=_params("parallel", "parallel"),
    )(c_idx, *gs, *rs)


def _owner_sums(ss, rs, jc_idx, *, name):
    n = len(ss)
    rows = [s.shape[1] // SUM_STEPS for s in ss]

    def body(jc_ref, *refs):
        for s_ref, r_ref, m_ref, o_ref in zip(refs[:n], refs[n:2 * n], refs[2 * n:3 * n], refs[3 * n:]):
            acc = s_ref[...].astype(F32)
            for k in range(3):
                acc = acc + r_ref[k].astype(F32)
            m_ref[...] = acc
            o_ref[...] = acc

    outs = pl.pallas_call(
        body, name=name,
        out_shape=[jax.ShapeDtypeStruct(s.shape[1:], F32) for s in ss]
        + [jax.ShapeDtypeStruct((2,) + s.shape[1:], F32) for s in ss],
        grid_spec=pltpu.PrefetchScalarGridSpec(
            num_scalar_prefetch=1, grid=(SUM_STEPS,),
            in_specs=[pl.BlockSpec((None, tr, s.shape[2]), lambda i, p: (p[0], i, 0)) for s, tr in zip(ss, rows)]
            + [pl.BlockSpec((3, tr, s.shape[2]), lambda i, p: (0, i, 0)) for s, tr in zip(ss, rows)],
            out_specs=[pl.BlockSpec((tr, s.shape[2]), lambda i, p: (i, 0)) for s, tr in zip(ss, rows)]
            + [pl.BlockSpec((None, tr, s.shape[2]), lambda i, p: (p[1], i, 0)) for s, tr in zip(ss, rows)]),
        compiler_params=_params("parallel"),
    )(jc_idx, *ss, *rs)
    return outs[:n], outs[n:]


def _chip_sums(grads, c_idx, tag):
    views = [g.reshape(N_CHIPS, 2, g.shape[1] // 2, g.shape[2]) for g in grads]
    arrived = _swap_halves(views, tag)
    return _pair_sums(views, arrived, c_idx, name=f"grad_pair_sums_{tag}")


def _sum_devices(blocks):
    R = blocks.shape[2]
    tr = _sum_rows(R, 2 * N_CHIPS * LANES)

    def body(b_ref, o_ref):
        acc = b_ref[0, 0]
        for d in range(1, 2 * N_CHIPS):
            acc = acc + b_ref[d // 2, d % 2]
        o_ref[...] = acc

    return pl.pallas_call(
        body, name="sum_small_over_devices", out_shape=jax.ShapeDtypeStruct((R, LANES), F32),
        grid=(R // tr,),
        in_specs=[pl.BlockSpec((N_CHIPS, 2, tr, LANES), lambda i: (0, 0, i, 0))],
        out_specs=pl.BlockSpec((tr, LANES), lambda i: (i, 0)),
        compiler_params=_params("parallel"),
    )(blocks)


def _adamw(w, g, m, v, *, name):
    R, C = w.shape
    tr = max(d for d in range(SUBLANES, R + 1, SUBLANES)
             if R % d == 0 and 7 * 2 * d * C * 4 <= VMEM_LIMIT_BYTES // 2)

    def body(w_ref, g_ref, m_ref, v_ref, d_ref, nm_ref, nv_ref):
        g = g_ref[...]
        m = ADAM_B1 * m_ref[...] + (1.0 - ADAM_B1) * g
        v = ADAM_B2 * v_ref[...] + (1.0 - ADAM_B2) * (g * g)
        nm_ref[...] = m
        nv_ref[...] = v
        m_hat = m / (1.0 - ADAM_B1 ** ADAM_STEP)
        v_hat = v / (1.0 - ADAM_B2 ** ADAM_STEP)
        d_ref[...] = -ADAM_LR * (m_hat / (jnp.sqrt(v_hat) + ADAM_EPS) + ADAM_WD * w_ref[...])

    blk = pl.BlockSpec((tr, C), lambda i: (i, 0))
    sds = jax.ShapeDtypeStruct((R, C), F32)
    return pl.pallas_call(
        body, name=name, out_shape=(sds, sds, sds), grid=(R // tr,),
        in_specs=[blk] * 4, out_specs=(blk, blk, blk),
        compiler_params=_params("parallel"),
    )(w, g, m, v)


_TILE = SUBLANES * LANES


def _pack(arrays):
    rows = []
    for a in arrays:
        flat = a.reshape(-1)
        flat = jnp.pad(flat, (0, (-flat.shape[0]) % _TILE))
        rows.append(flat.reshape(-1, LANES))
    return jnp.concatenate(rows, axis=0)


def _unpack(buf, shapes):
    out, r = [], 0
    for s in shapes:
        size = math.prod(s)
        nr = -(-size // _TILE) * SUBLANES
        out.append(buf[r:r + nr].reshape(-1)[:size].reshape(s))
        r += nr
    return out


_BIG = ("w_in_a", "w_glu", "w_kv", "w_in_b", "w_mem_kv", "w_out")
_REPLICATED = ("pre_norm_g", "post_norm_g", "lam_re", "lam_im", "log_step", "b_re", "b_im", "c_re", "c_im",
               "kv_norm_g", "b_fgate", "mem_norm_g")
_SHARDED_SMALL = ("d_skip", "b_glu", "w_fgate")
_WEIGHTS = ("pre_norm_g", "post_norm_g", "w_in_a", "lam_re", "lam_im", "log_step", "b_re", "b_im", "c_re",
            "c_im", "d_skip", "w_glu", "b_glu", "kv_norm_g", "w_kv", "w_fgate", "b_fgate", "w_in_b",
            "mem_norm_g", "w_mem_kv", "w_out")


def _halves(a):
    return a.reshape(2, a.shape[0] // 2, a.shape[1])


def _unhalve(a):
    return a.reshape(N_CHIPS, 2 * a.shape[2], a.shape[3])


def _columns(a):
    return jnp.transpose(a, (1, 0, 2)).reshape(a.shape[1], N_CHIPS * a.shape[2])


def kernel(x, mem, pre_norm_g, post_norm_g, w_in_a, lam_re, lam_im, log_step, b_re, b_im, c_re, c_im, d_skip, w_glu, b_glu, kv_norm_g, w_kv, w_fgate, b_fgate, w_in_b, mem_norm_g, w_mem_kv, w_out, loss_target, m_pre_norm_g, m_post_norm_g, m_w_in_a, m_lam_re, m_lam_im, m_log_step, m_b_re, m_b_im, m_c_re, m_c_im, m_d_skip, m_w_glu, m_b_glu, m_kv_norm_g, m_w_kv, m_w_fgate, m_b_fgate, m_w_in_b, m_mem_norm_g, m_w_mem_kv, m_w_out, v_pre_norm_g, v_post_norm_g, v_w_in_a, v_lam_re, v_lam_im, v_log_step, v_b_re, v_b_im, v_c_re, v_c_im, v_d_skip, v_w_glu, v_b_glu, v_kv_norm_g, v_w_kv, v_w_fgate, v_b_fgate, v_w_in_b, v_mem_norm_g, v_w_mem_kv, v_w_out):
    a = dict(locals())
    xi, yi, ci = lax.axis_index("x"), lax.axis_index("y"), lax.axis_index("c")
    chip = 2 * xi + yi
    c_idx = jnp.reshape(ci, (1,)).astype(jnp.int32)
    jc_idx = jnp.stack([chip, ci]).astype(jnp.int32)

    vec = jnp.zeros((2 * SUBLANES, MAIN_WIDTH // N_CHIPS), F32)
    vec = vec.at[0].set(a["d_skip"][0]).at[1].set(a["b_glu"][0])
    def own_slot(gathered, parts):
        return [lax.dynamic_update_index_in_dim(g, p, chip, 0) for g, p in zip(gathered, parts)]

    travelling, token = {}, a["pre_norm_g"]

    def start_gather(tag, parts, token):
        lands = [lax.empty((N_CHIPS,) + p.shape, p.dtype) for p in parts]
        travelling[tag], token = _ici_start(parts, lands, token, _GATHER_ROUTE, name=f"gather_{tag}_start")
        return token

    token = start_gather("a", [_halves(a["w_in_a"][0].astype(BF16)), _halves(vec)], token)
    later = ("w_glu", "w_mem_kv", "w_out", "w_kv", "w_fgate", "w_in_b")
    token, *raw = lax.optimization_barrier((token, *[a[n] for n in later]))
    raw = dict(zip(later, raw))
    token = start_gather("b", [_halves(raw["w_glu"][0].astype(BF16)),
                               *[_halves(raw["w_mem_kv"][i].astype(BF16)) for i in range(2)],
                               *[_halves(raw["w_out"][i].astype(BF16)) for i in range(2)]], token)
    token = start_gather("c", [_halves(raw["w_kv"].astype(BF16)), _halves(_pad_lanes(raw["w_fgate"]).astype(BF16)),
                               _halves(raw["w_in_b"][0].astype(BF16))], token)

    def fetch(tag, after):
        parts, lands = _ici_wait(travelling[tag], after, _GATHER_ROUTE, name=f"gather_{tag}_wait")
        full = own_slot(_gather_forward(lands, tag), parts)
        if tag == "a":
            w_in_a, vecs = full
            return dict(w_in_a=_columns(_unhalve(w_in_a)), d_skip=vecs[:, 0, 0, :].reshape(MAIN_WIDTH),
                        b_glu=vecs[:, 0, 1, :].reshape(MAIN_WIDTH))
        if tag == "b":
            w_glu, w_mk0, w_mk1, w_out0, w_out1 = full
            return dict(w_glu=w_glu.reshape(MAIN_WIDTH, MAIN_WIDTH),
                        w_mem_kv=[m.reshape(D_MODEL, 2 * MEM_WIDTH) for m in (w_mk0, w_mk1)],
                        w_out=[o.reshape(D_MODEL, D_MODEL) for o in (w_out0, w_out1)])
        w_kv, w_fg, w_in_b = full
        return dict(w_kv=_columns(_unhalve(w_kv)), w_fgate=w_fg.reshape(D_MODEL, LANES),
                    w_in_b=_columns(_unhalve(w_in_b)))

    early = ("mem_norm_g", "lam_re", "lam_im", "log_step", "b_re", "b_im", "c_re", "c_im")
    token, *held = lax.optimization_barrier((token, *[a[n] for n in early]))
    held = dict(zip(early, held))
    w = dict(
        pre_norm_g=token, post_norm_g=a["post_norm_g"], mem_norm_g=held["mem_norm_g"],
        kv_norm_g=a["kv_norm_g"], b_fgate=a["b_fgate"],
        **{n: held[n][0] for n in early[1:]})

    sent = {}

    swapping = {}

    def grads_ready(event, g, token):
        tag = event.split("_")[0]
        if event in ("b", "a1"):
            big = {"b": lambda: [g["w_kv"], g["w_in_b"], g["w_mem_kv_1"].reshape(N_CHIPS, -1, 2 * MEM_WIDTH),
                                 g["w_out_1"].reshape(N_CHIPS, -1, D_MODEL)],
                   "a1": lambda: [g["w_glu"].reshape(N_CHIPS, -1, MAIN_WIDTH),
                                  g["w_mem_kv_0"].reshape(N_CHIPS, -1, 2 * MEM_WIDTH),
                                  g["w_out_0"].reshape(N_CHIPS, -1, D_MODEL)]}[tag]()
            views = [b.reshape(N_CHIPS, 2, b.shape[1] // 2, b.shape[2]) for b in big]
            lands = [lax.empty((N_CHIPS,) + v.shape[2:], v.dtype) for v in views]
            swapping[tag], token = _ici_start(views, lands, token, _SWAP_ROUTE, name=f"grad_swap_{tag}_start")
            return token
        if event == "a2":
            sums = _chip_sums([g["w_in_a"]], c_idx, tag)
        else:
            views, arrived = _ici_wait(swapping[tag], token, _SWAP_ROUTE, name=f"grad_swap_{tag}_wait")
            sums = _pair_sums(views, arrived, c_idx, name=f"grad_pair_sums_{tag}")
        lands = [lax.empty((3,) + s.shape[1:], s.dtype) for s in sums]
        sent[tag], token = _ici_start(sums, lands, token, _SCATTER_ROUTE, name=f"grad_send_{tag}_start")
        return token

    loss_row, grad_x, g = _local_step(a["x"][0], a["mem"][0], a["loss_target"][0], w, fetch, grads_ready)

    small_names = _REPLICATED + _SHARDED_SMALL
    pack = _pack([g[n] for n in small_names])
    blocks = lax.empty((N_CHIPS, 2) + pack.shape, F32)
    small_sent, token = _ici_start([pack], [blocks], loss_row, _BLOCK_ROUTE, name="small_sums_start")

    sharing = {}
    for tag in ("b", "a1", "a2"):
        sums, arrived = _ici_wait(sent[tag], [grad_x, token], _SCATTER_ROUTE, name=f"grad_send_{tag}_wait")
        mine, bufs = _owner_sums(sums, arrived, jc_idx, name=f"grad_owner_sums_{tag}")
        sharing[tag], token = _ici_start(mine, bufs, token, _SHARE_ROUTE, name=f"grad_share_{tag}_start")
    loss = lax.psum(jnp.sum(token), MESH_AXES)

    def shared(tag, after):
        _, bufs = _ici_wait(sharing[tag], after, _SHARE_ROUTE, name=f"grad_share_{tag}_wait")
        return [b.reshape(-1, b.shape[2]) for b in bufs]

    grads, delta, new_m, new_v = {}, {}, {}, {}

    def adam(n):
        shape = a[n].shape
        d2 = (-1, shape[-1])
        d, m, v = _adamw(a[n].reshape(d2), grads[n].reshape(d2), a["m_" + n].reshape(d2),
                         a["v_" + n].reshape(d2), name="adamw_" + n)
        delta[n], new_m[n], new_v[n] = d.reshape(shape), m.reshape(shape), v.reshape(shape)
        return d

    r_kv, r_in_b, r_mk1, r_out1 = shared("b", token)
    grads["w_kv"], grads["w_in_b"] = r_kv, r_in_b[None]
    done = [adam("w_kv"), adam("w_in_b")]
    r_glu, r_mk0, r_out0 = shared("a1", done)
    grads["w_glu"], grads["w_mem_kv"], grads["w_out"] = r_glu[None], jnp.stack([r_mk0, r_mk1]), jnp.stack([r_out0, r_out1])
    done = [adam("w_glu"), adam("w_mem_kv"), adam("w_out")]
    (r_in_a,) = shared("a2", done)
    grads["w_in_a"] = r_in_a[None]
    adam("w_in_a")

    (pack,), (blocks,) = _ici_wait(small_sent, [delta[n] for n in _BIG], _BLOCK_ROUTE, name="small_sums_wait")
    blocks = lax.dynamic_update_slice(blocks, pack[None, None], (chip, ci, 0, 0))
    (blocks,) = _gather_forward([blocks], "small", own=True)
    small = dict(zip(small_names, _unpack(_sum_devices(blocks), [g[n].shape for n in small_names])))
    for n in _REPLICATED:
        grads[n] = small[n].reshape(a[n].shape)
    nd = MAIN_WIDTH // N_CHIPS
    grads["d_skip"] = lax.dynamic_slice(small["d_skip"], (chip * nd,), (nd,))[None]
    grads["b_glu"] = lax.dynamic_slice(small["b_glu"], (chip * nd,), (nd,))[None]
    nf = D_MODEL // N_CHIPS
    grads["w_fgate"] = lax.dynamic_slice(small["w_fgate"], (chip * nf, 0), (nf, FOX_HEADS))

    shapes = [a[n].shape for n in small_names]
    d, m, v = _adamw(_pack([a[n] for n in small_names]), _pack([grads[n] for n in small_names]),
                     _pack([a["m_" + n] for n in small_names]), _pack([a["v_" + n] for n in small_names]),
                     name="adamw_small")
    for n, dd, mm, vv in zip(small_names, _unpack(d, shapes), _unpack(m, shapes), _unpack(v, shapes)):
        delta[n], new_m[n], new_v[n] = dd, mm, vv

    return (loss, grad_x[None], *[grads[n] for n in _WEIGHTS], *[delta[n] for n in _WEIGHTS],
            *[new_m[n] for n in _WEIGHTS], *[new_v[n] for n in _WEIGHTS])
```

```python
import math

import jax
import jax.numpy as jnp
from jax import lax
from jax.experimental import pallas as pl
from jax.experimental.pallas import tpu as pltpu

F32 = jnp.float32
BF16 = jnp.bfloat16

D_MODEL = 2048
N_MEM = 256
MAIN_WIDTH = 1536
MEM_WIDTH = 512
IN_WIDTH = 2 * MAIN_WIDTH + 2 * MEM_WIDTH
HEAD_DIM = 128
FOX_HEADS = MAIN_WIDTH // HEAD_DIM
MEM_HEADS = MEM_WIDTH // HEAD_DIM
SSM_GROUP = 16
SSM_GROUPS = MAIN_WIDTH // SSM_GROUP
SSM_STATE = 64
GROUPS_PER_BLOCK = 8
SSM_BLOCKS = SSM_GROUPS // GROUPS_PER_BLOCK
STATE_COLS = GROUPS_PER_BLOCK * SSM_STATE
EPS = 1e-6
ADAM_LR = 0.001
ADAM_B1 = 0.9
ADAM_B2 = 0.999
ADAM_EPS = 1e-08
ADAM_WD = 0.01
ADAM_STEP = 10
N_CHIPS = 4
LANES = 128
SUBLANES = 8
VMEM_LIMIT_BYTES = 56 * 1024 * 1024
NEG_BIG = -1e30
MESH_AXES = ("x", "y", "c")


def _params(*sem):
    return pltpu.CompilerParams(dimension_semantics=sem if sem else None,
                                vmem_limit_bytes=VMEM_LIMIT_BYTES)


def _sigmoid(x):
    return 1.0 / (1.0 + jnp.exp(-x))


def _gelu(x):
    c = math.sqrt(2.0 / math.pi)
    return 0.5 * x * (1.0 + jnp.tanh(c * (x + 0.044715 * (x * x * x))))


def _gelu_grad(x):
    c = math.sqrt(2.0 / math.pi)
    t = jnp.tanh(c * (x + 0.044715 * (x * x * x)))
    return 0.5 * (1.0 + t) + 0.5 * x * (1.0 - t * t) * (c * (1.0 + 3.0 * 0.044715 * (x * x)))


def _silu_and_grad(z):
    s = _sigmoid(z)
    return z * s, s * (1.0 + z * (1.0 - s))


_TILE_CHOICES = (4096, 3072, 2048, 1536, 1024, 768, 512, 384, 256, LANES)


def _tile(n, cap):
    return next(c for c in _TILE_CHOICES if c <= cap and n % c == 0)


def _mm(a, b, *, name, ta=False, tb=False, out_dtype=F32, shards=1, tm=1024, tn=1024, tk=4096):
    if ta:
        K, M = a.shape
    else:
        M, K = a.shape
    if tb:
        N, kb = b.shape
    else:
        kb, N = b.shape
    assert K == kb, (a.shape, b.shape)
    ns = N // shards
    tm, tn, tk = _tile(M, tm), _tile(ns, tn), _tile(K, tk)
    assert M % tm == 0 and ns % tn == 0 and K % tk == 0 and N % shards == 0
    nk = K // tk
    dn = (((0 if ta else 1,), (1 if tb else 0,)), ((), ()))

    def body(a_ref, b_ref, o_ref, *acc):
        prod = lax.dot_general(a_ref[...].astype(BF16), b_ref[...].astype(BF16), dn, preferred_element_type=F32)
        if nk == 1:
            o_ref[...] = prod.astype(o_ref.dtype)
            return
        acc_ref, = acc
        k = pl.program_id(2)

        @pl.when(k == 0)
        def _():
            acc_ref[...] = jnp.zeros_like(acc_ref)

        acc_ref[...] += prod

        @pl.when(k == nk - 1)
        def _():
            o_ref[...] = acc_ref[...].astype(o_ref.dtype)

    a_spec = (pl.BlockSpec((tk, tm), lambda i, j, k: (k, i)) if ta
              else pl.BlockSpec((tm, tk), lambda i, j, k: (i, k)))
    b_spec = (pl.BlockSpec((tn, tk), lambda i, j, k: (j, k)) if tb
              else pl.BlockSpec((tk, tn), lambda i, j, k: (k, j)))
    if shards == 1:
        out_shape = jax.ShapeDtypeStruct((M, N), out_dtype)
        o_spec = pl.BlockSpec((tm, tn), lambda i, j, k: (i, j))
    else:
        nb = ns // tn
        out_shape = jax.ShapeDtypeStruct((shards, M, ns), out_dtype)
        o_spec = pl.BlockSpec((None, tm, tn), lambda i, j, k: (j // nb, i, j % nb))
    return pl.pallas_call(
        body, name=name, out_shape=out_shape,
        grid=(M // tm, N // tn, nk),
        in_specs=[a_spec, b_spec], out_specs=o_spec,
        scratch_shapes=[] if nk == 1 else [pltpu.VMEM((tm, tn), F32)],
        compiler_params=_params("parallel", "parallel", "arbitrary"),
    )(a, b)


def _rmsnorm_fwd(x, g, *, name, out_dtype=F32, tr=256):
    L, D = x.shape
    tr = min(tr, L)

    def body(x_ref, g_ref, o_ref):
        xf = x_ref[...]
        r = lax.rsqrt(jnp.mean(xf * xf, axis=-1, keepdims=True) + EPS)
        o_ref[...] = (xf * r * g_ref[...]).astype(o_ref.dtype)

    row = pl.BlockSpec((tr, D), lambda i: (i, 0))
    vec = pl.BlockSpec((1, D), lambda i: (0, 0))
    return pl.pallas_call(
        body, name=name, out_shape=jax.ShapeDtypeStruct((L, D), out_dtype),
        grid=(L // tr,), in_specs=[row, vec], out_specs=row,
        compiler_params=_params("parallel"),
    )(x, g.reshape(1, D))


def _post_norm_and_next_norms(o, g_post, res, g_kv, g_pre, *, name, tr=256):
    L, D = o.shape
    tr = min(tr, L)

    def body(o_ref, gp_ref, r_ref, gk_ref, gn_ref, h_ref, kv_ref, hn_ref):
        of = o_ref[...]
        r = lax.rsqrt(jnp.mean(of * of, axis=-1, keepdims=True) + EPS)
        h = r_ref[...] + of * r * gp_ref[...]
        h_ref[...] = h
        hr = h * lax.rsqrt(jnp.mean(h * h, axis=-1, keepdims=True) + EPS)
        kv_ref[...] = (hr * gk_ref[...]).astype(kv_ref.dtype)
        hn_ref[...] = (hr * gn_ref[...]).astype(hn_ref.dtype)

    row = pl.BlockSpec((tr, D), lambda i: (i, 0))
    vec = pl.BlockSpec((1, D), lambda i: (0, 0))
    return pl.pallas_call(
        body, name=name,
        out_shape=(jax.ShapeDtypeStruct((L, D), F32), jax.ShapeDtypeStruct((L, D), BF16),
                   jax.ShapeDtypeStruct((L, D), BF16)),
        grid=(L // tr,), in_specs=[row, vec, row, vec, vec], out_specs=(row, row, row),
        compiler_params=_params("parallel"),
    )(o, g_post.reshape(1, D), res, g_kv.reshape(1, D), g_pre.reshape(1, D))


def _rmsnorm_bwd(x, g, dy, *, name, adds=(), dx_dtype=F32, tr=256):
    L, D = x.shape
    tr = min(tr, L)
    dys = dy if isinstance(dy, tuple) else (dy,)
    n_dy, n_add = len(dys), len(adds)

    def body(*refs):
        x_ref, g_ref = refs[:2]
        dy_refs = refs[2:2 + n_dy]
        add_refs = refs[2 + n_dy:2 + n_dy + n_add]
        dx_ref, dg_ref = refs[2 + n_dy + n_add:]
        xf = x_ref[...]
        dyf = dy_refs[0][...].astype(F32)
        for d_ref in dy_refs[1:]:
            dyf = dyf + d_ref[...].astype(F32)
        r = lax.rsqrt(jnp.mean(xf * xf, axis=-1, keepdims=True) + EPS)
        gy = dyf * g_ref[...]
        c = jnp.mean(xf * gy, axis=-1, keepdims=True) * (r * r * r)
        dx = gy * r - xf * c
        for a_ref in add_refs:
            dx = dx + a_ref[...].astype(F32)
        dx_ref[...] = dx.astype(dx_ref.dtype)

        @pl.when(pl.program_id(0) == 0)
        def _():
            dg_ref[...] = jnp.zeros_like(dg_ref)

        dg_ref[...] += jnp.sum(dyf * xf * r, axis=0, keepdims=True)

    row = pl.BlockSpec((tr, D), lambda i: (i, 0))
    vec = pl.BlockSpec((1, D), lambda i: (0, 0))
    dx, dg = pl.pallas_call(
        body, name=name,
        out_shape=(jax.ShapeDtypeStruct((L, D), dx_dtype), jax.ShapeDtypeStruct((1, D), F32)),
        grid=(L // tr,), in_specs=[row, vec] + [row] * (n_dy + n_add), out_specs=(row, vec),
        compiler_params=_params("arbitrary"),
    )(x, g.reshape(1, D), *dys, *adds)
    return dx, dg.reshape(D)


def _rmsnorm_bwd_pair(x, g1, dy1, g2, dy2, *, name, adds=(), tr=256):
    L, D = x.shape
    tr = min(tr, L)
    dy1s = dy1 if isinstance(dy1, tuple) else (dy1,)
    n1, n_add = len(dy1s), len(adds)

    def body(*refs):
        x_ref, g1_ref, g2_ref = refs[:3]
        dy1_refs = refs[3:3 + n1]
        dy2_ref = refs[3 + n1]
        add_refs = refs[4 + n1:4 + n1 + n_add]
        dx_ref, dg1_ref, dg2_ref = refs[4 + n1 + n_add:]
        xf = x_ref[...]
        d1 = dy1_refs[0][...].astype(F32)
        for d_ref in dy1_refs[1:]:
            d1 = d1 + d_ref[...].astype(F32)
        d2 = dy2_ref[...].astype(F32)
        r = lax.rsqrt(jnp.mean(xf * xf, axis=-1, keepdims=True) + EPS)
        gy = d1 * g1_ref[...] + d2 * g2_ref[...]
        c = jnp.mean(xf * gy, axis=-1, keepdims=True) * (r * r * r)
        dx = gy * r - xf * c
        for a_ref in add_refs:
            dx = dx + a_ref[...].astype(F32)
        dx_ref[...] = dx

        @pl.when(pl.program_id(0) == 0)
        def _():
            dg1_ref[...] = jnp.zeros_like(dg1_ref)
            dg2_ref[...] = jnp.zeros_like(dg2_ref)

        xr = xf * r
        dg1_ref[...] += jnp.sum(d1 * xr, axis=0, keepdims=True)
        dg2_ref[...] += jnp.sum(d2 * xr, axis=0, keepdims=True)

    row = pl.BlockSpec((tr, D), lambda i: (i, 0))
    vec = pl.BlockSpec((1, D), lambda i: (0, 0))
    dx, dg1, dg2 = pl.pallas_call(
        body, name=name,
        out_shape=(jax.ShapeDtypeStruct((L, D), F32), jax.ShapeDtypeStruct((1, D), F32),
                   jax.ShapeDtypeStruct((1, D), F32)),
        grid=(L // tr,), in_specs=[row, vec, vec] + [row] * (n1 + 1 + n_add), out_specs=(row, vec, vec),
        compiler_params=_params("arbitrary"),
    )(x, g1.reshape(1, D), g2.reshape(1, D), *dy1s, dy2, *adds)
    return dx, dg1.reshape(D), dg2.reshape(D)


def _final_norm_loss(o, g, res, target, *, tr=256):
    L, D = o.shape
    tr = min(tr, L)

    def body(o_ref, g_ref, r_ref, t_ref, dh_ref, loss_ref):
        xf = o_ref[...]
        r = lax.rsqrt(jnp.mean(xf * xf, axis=-1, keepdims=True) + EPS)
        e = (r_ref[...] + xf * r * g_ref[...]) - t_ref[...]
        dh_ref[...] = e * (1.0 / D)

        @pl.when(pl.program_id(0) == 0)
        def _():
            loss_ref[...] = jnp.zeros_like(loss_ref)

        loss_ref[...] += jnp.sum(e * e, axis=0, keepdims=True) * (0.5 / D)

    row = pl.BlockSpec((tr, D), lambda i: (i, 0))
    vec = pl.BlockSpec((1, D), lambda i: (0, 0))
    dh, lp = pl.pallas_call(
        body, name="post_norm_1_loss",
        out_shape=(jax.ShapeDtypeStruct((L, D), F32), jax.ShapeDtypeStruct((1, D), F32)),
        grid=(L // tr,), in_specs=[row, vec, row, row], out_specs=(row, vec),
        compiler_params=_params("arbitrary"),
    )(o, g.reshape(1, D), res, target)
    return dh, lp


def _s5_coeffs(lr, li, ls):
    dt = jnp.exp(ls)
    mag = jnp.exp(lr * dt)
    ar = mag * jnp.cos(li * dt)
    ai = mag * jnp.sin(li * dt)
    den = lr * lr + li * li
    cr = ((ar - 1.0) * lr + ai * li) / den
    ci = (ai * lr - (ar - 1.0) * li) / den
    return dt, ar, ai, den, cr, ci


def _s5_prep(lam_re, lam_im, log_step, b_re_t, b_im_t):
    G, P = lam_re.shape
    H = b_re_t.shape[1]

    def body(lr_ref, li_ref, ls_ref, br_ref, bi_ref, ar_ref, ai_ref, bbr_ref, bbi_ref):
        _, ar, ai, _, cr, ci = _s5_coeffs(lr_ref[...], li_ref[...], ls_ref[...])
        ar_ref[...] = ar
        ai_ref[...] = ai
        br, bi = br_ref[...], bi_ref[...]
        crb, cib = cr[:, None, :], ci[:, None, :]
        bbr_ref[...] = crb * br - cib * bi
        bbi_ref[...] = crb * bi + cib * br

    return pl.pallas_call(
        body, name="s5_prep",
        out_shape=(jax.ShapeDtypeStruct((G, P), F32), jax.ShapeDtypeStruct((G, P), F32),
                   jax.ShapeDtypeStruct((G, H, P), F32), jax.ShapeDtypeStruct((G, H, P), F32)),
        compiler_params=_params(),
    )(lam_re, lam_im, log_step.reshape(G, 1), b_re_t, b_im_t)


def _s5_prep_bwd(lam_re, lam_im, log_step, b_re_t, b_im_t, d_ar, d_ai, d_bbr, d_bbi):
    G, P = lam_re.shape
    H = b_re_t.shape[1]

    def body(lr_ref, li_ref, ls_ref, br_ref, bi_ref, dar_ref, dai_ref, dbbr_ref, dbbi_ref,
             dlr_ref, dli_ref, dls_ref, dbr_ref, dbi_ref):
        lr, li = lr_ref[...], li_ref[...]
        dt, ar, ai, den, cr, ci = _s5_coeffs(lr, li, ls_ref[...])
        br, bi = br_ref[...], bi_ref[...]
        gbr, gbi = dbbr_ref[...], dbbi_ref[...]
        crb, cib = cr[:, None, :], ci[:, None, :]
        dbr_ref[...] = crb * gbr + cib * gbi
        dbi_ref[...] = crb * gbi - cib * gbr
        gcr = jnp.sum(br * gbr + bi * gbi, axis=1)
        gci = jnp.sum(br * gbi - bi * gbr, axis=1)
        ilr, ili = lr / den, -li / den
        gar = dar_ref[...] + (ilr * gcr + ili * gci)
        gai = dai_ref[...] + (ilr * gci - ili * gcr)
        qr, qi = cr * ilr - ci * ili, cr * ili + ci * ilr
        glr = -(qr * gcr + qi * gci)
        gli = -(qr * gci - qi * gcr)
        glr = glr + dt * (ar * gar + ai * gai)
        gli = gli + dt * (ar * gai - ai * gar)
        wr, wi = lr * ar - li * ai, lr * ai + li * ar
        gdt = jnp.sum(wr * gar + wi * gai, axis=1, keepdims=True)
        dlr_ref[...] = glr
        dli_ref[...] = gli
        dls_ref[...] = gdt * dt

    return pl.pallas_call(
        body, name="s5_prep_bwd",
        out_shape=(jax.ShapeDtypeStruct((G, P), F32), jax.ShapeDtypeStruct((G, P), F32),
                   jax.ShapeDtypeStruct((G, 1), F32),
                   jax.ShapeDtypeStruct((G, H, P), F32), jax.ShapeDtypeStruct((G, H, P), F32)),
        compiler_params=_params(),
    )(lam_re, lam_im, log_step.reshape(G, 1), b_re_t, b_im_t, d_ar, d_ai, d_bbr, d_bbi)


def _s5_block_mats(bbr_t, bbi_t, c_re, c_im):
    bmat = _s5_expand(bbr_t, bbi_t)
    cmat = jnp.transpose(_s5_expand(c_re, -c_im), (0, 2, 1))
    return bmat.astype(BF16), cmat.astype(BF16)


def _s5_diag_mask():
    r = lax.broadcasted_iota(jnp.int32, (LANES, 2 * STATE_COLS), 0) // SSM_GROUP
    c = (lax.broadcasted_iota(jnp.int32, (LANES, 2 * STATE_COLS), 1) % STATE_COLS) // SSM_STATE
    return (r == c).astype(F32)


def _s5_expand(re, im):
    re = jnp.tile(re.reshape(SSM_BLOCKS, LANES, SSM_STATE), (1, 1, GROUPS_PER_BLOCK))
    im = jnp.tile(im.reshape(SSM_BLOCKS, LANES, SSM_STATE), (1, 1, GROUPS_PER_BLOCK))
    return jnp.concatenate([re, im], axis=-1) * _s5_diag_mask()[None]


def _s5_unfold(dmat):
    d = dmat.reshape(SSM_GROUPS, SSM_GROUP, 2, SSM_STATE)
    return jnp.transpose(d, (2, 0, 1, 3))


def _s5_a_rows(ar, ai):
    a = jnp.concatenate([ar.reshape(SSM_BLOCKS, STATE_COLS), ai.reshape(SSM_BLOCKS, STATE_COLS)], axis=1)
    return jnp.broadcast_to(a[:, None, :], (SSM_BLOCKS, SUBLANES, 2 * STATE_COLS))


def _to_step_major(src_ref, dst_ref, seg):
    for s in range(SUBLANES):
        dst_ref[pl.ds(s, seg, stride=SUBLANES), :] = src_ref[pl.ds(seg * s, seg), :]


def _segment_rows(ref, s, seg):
    return ref[pl.ds(s, seg, stride=SUBLANES), :]


def _cmul(ar, ai, xr, xi):
    return ar * xr - ai * xi, ar * xi + ai * xr


def _s5_tables(a_ref, pw_s, pwr_s, S, seg):
    ar, ai = a_ref[:, :S], a_ref[:, S:]

    def step(i, c):
        pr, pi = c
        pw_s[i, :, :S] = pr
        pw_s[i, :, S:] = pi
        nr, ni = _cmul(ar, ai, pr, pi)
        pwr_s[seg - 1 - i, :, :S] = nr
        pwr_s[seg - 1 - i, :, S:] = ni
        return nr, ni

    pr, pi = lax.fori_loop(0, seg, step, (jnp.ones_like(ar), jnp.zeros_like(ai)))
    pw_s[seg, :, :S] = pr
    pw_s[seg, :, S:] = pi


def _s5_fwd(proj, bmat, cmat, a_rows, d_skip, *, tc=512):
    L = proj.shape[0]
    tc = min(tc, L)
    nt = L // tc
    seg = tc // SUBLANES
    S = STATE_COLS

    def body(u_ref, b_ref, c_ref, a_ref, d_ref, y_ref, yg_ref, xp_ref,
             bu_s, xp_s, pw_s, pwr_s, carry_s, e_s, up_s, yc_s):
        @pl.when(pl.program_id(1) == 0)
        def _():
            carry_s[...] = jnp.zeros_like(carry_s)
            _s5_tables(a_ref, pw_s, pwr_s, S, seg)

        ar, ai = a_ref[:, :S], a_ref[:, S:]
        _to_step_major(u_ref, up_s, seg)
        bu = jnp.dot(up_s[...].astype(BF16), b_ref[...], preferred_element_type=F32)
        bu_s[...] = bu.reshape(seg, SUBLANES, 2 * S)

        def step(i, carry):
            cr, ci = carry
            xp_s[i, :, :S] = cr
            xp_s[i, :, S:] = ci
            return ar * cr - ai * ci + bu_s[i, :, :S], ar * ci + ai * cr + bu_s[i, :, S:]

        zero = jnp.zeros((SUBLANES, S), F32)
        fr, fi = lax.fori_loop(0, seg, step, (zero, zero))
        pr, pi = pw_s[seg, 0:1, :S], pw_s[seg, 0:1, S:]
        er, ei = carry_s[0:1, :S], carry_s[0:1, S:]
        for s in range(SUBLANES):
            e_s[s:s + 1, :S] = er
            e_s[s:s + 1, S:] = ei
            tr, ti = _cmul(pr, pi, er, ei)
            er, ei = fr[s:s + 1] + tr, fi[s:s + 1] + ti
        carry_s[0:1, :S] = er
        carry_s[0:1, S:] = ei
        pw = pw_s[0:seg]
        tr, ti = _cmul(pw[:, :, :S], pw[:, :, S:], e_s[:, :S][None], e_s[:, S:][None])
        xl = xp_s[...]
        xp = jnp.concatenate([xl[:, :, :S] + tr, xl[:, :, S:] + ti], axis=-1).reshape(tc, 2 * S)
        xp_ref[...] = xp
        a1r, a1i = ar[0:1], ai[0:1]
        x_re = a1r * xp[:, :S] - a1i * xp[:, S:] + bu[:, :S]
        x_im = a1r * xp[:, S:] + a1i * xp[:, :S] + bu[:, S:]
        xs = jnp.concatenate([x_re, x_im], axis=1).astype(BF16)
        yc_s[...] = jnp.dot(xs, c_ref[...], preferred_element_type=F32)
        for s in range(SUBLANES):
            rows = pl.ds(seg * s, seg)
            y = _segment_rows(yc_s, s, seg) + d_ref[...] * u_ref[rows, :]
            y_ref[rows, :] = y
            yg_ref[rows, :] = _gelu(y).astype(BF16)

    return pl.pallas_call(
        body, name="s5_fwd",
        out_shape=(jax.ShapeDtypeStruct((L, MAIN_WIDTH), F32),
                   jax.ShapeDtypeStruct((L, MAIN_WIDTH), BF16),
                   jax.ShapeDtypeStruct((L, SSM_BLOCKS * 2 * S), F32)),
        grid=(SSM_BLOCKS, nt),
        in_specs=[pl.BlockSpec((tc, LANES), lambda b, t: (t, b)),
                  pl.BlockSpec((None, LANES, 2 * S), lambda b, t: (b, 0, 0)),
                  pl.BlockSpec((None, 2 * S, LANES), lambda b, t: (b, 0, 0)),
                  pl.BlockSpec((None, SUBLANES, 2 * S), lambda b, t: (b, 0, 0)),
                  pl.BlockSpec((1, LANES), lambda b, t: (0, b))],
        out_specs=(pl.BlockSpec((tc, LANES), lambda b, t: (t, b)),
                   pl.BlockSpec((tc, LANES), lambda b, t: (t, b)),
                   pl.BlockSpec((tc, 2 * S), lambda b, t: (t, b))),
        scratch_shapes=[pltpu.VMEM((seg, SUBLANES, 2 * S), F32),
                        pltpu.VMEM((seg, SUBLANES, 2 * S), F32),
                        pltpu.VMEM((seg + 1, SUBLANES, 2 * S), F32),
                        pltpu.VMEM((seg, SUBLANES, 2 * S), F32),
                        pltpu.VMEM((SUBLANES, 2 * S), F32),
                        pltpu.VMEM((SUBLANES, 2 * S), F32),
                        pltpu.VMEM((tc, LANES), F32),
                        pltpu.VMEM((tc, LANES), F32)],
        compiler_params=_params("parallel", "arbitrary"),
    )(proj, bmat, cmat, a_rows, d_skip.reshape(1, MAIN_WIDTH))


def _s5_bwd(proj, dyg_a, dyg_b, y, xp, bmat, cmat, a_rows, d_skip, dproj, *, tc=512):
    L = proj.shape[0]
    tc = min(tc, L)
    nt = L // tc
    seg = tc // SUBLANES
    S = STATE_COLS
    nn = (((1,), (1,)), ((), ()))
    tn = (((0,), (0,)), ((), ()))

    def fold_diagonal(acc_ref, mask_ref, fold_ref):
        x = acc_ref[...] * mask_ref[...]
        hi = x.astype(BF16)
        rest = x - hi.astype(F32)
        mid = rest.astype(BF16)
        low = (rest - mid.astype(F32)).astype(BF16)
        return sum(jnp.dot(piece, fold_ref[...], preferred_element_type=F32) for piece in (hi, mid, low))

    def body(u_ref, dyga_ref, dygb_ref, y_ref, xp_ref, b_ref, c_ref, a_ref, d_ref, mask_ref, fold_ref, dp_hbm,
             du_ref, dbd_ref, dcd_ref, da_ref, dd_ref,
             dl_s, pw_s, pwr_s, carry_s, e_s, up_s, dy_s, dyp_s, dup_s, db_ref, dc_ref):
        @pl.when(pl.program_id(1) == 0)
        def _():
            carry_s[...] = jnp.zeros_like(carry_s)
            db_ref[...] = jnp.zeros_like(db_ref)
            dc_ref[...] = jnp.zeros_like(dc_ref)
            da_ref[...] = jnp.zeros_like(da_ref)
            dd_ref[...] = jnp.zeros_like(dd_ref)
            _s5_tables(a_ref, pw_s, pwr_s, S, seg)

        ar, ai = a_ref[:, :S], a_ref[:, S:]
        a1r, a1i = ar[0:1], ai[0:1]
        u = u_ref[...]
        dy = (dyga_ref[...] + dygb_ref[...]) * _gelu_grad(y_ref[...])
        dy_s[...] = dy
        xp = xp_ref[...]
        _to_step_major(u_ref, up_s, seg)
        _to_step_major(dy_s, dyp_s, seg)
        ubp = up_s[...].astype(BF16)
        dyp = dyp_s[...].astype(BF16)
        bu = jnp.dot(ubp, b_ref[...], preferred_element_type=F32)
        x_re = a1r * xp[:, :S] - a1i * xp[:, S:] + bu[:, :S]
        x_im = a1r * xp[:, S:] + a1i * xp[:, :S] + bu[:, S:]
        xs = jnp.concatenate([x_re, x_im], axis=1).astype(BF16)
        dc_ref[...] += lax.dot_general(dyp, xs, tn, preferred_element_type=F32)
        dx = lax.dot_general(dyp, c_ref[...], nn, preferred_element_type=F32)
        dl_s[...] = dx.reshape(seg, SUBLANES, 2 * S)

        def step(k, carry):
            cr, ci = carry
            i = seg - 1 - k
            lr = dl_s[i, :, :S] + (ar * cr + ai * ci)
            li = dl_s[i, :, S:] + (ar * ci - ai * cr)
            dl_s[i, :, :S] = lr
            dl_s[i, :, S:] = li
            return lr, li

        zero = jnp.zeros((SUBLANES, S), F32)
        fr, fi = lax.fori_loop(0, seg, step, (zero, zero))
        pr, pi = pw_s[seg, 0:1, :S], pw_s[seg, 0:1, S:]
        er, ei = carry_s[0:1, :S], carry_s[0:1, S:]
        for s in range(SUBLANES - 1, -1, -1):
            e_s[s:s + 1, :S] = er
            e_s[s:s + 1, S:] = ei
            er, ei = fr[s:s + 1] + (pr * er + pi * ei), fi[s:s + 1] + (pr * ei - pi * er)
        carry_s[0:1, :S] = er
        carry_s[0:1, S:] = ei
        er, ei = e_s[:, :S][None], e_s[:, S:][None]
        pw = pwr_s[...]
        pwr, pwi = pw[:, :, :S], pw[:, :, S:]
        ll = dl_s[...]
        lam = jnp.concatenate([ll[:, :, :S] + (pwr * er + pwi * ei), ll[:, :, S:] + (pwr * ei - pwi * er)],
                              axis=-1).reshape(tc, 2 * S)
        l_re, l_im = lam[:, :S], lam[:, S:]
        da_ref[0:1, :S] += jnp.sum(l_re * xp[:, :S] + l_im * xp[:, S:], axis=0, keepdims=True)
        da_ref[0:1, S:] += jnp.sum(l_im * xp[:, :S] - l_re * xp[:, S:], axis=0, keepdims=True)
        lamb = lam.astype(BF16)
        dup_s[...] = lax.dot_general(lamb, b_ref[...], nn, preferred_element_type=F32)
        for s in range(SUBLANES):
            rows = pl.ds(seg * s, seg)
            du = _segment_rows(dup_s, s, seg) + d_ref[...] * dy_s[rows, :]
            du_ref[rows, :] = du.astype(du_ref.dtype)
        db_ref[...] += lax.dot_general(ubp, lamb, tn, preferred_element_type=F32)
        dd_ref[0:1, :] += jnp.sum(dy * u, axis=0, keepdims=True)

        @pl.when(pl.program_id(1) == nt - 1)
        def _():
            dbd_ref[...] = fold_diagonal(db_ref, mask_ref, fold_ref)
            dcd_ref[...] = fold_diagonal(dc_ref, mask_ref, fold_ref)

    rev = lambda b, t: (nt - 1 - t, b)
    col = jnp.arange(2 * S)
    fold = ((col // S * SSM_STATE + col % SSM_STATE)[:, None] == jnp.arange(LANES)[None, :]).astype(BF16)
    return pl.pallas_call(
        body, name="s5_bwd",
        out_shape=(jax.ShapeDtypeStruct(dproj.shape, dproj.dtype),
                   jax.ShapeDtypeStruct((SSM_BLOCKS, LANES, LANES), F32),
                   jax.ShapeDtypeStruct((SSM_BLOCKS, LANES, LANES), F32),
                   jax.ShapeDtypeStruct((SSM_BLOCKS, SUBLANES, 2 * S), F32),
                   jax.ShapeDtypeStruct((SUBLANES, MAIN_WIDTH), F32)),
        input_output_aliases={11: 0},
        grid=(SSM_BLOCKS, nt),
        in_specs=[pl.BlockSpec((tc, LANES), rev),
                  pl.BlockSpec((tc, LANES), rev),
                  pl.BlockSpec((tc, LANES), rev),
                  pl.BlockSpec((tc, LANES), rev),
                  pl.BlockSpec((tc, 2 * S), rev),
                  pl.BlockSpec((None, LANES, 2 * S), lambda b, t: (b, 0, 0)),
                  pl.BlockSpec((None, 2 * S, LANES), lambda b, t: (b, 0, 0)),
                  pl.BlockSpec((None, SUBLANES, 2 * S), lambda b, t: (b, 0, 0)),
                  pl.BlockSpec((1, LANES), lambda b, t: (0, b)),
                  pl.BlockSpec((LANES, 2 * S), lambda b, t: (0, 0)),
                  pl.BlockSpec((2 * S, LANES), lambda b, t: (0, 0)),
                  _ANY],
        out_specs=(pl.BlockSpec((tc, LANES), rev),
                   pl.BlockSpec((None, LANES, LANES), lambda b, t: (b, 0, 0)),
                   pl.BlockSpec((None, LANES, LANES), lambda b, t: (b, 0, 0)),
                   pl.BlockSpec((None, SUBLANES, 2 * S), lambda b, t: (b, 0, 0)),
                   pl.BlockSpec((SUBLANES, LANES), lambda b, t: (0, b))),
        scratch_shapes=[pltpu.VMEM((seg, SUBLANES, 2 * S), F32),
                        pltpu.VMEM((seg + 1, SUBLANES, 2 * S), F32),
                        pltpu.VMEM((seg, SUBLANES, 2 * S), F32),
                        pltpu.VMEM((SUBLANES, 2 * S), F32),
                        pltpu.VMEM((SUBLANES, 2 * S), F32),
                        pltpu.VMEM((tc, LANES), F32),
                        pltpu.VMEM((tc, LANES), F32),
                        pltpu.VMEM((tc, LANES), F32),
                        pltpu.VMEM((tc, LANES), F32),
                        pltpu.VMEM((LANES, 2 * S), F32),
                        pltpu.VMEM((LANES, 2 * S), F32)],
        compiler_params=_params("parallel", "arbitrary"),
    )(proj, dyg_a, dyg_b, y, xp, bmat, cmat, a_rows, d_skip.reshape(1, MAIN_WIDTH), _s5_diag_mask(), fold, dproj)


_Z_COLS = slice(MAIN_WIDTH, 2 * MAIN_WIDTH)
_ZM_COLS = slice(2 * MAIN_WIDTH + MEM_WIDTH, IN_WIDTH)


def _proj_rows(tr):
    return pl.BlockSpec((tr, IN_WIDTH), lambda i: (i, 0))


def _row_specs(tr):
    main = pl.BlockSpec((tr, MAIN_WIDTH), lambda i: (i, 0))
    z = pl.BlockSpec((tr, MAIN_WIDTH), lambda i: (i, 1))
    zm = pl.BlockSpec((tr, MEM_WIDTH), lambda i: (i, IN_WIDTH // MEM_WIDTH - 1))
    mem = pl.BlockSpec((tr, MEM_WIDTH), lambda i: (i, 0))
    cat = pl.BlockSpec((tr, D_MODEL), lambda i: (i, 0))
    vec = pl.BlockSpec((1, MAIN_WIDTH), lambda i: (0, 0))
    return main, z, zm, mem, cat, vec


def _gate_a_fwd(y, t, b_glu, proj, o_mem, *, tr=256):
    L = y.shape[0]
    tr = min(tr, L)

    def body(y_ref, t_ref, b_ref, z_ref, zm_ref, om_ref, o_ref):
        yg = _gelu(y_ref[...])
        sz, _ = _silu_and_grad(z_ref[...])
        o_ref[:, :MAIN_WIDTH] = (yg * _sigmoid(t_ref[...] + b_ref[...]) * sz).astype(BF16)
        szm, _ = _silu_and_grad(zm_ref[...])
        o_ref[:, MAIN_WIDTH:] = (om_ref[...] * szm).astype(BF16)

    main, z, zm, mem, cat, vec = _row_specs(tr)
    return pl.pallas_call(
        body, name="gate_a_fwd", out_shape=jax.ShapeDtypeStruct((L, D_MODEL), BF16),
        grid=(L // tr,), in_specs=[main, main, vec, z, zm, mem], out_specs=cat,
        compiler_params=_params("parallel"),
    )(y, t, b_glu.reshape(1, MAIN_WIDTH), proj, proj, o_mem)


def _gate_a_bwd(dcat, y, t, b_glu, proj, o_mem, *, tr=256):
    L = y.shape[0]
    tr = min(tr, L)

    def body(dc_ref, y_ref, t_ref, b_ref, z_ref, zm_ref, om_ref,
             dp_ref, dt_ref, dyg_ref, dom_ref, db_ref):
        dmain = dc_ref[:, :MAIN_WIDTH]
        dmemo = dc_ref[:, MAIN_WIDTH:]
        yg = _gelu(y_ref[...])
        sg = _sigmoid(t_ref[...] + b_ref[...])
        sz, gz = _silu_and_grad(z_ref[...])
        dp_ref[:, _Z_COLS] = (dmain * (yg * sg) * gz).astype(BF16)
        dy2 = dmain * sz
        dyg_ref[...] = dy2 * sg
        dt = dy2 * yg * (sg * (1.0 - sg))
        dt_ref[...] = dt.astype(BF16)

        @pl.when(pl.program_id(0) == 0)
        def _():
            db_ref[...] = jnp.zeros_like(db_ref)

        db_ref[...] += jnp.sum(dt, axis=0, keepdims=True)
        szm, gzm = _silu_and_grad(zm_ref[...])
        dom_ref[...] = dmemo * szm
        dp_ref[:, _ZM_COLS] = (dmemo * om_ref[...] * gzm).astype(BF16)

    main, z, zm, mem, cat, vec = _row_specs(tr)
    outs = pl.pallas_call(
        body, name="gate_a_bwd",
        out_shape=(jax.ShapeDtypeStruct((L, IN_WIDTH), BF16),
                   jax.ShapeDtypeStruct((L, MAIN_WIDTH), BF16), jax.ShapeDtypeStruct((L, MAIN_WIDTH), F32),
                   jax.ShapeDtypeStruct((L, MEM_WIDTH), F32), jax.ShapeDtypeStruct((1, MAIN_WIDTH), F32)),
        grid=(L // tr,), in_specs=[cat, main, main, vec, z, zm, mem],
        out_specs=(_proj_rows(tr), main, main, mem, vec),
        compiler_params=_params("arbitrary"),
    )(dcat, y, t, b_glu.reshape(1, MAIN_WIDTH), proj, proj, o_mem)
    return outs


def _gate_b_fwd(att, proj, o_mem, *, tr=256):
    L = att.shape[0]
    tr = min(tr, L)

    def body(a_ref, z_ref, zm_ref, om_ref, o_ref):
        sz, _ = _silu_and_grad(z_ref[...])
        o_ref[:, :MAIN_WIDTH] = (a_ref[...] * sz).astype(BF16)
        szm, _ = _silu_and_grad(zm_ref[...])
        o_ref[:, MAIN_WIDTH:] = (om_ref[...] * szm).astype(BF16)

    main, z, zm, mem, cat, _ = _row_specs(tr)
    return pl.pallas_call(
        body, name="gate_b_fwd", out_shape=jax.ShapeDtypeStruct((L, D_MODEL), BF16),
        grid=(L // tr,), in_specs=[main, z, zm, mem], out_specs=cat,
        compiler_params=_params("parallel"),
    )(att, proj, proj, o_mem)


def _gate_b_bwd(dcat, att, proj, o_mem, *, tr=256):
    L = att.shape[0]
    tr = min(tr, L)

    def body(dc_ref, a_ref, z_ref, zm_ref, om_ref, da_ref, dp_ref, dom_ref, dl_ref):
        dmain = dc_ref[:, :MAIN_WIDTH]
        dmemo = dc_ref[:, MAIN_WIDTH:]
        att = a_ref[...]
        sz, gz = _silu_and_grad(z_ref[...])
        datt = dmain * sz
        da_ref[...] = datt
        dp_ref[:, _Z_COLS] = (dmain * att * gz).astype(BF16)
        szm, gzm = _silu_and_grad(zm_ref[...])
        dom_ref[...] = dmemo * szm
        dp_ref[:, _ZM_COLS] = (dmemo * om_ref[...] * gzm).astype(BF16)
        prod = datt * att
        for h in range(FOX_HEADS):
            dl_ref[h] = jnp.sum(prod[:, h * HEAD_DIM:(h + 1) * HEAD_DIM], axis=1, keepdims=True)

    main, z, zm, mem, cat, _ = _row_specs(tr)
    delta = pl.BlockSpec((FOX_HEADS, tr, 1), lambda i: (0, i, 0))
    return pl.pallas_call(
        body, name="gate_b_bwd",
        out_shape=(jax.ShapeDtypeStruct((L, MAIN_WIDTH), F32), jax.ShapeDtypeStruct((L, IN_WIDTH), BF16),
                   jax.ShapeDtypeStruct((L, MEM_WIDTH), F32), jax.ShapeDtypeStruct((FOX_HEADS, L, 1), F32)),
        grid=(L // tr,), in_specs=[cat, main, z, zm, mem], out_specs=(main, _proj_rows(tr), mem, delta),
        compiler_params=_params("parallel"),
    )(dcat, att, proj, proj, o_mem)


_MEM_Q_COL = (2 * MAIN_WIDTH) // HEAD_DIM
_NT = (((1,), (1,)), ((), ()))
_TN = (((0,), (0,)), ((), ()))


def _mem_probs(q_ref, k_ref):
    qs = (q_ref[...] * (HEAD_DIM ** -0.5)).astype(BF16)
    s = lax.dot_general(qs, k_ref[...].astype(BF16), _NT, preferred_element_type=F32)
    e = jnp.exp(s - jnp.max(s, axis=-1, keepdims=True))
    return qs, e / jnp.sum(e, axis=-1, keepdims=True)


def _mem_attn_fwd(proj, kvm, *, tq=2048):
    L = proj.shape[0]
    tq = min(tq, L)

    def body(q_ref, k_ref, v_ref, o_ref):
        _, p = _mem_probs(q_ref, k_ref)
        o_ref[...] = jnp.dot(p.astype(BF16), v_ref[...].astype(BF16), preferred_element_type=F32)

    return pl.pallas_call(
        body, name="mem_attn_fwd", out_shape=jax.ShapeDtypeStruct((L, MEM_WIDTH), F32),
        grid=(MEM_HEADS, L // tq),
        in_specs=[pl.BlockSpec((tq, HEAD_DIM), lambda h, i: (i, _MEM_Q_COL + h)),
                  pl.BlockSpec((N_MEM, HEAD_DIM), lambda h, i: (0, h)),
                  pl.BlockSpec((N_MEM, HEAD_DIM), lambda h, i: (0, MEM_HEADS + h))],
        out_specs=pl.BlockSpec((tq, HEAD_DIM), lambda h, i: (i, h)),
        compiler_params=_params("parallel", "parallel"),
    )(proj, kvm, kvm)


def _mem_attn_bwd(proj, kvm, do, dproj, *, tq=2048):
    L = proj.shape[0]
    tq = min(tq, L)

    def body(q_ref, k_ref, v_ref, do_ref, dp_hbm, dq_ref, dk_ref, dv_ref):
        @pl.when(pl.program_id(1) == 0)
        def _():
            dk_ref[...] = jnp.zeros_like(dk_ref)
            dv_ref[...] = jnp.zeros_like(dv_ref)

        qs, p = _mem_probs(q_ref, k_ref)
        dob = do_ref[...].astype(BF16)
        dp = lax.dot_general(dob, v_ref[...].astype(BF16), _NT, preferred_element_type=F32)
        ds = p * (dp - jnp.sum(p * dp, axis=-1, keepdims=True))
        dsb = ds.astype(BF16)
        dq = jnp.dot(dsb, k_ref[...].astype(BF16), preferred_element_type=F32) * (HEAD_DIM ** -0.5)
        dq_ref[...] = dq.astype(BF16)
        dk_ref[...] += lax.dot_general(dsb, qs, _TN, preferred_element_type=F32)
        dv_ref[...] += lax.dot_general(p.astype(BF16), dob, _TN, preferred_element_type=F32)

    dproj, dk, dv = pl.pallas_call(
        body, name="mem_attn_bwd",
        out_shape=(jax.ShapeDtypeStruct(dproj.shape, dproj.dtype),
                   jax.ShapeDtypeStruct((N_MEM, MEM_WIDTH), F32),
                   jax.ShapeDtypeStruct((N_MEM, MEM_WIDTH), F32)),
        grid=(MEM_HEADS, L // tq),
        in_specs=[pl.BlockSpec((tq, HEAD_DIM), lambda h, i: (i, _MEM_Q_COL + h)),
                  pl.BlockSpec((N_MEM, HEAD_DIM), lambda h, i: (0, h)),
                  pl.BlockSpec((N_MEM, HEAD_DIM), lambda h, i: (0, MEM_HEADS + h)),
                  pl.BlockSpec((tq, HEAD_DIM), lambda h, i: (i, h)),
                  _ANY],
        out_specs=(pl.BlockSpec((tq, HEAD_DIM), lambda h, i: (i, _MEM_Q_COL + h)),
                   pl.BlockSpec((N_MEM, HEAD_DIM), lambda h, i: (0, h)),
                   pl.BlockSpec((N_MEM, HEAD_DIM), lambda h, i: (0, h))),
        input_output_aliases={4: 0},
        compiler_params=_params("parallel", "arbitrary"),
    )(proj, kvm, kvm, do, dproj)
    return dproj, jnp.concatenate([dk, dv], axis=1)


def _tile_cumsum(x, row, reverse):
    for sh in (1, 2, 4):
        if reverse:
            x = x + jnp.where(row < SUBLANES - sh, pltpu.roll(x, SUBLANES - sh, 0), 0.0)
        else:
            x = x + jnp.where(row >= sh, pltpu.roll(x, sh, 0), 0.0)
    return x


def _fgate_fwd(pre, b_pad):
    L = pre.shape[0]
    n8 = L // SUBLANES

    def body(p_ref, b_ref, o_ref):
        row = lax.broadcasted_iota(jnp.int32, (SUBLANES, LANES), 0)
        b = b_ref[...]

        def step(i, carry):
            x = p_ref[i] + b
            logf = jnp.minimum(x, 0.0) - jnp.log(1.0 + jnp.exp(-jnp.abs(x)))
            t = _tile_cumsum(logf, row, False) + carry
            o_ref[i] = t
            return t[SUBLANES - 1:SUBLANES, :]

        lax.fori_loop(0, n8, step, jnp.zeros((1, LANES), F32))

    out = pl.pallas_call(
        body, name="fgate_fwd", out_shape=jax.ShapeDtypeStruct((n8, SUBLANES, LANES), F32),
        compiler_params=_params(),
    )(pre.reshape(n8, SUBLANES, LANES), b_pad.reshape(1, LANES))
    return out.reshape(L, LANES)


def _fgate_bwd(dfcum, pre, b_pad):
    L = pre.shape[0]
    n8 = L // SUBLANES

    def body(d_ref, p_ref, b_ref, o_ref, s_ref):
        row = lax.broadcasted_iota(jnp.int32, (SUBLANES, LANES), 0)
        b = b_ref[...]

        def step(k, carry):
            c, acc = carry
            i = n8 - 1 - k
            t = _tile_cumsum(d_ref[i], row, True) + c
            dpre = t * _sigmoid(-(p_ref[i] + b))
            o_ref[i] = dpre
            return t[0:1, :], acc + dpre

        _, acc = lax.fori_loop(0, n8, step, (jnp.zeros((1, LANES), F32), jnp.zeros((SUBLANES, LANES), F32)))
        s_ref[...] = jnp.sum(acc, axis=0, keepdims=True)

    dpre, db = pl.pallas_call(
        body, name="fgate_bwd",
        out_shape=(jax.ShapeDtypeStruct((n8, SUBLANES, LANES), F32), jax.ShapeDtypeStruct((1, LANES), F32)),
        compiler_params=_params(),
    )(dfcum.reshape(n8, SUBLANES, LANES), pre.reshape(n8, SUBLANES, LANES), b_pad.reshape(1, LANES))
    return dpre.reshape(L, LANES), db


FOX_BLOCK = 1024


def _fox_scores(qs, k, fk, diagonal, row0=0):
    s = lax.dot_general(qs, k, _NT, preferred_element_type=F32) - fk
    if diagonal:
        row = row0 + lax.broadcasted_iota(jnp.int32, s.shape, 0)
        col = lax.broadcasted_iota(jnp.int32, s.shape, 1)
        s = jnp.where(row >= col, s, NEG_BIG)
    return s


def _fox_diagonal_parts(tq):
    half = tq // 2
    return ((slice(0, half), half), (slice(half, tq), tq))


def _fox_specs(tq, L):
    nq = L // tq
    return dict(
        rows=lambda off: pl.BlockSpec((tq, HEAD_DIM), lambda h, i: (i, off + h)),
        seq=lambda off: pl.BlockSpec((L, HEAD_DIM), lambda h, i: (0, off + h)),
        col=pl.BlockSpec((None, None, tq, 1), lambda h, i: (h, i, 0, 0)),
        col_all=pl.BlockSpec((None, nq, tq, 1), lambda h, i: (h, 0, 0, 0)),
        row=pl.BlockSpec((None, None, 1, tq), lambda h, i: (h, i, 0, 0)),
        row_all=pl.BlockSpec((None, nq, 1, tq), lambda h, i: (h, 0, 0, 0)))


FOX_FWD_HEADS = 2
FOX_FWD_BLOCK = 1024


def _fox_fwd(proj, kv, fk):
    L = proj.shape[0]
    tq = min(FOX_FWD_BLOCK, L)
    nq = L // tq
    nh = FOX_FWD_HEADS
    W = nh * HEAD_DIM
    lse_shape = fk.shape[:2] + (fk.shape[3], 1)
    fk = fk.reshape(FOX_HEADS, nq, 1, tq)

    def body(q_ref, k_ref, v_ref, fk_ref, o_ref, lse_ref, m_s, l_s, acc_s):
        qi = pl.program_id(1)
        cols = [slice(a * HEAD_DIM, (a + 1) * HEAD_DIM) for a in range(nh)]
        qs = [(q_ref[:, cs] * (HEAD_DIM ** -0.5)).astype(BF16) for cs in cols]
        m_s[...] = jnp.full_like(m_s, NEG_BIG)
        l_s[...] = jnp.zeros_like(l_s)
        acc_s[...] = jnp.zeros_like(acc_s)

        def block(j, diagonal):
            r0 = pl.multiple_of(j * tq, tq)
            for a, cs in enumerate(cols):
                s = _fox_scores(qs[a], k_ref[pl.ds(r0, tq), cs], fk_ref[a, j], diagonal)
                m_new = jnp.maximum(m_s[a], jnp.max(s, axis=-1, keepdims=True))
                alpha = jnp.exp(m_s[a] - m_new)
                p = jnp.exp(s - m_new)
                l_s[a] = alpha * l_s[a] + jnp.sum(p, axis=-1, keepdims=True)
                acc_s[a] = alpha * acc_s[a] + jnp.dot(p.astype(BF16), v_ref[pl.ds(r0, tq), cs],
                                                      preferred_element_type=F32)
                m_s[a] = m_new

        def below(j, carry):
            block(j, False)
            return carry

        lax.fori_loop(0, qi, below, 0)
        block(qi, True)
        for a, cs in enumerate(cols):
            o_ref[:, cs] = acc_s[a] / l_s[a]
            lse_ref[a] = m_s[a] + jnp.log(l_s[a])

    att, lse = pl.pallas_call(
        body, name="fox_fwd",
        out_shape=(jax.ShapeDtypeStruct((L, MAIN_WIDTH), F32),
                   jax.ShapeDtypeStruct((FOX_HEADS, nq, tq, 1), F32)),
        grid=(FOX_HEADS // nh, nq),
        in_specs=[pl.BlockSpec((tq, W), lambda h, i: (i, h)),
                  pl.BlockSpec((L, W), lambda h, i: (0, h)),
                  pl.BlockSpec((L, W), lambda h, i: (0, FOX_HEADS // nh + h)),
                  pl.BlockSpec((nh, nq, 1, tq), lambda h, i: (h, 0, 0, 0))],
        out_specs=(pl.BlockSpec((tq, W), lambda h, i: (i, h)),
                   pl.BlockSpec((nh, None, tq, 1), lambda h, i: (h, i, 0, 0))),
        scratch_shapes=[pltpu.VMEM((nh, tq, 1), F32), pltpu.VMEM((nh, tq, 1), F32),
                        pltpu.VMEM((nh, tq, HEAD_DIM), F32)],
        compiler_params=_params("parallel", "parallel"),
    )(proj, kv, kv, fk)
    return att, lse.reshape(lse_shape)


def _fox_bwd(proj, kv, fk, lse, delta, datt, dproj):
    L = proj.shape[0]
    tq = min(FOX_BLOCK, L)
    nq = L // tq
    sp = _fox_specs(tq, L)

    def body(q_ref, k_ref, v_ref, fk_ref, lse_ref, dl_ref, do_ref, dp_hbm,
             dq_ref, dk_ref, dv_ref, dfq_ref, dfk_ref, dk_s, dv_s, df_s, dq_s, dfq_s):
        ki = pl.program_id(1)

        @pl.when(ki == 0)
        def _():
            dq_s[...] = jnp.zeros_like(dq_s)
            dfq_s[...] = jnp.zeros_like(dfq_s)

        k, v, fk = k_ref[...], v_ref[...], fk_ref[...]
        dk_s[...] = jnp.zeros_like(dk_s)
        dv_s[...] = jnp.zeros_like(dv_s)
        df_s[...] = jnp.zeros_like(df_s)

        def block(i, rows, width, diagonal):
            n = rows.stop - rows.start
            r0 = pl.multiple_of(i * tq + rows.start, n)
            qs = (q_ref[pl.ds(r0, n), :] * (HEAD_DIM ** -0.5)).astype(BF16)
            dob = do_ref[pl.ds(r0, n), :].astype(BF16)
            kw, vw = k[:width], v[:width]
            p = jnp.exp(_fox_scores(qs, kw, fk[:, :width], diagonal, rows.start) - lse_ref[i][rows])
            dp = lax.dot_general(dob, vw, _NT, preferred_element_type=F32)
            ds = p * (dp - dl_ref[i][rows])
            dsb = ds.astype(BF16)
            dv_s[:width] += lax.dot_general(p.astype(BF16), dob, _TN, preferred_element_type=F32)
            dk_s[:width] += lax.dot_general(dsb, qs, _TN, preferred_element_type=F32)
            df_s[:, :width] -= jnp.sum(ds, axis=0, keepdims=True)
            dq_s[i, rows] += jnp.dot(dsb, kw, preferred_element_type=F32)
            dfq_s[i, rows] += jnp.sum(ds, axis=1, keepdims=True)

        def above(i, carry):
            block(i, slice(0, tq), tq, False)
            return carry

        for rows, width in _fox_diagonal_parts(tq):
            block(ki, rows, width, True)
        lax.fori_loop(ki + 1, nq, above, 0)
        dk_ref[...] = dk_s[...].astype(BF16)
        dv_ref[...] = dv_s[...].astype(BF16)
        dfk_ref[...] = df_s[...]

        @pl.when(ki == nq - 1)
        def _():
            dq_ref[...] = (dq_s[...].reshape(L, HEAD_DIM) * (HEAD_DIM ** -0.5)).astype(BF16)
            dfq_ref[...] = dfq_s[...]

    return pl.pallas_call(
        body, name="fox_bwd",
        out_shape=(jax.ShapeDtypeStruct(dproj.shape, dproj.dtype),
                   jax.ShapeDtypeStruct((L, MAIN_WIDTH), BF16),
                   jax.ShapeDtypeStruct((L, MAIN_WIDTH), BF16),
                   jax.ShapeDtypeStruct((FOX_HEADS, nq, tq, 1), F32),
                   jax.ShapeDtypeStruct((FOX_HEADS, nq, 1, tq), F32)),
        grid=(FOX_HEADS, nq),
        in_specs=[sp["seq"](0), sp["rows"](0), sp["rows"](FOX_HEADS), sp["row"],
                  sp["col_all"], sp["col_all"], sp["seq"](0), _ANY],
        out_specs=(sp["seq"](0), sp["rows"](0), sp["rows"](0), sp["col_all"], sp["row"]),
        input_output_aliases={7: 0},
        scratch_shapes=[pltpu.VMEM((tq, HEAD_DIM), F32), pltpu.VMEM((tq, HEAD_DIM), F32),
                        pltpu.VMEM((1, tq), F32), pltpu.VMEM((nq, tq, HEAD_DIM), F32),
                        pltpu.VMEM((nq, tq, 1), F32)],
        compiler_params=_params("parallel", "arbitrary"),
    )(proj, kv, kv, fk, lse, delta, datt, dproj)


def _pad_lanes(a):
    return jnp.pad(a, ((0, 0), (0, LANES - a.shape[1])))


def _mem_branch_fwd(memn, w_mk, proj, tag):
    kvm = _mm(memn, w_mk, name="mem_kv_" + tag)
    return kvm, _mem_attn_fwd(proj, kvm)


def _mem_branch_bwd(mem, g, w_mk, proj, memn, kvm, do_mem, dproj, tag):
    dproj, dkvm = _mem_attn_bwd(proj, kvm, do_mem, dproj)
    dkvm = dkvm.astype(BF16)
    dw_mk = _mm(memn, dkvm, ta=True, name="dw_mem_kv_" + tag, out_dtype=BF16)
    dmemn = _mm(dkvm, w_mk, tb=True, name="dmemn_" + tag)
    _, dg = _rmsnorm_bwd(mem, g, dmemn, name="mem_norm_bwd_" + tag, dx_dtype=BF16)
    return dproj, dw_mk, dg


def _local_step(x, mem, target, w, fetch=None, grads_ready=None):
    if grads_ready is None:
        grads_ready = lambda group, grads, token: token
    L = x.shape[0]
    g = {}
    w = dict(w)

    b_re_t = jnp.transpose(w["b_re"], (0, 2, 1))
    b_im_t = jnp.transpose(w["b_im"], (0, 2, 1))
    ar, ai, bbr_t, bbi_t = _s5_prep(w["lam_re"], w["lam_im"], w["log_step"], b_re_t, b_im_t)
    bmat, cmat = _s5_block_mats(bbr_t, bbi_t, w["c_re"], w["c_im"])
    a_rows = _s5_a_rows(ar, ai)

    hn0 = _rmsnorm_fwd(x, w["pre_norm_g"][0], name="pre_norm_0", out_dtype=BF16)
    memn0 = _rmsnorm_fwd(mem, w["mem_norm_g"][0], name="mem_norm_0", out_dtype=BF16)
    memn1 = _rmsnorm_fwd(mem, w["mem_norm_g"][1], name="mem_norm_1", out_dtype=BF16)
    if fetch is not None:
        w.update(fetch("a", [hn0, memn0, memn1, bmat, cmat, a_rows]))
    proj_a = _mm(hn0, w["w_in_a"], name="in_proj_a")
    y, yg, xp = _s5_fwd(proj_a, bmat, cmat, a_rows, w["d_skip"])
    if fetch is not None:
        w.update(fetch("b", yg))
    t = _mm(yg, w["w_glu"], name="glu_proj")
    kvm0, om0 = _mem_branch_fwd(memn0, w["w_mem_kv"][0], proj_a, "0")
    cat0 = _gate_a_fwd(y, t, w["b_glu"], proj_a, om0)
    o0 = _mm(cat0, w["w_out"][0], name="out_proj_0")
    h1, kv_in, hn1 = _post_norm_and_next_norms(
        o0, w["post_norm_g"][0], x, w["kv_norm_g"], w["pre_norm_g"][1], name="post_norm_0_kv_pre_norm_1")

    if fetch is not None:
        w.update(fetch("c", kv_in))
    kv = _mm(kv_in, w["w_kv"], name="kv_proj", out_dtype=BF16)
    pre_f = _mm(kv_in, w["w_fgate"], name="fgate_proj")
    b_f = jnp.pad(w["b_fgate"], (0, LANES - FOX_HEADS))
    fcum = _fgate_fwd(pre_f, b_f)
    fc = jnp.transpose(fcum[:, :FOX_HEADS])
    tq = min(FOX_BLOCK, L)
    fk = fc.reshape(FOX_HEADS, L // tq, 1, tq)

    proj_b = _mm(hn1, w["w_in_b"], name="in_proj_b")
    att, lse = _fox_fwd(proj_b, kv, fk)
    kvm1, om1 = _mem_branch_fwd(memn1, w["w_mem_kv"][1], proj_b, "1")
    cat1 = _gate_b_fwd(att, proj_b, om1)
    o1 = _mm(cat1, w["w_out"][1], name="out_proj_1")
    dh2, loss_row = _final_norm_loss(o1, w["post_norm_g"][1], h1, target)

    do1, dpost1 = _rmsnorm_bwd(o1, w["post_norm_g"][1], dh2, name="post_norm_bwd_1", dx_dtype=BF16)
    dcat1 = _mm(do1, w["w_out"][1], tb=True, name="dcat_1", out_dtype=BF16)
    g["w_out_1"] = _mm(cat1, do1, ta=True, name="dw_out_1", out_dtype=BF16)
    datt, dproj_b, dom1, delta = _gate_b_bwd(dcat1, att, proj_b, om1)
    dproj_b, g["w_mem_kv_1"], dmemg1 = _mem_branch_bwd(mem, w["mem_norm_g"][1], w["w_mem_kv"][1], proj_b,
                                                      memn1, kvm1, dom1, dproj_b, "1")
    delta = delta.reshape(lse.shape)
    dproj_b, dk, dv, dfq, dfk = _fox_bwd(proj_b, kv, fk, lse, delta, datt, dproj_b)
    g["w_in_b"] = _mm(hn1, dproj_b, ta=True, name="dw_in_b", out_dtype=BF16, shards=N_CHIPS)
    dhn1 = _mm(dproj_b, w["w_in_b"], tb=True, name="dhn_1")

    dkv = jnp.concatenate([dk, dv], axis=1)
    g["w_kv"] = _mm(kv_in, dkv, ta=True, name="dw_kv", out_dtype=BF16, shards=N_CHIPS)
    dkv_in_a = _mm(dkv, w["w_kv"], tb=True, name="dkv_in_kv")
    dfcum = _pad_lanes(jnp.transpose(dfq.reshape(FOX_HEADS, L) + dfk.reshape(FOX_HEADS, L)))
    dpre_f, db_f = _fgate_bwd(dfcum, pre_f, b_f)
    g["b_fgate"] = db_f[0, :FOX_HEADS]
    g["w_fgate"] = _mm(kv_in, dpre_f, ta=True, name="dw_fgate")[:, :FOX_HEADS]
    dkv_in_b = _mm(dpre_f, w["w_fgate"], tb=True, name="dkv_in_fgate")
    dh1, g["kv_norm_g"], dpre1 = _rmsnorm_bwd_pair(h1, w["kv_norm_g"], (dkv_in_a, dkv_in_b), w["pre_norm_g"][1],
                                                   dhn1, adds=(dh2,), name="kv_pre_norm_bwd")
    dh1 = grads_ready("b", g, dh1)

    do0, dpost0 = _rmsnorm_bwd(o0, w["post_norm_g"][0], dh1, name="post_norm_bwd_0", dx_dtype=BF16)
    dcat0 = _mm(do0, w["w_out"][0], tb=True, name="dcat_0", out_dtype=BF16)
    g["w_out_0"] = _mm(cat0, do0, ta=True, name="dw_out_0", out_dtype=BF16)
    dcat0 = grads_ready("b_send", g, dcat0)
    dproj_a, dt, dyg_a, dom0, db_glu = _gate_a_bwd(dcat0, y, t, w["b_glu"], proj_a, om0)
    g["b_glu"] = db_glu[0]
    g["w_glu"] = _mm(yg, dt, ta=True, name="dw_glu", out_dtype=BF16)
    dyg_b = _mm(dt, w["w_glu"], tb=True, name="dyg")
    dproj_a, g["w_mem_kv_0"], dmemg0 = _mem_branch_bwd(mem, w["mem_norm_g"][0], w["w_mem_kv"][0], proj_a,
                                                      memn0, kvm0, dom0, dproj_a, "0")
    dyg_b = grads_ready("a1", g, dyg_b)
    dproj_a, db_blk, dc_blk, da_rows, dd_skip = _s5_bwd(proj_a, dyg_a, dyg_b, y, xp, bmat, cmat, a_rows,
                                                        w["d_skip"], dproj_a)
    dproj_a = grads_ready("a1_send", g, dproj_a)
    g["d_skip"] = dd_skip[0]
    g["w_in_a"] = _mm(hn0, dproj_a, ta=True, name="dw_in_a", out_dtype=BF16, shards=N_CHIPS)
    dproj_a = grads_ready("a2", g, dproj_a)
    dhn0 = _mm(dproj_a, w["w_in_a"], tb=True, name="dhn_0")
    grad_x, dpre0 = _rmsnorm_bwd(x, w["pre_norm_g"][0], dhn0, adds=(dh1,), name="pre_norm_bwd_0")

    dbb = _s5_unfold(db_blk)
    dcc = _s5_unfold(dc_blk)
    g["c_re"], g["c_im"] = dcc[0], -dcc[1]
    d_ar = da_rows[:, 0, :STATE_COLS].reshape(SSM_GROUPS, SSM_STATE)
    d_ai = da_rows[:, 0, STATE_COLS:].reshape(SSM_GROUPS, SSM_STATE)
    dlr, dli, dls, dbr_t, dbi_t = _s5_prep_bwd(w["lam_re"], w["lam_im"], w["log_step"], b_re_t, b_im_t,
                                               d_ar, d_ai, dbb[0], dbb[1])
    g["lam_re"], g["lam_im"], g["log_step"] = dlr, dli, dls[:, 0]
    g["b_re"] = jnp.transpose(dbr_t, (0, 2, 1))
    g["b_im"] = jnp.transpose(dbi_t, (0, 2, 1))
    g["pre_norm_g"] = jnp.stack([dpre0, dpre1])
    g["post_norm_g"] = jnp.stack([dpost0, dpost1])
    g["mem_norm_g"] = jnp.stack([dmemg0, dmemg1])
    return loss_row, grad_x, g


_MESH = pl.DeviceIdType.MESH
_ANY = pl.BlockSpec(memory_space=pl.ANY)


def _place():
    x, y, c = lax.axis_index("x"), lax.axis_index("y"), lax.axis_index("c")
    chips = [(1 - x, y), (x, 1 - y), (1 - x, 1 - y)]
    return x, y, c, chips


_HBM = pl.BlockSpec(memory_space=pltpu.HBM)
_SEM = pl.BlockSpec(memory_space=pltpu.SEMAPHORE)
_SIDE = pltpu.SideEffectType.DATAFLOW_SIDE_EFFECTING


def _in_hbm(a):
    return pltpu.with_memory_space_constraint(a, pltpu.HBM)


def _hbm_like(a):
    return pltpu.HBM(a.shape, a.dtype)


def _ici_copies(srcs, lands, send_sem, recv_sem, src_at, dst_at, wait_at, to_sibling=False):
    x, y, c, chips = _place()
    peers = [(x, y, 1 - c)] if to_sibling else [(cx, cy, c) for cx, cy in chips]
    m = len(peers)
    start, wait = [], []
    for i in range(len(srcs)):
        for k, (px, py, pc) in enumerate(peers):
            sem = dict(send_sem=send_sem.at[m * i + k], recv_sem=recv_sem.at[m * i + k],
                       device_id=(px, py, pc), device_id_type=_MESH)
            src = src_at(srcs[i], 2 * px + py, c)
            start.append(pltpu.make_async_remote_copy(src_ref=src, dst_ref=dst_at(lands[i], 2 * x + y, k, c), **sem))
            wait.append(pltpu.make_async_remote_copy(src_ref=src, dst_ref=wait_at(lands[i], 2 * px + py, k, c), **sem))
    return start, wait


def _route_peers(route):
    return 1 if len(route) == 4 else 3


_BLOCK_ROUTE = (lambda s, j, c: s, lambda l, me, k, c: l.at[me, c], lambda l, j, k, c: l.at[j, c])


def _ici_start(srcs, lands, token, route, *, name):
    n = len(srcs)

    def body(*refs):
        start, _ = _ici_copies(refs[:n], refs[n:2 * n], refs[2 * n + 1], refs[2 * n + 2], *route)
        for cp in start:
            cp.start()

    sems = pltpu.SemaphoreType.DMA((_route_peers(route) * n,))
    outs = pl.pallas_call(
        body, name=name,
        out_shape=(sems, sems, *[_hbm_like(a) for a in srcs], *[_hbm_like(a) for a in lands], _hbm_like(token)),
        in_specs=[_HBM] * (2 * n + 1), out_specs=(_SEM, _SEM, *[_HBM] * (2 * n + 1)),
        input_output_aliases={i: 2 + i for i in range(2 * n + 1)},
        compiler_params=pltpu.CompilerParams(has_side_effects=_SIDE),
    )(*[_in_hbm(a) for a in srcs], *[_in_hbm(a) for a in lands], _in_hbm(token))
    return (outs[0], outs[1], list(outs[2:2 + n]), list(outs[2 + n:2 + 2 * n])), outs[2 + 2 * n]


def _ici_wait(handle, after, route, *, name):
    send_sem, recv_sem, srcs, lands = handle
    n = len(srcs)
    after = list(after) if isinstance(after, (list, tuple)) else [after]

    def body(*refs):
        _, wait = _ici_copies(refs[:n], refs[n:2 * n], refs[2 * n], refs[2 * n + 1], *route)
        for cp in wait:
            cp.wait_send()
            cp.wait_recv()

    outs = pl.pallas_call(
        body, name=name,
        out_shape=(*[_hbm_like(a) for a in srcs], *[_hbm_like(a) for a in lands]),
        in_specs=[_HBM] * (2 * n) + [_SEM, _SEM] + [_ANY] * len(after), out_specs=tuple([_HBM] * (2 * n)),
        input_output_aliases={i: i for i in range(2 * n)},
        compiler_params=pltpu.CompilerParams(has_side_effects=_SIDE),
    )(*srcs, *lands, send_sem, recv_sem, *after)
    return list(outs[:n]), list(outs[n:])


_GATHER_ROUTE = (lambda s, j, c: s.at[c], lambda l, me, k, c: l.at[me, c], lambda l, j, k, c: l.at[j, c])
_SCATTER_ROUTE = (lambda s, j, c: s.at[j], lambda l, me, k, c: l.at[k], lambda l, j, k, c: l.at[k])
_SHARE_ROUTE = (lambda s, j, c: s, lambda l, me, k, c: l.at[c], lambda l, j, k, c: l.at[1 - c], True)
_SWAP_ROUTE = (lambda s, j, c: s.at[:, 1 - c], lambda l, me, k, c: l, lambda l, j, k, c: l, True)


def _gather_forward(lands, tag, own=False):
    n = len(lands)
    m = 4 if own else 3

    def body(*refs):
        ins, outs = refs[:n], refs[n:2 * n]
        send_sem, recv_sem = refs[2 * n:]
        x, y, c, chips = _place()
        slots = [2 * cx + cy for cx, cy in chips] + [2 * x + y]

        def copy(i, k, half):
            return pltpu.make_async_remote_copy(
                src_ref=ins[i].at[slots[k], half], dst_ref=outs[i].at[slots[k], half],
                send_sem=send_sem.at[m * i + k], recv_sem=recv_sem.at[m * i + k],
                device_id=(x, y, 1 - c), device_id_type=_MESH)

        copies = [copy(i, k, c) for i in range(n) for k in range(m)]
        for cp in copies:
            cp.start()
        for i in range(n):
            for k in range(m):
                copy(i, k, 1 - c).wait_recv()
        for cp in copies:
            cp.wait_send()

    return pl.pallas_call(
        body, name="gather_forward_to_sibling_" + tag,
        out_shape=[jax.ShapeDtypeStruct(a.shape, a.dtype) for a in lands],
        in_specs=[_ANY] * n, out_specs=[_ANY] * n,
        input_output_aliases={i: i for i in range(n)},
        scratch_shapes=[pltpu.SemaphoreType.DMA((m * n,)), pltpu.SemaphoreType.DMA((m * n,))],
    )(*lands)


def _swap_halves(grads, tag):
    n = len(grads)

    def body(*refs):
        ins, outs = refs[:n], refs[n:2 * n]
        send_sem, recv_sem = refs[2 * n:]
        x, y, c, _ = _place()
        copies = [pltpu.make_async_remote_copy(
            src_ref=ins[i].at[:, 1 - c], dst_ref=outs[i],
            send_sem=send_sem.at[i], recv_sem=recv_sem.at[i],
            device_id=(x, y, 1 - c), device_id_type=_MESH) for i in range(n)]
        for cp in copies:
            cp.start()
        for cp in copies:
            cp.wait()

    return pl.pallas_call(
        body, name="grad_swap_halves_" + tag,
        out_shape=[jax.ShapeDtypeStruct((N_CHIPS,) + g.shape[2:], g.dtype) for g in grads],
        in_specs=[_ANY] * n, out_specs=[_ANY] * n,
        scratch_shapes=[pltpu.SemaphoreType.DMA((n,)), pltpu.SemaphoreType.DMA((n,))],
    )(*grads)


def _sum_rows(h, C):
    return max(d for d in range(SUBLANES, h + 1, SUBLANES) if h % d == 0 and d * C <= 1 << 20)


SUM_STEPS = 4


def _pair_sums(gs, rs, c_idx, *, name):
    n = len(gs)
    rows = [g.shape[2] // SUM_STEPS for g in gs]

    def body(c_ref, *refs):
        for g_ref, r_ref, o_ref in zip(refs[:n], refs[n:2 * n], refs[2 * n:]):
            o_ref[...] = (g_ref[...].astype(F32) + r_ref[...].astype(F32)).astype(o_ref.dtype)

    return pl.pallas_call(
        body, name=name,
        out_shape=[jax.ShapeDtypeStruct((N_CHIPS,) + g.shape[2:], g.dtype) for g in gs],
        grid_spec=pltpu.PrefetchScalarGridSpec(
            num_scalar_prefetch=1, grid=(N_CHIPS, SUM_STEPS),
            in_specs=[pl.BlockSpec((None, None, tr, g.shape[3]), lambda j, i, s: (j, s[0], i, 0))
                      for g, tr in zip(gs, rows)]
            + [pl.BlockSpec((None, tr, g.shape[3]), lambda j, i, s: (j, i, 0)) for g, tr in zip(gs, rows)],
            out_specs=[pl.BlockSpec((None, tr, g.shape[3]), lambda j, i, s: (j, i, 0)) for g, tr in zip(gs, rows)]),
        compiler_params=_params("parallel", "parallel"),
    )(c_idx, *gs, *rs)


def _owner_sums(ss, rs, jc_idx, *, name):
    n = len(ss)
    rows = [s.shape[1] // SUM_STEPS for s in ss]

    def body(jc_ref, *refs):
        for s_ref, r_ref, m_ref, o_ref in zip(refs[:n], refs[n:2 * n], refs[2 * n:3 * n], refs[3 * n:]):
            acc = s_ref[...].astype(F32)
            for k in range(3):
                acc = acc + r_ref[k].astype(F32)
            m_ref[...] = acc
            o_ref[...] = acc

    outs = pl.pallas_call(
        body, name=name,
        out_shape=[jax.ShapeDtypeStruct(s.shape[1:], F32) for s in ss]
        + [jax.ShapeDtypeStruct((2,) + s.shape[1:], F32) for s in ss],
        grid_spec=pltpu.PrefetchScalarGridSpec(
            num_scalar_prefetch=1, grid=(SUM_STEPS,),
            in_specs=[pl.BlockSpec((None, tr, s.shape[2]), lambda i, p: (p[0], i, 0)) for s, tr in zip(ss, rows)]
            + [pl.BlockSpec((3, tr, s.shape[2]), lambda i, p: (0, i, 0)) for s, tr in zip(ss, rows)],
            out_specs=[pl.BlockSpec((tr, s.shape[2]), lambda i, p: (i, 0)) for s, tr in zip(ss, rows)]
            + [pl.BlockSpec((None, tr, s.shape[2]), lambda i, p: (p[1], i, 0)) for s, tr in zip(ss, rows)]),
        compiler_params=_params("parallel"),
    )(jc_idx, *ss, *rs)
    return outs[:n], outs[n:]


def _chip_sums(grads, c_idx, tag):
    views = [g.reshape(N_CHIPS, 2, g.shape[1] // 2, g.shape[2]) for g in grads]
    arrived = _swap_halves(views, tag)
    return _pair_sums(views, arrived, c_idx, name=f"grad_pair_sums_{tag}")


def _sum_devices(blocks):
    R = blocks.shape[2]
    tr = _sum_rows(R, 2 * N_CHIPS * LANES)

    def body(b_ref, o_ref):
        acc = b_ref[0, 0]
        for d in range(1, 2 * N_CHIPS):
            acc = acc + b_ref[d // 2, d % 2]
        o_ref[...] = acc

    return pl.pallas_call(
        body, name="sum_small_over_devices", out_shape=jax.ShapeDtypeStruct((R, LANES), F32),
        grid=(R // tr,),
        in_specs=[pl.BlockSpec((N_CHIPS, 2, tr, LANES), lambda i: (0, 0, i, 0))],
        out_specs=pl.BlockSpec((tr, LANES), lambda i: (i, 0)),
        compiler_params=_params("parallel"),
    )(blocks)


def _adamw(w, g, m, v, *, name):
    R, C = w.shape
    tr = max(d for d in range(SUBLANES, R + 1, SUBLANES)
             if R % d == 0 and 7 * 2 * d * C * 4 <= VMEM_LIMIT_BYTES // 2)

    def body(w_ref, g_ref, m_ref, v_ref, d_ref, nm_ref, nv_ref):
        g = g_ref[...]
        m = ADAM_B1 * m_ref[...] + (1.0 - ADAM_B1) * g
        v = ADAM_B2 * v_ref[...] + (1.0 - ADAM_B2) * (g * g)
        nm_ref[...] = m
        nv_ref[...] = v
        m_hat = m / (1.0 - ADAM_B1 ** ADAM_STEP)
        v_hat = v / (1.0 - ADAM_B2 ** ADAM_STEP)
        d_ref[...] = -ADAM_LR * (m_hat / (jnp.sqrt(v_hat) + ADAM_EPS) + ADAM_WD * w_ref[...])

    blk = pl.BlockSpec((tr, C), lambda i: (i, 0))
    sds = jax.ShapeDtypeStruct((R, C), F32)
    return pl.pallas_call(
        body, name=name, out_shape=(sds, sds, sds), grid=(R // tr,),
        in_specs=[blk] * 4, out_specs=(blk, blk, blk),
        compiler_params=_params("parallel"),
    )(w, g, m, v)


_TILE = SUBLANES * LANES


def _pack(arrays):
    rows = []
    for a in arrays:
        flat = a.reshape(-1)
        flat = jnp.pad(flat, (0, (-flat.shape[0]) % _TILE))
        rows.append(flat.reshape(-1, LANES))
    return jnp.concatenate(rows, axis=0)


def _unpack(buf, shapes):
    out, r = [], 0
    for s in shapes:
        size = math.prod(s)
        nr = -(-size // _TILE) * SUBLANES
        out.append(buf[r:r + nr].reshape(-1)[:size].reshape(s))
        r += nr
    return out


_BIG = ("w_in_a", "w_glu", "w_kv", "w_in_b", "w_mem_kv", "w_out")
_REPLICATED = ("pre_norm_g", "post_norm_g", "lam_re", "lam_im", "log_step", "b_re", "b_im", "c_re", "c_im",
               "kv_norm_g", "b_fgate", "mem_norm_g")
_SHARDED_SMALL = ("d_skip", "b_glu", "w_fgate")
_WEIGHTS = ("pre_norm_g", "post_norm_g", "w_in_a", "lam_re", "lam_im", "log_step", "b_re", "b_im", "c_re",
            "c_im", "d_skip", "w_glu", "b_glu", "kv_norm_g", "w_kv", "w_fgate", "b_fgate", "w_in_b",
            "mem_norm_g", "w_mem_kv", "w_out")


def _halves(a):
    return a.reshape(2, a.shape[0] // 2, a.shape[1])


def _unhalve(a):
    return a.reshape(N_CHIPS, 2 * a.shape[2], a.shape[3])


def _columns(a):
    return jnp.transpose(a, (1, 0, 2)).reshape(a.shape[1], N_CHIPS * a.shape[2])


def kernel(x, mem, pre_norm_g, post_norm_g, w_in_a, lam_re, lam_im, log_step, b_re, b_im, c_re, c_im, d_skip, w_glu, b_glu, kv_norm_g, w_kv, w_fgate, b_fgate, w_in_b, mem_norm_g, w_mem_kv, w_out, loss_target, m_pre_norm_g, m_post_norm_g, m_w_in_a, m_lam_re, m_lam_im, m_log_step, m_b_re, m_b_im, m_c_re, m_c_im, m_d_skip, m_w_glu, m_b_glu, m_kv_norm_g, m_w_kv, m_w_fgate, m_b_fgate, m_w_in_b, m_mem_norm_g, m_w_mem_kv, m_w_out, v_pre_norm_g, v_post_norm_g, v_w_in_a, v_lam_re, v_lam_im, v_log_step, v_b_re, v_b_im, v_c_re, v_c_im, v_d_skip, v_w_glu, v_b_glu, v_kv_norm_g, v_w_kv, v_w_fgate, v_b_fgate, v_w_in_b, v_mem_norm_g, v_w_mem_kv, v_w_out):
    a = dict(locals())
    xi, yi, ci = lax.axis_index("x"), lax.axis_index("y"), lax.axis_index("c")
    chip = 2 * xi + yi
    c_idx = jnp.reshape(ci, (1,)).astype(jnp.int32)
    jc_idx = jnp.stack([chip, ci]).astype(jnp.int32)

    vec = jnp.zeros((2 * SUBLANES, MAIN_WIDTH // N_CHIPS), F32)
    vec = vec.at[0].set(a["d_skip"][0]).at[1].set(a["b_glu"][0])
    def own_slot(gathered, parts):
        return [lax.dynamic_update_index_in_dim(g, p, chip, 0) for g, p in zip(gathered, parts)]

    travelling, token = {}, a["pre_norm_g"]

    def start_gather(tag, parts, token):
        lands = [lax.empty((N_CHIPS,) + p.shape, p.dtype) for p in parts]
        travelling[tag], token = _ici_start(parts, lands, token, _GATHER_ROUTE, name=f"gather_{tag}_start")
        return token

    token = start_gather("a", [_halves(a["w_in_a"][0].astype(BF16)), _halves(vec)], token)
    later = ("w_glu", "w_mem_kv", "w_out", "w_kv", "w_fgate", "w_in_b")
    token, *raw = lax.optimization_barrier((token, *[a[n] for n in later]))
    raw = dict(zip(later, raw))
    token = start_gather("b", [_halves(raw["w_glu"][0].astype(BF16)),
                               *[_halves(raw["w_mem_kv"][i].astype(BF16)) for i in range(2)],
                               *[_halves(raw["w_out"][i].astype(BF16)) for i in range(2)]], token)
    token = start_gather("c", [_halves(raw["w_kv"].astype(BF16)), _halves(_pad_lanes(raw["w_fgate"]).astype(BF16)),
                               _halves(raw["w_in_b"][0].astype(BF16))], token)

    small_names = _REPLICATED + _SHARDED_SMALL
    small_state = [_pack([a[prefix + n] for n in small_names]) for prefix in ("", "m_", "v_")]

    def fetch(tag, after):
        if tag == "a":
            after = list(after) + small_state
        parts, lands = _ici_wait(travelling[tag], after, _GATHER_ROUTE, name=f"gather_{tag}_wait")
        full = own_slot(_gather_forward(lands, tag), parts)
        if tag == "a":
            w_in_a, vecs = full
            return dict(w_in_a=_columns(_unhalve(w_in_a)), d_skip=vecs[:, 0, 0, :].reshape(MAIN_WIDTH),
                        b_glu=vecs[:, 0, 1, :].reshape(MAIN_WIDTH))
        if tag == "b":
            w_glu, w_mk0, w_mk1, w_out0, w_out1 = full
            return dict(w_glu=w_glu.reshape(MAIN_WIDTH, MAIN_WIDTH),
                        w_mem_kv=[m.reshape(D_MODEL, 2 * MEM_WIDTH) for m in (w_mk0, w_mk1)],
                        w_out=[o.reshape(D_MODEL, D_MODEL) for o in (w_out0, w_out1)])
        w_kv, w_fg, w_in_b = full
        return dict(w_kv=_columns(_unhalve(w_kv)), w_fgate=w_fg.reshape(D_MODEL, LANES),
                    w_in_b=_columns(_unhalve(w_in_b)))

    early = ("mem_norm_g", "lam_re", "lam_im", "log_step", "b_re", "b_im", "c_re", "c_im")
    token, *held = lax.optimization_barrier((token, *[a[n] for n in early]))
    held = dict(zip(early, held))
    w = dict(
        pre_norm_g=token, post_norm_g=a["post_norm_g"], mem_norm_g=held["mem_norm_g"],
        kv_norm_g=a["kv_norm_g"], b_fgate=a["b_fgate"],
        **{n: held[n][0] for n in early[1:]})

    sent = {}

    swapping = {}

    def grads_ready(event, g, token):
        tag = event.split("_")[0]
        if event in ("b", "a1"):
            big = {"b": lambda: [g["w_kv"], g["w_in_b"], g["w_mem_kv_1"].reshape(N_CHIPS, -1, 2 * MEM_WIDTH),
                                 g["w_out_1"].reshape(N_CHIPS, -1, D_MODEL)],
                   "a1": lambda: [g["w_glu"].reshape(N_CHIPS, -1, MAIN_WIDTH),
                                  g["w_mem_kv_0"].reshape(N_CHIPS, -1, 2 * MEM_WIDTH),
                                  g["w_out_0"].reshape(N_CHIPS, -1, D_MODEL)]}[tag]()
            views = [b.reshape(N_CHIPS, 2, b.shape[1] // 2, b.shape[2]) for b in big]
            lands = [lax.empty((N_CHIPS,) + v.shape[2:], v.dtype) for v in views]
            swapping[tag], token = _ici_start(views, lands, token, _SWAP_ROUTE, name=f"grad_swap_{tag}_start")
            return token
        if event == "a2":
            sums = _chip_sums([g["w_in_a"]], c_idx, tag)
        else:
            views, arrived = _ici_wait(swapping[tag], token, _SWAP_ROUTE, name=f"grad_swap_{tag}_wait")
            sums = _pair_sums(views, arrived, c_idx, name=f"grad_pair_sums_{tag}")
        lands = [lax.empty((3,) + s.shape[1:], s.dtype) for s in sums]
        sent[tag], token = _ici_start(sums, lands, token, _SCATTER_ROUTE, name=f"grad_send_{tag}_start")
        return token

    loss_row, grad_x, g = _local_step(a["x"][0], a["mem"][0], a["loss_target"][0], w, fetch, grads_ready)

    pack = _pack([g[n] for n in small_names])
    blocks = lax.empty((N_CHIPS, 2) + pack.shape, F32)
    small_sent, token = _ici_start([pack], [blocks], loss_row, _BLOCK_ROUTE, name="small_sums_start")

    sharing = {}
    for tag in ("b", "a1", "a2"):
        sums, arrived = _ici_wait(sent[tag], [grad_x, token], _SCATTER_ROUTE, name=f"grad_send_{tag}_wait")
        mine, bufs = _owner_sums(sums, arrived, jc_idx, name=f"grad_owner_sums_{tag}")
        sharing[tag], token = _ici_start(mine, bufs, token, _SHARE_ROUTE, name=f"grad_share_{tag}_start")
    loss = lax.psum(jnp.sum(token), MESH_AXES)

    def shared(tag, after):
        _, bufs = _ici_wait(sharing[tag], after, _SHARE_ROUTE, name=f"grad_share_{tag}_wait")
        return [b.reshape(-1, b.shape[2]) for b in bufs]

    grads, delta, new_m, new_v = {}, {}, {}, {}

    def adam(n):
        shape = a[n].shape
        d2 = (-1, shape[-1])
        d, m, v = _adamw(a[n].reshape(d2), grads[n].reshape(d2), a["m_" + n].reshape(d2),
                         a["v_" + n].reshape(d2), name="adamw_" + n)
        delta[n], new_m[n], new_v[n] = d.reshape(shape), m.reshape(shape), v.reshape(shape)
        return d

    r_kv, r_in_b, r_mk1, r_out1 = shared("b", token)
    grads["w_kv"], grads["w_in_b"] = r_kv, r_in_b[None]
    done = [adam("w_kv"), adam("w_in_b")]
    r_glu, r_mk0, r_out0 = shared("a1", done)
    grads["w_glu"], grads["w_mem_kv"], grads["w_out"] = r_glu[None], jnp.stack([r_mk0, r_mk1]), jnp.stack([r_out0, r_out1])
    done = [adam("w_glu"), adam("w_mem_kv"), adam("w_out")]
    (r_in_a,) = shared("a2", done)
    grads["w_in_a"] = r_in_a[None]
    adam("w_in_a")

    (pack,), (blocks,) = _ici_wait(small_sent, [delta[n] for n in _BIG], _BLOCK_ROUTE, name="small_sums_wait")
    blocks = lax.dynamic_update_slice(blocks, pack[None, None], (chip, ci, 0, 0))
    (blocks,) = _gather_forward([blocks], "small", own=True)
    small = dict(zip(small_names, _unpack(_sum_devices(blocks), [g[n].shape for n in small_names])))
    for n in _REPLICATED:
        grads[n] = small[n].reshape(a[n].shape)
    nd = MAIN_WIDTH // N_CHIPS
    grads["d_skip"] = lax.dynamic_slice(small["d_skip"], (chip * nd,), (nd,))[None]
    grads["b_glu"] = lax.dynamic_slice(small["b_glu"], (chip * nd,), (nd,))[None]
    nf = D_MODEL // N_CHIPS
    grads["w_fgate"] = lax.dynamic_slice(small["w_fgate"], (chip * nf, 0), (nf, FOX_HEADS))

    shapes = [a[n].shape for n in small_names]
    d, m, v = _adamw(small_state[0], _pack([grads[n] for n in small_names]), *small_state[1:], name="adamw_small")
    for n, dd, mm, vv in zip(small_names, _unpack(d, shapes), _unpack(m, shapes), _unpack(v, shapes)):
        delta[n], new_m[n], new_v[n] = dd, mm, vv

    return (loss, grad_x[None], *[grads[n] for n in _WEIGHTS], *[delta[n] for n in _WEIGHTS],
            *[new_m[n] for n in _WEIGHTS], *[new_v[n] for n in _WEIGHTS])
```

```python
import math

import jax
import jax.numpy as jnp
from jax import lax
from jax.experimental import pallas as pl
from jax.experimental.pallas import tpu as pltpu

F32 = jnp.float32
BF16 = jnp.bfloat16

D_MODEL = 2048
N_MEM = 256
MAIN_WIDTH = 1536
MEM_WIDTH = 512
IN_WIDTH = 2 * MAIN_WIDTH + 2 * MEM_WIDTH
HEAD_DIM = 128
FOX_HEADS = MAIN_WIDTH // HEAD_DIM
MEM_HEADS = MEM_WIDTH // HEAD_DIM
SSM_GROUP = 16
SSM_GROUPS = MAIN_WIDTH // SSM_GROUP
SSM_STATE = 64
GROUPS_PER_BLOCK = 8
SSM_BLOCKS = SSM_GROUPS // GROUPS_PER_BLOCK
STATE_COLS = GROUPS_PER_BLOCK * SSM_STATE
EPS = 1e-6
ADAM_LR = 0.001
ADAM_B1 = 0.9
ADAM_B2 = 0.999
ADAM_EPS = 1e-08
ADAM_WD = 0.01
ADAM_STEP = 10
N_CHIPS = 4
LANES = 128
SUBLANES = 8
VMEM_LIMIT_BYTES = 56 * 1024 * 1024
NEG_BIG = -1e30
MESH_AXES = ("x", "y", "c")


def _params(*sem):
    return pltpu.CompilerParams(dimension_semantics=sem if sem else None,
                                vmem_limit_bytes=VMEM_LIMIT_BYTES)


def _sigmoid(x):
    return 1.0 / (1.0 + jnp.exp(-x))


def _gelu(x):
    c = math.sqrt(2.0 / math.pi)
    return 0.5 * x * (1.0 + jnp.tanh(c * (x + 0.044715 * (x * x * x))))


def _gelu_grad(x):
    c = math.sqrt(2.0 / math.pi)
    t = jnp.tanh(c * (x + 0.044715 * (x * x * x)))
    return 0.5 * (1.0 + t) + 0.5 * x * (1.0 - t * t) * (c * (1.0 + 3.0 * 0.044715 * (x * x)))


def _silu_and_grad(z):
    s = _sigmoid(z)
    return z * s, s * (1.0 + z * (1.0 - s))


_TILE_CHOICES = (4096, 3072, 2048, 1536, 1024, 768, 512, 384, 256, LANES)


def _tile(n, cap):
    return next(c for c in _TILE_CHOICES if c <= cap and n % c == 0)


def _mm(a, b, *, name, ta=False, tb=False, out_dtype=F32, shards=1, tm=1024, tn=1024, tk=4096):
    if ta:
        K, M = a.shape
    else:
        M, K = a.shape
    if tb:
        N, kb = b.shape
    else:
        kb, N = b.shape
    assert K == kb, (a.shape, b.shape)
    ns = N // shards
    tm, tn, tk = _tile(M, tm), _tile(ns, tn), _tile(K, tk)
    assert M % tm == 0 and ns % tn == 0 and K % tk == 0 and N % shards == 0
    nk = K // tk
    dn = (((0 if ta else 1,), (1 if tb else 0,)), ((), ()))

    def body(a_ref, b_ref, o_ref, *acc):
        prod = lax.dot_general(a_ref[...].astype(BF16), b_ref[...].astype(BF16), dn, preferred_element_type=F32)
        if nk == 1:
            o_ref[...] = prod.astype(o_ref.dtype)
            return
        acc_ref, = acc
        k = pl.program_id(2)

        @pl.when(k == 0)
        def _():
            acc_ref[...] = jnp.zeros_like(acc_ref)

        acc_ref[...] += prod

        @pl.when(k == nk - 1)
        def _():
            o_ref[...] = acc_ref[...].astype(o_ref.dtype)

    a_spec = (pl.BlockSpec((tk, tm), lambda i, j, k: (k, i)) if ta
              else pl.BlockSpec((tm, tk), lambda i, j, k: (i, k)))
    b_spec = (pl.BlockSpec((tn, tk), lambda i, j, k: (j, k)) if tb
              else pl.BlockSpec((tk, tn), lambda i, j, k: (k, j)))
    if shards == 1:
        out_shape = jax.ShapeDtypeStruct((M, N), out_dtype)
        o_spec = pl.BlockSpec((tm, tn), lambda i, j, k: (i, j))
    else:
        nb = ns // tn
        out_shape = jax.ShapeDtypeStruct((shards, M, ns), out_dtype)
        o_spec = pl.BlockSpec((None, tm, tn), lambda i, j, k: (j // nb, i, j % nb))
    return pl.pallas_call(
        body, name=name, out_shape=out_shape,
        grid=(M // tm, N // tn, nk),
        in_specs=[a_spec, b_spec], out_specs=o_spec,
        scratch_shapes=[] if nk == 1 else [pltpu.VMEM((tm, tn), F32)],
        compiler_params=_params("parallel", "parallel", "arbitrary"),
    )(a, b)


def _rmsnorm_fwd(x, g, *, name, out_dtype=F32, tr=256):
    L, D = x.shape
    tr = min(tr, L)

    def body(x_ref, g_ref, o_ref):
        xf = x_ref[...]
        r = lax.rsqrt(jnp.mean(xf * xf, axis=-1, keepdims=True) + EPS)
        o_ref[...] = (xf * r * g_ref[...]).astype(o_ref.dtype)

    row = pl.BlockSpec((tr, D), lambda i: (i, 0))
    vec = pl.BlockSpec((1, D), lambda i: (0, 0))
    return pl.pallas_call(
        body, name=name, out_shape=jax.ShapeDtypeStruct((L, D), out_dtype),
        grid=(L // tr,), in_specs=[row, vec], out_specs=row,
        compiler_params=_params("parallel"),
    )(x, g.reshape(1, D))


def _post_norm_and_next_norms(o, g_post, res, g_kv, g_pre, *, name, tr=256):
    L, D = o.shape
    tr = min(tr, L)

    def body(o_ref, gp_ref, r_ref, gk_ref, gn_ref, h_ref, kv_ref, hn_ref):
        of = o_ref[...]
        r = lax.rsqrt(jnp.mean(of * of, axis=-1, keepdims=True) + EPS)
        h = r_ref[...] + of * r * gp_ref[...]
        h_ref[...] = h
        hr = h * lax.rsqrt(jnp.mean(h * h, axis=-1, keepdims=True) + EPS)
        kv_ref[...] = (hr * gk_ref[...]).astype(kv_ref.dtype)
        hn_ref[...] = (hr * gn_ref[...]).astype(hn_ref.dtype)

    row = pl.BlockSpec((tr, D), lambda i: (i, 0))
    vec = pl.BlockSpec((1, D), lambda i: (0, 0))
    return pl.pallas_call(
        body, name=name,
        out_shape=(jax.ShapeDtypeStruct((L, D), F32), jax.ShapeDtypeStruct((L, D), BF16),
                   jax.ShapeDtypeStruct((L, D), BF16)),
        grid=(L // tr,), in_specs=[row, vec, row, vec, vec], out_specs=(row, row, row),
        compiler_params=_params("parallel"),
    )(o, g_post.reshape(1, D), res, g_kv.reshape(1, D), g_pre.reshape(1, D))


def _rmsnorm_bwd(x, g, dy, *, name, adds=(), dx_dtype=F32, tr=256):
    L, D = x.shape
    tr = min(tr, L)
    dys = dy if isinstance(dy, tuple) else (dy,)
    n_dy, n_add = len(dys), len(adds)

    def body(*refs):
        x_ref, g_ref = refs[:2]
        dy_refs = refs[2:2 + n_dy]
        add_refs = refs[2 + n_dy:2 + n_dy + n_add]
        dx_ref, dg_ref = refs[2 + n_dy + n_add:]
        xf = x_ref[...]
        dyf = dy_refs[0][...].astype(F32)
        for d_ref in dy_refs[1:]:
            dyf = dyf + d_ref[...].astype(F32)
        r = lax.rsqrt(jnp.mean(xf * xf, axis=-1, keepdims=True) + EPS)
        gy = dyf * g_ref[...]
        c = jnp.mean(xf * gy, axis=-1, keepdims=True) * (r * r * r)
        dx = gy * r - xf * c
        for a_ref in add_refs:
            dx = dx + a_ref[...].astype(F32)
        dx_ref[...] = dx.astype(dx_ref.dtype)

        @pl.when(pl.program_id(0) == 0)
        def _():
            dg_ref[...] = jnp.zeros_like(dg_ref)

        dg_ref[...] += jnp.sum(dyf * xf * r, axis=0, keepdims=True)

    row = pl.BlockSpec((tr, D), lambda i: (i, 0))
    vec = pl.BlockSpec((1, D), lambda i: (0, 0))
    dx, dg = pl.pallas_call(
        body, name=name,
        out_shape=(jax.ShapeDtypeStruct((L, D), dx_dtype), jax.ShapeDtypeStruct((1, D), F32)),
        grid=(L // tr,), in_specs=[row, vec] + [row] * (n_dy + n_add), out_specs=(row, vec),
        compiler_params=_params("arbitrary"),
    )(x, g.reshape(1, D), *dys, *adds)
    return dx, dg.reshape(D)


def _rmsnorm_bwd_pair(x, g1, dy1, g2, dy2, *, name, adds=(), tr=256):
    L, D = x.shape
    tr = min(tr, L)
    dy1s = dy1 if isinstance(dy1, tuple) else (dy1,)
    n1, n_add = len(dy1s), len(adds)

    def body(*refs):
        x_ref, g1_ref, g2_ref = refs[:3]
        dy1_refs = refs[3:3 + n1]
        dy2_ref = refs[3 + n1]
        add_refs = refs[4 + n1:4 + n1 + n_add]
        dx_ref, dg1_ref, dg2_ref = refs[4 + n1 + n_add:]
        xf = x_ref[...]
        d1 = dy1_refs[0][...].astype(F32)
        for d_ref in dy1_refs[1:]:
            d1 = d1 + d_ref[...].astype(F32)
        d2 = dy2_ref[...].astype(F32)
        r = lax.rsqrt(jnp.mean(xf * xf, axis=-1, keepdims=True) + EPS)
        gy = d1 * g1_ref[...] + d2 * g2_ref[...]
        c = jnp.mean(xf * gy, axis=-1, keepdims=True) * (r * r * r)
        dx = gy * r - xf * c
        for a_ref in add_refs:
            dx = dx + a_ref[...].astype(F32)
        dx_ref[...] = dx

        @pl.when(pl.program_id(0) == 0)
        def _():
            dg1_ref[...] = jnp.zeros_like(dg1_ref)
            dg2_ref[...] = jnp.zeros_like(dg2_ref)

        xr = xf * r
        dg1_ref[...] += jnp.sum(d1 * xr, axis=0, keepdims=True)
        dg2_ref[...] += jnp.sum(d2 * xr, axis=0, keepdims=True)

    row = pl.BlockSpec((tr, D), lambda i: (i, 0))
    vec = pl.BlockSpec((1, D), lambda i: (0, 0))
    dx, dg1, dg2 = pl.pallas_call(
        body, name=name,
        out_shape=(jax.ShapeDtypeStruct((L, D), F32), jax.ShapeDtypeStruct((1, D), F32),
                   jax.ShapeDtypeStruct((1, D), F32)),
        grid=(L // tr,), in_specs=[row, vec, vec] + [row] * (n1 + 1 + n_add), out_specs=(row, vec, vec),
        compiler_params=_params("arbitrary"),
    )(x, g1.reshape(1, D), g2.reshape(1, D), *dy1s, dy2, *adds)
    return dx, dg1.reshape(D), dg2.reshape(D)


def _final_norm_loss(o, g, res, target, *, tr=256):
    L, D = o.shape
    tr = min(tr, L)

    def body(o_ref, g_ref, r_ref, t_ref, dh_ref, loss_ref):
        xf = o_ref[...]
        r = lax.rsqrt(jnp.mean(xf * xf, axis=-1, keepdims=True) + EPS)
        e = (r_ref[...] + xf * r * g_ref[...]) - t_ref[...]
        dh_ref[...] = e * (1.0 / D)

        @pl.when(pl.program_id(0) == 0)
        def _():
            loss_ref[...] = jnp.zeros_like(loss_ref)

        loss_ref[...] += jnp.sum(e * e, axis=0, keepdims=True) * (0.5 / D)

    row = pl.BlockSpec((tr, D), lambda i: (i, 0))
    vec = pl.BlockSpec((1, D), lambda i: (0, 0))
    dh, lp = pl.pallas_call(
        body, name="post_norm_1_loss",
        out_shape=(jax.ShapeDtypeStruct((L, D), F32), jax.ShapeDtypeStruct((1, D), F32)),
        grid=(L // tr,), in_specs=[row, vec, row, row], out_specs=(row, vec),
        compiler_params=_params("arbitrary"),
    )(o, g.reshape(1, D), res, target)
    return dh, lp


def _s5_coeffs(lr, li, ls):
    dt = jnp.exp(ls)
    mag = jnp.exp(lr * dt)
    ar = mag * jnp.cos(li * dt)
    ai = mag * jnp.sin(li * dt)
    den = lr * lr + li * li
    cr = ((ar - 1.0) * lr + ai * li) / den
    ci = (ai * lr - (ar - 1.0) * li) / den
    return dt, ar, ai, den, cr, ci


def _s5_prep(lam_re, lam_im, log_step, b_re_t, b_im_t):
    G, P = lam_re.shape
    H = b_re_t.shape[1]

    def body(lr_ref, li_ref, ls_ref, br_ref, bi_ref, ar_ref, ai_ref, bbr_ref, bbi_ref):
        _, ar, ai, _, cr, ci = _s5_coeffs(lr_ref[...], li_ref[...], ls_ref[...])
        ar_ref[...] = ar
        ai_ref[...] = ai
        br, bi = br_ref[...], bi_ref[...]
        crb, cib = cr[:, None, :], ci[:, None, :]
        bbr_ref[...] = crb * br - cib * bi
        bbi_ref[...] = crb * bi + cib * br

    return pl.pallas_call(
        body, name="s5_prep",
        out_shape=(jax.ShapeDtypeStruct((G, P), F32), jax.ShapeDtypeStruct((G, P), F32),
                   jax.ShapeDtypeStruct((G, H, P), F32), jax.ShapeDtypeStruct((G, H, P), F32)),
        compiler_params=_params(),
    )(lam_re, lam_im, log_step.reshape(G, 1), b_re_t, b_im_t)


def _s5_prep_bwd(lam_re, lam_im, log_step, b_re_t, b_im_t, d_ar, d_ai, d_bbr, d_bbi):
    G, P = lam_re.shape
    H = b_re_t.shape[1]

    def body(lr_ref, li_ref, ls_ref, br_ref, bi_ref, dar_ref, dai_ref, dbbr_ref, dbbi_ref,
             dlr_ref, dli_ref, dls_ref, dbr_ref, dbi_ref):
        lr, li = lr_ref[...], li_ref[...]
        dt, ar, ai, den, cr, ci = _s5_coeffs(lr, li, ls_ref[...])
        br, bi = br_ref[...], bi_ref[...]
        gbr, gbi = dbbr_ref[...], dbbi_ref[...]
        crb, cib = cr[:, None, :], ci[:, None, :]
        dbr_ref[...] = crb * gbr + cib * gbi
        dbi_ref[...] = crb * gbi - cib * gbr
        gcr = jnp.sum(br * gbr + bi * gbi, axis=1)
        gci = jnp.sum(br * gbi - bi * gbr, axis=1)
        ilr, ili = lr / den, -li / den
        gar = dar_ref[...] + (ilr * gcr + ili * gci)
        gai = dai_ref[...] + (ilr * gci - ili * gcr)
        qr, qi = cr * ilr - ci * ili, cr * ili + ci * ilr
        glr = -(qr * gcr + qi * gci)
        gli = -(qr * gci - qi * gcr)
        glr = glr + dt * (ar * gar + ai * gai)
        gli = gli + dt * (ar * gai - ai * gar)
        wr, wi = lr * ar - li * ai, lr * ai + li * ar
        gdt = jnp.sum(wr * gar + wi * gai, axis=1, keepdims=True)
        dlr_ref[...] = glr
        dli_ref[...] = gli
        dls_ref[...] = gdt * dt

    return pl.pallas_call(
        body, name="s5_prep_bwd",
        out_shape=(jax.ShapeDtypeStruct((G, P), F32), jax.ShapeDtypeStruct((G, P), F32),
                   jax.ShapeDtypeStruct((G, 1), F32),
                   jax.ShapeDtypeStruct((G, H, P), F32), jax.ShapeDtypeStruct((G, H, P), F32)),
        compiler_params=_params(),
    )(lam_re, lam_im, log_step.reshape(G, 1), b_re_t, b_im_t, d_ar, d_ai, d_bbr, d_bbi)


def _s5_block_mats(bbr_t, bbi_t, c_re, c_im):
    bmat = _s5_expand(bbr_t, bbi_t)
    cmat = jnp.transpose(_s5_expand(c_re, -c_im), (0, 2, 1))
    return bmat.astype(BF16), cmat.astype(BF16)


def _s5_diag_mask():
    r = lax.broadcasted_iota(jnp.int32, (LANES, 2 * STATE_COLS), 0) // SSM_GROUP
    c = (lax.broadcasted_iota(jnp.int32, (LANES, 2 * STATE_COLS), 1) % STATE_COLS) // SSM_STATE
    return (r == c).astype(F32)


def _s5_expand(re, im):
    re = jnp.tile(re.reshape(SSM_BLOCKS, LANES, SSM_STATE), (1, 1, GROUPS_PER_BLOCK))
    im = jnp.tile(im.reshape(SSM_BLOCKS, LANES, SSM_STATE), (1, 1, GROUPS_PER_BLOCK))
    return jnp.concatenate([re, im], axis=-1) * _s5_diag_mask()[None]


def _s5_unfold(dmat):
    d = dmat.reshape(SSM_GROUPS, SSM_GROUP, 2, SSM_STATE)
    return jnp.transpose(d, (2, 0, 1, 3))


def _s5_a_rows(ar, ai):
    a = jnp.concatenate([ar.reshape(SSM_BLOCKS, STATE_COLS), ai.reshape(SSM_BLOCKS, STATE_COLS)], axis=1)
    return jnp.broadcast_to(a[:, None, :], (SSM_BLOCKS, SUBLANES, 2 * STATE_COLS))


def _to_step_major(src_ref, dst_ref, seg):
    for s in range(SUBLANES):
        dst_ref[pl.ds(s, seg, stride=SUBLANES), :] = src_ref[pl.ds(seg * s, seg), :]


def _segment_rows(ref, s, seg):
    return ref[pl.ds(s, seg, stride=SUBLANES), :]


def _cmul(ar, ai, xr, xi):
    return ar * xr - ai * xi, ar * xi + ai * xr


def _s5_tables(a_ref, pw_s, pwr_s, S, seg):
    ar, ai = a_ref[:, :S], a_ref[:, S:]

    def step(i, c):
        pr, pi = c
        pw_s[i, :, :S] = pr
        pw_s[i, :, S:] = pi
        nr, ni = _cmul(ar, ai, pr, pi)
        pwr_s[seg - 1 - i, :, :S] = nr
        pwr_s[seg - 1 - i, :, S:] = ni
        return nr, ni

    pr, pi = lax.fori_loop(0, seg, step, (jnp.ones_like(ar), jnp.zeros_like(ai)))
    pw_s[seg, :, :S] = pr
    pw_s[seg, :, S:] = pi


def _s5_fwd(proj, bmat, cmat, a_rows, d_skip, *, tc=512):
    L = proj.shape[0]
    tc = min(tc, L)
    nt = L // tc
    seg = tc // SUBLANES
    S = STATE_COLS

    def body(u_ref, b_ref, c_ref, a_ref, d_ref, y_ref, yg_ref, xp_ref,
             bu_s, xp_s, pw_s, pwr_s, carry_s, e_s, up_s, yc_s):
        @pl.when(pl.program_id(1) == 0)
        def _():
            carry_s[...] = jnp.zeros_like(carry_s)
            _s5_tables(a_ref, pw_s, pwr_s, S, seg)

        ar, ai = a_ref[:, :S], a_ref[:, S:]
        _to_step_major(u_ref, up_s, seg)
        bu = jnp.dot(up_s[...].astype(BF16), b_ref[...], preferred_element_type=F32)
        bu_s[...] = bu.reshape(seg, SUBLANES, 2 * S)

        def step(i, carry):
            cr, ci = carry
            xp_s[i, :, :S] = cr
            xp_s[i, :, S:] = ci
            return ar * cr - ai * ci + bu_s[i, :, :S], ar * ci + ai * cr + bu_s[i, :, S:]

        zero = jnp.zeros((SUBLANES, S), F32)
        fr, fi = lax.fori_loop(0, seg, step, (zero, zero))
        pr, pi = pw_s[seg, 0:1, :S], pw_s[seg, 0:1, S:]
        er, ei = carry_s[0:1, :S], carry_s[0:1, S:]
        for s in range(SUBLANES):
            e_s[s:s + 1, :S] = er
            e_s[s:s + 1, S:] = ei
            tr, ti = _cmul(pr, pi, er, ei)
            er, ei = fr[s:s + 1] + tr, fi[s:s + 1] + ti
        carry_s[0:1, :S] = er
        carry_s[0:1, S:] = ei
        pw = pw_s[0:seg]
        tr, ti = _cmul(pw[:, :, :S], pw[:, :, S:], e_s[:, :S][None], e_s[:, S:][None])
        xl = xp_s[...]
        xp = jnp.concatenate([xl[:, :, :S] + tr, xl[:, :, S:] + ti], axis=-1).reshape(tc, 2 * S)
        xp_ref[...] = xp
        a1r, a1i = ar[0:1], ai[0:1]
        x_re = a1r * xp[:, :S] - a1i * xp[:, S:] + bu[:, :S]
        x_im = a1r * xp[:, S:] + a1i * xp[:, :S] + bu[:, S:]
        xs = jnp.concatenate([x_re, x_im], axis=1).astype(BF16)
        yc_s[...] = jnp.dot(xs, c_ref[...], preferred_element_type=F32)
        for s in range(SUBLANES):
            rows = pl.ds(seg * s, seg)
            y = _segment_rows(yc_s, s, seg) + d_ref[...] * u_ref[rows, :]
            y_ref[rows, :] = y
            yg_ref[rows, :] = _gelu(y).astype(BF16)

    return pl.pallas_call(
        body, name="s5_fwd",
        out_shape=(jax.ShapeDtypeStruct((L, MAIN_WIDTH), F32),
                   jax.ShapeDtypeStruct((L, MAIN_WIDTH), BF16),
                   jax.ShapeDtypeStruct((L, SSM_BLOCKS * 2 * S), F32)),
        grid=(SSM_BLOCKS, nt),
        in_specs=[pl.BlockSpec((tc, LANES), lambda b, t: (t, b)),
                  pl.BlockSpec((None, LANES, 2 * S), lambda b, t: (b, 0, 0)),
                  pl.BlockSpec((None, 2 * S, LANES), lambda b, t: (b, 0, 0)),
                  pl.BlockSpec((None, SUBLANES, 2 * S), lambda b, t: (b, 0, 0)),
                  pl.BlockSpec((1, LANES), lambda b, t: (0, b))],
        out_specs=(pl.BlockSpec((tc, LANES), lambda b, t: (t, b)),
                   pl.BlockSpec((tc, LANES), lambda b, t: (t, b)),
                   pl.BlockSpec((tc, 2 * S), lambda b, t: (t, b))),
        scratch_shapes=[pltpu.VMEM((seg, SUBLANES, 2 * S), F32),
                        pltpu.VMEM((seg, SUBLANES, 2 * S), F32),
                        pltpu.VMEM((seg + 1, SUBLANES, 2 * S), F32),
                        pltpu.VMEM((seg, SUBLANES, 2 * S), F32),
                        pltpu.VMEM((SUBLANES, 2 * S), F32),
                        pltpu.VMEM((SUBLANES, 2 * S), F32),
                        pltpu.VMEM((tc, LANES), F32),
                        pltpu.VMEM((tc, LANES), F32)],
        compiler_params=_params("parallel", "arbitrary"),
    )(proj, bmat, cmat, a_rows, d_skip.reshape(1, MAIN_WIDTH))


def _s5_bwd(proj, dyg_a, dyg_b, y, xp, bmat, cmat, a_rows, d_skip, dproj, *, tc=512):
    L = proj.shape[0]
    tc = min(tc, L)
    nt = L // tc
    seg = tc // SUBLANES
    S = STATE_COLS
    nn = (((1,), (1,)), ((), ()))
    tn = (((0,), (0,)), ((), ()))

    def fold_diagonal(acc_ref, mask_ref, fold_ref):
        x = acc_ref[...] * mask_ref[...]
        hi = x.astype(BF16)
        rest = x - hi.astype(F32)
        mid = rest.astype(BF16)
        low = (rest - mid.astype(F32)).astype(BF16)
        return sum(jnp.dot(piece, fold_ref[...], preferred_element_type=F32) for piece in (hi, mid, low))

    def body(u_ref, dyga_ref, dygb_ref, y_ref, xp_ref, b_ref, c_ref, a_ref, d_ref, mask_ref, fold_ref, dp_hbm,
             du_ref, dbd_ref, dcd_ref, da_ref, dd_ref,
             dl_s, pw_s, pwr_s, carry_s, e_s, up_s, dy_s, dyp_s, dup_s, db_ref, dc_ref):
        @pl.when(pl.program_id(1) == 0)
        def _():
            carry_s[...] = jnp.zeros_like(carry_s)
            db_ref[...] = jnp.zeros_like(db_ref)
            dc_ref[...] = jnp.zeros_like(dc_ref)
            da_ref[...] = jnp.zeros_like(da_ref)
            dd_ref[...] = jnp.zeros_like(dd_ref)
            _s5_tables(a_ref, pw_s, pwr_s, S, seg)

        ar, ai = a_ref[:, :S], a_ref[:, S:]
        a1r, a1i = ar[0:1], ai[0:1]
        u = u_ref[...]
        dy = (dyga_ref[...] + dygb_ref[...]) * _gelu_grad(y_ref[...])
        dy_s[...] = dy
        xp = xp_ref[...]
        _to_step_major(u_ref, up_s, seg)
        _to_step_major(dy_s, dyp_s, seg)
        ubp = up_s[...].astype(BF16)
        dyp = dyp_s[...].astype(BF16)
        bu = jnp.dot(ubp, b_ref[...], preferred_element_type=F32)
        x_re = a1r * xp[:, :S] - a1i * xp[:, S:] + bu[:, :S]
        x_im = a1r * xp[:, S:] + a1i * xp[:, :S] + bu[:, S:]
        xs = jnp.concatenate([x_re, x_im], axis=1).astype(BF16)
        dc_ref[...] += lax.dot_general(dyp, xs, tn, preferred_element_type=F32)
        dx = lax.dot_general(dyp, c_ref[...], nn, preferred_element_type=F32)
        dl_s[...] = dx.reshape(seg, SUBLANES, 2 * S)

        def step(k, carry):
            cr, ci = carry
            i = seg - 1 - k
            lr = dl_s[i, :, :S] + (ar * cr + ai * ci)
            li = dl_s[i, :, S:] + (ar * ci - ai * cr)
            dl_s[i, :, :S] = lr
            dl_s[i, :, S:] = li
            return lr, li

        zero = jnp.zeros((SUBLANES, S), F32)
        fr, fi = lax.fori_loop(0, seg, step, (zero, zero))
        pr, pi = pw_s[seg, 0:1, :S], pw_s[seg, 0:1, S:]
        er, ei = carry_s[0:1, :S], carry_s[0:1, S:]
        for s in range(SUBLANES - 1, -1, -1):
            e_s[s:s + 1, :S] = er
            e_s[s:s + 1, S:] = ei
            er, ei = fr[s:s + 1] + (pr * er + pi * ei), fi[s:s + 1] + (pr * ei - pi * er)
        carry_s[0:1, :S] = er
        carry_s[0:1, S:] = ei
        er, ei = e_s[:, :S][None], e_s[:, S:][None]
        pw = pwr_s[...]
        pwr, pwi = pw[:, :, :S], pw[:, :, S:]
        ll = dl_s[...]
        lam = jnp.concatenate([ll[:, :, :S] + (pwr * er + pwi * ei), ll[:, :, S:] + (pwr * ei - pwi * er)],
                              axis=-1).reshape(tc, 2 * S)
        l_re, l_im = lam[:, :S], lam[:, S:]
        da_ref[0:1, :S] += jnp.sum(l_re * xp[:, :S] + l_im * xp[:, S:], axis=0, keepdims=True)
        da_ref[0:1, S:] += jnp.sum(l_im * xp[:, :S] - l_re * xp[:, S:], axis=0, keepdims=True)
        lamb = lam.astype(BF16)
        dup_s[...] = lax.dot_general(lamb, b_ref[...], nn, preferred_element_type=F32)
        for s in range(SUBLANES):
            rows = pl.ds(seg * s, seg)
            du = _segment_rows(dup_s, s, seg) + d_ref[...] * dy_s[rows, :]
            du_ref[rows, :] = du.astype(du_ref.dtype)
        db_ref[...] += lax.dot_general(ubp, lamb, tn, preferred_element_type=F32)
        dd_ref[0:1, :] += jnp.sum(dy * u, axis=0, keepdims=True)

        @pl.when(pl.program_id(1) == nt - 1)
        def _():
            dbd_ref[...] = fold_diagonal(db_ref, mask_ref, fold_ref)
            dcd_ref[...] = fold_diagonal(dc_ref, mask_ref, fold_ref)

    rev = lambda b, t: (nt - 1 - t, b)
    col = jnp.arange(2 * S)
    fold = ((col // S * SSM_STATE + col % SSM_STATE)[:, None] == jnp.arange(LANES)[None, :]).astype(BF16)
    return pl.pallas_call(
        body, name="s5_bwd",
        out_shape=(jax.ShapeDtypeStruct(dproj.shape, dproj.dtype),
                   jax.ShapeDtypeStruct((SSM_BLOCKS, LANES, LANES), F32),
                   jax.ShapeDtypeStruct((SSM_BLOCKS, LANES, LANES), F32),
                   jax.ShapeDtypeStruct((SSM_BLOCKS, SUBLANES, 2 * S), F32),
                   jax.ShapeDtypeStruct((SUBLANES, MAIN_WIDTH), F32)),
        input_output_aliases={11: 0},
        grid=(SSM_BLOCKS, nt),
        in_specs=[pl.BlockSpec((tc, LANES), rev),
                  pl.BlockSpec((tc, LANES), rev),
                  pl.BlockSpec((tc, LANES), rev),
                  pl.BlockSpec((tc, LANES), rev),
                  pl.BlockSpec((tc, 2 * S), rev),
                  pl.BlockSpec((None, LANES, 2 * S), lambda b, t: (b, 0, 0)),
                  pl.BlockSpec((None, 2 * S, LANES), lambda b, t: (b, 0, 0)),
                  pl.BlockSpec((None, SUBLANES, 2 * S), lambda b, t: (b, 0, 0)),
                  pl.BlockSpec((1, LANES), lambda b, t: (0, b)),
                  pl.BlockSpec((LANES, 2 * S), lambda b, t: (0, 0)),
                  pl.BlockSpec((2 * S, LANES), lambda b, t: (0, 0)),
                  _ANY],
        out_specs=(pl.BlockSpec((tc, LANES), rev),
                   pl.BlockSpec((None, LANES, LANES), lambda b, t: (b, 0, 0)),
                   pl.BlockSpec((None, LANES, LANES), lambda b, t: (b, 0, 0)),
                   pl.BlockSpec((None, SUBLANES, 2 * S), lambda b, t: (b, 0, 0)),
                   pl.BlockSpec((SUBLANES, LANES), lambda b, t: (0, b))),
        scratch_shapes=[pltpu.VMEM((seg, SUBLANES, 2 * S), F32),
                        pltpu.VMEM((seg + 1, SUBLANES, 2 * S), F32),
                        pltpu.VMEM((seg, SUBLANES, 2 * S), F32),
                        pltpu.VMEM((SUBLANES, 2 * S), F32),
                        pltpu.VMEM((SUBLANES, 2 * S), F32),
                        pltpu.VMEM((tc, LANES), F32),
                        pltpu.VMEM((tc, LANES), F32),
                        pltpu.VMEM((tc, LANES), F32),
                        pltpu.VMEM((tc, LANES), F32),
                        pltpu.VMEM((LANES, 2 * S), F32),
                        pltpu.VMEM((LANES, 2 * S), F32)],
        compiler_params=_params("parallel", "arbitrary"),
    )(proj, dyg_a, dyg_b, y, xp, bmat, cmat, a_rows, d_skip.reshape(1, MAIN_WIDTH), _s5_diag_mask(), fold, dproj)


_Z_COLS = slice(MAIN_WIDTH, 2 * MAIN_WIDTH)
_ZM_COLS = slice(2 * MAIN_WIDTH + MEM_WIDTH, IN_WIDTH)


def _proj_rows(tr):
    return pl.BlockSpec((tr, IN_WIDTH), lambda i: (i, 0))


def _row_specs(tr):
    main = pl.BlockSpec((tr, MAIN_WIDTH), lambda i: (i, 0))
    z = pl.BlockSpec((tr, MAIN_WIDTH), lambda i: (i, 1))
    zm = pl.BlockSpec((tr, MEM_WIDTH), lambda i: (i, IN_WIDTH // MEM_WIDTH - 1))
    mem = pl.BlockSpec((tr, MEM_WIDTH), lambda i: (i, 0))
    cat = pl.BlockSpec((tr, D_MODEL), lambda i: (i, 0))
    vec = pl.BlockSpec((1, MAIN_WIDTH), lambda i: (0, 0))
    return main, z, zm, mem, cat, vec


def _gate_a_fwd(y, t, b_glu, proj, o_mem, *, tr=256):
    L = y.shape[0]
    tr = min(tr, L)

    def body(y_ref, t_ref, b_ref, z_ref, zm_ref, om_ref, o_ref):
        yg = _gelu(y_ref[...])
        sz, _ = _silu_and_grad(z_ref[...])
        o_ref[:, :MAIN_WIDTH] = (yg * _sigmoid(t_ref[...] + b_ref[...]) * sz).astype(BF16)
        szm, _ = _silu_and_grad(zm_ref[...])
        o_ref[:, MAIN_WIDTH:] = (om_ref[...] * szm).astype(BF16)

    main, z, zm, mem, cat, vec = _row_specs(tr)
    return pl.pallas_call(
        body, name="gate_a_fwd", out_shape=jax.ShapeDtypeStruct((L, D_MODEL), BF16),
        grid=(L // tr,), in_specs=[main, main, vec, z, zm, mem], out_specs=cat,
        compiler_params=_params("parallel"),
    )(y, t, b_glu.reshape(1, MAIN_WIDTH), proj, proj, o_mem)


def _gate_a_bwd(dcat, y, t, b_glu, proj, o_mem, *, tr=256):
    L = y.shape[0]
    tr = min(tr, L)

    def body(dc_ref, y_ref, t_ref, b_ref, z_ref, zm_ref, om_ref,
             dp_ref, dt_ref, dyg_ref, dom_ref, db_ref):
        dmain = dc_ref[:, :MAIN_WIDTH]
        dmemo = dc_ref[:, MAIN_WIDTH:]
        yg = _gelu(y_ref[...])
        sg = _sigmoid(t_ref[...] + b_ref[...])
        sz, gz = _silu_and_grad(z_ref[...])
        dp_ref[:, _Z_COLS] = (dmain * (yg * sg) * gz).astype(BF16)
        dy2 = dmain * sz
        dyg_ref[...] = dy2 * sg
        dt = dy2 * yg * (sg * (1.0 - sg))
        dt_ref[...] = dt.astype(BF16)

        @pl.when(pl.program_id(0) == 0)
        def _():
            db_ref[...] = jnp.zeros_like(db_ref)

        db_ref[...] += jnp.sum(dt, axis=0, keepdims=True)
        szm, gzm = _silu_and_grad(zm_ref[...])
        dom_ref[...] = dmemo * szm
        dp_ref[:, _ZM_COLS] = (dmemo * om_ref[...] * gzm).astype(BF16)

    main, z, zm, mem, cat, vec = _row_specs(tr)
    outs = pl.pallas_call(
        body, name="gate_a_bwd",
        out_shape=(jax.ShapeDtypeStruct((L, IN_WIDTH), BF16),
                   jax.ShapeDtypeStruct((L, MAIN_WIDTH), BF16), jax.ShapeDtypeStruct((L, MAIN_WIDTH), F32),
                   jax.ShapeDtypeStruct((L, MEM_WIDTH), F32), jax.ShapeDtypeStruct((1, MAIN_WIDTH), F32)),
        grid=(L // tr,), in_specs=[cat, main, main, vec, z, zm, mem],
        out_specs=(_proj_rows(tr), main, main, mem, vec),
        compiler_params=_params("arbitrary"),
    )(dcat, y, t, b_glu.reshape(1, MAIN_WIDTH), proj, proj, o_mem)
    return outs


def _gate_b_fwd(att, proj, o_mem, *, tr=256):
    L = att.shape[0]
    tr = min(tr, L)

    def body(a_ref, z_ref, zm_ref, om_ref, o_ref):
        sz, _ = _silu_and_grad(z_ref[...])
        o_ref[:, :MAIN_WIDTH] = (a_ref[...] * sz).astype(BF16)
        szm, _ = _silu_and_grad(zm_ref[...])
        o_ref[:, MAIN_WIDTH:] = (om_ref[...] * szm).astype(BF16)

    main, z, zm, mem, cat, _ = _row_specs(tr)
    return pl.pallas_call(
        body, name="gate_b_fwd", out_shape=jax.ShapeDtypeStruct((L, D_MODEL), BF16),
        grid=(L // tr,), in_specs=[main, z, zm, mem], out_specs=cat,
        compiler_params=_params("parallel"),
    )(att, proj, proj, o_mem)


def _gate_b_bwd(dcat, att, proj, o_mem, *, tr=256):
    L = att.shape[0]
    tr = min(tr, L)

    def body(dc_ref, a_ref, z_ref, zm_ref, om_ref, da_ref, dp_ref, dom_ref, dl_ref):
        dmain = dc_ref[:, :MAIN_WIDTH]
        dmemo = dc_ref[:, MAIN_WIDTH:]
        att = a_ref[...]
        sz, gz = _silu_and_grad(z_ref[...])
        datt = dmain * sz
        da_ref[...] = datt
        dp_ref[:, _Z_COLS] = (dmain * att * gz).astype(BF16)
        szm, gzm = _silu_and_grad(zm_ref[...])
        dom_ref[...] = dmemo * szm
        dp_ref[:, _ZM_COLS] = (dmemo * om_ref[...] * gzm).astype(BF16)
        prod = datt * att
        for h in range(FOX_HEADS):
            dl_ref[h] = jnp.sum(prod[:, h * HEAD_DIM:(h + 1) * HEAD_DIM], axis=1, keepdims=True)

    main, z, zm, mem, cat, _ = _row_specs(tr)
    delta = pl.BlockSpec((FOX_HEADS, tr, 1), lambda i: (0, i, 0))
    return pl.pallas_call(
        body, name="gate_b_bwd",
        out_shape=(jax.ShapeDtypeStruct((L, MAIN_WIDTH), F32), jax.ShapeDtypeStruct((L, IN_WIDTH), BF16),
                   jax.ShapeDtypeStruct((L, MEM_WIDTH), F32), jax.ShapeDtypeStruct((FOX_HEADS, L, 1), F32)),
        grid=(L // tr,), in_specs=[cat, main, z, zm, mem], out_specs=(main, _proj_rows(tr), mem, delta),
        compiler_params=_params("parallel"),
    )(dcat, att, proj, proj, o_mem)


_MEM_Q_COL = (2 * MAIN_WIDTH) // HEAD_DIM
_NT = (((1,), (1,)), ((), ()))
_TN = (((0,), (0,)), ((), ()))


def _mem_probs(q_ref, k_ref):
    qs = (q_ref[...] * (HEAD_DIM ** -0.5)).astype(BF16)
    s = lax.dot_general(qs, k_ref[...].astype(BF16), _NT, preferred_element_type=F32)
    e = jnp.exp(s - jnp.max(s, axis=-1, keepdims=True))
    return qs, e / jnp.sum(e, axis=-1, keepdims=True)


def _mem_attn_fwd(proj, kvm, *, tq=2048):
    L = proj.shape[0]
    tq = min(tq, L)

    def body(q_ref, k_ref, v_ref, o_ref):
        _, p = _mem_probs(q_ref, k_ref)
        o_ref[...] = jnp.dot(p.astype(BF16), v_ref[...].astype(BF16), preferred_element_type=F32)

    return pl.pallas_call(
        body, name="mem_attn_fwd", out_shape=jax.ShapeDtypeStruct((L, MEM_WIDTH), F32),
        grid=(MEM_HEADS, L // tq),
        in_specs=[pl.BlockSpec((tq, HEAD_DIM), lambda h, i: (i, _MEM_Q_COL + h)),
                  pl.BlockSpec((N_MEM, HEAD_DIM), lambda h, i: (0, h)),
                  pl.BlockSpec((N_MEM, HEAD_DIM), lambda h, i: (0, MEM_HEADS + h))],
        out_specs=pl.BlockSpec((tq, HEAD_DIM), lambda h, i: (i, h)),
        compiler_params=_params("parallel", "parallel"),
    )(proj, kvm, kvm)


def _mem_attn_bwd(proj, kvm, do, dproj, *, tq=2048):
    L = proj.shape[0]
    tq = min(tq, L)

    def body(q_ref, k_ref, v_ref, do_ref, dp_hbm, dq_ref, dk_ref, dv_ref):
        @pl.when(pl.program_id(1) == 0)
        def _():
            dk_ref[...] = jnp.zeros_like(dk_ref)
            dv_ref[...] = jnp.zeros_like(dv_ref)

        qs, p = _mem_probs(q_ref, k_ref)
        dob = do_ref[...].astype(BF16)
        dp = lax.dot_general(dob, v_ref[...].astype(BF16), _NT, preferred_element_type=F32)
        ds = p * (dp - jnp.sum(p * dp, axis=-1, keepdims=True))
        dsb = ds.astype(BF16)
        dq = jnp.dot(dsb, k_ref[...].astype(BF16), preferred_element_type=F32) * (HEAD_DIM ** -0.5)
        dq_ref[...] = dq.astype(BF16)
        dk_ref[...] += lax.dot_general(dsb, qs, _TN, preferred_element_type=F32)
        dv_ref[...] += lax.dot_general(p.astype(BF16), dob, _TN, preferred_element_type=F32)

    dproj, dk, dv = pl.pallas_call(
        body, name="mem_attn_bwd",
        out_shape=(jax.ShapeDtypeStruct(dproj.shape, dproj.dtype),
                   jax.ShapeDtypeStruct((N_MEM, MEM_WIDTH), F32),
                   jax.ShapeDtypeStruct((N_MEM, MEM_WIDTH), F32)),
        grid=(MEM_HEADS, L // tq),
        in_specs=[pl.BlockSpec((tq, HEAD_DIM), lambda h, i: (i, _MEM_Q_COL + h)),
                  pl.BlockSpec((N_MEM, HEAD_DIM), lambda h, i: (0, h)),
                  pl.BlockSpec((N_MEM, HEAD_DIM), lambda h, i: (0, MEM_HEADS + h)),
                  pl.BlockSpec((tq, HEAD_DIM), lambda h, i: (i, h)),
                  _ANY],
        out_specs=(pl.BlockSpec((tq, HEAD_DIM), lambda h, i: (i, _MEM_Q_COL + h)),
                   pl.BlockSpec((N_MEM, HEAD_DIM), lambda h, i: (0, h)),
                   pl.BlockSpec((N_MEM, HEAD_DIM), lambda h, i: (0, h))),
        input_output_aliases={4: 0},
        compiler_params=_params("parallel", "arbitrary"),
    )(proj, kvm, kvm, do, dproj)
    return dproj, jnp.concatenate([dk, dv], axis=1)


def _tile_cumsum(x, row, reverse):
    for sh in (1, 2, 4):
        if reverse:
            x = x + jnp.where(row < SUBLANES - sh, pltpu.roll(x, SUBLANES - sh, 0), 0.0)
        else:
            x = x + jnp.where(row >= sh, pltpu.roll(x, sh, 0), 0.0)
    return x


def _fgate_fwd(pre, b_pad):
    L = pre.shape[0]
    n8 = L // SUBLANES

    def body(p_ref, b_ref, o_ref):
        row = lax.broadcasted_iota(jnp.int32, (SUBLANES, LANES), 0)
        b = b_ref[...]

        def step(i, carry):
            x = p_ref[i] + b
            logf = jnp.minimum(x, 0.0) - jnp.log(1.0 + jnp.exp(-jnp.abs(x)))
            t = _tile_cumsum(logf, row, False) + carry
            o_ref[i] = t
            return t[SUBLANES - 1:SUBLANES, :]

        lax.fori_loop(0, n8, step, jnp.zeros((1, LANES), F32))

    out = pl.pallas_call(
        body, name="fgate_fwd", out_shape=jax.ShapeDtypeStruct((n8, SUBLANES, LANES), F32),
        compiler_params=_params(),
    )(pre.reshape(n8, SUBLANES, LANES), b_pad.reshape(1, LANES))
    return out.reshape(L, LANES)


def _fgate_bwd(dfcum, pre, b_pad):
    L = pre.shape[0]
    n8 = L // SUBLANES

    def body(d_ref, p_ref, b_ref, o_ref, s_ref):
        row = lax.broadcasted_iota(jnp.int32, (SUBLANES, LANES), 0)
        b = b_ref[...]

        def step(k, carry):
            c, acc = carry
            i = n8 - 1 - k
            t = _tile_cumsum(d_ref[i], row, True) + c
            dpre = t * _sigmoid(-(p_ref[i] + b))
            o_ref[i] = dpre
            return t[0:1, :], acc + dpre

        _, acc = lax.fori_loop(0, n8, step, (jnp.zeros((1, LANES), F32), jnp.zeros((SUBLANES, LANES), F32)))
        s_ref[...] = jnp.sum(acc, axis=0, keepdims=True)

    dpre, db = pl.pallas_call(
        body, name="fgate_bwd",
        out_shape=(jax.ShapeDtypeStruct((n8, SUBLANES, LANES), F32), jax.ShapeDtypeStruct((1, LANES), F32)),
        compiler_params=_params(),
    )(dfcum.reshape(n8, SUBLANES, LANES), pre.reshape(n8, SUBLANES, LANES), b_pad.reshape(1, LANES))
    return dpre.reshape(L, LANES), db


FOX_BLOCK = 1024


def _fox_scores(qs, k, fk, diagonal, row0=0):
    s = lax.dot_general(qs, k, _NT, preferred_element_type=F32) - fk
    if diagonal:
        row = row0 + lax.broadcasted_iota(jnp.int32, s.shape, 0)
        col = lax.broadcasted_iota(jnp.int32, s.shape, 1)
        s = jnp.where(row >= col, s, NEG_BIG)
    return s


def _fox_diagonal_parts(tq):
    half = tq // 2
    return ((slice(0, half), half), (slice(half, tq), tq))


def _fox_specs(tq, L):
    nq = L // tq
    return dict(
        rows=lambda off: pl.BlockSpec((tq, HEAD_DIM), lambda h, i: (i, off + h)),
        seq=lambda off: pl.BlockSpec((L, HEAD_DIM), lambda h, i: (0, off + h)),
        col=pl.BlockSpec((None, None, tq, 1), lambda h, i: (h, i, 0, 0)),
        col_all=pl.BlockSpec((None, nq, tq, 1), lambda h, i: (h, 0, 0, 0)),
        row=pl.BlockSpec((None, None, 1, tq), lambda h, i: (h, i, 0, 0)),
        row_all=pl.BlockSpec((None, nq, 1, tq), lambda h, i: (h, 0, 0, 0)))


FOX_FWD_HEADS = 2
FOX_FWD_BLOCK = 1024


def _fox_fwd(proj, kv, fk):
    L = proj.shape[0]
    tq = min(FOX_FWD_BLOCK, L)
    nq = L // tq
    nh = FOX_FWD_HEADS
    W = nh * HEAD_DIM
    lse_shape = fk.shape[:2] + (fk.shape[3], 1)
    fk = fk.reshape(FOX_HEADS, nq, 1, tq)

    def body(q_ref, k_ref, v_ref, fk_ref, o_ref, lse_ref, m_s, l_s, acc_s):
        qi = pl.program_id(1)
        cols = [slice(a * HEAD_DIM, (a + 1) * HEAD_DIM) for a in range(nh)]
        qs = [(q_ref[:, cs] * (HEAD_DIM ** -0.5)).astype(BF16) for cs in cols]
        m_s[...] = jnp.full_like(m_s, NEG_BIG)
        l_s[...] = jnp.zeros_like(l_s)
        acc_s[...] = jnp.zeros_like(acc_s)

        def block(j, diagonal):
            r0 = pl.multiple_of(j * tq, tq)
            for a, cs in enumerate(cols):
                s = _fox_scores(qs[a], k_ref[pl.ds(r0, tq), cs], fk_ref[a, j], diagonal)
                m_new = jnp.maximum(m_s[a], jnp.max(s, axis=-1, keepdims=True))
                alpha = jnp.exp(m_s[a] - m_new)
                p = jnp.exp(s - m_new)
                l_s[a] = alpha * l_s[a] + jnp.sum(p, axis=-1, keepdims=True)
                acc_s[a] = alpha * acc_s[a] + jnp.dot(p.astype(BF16), v_ref[pl.ds(r0, tq), cs],
                                                      preferred_element_type=F32)
                m_s[a] = m_new

        def below(j, carry):
            block(j, False)
            return carry

        lax.fori_loop(0, qi, below, 0)
        block(qi, True)
        for a, cs in enumerate(cols):
            o_ref[:, cs] = acc_s[a] / l_s[a]
            lse_ref[a] = m_s[a] + jnp.log(l_s[a])

    att, lse = pl.pallas_call(
        body, name="fox_fwd",
        out_shape=(jax.ShapeDtypeStruct((L, MAIN_WIDTH), F32),
                   jax.ShapeDtypeStruct((FOX_HEADS, nq, tq, 1), F32)),
        grid=(FOX_HEADS // nh, nq),
        in_specs=[pl.BlockSpec((tq, W), lambda h, i: (i, h)),
                  pl.BlockSpec((L, W), lambda h, i: (0, h)),
                  pl.BlockSpec((L, W), lambda h, i: (0, FOX_HEADS // nh + h)),
                  pl.BlockSpec((nh, nq, 1, tq), lambda h, i: (h, 0, 0, 0))],
        out_specs=(pl.BlockSpec((tq, W), lambda h, i: (i, h)),
                   pl.BlockSpec((nh, None, tq, 1), lambda h, i: (h, i, 0, 0))),
        scratch_shapes=[pltpu.VMEM((nh, tq, 1), F32), pltpu.VMEM((nh, tq, 1), F32),
                        pltpu.VMEM((nh, tq, HEAD_DIM), F32)],
        compiler_params=_params("parallel", "parallel"),
    )(proj, kv, kv, fk)
    return att, lse.reshape(lse_shape)


def _fox_bwd(proj, kv, fk, lse, delta, datt, dproj):
    L = proj.shape[0]
    tq = min(FOX_BLOCK, L)
    nq = L // tq
    sp = _fox_specs(tq, L)

    def body(q_ref, k_ref, v_ref, fk_ref, lse_ref, dl_ref, do_ref, dp_hbm,
             dq_ref, dk_ref, dv_ref, dfq_ref, dfk_ref, dk_s, dv_s, df_s, dq_s, dfq_s):
        ki = pl.program_id(1)

        @pl.when(ki == 0)
        def _():
            dq_s[...] = jnp.zeros_like(dq_s)
            dfq_s[...] = jnp.zeros_like(dfq_s)

        k, v, fk = k_ref[...], v_ref[...], fk_ref[...]
        dk_s[...] = jnp.zeros_like(dk_s)
        dv_s[...] = jnp.zeros_like(dv_s)
        df_s[...] = jnp.zeros_like(df_s)

        def block(i, rows, width, diagonal):
            n = rows.stop - rows.start
            r0 = pl.multiple_of(i * tq + rows.start, n)
            qs = (q_ref[pl.ds(r0, n), :] * (HEAD_DIM ** -0.5)).astype(BF16)
            dob = do_ref[pl.ds(r0, n), :].astype(BF16)
            kw, vw = k[:width], v[:width]
            p = jnp.exp(_fox_scores(qs, kw, fk[:, :width], diagonal, rows.start) - lse_ref[i][rows])
            dp = lax.dot_general(dob, vw, _NT, preferred_element_type=F32)
            ds = p * (dp - dl_ref[i][rows])
            dsb = ds.astype(BF16)
            dv_s[:width] += lax.dot_general(p.astype(BF16), dob, _TN, preferred_element_type=F32)
            dk_s[:width] += lax.dot_general(dsb, qs, _TN, preferred_element_type=F32)
            df_s[:, :width] -= jnp.sum(ds, axis=0, keepdims=True)
            dq_s[i, rows] += jnp.dot(dsb, kw, preferred_element_type=F32)
            dfq_s[i, rows] += jnp.sum(ds, axis=1, keepdims=True)

        def above(i, carry):
            block(i, slice(0, tq), tq, False)
            return carry

        for rows, width in _fox_diagonal_parts(tq):
            block(ki, rows, width, True)
        lax.fori_loop(ki + 1, nq, above, 0)
        dk_ref[...] = dk_s[...].astype(BF16)
        dv_ref[...] = dv_s[...].astype(BF16)
        dfk_ref[...] = df_s[...]

        @pl.when(ki == nq - 1)
        def _():
            dq_ref[...] = (dq_s[...].reshape(L, HEAD_DIM) * (HEAD_DIM ** -0.5)).astype(BF16)
            dfq_ref[...] = dfq_s[...]

    return pl.pallas_call(
        body, name="fox_bwd",
        out_shape=(jax.ShapeDtypeStruct(dproj.shape, dproj.dtype),
                   jax.ShapeDtypeStruct((L, MAIN_WIDTH), BF16),
                   jax.ShapeDtypeStruct((L, MAIN_WIDTH), BF16),
                   jax.ShapeDtypeStruct((FOX_HEADS, nq, tq, 1), F32),
                   jax.ShapeDtypeStruct((FOX_HEADS, nq, 1, tq), F32)),
        grid=(FOX_HEADS, nq),
        in_specs=[sp["seq"](0), sp["rows"](0), sp["rows"](FOX_HEADS), sp["row"],
                  sp["col_all"], sp["col_all"], sp["seq"](0), _ANY],
        out_specs=(sp["seq"](0), sp["rows"](0), sp["rows"](0), sp["col_all"], sp["row"]),
        input_output_aliases={7: 0},
        scratch_shapes=[pltpu.VMEM((tq, HEAD_DIM), F32), pltpu.VMEM((tq, HEAD_DIM), F32),
                        pltpu.VMEM((1, tq), F32), pltpu.VMEM((nq, tq, HEAD_DIM), F32),
                        pltpu.VMEM((nq, tq, 1), F32)],
        compiler_params=_params("parallel", "arbitrary"),
    )(proj, kv, kv, fk, lse, delta, datt, dproj)


def _pad_lanes(a):
    return jnp.pad(a, ((0, 0), (0, LANES - a.shape[1])))


def _mem_branch_fwd(memn, w_mk, proj, tag):
    kvm = _mm(memn, w_mk, name="mem_kv_" + tag)
    return kvm, _mem_attn_fwd(proj, kvm)


def _mem_branch_bwd(mem, g, w_mk, proj, memn, kvm, do_mem, dproj, tag):
    dproj, dkvm = _mem_attn_bwd(proj, kvm, do_mem, dproj)
    dkvm = dkvm.astype(BF16)
    dw_mk = _mm(memn, dkvm, ta=True, name="dw_mem_kv_" + tag, out_dtype=BF16)
    dmemn = _mm(dkvm, w_mk, tb=True, name="dmemn_" + tag)
    _, dg = _rmsnorm_bwd(mem, g, dmemn, name="mem_norm_bwd_" + tag, dx_dtype=BF16)
    return dproj, dw_mk, dg


def _local_step(x, mem, target, w, fetch=None, grads_ready=None):
    if grads_ready is None:
        grads_ready = lambda group, grads, token: token
    L = x.shape[0]
    g = {}
    w = dict(w)

    b_re_t = jnp.transpose(w["b_re"], (0, 2, 1))
    b_im_t = jnp.transpose(w["b_im"], (0, 2, 1))
    ar, ai, bbr_t, bbi_t = _s5_prep(w["lam_re"], w["lam_im"], w["log_step"], b_re_t, b_im_t)
    bmat, cmat = _s5_block_mats(bbr_t, bbi_t, w["c_re"], w["c_im"])
    a_rows = _s5_a_rows(ar, ai)

    hn0 = _rmsnorm_fwd(x, w["pre_norm_g"][0], name="pre_norm_0", out_dtype=BF16)
    memn0 = _rmsnorm_fwd(mem, w["mem_norm_g"][0], name="mem_norm_0", out_dtype=BF16)
    memn1 = _rmsnorm_fwd(mem, w["mem_norm_g"][1], name="mem_norm_1", out_dtype=BF16)
    if fetch is not None:
        w.update(fetch("a", [hn0, memn0, memn1, bmat, cmat, a_rows]))
    proj_a = _mm(hn0, w["w_in_a"], name="in_proj_a")
    y, yg, xp = _s5_fwd(proj_a, bmat, cmat, a_rows, w["d_skip"])
    if fetch is not None:
        w.update(fetch("b", yg))
    t = _mm(yg, w["w_glu"], name="glu_proj")
    kvm0, om0 = _mem_branch_fwd(memn0, w["w_mem_kv"][0], proj_a, "0")
    cat0 = _gate_a_fwd(y, t, w["b_glu"], proj_a, om0)
    o0 = _mm(cat0, w["w_out"][0], name="out_proj_0")
    h1, kv_in, hn1 = _post_norm_and_next_norms(
        o0, w["post_norm_g"][0], x, w["kv_norm_g"], w["pre_norm_g"][1], name="post_norm_0_kv_pre_norm_1")

    if fetch is not None:
        w.update(fetch("c", kv_in))
    kv = _mm(kv_in, w["w_kv"], name="kv_proj", out_dtype=BF16)
    pre_f = _mm(kv_in, w["w_fgate"], name="fgate_proj")
    b_f = jnp.pad(w["b_fgate"], (0, LANES - FOX_HEADS))
    fcum = _fgate_fwd(pre_f, b_f)
    fc = jnp.transpose(fcum[:, :FOX_HEADS])
    tq = min(FOX_BLOCK, L)
    fk = fc.reshape(FOX_HEADS, L // tq, 1, tq)

    proj_b = _mm(hn1, w["w_in_b"], name="in_proj_b")
    att, lse = _fox_fwd(proj_b, kv, fk)
    kvm1, om1 = _mem_branch_fwd(memn1, w["w_mem_kv"][1], proj_b, "1")
    cat1 = _gate_b_fwd(att, proj_b, om1)
    o1 = _mm(cat1, w["w_out"][1], name="out_proj_1")
    dh2, loss_row = _final_norm_loss(o1, w["post_norm_g"][1], h1, target)

    do1, dpost1 = _rmsnorm_bwd(o1, w["post_norm_g"][1], dh2, name="post_norm_bwd_1", dx_dtype=BF16)
    dcat1 = _mm(do1, w["w_out"][1], tb=True, name="dcat_1", out_dtype=BF16)
    g["w_out_1"] = _mm(cat1, do1, ta=True, name="dw_out_1", out_dtype=BF16)
    datt, dproj_b, dom1, delta = _gate_b_bwd(dcat1, att, proj_b, om1)
    dproj_b, g["w_mem_kv_1"], dmemg1 = _mem_branch_bwd(mem, w["mem_norm_g"][1], w["w_mem_kv"][1], proj_b,
                                                      memn1, kvm1, dom1, dproj_b, "1")
    delta = delta.reshape(lse.shape)
    dproj_b, dk, dv, dfq, dfk = _fox_bwd(proj_b, kv, fk, lse, delta, datt, dproj_b)
    g["w_in_b"] = _mm(hn1, dproj_b, ta=True, name="dw_in_b", out_dtype=BF16, shards=N_CHIPS)
    dhn1 = _mm(dproj_b, w["w_in_b"], tb=True, name="dhn_1")

    dkv = jnp.concatenate([dk, dv], axis=1)
    g["w_kv"] = _mm(kv_in, dkv, ta=True, name="dw_kv", out_dtype=BF16, shards=N_CHIPS)
    dkv_in_a = _mm(dkv, w["w_kv"], tb=True, name="dkv_in_kv")
    dfcum = _pad_lanes(jnp.transpose(dfq.reshape(FOX_HEADS, L) + dfk.reshape(FOX_HEADS, L)))
    dpre_f, db_f = _fgate_bwd(dfcum, pre_f, b_f)
    g["b_fgate"] = db_f[0, :FOX_HEADS]
    g["w_fgate"] = _mm(kv_in, dpre_f, ta=True, name="dw_fgate")[:, :FOX_HEADS]
    dkv_in_b = _mm(dpre_f, w["w_fgate"], tb=True, name="dkv_in_fgate")
    dh1, g["kv_norm_g"], dpre1 = _rmsnorm_bwd_pair(h1, w["kv_norm_g"], (dkv_in_a, dkv_in_b), w["pre_norm_g"][1],
                                                   dhn1, adds=(dh2,), name="kv_pre_norm_bwd")
    dh1 = grads_ready("b", g, dh1)

    do0, dpost0 = _rmsnorm_bwd(o0, w["post_norm_g"][0], dh1, name="post_norm_bwd_0", dx_dtype=BF16)
    dcat0 = _mm(do0, w["w_out"][0], tb=True, name="dcat_0", out_dtype=BF16)
    g["w_out_0"] = _mm(cat0, do0, ta=True, name="dw_out_0", out_dtype=BF16)
    dcat0 = grads_ready("b_send", g, dcat0)
    dproj_a, dt, dyg_a, dom0, db_glu = _gate_a_bwd(dcat0, y, t, w["b_glu"], proj_a, om0)
    g["b_glu"] = db_glu[0]
    g["w_glu"] = _mm(yg, dt, ta=True, name="dw_glu", out_dtype=BF16)
    dyg_b = _mm(dt, w["w_glu"], tb=True, name="dyg")
    dproj_a, g["w_mem_kv_0"], dmemg0 = _mem_branch_bwd(mem, w["mem_norm_g"][0], w["w_mem_kv"][0], proj_a,
                                                      memn0, kvm0, dom0, dproj_a, "0")
    dyg_b = grads_ready("a1", g, dyg_b)
    dproj_a, db_blk, dc_blk, da_rows, dd_skip = _s5_bwd(proj_a, dyg_a, dyg_b, y, xp, bmat, cmat, a_rows,
                                                        w["d_skip"], dproj_a)
    dproj_a = grads_ready("a1_send", g, dproj_a)
    g["d_skip"] = dd_skip[0]
    g["w_in_a"] = _mm(hn0, dproj_a, ta=True, name="dw_in_a", out_dtype=BF16, shards=N_CHIPS)
    dproj_a = grads_ready("a2", g, dproj_a)
    dhn0 = _mm(dproj_a, w["w_in_a"], tb=True, name="dhn_0")
    grad_x, dpre0 = _rmsnorm_bwd(x, w["pre_norm_g"][0], dhn0, adds=(dh1,), name="pre_norm_bwd_0")

    dbb = _s5_unfold(db_blk)
    dcc = _s5_unfold(dc_blk)
    g["c_re"], g["c_im"] = dcc[0], -dcc[1]
    d_ar = da_rows[:, 0, :STATE_COLS].reshape(SSM_GROUPS, SSM_STATE)
    d_ai = da_rows[:, 0, STATE_COLS:].reshape(SSM_GROUPS, SSM_STATE)
    dlr, dli, dls, dbr_t, dbi_t = _s5_prep_bwd(w["lam_re"], w["lam_im"], w["log_step"], b_re_t, b_im_t,
                                               d_ar, d_ai, dbb[0], dbb[1])
    g["lam_re"], g["lam_im"], g["log_step"] = dlr, dli, dls[:, 0]
    g["b_re"] = jnp.transpose(dbr_t, (0, 2, 1))
    g["b_im"] = jnp.transpose(dbi_t, (0, 2, 1))
    g["pre_norm_g"] = jnp.stack([dpre0, dpre1])
    g["post_norm_g"] = jnp.stack([dpost0, dpost1])
    g["mem_norm_g"] = jnp.stack([dmemg0, dmemg1])
    return loss_row, grad_x, g


_MESH = pl.DeviceIdType.MESH
_ANY = pl.BlockSpec(memory_space=pl.ANY)


def _place():
    x, y, c = lax.axis_index("x"), lax.axis_index("y"), lax.axis_index("c")
    chips = [(1 - x, y), (x, 1 - y), (1 - x, 1 - y)]
    return x, y, c, chips


_HBM = pl.BlockSpec(memory_space=pltpu.HBM)
_SEM = pl.BlockSpec(memory_space=pltpu.SEMAPHORE)
_SIDE = pltpu.SideEffectType.DATAFLOW_SIDE_EFFECTING


def _in_hbm(a):
    return pltpu.with_memory_space_constraint(a, pltpu.HBM)


def _hbm_like(a):
    return pltpu.HBM(a.shape, a.dtype)


def _ici_copies(srcs, lands, send_sem, recv_sem, src_at, dst_at, wait_at, to_sibling=False):
    x, y, c, chips = _place()
    peers = [(x, y, 1 - c)] if to_sibling else [(cx, cy, c) for cx, cy in chips]
    m = len(peers)
    start, wait = [], []
    for i in range(len(srcs)):
        for k, (px, py, pc) in enumerate(peers):
            sem = dict(send_sem=send_sem.at[m * i + k], recv_sem=recv_sem.at[m * i + k],
                       device_id=(px, py, pc), device_id_type=_MESH)
            src = src_at(srcs[i], 2 * px + py, c)
            start.append(pltpu.make_async_remote_copy(src_ref=src, dst_ref=dst_at(lands[i], 2 * x + y, k, c), **sem))
            wait.append(pltpu.make_async_remote_copy(src_ref=src, dst_ref=wait_at(lands[i], 2 * px + py, k, c), **sem))
    return start, wait


def _route_peers(route):
    return 1 if len(route) == 4 else 3


_BLOCK_ROUTE = (lambda s, j, c: s, lambda l, me, k, c: l.at[me, c], lambda l, j, k, c: l.at[j, c])


def _ici_start(srcs, lands, token, route, *, name):
    n = len(srcs)

    def body(*refs):
        start, _ = _ici_copies(refs[:n], refs[n:2 * n], refs[2 * n + 1], refs[2 * n + 2], *route)
        for cp in start:
            cp.start()

    sems = pltpu.SemaphoreType.DMA((_route_peers(route) * n,))
    outs = pl.pallas_call(
        body, name=name,
        out_shape=(sems, sems, *[_hbm_like(a) for a in srcs], *[_hbm_like(a) for a in lands], _hbm_like(token)),
        in_specs=[_HBM] * (2 * n + 1), out_specs=(_SEM, _SEM, *[_HBM] * (2 * n + 1)),
        input_output_aliases={i: 2 + i for i in range(2 * n + 1)},
        compiler_params=pltpu.CompilerParams(has_side_effects=_SIDE),
    )(*[_in_hbm(a) for a in srcs], *[_in_hbm(a) for a in lands], _in_hbm(token))
    return (outs[0], outs[1], list(outs[2:2 + n]), list(outs[2 + n:2 + 2 * n])), outs[2 + 2 * n]


def _ici_wait(handle, after, route, *, name):
    send_sem, recv_sem, srcs, lands = handle
    n = len(srcs)
    after = list(after) if isinstance(after, (list, tuple)) else [after]

    def body(*refs):
        _, wait = _ici_copies(refs[:n], refs[n:2 * n], refs[2 * n], refs[2 * n + 1], *route)
        for cp in wait:
            cp.wait_send()
            cp.wait_recv()

    outs = pl.pallas_call(
        body, name=name,
        out_shape=(*[_hbm_like(a) for a in srcs], *[_hbm_like(a) for a in lands]),
        in_specs=[_HBM] * (2 * n) + [_SEM, _SEM] + [_ANY] * len(after), out_specs=tuple([_HBM] * (2 * n)),
        input_output_aliases={i: i for i in range(2 * n)},
        compiler_params=pltpu.CompilerParams(has_side_effects=_SIDE),
    )(*srcs, *lands, send_sem, recv_sem, *after)
    return list(outs[:n]), list(outs[n:])


_GATHER_ROUTE = (lambda s, j, c: s.at[c], lambda l, me, k, c: l.at[me, c], lambda l, j, k, c: l.at[j, c])
_SCATTER_ROUTE = (lambda s, j, c: s.at[j], lambda l, me, k, c: l.at[k], lambda l, j, k, c: l.at[k])
_SHARE_ROUTE = (lambda s, j, c: s, lambda l, me, k, c: l.at[c], lambda l, j, k, c: l.at[1 - c], True)
_SWAP_ROUTE = (lambda s, j, c: s.at[:, 1 - c], lambda l, me, k, c: l, lambda l, j, k, c: l, True)


def _gather_forward(lands, tag, own=False):
    n = len(lands)
    m = 4 if own else 3

    def body(*refs):
        ins, outs = refs[:n], refs[n:2 * n]
        send_sem, recv_sem = refs[2 * n:]
        x, y, c, chips = _place()
        slots = [2 * cx + cy for cx, cy in chips] + [2 * x + y]

        def copy(i, k, half):
            return pltpu.make_async_remote_copy(
                src_ref=ins[i].at[slots[k], half], dst_ref=outs[i].at[slots[k], half],
                send_sem=send_sem.at[m * i + k], recv_sem=recv_sem.at[m * i + k],
                device_id=(x, y, 1 - c), device_id_type=_MESH)

        copies = [copy(i, k, c) for i in range(n) for k in range(m)]
        for cp in copies:
            cp.start()
        for i in range(n):
            for k in range(m):
                copy(i, k, 1 - c).wait_recv()
        for cp in copies:
            cp.wait_send()

    return pl.pallas_call(
        body, name="gather_forward_to_sibling_" + tag,
        out_shape=[jax.ShapeDtypeStruct(a.shape, a.dtype) for a in lands],
        in_specs=[_ANY] * n, out_specs=[_ANY] * n,
        input_output_aliases={i: i for i in range(n)},
        scratch_shapes=[pltpu.SemaphoreType.DMA((m * n,)), pltpu.SemaphoreType.DMA((m * n,))],
    )(*lands)


def _swap_halves(grads, tag):
    n = len(grads)

    def body(*refs):
        ins, outs = refs[:n], refs[n:2 * n]
        send_sem, recv_sem = refs[2 * n:]
        x, y, c, _ = _place()
        copies = [pltpu.make_async_remote_copy(
            src_ref=ins[i].at[:, 1 - c], dst_ref=outs[i],
            send_sem=send_sem.at[i], recv_sem=recv_sem.at[i],
            device_id=(x, y, 1 - c), device_id_type=_MESH) for i in range(n)]
        for cp in copies:
            cp.start()
        for cp in copies:
            cp.wait()

    return pl.pallas_call(
        body, name="grad_swap_halves_" + tag,
        out_shape=[jax.ShapeDtypeStruct((N_CHIPS,) + g.shape[2:], g.dtype) for g in grads],
        in_specs=[_ANY] * n, out_specs=[_ANY] * n,
        scratch_shapes=[pltpu.SemaphoreType.DMA((n,)), pltpu.SemaphoreType.DMA((n,))],
    )(*grads)


def _sum_rows(h, C):
    return max(d for d in range(SUBLANES, h + 1, SUBLANES) if h % d == 0 and d * C <= 1 << 20)


SUM_STEPS = 4


def _pair_sums(gs, rs, c_idx, *, name):
    n = len(gs)
    rows = [g.shape[2] // SUM_STEPS for g in gs]

    def body(c_ref, *refs):
        for g_ref, r_ref, o_ref in zip(refs[:n], refs[n:2 * n], refs[2 * n:]):
            o_ref[...] = (g_ref[...].astype(F32) + r_ref[...].astype(F32)).astype(o_ref.dtype)

    return pl.pallas_call(
        body, name=name,
        out_shape=[jax.ShapeDtypeStruct((N_CHIPS,) + g.shape[2:], g.dtype) for g in gs],
        grid_spec=pltpu.PrefetchScalarGridSpec(
            num_scalar_prefetch=1, grid=(N_CHIPS, SUM_STEPS),
            in_specs=[pl.BlockSpec((None, None, tr, g.shape[3]), lambda j, i, s: (j, s[0], i, 0))
                      for g, tr in zip(gs, rows)]
            + [pl.BlockSpec((None, tr, g.shape[3]), lambda j, i, s: (j, i, 0)) for g, tr in zip(gs, rows)],
            out_specs=[pl.BlockSpec((None, tr, g.shape[3]), lambda j, i, s: (j, i, 0)) for g, tr in zip(gs, rows)]),
        compiler_params=_params("parallel", "parallel"),
    )(c_idx, *gs, *rs)


def _owner_sums(ss, rs, jc_idx, *, name):
    n = len(ss)
    rows = [s.shape[1] // SUM_STEPS for s in ss]

    def body(jc_ref, *refs):
        for s_ref, r_ref, m_ref, o_ref in zip(refs[:n], refs[n:2 * n], refs[2 * n:3 * n], refs[3 * n:]):
            acc = s_ref[...].astype(F32)
            for k in range(3):
                acc = acc + r_ref[k].astype(F32)
            m_ref[...] = acc
            o_ref[...] = acc

    outs = pl.pallas_call(
        body, name=name,
        out_shape=[jax.ShapeDtypeStruct(s.shape[1:], F32) for s in ss]
        + [jax.ShapeDtypeStruct((2,) + s.shape[1:], F32) for s in ss],
        grid_spec=pltpu.PrefetchScalarGridSpec(
            num_scalar_prefetch=1, grid=(SUM_STEPS,),
            in_specs=[pl.BlockSpec((None, tr, s.shape[2]), lambda i, p: (p[0], i, 0)) for s, tr in zip(ss, rows)]
            + [pl.BlockSpec((3, tr, s.shape[2]), lambda i, p: (0, i, 0)) for s, tr in zip(ss, rows)],
            out_specs=[pl.BlockSpec((tr, s.shape[2]), lambda i, p: (i, 0)) for s, tr in zip(ss, rows)]
            + [pl.BlockSpec((None, tr, s.shape[2]), lambda i, p: (p[1], i, 0)) for s, tr in zip(ss, rows)]),
        compiler_params=_params("parallel"),
    )(jc_idx, *ss, *rs)
    return outs[:n], outs[n:]


def _chip_sums(grads, c_idx, tag):
    views = [g.reshape(N_CHIPS, 2, g.shape[1] // 2, g.shape[2]) for g in grads]
    arrived = _swap_halves(views, tag)
    return _pair_sums(views, arrived, c_idx, name=f"grad_pair_sums_{tag}")


def _sum_devices(blocks):
    R = blocks.shape[2]
    tr = _sum_rows(R, 2 * N_CHIPS * LANES)

    def body(b_ref, o_ref):
        acc = b_ref[0, 0]
        for d in range(1, 2 * N_CHIPS):
            acc = acc + b_ref[d // 2, d % 2]
        o_ref[...] = acc

    return pl.pallas_call(
        body, name="sum_small_over_devices", out_shape=jax.ShapeDtypeStruct((R, LANES), F32),
        grid=(R // tr,),
        in_specs=[pl.BlockSpec((N_CHIPS, 2, tr, LANES), lambda i: (0, 0, i, 0))],
        out_specs=pl.BlockSpec((tr, LANES), lambda i: (i, 0)),
        compiler_params=_params("parallel"),
    )(blocks)


def _adamw(w, g, m, v, *, name):
    R, C = w.shape
    tr = max(d for d in range(SUBLANES, R + 1, SUBLANES)
             if R % d == 0 and 7 * 2 * d * C * 4 <= VMEM_LIMIT_BYTES // 2)

    def body(w_ref, g_ref, m_ref, v_ref, d_ref, nm_ref, nv_ref):
        g = g_ref[...]
        m = ADAM_B1 * m_ref[...] + (1.0 - ADAM_B1) * g
        v = ADAM_B2 * v_ref[...] + (1.0 - ADAM_B2) * (g * g)
        nm_ref[...] = m
        nv_ref[...] = v
        m_hat = m / (1.0 - ADAM_B1 ** ADAM_STEP)
        v_hat = v / (1.0 - ADAM_B2 ** ADAM_STEP)
        d_ref[...] = -ADAM_LR * (m_hat / (jnp.sqrt(v_hat) + ADAM_EPS) + ADAM_WD * w_ref[...])

    blk = pl.BlockSpec((tr, C), lambda i: (i, 0))
    sds = jax.ShapeDtypeStruct((R, C), F32)
    return pl.pallas_call(
        body, name=name, out_shape=(sds, sds, sds), grid=(R // tr,),
        in_specs=[blk] * 4, out_specs=(blk, blk, blk),
        compiler_params=_params("parallel"),
    )(w, g, m, v)


_TILE = SUBLANES * LANES


def _pack(arrays):
    rows = []
    for a in arrays:
        flat = a.reshape(-1)
        flat = jnp.pad(flat, (0, (-flat.shape[0]) % _TILE))
        rows.append(flat.reshape(-1, LANES))
    return jnp.concatenate(rows, axis=0)


def _unpack(buf, shapes):
    out, r = [], 0
    for s in shapes:
        size = math.prod(s)
        nr = -(-size // _TILE) * SUBLANES
        out.append(buf[r:r + nr].reshape(-1)[:size].reshape(s))
        r += nr
    return out


_BIG = ("w_in_a", "w_glu", "w_kv", "w_in_b", "w_mem_kv", "w_out")
_REPLICATED = ("pre_norm_g", "post_norm_g", "lam_re", "lam_im", "log_step", "b_re", "b_im", "c_re", "c_im",
               "kv_norm_g", "b_fgate", "mem_norm_g")
_SHARDED_SMALL = ("d_skip", "b_glu", "w_fgate")
_WEIGHTS = ("pre_norm_g", "post_norm_g", "w_in_a", "lam_re", "lam_im", "log_step", "b_re", "b_im", "c_re",
            "c_im", "d_skip", "w_glu", "b_glu", "kv_norm_g", "w_kv", "w_fgate", "b_fgate", "w_in_b",
            "mem_norm_g", "w_mem_kv", "w_out")


def _halves(a):
    return a.reshape(2, a.shape[0] // 2, a.shape[1])


def _unhalve(a):
    return a.reshape(N_CHIPS, 2 * a.shape[2], a.shape[3])


def _columns(a):
    return jnp.transpose(a, (1, 0, 2)).reshape(a.shape[1], N_CHIPS * a.shape[2])


def kernel(x, mem, pre_norm_g, post_norm_g, w_in_a, lam_re, lam_im, log_step, b_re, b_im, c_re, c_im, d_skip, w_glu, b_glu, kv_norm_g, w_kv, w_fgate, b_fgate, w_in_b, mem_norm_g, w_mem_kv, w_out, loss_target, m_pre_norm_g, m_post_norm_g, m_w_in_a, m_lam_re, m_lam_im, m_log_step, m_b_re, m_b_im, m_c_re, m_c_im, m_d_skip, m_w_glu, m_b_glu, m_kv_norm_g, m_w_kv, m_w_fgate, m_b_fgate, m_w_in_b, m_mem_norm_g, m_w_mem_kv, m_w_out, v_pre_norm_g, v_post_norm_g, v_w_in_a, v_lam_re, v_lam_im, v_log_step, v_b_re, v_b_im, v_c_re, v_c_im, v_d_skip, v_w_glu, v_b_glu, v_kv_norm_g, v_w_kv, v_w_fgate, v_b_fgate, v_w_in_b, v_mem_norm_g, v_w_mem_kv, v_w_out):
    a = dict(locals())
    xi, yi, ci = lax.axis_index("x"), lax.axis_index("y"), lax.axis_index("c")
    chip = 2 * xi + yi
    c_idx = jnp.reshape(ci, (1,)).astype(jnp.int32)
    jc_idx = jnp.stack([chip, ci]).astype(jnp.int32)

    vec = jnp.zeros((2 * SUBLANES, MAIN_WIDTH // N_CHIPS), F32)
    vec = vec.at[0].set(a["d_skip"][0]).at[1].set(a["b_glu"][0])
    def own_slot(gathered, parts):
        return [lax.dynamic_update_index_in_dim(g, p, chip, 0) for g, p in zip(gathered, parts)]

    travelling, token = {}, a["pre_norm_g"]

    def start_gather(tag, parts, token):
        lands = [lax.empty((N_CHIPS,) + p.shape, p.dtype) for p in parts]
        travelling[tag], token = _ici_start(parts, lands, token, _GATHER_ROUTE, name=f"gather_{tag}_start")
        return token

    token = start_gather("a", [_halves(a["w_in_a"][0].astype(BF16)), _halves(vec)], token)
    later = ("w_glu", "w_mem_kv", "w_out", "w_kv", "w_fgate", "w_in_b")
    token, *raw = lax.optimization_barrier((token, *[a[n] for n in later]))
    raw = dict(zip(later, raw))
    token = start_gather("b", [_halves(raw["w_glu"][0].astype(BF16)),
                               *[_halves(raw["w_mem_kv"][i].astype(BF16)) for i in range(2)],
                               *[_halves(raw["w_out"][i].astype(BF16)) for i in range(2)]], token)
    token = start_gather("c", [_halves(raw["w_kv"].astype(BF16)), _halves(_pad_lanes(raw["w_fgate"]).astype(BF16)),
                               _halves(raw["w_in_b"][0].astype(BF16))], token)

    small_names = _REPLICATED + _SHARDED_SMALL
    small_state = [_pack([a[n] for n in small_names])]

    def fetch(tag, after):
        if tag == "a":
            after = list(after) + small_state
        parts, lands = _ici_wait(travelling[tag], after, _GATHER_ROUTE, name=f"gather_{tag}_wait")
        full = own_slot(_gather_forward(lands, tag), parts)
        if tag == "a":
            w_in_a, vecs = full
            moments = [a[prefix + n] for prefix in ("m_", "v_") for n in small_names]
            w_in_a, *moments = lax.optimization_barrier((w_in_a, *moments))
            packs = [_pack(moments[:len(small_names)]), _pack(moments[len(small_names):])]
            w_in_a, *packs = lax.optimization_barrier((_columns(_unhalve(w_in_a)), *packs))
            small_state.extend(packs)
            return dict(w_in_a=w_in_a, d_skip=vecs[:, 0, 0, :].reshape(MAIN_WIDTH),
                        b_glu=vecs[:, 0, 1, :].reshape(MAIN_WIDTH))
        if tag == "b":
            w_glu, w_mk0, w_mk1, w_out0, w_out1 = full
            return dict(w_glu=w_glu.reshape(MAIN_WIDTH, MAIN_WIDTH),
                        w_mem_kv=[m.reshape(D_MODEL, 2 * MEM_WIDTH) for m in (w_mk0, w_mk1)],
                        w_out=[o.reshape(D_MODEL, D_MODEL) for o in (w_out0, w_out1)])
        w_kv, w_fg, w_in_b = full
        return dict(w_kv=_columns(_unhalve(w_kv)), w_fgate=w_fg.reshape(D_MODEL, LANES),
                    w_in_b=_columns(_unhalve(w_in_b)))

    early = ("mem_norm_g", "lam_re", "lam_im", "log_step", "b_re", "b_im", "c_re", "c_im")
    token, *held = lax.optimization_barrier((token, *[a[n] for n in early]))
    held = dict(zip(early, held))
    w = dict(
        pre_norm_g=token, post_norm_g=a["post_norm_g"], mem_norm_g=held["mem_norm_g"],
        kv_norm_g=a["kv_norm_g"], b_fgate=a["b_fgate"],
        **{n: held[n][0] for n in early[1:]})

    sent = {}

    swapping = {}

    def grads_ready(event, g, token):
        tag = event.split("_")[0]
        if event in ("b", "a1"):
            big = {"b": lambda: [g["w_kv"], g["w_in_b"], g["w_mem_kv_1"].reshape(N_CHIPS, -1, 2 * MEM_WIDTH),
                                 g["w_out_1"].reshape(N_CHIPS, -1, D_MODEL)],
                   "a1": lambda: [g["w_glu"].reshape(N_CHIPS, -1, MAIN_WIDTH),
                                  g["w_mem_kv_0"].reshape(N_CHIPS, -1, 2 * MEM_WIDTH),
                                  g["w_out_0"].reshape(N_CHIPS, -1, D_MODEL)]}[tag]()
            views = [b.reshape(N_CHIPS, 2, b.shape[1] // 2, b.shape[2]) for b in big]
            lands = [lax.empty((N_CHIPS,) + v.shape[2:], v.dtype) for v in views]
            swapping[tag], token = _ici_start(views, lands, token, _SWAP_ROUTE, name=f"grad_swap_{tag}_start")
            return token
        if event == "a2":
            sums = _chip_sums([g["w_in_a"]], c_idx, tag)
        else:
            views, arrived = _ici_wait(swapping[tag], token, _SWAP_ROUTE, name=f"grad_swap_{tag}_wait")
            sums = _pair_sums(views, arrived, c_idx, name=f"grad_pair_sums_{tag}")
        lands = [lax.empty((3,) + s.shape[1:], s.dtype) for s in sums]
        sent[tag], token = _ici_start(sums, lands, token, _SCATTER_ROUTE, name=f"grad_send_{tag}_start")
        return token

    loss_row, grad_x, g = _local_step(a["x"][0], a["mem"][0], a["loss_target"][0], w, fetch, grads_ready)

    pack = _pack([g[n] for n in small_names])
    blocks = lax.empty((N_CHIPS, 2) + pack.shape, F32)
    small_sent, token = _ici_start([pack], [blocks], loss_row, _BLOCK_ROUTE, name="small_sums_start")

    sharing = {}
    for tag in ("b", "a1", "a2"):
        sums, arrived = _ici_wait(sent[tag], [grad_x, token], _SCATTER_ROUTE, name=f"grad_send_{tag}_wait")
        mine, bufs = _owner_sums(sums, arrived, jc_idx, name=f"grad_owner_sums_{tag}")
        sharing[tag], token = _ici_start(mine, bufs, token, _SHARE_ROUTE, name=f"grad_share_{tag}_start")
    loss = lax.psum(jnp.sum(token), MESH_AXES)

    def shared(tag, after):
        _, bufs = _ici_wait(sharing[tag], after, _SHARE_ROUTE, name=f"grad_share_{tag}_wait")
        return [b.reshape(-1, b.shape[2]) for b in bufs]

    grads, delta, new_m, new_v = {}, {}, {}, {}

    def adam(n):
        shape = a[n].shape
        d2 = (-1, shape[-1])
        d, m, v = _adamw(a[n].reshape(d2), grads[n].reshape(d2), a["m_" + n].reshape(d2),
                         a["v_" + n].reshape(d2), name="adamw_" + n)
        delta[n], new_m[n], new_v[n] = d.reshape(shape), m.reshape(shape), v.reshape(shape)
        return d

    r_kv, r_in_b, r_mk1, r_out1 = shared("b", token)
    grads["w_kv"], grads["w_in_b"] = r_kv, r_in_b[None]
    done = [adam("w_kv"), adam("w_in_b")]
    r_glu, r_mk0, r_out0 = shared("a1", done)
    grads["w_glu"], grads["w_mem_kv"], grads["w_out"] = r_glu[None], jnp.stack([r_mk0, r_mk1]), jnp.stack([r_out0, r_out1])
    done = [adam("w_glu"), adam("w_mem_kv"), adam("w_out")]
    (r_in_a,) = shared("a2", done)
    grads["w_in_a"] = r_in_a[None]
    adam("w_in_a")

    (pack,), (blocks,) = _ici_wait(small_sent, [delta[n] for n in _BIG], _BLOCK_ROUTE, name="small_sums_wait")
    blocks = lax.dynamic_update_slice(blocks, pack[None, None], (chip, ci, 0, 0))
    (blocks,) = _gather_forward([blocks], "small", own=True)
    small = dict(zip(small_names, _unpack(_sum_devices(blocks), [g[n].shape for n in small_names])))
    for n in _REPLICATED:
        grads[n] = small[n].reshape(a[n].shape)
    nd = MAIN_WIDTH // N_CHIPS
    grads["d_skip"] = lax.dynamic_slice(small["d_skip"], (chip * nd,), (nd,))[None]
    grads["b_glu"] = lax.dynamic_slice(small["b_glu"], (chip * nd,), (nd,))[None]
    nf = D_MODEL // N_CHIPS
    grads["w_fgate"] = lax.dynamic_slice(small["w_fgate"], (chip * nf, 0), (nf, FOX_HEADS))

    shapes = [a[n].shape for n in small_names]
    d, m, v = _adamw(small_state[0], _pack([grads[n] for n in small_names]), *small_state[1:], name="adamw_small")
    for n, dd, mm, vv in zip(small_names, _unpack(d, shapes), _unpack(m, shapes), _unpack(v, shapes)):
        delta[n], new_m[n], new_v[n] = dd, mm, vv

    return (loss, grad_x[None], *[grads[n] for n in _WEIGHTS], *[delta[n] for n in _WEIGHTS],
            *[new_m[n] for n in _WEIGHTS], *[new_v[n] for n in _WEIGHTS])
```

```python
import math

import jax
import jax.numpy as jnp
from jax import lax
from jax.experimental import pallas as pl
from jax.experimental.pallas import tpu as pltpu

F32 = jnp.float32
BF16 = jnp.bfloat16

D_MODEL = 2048
N_MEM = 256
MAIN_WIDTH = 1536
MEM_WIDTH = 512
IN_WIDTH = 2 * MAIN_WIDTH + 2 * MEM_WIDTH
HEAD_DIM = 128
FOX_HEADS = MAIN_WIDTH // HEAD_DIM
MEM_HEADS = MEM_WIDTH // HEAD_DIM
SSM_GROUP = 16
SSM_GROUPS = MAIN_WIDTH // SSM_GROUP
SSM_STATE = 64
GROUPS_PER_BLOCK = 8
SSM_BLOCKS = SSM_GROUPS // GROUPS_PER_BLOCK
STATE_COLS = GROUPS_PER_BLOCK * SSM_STATE
EPS = 1e-6
ADAM_LR = 0.001
ADAM_B1 = 0.9
ADAM_B2 = 0.999
ADAM_EPS = 1e-08
ADAM_WD = 0.01
ADAM_STEP = 10
N_CHIPS = 4
LANES = 128
SUBLANES = 8
VMEM_LIMIT_BYTES = 56 * 1024 * 1024
NEG_BIG = -1e30
MESH_AXES = ("x", "y", "c")


def _params(*sem):
    return pltpu.CompilerParams(dimension_semantics=sem if sem else None,
                                vmem_limit_bytes=VMEM_LIMIT_BYTES)


def _sigmoid(x):
    return 1.0 / (1.0 + jnp.exp(-x))


def _gelu(x):
    c = math.sqrt(2.0 / math.pi)
    return 0.5 * x * (1.0 + jnp.tanh(c * (x + 0.044715 * (x * x * x))))


def _gelu_grad(x):
    c = math.sqrt(2.0 / math.pi)
    t = jnp.tanh(c * (x + 0.044715 * (x * x * x)))
    return 0.5 * (1.0 + t) + 0.5 * x * (1.0 - t * t) * (c * (1.0 + 3.0 * 0.044715 * (x * x)))


def _silu_and_grad(z):
    s = _sigmoid(z)
    return z * s, s * (1.0 + z * (1.0 - s))


_TILE_CHOICES = (4096, 3072, 2048, 1536, 1024, 768, 512, 384, 256, LANES)


def _tile(n, cap):
    return next(c for c in _TILE_CHOICES if c <= cap and n % c == 0)


def _mm(a, b, *, name, ta=False, tb=False, out_dtype=F32, shards=1, tm=1024, tn=1024, tk=4096):
    if ta:
        K, M = a.shape
    else:
        M, K = a.shape
    if tb:
        N, kb = b.shape
    else:
        kb, N = b.shape
    assert K == kb, (a.shape, b.shape)
    ns = N // shards
    tm, tn, tk = _tile(M, tm), _tile(ns, tn), _tile(K, tk)
    assert M % tm == 0 and ns % tn == 0 and K % tk == 0 and N % shards == 0
    nk = K // tk
    dn = (((0 if ta else 1,), (1 if tb else 0,)), ((), ()))

    def body(a_ref, b_ref, o_ref, *acc):
        prod = lax.dot_general(a_ref[...].astype(BF16), b_ref[...].astype(BF16), dn, preferred_element_type=F32)
        if nk == 1:
            o_ref[...] = prod.astype(o_ref.dtype)
            return
        acc_ref, = acc
        k = pl.program_id(2)

        @pl.when(k == 0)
        def _():
            acc_ref[...] = jnp.zeros_like(acc_ref)

        acc_ref[...] += prod

        @pl.when(k == nk - 1)
        def _():
            o_ref[...] = acc_ref[...].astype(o_ref.dtype)

    a_spec = (pl.BlockSpec((tk, tm), lambda i, j, k: (k, i)) if ta
              else pl.BlockSpec((tm, tk), lambda i, j, k: (i, k)))
    b_spec = (pl.BlockSpec((tn, tk), lambda i, j, k: (j, k)) if tb
              else pl.BlockSpec((tk, tn), lambda i, j, k: (k, j)))
    if shards == 1:
        out_shape = jax.ShapeDtypeStruct((M, N), out_dtype)
        o_spec = pl.BlockSpec((tm, tn), lambda i, j, k: (i, j))
    else:
        nb = ns // tn
        out_shape = jax.ShapeDtypeStruct((shards, M, ns), out_dtype)
        o_spec = pl.BlockSpec((None, tm, tn), lambda i, j, k: (j // nb, i, j % nb))
    return pl.pallas_call(
        body, name=name, out_shape=out_shape,
        grid=(M // tm, N // tn, nk),
        in_specs=[a_spec, b_spec], out_specs=o_spec,
        scratch_shapes=[] if nk == 1 else [pltpu.VMEM((tm, tn), F32)],
        compiler_params=_params("parallel", "parallel", "arbitrary"),
    )(a, b)


def _rmsnorm_fwd(x, g, *, name, out_dtype=F32, tr=256):
    L, D = x.shape
    tr = min(tr, L)

    def body(x_ref, g_ref, o_ref):
        xf = x_ref[...]
        r = lax.rsqrt(jnp.mean(xf * xf, axis=-1, keepdims=True) + EPS)
        o_ref[...] = (xf * r * g_ref[...]).astype(o_ref.dtype)

    row = pl.BlockSpec((tr, D), lambda i: (i, 0))
    vec = pl.BlockSpec((1, D), lambda i: (0, 0))
    return pl.pallas_call(
        body, name=name, out_shape=jax.ShapeDtypeStruct((L, D), out_dtype),
        grid=(L // tr,), in_specs=[row, vec], out_specs=row,
        compiler_params=_params("parallel"),
    )(x, g.reshape(1, D))


def _post_norm_and_next_norms(o, g_post, res, g_kv, g_pre, *, name, tr=256):
    L, D = o.shape
    tr = min(tr, L)

    def body(o_ref, gp_ref, r_ref, gk_ref, gn_ref, h_ref, kv_ref, hn_ref):
        of = o_ref[...]
        r = lax.rsqrt(jnp.mean(of * of, axis=-1, keepdims=True) + EPS)
        h = r_ref[...] + of * r * gp_ref[...]
        h_ref[...] = h
        hr = h * lax.rsqrt(jnp.mean(h * h, axis=-1, keepdims=True) + EPS)
        kv_ref[...] = (hr * gk_ref[...]).astype(kv_ref.dtype)
        hn_ref[...] = (hr * gn_ref[...]).astype(hn_ref.dtype)

    row = pl.BlockSpec((tr, D), lambda i: (i, 0))
    vec = pl.BlockSpec((1, D), lambda i: (0, 0))
    return pl.pallas_call(
        body, name=name,
        out_shape=(jax.ShapeDtypeStruct((L, D), F32), jax.ShapeDtypeStruct((L, D), BF16),
                   jax.ShapeDtypeStruct((L, D), BF16)),
        grid=(L // tr,), in_specs=[row, vec, row, vec, vec], out_specs=(row, row, row),
        compiler_params=_params("parallel"),
    )(o, g_post.reshape(1, D), res, g_kv.reshape(1, D), g_pre.reshape(1, D))


def _rmsnorm_bwd(x, g, dy, *, name, adds=(), dx_dtype=F32, tr=256):
    L, D = x.shape
    tr = min(tr, L)
    dys = dy if isinstance(dy, tuple) else (dy,)
    n_dy, n_add = len(dys), len(adds)

    def body(*refs):
        x_ref, g_ref = refs[:2]
        dy_refs = refs[2:2 + n_dy]
        add_refs = refs[2 + n_dy:2 + n_dy + n_add]
        dx_ref, dg_ref = refs[2 + n_dy + n_add:]
        xf = x_ref[...]
        dyf = dy_refs[0][...].astype(F32)
        for d_ref in dy_refs[1:]:
            dyf = dyf + d_ref[...].astype(F32)
        r = lax.rsqrt(jnp.mean(xf * xf, axis=-1, keepdims=True) + EPS)
        gy = dyf * g_ref[...]
        c = jnp.mean(xf * gy, axis=-1, keepdims=True) * (r * r * r)
        dx = gy * r - xf * c
        for a_ref in add_refs:
            dx = dx + a_ref[...].astype(F32)
        dx_ref[...] = dx.astype(dx_ref.dtype)

        @pl.when(pl.program_id(0) == 0)
        def _():
            dg_ref[...] = jnp.zeros_like(dg_ref)

        dg_ref[...] += jnp.sum(dyf * xf * r, axis=0, keepdims=True)

    row = pl.BlockSpec((tr, D), lambda i: (i, 0))
    vec = pl.BlockSpec((1, D), lambda i: (0, 0))
    dx, dg = pl.pallas_call(
        body, name=name,
        out_shape=(jax.ShapeDtypeStruct((L, D), dx_dtype), jax.ShapeDtypeStruct((1, D), F32)),
        grid=(L // tr,), in_specs=[row, vec] + [row] * (n_dy + n_add), out_specs=(row, vec),
        compiler_params=_params("arbitrary"),
    )(x, g.reshape(1, D), *dys, *adds)
    return dx, dg.reshape(D)


def _rmsnorm_bwd_pair(x, g1, dy1, g2, dy2, *, name, adds=(), tr=256):
    L, D = x.shape
    tr = min(tr, L)
    dy1s = dy1 if isinstance(dy1, tuple) else (dy1,)
    n1, n_add = len(dy1s), len(adds)

    def body(*refs):
        x_ref, g1_ref, g2_ref = refs[:3]
        dy1_refs = refs[3:3 + n1]
        dy2_ref = refs[3 + n1]
        add_refs = refs[4 + n1:4 + n1 + n_add]
        dx_ref, dg1_ref, dg2_ref = refs[4 + n1 + n_add:]
        xf = x_ref[...]
        d1 = dy1_refs[0][...].astype(F32)
        for d_ref in dy1_refs[1:]:
            d1 = d1 + d_ref[...].astype(F32)
        d2 = dy2_ref[...].astype(F32)
        r = lax.rsqrt(jnp.mean(xf * xf, axis=-1, keepdims=True) + EPS)
        gy = d1 * g1_ref[...] + d2 * g2_ref[...]
        c = jnp.mean(xf * gy, axis=-1, keepdims=True) * (r * r * r)
        dx = gy * r - xf * c
        for a_ref in add_refs:
            dx = dx + a_ref[...].astype(F32)
        dx_ref[...] = dx

        @pl.when(pl.program_id(0) == 0)
        def _():
            dg1_ref[...] = jnp.zeros_like(dg1_ref)
            dg2_ref[...] = jnp.zeros_like(dg2_ref)

        xr = xf * r
        dg1_ref[...] += jnp.sum(d1 * xr, axis=0, keepdims=True)
        dg2_ref[...] += jnp.sum(d2 * xr, axis=0, keepdims=True)

    row = pl.BlockSpec((tr, D), lambda i: (i, 0))
    vec = pl.BlockSpec((1, D), lambda i: (0, 0))
    dx, dg1, dg2 = pl.pallas_call(
        body, name=name,
        out_shape=(jax.ShapeDtypeStruct((L, D), F32), jax.ShapeDtypeStruct((1, D), F32),
                   jax.ShapeDtypeStruct((1, D), F32)),
        grid=(L // tr,), in_specs=[row, vec, vec] + [row] * (n1 + 1 + n_add), out_specs=(row, vec, vec),
        compiler_params=_params("arbitrary"),
    )(x, g1.reshape(1, D), g2.reshape(1, D), *dy1s, dy2, *adds)
    return dx, dg1.reshape(D), dg2.reshape(D)


def _final_norm_loss(o, g, res, target, *, tr=256):
    L, D = o.shape
    tr = min(tr, L)

    def body(o_ref, g_ref, r_ref, t_ref, dh_ref, loss_ref):
        xf = o_ref[...]
        r = lax.rsqrt(jnp.mean(xf * xf, axis=-1, keepdims=True) + EPS)
        e = (r_ref[...] + xf * r * g_ref[...]) - t_ref[...]
        dh_ref[...] = e * (1.0 / D)

        @pl.when(pl.program_id(0) == 0)
        def _():
            loss_ref[...] = jnp.zeros_like(loss_ref)

        loss_ref[...] += jnp.sum(e * e, axis=0, keepdims=True) * (0.5 / D)

    row = pl.BlockSpec((tr, D), lambda i: (i, 0))
    vec = pl.BlockSpec((1, D), lambda i: (0, 0))
    dh, lp = pl.pallas_call(
        body, name="post_norm_1_loss",
        out_shape=(jax.ShapeDtypeStruct((L, D), F32), jax.ShapeDtypeStruct((1, D), F32)),
        grid=(L // tr,), in_specs=[row, vec, row, row], out_specs=(row, vec),
        compiler_params=_params("arbitrary"),
    )(o, g.reshape(1, D), res, target)
    return dh, lp


def _s5_coeffs(lr, li, ls):
    dt = jnp.exp(ls)
    mag = jnp.exp(lr * dt)
    ar = mag * jnp.cos(li * dt)
    ai = mag * jnp.sin(li * dt)
    den = lr * lr + li * li
    cr = ((ar - 1.0) * lr + ai * li) / den
    ci = (ai * lr - (ar - 1.0) * li) / den
    return dt, ar, ai, den, cr, ci


def _s5_prep(lam_re, lam_im, log_step, b_re_t, b_im_t):
    G, P = lam_re.shape
    H = b_re_t.shape[1]

    def body(lr_ref, li_ref, ls_ref, br_ref, bi_ref, ar_ref, ai_ref, bbr_ref, bbi_ref):
        _, ar, ai, _, cr, ci = _s5_coeffs(lr_ref[...], li_ref[...], ls_ref[...])
        ar_ref[...] = ar
        ai_ref[...] = ai
        br, bi = br_ref[...], bi_ref[...]
        crb, cib = cr[:, None, :], ci[:, None, :]
        bbr_ref[...] = crb * br - cib * bi
        bbi_ref[...] = crb * bi + cib * br

    return pl.pallas_call(
        body, name="s5_prep",
        out_shape=(jax.ShapeDtypeStruct((G, P), F32), jax.ShapeDtypeStruct((G, P), F32),
                   jax.ShapeDtypeStruct((G, H, P), F32), jax.ShapeDtypeStruct((G, H, P), F32)),
        compiler_params=_params(),
    )(lam_re, lam_im, log_step.reshape(G, 1), b_re_t, b_im_t)


def _s5_prep_bwd(lam_re, lam_im, log_step, b_re_t, b_im_t, d_ar, d_ai, d_bbr, d_bbi):
    G, P = lam_re.shape
    H = b_re_t.shape[1]

    def body(lr_ref, li_ref, ls_ref, br_ref, bi_ref, dar_ref, dai_ref, dbbr_ref, dbbi_ref,
             dlr_ref, dli_ref, dls_ref, dbr_ref, dbi_ref):
        lr, li = lr_ref[...], li_ref[...]
        dt, ar, ai, den, cr, ci = _s5_coeffs(lr, li, ls_ref[...])
        br, bi = br_ref[...], bi_ref[...]
        gbr, gbi = dbbr_ref[...], dbbi_ref[...]
        crb, cib = cr[:, None, :], ci[:, None, :]
        dbr_ref[...] = crb * gbr + cib * gbi
        dbi_ref[...] = crb * gbi - cib * gbr
        gcr = jnp.sum(br * gbr + bi * gbi, axis=1)
        gci = jnp.sum(br * gbi - bi * gbr, axis=1)
        ilr, ili = lr / den, -li / den
        gar = dar_ref[...] + (ilr * gcr + ili * gci)
        gai = dai_ref[...] + (ilr * gci - ili * gcr)
        qr, qi = cr * ilr - ci * ili, cr * ili + ci * ilr
        glr = -(qr * gcr + qi * gci)
        gli = -(qr * gci - qi * gcr)
        glr = glr + dt * (ar * gar + ai * gai)
        gli = gli + dt * (ar * gai - ai * gar)
        wr, wi = lr * ar - li * ai, lr * ai + li * ar
        gdt = jnp.sum(wr * gar + wi * gai, axis=1, keepdims=True)
        dlr_ref[...] = glr
        dli_ref[...] = gli
        dls_ref[...] = gdt * dt

    return pl.pallas_call(
        body, name="s5_prep_bwd",
        out_shape=(jax.ShapeDtypeStruct((G, P), F32), jax.ShapeDtypeStruct((G, P), F32),
                   jax.ShapeDtypeStruct((G, 1), F32),
                   jax.ShapeDtypeStruct((G, H, P), F32), jax.ShapeDtypeStruct((G, H, P), F32)),
        compiler_params=_params(),
    )(lam_re, lam_im, log_step.reshape(G, 1), b_re_t, b_im_t, d_ar, d_ai, d_bbr, d_bbi)


def _s5_block_mats(bbr_t, bbi_t, c_re, c_im):
    bmat = _s5_expand(bbr_t, bbi_t)
    cmat = jnp.transpose(_s5_expand(c_re, -c_im), (0, 2, 1))
    return bmat.astype(BF16), cmat.astype(BF16)


def _s5_diag_mask():
    r = lax.broadcasted_iota(jnp.int32, (LANES, 2 * STATE_COLS), 0) // SSM_GROUP
    c = (lax.broadcasted_iota(jnp.int32, (LANES, 2 * STATE_COLS), 1) % STATE_COLS) // SSM_STATE
    return (r == c).astype(F32)


def _s5_expand(re, im):
    re = jnp.tile(re.reshape(SSM_BLOCKS, LANES, SSM_STATE), (1, 1, GROUPS_PER_BLOCK))
    im = jnp.tile(im.reshape(SSM_BLOCKS, LANES, SSM_STATE), (1, 1, GROUPS_PER_BLOCK))
    return jnp.concatenate([re, im], axis=-1) * _s5_diag_mask()[None]


def _s5_unfold(dmat):
    d = dmat.reshape(SSM_GROUPS, SSM_GROUP, 2, SSM_STATE)
    return jnp.transpose(d, (2, 0, 1, 3))


def _s5_a_rows(ar, ai):
    a = jnp.concatenate([ar.reshape(SSM_BLOCKS, STATE_COLS), ai.reshape(SSM_BLOCKS, STATE_COLS)], axis=1)
    return jnp.broadcast_to(a[:, None, :], (SSM_BLOCKS, SUBLANES, 2 * STATE_COLS))


def _to_step_major(src_ref, dst_ref, seg):
    for s in range(SUBLANES):
        dst_ref[pl.ds(s, seg, stride=SUBLANES), :] = src_ref[pl.ds(seg * s, seg), :]


def _segment_rows(ref, s, seg):
    return ref[pl.ds(s, seg, stride=SUBLANES), :]


def _cmul(ar, ai, xr, xi):
    return ar * xr - ai * xi, ar * xi + ai * xr


def _s5_tables(a_ref, pw_s, pwr_s, S, seg):
    ar, ai = a_ref[:, :S], a_ref[:, S:]

    def step(i, c):
        pr, pi = c
        pw_s[i, :, :S] = pr
        pw_s[i, :, S:] = pi
        nr, ni = _cmul(ar, ai, pr, pi)
        pwr_s[seg - 1 - i, :, :S] = nr
        pwr_s[seg - 1 - i, :, S:] = ni
        return nr, ni

    pr, pi = lax.fori_loop(0, seg, step, (jnp.ones_like(ar), jnp.zeros_like(ai)))
    pw_s[seg, :, :S] = pr
    pw_s[seg, :, S:] = pi


def _s5_fwd(proj, bmat, cmat, a_rows, d_skip, *, tc=512):
    L = proj.shape[0]
    tc = min(tc, L)
    nt = L // tc
    seg = tc // SUBLANES
    S = STATE_COLS

    def body(u_ref, b_ref, c_ref, a_ref, d_ref, y_ref, yg_ref, xp_ref,
             bu_s, xp_s, pw_s, pwr_s, carry_s, e_s, up_s, yc_s):
        @pl.when(pl.program_id(1) == 0)
        def _():
            carry_s[...] = jnp.zeros_like(carry_s)
            _s5_tables(a_ref, pw_s, pwr_s, S, seg)

        ar, ai = a_ref[:, :S], a_ref[:, S:]
        _to_step_major(u_ref, up_s, seg)
        bu = jnp.dot(up_s[...].astype(BF16), b_ref[...], preferred_element_type=F32)
        bu_s[...] = bu.reshape(seg, SUBLANES, 2 * S)

        def step(i, carry):
            cr, ci = carry
            xp_s[i, :, :S] = cr
            xp_s[i, :, S:] = ci
            return ar * cr - ai * ci + bu_s[i, :, :S], ar * ci + ai * cr + bu_s[i, :, S:]

        zero = jnp.zeros((SUBLANES, S), F32)
        fr, fi = lax.fori_loop(0, seg, step, (zero, zero))
        pr, pi = pw_s[seg, 0:1, :S], pw_s[seg, 0:1, S:]
        er, ei = carry_s[0:1, :S], carry_s[0:1, S:]
        for s in range(SUBLANES):
            e_s[s:s + 1, :S] = er
            e_s[s:s + 1, S:] = ei
            tr, ti = _cmul(pr, pi, er, ei)
            er, ei = fr[s:s + 1] + tr, fi[s:s + 1] + ti
        carry_s[0:1, :S] = er
        carry_s[0:1, S:] = ei
        pw = pw_s[0:seg]
        tr, ti = _cmul(pw[:, :, :S], pw[:, :, S:], e_s[:, :S][None], e_s[:, S:][None])
        xl = xp_s[...]
        xp = jnp.concatenate([xl[:, :, :S] + tr, xl[:, :, S:] + ti], axis=-1).reshape(tc, 2 * S)
        xp_ref[...] = xp
        a1r, a1i = ar[0:1], ai[0:1]
        x_re = a1r * xp[:, :S] - a1i * xp[:, S:] + bu[:, :S]
        x_im = a1r * xp[:, S:] + a1i * xp[:, :S] + bu[:, S:]
        xs = jnp.concatenate([x_re, x_im], axis=1).astype(BF16)
        yc_s[...] = jnp.dot(xs, c_ref[...], preferred_element_type=F32)
        for s in range(SUBLANES):
            rows = pl.ds(seg * s, seg)
            y = _segment_rows(yc_s, s, seg) + d_ref[...] * u_ref[rows, :]
            y_ref[rows, :] = y
            yg_ref[rows, :] = _gelu(y).astype(BF16)

    return pl.pallas_call(
        body, name="s5_fwd",
        out_shape=(jax.ShapeDtypeStruct((L, MAIN_WIDTH), F32),
                   jax.ShapeDtypeStruct((L, MAIN_WIDTH), BF16),
                   jax.ShapeDtypeStruct((L, SSM_BLOCKS * 2 * S), F32)),
        grid=(SSM_BLOCKS, nt),
        in_specs=[pl.BlockSpec((tc, LANES), lambda b, t: (t, b)),
                  pl.BlockSpec((None, LANES, 2 * S), lambda b, t: (b, 0, 0)),
                  pl.BlockSpec((None, 2 * S, LANES), lambda b, t: (b, 0, 0)),
                  pl.BlockSpec((None, SUBLANES, 2 * S), lambda b, t: (b, 0, 0)),
                  pl.BlockSpec((1, LANES), lambda b, t: (0, b))],
        out_specs=(pl.BlockSpec((tc, LANES), lambda b, t: (t, b)),
                   pl.BlockSpec((tc, LANES), lambda b, t: (t, b)),
                   pl.BlockSpec((tc, 2 * S), lambda b, t: (t, b))),
        scratch_shapes=[pltpu.VMEM((seg, SUBLANES, 2 * S), F32),
                        pltpu.VMEM((seg, SUBLANES, 2 * S), F32),
                        pltpu.VMEM((seg + 1, SUBLANES, 2 * S), F32),
                        pltpu.VMEM((seg, SUBLANES, 2 * S), F32),
                        pltpu.VMEM((SUBLANES, 2 * S), F32),
                        pltpu.VMEM((SUBLANES, 2 * S), F32),
                        pltpu.VMEM((tc, LANES), F32),
                        pltpu.VMEM((tc, LANES), F32)],
        compiler_params=_params("parallel", "arbitrary"),
    )(proj, bmat, cmat, a_rows, d_skip.reshape(1, MAIN_WIDTH))


def _s5_bwd(proj, dyg_a, dyg_b, y, xp, bmat, cmat, a_rows, d_skip, dproj, *, tc=512):
    L = proj.shape[0]
    tc = min(tc, L)
    nt = L // tc
    seg = tc // SUBLANES
    S = STATE_COLS
    nn = (((1,), (1,)), ((), ()))
    tn = (((0,), (0,)), ((), ()))

    def fold_diagonal(acc_ref, mask_ref, fold_ref):
        x = acc_ref[...] * mask_ref[...]
        hi = x.astype(BF16)
        rest = x - hi.astype(F32)
        mid = rest.astype(BF16)
        low = (rest - mid.astype(F32)).astype(BF16)
        return sum(jnp.dot(piece, fold_ref[...], preferred_element_type=F32) for piece in (hi, mid, low))

    def body(u_ref, dyga_ref, dygb_ref, y_ref, xp_ref, b_ref, c_ref, a_ref, d_ref, mask_ref, fold_ref, dp_hbm,
             du_ref, dbd_ref, dcd_ref, da_ref, dd_ref,
             dl_s, pw_s, pwr_s, carry_s, e_s, up_s, dy_s, dyp_s, dup_s, db_ref, dc_ref):
        @pl.when(pl.program_id(1) == 0)
        def _():
            carry_s[...] = jnp.zeros_like(carry_s)
            db_ref[...] = jnp.zeros_like(db_ref)
            dc_ref[...] = jnp.zeros_like(dc_ref)
            da_ref[...] = jnp.zeros_like(da_ref)
            dd_ref[...] = jnp.zeros_like(dd_ref)
            _s5_tables(a_ref, pw_s, pwr_s, S, seg)

        ar, ai = a_ref[:, :S], a_ref[:, S:]
        a1r, a1i = ar[0:1], ai[0:1]
        u = u_ref[...]
        dy = (dyga_ref[...] + dygb_ref[...]) * _gelu_grad(y_ref[...])
        dy_s[...] = dy
        xp = xp_ref[...]
        _to_step_major(u_ref, up_s, seg)
        _to_step_major(dy_s, dyp_s, seg)
        ubp = up_s[...].astype(BF16)
        dyp = dyp_s[...].astype(BF16)
        bu = jnp.dot(ubp, b_ref[...], preferred_element_type=F32)
        x_re = a1r * xp[:, :S] - a1i * xp[:, S:] + bu[:, :S]
        x_im = a1r * xp[:, S:] + a1i * xp[:, :S] + bu[:, S:]
        xs = jnp.concatenate([x_re, x_im], axis=1).astype(BF16)
        dc_ref[...] += lax.dot_general(dyp, xs, tn, preferred_element_type=F32)
        dx = lax.dot_general(dyp, c_ref[...], nn, preferred_element_type=F32)
        dl_s[...] = dx.reshape(seg, SUBLANES, 2 * S)

        def step(k, carry):
            cr, ci = carry
            i = seg - 1 - k
            lr = dl_s[i, :, :S] + (ar * cr + ai * ci)
            li = dl_s[i, :, S:] + (ar * ci - ai * cr)
            dl_s[i, :, :S] = lr
            dl_s[i, :, S:] = li
            return lr, li

        zero = jnp.zeros((SUBLANES, S), F32)
        fr, fi = lax.fori_loop(0, seg, step, (zero, zero))
        pr, pi = pw_s[seg, 0:1, :S], pw_s[seg, 0:1, S:]
        er, ei = carry_s[0:1, :S], carry_s[0:1, S:]
        for s in range(SUBLANES - 1, -1, -1):
            e_s[s:s + 1, :S] = er
            e_s[s:s + 1, S:] = ei
            er, ei = fr[s:s + 1] + (pr * er + pi * ei), fi[s:s + 1] + (pr * ei - pi * er)
        carry_s[0:1, :S] = er
        carry_s[0:1, S:] = ei
        er, ei = e_s[:, :S][None], e_s[:, S:][None]
        pw = pwr_s[...]
        pwr, pwi = pw[:, :, :S], pw[:, :, S:]
        ll = dl_s[...]
        lam = jnp.concatenate([ll[:, :, :S] + (pwr * er + pwi * ei), ll[:, :, S:] + (pwr * ei - pwi * er)],
                              axis=-1).reshape(tc, 2 * S)
        l_re, l_im = lam[:, :S], lam[:, S:]
        da_ref[0:1, :S] += jnp.sum(l_re * xp[:, :S] + l_im * xp[:, S:], axis=0, keepdims=True)
        da_ref[0:1, S:] += jnp.sum(l_im * xp[:, :S] - l_re * xp[:, S:], axis=0, keepdims=True)
        lamb = lam.astype(BF16)
        dup_s[...] = lax.dot_general(lamb, b_ref[...], nn, preferred_element_type=F32)
        for s in range(SUBLANES):
            rows = pl.ds(seg * s, seg)
            du = _segment_rows(dup_s, s, seg) + d_ref[...] * dy_s[rows, :]
            du_ref[rows, :] = du.astype(du_ref.dtype)
        db_ref[...] += lax.dot_general(ubp, lamb, tn, preferred_element_type=F32)
        dd_ref[0:1, :] += jnp.sum(dy * u, axis=0, keepdims=True)

        @pl.when(pl.program_id(1) == nt - 1)
        def _():
            dbd_ref[...] = fold_diagonal(db_ref, mask_ref, fold_ref)
            dcd_ref[...] = fold_diagonal(dc_ref, mask_ref, fold_ref)

    rev = lambda b, t: (nt - 1 - t, b)
    col = jnp.arange(2 * S)
    fold = ((col // S * SSM_STATE + col % SSM_STATE)[:, None] == jnp.arange(LANES)[None, :]).astype(BF16)
    return pl.pallas_call(
        body, name="s5_bwd",
        out_shape=(jax.ShapeDtypeStruct(dproj.shape, dproj.dtype),
                   jax.ShapeDtypeStruct((SSM_BLOCKS, LANES, LANES), F32),
                   jax.ShapeDtypeStruct((SSM_BLOCKS, LANES, LANES), F32),
                   jax.ShapeDtypeStruct((SSM_BLOCKS, SUBLANES, 2 * S), F32),
                   jax.ShapeDtypeStruct((SUBLANES, MAIN_WIDTH), F32)),
        input_output_aliases={11: 0},
        grid=(SSM_BLOCKS, nt),
        in_specs=[pl.BlockSpec((tc, LANES), rev),
                  pl.BlockSpec((tc, LANES), rev),
                  pl.BlockSpec((tc, LANES), rev),
                  pl.BlockSpec((tc, LANES), rev),
                  pl.BlockSpec((tc, 2 * S), rev),
                  pl.BlockSpec((None, LANES, 2 * S), lambda b, t: (b, 0, 0)),
                  pl.BlockSpec((None, 2 * S, LANES), lambda b, t: (b, 0, 0)),
                  pl.BlockSpec((None, SUBLANES, 2 * S), lambda b, t: (b, 0, 0)),
                  pl.BlockSpec((1, LANES), lambda b, t: (0, b)),
                  pl.BlockSpec((LANES, 2 * S), lambda b, t: (0, 0)),
                  pl.BlockSpec((2 * S, LANES), lambda b, t: (0, 0)),
                  _ANY],
        out_specs=(pl.BlockSpec((tc, LANES), rev),
                   pl.BlockSpec((None, LANES, LANES), lambda b, t: (b, 0, 0)),
                   pl.BlockSpec((None, LANES, LANES), lambda b, t: (b, 0, 0)),
                   pl.BlockSpec((None, SUBLANES, 2 * S), lambda b, t: (b, 0, 0)),
                   pl.BlockSpec((SUBLANES, LANES), lambda b, t: (0, b))),
        scratch_shapes=[pltpu.VMEM((seg, SUBLANES, 2 * S), F32),
                        pltpu.VMEM((seg + 1, SUBLANES, 2 * S), F32),
                        pltpu.VMEM((seg, SUBLANES, 2 * S), F32),
                        pltpu.VMEM((SUBLANES, 2 * S), F32),
                        pltpu.VMEM((SUBLANES, 2 * S), F32),
                        pltpu.VMEM((tc, LANES), F32),
                        pltpu.VMEM((tc, LANES), F32),
                        pltpu.VMEM((tc, LANES), F32),
                        pltpu.VMEM((tc, LANES), F32),
                        pltpu.VMEM((LANES, 2 * S), F32),
                        pltpu.VMEM((LANES, 2 * S), F32)],
        compiler_params=_params("parallel", "arbitrary"),
    )(proj, dyg_a, dyg_b, y, xp, bmat, cmat, a_rows, d_skip.reshape(1, MAIN_WIDTH), _s5_diag_mask(), fold, dproj)


_Z_COLS = slice(MAIN_WIDTH, 2 * MAIN_WIDTH)
_ZM_COLS = slice(2 * MAIN_WIDTH + MEM_WIDTH, IN_WIDTH)


def _proj_rows(tr):
    return pl.BlockSpec((tr, IN_WIDTH), lambda i: (i, 0))


def _row_specs(tr):
    main = pl.BlockSpec((tr, MAIN_WIDTH), lambda i: (i, 0))
    z = pl.BlockSpec((tr, MAIN_WIDTH), lambda i: (i, 1))
    zm = pl.BlockSpec((tr, MEM_WIDTH), lambda i: (i, IN_WIDTH // MEM_WIDTH - 1))
    mem = pl.BlockSpec((tr, MEM_WIDTH), lambda i: (i, 0))
    cat = pl.BlockSpec((tr, D_MODEL), lambda i: (i, 0))
    vec = pl.BlockSpec((1, MAIN_WIDTH), lambda i: (0, 0))
    return main, z, zm, mem, cat, vec


def _gate_a_fwd(y, t, b_glu, proj, o_mem, *, tr=256):
    L = y.shape[0]
    tr = min(tr, L)

    def body(y_ref, t_ref, b_ref, z_ref, zm_ref, om_ref, o_ref):
        yg = _gelu(y_ref[...])
        sz, _ = _silu_and_grad(z_ref[...])
        o_ref[:, :MAIN_WIDTH] = (yg * _sigmoid(t_ref[...] + b_ref[...]) * sz).astype(BF16)
        szm, _ = _silu_and_grad(zm_ref[...])
        o_ref[:, MAIN_WIDTH:] = (om_ref[...] * szm).astype(BF16)

    main, z, zm, mem, cat, vec = _row_specs(tr)
    return pl.pallas_call(
        body, name="gate_a_fwd", out_shape=jax.ShapeDtypeStruct((L, D_MODEL), BF16),
        grid=(L // tr,), in_specs=[main, main, vec, z, zm, mem], out_specs=cat,
        compiler_params=_params("parallel"),
    )(y, t, b_glu.reshape(1, MAIN_WIDTH), proj, proj, o_mem)


def _gate_a_bwd(dcat, y, t, b_glu, proj, o_mem, *, tr=256):
    L = y.shape[0]
    tr = min(tr, L)

    def body(dc_ref, y_ref, t_ref, b_ref, z_ref, zm_ref, om_ref,
             dp_ref, dt_ref, dyg_ref, dom_ref, db_ref):
        dmain = dc_ref[:, :MAIN_WIDTH]
        dmemo = dc_ref[:, MAIN_WIDTH:]
        yg = _gelu(y_ref[...])
        sg = _sigmoid(t_ref[...] + b_ref[...])
        sz, gz = _silu_and_grad(z_ref[...])
        dp_ref[:, _Z_COLS] = (dmain * (yg * sg) * gz).astype(BF16)
        dy2 = dmain * sz
        dyg_ref[...] = dy2 * sg
        dt = dy2 * yg * (sg * (1.0 - sg))
        dt_ref[...] = dt.astype(BF16)

        @pl.when(pl.program_id(0) == 0)
        def _():
            db_ref[...] = jnp.zeros_like(db_ref)

        db_ref[...] += jnp.sum(dt, axis=0, keepdims=True)
        szm, gzm = _silu_and_grad(zm_ref[...])
        dom_ref[...] = dmemo * szm
        dp_ref[:, _ZM_COLS] = (dmemo * om_ref[...] * gzm).astype(BF16)

    main, z, zm, mem, cat, vec = _row_specs(tr)
    outs = pl.pallas_call(
        body, name="gate_a_bwd",
        out_shape=(jax.ShapeDtypeStruct((L, IN_WIDTH), BF16),
                   jax.ShapeDtypeStruct((L, MAIN_WIDTH), BF16), jax.ShapeDtypeStruct((L, MAIN_WIDTH), F32),
                   jax.ShapeDtypeStruct((L, MEM_WIDTH), F32), jax.ShapeDtypeStruct((1, MAIN_WIDTH), F32)),
        grid=(L // tr,), in_specs=[cat, main, main, vec, z, zm, mem],
        out_specs=(_proj_rows(tr), main, main, mem, vec),
        compiler_params=_params("arbitrary"),
    )(dcat, y, t, b_glu.reshape(1, MAIN_WIDTH), proj, proj, o_mem)
    return outs


def _gate_b_fwd(att, proj, o_mem, *, tr=256):
    L = att.shape[0]
    tr = min(tr, L)

    def body(a_ref, z_ref, zm_ref, om_ref, o_ref):
        sz, _ = _silu_and_grad(z_ref[...])
        o_ref[:, :MAIN_WIDTH] = (a_ref[...] * sz).astype(BF16)
        szm, _ = _silu_and_grad(zm_ref[...])
        o_ref[:, MAIN_WIDTH:] = (om_ref[...] * szm).astype(BF16)

    main, z, zm, mem, cat, _ = _row_specs(tr)
    return pl.pallas_call(
        body, name="gate_b_fwd", out_shape=jax.ShapeDtypeStruct((L, D_MODEL), BF16),
        grid=(L // tr,), in_specs=[main, z, zm, mem], out_specs=cat,
        compiler_params=_params("parallel"),
    )(att, proj, proj, o_mem)


def _gate_b_bwd(dcat, att, proj, o_mem, *, tr=256):
    L = att.shape[0]
    tr = min(tr, L)

    def body(dc_ref, a_ref, z_ref, zm_ref, om_ref, da_ref, dp_ref, dom_ref, dl_ref):
        dmain = dc_ref[:, :MAIN_WIDTH]
        dmemo = dc_ref[:, MAIN_WIDTH:]
        att = a_ref[...]
        sz, gz = _silu_and_grad(z_ref[...])
        datt = dmain * sz
        da_ref[...] = datt
        dp_ref[:, _Z_COLS] = (dmain * att * gz).astype(BF16)
        szm, gzm = _silu_and_grad(zm_ref[...])
        dom_ref[...] = dmemo * szm
        dp_ref[:, _ZM_COLS] = (dmemo * om_ref[...] * gzm).astype(BF16)
        prod = datt * att
        for h in range(FOX_HEADS):
            dl_ref[h] = jnp.sum(prod[:, h * HEAD_DIM:(h + 1) * HEAD_DIM], axis=1, keepdims=True)

    main, z, zm, mem, cat, _ = _row_specs(tr)
    delta = pl.BlockSpec((FOX_HEADS, tr, 1), lambda i: (0, i, 0))
    return pl.pallas_call(
        body, name="gate_b_bwd",
        out_shape=(jax.ShapeDtypeStruct((L, MAIN_WIDTH), F32), jax.ShapeDtypeStruct((L, IN_WIDTH), BF16),
                   jax.ShapeDtypeStruct((L, MEM_WIDTH), F32), jax.ShapeDtypeStruct((FOX_HEADS, L, 1), F32)),
        grid=(L // tr,), in_specs=[cat, main, z, zm, mem], out_specs=(main, _proj_rows(tr), mem, delta),
        compiler_params=_params("parallel"),
    )(dcat, att, proj, proj, o_mem)


_MEM_Q_COL = (2 * MAIN_WIDTH) // HEAD_DIM
_NT = (((1,), (1,)), ((), ()))
_TN = (((0,), (0,)), ((), ()))


def _mem_probs(q_ref, k_ref):
    qs = (q_ref[...] * (HEAD_DIM ** -0.5)).astype(BF16)
    s = lax.dot_general(qs, k_ref[...].astype(BF16), _NT, preferred_element_type=F32)
    e = jnp.exp(s - jnp.max(s, axis=-1, keepdims=True))
    return qs, e / jnp.sum(e, axis=-1, keepdims=True)


def _mem_attn_fwd(proj, kvm, *, tq=2048):
    L = proj.shape[0]
    tq = min(tq, L)

    def body(q_ref, k_ref, v_ref, o_ref):
        _, p = _mem_probs(q_ref, k_ref)
        o_ref[...] = jnp.dot(p.astype(BF16), v_ref[...].astype(BF16), preferred_element_type=F32)

    return pl.pallas_call(
        body, name="mem_attn_fwd", out_shape=jax.ShapeDtypeStruct((L, MEM_WIDTH), F32),
        grid=(MEM_HEADS, L // tq),
        in_specs=[pl.BlockSpec((tq, HEAD_DIM), lambda h, i: (i, _MEM_Q_COL + h)),
                  pl.BlockSpec((N_MEM, HEAD_DIM), lambda h, i: (0, h)),
                  pl.BlockSpec((N_MEM, HEAD_DIM), lambda h, i: (0, MEM_HEADS + h))],
        out_specs=pl.BlockSpec((tq, HEAD_DIM), lambda h, i: (i, h)),
        compiler_params=_params("parallel", "parallel"),
    )(proj, kvm, kvm)


def _mem_attn_bwd(proj, kvm, do, dproj, *, tq=2048):
    L = proj.shape[0]
    tq = min(tq, L)

    def body(q_ref, k_ref, v_ref, do_ref, dp_hbm, dq_ref, dk_ref, dv_ref):
        @pl.when(pl.program_id(1) == 0)
        def _():
            dk_ref[...] = jnp.zeros_like(dk_ref)
            dv_ref[...] = jnp.zeros_like(dv_ref)

        qs, p = _mem_probs(q_ref, k_ref)
        dob = do_ref[...].astype(BF16)
        dp = lax.dot_general(dob, v_ref[...].astype(BF16), _NT, preferred_element_type=F32)
        ds = p * (dp - jnp.sum(p * dp, axis=-1, keepdims=True))
        dsb = ds.astype(BF16)
        dq = jnp.dot(dsb, k_ref[...].astype(BF16), preferred_element_type=F32) * (HEAD_DIM ** -0.5)
        dq_ref[...] = dq.astype(BF16)
        dk_ref[...] += lax.dot_general(dsb, qs, _TN, preferred_element_type=F32)
        dv_ref[...] += lax.dot_general(p.astype(BF16), dob, _TN, preferred_element_type=F32)

    dproj, dk, dv = pl.pallas_call(
        body, name="mem_attn_bwd",
        out_shape=(jax.ShapeDtypeStruct(dproj.shape, dproj.dtype),
                   jax.ShapeDtypeStruct((N_MEM, MEM_WIDTH), F32),
                   jax.ShapeDtypeStruct((N_MEM, MEM_WIDTH), F32)),
        grid=(MEM_HEADS, L // tq),
        in_specs=[pl.BlockSpec((tq, HEAD_DIM), lambda h, i: (i, _MEM_Q_COL + h)),
                  pl.BlockSpec((N_MEM, HEAD_DIM), lambda h, i: (0, h)),
                  pl.BlockSpec((N_MEM, HEAD_DIM), lambda h, i: (0, MEM_HEADS + h)),
                  pl.BlockSpec((tq, HEAD_DIM), lambda h, i: (i, h)),
                  _ANY],
        out_specs=(pl.BlockSpec((tq, HEAD_DIM), lambda h, i: (i, _MEM_Q_COL + h)),
                   pl.BlockSpec((N_MEM, HEAD_DIM), lambda h, i: (0, h)),
                   pl.BlockSpec((N_MEM, HEAD_DIM), lambda h, i: (0, h))),
        input_output_aliases={4: 0},
        compiler_params=_params("parallel", "arbitrary"),
    )(proj, kvm, kvm, do, dproj)
    return dproj, jnp.concatenate([dk, dv], axis=1)


def _tile_cumsum(x, row, reverse):
    for sh in (1, 2, 4):
        if reverse:
            x = x + jnp.where(row < SUBLANES - sh, pltpu.roll(x, SUBLANES - sh, 0), 0.0)
        else:
            x = x + jnp.where(row >= sh, pltpu.roll(x, sh, 0), 0.0)
    return x


def _fgate_fwd(pre, b_pad):
    L = pre.shape[0]
    n8 = L // SUBLANES

    def body(p_ref, b_ref, o_ref):
        row = lax.broadcasted_iota(jnp.int32, (SUBLANES, LANES), 0)
        b = b_ref[...]

        def step(i, carry):
            x = p_ref[i] + b
            logf = jnp.minimum(x, 0.0) - jnp.log(1.0 + jnp.exp(-jnp.abs(x)))
            t = _tile_cumsum(logf, row, False) + carry
            o_ref[i] = t
            return t[SUBLANES - 1:SUBLANES, :]

        lax.fori_loop(0, n8, step, jnp.zeros((1, LANES), F32))

    out = pl.pallas_call(
        body, name="fgate_fwd", out_shape=jax.ShapeDtypeStruct((n8, SUBLANES, LANES), F32),
        compiler_params=_params(),
    )(pre.reshape(n8, SUBLANES, LANES), b_pad.reshape(1, LANES))
    return out.reshape(L, LANES)


def _fgate_bwd(dfcum, pre, b_pad):
    L = pre.shape[0]
    n8 = L // SUBLANES

    def body(d_ref, p_ref, b_ref, o_ref, s_ref):
        row = lax.broadcasted_iota(jnp.int32, (SUBLANES, LANES), 0)
        b = b_ref[...]

        def step(k, carry):
            c, acc = carry
            i = n8 - 1 - k
            t = _tile_cumsum(d_ref[i], row, True) + c
            dpre = t * _sigmoid(-(p_ref[i] + b))
            o_ref[i] = dpre
            return t[0:1, :], acc + dpre

        _, acc = lax.fori_loop(0, n8, step, (jnp.zeros((1, LANES), F32), jnp.zeros((SUBLANES, LANES), F32)))
        s_ref[...] = jnp.sum(acc, axis=0, keepdims=True)

    dpre, db = pl.pallas_call(
        body, name="fgate_bwd",
        out_shape=(jax.ShapeDtypeStruct((n8, SUBLANES, LANES), F32), jax.ShapeDtypeStruct((1, LANES), F32)),
        compiler_params=_params(),
    )(dfcum.reshape(n8, SUBLANES, LANES), pre.reshape(n8, SUBLANES, LANES), b_pad.reshape(1, LANES))
    return dpre.reshape(L, LANES), db


FOX_BLOCK = 1024


def _fox_scores(qs, k, fk, diagonal, row0=0):
    s = lax.dot_general(qs, k, _NT, preferred_element_type=F32) - fk
    if diagonal:
        row = row0 + lax.broadcasted_iota(jnp.int32, s.shape, 0)
        col = lax.broadcasted_iota(jnp.int32, s.shape, 1)
        s = jnp.where(row >= col, s, NEG_BIG)
    return s


def _fox_diagonal_parts(tq):
    half = tq // 2
    return ((slice(0, half), half), (slice(half, tq), tq))


def _fox_specs(tq, L):
    nq = L // tq
    return dict(
        rows=lambda off: pl.BlockSpec((tq, HEAD_DIM), lambda h, i: (i, off + h)),
        seq=lambda off: pl.BlockSpec((L, HEAD_DIM), lambda h, i: (0, off + h)),
        col=pl.BlockSpec((None, None, tq, 1), lambda h, i: (h, i, 0, 0)),
        col_all=pl.BlockSpec((None, nq, tq, 1), lambda h, i: (h, 0, 0, 0)),
        row=pl.BlockSpec((None, None, 1, tq), lambda h, i: (h, i, 0, 0)),
        row_all=pl.BlockSpec((None, nq, 1, tq), lambda h, i: (h, 0, 0, 0)))


FOX_FWD_HEADS = 2
FOX_FWD_BLOCK = 1024


def _fox_fwd(proj, kv, fk):
    L = proj.shape[0]
    tq = min(FOX_FWD_BLOCK, L)
    nq = L // tq
    nh = FOX_FWD_HEADS
    W = nh * HEAD_DIM
    lse_shape = fk.shape[:2] + (fk.shape[3], 1)
    fk = fk.reshape(FOX_HEADS, nq, 1, tq)

    def body(q_ref, k_ref, v_ref, fk_ref, o_ref, lse_ref, m_s, l_s, acc_s):
        qi = pl.program_id(1)
        cols = [slice(a * HEAD_DIM, (a + 1) * HEAD_DIM) for a in range(nh)]
        qs = [(q_ref[:, cs] * (HEAD_DIM ** -0.5)).astype(BF16) for cs in cols]
        m_s[...] = jnp.full_like(m_s, NEG_BIG)
        l_s[...] = jnp.zeros_like(l_s)
        acc_s[...] = jnp.zeros_like(acc_s)

        def block(j, diagonal):
            r0 = pl.multiple_of(j * tq, tq)
            for a, cs in enumerate(cols):
                s = _fox_scores(qs[a], k_ref[pl.ds(r0, tq), cs], fk_ref[a, j], diagonal)
                m_new = jnp.maximum(m_s[a], jnp.max(s, axis=-1, keepdims=True))
                alpha = jnp.exp(m_s[a] - m_new)
                p = jnp.exp(s - m_new)
                l_s[a] = alpha * l_s[a] + jnp.sum(p, axis=-1, keepdims=True)
                acc_s[a] = alpha * acc_s[a] + jnp.dot(p.astype(BF16), v_ref[pl.ds(r0, tq), cs],
                                                      preferred_element_type=F32)
                m_s[a] = m_new

        def below(j, carry):
            block(j, False)
            return carry

        lax.fori_loop(0, qi, below, 0)
        block(qi, True)
        for a, cs in enumerate(cols):
            o_ref[:, cs] = acc_s[a] / l_s[a]
            lse_ref[a] = m_s[a] + jnp.log(l_s[a])

    att, lse = pl.pallas_call(
        body, name="fox_fwd",
        out_shape=(jax.ShapeDtypeStruct((L, MAIN_WIDTH), F32),
                   jax.ShapeDtypeStruct((FOX_HEADS, nq, tq, 1), F32)),
        grid=(FOX_HEADS // nh, nq),
        in_specs=[pl.BlockSpec((tq, W), lambda h, i: (i, h)),
                  pl.BlockSpec((L, W), lambda h, i: (0, h)),
                  pl.BlockSpec((L, W), lambda h, i: (0, FOX_HEADS // nh + h)),
                  pl.BlockSpec((nh, nq, 1, tq), lambda h, i: (h, 0, 0, 0))],
        out_specs=(pl.BlockSpec((tq, W), lambda h, i: (i, h)),
                   pl.BlockSpec((nh, None, tq, 1), lambda h, i: (h, i, 0, 0))),
        scratch_shapes=[pltpu.VMEM((nh, tq, 1), F32), pltpu.VMEM((nh, tq, 1), F32),
                        pltpu.VMEM((nh, tq, HEAD_DIM), F32)],
        compiler_params=_params("parallel", "parallel"),
    )(proj, kv, kv, fk)
    return att, lse.reshape(lse_shape)


def _fox_bwd(proj, kv, fk, lse, delta, datt, dproj):
    L = proj.shape[0]
    tq = min(FOX_BLOCK, L)
    nq = L // tq
    sp = _fox_specs(tq, L)

    def body(q_ref, k_ref, v_ref, fk_ref, lse_ref, dl_ref, do_ref, dp_hbm,
             dq_ref, dk_ref, dv_ref, dfq_ref, dfk_ref, dk_s, dv_s, df_s, dq_s, dfq_s):
        ki = pl.program_id(1)

        @pl.when(ki == 0)
        def _():
            dq_s[...] = jnp.zeros_like(dq_s)
            dfq_s[...] = jnp.zeros_like(dfq_s)

        k, v, fk = k_ref[...], v_ref[...], fk_ref[...]
        dk_s[...] = jnp.zeros_like(dk_s)
        dv_s[...] = jnp.zeros_like(dv_s)
        df_s[...] = jnp.zeros_like(df_s)

        def block(i, rows, width, diagonal):
            n = rows.stop - rows.start
            r0 = pl.multiple_of(i * tq + rows.start, n)
            qs = (q_ref[pl.ds(r0, n), :] * (HEAD_DIM ** -0.5)).astype(BF16)
            dob = do_ref[pl.ds(r0, n), :].astype(BF16)
            kw, vw = k[:width], v[:width]
            p = jnp.exp(_fox_scores(qs, kw, fk[:, :width], diagonal, rows.start) - lse_ref[i][rows])
            dp = lax.dot_general(dob, vw, _NT, preferred_element_type=F32)
            ds = p * (dp - dl_ref[i][rows])
            dsb = ds.astype(BF16)
            dv_s[:width] += lax.dot_general(p.astype(BF16), dob, _TN, preferred_element_type=F32)
            dk_s[:width] += lax.dot_general(dsb, qs, _TN, preferred_element_type=F32)
            df_s[:, :width] -= jnp.sum(ds, axis=0, keepdims=True)
            dq_s[i, rows] += jnp.dot(dsb, kw, preferred_element_type=F32)
            dfq_s[i, rows] += jnp.sum(ds, axis=1, keepdims=True)

        def above(i, carry):
            block(i, slice(0, tq), tq, False)
            return carry

        for rows, width in _fox_diagonal_parts(tq):
            block(ki, rows, width, True)
        lax.fori_loop(ki + 1, nq, above, 0)
        dk_ref[...] = dk_s[...].astype(BF16)
        dv_ref[...] = dv_s[...].astype(BF16)
        dfk_ref[...] = df_s[...]

        @pl.when(ki == nq - 1)
        def _():
            dq_ref[...] = (dq_s[...].reshape(L, HEAD_DIM) * (HEAD_DIM ** -0.5)).astype(BF16)
            dfq_ref[...] = dfq_s[...]

    return pl.pallas_call(
        body, name="fox_bwd",
        out_shape=(jax.ShapeDtypeStruct(dproj.shape, dproj.dtype),
                   jax.ShapeDtypeStruct((L, MAIN_WIDTH), BF16),
                   jax.ShapeDtypeStruct((L, MAIN_WIDTH), BF16),
                   jax.ShapeDtypeStruct((FOX_HEADS, nq, tq, 1), F32),
                   jax.ShapeDtypeStruct((FOX_HEADS, nq, 1, tq), F32)),
        grid=(FOX_HEADS, nq),
        in_specs=[sp["seq"](0), sp["rows"](0), sp["rows"](FOX_HEADS), sp["row"],
                  sp["col_all"], sp["col_all"], sp["seq"](0), _ANY],
        out_specs=(sp["seq"](0), sp["rows"](0), sp["rows"](0), sp["col_all"], sp["row"]),
        input_output_aliases={7: 0},
        scratch_shapes=[pltpu.VMEM((tq, HEAD_DIM), F32), pltpu.VMEM((tq, HEAD_DIM), F32),
                        pltpu.VMEM((1, tq), F32), pltpu.VMEM((nq, tq, HEAD_DIM), F32),
                        pltpu.VMEM((nq, tq, 1), F32)],
        compiler_params=_params("parallel", "arbitrary"),
    )(proj, kv, kv, fk, lse, delta, datt, dproj)


def _pad_lanes(a):
    return jnp.pad(a, ((0, 0), (0, LANES - a.shape[1])))


def _mem_branch_fwd(memn, w_mk, proj, tag):
    kvm = _mm(memn, w_mk, name="mem_kv_" + tag)
    return kvm, _mem_attn_fwd(proj, kvm)


def _mem_branch_bwd(mem, g, w_mk, proj, memn, kvm, do_mem, dproj, tag):
    dproj, dkvm = _mem_attn_bwd(proj, kvm, do_mem, dproj)
    dkvm = dkvm.astype(BF16)
    dw_mk = _mm(memn, dkvm, ta=True, name="dw_mem_kv_" + tag, out_dtype=BF16)
    dmemn = _mm(dkvm, w_mk, tb=True, name="dmemn_" + tag)
    _, dg = _rmsnorm_bwd(mem, g, dmemn, name="mem_norm_bwd_" + tag, dx_dtype=BF16)
    return dproj, dw_mk, dg


def _local_step(x, mem, target, w, fetch=None, grads_ready=None):
    if grads_ready is None:
        grads_ready = lambda group, grads, token: token
    L = x.shape[0]
    g = {}
    w = dict(w)

    b_re_t = jnp.transpose(w["b_re"], (0, 2, 1))
    b_im_t = jnp.transpose(w["b_im"], (0, 2, 1))
    ar, ai, bbr_t, bbi_t = _s5_prep(w["lam_re"], w["lam_im"], w["log_step"], b_re_t, b_im_t)
    bmat, cmat = _s5_block_mats(bbr_t, bbi_t, w["c_re"], w["c_im"])
    a_rows = _s5_a_rows(ar, ai)

    hn0 = _rmsnorm_fwd(x, w["pre_norm_g"][0], name="pre_norm_0", out_dtype=BF16)
    memn0 = _rmsnorm_fwd(mem, w["mem_norm_g"][0], name="mem_norm_0", out_dtype=BF16)
    memn1 = _rmsnorm_fwd(mem, w["mem_norm_g"][1], name="mem_norm_1", out_dtype=BF16)
    if fetch is not None:
        w.update(fetch("a", [hn0, memn0, memn1, bmat, cmat, a_rows]))
    proj_a = _mm(hn0, w["w_in_a"], name="in_proj_a")
    y, yg, xp = _s5_fwd(proj_a, bmat, cmat, a_rows, w["d_skip"])
    if fetch is not None:
        w.update(fetch("b", yg))
    t = _mm(yg, w["w_glu"], name="glu_proj")
    kvm0, om0 = _mem_branch_fwd(memn0, w["w_mem_kv"][0], proj_a, "0")
    cat0 = _gate_a_fwd(y, t, w["b_glu"], proj_a, om0)
    o0 = _mm(cat0, w["w_out"][0], name="out_proj_0")
    h1, kv_in, hn1 = _post_norm_and_next_norms(
        o0, w["post_norm_g"][0], x, w["kv_norm_g"], w["pre_norm_g"][1], name="post_norm_0_kv_pre_norm_1")

    if fetch is not None:
        w.update(fetch("c", kv_in))
    kv = _mm(kv_in, w["w_kv"], name="kv_proj", out_dtype=BF16)
    pre_f = _mm(kv_in, w["w_fgate"], name="fgate_proj")
    b_f = jnp.pad(w["b_fgate"], (0, LANES - FOX_HEADS))
    fcum = _fgate_fwd(pre_f, b_f)
    fc = jnp.transpose(fcum[:, :FOX_HEADS])
    tq = min(FOX_BLOCK, L)
    fk = fc.reshape(FOX_HEADS, L // tq, 1, tq)

    proj_b = _mm(hn1, w["w_in_b"], name="in_proj_b")
    att, lse = _fox_fwd(proj_b, kv, fk)
    kvm1, om1 = _mem_branch_fwd(memn1, w["w_mem_kv"][1], proj_b, "1")
    cat1 = _gate_b_fwd(att, proj_b, om1)
    o1 = _mm(cat1, w["w_out"][1], name="out_proj_1")
    dh2, loss_row = _final_norm_loss(o1, w["post_norm_g"][1], h1, target)

    do1, dpost1 = _rmsnorm_bwd(o1, w["post_norm_g"][1], dh2, name="post_norm_bwd_1", dx_dtype=BF16)
    dcat1 = _mm(do1, w["w_out"][1], tb=True, name="dcat_1", out_dtype=BF16)
    g["w_out_1"] = _mm(cat1, do1, ta=True, name="dw_out_1", out_dtype=BF16)
    datt, dproj_b, dom1, delta = _gate_b_bwd(dcat1, att, proj_b, om1)
    dproj_b, g["w_mem_kv_1"], dmemg1 = _mem_branch_bwd(mem, w["mem_norm_g"][1], w["w_mem_kv"][1], proj_b,
                                                      memn1, kvm1, dom1, dproj_b, "1")
    delta = delta.reshape(lse.shape)
    dproj_b, dk, dv, dfq, dfk = _fox_bwd(proj_b, kv, fk, lse, delta, datt, dproj_b)
    g["w_in_b"] = _mm(hn1, dproj_b, ta=True, name="dw_in_b", out_dtype=BF16, shards=N_CHIPS)
    dhn1 = _mm(dproj_b, w["w_in_b"], tb=True, name="dhn_1")

    dkv = jnp.concatenate([dk, dv], axis=1)
    g["w_kv"] = _mm(kv_in, dkv, ta=True, name="dw_kv", out_dtype=BF16, shards=N_CHIPS)
    dkv_in_a = _mm(dkv, w["w_kv"], tb=True, name="dkv_in_kv")
    dfcum = _pad_lanes(jnp.transpose(dfq.reshape(FOX_HEADS, L) + dfk.reshape(FOX_HEADS, L)))
    dpre_f, db_f = _fgate_bwd(dfcum, pre_f, b_f)
    g["b_fgate"] = db_f[0, :FOX_HEADS]
    g["w_fgate"] = _mm(kv_in, dpre_f, ta=True, name="dw_fgate")[:, :FOX_HEADS]
    dkv_in_b = _mm(dpre_f, w["w_fgate"], tb=True, name="dkv_in_fgate")
    dh1, g["kv_norm_g"], dpre1 = _rmsnorm_bwd_pair(h1, w["kv_norm_g"], (dkv_in_a, dkv_in_b), w["pre_norm_g"][1],
                                                   dhn1, adds=(dh2,), name="kv_pre_norm_bwd")
    dh1 = grads_ready("b", g, dh1)

    do0, dpost0 = _rmsnorm_bwd(o0, w["post_norm_g"][0], dh1, name="post_norm_bwd_0", dx_dtype=BF16)
    dcat0 = _mm(do0, w["w_out"][0], tb=True, name="dcat_0", out_dtype=BF16)
    g["w_out_0"] = _mm(cat0, do0, ta=True, name="dw_out_0", out_dtype=BF16)
    dcat0 = grads_ready("b_send", g, dcat0)
    dproj_a, dt, dyg_a, dom0, db_glu = _gate_a_bwd(dcat0, y, t, w["b_glu"], proj_a, om0)
    g["b_glu"] = db_glu[0]
    g["w_glu"] = _mm(yg, dt, ta=True, name="dw_glu", out_dtype=BF16)
    dyg_b = _mm(dt, w["w_glu"], tb=True, name="dyg")
    dproj_a, g["w_mem_kv_0"], dmemg0 = _mem_branch_bwd(mem, w["mem_norm_g"][0], w["w_mem_kv"][0], proj_a,
                                                      memn0, kvm0, dom0, dproj_a, "0")
    dyg_b = grads_ready("a1", g, dyg_b)
    dproj_a, db_blk, dc_blk, da_rows, dd_skip = _s5_bwd(proj_a, dyg_a, dyg_b, y, xp, bmat, cmat, a_rows,
                                                        w["d_skip"], dproj_a)
    dproj_a = grads_ready("a1_send", g, dproj_a)
    g["d_skip"] = dd_skip[0]
    g["w_in_a"] = _mm(hn0, dproj_a, ta=True, name="dw_in_a", out_dtype=BF16, shards=N_CHIPS)
    dproj_a = grads_ready("a2", g, dproj_a)
    dhn0 = _mm(dproj_a, w["w_in_a"], tb=True, name="dhn_0")
    grad_x, dpre0 = _rmsnorm_bwd(x, w["pre_norm_g"][0], dhn0, adds=(dh1,), name="pre_norm_bwd_0")

    dbb = _s5_unfold(db_blk)
    dcc = _s5_unfold(dc_blk)
    g["c_re"], g["c_im"] = dcc[0], -dcc[1]
    d_ar = da_rows[:, 0, :STATE_COLS].reshape(SSM_GROUPS, SSM_STATE)
    d_ai = da_rows[:, 0, STATE_COLS:].reshape(SSM_GROUPS, SSM_STATE)
    dlr, dli, dls, dbr_t, dbi_t = _s5_prep_bwd(w["lam_re"], w["lam_im"], w["log_step"], b_re_t, b_im_t,
                                               d_ar, d_ai, dbb[0], dbb[1])
    g["lam_re"], g["lam_im"], g["log_step"] = dlr, dli, dls[:, 0]
    g["b_re"] = jnp.transpose(dbr_t, (0, 2, 1))
    g["b_im"] = jnp.transpose(dbi_t, (0, 2, 1))
    g["pre_norm_g"] = jnp.stack([dpre0, dpre1])
    g["post_norm_g"] = jnp.stack([dpost0, dpost1])
    g["mem_norm_g"] = jnp.stack([dmemg0, dmemg1])
    return loss_row, grad_x, g


_MESH = pl.DeviceIdType.MESH
_ANY = pl.BlockSpec(memory_space=pl.ANY)


def _place():
    x, y, c = lax.axis_index("x"), lax.axis_index("y"), lax.axis_index("c")
    chips = [(1 - x, y), (x, 1 - y), (1 - x, 1 - y)]
    return x, y, c, chips


_HBM = pl.BlockSpec(memory_space=pltpu.HBM)
_SEM = pl.BlockSpec(memory_space=pltpu.SEMAPHORE)
_SIDE = pltpu.SideEffectType.DATAFLOW_SIDE_EFFECTING


def _in_hbm(a):
    return pltpu.with_memory_space_constraint(a, pltpu.HBM)


def _hbm_like(a):
    return pltpu.HBM(a.shape, a.dtype)


def _ici_copies(srcs, lands, send_sem, recv_sem, src_at, dst_at, wait_at, to_sibling=False):
    x, y, c, chips = _place()
    peers = [(x, y, 1 - c)] if to_sibling else [(cx, cy, c) for cx, cy in chips]
    m = len(peers)
    start, wait = [], []
    for i in range(len(srcs)):
        for k, (px, py, pc) in enumerate(peers):
            sem = dict(send_sem=send_sem.at[m * i + k], recv_sem=recv_sem.at[m * i + k],
                       device_id=(px, py, pc), device_id_type=_MESH)
            src = src_at(srcs[i], 2 * px + py, c)
            start.append(pltpu.make_async_remote_copy(src_ref=src, dst_ref=dst_at(lands[i], 2 * x + y, k, c), **sem))
            wait.append(pltpu.make_async_remote_copy(src_ref=src, dst_ref=wait_at(lands[i], 2 * px + py, k, c), **sem))
    return start, wait


def _route_peers(route):
    return 1 if len(route) == 4 else 3


_BLOCK_ROUTE = (lambda s, j, c: s, lambda l, me, k, c: l.at[me, c], lambda l, j, k, c: l.at[j, c])


def _ici_start(srcs, lands, token, route, *, name):
    n = len(srcs)

    def body(*refs):
        start, _ = _ici_copies(refs[:n], refs[n:2 * n], refs[2 * n + 1], refs[2 * n + 2], *route)
        for cp in start:
            cp.start()

    sems = pltpu.SemaphoreType.DMA((_route_peers(route) * n,))
    outs = pl.pallas_call(
        body, name=name,
        out_shape=(sems, sems, *[_hbm_like(a) for a in srcs], *[_hbm_like(a) for a in lands], _hbm_like(token)),
        in_specs=[_HBM] * (2 * n + 1), out_specs=(_SEM, _SEM, *[_HBM] * (2 * n + 1)),
        input_output_aliases={i: 2 + i for i in range(2 * n + 1)},
        compiler_params=pltpu.CompilerParams(has_side_effects=_SIDE),
    )(*[_in_hbm(a) for a in srcs], *[_in_hbm(a) for a in lands], _in_hbm(token))
    return (outs[0], outs[1], list(outs[2:2 + n]), list(outs[2 + n:2 + 2 * n])), outs[2 + 2 * n]


def _ici_wait(handle, after, route, *, name):
    send_sem, recv_sem, srcs, lands = handle
    n = len(srcs)
    after = list(after) if isinstance(after, (list, tuple)) else [after]

    def body(*refs):
        _, wait = _ici_copies(refs[:n], refs[n:2 * n], refs[2 * n], refs[2 * n + 1], *route)
        for cp in wait:
            cp.wait_send()
            cp.wait_recv()

    outs = pl.pallas_call(
        body, name=name,
        out_shape=(*[_hbm_like(a) for a in srcs], *[_hbm_like(a) for a in lands]),
        in_specs=[_HBM] * (2 * n) + [_SEM, _SEM] + [_ANY] * len(after), out_specs=tuple([_HBM] * (2 * n)),
        input_output_aliases={i: i for i in range(2 * n)},
        compiler_params=pltpu.CompilerParams(has_side_effects=_SIDE),
    )(*srcs, *lands, send_sem, recv_sem, *after)
    return list(outs[:n]), list(outs[n:])


_GATHER_ROUTE = (lambda s, j, c: s.at[c], lambda l, me, k, c: l.at[me, c], lambda l, j, k, c: l.at[j, c])
_SCATTER_ROUTE = (lambda s, j, c: s.at[j], lambda l, me, k, c: l.at[k], lambda l, j, k, c: l.at[k])
_SHARE_ROUTE = (lambda s, j, c: s, lambda l, me, k, c: l.at[c], lambda l, j, k, c: l.at[1 - c], True)
_SWAP_ROUTE = (lambda s, j, c: s.at[:, 1 - c], lambda l, me, k, c: l, lambda l, j, k, c: l, True)


def _gather_forward(lands, tag, own=False):
    n = len(lands)
    m = 4 if own else 3

    def body(*refs):
        ins, outs = refs[:n], refs[n:2 * n]
        send_sem, recv_sem = refs[2 * n:]
        x, y, c, chips = _place()
        slots = [2 * cx + cy for cx, cy in chips] + [2 * x + y]

        def copy(i, k, half):
            return pltpu.make_async_remote_copy(
                src_ref=ins[i].at[slots[k], half], dst_ref=outs[i].at[slots[k], half],
                send_sem=send_sem.at[m * i + k], recv_sem=recv_sem.at[m * i + k],
                device_id=(x, y, 1 - c), device_id_type=_MESH)

        copies = [copy(i, k, c) for i in range(n) for k in range(m)]
        for cp in copies:
            cp.start()
        for i in range(n):
            for k in range(m):
                copy(i, k, 1 - c).wait_recv()
        for cp in copies:
            cp.wait_send()

    return pl.pallas_call(
        body, name="gather_forward_to_sibling_" + tag,
        out_shape=[jax.ShapeDtypeStruct(a.shape, a.dtype) for a in lands],
        in_specs=[_ANY] * n, out_specs=[_ANY] * n,
        input_output_aliases={i: i for i in range(n)},
        scratch_shapes=[pltpu.SemaphoreType.DMA((m * n,)), pltpu.SemaphoreType.DMA((m * n,))],
    )(*lands)


def _swap_halves(grads, tag):
    n = len(grads)

    def body(*refs):
        ins, outs = refs[:n], refs[n:2 * n]
        send_sem, recv_sem = refs[2 * n:]
        x, y, c, _ = _place()
        copies = [pltpu.make_async_remote_copy(
            src_ref=ins[i].at[:, 1 - c], dst_ref=outs[i],
            send_sem=send_sem.at[i], recv_sem=recv_sem.at[i],
            device_id=(x, y, 1 - c), device_id_type=_MESH) for i in range(n)]
        for cp in copies:
            cp.start()
        for cp in copies:
            cp.wait()

    return pl.pallas_call(
        body, name="grad_swap_halves_" + tag,
        out_shape=[jax.ShapeDtypeStruct((N_CHIPS,) + g.shape[2:], g.dtype) for g in grads],
        in_specs=[_ANY] * n, out_specs=[_ANY] * n,
        scratch_shapes=[pltpu.SemaphoreType.DMA((n,)), pltpu.SemaphoreType.DMA((n,))],
    )(*grads)


def _sum_rows(h, C):
    return max(d for d in range(SUBLANES, h + 1, SUBLANES) if h % d == 0 and d * C <= 1 << 20)


SUM_STEPS = 4


def _pair_sums(gs, rs, c_idx, *, name):
    n = len(gs)
    rows = [g.shape[2] // SUM_STEPS for g in gs]

    def body(c_ref, *refs):
        for g_ref, r_ref, o_ref in zip(refs[:n], refs[n:2 * n], refs[2 * n:]):
            o_ref[...] = (g_ref[...].astype(F32) + r_ref[...].astype(F32)).astype(o_ref.dtype)

    return pl.pallas_call(
        body, name=name,
        out_shape=[jax.ShapeDtypeStruct((N_CHIPS,) + g.shape[2:], g.dtype) for g in gs],
        grid_spec=pltpu.PrefetchScalarGridSpec(
            num_scalar_prefetch=1, grid=(N_CHIPS, SUM_STEPS),
            in_specs=[pl.BlockSpec((None, None, tr, g.shape[3]), lambda j, i, s: (j, s[0], i, 0))
                      for g, tr in zip(gs, rows)]
            + [pl.BlockSpec((None, tr, g.shape[3]), lambda j, i, s: (j, i, 0)) for g, tr in zip(gs, rows)],
            out_specs=[pl.BlockSpec((None, tr, g.shape[3]), lambda j, i, s: (j, i, 0)) for g, tr in zip(gs, rows)]),
        compiler_params=_params("parallel", "parallel"),
    )(c_idx, *gs, *rs)


def _owner_sums(ss, rs, jc_idx, *, name):
    n = len(ss)
    rows = [s.shape[1] // SUM_STEPS for s in ss]

    def body(jc_ref, *refs):
        for s_ref, r_ref, m_ref, o_ref in zip(refs[:n], refs[n:2 * n], refs[2 * n:3 * n], refs[3 * n:]):
            acc = s_ref[...].astype(F32)
            for k in range(3):
                acc = acc + r_ref[k].astype(F32)
            m_ref[...] = acc
            o_ref[...] = acc

    outs = pl.pallas_call(
        body, name=name,
        out_shape=[jax.ShapeDtypeStruct(s.shape[1:], F32) for s in ss]
        + [jax.ShapeDtypeStruct((2,) + s.shape[1:], F32) for s in ss],
        grid_spec=pltpu.PrefetchScalarGridSpec(
            num_scalar_prefetch=1, grid=(SUM_STEPS,),
            in_specs=[pl.BlockSpec((None, tr, s.shape[2]), lambda i, p: (p[0], i, 0)) for s, tr in zip(ss, rows)]
            + [pl.BlockSpec((3, tr, s.shape[2]), lambda i, p: (0, i, 0)) for s, tr in zip(ss, rows)],
            out_specs=[pl.BlockSpec((tr, s.shape[2]), lambda i, p: (i, 0)) for s, tr in zip(ss, rows)]
            + [pl.BlockSpec((None, tr, s.shape[2]), lambda i, p: (p[1], i, 0)) for s, tr in zip(ss, rows)]),
        compiler_params=_params("parallel"),
    )(jc_idx, *ss, *rs)
    return outs[:n], outs[n:]


def _chip_sums(grads, c_idx, tag):
    views = [g.reshape(N_CHIPS, 2, g.shape[1] // 2, g.shape[2]) for g in grads]
    arrived = _swap_halves(views, tag)
    return _pair_sums(views, arrived, c_idx, name=f"grad_pair_sums_{tag}")


def _sum_devices(blocks):
    R = blocks.shape[2]
    tr = _sum_rows(R, 2 * N_CHIPS * LANES)

    def body(b_ref, o_ref):
        acc = b_ref[0, 0]
        for d in range(1, 2 * N_CHIPS):
            acc = acc + b_ref[d // 2, d % 2]
        o_ref[...] = acc

    return pl.pallas_call(
        body, name="sum_small_over_devices", out_shape=jax.ShapeDtypeStruct((R, LANES), F32),
        grid=(R // tr,),
        in_specs=[pl.BlockSpec((N_CHIPS, 2, tr, LANES), lambda i: (0, 0, i, 0))],
        out_specs=pl.BlockSpec((tr, LANES), lambda i: (i, 0)),
        compiler_params=_params("parallel"),
    )(blocks)


def _adamw(w, g, m, v, *, name):
    R, C = w.shape
    tr = max(d for d in range(SUBLANES, R + 1, SUBLANES)
             if R % d == 0 and 7 * 2 * d * C * 4 <= VMEM_LIMIT_BYTES // 2)

    def body(w_ref, g_ref, m_ref, v_ref, d_ref, nm_ref, nv_ref):
        g = g_ref[...]
        m = ADAM_B1 * m_ref[...] + (1.0 - ADAM_B1) * g
        v = ADAM_B2 * v_ref[...] + (1.0 - ADAM_B2) * (g * g)
        nm_ref[...] = m
        nv_ref[...] = v
        m_hat = m / (1.0 - ADAM_B1 ** ADAM_STEP)
        v_hat = v / (1.0 - ADAM_B2 ** ADAM_STEP)
        d_ref[...] = -ADAM_LR * (m_hat / (jnp.sqrt(v_hat) + ADAM_EPS) + ADAM_WD * w_ref[...])

    blk = pl.BlockSpec((tr, C), lambda i: (i, 0))
    sds = jax.ShapeDtypeStruct((R, C), F32)
    return pl.pallas_call(
        body, name=name, out_shape=(sds, sds, sds), grid=(R // tr,),
        in_specs=[blk] * 4, out_specs=(blk, blk, blk),
        compiler_params=_params("parallel"),
    )(w, g, m, v)


_TILE = SUBLANES * LANES


def _pack(arrays):
    rows = []
    for a in arrays:
        flat = a.reshape(-1)
        flat = jnp.pad(flat, (0, (-flat.shape[0]) % _TILE))
        rows.append(flat.reshape(-1, LANES))
    return jnp.concatenate(rows, axis=0)


def _unpack(buf, shapes):
    out, r = [], 0
    for s in shapes:
        size = math.prod(s)
        nr = -(-size // _TILE) * SUBLANES
        out.append(buf[r:r + nr].reshape(-1)[:size].reshape(s))
        r += nr
    return out


_BIG = ("w_in_a", "w_glu", "w_kv", "w_in_b", "w_mem_kv", "w_out")
_REPLICATED = ("pre_norm_g", "post_norm_g", "lam_re", "lam_im", "log_step", "b_re", "b_im", "c_re", "c_im",
               "kv_norm_g", "b_fgate", "mem_norm_g")
_SHARDED_SMALL = ("d_skip", "b_glu", "w_fgate")
_WEIGHTS = ("pre_norm_g", "post_norm_g", "w_in_a", "lam_re", "lam_im", "log_step", "b_re", "b_im", "c_re",
            "c_im", "d_skip", "w_glu", "b_glu", "kv_norm_g", "w_kv", "w_fgate", "b_fgate", "w_in_b",
            "mem_norm_g", "w_mem_kv", "w_out")


def _halves(a):
    return a.reshape(2, a.shape[0] // 2, a.shape[1])


def _unhalve(a):
    return a.reshape(N_CHIPS, 2 * a.shape[2], a.shape[3])


def _columns(a):
    return jnp.transpose(a, (1, 0, 2)).reshape(a.shape[1], N_CHIPS * a.shape[2])


def kernel(x, mem, pre_norm_g, post_norm_g, w_in_a, lam_re, lam_im, log_step, b_re, b_im, c_re, c_im, d_skip, w_glu, b_glu, kv_norm_g, w_kv, w_fgate, b_fgate, w_in_b, mem_norm_g, w_mem_kv, w_out, loss_target, m_pre_norm_g, m_post_norm_g, m_w_in_a, m_lam_re, m_lam_im, m_log_step, m_b_re, m_b_im, m_c_re, m_c_im, m_d_skip, m_w_glu, m_b_glu, m_kv_norm_g, m_w_kv, m_w_fgate, m_b_fgate, m_w_in_b, m_mem_norm_g, m_w_mem_kv, m_w_out, v_pre_norm_g, v_post_norm_g, v_w_in_a, v_lam_re, v_lam_im, v_log_step, v_b_re, v_b_im, v_c_re, v_c_im, v_d_skip, v_w_glu, v_b_glu, v_kv_norm_g, v_w_kv, v_w_fgate, v_b_fgate, v_w_in_b, v_mem_norm_g, v_w_mem_kv, v_w_out):
    a = dict(locals())
    xi, yi, ci = lax.axis_index("x"), lax.axis_index("y"), lax.axis_index("c")
    chip = 2 * xi + yi
    c_idx = jnp.reshape(ci, (1,)).astype(jnp.int32)
    jc_idx = jnp.stack([chip, ci]).astype(jnp.int32)

    vec = jnp.zeros((2 * SUBLANES, MAIN_WIDTH // N_CHIPS), F32)
    vec = vec.at[0].set(a["d_skip"][0]).at[1].set(a["b_glu"][0])
    def own_slot(gathered, parts):
        return [lax.dynamic_update_index_in_dim(g, p, chip, 0) for g, p in zip(gathered, parts)]

    travelling, token = {}, a["pre_norm_g"]

    def start_gather(tag, parts, token):
        lands = [lax.empty((N_CHIPS,) + p.shape, p.dtype) for p in parts]
        travelling[tag], token = _ici_start(parts, lands, token, _GATHER_ROUTE, name=f"gather_{tag}_start")
        return token

    token = start_gather("a", [_halves(a["w_in_a"][0].astype(BF16)), _halves(vec)], token)
    later = ("w_glu", "w_mem_kv", "w_out", "w_kv", "w_fgate", "w_in_b")
    token, *raw = lax.optimization_barrier((token, *[a[n] for n in later]))
    raw = dict(zip(later, raw))
    token = start_gather("b", [_halves(raw["w_glu"][0].astype(BF16)),
                               *[_halves(raw["w_mem_kv"][i].astype(BF16)) for i in range(2)],
                               *[_halves(raw["w_out"][i].astype(BF16)) for i in range(2)]], token)
    token = start_gather("c", [_halves(raw["w_kv"].astype(BF16)), _halves(_pad_lanes(raw["w_fgate"]).astype(BF16)),
                               _halves(raw["w_in_b"][0].astype(BF16))], token)

    small_names = _REPLICATED + _SHARDED_SMALL
    small_state = [_pack([a[n] for n in small_names])]

    def fetch(tag, after):
        if tag == "a":
            after = list(after) + small_state
        parts, lands = _ici_wait(travelling[tag], after, _GATHER_ROUTE, name=f"gather_{tag}_wait")
        full = own_slot(_gather_forward(lands, tag), parts)
        if tag == "a":
            w_in_a, vecs = full
            moments = [a[prefix + n] for prefix in ("m_", "v_") for n in small_names]
            w_in_a, *moments = lax.optimization_barrier((w_in_a, *moments))
            packs = [_pack(moments[:len(small_names)]), _pack(moments[len(small_names):])]
            w_in_a, *packs = lax.optimization_barrier((_columns(_unhalve(w_in_a)), *packs))
            small_state.extend(packs)
            return dict(w_in_a=w_in_a, d_skip=vecs[:, 0, 0, :].reshape(MAIN_WIDTH),
                        b_glu=vecs[:, 0, 1, :].reshape(MAIN_WIDTH))
        if tag == "b":
            w_glu, w_mk0, w_mk1, w_out0, w_out1 = full
            return dict(w_glu=w_glu.reshape(MAIN_WIDTH, MAIN_WIDTH),
                        w_mem_kv=[m.reshape(D_MODEL, 2 * MEM_WIDTH) for m in (w_mk0, w_mk1)],
                        w_out=[o.reshape(D_MODEL, D_MODEL) for o in (w_out0, w_out1)])
        w_kv, w_fg, w_in_b = full
        return dict(w_kv=_columns(_unhalve(w_kv)), w_fgate=w_fg.reshape(D_MODEL, LANES),
                    w_in_b=_columns(_unhalve(w_in_b)))

    early = ("mem_norm_g", "lam_re", "lam_im", "log_step", "b_re", "b_im", "c_re", "c_im")
    token, *held = lax.optimization_barrier((token, *[a[n] for n in early]))
    held = dict(zip(early, held))
    w = dict(
        pre_norm_g=token, post_norm_g=a["post_norm_g"], mem_norm_g=held["mem_norm_g"],
        kv_norm_g=a["kv_norm_g"], b_fgate=a["b_fgate"],
        **{n: held[n][0] for n in early[1:]})

    sent = {}

    swapping = {}

    def grads_ready(event, g, token):
        tag = event.split("_")[0]
        if event in ("b", "a1"):
            big = {"b": lambda: [g["w_kv"], g["w_in_b"], g["w_mem_kv_1"].reshape(N_CHIPS, -1, 2 * MEM_WIDTH),
                                 g["w_out_1"].reshape(N_CHIPS, -1, D_MODEL)],
                   "a1": lambda: [g["w_glu"].reshape(N_CHIPS, -1, MAIN_WIDTH),
                                  g["w_mem_kv_0"].reshape(N_CHIPS, -1, 2 * MEM_WIDTH),
                                  g["w_out_0"].reshape(N_CHIPS, -1, D_MODEL)]}[tag]()
            views = [b.reshape(N_CHIPS, 2, b.shape[1] // 2, b.shape[2]) for b in big]
            lands = [lax.empty((N_CHIPS,) + v.shape[2:], v.dtype) for v in views]
            swapping[tag], token = _ici_start(views, lands, token, _SWAP_ROUTE, name=f"grad_swap_{tag}_start")
            return token
        if event == "a2":
            sums = _chip_sums([g["w_in_a"]], c_idx, tag)
        else:
            views, arrived = _ici_wait(swapping[tag], token, _SWAP_ROUTE, name=f"grad_swap_{tag}_wait")
            sums = _pair_sums(views, arrived, c_idx, name=f"grad_pair_sums_{tag}")
        lands = [lax.empty((3,) + s.shape[1:], s.dtype) for s in sums]
        sent[tag], token = _ici_start(sums, lands, token, _SCATTER_ROUTE, name=f"grad_send_{tag}_start")
        return token

    loss_row, grad_x, g = _local_step(a["x"][0], a["mem"][0], a["loss_target"][0], w, fetch, grads_ready)

    pack = _pack([g[n] for n in small_names] + [loss_row])
    blocks = lax.empty((N_CHIPS, 2) + pack.shape, F32)
    small_sent, token = _ici_start([pack], [blocks], loss_row, _BLOCK_ROUTE, name="small_sums_start")

    sharing = {}
    for tag in ("b", "a1", "a2"):
        sums, arrived = _ici_wait(sent[tag], [grad_x, token], _SCATTER_ROUTE, name=f"grad_send_{tag}_wait")
        mine, bufs = _owner_sums(sums, arrived, jc_idx, name=f"grad_owner_sums_{tag}")
        sharing[tag], token = _ici_start(mine, bufs, token, _SHARE_ROUTE, name=f"grad_share_{tag}_start")

    def shared(tag, after):
        _, bufs = _ici_wait(sharing[tag], after, _SHARE_ROUTE, name=f"grad_share_{tag}_wait")
        return [b.reshape(-1, b.shape[2]) for b in bufs]

    grads, delta, new_m, new_v = {}, {}, {}, {}

    def adam(n):
        shape = a[n].shape
        d2 = (-1, shape[-1])
        d, m, v = _adamw(a[n].reshape(d2), grads[n].reshape(d2), a["m_" + n].reshape(d2),
                         a["v_" + n].reshape(d2), name="adamw_" + n)
        delta[n], new_m[n], new_v[n] = d.reshape(shape), m.reshape(shape), v.reshape(shape)
        return d

    r_kv, r_in_b, r_mk1, r_out1 = shared("b", token)
    grads["w_kv"], grads["w_in_b"] = r_kv, r_in_b[None]
    done = [adam("w_kv"), adam("w_in_b")]
    r_glu, r_mk0, r_out0 = shared("a1", done)
    grads["w_glu"], grads["w_mem_kv"], grads["w_out"] = r_glu[None], jnp.stack([r_mk0, r_mk1]), jnp.stack([r_out0, r_out1])
    done = [adam("w_glu"), adam("w_mem_kv"), adam("w_out")]
    (r_in_a,) = shared("a2", done)
    grads["w_in_a"] = r_in_a[None]
    adam("w_in_a")

    (pack,), (blocks,) = _ici_wait(small_sent, [delta[n] for n in _BIG], _BLOCK_ROUTE, name="small_sums_wait")
    blocks = lax.dynamic_update_slice(blocks, pack[None, None], (chip, ci, 0, 0))
    (blocks,) = _gather_forward([blocks], "small", own=True)
    *summed, loss_sums = _unpack(_sum_devices(blocks), [g[n].shape for n in small_names] + [loss_row.shape])
    small = dict(zip(small_names, summed))
    loss = jnp.sum(loss_sums)
    for n in _REPLICATED:
        grads[n] = small[n].reshape(a[n].shape)
    nd = MAIN_WIDTH // N_CHIPS
    grads["d_skip"] = lax.dynamic_slice(small["d_skip"], (chip * nd,), (nd,))[None]
    grads["b_glu"] = lax.dynamic_slice(small["b_glu"], (chip * nd,), (nd,))[None]
    nf = D_MODEL // N_CHIPS
    grads["w_fgate"] = lax.dynamic_slice(small["w_fgate"], (chip * nf, 0), (nf, FOX_HEADS))

    shapes = [a[n].shape for n in small_names]
    d, m, v = _adamw(small_state[0], _pack([grads[n] for n in small_names]), *small_state[1:], name="adamw_small")
    for n, dd, mm, vv in zip(small_names, _unpack(d, shapes), _unpack(m, shapes), _unpack(v, shapes)):
        delta[n], new_m[n], new_v[n] = dd, mm, vv

    return (loss, grad_x[None], *[grads[n] for n in _WEIGHTS], *[delta[n] for n in _WEIGHTS],
            *[new_m[n] for n in _WEIGHTS], *[new_v[n] for n in _WEIGHTS])
```

```python
import math

import jax
import jax.numpy as jnp
from jax import lax
from jax.experimental import pallas as pl
from jax.experimental.pallas import tpu as pltpu

F32 = jnp.float32
BF16 = jnp.bfloat16

D_MODEL = 2048
N_MEM = 256
MAIN_WIDTH = 1536
MEM_WIDTH = 512
IN_WIDTH = 2 * MAIN_WIDTH + 2 * MEM_WIDTH
HEAD_DIM = 128
FOX_HEADS = MAIN_WIDTH // HEAD_DIM
MEM_HEADS = MEM_WIDTH // HEAD_DIM
SSM_GROUP = 16
SSM_GROUPS = MAIN_WIDTH // SSM_GROUP
SSM_STATE = 64
GROUPS_PER_BLOCK = 8
SSM_BLOCKS = SSM_GROUPS // GROUPS_PER_BLOCK
STATE_COLS = GROUPS_PER_BLOCK * SSM_STATE
EPS = 1e-6
ADAM_LR = 0.001
ADAM_B1 = 0.9
ADAM_B2 = 0.999
ADAM_EPS = 1e-08
ADAM_WD = 0.01
ADAM_STEP = 10
N_CHIPS = 4
LANES = 128
SUBLANES = 8
VMEM_LIMIT_BYTES = 56 * 1024 * 1024
NEG_BIG = -1e30
MESH_AXES = ("x", "y", "c")


def _params(*sem):
    return pltpu.CompilerParams(dimension_semantics=sem if sem else None,
                                vmem_limit_bytes=VMEM_LIMIT_BYTES)


def _sigmoid(x):
    return 1.0 / (1.0 + jnp.exp(-x))


def _gelu(x):
    c = math.sqrt(2.0 / math.pi)
    return 0.5 * x * (1.0 + jnp.tanh(c * (x + 0.044715 * (x * x * x))))


def _gelu_grad(x):
    c = math.sqrt(2.0 / math.pi)
    t = jnp.tanh(c * (x + 0.044715 * (x * x * x)))
    return 0.5 * (1.0 + t) + 0.5 * x * (1.0 - t * t) * (c * (1.0 + 3.0 * 0.044715 * (x * x)))


def _silu_and_grad(z):
    s = _sigmoid(z)
    return z * s, s * (1.0 + z * (1.0 - s))


_TILE_CHOICES = (4096, 3072, 2048, 1536, 1024, 768, 512, 384, 256, LANES)


def _tile(n, cap):
    return next(c for c in _TILE_CHOICES if c <= cap and n % c == 0)


def _mm(a, b, *, name, ta=False, tb=False, out_dtype=F32, shards=1, tm=1024, tn=1024, tk=4096):
    if ta:
        K, M = a.shape
    else:
        M, K = a.shape
    if tb:
        N, kb = b.shape
    else:
        kb, N = b.shape
    assert K == kb, (a.shape, b.shape)
    ns = N // shards
    tm, tn, tk = _tile(M, tm), _tile(ns, tn), _tile(K, tk)
    assert M % tm == 0 and ns % tn == 0 and K % tk == 0 and N % shards == 0
    nk = K // tk
    dn = (((0 if ta else 1,), (1 if tb else 0,)), ((), ()))

    def body(a_ref, b_ref, o_ref, *acc):
        prod = lax.dot_general(a_ref[...].astype(BF16), b_ref[...].astype(BF16), dn, preferred_element_type=F32)
        if nk == 1:
            o_ref[...] = prod.astype(o_ref.dtype)
            return
        acc_ref, = acc
        k = pl.program_id(2)

        @pl.when(k == 0)
        def _():
            acc_ref[...] = jnp.zeros_like(acc_ref)

        acc_ref[...] += prod

        @pl.when(k == nk - 1)
        def _():
            o_ref[...] = acc_ref[...].astype(o_ref.dtype)

    a_spec = (pl.BlockSpec((tk, tm), lambda i, j, k: (k, i)) if ta
              else pl.BlockSpec((tm, tk), lambda i, j, k: (i, k)))
    b_spec = (pl.BlockSpec((tn, tk), lambda i, j, k: (j, k)) if tb
              else pl.BlockSpec((tk, tn), lambda i, j, k: (k, j)))
    if shards == 1:
        out_shape = jax.ShapeDtypeStruct((M, N), out_dtype)
        o_spec = pl.BlockSpec((tm, tn), lambda i, j, k: (i, j))
    else:
        nb = ns // tn
        out_shape = jax.ShapeDtypeStruct((shards, M, ns), out_dtype)
        o_spec = pl.BlockSpec((None, tm, tn), lambda i, j, k: (j // nb, i, j % nb))
    return pl.pallas_call(
        body, name=name, out_shape=out_shape,
        grid=(M // tm, N // tn, nk),
        in_specs=[a_spec, b_spec], out_specs=o_spec,
        scratch_shapes=[] if nk == 1 else [pltpu.VMEM((tm, tn), F32)],
        compiler_params=_params("parallel", "parallel", "arbitrary"),
    )(a, b)


def _rmsnorm_fwd(x, g, *, name, out_dtype=F32, tr=256):
    L, D = x.shape
    tr = min(tr, L)

    def body(x_ref, g_ref, o_ref):
        xf = x_ref[...]
        r = lax.rsqrt(jnp.mean(xf * xf, axis=-1, keepdims=True) + EPS)
        o_ref[...] = (xf * r * g_ref[...]).astype(o_ref.dtype)

    row = pl.BlockSpec((tr, D), lambda i: (i, 0))
    vec = pl.BlockSpec((1, D), lambda i: (0, 0))
    return pl.pallas_call(
        body, name=name, out_shape=jax.ShapeDtypeStruct((L, D), out_dtype),
        grid=(L // tr,), in_specs=[row, vec], out_specs=row,
        compiler_params=_params("parallel"),
    )(x, g.reshape(1, D))


def _post_norm_and_next_norms(o, g_post, res, g_kv, g_pre, *, name, tr=256):
    L, D = o.shape
    tr = min(tr, L)

    def body(o_ref, gp_ref, r_ref, gk_ref, gn_ref, h_ref, kv_ref, hn_ref):
        of = o_ref[...]
        r = lax.rsqrt(jnp.mean(of * of, axis=-1, keepdims=True) + EPS)
        h = r_ref[...] + of * r * gp_ref[...]
        h_ref[...] = h
        hr = h * lax.rsqrt(jnp.mean(h * h, axis=-1, keepdims=True) + EPS)
        kv_ref[...] = (hr * gk_ref[...]).astype(kv_ref.dtype)
        hn_ref[...] = (hr * gn_ref[...]).astype(hn_ref.dtype)

    row = pl.BlockSpec((tr, D), lambda i: (i, 0))
    vec = pl.BlockSpec((1, D), lambda i: (0, 0))
    return pl.pallas_call(
        body, name=name,
        out_shape=(jax.ShapeDtypeStruct((L, D), F32), jax.ShapeDtypeStruct((L, D), BF16),
                   jax.ShapeDtypeStruct((L, D), BF16)),
        grid=(L // tr,), in_specs=[row, vec, row, vec, vec], out_specs=(row, row, row),
        compiler_params=_params("parallel"),
    )(o, g_post.reshape(1, D), res, g_kv.reshape(1, D), g_pre.reshape(1, D))


def _rmsnorm_bwd(x, g, dy, *, name, adds=(), dx_dtype=F32, tr=256):
    L, D = x.shape
    tr = min(tr, L)
    dys = dy if isinstance(dy, tuple) else (dy,)
    n_dy, n_add = len(dys), len(adds)

    def body(*refs):
        x_ref, g_ref = refs[:2]
        dy_refs = refs[2:2 + n_dy]
        add_refs = refs[2 + n_dy:2 + n_dy + n_add]
        dx_ref, dg_ref = refs[2 + n_dy + n_add:]
        xf = x_ref[...]
        dyf = dy_refs[0][...].astype(F32)
        for d_ref in dy_refs[1:]:
            dyf = dyf + d_ref[...].astype(F32)
        r = lax.rsqrt(jnp.mean(xf * xf, axis=-1, keepdims=True) + EPS)
        gy = dyf * g_ref[...]
        c = jnp.mean(xf * gy, axis=-1, keepdims=True) * (r * r * r)
        dx = gy * r - xf * c
        for a_ref in add_refs:
            dx = dx + a_ref[...].astype(F32)
        dx_ref[...] = dx.astype(dx_ref.dtype)

        @pl.when(pl.program_id(0) == 0)
        def _():
            dg_ref[...] = jnp.zeros_like(dg_ref)

        dg_ref[...] += jnp.sum(dyf * xf * r, axis=0, keepdims=True)

    row = pl.BlockSpec((tr, D), lambda i: (i, 0))
    vec = pl.BlockSpec((1, D), lambda i: (0, 0))
    dx, dg = pl.pallas_call(
        body, name=name,
        out_shape=(jax.ShapeDtypeStruct((L, D), dx_dtype), jax.ShapeDtypeStruct((1, D), F32)),
        grid=(L // tr,), in_specs=[row, vec] + [row] * (n_dy + n_add), out_specs=(row, vec),
        compiler_params=_params("arbitrary"),
    )(x, g.reshape(1, D), *dys, *adds)
    return dx, dg.reshape(D)


def _rmsnorm_bwd_pair(x, g1, dy1, g2, dy2, *, name, adds=(), tr=256):
    L, D = x.shape
    tr = min(tr, L)
    dy1s = dy1 if isinstance(dy1, tuple) else (dy1,)
    n1, n_add = len(dy1s), len(adds)

    def body(*refs):
        x_ref, g1_ref, g2_ref = refs[:3]
        dy1_refs = refs[3:3 + n1]
        dy2_ref = refs[3 + n1]
        add_refs = refs[4 + n1:4 + n1 + n_add]
        dx_ref, dg1_ref, dg2_ref = refs[4 + n1 + n_add:]
        xf = x_ref[...]
        d1 = dy1_refs[0][...].astype(F32)
        for d_ref in dy1_refs[1:]:
            d1 = d1 + d_ref[...].astype(F32)
        d2 = dy2_ref[...].astype(F32)
        r = lax.rsqrt(jnp.mean(xf * xf, axis=-1, keepdims=True) + EPS)
        gy = d1 * g1_ref[...] + d2 * g2_ref[...]
        c = jnp.mean(xf * gy, axis=-1, keepdims=True) * (r * r * r)
        dx = gy * r - xf * c
        for a_ref in add_refs:
            dx = dx + a_ref[...].astype(F32)
        dx_ref[...] = dx

        @pl.when(pl.program_id(0) == 0)
        def _():
            dg1_ref[...] = jnp.zeros_like(dg1_ref)
            dg2_ref[...] = jnp.zeros_like(dg2_ref)

        xr = xf * r
        dg1_ref[...] += jnp.sum(d1 * xr, axis=0, keepdims=True)
        dg2_ref[...] += jnp.sum(d2 * xr, axis=0, keepdims=True)

    row = pl.BlockSpec((tr, D), lambda i: (i, 0))
    vec = pl.BlockSpec((1, D), lambda i: (0, 0))
    dx, dg1, dg2 = pl.pallas_call(
        body, name=name,
        out_shape=(jax.ShapeDtypeStruct((L, D), F32), jax.ShapeDtypeStruct((1, D), F32),
                   jax.ShapeDtypeStruct((1, D), F32)),
        grid=(L // tr,), in_specs=[row, vec, vec] + [row] * (n1 + 1 + n_add), out_specs=(row, vec, vec),
        compiler_params=_params("arbitrary"),
    )(x, g1.reshape(1, D), g2.reshape(1, D), *dy1s, dy2, *adds)
    return dx, dg1.reshape(D), dg2.reshape(D)


def _final_norm_loss(o, g, res, target, *, tr=256):
    L, D = o.shape
    tr = min(tr, L)

    def body(o_ref, g_ref, r_ref, t_ref, dh_ref, loss_ref):
        xf = o_ref[...]
        r = lax.rsqrt(jnp.mean(xf * xf, axis=-1, keepdims=True) + EPS)
        e = (r_ref[...] + xf * r * g_ref[...]) - t_ref[...]
        dh_ref[...] = e * (1.0 / D)

        @pl.when(pl.program_id(0) == 0)
        def _():
            loss_ref[...] = jnp.zeros_like(loss_ref)

        loss_ref[...] += jnp.sum(e * e, axis=0, keepdims=True) * (0.5 / D)

    row = pl.BlockSpec((tr, D), lambda i: (i, 0))
    vec = pl.BlockSpec((1, D), lambda i: (0, 0))
    dh, lp = pl.pallas_call(
        body, name="post_norm_1_loss",
        out_shape=(jax.ShapeDtypeStruct((L, D), F32), jax.ShapeDtypeStruct((1, D), F32)),
        grid=(L // tr,), in_specs=[row, vec, row, row], out_specs=(row, vec),
        compiler_params=_params("arbitrary"),
    )(o, g.reshape(1, D), res, target)
    return dh, lp


def _s5_coeffs(lr, li, ls):
    dt = jnp.exp(ls)
    mag = jnp.exp(lr * dt)
    ar = mag * jnp.cos(li * dt)
    ai = mag * jnp.sin(li * dt)
    den = lr * lr + li * li
    cr = ((ar - 1.0) * lr + ai * li) / den
    ci = (ai * lr - (ar - 1.0) * li) / den
    return dt, ar, ai, den, cr, ci


def _s5_prep(lam_re, lam_im, log_step, b_re_t, b_im_t):
    G, P = lam_re.shape
    H = b_re_t.shape[1]

    def body(lr_ref, li_ref, ls_ref, br_ref, bi_ref, ar_ref, ai_ref, bbr_ref, bbi_ref):
        _, ar, ai, _, cr, ci = _s5_coeffs(lr_ref[...], li_ref[...], ls_ref[...])
        ar_ref[...] = ar
        ai_ref[...] = ai
        br, bi = br_ref[...], bi_ref[...]
        crb, cib = cr[:, None, :], ci[:, None, :]
        bbr_ref[...] = crb * br - cib * bi
        bbi_ref[...] = crb * bi + cib * br

    return pl.pallas_call(
        body, name="s5_prep",
        out_shape=(jax.ShapeDtypeStruct((G, P), F32), jax.ShapeDtypeStruct((G, P), F32),
                   jax.ShapeDtypeStruct((G, H, P), F32), jax.ShapeDtypeStruct((G, H, P), F32)),
        compiler_params=_params(),
    )(lam_re, lam_im, log_step.reshape(G, 1), b_re_t, b_im_t)


def _s5_prep_bwd(lam_re, lam_im, log_step, b_re_t, b_im_t, d_ar, d_ai, d_bbr, d_bbi):
    G, P = lam_re.shape
    H = b_re_t.shape[1]

    def body(lr_ref, li_ref, ls_ref, br_ref, bi_ref, dar_ref, dai_ref, dbbr_ref, dbbi_ref,
             dlr_ref, dli_ref, dls_ref, dbr_ref, dbi_ref):
        lr, li = lr_ref[...], li_ref[...]
        dt, ar, ai, den, cr, ci = _s5_coeffs(lr, li, ls_ref[...])
        br, bi = br_ref[...], bi_ref[...]
        gbr, gbi = dbbr_ref[...], dbbi_ref[...]
        crb, cib = cr[:, None, :], ci[:, None, :]
        dbr_ref[...] = crb * gbr + cib * gbi
        dbi_ref[...] = crb * gbi - cib * gbr
        gcr = jnp.sum(br * gbr + bi * gbi, axis=1)
        gci = jnp.sum(br * gbi - bi * gbr, axis=1)
        ilr, ili = lr / den, -li / den
        gar = dar_ref[...] + (ilr * gcr + ili * gci)
        gai = dai_ref[...] + (ilr * gci - ili * gcr)
        qr, qi = cr * ilr - ci * ili, cr * ili + ci * ilr
        glr = -(qr * gcr + qi * gci)
        gli = -(qr * gci - qi * gcr)
        glr = glr + dt * (ar * gar + ai * gai)
        gli = gli + dt * (ar * gai - ai * gar)
        wr, wi = lr * ar - li * ai, lr * ai + li * ar
        gdt = jnp.sum(wr * gar + wi * gai, axis=1, keepdims=True)
        dlr_ref[...] = glr
        dli_ref[...] = gli
        dls_ref[...] = gdt * dt

    return pl.pallas_call(
        body, name="s5_prep_bwd",
        out_shape=(jax.ShapeDtypeStruct((G, P), F32), jax.ShapeDtypeStruct((G, P), F32),
                   jax.ShapeDtypeStruct((G, 1), F32),
                   jax.ShapeDtypeStruct((G, H, P), F32), jax.ShapeDtypeStruct((G, H, P), F32)),
        compiler_params=_params(),
    )(lam_re, lam_im, log_step.reshape(G, 1), b_re_t, b_im_t, d_ar, d_ai, d_bbr, d_bbi)


def _s5_block_mats(bbr_t, bbi_t, c_re, c_im):
    bmat = _s5_expand(bbr_t, bbi_t)
    cmat = jnp.transpose(_s5_expand(c_re, -c_im), (0, 2, 1))
    return bmat.astype(BF16), cmat.astype(BF16)


def _s5_diag_mask():
    r = lax.broadcasted_iota(jnp.int32, (LANES, 2 * STATE_COLS), 0) // SSM_GROUP
    c = (lax.broadcasted_iota(jnp.int32, (LANES, 2 * STATE_COLS), 1) % STATE_COLS) // SSM_STATE
    return (r == c).astype(F32)


def _s5_expand(re, im):
    re = jnp.tile(re.reshape(SSM_BLOCKS, LANES, SSM_STATE), (1, 1, GROUPS_PER_BLOCK))
    im = jnp.tile(im.reshape(SSM_BLOCKS, LANES, SSM_STATE), (1, 1, GROUPS_PER_BLOCK))
    return jnp.concatenate([re, im], axis=-1) * _s5_diag_mask()[None]


def _s5_unfold(dmat):
    d = dmat.reshape(SSM_GROUPS, SSM_GROUP, 2, SSM_STATE)
    return jnp.transpose(d, (2, 0, 1, 3))


def _s5_a_rows(ar, ai):
    a = jnp.concatenate([ar.reshape(SSM_BLOCKS, STATE_COLS), ai.reshape(SSM_BLOCKS, STATE_COLS)], axis=1)
    return jnp.broadcast_to(a[:, None, :], (SSM_BLOCKS, SUBLANES, 2 * STATE_COLS))


def _to_step_major(src_ref, dst_ref, seg):
    for s in range(SUBLANES):
        dst_ref[pl.ds(s, seg, stride=SUBLANES), :] = src_ref[pl.ds(seg * s, seg), :]


def _segment_rows(ref, s, seg):
    return ref[pl.ds(s, seg, stride=SUBLANES), :]


def _cmul(ar, ai, xr, xi):
    return ar * xr - ai * xi, ar * xi + ai * xr


def _s5_tables(a_ref, pw_s, pwr_s, S, seg):
    ar, ai = a_ref[:, :S], a_ref[:, S:]

    def step(i, c):
        pr, pi = c
        pw_s[i, :, :S] = pr
        pw_s[i, :, S:] = pi
        nr, ni = _cmul(ar, ai, pr, pi)
        pwr_s[seg - 1 - i, :, :S] = nr
        pwr_s[seg - 1 - i, :, S:] = ni
        return nr, ni

    pr, pi = lax.fori_loop(0, seg, step, (jnp.ones_like(ar), jnp.zeros_like(ai)))
    pw_s[seg, :, :S] = pr
    pw_s[seg, :, S:] = pi


def _s5_fwd(proj, bmat, cmat, a_rows, d_skip, *, tc=512):
    L = proj.shape[0]
    tc = min(tc, L)
    nt = L // tc
    seg = tc // SUBLANES
    S = STATE_COLS

    def body(u_ref, b_ref, c_ref, a_ref, d_ref, y_ref, yg_ref, xp_ref,
             bu_s, xp_s, pw_s, pwr_s, carry_s, e_s, up_s, yc_s):
        @pl.when(pl.program_id(1) == 0)
        def _():
            carry_s[...] = jnp.zeros_like(carry_s)
            _s5_tables(a_ref, pw_s, pwr_s, S, seg)

        ar, ai = a_ref[:, :S], a_ref[:, S:]
        _to_step_major(u_ref, up_s, seg)
        bu = jnp.dot(up_s[...].astype(BF16), b_ref[...], preferred_element_type=F32)
        bu_s[...] = bu.reshape(seg, SUBLANES, 2 * S)

        def step(i, carry):
            cr, ci = carry
            xp_s[i, :, :S] = cr
            xp_s[i, :, S:] = ci
            return ar * cr - ai * ci + bu_s[i, :, :S], ar * ci + ai * cr + bu_s[i, :, S:]

        zero = jnp.zeros((SUBLANES, S), F32)
        fr, fi = lax.fori_loop(0, seg, step, (zero, zero))
        pr, pi = pw_s[seg, 0:1, :S], pw_s[seg, 0:1, S:]
        er, ei = carry_s[0:1, :S], carry_s[0:1, S:]
        for s in range(SUBLANES):
            e_s[s:s + 1, :S] = er
            e_s[s:s + 1, S:] = ei
            tr, ti = _cmul(pr, pi, er, ei)
            er, ei = fr[s:s + 1] + tr, fi[s:s + 1] + ti
        carry_s[0:1, :S] = er
        carry_s[0:1, S:] = ei
        pw = pw_s[0:seg]
        tr, ti = _cmul(pw[:, :, :S], pw[:, :, S:], e_s[:, :S][None], e_s[:, S:][None])
        xl = xp_s[...]
        xp = jnp.concatenate([xl[:, :, :S] + tr, xl[:, :, S:] + ti], axis=-1).reshape(tc, 2 * S)
        xp_ref[...] = xp
        a1r, a1i = ar[0:1], ai[0:1]
        x_re = a1r * xp[:, :S] - a1i * xp[:, S:] + bu[:, :S]
        x_im = a1r * xp[:, S:] + a1i * xp[:, :S] + bu[:, S:]
        xs = jnp.concatenate([x_re, x_im], axis=1).astype(BF16)
        yc_s[...] = jnp.dot(xs, c_ref[...], preferred_element_type=F32)
        for s in range(SUBLANES):
            rows = pl.ds(seg * s, seg)
            y = _segment_rows(yc_s, s, seg) + d_ref[...] * u_ref[rows, :]
            y_ref[rows, :] = y
            yg_ref[rows, :] = _gelu(y).astype(BF16)

    return pl.pallas_call(
        body, name="s5_fwd",
        out_shape=(jax.ShapeDtypeStruct((L, MAIN_WIDTH), F32),
                   jax.ShapeDtypeStruct((L, MAIN_WIDTH), BF16),
                   jax.ShapeDtypeStruct((L, SSM_BLOCKS * 2 * S), F32)),
        grid=(SSM_BLOCKS, nt),
        in_specs=[pl.BlockSpec((tc, LANES), lambda b, t: (t, b)),
                  pl.BlockSpec((None, LANES, 2 * S), lambda b, t: (b, 0, 0)),
                  pl.BlockSpec((None, 2 * S, LANES), lambda b, t: (b, 0, 0)),
                  pl.BlockSpec((None, SUBLANES, 2 * S), lambda b, t: (b, 0, 0)),
                  pl.BlockSpec((1, LANES), lambda b, t: (0, b))],
        out_specs=(pl.BlockSpec((tc, LANES), lambda b, t: (t, b)),
                   pl.BlockSpec((tc, LANES), lambda b, t: (t, b)),
                   pl.BlockSpec((tc, 2 * S), lambda b, t: (t, b))),
        scratch_shapes=[pltpu.VMEM((seg, SUBLANES, 2 * S), F32),
                        pltpu.VMEM((seg, SUBLANES, 2 * S), F32),
                        pltpu.VMEM((seg + 1, SUBLANES, 2 * S), F32),
                        pltpu.VMEM((seg, SUBLANES, 2 * S), F32),
                        pltpu.VMEM((SUBLANES, 2 * S), F32),
                        pltpu.VMEM((SUBLANES, 2 * S), F32),
                        pltpu.VMEM((tc, LANES), F32),
                        pltpu.VMEM((tc, LANES), F32)],
        compiler_params=_params("parallel", "arbitrary"),
    )(proj, bmat, cmat, a_rows, d_skip.reshape(1, MAIN_WIDTH))


def _s5_bwd(proj, dyg_a, dyg_b, y, xp, bmat, cmat, a_rows, d_skip, dproj, *, tc=512):
    L = proj.shape[0]
    tc = min(tc, L)
    nt = L // tc
    seg = tc // SUBLANES
    S = STATE_COLS
    nn = (((1,), (1,)), ((), ()))
    tn = (((0,), (0,)), ((), ()))

    def fold_diagonal(acc_ref, mask_ref, fold_ref):
        x = acc_ref[...] * mask_ref[...]
        hi = x.astype(BF16)
        rest = x - hi.astype(F32)
        mid = rest.astype(BF16)
        low = (rest - mid.astype(F32)).astype(BF16)
        return sum(jnp.dot(piece, fold_ref[...], preferred_element_type=F32) for piece in (hi, mid, low))

    def body(u_ref, dyga_ref, dygb_ref, y_ref, xp_ref, b_ref, c_ref, a_ref, d_ref, mask_ref, fold_ref, dp_hbm,
             du_ref, dbd_ref, dcd_ref, da_ref, dd_ref,
             dl_s, pw_s, pwr_s, carry_s, e_s, up_s, dy_s, dyp_s, dup_s, db_ref, dc_ref):
        @pl.when(pl.program_id(1) == 0)
        def _():
            carry_s[...] = jnp.zeros_like(carry_s)
            db_ref[...] = jnp.zeros_like(db_ref)
            dc_ref[...] = jnp.zeros_like(dc_ref)
            da_ref[...] = jnp.zeros_like(da_ref)
            dd_ref[...] = jnp.zeros_like(dd_ref)
            _s5_tables(a_ref, pw_s, pwr_s, S, seg)

        ar, ai = a_ref[:, :S], a_ref[:, S:]
        a1r, a1i = ar[0:1], ai[0:1]
        u = u_ref[...]
        dy = (dyga_ref[...] + dygb_ref[...]) * _gelu_grad(y_ref[...])
        dy_s[...] = dy
        xp = xp_ref[...]
        _to_step_major(u_ref, up_s, seg)
        _to_step_major(dy_s, dyp_s, seg)
        ubp = up_s[...].astype(BF16)
        dyp = dyp_s[...].astype(BF16)
        bu = jnp.dot(ubp, b_ref[...], preferred_element_type=F32)
        x_re = a1r * xp[:, :S] - a1i * xp[:, S:] + bu[:, :S]
        x_im = a1r * xp[:, S:] + a1i * xp[:, :S] + bu[:, S:]
        xs = jnp.concatenate([x_re, x_im], axis=1).astype(BF16)
        dc_ref[...] += lax.dot_general(dyp, xs, tn, preferred_element_type=F32)
        dx = lax.dot_general(dyp, c_ref[...], nn, preferred_element_type=F32)
        dl_s[...] = dx.reshape(seg, SUBLANES, 2 * S)

        def step(k, carry):
            cr, ci = carry
            i = seg - 1 - k
            lr = dl_s[i, :, :S] + (ar * cr + ai * ci)
            li = dl_s[i, :, S:] + (ar * ci - ai * cr)
            dl_s[i, :, :S] = lr
            dl_s[i, :, S:] = li
            return lr, li

        zero = jnp.zeros((SUBLANES, S), F32)
        fr, fi = lax.fori_loop(0, seg, step, (zero, zero))
        pr, pi = pw_s[seg, 0:1, :S], pw_s[seg, 0:1, S:]
        er, ei = carry_s[0:1, :S], carry_s[0:1, S:]
        for s in range(SUBLANES - 1, -1, -1):
            e_s[s:s + 1, :S] = er
            e_s[s:s + 1, S:] = ei
            er, ei = fr[s:s + 1] + (pr * er + pi * ei), fi[s:s + 1] + (pr * ei - pi * er)
        carry_s[0:1, :S] = er
        carry_s[0:1, S:] = ei
        er, ei = e_s[:, :S][None], e_s[:, S:][None]
        pw = pwr_s[...]
        pwr, pwi = pw[:, :, :S], pw[:, :, S:]
        ll = dl_s[...]
        lam = jnp.concatenate([ll[:, :, :S] + (pwr * er + pwi * ei), ll[:, :, S:] + (pwr * ei - pwi * er)],
                              axis=-1).reshape(tc, 2 * S)
        l_re, l_im = lam[:, :S], lam[:, S:]
        da_ref[0:1, :S] += jnp.sum(l_re * xp[:, :S] + l_im * xp[:, S:], axis=0, keepdims=True)
        da_ref[0:1, S:] += jnp.sum(l_im * xp[:, :S] - l_re * xp[:, S:], axis=0, keepdims=True)
        lamb = lam.astype(BF16)
        dup_s[...] = lax.dot_general(lamb, b_ref[...], nn, preferred_element_type=F32)
        for s in range(SUBLANES):
            rows = pl.ds(seg * s, seg)
            du = _segment_rows(dup_s, s, seg) + d_ref[...] * dy_s[rows, :]
            du_ref[rows, :] = du.astype(du_ref.dtype)
        db_ref[...] += lax.dot_general(ubp, lamb, tn, preferred_element_type=F32)
        dd_ref[0:1, :] += jnp.sum(dy * u, axis=0, keepdims=True)

        @pl.when(pl.program_id(1) == nt - 1)
        def _():
            dbd_ref[...] = fold_diagonal(db_ref, mask_ref, fold_ref)
            dcd_ref[...] = fold_diagonal(dc_ref, mask_ref, fold_ref)

    rev = lambda b, t: (nt - 1 - t, b)
    col = jnp.arange(2 * S)
    fold = ((col // S * SSM_STATE + col % SSM_STATE)[:, None] == jnp.arange(LANES)[None, :]).astype(BF16)
    return pl.pallas_call(
        body, name="s5_bwd",
        out_shape=(jax.ShapeDtypeStruct(dproj.shape, dproj.dtype),
                   jax.ShapeDtypeStruct((SSM_BLOCKS, LANES, LANES), F32),
                   jax.ShapeDtypeStruct((SSM_BLOCKS, LANES, LANES), F32),
                   jax.ShapeDtypeStruct((SSM_BLOCKS, SUBLANES, 2 * S), F32),
                   jax.ShapeDtypeStruct((SUBLANES, MAIN_WIDTH), F32)),
        input_output_aliases={11: 0},
        grid=(SSM_BLOCKS, nt),
        in_specs=[pl.BlockSpec((tc, LANES), rev),
                  pl.BlockSpec((tc, LANES), rev),
                  pl.BlockSpec((tc, LANES), rev),
                  pl.BlockSpec((tc, LANES), rev),
                  pl.BlockSpec((tc, 2 * S), rev),
                  pl.BlockSpec((None, LANES, 2 * S), lambda b, t: (b, 0, 0)),
                  pl.BlockSpec((None, 2 * S, LANES), lambda b, t: (b, 0, 0)),
                  pl.BlockSpec((None, SUBLANES, 2 * S), lambda b, t: (b, 0, 0)),
                  pl.BlockSpec((1, LANES), lambda b, t: (0, b)),
                  pl.BlockSpec((LANES, 2 * S), lambda b, t: (0, 0)),
                  pl.BlockSpec((2 * S, LANES), lambda b, t: (0, 0)),
                  _ANY],
        out_specs=(pl.BlockSpec((tc, LANES), rev),
                   pl.BlockSpec((None, LANES, LANES), lambda b, t: (b, 0, 0)),
                   pl.BlockSpec((None, LANES, LANES), lambda b, t: (b, 0, 0)),
                   pl.BlockSpec((None, SUBLANES, 2 * S), lambda b, t: (b, 0, 0)),
                   pl.BlockSpec((SUBLANES, LANES), lambda b, t: (0, b))),
        scratch_shapes=[pltpu.VMEM((seg, SUBLANES, 2 * S), F32),
                        pltpu.VMEM((seg + 1, SUBLANES, 2 * S), F32),
                        pltpu.VMEM((seg, SUBLANES, 2 * S), F32),
                        pltpu.VMEM((SUBLANES, 2 * S), F32),
                        pltpu.VMEM((SUBLANES, 2 * S), F32),
                        pltpu.VMEM((tc, LANES), F32),
                        pltpu.VMEM((tc, LANES), F32),
                        pltpu.VMEM((tc, LANES), F32),
                        pltpu.VMEM((tc, LANES), F32),
                        pltpu.VMEM((LANES, 2 * S), F32),
                        pltpu.VMEM((LANES, 2 * S), F32)],
        compiler_params=_params("parallel", "arbitrary"),
    )(proj, dyg_a, dyg_b, y, xp, bmat, cmat, a_rows, d_skip.reshape(1, MAIN_WIDTH), _s5_diag_mask(), fold, dproj)


_Z_COLS = slice(MAIN_WIDTH, 2 * MAIN_WIDTH)
_ZM_COLS = slice(2 * MAIN_WIDTH + MEM_WIDTH, IN_WIDTH)


def _proj_rows(tr):
    return pl.BlockSpec((tr, IN_WIDTH), lambda i: (i, 0))


def _row_specs(tr):
    main = pl.BlockSpec((tr, MAIN_WIDTH), lambda i: (i, 0))
    z = pl.BlockSpec((tr, MAIN_WIDTH), lambda i: (i, 1))
    zm = pl.BlockSpec((tr, MEM_WIDTH), lambda i: (i, IN_WIDTH // MEM_WIDTH - 1))
    mem = pl.BlockSpec((tr, MEM_WIDTH), lambda i: (i, 0))
    cat = pl.BlockSpec((tr, D_MODEL), lambda i: (i, 0))
    vec = pl.BlockSpec((1, MAIN_WIDTH), lambda i: (0, 0))
    return main, z, zm, mem, cat, vec


def _gate_a_fwd(y, t, b_glu, proj, o_mem, *, tr=256):
    L = y.shape[0]
    tr = min(tr, L)

    def body(y_ref, t_ref, b_ref, z_ref, zm_ref, om_ref, o_ref):
        yg = _gelu(y_ref[...])
        sz, _ = _silu_and_grad(z_ref[...])
        o_ref[:, :MAIN_WIDTH] = (yg * _sigmoid(t_ref[...] + b_ref[...]) * sz).astype(BF16)
        szm, _ = _silu_and_grad(zm_ref[...])
        o_ref[:, MAIN_WIDTH:] = (om_ref[...] * szm).astype(BF16)

    main, z, zm, mem, cat, vec = _row_specs(tr)
    return pl.pallas_call(
        body, name="gate_a_fwd", out_shape=jax.ShapeDtypeStruct((L, D_MODEL), BF16),
        grid=(L // tr,), in_specs=[main, main, vec, z, zm, mem], out_specs=cat,
        compiler_params=_params("parallel"),
    )(y, t, b_glu.reshape(1, MAIN_WIDTH), proj, proj, o_mem)


def _gate_a_bwd(dcat, y, t, b_glu, proj, o_mem, *, tr=256):
    L = y.shape[0]
    tr = min(tr, L)

    def body(dc_ref, y_ref, t_ref, b_ref, z_ref, zm_ref, om_ref,
             dp_ref, dt_ref, dyg_ref, dom_ref, db_ref):
        dmain = dc_ref[:, :MAIN_WIDTH]
        dmemo = dc_ref[:, MAIN_WIDTH:]
        yg = _gelu(y_ref[...])
        sg = _sigmoid(t_ref[...] + b_ref[...])
        sz, gz = _silu_and_grad(z_ref[...])
        dp_ref[:, _Z_COLS] = (dmain * (yg * sg) * gz).astype(BF16)
        dy2 = dmain * sz
        dyg_ref[...] = dy2 * sg
        dt = dy2 * yg * (sg * (1.0 - sg))
        dt_ref[...] = dt.astype(BF16)

        @pl.when(pl.program_id(0) == 0)
        def _():
            db_ref[...] = jnp.zeros_like(db_ref)

        db_ref[...] += jnp.sum(dt, axis=0, keepdims=True)
        szm, gzm = _silu_and_grad(zm_ref[...])
        dom_ref[...] = dmemo * szm
        dp_ref[:, _ZM_COLS] = (dmemo * om_ref[...] * gzm).astype(BF16)

    main, z, zm, mem, cat, vec = _row_specs(tr)
    outs = pl.pallas_call(
        body, name="gate_a_bwd",
        out_shape=(jax.ShapeDtypeStruct((L, IN_WIDTH), BF16),
                   jax.ShapeDtypeStruct((L, MAIN_WIDTH), BF16), jax.ShapeDtypeStruct((L, MAIN_WIDTH), F32),
                   jax.ShapeDtypeStruct((L, MEM_WIDTH), F32), jax.ShapeDtypeStruct((1, MAIN_WIDTH), F32)),
        grid=(L // tr,), in_specs=[cat, main, main, vec, z, zm, mem],
        out_specs=(_proj_rows(tr), main, main, mem, vec),
        compiler_params=_params("arbitrary"),
    )(dcat, y, t, b_glu.reshape(1, MAIN_WIDTH), proj, proj, o_mem)
    return outs


def _gate_b_fwd(att, proj, o_mem, *, tr=256):
    L = att.shape[0]
    tr = min(tr, L)

    def body(a_ref, z_ref, zm_ref, om_ref, o_ref):
        sz, _ = _silu_and_grad(z_ref[...])
        o_ref[:, :MAIN_WIDTH] = (a_ref[...] * sz).astype(BF16)
        szm, _ = _silu_and_grad(zm_ref[...])
        o_ref[:, MAIN_WIDTH:] = (om_ref[...] * szm).astype(BF16)

    main, z, zm, mem, cat, _ = _row_specs(tr)
    return pl.pallas_call(
        body, name="gate_b_fwd", out_shape=jax.ShapeDtypeStruct((L, D_MODEL), BF16),
        grid=(L // tr,), in_specs=[main, z, zm, mem], out_specs=cat,
        compiler_params=_params("parallel"),
    )(att, proj, proj, o_mem)


def _gate_b_bwd(dcat, att, proj, o_mem, *, tr=256):
    L = att.shape[0]
    tr = min(tr, L)

    def body(dc_ref, a_ref, z_ref, zm_ref, om_ref, da_ref, dp_ref, dom_ref, dl_ref):
        dmain = dc_ref[:, :MAIN_WIDTH]
        dmemo = dc_ref[:, MAIN_WIDTH:]
        att = a_ref[...]
        sz, gz = _silu_and_grad(z_ref[...])
        datt = dmain * sz
        da_ref[...] = datt
        dp_ref[:, _Z_COLS] = (dmain * att * gz).astype(BF16)
        szm, gzm = _silu_and_grad(zm_ref[...])
        dom_ref[...] = dmemo * szm
        dp_ref[:, _ZM_COLS] = (dmemo * om_ref[...] * gzm).astype(BF16)
        prod = datt * att
        for h in range(FOX_HEADS):
            dl_ref[h] = jnp.sum(prod[:, h * HEAD_DIM:(h + 1) * HEAD_DIM], axis=1, keepdims=True)

    main, z, zm, mem, cat, _ = _row_specs(tr)
    delta = pl.BlockSpec((FOX_HEADS, tr, 1), lambda i: (0, i, 0))
    return pl.pallas_call(
        body, name="gate_b_bwd",
        out_shape=(jax.ShapeDtypeStruct((L, MAIN_WIDTH), F32), jax.ShapeDtypeStruct((L, IN_WIDTH), BF16),
                   jax.ShapeDtypeStruct((L, MEM_WIDTH), F32), jax.ShapeDtypeStruct((FOX_HEADS, L, 1), F32)),
        grid=(L // tr,), in_specs=[cat, main, z, zm, mem], out_specs=(main, _proj_rows(tr), mem, delta),
        compiler_params=_params("parallel"),
    )(dcat, att, proj, proj, o_mem)


_MEM_Q_COL = (2 * MAIN_WIDTH) // HEAD_DIM
_NT = (((1,), (1,)), ((), ()))
_TN = (((0,), (0,)), ((), ()))


def _mem_probs(q_ref, k_ref):
    qs = (q_ref[...] * (HEAD_DIM ** -0.5)).astype(BF16)
    s = lax.dot_general(qs, k_ref[...].astype(BF16), _NT, preferred_element_type=F32)
    e = jnp.exp(s - jnp.max(s, axis=-1, keepdims=True))
    return qs, e / jnp.sum(e, axis=-1, keepdims=True)


def _mem_attn_fwd(proj, kvm, *, tq=2048):
    L = proj.shape[0]
    tq = min(tq, L)

    def body(q_ref, k_ref, v_ref, o_ref):
        _, p = _mem_probs(q_ref, k_ref)
        o_ref[...] = jnp.dot(p.astype(BF16), v_ref[...].astype(BF16), preferred_element_type=F32)

    return pl.pallas_call(
        body, name="mem_attn_fwd", out_shape=jax.ShapeDtypeStruct((L, MEM_WIDTH), F32),
        grid=(MEM_HEADS, L // tq),
        in_specs=[pl.BlockSpec((tq, HEAD_DIM), lambda h, i: (i, _MEM_Q_COL + h)),
                  pl.BlockSpec((N_MEM, HEAD_DIM), lambda h, i: (0, h)),
                  pl.BlockSpec((N_MEM, HEAD_DIM), lambda h, i: (0, MEM_HEADS + h))],
        out_specs=pl.BlockSpec((tq, HEAD_DIM), lambda h, i: (i, h)),
        compiler_params=_params("parallel", "parallel"),
    )(proj, kvm, kvm)


def _mem_attn_bwd(proj, kvm, do, dproj, *, tq=2048):
    L = proj.shape[0]
    tq = min(tq, L)

    def body(q_ref, k_ref, v_ref, do_ref, dp_hbm, dq_ref, dk_ref, dv_ref):
        @pl.when(pl.program_id(1) == 0)
        def _():
            dk_ref[...] = jnp.zeros_like(dk_ref)
            dv_ref[...] = jnp.zeros_like(dv_ref)

        qs, p = _mem_probs(q_ref, k_ref)
        dob = do_ref[...].astype(BF16)
        dp = lax.dot_general(dob, v_ref[...].astype(BF16), _NT, preferred_element_type=F32)
        ds = p * (dp - jnp.sum(p * dp, axis=-1, keepdims=True))
        dsb = ds.astype(BF16)
        dq = jnp.dot(dsb, k_ref[...].astype(BF16), preferred_element_type=F32) * (HEAD_DIM ** -0.5)
        dq_ref[...] = dq.astype(BF16)
        dk_ref[...] += lax.dot_general(dsb, qs, _TN, preferred_element_type=F32)
        dv_ref[...] += lax.dot_general(p.astype(BF16), dob, _TN, preferred_element_type=F32)

    dproj, dk, dv = pl.pallas_call(
        body, name="mem_attn_bwd",
        out_shape=(jax.ShapeDtypeStruct(dproj.shape, dproj.dtype),
                   jax.ShapeDtypeStruct((N_MEM, MEM_WIDTH), F32),
                   jax.ShapeDtypeStruct((N_MEM, MEM_WIDTH), F32)),
        grid=(MEM_HEADS, L // tq),
        in_specs=[pl.BlockSpec((tq, HEAD_DIM), lambda h, i: (i, _MEM_Q_COL + h)),
                  pl.BlockSpec((N_MEM, HEAD_DIM), lambda h, i: (0, h)),
                  pl.BlockSpec((N_MEM, HEAD_DIM), lambda h, i: (0, MEM_HEADS + h)),
                  pl.BlockSpec((tq, HEAD_DIM), lambda h, i: (i, h)),
                  _ANY],
        out_specs=(pl.BlockSpec((tq, HEAD_DIM), lambda h, i: (i, _MEM_Q_COL + h)),
                   pl.BlockSpec((N_MEM, HEAD_DIM), lambda h, i: (0, h)),
                   pl.BlockSpec((N_MEM, HEAD_DIM), lambda h, i: (0, h))),
        input_output_aliases={4: 0},
        compiler_params=_params("parallel", "arbitrary"),
    )(proj, kvm, kvm, do, dproj)
    return dproj, jnp.concatenate([dk, dv], axis=1)


def _tile_cumsum(x, row, reverse):
    for sh in (1, 2, 4):
        if reverse:
            x = x + jnp.where(row < SUBLANES - sh, pltpu.roll(x, SUBLANES - sh, 0), 0.0)
        else:
            x = x + jnp.where(row >= sh, pltpu.roll(x, sh, 0), 0.0)
    return x


def _fgate_fwd(pre, b_pad):
    L = pre.shape[0]
    n8 = L // SUBLANES

    def body(p_ref, b_ref, o_ref):
        row = lax.broadcasted_iota(jnp.int32, (SUBLANES, LANES), 0)
        b = b_ref[...]

        def step(i, carry):
            x = p_ref[i] + b
            logf = jnp.minimum(x, 0.0) - jnp.log(1.0 + jnp.exp(-jnp.abs(x)))
            t = _tile_cumsum(logf, row, False) + carry
            o_ref[i] = t
            return t[SUBLANES - 1:SUBLANES, :]

        lax.fori_loop(0, n8, step, jnp.zeros((1, LANES), F32))

    out = pl.pallas_call(
        body, name="fgate_fwd", out_shape=jax.ShapeDtypeStruct((n8, SUBLANES, LANES), F32),
        compiler_params=_params(),
    )(pre.reshape(n8, SUBLANES, LANES), b_pad.reshape(1, LANES))
    return out.reshape(L, LANES)


def _fgate_bwd(dfcum, pre, b_pad):
    L = pre.shape[0]
    n8 = L // SUBLANES

    def body(d_ref, p_ref, b_ref, o_ref, s_ref):
        row = lax.broadcasted_iota(jnp.int32, (SUBLANES, LANES), 0)
        b = b_ref[...]

        def step(k, carry):
            c, acc = carry
            i = n8 - 1 - k
            t = _tile_cumsum(d_ref[i], row, True) + c
            dpre = t * _sigmoid(-(p_ref[i] + b))
            o_ref[i] = dpre
            return t[0:1, :], acc + dpre

        _, acc = lax.fori_loop(0, n8, step, (jnp.zeros((1, LANES), F32), jnp.zeros((SUBLANES, LANES), F32)))
        s_ref[...] = jnp.sum(acc, axis=0, keepdims=True)

    dpre, db = pl.pallas_call(
        body, name="fgate_bwd",
        out_shape=(jax.ShapeDtypeStruct((n8, SUBLANES, LANES), F32), jax.ShapeDtypeStruct((1, LANES), F32)),
        compiler_params=_params(),
    )(dfcum.reshape(n8, SUBLANES, LANES), pre.reshape(n8, SUBLANES, LANES), b_pad.reshape(1, LANES))
    return dpre.reshape(L, LANES), db


FOX_BLOCK = 1024


def _fox_scores(qs, k, fk, diagonal, row0=0):
    s = lax.dot_general(qs, k, _NT, preferred_element_type=F32) - fk
    if diagonal:
        row = row0 + lax.broadcasted_iota(jnp.int32, s.shape, 0)
        col = lax.broadcasted_iota(jnp.int32, s.shape, 1)
        s = jnp.where(row >= col, s, NEG_BIG)
    return s


def _fox_diagonal_parts(tq):
    half = tq // 2
    return ((slice(0, half), half), (slice(half, tq), tq))


def _fox_specs(tq, L):
    nq = L // tq
    return dict(
        rows=lambda off: pl.BlockSpec((tq, HEAD_DIM), lambda h, i: (i, off + h)),
        seq=lambda off: pl.BlockSpec((L, HEAD_DIM), lambda h, i: (0, off + h)),
        col=pl.BlockSpec((None, None, tq, 1), lambda h, i: (h, i, 0, 0)),
        col_all=pl.BlockSpec((None, nq, tq, 1), lambda h, i: (h, 0, 0, 0)),
        row=pl.BlockSpec((None, None, 1, tq), lambda h, i: (h, i, 0, 0)),
        row_all=pl.BlockSpec((None, nq, 1, tq), lambda h, i: (h, 0, 0, 0)))


FOX_FWD_HEADS = 2
FOX_FWD_BLOCK = 1024


def _fox_fwd(proj, kv, fk):
    L = proj.shape[0]
    tq = min(FOX_FWD_BLOCK, L)
    nq = L // tq
    nh = FOX_FWD_HEADS
    W = nh * HEAD_DIM
    lse_shape = fk.shape[:2] + (fk.shape[3], 1)
    fk = fk.reshape(FOX_HEADS, nq, 1, tq)

    def body(q_ref, k_ref, v_ref, fk_ref, o_ref, lse_ref, m_s, l_s, acc_s):
        qi = pl.program_id(1)
        cols = [slice(a * HEAD_DIM, (a + 1) * HEAD_DIM) for a in range(nh)]
        qs = [(q_ref[:, cs] * (HEAD_DIM ** -0.5)).astype(BF16) for cs in cols]
        m_s[...] = jnp.full_like(m_s, NEG_BIG)
        l_s[...] = jnp.zeros_like(l_s)
        acc_s[...] = jnp.zeros_like(acc_s)

        def block(j, diagonal):
            r0 = pl.multiple_of(j * tq, tq)
            for a, cs in enumerate(cols):
                s = _fox_scores(qs[a], k_ref[pl.ds(r0, tq), cs], fk_ref[a, j], diagonal)
                m_new = jnp.maximum(m_s[a], jnp.max(s, axis=-1, keepdims=True))
                alpha = jnp.exp(m_s[a] - m_new)
                p = jnp.exp(s - m_new)
                l_s[a] = alpha * l_s[a] + jnp.sum(p, axis=-1, keepdims=True)
                acc_s[a] = alpha * acc_s[a] + jnp.dot(p.astype(BF16), v_ref[pl.ds(r0, tq), cs],
                                                      preferred_element_type=F32)
                m_s[a] = m_new

        def below(j, carry):
            block(j, False)
            return carry

        lax.fori_loop(0, qi, below, 0)
        block(qi, True)
        for a, cs in enumerate(cols):
            o_ref[:, cs] = acc_s[a] / l_s[a]
            lse_ref[a] = m_s[a] + jnp.log(l_s[a])

    att, lse = pl.pallas_call(
        body, name="fox_fwd",
        out_shape=(jax.ShapeDtypeStruct((L, MAIN_WIDTH), F32),
                   jax.ShapeDtypeStruct((FOX_HEADS, nq, tq, 1), F32)),
        grid=(FOX_HEADS // nh, nq),
        in_specs=[pl.BlockSpec((tq, W), lambda h, i: (i, h)),
                  pl.BlockSpec((L, W), lambda h, i: (0, h)),
                  pl.BlockSpec((L, W), lambda h, i: (0, FOX_HEADS // nh + h)),
                  pl.BlockSpec((nh, nq, 1, tq), lambda h, i: (h, 0, 0, 0))],
        out_specs=(pl.BlockSpec((tq, W), lambda h, i: (i, h)),
                   pl.BlockSpec((nh, None, tq, 1), lambda h, i: (h, i, 0, 0))),
        scratch_shapes=[pltpu.VMEM((nh, tq, 1), F32), pltpu.VMEM((nh, tq, 1), F32),
                        pltpu.VMEM((nh, tq, HEAD_DIM), F32)],
        compiler_params=_params("parallel", "parallel"),
    )(proj, kv, kv, fk)
    return att, lse.reshape(lse_shape)


def _fox_bwd(proj, kv, fk, lse, delta, datt, dproj):
    L = proj.shape[0]
    tq = min(FOX_BLOCK, L)
    nq = L // tq
    sp = _fox_specs(tq, L)

    def body(q_ref, k_ref, v_ref, fk_ref, lse_ref, dl_ref, do_ref, dp_hbm,
             dq_ref, dk_ref, dv_ref, dfq_ref, dfk_ref, dk_s, dv_s, df_s, dq_s, dfq_s):
        ki = pl.program_id(1)

        @pl.when(ki == 0)
        def _():
            dq_s[...] = jnp.zeros_like(dq_s)
            dfq_s[...] = jnp.zeros_like(dfq_s)

        k, v, fk = k_ref[...], v_ref[...], fk_ref[...]
        dk_s[...] = jnp.zeros_like(dk_s)
        dv_s[...] = jnp.zeros_like(dv_s)
        df_s[...] = jnp.zeros_like(df_s)

        def block(i, rows, width, diagonal):
            n = rows.stop - rows.start
            r0 = pl.multiple_of(i * tq + rows.start, n)
            qs = (q_ref[pl.ds(r0, n), :] * (HEAD_DIM ** -0.5)).astype(BF16)
            dob = do_ref[pl.ds(r0, n), :].astype(BF16)
            kw, vw = k[:width], v[:width]
            p = jnp.exp(_fox_scores(qs, kw, fk[:, :width], diagonal, rows.start) - lse_ref[i][rows])
            dp = lax.dot_general(dob, vw, _NT, preferred_element_type=F32)
            ds = p * (dp - dl_ref[i][rows])
            dsb = ds.astype(BF16)
            dv_s[:width] += lax.dot_general(p.astype(BF16), dob, _TN, preferred_element_type=F32)
            dk_s[:width] += lax.dot_general(dsb, qs, _TN, preferred_element_type=F32)
            df_s[:, :width] -= jnp.sum(ds, axis=0, keepdims=True)
            dq_s[i, rows] += jnp.dot(dsb, kw, preferred_element_type=F32)
            dfq_s[i, rows] += jnp.sum(ds, axis=1, keepdims=True)

        def above(i, carry):
            block(i, slice(0, tq), tq, False)
            return carry

        for rows, width in _fox_diagonal_parts(tq):
            block(ki, rows, width, True)
        lax.fori_loop(ki + 1, nq, above, 0)
        dk_ref[...] = dk_s[...].astype(BF16)
        dv_ref[...] = dv_s[...].astype(BF16)
        dfk_ref[...] = df_s[...]

        @pl.when(ki == nq - 1)
        def _():
            dq_ref[...] = (dq_s[...].reshape(L, HEAD_DIM) * (HEAD_DIM ** -0.5)).astype(BF16)
            dfq_ref[...] = dfq_s[...]

    return pl.pallas_call(
        body, name="fox_bwd",
        out_shape=(jax.ShapeDtypeStruct(dproj.shape, dproj.dtype),
                   jax.ShapeDtypeStruct((L, MAIN_WIDTH), BF16),
                   jax.ShapeDtypeStruct((L, MAIN_WIDTH), BF16),
                   jax.ShapeDtypeStruct((FOX_HEADS, nq, tq, 1), F32),
                   jax.ShapeDtypeStruct((FOX_HEADS, nq, 1, tq), F32)),
        grid=(FOX_HEADS, nq),
        in_specs=[sp["seq"](0), sp["rows"](0), sp["rows"](FOX_HEADS), sp["row"],
                  sp["col_all"], sp["col_all"], sp["seq"](0), _ANY],
        out_specs=(sp["seq"](0), sp["rows"](0), sp["rows"](0), sp["col_all"], sp["row"]),
        input_output_aliases={7: 0},
        scratch_shapes=[pltpu.VMEM((tq, HEAD_DIM), F32), pltpu.VMEM((tq, HEAD_DIM), F32),
                        pltpu.VMEM((1, tq), F32), pltpu.VMEM((nq, tq, HEAD_DIM), F32),
                        pltpu.VMEM((nq, tq, 1), F32)],
        compiler_params=_params("parallel", "arbitrary"),
    )(proj, kv, kv, fk, lse, delta, datt, dproj)


def _pad_lanes(a):
    return jnp.pad(a, ((0, 0), (0, LANES - a.shape[1])))


def _mem_branch_fwd(memn, w_mk, proj, tag):
    kvm = _mm(memn, w_mk, name="mem_kv_" + tag)
    return kvm, _mem_attn_fwd(proj, kvm)


def _mem_branch_bwd(mem, g, w_mk, proj, memn, kvm, do_mem, dproj, tag):
    dproj, dkvm = _mem_attn_bwd(proj, kvm, do_mem, dproj)
    dkvm = dkvm.astype(BF16)
    dw_mk = _mm(memn, dkvm, ta=True, name="dw_mem_kv_" + tag, out_dtype=BF16)
    dmemn = _mm(dkvm, w_mk, tb=True, name="dmemn_" + tag)
    _, dg = _rmsnorm_bwd(mem, g, dmemn, name="mem_norm_bwd_" + tag, dx_dtype=BF16)
    return dproj, dw_mk, dg


def _local_step(x, mem, target, w, fetch=None, grads_ready=None):
    if grads_ready is None:
        grads_ready = lambda group, grads, token: token
    L = x.shape[0]
    g = {}
    w = dict(w)

    b_re_t = jnp.transpose(w["b_re"], (0, 2, 1))
    b_im_t = jnp.transpose(w["b_im"], (0, 2, 1))
    ar, ai, bbr_t, bbi_t = _s5_prep(w["lam_re"], w["lam_im"], w["log_step"], b_re_t, b_im_t)
    bmat, cmat = _s5_block_mats(bbr_t, bbi_t, w["c_re"], w["c_im"])
    a_rows = _s5_a_rows(ar, ai)

    hn0 = _rmsnorm_fwd(x, w["pre_norm_g"][0], name="pre_norm_0", out_dtype=BF16)
    memn0 = _rmsnorm_fwd(mem, w["mem_norm_g"][0], name="mem_norm_0", out_dtype=BF16)
    memn1 = _rmsnorm_fwd(mem, w["mem_norm_g"][1], name="mem_norm_1", out_dtype=BF16)
    if fetch is not None:
        w.update(fetch("a", [hn0, memn0, memn1, bmat, cmat, a_rows]))
    proj_a = _mm(hn0, w["w_in_a"], name="in_proj_a")
    y, yg, xp = _s5_fwd(proj_a, bmat, cmat, a_rows, w["d_skip"])
    if fetch is not None:
        w.update(fetch("b", yg))
    t = _mm(yg, w["w_glu"], name="glu_proj")
    kvm0, om0 = _mem_branch_fwd(memn0, w["w_mem_kv"][0], proj_a, "0")
    cat0 = _gate_a_fwd(y, t, w["b_glu"], proj_a, om0)
    o0 = _mm(cat0, w["w_out"][0], name="out_proj_0")
    h1, kv_in, hn1 = _post_norm_and_next_norms(
        o0, w["post_norm_g"][0], x, w["kv_norm_g"], w["pre_norm_g"][1], name="post_norm_0_kv_pre_norm_1")

    if fetch is not None:
        w.update(fetch("c", kv_in))
    kv = _mm(kv_in, w["w_kv"], name="kv_proj", out_dtype=BF16)
    pre_f = _mm(kv_in, w["w_fgate"], name="fgate_proj")
    b_f = jnp.pad(w["b_fgate"], (0, LANES - FOX_HEADS))
    fcum = _fgate_fwd(pre_f, b_f)
    fc = jnp.transpose(fcum[:, :FOX_HEADS])
    tq = min(FOX_BLOCK, L)
    fk = fc.reshape(FOX_HEADS, L // tq, 1, tq)

    proj_b = _mm(hn1, w["w_in_b"], name="in_proj_b")
    att, lse = _fox_fwd(proj_b, kv, fk)
    kvm1, om1 = _mem_branch_fwd(memn1, w["w_mem_kv"][1], proj_b, "1")
    cat1 = _gate_b_fwd(att, proj_b, om1)
    o1 = _mm(cat1, w["w_out"][1], name="out_proj_1")
    dh2, loss_row = _final_norm_loss(o1, w["post_norm_g"][1], h1, target)

    do1, dpost1 = _rmsnorm_bwd(o1, w["post_norm_g"][1], dh2, name="post_norm_bwd_1", dx_dtype=BF16)
    dcat1 = _mm(do1, w["w_out"][1], tb=True, name="dcat_1", out_dtype=BF16)
    g["w_out_1"] = _mm(cat1, do1, ta=True, name="dw_out_1", out_dtype=BF16)
    datt, dproj_b, dom1, delta = _gate_b_bwd(dcat1, att, proj_b, om1)
    dproj_b, g["w_mem_kv_1"], dmemg1 = _mem_branch_bwd(mem, w["mem_norm_g"][1], w["w_mem_kv"][1], proj_b,
                                                      memn1, kvm1, dom1, dproj_b, "1")
    delta = delta.reshape(lse.shape)
    dproj_b, dk, dv, dfq, dfk = _fox_bwd(proj_b, kv, fk, lse, delta, datt, dproj_b)
    g["w_in_b"] = _mm(hn1, dproj_b, ta=True, name="dw_in_b", out_dtype=BF16, shards=N_CHIPS)
    dhn1 = _mm(dproj_b, w["w_in_b"], tb=True, name="dhn_1")

    dkv = jnp.concatenate([dk, dv], axis=1)
    g["w_kv"] = _mm(kv_in, dkv, ta=True, name="dw_kv", out_dtype=BF16, shards=N_CHIPS)
    dkv_in_a = _mm(dkv, w["w_kv"], tb=True, name="dkv_in_kv")
    dfcum = _pad_lanes(jnp.transpose(dfq.reshape(FOX_HEADS, L) + dfk.reshape(FOX_HEADS, L)))
    dpre_f, db_f = _fgate_bwd(dfcum, pre_f, b_f)
    g["b_fgate"] = db_f[0, :FOX_HEADS]
    g["w_fgate"] = _mm(kv_in, dpre_f, ta=True, name="dw_fgate")[:, :FOX_HEADS]
    dkv_in_b = _mm(dpre_f, w["w_fgate"], tb=True, name="dkv_in_fgate")
    dh1, g["kv_norm_g"], dpre1 = _rmsnorm_bwd_pair(h1, w["kv_norm_g"], (dkv_in_a, dkv_in_b), w["pre_norm_g"][1],
                                                   dhn1, adds=(dh2,), name="kv_pre_norm_bwd")
    dh1 = grads_ready("b", g, dh1)

    do0, dpost0 = _rmsnorm_bwd(o0, w["post_norm_g"][0], dh1, name="post_norm_bwd_0", dx_dtype=BF16)
    dcat0 = _mm(do0, w["w_out"][0], tb=True, name="dcat_0", out_dtype=BF16)
    g["w_out_0"] = _mm(cat0, do0, ta=True, name="dw_out_0", out_dtype=BF16)
    dcat0 = grads_ready("b_send", g, dcat0)
    dproj_a, dt, dyg_a, dom0, db_glu = _gate_a_bwd(dcat0, y, t, w["b_glu"], proj_a, om0)
    g["b_glu"] = db_glu[0]
    g["w_glu"] = _mm(yg, dt, ta=True, name="dw_glu", out_dtype=BF16)
    dyg_b = _mm(dt, w["w_glu"], tb=True, name="dyg")
    dproj_a, g["w_mem_kv_0"], dmemg0 = _mem_branch_bwd(mem, w["mem_norm_g"][0], w["w_mem_kv"][0], proj_a,
                                                      memn0, kvm0, dom0, dproj_a, "0")
    dyg_b = grads_ready("a1", g, dyg_b)
    dproj_a, db_blk, dc_blk, da_rows, dd_skip = _s5_bwd(proj_a, dyg_a, dyg_b, y, xp, bmat, cmat, a_rows,
                                                        w["d_skip"], dproj_a)
    dproj_a = grads_ready("a1_send", g, dproj_a)
    g["d_skip"] = dd_skip[0]
    g["w_in_a"] = _mm(hn0, dproj_a, ta=True, name="dw_in_a", out_dtype=BF16, shards=N_CHIPS)
    dproj_a = grads_ready("a2", g, dproj_a)
    dhn0 = _mm(dproj_a, w["w_in_a"], tb=True, name="dhn_0")
    grad_x, dpre0 = _rmsnorm_bwd(x, w["pre_norm_g"][0], dhn0, adds=(dh1,), name="pre_norm_bwd_0")

    dbb = _s5_unfold(db_blk)
    dcc = _s5_unfold(dc_blk)
    g["c_re"], g["c_im"] = dcc[0], -dcc[1]
    d_ar = da_rows[:, 0, :STATE_COLS].reshape(SSM_GROUPS, SSM_STATE)
    d_ai = da_rows[:, 0, STATE_COLS:].reshape(SSM_GROUPS, SSM_STATE)
    dlr, dli, dls, dbr_t, dbi_t = _s5_prep_bwd(w["lam_re"], w["lam_im"], w["log_step"], b_re_t, b_im_t,
                                               d_ar, d_ai, dbb[0], dbb[1])
    g["lam_re"], g["lam_im"], g["log_step"] = dlr, dli, dls[:, 0]
    g["b_re"] = jnp.transpose(dbr_t, (0, 2, 1))
    g["b_im"] = jnp.transpose(dbi_t, (0, 2, 1))
    g["pre_norm_g"] = jnp.stack([dpre0, dpre1])
    g["post_norm_g"] = jnp.stack([dpost0, dpost1])
    g["mem_norm_g"] = jnp.stack([dmemg0, dmemg1])
    return loss_row, grad_x, g


_MESH = pl.DeviceIdType.MESH
_ANY = pl.BlockSpec(memory_space=pl.ANY)


def _place():
    x, y, c = lax.axis_index("x"), lax.axis_index("y"), lax.axis_index("c")
    chips = [(1 - x, y), (x, 1 - y), (1 - x, 1 - y)]
    return x, y, c, chips


_HBM = pl.BlockSpec(memory_space=pltpu.HBM)
_SEM = pl.BlockSpec(memory_space=pltpu.SEMAPHORE)
_SIDE = pltpu.SideEffectType.DATAFLOW_SIDE_EFFECTING


def _in_hbm(a):
    return pltpu.with_memory_space_constraint(a, pltpu.HBM)


def _hbm_like(a):
    return pltpu.HBM(a.shape, a.dtype)


def _ici_copies(srcs, lands, send_sem, recv_sem, src_at, dst_at, wait_at, to_sibling=False):
    x, y, c, chips = _place()
    peers = [(x, y, 1 - c)] if to_sibling else [(cx, cy, c) for cx, cy in chips]
    m = len(peers)
    start, wait = [], []
    for i in range(len(srcs)):
        for k, (px, py, pc) in enumerate(peers):
            sem = dict(send_sem=send_sem.at[m * i + k], recv_sem=recv_sem.at[m * i + k],
                       device_id=(px, py, pc), device_id_type=_MESH)
            src = src_at(srcs[i], 2 * px + py, c)
            start.append(pltpu.make_async_remote_copy(src_ref=src, dst_ref=dst_at(lands[i], 2 * x + y, k, c), **sem))
            wait.append(pltpu.make_async_remote_copy(src_ref=src, dst_ref=wait_at(lands[i], 2 * px + py, k, c), **sem))
    return start, wait


def _route_peers(route):
    return 1 if len(route) == 4 else 3


_BLOCK_ROUTE = (lambda s, j, c: s, lambda l, me, k, c: l.at[me, c], lambda l, j, k, c: l.at[j, c])


def _ici_start(srcs, lands, token, route, *, name):
    n = len(srcs)

    def body(*refs):
        start, _ = _ici_copies(refs[:n], refs[n:2 * n], refs[2 * n + 1], refs[2 * n + 2], *route)
        for cp in start:
            cp.start()

    sems = pltpu.SemaphoreType.DMA((_route_peers(route) * n,))
    outs = pl.pallas_call(
        body, name=name,
        out_shape=(sems, sems, *[_hbm_like(a) for a in srcs], *[_hbm_like(a) for a in lands], _hbm_like(token)),
        in_specs=[_HBM] * (2 * n + 1), out_specs=(_SEM, _SEM, *[_HBM] * (2 * n + 1)),
        input_output_aliases={i: 2 + i for i in range(2 * n + 1)},
        compiler_params=pltpu.CompilerParams(has_side_effects=_SIDE),
    )(*[_in_hbm(a) for a in srcs], *[_in_hbm(a) for a in lands], _in_hbm(token))
    return (outs[0], outs[1], list(outs[2:2 + n]), list(outs[2 + n:2 + 2 * n])), outs[2 + 2 * n]


def _ici_wait(handle, after, route, *, name):
    send_sem, recv_sem, srcs, lands = handle
    n = len(srcs)
    after = list(after) if isinstance(after, (list, tuple)) else [after]

    def body(*refs):
        _, wait = _ici_copies(refs[:n], refs[n:2 * n], refs[2 * n], refs[2 * n + 1], *route)
        for cp in wait:
            cp.wait_send()
            cp.wait_recv()

    outs = pl.pallas_call(
        body, name=name,
        out_shape=(*[_hbm_like(a) for a in srcs], *[_hbm_like(a) for a in lands]),
        in_specs=[_HBM] * (2 * n) + [_SEM, _SEM] + [_ANY] * len(after), out_specs=tuple([_HBM] * (2 * n)),
        input_output_aliases={i: i for i in range(2 * n)},
        compiler_params=pltpu.CompilerParams(has_side_effects=_SIDE),
    )(*srcs, *lands, send_sem, recv_sem, *after)
    return list(outs[:n]), list(outs[n:])


_GATHER_ROUTE = (lambda s, j, c: s.at[c], lambda l, me, k, c: l.at[me, c], lambda l, j, k, c: l.at[j, c])
_SCATTER_ROUTE = (lambda s, j, c: s.at[j], lambda l, me, k, c: l.at[k], lambda l, j, k, c: l.at[k])
_SHARE_ROUTE = (lambda s, j, c: s, lambda l, me, k, c: l.at[c], lambda l, j, k, c: l.at[1 - c], True)
_SWAP_ROUTE = (lambda s, j, c: s.at[:, 1 - c], lambda l, me, k, c: l, lambda l, j, k, c: l, True)


def _gather_forward(lands, tag, own=False):
    n = len(lands)
    m = 4 if own else 3

    def body(*refs):
        ins, outs = refs[:n], refs[n:2 * n]
        send_sem, recv_sem = refs[2 * n:]
        x, y, c, chips = _place()
        slots = [2 * cx + cy for cx, cy in chips] + [2 * x + y]

        def copy(i, k, half):
            return pltpu.make_async_remote_copy(
                src_ref=ins[i].at[slots[k], half], dst_ref=outs[i].at[slots[k], half],
                send_sem=send_sem.at[m * i + k], recv_sem=recv_sem.at[m * i + k],
                device_id=(x, y, 1 - c), device_id_type=_MESH)

        copies = [copy(i, k, c) for i in range(n) for k in range(m)]
        for cp in copies:
            cp.start()
        for i in range(n):
            for k in range(m):
                copy(i, k, 1 - c).wait_recv()
        for cp in copies:
            cp.wait_send()

    return pl.pallas_call(
        body, name="gather_forward_to_sibling_" + tag,
        out_shape=[jax.ShapeDtypeStruct(a.shape, a.dtype) for a in lands],
        in_specs=[_ANY] * n, out_specs=[_ANY] * n,
        input_output_aliases={i: i for i in range(n)},
        scratch_shapes=[pltpu.SemaphoreType.DMA((m * n,)), pltpu.SemaphoreType.DMA((m * n,))],
    )(*lands)


def _swap_halves(grads, tag):
    n = len(grads)

    def body(*refs):
        ins, outs = refs[:n], refs[n:2 * n]
        send_sem, recv_sem = refs[2 * n:]
        x, y, c, _ = _place()
        copies = [pltpu.make_async_remote_copy(
            src_ref=ins[i].at[:, 1 - c], dst_ref=outs[i],
            send_sem=send_sem.at[i], recv_sem=recv_sem.at[i],
            device_id=(x, y, 1 - c), device_id_type=_MESH) for i in range(n)]
        for cp in copies:
            cp.start()
        for cp in copies:
            cp.wait()

    return pl.pallas_call(
        body, name="grad_swap_halves_" + tag,
        out_shape=[jax.ShapeDtypeStruct((N_CHIPS,) + g.shape[2:], g.dtype) for g in grads],
        in_specs=[_ANY] * n, out_specs=[_ANY] * n,
        scratch_shapes=[pltpu.SemaphoreType.DMA((n,)), pltpu.SemaphoreType.DMA((n,))],
    )(*grads)


def _sum_rows(h, C):
    return max(d for d in range(SUBLANES, h + 1, SUBLANES) if h % d == 0 and d * C <= 1 << 20)


SUM_STEPS = 4


def _pair_sums(gs, rs, c_idx, *, name):
    n = len(gs)
    rows = [g.shape[2] // SUM_STEPS for g in gs]

    def body(c_ref, *refs):
        for g_ref, r_ref, o_ref in zip(refs[:n], refs[n:2 * n], refs[2 * n:]):
            o_ref[...] = (g_ref[...].astype(F32) + r_ref[...].astype(F32)).astype(o_ref.dtype)

    return pl.pallas_call(
        body, name=name,
        out_shape=[jax.ShapeDtypeStruct((N_CHIPS,) + g.shape[2:], g.dtype) for g in gs],
        grid_spec=pltpu.PrefetchScalarGridSpec(
            num_scalar_prefetch=1, grid=(N_CHIPS, SUM_STEPS),
            in_specs=[pl.BlockSpec((None, None, tr, g.shape[3]), lambda j, i, s: (j, s[0], i, 0))
                      for g, tr in zip(gs, rows)]
            + [pl.BlockSpec((None, tr, g.shape[3]), lambda j, i, s: (j, i, 0)) for g, tr in zip(gs, rows)],
            out_specs=[pl.BlockSpec((None, tr, g.shape[3]), lambda j, i, s: (j, i, 0)) for g, tr in zip(gs, rows)]),
        compiler_params=_params("parallel", "parallel"),
    )(c_idx, *gs, *rs)


def _owner_sums(ss, rs, jc_idx, *, name):
    n = len(ss)
    rows = [s.shape[1] // SUM_STEPS for s in ss]

    def body(jc_ref, *refs):
        for s_ref, r_ref, m_ref, o_ref in zip(refs[:n], refs[n:2 * n], refs[2 * n:3 * n], refs[3 * n:]):
            acc = s_ref[...].astype(F32)
            for k in range(3):
                acc = acc + r_ref[k].astype(F32)
            m_ref[...] = acc
            o_ref[...] = acc

    outs = pl.pallas_call(
        body, name=name,
        out_shape=[jax.ShapeDtypeStruct(s.shape[1:], F32) for s in ss]
        + [jax.ShapeDtypeStruct((2,) + s.shape[1:], F32) for s in ss],
        grid_spec=pltpu.PrefetchScalarGridSpec(
            num_scalar_prefetch=1, grid=(SUM_STEPS,),
            in_specs=[pl.BlockSpec((None, tr, s.shape[2]), lambda i, p: (p[0], i, 0)) for s, tr in zip(ss, rows)]
            + [pl.BlockSpec((3, tr, s.shape[2]), lambda i, p: (0, i, 0)) for s, tr in zip(ss, rows)],
            out_specs=[pl.BlockSpec((tr, s.shape[2]), lambda i, p: (i, 0)) for s, tr in zip(ss, rows)]
            + [pl.BlockSpec((None, tr, s.shape[2]), lambda i, p: (p[1], i, 0)) for s, tr in zip(ss, rows)]),
        compiler_params=_params("parallel"),
    )(jc_idx, *ss, *rs)
    return outs[:n], outs[n:]


def _chip_sums(grads, c_idx, tag):
    views = [g.reshape(N_CHIPS, 2, g.shape[1] // 2, g.shape[2]) for g in grads]
    arrived = _swap_halves(views, tag)
    return _pair_sums(views, arrived, c_idx, name=f"grad_pair_sums_{tag}")


def _sum_devices(blocks):
    R = blocks.shape[2]
    tr = _sum_rows(R, 2 * N_CHIPS * LANES)

    def body(b_ref, o_ref):
        acc = b_ref[0, 0]
        for d in range(1, 2 * N_CHIPS):
            acc = acc + b_ref[d // 2, d % 2]
        o_ref[...] = acc

    return pl.pallas_call(
        body, name="sum_small_over_devices", out_shape=jax.ShapeDtypeStruct((R, LANES), F32),
        grid=(R // tr,),
        in_specs=[pl.BlockSpec((N_CHIPS, 2, tr, LANES), lambda i: (0, 0, i, 0))],
        out_specs=pl.BlockSpec((tr, LANES), lambda i: (i, 0)),
        compiler_params=_params("parallel"),
    )(blocks)


def _adamw(w, g, m, v, *, name):
    R, C = w.shape
    tr = max(d for d in range(SUBLANES, R + 1, SUBLANES)
             if R % d == 0 and 7 * 2 * d * C * 4 <= VMEM_LIMIT_BYTES // 2)

    def body(w_ref, g_ref, m_ref, v_ref, d_ref, nm_ref, nv_ref):
        g = g_ref[...]
        m = ADAM_B1 * m_ref[...] + (1.0 - ADAM_B1) * g
        v = ADAM_B2 * v_ref[...] + (1.0 - ADAM_B2) * (g * g)
        nm_ref[...] = m
        nv_ref[...] = v
        m_hat = m / (1.0 - ADAM_B1 ** ADAM_STEP)
        v_hat = v / (1.0 - ADAM_B2 ** ADAM_STEP)
        d_ref[...] = -ADAM_LR * (m_hat / (jnp.sqrt(v_hat) + ADAM_EPS) + ADAM_WD * w_ref[...])

    blk = pl.BlockSpec((tr, C), lambda i: (i, 0))
    sds = jax.ShapeDtypeStruct((R, C), F32)
    return pl.pallas_call(
        body, name=name, out_shape=(sds, sds, sds), grid=(R // tr,),
        in_specs=[blk] * 4, out_specs=(blk, blk, blk),
        compiler_params=_params("parallel"),
    )(w, g, m, v)


_TILE = SUBLANES * LANES


def _pack(arrays):
    rows = []
    for a in arrays:
        flat = a.reshape(-1)
        flat = jnp.pad(flat, (0, (-flat.shape[0]) % _TILE))
        rows.append(flat.reshape(-1, LANES))
    return jnp.concatenate(rows, axis=0)


def _unpack(buf, shapes):
    out, r = [], 0
    for s in shapes:
        size = math.prod(s)
        nr = -(-size // _TILE) * SUBLANES
        out.append(buf[r:r + nr].reshape(-1)[:size].reshape(s))
        r += nr
    return out


_BIG = ("w_in_a", "w_glu", "w_kv", "w_in_b", "w_mem_kv", "w_out")
_REPLICATED = ("pre_norm_g", "post_norm_g", "lam_re", "lam_im", "log_step", "b_re", "b_im", "c_re", "c_im",
               "kv_norm_g", "b_fgate", "mem_norm_g")
_SHARDED_SMALL = ("d_skip", "b_glu", "w_fgate")
_WEIGHTS = ("pre_norm_g", "post_norm_g", "w_in_a", "lam_re", "lam_im", "log_step", "b_re", "b_im", "c_re",
            "c_im", "d_skip", "w_glu", "b_glu", "kv_norm_g", "w_kv", "w_fgate", "b_fgate", "w_in_b",
            "mem_norm_g", "w_mem_kv", "w_out")


def _halves(a):
    return a.reshape(2, a.shape[0] // 2, a.shape[1])


def _unhalve(a):
    return a.reshape(N_CHIPS, 2 * a.shape[2], a.shape[3])


def _columns(a):
    return jnp.transpose(a, (1, 0, 2)).reshape(a.shape[1], N_CHIPS * a.shape[2])


def _columns_copy(a, *, name, tr=512):
    n, R, C = a.shape

    def body(i_ref, o_ref):
        o_ref[...] = i_ref[...]

    return pl.pallas_call(
        body, name=name, out_shape=jax.ShapeDtypeStruct((R, n * C), a.dtype), grid=(n, R // tr),
        in_specs=[pl.BlockSpec((None, tr, C), lambda j, i: (j, i, 0))],
        out_specs=pl.BlockSpec((tr, C), lambda j, i: (i, j)),
        compiler_params=_params("parallel", "parallel"),
    )(a)


def kernel(x, mem, pre_norm_g, post_norm_g, w_in_a, lam_re, lam_im, log_step, b_re, b_im, c_re, c_im, d_skip, w_glu, b_glu, kv_norm_g, w_kv, w_fgate, b_fgate, w_in_b, mem_norm_g, w_mem_kv, w_out, loss_target, m_pre_norm_g, m_post_norm_g, m_w_in_a, m_lam_re, m_lam_im, m_log_step, m_b_re, m_b_im, m_c_re, m_c_im, m_d_skip, m_w_glu, m_b_glu, m_kv_norm_g, m_w_kv, m_w_fgate, m_b_fgate, m_w_in_b, m_mem_norm_g, m_w_mem_kv, m_w_out, v_pre_norm_g, v_post_norm_g, v_w_in_a, v_lam_re, v_lam_im, v_log_step, v_b_re, v_b_im, v_c_re, v_c_im, v_d_skip, v_w_glu, v_b_glu, v_kv_norm_g, v_w_kv, v_w_fgate, v_b_fgate, v_w_in_b, v_mem_norm_g, v_w_mem_kv, v_w_out):
    a = dict(locals())
    xi, yi, ci = lax.axis_index("x"), lax.axis_index("y"), lax.axis_index("c")
    chip = 2 * xi + yi
    c_idx = jnp.reshape(ci, (1,)).astype(jnp.int32)
    jc_idx = jnp.stack([chip, ci]).astype(jnp.int32)

    vec = jnp.zeros((2 * SUBLANES, MAIN_WIDTH // N_CHIPS), F32)
    vec = vec.at[0].set(a["d_skip"][0]).at[1].set(a["b_glu"][0])
    def own_slot(gathered, parts):
        return [lax.dynamic_update_index_in_dim(g, p, chip, 0) for g, p in zip(gathered, parts)]

    travelling, token = {}, a["pre_norm_g"]

    def start_gather(tag, parts, token):
        lands = [lax.empty((N_CHIPS,) + p.shape, p.dtype) for p in parts]
        travelling[tag], token = _ici_start(parts, lands, token, _GATHER_ROUTE, name=f"gather_{tag}_start")
        return token

    token = start_gather("a", [_halves(a["w_in_a"][0].astype(BF16)), _halves(vec)], token)
    later = ("w_glu", "w_mem_kv", "w_out", "w_kv", "w_fgate", "w_in_b")
    token, *raw = lax.optimization_barrier((token, *[a[n] for n in later]))
    raw = dict(zip(later, raw))
    token = start_gather("b", [_halves(raw["w_glu"][0].astype(BF16)),
                               *[_halves(raw["w_mem_kv"][i].astype(BF16)) for i in range(2)],
                               *[_halves(raw["w_out"][i].astype(BF16)) for i in range(2)]], token)
    token = start_gather("c", [_halves(raw["w_kv"].astype(BF16)), _halves(_pad_lanes(raw["w_fgate"]).astype(BF16)),
                               _halves(raw["w_in_b"][0].astype(BF16))], token)

    small_names = _REPLICATED + _SHARDED_SMALL
    small_state = [_pack([a[prefix + n] for n in small_names]) for prefix in ("", "m_")]

    def fetch(tag, after):
        if tag == "a":
            after = list(after) + small_state
        parts, lands = _ici_wait(travelling[tag], after, _GATHER_ROUTE, name=f"gather_{tag}_wait")
        full = own_slot(_gather_forward(lands, tag), parts)
        if tag == "a":
            w_in_a, vecs = full
            w_in_a, *moments = lax.optimization_barrier((w_in_a, *[a["v_" + n] for n in small_names]))
            small_state.append(_pack(moments))
            return dict(w_in_a=_columns_copy(_unhalve(w_in_a), name="w_in_a_columns"),
                        d_skip=vecs[:, 0, 0, :].reshape(MAIN_WIDTH),
                        b_glu=vecs[:, 0, 1, :].reshape(MAIN_WIDTH))
        if tag == "b":
            w_glu, w_mk0, w_mk1, w_out0, w_out1 = full
            return dict(w_glu=w_glu.reshape(MAIN_WIDTH, MAIN_WIDTH),
                        w_mem_kv=[m.reshape(D_MODEL, 2 * MEM_WIDTH) for m in (w_mk0, w_mk1)],
                        w_out=[o.reshape(D_MODEL, D_MODEL) for o in (w_out0, w_out1)])
        w_kv, w_fg, w_in_b = full
        return dict(w_kv=_columns(_unhalve(w_kv)), w_fgate=w_fg.reshape(D_MODEL, LANES),
                    w_in_b=_columns(_unhalve(w_in_b)))

    early = ("mem_norm_g", "lam_re", "lam_im", "log_step", "b_re", "b_im", "c_re", "c_im")
    token, *held = lax.optimization_barrier((token, *[a[n] for n in early]))
    held = dict(zip(early, held))
    w = dict(
        pre_norm_g=token, post_norm_g=a["post_norm_g"], mem_norm_g=held["mem_norm_g"],
        kv_norm_g=a["kv_norm_g"], b_fgate=a["b_fgate"],
        **{n: held[n][0] for n in early[1:]})

    sent = {}

    swapping = {}

    def grads_ready(event, g, token):
        tag = event.split("_")[0]
        if event in ("b", "a1"):
            big = {"b": lambda: [g["w_kv"], g["w_in_b"], g["w_mem_kv_1"].reshape(N_CHIPS, -1, 2 * MEM_WIDTH),
                                 g["w_out_1"].reshape(N_CHIPS, -1, D_MODEL)],
                   "a1": lambda: [g["w_glu"].reshape(N_CHIPS, -1, MAIN_WIDTH),
                                  g["w_mem_kv_0"].reshape(N_CHIPS, -1, 2 * MEM_WIDTH),
                                  g["w_out_0"].reshape(N_CHIPS, -1, D_MODEL)]}[tag]()
            views = [b.reshape(N_CHIPS, 2, b.shape[1] // 2, b.shape[2]) for b in big]
            lands = [lax.empty((N_CHIPS,) + v.shape[2:], v.dtype) for v in views]
            swapping[tag], token = _ici_start(views, lands, token, _SWAP_ROUTE, name=f"grad_swap_{tag}_start")
            return token
        if event == "a2":
            sums = _chip_sums([g["w_in_a"]], c_idx, tag)
        else:
            views, arrived = _ici_wait(swapping[tag], token, _SWAP_ROUTE, name=f"grad_swap_{tag}_wait")
            sums = _pair_sums(views, arrived, c_idx, name=f"grad_pair_sums_{tag}")
        lands = [lax.empty((3,) + s.shape[1:], s.dtype) for s in sums]
        sent[tag], token = _ici_start(sums, lands, token, _SCATTER_ROUTE, name=f"grad_send_{tag}_start")
        return token

    loss_row, grad_x, g = _local_step(a["x"][0], a["mem"][0], a["loss_target"][0], w, fetch, grads_ready)

    pack = _pack([g[n] for n in small_names] + [loss_row])
    blocks = lax.empty((N_CHIPS, 2) + pack.shape, F32)
    small_sent, token = _ici_start([pack], [blocks], loss_row, _BLOCK_ROUTE, name="small_sums_start")

    sharing = {}
    for tag in ("b", "a1", "a2"):
        sums, arrived = _ici_wait(sent[tag], [grad_x, token], _SCATTER_ROUTE, name=f"grad_send_{tag}_wait")
        mine, bufs = _owner_sums(sums, arrived, jc_idx, name=f"grad_owner_sums_{tag}")
        sharing[tag], token = _ici_start(mine, bufs, token, _SHARE_ROUTE, name=f"grad_share_{tag}_start")

    def shared(tag, after):
        _, bufs = _ici_wait(sharing[tag], after, _SHARE_ROUTE, name=f"grad_share_{tag}_wait")
        return [b.reshape(-1, b.shape[2]) for b in bufs]

    grads, delta, new_m, new_v = {}, {}, {}, {}

    def adam(n):
        shape = a[n].shape
        d2 = (-1, shape[-1])
        d, m, v = _adamw(a[n].reshape(d2), grads[n].reshape(d2), a["m_" + n].reshape(d2),
                         a["v_" + n].reshape(d2), name="adamw_" + n)
        delta[n], new_m[n], new_v[n] = d.reshape(shape), m.reshape(shape), v.reshape(shape)
        return d

    r_kv, r_in_b, r_mk1, r_out1 = shared("b", token)
    grads["w_kv"], grads["w_in_b"] = r_kv, r_in_b[None]
    done = [adam("w_kv"), adam("w_in_b")]
    r_glu, r_mk0, r_out0 = shared("a1", done)
    grads["w_glu"], grads["w_mem_kv"], grads["w_out"] = r_glu[None], jnp.stack([r_mk0, r_mk1]), jnp.stack([r_out0, r_out1])
    done = [adam("w_glu"), adam("w_mem_kv"), adam("w_out")]
    (r_in_a,) = shared("a2", done)
    grads["w_in_a"] = r_in_a[None]
    adam("w_in_a")

    (pack,), (blocks,) = _ici_wait(small_sent, [delta[n] for n in _BIG], _BLOCK_ROUTE, name="small_sums_wait")
    blocks = lax.dynamic_update_slice(blocks, pack[None, None], (chip, ci, 0, 0))
    (blocks,) = _gather_forward([blocks], "small", own=True)
    *summed, loss_sums = _unpack(_sum_devices(blocks), [g[n].shape for n in small_names] + [loss_row.shape])
    small = dict(zip(small_names, summed))
    loss = jnp.sum(loss_sums)
    for n in _REPLICATED:
        grads[n] = small[n].reshape(a[n].shape)
    nd = MAIN_WIDTH // N_CHIPS
    grads["d_skip"] = lax.dynamic_slice(small["d_skip"], (chip * nd,), (nd,))[None]
    grads["b_glu"] = lax.dynamic_slice(small["b_glu"], (chip * nd,), (nd,))[None]
    nf = D_MODEL // N_CHIPS
    grads["w_fgate"] = lax.dynamic_slice(small["w_fgate"], (chip * nf, 0), (nf, FOX_HEADS))

    shapes = [a[n].shape for n in small_names]
    d, m, v = _adamw(small_state[0], _pack([grads[n] for n in small_names]), *small_state[1:], name="adamw_small")
    for n, dd, mm, vv in zip(small_names, _unpack(d, shapes), _unpack(m, shapes), _unpack(v, shapes)):
        delta[n], new_m[n], new_v[n] = dd, mm, vv

    return (loss, grad_x[None], *[grads[n] for n in _WEIGHTS], *[delta[n] for n in _WEIGHTS],
            *[new_m[n] for n in _WEIGHTS], *[new_v[n] for n in _WEIGHTS])
```

```python
import math

import jax
import jax.numpy as jnp
from jax import lax
from jax.experimental import pallas as pl
from jax.experimental.pallas import tpu as pltpu

F32 = jnp.float32
BF16 = jnp.bfloat16

D_MODEL = 2048
N_MEM = 256
MAIN_WIDTH = 1536
MEM_WIDTH = 512
IN_WIDTH = 2 * MAIN_WIDTH + 2 * MEM_WIDTH
HEAD_DIM = 128
FOX_HEADS = MAIN_WIDTH // HEAD_DIM
MEM_HEADS = MEM_WIDTH // HEAD_DIM
SSM_GROUP = 16
SSM_GROUPS = MAIN_WIDTH // SSM_GROUP
SSM_STATE = 64
GROUPS_PER_BLOCK = 8
SSM_BLOCKS = SSM_GROUPS // GROUPS_PER_BLOCK
STATE_COLS = GROUPS_PER_BLOCK * SSM_STATE
EPS = 1e-6
ADAM_LR = 0.001
ADAM_B1 = 0.9
ADAM_B2 = 0.999
ADAM_EPS = 1e-08
ADAM_WD = 0.01
ADAM_STEP = 10
N_CHIPS = 4
LANES = 128
SUBLANES = 8
VMEM_LIMIT_BYTES = 56 * 1024 * 1024
NEG_BIG = -1e30
MESH_AXES = ("x", "y", "c")


def _params(*sem):
    return pltpu.CompilerParams(dimension_semantics=sem if sem else None,
                                vmem_limit_bytes=VMEM_LIMIT_BYTES)


def _sigmoid(x):
    return 1.0 / (1.0 + jnp.exp(-x))


def _gelu(x):
    c = math.sqrt(2.0 / math.pi)
    return 0.5 * x * (1.0 + jnp.tanh(c * (x + 0.044715 * (x * x * x))))


def _gelu_grad(x):
    c = math.sqrt(2.0 / math.pi)
    t = jnp.tanh(c * (x + 0.044715 * (x * x * x)))
    return 0.5 * (1.0 + t) + 0.5 * x * (1.0 - t * t) * (c * (1.0 + 3.0 * 0.044715 * (x * x)))


def _silu_and_grad(z):
    s = _sigmoid(z)
    return z * s, s * (1.0 + z * (1.0 - s))


_TILE_CHOICES = (4096, 3072, 2048, 1536, 1024, 768, 512, 384, 256, LANES)


def _tile(n, cap):
    return next(c for c in _TILE_CHOICES if c <= cap and n % c == 0)


def _mm(a, b, *, name, ta=False, tb=False, out_dtype=F32, shards=1, tm=1024, tn=1024, tk=4096):
    if ta:
        K, M = a.shape
    else:
        M, K = a.shape
    if tb:
        N, kb = b.shape
    else:
        kb, N = b.shape
    assert K == kb, (a.shape, b.shape)
    ns = N // shards
    tm, tn, tk = _tile(M, tm), _tile(ns, tn), _tile(K, tk)
    assert M % tm == 0 and ns % tn == 0 and K % tk == 0 and N % shards == 0
    nk = K // tk
    dn = (((0 if ta else 1,), (1 if tb else 0,)), ((), ()))

    def body(a_ref, b_ref, o_ref, *acc):
        prod = lax.dot_general(a_ref[...].astype(BF16), b_ref[...].astype(BF16), dn, preferred_element_type=F32)
        if nk == 1:
            o_ref[...] = prod.astype(o_ref.dtype)
            return
        acc_ref, = acc
        k = pl.program_id(2)

        @pl.when(k == 0)
        def _():
            acc_ref[...] = jnp.zeros_like(acc_ref)

        acc_ref[...] += prod

        @pl.when(k == nk - 1)
        def _():
            o_ref[...] = acc_ref[...].astype(o_ref.dtype)

    a_spec = (pl.BlockSpec((tk, tm), lambda i, j, k: (k, i)) if ta
              else pl.BlockSpec((tm, tk), lambda i, j, k: (i, k)))
    b_spec = (pl.BlockSpec((tn, tk), lambda i, j, k: (j, k)) if tb
              else pl.BlockSpec((tk, tn), lambda i, j, k: (k, j)))
    if shards == 1:
        out_shape = jax.ShapeDtypeStruct((M, N), out_dtype)
        o_spec = pl.BlockSpec((tm, tn), lambda i, j, k: (i, j))
    else:
        nb = ns // tn
        out_shape = jax.ShapeDtypeStruct((shards, M, ns), out_dtype)
        o_spec = pl.BlockSpec((None, tm, tn), lambda i, j, k: (j // nb, i, j % nb))
    return pl.pallas_call(
        body, name=name, out_shape=out_shape,
        grid=(M // tm, N // tn, nk),
        in_specs=[a_spec, b_spec], out_specs=o_spec,
        scratch_shapes=[] if nk == 1 else [pltpu.VMEM((tm, tn), F32)],
        compiler_params=_params("parallel", "parallel", "arbitrary"),
    )(a, b)


def _rmsnorm_fwd(x, g, *, name, out_dtype=F32, tr=256):
    L, D = x.shape
    tr = min(tr, L)

    def body(x_ref, g_ref, o_ref):
        xf = x_ref[...]
        r = lax.rsqrt(jnp.mean(xf * xf, axis=-1, keepdims=True) + EPS)
        o_ref[...] = (xf * r * g_ref[...]).astype(o_ref.dtype)

    row = pl.BlockSpec((tr, D), lambda i: (i, 0))
    vec = pl.BlockSpec((1, D), lambda i: (0, 0))
    return pl.pallas_call(
        body, name=name, out_shape=jax.ShapeDtypeStruct((L, D), out_dtype),
        grid=(L // tr,), in_specs=[row, vec], out_specs=row,
        compiler_params=_params("parallel"),
    )(x, g.reshape(1, D))


def _post_norm_and_next_norms(o, g_post, res, g_kv, g_pre, *, name, tr=256):
    L, D = o.shape
    tr = min(tr, L)

    def body(o_ref, gp_ref, r_ref, gk_ref, gn_ref, h_ref, kv_ref, hn_ref):
        of = o_ref[...]
        r = lax.rsqrt(jnp.mean(of * of, axis=-1, keepdims=True) + EPS)
        h = r_ref[...] + of * r * gp_ref[...]
        h_ref[...] = h
        hr = h * lax.rsqrt(jnp.mean(h * h, axis=-1, keepdims=True) + EPS)
        kv_ref[...] = (hr * gk_ref[...]).astype(kv_ref.dtype)
        hn_ref[...] = (hr * gn_ref[...]).astype(hn_ref.dtype)

    row = pl.BlockSpec((tr, D), lambda i: (i, 0))
    vec = pl.BlockSpec((1, D), lambda i: (0, 0))
    return pl.pallas_call(
        body, name=name,
        out_shape=(jax.ShapeDtypeStruct((L, D), F32), jax.ShapeDtypeStruct((L, D), BF16),
                   jax.ShapeDtypeStruct((L, D), BF16)),
        grid=(L // tr,), in_specs=[row, vec, row, vec, vec], out_specs=(row, row, row),
        compiler_params=_params("parallel"),
    )(o, g_post.reshape(1, D), res, g_kv.reshape(1, D), g_pre.reshape(1, D))


def _rmsnorm_bwd(x, g, dy, *, name, adds=(), dx_dtype=F32, tr=256):
    L, D = x.shape
    tr = min(tr, L)
    dys = dy if isinstance(dy, tuple) else (dy,)
    n_dy, n_add = len(dys), len(adds)

    def body(*refs):
        x_ref, g_ref = refs[:2]
        dy_refs = refs[2:2 + n_dy]
        add_refs = refs[2 + n_dy:2 + n_dy + n_add]
        dx_ref, dg_ref = refs[2 + n_dy + n_add:]
        xf = x_ref[...]
        dyf = dy_refs[0][...].astype(F32)
        for d_ref in dy_refs[1:]:
            dyf = dyf + d_ref[...].astype(F32)
        r = lax.rsqrt(jnp.mean(xf * xf, axis=-1, keepdims=True) + EPS)
        gy = dyf * g_ref[...]
        c = jnp.mean(xf * gy, axis=-1, keepdims=True) * (r * r * r)
        dx = gy * r - xf * c
        for a_ref in add_refs:
            dx = dx + a_ref[...].astype(F32)
        dx_ref[...] = dx.astype(dx_ref.dtype)

        @pl.when(pl.program_id(0) == 0)
        def _():
            dg_ref[...] = jnp.zeros_like(dg_ref)

        dg_ref[...] += jnp.sum(dyf * xf * r, axis=0, keepdims=True)

    row = pl.BlockSpec((tr, D), lambda i: (i, 0))
    vec = pl.BlockSpec((1, D), lambda i: (0, 0))
    dx, dg = pl.pallas_call(
        body, name=name,
        out_shape=(jax.ShapeDtypeStruct((L, D), dx_dtype), jax.ShapeDtypeStruct((1, D), F32)),
        grid=(L // tr,), in_specs=[row, vec] + [row] * (n_dy + n_add), out_specs=(row, vec),
        compiler_params=_params("arbitrary"),
    )(x, g.reshape(1, D), *dys, *adds)
    return dx, dg.reshape(D)


def _rmsnorm_bwd_pair(x, g1, dy1, g2, dy2, *, name, adds=(), tr=256):
    L, D = x.shape
    tr = min(tr, L)
    dy1s = dy1 if isinstance(dy1, tuple) else (dy1,)
    n1, n_add = len(dy1s), len(adds)

    def body(*refs):
        x_ref, g1_ref, g2_ref = refs[:3]
        dy1_refs = refs[3:3 + n1]
        dy2_ref = refs[3 + n1]
        add_refs = refs[4 + n1:4 + n1 + n_add]
        dx_ref, dg1_ref, dg2_ref = refs[4 + n1 + n_add:]
        xf = x_ref[...]
        d1 = dy1_refs[0][...].astype(F32)
        for d_ref in dy1_refs[1:]:
            d1 = d1 + d_ref[...].astype(F32)
        d2 = dy2_ref[...].astype(F32)
        r = lax.rsqrt(jnp.mean(xf * xf, axis=-1, keepdims=True) + EPS)
        gy = d1 * g1_ref[...] + d2 * g2_ref[...]
        c = jnp.mean(xf * gy, axis=-1, keepdims=True) * (r * r * r)
        dx = gy * r - xf * c
        for a_ref in add_refs:
            dx = dx + a_ref[...].astype(F32)
        dx_ref[...] = dx

        @pl.when(pl.program_id(0) == 0)
        def _():
            dg1_ref[...] = jnp.zeros_like(dg1_ref)
            dg2_ref[...] = jnp.zeros_like(dg2_ref)

        xr = xf * r
        dg1_ref[...] += jnp.sum(d1 * xr, axis=0, keepdims=True)
        dg2_ref[...] += jnp.sum(d2 * xr, axis=0, keepdims=True)

    row = pl.BlockSpec((tr, D), lambda i: (i, 0))
    vec = pl.BlockSpec((1, D), lambda i: (0, 0))
    dx, dg1, dg2 = pl.pallas_call(
        body, name=name,
        out_shape=(jax.ShapeDtypeStruct((L, D), F32), jax.ShapeDtypeStruct((1, D), F32),
                   jax.ShapeDtypeStruct((1, D), F32)),
        grid=(L // tr,), in_specs=[row, vec, vec] + [row] * (n1 + 1 + n_add), out_specs=(row, vec, vec),
        compiler_params=_params("arbitrary"),
    )(x, g1.reshape(1, D), g2.reshape(1, D), *dy1s, dy2, *adds)
    return dx, dg1.reshape(D), dg2.reshape(D)


def _final_norm_loss(o, g, res, target, *, tr=256):
    L, D = o.shape
    tr = min(tr, L)

    def body(o_ref, g_ref, r_ref, t_ref, dh_ref, loss_ref):
        xf = o_ref[...]
        r = lax.rsqrt(jnp.mean(xf * xf, axis=-1, keepdims=True) + EPS)
        e = (r_ref[...] + xf * r * g_ref[...]) - t_ref[...]
        dh_ref[...] = e * (1.0 / D)

        @pl.when(pl.program_id(0) == 0)
        def _():
            loss_ref[...] = jnp.zeros_like(loss_ref)

        loss_ref[...] += jnp.sum(e * e, axis=0, keepdims=True) * (0.5 / D)

    row = pl.BlockSpec((tr, D), lambda i: (i, 0))
    vec = pl.BlockSpec((1, D), lambda i: (0, 0))
    dh, lp = pl.pallas_call(
        body, name="post_norm_1_loss",
        out_shape=(jax.ShapeDtypeStruct((L, D), F32), jax.ShapeDtypeStruct((1, D), F32)),
        grid=(L // tr,), in_specs=[row, vec, row, row], out_specs=(row, vec),
        compiler_params=_params("arbitrary"),
    )(o, g.reshape(1, D), res, target)
    return dh, lp


def _s5_coeffs(lr, li, ls):
    dt = jnp.exp(ls)
    mag = jnp.exp(lr * dt)
    ar = mag * jnp.cos(li * dt)
    ai = mag * jnp.sin(li * dt)
    den = lr * lr + li * li
    cr = ((ar - 1.0) * lr + ai * li) / den
    ci = (ai * lr - (ar - 1.0) * li) / den
    return dt, ar, ai, den, cr, ci


def _s5_prep(lam_re, lam_im, log_step, b_re_t, b_im_t):
    G, P = lam_re.shape
    H = b_re_t.shape[1]

    def body(lr_ref, li_ref, ls_ref, br_ref, bi_ref, ar_ref, ai_ref, bbr_ref, bbi_ref):
        _, ar, ai, _, cr, ci = _s5_coeffs(lr_ref[...], li_ref[...], ls_ref[...])
        ar_ref[...] = ar
        ai_ref[...] = ai
        br, bi = br_ref[...], bi_ref[...]
        crb, cib = cr[:, None, :], ci[:, None, :]
        bbr_ref[...] = crb * br - cib * bi
        bbi_ref[...] = crb * bi + cib * br

    return pl.pallas_call(
        body, name="s5_prep",
        out_shape=(jax.ShapeDtypeStruct((G, P), F32), jax.ShapeDtypeStruct((G, P), F32),
                   jax.ShapeDtypeStruct((G, H, P), F32), jax.ShapeDtypeStruct((G, H, P), F32)),
        compiler_params=_params(),
    )(lam_re, lam_im, log_step.reshape(G, 1), b_re_t, b_im_t)


def _s5_prep_bwd(lam_re, lam_im, log_step, b_re_t, b_im_t, d_ar, d_ai, d_bbr, d_bbi):
    G, P = lam_re.shape
    H = b_re_t.shape[1]

    def body(lr_ref, li_ref, ls_ref, br_ref, bi_ref, dar_ref, dai_ref, dbbr_ref, dbbi_ref,
             dlr_ref, dli_ref, dls_ref, dbr_ref, dbi_ref):
        lr, li = lr_ref[...], li_ref[...]
        dt, ar, ai, den, cr, ci = _s5_coeffs(lr, li, ls_ref[...])
        br, bi = br_ref[...], bi_ref[...]
        gbr, gbi = dbbr_ref[...], dbbi_ref[...]
        crb, cib = cr[:, None, :], ci[:, None, :]
        dbr_ref[...] = crb * gbr + cib * gbi
        dbi_ref[...] = crb * gbi - cib * gbr
        gcr = jnp.sum(br * gbr + bi * gbi, axis=1)
        gci = jnp.sum(br * gbi - bi * gbr, axis=1)
        ilr, ili = lr / den, -li / den
        gar = dar_ref[...] + (ilr * gcr + ili * gci)
        gai = dai_ref[...] + (ilr * gci - ili * gcr)
        qr, qi = cr * ilr - ci * ili, cr * ili + ci * ilr
        glr = -(qr * gcr + qi * gci)
        gli = -(qr * gci - qi * gcr)
        glr = glr + dt * (ar * gar + ai * gai)
        gli = gli + dt * (ar * gai - ai * gar)
        wr, wi = lr * ar - li * ai, lr * ai + li * ar
        gdt = jnp.sum(wr * gar + wi * gai, axis=1, keepdims=True)
        dlr_ref[...] = glr
        dli_ref[...] = gli
        dls_ref[...] = gdt * dt

    return pl.pallas_call(
        body, name="s5_prep_bwd",
        out_shape=(jax.ShapeDtypeStruct((G, P), F32), jax.ShapeDtypeStruct((G, P), F32),
                   jax.ShapeDtypeStruct((G, 1), F32),
                   jax.ShapeDtypeStruct((G, H, P), F32), jax.ShapeDtypeStruct((G, H, P), F32)),
        compiler_params=_params(),
    )(lam_re, lam_im, log_step.reshape(G, 1), b_re_t, b_im_t, d_ar, d_ai, d_bbr, d_bbi)


def _s5_block_mats(bbr_t, bbi_t, c_re, c_im):
    bmat = _s5_expand(bbr_t, bbi_t)
    cmat = jnp.transpose(_s5_expand(c_re, -c_im), (0, 2, 1))
    return bmat.astype(BF16), cmat.astype(BF16)


def _s5_diag_mask():
    r = lax.broadcasted_iota(jnp.int32, (LANES, 2 * STATE_COLS), 0) // SSM_GROUP
    c = (lax.broadcasted_iota(jnp.int32, (LANES, 2 * STATE_COLS), 1) % STATE_COLS) // SSM_STATE
    return (r == c).astype(F32)


def _s5_expand(re, im):
    re = jnp.tile(re.reshape(SSM_BLOCKS, LANES, SSM_STATE), (1, 1, GROUPS_PER_BLOCK))
    im = jnp.tile(im.reshape(SSM_BLOCKS, LANES, SSM_STATE), (1, 1, GROUPS_PER_BLOCK))
    return jnp.concatenate([re, im], axis=-1) * _s5_diag_mask()[None]


def _s5_unfold(dmat):
    d = dmat.reshape(SSM_GROUPS, SSM_GROUP, 2, SSM_STATE)
    return jnp.transpose(d, (2, 0, 1, 3))


def _s5_a_rows(ar, ai):
    a = jnp.concatenate([ar.reshape(SSM_BLOCKS, STATE_COLS), ai.reshape(SSM_BLOCKS, STATE_COLS)], axis=1)
    return jnp.broadcast_to(a[:, None, :], (SSM_BLOCKS, SUBLANES, 2 * STATE_COLS))


def _to_step_major(src_ref, dst_ref, seg):
    for s in range(SUBLANES):
        dst_ref[pl.ds(s, seg, stride=SUBLANES), :] = src_ref[pl.ds(seg * s, seg), :]


def _segment_rows(ref, s, seg):
    return ref[pl.ds(s, seg, stride=SUBLANES), :]


def _cmul(ar, ai, xr, xi):
    return ar * xr - ai * xi, ar * xi + ai * xr


def _s5_tables(a_ref, pw_s, pwr_s, S, seg):
    ar, ai = a_ref[:, :S], a_ref[:, S:]

    def step(i, c):
        pr, pi = c
        pw_s[i, :, :S] = pr
        pw_s[i, :, S:] = pi
        nr, ni = _cmul(ar, ai, pr, pi)
        pwr_s[seg - 1 - i, :, :S] = nr
        pwr_s[seg - 1 - i, :, S:] = ni
        return nr, ni

    pr, pi = lax.fori_loop(0, seg, step, (jnp.ones_like(ar), jnp.zeros_like(ai)))
    pw_s[seg, :, :S] = pr
    pw_s[seg, :, S:] = pi


def _s5_fwd(proj, bmat, cmat, a_rows, d_skip, *, tc=512):
    L = proj.shape[0]
    tc = min(tc, L)
    nt = L // tc
    seg = tc // SUBLANES
    S = STATE_COLS

    def body(u_ref, b_ref, c_ref, a_ref, d_ref, y_ref, yg_ref, xp_ref,
             bu_s, xp_s, pw_s, pwr_s, carry_s, e_s, up_s, yc_s):
        @pl.when(pl.program_id(1) == 0)
        def _():
            carry_s[...] = jnp.zeros_like(carry_s)
            _s5_tables(a_ref, pw_s, pwr_s, S, seg)

        ar, ai = a_ref[:, :S], a_ref[:, S:]
        _to_step_major(u_ref, up_s, seg)
        bu = jnp.dot(up_s[...].astype(BF16), b_ref[...], preferred_element_type=F32)
        bu_s[...] = bu.reshape(seg, SUBLANES, 2 * S)

        def step(i, carry):
            cr, ci = carry
            xp_s[i, :, :S] = cr
            xp_s[i, :, S:] = ci
            return ar * cr - ai * ci + bu_s[i, :, :S], ar * ci + ai * cr + bu_s[i, :, S:]

        zero = jnp.zeros((SUBLANES, S), F32)
        fr, fi = lax.fori_loop(0, seg, step, (zero, zero))
        pr, pi = pw_s[seg, 0:1, :S], pw_s[seg, 0:1, S:]
        er, ei = carry_s[0:1, :S], carry_s[0:1, S:]
        for s in range(SUBLANES):
            e_s[s:s + 1, :S] = er
            e_s[s:s + 1, S:] = ei
            tr, ti = _cmul(pr, pi, er, ei)
            er, ei = fr[s:s + 1] + tr, fi[s:s + 1] + ti
        carry_s[0:1, :S] = er
        carry_s[0:1, S:] = ei
        pw = pw_s[0:seg]
        tr, ti = _cmul(pw[:, :, :S], pw[:, :, S:], e_s[:, :S][None], e_s[:, S:][None])
        xl = xp_s[...]
        xp = jnp.concatenate([xl[:, :, :S] + tr, xl[:, :, S:] + ti], axis=-1).reshape(tc, 2 * S)
        xp_ref[...] = xp
        a1r, a1i = ar[0:1], ai[0:1]
        x_re = a1r * xp[:, :S] - a1i * xp[:, S:] + bu[:, :S]
        x_im = a1r * xp[:, S:] + a1i * xp[:, :S] + bu[:, S:]
        xs = jnp.concatenate([x_re, x_im], axis=1).astype(BF16)
        yc_s[...] = jnp.dot(xs, c_ref[...], preferred_element_type=F32)
        for s in range(SUBLANES):
            rows = pl.ds(seg * s, seg)
            y = _segment_rows(yc_s, s, seg) + d_ref[...] * u_ref[rows, :]
            y_ref[rows, :] = y
            yg_ref[rows, :] = _gelu(y).astype(BF16)

    return pl.pallas_call(
        body, name="s5_fwd",
        out_shape=(jax.ShapeDtypeStruct((L, MAIN_WIDTH), F32),
                   jax.ShapeDtypeStruct((L, MAIN_WIDTH), BF16),
                   jax.ShapeDtypeStruct((L, SSM_BLOCKS * 2 * S), F32)),
        grid=(SSM_BLOCKS, nt),
        in_specs=[pl.BlockSpec((tc, LANES), lambda b, t: (t, b)),
                  pl.BlockSpec((None, LANES, 2 * S), lambda b, t: (b, 0, 0)),
                  pl.BlockSpec((None, 2 * S, LANES), lambda b, t: (b, 0, 0)),
                  pl.BlockSpec((None, SUBLANES, 2 * S), lambda b, t: (b, 0, 0)),
                  pl.BlockSpec((1, LANES), lambda b, t: (0, b))],
        out_specs=(pl.BlockSpec((tc, LANES), lambda b, t: (t, b)),
                   pl.BlockSpec((tc, LANES), lambda b, t: (t, b)),
                   pl.BlockSpec((tc, 2 * S), lambda b, t: (t, b))),
        scratch_shapes=[pltpu.VMEM((seg, SUBLANES, 2 * S), F32),
                        pltpu.VMEM((seg, SUBLANES, 2 * S), F32),
                        pltpu.VMEM((seg + 1, SUBLANES, 2 * S), F32),
                        pltpu.VMEM((seg, SUBLANES, 2 * S), F32),
                        pltpu.VMEM((SUBLANES, 2 * S), F32),
                        pltpu.VMEM((SUBLANES, 2 * S), F32),
                        pltpu.VMEM((tc, LANES), F32),
                        pltpu.VMEM((tc, LANES), F32)],
        compiler_params=_params("parallel", "arbitrary"),
    )(proj, bmat, cmat, a_rows, d_skip.reshape(1, MAIN_WIDTH))


def _s5_bwd(proj, dyg_a, dyg_b, y, xp, bmat, cmat, a_rows, d_skip, dproj, *, tc=512):
    L = proj.shape[0]
    tc = min(tc, L)
    nt = L // tc
    seg = tc // SUBLANES
    S = STATE_COLS
    nn = (((1,), (1,)), ((), ()))
    tn = (((0,), (0,)), ((), ()))

    def fold_diagonal(acc_ref, mask_ref, fold_ref):
        x = acc_ref[...] * mask_ref[...]
        hi = x.astype(BF16)
        rest = x - hi.astype(F32)
        mid = rest.astype(BF16)
        low = (rest - mid.astype(F32)).astype(BF16)
        return sum(jnp.dot(piece, fold_ref[...], preferred_element_type=F32) for piece in (hi, mid, low))

    def body(u_ref, dyga_ref, dygb_ref, y_ref, xp_ref, b_ref, c_ref, a_ref, d_ref, mask_ref, fold_ref, dp_hbm,
             du_ref, dbd_ref, dcd_ref, da_ref, dd_ref,
             dl_s, pw_s, pwr_s, carry_s, e_s, up_s, dy_s, dyp_s, dup_s, db_ref, dc_ref):
        @pl.when(pl.program_id(1) == 0)
        def _():
            carry_s[...] = jnp.zeros_like(carry_s)
            db_ref[...] = jnp.zeros_like(db_ref)
            dc_ref[...] = jnp.zeros_like(dc_ref)
            da_ref[...] = jnp.zeros_like(da_ref)
            dd_ref[...] = jnp.zeros_like(dd_ref)
            _s5_tables(a_ref, pw_s, pwr_s, S, seg)

        ar, ai = a_ref[:, :S], a_ref[:, S:]
        a1r, a1i = ar[0:1], ai[0:1]
        u = u_ref[...]
        dy = (dyga_ref[...] + dygb_ref[...]) * _gelu_grad(y_ref[...])
        dy_s[...] = dy
        xp = xp_ref[...]
        _to_step_major(u_ref, up_s, seg)
        _to_step_major(dy_s, dyp_s, seg)
        ubp = up_s[...].astype(BF16)
        dyp = dyp_s[...].astype(BF16)
        bu = jnp.dot(ubp, b_ref[...], preferred_element_type=F32)
        x_re = a1r * xp[:, :S] - a1i * xp[:, S:] + bu[:, :S]
        x_im = a1r * xp[:, S:] + a1i * xp[:, :S] + bu[:, S:]
        xs = jnp.concatenate([x_re, x_im], axis=1).astype(BF16)
        dc_ref[...] += lax.dot_general(dyp, xs, tn, preferred_element_type=F32)
        dx = lax.dot_general(dyp, c_ref[...], nn, preferred_element_type=F32)
        dl_s[...] = dx.reshape(seg, SUBLANES, 2 * S)

        def step(k, carry):
            cr, ci = carry
            i = seg - 1 - k
            lr = dl_s[i, :, :S] + (ar * cr + ai * ci)
            li = dl_s[i, :, S:] + (ar * ci - ai * cr)
            dl_s[i, :, :S] = lr
            dl_s[i, :, S:] = li
            return lr, li

        zero = jnp.zeros((SUBLANES, S), F32)
        fr, fi = lax.fori_loop(0, seg, step, (zero, zero))
        pr, pi = pw_s[seg, 0:1, :S], pw_s[seg, 0:1, S:]
        er, ei = carry_s[0:1, :S], carry_s[0:1, S:]
        for s in range(SUBLANES - 1, -1, -1):
            e_s[s:s + 1, :S] = er
            e_s[s:s + 1, S:] = ei
            er, ei = fr[s:s + 1] + (pr * er + pi * ei), fi[s:s + 1] + (pr * ei - pi * er)
        carry_s[0:1, :S] = er
        carry_s[0:1, S:] = ei
        er, ei = e_s[:, :S][None], e_s[:, S:][None]
        pw = pwr_s[...]
        pwr, pwi = pw[:, :, :S], pw[:, :, S:]
        ll = dl_s[...]
        lam = jnp.concatenate([ll[:, :, :S] + (pwr * er + pwi * ei), ll[:, :, S:] + (pwr * ei - pwi * er)],
                              axis=-1).reshape(tc, 2 * S)
        l_re, l_im = lam[:, :S], lam[:, S:]
        da_ref[0:1, :S] += jnp.sum(l_re * xp[:, :S] + l_im * xp[:, S:], axis=0, keepdims=True)
        da_ref[0:1, S:] += jnp.sum(l_im * xp[:, :S] - l_re * xp[:, S:], axis=0, keepdims=True)
        lamb = lam.astype(BF16)
        dup_s[...] = lax.dot_general(lamb, b_ref[...], nn, preferred_element_type=F32)
        for s in range(SUBLANES):
            rows = pl.ds(seg * s, seg)
            du = _segment_rows(dup_s, s, seg) + d_ref[...] * dy_s[rows, :]
            du_ref[rows, :] = du.astype(du_ref.dtype)
        db_ref[...] += lax.dot_general(ubp, lamb, tn, preferred_element_type=F32)
        dd_ref[0:1, :] += jnp.sum(dy * u, axis=0, keepdims=True)

        @pl.when(pl.program_id(1) == nt - 1)
        def _():
            dbd_ref[...] = fold_diagonal(db_ref, mask_ref, fold_ref)
            dcd_ref[...] = fold_diagonal(dc_ref, mask_ref, fold_ref)

    rev = lambda b, t: (nt - 1 - t, b)
    col = jnp.arange(2 * S)
    fold = ((col // S * SSM_STATE + col % SSM_STATE)[:, None] == jnp.arange(LANES)[None, :]).astype(BF16)
    return pl.pallas_call(
        body, name="s5_bwd",
        out_shape=(jax.ShapeDtypeStruct(dproj.shape, dproj.dtype),
                   jax.ShapeDtypeStruct((SSM_BLOCKS, LANES, LANES), F32),
                   jax.ShapeDtypeStruct((SSM_BLOCKS, LANES, LANES), F32),
                   jax.ShapeDtypeStruct((SSM_BLOCKS, SUBLANES, 2 * S), F32),
                   jax.ShapeDtypeStruct((SUBLANES, MAIN_WIDTH), F32)),
        input_output_aliases={11: 0},
        grid=(SSM_BLOCKS, nt),
        in_specs=[pl.BlockSpec((tc, LANES), rev),
                  pl.BlockSpec((tc, LANES), rev),
                  pl.BlockSpec((tc, LANES), rev),
                  pl.BlockSpec((tc, LANES), rev),
                  pl.BlockSpec((tc, 2 * S), rev),
                  pl.BlockSpec((None, LANES, 2 * S), lambda b, t: (b, 0, 0)),
                  pl.BlockSpec((None, 2 * S, LANES), lambda b, t: (b, 0, 0)),
                  pl.BlockSpec((None, SUBLANES, 2 * S), lambda b, t: (b, 0, 0)),
                  pl.BlockSpec((1, LANES), lambda b, t: (0, b)),
                  pl.BlockSpec((LANES, 2 * S), lambda b, t: (0, 0)),
                  pl.BlockSpec((2 * S, LANES), lambda b, t: (0, 0)),
                  _ANY],
        out_specs=(pl.BlockSpec((tc, LANES), rev),
                   pl.BlockSpec((None, LANES, LANES), lambda b, t: (b, 0, 0)),
                   pl.BlockSpec((None, LANES, LANES), lambda b, t: (b, 0, 0)),
                   pl.BlockSpec((None, SUBLANES, 2 * S), lambda b, t: (b, 0, 0)),
                   pl.BlockSpec((SUBLANES, LANES), lambda b, t: (0, b))),
        scratch_shapes=[pltpu.VMEM((seg, SUBLANES, 2 * S), F32),
                        pltpu.VMEM((seg + 1, SUBLANES, 2 * S), F32),
                        pltpu.VMEM((seg, SUBLANES, 2 * S), F32),
                        pltpu.VMEM((SUBLANES, 2 * S), F32),
                        pltpu.VMEM((SUBLANES, 2 * S), F32),
                        pltpu.VMEM((tc, LANES), F32),
                        pltpu.VMEM((tc, LANES), F32),
                        pltpu.VMEM((tc, LANES), F32),
                        pltpu.VMEM((tc, LANES), F32),
                        pltpu.VMEM((LANES, 2 * S), F32),
                        pltpu.VMEM((LANES, 2 * S), F32)],
        compiler_params=_params("parallel", "arbitrary"),
    )(proj, dyg_a, dyg_b, y, xp, bmat, cmat, a_rows, d_skip.reshape(1, MAIN_WIDTH), _s5_diag_mask(), fold, dproj)


_Z_COLS = slice(MAIN_WIDTH, 2 * MAIN_WIDTH)
_ZM_COLS = slice(2 * MAIN_WIDTH + MEM_WIDTH, IN_WIDTH)


def _proj_rows(tr):
    return pl.BlockSpec((tr, IN_WIDTH), lambda i: (i, 0))


def _row_specs(tr):
    main = pl.BlockSpec((tr, MAIN_WIDTH), lambda i: (i, 0))
    z = pl.BlockSpec((tr, MAIN_WIDTH), lambda i: (i, 1))
    zm = pl.BlockSpec((tr, MEM_WIDTH), lambda i: (i, IN_WIDTH // MEM_WIDTH - 1))
    mem = pl.BlockSpec((tr, MEM_WIDTH), lambda i: (i, 0))
    cat = pl.BlockSpec((tr, D_MODEL), lambda i: (i, 0))
    vec = pl.BlockSpec((1, MAIN_WIDTH), lambda i: (0, 0))
    return main, z, zm, mem, cat, vec


def _gate_a_fwd(y, t, b_glu, proj, o_mem, *, tr=256):
    L = y.shape[0]
    tr = min(tr, L)

    def body(y_ref, t_ref, b_ref, z_ref, zm_ref, om_ref, o_ref):
        yg = _gelu(y_ref[...])
        sz, _ = _silu_and_grad(z_ref[...])
        o_ref[:, :MAIN_WIDTH] = (yg * _sigmoid(t_ref[...] + b_ref[...]) * sz).astype(BF16)
        szm, _ = _silu_and_grad(zm_ref[...])
        o_ref[:, MAIN_WIDTH:] = (om_ref[...] * szm).astype(BF16)

    main, z, zm, mem, cat, vec = _row_specs(tr)
    return pl.pallas_call(
        body, name="gate_a_fwd", out_shape=jax.ShapeDtypeStruct((L, D_MODEL), BF16),
        grid=(L // tr,), in_specs=[main, main, vec, z, zm, mem], out_specs=cat,
        compiler_params=_params("parallel"),
    )(y, t, b_glu.reshape(1, MAIN_WIDTH), proj, proj, o_mem)


def _gate_a_bwd(dcat, y, t, b_glu, proj, o_mem, *, tr=256):
    L = y.shape[0]
    tr = min(tr, L)

    def body(dc_ref, y_ref, t_ref, b_ref, z_ref, zm_ref, om_ref,
             dp_ref, dt_ref, dyg_ref, dom_ref, db_ref):
        dmain = dc_ref[:, :MAIN_WIDTH]
        dmemo = dc_ref[:, MAIN_WIDTH:]
        yg = _gelu(y_ref[...])
        sg = _sigmoid(t_ref[...] + b_ref[...])
        sz, gz = _silu_and_grad(z_ref[...])
        dp_ref[:, _Z_COLS] = (dmain * (yg * sg) * gz).astype(BF16)
        dy2 = dmain * sz
        dyg_ref[...] = dy2 * sg
        dt = dy2 * yg * (sg * (1.0 - sg))
        dt_ref[...] = dt.astype(BF16)

        @pl.when(pl.program_id(0) == 0)
        def _():
            db_ref[...] = jnp.zeros_like(db_ref)

        db_ref[...] += jnp.sum(dt, axis=0, keepdims=True)
        szm, gzm = _silu_and_grad(zm_ref[...])
        dom_ref[...] = dmemo * szm
        dp_ref[:, _ZM_COLS] = (dmemo * om_ref[...] * gzm).astype(BF16)

    main, z, zm, mem, cat, vec = _row_specs(tr)
    outs = pl.pallas_call(
        body, name="gate_a_bwd",
        out_shape=(jax.ShapeDtypeStruct((L, IN_WIDTH), BF16),
                   jax.ShapeDtypeStruct((L, MAIN_WIDTH), BF16), jax.ShapeDtypeStruct((L, MAIN_WIDTH), F32),
                   jax.ShapeDtypeStruct((L, MEM_WIDTH), F32), jax.ShapeDtypeStruct((1, MAIN_WIDTH), F32)),
        grid=(L // tr,), in_specs=[cat, main, main, vec, z, zm, mem],
        out_specs=(_proj_rows(tr), main, main, mem, vec),
        compiler_params=_params("arbitrary"),
    )(dcat, y, t, b_glu.reshape(1, MAIN_WIDTH), proj, proj, o_mem)
    return outs


def _gate_b_fwd(att, proj, o_mem, *, tr=256):
    L = att.shape[0]
    tr = min(tr, L)

    def body(a_ref, z_ref, zm_ref, om_ref, o_ref):
        sz, _ = _silu_and_grad(z_ref[...])
        o_ref[:, :MAIN_WIDTH] = (a_ref[...] * sz).astype(BF16)
        szm, _ = _silu_and_grad(zm_ref[...])
        o_ref[:, MAIN_WIDTH:] = (om_ref[...] * szm).astype(BF16)

    main, z, zm, mem, cat, _ = _row_specs(tr)
    return pl.pallas_call(
        body, name="gate_b_fwd", out_shape=jax.ShapeDtypeStruct((L, D_MODEL), BF16),
        grid=(L // tr,), in_specs=[main, z, zm, mem], out_specs=cat,
        compiler_params=_params("parallel"),
    )(att, proj, proj, o_mem)


def _gate_b_bwd(dcat, att, proj, o_mem, *, tr=256):
    L = att.shape[0]
    tr = min(tr, L)

    def body(dc_ref, a_ref, z_ref, zm_ref, om_ref, da_ref, dp_ref, dom_ref, dl_ref):
        dmain = dc_ref[:, :MAIN_WIDTH]
        dmemo = dc_ref[:, MAIN_WIDTH:]
        att = a_ref[...]
        sz, gz = _silu_and_grad(z_ref[...])
        datt = dmain * sz
        da_ref[...] = datt
        dp_ref[:, _Z_COLS] = (dmain * att * gz).astype(BF16)
        szm, gzm = _silu_and_grad(zm_ref[...])
        dom_ref[...] = dmemo * szm
        dp_ref[:, _ZM_COLS] = (dmemo * om_ref[...] * gzm).astype(BF16)
        prod = datt * att
        for h in range(FOX_HEADS):
            dl_ref[h] = jnp.sum(prod[:, h * HEAD_DIM:(h + 1) * HEAD_DIM], axis=1, keepdims=True)

    main, z, zm, mem, cat, _ = _row_specs(tr)
    delta = pl.BlockSpec((FOX_HEADS, tr, 1), lambda i: (0, i, 0))
    return pl.pallas_call(
        body, name="gate_b_bwd",
        out_shape=(jax.ShapeDtypeStruct((L, MAIN_WIDTH), F32), jax.ShapeDtypeStruct((L, IN_WIDTH), BF16),
                   jax.ShapeDtypeStruct((L, MEM_WIDTH), F32), jax.ShapeDtypeStruct((FOX_HEADS, L, 1), F32)),
        grid=(L // tr,), in_specs=[cat, main, z, zm, mem], out_specs=(main, _proj_rows(tr), mem, delta),
        compiler_params=_params("parallel"),
    )(dcat, att, proj, proj, o_mem)


_MEM_Q_COL = (2 * MAIN_WIDTH) // HEAD_DIM
_NT = (((1,), (1,)), ((), ()))
_TN = (((0,), (0,)), ((), ()))


def _mem_probs(q_ref, k_ref):
    qs = (q_ref[...] * (HEAD_DIM ** -0.5)).astype(BF16)
    s = lax.dot_general(qs, k_ref[...].astype(BF16), _NT, preferred_element_type=F32)
    e = jnp.exp(s - jnp.max(s, axis=-1, keepdims=True))
    return qs, e / jnp.sum(e, axis=-1, keepdims=True)


def _mem_attn_fwd(proj, kvm, *, tq=2048):
    L = proj.shape[0]
    tq = min(tq, L)

    def body(q_ref, k_ref, v_ref, o_ref):
        _, p = _mem_probs(q_ref, k_ref)
        o_ref[...] = jnp.dot(p.astype(BF16), v_ref[...].astype(BF16), preferred_element_type=F32)

    return pl.pallas_call(
        body, name="mem_attn_fwd", out_shape=jax.ShapeDtypeStruct((L, MEM_WIDTH), F32),
        grid=(MEM_HEADS, L // tq),
        in_specs=[pl.BlockSpec((tq, HEAD_DIM), lambda h, i: (i, _MEM_Q_COL + h)),
                  pl.BlockSpec((N_MEM, HEAD_DIM), lambda h, i: (0, h)),
                  pl.BlockSpec((N_MEM, HEAD_DIM), lambda h, i: (0, MEM_HEADS + h))],
        out_specs=pl.BlockSpec((tq, HEAD_DIM), lambda h, i: (i, h)),
        compiler_params=_params("parallel", "parallel"),
    )(proj, kvm, kvm)


def _mem_attn_bwd(proj, kvm, do, dproj, *, tq=2048):
    L = proj.shape[0]
    tq = min(tq, L)

    def body(q_ref, k_ref, v_ref, do_ref, dp_hbm, dq_ref, dk_ref, dv_ref):
        @pl.when(pl.program_id(1) == 0)
        def _():
            dk_ref[...] = jnp.zeros_like(dk_ref)
            dv_ref[...] = jnp.zeros_like(dv_ref)

        qs, p = _mem_probs(q_ref, k_ref)
        dob = do_ref[...].astype(BF16)
        dp = lax.dot_general(dob, v_ref[...].astype(BF16), _NT, preferred_element_type=F32)
        ds = p * (dp - jnp.sum(p * dp, axis=-1, keepdims=True))
        dsb = ds.astype(BF16)
        dq = jnp.dot(dsb, k_ref[...].astype(BF16), preferred_element_type=F32) * (HEAD_DIM ** -0.5)
        dq_ref[...] = dq.astype(BF16)
        dk_ref[...] += lax.dot_general(dsb, qs, _TN, preferred_element_type=F32)
        dv_ref[...] += lax.dot_general(p.astype(BF16), dob, _TN, preferred_element_type=F32)

    dproj, dk, dv = pl.pallas_call(
        body, name="mem_attn_bwd",
        out_shape=(jax.ShapeDtypeStruct(dproj.shape, dproj.dtype),
                   jax.ShapeDtypeStruct((N_MEM, MEM_WIDTH), F32),
                   jax.ShapeDtypeStruct((N_MEM, MEM_WIDTH), F32)),
        grid=(MEM_HEADS, L // tq),
        in_specs=[pl.BlockSpec((tq, HEAD_DIM), lambda h, i: (i, _MEM_Q_COL + h)),
                  pl.BlockSpec((N_MEM, HEAD_DIM), lambda h, i: (0, h)),
                  pl.BlockSpec((N_MEM, HEAD_DIM), lambda h, i: (0, MEM_HEADS + h)),
                  pl.BlockSpec((tq, HEAD_DIM), lambda h, i: (i, h)),
                  _ANY],
        out_specs=(pl.BlockSpec((tq, HEAD_DIM), lambda h, i: (i, _MEM_Q_COL + h)),
                   pl.BlockSpec((N_MEM, HEAD_DIM), lambda h, i: (0, h)),
                   pl.BlockSpec((N_MEM, HEAD_DIM), lambda h, i: (0, h))),
        input_output_aliases={4: 0},
        compiler_params=_params("parallel", "arbitrary"),
    )(proj, kvm, kvm, do, dproj)
    return dproj, jnp.concatenate([dk, dv], axis=1)


def _tile_cumsum(x, row, reverse):
    for sh in (1, 2, 4):
        if reverse:
            x = x + jnp.where(row < SUBLANES - sh, pltpu.roll(x, SUBLANES - sh, 0), 0.0)
        else:
            x = x + jnp.where(row >= sh, pltpu.roll(x, sh, 0), 0.0)
    return x


def _fgate_fwd(pre, b_pad):
    L = pre.shape[0]
    n8 = L // SUBLANES

    def body(p_ref, b_ref, o_ref):
        row = lax.broadcasted_iota(jnp.int32, (SUBLANES, LANES), 0)
        b = b_ref[...]

        def step(i, carry):
            x = p_ref[i] + b
            logf = jnp.minimum(x, 0.0) - jnp.log(1.0 + jnp.exp(-jnp.abs(x)))
            t = _tile_cumsum(logf, row, False) + carry
            o_ref[i] = t
            return t[SUBLANES - 1:SUBLANES, :]

        lax.fori_loop(0, n8, step, jnp.zeros((1, LANES), F32))

    out = pl.pallas_call(
        body, name="fgate_fwd", out_shape=jax.ShapeDtypeStruct((n8, SUBLANES, LANES), F32),
        compiler_params=_params(),
    )(pre.reshape(n8, SUBLANES, LANES), b_pad.reshape(1, LANES))
    return out.reshape(L, LANES)


def _fgate_bwd(dfcum, pre, b_pad):
    L = pre.shape[0]
    n8 = L // SUBLANES

    def body(d_ref, p_ref, b_ref, o_ref, s_ref):
        row = lax.broadcasted_iota(jnp.int32, (SUBLANES, LANES), 0)
        b = b_ref[...]

        def step(k, carry):
            c, acc = carry
            i = n8 - 1 - k
            t = _tile_cumsum(d_ref[i], row, True) + c
            dpre = t * _sigmoid(-(p_ref[i] + b))
            o_ref[i] = dpre
            return t[0:1, :], acc + dpre

        _, acc = lax.fori_loop(0, n8, step, (jnp.zeros((1, LANES), F32), jnp.zeros((SUBLANES, LANES), F32)))
        s_ref[...] = jnp.sum(acc, axis=0, keepdims=True)

    dpre, db = pl.pallas_call(
        body, name="fgate_bwd",
        out_shape=(jax.ShapeDtypeStruct((n8, SUBLANES, LANES), F32), jax.ShapeDtypeStruct((1, LANES), F32)),
        compiler_params=_params(),
    )(dfcum.reshape(n8, SUBLANES, LANES), pre.reshape(n8, SUBLANES, LANES), b_pad.reshape(1, LANES))
    return dpre.reshape(L, LANES), db


FOX_BLOCK = 1024


def _fox_scores(qs, k, fk, diagonal, row0=0):
    s = lax.dot_general(qs, k, _NT, preferred_element_type=F32) - fk
    if diagonal:
        row = row0 + lax.broadcasted_iota(jnp.int32, s.shape, 0)
        col = lax.broadcasted_iota(jnp.int32, s.shape, 1)
        s = jnp.where(row >= col, s, NEG_BIG)
    return s


def _fox_diagonal_parts(tq):
    half = tq // 2
    return ((slice(0, half), half), (slice(half, tq), tq))


def _fox_specs(tq, L):
    nq = L // tq
    return dict(
        rows=lambda off: pl.BlockSpec((tq, HEAD_DIM), lambda h, i: (i, off + h)),
        seq=lambda off: pl.BlockSpec((L, HEAD_DIM), lambda h, i: (0, off + h)),
        col=pl.BlockSpec((None, None, tq, 1), lambda h, i: (h, i, 0, 0)),
        col_all=pl.BlockSpec((None, nq, tq, 1), lambda h, i: (h, 0, 0, 0)),
        row=pl.BlockSpec((None, None, 1, tq), lambda h, i: (h, i, 0, 0)),
        row_all=pl.BlockSpec((None, nq, 1, tq), lambda h, i: (h, 0, 0, 0)))


FOX_FWD_HEADS = 2
FOX_FWD_BLOCK = 1024


def _fox_fwd(proj, kv, fk):
    L = proj.shape[0]
    tq = min(FOX_FWD_BLOCK, L)
    nq = L // tq
    nh = FOX_FWD_HEADS
    W = nh * HEAD_DIM
    lse_shape = fk.shape[:2] + (fk.shape[3], 1)
    fk = fk.reshape(FOX_HEADS, nq, 1, tq)

    def body(q_ref, k_ref, v_ref, fk_ref, o_ref, lse_ref, m_s, l_s, acc_s):
        qi = pl.program_id(1)
        cols = [slice(a * HEAD_DIM, (a + 1) * HEAD_DIM) for a in range(nh)]
        qs = [(q_ref[:, cs] * (HEAD_DIM ** -0.5)).astype(BF16) for cs in cols]
        m_s[...] = jnp.full_like(m_s, NEG_BIG)
        l_s[...] = jnp.zeros_like(l_s)
        acc_s[...] = jnp.zeros_like(acc_s)

        def block(j, diagonal):
            r0 = pl.multiple_of(j * tq, tq)
            for a, cs in enumerate(cols):
                s = _fox_scores(qs[a], k_ref[pl.ds(r0, tq), cs], fk_ref[a, j], diagonal)
                m_new = jnp.maximum(m_s[a], jnp.max(s, axis=-1, keepdims=True))
                alpha = jnp.exp(m_s[a] - m_new)
                p = jnp.exp(s - m_new)
                l_s[a] = alpha * l_s[a] + jnp.sum(p, axis=-1, keepdims=True)
                acc_s[a] = alpha * acc_s[a] + jnp.dot(p.astype(BF16), v_ref[pl.ds(r0, tq), cs],
                                                      preferred_element_type=F32)
                m_s[a] = m_new

        def below(j, carry):
            block(j, False)
            return carry

        lax.fori_loop(0, qi, below, 0)
        block(qi, True)
        for a, cs in enumerate(cols):
            o_ref[:, cs] = acc_s[a] / l_s[a]
            lse_ref[a] = m_s[a] + jnp.log(l_s[a])

    att, lse = pl.pallas_call(
        body, name="fox_fwd",
        out_shape=(jax.ShapeDtypeStruct((L, MAIN_WIDTH), F32),
                   jax.ShapeDtypeStruct((FOX_HEADS, nq, tq, 1), F32)),
        grid=(FOX_HEADS // nh, nq),
        in_specs=[pl.BlockSpec((tq, W), lambda h, i: (i, h)),
                  pl.BlockSpec((L, W), lambda h, i: (0, h)),
                  pl.BlockSpec((L, W), lambda h, i: (0, FOX_HEADS // nh + h)),
                  pl.BlockSpec((nh, nq, 1, tq), lambda h, i: (h, 0, 0, 0))],
        out_specs=(pl.BlockSpec((tq, W), lambda h, i: (i, h)),
                   pl.BlockSpec((nh, None, tq, 1), lambda h, i: (h, i, 0, 0))),
        scratch_shapes=[pltpu.VMEM((nh, tq, 1), F32), pltpu.VMEM((nh, tq, 1), F32),
                        pltpu.VMEM((nh, tq, HEAD_DIM), F32)],
        compiler_params=_params("parallel", "parallel"),
    )(proj, kv, kv, fk)
    return att, lse.reshape(lse_shape)


def _fox_bwd(proj, kv, fk, lse, delta, datt, dproj):
    L = proj.shape[0]
    tq = min(FOX_BLOCK, L)
    nq = L // tq
    sp = _fox_specs(tq, L)

    def body(q_ref, k_ref, v_ref, fk_ref, lse_ref, dl_ref, do_ref, dp_hbm,
             dq_ref, dk_ref, dv_ref, dfq_ref, dfk_ref, dk_s, dv_s, df_s, dq_s, dfq_s):
        ki = pl.program_id(1)

        @pl.when(ki == 0)
        def _():
            dq_s[...] = jnp.zeros_like(dq_s)
            dfq_s[...] = jnp.zeros_like(dfq_s)

        k, v, fk = k_ref[...], v_ref[...], fk_ref[...]
        dk_s[...] = jnp.zeros_like(dk_s)
        dv_s[...] = jnp.zeros_like(dv_s)
        df_s[...] = jnp.zeros_like(df_s)

        def block(i, rows, width, diagonal):
            n = rows.stop - rows.start
            r0 = pl.multiple_of(i * tq + rows.start, n)
            qs = (q_ref[pl.ds(r0, n), :] * (HEAD_DIM ** -0.5)).astype(BF16)
            dob = do_ref[pl.ds(r0, n), :].astype(BF16)
            kw, vw = k[:width], v[:width]
            p = jnp.exp(_fox_scores(qs, kw, fk[:, :width], diagonal, rows.start) - lse_ref[i][rows])
            dp = lax.dot_general(dob, vw, _NT, preferred_element_type=F32)
            ds = p * (dp - dl_ref[i][rows])
            dsb = ds.astype(BF16)
            dv_s[:width] += lax.dot_general(p.astype(BF16), dob, _TN, preferred_element_type=F32)
            dk_s[:width] += lax.dot_general(dsb, qs, _TN, preferred_element_type=F32)
            df_s[:, :width] -= jnp.sum(ds, axis=0, keepdims=True)
            dq_s[i, rows] += jnp.dot(dsb, kw, preferred_element_type=F32)
            dfq_s[i, rows] += jnp.sum(ds, axis=1, keepdims=True)

        def above(i, carry):
            block(i, slice(0, tq), tq, False)
            return carry

        for rows, width in _fox_diagonal_parts(tq):
            block(ki, rows, width, True)
        lax.fori_loop(ki + 1, nq, above, 0)
        dk_ref[...] = dk_s[...].astype(BF16)
        dv_ref[...] = dv_s[...].astype(BF16)
        dfk_ref[...] = df_s[...]

        @pl.when(ki == nq - 1)
        def _():
            dq_ref[...] = (dq_s[...].reshape(L, HEAD_DIM) * (HEAD_DIM ** -0.5)).astype(BF16)
            dfq_ref[...] = dfq_s[...]

    return pl.pallas_call(
        body, name="fox_bwd",
        out_shape=(jax.ShapeDtypeStruct(dproj.shape, dproj.dtype),
                   jax.ShapeDtypeStruct((L, MAIN_WIDTH), BF16),
                   jax.ShapeDtypeStruct((L, MAIN_WIDTH), BF16),
                   jax.ShapeDtypeStruct((FOX_HEADS, nq, tq, 1), F32),
                   jax.ShapeDtypeStruct((FOX_HEADS, nq, 1, tq), F32)),
        grid=(FOX_HEADS, nq),
        in_specs=[sp["seq"](0), sp["rows"](0), sp["rows"](FOX_HEADS), sp["row"],
                  sp["col_all"], sp["col_all"], sp["seq"](0), _ANY],
        out_specs=(sp["seq"](0), sp["rows"](0), sp["rows"](0), sp["col_all"], sp["row"]),
        input_output_aliases={7: 0},
        scratch_shapes=[pltpu.VMEM((tq, HEAD_DIM), F32), pltpu.VMEM((tq, HEAD_DIM), F32),
                        pltpu.VMEM((1, tq), F32), pltpu.VMEM((nq, tq, HEAD_DIM), F32),
                        pltpu.VMEM((nq, tq, 1), F32)],
        compiler_params=_params("parallel", "arbitrary"),
    )(proj, kv, kv, fk, lse, delta, datt, dproj)


def _pad_lanes(a):
    return jnp.pad(a, ((0, 0), (0, LANES - a.shape[1])))


def _mem_branch_fwd(memn, w_mk, proj, tag):
    kvm = _mm(memn, w_mk, name="mem_kv_" + tag)
    return kvm, _mem_attn_fwd(proj, kvm)


def _mem_branch_bwd(mem, g, w_mk, proj, memn, kvm, do_mem, dproj, tag):
    dproj, dkvm = _mem_attn_bwd(proj, kvm, do_mem, dproj)
    dkvm = dkvm.astype(BF16)
    dw_mk = _mm(memn, dkvm, ta=True, name="dw_mem_kv_" + tag, out_dtype=BF16)
    dmemn = _mm(dkvm, w_mk, tb=True, name="dmemn_" + tag)
    _, dg = _rmsnorm_bwd(mem, g, dmemn, name="mem_norm_bwd_" + tag, dx_dtype=BF16)
    return dproj, dw_mk, dg


def _local_step(x, mem, target, w, fetch=None, grads_ready=None):
    if grads_ready is None:
        grads_ready = lambda group, grads, token: token
    L = x.shape[0]
    g = {}
    w = dict(w)

    b_re_t = jnp.transpose(w["b_re"], (0, 2, 1))
    b_im_t = jnp.transpose(w["b_im"], (0, 2, 1))
    ar, ai, bbr_t, bbi_t = _s5_prep(w["lam_re"], w["lam_im"], w["log_step"], b_re_t, b_im_t)
    bmat, cmat = _s5_block_mats(bbr_t, bbi_t, w["c_re"], w["c_im"])
    a_rows = _s5_a_rows(ar, ai)

    hn0 = _rmsnorm_fwd(x, w["pre_norm_g"][0], name="pre_norm_0", out_dtype=BF16)
    memn0 = _rmsnorm_fwd(mem, w["mem_norm_g"][0], name="mem_norm_0", out_dtype=BF16)
    memn1 = _rmsnorm_fwd(mem, w["mem_norm_g"][1], name="mem_norm_1", out_dtype=BF16)
    if fetch is not None:
        w.update(fetch("a", [hn0, memn0, memn1, bmat, cmat, a_rows]))
    proj_a = _mm(hn0, w["w_in_a"], name="in_proj_a")
    y, yg, xp = _s5_fwd(proj_a, bmat, cmat, a_rows, w["d_skip"])
    if fetch is not None:
        w.update(fetch("b", yg))
    t = _mm(yg, w["w_glu"], name="glu_proj")
    kvm0, om0 = _mem_branch_fwd(memn0, w["w_mem_kv"][0], proj_a, "0")
    cat0 = _gate_a_fwd(y, t, w["b_glu"], proj_a, om0)
    o0 = _mm(cat0, w["w_out"][0], name="out_proj_0")
    h1, kv_in, hn1 = _post_norm_and_next_norms(
        o0, w["post_norm_g"][0], x, w["kv_norm_g"], w["pre_norm_g"][1], name="post_norm_0_kv_pre_norm_1")

    if fetch is not None:
        w.update(fetch("c", kv_in))
    kv = _mm(kv_in, w["w_kv"], name="kv_proj", out_dtype=BF16)
    pre_f = _mm(kv_in, w["w_fgate"], name="fgate_proj")
    b_f = jnp.pad(w["b_fgate"], (0, LANES - FOX_HEADS))
    fcum = _fgate_fwd(pre_f, b_f)
    fc = jnp.transpose(fcum[:, :FOX_HEADS])
    tq = min(FOX_BLOCK, L)
    fk = fc.reshape(FOX_HEADS, L // tq, 1, tq)

    proj_b = _mm(hn1, w["w_in_b"], name="in_proj_b")
    att, lse = _fox_fwd(proj_b, kv, fk)
    kvm1, om1 = _mem_branch_fwd(memn1, w["w_mem_kv"][1], proj_b, "1")
    cat1 = _gate_b_fwd(att, proj_b, om1)
    o1 = _mm(cat1, w["w_out"][1], name="out_proj_1")
    dh2, loss_row = _final_norm_loss(o1, w["post_norm_g"][1], h1, target)

    do1, dpost1 = _rmsnorm_bwd(o1, w["post_norm_g"][1], dh2, name="post_norm_bwd_1", dx_dtype=BF16)
    dcat1 = _mm(do1, w["w_out"][1], tb=True, name="dcat_1", out_dtype=BF16)
    g["w_out_1"] = _mm(cat1, do1, ta=True, name="dw_out_1", out_dtype=BF16)
    datt, dproj_b, dom1, delta = _gate_b_bwd(dcat1, att, proj_b, om1)
    dproj_b, g["w_mem_kv_1"], dmemg1 = _mem_branch_bwd(mem, w["mem_norm_g"][1], w["w_mem_kv"][1], proj_b,
                                                      memn1, kvm1, dom1, dproj_b, "1")
    delta = delta.reshape(lse.shape)
    dproj_b, dk, dv, dfq, dfk = _fox_bwd(proj_b, kv, fk, lse, delta, datt, dproj_b)
    g["w_in_b"] = _mm(hn1, dproj_b, ta=True, name="dw_in_b", out_dtype=BF16, shards=N_CHIPS)
    dhn1 = _mm(dproj_b, w["w_in_b"], tb=True, name="dhn_1")

    dkv = jnp.concatenate([dk, dv], axis=1)
    g["w_kv"] = _mm(kv_in, dkv, ta=True, name="dw_kv", out_dtype=BF16, shards=N_CHIPS)
    dkv_in_a = _mm(dkv, w["w_kv"], tb=True, name="dkv_in_kv")
    dfcum = _pad_lanes(jnp.transpose(dfq.reshape(FOX_HEADS, L) + dfk.reshape(FOX_HEADS, L)))
    dpre_f, db_f = _fgate_bwd(dfcum, pre_f, b_f)
    g["b_fgate"] = db_f[0, :FOX_HEADS]
    g["w_fgate"] = _mm(kv_in, dpre_f, ta=True, name="dw_fgate")[:, :FOX_HEADS]
    dkv_in_b = _mm(dpre_f, w["w_fgate"], tb=True, name="dkv_in_fgate")
    dh1, g["kv_norm_g"], dpre1 = _rmsnorm_bwd_pair(h1, w["kv_norm_g"], (dkv_in_a, dkv_in_b), w["pre_norm_g"][1],
                                                   dhn1, adds=(dh2,), name="kv_pre_norm_bwd")
    dh1 = grads_ready("b", g, dh1)

    do0, dpost0 = _rmsnorm_bwd(o0, w["post_norm_g"][0], dh1, name="post_norm_bwd_0", dx_dtype=BF16)
    dcat0 = _mm(do0, w["w_out"][0], tb=True, name="dcat_0", out_dtype=BF16)
    g["w_out_0"] = _mm(cat0, do0, ta=True, name="dw_out_0", out_dtype=BF16)
    dcat0 = grads_ready("b_send", g, dcat0)
    dproj_a, dt, dyg_a, dom0, db_glu = _gate_a_bwd(dcat0, y, t, w["b_glu"], proj_a, om0)
    g["b_glu"] = db_glu[0]
    g["w_glu"] = _mm(yg, dt, ta=True, name="dw_glu", out_dtype=BF16)
    dyg_b = _mm(dt, w["w_glu"], tb=True, name="dyg")
    dproj_a, g["w_mem_kv_0"], dmemg0 = _mem_branch_bwd(mem, w["mem_norm_g"][0], w["w_mem_kv"][0], proj_a,
                                                      memn0, kvm0, dom0, dproj_a, "0")
    dyg_b = grads_ready("a1", g, dyg_b)
    dproj_a, db_blk, dc_blk, da_rows, dd_skip = _s5_bwd(proj_a, dyg_a, dyg_b, y, xp, bmat, cmat, a_rows,
                                                        w["d_skip"], dproj_a)
    dproj_a = grads_ready("a1_send", g, dproj_a)
    g["d_skip"] = dd_skip[0]
    g["w_in_a"] = _mm(hn0, dproj_a, ta=True, name="dw_in_a", out_dtype=BF16, shards=N_CHIPS)
    dproj_a = grads_ready("a2", g, dproj_a)
    dhn0 = _mm(dproj_a, w["w_in_a"], tb=True, name="dhn_0")
    grad_x, dpre0 = _rmsnorm_bwd(x, w["pre_norm_g"][0], dhn0, adds=(dh1,), name="pre_norm_bwd_0")

    dbb = _s5_unfold(db_blk)
    dcc = _s5_unfold(dc_blk)
    g["c_re"], g["c_im"] = dcc[0], -dcc[1]
    d_ar = da_rows[:, 0, :STATE_COLS].reshape(SSM_GROUPS, SSM_STATE)
    d_ai = da_rows[:, 0, STATE_COLS:].reshape(SSM_GROUPS, SSM_STATE)
    dlr, dli, dls, dbr_t, dbi_t = _s5_prep_bwd(w["lam_re"], w["lam_im"], w["log_step"], b_re_t, b_im_t,
                                               d_ar, d_ai, dbb[0], dbb[1])
    g["lam_re"], g["lam_im"], g["log_step"] = dlr, dli, dls[:, 0]
    g["b_re"] = jnp.transpose(dbr_t, (0, 2, 1))
    g["b_im"] = jnp.transpose(dbi_t, (0, 2, 1))
    g["pre_norm_g"] = jnp.stack([dpre0, dpre1])
    g["post_norm_g"] = jnp.stack([dpost0, dpost1])
    g["mem_norm_g"] = jnp.stack([dmemg0, dmemg1])
    return loss_row, grad_x, g


_MESH = pl.DeviceIdType.MESH
_ANY = pl.BlockSpec(memory_space=pl.ANY)


def _place():
    x, y, c = lax.axis_index("x"), lax.axis_index("y"), lax.axis_index("c")
    chips = [(1 - x, y), (x, 1 - y), (1 - x, 1 - y)]
    return x, y, c, chips


_HBM = pl.BlockSpec(memory_space=pltpu.HBM)
_SEM = pl.BlockSpec(memory_space=pltpu.SEMAPHORE)
_SIDE = pltpu.SideEffectType.DATAFLOW_SIDE_EFFECTING


def _in_hbm(a):
    return pltpu.with_memory_space_constraint(a, pltpu.HBM)


def _hbm_like(a):
    return pltpu.HBM(a.shape, a.dtype)


def _ici_copies(srcs, lands, send_sem, recv_sem, src_at, dst_at, wait_at, to_sibling=False):
    x, y, c, chips = _place()
    peers = [(x, y, 1 - c)] if to_sibling else [(cx, cy, c) for cx, cy in chips]
    m = len(peers)
    start, wait = [], []
    for i in range(len(srcs)):
        for k, (px, py, pc) in enumerate(peers):
            sem = dict(send_sem=send_sem.at[m * i + k], recv_sem=recv_sem.at[m * i + k],
                       device_id=(px, py, pc), device_id_type=_MESH)
            src = src_at(srcs[i], 2 * px + py, c)
            start.append(pltpu.make_async_remote_copy(src_ref=src, dst_ref=dst_at(lands[i], 2 * x + y, k, c), **sem))
            wait.append(pltpu.make_async_remote_copy(src_ref=src, dst_ref=wait_at(lands[i], 2 * px + py, k, c), **sem))
    return start, wait


def _route_peers(route):
    return 1 if len(route) == 4 else 3


_BLOCK_ROUTE = (lambda s, j, c: s, lambda l, me, k, c: l.at[me, c], lambda l, j, k, c: l.at[j, c])


def _ici_start(srcs, lands, token, route, *, name):
    n = len(srcs)

    def body(*refs):
        start, _ = _ici_copies(refs[:n], refs[n:2 * n], refs[2 * n + 1], refs[2 * n + 2], *route)
        for cp in start:
            cp.start()

    sems = pltpu.SemaphoreType.DMA((_route_peers(route) * n,))
    outs = pl.pallas_call(
        body, name=name,
        out_shape=(sems, sems, *[_hbm_like(a) for a in srcs], *[_hbm_like(a) for a in lands], _hbm_like(token)),
        in_specs=[_HBM] * (2 * n + 1), out_specs=(_SEM, _SEM, *[_HBM] * (2 * n + 1)),
        input_output_aliases={i: 2 + i for i in range(2 * n + 1)},
        compiler_params=pltpu.CompilerParams(has_side_effects=_SIDE),
    )(*[_in_hbm(a) for a in srcs], *[_in_hbm(a) for a in lands], _in_hbm(token))
    return (outs[0], outs[1], list(outs[2:2 + n]), list(outs[2 + n:2 + 2 * n])), outs[2 + 2 * n]


def _ici_wait(handle, after, route, *, name):
    send_sem, recv_sem, srcs, lands = handle
    n = len(srcs)
    after = list(after) if isinstance(after, (list, tuple)) else [after]

    def body(*refs):
        _, wait = _ici_copies(refs[:n], refs[n:2 * n], refs[2 * n], refs[2 * n + 1], *route)
        for cp in wait:
            cp.wait_send()
            cp.wait_recv()

    outs = pl.pallas_call(
        body, name=name,
        out_shape=(*[_hbm_like(a) for a in srcs], *[_hbm_like(a) for a in lands]),
        in_specs=[_HBM] * (2 * n) + [_SEM, _SEM] + [_ANY] * len(after), out_specs=tuple([_HBM] * (2 * n)),
        input_output_aliases={i: i for i in range(2 * n)},
        compiler_params=pltpu.CompilerParams(has_side_effects=_SIDE),
    )(*srcs, *lands, send_sem, recv_sem, *after)
    return list(outs[:n]), list(outs[n:])


_GATHER_ROUTE = (lambda s, j, c: s.at[c], lambda l, me, k, c: l.at[me, c], lambda l, j, k, c: l.at[j, c])
_SCATTER_ROUTE = (lambda s, j, c: s.at[j], lambda l, me, k, c: l.at[k], lambda l, j, k, c: l.at[k])
_SHARE_ROUTE = (lambda s, j, c: s, lambda l, me, k, c: l.at[c], lambda l, j, k, c: l.at[1 - c], True)
_SWAP_ROUTE = (lambda s, j, c: s.at[:, 1 - c], lambda l, me, k, c: l, lambda l, j, k, c: l, True)


def _gather_forward(lands, tag, own=False):
    n = len(lands)
    m = 4 if own else 3

    def body(*refs):
        ins, outs = refs[:n], refs[n:2 * n]
        send_sem, recv_sem = refs[2 * n:]
        x, y, c, chips = _place()
        slots = [2 * cx + cy for cx, cy in chips] + [2 * x + y]

        def copy(i, k, half):
            return pltpu.make_async_remote_copy(
                src_ref=ins[i].at[slots[k], half], dst_ref=outs[i].at[slots[k], half],
                send_sem=send_sem.at[m * i + k], recv_sem=recv_sem.at[m * i + k],
                device_id=(x, y, 1 - c), device_id_type=_MESH)

        copies = [copy(i, k, c) for i in range(n) for k in range(m)]
        for cp in copies:
            cp.start()
        for i in range(n):
            for k in range(m):
                copy(i, k, 1 - c).wait_recv()
        for cp in copies:
            cp.wait_send()

    return pl.pallas_call(
        body, name="gather_forward_to_sibling_" + tag,
        out_shape=[jax.ShapeDtypeStruct(a.shape, a.dtype) for a in lands],
        in_specs=[_ANY] * n, out_specs=[_ANY] * n,
        input_output_aliases={i: i for i in range(n)},
        scratch_shapes=[pltpu.SemaphoreType.DMA((m * n,)), pltpu.SemaphoreType.DMA((m * n,))],
    )(*lands)


def _swap_halves(grads, tag):
    n = len(grads)

    def body(*refs):
        ins, outs = refs[:n], refs[n:2 * n]
        send_sem, recv_sem = refs[2 * n:]
        x, y, c, _ = _place()
        copies = [pltpu.make_async_remote_copy(
            src_ref=ins[i].at[:, 1 - c], dst_ref=outs[i],
            send_sem=send_sem.at[i], recv_sem=recv_sem.at[i],
            device_id=(x, y, 1 - c), device_id_type=_MESH) for i in range(n)]
        for cp in copies:
            cp.start()
        for cp in copies:
            cp.wait()

    return pl.pallas_call(
        body, name="grad_swap_halves_" + tag,
        out_shape=[jax.ShapeDtypeStruct((N_CHIPS,) + g.shape[2:], g.dtype) for g in grads],
        in_specs=[_ANY] * n, out_specs=[_ANY] * n,
        scratch_shapes=[pltpu.SemaphoreType.DMA((n,)), pltpu.SemaphoreType.DMA((n,))],
    )(*grads)


def _sum_rows(h, C):
    return max(d for d in range(SUBLANES, h + 1, SUBLANES) if h % d == 0 and d * C <= 1 << 20)


SUM_STEPS = 4


def _pair_sums(gs, rs, c_idx, *, name):
    n = len(gs)
    rows = [g.shape[2] // SUM_STEPS for g in gs]

    def body(c_ref, *refs):
        for g_ref, r_ref, o_ref in zip(refs[:n], refs[n:2 * n], refs[2 * n:]):
            o_ref[...] = (g_ref[...].astype(F32) + r_ref[...].astype(F32)).astype(o_ref.dtype)

    return pl.pallas_call(
        body, name=name,
        out_shape=[jax.ShapeDtypeStruct((N_CHIPS,) + g.shape[2:], g.dtype) for g in gs],
        grid_spec=pltpu.PrefetchScalarGridSpec(
            num_scalar_prefetch=1, grid=(N_CHIPS, SUM_STEPS),
            in_specs=[pl.BlockSpec((None, None, tr, g.shape[3]), lambda j, i, s: (j, s[0], i, 0))
                      for g, tr in zip(gs, rows)]
            + [pl.BlockSpec((None, tr, g.shape[3]), lambda j, i, s: (j, i, 0)) for g, tr in zip(gs, rows)],
            out_specs=[pl.BlockSpec((None, tr, g.shape[3]), lambda j, i, s: (j, i, 0)) for g, tr in zip(gs, rows)]),
        compiler_params=_params("parallel", "parallel"),
    )(c_idx, *gs, *rs)


def _owner_sums(ss, rs, jc_idx, *, name):
    n = len(ss)
    rows = [s.shape[1] // SUM_STEPS for s in ss]

    def body(jc_ref, *refs):
        for s_ref, r_ref, m_ref, o_ref in zip(refs[:n], refs[n:2 * n], refs[2 * n:3 * n], refs[3 * n:]):
            acc = s_ref[...].astype(F32)
            for k in range(3):
                acc = acc + r_ref[k].astype(F32)
            m_ref[...] = acc
            o_ref[...] = acc

    outs = pl.pallas_call(
        body, name=name,
        out_shape=[jax.ShapeDtypeStruct(s.shape[1:], F32) for s in ss]
        + [jax.ShapeDtypeStruct((2,) + s.shape[1:], F32) for s in ss],
        grid_spec=pltpu.PrefetchScalarGridSpec(
            num_scalar_prefetch=1, grid=(SUM_STEPS,),
            in_specs=[pl.BlockSpec((None, tr, s.shape[2]), lambda i, p: (p[0], i, 0)) for s, tr in zip(ss, rows)]
            + [pl.BlockSpec((3, tr, s.shape[2]), lambda i, p: (0, i, 0)) for s, tr in zip(ss, rows)],
            out_specs=[pl.BlockSpec((tr, s.shape[2]), lambda i, p: (i, 0)) for s, tr in zip(ss, rows)]
            + [pl.BlockSpec((None, tr, s.shape[2]), lambda i, p: (p[1], i, 0)) for s, tr in zip(ss, rows)]),
        compiler_params=_params("parallel"),
    )(jc_idx, *ss, *rs)
    return outs[:n], outs[n:]


def _chip_sums(grads, c_idx, tag):
    views = [g.reshape(N_CHIPS, 2, g.shape[1] // 2, g.shape[2]) for g in grads]
    arrived = _swap_halves(views, tag)
    return _pair_sums(views, arrived, c_idx, name=f"grad_pair_sums_{tag}")


def _sum_devices(blocks):
    R = blocks.shape[2]
    tr = _sum_rows(R, 2 * N_CHIPS * LANES)

    def body(b_ref, o_ref):
        acc = b_ref[0, 0]
        for d in range(1, 2 * N_CHIPS):
            acc = acc + b_ref[d // 2, d % 2]
        o_ref[...] = acc

    return pl.pallas_call(
        body, name="sum_small_over_devices", out_shape=jax.ShapeDtypeStruct((R, LANES), F32),
        grid=(R // tr,),
        in_specs=[pl.BlockSpec((N_CHIPS, 2, tr, LANES), lambda i: (0, 0, i, 0))],
        out_specs=pl.BlockSpec((tr, LANES), lambda i: (i, 0)),
        compiler_params=_params("parallel"),
    )(blocks)


def _adamw(w, g, m, v, *, name):
    R, C = w.shape
    tr = max(d for d in range(SUBLANES, R + 1, SUBLANES)
             if R % d == 0 and 7 * 2 * d * C * 4 <= VMEM_LIMIT_BYTES // 2)

    def body(w_ref, g_ref, m_ref, v_ref, d_ref, nm_ref, nv_ref):
        g = g_ref[...]
        m = ADAM_B1 * m_ref[...] + (1.0 - ADAM_B1) * g
        v = ADAM_B2 * v_ref[...] + (1.0 - ADAM_B2) * (g * g)
        nm_ref[...] = m
        nv_ref[...] = v
        m_hat = m / (1.0 - ADAM_B1 ** ADAM_STEP)
        v_hat = v / (1.0 - ADAM_B2 ** ADAM_STEP)
        d_ref[...] = -ADAM_LR * (m_hat / (jnp.sqrt(v_hat) + ADAM_EPS) + ADAM_WD * w_ref[...])

    blk = pl.BlockSpec((tr, C), lambda i: (i, 0))
    sds = jax.ShapeDtypeStruct((R, C), F32)
    return pl.pallas_call(
        body, name=name, out_shape=(sds, sds, sds), grid=(R // tr,),
        in_specs=[blk] * 4, out_specs=(blk, blk, blk),
        compiler_params=_params("parallel"),
    )(w, g, m, v)


_TILE = SUBLANES * LANES


def _pack(arrays):
    rows = []
    for a in arrays:
        flat = a.reshape(-1)
        flat = jnp.pad(flat, (0, (-flat.shape[0]) % _TILE))
        rows.append(flat.reshape(-1, LANES))
    return jnp.concatenate(rows, axis=0)


def _unpack(buf, shapes):
    out, r = [], 0
    for s in shapes:
        size = math.prod(s)
        nr = -(-size // _TILE) * SUBLANES
        out.append(buf[r:r + nr].reshape(-1)[:size].reshape(s))
        r += nr
    return out


_BIG = ("w_in_a", "w_glu", "w_kv", "w_in_b", "w_mem_kv", "w_out")
_REPLICATED = ("pre_norm_g", "post_norm_g", "lam_re", "lam_im", "log_step", "b_re", "b_im", "c_re", "c_im",
               "kv_norm_g", "b_fgate", "mem_norm_g")
_SHARDED_SMALL = ("d_skip", "b_glu", "w_fgate")
_WEIGHTS = ("pre_norm_g", "post_norm_g", "w_in_a", "lam_re", "lam_im", "log_step", "b_re", "b_im", "c_re",
            "c_im", "d_skip", "w_glu", "b_glu", "kv_norm_g", "w_kv", "w_fgate", "b_fgate", "w_in_b",
            "mem_norm_g", "w_mem_kv", "w_out")


def _halves(a):
    return a.reshape(2, a.shape[0] // 2, a.shape[1])


def _unhalve(a):
    return a.reshape(N_CHIPS, 2 * a.shape[2], a.shape[3])


def _columns(a):
    return jnp.transpose(a, (1, 0, 2)).reshape(a.shape[1], N_CHIPS * a.shape[2])


def _columns_copy(a, *, name, tr=1024):
    n, R, C = a.shape

    def body(i_ref, o_ref):
        o_ref[...] = i_ref[...]

    return pl.pallas_call(
        body, name=name, out_shape=jax.ShapeDtypeStruct((R, n * C), a.dtype), grid=(n, R // tr),
        in_specs=[pl.BlockSpec((None, tr, C), lambda j, i: (j, i, 0))],
        out_specs=pl.BlockSpec((tr, C), lambda j, i: (i, j)),
        compiler_params=_params("parallel", "parallel"),
    )(a)


def kernel(x, mem, pre_norm_g, post_norm_g, w_in_a, lam_re, lam_im, log_step, b_re, b_im, c_re, c_im, d_skip, w_glu, b_glu, kv_norm_g, w_kv, w_fgate, b_fgate, w_in_b, mem_norm_g, w_mem_kv, w_out, loss_target, m_pre_norm_g, m_post_norm_g, m_w_in_a, m_lam_re, m_lam_im, m_log_step, m_b_re, m_b_im, m_c_re, m_c_im, m_d_skip, m_w_glu, m_b_glu, m_kv_norm_g, m_w_kv, m_w_fgate, m_b_fgate, m_w_in_b, m_mem_norm_g, m_w_mem_kv, m_w_out, v_pre_norm_g, v_post_norm_g, v_w_in_a, v_lam_re, v_lam_im, v_log_step, v_b_re, v_b_im, v_c_re, v_c_im, v_d_skip, v_w_glu, v_b_glu, v_kv_norm_g, v_w_kv, v_w_fgate, v_b_fgate, v_w_in_b, v_mem_norm_g, v_w_mem_kv, v_w_out):
    a = dict(locals())
    xi, yi, ci = lax.axis_index("x"), lax.axis_index("y"), lax.axis_index("c")
    chip = 2 * xi + yi
    c_idx = jnp.reshape(ci, (1,)).astype(jnp.int32)
    jc_idx = jnp.stack([chip, ci]).astype(jnp.int32)

    vec = jnp.zeros((2 * SUBLANES, MAIN_WIDTH // N_CHIPS), F32)
    vec = vec.at[0].set(a["d_skip"][0]).at[1].set(a["b_glu"][0])
    def own_slot(gathered, parts):
        return [lax.dynamic_update_index_in_dim(g, p, chip, 0) for g, p in zip(gathered, parts)]

    travelling, token = {}, a["pre_norm_g"]

    def start_gather(tag, parts, token):
        lands = [lax.empty((N_CHIPS,) + p.shape, p.dtype) for p in parts]
        travelling[tag], token = _ici_start(parts, lands, token, _GATHER_ROUTE, name=f"gather_{tag}_start")
        return token

    token = start_gather("a", [_halves(a["w_in_a"][0].astype(BF16)), _halves(vec)], token)
    later = ("w_glu", "w_mem_kv", "w_out", "w_kv", "w_fgate", "w_in_b")
    token, *raw = lax.optimization_barrier((token, *[a[n] for n in later]))
    raw = dict(zip(later, raw))
    token = start_gather("b", [_halves(raw["w_glu"][0].astype(BF16)),
                               *[_halves(raw["w_mem_kv"][i].astype(BF16)) for i in range(2)],
                               *[_halves(raw["w_out"][i].astype(BF16)) for i in range(2)]], token)
    token = start_gather("c", [_halves(raw["w_kv"].astype(BF16)), _halves(_pad_lanes(raw["w_fgate"]).astype(BF16)),
                               _halves(raw["w_in_b"][0].astype(BF16))], token)

    small_names = _REPLICATED + _SHARDED_SMALL
    small_state = [_pack([a[prefix + n] for n in small_names]) for prefix in ("", "m_")]

    def fetch(tag, after):
        if tag == "a":
            after = list(after) + small_state
        parts, lands = _ici_wait(travelling[tag], after, _GATHER_ROUTE, name=f"gather_{tag}_wait")
        full = own_slot(_gather_forward(lands, tag), parts)
        if tag == "a":
            w_in_a, vecs = full
            w_in_a, *moments = lax.optimization_barrier((w_in_a, *[a["v_" + n] for n in small_names]))
            small_state.append(_pack(moments))
            return dict(w_in_a=_columns_copy(_unhalve(w_in_a), name="w_in_a_columns"),
                        d_skip=vecs[:, 0, 0, :].reshape(MAIN_WIDTH),
                        b_glu=vecs[:, 0, 1, :].reshape(MAIN_WIDTH))
        if tag == "b":
            w_glu, w_mk0, w_mk1, w_out0, w_out1 = full
            return dict(w_glu=w_glu.reshape(MAIN_WIDTH, MAIN_WIDTH),
                        w_mem_kv=[m.reshape(D_MODEL, 2 * MEM_WIDTH) for m in (w_mk0, w_mk1)],
                        w_out=[o.reshape(D_MODEL, D_MODEL) for o in (w_out0, w_out1)])
        w_kv, w_fg, w_in_b = full
        return dict(w_kv=_columns(_unhalve(w_kv)), w_fgate=w_fg.reshape(D_MODEL, LANES),
                    w_in_b=_columns(_unhalve(w_in_b)))

    early = ("mem_norm_g", "lam_re", "lam_im", "log_step", "b_re", "b_im", "c_re", "c_im")
    token, *held = lax.optimization_barrier((token, *[a[n] for n in early]))
    held = dict(zip(early, held))
    w = dict(
        pre_norm_g=token, post_norm_g=a["post_norm_g"], mem_norm_g=held["mem_norm_g"],
        kv_norm_g=a["kv_norm_g"], b_fgate=a["b_fgate"],
        **{n: held[n][0] for n in early[1:]})

    sent = {}

    swapping = {}

    def grads_ready(event, g, token):
        tag = event.split("_")[0]
        if event in ("b", "a1"):
            big = {"b": lambda: [g["w_kv"], g["w_in_b"], g["w_mem_kv_1"].reshape(N_CHIPS, -1, 2 * MEM_WIDTH),
                                 g["w_out_1"].reshape(N_CHIPS, -1, D_MODEL)],
                   "a1": lambda: [g["w_glu"].reshape(N_CHIPS, -1, MAIN_WIDTH),
                                  g["w_mem_kv_0"].reshape(N_CHIPS, -1, 2 * MEM_WIDTH),
                                  g["w_out_0"].reshape(N_CHIPS, -1, D_MODEL)]}[tag]()
            views = [b.reshape(N_CHIPS, 2, b.shape[1] // 2, b.shape[2]) for b in big]
            lands = [lax.empty((N_CHIPS,) + v.shape[2:], v.dtype) for v in views]
            swapping[tag], token = _ici_start(views, lands, token, _SWAP_ROUTE, name=f"grad_swap_{tag}_start")
            return token
        if event == "a2":
            sums = _chip_sums([g["w_in_a"]], c_idx, tag)
        else:
            views, arrived = _ici_wait(swapping[tag], token, _SWAP_ROUTE, name=f"grad_swap_{tag}_wait")
            sums = _pair_sums(views, arrived, c_idx, name=f"grad_pair_sums_{tag}")
        lands = [lax.empty((3,) + s.shape[1:], s.dtype) for s in sums]
        sent[tag], token = _ici_start(sums, lands, token, _SCATTER_ROUTE, name=f"grad_send_{tag}_start")
        return token

    loss_row, grad_x, g = _local_step(a["x"][0], a["mem"][0], a["loss_target"][0], w, fetch, grads_ready)

    pack = _pack([g[n] for n in small_names] + [loss_row])
    blocks = lax.empty((N_CHIPS, 2) + pack.shape, F32)
    small_sent, token = _ici_start([pack], [blocks], loss_row, _BLOCK_ROUTE, name="small_sums_start")

    sharing = {}
    for tag in ("b", "a1", "a2"):
        sums, arrived = _ici_wait(sent[tag], [grad_x, token], _SCATTER_ROUTE, name=f"grad_send_{tag}_wait")
        mine, bufs = _owner_sums(sums, arrived, jc_idx, name=f"grad_owner_sums_{tag}")
        sharing[tag], token = _ici_start(mine, bufs, token, _SHARE_ROUTE, name=f"grad_share_{tag}_start")

    def shared(tag, after):
        _, bufs = _ici_wait(sharing[tag], after, _SHARE_ROUTE, name=f"grad_share_{tag}_wait")
        return [b.reshape(-1, b.shape[2]) for b in bufs]

    grads, delta, new_m, new_v = {}, {}, {}, {}

    def adam(n):
        shape = a[n].shape
        d2 = (-1, shape[-1])
        d, m, v = _adamw(a[n].reshape(d2), grads[n].reshape(d2), a["m_" + n].reshape(d2),
                         a["v_" + n].reshape(d2), name="adamw_" + n)
        delta[n], new_m[n], new_v[n] = d.reshape(shape), m.reshape(shape), v.reshape(shape)
        return d

    r_kv, r_in_b, r_mk1, r_out1 = shared("b", token)
    grads["w_kv"], grads["w_in_b"] = r_kv, r_in_b[None]
    done = [adam("w_kv"), adam("w_in_b")]
    r_glu, r_mk0, r_out0 = shared("a1", done)
    grads["w_glu"], grads["w_mem_kv"], grads["w_out"] = r_glu[None], jnp.stack([r_mk0, r_mk1]), jnp.stack([r_out0, r_out1])
    done = [adam("w_glu"), adam("w_mem_kv"), adam("w_out")]
    (r_in_a,) = shared("a2", done)
    grads["w_in_a"] = r_in_a[None]
    adam("w_in_a")

    (pack,), (blocks,) = _ici_wait(small_sent, [delta[n] for n in _BIG], _BLOCK_ROUTE, name="small_sums_wait")
    blocks = lax.dynamic_update_slice(blocks, pack[None, None], (chip, ci, 0, 0))
    (blocks,) = _gather_forward([blocks], "small", own=True)
    *summed, loss_sums = _unpack(_sum_devices(blocks), [g[n].shape for n in small_names] + [loss_row.shape])
    small = dict(zip(small_names, summed))
    loss = jnp.sum(loss_sums)
    for n in _REPLICATED:
        grads[n] = small[n].reshape(a[n].shape)
    nd = MAIN_WIDTH // N_CHIPS
    grads["d_skip"] = lax.dynamic_slice(small["d_skip"], (chip * nd,), (nd,))[None]
    grads["b_glu"] = lax.dynamic_slice(small["b_glu"], (chip * nd,), (nd,))[None]
    nf = D_MODEL // N_CHIPS
    grads["w_fgate"] = lax.dynamic_slice(small["w_fgate"], (chip * nf, 0), (nf, FOX_HEADS))

    shapes = [a[n].shape for n in small_names]
    d, m, v = _adamw(small_state[0], _pack([grads[n] for n in small_names]), *small_state[1:], name="adamw_small")
    for n, dd, mm, vv in zip(small_names, _unpack(d, shapes), _unpack(m, shapes), _unpack(v, shapes)):
        delta[n], new_m[n], new_v[n] = dd, mm, vv

    return (loss, grad_x[None], *[grads[n] for n in _WEIGHTS], *[delta[n] for n in _WEIGHTS],
            *[new_m[n] for n in _WEIGHTS], *[new_v[n] for n in _WEIGHTS])
```
